```python
import math
import jax, jax.numpy as jnp
from jax import lax
import numpy as np

D_MODEL = 1024
BATCH = 8
SEQ = 8192
DEPTH = 1

PLE_DIM = 256
CHUNK = 128
EPS = 1e-6
E_A = D_MODEL
G_A = 4
D_INNER = 2 * D_MODEL
HEAD_DIM = 64
N_HEADS = D_INNER // HEAD_DIM
N_STATE = 128
N_GROUPS = 4
CONV_K = 4
CONV_DIM = D_INNER + 2 * N_GROUPS * N_STATE
COL_SIZES = (E_A, E_A, E_A, D_INNER, CONV_DIM, N_HEADS, D_MODEL, D_MODEL)
N_IN = sum(COL_SIZES)

kernel_name = 'hybrid_gmlp_ssd_gated_merge_ple'


def rms_norm(x, g):
    xf = x.astype(jnp.float32)
    y = xf * lax.rsqrt(jnp.mean(xf * xf, axis=-1, keepdims=True) + EPS)
    return (y * g.astype(jnp.float32)).astype(x.dtype)


def layer_norm(x, g, b):
    xf = x.astype(jnp.float32)
    mu = jnp.mean(xf, axis=-1, keepdims=True)
    xc = xf - mu
    y = xc * lax.rsqrt(jnp.mean(xc * xc, axis=-1, keepdims=True) + EPS)
    return (y * g.astype(jnp.float32) + b.astype(jnp.float32)).astype(x.dtype)


def gmlp_branch(u, v, z, ln_g, ln_b, w_s, b_s):
    bsz, s, e = u.shape
    nc = s // CHUNK
    u = jax.nn.gelu(u)
    v = layer_norm(jax.nn.gelu(v), ln_g, ln_b)
    mask = jnp.tril(jnp.ones((CHUNK, CHUNK), dtype=bool))
    ws = jnp.where(mask[None], w_s, jnp.zeros_like(w_s)).astype(v.dtype)
    vc = v.reshape(bsz, nc, CHUNK, G_A, e // G_A)
    sv = jnp.einsum('gts,bcsgd->bctgd', ws, vc) + b_s.T.astype(v.dtype)[None, None, :, :, None]
    return u * sv.reshape(bsz, s, e) * jax.nn.silu(z)


def causal_depthwise_conv(x, w, b):
    c = x.shape[-1]
    y = lax.conv_general_dilated(x, w.astype(x.dtype)[:, None, :], window_strides=(1,),
                                 padding=[(CONV_K - 1, 0)],
                                 dimension_numbers=('NWC', 'WIO', 'NWC'),
                                 feature_group_count=c)
    return y + b.astype(x.dtype)


def ssd_scan(xs, dt, a_log, bm, cm, d_skip):
    bsz, s, h, pdim = xs.shape
    nc = s // CHUNK
    g = N_GROUPS
    r = h // g
    dtype = xs.dtype
    a = -jnp.exp(a_log.astype(jnp.float32)).reshape(g, r)
    X = xs.reshape(bsz, nc, CHUNK, g, r, pdim)
    dtc = dt.reshape(bsz, nc, CHUNK, g, r)
    Xdt = X * dtc[..., None].astype(dtype)
    dA_cs = jnp.cumsum(dtc * a, axis=2)
    Bc = bm.reshape(bsz, nc, CHUNK, g, N_STATE)
    Cc = cm.reshape(bsz, nc, CHUNK, g, N_STATE)
    mask = jnp.tril(jnp.ones((CHUNK, CHUNK), dtype=bool))[None, None, :, :, None, None]
    seg = dA_cs[:, :, :, None] - dA_cs[:, :, None, :]
    Lmat = jnp.exp(jnp.where(mask, seg, -jnp.inf)).astype(dtype)
    CB = jnp.einsum('bclgn,bcsgn->bclsg', Cc, Bc)
    y_diag = jnp.einsum('bclsg,bclsgr,bcsgrp->bclgrp', CB, Lmat, Xdt)
    decay_s = jnp.exp(dA_cs[:, :, -1:] - dA_cs).astype(dtype)
    states = jnp.einsum('bcsgn,bcsgr,bcsgrp->bcgrpn', Bc, decay_s, Xdt)
    chunk_decay = jnp.exp(dA_cs[:, :, -1]).astype(dtype)

    def step(hstate, inp):
        st, dec = inp
        return dec[..., None, None] * hstate + st, hstate

    h0 = jnp.zeros((bsz, g, r, pdim, N_STATE), dtype=states.dtype)
    _, prev = lax.scan(step, h0, (jnp.moveaxis(states, 1, 0), jnp.moveaxis(chunk_decay, 1, 0)))
    prev = jnp.moveaxis(prev, 0, 1)
    y_off = jnp.einsum('bclgn,bcgrpn,bclgr->bclgrp', Cc, prev, jnp.exp(dA_cs).astype(dtype))
    y = y_diag + y_off + X * d_skip.astype(dtype).reshape(g, r)[..., None]
    return y.reshape(bsz, s, h * pdim)


def gated_group_rms_norm(y, z, g):
    yz = y * jax.nn.silu(z)
    bsz, s, e = yz.shape
    yg = yz.reshape(bsz, s, N_GROUPS, e // N_GROUPS).astype(jnp.float32)
    yg = yg * lax.rsqrt(jnp.mean(yg * yg, axis=-1, keepdims=True) + EPS)
    return (yg.reshape(bsz, s, e) * g.astype(jnp.float32)).astype(y.dtype)


def _fwd_setup_inputs(seed: int = 0) -> dict:
    key = jax.random.key(seed)
    ks = jax.random.split(key, 24)
    f32 = jnp.float32
    nrm = lambda k, shape, scale: jax.random.normal(k, shape, f32) * scale
    gain = lambda k, shape: 1.0 + 0.02 * jax.random.normal(k, shape, f32)
    dt0 = jnp.exp(jax.random.uniform(ks[10], (DEPTH, N_HEADS), f32) * (math.log(0.1) - math.log(0.001)) + math.log(0.001))
    return {
        'x': jax.random.normal(ks[0], (BATCH, SEQ, D_MODEL), f32),
        'p': jax.random.normal(ks[1], (DEPTH, BATCH, SEQ, PLE_DIM), f32),
        'norm_g': gain(ks[2], (DEPTH, D_MODEL)),
        'w_in': nrm(ks[3], (DEPTH, D_MODEL, N_IN), D_MODEL ** -0.5),
        'ln_a_g': gain(ks[4], (DEPTH, E_A)),
        'ln_a_b': nrm(ks[5], (DEPTH, E_A), 0.02),
        'w_s': nrm(ks[6], (DEPTH, G_A, CHUNK, CHUNK), CHUNK ** -0.5),
        'b_s': gain(ks[7], (DEPTH, G_A, CHUNK)),
        'conv_w': nrm(ks[8], (DEPTH, CONV_K, CONV_DIM), CONV_K ** -0.5),
        'conv_b': nrm(ks[9], (DEPTH, CONV_DIM), 0.02),
        'dt_bias': dt0 + jnp.log(-jnp.expm1(-dt0)),
        'a_log': jnp.log(jax.random.uniform(ks[11], (DEPTH, N_HEADS), f32, 1.0, 16.0)),
        'd_skip': gain(ks[12], (DEPTH, N_HEADS)),
        'ssm_norm_g': gain(ks[13], (DEPTH, D_INNER)),
        'w_oa': nrm(ks[14], (DEPTH, E_A, D_MODEL), E_A ** -0.5),
        'w_ob': nrm(ks[15], (DEPTH, D_INNER, D_MODEL), D_INNER ** -0.5),
        'w_out': nrm(ks[16], (DEPTH, D_MODEL, D_MODEL), D_MODEL ** -0.5),
        'ple_norm_g': gain(ks[17], (DEPTH, D_MODEL)),
        'w_pg': nrm(ks[18], (DEPTH, D_MODEL, D_MODEL), D_MODEL ** -0.5),
        'w_ple': nrm(ks[19], (DEPTH, PLE_DIM, D_MODEL), PLE_DIM ** -0.5),
        'final_g': gain(ks[20], (D_MODEL,)),
    }


def _fwd_reference(x, p, norm_g, w_in, ln_a_g, ln_a_b, w_s, b_s, conv_w, conv_b, dt_bias, a_log,
              d_skip, ssm_norm_g, w_oa, w_ob, w_out, ple_norm_g, w_pg, w_ple, final_g):
    bsz, s, _ = x.shape
    splits = list(np.cumsum(COL_SIZES)[:-1])
    for i in range(DEPTH):
        h = rms_norm(x, norm_g[i])
        proj = h @ w_in[i].astype(h.dtype)
        u, v, z_a, z_b, xbc, dt_raw, g_a, g_b = jnp.split(proj, splits, axis=-1)
        y_a = gmlp_branch(u, v, z_a, ln_a_g[i], ln_a_b[i], w_s[i], b_s[i])
        o_a = y_a @ w_oa[i].astype(y_a.dtype)
        xbc = jax.nn.silu(causal_depthwise_conv(xbc, conv_w[i], conv_b[i]))
        xs, bm, cm = jnp.split(xbc, [D_INNER, D_INNER + N_GROUPS * N_STATE], axis=-1)
        dt = jax.nn.softplus(dt_raw.astype(jnp.float32) + dt_bias[i].astype(jnp.float32))
        y = ssd_scan(xs.reshape(bsz, s, N_HEADS, HEAD_DIM), dt, a_log[i],
                     bm.reshape(bsz, s, N_GROUPS, N_STATE), cm.reshape(bsz, s, N_GROUPS, N_STATE), d_skip[i])
        y_b = gated_group_rms_norm(y, z_b, ssm_norm_g[i])
        o_b = y_b @ w_ob[i].astype(y_b.dtype)
        merged = jax.nn.sigmoid(g_a) * o_a + jax.nn.sigmoid(g_b) * o_b
        x = x + merged @ w_out[i].astype(merged.dtype)
        hp = rms_norm(x, ple_norm_g[i])
        x = x + jax.nn.sigmoid(hp @ w_pg[i].astype(hp.dtype)) * (p[i] @ w_ple[i].astype(p.dtype))
    return rms_norm(x, final_g)


import jax as _jax
import jax.numpy as _jnp

TWIN_FORMAT = 'train_step'
FWD_PARAMS = ['x', 'p', 'norm_g', 'w_in', 'ln_a_g', 'ln_a_b', 'w_s', 'b_s', 'conv_w', 'conv_b', 'dt_bias', 'a_log', 'd_skip', 'ssm_norm_g', 'w_oa', 'w_ob', 'w_out', 'ple_norm_g', 'w_pg', 'w_ple', 'final_g']
TWIN_WEIGHTS = ['norm_g', 'w_in', 'ln_a_g', 'ln_a_b', 'w_s', 'b_s', 'conv_w', 'conv_b', 'dt_bias', 'a_log', 'd_skip', 'ssm_norm_g', 'w_oa', 'w_ob', 'w_out', 'ple_norm_g', 'w_pg', 'w_ple', 'final_g']
TWIN_DIFF_INPUT = 'x'
TWIN_INPUTS = ['x', 'p', 'norm_g', 'w_in', 'ln_a_g', 'ln_a_b', 'w_s', 'b_s', 'conv_w', 'conv_b', 'dt_bias', 'a_log', 'd_skip', 'ssm_norm_g', 'w_oa', 'w_ob', 'w_out', 'ple_norm_g', 'w_pg', 'w_ple', 'final_g', 'loss_target', 'm_norm_g', 'm_w_in', 'm_ln_a_g', 'm_ln_a_b', 'm_w_s', 'm_b_s', 'm_conv_w', 'm_conv_b', 'm_dt_bias', 'm_a_log', 'm_d_skip', 'm_ssm_norm_g', 'm_w_oa', 'm_w_ob', 'm_w_out', 'm_ple_norm_g', 'm_w_pg', 'm_w_ple', 'm_final_g', 'v_norm_g', 'v_w_in', 'v_ln_a_g', 'v_ln_a_b', 'v_w_s', 'v_b_s', 'v_conv_w', 'v_conv_b', 'v_dt_bias', 'v_a_log', 'v_d_skip', 'v_ssm_norm_g', 'v_w_oa', 'v_w_ob', 'v_w_out', 'v_ple_norm_g', 'v_w_pg', 'v_w_ple', 'v_final_g']
TWIN_OUTPUTS = ['loss', 'grad_x', 'grad_norm_g', 'grad_w_in', 'grad_ln_a_g', 'grad_ln_a_b', 'grad_w_s', 'grad_b_s', 'grad_conv_w', 'grad_conv_b', 'grad_dt_bias', 'grad_a_log', 'grad_d_skip', 'grad_ssm_norm_g', 'grad_w_oa', 'grad_w_ob', 'grad_w_out', 'grad_ple_norm_g', 'grad_w_pg', 'grad_w_ple', 'grad_final_g', 'delta_norm_g', 'delta_w_in', 'delta_ln_a_g', 'delta_ln_a_b', 'delta_w_s', 'delta_b_s', 'delta_conv_w', 'delta_conv_b', 'delta_dt_bias', 'delta_a_log', 'delta_d_skip', 'delta_ssm_norm_g', 'delta_w_oa', 'delta_w_ob', 'delta_w_out', 'delta_ple_norm_g', 'delta_w_pg', 'delta_w_ple', 'delta_final_g', 'new_m_norm_g', 'new_m_w_in', 'new_m_ln_a_g', 'new_m_ln_a_b', 'new_m_w_s', 'new_m_b_s', 'new_m_conv_w', 'new_m_conv_b', 'new_m_dt_bias', 'new_m_a_log', 'new_m_d_skip', 'new_m_ssm_norm_g', 'new_m_w_oa', 'new_m_w_ob', 'new_m_w_out', 'new_m_ple_norm_g', 'new_m_w_pg', 'new_m_w_ple', 'new_m_final_g', 'new_v_norm_g', 'new_v_w_in', 'new_v_ln_a_g', 'new_v_ln_a_b', 'new_v_w_s', 'new_v_b_s', 'new_v_conv_w', 'new_v_conv_b', 'new_v_dt_bias', 'new_v_a_log', 'new_v_d_skip', 'new_v_ssm_norm_g', 'new_v_w_oa', 'new_v_w_ob', 'new_v_w_out', 'new_v_ple_norm_g', 'new_v_w_pg', 'new_v_w_ple', 'new_v_final_g']
TWIN_LEAF_KINDS = {'loss': 'loss', 'grad_x': 'grad_x', 'grad_norm_g': 'grad_w', 'grad_w_in': 'grad_w', 'grad_ln_a_g': 'grad_w', 'grad_ln_a_b': 'grad_w', 'grad_w_s': 'grad_w', 'grad_b_s': 'grad_w', 'grad_conv_w': 'grad_w', 'grad_conv_b': 'grad_w', 'grad_dt_bias': 'grad_w', 'grad_a_log': 'grad_w', 'grad_d_skip': 'grad_w', 'grad_ssm_norm_g': 'grad_w', 'grad_w_oa': 'grad_w', 'grad_w_ob': 'grad_w', 'grad_w_out': 'grad_w', 'grad_ple_norm_g': 'grad_w', 'grad_w_pg': 'grad_w', 'grad_w_ple': 'grad_w', 'grad_final_g': 'grad_w', 'delta_norm_g': 'delta_w', 'delta_w_in': 'delta_w', 'delta_ln_a_g': 'delta_w', 'delta_ln_a_b': 'delta_w', 'delta_w_s': 'delta_w', 'delta_b_s': 'delta_w', 'delta_conv_w': 'delta_w', 'delta_conv_b': 'delta_w', 'delta_dt_bias': 'delta_w', 'delta_a_log': 'delta_w', 'delta_d_skip': 'delta_w', 'delta_ssm_norm_g': 'delta_w', 'delta_w_oa': 'delta_w', 'delta_w_ob': 'delta_w', 'delta_w_out': 'delta_w', 'delta_ple_norm_g': 'delta_w', 'delta_w_pg': 'delta_w', 'delta_w_ple': 'delta_w', 'delta_final_g': 'delta_w', 'new_m_norm_g': 'new_m', 'new_m_w_in': 'new_m', 'new_m_ln_a_g': 'new_m', 'new_m_ln_a_b': 'new_m', 'new_m_w_s': 'new_m', 'new_m_b_s': 'new_m', 'new_m_conv_w': 'new_m', 'new_m_conv_b': 'new_m', 'new_m_dt_bias': 'new_m', 'new_m_a_log': 'new_m', 'new_m_d_skip': 'new_m', 'new_m_ssm_norm_g': 'new_m', 'new_m_w_oa': 'new_m', 'new_m_w_ob': 'new_m', 'new_m_w_out': 'new_m', 'new_m_ple_norm_g': 'new_m', 'new_m_w_pg': 'new_m', 'new_m_w_ple': 'new_m', 'new_m_final_g': 'new_m', 'new_v_norm_g': 'new_v', 'new_v_w_in': 'new_v', 'new_v_ln_a_g': 'new_v', 'new_v_ln_a_b': 'new_v', 'new_v_w_s': 'new_v', 'new_v_b_s': 'new_v', 'new_v_conv_w': 'new_v', 'new_v_conv_b': 'new_v', 'new_v_dt_bias': 'new_v', 'new_v_a_log': 'new_v', 'new_v_d_skip': 'new_v', 'new_v_ssm_norm_g': 'new_v', 'new_v_w_oa': 'new_v', 'new_v_w_ob': 'new_v', 'new_v_w_out': 'new_v', 'new_v_ple_norm_g': 'new_v', 'new_v_w_pg': 'new_v', 'new_v_w_ple': 'new_v', 'new_v_final_g': 'new_v'}


def _forward(args):
    return _fwd_reference(*[args[k] for k in FWD_PARAMS])


def _output_shape():
    out = _jax.eval_shape(lambda: _forward(_fwd_setup_inputs(0)))
    return out.shape, out.dtype

N_MICROBATCH = 1
ADAM_LR = 0.001
ADAM_B1 = 0.9
ADAM_B2 = 0.999
ADAM_EPS = 1e-08
ADAM_WD = 0.01
ADAM_STEP = 10
PER_EXAMPLE_BATCH_AXIS = {'x': 0, 'p': 1, 'loss_target': 0}
SHARED_INPUTS = []
_WEIGHT_DTYPES = {'norm_g': _jnp.float32, 'w_in': _jnp.float32, 'ln_a_g': _jnp.float32, 'ln_a_b': _jnp.float32, 'w_s': _jnp.float32, 'b_s': _jnp.float32, 'conv_w': _jnp.float32, 'conv_b': _jnp.float32, 'dt_bias': _jnp.float32, 'a_log': _jnp.float32, 'd_skip': _jnp.float32, 'ssm_norm_g': _jnp.float32, 'w_oa': _jnp.float32, 'w_ob': _jnp.float32, 'w_out': _jnp.float32, 'ple_norm_g': _jnp.float32, 'w_pg': _jnp.float32, 'w_ple': _jnp.float32, 'final_g': _jnp.float32}
MOMENT_SCALE = {'norm_g': 1.810705e-01, 'w_in': 5.914222e-02, 'ln_a_g': 3.157500e-02, 'ln_a_b': 3.208135e-02, 'w_s': 4.175991e-02, 'b_s': 6.320794e-02, 'conv_w': 6.757652e-02, 'conv_b': 9.764355e-02, 'dt_bias': 1.260180e-01, 'a_log': 3.857808e-01, 'd_skip': 1.210003e+00, 'ssm_norm_g': 8.226517e-02, 'w_oa': 5.113537e-02, 'w_ob': 1.074961e-01, 'w_out': 1.191621e-01, 'ple_norm_g': 4.414541e-02, 'w_pg': 4.163147e-02, 'w_ple': 1.061464e-01, 'final_g': 6.395058e+01}


def _to_microbatches(a, axis):
    t = _jnp.moveaxis(a, axis, 0)
    t = t.reshape((N_MICROBATCH, t.shape[0] // N_MICROBATCH) + t.shape[1:])
    return _jnp.moveaxis(t, 1, axis + 1)


def setup_inputs(seed: int = 0) -> dict:
    inp = _fwd_setup_inputs(seed)
    key = _jax.random.fold_in(_jax.random.key(seed), 7919)
    shape, _ = _output_shape()
    out = dict(inp)
    out["loss_target"] = _jax.random.normal(_jax.random.fold_in(key, 0), shape, _jnp.float32)
    for i, name in enumerate(TWIN_WEIGHTS):
        w = inp[name].astype(_jnp.float32)
        if MOMENT_SCALE is None:
            s = _jnp.sqrt(_jnp.mean(_jnp.square(w)) + 1e-30)
        else:
            s = MOMENT_SCALE[name]
        km, kv = _jax.random.split(_jax.random.fold_in(key, i + 1))
        out[name] = w
        out["m_" + name] = s * _jax.random.normal(km, w.shape, _jnp.float32)
        out["v_" + name] = (s * s) * _jax.random.uniform(kv, w.shape, _jnp.float32, 0.5, 1.5)
    if N_MICROBATCH > 1:
        for name, axis in PER_EXAMPLE_BATCH_AXIS.items():
            out[name] = _to_microbatches(out[name], axis)
    return {'x': out['x'], 'p': out['p'], 'norm_g': out['norm_g'], 'w_in': out['w_in'], 'ln_a_g': out['ln_a_g'], 'ln_a_b': out['ln_a_b'], 'w_s': out['w_s'], 'b_s': out['b_s'], 'conv_w': out['conv_w'], 'conv_b': out['conv_b'], 'dt_bias': out['dt_bias'], 'a_log': out['a_log'], 'd_skip': out['d_skip'], 'ssm_norm_g': out['ssm_norm_g'], 'w_oa': out['w_oa'], 'w_ob': out['w_ob'], 'w_out': out['w_out'], 'ple_norm_g': out['ple_norm_g'], 'w_pg': out['w_pg'], 'w_ple': out['w_ple'], 'final_g': out['final_g'], 'loss_target': out['loss_target'], 'm_norm_g': out['m_norm_g'], 'm_w_in': out['m_w_in'], 'm_ln_a_g': out['m_ln_a_g'], 'm_ln_a_b': out['m_ln_a_b'], 'm_w_s': out['m_w_s'], 'm_b_s': out['m_b_s'], 'm_conv_w': out['m_conv_w'], 'm_conv_b': out['m_conv_b'], 'm_dt_bias': out['m_dt_bias'], 'm_a_log': out['m_a_log'], 'm_d_skip': out['m_d_skip'], 'm_ssm_norm_g': out['m_ssm_norm_g'], 'm_w_oa': out['m_w_oa'], 'm_w_ob': out['m_w_ob'], 'm_w_out': out['m_w_out'], 'm_ple_norm_g': out['m_ple_norm_g'], 'm_w_pg': out['m_w_pg'], 'm_w_ple': out['m_w_ple'], 'm_final_g': out['m_final_g'], 'v_norm_g': out['v_norm_g'], 'v_w_in': out['v_w_in'], 'v_ln_a_g': out['v_ln_a_g'], 'v_ln_a_b': out['v_ln_a_b'], 'v_w_s': out['v_w_s'], 'v_b_s': out['v_b_s'], 'v_conv_w': out['v_conv_w'], 'v_conv_b': out['v_conv_b'], 'v_dt_bias': out['v_dt_bias'], 'v_a_log': out['v_a_log'], 'v_d_skip': out['v_d_skip'], 'v_ssm_norm_g': out['v_ssm_norm_g'], 'v_w_oa': out['v_w_oa'], 'v_w_ob': out['v_w_ob'], 'v_w_out': out['v_w_out'], 'v_ple_norm_g': out['v_ple_norm_g'], 'v_w_pg': out['v_w_pg'], 'v_w_ple': out['v_w_ple'], 'v_final_g': out['v_final_g']}


def _loss(weights, diff, rest, loss_target):
    with _jax.named_scope("forward"):
        args = {**rest, TWIN_DIFF_INPUT: diff, **{k: w.astype(_WEIGHT_DTYPES[k]) for k, w in weights.items()}}
        y = _forward(args)
    with _jax.named_scope("loss_head"):
        err = _jnp.square(y.astype(_jnp.float32) - loss_target)
        return 0.5 * _jnp.sum(_jnp.mean(err, axis=-1)) if err.ndim else 0.5 * err


def _adamw(w, g, m, v):
    m = ADAM_B1 * m + (1.0 - ADAM_B1) * g
    v = ADAM_B2 * v + (1.0 - ADAM_B2) * _jnp.square(g)
    m_hat = m / (1.0 - ADAM_B1 ** ADAM_STEP)
    v_hat = v / (1.0 - ADAM_B2 ** ADAM_STEP)
    delta = -ADAM_LR * (m_hat / (_jnp.sqrt(v_hat) + ADAM_EPS) + ADAM_WD * w)
    return delta, m, v


def reference(x, p, norm_g, w_in, ln_a_g, ln_a_b, w_s, b_s, conv_w, conv_b, dt_bias, a_log, d_skip, ssm_norm_g, w_oa, w_ob, w_out, ple_norm_g, w_pg, w_ple, final_g, loss_target, m_norm_g, m_w_in, m_ln_a_g, m_ln_a_b, m_w_s, m_b_s, m_conv_w, m_conv_b, m_dt_bias, m_a_log, m_d_skip, m_ssm_norm_g, m_w_oa, m_w_ob, m_w_out, m_ple_norm_g, m_w_pg, m_w_ple, m_final_g, v_norm_g, v_w_in, v_ln_a_g, v_ln_a_b, v_w_s, v_b_s, v_conv_w, v_conv_b, v_dt_bias, v_a_log, v_d_skip, v_ssm_norm_g, v_w_oa, v_w_ob, v_w_out, v_ple_norm_g, v_w_pg, v_w_ple, v_final_g):
    given = dict(x=x, p=p, norm_g=norm_g, w_in=w_in, ln_a_g=ln_a_g, ln_a_b=ln_a_b, w_s=w_s, b_s=b_s, conv_w=conv_w, conv_b=conv_b, dt_bias=dt_bias, a_log=a_log, d_skip=d_skip, ssm_norm_g=ssm_norm_g, w_oa=w_oa, w_ob=w_ob, w_out=w_out, ple_norm_g=ple_norm_g, w_pg=w_pg, w_ple=w_ple, final_g=final_g, loss_target=loss_target, m_norm_g=m_norm_g, m_w_in=m_w_in, m_ln_a_g=m_ln_a_g, m_ln_a_b=m_ln_a_b, m_w_s=m_w_s, m_b_s=m_b_s, m_conv_w=m_conv_w, m_conv_b=m_conv_b, m_dt_bias=m_dt_bias, m_a_log=m_a_log, m_d_skip=m_d_skip, m_ssm_norm_g=m_ssm_norm_g, m_w_oa=m_w_oa, m_w_ob=m_w_ob, m_w_out=m_w_out, m_ple_norm_g=m_ple_norm_g, m_w_pg=m_w_pg, m_w_ple=m_w_ple, m_final_g=m_final_g, v_norm_g=v_norm_g, v_w_in=v_w_in, v_ln_a_g=v_ln_a_g, v_ln_a_b=v_ln_a_b, v_w_s=v_w_s, v_b_s=v_b_s, v_conv_w=v_conv_w, v_conv_b=v_conv_b, v_dt_bias=v_dt_bias, v_a_log=v_a_log, v_d_skip=v_d_skip, v_ssm_norm_g=v_ssm_norm_g, v_w_oa=v_w_oa, v_w_ob=v_w_ob, v_w_out=v_w_out, v_ple_norm_g=v_ple_norm_g, v_w_pg=v_w_pg, v_w_ple=v_w_ple, v_final_g=v_final_g)
    weights = {n: given[n] for n in TWIN_WEIGHTS}
    shared = {n: given[n] for n in SHARED_INPUTS}
    per_example = {n: given[n] for n in ['x', 'p']}
    grad_fn = _jax.value_and_grad(_loss, argnums=(0, 1))

    def one_microbatch(ex, loss_target):
        ex = dict(ex)
        diff = ex.pop(TWIN_DIFF_INPUT)
        return grad_fn(weights, diff, {**shared, **ex}, loss_target)

    if N_MICROBATCH == 1:
        loss, (grad_w, grad_x) = one_microbatch(per_example, given["loss_target"])
    else:
        def body(carry, xs):
            loss_sum, grad_sum = carry
            l_k, (gw_k, gx_k) = one_microbatch(xs[0], xs[1])
            with _jax.named_scope("update"):
                return (loss_sum + l_k, _jax.tree.map(_jnp.add, grad_sum, gw_k)), gx_k

        init = (_jnp.zeros((), _jnp.float32), _jax.tree.map(_jnp.zeros_like, weights))
        (loss, grad_w), grad_x = _jax.lax.scan(body, init, (per_example, given["loss_target"]))
    with _jax.named_scope("update"):
        delta_w, new_m, new_v = {}, {}, {}
        for n in TWIN_WEIGHTS:
            delta_w[n], new_m[n], new_v[n] = _adamw(weights[n], grad_w[n], given["m_" + n], given["v_" + n])
    return (loss, grad_x, *[grad_w[n] for n in TWIN_WEIGHTS], *[delta_w[n] for n in TWIN_WEIGHTS],
            *[new_m[n] for n in TWIN_WEIGHTS], *[new_v[n] for n in TWIN_WEIGHTS])
```

```python
import math

import jax
import jax.numpy as jnp
from jax import lax
from jax.experimental import pallas as pl
from jax.experimental.pallas import tpu as pltpu

F32 = jnp.float32
BF16 = jnp.bfloat16
HI = lax.Precision.HIGHEST
MESH = pl.DeviceIdType.MESH

D = 1024
DI = 2048
CD = 3072
NH = 32
HD = 64
NST = 128
NG = 4
CH = 128
PLE = 256
NIN = 10272
NDEV = 8
WSH = NIN // NDEV
EPS = 1e-6
OFF_XBC, OFF_U, OFF_ZB, OFF_V, OFF_ZA, OFF_GA, OFF_GB, OFF_DT = 0, 3072, 4096, 6144, 7168, 8192, 9216, 10240
NP = 10368
DTW = 128
R_OA, R_OB, R_OUT, R_PG, R_PLE, R_ROWS = 0, 128, 384, 512, 640, 672

ADAM_LR, ADAM_B1, ADAM_B2, ADAM_EPS, ADAM_WD, ADAM_STEP = 0.001, 0.9, 0.999, 1e-08, 0.01, 10

V7X_VMEM_LIMIT = 56 * 1024 * 1024


def _cp(sem=None):
    return pltpu.CompilerParams(dimension_semantics=sem, vmem_limit_bytes=V7X_VMEM_LIMIT)


def _dot(a, b, prec=None):
    return jnp.dot(a, b, preferred_element_type=F32, precision=prec)


def _dot_nt(a, b, prec=None):
    return lax.dot_general(a, b, (((1,), (1,)), ((), ())), preferred_element_type=F32, precision=prec)


def _dot_tn(a, b, prec=None):
    return lax.dot_general(a, b, (((0,), (0,)), ((), ())), preferred_element_type=F32, precision=prec)


def _sigmoid(x):
    return 1.0 / (1.0 + jnp.exp(-x))


def _gelu_and_grad(x):
    c = math.sqrt(2.0 / math.pi)
    x2 = x * x
    t = jnp.tanh(c * (x + 0.044715 * x * x2))
    g = 0.5 * x * (1.0 + t)
    dg = 0.5 * (1.0 + t) + 0.5 * x * (1.0 - t * t) * c * (1.0 + 3.0 * 0.044715 * x2)
    return g, dg


def _gelu(x):
    c = math.sqrt(2.0 / math.pi)
    return 0.5 * x * (1.0 + jnp.tanh(c * (x + 0.044715 * x * x * x)))


def _softplus(x):
    return jnp.maximum(x, 0.0) + jnp.log(1.0 + jnp.exp(-jnp.abs(x)))


def _full(shape):
    n = len(shape)
    return pl.BlockSpec(shape, lambda *_: (0,) * n)


def _proj_fwd(x, norm_g, wp):
    T = x.shape[0]
    tm, tn = min(T, 1024), 1152

    def body(x_ref, g_ref, w_ref, proj_ref, h_ref, hs_ref):
        @pl.when(pl.program_id(1) == 0)
        def _():
            xf = x_ref[...]
            r = lax.rsqrt(jnp.mean(xf * xf, axis=-1, keepdims=True) + EPS)
            h = (xf * r * g_ref[...]).astype(BF16)
            hs_ref[...] = h
            h_ref[...] = h

        proj_ref[...] = _dot(hs_ref[...], w_ref[...])

    return pl.pallas_call(
        body, name="proj_fwd", grid=(T // tm, NP // tn),
        in_specs=[pl.BlockSpec((tm, D), lambda i, j: (i, 0)), _full((1, D)), pl.BlockSpec((D, tn), lambda i, j: (0, j))],
        out_specs=[pl.BlockSpec((tm, tn), lambda i, j: (i, j)), pl.BlockSpec((tm, D), lambda i, j: (i, 0))],
        out_shape=[jax.ShapeDtypeStruct((T, NP), F32), jax.ShapeDtypeStruct((T, D), BF16)],
        scratch_shapes=[pltpu.VMEM((tm, D), BF16)],
        compiler_params=_cp(("parallel", "arbitrary")),
    )(x, norm_g, wp)


def _gmlp_tile():
    return 256


def _gmlp_fwd(proj, ln_g, ln_b, ws, bst):
    T = proj.shape[0]
    tm = min(T, _gmlp_tile())

    def body(u_ref, v_ref, z_ref, lg_ref, lb_ref, ws_ref, bs_ref, ya_ref, vn_s):
        vg = _gelu(v_ref[...])
        mu = jnp.mean(vg, axis=-1, keepdims=True)
        xc = vg - mu
        rstd = lax.rsqrt(jnp.mean(xc * xc, axis=-1, keepdims=True) + EPS)
        vn_s[...] = (xc * rstd * lg_ref[...] + lb_ref[...]).astype(BF16)
        for c in range(tm // CH):
            rs = slice(c * CH, (c + 1) * CH)
            for g in range(NG):
                cs_ = slice(g * 256, (g + 1) * 256)
                sv = _dot(ws_ref[g], vn_s[rs, cs_]) + bs_ref[g]
                z = z_ref[rs, cs_]
                ya_ref[rs, cs_] = (_gelu(u_ref[rs, cs_]) * sv * (z * _sigmoid(z))).astype(BF16)

    blk = lambda off: pl.BlockSpec((tm, D), lambda i: (i, off // D))
    return pl.pallas_call(
        body, name="gmlp_fwd", grid=(T // tm,),
        in_specs=[blk(OFF_U), blk(OFF_V), blk(OFF_ZA), _full((1, D)), _full((1, D)), _full((NG, CH, CH)), _full((NG, CH, 256))],
        out_specs=pl.BlockSpec((tm, D), lambda i: (i, 0)),
        out_shape=jax.ShapeDtypeStruct((T, D), BF16),
        scratch_shapes=[pltpu.VMEM((tm, D), BF16)],
        compiler_params=_cp(("parallel",)),
    )(proj, proj, proj, ln_g, ln_b, ws, bst)


def _gmlp_bwd(proj, dya, ln_g, ln_b, ws, wst, bst, gsel):
    T = proj.shape[0]
    tm = min(T, _gmlp_tile())

    def body(u_ref, v_ref, z_ref, dy_ref, lg_ref, lb_ref, ws_ref, wst_ref, bs_ref, gsel_ref,
             du_ref, dv_ref, dz_ref, dws_ref, dbs_ref, dln_ref, vn_s, dsv_s, dvn_s):
        @pl.when(pl.program_id(0) == 0)
        def _():
            dws_ref[...] = jnp.zeros_like(dws_ref)
            dbs_ref[...] = jnp.zeros_like(dbs_ref)
            dln_ref[...] = jnp.zeros_like(dln_ref)

        vg, dvg_dv = _gelu_and_grad(v_ref[...])
        mu = jnp.mean(vg, axis=-1, keepdims=True)
        xc = vg - mu
        rstd = lax.rsqrt(jnp.mean(xc * xc, axis=-1, keepdims=True) + EPS)
        vhat = xc * rstd
        vn_s[...] = (vhat * lg_ref[...] + lb_ref[...]).astype(BF16)
        ri = lax.broadcasted_iota(jnp.int32, (CH, CH), 0)
        ci = lax.broadcasted_iota(jnp.int32, (CH, CH), 1)
        tril = (ri >= ci).astype(F32)
        for c in range(tm // CH):
            rs = slice(c * CH, (c + 1) * CH)
            for g in range(NG):
                cs_ = slice(g * 256, (g + 1) * 256)
                vn = vn_s[rs, cs_]
                sv = _dot(ws_ref[g], vn) + bs_ref[g]
                z = z_ref[rs, cs_]
                sz = _sigmoid(z)
                ug, dug_du = _gelu_and_grad(u_ref[rs, cs_])
                dy = dy_ref[rs, cs_].astype(F32)
                t = dy * z * sz
                du_ref[rs, cs_] = (t * sv * dug_du).astype(BF16)
                dz_ref[rs, cs_] = (dy * ug * sv * sz * (1.0 + z * (1.0 - sz))).astype(BF16)
                dsv = (t * ug).astype(BF16)
                dsv_s[rs, cs_] = dsv
                dvn_s[rs, cs_] = _dot(wst_ref[g], dsv)
                dws_ref[g] += _dot_nt(dsv, vn) * tril
            dbs_ref[...] += _dot(dsv_s[rs, :], gsel_ref[...])
        dvn = dvn_s[...]
        dln_ref[0:1, :] += jnp.sum(dvn * vhat, axis=0, keepdims=True)
        dln_ref[1:2, :] += jnp.sum(dvn, axis=0, keepdims=True)
        dvh = dvn * lg_ref[...]
        dvg = rstd * (dvh - jnp.mean(dvh, axis=-1, keepdims=True) - vhat * jnp.mean(dvh * vhat, axis=-1, keepdims=True))
        dv_ref[...] = (dvg * dvg_dv).astype(BF16)

    blk = lambda off: pl.BlockSpec((tm, D), lambda i: (i, off // D))
    row = pl.BlockSpec((tm, D), lambda i: (i, 0))
    return pl.pallas_call(
        body, name="gmlp_bwd", grid=(T // tm,),
        in_specs=[blk(OFF_U), blk(OFF_V), blk(OFF_ZA), row, _full((1, D)), _full((1, D)), _full((NG, CH, CH)),
                  _full((NG, CH, CH)), _full((NG, CH, 256)), _full((D, 128))],
        out_specs=[row, row, row, _full((NG, CH, CH)), _full((CH, 128)), _full((8, D))],
        out_shape=[jax.ShapeDtypeStruct((T, D), BF16)] * 3
        + [jax.ShapeDtypeStruct((NG, CH, CH), F32), jax.ShapeDtypeStruct((CH, 128), F32), jax.ShapeDtypeStruct((8, D), F32)],
        scratch_shapes=[pltpu.VMEM((tm, D), BF16), pltpu.VMEM((tm, D), BF16), pltpu.VMEM((tm, D), F32)],
        compiler_params=_cp(("arbitrary",)),
    )(proj, proj, proj, dya, ln_g, ln_b, ws, wst, bst, gsel)


def _conv_pre(x, xp, cw_ref, cb_ref):
    row = lax.broadcasted_iota(jnp.int32, (CH, 1), 0)
    shifted = [x]
    for j in (1, 2, 3):
        shifted.append(jnp.where(row >= j, pltpu.roll(x, j, 0), pltpu.roll(xp, j, 0)))
    pre = cb_ref[...] + cw_ref[3:4, :] * x
    for j in (1, 2, 3):
        pre = pre + cw_ref[3 - j:4 - j, :] * shifted[j]
    return pre, shifted


def _chunk_decays(dt, alog_ref, cs_s, cst_s):
    a = -jnp.exp(alog_ref[...])
    ri = lax.broadcasted_iota(jnp.int32, (CH, CH), 0)
    ci = lax.broadcasted_iota(jnp.int32, (CH, CH), 1)
    tril = ri >= ci
    cs = _dot(tril.astype(F32), dt * a, HI)
    cs_s[...] = cs
    cst_s[...] = cs.T
    return a, cs, tril, ci


def _lmat(cst_s, h, tril):
    rowb = jnp.broadcast_to(cst_s[h:h + 1, :], (CH, CH))
    return jnp.exp(jnp.where(tril, rowb.T - rowb, -jnp.inf))


def _ssd_fwd(proj, cw, cb, dtb, alog, dskx, sg, e128, et128):
    T = proj.shape[0]
    nc = T // CH

    def body(xbc_ref, zb_ref, dt_ref, cw_ref, cb_ref, dtb_ref, alog_ref, dx_ref, sg_ref, e_ref, et_ref,
             y_ref, yb_ref, hp_ref, xprev_s, h_s, cs_s, cst_s, yz_s):
        @pl.when(pl.program_id(0) == 0)
        def _():
            xprev_s[...] = jnp.zeros_like(xprev_s)
            h_s[...] = jnp.zeros_like(h_s)

        x = xbc_ref[...]
        pre, _ = _conv_pre(x, xprev_s[...], cw_ref, cb_ref)
        xprev_s[...] = x
        xc = pre * _sigmoid(pre)
        dt = _softplus(dt_ref[...] + dtb_ref[...])
        a, cs, tril, lane = _chunk_decays(dt, alog_ref, cs_s, cst_s)
        e = e_ref[...]
        dt_x = _dot(dt, e, HI)
        ecs_x = _dot(jnp.exp(cs), e, HI)
        cs_last = cs_s[CH - 1:CH, :]
        dec_x = _dot(jnp.exp(cs_last - cs), e, HI)
        cdmat = jnp.broadcast_to(jnp.exp(cs_last), (CH, CH)).T
        cd_full = _dot(et_ref[...], cdmat, HI)
        hp_ref[0] = h_s[...]
        for g in range(NG):
            gs = slice(g * 512, (g + 1) * 512)
            bg = xc[:, DI + g * NST:DI + (g + 1) * NST].astype(BF16)
            cg = xc[:, DI + 512 + g * NST:DI + 512 + (g + 1) * NST].astype(BF16)
            cbm = _dot_nt(cg, bg)
            xg = xc[:, gs]
            xdt = xg * dt_x[:, gs]
            hprev = h_s[gs, :]
            yoff = _dot_nt(cg, hprev.astype(BF16)) * ecs_x[:, gs]
            st = _dot_tn((xdt * dec_x[:, gs]).astype(BF16), bg)
            h_s[gs, :] = cd_full[gs, :] * hprev + st
            ssq = jnp.zeros((CH, 1), F32)
            for q in range(4):
                h0 = g * 8 + 2 * q
                ps = slice(q * 128, (q + 1) * 128)
                cols = slice(g * 512 + q * 128, g * 512 + (q + 1) * 128)
                xp = xdt[:, ps].astype(BF16)
                y0 = _dot((cbm * _lmat(cst_s, h0, tril)).astype(BF16), xp)
                y1 = _dot((cbm * _lmat(cst_s, h0 + 1, tril)).astype(BF16), xp)
                yq = jnp.where(lane < HD, y0, y1) + yoff[:, ps] + xg[:, ps] * dx_ref[:, cols]
                y_ref[:, cols] = yq
                z = zb_ref[:, cols]
                yz = yq * z * _sigmoid(z)
                yz_s[:, cols] = yz
                ssq = ssq + jnp.sum(yz * yz, axis=1, keepdims=True)
            rg = lax.rsqrt(ssq * (1.0 / 512.0) + EPS)
            yb_ref[:, gs] = (yz_s[:, gs] * rg * sg_ref[:, gs]).astype(BF16)

    return pl.pallas_call(
        body, name="ssd_fwd", grid=(nc,),
        in_specs=[pl.BlockSpec((CH, CD), lambda c: (c, OFF_XBC // CD)), pl.BlockSpec((CH, DI), lambda c: (c, OFF_ZB // DI)),
                  pl.BlockSpec((CH, DTW), lambda c: (c, OFF_DT // DTW)), _full((4, CD)), _full((1, CD)), _full((1, DTW)),
                  _full((1, DTW)), _full((1, DI)), _full((1, DI)), _full((DTW, DI)), _full((DI, DTW))],
        out_specs=[pl.BlockSpec((CH, DI), lambda c: (c, 0)), pl.BlockSpec((CH, DI), lambda c: (c, 0)),
                   pl.BlockSpec((1, DI, NST), lambda c: (c, 0, 0))],
        out_shape=[jax.ShapeDtypeStruct((T, DI), F32), jax.ShapeDtypeStruct((T, DI), BF16),
                   jax.ShapeDtypeStruct((nc, DI, NST), F32)],
        scratch_shapes=[pltpu.VMEM((CH, CD), F32), pltpu.VMEM((DI, NST), F32), pltpu.VMEM((CH, CH), F32),
                        pltpu.VMEM((CH, CH), F32), pltpu.VMEM((CH, DI), F32)],
        compiler_params=_cp(("arbitrary",)),
    )(proj, proj, proj, cw, cb, dtb, alog, dskx, sg, e128, et128)


def _ssd_bwd(proj, y, dyb, hprev_all, cw, cb, dtb, alog, dskx, sg, e128, et128):
    T = proj.shape[0]
    nc = T // CH

    def body(xbc_ref, xbcp_ref, zb_ref, dt_ref, y_ref, dyb_ref, hp_ref, cw_ref, cb_ref, dtb_ref, alog_ref, dx_ref, sg_ref,
             e_ref, et_ref, dxbc_ref, dzb_ref, ddt_ref, dcw_ref, dsm_ref, dsg_ref,
             g_s, dpn_s, cs_s, cst_s, dy_s, dxdt_s, dxs_s, dsd_s, dxc_s):
        i = pl.program_id(0)

        @pl.when(i == 0)
        def _():
            g_s[...] = jnp.zeros_like(g_s)
            dpn_s[...] = jnp.zeros_like(dpn_s)
            dcw_ref[...] = jnp.zeros_like(dcw_ref)
            dsm_ref[...] = jnp.zeros_like(dsm_ref)
            dsg_ref[...] = jnp.zeros_like(dsg_ref)

        x = xbc_ref[...]
        xp = jnp.where(i == nc - 1, 0.0, xbcp_ref[...])
        pre, shifted = _conv_pre(x, xp, cw_ref, cb_ref)
        sp = _sigmoid(pre)
        xc = pre * sp
        dtr = dt_ref[...] + dtb_ref[...]
        dt = _softplus(dtr)
        a, cs, tril, lane = _chunk_decays(dt, alog_ref, cs_s, cst_s)
        e = e_ref[...]
        et = et_ref[...]
        dt_x = _dot(dt, e, HI)
        ecs_x = _dot(jnp.exp(cs), e, HI)
        cs_last = cs_s[CH - 1:CH, :]
        dec_x = _dot(jnp.exp(cs_last - cs), e, HI)
        cd_full = _dot(et, jnp.broadcast_to(jnp.exp(cs_last), (CH, CH)).T, HI)

        for g in range(NG):
            gs = slice(g * 512, (g + 1) * 512)
            z = zb_ref[:, gs]
            sz = _sigmoid(z)
            yv = y_ref[:, gs]
            yz = yv * z * sz
            rg = lax.rsqrt(jnp.mean(yz * yz, axis=-1, keepdims=True) + EPS)
            yn = yz * rg
            dyb = dyb_ref[:, gs].astype(F32)
            dsg_ref[0:1, gs] += jnp.sum(dyb * yn, axis=0, keepdims=True)
            dyn = dyb * sg_ref[:, gs]
            dyz = rg * (dyn - yn * jnp.mean(dyn * yn, axis=-1, keepdims=True))
            dy_s[:, gs] = dyz * z * sz
            dzb_ref[:, gs] = (dyz * yv * sz * (1.0 + z * (1.0 - sz))).astype(BF16)

        gh = jnp.zeros((CH, NST), F32)
        ri = lax.broadcasted_iota(jnp.int32, (CH, CH), 0)
        rsum = jnp.zeros((CH, DTW), F32)
        csum_t = jnp.zeros((DTW, CH), F32)
        for g in range(NG):
            gs = slice(g * 512, (g + 1) * 512)
            bg = xc[:, DI + g * NST:DI + (g + 1) * NST].astype(BF16)
            cg = xc[:, DI + 512 + g * NST:DI + 512 + (g + 1) * NST].astype(BF16)
            cbm = _dot_nt(cg, bg)
            xdt = xc[:, gs] * dt_x[:, gs]
            hprev = hp_ref[0, gs, :]
            gn = g_s[gs, :]
            gnb = gn.astype(BF16)
            dy = dy_s[:, gs]
            dye = (dy * ecs_x[:, gs]).astype(BF16)
            dc = _dot(dye, hprev.astype(BF16))
            dprev = _dot_tn(dye, cg)
            dxdt_state = dec_x[:, gs] * _dot_nt(bg, gnb)
            db = _dot((xdt * dec_x[:, gs]).astype(BF16), gnb)
            dcb = jnp.zeros((CH, CH), F32)
            for q in range(4):
                h0 = g * 8 + 2 * q
                ps = slice(q * 128, (q + 1) * 128)
                xpb = xdt[:, ps].astype(BF16)
                dyp = dy[:, ps]
                dy0 = jnp.where(lane < HD, dyp, 0.0).astype(BF16)
                dy1 = jnp.where(lane < HD, 0.0, dyp).astype(BF16)
                l0 = _lmat(cst_s, h0, tril)
                l1 = _lmat(cst_s, h0 + 1, tril)
                m0 = cbm * l0
                m1 = cbm * l1
                dm0 = _dot_nt(dy0, xpb)
                dm1 = _dot_nt(dy1, xpb)
                dcb = dcb + dm0 * l0 + dm1 * l1
                for hh, qm in ((h0, dm0 * m0), (h0 + 1, dm1 * m1)):
                    rsum = jnp.where(lane == hh, jnp.sum(qm, axis=1, keepdims=True), rsum)
                    csum_t = jnp.where(ri == hh, jnp.sum(qm, axis=0, keepdims=True), csum_t)
                dypb = dyp.astype(BF16)
                d0 = _dot_tn(m0.astype(BF16), dypb)
                d1 = _dot_tn(m1.astype(BF16), dypb)
                dxdt_s[:, g * 512 + q * 128:g * 512 + (q + 1) * 128] = jnp.where(lane < HD, d0, d1) + dxdt_state[:, ps]
            yoff = _dot_nt(cg, hprev.astype(BF16)) * ecs_x[:, gs]
            dxs_s[:, gs] = dy * yoff - xdt * dxdt_state
            dsd_s[:, gs] = xdt * dxdt_state
            dcbb = dcb.astype(BF16)
            dxc_s[:, DI + 512 + g * NST:DI + 512 + (g + 1) * NST] = dc + _dot(dcbb, bg)
            dxc_s[:, DI + g * NST:DI + (g + 1) * NST] = db + _dot_tn(dcbb, cg)
            gh = gh + _dot(e[:, gs], gn * hprev, HI)
            g_s[gs, :] = dprev + cd_full[gs, :] * gn

        xs = xc[:, :DI]
        dy = dy_s[...]
        dxdt = dxdt_s[...]
        dcs = rsum - csum_t.T + _dot(dxs_s[...], et, HI)
        sdec = jnp.sum(_dot(dsd_s[...], et, HI), axis=0, keepdims=True)
        dcd = 0.125 * jnp.sum(_dot_nt(jnp.ones((8, NST), F32), gh, HI), axis=0, keepdims=True) * jnp.exp(cs_last)
        row = lax.broadcasted_iota(jnp.int32, (CH, 1), 0)
        dcs = dcs + jnp.where(row == CH - 1, sdec + dcd, 0.0)
        dda = _dot((lane >= ri).astype(F32), dcs, HI)
        ddt = dda * a + _dot(dxdt * xs, et, HI)
        ddtr = jnp.where(lane < NH, ddt * _sigmoid(dtr), 0.0)
        ddt_ref[...] = ddtr.astype(BF16)
        dsm_ref[0:1, :] += jnp.sum(ddtr, axis=0, keepdims=True)
        dsm_ref[1:2, :] += jnp.sum(dda * dt, axis=0, keepdims=True) * a
        dsm_ref[2:3, :] += jnp.sum(_dot(dy * xs, et, HI), axis=0, keepdims=True)
        dxc_s[:, :DI] = dxdt * dt_x + dy * dx_ref[...]

        dpre = dxc_s[...] * sp * (1.0 + pre * (1.0 - sp))
        dpn = dpn_s[...]
        dcw_ref[4:5, :] += jnp.sum(dpre, axis=0, keepdims=True)
        dxbc = cw_ref[3:4, :] * dpre
        dcw_ref[3:4, :] += jnp.sum(dpre * shifted[0], axis=0, keepdims=True)
        for j in (1, 2, 3):
            dcw_ref[3 - j:4 - j, :] += jnp.sum(dpre * shifted[j], axis=0, keepdims=True)
            ahead = jnp.where(row < CH - j, pltpu.roll(dpre, CH - j, 0), pltpu.roll(dpn, CH - j, 0))
            dxbc = dxbc + cw_ref[3 - j:4 - j, :] * ahead
        dpn_s[...] = dpre
        dxbc_ref[...] = dxbc.astype(BF16)

    rev = lambda c: nc - 1 - c
    return pl.pallas_call(
        body, name="ssd_bwd", grid=(nc,),
        in_specs=[pl.BlockSpec((CH, CD), lambda c: (rev(c), OFF_XBC // CD)),
                  pl.BlockSpec((CH, CD), lambda c: (jnp.maximum(rev(c) - 1, 0), OFF_XBC // CD)),
                  pl.BlockSpec((CH, DI), lambda c: (rev(c), OFF_ZB // DI)),
                  pl.BlockSpec((CH, DTW), lambda c: (rev(c), OFF_DT // DTW)),
                  pl.BlockSpec((CH, DI), lambda c: (rev(c), 0)), pl.BlockSpec((CH, DI), lambda c: (rev(c), 0)),
                  pl.BlockSpec((1, DI, NST), lambda c: (rev(c), 0, 0)),
                  _full((4, CD)), _full((1, CD)), _full((1, DTW)), _full((1, DTW)), _full((1, DI)), _full((1, DI)),
                  _full((DTW, DI)), _full((DI, DTW))],
        out_specs=[pl.BlockSpec((CH, CD), lambda c: (rev(c), 0)), pl.BlockSpec((CH, DI), lambda c: (rev(c), 0)),
                   pl.BlockSpec((CH, DTW), lambda c: (rev(c), 0)), _full((8, CD)), _full((8, DTW)), _full((8, DI))],
        out_shape=[jax.ShapeDtypeStruct((T, CD), BF16), jax.ShapeDtypeStruct((T, DI), BF16), jax.ShapeDtypeStruct((T, DTW), BF16),
                   jax.ShapeDtypeStruct((8, CD), F32), jax.ShapeDtypeStruct((8, DTW), F32), jax.ShapeDtypeStruct((8, DI), F32)],
        scratch_shapes=[pltpu.VMEM((DI, NST), F32), pltpu.VMEM((CH, CD), F32), pltpu.VMEM((CH, CH), F32), pltpu.VMEM((CH, CH), F32),
                        pltpu.VMEM((CH, DI), F32), pltpu.VMEM((CH, DI), F32), pltpu.VMEM((CH, DI), F32), pltpu.VMEM((CH, DI), F32),
                        pltpu.VMEM((CH, CD), F32)],
        compiler_params=_cp(("arbitrary",)),
    )(proj, proj, proj, proj, y, dyb, hprev_all, cw, cb, dtb, alog, dskx, sg, e128, et128)


def _merge_tile():
    return 256


def _merge_fwd(x, ya, yb, proj, p, tgt, w_oa, w_ob, w_out, w_pg, w_ple, w_pgt, ple_g, fin_g):
    T = x.shape[0]
    tm = min(T, _merge_tile())

    def body(x_ref, ya_ref, yb_ref, ga_ref, gb_ref, p_ref, t_ref, woa, wob, wout, wpg, wple, wpgt, pg_ref, fg_ref,
             dx1_ref, oa_ref, ob_ref, mg_ref, hp_ref, dpre_ref, dpe_ref, acc_ref):
        @pl.when(pl.program_id(0) == 0)
        def _():
            acc_ref[...] = jnp.zeros_like(acc_ref)

        oa = _dot(ya_ref[...], woa[...])
        ob = _dot(yb_ref[...], wob[...])
        oa_ref[...] = oa.astype(BF16)
        ob_ref[...] = ob.astype(BF16)
        mg = _sigmoid(ga_ref[...]) * oa + _sigmoid(gb_ref[...]) * ob
        mgb = mg.astype(BF16)
        mg_ref[...] = mgb
        x1 = x_ref[...] + _dot(mgb, wout[...])
        r2 = lax.rsqrt(jnp.mean(x1 * x1, axis=-1, keepdims=True) + EPS)
        xh1 = x1 * r2
        hpb = (xh1 * pg_ref[...]).astype(BF16)
        hp_ref[...] = hpb
        gate = _sigmoid(_dot(hpb, wpg[...]))
        pe = _dot(p_ref[...].astype(BF16), wple[...])
        x2 = x1 + gate * pe
        r3 = lax.rsqrt(jnp.mean(x2 * x2, axis=-1, keepdims=True) + EPS)
        xh2 = x2 * r3
        err = xh2 * fg_ref[...] - t_ref[...]
        acc_ref[2:3, :] += 0.5 * jnp.sum(jnp.mean(err * err, axis=-1, keepdims=True))
        dyo = err * (1.0 / D)
        acc_ref[0:1, :] += jnp.sum(dyo * xh2, axis=0, keepdims=True)
        dn = dyo * fg_ref[...]
        dx2 = r3 * (dn - xh2 * jnp.mean(dn * xh2, axis=-1, keepdims=True))
        dpe_ref[...] = (dx2 * gate).astype(BF16)
        dpre = (dx2 * pe * gate * (1.0 - gate)).astype(BF16)
        dpre_ref[...] = dpre
        dhp = _dot(dpre, wpgt[...])
        acc_ref[1:2, :] += jnp.sum(dhp * xh1, axis=0, keepdims=True)
        dhn = dhp * pg_ref[...]
        dx1_ref[...] = dx2 + r2 * (dhn - xh1 * jnp.mean(dhn * xh1, axis=-1, keepdims=True))

    row = lambda w: pl.BlockSpec((tm, w), lambda i: (i, 0))
    wsp = lambda s: pl.BlockSpec(s, lambda i: (0, 0), pipeline_mode=pl.Buffered(1))
    return pl.pallas_call(
        body, name="merge_fwd", grid=(T // tm,),
        in_specs=[row(D), row(D), row(DI), pl.BlockSpec((tm, D), lambda i: (i, OFF_GA // D)),
                  pl.BlockSpec((tm, D), lambda i: (i, OFF_GB // D)), row(PLE), row(D),
                  wsp((D, D)), wsp((DI, D)), wsp((D, D)), wsp((D, D)), wsp((PLE, D)), wsp((D, D)), _full((1, D)), _full((1, D))],
        out_specs=[row(D)] * 7 + [_full((8, D))],
        out_shape=[jax.ShapeDtypeStruct((T, D), F32)] + [jax.ShapeDtypeStruct((T, D), BF16)] * 6 + [jax.ShapeDtypeStruct((8, D), F32)],
        compiler_params=_cp(("arbitrary",)),
    )(x, ya, yb, proj, proj, p, tgt, w_oa, w_ob, w_out, w_pg, w_ple, w_pgt, ple_g, fin_g)


def _merge_bwd(dx1, oa, ob, proj, w_outt, w_oat, w_obt):
    T = dx1.shape[0]
    tm = min(T, _merge_tile())

    def body(dx1_ref, oa_ref, ob_ref, ga_ref, gb_ref, woutt, woat, wobt, doa_ref, dob_ref, dya_ref, dyb_ref, dg_ref):
        dmg = _dot(dx1_ref[...].astype(BF16), woutt[...])
        sa = _sigmoid(ga_ref[...])
        sb = _sigmoid(gb_ref[...])
        doa = (dmg * sa).astype(BF16)
        dob = (dmg * sb).astype(BF16)
        doa_ref[...] = doa
        dob_ref[...] = dob
        dg_ref[:, :D] = (dmg * oa_ref[...].astype(F32) * sa * (1.0 - sa)).astype(BF16)
        dg_ref[:, D:] = (dmg * ob_ref[...].astype(F32) * sb * (1.0 - sb)).astype(BF16)
        dya_ref[...] = _dot(doa, woat[...]).astype(BF16)
        dyb_ref[...] = _dot(dob, wobt[...]).astype(BF16)

    row = lambda w: pl.BlockSpec((tm, w), lambda i: (i, 0))
    wsp = lambda s: pl.BlockSpec(s, lambda i: (0, 0), pipeline_mode=pl.Buffered(1))
    return pl.pallas_call(
        body, name="merge_bwd", grid=(T // tm,),
        in_specs=[row(D), row(D), row(D), pl.BlockSpec((tm, D), lambda i: (i, OFF_GA // D)),
                  pl.BlockSpec((tm, D), lambda i: (i, OFF_GB // D)), wsp((D, D)), wsp((D, D)), wsp((D, DI))],
        out_specs=[row(D), row(D), row(D), row(DI), row(2 * D)],
        out_shape=[jax.ShapeDtypeStruct((T, D), BF16)] * 3 + [jax.ShapeDtypeStruct((T, DI), BF16), jax.ShapeDtypeStruct((T, 2 * D), BF16)],
        compiler_params=_cp(("parallel",)),
    )(dx1, oa, ob, proj, proj, w_outt, w_oat, w_obt)


def _wgrad(a, b, name):
    T, K = a.shape
    N = b.shape[1]
    tt, tn = min(T, 1024), min(N, 1024)

    def body(a_ref, b_ref, o_ref):
        @pl.when(pl.program_id(1) == 0)
        def _():
            o_ref[...] = jnp.zeros_like(o_ref)

        o_ref[...] += _dot_tn(a_ref[...], b_ref[...])

    return pl.pallas_call(
        body, name=name, grid=(N // tn, T // tt),
        in_specs=[pl.BlockSpec((tt, K), lambda n, t: (t, 0)), pl.BlockSpec((tt, tn), lambda n, t: (t, n))],
        out_specs=pl.BlockSpec((K, tn), lambda n, t: (0, n)),
        out_shape=jax.ShapeDtypeStruct((K, N), F32),
        compiler_params=_cp(("parallel", "arbitrary")),
    )(a, b)


def _proj_bwd(x, dx1, norm_g, wpt, dxbc, du, dzb, dv, dza, dgab, ddt):
    T = x.shape[0]
    tm = min(T, 512)
    nk = OFF_DT // D + 1
    pieces = [(dxbc, OFF_XBC), (du, OFF_U), (dzb, OFF_ZB), (dv, OFF_V), (dza, OFF_ZA), (dgab, OFF_GA)]
    ranges = [(off // D, off // D + arr.shape[1] // D) for arr, off in pieces]

    def body(x_ref, dx1_ref, g_ref, w_ref, wdt_ref, *rest):
        piece_refs, ddt_ref, gx_ref, dng_ref, acc_s = rest[:6], rest[6], rest[7], rest[8], rest[9]
        i, k = pl.program_id(0), pl.program_id(1)

        @pl.when((i == 0) & (k == 0))
        def _():
            dng_ref[...] = jnp.zeros_like(dng_ref)

        @pl.when(k == 0)
        def _():
            acc_s[...] = jnp.zeros_like(acc_s)

        for ref, (lo, hi) in zip(piece_refs, ranges):
            @pl.when((k >= lo) & (k < hi))
            def _(ref=ref):
                acc_s[...] += _dot(ref[...], w_ref[...])

        @pl.when(k == nk - 1)
        def _():
            dh = acc_s[...] + _dot(ddt_ref[...], wdt_ref[...])
            xf = x_ref[...]
            r = lax.rsqrt(jnp.mean(xf * xf, axis=-1, keepdims=True) + EPS)
            xh = xf * r
            dng_ref[0:1, :] += jnp.sum(dh * xh, axis=0, keepdims=True)
            dxn = dh * g_ref[...]
            gx_ref[...] = dx1_ref[...] + r * (dxn - xh * jnp.mean(dxn * xh, axis=-1, keepdims=True))

    def piece_spec(lo, hi):
        return pl.BlockSpec((tm, D), lambda i, k: (i, jnp.clip(k - lo, 0, hi - lo - 1)))

    row = pl.BlockSpec((tm, D), lambda i, k: (i, 0))
    return pl.pallas_call(
        body, name="proj_bwd", grid=(T // tm, nk),
        in_specs=[row, row, _full((1, D)), pl.BlockSpec((D, D), lambda i, k: (jnp.minimum(k, nk - 2), 0)),
                  pl.BlockSpec((DTW, D), lambda i, k: (OFF_DT // DTW, 0))]
        + [piece_spec(lo, hi) for lo, hi in ranges] + [pl.BlockSpec((tm, DTW), lambda i, k: (i, 0))],
        out_specs=[row, _full((8, D))],
        out_shape=[jax.ShapeDtypeStruct((T, D), F32), jax.ShapeDtypeStruct((8, D), F32)],
        scratch_shapes=[pltpu.VMEM((tm, D), F32)],
        compiler_params=_cp(("arbitrary", "arbitrary")),
    )(x, dx1, norm_g, wpt, wpt, *[arr for arr, _ in pieces], ddt)


def _row_tile(R, C, limit=1 << 20):
    if R * C * 4 <= limit:
        return R
    fits = [t for t in range(16, R, 16) if R % t == 0 and t * C * 4 <= limit]
    return fits[-1]


def _adamw(w, m, v, parts, name):
    R, C = w.shape
    tr = _row_tile(R, C)
    c1 = 1.0 - ADAM_B1 ** ADAM_STEP
    c2 = 1.0 - ADAM_B2 ** ADAM_STEP
    n = len(parts)

    def body(*refs):
        w_ref, m_ref, v_ref = refs[:3]
        g_ref, d_ref, nm_ref, nv_ref = refs[3 + n:]
        g = refs[3][...].astype(F32)
        for r in refs[4:3 + n]:
            g = g + r[...].astype(F32)
        mm = ADAM_B1 * m_ref[...] + (1.0 - ADAM_B1) * g
        vv = ADAM_B2 * v_ref[...] + (1.0 - ADAM_B2) * (g * g)
        g_ref[...] = g
        nm_ref[...] = mm
        nv_ref[...] = vv
        d_ref[...] = -ADAM_LR * ((mm / c1) / (jnp.sqrt(vv / c2) + ADAM_EPS) + ADAM_WD * w_ref[...])

    spec = pl.BlockSpec((tr, C), lambda i: (i, 0))
    return pl.pallas_call(
        body, name=name, grid=(R // tr,), in_specs=[spec] * (3 + n), out_specs=[spec] * 4,
        out_shape=[jax.ShapeDtypeStruct((R, C), F32)] * 4, compiler_params=_cp(("parallel",)),
    )(w, m, v, *parts)


def _pair_sum(a, b, name):
    Q, R, C = a.shape
    tr = _row_tile(R, C)

    def body(a_ref, b_ref, o_ref):
        o_ref[...] = (a_ref[...].astype(F32) + b_ref[...].astype(F32)).astype(BF16)

    spec = pl.BlockSpec((1, tr, C), lambda q, i: (q, i, 0))
    return pl.pallas_call(
        body, name=name, grid=(Q, R // tr), in_specs=[spec, spec], out_specs=spec,
        out_shape=jax.ShapeDtypeStruct((Q, R, C), BF16), compiler_params=_cp(("parallel", "parallel")),
    )(a, b)


_ANY = pl.BlockSpec(memory_space=pl.ANY)


def _dev_index(px, py, pc):
    return 4 * px + 2 * py + pc


def _all_gather(blocks):
    n = len(blocks)

    def body(*refs):
        ins, outs = refs[:n], refs[n:2 * n]
        send_sems, recv_sems, local_sems = refs[2 * n:]
        x, y, c = lax.axis_index("x"), lax.axis_index("y"), lax.axis_index("c")
        me, sibling = (x, y, c), (x, y, 1 - c)
        chips = [(1 - x, y), (x, 1 - y), (1 - x, 1 - y)]

        def copy(a, k, block, to, src=None):
            dst = outs[a].at[_dev_index(*block)]
            return pltpu.make_async_remote_copy(src_ref=dst if src is None else src, dst_ref=dst, send_sem=send_sems.at[a, k],
                                                recv_sem=recv_sems.at[a, k], device_id=to, device_id_type=MESH)

        mine = [pltpu.make_async_copy(ins[a], outs[a].at[_dev_index(*me)], local_sems.at[a]) for a in range(n)]
        for cp in mine:
            cp.start()
        first = []
        for a in range(n):
            first.append(copy(a, 0, me, sibling, src=ins[a]))
            first += [copy(a, 1 + j, me, (*chip, c), src=ins[a]) for j, chip in enumerate(chips)]
        for cp in first:
            cp.start()
        passed = []
        for j, chip in enumerate(chips):
            for a in range(n):
                copy(a, 1 + j, (*chip, c), me).wait_recv()
                fwd = copy(a, 4 + j, (*chip, c), sibling)
                fwd.start()
                passed.append(fwd)
        for a in range(n):
            copy(a, 0, sibling, me).wait_recv()
            for j, chip in enumerate(chips):
                copy(a, 4 + j, (*chip, 1 - c), me).wait_recv()
        for cp in first + passed:
            cp.wait_send()
        for cp in mine:
            cp.wait()

    return pl.pallas_call(
        body, name="all_gather_weights", in_specs=[_ANY] * n, out_specs=[_ANY] * n,
        out_shape=[jax.ShapeDtypeStruct((NDEV,) + b.shape, b.dtype) for b in blocks],
        scratch_shapes=[pltpu.SemaphoreType.DMA((n, 7)), pltpu.SemaphoreType.DMA((n, 7)), pltpu.SemaphoreType.DMA((n,))],
    )(*blocks)


def _exchange_sibling(grads):
    n = len(grads)

    def body(*refs):
        ins, outs = refs[:n], refs[n:2 * n]
        send_sems, recv_sems = refs[2 * n:]
        x, y, c = lax.axis_index("x"), lax.axis_index("y"), lax.axis_index("c")
        copies = []
        for a in range(n):
            for q in range(4):
                copies.append(pltpu.make_async_remote_copy(
                    src_ref=ins[a].at[2 * q + (1 - c)], dst_ref=outs[a].at[q], send_sem=send_sems.at[a, q],
                    recv_sem=recv_sems.at[a, q], device_id=(x, y, 1 - c), device_id_type=MESH))
        for cp in copies:
            cp.start()
        for cp in copies:
            cp.wait()

    return pl.pallas_call(
        body, name="reduce_scatter_sibling", in_specs=[_ANY] * n, out_specs=[_ANY] * n,
        out_shape=[jax.ShapeDtypeStruct((4,) + g.shape[1:], g.dtype) for g in grads],
        scratch_shapes=[pltpu.SemaphoreType.DMA((n, 4)), pltpu.SemaphoreType.DMA((n, 4))],
    )(*grads)


def _exchange_chips(partials, small):
    n = len(partials)

    def body(*refs):
        ins, small_ref = refs[:n], refs[n]
        outs, gath = refs[n + 1:2 * n + 1], refs[2 * n + 1]
        send_sems, recv_sems, ssend, srecv, local_sem = refs[2 * n + 2:]
        x, y, c = lax.axis_index("x"), lax.axis_index("y"), lax.axis_index("c")
        chips = [(1 - x, y), (x, 1 - y), (1 - x, 1 - y)]
        mine = pltpu.make_async_copy(small_ref, gath.at[_dev_index(x, y, c)], local_sem)
        mine.start()
        copies = []
        for a in range(n):
            for k, (px, py) in enumerate(chips):
                copies.append(pltpu.make_async_remote_copy(
                    src_ref=ins[a].at[2 * px + py], dst_ref=outs[a].at[k], send_sem=send_sems.at[a, k],
                    recv_sem=recv_sems.at[a, k], device_id=(px, py, c), device_id_type=MESH))
        for r in range(1, NDEV):
            bx, by, bc = (r >> 2) & 1, (r >> 1) & 1, r & 1
            peer = (x ^ bx, y ^ by, c ^ bc)
            copies.append(pltpu.make_async_remote_copy(
                src_ref=small_ref, dst_ref=gath.at[_dev_index(x, y, c)], send_sem=ssend.at[r - 1], recv_sem=srecv.at[r - 1],
                device_id=peer, device_id_type=MESH))
        for cp in copies:
            cp.start()
        for cp in copies:
            cp.wait()
        mine.wait()

    return pl.pallas_call(
        body, name="reduce_scatter_chips", in_specs=[_ANY] * (n + 1), out_specs=[_ANY] * (n + 1),
        out_shape=[jax.ShapeDtypeStruct((3,) + g.shape[1:], g.dtype) for g in partials]
        + [jax.ShapeDtypeStruct((NDEV,) + small.shape, small.dtype)],
        scratch_shapes=[pltpu.SemaphoreType.DMA((n, 3)), pltpu.SemaphoreType.DMA((n, 3)), pltpu.SemaphoreType.DMA((7,)),
                        pltpu.SemaphoreType.DMA((7,)), pltpu.SemaphoreType.DMA(())],
    )(*partials, small)


def _w_in_to_proj_layout(w):
    z = jnp.zeros((w.shape[0], NP - NIN), w.dtype)
    return jnp.concatenate([w[:, 5120:8192], w[:, 0:1024], w[:, 3072:5120], w[:, 1024:2048], w[:, 2048:3072],
                            w[:, 8224:9248], w[:, 9248:10272], w[:, 8192:8224], z], axis=1)


_SMALL = [("norm_g", 1024), ("ln_a_g", 1024), ("ln_a_b", 1024), ("w_s", 65536), ("b_s", 512), ("conv_w", 12288),
          ("conv_b", 3072), ("dt_bias", 128), ("a_log", 128), ("d_skip", 128), ("ssm_norm_g", 2048), ("ple_norm_g", 1024),
          ("final_g", 1024)]
_SMALL_ROWS = sum(s for _, s in _SMALL) // 128


def _pack_small(d):
    parts = []
    for name, size in _SMALL:
        if name in d:
            f = d[name].reshape(-1).astype(F32)
            parts.append(jnp.pad(f, (0, size - f.shape[0])))
        else:
            parts.append(jnp.zeros((size,), F32))
    return jnp.concatenate(parts).reshape(_SMALL_ROWS, 128)


def _unpack_small(packed, shapes):
    flat = packed.reshape(-1)
    out, off = {}, 0
    for name, size in _SMALL:
        if name in shapes:
            k = math.prod(shapes[name])
            out[name] = flat[off:off + k].reshape(shapes[name])
        off += size
    return out


def _local_step(x, p, tgt, sm, wp, w_oa, w_ob, w_out, w_pg, w_ple):
    norm_g = sm["norm_g"].reshape(1, D)
    ws = jnp.where(jnp.tril(jnp.ones((CH, CH), bool))[None], sm["w_s"].reshape(NG, CH, CH), 0.0).astype(BF16)
    wst = jnp.transpose(ws, (0, 2, 1))
    bst = jnp.broadcast_to(sm["b_s"].reshape(NG, CH, 1), (NG, CH, 256))
    ln_g, ln_b = sm["ln_a_g"].reshape(1, D), sm["ln_a_b"].reshape(1, D)
    cw, cb = sm["conv_w"].reshape(4, CD), sm["conv_b"].reshape(1, CD)
    pad32 = lambda a: jnp.pad(a.reshape(1, NH), ((0, 0), (0, DTW - NH)))
    dtb, alog = pad32(sm["dt_bias"]), pad32(sm["a_log"])
    dskx = jnp.repeat(sm["d_skip"].reshape(NH), HD).reshape(1, DI)
    sg = sm["ssm_norm_g"].reshape(1, DI)
    ple_g, fin_g = sm["ple_norm_g"].reshape(1, D), sm["final_g"].reshape(1, D)
    e128 = (jnp.arange(DTW)[:, None] == (jnp.arange(DI)[None, :] // HD)).astype(F32)
    et128 = e128.T
    gsel = ((jnp.arange(D)[:, None] // 256) == jnp.arange(128)[None, :]).astype(BF16)

    proj, h = _proj_fwd(x, norm_g, wp)
    ya = _gmlp_fwd(proj, ln_g, ln_b, ws, bst)
    y, yb, hprev = _ssd_fwd(proj, cw, cb, dtb, alog, dskx, sg, e128, et128)
    dx1, oa, ob, mg, hp, dpre, dpe, acc = _merge_fwd(x, ya, yb, proj, p, tgt, w_oa, w_ob, w_out, w_pg, w_ple, w_pg.T, ple_g, fin_g)
    doa, dob, dya, dyb, dgab = _merge_bwd(dx1, oa, ob, proj, w_out.T, w_oa.T, w_ob.T)
    du, dv, dza, dws, dbs, dln = _gmlp_bwd(proj, dya, ln_g, ln_b, ws, wst, bst, gsel)
    dxbc, dzb, ddt, dcw, dsm, dsg = _ssd_bwd(proj, y, dyb, hprev, cw, cb, dtb, alog, dskx, sg, e128, et128)
    gx, dng = _proj_bwd(x, dx1, norm_g, wp.T, dxbc, du, dzb, dv, dza, dgab, ddt)

    dx1b = dx1.astype(BF16)
    g_w_in = jnp.concatenate([_wgrad(h, du, "wgrad_u"), _wgrad(h, dv, "wgrad_v"), _wgrad(h, dza, "wgrad_za"),
                              _wgrad(h, dzb, "wgrad_zb"), _wgrad(h, dxbc, "wgrad_xbc"), _wgrad(h, ddt, "wgrad_dt")[:, :NH],
                              _wgrad(h, dgab, "wgrad_gab")], axis=1)
    grads = {
        "w_in": g_w_in,
        "w_oa": _wgrad(ya, doa, "wgrad_oa"), "w_ob": _wgrad(yb, dob, "wgrad_ob"), "w_out": _wgrad(mg, dx1b, "wgrad_out"),
        "w_pg": _wgrad(hp, dpre, "wgrad_pg"), "w_ple": _wgrad(p.astype(BF16), dpe, "wgrad_ple"),
        "norm_g": dng[0], "ln_a_g": dln[0], "ln_a_b": dln[1], "w_s": dws, "b_s": dbs[:, :NG].T,
        "conv_w": dcw[0:4], "conv_b": dcw[4], "dt_bias": dsm[0, :NH], "a_log": dsm[1, :NH], "d_skip": dsm[2, :NH],
        "ssm_norm_g": dsg[0], "ple_norm_g": acc[1], "final_g": acc[0],
    }
    return acc[2, 0], gx, grads


_WEIGHTS = ["norm_g", "w_in", "ln_a_g", "ln_a_b", "w_s", "b_s", "conv_w", "conv_b", "dt_bias", "a_log", "d_skip", "ssm_norm_g",
            "w_oa", "w_ob", "w_out", "ple_norm_g", "w_pg", "w_ple", "final_g"]
_REPLICATED = ["norm_g", "ln_a_g", "ln_a_b", "w_s", "b_s", "conv_b", "dt_bias", "a_log", "d_skip", "ssm_norm_g", "ple_norm_g", "final_g"]


def _rows_pack(d, dtype):
    return jnp.concatenate([d["w_oa"].reshape(128, D), d["w_ob"].reshape(256, D), d["w_out"].reshape(128, D),
                            d["w_pg"].reshape(128, D), d["w_ple"].reshape(32, D)], axis=0).astype(dtype)


def kernel(x, p, norm_g, w_in, ln_a_g, ln_a_b, w_s, b_s, conv_w, conv_b, dt_bias, a_log, d_skip, ssm_norm_g, w_oa, w_ob, w_out, ple_norm_g, w_pg, w_ple, final_g, loss_target, m_norm_g, m_w_in, m_ln_a_g, m_ln_a_b, m_w_s, m_b_s, m_conv_w, m_conv_b, m_dt_bias, m_a_log, m_d_skip, m_ssm_norm_g, m_w_oa, m_w_ob, m_w_out, m_ple_norm_g, m_w_pg, m_w_ple, m_final_g, v_norm_g, v_w_in, v_ln_a_g, v_ln_a_b, v_w_s, v_b_s, v_conv_w, v_conv_b, v_dt_bias, v_a_log, v_d_skip, v_ssm_norm_g, v_w_oa, v_w_ob, v_w_out, v_ple_norm_g, v_w_pg, v_w_ple, v_final_g):
    args = dict(locals())
    w = {n: args[n] for n in _WEIGHTS}
    m = {n: args["m_" + n] for n in _WEIGHTS}
    v = {n: args["v_" + n] for n in _WEIGHTS}
    T = x.shape[1]
    xi, yi, ci = lax.axis_index("x"), lax.axis_index("y"), lax.axis_index("c")
    me = 4 * xi + 2 * yi + ci

    a_all, r_all, cw_all = _all_gather([w["w_in"].reshape(D, WSH).astype(BF16), _rows_pack(w, BF16),
                                        w["conv_w"].reshape(4, CD // NDEV)])
    wp = _w_in_to_proj_layout(jnp.transpose(a_all, (1, 0, 2)).reshape(D, NIN))
    f_oa = r_all[:, R_OA:R_OB].reshape(D, D)
    f_ob = r_all[:, R_OB:R_OUT].reshape(DI, D)
    f_out = r_all[:, R_OUT:R_PG].reshape(D, D)
    f_pg = r_all[:, R_PG:R_PLE].reshape(D, D)
    f_ple = jnp.transpose(r_all[:, R_PLE:R_ROWS].reshape(NDEV, PLE, 128), (1, 0, 2)).reshape(PLE, D)

    sm = {n: w[n] for n in _REPLICATED}
    sm["conv_w"] = jnp.transpose(cw_all, (1, 0, 2)).reshape(4, CD)

    loss, gx, g = _local_step(x.reshape(T, D), p.reshape(T, PLE), loss_target.reshape(T, D), sm, wp, f_oa, f_ob, f_out, f_pg, f_ple)
    loss = lax.psum(loss, ("x", "y", "c"))

    ga = jnp.transpose(g["w_in"].reshape(D, NDEV, WSH), (1, 0, 2)).astype(BF16)
    gple = jnp.transpose(g["w_ple"].reshape(PLE, NDEV, 128), (1, 0, 2)).reshape(NDEV, 32, D)
    gr = jnp.concatenate([g["w_oa"].reshape(NDEV, 128, D), g["w_ob"].reshape(NDEV, 256, D), g["w_out"].reshape(NDEV, 128, D),
                          g["w_pg"].reshape(NDEV, 128, D), gple], axis=1).astype(BF16)
    from_sib_a, from_sib_r = _exchange_sibling([ga, gr])
    own_a = ga.reshape(4, 2, D, WSH)
    own_r = gr.reshape(4, 2, R_ROWS, D)
    pa = _pair_sum(lax.dynamic_index_in_dim(own_a, ci, 1, keepdims=False), from_sib_a, "pair_sum_w_in")
    pr = _pair_sum(lax.dynamic_index_in_dim(own_r, ci, 1, keepdims=False), from_sib_r, "pair_sum_rows")
    small = _pack_small({n: g[n] for n, _ in _SMALL})
    ra, rr, small_all = _exchange_chips([pa, pr], small)
    chip = 2 * xi + yi
    pa_own = lax.dynamic_index_in_dim(pa, chip, 0, keepdims=False)
    pr_own = lax.dynamic_index_in_dim(pr, chip, 0, keepdims=False)

    out_g, out_d, out_m, out_v = {}, {}, {}, {}
    ga_, da_, ma_, va_ = _adamw(w["w_in"].reshape(D, WSH), m["w_in"].reshape(D, WSH), v["w_in"].reshape(D, WSH),
                                [pa_own, ra[0], ra[1], ra[2]], "adamw_w_in")
    for dst, val in ((out_g, ga_), (out_d, da_), (out_m, ma_), (out_v, va_)):
        dst["w_in"] = val.reshape(1, D, WSH)
    res = _adamw(_rows_pack(w, F32), _rows_pack(m, F32), _rows_pack(v, F32), [pr_own, rr[0], rr[1], rr[2]], "adamw_rows")
    for dst, val in zip((out_g, out_d, out_m, out_v), res):
        dst["w_oa"] = val[R_OA:R_OB].reshape(1, 128, D)
        dst["w_ob"] = val[R_OB:R_OUT].reshape(1, 256, D)
        dst["w_out"] = val[R_OUT:R_PG].reshape(1, 128, D)
        dst["w_pg"] = val[R_PG:R_PLE].reshape(1, 128, D)
        dst["w_ple"] = val[R_PLE:R_ROWS].reshape(1, PLE, 128)
    shapes = {n: w[n].shape for n in _REPLICATED}
    res = _adamw(_pack_small({n: w[n] for n in _REPLICATED}), _pack_small({n: m[n] for n in _REPLICATED}),
                 _pack_small({n: v[n] for n in _REPLICATED}), [small_all[k] for k in range(NDEV)], "adamw_small")
    for dst, val in zip((out_g, out_d, out_m, out_v), res):
        dst.update(_unpack_small(val, shapes))
    g_cw_full = _unpack_small(res[0], {"conv_w": (4, CD)})["conv_w"]
    g_cw = lax.dynamic_slice_in_dim(g_cw_full, me * (CD // NDEV), CD // NDEV, axis=1).reshape(12, 128)
    res = _adamw(w["conv_w"].reshape(12, 128), m["conv_w"].reshape(12, 128), v["conv_w"].reshape(12, 128), [g_cw], "adamw_conv_w")
    for dst, val in zip((out_g, out_d, out_m, out_v), res):
        dst["conv_w"] = val.reshape(1, 4, CD // NDEV)

    return (loss, gx.reshape(1, T, D), *[out_g[n] for n in _WEIGHTS], *[out_d[n] for n in _WEIGHTS],
            *[out_m[n] for n in _WEIGHTS], *[out_v[n] for n in _WEIGHTS])
```

```python
import math

import jax
import jax.numpy as jnp
from jax import lax
from jax.experimental import pallas as pl
from jax.experimental.pallas import tpu as pltpu

F32 = jnp.float32
BF16 = jnp.bfloat16
HI = lax.Precision.HIGHEST
MESH = pl.DeviceIdType.MESH

D = 1024
DI = 2048
CD = 3072
NH = 32
HD = 64
NST = 128
NG = 4
CH = 128
PLE = 256
NIN = 10272
NDEV = 8
WSH = NIN // NDEV
EPS = 1e-6
OFF_XBC, OFF_U, OFF_ZB, OFF_V, OFF_ZA, OFF_GA, OFF_GB, OFF_DT = 0, 3072, 4096, 6144, 7168, 8192, 9216, 10240
NP = 10368
DTW = 128
R_OA, R_OB, R_OUT, R_PG, R_PLE, R_ROWS = 0, 128, 384, 512, 640, 672

ADAM_LR, ADAM_B1, ADAM_B2, ADAM_EPS, ADAM_WD, ADAM_STEP = 0.001, 0.9, 0.999, 1e-08, 0.01, 10

V7X_VMEM_LIMIT = 56 * 1024 * 1024


def _cp(sem=None):
    return pltpu.CompilerParams(dimension_semantics=sem, vmem_limit_bytes=V7X_VMEM_LIMIT)


def _dot(a, b, prec=None):
    return jnp.dot(a, b, preferred_element_type=F32, precision=prec)


def _dot_nt(a, b, prec=None):
    return lax.dot_general(a, b, (((1,), (1,)), ((), ())), preferred_element_type=F32, precision=prec)


def _dot_tn(a, b, prec=None):
    return lax.dot_general(a, b, (((0,), (0,)), ((), ())), preferred_element_type=F32, precision=prec)


def _sigmoid(x):
    return 1.0 / (1.0 + jnp.exp(-x))


def _gelu_and_grad(x):
    c = math.sqrt(2.0 / math.pi)
    x2 = x * x
    t = jnp.tanh(c * (x + 0.044715 * x * x2))
    g = 0.5 * x * (1.0 + t)
    dg = 0.5 * (1.0 + t) + 0.5 * x * (1.0 - t * t) * c * (1.0 + 3.0 * 0.044715 * x2)
    return g, dg


def _gelu(x):
    c = math.sqrt(2.0 / math.pi)
    return 0.5 * x * (1.0 + jnp.tanh(c * (x + 0.044715 * x * x * x)))


def _softplus(x):
    return jnp.maximum(x, 0.0) + jnp.log(1.0 + jnp.exp(-jnp.abs(x)))


def _full(shape):
    n = len(shape)
    return pl.BlockSpec(shape, lambda *_: (0,) * n)


def _proj_fwd(x, norm_g, wp):
    T = x.shape[0]
    tm, tn = min(T, 1024), 1152

    def body(x_ref, g_ref, w_ref, proj_ref, h_ref, hs_ref):
        @pl.when(pl.program_id(1) == 0)
        def _():
            xf = x_ref[...]
            r = lax.rsqrt(jnp.mean(xf * xf, axis=-1, keepdims=True) + EPS)
            h = (xf * r * g_ref[...]).astype(BF16)
            hs_ref[...] = h
            h_ref[...] = h

        proj_ref[...] = _dot(hs_ref[...], w_ref[...])

    return pl.pallas_call(
        body, name="proj_fwd", grid=(T // tm, NP // tn),
        in_specs=[pl.BlockSpec((tm, D), lambda i, j: (i, 0)), _full((1, D)), pl.BlockSpec((D, tn), lambda i, j: (0, j))],
        out_specs=[pl.BlockSpec((tm, tn), lambda i, j: (i, j)), pl.BlockSpec((tm, D), lambda i, j: (i, 0))],
        out_shape=[jax.ShapeDtypeStruct((T, NP), F32), jax.ShapeDtypeStruct((T, D), BF16)],
        scratch_shapes=[pltpu.VMEM((tm, D), BF16)],
        compiler_params=_cp(("parallel", "arbitrary")),
    )(x, norm_g, wp)


def _gmlp_tile():
    return 256


def _gmlp_fwd(proj, ln_g, ln_b, ws, bst):
    T = proj.shape[0]
    tm = min(T, _gmlp_tile())

    def body(u_ref, v_ref, z_ref, lg_ref, lb_ref, ws_ref, bs_ref, ya_ref, vn_s):
        vg = _gelu(v_ref[...])
        mu = jnp.mean(vg, axis=-1, keepdims=True)
        xc = vg - mu
        rstd = lax.rsqrt(jnp.mean(xc * xc, axis=-1, keepdims=True) + EPS)
        vn_s[...] = (xc * rstd * lg_ref[...] + lb_ref[...]).astype(BF16)
        for c in range(tm // CH):
            rs = slice(c * CH, (c + 1) * CH)
            for g in range(NG):
                cs_ = slice(g * 256, (g + 1) * 256)
                sv = _dot(ws_ref[g], vn_s[rs, cs_]) + bs_ref[g]
                z = z_ref[rs, cs_]
                ya_ref[rs, cs_] = (_gelu(u_ref[rs, cs_]) * sv * (z * _sigmoid(z))).astype(BF16)

    blk = lambda off: pl.BlockSpec((tm, D), lambda i: (i, off // D))
    return pl.pallas_call(
        body, name="gmlp_fwd", grid=(T // tm,),
        in_specs=[blk(OFF_U), blk(OFF_V), blk(OFF_ZA), _full((1, D)), _full((1, D)), _full((NG, CH, CH)), _full((NG, CH, 256))],
        out_specs=pl.BlockSpec((tm, D), lambda i: (i, 0)),
        out_shape=jax.ShapeDtypeStruct((T, D), BF16),
        scratch_shapes=[pltpu.VMEM((tm, D), BF16)],
        compiler_params=_cp(("parallel",)),
    )(proj, proj, proj, ln_g, ln_b, ws, bst)


def _gmlp_bwd(proj, dya, ln_g, ln_b, ws, wst, bst, gsel):
    T = proj.shape[0]
    tm = min(T, _gmlp_tile())

    def body(u_ref, v_ref, z_ref, dy_ref, lg_ref, lb_ref, ws_ref, wst_ref, bs_ref, gsel_ref,
             du_ref, dv_ref, dz_ref, dws_ref, dbs_ref, dln_ref, vn_s, dsv_s, dvn_s):
        @pl.when(pl.program_id(0) == 0)
        def _():
            dws_ref[...] = jnp.zeros_like(dws_ref)
            dbs_ref[...] = jnp.zeros_like(dbs_ref)
            dln_ref[...] = jnp.zeros_like(dln_ref)

        vg, dvg_dv = _gelu_and_grad(v_ref[...])
        mu = jnp.mean(vg, axis=-1, keepdims=True)
        xc = vg - mu
        rstd = lax.rsqrt(jnp.mean(xc * xc, axis=-1, keepdims=True) + EPS)
        vhat = xc * rstd
        vn_s[...] = (vhat * lg_ref[...] + lb_ref[...]).astype(BF16)
        ri = lax.broadcasted_iota(jnp.int32, (CH, CH), 0)
        ci = lax.broadcasted_iota(jnp.int32, (CH, CH), 1)
        tril = (ri >= ci).astype(F32)
        for c in range(tm // CH):
            rs = slice(c * CH, (c + 1) * CH)
            for g in range(NG):
                cs_ = slice(g * 256, (g + 1) * 256)
                vn = vn_s[rs, cs_]
                sv = _dot(ws_ref[g], vn) + bs_ref[g]
                z = z_ref[rs, cs_]
                sz = _sigmoid(z)
                ug, dug_du = _gelu_and_grad(u_ref[rs, cs_])
                dy = dy_ref[rs, cs_].astype(F32)
                t = dy * z * sz
                du_ref[rs, cs_] = (t * sv * dug_du).astype(BF16)
                dz_ref[rs, cs_] = (dy * ug * sv * sz * (1.0 + z * (1.0 - sz))).astype(BF16)
                dsv = (t * ug).astype(BF16)
                dsv_s[rs, cs_] = dsv
                dvn_s[rs, cs_] = _dot(wst_ref[g], dsv)
                dws_ref[g] += _dot_nt(dsv, vn) * tril
            dbs_ref[...] += _dot(dsv_s[rs, :], gsel_ref[...])
        dvn = dvn_s[...]
        dln_ref[0:1, :] += jnp.sum(dvn * vhat, axis=0, keepdims=True)
        dln_ref[1:2, :] += jnp.sum(dvn, axis=0, keepdims=True)
        dvh = dvn * lg_ref[...]
        dvg = rstd * (dvh - jnp.mean(dvh, axis=-1, keepdims=True) - vhat * jnp.mean(dvh * vhat, axis=-1, keepdims=True))
        dv_ref[...] = (dvg * dvg_dv).astype(BF16)

    blk = lambda off: pl.BlockSpec((tm, D), lambda i: (i, off // D))
    row = pl.BlockSpec((tm, D), lambda i: (i, 0))
    return pl.pallas_call(
        body, name="gmlp_bwd", grid=(T // tm,),
        in_specs=[blk(OFF_U), blk(OFF_V), blk(OFF_ZA), row, _full((1, D)), _full((1, D)), _full((NG, CH, CH)),
                  _full((NG, CH, CH)), _full((NG, CH, 256)), _full((D, 128))],
        out_specs=[row, row, row, _full((NG, CH, CH)), _full((CH, 128)), _full((8, D))],
        out_shape=[jax.ShapeDtypeStruct((T, D), BF16)] * 3
        + [jax.ShapeDtypeStruct((NG, CH, CH), F32), jax.ShapeDtypeStruct((CH, 128), F32), jax.ShapeDtypeStruct((8, D), F32)],
        scratch_shapes=[pltpu.VMEM((tm, D), BF16), pltpu.VMEM((tm, D), BF16), pltpu.VMEM((tm, D), F32)],
        compiler_params=_cp(("arbitrary",)),
    )(proj, proj, proj, dya, ln_g, ln_b, ws, wst, bst, gsel)


def _conv_pre(x, xp, cw_ref, cb_ref):
    row = lax.broadcasted_iota(jnp.int32, (CH, 1), 0)
    shifted = [x]
    for j in (1, 2, 3):
        shifted.append(jnp.where(row >= j, pltpu.roll(x, j, 0), pltpu.roll(xp, j, 0)))
    pre = cb_ref[...] + cw_ref[3:4, :] * x
    for j in (1, 2, 3):
        pre = pre + cw_ref[3 - j:4 - j, :] * shifted[j]
    return pre, shifted


def _split_dot(x, w, parts, w_left=False):
    acc, r = None, x
    for k in range(parts):
        hi = r.astype(BF16)
        d = _dot(w, hi) if w_left else _dot(hi, w)
        acc = d if acc is None else acc + d
        if k + 1 < parts:
            r = r - hi.astype(F32)
    return acc


def _chunk_decays(dt, alog_ref, e_ref, cs_s, cst_s, csx_s):
    a = -jnp.exp(alog_ref[...])
    ri = lax.broadcasted_iota(jnp.int32, (CH, CH), 0)
    ci = lax.broadcasted_iota(jnp.int32, (CH, CH), 1)
    tril = ri >= ci
    cs = _split_dot(dt * a, tril.astype(BF16), 3, w_left=True)
    cs_s[...] = cs
    cst_s[...] = cs.T
    csx_s[...] = _split_dot(cs, e_ref[...], 3)
    return a, tril, ri, ci


def _lmat(cst_s, h, tril):
    rowb = jnp.broadcast_to(cst_s[h:h + 1, :], (CH, CH))
    return jnp.exp(jnp.where(tril, rowb.T - rowb, -jnp.inf))


def _head_pair_rows(v, lane):
    return jnp.concatenate([jnp.where(lane < HD, v, 0.0), jnp.where(lane < HD, 0.0, v)], axis=0).astype(BF16)


def _ssd_fwd(proj, cw, cb, dtb, alog, dskx, sg, e128):
    T = proj.shape[0]
    nc = T // CH

    def body(xbc_ref, zb_ref, dt_ref, cw_ref, cb_ref, dtb_ref, alog_ref, dx_ref, sg_ref, e_ref,
             y_ref, yb_ref, hp_ref, xprev_s, h_s, cs_s, cst_s, csx_s, yz_s):
        @pl.when(pl.program_id(0) == 0)
        def _():
            xprev_s[...] = jnp.zeros_like(xprev_s)
            h_s[...] = jnp.zeros_like(h_s)

        x = xbc_ref[...]
        pre, _ = _conv_pre(x, xprev_s[...], cw_ref, cb_ref)
        xprev_s[...] = x
        xc = pre * _sigmoid(pre)
        dt = _softplus(dt_ref[...] + dtb_ref[...])
        a, tril, _, lane = _chunk_decays(dt, alog_ref, e_ref, cs_s, cst_s, csx_s)
        dt_x = _split_dot(dt, e_ref[...], 2)
        cs_last_x = csx_s[CH - 1:CH, :]
        hp_ref[0] = h_s[...]
        for g in range(NG):
            gs = slice(g * 512, (g + 1) * 512)
            bg = xc[:, DI + g * NST:DI + (g + 1) * NST].astype(BF16)
            cg = xc[:, DI + 512 + g * NST:DI + 512 + (g + 1) * NST].astype(BF16)
            cbm = _dot_nt(cg, bg)
            xg = xc[:, gs]
            xdt = xg * dt_x[:, gs]
            hprev = h_s[:, gs]
            csx = csx_s[:, gs]
            yoff = _dot(cg, hprev.astype(BF16)) * jnp.exp(csx)
            st = _dot_tn(bg, (xdt * jnp.exp(cs_last_x[:, gs] - csx)).astype(BF16))
            h_s[:, gs] = jnp.exp(cs_last_x[:, gs]) * hprev + st
            ssq = jnp.zeros((CH, 1), F32)
            for q in range(4):
                h0 = g * 8 + 2 * q
                ps = slice(q * 128, (q + 1) * 128)
                cols = slice(g * 512 + q * 128, g * 512 + (q + 1) * 128)
                m01 = jnp.concatenate([cbm * _lmat(cst_s, h0, tril), cbm * _lmat(cst_s, h0 + 1, tril)], axis=1).astype(BF16)
                yq = _dot(m01, _head_pair_rows(xdt[:, ps], lane)) + yoff[:, ps] + xg[:, ps] * dx_ref[:, cols]
                y_ref[:, cols] = yq
                z = zb_ref[:, cols]
                yz = yq * z * _sigmoid(z)
                yz_s[:, cols] = yz
                ssq = ssq + jnp.sum(yz * yz, axis=1, keepdims=True)
            rg = lax.rsqrt(ssq * (1.0 / 512.0) + EPS)
            yb_ref[:, gs] = (yz_s[:, gs] * rg * sg_ref[:, gs]).astype(BF16)

    return pl.pallas_call(
        body, name="ssd_fwd", grid=(nc,),
        in_specs=[pl.BlockSpec((CH, CD), lambda c: (c, OFF_XBC // CD)), pl.BlockSpec((CH, DI), lambda c: (c, OFF_ZB // DI)),
                  pl.BlockSpec((CH, DTW), lambda c: (c, OFF_DT // DTW)), _full((4, CD)), _full((1, CD)), _full((1, DTW)),
                  _full((1, DTW)), _full((1, DI)), _full((1, DI)), _full((DTW, DI))],
        out_specs=[pl.BlockSpec((CH, DI), lambda c: (c, 0)), pl.BlockSpec((CH, DI), lambda c: (c, 0)),
                   pl.BlockSpec((1, NST, DI), lambda c: (c, 0, 0))],
        out_shape=[jax.ShapeDtypeStruct((T, DI), F32), jax.ShapeDtypeStruct((T, DI), BF16),
                   jax.ShapeDtypeStruct((nc, NST, DI), F32)],
        scratch_shapes=[pltpu.VMEM((CH, CD), F32), pltpu.VMEM((NST, DI), F32), pltpu.VMEM((CH, CH), F32),
                        pltpu.VMEM((CH, CH), F32), pltpu.VMEM((CH, DI), F32), pltpu.VMEM((CH, DI), F32)],
        compiler_params=_cp(("arbitrary",)),
    )(proj, proj, proj, cw, cb, dtb, alog, dskx, sg, e128)


def _ssd_bwd(proj, y, dyb, hprev_all, cw, cb, dtb, alog, dskx, sg, e128, et128):
    T = proj.shape[0]
    nc = T // CH

    def body(xbc_ref, xbcp_ref, zb_ref, dt_ref, y_ref, dyb_ref, hp_ref, cw_ref, cb_ref, dtb_ref, alog_ref, dx_ref, sg_ref,
             e_ref, et_ref, dxbc_ref, dzb_ref, ddt_ref, dcw_ref, dsm_ref, dsg_ref,
             g_s, dpn_s, cs_s, cst_s, csx_s, dy_s, dxdt_s, dxs_s, dsd_s, dxc_s, gh_s):
        i = pl.program_id(0)

        @pl.when(i == 0)
        def _():
            g_s[...] = jnp.zeros_like(g_s)
            dpn_s[...] = jnp.zeros_like(dpn_s)
            dcw_ref[...] = jnp.zeros_like(dcw_ref)
            dsm_ref[...] = jnp.zeros_like(dsm_ref)
            dsg_ref[...] = jnp.zeros_like(dsg_ref)

        x = xbc_ref[...]
        xp = jnp.where(i == nc - 1, 0.0, xbcp_ref[...])
        pre, shifted = _conv_pre(x, xp, cw_ref, cb_ref)
        sp = _sigmoid(pre)
        xc = pre * sp
        dtr = dt_ref[...] + dtb_ref[...]
        dt = _softplus(dtr)
        a, tril, ri, lane = _chunk_decays(dt, alog_ref, e_ref, cs_s, cst_s, csx_s)
        et = et_ref[...]
        dt_x = _split_dot(dt, e_ref[...], 2)
        cs_last_x = csx_s[CH - 1:CH, :]

        for g in range(NG):
            gs = slice(g * 512, (g + 1) * 512)
            z = zb_ref[:, gs]
            sz = _sigmoid(z)
            yv = y_ref[:, gs]
            yz = yv * z * sz
            rg = lax.rsqrt(jnp.mean(yz * yz, axis=-1, keepdims=True) + EPS)
            yn = yz * rg
            dyb = dyb_ref[:, gs].astype(F32)
            dsg_ref[0:1, gs] += jnp.sum(dyb * yn, axis=0, keepdims=True)
            dyn = dyb * sg_ref[:, gs]
            dyz = rg * (dyn - yn * jnp.mean(dyn * yn, axis=-1, keepdims=True))
            dy_s[:, gs] = dyz * z * sz
            dzb_ref[:, gs] = (dyz * yv * sz * (1.0 + z * (1.0 - sz))).astype(BF16)

        rsum = jnp.zeros((CH, DTW), F32)
        csum_t = jnp.zeros((DTW, CH), F32)
        for g in range(NG):
            gs = slice(g * 512, (g + 1) * 512)
            bg = xc[:, DI + g * NST:DI + (g + 1) * NST].astype(BF16)
            cg = xc[:, DI + 512 + g * NST:DI + 512 + (g + 1) * NST].astype(BF16)
            cbm = _dot_nt(cg, bg)
            xdt = xc[:, gs] * dt_x[:, gs]
            hprev = hp_ref[0, :, gs]
            hpb = hprev.astype(BF16)
            gn = g_s[:, gs]
            gnb = gn.astype(BF16)
            dy = dy_s[:, gs]
            csx = csx_s[:, gs]
            ecs = jnp.exp(csx)
            dec = jnp.exp(cs_last_x[:, gs] - csx)
            dye = (dy * ecs).astype(BF16)
            dc = _dot_nt(dye, hpb)
            dprev = _dot_tn(cg, dye)
            dxdt_state = dec * _dot(bg, gnb)
            db = _dot_nt((xdt * dec).astype(BF16), gnb)
            dcb = jnp.zeros((CH, CH), F32)
            for q in range(4):
                h0 = g * 8 + 2 * q
                ps = slice(q * 128, (q + 1) * 128)
                dyp = dy[:, ps]
                l0 = _lmat(cst_s, h0, tril)
                l1 = _lmat(cst_s, h0 + 1, tril)
                m0 = cbm * l0
                m1 = cbm * l1
                dm = _dot_nt(dyp.astype(BF16), _head_pair_rows(xdt[:, ps], lane))
                dm0 = dm[:, :CH]
                dm1 = dm[:, CH:]
                dcb = dcb + dm0 * l0 + dm1 * l1
                for hh, qm in ((h0, dm0 * m0), (h0 + 1, dm1 * m1)):
                    rsum = jnp.where(lane == hh, jnp.sum(qm, axis=1, keepdims=True), rsum)
                    csum_t = jnp.where(ri == hh, jnp.sum(qm, axis=0, keepdims=True), csum_t)
                mst = jnp.concatenate([m0, m1], axis=0).astype(BF16)
                d = _dot_tn(mst, _head_pair_rows(dyp, lane))
                dxdt_s[:, g * 512 + q * 128:g * 512 + (q + 1) * 128] = d + dxdt_state[:, ps]
            yoff = _dot(cg, hpb) * ecs
            dsd_s[:, gs] = xdt * dxdt_state
            dxs_s[:, gs] = dy * yoff
            dcbb = dcb.astype(BF16)
            dxc_s[:, DI + 512 + g * NST:DI + 512 + (g + 1) * NST] = dc + _dot(dcbb, bg)
            dxc_s[:, DI + g * NST:DI + (g + 1) * NST] = db + _dot_tn(dcbb, cg)
            gh_s[:, gs] = jnp.broadcast_to(jnp.sum(gn * hprev, axis=0, keepdims=True), (8, 512))
            g_s[:, gs] = dprev + jnp.exp(cs_last_x[:, gs]) * gn

        xs = xc[:, :DI]
        dy = dy_s[...]
        dxdt = dxdt_s[...]
        cs_last = cs_s[CH - 1:CH, :]
        state_e = _split_dot(dsd_s[...], et, 2)
        dcd = 0.125 * jnp.sum(_split_dot(gh_s[...], et, 2), axis=0, keepdims=True) * jnp.exp(cs_last)
        row = lax.broadcasted_iota(jnp.int32, (CH, 1), 0)
        dcs = rsum - csum_t.T + _split_dot(dxs_s[...], et, 2) - state_e
        dcs = dcs + jnp.where(row == CH - 1, jnp.sum(state_e, axis=0, keepdims=True) + dcd, 0.0)
        dda = _split_dot(dcs, (lane >= ri).astype(BF16), 3, w_left=True)
        ddt = dda * a + _dot((dxdt * xs).astype(BF16), et)
        ddtr = jnp.where(lane < NH, ddt * _sigmoid(dtr), 0.0)
        ddt_ref[...] = ddtr.astype(BF16)
        dsm_ref[0:1, :] += jnp.sum(ddtr, axis=0, keepdims=True)
        dsm_ref[1:2, :] += jnp.sum(dda * dt, axis=0, keepdims=True) * a
        dsm_ref[2:3, :] += jnp.sum(_dot((dy * xs).astype(BF16), et), axis=0, keepdims=True)
        dxc_s[:, :DI] = dxdt * dt_x + dy * dx_ref[...]

        dpre = dxc_s[...] * sp * (1.0 + pre * (1.0 - sp))
        dpn = dpn_s[...]
        dcw_ref[4:5, :] += jnp.sum(dpre, axis=0, keepdims=True)
        dxbc = cw_ref[3:4, :] * dpre
        dcw_ref[3:4, :] += jnp.sum(dpre * shifted[0], axis=0, keepdims=True)
        for j in (1, 2, 3):
            dcw_ref[3 - j:4 - j, :] += jnp.sum(dpre * shifted[j], axis=0, keepdims=True)
            ahead = jnp.where(row < CH - j, pltpu.roll(dpre, CH - j, 0), pltpu.roll(dpn, CH - j, 0))
            dxbc = dxbc + cw_ref[3 - j:4 - j, :] * ahead
        dpn_s[...] = dpre
        dxbc_ref[...] = dxbc.astype(BF16)

    rev = lambda c: nc - 1 - c
    return pl.pallas_call(
        body, name="ssd_bwd", grid=(nc,),
        in_specs=[pl.BlockSpec((CH, CD), lambda c: (rev(c), OFF_XBC // CD)),
                  pl.BlockSpec((CH, CD), lambda c: (jnp.maximum(rev(c) - 1, 0), OFF_XBC // CD)),
                  pl.BlockSpec((CH, DI), lambda c: (rev(c), OFF_ZB // DI)),
                  pl.BlockSpec((CH, DTW), lambda c: (rev(c), OFF_DT // DTW)),
                  pl.BlockSpec((CH, DI), lambda c: (rev(c), 0)), pl.BlockSpec((CH, DI), lambda c: (rev(c), 0)),
                  pl.BlockSpec((1, NST, DI), lambda c: (rev(c), 0, 0)),
                  _full((4, CD)), _full((1, CD)), _full((1, DTW)), _full((1, DTW)), _full((1, DI)), _full((1, DI)),
                  _full((DTW, DI)), _full((DI, DTW))],
        out_specs=[pl.BlockSpec((CH, CD), lambda c: (rev(c), 0)), pl.BlockSpec((CH, DI), lambda c: (rev(c), 0)),
                   pl.BlockSpec((CH, DTW), lambda c: (rev(c), 0)), _full((8, CD)), _full((8, DTW)), _full((8, DI))],
        out_shape=[jax.ShapeDtypeStruct((T, CD), BF16), jax.ShapeDtypeStruct((T, DI), BF16), jax.ShapeDtypeStruct((T, DTW), BF16),
                   jax.ShapeDtypeStruct((8, CD), F32), jax.ShapeDtypeStruct((8, DTW), F32), jax.ShapeDtypeStruct((8, DI), F32)],
        scratch_shapes=[pltpu.VMEM((NST, DI), F32), pltpu.VMEM((CH, CD), F32), pltpu.VMEM((CH, CH), F32), pltpu.VMEM((CH, CH), F32),
                        pltpu.VMEM((CH, DI), F32), pltpu.VMEM((CH, DI), F32), pltpu.VMEM((CH, DI), F32), pltpu.VMEM((CH, DI), F32),
                        pltpu.VMEM((CH, DI), F32), pltpu.VMEM((CH, CD), F32), pltpu.VMEM((8, DI), F32)],
        compiler_params=_cp(("arbitrary",)),
    )(proj, proj, proj, proj, y, dyb, hprev_all, cw, cb, dtb, alog, dskx, sg, e128, et128)


def _merge_tile():
    return 256


def _merge_fwd(x, ya, yb, proj, p, tgt, w_oa, w_ob, w_out, w_pg, w_ple, w_pgt, ple_g, fin_g):
    T = x.shape[0]
    tm = min(T, _merge_tile())

    def body(x_ref, ya_ref, yb_ref, ga_ref, gb_ref, p_ref, t_ref, woa, wob, wout, wpg, wple, wpgt, pg_ref, fg_ref,
             dx1_ref, dx1b_ref, oa_ref, ob_ref, mg_ref, hp_ref, dpre_ref, dpe_ref, acc_ref):
        @pl.when(pl.program_id(0) == 0)
        def _():
            acc_ref[...] = jnp.zeros_like(acc_ref)

        oa = _dot(ya_ref[...], woa[...])
        ob = _dot(yb_ref[...], wob[...])
        oa_ref[...] = oa.astype(BF16)
        ob_ref[...] = ob.astype(BF16)
        mg = _sigmoid(ga_ref[...]) * oa + _sigmoid(gb_ref[...]) * ob
        mgb = mg.astype(BF16)
        mg_ref[...] = mgb
        x1 = x_ref[...] + _dot(mgb, wout[...])
        r2 = lax.rsqrt(jnp.mean(x1 * x1, axis=-1, keepdims=True) + EPS)
        xh1 = x1 * r2
        hpb = (xh1 * pg_ref[...]).astype(BF16)
        hp_ref[...] = hpb
        gate = _sigmoid(_dot(hpb, wpg[...]))
        pe = _dot(p_ref[...].astype(BF16), wple[...])
        x2 = x1 + gate * pe
        r3 = lax.rsqrt(jnp.mean(x2 * x2, axis=-1, keepdims=True) + EPS)
        xh2 = x2 * r3
        err = xh2 * fg_ref[...] - t_ref[...]
        acc_ref[2:3, :] += 0.5 * jnp.sum(jnp.mean(err * err, axis=-1, keepdims=True))
        dyo = err * (1.0 / D)
        acc_ref[0:1, :] += jnp.sum(dyo * xh2, axis=0, keepdims=True)
        dn = dyo * fg_ref[...]
        dx2 = r3 * (dn - xh2 * jnp.mean(dn * xh2, axis=-1, keepdims=True))
        dpe_ref[...] = (dx2 * gate).astype(BF16)
        dpre = (dx2 * pe * gate * (1.0 - gate)).astype(BF16)
        dpre_ref[...] = dpre
        dhp = _dot(dpre, wpgt[...])
        acc_ref[1:2, :] += jnp.sum(dhp * xh1, axis=0, keepdims=True)
        dhn = dhp * pg_ref[...]
        dx1 = dx2 + r2 * (dhn - xh1 * jnp.mean(dhn * xh1, axis=-1, keepdims=True))
        dx1_ref[...] = dx1
        dx1b_ref[...] = dx1.astype(BF16)

    row = lambda w: pl.BlockSpec((tm, w), lambda i: (i, 0))
    wsp = lambda s: pl.BlockSpec(s, lambda i: (0, 0), pipeline_mode=pl.Buffered(1))
    return pl.pallas_call(
        body, name="merge_fwd", grid=(T // tm,),
        in_specs=[row(D), row(D), row(DI), pl.BlockSpec((tm, D), lambda i: (i, OFF_GA // D)),
                  pl.BlockSpec((tm, D), lambda i: (i, OFF_GB // D)), row(PLE), row(D),
                  wsp((D, D)), wsp((DI, D)), wsp((D, D)), wsp((D, D)), wsp((PLE, D)), wsp((D, D)), _full((1, D)), _full((1, D))],
        out_specs=[row(D)] * 8 + [_full((8, D))],
        out_shape=[jax.ShapeDtypeStruct((T, D), F32)] + [jax.ShapeDtypeStruct((T, D), BF16)] * 7 + [jax.ShapeDtypeStruct((8, D), F32)],
        compiler_params=_cp(("arbitrary",)),
    )(x, ya, yb, proj, proj, p, tgt, w_oa, w_ob, w_out, w_pg, w_ple, w_pgt, ple_g, fin_g)


def _merge_bwd(dx1, oa, ob, proj, w_outt, w_oat, w_obt):
    T = dx1.shape[0]
    tm = min(T, _merge_tile())

    def body(dx1_ref, oa_ref, ob_ref, ga_ref, gb_ref, woutt, woat, wobt, doa_ref, dob_ref, dya_ref, dyb_ref, dg_ref):
        dmg = _dot(dx1_ref[...].astype(BF16), woutt[...])
        sa = _sigmoid(ga_ref[...])
        sb = _sigmoid(gb_ref[...])
        doa = (dmg * sa).astype(BF16)
        dob = (dmg * sb).astype(BF16)
        doa_ref[...] = doa
        dob_ref[...] = dob
        dg_ref[:, :D] = (dmg * oa_ref[...].astype(F32) * sa * (1.0 - sa)).astype(BF16)
        dg_ref[:, D:] = (dmg * ob_ref[...].astype(F32) * sb * (1.0 - sb)).astype(BF16)
        dya_ref[...] = _dot(doa, woat[...]).astype(BF16)
        dyb_ref[...] = _dot(dob, wobt[...]).astype(BF16)

    row = lambda w: pl.BlockSpec((tm, w), lambda i: (i, 0))
    wsp = lambda s: pl.BlockSpec(s, lambda i: (0, 0), pipeline_mode=pl.Buffered(1))
    return pl.pallas_call(
        body, name="merge_bwd", grid=(T // tm,),
        in_specs=[row(D), row(D), row(D), pl.BlockSpec((tm, D), lambda i: (i, OFF_GA // D)),
                  pl.BlockSpec((tm, D), lambda i: (i, OFF_GB // D)), wsp((D, D)), wsp((D, D)), wsp((D, DI))],
        out_specs=[row(D), row(D), row(D), row(DI), row(2 * D)],
        out_shape=[jax.ShapeDtypeStruct((T, D), BF16)] * 3 + [jax.ShapeDtypeStruct((T, DI), BF16), jax.ShapeDtypeStruct((T, 2 * D), BF16)],
        compiler_params=_cp(("parallel",)),
    )(dx1, oa, ob, proj, proj, w_outt, w_oat, w_obt)


def _wgrad(a, b, name):
    T, K = a.shape
    N = b.shape[1]
    tt, tn = min(T, 1024), min(N, 1024)
    nt = T // tt

    def body(a_ref, b_ref, o_ref, acc_s):
        t = pl.program_id(1)

        @pl.when(t == 0)
        def _():
            acc_s[...] = jnp.zeros_like(acc_s)

        acc_s[...] += _dot_tn(a_ref[...].astype(BF16), b_ref[...])

        @pl.when(t == nt - 1)
        def _():
            o_ref[...] = acc_s[...].astype(BF16)

    return pl.pallas_call(
        body, name=name, grid=(N // tn, nt),
        in_specs=[pl.BlockSpec((tt, K), lambda n, t: (t, 0)), pl.BlockSpec((tt, tn), lambda n, t: (t, n))],
        out_specs=pl.BlockSpec((K, tn), lambda n, t: (0, n)),
        out_shape=jax.ShapeDtypeStruct((K, N), BF16),
        scratch_shapes=[pltpu.VMEM((K, tn), F32)],
        compiler_params=_cp(("parallel", "arbitrary")),
    )(a, b)


def _proj_bwd(x, dx1, norm_g, wpt, dxbc, du, dzb, dv, dza, dgab, ddt):
    T = x.shape[0]
    tm = min(T, 512)
    nk = OFF_DT // D + 1
    pieces = [(dxbc, OFF_XBC), (du, OFF_U), (dzb, OFF_ZB), (dv, OFF_V), (dza, OFF_ZA), (dgab, OFF_GA)]
    ranges = [(off // D, off // D + arr.shape[1] // D) for arr, off in pieces]

    def body(x_ref, dx1_ref, g_ref, w_ref, wdt_ref, *rest):
        piece_refs, ddt_ref, gx_ref, dng_ref, acc_s = rest[:6], rest[6], rest[7], rest[8], rest[9]
        i, k = pl.program_id(0), pl.program_id(1)

        @pl.when((i == 0) & (k == 0))
        def _():
            dng_ref[...] = jnp.zeros_like(dng_ref)

        @pl.when(k == 0)
        def _():
            acc_s[...] = jnp.zeros_like(acc_s)

        for ref, (lo, hi) in zip(piece_refs, ranges):
            @pl.when((k >= lo) & (k < hi))
            def _(ref=ref):
                acc_s[...] += _dot(ref[...], w_ref[...])

        @pl.when(k == nk - 1)
        def _():
            dh = acc_s[...] + _dot(ddt_ref[...], wdt_ref[...])
            xf = x_ref[...]
            r = lax.rsqrt(jnp.mean(xf * xf, axis=-1, keepdims=True) + EPS)
            xh = xf * r
            dng_ref[0:1, :] += jnp.sum(dh * xh, axis=0, keepdims=True)
            dxn = dh * g_ref[...]
            gx_ref[...] = dx1_ref[...] + r * (dxn - xh * jnp.mean(dxn * xh, axis=-1, keepdims=True))

    def piece_spec(lo, hi):
        return pl.BlockSpec((tm, D), lambda i, k: (i, jnp.clip(k - lo, 0, hi - lo - 1)))

    row = pl.BlockSpec((tm, D), lambda i, k: (i, 0))
    return pl.pallas_call(
        body, name="proj_bwd", grid=(T // tm, nk),
        in_specs=[row, row, _full((1, D)), pl.BlockSpec((D, D), lambda i, k: (jnp.minimum(k, nk - 2), 0)),
                  pl.BlockSpec((DTW, D), lambda i, k: (OFF_DT // DTW, 0))]
        + [piece_spec(lo, hi) for lo, hi in ranges] + [pl.BlockSpec((tm, DTW), lambda i, k: (i, 0))],
        out_specs=[row, _full((8, D))],
        out_shape=[jax.ShapeDtypeStruct((T, D), F32), jax.ShapeDtypeStruct((8, D), F32)],
        scratch_shapes=[pltpu.VMEM((tm, D), F32)],
        compiler_params=_cp(("arbitrary", "arbitrary")),
    )(x, dx1, norm_g, wpt, wpt, *[arr for arr, _ in pieces], ddt)


def _row_tile(R, C, limit=1 << 20):
    if R * C * 4 <= limit:
        return R
    fits = [t for t in range(16, R, 16) if R % t == 0 and t * C * 4 <= limit]
    return fits[-1]


def _adamw(w, m, v, parts, name):
    R, C = w.shape[-2:]
    tr = _row_tile(R, C)
    c1 = 1.0 - ADAM_B1 ** ADAM_STEP
    c2 = 1.0 - ADAM_B2 ** ADAM_STEP
    n = len(parts)
    spec2 = pl.BlockSpec((tr, C), lambda i: (i, 0))
    slot = lambda k: pl.BlockSpec((None, tr, C), lambda i: (k, i, 0))
    wspec = spec2 if w.ndim == 2 else slot(0)
    part_specs = [slot(p[1]) if isinstance(p, tuple) else spec2 for p in parts]
    part_arrays = [p[0] if isinstance(p, tuple) else p for p in parts]

    def body(*refs):
        w_ref, m_ref, v_ref = refs[:3]
        g_ref, d_ref, nm_ref, nv_ref = refs[3 + n:]
        g = refs[3][...].astype(F32)
        for r in refs[4:3 + n]:
            g = g + r[...].astype(F32)
        mm = ADAM_B1 * m_ref[...] + (1.0 - ADAM_B1) * g
        vv = ADAM_B2 * v_ref[...] + (1.0 - ADAM_B2) * (g * g)
        g_ref[...] = g
        nm_ref[...] = mm
        nv_ref[...] = vv
        d_ref[...] = -ADAM_LR * ((mm / c1) / (jnp.sqrt(vv / c2) + ADAM_EPS) + ADAM_WD * w_ref[...])

    return pl.pallas_call(
        body, name=name, grid=(R // tr,), in_specs=[wspec] * 3 + part_specs, out_specs=[wspec] * 4,
        out_shape=[jax.ShapeDtypeStruct(w.shape, F32)] * 4, compiler_params=_cp(("parallel",)),
    )(w, m, v, *part_arrays)


def _pair_sum(a, b, name):
    Q, R, C = a.shape
    tr = _row_tile(R, C)

    def body(a_ref, b_ref, o_ref):
        o_ref[...] = (a_ref[...].astype(F32) + b_ref[...].astype(F32)).astype(BF16)

    spec = pl.BlockSpec((1, tr, C), lambda q, i: (q, i, 0))
    return pl.pallas_call(
        body, name=name, grid=(Q, R // tr), in_specs=[spec, spec], out_specs=spec,
        out_shape=jax.ShapeDtypeStruct((Q, R, C), BF16), compiler_params=_cp(("parallel", "parallel")),
    )(a, b)


_ANY = pl.BlockSpec(memory_space=pl.ANY)


def _dev_index(px, py, pc):
    return 4 * px + 2 * py + pc


def _all_gather(blocks):
    n = len(blocks)

    def body(*refs):
        ins, outs = refs[:n], refs[n:2 * n]
        send_sems, recv_sems, local_sems = refs[2 * n:]
        x, y, c = lax.axis_index("x"), lax.axis_index("y"), lax.axis_index("c")
        me, sibling = (x, y, c), (x, y, 1 - c)
        chips = [(1 - x, y), (x, 1 - y), (1 - x, 1 - y)]

        def copy(a, k, block, to, src=None):
            dst = outs[a].at[_dev_index(*block)]
            return pltpu.make_async_remote_copy(src_ref=dst if src is None else src, dst_ref=dst, send_sem=send_sems.at[a, k],
                                                recv_sem=recv_sems.at[a, k], device_id=to, device_id_type=MESH)

        mine = [pltpu.make_async_copy(ins[a], outs[a].at[_dev_index(*me)], local_sems.at[a]) for a in range(n)]
        for cp in mine:
            cp.start()
        first = []
        for a in range(n):
            first.append(copy(a, 0, me, sibling, src=ins[a]))
            first += [copy(a, 1 + j, me, (*chip, c), src=ins[a]) for j, chip in enumerate(chips)]
        for cp in first:
            cp.start()
        passed = []
        for j, chip in enumerate(chips):
            for a in range(n):
                copy(a, 1 + j, (*chip, c), me).wait_recv()
                fwd = copy(a, 4 + j, (*chip, c), sibling)
                fwd.start()
                passed.append(fwd)
        for a in range(n):
            copy(a, 0, sibling, me).wait_recv()
            for j, chip in enumerate(chips):
                copy(a, 4 + j, (*chip, 1 - c), me).wait_recv()
        for cp in first + passed:
            cp.wait_send()
        for cp in mine:
            cp.wait()

    return pl.pallas_call(
        body, name="all_gather_weights", in_specs=[_ANY] * n, out_specs=[_ANY] * n,
        out_shape=[jax.ShapeDtypeStruct((NDEV,) + b.shape, b.dtype) for b in blocks],
        scratch_shapes=[pltpu.SemaphoreType.DMA((n, 7)), pltpu.SemaphoreType.DMA((n, 7)), pltpu.SemaphoreType.DMA((n,))],
    )(*blocks)


def _exchange_sibling(grads):
    n = len(grads)

    def body(*refs):
        ins, outs = refs[:n], refs[n:2 * n]
        send_sems, recv_sems = refs[2 * n:]
        x, y, c = lax.axis_index("x"), lax.axis_index("y"), lax.axis_index("c")
        copies = []
        for a in range(n):
            for q in range(4):
                copies.append(pltpu.make_async_remote_copy(
                    src_ref=ins[a].at[2 * q + (1 - c)], dst_ref=outs[a].at[q], send_sem=send_sems.at[a, q],
                    recv_sem=recv_sems.at[a, q], device_id=(x, y, 1 - c), device_id_type=MESH))
        for cp in copies:
            cp.start()
        for cp in copies:
            cp.wait()

    return pl.pallas_call(
        body, name="reduce_scatter_sibling", in_specs=[_ANY] * n, out_specs=[_ANY] * n,
        out_shape=[jax.ShapeDtypeStruct((4,) + g.shape[1:], g.dtype) for g in grads],
        scratch_shapes=[pltpu.SemaphoreType.DMA((n, 4)), pltpu.SemaphoreType.DMA((n, 4))],
    )(*grads)


def _exchange_chips(partials, small):
    n = len(partials)

    def body(*refs):
        ins, small_ref = refs[:n], refs[n]
        outs, gath = refs[n + 1:2 * n + 1], refs[2 * n + 1]
        send_sems, recv_sems, ssend, srecv, local_sem = refs[2 * n + 2:]
        x, y, c = lax.axis_index("x"), lax.axis_index("y"), lax.axis_index("c")
        chips = [(1 - x, y), (x, 1 - y), (1 - x, 1 - y)]
        mine = pltpu.make_async_copy(small_ref, gath.at[_dev_index(x, y, c)], local_sem)
        mine.start()
        copies = []
        for a in range(n):
            for k, (px, py) in enumerate(chips):
                copies.append(pltpu.make_async_remote_copy(
                    src_ref=ins[a].at[2 * px + py], dst_ref=outs[a].at[k], send_sem=send_sems.at[a, k],
                    recv_sem=recv_sems.at[a, k], device_id=(px, py, c), device_id_type=MESH))
        for r in range(1, NDEV):
            bx, by, bc = (r >> 2) & 1, (r >> 1) & 1, r & 1
            peer = (x ^ bx, y ^ by, c ^ bc)
            copies.append(pltpu.make_async_remote_copy(
                src_ref=small_ref, dst_ref=gath.at[_dev_index(x, y, c)], send_sem=ssend.at[r - 1], recv_sem=srecv.at[r - 1],
                device_id=peer, device_id_type=MESH))
        for cp in copies:
            cp.start()
        for cp in copies:
            cp.wait()
        mine.wait()

    return pl.pallas_call(
        body, name="reduce_scatter_chips", in_specs=[_ANY] * (n + 1), out_specs=[_ANY] * (n + 1),
        out_shape=[jax.ShapeDtypeStruct((3,) + g.shape[1:], g.dtype) for g in partials]
        + [jax.ShapeDtypeStruct((NDEV,) + small.shape, small.dtype)],
        scratch_shapes=[pltpu.SemaphoreType.DMA((n, 3)), pltpu.SemaphoreType.DMA((n, 3)), pltpu.SemaphoreType.DMA((7,)),
                        pltpu.SemaphoreType.DMA((7,)), pltpu.SemaphoreType.DMA(())],
    )(*partials, small)


def _w_in_to_proj_layout(w):
    z = jnp.zeros((w.shape[0], NP - NIN), w.dtype)
    return jnp.concatenate([w[:, 5120:8192], w[:, 0:1024], w[:, 3072:5120], w[:, 1024:2048], w[:, 2048:3072],
                            w[:, 8224:9248], w[:, 9248:10272], w[:, 8192:8224], z], axis=1)


_SMALL = [("norm_g", 1024), ("ln_a_g", 1024), ("ln_a_b", 1024), ("w_s", 65536), ("b_s", 512), ("conv_w", 12288),
          ("conv_b", 3072), ("dt_bias", 128), ("a_log", 128), ("d_skip", 128), ("ssm_norm_g", 2048), ("ple_norm_g", 1024),
          ("final_g", 1024)]
_SMALL_ROWS = sum(s for _, s in _SMALL) // 128


def _pack_small(d):
    parts = []
    for name, size in _SMALL:
        if name in d:
            f = d[name].reshape(-1).astype(F32)
            parts.append(jnp.pad(f, (0, size - f.shape[0])))
        else:
            parts.append(jnp.zeros((size,), F32))
    return jnp.concatenate(parts).reshape(_SMALL_ROWS, 128)


def _unpack_small(packed, shapes):
    flat = packed.reshape(-1)
    out, off = {}, 0
    for name, size in _SMALL:
        if name in shapes:
            k = math.prod(shapes[name])
            out[name] = flat[off:off + k].reshape(shapes[name])
        off += size
    return out


def _local_step(x, p, tgt, sm, wp, w_oa, w_ob, w_out, w_pg, w_ple):
    norm_g = sm["norm_g"].reshape(1, D)
    ws = jnp.where(jnp.tril(jnp.ones((CH, CH), bool))[None], sm["w_s"].reshape(NG, CH, CH), 0.0).astype(BF16)
    wst = jnp.transpose(ws, (0, 2, 1))
    bst = jnp.broadcast_to(sm["b_s"].reshape(NG, CH, 1), (NG, CH, 256))
    ln_g, ln_b = sm["ln_a_g"].reshape(1, D), sm["ln_a_b"].reshape(1, D)
    cw, cb = sm["conv_w"].reshape(4, CD), sm["conv_b"].reshape(1, CD)
    pad32 = lambda a: jnp.pad(a.reshape(1, NH), ((0, 0), (0, DTW - NH)))
    dtb, alog = pad32(sm["dt_bias"]), pad32(sm["a_log"])
    dskx = jnp.repeat(sm["d_skip"].reshape(NH), HD).reshape(1, DI)
    sg = sm["ssm_norm_g"].reshape(1, DI)
    ple_g, fin_g = sm["ple_norm_g"].reshape(1, D), sm["final_g"].reshape(1, D)
    e128 = (jnp.arange(DTW)[:, None] == (jnp.arange(DI)[None, :] // HD)).astype(BF16)
    et128 = e128.T
    gsel = ((jnp.arange(D)[:, None] // 256) == jnp.arange(128)[None, :]).astype(BF16)

    proj, h = _proj_fwd(x, norm_g, wp)
    ya = _gmlp_fwd(proj, ln_g, ln_b, ws, bst)
    y, yb, hprev = _ssd_fwd(proj, cw, cb, dtb, alog, dskx, sg, e128)
    dx1, dx1b, oa, ob, mg, hp, dpre, dpe, acc = _merge_fwd(x, ya, yb, proj, p, tgt, w_oa, w_ob, w_out, w_pg, w_ple, w_pg.T, ple_g, fin_g)
    doa, dob, dya, dyb, dgab = _merge_bwd(dx1b, oa, ob, proj, w_out.T, w_oa.T, w_ob.T)
    du, dv, dza, dws, dbs, dln = _gmlp_bwd(proj, dya, ln_g, ln_b, ws, wst, bst, gsel)
    dxbc, dzb, ddt, dcw, dsm, dsg = _ssd_bwd(proj, y, dyb, hprev, cw, cb, dtb, alog, dskx, sg, e128, et128)
    gx, dng = _proj_bwd(x, dx1, norm_g, wp.T, dxbc, du, dzb, dv, dza, dgab, ddt)

    g_w_in = jnp.concatenate([_wgrad(h, du, "wgrad_u"), _wgrad(h, dv, "wgrad_v"), _wgrad(h, dza, "wgrad_za"),
                              _wgrad(h, dzb, "wgrad_zb"), _wgrad(h, dxbc, "wgrad_xbc"), _wgrad(h, ddt, "wgrad_dt")[:, :NH],
                              _wgrad(h, dgab, "wgrad_gab")], axis=1)
    grads = {
        "w_in": g_w_in,
        "w_oa": _wgrad(ya, doa, "wgrad_oa"), "w_ob": _wgrad(yb, dob, "wgrad_ob"), "w_out": _wgrad(mg, dx1b, "wgrad_out"),
        "w_pg": _wgrad(hp, dpre, "wgrad_pg"), "w_ple": _wgrad(p, dpe, "wgrad_ple"),
        "norm_g": dng[0], "ln_a_g": dln[0], "ln_a_b": dln[1], "w_s": dws, "b_s": dbs[:, :NG].T,
        "conv_w": dcw[0:4], "conv_b": dcw[4], "dt_bias": dsm[0, :NH], "a_log": dsm[1, :NH], "d_skip": dsm[2, :NH],
        "ssm_norm_g": dsg[0], "ple_norm_g": acc[1], "final_g": acc[0],
    }
    return acc[2, 0], gx, grads


_WEIGHTS = ["norm_g", "w_in", "ln_a_g", "ln_a_b", "w_s", "b_s", "conv_w", "conv_b", "dt_bias", "a_log", "d_skip", "ssm_norm_g",
            "w_oa", "w_ob", "w_out", "ple_norm_g", "w_pg", "w_ple", "final_g"]
_REPLICATED = ["norm_g", "ln_a_g", "ln_a_b", "w_s", "b_s", "conv_b", "dt_bias", "a_log", "d_skip", "ssm_norm_g", "ple_norm_g", "final_g"]


def _rows_pack(d, dtype):
    return jnp.concatenate([d["w_oa"].reshape(128, D), d["w_ob"].reshape(256, D), d["w_out"].reshape(128, D),
                            d["w_pg"].reshape(128, D), d["w_ple"].reshape(32, D)], axis=0).astype(dtype)


def kernel(x, p, norm_g, w_in, ln_a_g, ln_a_b, w_s, b_s, conv_w, conv_b, dt_bias, a_log, d_skip, ssm_norm_g, w_oa, w_ob, w_out, ple_norm_g, w_pg, w_ple, final_g, loss_target, m_norm_g, m_w_in, m_ln_a_g, m_ln_a_b, m_w_s, m_b_s, m_conv_w, m_conv_b, m_dt_bias, m_a_log, m_d_skip, m_ssm_norm_g, m_w_oa, m_w_ob, m_w_out, m_ple_norm_g, m_w_pg, m_w_ple, m_final_g, v_norm_g, v_w_in, v_ln_a_g, v_ln_a_b, v_w_s, v_b_s, v_conv_w, v_conv_b, v_dt_bias, v_a_log, v_d_skip, v_ssm_norm_g, v_w_oa, v_w_ob, v_w_out, v_ple_norm_g, v_w_pg, v_w_ple, v_final_g):
    args = dict(locals())
    w = {n: args[n] for n in _WEIGHTS}
    m = {n: args["m_" + n] for n in _WEIGHTS}
    v = {n: args["v_" + n] for n in _WEIGHTS}
    T = x.shape[1]
    xi, yi, ci = lax.axis_index("x"), lax.axis_index("y"), lax.axis_index("c")
    me = 4 * xi + 2 * yi + ci

    a_all, r_all, cw_all = _all_gather([w["w_in"].astype(BF16), _rows_pack(w, BF16),
                                        w["conv_w"].reshape(4, CD // NDEV)])
    wp = _w_in_to_proj_layout(jnp.transpose(a_all.reshape(NDEV, D, WSH), (1, 0, 2)).reshape(D, NIN))
    f_oa = r_all[:, R_OA:R_OB].reshape(D, D)
    f_ob = r_all[:, R_OB:R_OUT].reshape(DI, D)
    f_out = r_all[:, R_OUT:R_PG].reshape(D, D)
    f_pg = r_all[:, R_PG:R_PLE].reshape(D, D)
    f_ple = jnp.transpose(r_all[:, R_PLE:R_ROWS].reshape(NDEV, PLE, 128), (1, 0, 2)).reshape(PLE, D)

    sm = {n: w[n] for n in _REPLICATED}
    sm["conv_w"] = jnp.transpose(cw_all, (1, 0, 2)).reshape(4, CD)

    loss, gx, g = _local_step(x.reshape(T, D), p.reshape(T, PLE), loss_target.reshape(T, D), sm, wp, f_oa, f_ob, f_out, f_pg, f_ple)
    loss = lax.psum(loss, ("x", "y", "c"))

    ga = jnp.transpose(g["w_in"].reshape(D, NDEV, WSH), (1, 0, 2))
    gple = jnp.transpose(g["w_ple"].reshape(PLE, NDEV, 128), (1, 0, 2)).reshape(NDEV, 32, D)
    gr = jnp.concatenate([g["w_oa"].reshape(NDEV, 128, D), g["w_ob"].reshape(NDEV, 256, D), g["w_out"].reshape(NDEV, 128, D),
                          g["w_pg"].reshape(NDEV, 128, D), gple], axis=1)
    from_sib_a, from_sib_r = _exchange_sibling([ga, gr])
    own_a = ga.reshape(4, 2, D, WSH)
    own_r = gr.reshape(4, 2, R_ROWS, D)
    pa = _pair_sum(lax.dynamic_index_in_dim(own_a, ci, 1, keepdims=False), from_sib_a, "pair_sum_w_in")
    pr = _pair_sum(lax.dynamic_index_in_dim(own_r, ci, 1, keepdims=False), from_sib_r, "pair_sum_rows")
    small = _pack_small({n: g[n] for n, _ in _SMALL})
    ra, rr, small_all = _exchange_chips([pa, pr], small)
    chip = 2 * xi + yi
    pa_own = lax.dynamic_index_in_dim(pa, chip, 0, keepdims=False)
    pr_own = lax.dynamic_index_in_dim(pr, chip, 0, keepdims=False)

    out_g, out_d, out_m, out_v = {}, {}, {}, {}
    res = _adamw(w["w_in"], m["w_in"], v["w_in"], [pa_own, (ra, 0), (ra, 1), (ra, 2)], "adamw_w_in")
    for dst, val in zip((out_g, out_d, out_m, out_v), res):
        dst["w_in"] = val
    res = _adamw(_rows_pack(w, F32), _rows_pack(m, F32), _rows_pack(v, F32), [pr_own, (rr, 0), (rr, 1), (rr, 2)], "adamw_rows")
    for dst, val in zip((out_g, out_d, out_m, out_v), res):
        dst["w_oa"] = val[R_OA:R_OB].reshape(1, 128, D)
        dst["w_ob"] = val[R_OB:R_OUT].reshape(1, 256, D)
        dst["w_out"] = val[R_OUT:R_PG].reshape(1, 128, D)
        dst["w_pg"] = val[R_PG:R_PLE].reshape(1, 128, D)
        dst["w_ple"] = val[R_PLE:R_ROWS].reshape(1, PLE, 128)
    shapes = {n: w[n].shape for n in _REPLICATED}
    res = _adamw(_pack_small({n: w[n] for n in _REPLICATED}), _pack_small({n: m[n] for n in _REPLICATED}),
                 _pack_small({n: v[n] for n in _REPLICATED}), [(small_all, k) for k in range(NDEV)], "adamw_small")
    for dst, val in zip((out_g, out_d, out_m, out_v), res):
        dst.update(_unpack_small(val, shapes))
    g_cw_full = _unpack_small(res[0], {"conv_w": (4, CD)})["conv_w"]
    g_cw = lax.dynamic_slice_in_dim(g_cw_full, me * (CD // NDEV), CD // NDEV, axis=1).reshape(12, 128)
    res = _adamw(w["conv_w"].reshape(12, 128), m["conv_w"].reshape(12, 128), v["conv_w"].reshape(12, 128), [g_cw], "adamw_conv_w")
    for dst, val in zip((out_g, out_d, out_m, out_v), res):
        dst["conv_w"] = val.reshape(1, 4, CD // NDEV)

    return (loss, gx.reshape(1, T, D), *[out_g[n] for n in _WEIGHTS], *[out_d[n] for n in _WEIGHTS],
            *[out_m[n] for n in _WEIGHTS], *[out_v[n] for n in _WEIGHTS])
```

```python
import functools
import math
from typing import Callable, NamedTuple

import jax
import jax.numpy as jnp
from jax import lax
from jax.experimental import pallas as pl
from jax.experimental.pallas import tpu as pltpu

F32 = jnp.float32
BF16 = jnp.bfloat16
HI = lax.Precision.HIGHEST
MESH = pl.DeviceIdType.MESH

D = 1024
DI = 2048
CD = 3072
NH = 32
HD = 64
NST = 128
NG = 4
CH = 128
PLE = 256
NIN = 10272
NDEV = 8
WSH = NIN // NDEV
EPS = 1e-6
OFF_XBC, OFF_U, OFF_ZB, OFF_V, OFF_ZA, OFF_GA, OFF_GB, OFF_DT = 0, 3072, 4096, 6144, 7168, 8192, 9216, 10240
NP = 10368
DTW = 128
R_OA, R_OB, R_OUT, R_PG, R_PLE, R_ROWS = 0, 128, 384, 512, 640, 672

ADAM_LR, ADAM_B1, ADAM_B2, ADAM_EPS, ADAM_WD, ADAM_STEP = 0.001, 0.9, 0.999, 1e-08, 0.01, 10

V7X_VMEM_LIMIT = 56 * 1024 * 1024


def _cp(sem=None):
    return pltpu.CompilerParams(dimension_semantics=sem, vmem_limit_bytes=V7X_VMEM_LIMIT)


def _dot(a, b, prec=None):
    return jnp.dot(a, b, preferred_element_type=F32, precision=prec)


def _dot_nt(a, b, prec=None):
    return lax.dot_general(a, b, (((1,), (1,)), ((), ())), preferred_element_type=F32, precision=prec)


def _dot_tn(a, b, prec=None):
    return lax.dot_general(a, b, (((0,), (0,)), ((), ())), preferred_element_type=F32, precision=prec)


def _sigmoid(x):
    return 1.0 / (1.0 + jnp.exp(-x))


def _gelu_and_grad(x):
    c = math.sqrt(2.0 / math.pi)
    x2 = x * x
    t = jnp.tanh(c * (x + 0.044715 * x * x2))
    g = 0.5 * x * (1.0 + t)
    dg = 0.5 * (1.0 + t) + 0.5 * x * (1.0 - t * t) * c * (1.0 + 3.0 * 0.044715 * x2)
    return g, dg


def _gelu(x):
    c = math.sqrt(2.0 / math.pi)
    return 0.5 * x * (1.0 + jnp.tanh(c * (x + 0.044715 * x * x * x)))


def _softplus(x):
    return jnp.maximum(x, 0.0) + jnp.log(1.0 + jnp.exp(-jnp.abs(x)))


def _full(shape):
    n = len(shape)
    return pl.BlockSpec(shape, lambda *_: (0,) * n)


_ANY = pl.BlockSpec(memory_space=pl.ANY)


class _Exchange(NamedTuple):
    arrays: list
    out_shape: list
    sems: list
    make: Callable


def _call(body, ex, *, name, grid, in_specs, out_specs, out_shape, scratch_shapes, args):
    ki, ko, ks = len(in_specs), len(out_specs), len(scratch_shapes)
    ei, eo = len(ex.arrays), len(ex.out_shape)
    last = [g - 1 for g in grid]

    def full_body(*refs):
        r = list(refs)
        ins, eins, r = r[:ki], r[ki:ki + ei], r[ki + ei:]
        outs, eouts, r = r[:ko], r[ko:ko + eo], r[ko + eo:]
        scr, esems = r[:ks], r[ks:]
        start, finish = ex.make(eins, eouts, esems)
        ids = [pl.program_id(a) for a in range(len(grid))]
        is_first = functools.reduce(lambda p, q: p & q, [i == 0 for i in ids])
        is_last = functools.reduce(lambda p, q: p & q, [i == l for i, l in zip(ids, last)])
        pl.when(is_first)(start)
        body(*ins, *outs, *scr)
        pl.when(is_last)(finish)

    res = pl.pallas_call(
        full_body, name=name, grid=grid, in_specs=list(in_specs) + [_ANY] * ei, out_specs=list(out_specs) + [_ANY] * eo,
        out_shape=list(out_shape) + list(ex.out_shape), scratch_shapes=list(scratch_shapes) + list(ex.sems),
        compiler_params=_cp(("arbitrary",) * len(grid)),
    )(*args, *ex.arrays)
    return res[:ko], res[ko:]


def _run_exchange(ex, name):
    ni, no = len(ex.arrays), len(ex.out_shape)

    def body(*refs):
        start, finish = ex.make(refs[:ni], refs[ni:ni + no], refs[ni + no:])
        start()
        finish()

    return pl.pallas_call(body, name=name, in_specs=[_ANY] * ni, out_specs=[_ANY] * no, out_shape=list(ex.out_shape),
                          scratch_shapes=list(ex.sems))(*ex.arrays)


def _proj_fwd(x, norm_g, wp, ex):
    T = x.shape[0]
    tm, tn = min(T, 1024), 1152

    def body(x_ref, g_ref, w_ref, proj_ref, h_ref, hs_ref):
        @pl.when(pl.program_id(1) == 0)
        def _():
            xf = x_ref[...]
            r = lax.rsqrt(jnp.mean(xf * xf, axis=-1, keepdims=True) + EPS)
            h = (xf * r * g_ref[...]).astype(BF16)
            hs_ref[...] = h
            h_ref[...] = h

        proj_ref[...] = _dot(hs_ref[...], w_ref[...])

    return _call(
        body, ex, name="proj_fwd", grid=(T // tm, NP // tn),
        in_specs=[pl.BlockSpec((tm, D), lambda i, j: (i, 0)), _full((1, D)), pl.BlockSpec((D, tn), lambda i, j: (0, j))],
        out_specs=[pl.BlockSpec((tm, tn), lambda i, j: (i, j)), pl.BlockSpec((tm, D), lambda i, j: (i, 0))],
        out_shape=[jax.ShapeDtypeStruct((T, NP), F32), jax.ShapeDtypeStruct((T, D), BF16)],
        scratch_shapes=[pltpu.VMEM((tm, D), BF16)], args=(x, norm_g, wp))


def _gmlp_tile():
    return 256


def _gmlp_fwd(proj, ln_g, ln_b, ws, bst):
    T = proj.shape[0]
    tm = min(T, _gmlp_tile())

    def body(u_ref, v_ref, z_ref, lg_ref, lb_ref, ws_ref, bs_ref, ya_ref, vn_s):
        vg = _gelu(v_ref[...])
        mu = jnp.mean(vg, axis=-1, keepdims=True)
        xc = vg - mu
        rstd = lax.rsqrt(jnp.mean(xc * xc, axis=-1, keepdims=True) + EPS)
        vn_s[...] = (xc * rstd * lg_ref[...] + lb_ref[...]).astype(BF16)
        for c in range(tm // CH):
            rs = slice(c * CH, (c + 1) * CH)
            for g in range(NG):
                cs_ = slice(g * 256, (g + 1) * 256)
                sv = _dot(ws_ref[g], vn_s[rs, cs_]) + bs_ref[g]
                z = z_ref[rs, cs_]
                ya_ref[rs, cs_] = (_gelu(u_ref[rs, cs_]) * sv * (z * _sigmoid(z))).astype(BF16)

    blk = lambda off: pl.BlockSpec((tm, D), lambda i: (i, off // D))
    return pl.pallas_call(
        body, name="gmlp_fwd", grid=(T // tm,),
        in_specs=[blk(OFF_U), blk(OFF_V), blk(OFF_ZA), _full((1, D)), _full((1, D)), _full((NG, CH, CH)), _full((NG, CH, 256))],
        out_specs=pl.BlockSpec((tm, D), lambda i: (i, 0)),
        out_shape=jax.ShapeDtypeStruct((T, D), BF16),
        scratch_shapes=[pltpu.VMEM((tm, D), BF16)],
        compiler_params=_cp(("parallel",)),
    )(proj, proj, proj, ln_g, ln_b, ws, bst)


def _gmlp_bwd(proj, dya, ln_g, ln_b, ws, wst, bst, gsel, ex):
    T = proj.shape[0]
    tm = min(T, _gmlp_tile())

    def body(u_ref, v_ref, z_ref, dy_ref, lg_ref, lb_ref, ws_ref, wst_ref, bs_ref, gsel_ref,
             du_ref, dv_ref, dz_ref, dws_ref, dbs_ref, dln_ref, vn_s, dsv_s, dvn_s):
        @pl.when(pl.program_id(0) == 0)
        def _():
            dws_ref[...] = jnp.zeros_like(dws_ref)
            dbs_ref[...] = jnp.zeros_like(dbs_ref)
            dln_ref[...] = jnp.zeros_like(dln_ref)

        vg, dvg_dv = _gelu_and_grad(v_ref[...])
        mu = jnp.mean(vg, axis=-1, keepdims=True)
        xc = vg - mu
        rstd = lax.rsqrt(jnp.mean(xc * xc, axis=-1, keepdims=True) + EPS)
        vhat = xc * rstd
        vn_s[...] = (vhat * lg_ref[...] + lb_ref[...]).astype(BF16)
        ri = lax.broadcasted_iota(jnp.int32, (CH, CH), 0)
        ci = lax.broadcasted_iota(jnp.int32, (CH, CH), 1)
        tril = (ri >= ci).astype(F32)
        for c in range(tm // CH):
            rs = slice(c * CH, (c + 1) * CH)
            for g in range(NG):
                cs_ = slice(g * 256, (g + 1) * 256)
                vn = vn_s[rs, cs_]
                sv = _dot(ws_ref[g], vn) + bs_ref[g]
                z = z_ref[rs, cs_]
                sz = _sigmoid(z)
                ug, dug_du = _gelu_and_grad(u_ref[rs, cs_])
                dy = dy_ref[rs, cs_].astype(F32)
                t = dy * z * sz
                du_ref[rs, cs_] = (t * sv * dug_du).astype(BF16)
                dz_ref[rs, cs_] = (dy * ug * sv * sz * (1.0 + z * (1.0 - sz))).astype(BF16)
                dsv = (t * ug).astype(BF16)
                dsv_s[rs, cs_] = dsv
                dvn_s[rs, cs_] = _dot(wst_ref[g], dsv)
                dws_ref[g] += _dot_nt(dsv, vn) * tril
            dbs_ref[...] += _dot(dsv_s[rs, :], gsel_ref[...])
        dvn = dvn_s[...]
        dln_ref[0:1, :] += jnp.sum(dvn * vhat, axis=0, keepdims=True)
        dln_ref[1:2, :] += jnp.sum(dvn, axis=0, keepdims=True)
        dvh = dvn * lg_ref[...]
        dvg = rstd * (dvh - jnp.mean(dvh, axis=-1, keepdims=True) - vhat * jnp.mean(dvh * vhat, axis=-1, keepdims=True))
        dv_ref[...] = (dvg * dvg_dv).astype(BF16)

    blk = lambda off: pl.BlockSpec((tm, D), lambda i: (i, off // D))
    row = pl.BlockSpec((tm, D), lambda i: (i, 0))
    return _call(
        body, ex, name="gmlp_bwd", grid=(T // tm,),
        in_specs=[blk(OFF_U), blk(OFF_V), blk(OFF_ZA), row, _full((1, D)), _full((1, D)), _full((NG, CH, CH)),
                  _full((NG, CH, CH)), _full((NG, CH, 256)), _full((D, 128))],
        out_specs=[row, row, row, _full((NG, CH, CH)), _full((CH, 128)), _full((8, D))],
        out_shape=[jax.ShapeDtypeStruct((T, D), BF16)] * 3
        + [jax.ShapeDtypeStruct((NG, CH, CH), F32), jax.ShapeDtypeStruct((CH, 128), F32), jax.ShapeDtypeStruct((8, D), F32)],
        scratch_shapes=[pltpu.VMEM((tm, D), BF16), pltpu.VMEM((tm, D), BF16), pltpu.VMEM((tm, D), F32)],
        args=(proj, proj, proj, dya, ln_g, ln_b, ws, wst, bst, gsel))


def _conv_pre(x, xp, cw_ref, cb_ref):
    row = lax.broadcasted_iota(jnp.int32, (CH, 1), 0)
    shifted = [x]
    for j in (1, 2, 3):
        shifted.append(jnp.where(row >= j, pltpu.roll(x, j, 0), pltpu.roll(xp, j, 0)))
    pre = cb_ref[...] + cw_ref[3:4, :] * x
    for j in (1, 2, 3):
        pre = pre + cw_ref[3 - j:4 - j, :] * shifted[j]
    return pre, shifted


def _split_dot(x, w, parts, w_left=False):
    acc, r = None, x
    for k in range(parts):
        hi = r.astype(BF16)
        d = _dot(w, hi) if w_left else _dot(hi, w)
        acc = d if acc is None else acc + d
        if k + 1 < parts:
            r = r - hi.astype(F32)
    return acc


def _chunk_decays(dt, alog_ref, e_ref, cs_s, cst_s, csx_s):
    a = -jnp.exp(alog_ref[...])
    ri = lax.broadcasted_iota(jnp.int32, (CH, CH), 0)
    ci = lax.broadcasted_iota(jnp.int32, (CH, CH), 1)
    tril = ri >= ci
    cs = _split_dot(dt * a, tril.astype(BF16), 3, w_left=True)
    cs_s[...] = cs
    cst_s[...] = cs.T
    csx_s[...] = _split_dot(cs, e_ref[...], 3)
    return a, tril, ri, ci


def _lmat(cst_s, h, tril):
    rowb = jnp.broadcast_to(cst_s[h:h + 1, :], (CH, CH))
    return jnp.exp(jnp.where(tril, rowb.T - rowb, -jnp.inf))


def _head_pair_rows(v, lane):
    return jnp.concatenate([jnp.where(lane < HD, v, 0.0), jnp.where(lane < HD, 0.0, v)], axis=0).astype(BF16)


def _ssd_fwd(proj, cw, cb, dtb, alog, dskx, sg, e128):
    T = proj.shape[0]
    nc = T // CH

    def body(xbc_ref, zb_ref, dt_ref, cw_ref, cb_ref, dtb_ref, alog_ref, dx_ref, sg_ref, e_ref,
             y_ref, yb_ref, hp_ref, xprev_s, h_s, cs_s, cst_s, csx_s, yz_s):
        @pl.when(pl.program_id(0) == 0)
        def _():
            xprev_s[...] = jnp.zeros_like(xprev_s)
            h_s[...] = jnp.zeros_like(h_s)

        x = xbc_ref[...]
        pre, _ = _conv_pre(x, xprev_s[...], cw_ref, cb_ref)
        xprev_s[...] = x
        xc = pre * _sigmoid(pre)
        dt = _softplus(dt_ref[...] + dtb_ref[...])
        a, tril, _, lane = _chunk_decays(dt, alog_ref, e_ref, cs_s, cst_s, csx_s)
        dt_x = _split_dot(dt, e_ref[...], 2)
        cs_last_x = csx_s[CH - 1:CH, :]
        hp_ref[0] = h_s[...]
        for g in range(NG):
            gs = slice(g * 512, (g + 1) * 512)
            bg = xc[:, DI + g * NST:DI + (g + 1) * NST].astype(BF16)
            cg = xc[:, DI + 512 + g * NST:DI + 512 + (g + 1) * NST].astype(BF16)
            cbm = _dot_nt(cg, bg)
            xg = xc[:, gs]
            xdt = xg * dt_x[:, gs]
            hprev = h_s[:, gs]
            csx = csx_s[:, gs]
            yoff = _dot(cg, hprev.astype(BF16)) * jnp.exp(csx)
            st = _dot_tn(bg, (xdt * jnp.exp(cs_last_x[:, gs] - csx)).astype(BF16))
            h_s[:, gs] = jnp.exp(cs_last_x[:, gs]) * hprev + st
            ssq = jnp.zeros((CH, 1), F32)
            for q in range(4):
                h0 = g * 8 + 2 * q
                ps = slice(q * 128, (q + 1) * 128)
                cols = slice(g * 512 + q * 128, g * 512 + (q + 1) * 128)
                m01 = jnp.concatenate([cbm * _lmat(cst_s, h0, tril), cbm * _lmat(cst_s, h0 + 1, tril)], axis=1).astype(BF16)
                yq = _dot(m01, _head_pair_rows(xdt[:, ps], lane)) + yoff[:, ps] + xg[:, ps] * dx_ref[:, cols]
                y_ref[:, cols] = yq
                z = zb_ref[:, cols]
                yz = yq * z * _sigmoid(z)
                yz_s[:, cols] = yz
                ssq = ssq + jnp.sum(yz * yz, axis=1, keepdims=True)
            rg = lax.rsqrt(ssq * (1.0 / 512.0) + EPS)
            yb_ref[:, gs] = (yz_s[:, gs] * rg * sg_ref[:, gs]).astype(BF16)

    return pl.pallas_call(
        body, name="ssd_fwd", grid=(nc,),
        in_specs=[pl.BlockSpec((CH, CD), lambda c: (c, OFF_XBC // CD)), pl.BlockSpec((CH, DI), lambda c: (c, OFF_ZB // DI)),
                  pl.BlockSpec((CH, DTW), lambda c: (c, OFF_DT // DTW)), _full((4, CD)), _full((1, CD)), _full((1, DTW)),
                  _full((1, DTW)), _full((1, DI)), _full((1, DI)), _full((DTW, DI))],
        out_specs=[pl.BlockSpec((CH, DI), lambda c: (c, 0)), pl.BlockSpec((CH, DI), lambda c: (c, 0)),
                   pl.BlockSpec((1, NST, DI), lambda c: (c, 0, 0))],
        out_shape=[jax.ShapeDtypeStruct((T, DI), F32), jax.ShapeDtypeStruct((T, DI), BF16),
                   jax.ShapeDtypeStruct((nc, NST, DI), F32)],
        scratch_shapes=[pltpu.VMEM((CH, CD), F32), pltpu.VMEM((NST, DI), F32), pltpu.VMEM((CH, CH), F32),
                        pltpu.VMEM((CH, CH), F32), pltpu.VMEM((CH, DI), F32), pltpu.VMEM((CH, DI), F32)],
        compiler_params=_cp(("arbitrary",)),
    )(proj, proj, proj, cw, cb, dtb, alog, dskx, sg, e128)


def _ssd_bwd(proj, y, dyb, hprev_all, cw, cb, dtb, alog, dskx, sg, e128, et128, ex):
    T = proj.shape[0]
    nc = T // CH

    def body(xbc_ref, xbcp_ref, zb_ref, dt_ref, y_ref, dyb_ref, hp_ref, cw_ref, cb_ref, dtb_ref, alog_ref, dx_ref, sg_ref,
             e_ref, et_ref, dxbc_ref, dzb_ref, ddt_ref, dcw_ref, dsm_ref, dsg_ref,
             g_s, dpn_s, cs_s, cst_s, csx_s, dy_s, dxdt_s, dxs_s, dsd_s, dxc_s, gh_s):
        i = pl.program_id(0)

        @pl.when(i == 0)
        def _():
            g_s[...] = jnp.zeros_like(g_s)
            dpn_s[...] = jnp.zeros_like(dpn_s)
            dcw_ref[...] = jnp.zeros_like(dcw_ref)
            dsm_ref[...] = jnp.zeros_like(dsm_ref)
            dsg_ref[...] = jnp.zeros_like(dsg_ref)

        x = xbc_ref[...]
        xp = jnp.where(i == nc - 1, 0.0, xbcp_ref[...])
        pre, shifted = _conv_pre(x, xp, cw_ref, cb_ref)
        sp = _sigmoid(pre)
        xc = pre * sp
        dtr = dt_ref[...] + dtb_ref[...]
        dt = _softplus(dtr)
        a, tril, ri, lane = _chunk_decays(dt, alog_ref, e_ref, cs_s, cst_s, csx_s)
        et = et_ref[...]
        dt_x = _split_dot(dt, e_ref[...], 2)
        cs_last_x = csx_s[CH - 1:CH, :]

        for g in range(NG):
            gs = slice(g * 512, (g + 1) * 512)
            z = zb_ref[:, gs]
            sz = _sigmoid(z)
            yv = y_ref[:, gs]
            yz = yv * z * sz
            rg = lax.rsqrt(jnp.mean(yz * yz, axis=-1, keepdims=True) + EPS)
            yn = yz * rg
            dyb = dyb_ref[:, gs].astype(F32)
            dsg_ref[0:1, gs] += jnp.sum(dyb * yn, axis=0, keepdims=True)
            dyn = dyb * sg_ref[:, gs]
            dyz = rg * (dyn - yn * jnp.mean(dyn * yn, axis=-1, keepdims=True))
            dy_s[:, gs] = dyz * z * sz
            dzb_ref[:, gs] = (dyz * yv * sz * (1.0 + z * (1.0 - sz))).astype(BF16)

        rsum = jnp.zeros((CH, DTW), F32)
        csum_t = jnp.zeros((DTW, CH), F32)
        for g in range(NG):
            gs = slice(g * 512, (g + 1) * 512)
            bg = xc[:, DI + g * NST:DI + (g + 1) * NST].astype(BF16)
            cg = xc[:, DI + 512 + g * NST:DI + 512 + (g + 1) * NST].astype(BF16)
            cbm = _dot_nt(cg, bg)
            xdt = xc[:, gs] * dt_x[:, gs]
            hprev = hp_ref[0, :, gs]
            hpb = hprev.astype(BF16)
            gn = g_s[:, gs]
            gnb = gn.astype(BF16)
            dy = dy_s[:, gs]
            csx = csx_s[:, gs]
            ecs = jnp.exp(csx)
            dec = jnp.exp(cs_last_x[:, gs] - csx)
            dye = (dy * ecs).astype(BF16)
            dc = _dot_nt(dye, hpb)
            dprev = _dot_tn(cg, dye)
            dxdt_state = dec * _dot(bg, gnb)
            db = _dot_nt((xdt * dec).astype(BF16), gnb)
            dcb = jnp.zeros((CH, CH), F32)
            for q in range(4):
                h0 = g * 8 + 2 * q
                ps = slice(q * 128, (q + 1) * 128)
                dyp = dy[:, ps]
                l0 = _lmat(cst_s, h0, tril)
                l1 = _lmat(cst_s, h0 + 1, tril)
                m0 = cbm * l0
                m1 = cbm * l1
                dm = _dot_nt(dyp.astype(BF16), _head_pair_rows(xdt[:, ps], lane))
                dm0 = dm[:, :CH]
                dm1 = dm[:, CH:]
                dcb = dcb + dm0 * l0 + dm1 * l1
                for hh, qm in ((h0, dm0 * m0), (h0 + 1, dm1 * m1)):
                    rsum = jnp.where(lane == hh, jnp.sum(qm, axis=1, keepdims=True), rsum)
                    csum_t = jnp.where(ri == hh, jnp.sum(qm, axis=0, keepdims=True), csum_t)
                mst = jnp.concatenate([m0, m1], axis=0).astype(BF16)
                d = _dot_tn(mst, _head_pair_rows(dyp, lane))
                dxdt_s[:, g * 512 + q * 128:g * 512 + (q + 1) * 128] = d + dxdt_state[:, ps]
            yoff = _dot(cg, hpb) * ecs
            dsd_s[:, gs] = xdt * dxdt_state
            dxs_s[:, gs] = dy * yoff
            dcbb = dcb.astype(BF16)
            dxc_s[:, DI + 512 + g * NST:DI + 512 + (g + 1) * NST] = dc + _dot(dcbb, bg)
            dxc_s[:, DI + g * NST:DI + (g + 1) * NST] = db + _dot_tn(dcbb, cg)
            gh_s[:, gs] = jnp.broadcast_to(jnp.sum(gn * hprev, axis=0, keepdims=True), (8, 512))
            g_s[:, gs] = dprev + jnp.exp(cs_last_x[:, gs]) * gn

        xs = xc[:, :DI]
        dy = dy_s[...]
        dxdt = dxdt_s[...]
        cs_last = cs_s[CH - 1:CH, :]
        state_e = _split_dot(dsd_s[...], et, 2)
        dcd = 0.125 * jnp.sum(_split_dot(gh_s[...], et, 2), axis=0, keepdims=True) * jnp.exp(cs_last)
        row = lax.broadcasted_iota(jnp.int32, (CH, 1), 0)
        dcs = rsum - csum_t.T + _split_dot(dxs_s[...], et, 2) - state_e
        dcs = dcs + jnp.where(row == CH - 1, jnp.sum(state_e, axis=0, keepdims=True) + dcd, 0.0)
        dda = _split_dot(dcs, (lane >= ri).astype(BF16), 3, w_left=True)
        ddt = dda * a + _dot((dxdt * xs).astype(BF16), et)
        ddtr = jnp.where(lane < NH, ddt * _sigmoid(dtr), 0.0)
        ddt_ref[...] = ddtr.astype(BF16)
        dsm_ref[0:1, :] += jnp.sum(ddtr, axis=0, keepdims=True)
        dsm_ref[1:2, :] += jnp.sum(dda * dt, axis=0, keepdims=True) * a
        dsm_ref[2:3, :] += jnp.sum(_dot((dy * xs).astype(BF16), et), axis=0, keepdims=True)
        dxc_s[:, :DI] = dxdt * dt_x + dy * dx_ref[...]

        dpre = dxc_s[...] * sp * (1.0 + pre * (1.0 - sp))
        dpn = dpn_s[...]
        dcw_ref[4:5, :] += jnp.sum(dpre, axis=0, keepdims=True)
        dxbc = cw_ref[3:4, :] * dpre
        dcw_ref[3:4, :] += jnp.sum(dpre * shifted[0], axis=0, keepdims=True)
        for j in (1, 2, 3):
            dcw_ref[3 - j:4 - j, :] += jnp.sum(dpre * shifted[j], axis=0, keepdims=True)
            ahead = jnp.where(row < CH - j, pltpu.roll(dpre, CH - j, 0), pltpu.roll(dpn, CH - j, 0))
            dxbc = dxbc + cw_ref[3 - j:4 - j, :] * ahead
        dpn_s[...] = dpre
        dxbc_ref[...] = dxbc.astype(BF16)

    rev = lambda c: nc - 1 - c
    return _call(
        body, ex, name="ssd_bwd", grid=(nc,),
        in_specs=[pl.BlockSpec((CH, CD), lambda c: (rev(c), OFF_XBC // CD)),
                  pl.BlockSpec((CH, CD), lambda c: (jnp.maximum(rev(c) - 1, 0), OFF_XBC // CD)),
                  pl.BlockSpec((CH, DI), lambda c: (rev(c), OFF_ZB // DI)),
                  pl.BlockSpec((CH, DTW), lambda c: (rev(c), OFF_DT // DTW)),
                  pl.BlockSpec((CH, DI), lambda c: (rev(c), 0)), pl.BlockSpec((CH, DI), lambda c: (rev(c), 0)),
                  pl.BlockSpec((1, NST, DI), lambda c: (rev(c), 0, 0)),
                  _full((4, CD)), _full((1, CD)), _full((1, DTW)), _full((1, DTW)), _full((1, DI)), _full((1, DI)),
                  _full((DTW, DI)), _full((DI, DTW))],
        out_specs=[pl.BlockSpec((CH, CD), lambda c: (rev(c), 0)), pl.BlockSpec((CH, DI), lambda c: (rev(c), 0)),
                   pl.BlockSpec((CH, DTW), lambda c: (rev(c), 0)), _full((8, CD)), _full((8, DTW)), _full((8, DI))],
        out_shape=[jax.ShapeDtypeStruct((T, CD), BF16), jax.ShapeDtypeStruct((T, DI), BF16), jax.ShapeDtypeStruct((T, DTW), BF16),
                   jax.ShapeDtypeStruct((8, CD), F32), jax.ShapeDtypeStruct((8, DTW), F32), jax.ShapeDtypeStruct((8, DI), F32)],
        scratch_shapes=[pltpu.VMEM((NST, DI), F32), pltpu.VMEM((CH, CD), F32), pltpu.VMEM((CH, CH), F32), pltpu.VMEM((CH, CH), F32),
                        pltpu.VMEM((CH, DI), F32), pltpu.VMEM((CH, DI), F32), pltpu.VMEM((CH, DI), F32), pltpu.VMEM((CH, DI), F32),
                        pltpu.VMEM((CH, DI), F32), pltpu.VMEM((CH, CD), F32), pltpu.VMEM((8, DI), F32)],
        args=(proj, proj, proj, proj, y, dyb, hprev_all, cw, cb, dtb, alog, dskx, sg, e128, et128))


def _merge_tile():
    return 256


def _merge_fwd(x, ya, yb, proj, p, tgt, w_oa, w_ob, w_out, w_pg, w_ple, w_pgt, ple_g, fin_g):
    T = x.shape[0]
    tm = min(T, _merge_tile())

    def body(x_ref, ya_ref, yb_ref, ga_ref, gb_ref, p_ref, t_ref, woa, wob, wout, wpg, wple, wpgt, pg_ref, fg_ref,
             dx1_ref, dx1b_ref, oa_ref, ob_ref, mg_ref, hp_ref, dpre_ref, dpe_ref, acc_ref):
        @pl.when(pl.program_id(0) == 0)
        def _():
            acc_ref[...] = jnp.zeros_like(acc_ref)

        oa = _dot(ya_ref[...], woa[...])
        ob = _dot(yb_ref[...], wob[...])
        oa_ref[...] = oa.astype(BF16)
        ob_ref[...] = ob.astype(BF16)
        mg = _sigmoid(ga_ref[...]) * oa + _sigmoid(gb_ref[...]) * ob
        mgb = mg.astype(BF16)
        mg_ref[...] = mgb
        x1 = x_ref[...] + _dot(mgb, wout[...])
        r2 = lax.rsqrt(jnp.mean(x1 * x1, axis=-1, keepdims=True) + EPS)
        xh1 = x1 * r2
        hpb = (xh1 * pg_ref[...]).astype(BF16)
        hp_ref[...] = hpb
        gate = _sigmoid(_dot(hpb, wpg[...]))
        pe = _dot(p_ref[...].astype(BF16), wple[...])
        x2 = x1 + gate * pe
        r3 = lax.rsqrt(jnp.mean(x2 * x2, axis=-1, keepdims=True) + EPS)
        xh2 = x2 * r3
        err = xh2 * fg_ref[...] - t_ref[...]
        acc_ref[2:3, :] += 0.5 * jnp.sum(jnp.mean(err * err, axis=-1, keepdims=True))
        dyo = err * (1.0 / D)
        acc_ref[0:1, :] += jnp.sum(dyo * xh2, axis=0, keepdims=True)
        dn = dyo * fg_ref[...]
        dx2 = r3 * (dn - xh2 * jnp.mean(dn * xh2, axis=-1, keepdims=True))
        dpe_ref[...] = (dx2 * gate).astype(BF16)
        dpre = (dx2 * pe * gate * (1.0 - gate)).astype(BF16)
        dpre_ref[...] = dpre
        dhp = _dot(dpre, wpgt[...])
        acc_ref[1:2, :] += jnp.sum(dhp * xh1, axis=0, keepdims=True)
        dhn = dhp * pg_ref[...]
        dx1 = dx2 + r2 * (dhn - xh1 * jnp.mean(dhn * xh1, axis=-1, keepdims=True))
        dx1_ref[...] = dx1
        dx1b_ref[...] = dx1.astype(BF16)

    row = lambda w: pl.BlockSpec((tm, w), lambda i: (i, 0))
    wsp = lambda s: pl.BlockSpec(s, lambda i: (0, 0), pipeline_mode=pl.Buffered(1))
    return pl.pallas_call(
        body, name="merge_fwd", grid=(T // tm,),
        in_specs=[row(D), row(D), row(DI), pl.BlockSpec((tm, D), lambda i: (i, OFF_GA // D)),
                  pl.BlockSpec((tm, D), lambda i: (i, OFF_GB // D)), row(PLE), row(D),
                  wsp((D, D)), wsp((DI, D)), wsp((D, D)), wsp((D, D)), wsp((PLE, D)), wsp((D, D)), _full((1, D)), _full((1, D))],
        out_specs=[row(D)] * 8 + [_full((8, D))],
        out_shape=[jax.ShapeDtypeStruct((T, D), F32)] + [jax.ShapeDtypeStruct((T, D), BF16)] * 7 + [jax.ShapeDtypeStruct((8, D), F32)],
        compiler_params=_cp(("arbitrary",)),
    )(x, ya, yb, proj, proj, p, tgt, w_oa, w_ob, w_out, w_pg, w_ple, w_pgt, ple_g, fin_g)


def _merge_bwd(dx1, oa, ob, proj, w_outt, w_oat, w_obt):
    T = dx1.shape[0]
    tm = min(T, _merge_tile())

    def body(dx1_ref, oa_ref, ob_ref, ga_ref, gb_ref, woutt, woat, wobt, doa_ref, dob_ref, dya_ref, dyb_ref, dg_ref):
        dmg = _dot(dx1_ref[...].astype(BF16), woutt[...])
        sa = _sigmoid(ga_ref[...])
        sb = _sigmoid(gb_ref[...])
        doa = (dmg * sa).astype(BF16)
        dob = (dmg * sb).astype(BF16)
        doa_ref[...] = doa
        dob_ref[...] = dob
        dg_ref[:, :D] = (dmg * oa_ref[...].astype(F32) * sa * (1.0 - sa)).astype(BF16)
        dg_ref[:, D:] = (dmg * ob_ref[...].astype(F32) * sb * (1.0 - sb)).astype(BF16)
        dya_ref[...] = _dot(doa, woat[...]).astype(BF16)
        dyb_ref[...] = _dot(dob, wobt[...]).astype(BF16)

    row = lambda w: pl.BlockSpec((tm, w), lambda i: (i, 0))
    wsp = lambda s: pl.BlockSpec(s, lambda i: (0, 0), pipeline_mode=pl.Buffered(1))
    return pl.pallas_call(
        body, name="merge_bwd", grid=(T // tm,),
        in_specs=[row(D), row(D), row(D), pl.BlockSpec((tm, D), lambda i: (i, OFF_GA // D)),
                  pl.BlockSpec((tm, D), lambda i: (i, OFF_GB // D)), wsp((D, D)), wsp((D, D)), wsp((D, DI))],
        out_specs=[row(D), row(D), row(D), row(DI), row(2 * D)],
        out_shape=[jax.ShapeDtypeStruct((T, D), BF16)] * 3 + [jax.ShapeDtypeStruct((T, DI), BF16), jax.ShapeDtypeStruct((T, 2 * D), BF16)],
        compiler_params=_cp(("parallel",)),
    )(dx1, oa, ob, proj, proj, w_outt, w_oat, w_obt)


def _wgrad(a, b, name):
    T, K = a.shape
    N = b.shape[1]
    tt, tn = min(T, 1024), min(N, 1024)
    nt = T // tt

    def body(a_ref, b_ref, o_ref, acc_s):
        t = pl.program_id(1)

        @pl.when(t == 0)
        def _():
            acc_s[...] = jnp.zeros_like(acc_s)

        acc_s[...] += _dot_tn(a_ref[...].astype(BF16), b_ref[...])

        @pl.when(t == nt - 1)
        def _():
            o_ref[...] = acc_s[...].astype(BF16)

    return pl.pallas_call(
        body, name=name, grid=(N // tn, nt),
        in_specs=[pl.BlockSpec((tt, K), lambda n, t: (t, 0)), pl.BlockSpec((tt, tn), lambda n, t: (t, n))],
        out_specs=pl.BlockSpec((K, tn), lambda n, t: (0, n)),
        out_shape=jax.ShapeDtypeStruct((K, N), BF16),
        scratch_shapes=[pltpu.VMEM((K, tn), F32)],
        compiler_params=_cp(("parallel", "arbitrary")),
    )(a, b)


def _proj_bwd(x, dx1, norm_g, wpt, dxbc, du, dzb, dv, dza, dgab, ddt, ex):
    T = x.shape[0]
    tm = min(T, 512)
    nk = OFF_DT // D + 1
    pieces = [(dxbc, OFF_XBC), (du, OFF_U), (dzb, OFF_ZB), (dv, OFF_V), (dza, OFF_ZA), (dgab, OFF_GA)]
    ranges = [(off // D, off // D + arr.shape[1] // D) for arr, off in pieces]

    def body(x_ref, dx1_ref, g_ref, w_ref, wdt_ref, *rest):
        piece_refs, ddt_ref, gx_ref, dng_ref, acc_s = rest[:6], rest[6], rest[7], rest[8], rest[9]
        i, k = pl.program_id(0), pl.program_id(1)

        @pl.when((i == 0) & (k == 0))
        def _():
            dng_ref[...] = jnp.zeros_like(dng_ref)

        @pl.when(k == 0)
        def _():
            acc_s[...] = jnp.zeros_like(acc_s)

        for ref, (lo, hi) in zip(piece_refs, ranges):
            @pl.when((k >= lo) & (k < hi))
            def _(ref=ref):
                acc_s[...] += _dot(ref[...], w_ref[...])

        @pl.when(k == nk - 1)
        def _():
            dh = acc_s[...] + _dot(ddt_ref[...], wdt_ref[...])
            xf = x_ref[...]
            r = lax.rsqrt(jnp.mean(xf * xf, axis=-1, keepdims=True) + EPS)
            xh = xf * r
            dng_ref[0:1, :] += jnp.sum(dh * xh, axis=0, keepdims=True)
            dxn = dh * g_ref[...]
            gx_ref[...] = dx1_ref[...] + r * (dxn - xh * jnp.mean(dxn * xh, axis=-1, keepdims=True))

    def piece_spec(lo, hi):
        return pl.BlockSpec((tm, D), lambda i, k: (i, jnp.clip(k - lo, 0, hi - lo - 1)))

    row = pl.BlockSpec((tm, D), lambda i, k: (i, 0))
    return _call(
        body, ex, name="proj_bwd", grid=(T // tm, nk),
        in_specs=[row, row, _full((1, D)), pl.BlockSpec((D, D), lambda i, k: (jnp.minimum(k, nk - 2), 0)),
                  pl.BlockSpec((DTW, D), lambda i, k: (OFF_DT // DTW, 0))]
        + [piece_spec(lo, hi) for lo, hi in ranges] + [pl.BlockSpec((tm, DTW), lambda i, k: (i, 0))],
        out_specs=[row, _full((8, D))],
        out_shape=[jax.ShapeDtypeStruct((T, D), F32), jax.ShapeDtypeStruct((8, D), F32)],
        scratch_shapes=[pltpu.VMEM((tm, D), F32)],
        args=(x, dx1, norm_g, wpt, wpt, *[arr for arr, _ in pieces], ddt))


def _row_tile(R, C, limit=1 << 20):
    if R * C * 4 <= limit:
        return R
    fits = [t for t in range(16, R, 16) if R % t == 0 and t * C * 4 <= limit]
    return fits[-1]


def _adamw(w, m, v, parts, name):
    R, C = w.shape[-2:]
    tr = _row_tile(R, C)
    c1 = 1.0 - ADAM_B1 ** ADAM_STEP
    c2 = 1.0 - ADAM_B2 ** ADAM_STEP
    n = len(parts)
    spec2 = pl.BlockSpec((tr, C), lambda i: (i, 0))
    slot = lambda k: pl.BlockSpec((None, tr, C), lambda i: (k, i, 0))
    wspec = spec2 if w.ndim == 2 else slot(0)
    part_specs = [slot(p[1]) if isinstance(p, tuple) else spec2 for p in parts]
    part_arrays = [p[0] if isinstance(p, tuple) else p for p in parts]

    def body(*refs):
        w_ref, m_ref, v_ref = refs[:3]
        g_ref, d_ref, nm_ref, nv_ref = refs[3 + n:]
        g = refs[3][...].astype(F32)
        for r in refs[4:3 + n]:
            g = g + r[...].astype(F32)
        mm = ADAM_B1 * m_ref[...] + (1.0 - ADAM_B1) * g
        vv = ADAM_B2 * v_ref[...] + (1.0 - ADAM_B2) * (g * g)
        g_ref[...] = g
        nm_ref[...] = mm
        nv_ref[...] = vv
        d_ref[...] = -ADAM_LR * ((mm / c1) / (jnp.sqrt(vv / c2) + ADAM_EPS) + ADAM_WD * w_ref[...])

    return pl.pallas_call(
        body, name=name, grid=(R // tr,), in_specs=[wspec] * 3 + part_specs, out_specs=[wspec] * 4,
        out_shape=[jax.ShapeDtypeStruct(w.shape, F32)] * 4, compiler_params=_cp(("parallel",)),
    )(w, m, v, *part_arrays)


def _pair_sum(a, b, name):
    Q, R, C = a.shape
    tr = _row_tile(R, C)

    def body(a_ref, b_ref, o_ref):
        o_ref[...] = (a_ref[...].astype(F32) + b_ref[...].astype(F32)).astype(BF16)

    spec = pl.BlockSpec((1, tr, C), lambda q, i: (q, i, 0))
    return pl.pallas_call(
        body, name=name, grid=(Q, R // tr), in_specs=[spec, spec], out_specs=spec,
        out_shape=jax.ShapeDtypeStruct((Q, R, C), BF16), compiler_params=_cp(("parallel", "parallel")),
    )(a, b)


def _dev_index(px, py, pc):
    return 4 * px + 2 * py + pc


def _mesh_position():
    return lax.axis_index("x"), lax.axis_index("y"), lax.axis_index("c")


def _gather_exchange(blocks):
    n = len(blocks)

    def make(ins, outs, sems):
        send_sems, recv_sems, local_sems = sems
        x, y, c = _mesh_position()
        me, sibling = (x, y, c), (x, y, 1 - c)
        chips = [(1 - x, y), (x, 1 - y), (1 - x, 1 - y)]

        def copy(a, k, block, to, src=None):
            dst = outs[a].at[_dev_index(*block)]
            return pltpu.make_async_remote_copy(src_ref=dst if src is None else src, dst_ref=dst, send_sem=send_sems.at[a, k],
                                                recv_sem=recv_sems.at[a, k], device_id=to, device_id_type=MESH)

        mine = [pltpu.make_async_copy(ins[a], outs[a].at[_dev_index(*me)], local_sems.at[a]) for a in range(n)]
        first = []
        for a in range(n):
            first.append(copy(a, 0, me, sibling, src=ins[a]))
            first += [copy(a, 1 + j, me, (*chip, c), src=ins[a]) for j, chip in enumerate(chips)]

        def start():
            for cp in mine + first:
                cp.start()

        def finish():
            passed = []
            for j, chip in enumerate(chips):
                for a in range(n):
                    copy(a, 1 + j, (*chip, c), me).wait_recv()
                    fwd = copy(a, 4 + j, (*chip, c), sibling)
                    fwd.start()
                    passed.append(fwd)
            for a in range(n):
                copy(a, 0, sibling, me).wait_recv()
                for j, chip in enumerate(chips):
                    copy(a, 4 + j, (*chip, 1 - c), me).wait_recv()
            for cp in first + passed:
                cp.wait_send()
            for cp in mine:
                cp.wait()

        return start, finish

    return _Exchange(list(blocks), [jax.ShapeDtypeStruct((NDEV,) + b.shape, b.dtype) for b in blocks],
                     [pltpu.SemaphoreType.DMA((n, 7)), pltpu.SemaphoreType.DMA((n, 7)), pltpu.SemaphoreType.DMA((n,))], make)


def _start_wait_all(copies, local=()):
    def start():
        for cp in list(local) + list(copies):
            cp.start()

    def finish():
        for cp in copies:
            cp.wait()
        for cp in local:
            cp.wait()

    return start, finish


def _sibling_exchange(grads):
    n = len(grads)

    def make(ins, outs, sems):
        send_sems, recv_sems = sems
        x, y, c = _mesh_position()
        return _start_wait_all([pltpu.make_async_remote_copy(
            src_ref=ins[a].at[2 * q + (1 - c)], dst_ref=outs[a].at[q], send_sem=send_sems.at[a, q], recv_sem=recv_sems.at[a, q],
            device_id=(x, y, 1 - c), device_id_type=MESH) for a in range(n) for q in range(4)])

    return _Exchange(list(grads), [jax.ShapeDtypeStruct((4,) + g.shape[1:], g.dtype) for g in grads],
                     [pltpu.SemaphoreType.DMA((n, 4)), pltpu.SemaphoreType.DMA((n, 4))], make)


def _chips_exchange(partials):
    n = len(partials)

    def make(ins, outs, sems):
        send_sems, recv_sems = sems
        x, y, c = _mesh_position()
        chips = [(1 - x, y), (x, 1 - y), (1 - x, 1 - y)]
        return _start_wait_all([pltpu.make_async_remote_copy(
            src_ref=ins[a].at[2 * px + py], dst_ref=outs[a].at[k], send_sem=send_sems.at[a, k], recv_sem=recv_sems.at[a, k],
            device_id=(px, py, c), device_id_type=MESH) for a in range(n) for k, (px, py) in enumerate(chips)])

    return _Exchange(list(partials), [jax.ShapeDtypeStruct((3,) + g.shape[1:], g.dtype) for g in partials],
                     [pltpu.SemaphoreType.DMA((n, 3)), pltpu.SemaphoreType.DMA((n, 3))], make)


def _direct_gather_exchange(small):
    def make(ins, outs, sems):
        send_sems, recv_sems, local_sem = sems
        x, y, c = _mesh_position()
        slot = outs[0].at[_dev_index(x, y, c)]
        copies = []
        for r in range(1, NDEV):
            peer = (x ^ ((r >> 2) & 1), y ^ ((r >> 1) & 1), c ^ (r & 1))
            copies.append(pltpu.make_async_remote_copy(src_ref=ins[0], dst_ref=slot, send_sem=send_sems.at[r - 1],
                                                       recv_sem=recv_sems.at[r - 1], device_id=peer, device_id_type=MESH))
        return _start_wait_all(copies, local=[pltpu.make_async_copy(ins[0], slot, local_sem)])

    return _Exchange([small], [jax.ShapeDtypeStruct((NDEV,) + small.shape, small.dtype)],
                     [pltpu.SemaphoreType.DMA((7,)), pltpu.SemaphoreType.DMA((7,)), pltpu.SemaphoreType.DMA(())], make)


def _w_in_to_proj_layout(w):
    z = jnp.zeros((w.shape[0], NP - NIN), w.dtype)
    return jnp.concatenate([w[:, 5120:8192], w[:, 0:1024], w[:, 3072:5120], w[:, 1024:2048], w[:, 2048:3072],
                            w[:, 8224:9248], w[:, 9248:10272], w[:, 8192:8224], z], axis=1)


_SMALL = [("norm_g", 1024), ("ln_a_g", 1024), ("ln_a_b", 1024), ("w_s", 65536), ("b_s", 512), ("conv_w", 12288),
          ("conv_b", 3072), ("dt_bias", 128), ("a_log", 128), ("d_skip", 128), ("ssm_norm_g", 2048), ("ple_norm_g", 1024),
          ("final_g", 1024)]
_SMALL_ROWS = sum(s for _, s in _SMALL) // 128


def _pack_small(d):
    parts = []
    for name, size in _SMALL:
        if name in d:
            f = d[name].reshape(-1).astype(F32)
            parts.append(jnp.pad(f, (0, size - f.shape[0])))
        else:
            parts.append(jnp.zeros((size,), F32))
    return jnp.concatenate(parts).reshape(_SMALL_ROWS, 128)


def _unpack_small(packed, shapes):
    flat = packed.reshape(-1)
    out, off = {}, 0
    for name, size in _SMALL:
        if name in shapes:
            k = math.prod(shapes[name])
            out[name] = flat[off:off + k].reshape(shapes[name])
        off += size
    return out


_WEIGHTS = ["norm_g", "w_in", "ln_a_g", "ln_a_b", "w_s", "b_s", "conv_w", "conv_b", "dt_bias", "a_log", "d_skip", "ssm_norm_g",
            "w_oa", "w_ob", "w_out", "ple_norm_g", "w_pg", "w_ple", "final_g"]
_REPLICATED = ["norm_g", "ln_a_g", "ln_a_b", "w_s", "b_s", "conv_b", "dt_bias", "a_log", "d_skip", "ssm_norm_g", "ple_norm_g", "final_g"]


def _rows_pack(d, dtype):
    return jnp.concatenate([d["w_oa"].reshape(128, D), d["w_ob"].reshape(256, D), d["w_out"].reshape(128, D),
                            d["w_pg"].reshape(128, D), d["w_ple"].reshape(32, D)], axis=0).astype(dtype)


def kernel(x, p, norm_g, w_in, ln_a_g, ln_a_b, w_s, b_s, conv_w, conv_b, dt_bias, a_log, d_skip, ssm_norm_g, w_oa, w_ob, w_out, ple_norm_g, w_pg, w_ple, final_g, loss_target, m_norm_g, m_w_in, m_ln_a_g, m_ln_a_b, m_w_s, m_b_s, m_conv_w, m_conv_b, m_dt_bias, m_a_log, m_d_skip, m_ssm_norm_g, m_w_oa, m_w_ob, m_w_out, m_ple_norm_g, m_w_pg, m_w_ple, m_final_g, v_norm_g, v_w_in, v_ln_a_g, v_ln_a_b, v_w_s, v_b_s, v_conv_w, v_conv_b, v_dt_bias, v_a_log, v_d_skip, v_ssm_norm_g, v_w_oa, v_w_ob, v_w_out, v_ple_norm_g, v_w_pg, v_w_ple, v_final_g):
    args = dict(locals())
    w = {n: args[n] for n in _WEIGHTS}
    m = {n: args["m_" + n] for n in _WEIGHTS}
    v = {n: args["v_" + n] for n in _WEIGHTS}
    T = x.shape[1]
    xi, yi, ci = lax.axis_index("x"), lax.axis_index("y"), lax.axis_index("c")
    me = 4 * xi + 2 * yi + ci

    chip = 2 * xi + yi
    x2, p2, tgt = x.reshape(T, D), p.reshape(T, PLE), loss_target.reshape(T, D)

    norm_g2 = w["norm_g"].reshape(1, D)
    ws = jnp.where(jnp.tril(jnp.ones((CH, CH), bool))[None], w["w_s"].reshape(NG, CH, CH), 0.0).astype(BF16)
    wst = jnp.transpose(ws, (0, 2, 1))
    bst = jnp.broadcast_to(w["b_s"].reshape(NG, CH, 1), (NG, CH, 256))
    ln_g, ln_b = w["ln_a_g"].reshape(1, D), w["ln_a_b"].reshape(1, D)
    cb = w["conv_b"].reshape(1, CD)
    pad32 = lambda a: jnp.pad(a.reshape(1, NH), ((0, 0), (0, DTW - NH)))
    dtb, alog = pad32(w["dt_bias"]), pad32(w["a_log"])
    dskx = jnp.repeat(w["d_skip"].reshape(NH), HD).reshape(1, DI)
    sg = w["ssm_norm_g"].reshape(1, DI)
    ple_g, fin_g = w["ple_norm_g"].reshape(1, D), w["final_g"].reshape(1, D)
    e128 = (jnp.arange(DTW)[:, None] == (jnp.arange(DI)[None, :] // HD)).astype(BF16)
    et128 = e128.T
    gsel = ((jnp.arange(D)[:, None] // 256) == jnp.arange(128)[None, :]).astype(BF16)

    (a_all,) = _run_exchange(_gather_exchange([w["w_in"].astype(BF16)]), "all_gather_w_in")
    wp = _w_in_to_proj_layout(jnp.transpose(a_all.reshape(NDEV, D, WSH), (1, 0, 2)).reshape(D, NIN))
    (proj, h), (r_all, cw_all) = _proj_fwd(x2, norm_g2, wp, _gather_exchange([_rows_pack(w, BF16), w["conv_w"].reshape(4, CD // NDEV)]))
    f_oa = r_all[:, R_OA:R_OB].reshape(D, D)
    f_ob = r_all[:, R_OB:R_OUT].reshape(DI, D)
    f_out = r_all[:, R_OUT:R_PG].reshape(D, D)
    f_pg = r_all[:, R_PG:R_PLE].reshape(D, D)
    f_ple = jnp.transpose(r_all[:, R_PLE:R_ROWS].reshape(NDEV, PLE, 128), (1, 0, 2)).reshape(PLE, D)
    cw = jnp.transpose(cw_all, (1, 0, 2)).reshape(4, CD)

    ya = _gmlp_fwd(proj, ln_g, ln_b, ws, bst)
    y, yb, hprev = _ssd_fwd(proj, cw, cb, dtb, alog, dskx, sg, e128)
    dx1, dx1b, oa, ob, mg, hp, dpre, dpe, acc = _merge_fwd(x2, ya, yb, proj, p2, tgt, f_oa, f_ob, f_out, f_pg, f_ple, f_pg.T, ple_g, fin_g)
    doa, dob, dya, dyb, dgab = _merge_bwd(dx1b, oa, ob, proj, f_out.T, f_oa.T, f_ob.T)
    loss = lax.psum(acc[2, 0], ("x", "y", "c"))

    gple = jnp.transpose(_wgrad(p2, dpe, "wgrad_ple").reshape(PLE, NDEV, 128), (1, 0, 2)).reshape(NDEV, 32, D)
    gr = jnp.concatenate([_wgrad(ya, doa, "wgrad_oa").reshape(NDEV, 128, D), _wgrad(yb, dob, "wgrad_ob").reshape(NDEV, 256, D),
                          _wgrad(mg, dx1b, "wgrad_out").reshape(NDEV, 128, D), _wgrad(hp, dpre, "wgrad_pg").reshape(NDEV, 128, D),
                          gple], axis=1)
    (du, dv, dza, dws, dbs, dln), (from_sib_r,) = _gmlp_bwd(proj, dya, ln_g, ln_b, ws, wst, bst, gsel, _sibling_exchange([gr]))
    pr = _pair_sum(lax.dynamic_index_in_dim(gr.reshape(4, 2, R_ROWS, D), ci, 1, keepdims=False), from_sib_r, "pair_sum_rows")
    (dxbc, dzb, ddt, dcw, dsm, dsg), (rr,) = _ssd_bwd(proj, y, dyb, hprev, cw, cb, dtb, alog, dskx, sg, e128, et128, _chips_exchange([pr]))

    g_w_in = jnp.concatenate([_wgrad(h, du, "wgrad_u"), _wgrad(h, dv, "wgrad_v"), _wgrad(h, dza, "wgrad_za"),
                              _wgrad(h, dzb, "wgrad_zb"), _wgrad(h, dxbc, "wgrad_xbc"), _wgrad(h, ddt, "wgrad_dt")[:, :NH],
                              _wgrad(h, dgab, "wgrad_gab")], axis=1)
    ga = jnp.transpose(g_w_in.reshape(D, NDEV, WSH), (1, 0, 2))
    (from_sib_a,) = _run_exchange(_sibling_exchange([ga]), "reduce_scatter_sibling_w_in")
    pa = _pair_sum(lax.dynamic_index_in_dim(ga.reshape(4, 2, D, WSH), ci, 1, keepdims=False), from_sib_a, "pair_sum_w_in")
    (gx, dng), (ra,) = _proj_bwd(x2, dx1, norm_g2, wp.T, dxbc, du, dzb, dv, dza, dgab, ddt, _chips_exchange([pa]))
    pa_own = lax.dynamic_index_in_dim(pa, chip, 0, keepdims=False)
    pr_own = lax.dynamic_index_in_dim(pr, chip, 0, keepdims=False)

    g = {"norm_g": dng[0], "ln_a_g": dln[0], "ln_a_b": dln[1], "w_s": dws, "b_s": dbs[:, :NG].T,
         "conv_w": dcw[0:4], "conv_b": dcw[4], "dt_bias": dsm[0, :NH], "a_log": dsm[1, :NH], "d_skip": dsm[2, :NH],
         "ssm_norm_g": dsg[0], "ple_norm_g": acc[1], "final_g": acc[0]}
    (small_all,) = _run_exchange(_direct_gather_exchange(_pack_small(g)), "all_gather_small_grads")

    out_g, out_d, out_m, out_v = {}, {}, {}, {}
    res = _adamw(w["w_in"], m["w_in"], v["w_in"], [pa_own, (ra, 0), (ra, 1), (ra, 2)], "adamw_w_in")
    for dst, val in zip((out_g, out_d, out_m, out_v), res):
        dst["w_in"] = val
    res = _adamw(_rows_pack(w, F32), _rows_pack(m, F32), _rows_pack(v, F32), [pr_own, (rr, 0), (rr, 1), (rr, 2)], "adamw_rows")
    for dst, val in zip((out_g, out_d, out_m, out_v), res):
        dst["w_oa"] = val[R_OA:R_OB].reshape(1, 128, D)
        dst["w_ob"] = val[R_OB:R_OUT].reshape(1, 256, D)
        dst["w_out"] = val[R_OUT:R_PG].reshape(1, 128, D)
        dst["w_pg"] = val[R_PG:R_PLE].reshape(1, 128, D)
        dst["w_ple"] = val[R_PLE:R_ROWS].reshape(1, PLE, 128)
    shapes = {n: w[n].shape for n in _REPLICATED}
    res = _adamw(_pack_small({n: w[n] for n in _REPLICATED}), _pack_small({n: m[n] for n in _REPLICATED}),
                 _pack_small({n: v[n] for n in _REPLICATED}), [(small_all, k) for k in range(NDEV)], "adamw_small")
    for dst, val in zip((out_g, out_d, out_m, out_v), res):
        dst.update(_unpack_small(val, shapes))
    g_cw_full = _unpack_small(res[0], {"conv_w": (4, CD)})["conv_w"]
    g_cw = lax.dynamic_slice_in_dim(g_cw_full, me * (CD // NDEV), CD // NDEV, axis=1).reshape(12, 128)
    res = _adamw(w["conv_w"].reshape(12, 128), m["conv_w"].reshape(12, 128), v["conv_w"].reshape(12, 128), [g_cw], "adamw_conv_w")
    for dst, val in zip((out_g, out_d, out_m, out_v), res):
        dst["conv_w"] = val.reshape(1, 4, CD // NDEV)

    return (loss, gx.reshape(1, T, D), *[out_g[n] for n in _WEIGHTS], *[out_d[n] for n in _WEIGHTS],
            *[out_m[n] for n in _WEIGHTS], *[out_v[n] for n in _WEIGHTS])
```

```python
import functools
import math
from typing import Callable, NamedTuple

import jax
import jax.numpy as jnp
from jax import lax
from jax.experimental import pallas as pl
from jax.experimental.pallas import tpu as pltpu

F32 = jnp.float32
BF16 = jnp.bfloat16
HI = lax.Precision.HIGHEST
MESH = pl.DeviceIdType.MESH

D = 1024
DI = 2048
CD = 3072
NH = 32
HD = 64
NST = 128
NG = 4
CH = 128
PLE = 256
NIN = 10272
NDEV = 8
WSH = NIN // NDEV
EPS = 1e-6
OFF_XBC, OFF_U, OFF_ZB, OFF_V, OFF_ZA, OFF_GA, OFF_GB, OFF_DT = 0, 3072, 4096, 6144, 7168, 8192, 9216, 10240
NP = 10368
DTW = 128
R_OA, R_OB, R_OUT, R_PG, R_PLE, R_ROWS = 0, 128, 384, 512, 640, 672

ADAM_LR, ADAM_B1, ADAM_B2, ADAM_EPS, ADAM_WD, ADAM_STEP = 0.001, 0.9, 0.999, 1e-08, 0.01, 10

V7X_VMEM_LIMIT = 56 * 1024 * 1024


def _cp(sem=None):
    return pltpu.CompilerParams(dimension_semantics=sem, vmem_limit_bytes=V7X_VMEM_LIMIT)


def _dot(a, b, prec=None):
    return jnp.dot(a, b, preferred_element_type=F32, precision=prec)


def _dot_nt(a, b, prec=None):
    return lax.dot_general(a, b, (((1,), (1,)), ((), ())), preferred_element_type=F32, precision=prec)


def _dot_tn(a, b, prec=None):
    return lax.dot_general(a, b, (((0,), (0,)), ((), ())), preferred_element_type=F32, precision=prec)


def _sigmoid(x):
    return 1.0 / (1.0 + jnp.exp(-x))


def _gelu_and_grad(x):
    c = math.sqrt(2.0 / math.pi)
    x2 = x * x
    t = jnp.tanh(c * (x + 0.044715 * x * x2))
    g = 0.5 * x * (1.0 + t)
    dg = 0.5 * (1.0 + t) + 0.5 * x * (1.0 - t * t) * c * (1.0 + 3.0 * 0.044715 * x2)
    return g, dg


def _gelu(x):
    c = math.sqrt(2.0 / math.pi)
    return 0.5 * x * (1.0 + jnp.tanh(c * (x + 0.044715 * x * x * x)))


def _softplus(x):
    return jnp.maximum(x, 0.0) + jnp.log(1.0 + jnp.exp(-jnp.abs(x)))


def _full(shape):
    n = len(shape)
    return pl.BlockSpec(shape, lambda *_: (0,) * n)


_ANY = pl.BlockSpec(memory_space=pl.ANY)


class _Exchange(NamedTuple):
    arrays: list
    out_shape: list
    sems: list
    make: Callable


def _call(body, ex, *, name, grid, in_specs, out_specs, out_shape, scratch_shapes, args):
    ki, ko, ks = len(in_specs), len(out_specs), len(scratch_shapes)
    ei, eo = len(ex.arrays), len(ex.out_shape)
    last = [g - 1 for g in grid]

    def full_body(*refs):
        r = list(refs)
        ins, eins, r = r[:ki], r[ki:ki + ei], r[ki + ei:]
        outs, eouts, r = r[:ko], r[ko:ko + eo], r[ko + eo:]
        scr, esems = r[:ks], r[ks:]
        start, finish = ex.make(eins, eouts, esems)
        ids = [pl.program_id(a) for a in range(len(grid))]
        is_first = functools.reduce(lambda p, q: p & q, [i == 0 for i in ids])
        is_last = functools.reduce(lambda p, q: p & q, [i == l for i, l in zip(ids, last)])
        pl.when(is_first)(start)
        body(*ins, *outs, *scr)
        pl.when(is_last)(finish)

    res = pl.pallas_call(
        full_body, name=name, grid=grid, in_specs=list(in_specs) + [_ANY] * ei, out_specs=list(out_specs) + [_ANY] * eo,
        out_shape=list(out_shape) + list(ex.out_shape), scratch_shapes=list(scratch_shapes) + list(ex.sems),
        compiler_params=_cp(("arbitrary",) * len(grid)),
    )(*args, *ex.arrays)
    return res[:ko], res[ko:]


def _run_exchange(ex, name):
    ni, no = len(ex.arrays), len(ex.out_shape)

    def body(*refs):
        start, finish = ex.make(refs[:ni], refs[ni:ni + no], refs[ni + no:])
        start()
        finish()

    return pl.pallas_call(body, name=name, in_specs=[_ANY] * ni, out_specs=[_ANY] * no, out_shape=list(ex.out_shape),
                          scratch_shapes=list(ex.sems))(*ex.arrays)


def _proj_fwd(x, norm_g, wp, ex):
    T = x.shape[0]
    tm, tn = min(T, 1024), 1152

    def body(x_ref, g_ref, w_ref, proj_ref, h_ref, hs_ref):
        @pl.when(pl.program_id(1) == 0)
        def _():
            xf = x_ref[...]
            r = lax.rsqrt(jnp.mean(xf * xf, axis=-1, keepdims=True) + EPS)
            h = (xf * r * g_ref[...]).astype(BF16)
            hs_ref[...] = h
            h_ref[...] = h

        proj_ref[...] = _dot(hs_ref[...], w_ref[...])

    return _call(
        body, ex, name="proj_fwd", grid=(T // tm, NP // tn),
        in_specs=[pl.BlockSpec((tm, D), lambda i, j: (i, 0)), _full((1, D)), pl.BlockSpec((D, tn), lambda i, j: (0, j))],
        out_specs=[pl.BlockSpec((tm, tn), lambda i, j: (i, j)), pl.BlockSpec((tm, D), lambda i, j: (i, 0))],
        out_shape=[jax.ShapeDtypeStruct((T, NP), F32), jax.ShapeDtypeStruct((T, D), BF16)],
        scratch_shapes=[pltpu.VMEM((tm, D), BF16)], args=(x, norm_g, wp))


def _gmlp_tile():
    return 256


def _gmlp_fwd(proj, ln_g, ln_b, ws, bst):
    T = proj.shape[0]
    tm = min(T, _gmlp_tile())

    def body(u_ref, v_ref, z_ref, lg_ref, lb_ref, ws_ref, bs_ref, ya_ref, vn_s):
        vg = _gelu(v_ref[...])
        mu = jnp.mean(vg, axis=-1, keepdims=True)
        xc = vg - mu
        rstd = lax.rsqrt(jnp.mean(xc * xc, axis=-1, keepdims=True) + EPS)
        vn_s[...] = (xc * rstd * lg_ref[...] + lb_ref[...]).astype(BF16)
        for c in range(tm // CH):
            rs = slice(c * CH, (c + 1) * CH)
            for g in range(NG):
                cs_ = slice(g * 256, (g + 1) * 256)
                sv = _dot(ws_ref[g], vn_s[rs, cs_]) + bs_ref[g]
                z = z_ref[rs, cs_]
                ya_ref[rs, cs_] = (_gelu(u_ref[rs, cs_]) * sv * (z * _sigmoid(z))).astype(BF16)

    blk = lambda off: pl.BlockSpec((tm, D), lambda i: (i, off // D))
    return pl.pallas_call(
        body, name="gmlp_fwd", grid=(T // tm,),
        in_specs=[blk(OFF_U), blk(OFF_V), blk(OFF_ZA), _full((1, D)), _full((1, D)), _full((NG, CH, CH)), _full((NG, CH, 256))],
        out_specs=pl.BlockSpec((tm, D), lambda i: (i, 0)),
        out_shape=jax.ShapeDtypeStruct((T, D), BF16),
        scratch_shapes=[pltpu.VMEM((tm, D), BF16)],
        compiler_params=_cp(("parallel",)),
    )(proj, proj, proj, ln_g, ln_b, ws, bst)


def _gmlp_bwd(proj, dya, ln_g, ln_b, ws, wst, bst, gsel, ex):
    T = proj.shape[0]
    tm = min(T, _gmlp_tile())

    def body(u_ref, v_ref, z_ref, dy_ref, lg_ref, lb_ref, ws_ref, wst_ref, bs_ref, gsel_ref,
             du_ref, dv_ref, dz_ref, dws_ref, dbs_ref, dln_ref, vn_s, dsv_s, dvn_s):
        @pl.when(pl.program_id(0) == 0)
        def _():
            dws_ref[...] = jnp.zeros_like(dws_ref)
            dbs_ref[...] = jnp.zeros_like(dbs_ref)
            dln_ref[...] = jnp.zeros_like(dln_ref)

        vg, dvg_dv = _gelu_and_grad(v_ref[...])
        mu = jnp.mean(vg, axis=-1, keepdims=True)
        xc = vg - mu
        rstd = lax.rsqrt(jnp.mean(xc * xc, axis=-1, keepdims=True) + EPS)
        vhat = xc * rstd
        vn_s[...] = (vhat * lg_ref[...] + lb_ref[...]).astype(BF16)
        ri = lax.broadcasted_iota(jnp.int32, (CH, CH), 0)
        ci = lax.broadcasted_iota(jnp.int32, (CH, CH), 1)
        tril = (ri >= ci).astype(F32)
        for c in range(tm // CH):
            rs = slice(c * CH, (c + 1) * CH)
            for g in range(NG):
                cs_ = slice(g * 256, (g + 1) * 256)
                vn = vn_s[rs, cs_]
                sv = _dot(ws_ref[g], vn) + bs_ref[g]
                z = z_ref[rs, cs_]
                sz = _sigmoid(z)
                ug, dug_du = _gelu_and_grad(u_ref[rs, cs_])
                dy = dy_ref[rs, cs_].astype(F32)
                t = dy * z * sz
                du_ref[rs, cs_] = (t * sv * dug_du).astype(BF16)
                dz_ref[rs, cs_] = (dy * ug * sv * sz * (1.0 + z * (1.0 - sz))).astype(BF16)
                dsv = (t * ug).astype(BF16)
                dsv_s[rs, cs_] = dsv
                dvn_s[rs, cs_] = _dot(wst_ref[g], dsv)
                dws_ref[g] += _dot_nt(dsv, vn) * tril
            dbs_ref[...] += _dot(dsv_s[rs, :], gsel_ref[...])
        dvn = dvn_s[...]
        dln_ref[0:1, :] += jnp.sum(dvn * vhat, axis=0, keepdims=True)
        dln_ref[1:2, :] += jnp.sum(dvn, axis=0, keepdims=True)
        dvh = dvn * lg_ref[...]
        dvg = rstd * (dvh - jnp.mean(dvh, axis=-1, keepdims=True) - vhat * jnp.mean(dvh * vhat, axis=-1, keepdims=True))
        dv_ref[...] = (dvg * dvg_dv).astype(BF16)

    blk = lambda off: pl.BlockSpec((tm, D), lambda i: (i, off // D))
    row = pl.BlockSpec((tm, D), lambda i: (i, 0))
    return _call(
        body, ex, name="gmlp_bwd", grid=(T // tm,),
        in_specs=[blk(OFF_U), blk(OFF_V), blk(OFF_ZA), row, _full((1, D)), _full((1, D)), _full((NG, CH, CH)),
                  _full((NG, CH, CH)), _full((NG, CH, 256)), _full((D, 128))],
        out_specs=[row, row, row, _full((NG, CH, CH)), _full((CH, 128)), _full((8, D))],
        out_shape=[jax.ShapeDtypeStruct((T, D), BF16)] * 3
        + [jax.ShapeDtypeStruct((NG, CH, CH), F32), jax.ShapeDtypeStruct((CH, 128), F32), jax.ShapeDtypeStruct((8, D), F32)],
        scratch_shapes=[pltpu.VMEM((tm, D), BF16), pltpu.VMEM((tm, D), BF16), pltpu.VMEM((tm, D), F32)],
        args=(proj, proj, proj, dya, ln_g, ln_b, ws, wst, bst, gsel))


def _conv_pre(x, xp, cw_ref, cb_ref):
    row = lax.broadcasted_iota(jnp.int32, (CH, 1), 0)
    shifted = [x]
    for j in (1, 2, 3):
        shifted.append(jnp.where(row >= j, pltpu.roll(x, j, 0), pltpu.roll(xp, j, 0)))
    pre = cb_ref[...] + cw_ref[3:4, :] * x
    for j in (1, 2, 3):
        pre = pre + cw_ref[3 - j:4 - j, :] * shifted[j]
    return pre, shifted


def _split_dot(x, w, parts, w_left=False):
    acc, r = None, x
    for k in range(parts):
        hi = r.astype(BF16)
        d = _dot(w, hi) if w_left else _dot(hi, w)
        acc = d if acc is None else acc + d
        if k + 1 < parts:
            r = r - hi.astype(F32)
    return acc


def _chunk_decays(dt, alog_ref, e_ref, cs_s, cst_s, csx_s):
    a = -jnp.exp(alog_ref[...])
    ri = lax.broadcasted_iota(jnp.int32, (CH, CH), 0)
    ci = lax.broadcasted_iota(jnp.int32, (CH, CH), 1)
    tril = ri >= ci
    cs = _split_dot(dt * a, tril.astype(BF16), 3, w_left=True)
    cs_s[...] = cs
    cst_s[...] = cs.T
    csx_s[...] = _split_dot(cs, e_ref[...], 3)
    return a, tril, ri, ci


def _lmat(cst_s, h, tril):
    rowb = jnp.broadcast_to(cst_s[h:h + 1, :], (CH, CH))
    return jnp.exp(jnp.where(tril, rowb.T - rowb, -jnp.inf))


def _head_pair_rows(v, lane):
    return jnp.concatenate([jnp.where(lane < HD, v, 0.0), jnp.where(lane < HD, 0.0, v)], axis=0).astype(BF16)


def _ssd_fwd(proj, cw, cb, dtb, alog, dskx, sg, e128):
    T = proj.shape[0]
    nc = T // CH

    def body(xbc_ref, zb_ref, dt_ref, cw_ref, cb_ref, dtb_ref, alog_ref, dx_ref, sg_ref, e_ref,
             y_ref, yb_ref, hp_ref, xprev_s, h_s, cs_s, cst_s, csx_s, yz_s):
        @pl.when(pl.program_id(0) == 0)
        def _():
            xprev_s[...] = jnp.zeros_like(xprev_s)
            h_s[...] = jnp.zeros_like(h_s)

        x = xbc_ref[...]
        pre, _ = _conv_pre(x, xprev_s[...], cw_ref, cb_ref)
        xprev_s[...] = x
        xc = pre * _sigmoid(pre)
        dt = _softplus(dt_ref[...] + dtb_ref[...])
        a, tril, _, lane = _chunk_decays(dt, alog_ref, e_ref, cs_s, cst_s, csx_s)
        dt_x = _split_dot(dt, e_ref[...], 2)
        cs_last_x = csx_s[CH - 1:CH, :]
        hp_ref[0] = h_s[...]
        for g in range(NG):
            gs = slice(g * 512, (g + 1) * 512)
            bg = xc[:, DI + g * NST:DI + (g + 1) * NST].astype(BF16)
            cg = xc[:, DI + 512 + g * NST:DI + 512 + (g + 1) * NST].astype(BF16)
            cbm = _dot_nt(cg, bg)
            xg = xc[:, gs]
            xdt = xg * dt_x[:, gs]
            hprev = h_s[:, gs]
            csx = csx_s[:, gs]
            yoff = _dot(cg, hprev.astype(BF16)) * jnp.exp(csx)
            st = _dot_tn(bg, (xdt * jnp.exp(cs_last_x[:, gs] - csx)).astype(BF16))
            h_s[:, gs] = jnp.exp(cs_last_x[:, gs]) * hprev + st
            ssq = jnp.zeros((CH, 1), F32)
            for q in range(4):
                h0 = g * 8 + 2 * q
                ps = slice(q * 128, (q + 1) * 128)
                cols = slice(g * 512 + q * 128, g * 512 + (q + 1) * 128)
                m01 = jnp.concatenate([cbm * _lmat(cst_s, h0, tril), cbm * _lmat(cst_s, h0 + 1, tril)], axis=1).astype(BF16)
                yq = _dot(m01, _head_pair_rows(xdt[:, ps], lane)) + yoff[:, ps] + xg[:, ps] * dx_ref[:, cols]
                y_ref[:, cols] = yq
                z = zb_ref[:, cols]
                yz = yq * z * _sigmoid(z)
                yz_s[:, cols] = yz
                ssq = ssq + jnp.sum(yz * yz, axis=1, keepdims=True)
            rg = lax.rsqrt(ssq * (1.0 / 512.0) + EPS)
            yb_ref[:, gs] = (yz_s[:, gs] * rg * sg_ref[:, gs]).astype(BF16)

    return pl.pallas_call(
        body, name="ssd_fwd", grid=(nc,),
        in_specs=[pl.BlockSpec((CH, CD), lambda c: (c, OFF_XBC // CD)), pl.BlockSpec((CH, DI), lambda c: (c, OFF_ZB // DI)),
                  pl.BlockSpec((CH, DTW), lambda c: (c, OFF_DT // DTW)), _full((4, CD)), _full((1, CD)), _full((1, DTW)),
                  _full((1, DTW)), _full((1, DI)), _full((1, DI)), _full((DTW, DI))],
        out_specs=[pl.BlockSpec((CH, DI), lambda c: (c, 0)), pl.BlockSpec((CH, DI), lambda c: (c, 0)),
                   pl.BlockSpec((1, NST, DI), lambda c: (c, 0, 0))],
        out_shape=[jax.ShapeDtypeStruct((T, DI), F32), jax.ShapeDtypeStruct((T, DI), BF16),
                   jax.ShapeDtypeStruct((nc, NST, DI), F32)],
        scratch_shapes=[pltpu.VMEM((CH, CD), F32), pltpu.VMEM((NST, DI), F32), pltpu.VMEM((CH, CH), F32),
                        pltpu.VMEM((CH, CH), F32), pltpu.VMEM((CH, DI), F32), pltpu.VMEM((CH, DI), F32)],
        compiler_params=_cp(("arbitrary",)),
    )(proj, proj, proj, cw, cb, dtb, alog, dskx, sg, e128)


def _ssd_bwd(proj, y, dyb, hprev_all, cw, cb, dtb, alog, dskx, sg, e128, et128, ex):
    T = proj.shape[0]
    nc = T // CH

    def body(xbc_ref, xbcp_ref, zb_ref, dt_ref, y_ref, dyb_ref, hp_ref, cw_ref, cb_ref, dtb_ref, alog_ref, dx_ref, sg_ref,
             e_ref, et_ref, dxbc_ref, dzb_ref, ddt_ref, dcw_ref, dsm_ref, dsg_ref,
             g_s, dpn_s, cs_s, cst_s, csx_s, dy_s, dxdt_s, dxs_s, dsd_s, dxc_s, gh_s):
        i = pl.program_id(0)

        @pl.when(i == 0)
        def _():
            g_s[...] = jnp.zeros_like(g_s)
            dpn_s[...] = jnp.zeros_like(dpn_s)
            dcw_ref[...] = jnp.zeros_like(dcw_ref)
            dsm_ref[...] = jnp.zeros_like(dsm_ref)
            dsg_ref[...] = jnp.zeros_like(dsg_ref)

        x = xbc_ref[...]
        xp = jnp.where(i == nc - 1, 0.0, xbcp_ref[...])
        pre, shifted = _conv_pre(x, xp, cw_ref, cb_ref)
        sp = _sigmoid(pre)
        xc = pre * sp
        dtr = dt_ref[...] + dtb_ref[...]
        dt = _softplus(dtr)
        a, tril, ri, lane = _chunk_decays(dt, alog_ref, e_ref, cs_s, cst_s, csx_s)
        et = et_ref[...]
        dt_x = _split_dot(dt, e_ref[...], 2)
        cs_last_x = csx_s[CH - 1:CH, :]

        for g in range(NG):
            gs = slice(g * 512, (g + 1) * 512)
            z = zb_ref[:, gs]
            sz = _sigmoid(z)
            yv = y_ref[:, gs]
            yz = yv * z * sz
            rg = lax.rsqrt(jnp.mean(yz * yz, axis=-1, keepdims=True) + EPS)
            yn = yz * rg
            dyb = dyb_ref[:, gs].astype(F32)
            dsg_ref[0:1, gs] += jnp.sum(dyb * yn, axis=0, keepdims=True)
            dyn = dyb * sg_ref[:, gs]
            dyz = rg * (dyn - yn * jnp.mean(dyn * yn, axis=-1, keepdims=True))
            dy_s[:, gs] = dyz * z * sz
            dzb_ref[:, gs] = (dyz * yv * sz * (1.0 + z * (1.0 - sz))).astype(BF16)

        rsum = jnp.zeros((CH, DTW), F32)
        csum_t = jnp.zeros((DTW, CH), F32)
        for g in range(NG):
            gs = slice(g * 512, (g + 1) * 512)
            bg = xc[:, DI + g * NST:DI + (g + 1) * NST].astype(BF16)
            cg = xc[:, DI + 512 + g * NST:DI + 512 + (g + 1) * NST].astype(BF16)
            cbm = _dot_nt(cg, bg)
            xdt = xc[:, gs] * dt_x[:, gs]
            hprev = hp_ref[0, :, gs]
            hpb = hprev.astype(BF16)
            gn = g_s[:, gs]
            gnb = gn.astype(BF16)
            dy = dy_s[:, gs]
            csx = csx_s[:, gs]
            ecs = jnp.exp(csx)
            dec = jnp.exp(cs_last_x[:, gs] - csx)
            dye = (dy * ecs).astype(BF16)
            dc = _dot_nt(dye, hpb)
            dprev = _dot_tn(cg, dye)
            dxdt_state = dec * _dot(bg, gnb)
            db = _dot_nt((xdt * dec).astype(BF16), gnb)
            dcb = jnp.zeros((CH, CH), F32)
            for q in range(4):
                h0 = g * 8 + 2 * q
                ps = slice(q * 128, (q + 1) * 128)
                dyp = dy[:, ps]
                l0 = _lmat(cst_s, h0, tril)
                l1 = _lmat(cst_s, h0 + 1, tril)
                m0 = cbm * l0
                m1 = cbm * l1
                dm = _dot_nt(dyp.astype(BF16), _head_pair_rows(xdt[:, ps], lane))
                dm0 = dm[:, :CH]
                dm1 = dm[:, CH:]
                dcb = dcb + dm0 * l0 + dm1 * l1
                for hh, qm in ((h0, dm0 * m0), (h0 + 1, dm1 * m1)):
                    rsum = jnp.where(lane == hh, jnp.sum(qm, axis=1, keepdims=True), rsum)
                    csum_t = jnp.where(ri == hh, jnp.sum(qm, axis=0, keepdims=True), csum_t)
                mst = jnp.concatenate([m0, m1], axis=0).astype(BF16)
                d = _dot_tn(mst, _head_pair_rows(dyp, lane))
                dxdt_s[:, g * 512 + q * 128:g * 512 + (q + 1) * 128] = d + dxdt_state[:, ps]
            yoff = _dot(cg, hpb) * ecs
            dsd_s[:, gs] = xdt * dxdt_state
            dxs_s[:, gs] = dy * yoff
            dcbb = dcb.astype(BF16)
            dxc_s[:, DI + 512 + g * NST:DI + 512 + (g + 1) * NST] = dc + _dot(dcbb, bg)
            dxc_s[:, DI + g * NST:DI + (g + 1) * NST] = db + _dot_tn(dcbb, cg)
            gh_s[:, gs] = jnp.broadcast_to(jnp.sum(gn * hprev, axis=0, keepdims=True), (8, 512))
            g_s[:, gs] = dprev + jnp.exp(cs_last_x[:, gs]) * gn

        xs = xc[:, :DI]
        dy = dy_s[...]
        dxdt = dxdt_s[...]
        cs_last = cs_s[CH - 1:CH, :]
        state_e = _split_dot(dsd_s[...], et, 2)
        dcd = 0.125 * jnp.sum(_split_dot(gh_s[...], et, 2), axis=0, keepdims=True) * jnp.exp(cs_last)
        row = lax.broadcasted_iota(jnp.int32, (CH, 1), 0)
        dcs = rsum - csum_t.T + _split_dot(dxs_s[...], et, 2) - state_e
        dcs = dcs + jnp.where(row == CH - 1, jnp.sum(state_e, axis=0, keepdims=True) + dcd, 0.0)
        dda = _split_dot(dcs, (lane >= ri).astype(BF16), 3, w_left=True)
        ddt = dda * a + _dot((dxdt * xs).astype(BF16), et)
        ddtr = jnp.where(lane < NH, ddt * _sigmoid(dtr), 0.0)
        ddt_ref[...] = ddtr.astype(BF16)
        dsm_ref[0:1, :] += jnp.sum(ddtr, axis=0, keepdims=True)
        dsm_ref[1:2, :] += jnp.sum(dda * dt, axis=0, keepdims=True) * a
        dsm_ref[2:3, :] += jnp.sum(_dot((dy * xs).astype(BF16), et), axis=0, keepdims=True)
        dxc_s[:, :DI] = dxdt * dt_x + dy * dx_ref[...]

        dpre = dxc_s[...] * sp * (1.0 + pre * (1.0 - sp))
        dpn = dpn_s[...]
        dcw_ref[4:5, :] += jnp.sum(dpre, axis=0, keepdims=True)
        dxbc = cw_ref[3:4, :] * dpre
        dcw_ref[3:4, :] += jnp.sum(dpre * shifted[0], axis=0, keepdims=True)
        for j in (1, 2, 3):
            dcw_ref[3 - j:4 - j, :] += jnp.sum(dpre * shifted[j], axis=0, keepdims=True)
            ahead = jnp.where(row < CH - j, pltpu.roll(dpre, CH - j, 0), pltpu.roll(dpn, CH - j, 0))
            dxbc = dxbc + cw_ref[3 - j:4 - j, :] * ahead
        dpn_s[...] = dpre
        dxbc_ref[...] = dxbc.astype(BF16)

    rev = lambda c: nc - 1 - c
    return _call(
        body, ex, name="ssd_bwd", grid=(nc,),
        in_specs=[pl.BlockSpec((CH, CD), lambda c: (rev(c), OFF_XBC // CD)),
                  pl.BlockSpec((CH, CD), lambda c: (jnp.maximum(rev(c) - 1, 0), OFF_XBC // CD)),
                  pl.BlockSpec((CH, DI), lambda c: (rev(c), OFF_ZB // DI)),
                  pl.BlockSpec((CH, DTW), lambda c: (rev(c), OFF_DT // DTW)),
                  pl.BlockSpec((CH, DI), lambda c: (rev(c), 0)), pl.BlockSpec((CH, DI), lambda c: (rev(c), 0)),
                  pl.BlockSpec((1, NST, DI), lambda c: (rev(c), 0, 0)),
                  _full((4, CD)), _full((1, CD)), _full((1, DTW)), _full((1, DTW)), _full((1, DI)), _full((1, DI)),
                  _full((DTW, DI)), _full((DI, DTW))],
        out_specs=[pl.BlockSpec((CH, CD), lambda c: (rev(c), 0)), pl.BlockSpec((CH, DI), lambda c: (rev(c), 0)),
                   pl.BlockSpec((CH, DTW), lambda c: (rev(c), 0)), _full((8, CD)), _full((8, DTW)), _full((8, DI))],
        out_shape=[jax.ShapeDtypeStruct((T, CD), BF16), jax.ShapeDtypeStruct((T, DI), BF16), jax.ShapeDtypeStruct((T, DTW), BF16),
                   jax.ShapeDtypeStruct((8, CD), F32), jax.ShapeDtypeStruct((8, DTW), F32), jax.ShapeDtypeStruct((8, DI), F32)],
        scratch_shapes=[pltpu.VMEM((NST, DI), F32), pltpu.VMEM((CH, CD), F32), pltpu.VMEM((CH, CH), F32), pltpu.VMEM((CH, CH), F32),
                        pltpu.VMEM((CH, DI), F32), pltpu.VMEM((CH, DI), F32), pltpu.VMEM((CH, DI), F32), pltpu.VMEM((CH, DI), F32),
                        pltpu.VMEM((CH, DI), F32), pltpu.VMEM((CH, CD), F32), pltpu.VMEM((8, DI), F32)],
        args=(proj, proj, proj, proj, y, dyb, hprev_all, cw, cb, dtb, alog, dskx, sg, e128, et128))


def _merge_tile():
    return 256


def _merge_fwd(x, ya, yb, proj, p, tgt, w_oa, w_ob, w_out, w_pg, w_ple, w_pgt, ple_g, fin_g):
    T = x.shape[0]
    tm = min(T, _merge_tile())

    def body(x_ref, ya_ref, yb_ref, ga_ref, gb_ref, p_ref, t_ref, woa, wob, wout, wpg, wple, wpgt, pg_ref, fg_ref,
             dx1_ref, dx1b_ref, oa_ref, ob_ref, mg_ref, hp_ref, dpre_ref, dpe_ref, acc_ref):
        @pl.when(pl.program_id(0) == 0)
        def _():
            acc_ref[...] = jnp.zeros_like(acc_ref)

        oa = _dot(ya_ref[...], woa[...])
        ob = _dot(yb_ref[...], wob[...])
        oa_ref[...] = oa.astype(BF16)
        ob_ref[...] = ob.astype(BF16)
        mg = _sigmoid(ga_ref[...]) * oa + _sigmoid(gb_ref[...]) * ob
        mgb = mg.astype(BF16)
        mg_ref[...] = mgb
        x1 = x_ref[...] + _dot(mgb, wout[...])
        r2 = lax.rsqrt(jnp.mean(x1 * x1, axis=-1, keepdims=True) + EPS)
        xh1 = x1 * r2
        hpb = (xh1 * pg_ref[...]).astype(BF16)
        hp_ref[...] = hpb
        gate = _sigmoid(_dot(hpb, wpg[...]))
        pe = _dot(p_ref[...].astype(BF16), wple[...])
        x2 = x1 + gate * pe
        r3 = lax.rsqrt(jnp.mean(x2 * x2, axis=-1, keepdims=True) + EPS)
        xh2 = x2 * r3
        err = xh2 * fg_ref[...] - t_ref[...]
        acc_ref[2:3, :] += 0.5 * jnp.sum(jnp.mean(err * err, axis=-1, keepdims=True))
        dyo = err * (1.0 / D)
        acc_ref[0:1, :] += jnp.sum(dyo * xh2, axis=0, keepdims=True)
        dn = dyo * fg_ref[...]
        dx2 = r3 * (dn - xh2 * jnp.mean(dn * xh2, axis=-1, keepdims=True))
        dpe_ref[...] = (dx2 * gate).astype(BF16)
        dpre = (dx2 * pe * gate * (1.0 - gate)).astype(BF16)
        dpre_ref[...] = dpre
        dhp = _dot(dpre, wpgt[...])
        acc_ref[1:2, :] += jnp.sum(dhp * xh1, axis=0, keepdims=True)
        dhn = dhp * pg_ref[...]
        dx1 = dx2 + r2 * (dhn - xh1 * jnp.mean(dhn * xh1, axis=-1, keepdims=True))
        dx1_ref[...] = dx1
        dx1b_ref[...] = dx1.astype(BF16)

    row = lambda w: pl.BlockSpec((tm, w), lambda i: (i, 0))
    wsp = lambda s: pl.BlockSpec(s, lambda i: (0, 0), pipeline_mode=pl.Buffered(1))
    return pl.pallas_call(
        body, name="merge_fwd", grid=(T // tm,),
        in_specs=[row(D), row(D), row(DI), pl.BlockSpec((tm, D), lambda i: (i, OFF_GA // D)),
                  pl.BlockSpec((tm, D), lambda i: (i, OFF_GB // D)), row(PLE), row(D),
                  wsp((D, D)), wsp((DI, D)), wsp((D, D)), wsp((D, D)), wsp((PLE, D)), wsp((D, D)), _full((1, D)), _full((1, D))],
        out_specs=[row(D)] * 8 + [_full((8, D))],
        out_shape=[jax.ShapeDtypeStruct((T, D), F32)] + [jax.ShapeDtypeStruct((T, D), BF16)] * 7 + [jax.ShapeDtypeStruct((8, D), F32)],
        compiler_params=_cp(("arbitrary",)),
    )(x, ya, yb, proj, proj, p, tgt, w_oa, w_ob, w_out, w_pg, w_ple, w_pgt, ple_g, fin_g)


def _merge_bwd(dx1, oa, ob, proj, w_outt, w_oat, w_obt):
    T = dx1.shape[0]
    tm = min(T, _merge_tile())

    def body(dx1_ref, oa_ref, ob_ref, ga_ref, gb_ref, woutt, woat, wobt, doa_ref, dob_ref, dya_ref, dyb_ref, dg_ref):
        dmg = _dot(dx1_ref[...].astype(BF16), woutt[...])
        sa = _sigmoid(ga_ref[...])
        sb = _sigmoid(gb_ref[...])
        doa = (dmg * sa).astype(BF16)
        dob = (dmg * sb).astype(BF16)
        doa_ref[...] = doa
        dob_ref[...] = dob
        dg_ref[:, :D] = (dmg * oa_ref[...].astype(F32) * sa * (1.0 - sa)).astype(BF16)
        dg_ref[:, D:] = (dmg * ob_ref[...].astype(F32) * sb * (1.0 - sb)).astype(BF16)
        dya_ref[...] = _dot(doa, woat[...]).astype(BF16)
        dyb_ref[...] = _dot(dob, wobt[...]).astype(BF16)

    row = lambda w: pl.BlockSpec((tm, w), lambda i: (i, 0))
    wsp = lambda s: pl.BlockSpec(s, lambda i: (0, 0), pipeline_mode=pl.Buffered(1))
    return pl.pallas_call(
        body, name="merge_bwd", grid=(T // tm,),
        in_specs=[row(D), row(D), row(D), pl.BlockSpec((tm, D), lambda i: (i, OFF_GA // D)),
                  pl.BlockSpec((tm, D), lambda i: (i, OFF_GB // D)), wsp((D, D)), wsp((D, D)), wsp((D, DI))],
        out_specs=[row(D), row(D), row(D), row(DI), row(2 * D)],
        out_shape=[jax.ShapeDtypeStruct((T, D), BF16)] * 3 + [jax.ShapeDtypeStruct((T, DI), BF16), jax.ShapeDtypeStruct((T, 2 * D), BF16)],
        compiler_params=_cp(("parallel",)),
    )(dx1, oa, ob, proj, proj, w_outt, w_oat, w_obt)


def _wgrad(a, b, name):
    T, K = a.shape
    N = b.shape[1]
    tt, tk, tn = min(T, 1024), min(K, 1024), min(N, 1024)
    nt = T // tt

    def body(a_ref, b_ref, o_ref, acc_s):
        t = pl.program_id(2)

        @pl.when(t == 0)
        def _():
            acc_s[...] = jnp.zeros_like(acc_s)

        acc_s[...] += _dot_tn(a_ref[...].astype(BF16), b_ref[...])

        @pl.when(t == nt - 1)
        def _():
            o_ref[...] = acc_s[...].astype(BF16)

    return pl.pallas_call(
        body, name=name, grid=(K // tk, N // tn, nt),
        in_specs=[pl.BlockSpec((tt, tk), lambda k, n, t: (t, k)), pl.BlockSpec((tt, tn), lambda k, n, t: (t, n))],
        out_specs=pl.BlockSpec((tk, tn), lambda k, n, t: (k, n)),
        out_shape=jax.ShapeDtypeStruct((K, N), BF16),
        scratch_shapes=[pltpu.VMEM((tk, tn), F32)],
        compiler_params=_cp(("parallel", "parallel", "arbitrary")),
    )(a, b)


def _proj_bwd(x, dx1, norm_g, wpt, dxbc, du, dzb, dv, dza, dgab, ddt, ex):
    T = x.shape[0]
    tm = min(T, 512)
    nk = OFF_DT // D + 1
    pieces = [(dxbc, OFF_XBC), (du, OFF_U), (dzb, OFF_ZB), (dv, OFF_V), (dza, OFF_ZA), (dgab, OFF_GA)]
    ranges = [(off // D, off // D + arr.shape[1] // D) for arr, off in pieces]

    def body(x_ref, dx1_ref, g_ref, w_ref, wdt_ref, *rest):
        piece_refs, ddt_ref, gx_ref, dng_ref, acc_s = rest[:6], rest[6], rest[7], rest[8], rest[9]
        i, k = pl.program_id(0), pl.program_id(1)

        @pl.when((i == 0) & (k == 0))
        def _():
            dng_ref[...] = jnp.zeros_like(dng_ref)

        @pl.when(k == 0)
        def _():
            acc_s[...] = jnp.zeros_like(acc_s)

        for ref, (lo, hi) in zip(piece_refs, ranges):
            @pl.when((k >= lo) & (k < hi))
            def _(ref=ref):
                acc_s[...] += _dot(ref[...], w_ref[...])

        @pl.when(k == nk - 1)
        def _():
            dh = acc_s[...] + _dot(ddt_ref[...], wdt_ref[...])
            xf = x_ref[...]
            r = lax.rsqrt(jnp.mean(xf * xf, axis=-1, keepdims=True) + EPS)
            xh = xf * r
            dng_ref[0:1, :] += jnp.sum(dh * xh, axis=0, keepdims=True)
            dxn = dh * g_ref[...]
            gx_ref[...] = dx1_ref[...] + r * (dxn - xh * jnp.mean(dxn * xh, axis=-1, keepdims=True))

    def piece_spec(lo, hi):
        return pl.BlockSpec((tm, D), lambda i, k: (i, jnp.clip(k - lo, 0, hi - lo - 1)))

    row = pl.BlockSpec((tm, D), lambda i, k: (i, 0))
    return _call(
        body, ex, name="proj_bwd", grid=(T // tm, nk),
        in_specs=[row, row, _full((1, D)), pl.BlockSpec((D, D), lambda i, k: (jnp.minimum(k, nk - 2), 0)),
                  pl.BlockSpec((DTW, D), lambda i, k: (OFF_DT // DTW, 0))]
        + [piece_spec(lo, hi) for lo, hi in ranges] + [pl.BlockSpec((tm, DTW), lambda i, k: (i, 0))],
        out_specs=[row, _full((8, D))],
        out_shape=[jax.ShapeDtypeStruct((T, D), F32), jax.ShapeDtypeStruct((8, D), F32)],
        scratch_shapes=[pltpu.VMEM((tm, D), F32)],
        args=(x, dx1, norm_g, wpt, wpt, *[arr for arr, _ in pieces], ddt))


def _elementwise_tile(R, C, limit=1 << 20):
    if R * C * 4 <= limit:
        return R, C
    rows = [t for t in range(16, R, 16) if R % t == 0 and t * C * 4 <= limit]
    if rows:
        return rows[-1], C
    cols = [t for t in range(128, C, 128) if C % t == 0 and R * t * 4 <= limit]
    return R, cols[-1]


def _adamw(w, m, v, parts, name):
    R, C = w.shape
    tr, tc = _elementwise_tile(R, C)
    c1 = 1.0 - ADAM_B1 ** ADAM_STEP
    c2 = 1.0 - ADAM_B2 ** ADAM_STEP
    n = len(parts)
    wspec = pl.BlockSpec((tr, tc), lambda i, j: (i, j))
    slot = lambda k: pl.BlockSpec((None, tr, tc), lambda i, j: (k, i, j))
    part_specs = [slot(p[1]) if isinstance(p, tuple) else wspec for p in parts]
    part_arrays = [p[0] if isinstance(p, tuple) else p for p in parts]

    def body(*refs):
        w_ref, m_ref, v_ref = refs[:3]
        g_ref, d_ref, nm_ref, nv_ref = refs[3 + n:]
        g = refs[3][...].astype(F32)
        for r in refs[4:3 + n]:
            g = g + r[...].astype(F32)
        mm = ADAM_B1 * m_ref[...] + (1.0 - ADAM_B1) * g
        vv = ADAM_B2 * v_ref[...] + (1.0 - ADAM_B2) * (g * g)
        g_ref[...] = g
        nm_ref[...] = mm
        nv_ref[...] = vv
        d_ref[...] = -ADAM_LR * ((mm / c1) / (jnp.sqrt(vv / c2) + ADAM_EPS) + ADAM_WD * w_ref[...])

    return pl.pallas_call(
        body, name=name, grid=(R // tr, C // tc), in_specs=[wspec] * 3 + part_specs, out_specs=[wspec] * 4,
        out_shape=[jax.ShapeDtypeStruct(w.shape, F32)] * 4, compiler_params=_cp(("parallel", "parallel")),
    )(w, m, v, *part_arrays)


def _pair_sum(a, b, name):
    Q, R, C = a.shape
    tr, tc = _elementwise_tile(R, C)

    def body(a_ref, b_ref, o_ref):
        o_ref[...] = (a_ref[...].astype(F32) + b_ref[...].astype(F32)).astype(BF16)

    spec = pl.BlockSpec((None, tr, tc), lambda q, i, j: (q, i, j))
    return pl.pallas_call(
        body, name=name, grid=(Q, R // tr, C // tc), in_specs=[spec, spec], out_specs=spec,
        out_shape=jax.ShapeDtypeStruct((Q, R, C), BF16), compiler_params=_cp(("parallel", "parallel", "parallel")),
    )(a, b)


def _dev_index(px, py, pc):
    return 4 * px + 2 * py + pc


def _mesh_position():
    return lax.axis_index("x"), lax.axis_index("y"), lax.axis_index("c")


def _gather_exchange(blocks):
    n = len(blocks)

    def make(ins, outs, sems):
        send_sems, recv_sems, local_sems = sems
        x, y, c = _mesh_position()
        me, sibling = (x, y, c), (x, y, 1 - c)
        chips = [(1 - x, y), (x, 1 - y), (1 - x, 1 - y)]

        def copy(a, k, block, to, src=None):
            dst = outs[a].at[_dev_index(*block)]
            return pltpu.make_async_remote_copy(src_ref=dst if src is None else src, dst_ref=dst, send_sem=send_sems.at[a, k],
                                                recv_sem=recv_sems.at[a, k], device_id=to, device_id_type=MESH)

        mine = [pltpu.make_async_copy(ins[a], outs[a].at[_dev_index(*me)], local_sems.at[a]) for a in range(n)]
        first = []
        for a in range(n):
            first.append(copy(a, 0, me, sibling, src=ins[a]))
            first += [copy(a, 1 + j, me, (*chip, c), src=ins[a]) for j, chip in enumerate(chips)]

        def start():
            for cp in mine + first:
                cp.start()

        def finish():
            passed = []
            for j, chip in enumerate(chips):
                for a in range(n):
                    copy(a, 1 + j, (*chip, c), me).wait_recv()
                    fwd = copy(a, 4 + j, (*chip, c), sibling)
                    fwd.start()
                    passed.append(fwd)
            for a in range(n):
                copy(a, 0, sibling, me).wait_recv()
                for j, chip in enumerate(chips):
                    copy(a, 4 + j, (*chip, 1 - c), me).wait_recv()
            for cp in first + passed:
                cp.wait_send()
            for cp in mine:
                cp.wait()

        return start, finish

    return _Exchange(list(blocks), [jax.ShapeDtypeStruct((NDEV,) + b.shape, b.dtype) for b in blocks],
                     [pltpu.SemaphoreType.DMA((n, 7)), pltpu.SemaphoreType.DMA((n, 7)), pltpu.SemaphoreType.DMA((n,))], make)


def _start_wait_all(copies, local=()):
    def start():
        for cp in list(local) + list(copies):
            cp.start()

    def finish():
        for cp in copies:
            cp.wait()
        for cp in local:
            cp.wait()

    return start, finish


def _sibling_exchange(grads):
    n = len(grads)

    def make(ins, outs, sems):
        send_sems, recv_sems = sems
        x, y, c = _mesh_position()
        return _start_wait_all([pltpu.make_async_remote_copy(
            src_ref=ins[a].at[2 * q + (1 - c)], dst_ref=outs[a].at[q], send_sem=send_sems.at[a, q], recv_sem=recv_sems.at[a, q],
            device_id=(x, y, 1 - c), device_id_type=MESH) for a in range(n) for q in range(4)])

    return _Exchange(list(grads), [jax.ShapeDtypeStruct((4,) + g.shape[1:], g.dtype) for g in grads],
                     [pltpu.SemaphoreType.DMA((n, 4)), pltpu.SemaphoreType.DMA((n, 4))], make)


def _chips_exchange(partials):
    n = len(partials)

    def make(ins, outs, sems):
        send_sems, recv_sems = sems
        x, y, c = _mesh_position()
        chips = [(1 - x, y), (x, 1 - y), (1 - x, 1 - y)]
        return _start_wait_all([pltpu.make_async_remote_copy(
            src_ref=ins[a].at[2 * px + py], dst_ref=outs[a].at[k], send_sem=send_sems.at[a, k], recv_sem=recv_sems.at[a, k],
            device_id=(px, py, c), device_id_type=MESH) for a in range(n) for k, (px, py) in enumerate(chips)])

    return _Exchange(list(partials), [jax.ShapeDtypeStruct((3,) + g.shape[1:], g.dtype) for g in partials],
                     [pltpu.SemaphoreType.DMA((n, 3)), pltpu.SemaphoreType.DMA((n, 3))], make)


def _direct_gather_exchange(small):
    def make(ins, outs, sems):
        send_sems, recv_sems, local_sem = sems
        x, y, c = _mesh_position()
        slot = outs[0].at[_dev_index(x, y, c)]
        copies = []
        for r in range(1, NDEV):
            peer = (x ^ ((r >> 2) & 1), y ^ ((r >> 1) & 1), c ^ (r & 1))
            copies.append(pltpu.make_async_remote_copy(src_ref=ins[0], dst_ref=slot, send_sem=send_sems.at[r - 1],
                                                       recv_sem=recv_sems.at[r - 1], device_id=peer, device_id_type=MESH))
        return _start_wait_all(copies, local=[pltpu.make_async_copy(ins[0], slot, local_sem)])

    return _Exchange([small], [jax.ShapeDtypeStruct((NDEV,) + small.shape, small.dtype)],
                     [pltpu.SemaphoreType.DMA((7,)), pltpu.SemaphoreType.DMA((7,)), pltpu.SemaphoreType.DMA(())], make)


def _w_in_t_to_proj_layout(wt):
    z = jnp.zeros((NP - NIN, wt.shape[1]), wt.dtype)
    return jnp.concatenate([wt[5120:8192], wt[0:1024], wt[3072:5120], wt[1024:2048], wt[2048:3072],
                            wt[8224:9248], wt[9248:10272], wt[8192:8224], z], axis=0)


_SMALL = [("norm_g", 1024), ("ln_a_g", 1024), ("ln_a_b", 1024), ("w_s", 65536), ("b_s", 512), ("conv_w", 12288),
          ("conv_b", 3072), ("dt_bias", 128), ("a_log", 128), ("d_skip", 128), ("ssm_norm_g", 2048), ("ple_norm_g", 1024),
          ("final_g", 1024)]
_SMALL_ROWS = sum(s for _, s in _SMALL) // 128


def _pack_small(d):
    parts = []
    for name, size in _SMALL:
        if name in d:
            f = d[name].reshape(-1).astype(F32)
            parts.append(jnp.pad(f, (0, size - f.shape[0])))
        else:
            parts.append(jnp.zeros((size,), F32))
    return jnp.concatenate(parts).reshape(_SMALL_ROWS, 128)


def _unpack_small(packed, shapes):
    flat = packed.reshape(-1)
    out, off = {}, 0
    for name, size in _SMALL:
        if name in shapes:
            k = math.prod(shapes[name])
            out[name] = flat[off:off + k].reshape(shapes[name])
        off += size
    return out


_WEIGHTS = ["norm_g", "w_in", "ln_a_g", "ln_a_b", "w_s", "b_s", "conv_w", "conv_b", "dt_bias", "a_log", "d_skip", "ssm_norm_g",
            "w_oa", "w_ob", "w_out", "ple_norm_g", "w_pg", "w_ple", "final_g"]
_REPLICATED = ["norm_g", "ln_a_g", "ln_a_b", "w_s", "b_s", "conv_b", "dt_bias", "a_log", "d_skip", "ssm_norm_g", "ple_norm_g", "final_g"]


def _rows_pack(d, dtype):
    return jnp.concatenate([d["w_oa"].reshape(128, D), d["w_ob"].reshape(256, D), d["w_out"].reshape(128, D),
                            d["w_pg"].reshape(128, D), d["w_ple"].reshape(32, D)], axis=0).astype(dtype)


def kernel(x, p, norm_g, w_in, ln_a_g, ln_a_b, w_s, b_s, conv_w, conv_b, dt_bias, a_log, d_skip, ssm_norm_g, w_oa, w_ob, w_out, ple_norm_g, w_pg, w_ple, final_g, loss_target, m_norm_g, m_w_in, m_ln_a_g, m_ln_a_b, m_w_s, m_b_s, m_conv_w, m_conv_b, m_dt_bias, m_a_log, m_d_skip, m_ssm_norm_g, m_w_oa, m_w_ob, m_w_out, m_ple_norm_g, m_w_pg, m_w_ple, m_final_g, v_norm_g, v_w_in, v_ln_a_g, v_ln_a_b, v_w_s, v_b_s, v_conv_w, v_conv_b, v_dt_bias, v_a_log, v_d_skip, v_ssm_norm_g, v_w_oa, v_w_ob, v_w_out, v_ple_norm_g, v_w_pg, v_w_ple, v_final_g):
    args = dict(locals())
    w = {n: args[n] for n in _WEIGHTS}
    m = {n: args["m_" + n] for n in _WEIGHTS}
    v = {n: args["v_" + n] for n in _WEIGHTS}
    T = x.shape[1]
    xi, yi, ci = lax.axis_index("x"), lax.axis_index("y"), lax.axis_index("c")
    me = 4 * xi + 2 * yi + ci

    chip = 2 * xi + yi
    x2, p2, tgt = x.reshape(T, D), p.reshape(T, PLE), loss_target.reshape(T, D)

    norm_g2 = w["norm_g"].reshape(1, D)
    ws = jnp.where(jnp.tril(jnp.ones((CH, CH), bool))[None], w["w_s"].reshape(NG, CH, CH), 0.0).astype(BF16)
    wst = jnp.transpose(ws, (0, 2, 1))
    bst = jnp.broadcast_to(w["b_s"].reshape(NG, CH, 1), (NG, CH, 256))
    ln_g, ln_b = w["ln_a_g"].reshape(1, D), w["ln_a_b"].reshape(1, D)
    cb = w["conv_b"].reshape(1, CD)
    pad32 = lambda a: jnp.pad(a.reshape(1, NH), ((0, 0), (0, DTW - NH)))
    dtb, alog = pad32(w["dt_bias"]), pad32(w["a_log"])
    dskx = jnp.repeat(w["d_skip"].reshape(NH), HD).reshape(1, DI)
    sg = w["ssm_norm_g"].reshape(1, DI)
    ple_g, fin_g = w["ple_norm_g"].reshape(1, D), w["final_g"].reshape(1, D)
    e128 = (jnp.arange(DTW)[:, None] == (jnp.arange(DI)[None, :] // HD)).astype(BF16)
    et128 = e128.T
    gsel = ((jnp.arange(D)[:, None] // 256) == jnp.arange(128)[None, :]).astype(BF16)

    w_in_t = lambda a: jnp.transpose(a.reshape(D, WSH))
    (a_all,) = _run_exchange(_gather_exchange([w_in_t(w["w_in"]).astype(BF16)]), "all_gather_w_in")
    wpt = _w_in_t_to_proj_layout(a_all.reshape(NIN, D))
    wp = wpt.T
    (proj, h), (r_all, cw_all) = _proj_fwd(x2, norm_g2, wp, _gather_exchange([_rows_pack(w, BF16), w["conv_w"].reshape(4, CD // NDEV)]))
    f_oa = r_all[:, R_OA:R_OB].reshape(D, D)
    f_ob = r_all[:, R_OB:R_OUT].reshape(DI, D)
    f_out = r_all[:, R_OUT:R_PG].reshape(D, D)
    f_pg = r_all[:, R_PG:R_PLE].reshape(D, D)
    f_ple = jnp.transpose(r_all[:, R_PLE:R_ROWS].reshape(NDEV, PLE, 128), (1, 0, 2)).reshape(PLE, D)
    cw = jnp.transpose(cw_all, (1, 0, 2)).reshape(4, CD)

    ya = _gmlp_fwd(proj, ln_g, ln_b, ws, bst)
    y, yb, hprev = _ssd_fwd(proj, cw, cb, dtb, alog, dskx, sg, e128)
    dx1, dx1b, oa, ob, mg, hp, dpre, dpe, acc = _merge_fwd(x2, ya, yb, proj, p2, tgt, f_oa, f_ob, f_out, f_pg, f_ple, f_pg.T, ple_g, fin_g)
    doa, dob, dya, dyb, dgab = _merge_bwd(dx1b, oa, ob, proj, f_out.T, f_oa.T, f_ob.T)
    loss = lax.psum(acc[2, 0], ("x", "y", "c"))

    gple = jnp.transpose(_wgrad(p2, dpe, "wgrad_ple").reshape(PLE, NDEV, 128), (1, 0, 2)).reshape(NDEV, 32, D)
    gr = jnp.concatenate([_wgrad(ya, doa, "wgrad_oa").reshape(NDEV, 128, D), _wgrad(yb, dob, "wgrad_ob").reshape(NDEV, 256, D),
                          _wgrad(mg, dx1b, "wgrad_out").reshape(NDEV, 128, D), _wgrad(hp, dpre, "wgrad_pg").reshape(NDEV, 128, D),
                          gple], axis=1)
    (du, dv, dza, dws, dbs, dln), (from_sib_r,) = _gmlp_bwd(proj, dya, ln_g, ln_b, ws, wst, bst, gsel, _sibling_exchange([gr]))
    pr = _pair_sum(lax.dynamic_index_in_dim(gr.reshape(4, 2, R_ROWS, D), ci, 1, keepdims=False), from_sib_r, "pair_sum_rows")
    (dxbc, dzb, ddt, dcw, dsm, dsg), (rr,) = _ssd_bwd(proj, y, dyb, hprev, cw, cb, dtb, alog, dskx, sg, e128, et128, _chips_exchange([pr]))

    g_w_in_t = jnp.concatenate([_wgrad(du, h, "wgrad_u"), _wgrad(dv, h, "wgrad_v"), _wgrad(dza, h, "wgrad_za"),
                                _wgrad(dzb, h, "wgrad_zb"), _wgrad(dxbc, h, "wgrad_xbc"), _wgrad(ddt, h, "wgrad_dt")[:NH],
                                _wgrad(dgab, h, "wgrad_gab")], axis=0)
    ga = g_w_in_t.reshape(NDEV, WSH, D)
    (from_sib_a,) = _run_exchange(_sibling_exchange([ga]), "reduce_scatter_sibling_w_in")
    pa = _pair_sum(lax.dynamic_index_in_dim(ga.reshape(4, 2, WSH, D), ci, 1, keepdims=False), from_sib_a, "pair_sum_w_in")
    (gx, dng), (ra,) = _proj_bwd(x2, dx1, norm_g2, wpt, dxbc, du, dzb, dv, dza, dgab, ddt, _chips_exchange([pa]))
    pa_own = lax.dynamic_index_in_dim(pa, chip, 0, keepdims=False)
    pr_own = lax.dynamic_index_in_dim(pr, chip, 0, keepdims=False)

    g = {"norm_g": dng[0], "ln_a_g": dln[0], "ln_a_b": dln[1], "w_s": dws, "b_s": dbs[:, :NG].T,
         "conv_w": dcw[0:4], "conv_b": dcw[4], "dt_bias": dsm[0, :NH], "a_log": dsm[1, :NH], "d_skip": dsm[2, :NH],
         "ssm_norm_g": dsg[0], "ple_norm_g": acc[1], "final_g": acc[0]}
    (small_all,) = _run_exchange(_direct_gather_exchange(_pack_small(g)), "all_gather_small_grads")

    out_g, out_d, out_m, out_v = {}, {}, {}, {}
    res = _adamw(w_in_t(w["w_in"]), w_in_t(m["w_in"]), w_in_t(v["w_in"]), [pa_own, (ra, 0), (ra, 1), (ra, 2)], "adamw_w_in")
    for dst, val in zip((out_g, out_d, out_m, out_v), res):
        dst["w_in"] = jnp.transpose(val).reshape(1, D, WSH)
    res = _adamw(_rows_pack(w, F32), _rows_pack(m, F32), _rows_pack(v, F32), [pr_own, (rr, 0), (rr, 1), (rr, 2)], "adamw_rows")
    for dst, val in zip((out_g, out_d, out_m, out_v), res):
        dst["w_oa"] = val[R_OA:R_OB].reshape(1, 128, D)
        dst["w_ob"] = val[R_OB:R_OUT].reshape(1, 256, D)
        dst["w_out"] = val[R_OUT:R_PG].reshape(1, 128, D)
        dst["w_pg"] = val[R_PG:R_PLE].reshape(1, 128, D)
        dst["w_ple"] = val[R_PLE:R_ROWS].reshape(1, PLE, 128)
    shapes = {n: w[n].shape for n in _REPLICATED}
    res = _adamw(_pack_small({n: w[n] for n in _REPLICATED}), _pack_small({n: m[n] for n in _REPLICATED}),
                 _pack_small({n: v[n] for n in _REPLICATED}), [(small_all, k) for k in range(NDEV)], "adamw_small")
    for dst, val in zip((out_g, out_d, out_m, out_v), res):
        dst.update(_unpack_small(val, shapes))
    g_cw_full = _unpack_small(res[0], {"conv_w": (4, CD)})["conv_w"]
    g_cw = lax.dynamic_slice_in_dim(g_cw_full, me * (CD // NDEV), CD // NDEV, axis=1).reshape(12, 128)
    res = _adamw(w["conv_w"].reshape(12, 128), m["conv_w"].reshape(12, 128), v["conv_w"].reshape(12, 128), [g_cw], "adamw_conv_w")
    for dst, val in zip((out_g, out_d, out_m, out_v), res):
        dst["conv_w"] = val.reshape(1, 4, CD // NDEV)

    return (loss, gx.reshape(1, T, D), *[out_g[n] for n in _WEIGHTS], *[out_d[n] for n in _WEIGHTS],
            *[out_m[n] for n in _WEIGHTS], *[out_v[n] for n in _WEIGHTS])
```

```python
import functools
import math
from typing import Callable, NamedTuple

import jax
import jax.numpy as jnp
from jax import lax
from jax.experimental import pallas as pl
from jax.experimental.pallas import tpu as pltpu

F32 = jnp.float32
BF16 = jnp.bfloat16
HI = lax.Precision.HIGHEST
MESH = pl.DeviceIdType.MESH

D = 1024
DI = 2048
CD = 3072
NH = 32
HD = 64
NST = 128
NG = 4
CH = 128
PLE = 256
NIN = 10272
NDEV = 8
WSH = NIN // NDEV
EPS = 1e-6
OFF_XBC, OFF_U, OFF_ZB, OFF_V, OFF_ZA, OFF_GA, OFF_GB, OFF_DT = 0, 3072, 4096, 6144, 7168, 8192, 9216, 10240
NP = 10368
DTW = 128
R_OA, R_OB, R_OUT, R_PG, R_PLE, R_ROWS = 0, 128, 384, 512, 640, 672

ADAM_LR, ADAM_B1, ADAM_B2, ADAM_EPS, ADAM_WD, ADAM_STEP = 0.001, 0.9, 0.999, 1e-08, 0.01, 10

V7X_VMEM_LIMIT = 56 * 1024 * 1024


def _cp(sem=None):
    return pltpu.CompilerParams(dimension_semantics=sem, vmem_limit_bytes=V7X_VMEM_LIMIT)


def _dot(a, b, prec=None):
    return jnp.dot(a, b, preferred_element_type=F32, precision=prec)


def _dot_nt(a, b, prec=None):
    return lax.dot_general(a, b, (((1,), (1,)), ((), ())), preferred_element_type=F32, precision=prec)


def _dot_tn(a, b, prec=None):
    return lax.dot_general(a, b, (((0,), (0,)), ((), ())), preferred_element_type=F32, precision=prec)


def _sigmoid(x):
    return 1.0 / (1.0 + jnp.exp(-x))


def _gelu_and_grad(x):
    c = math.sqrt(2.0 / math.pi)
    x2 = x * x
    t = jnp.tanh(c * (x + 0.044715 * x * x2))
    g = 0.5 * x * (1.0 + t)
    dg = 0.5 * (1.0 + t) + 0.5 * x * (1.0 - t * t) * c * (1.0 + 3.0 * 0.044715 * x2)
    return g, dg


def _gelu(x):
    c = math.sqrt(2.0 / math.pi)
    return 0.5 * x * (1.0 + jnp.tanh(c * (x + 0.044715 * x * x * x)))


def _softplus(x):
    return jnp.maximum(x, 0.0) + jnp.log(1.0 + jnp.exp(-jnp.abs(x)))


def _full(shape):
    n = len(shape)
    return pl.BlockSpec(shape, lambda *_: (0,) * n)


_ANY = pl.BlockSpec(memory_space=pl.ANY)


class _Exchange(NamedTuple):
    arrays: list
    out_shape: list
    sems: list
    make: Callable


def _call(body, ex, *, name, grid, in_specs, out_specs, out_shape, scratch_shapes, args):
    ki, ko, ks = len(in_specs), len(out_specs), len(scratch_shapes)
    ei, eo = len(ex.arrays), len(ex.out_shape)
    last = [g - 1 for g in grid]

    def full_body(*refs):
        r = list(refs)
        ins, eins, r = r[:ki], r[ki:ki + ei], r[ki + ei:]
        outs, eouts, r = r[:ko], r[ko:ko + eo], r[ko + eo:]
        scr, esems = r[:ks], r[ks:]
        start, finish = ex.make(eins, eouts, esems)
        ids = [pl.program_id(a) for a in range(len(grid))]
        is_first = functools.reduce(lambda p, q: p & q, [i == 0 for i in ids])
        is_last = functools.reduce(lambda p, q: p & q, [i == l for i, l in zip(ids, last)])
        pl.when(is_first)(start)
        body(*ins, *outs, *scr)
        pl.when(is_last)(finish)

    res = pl.pallas_call(
        full_body, name=name, grid=grid, in_specs=list(in_specs) + [_ANY] * ei, out_specs=list(out_specs) + [_ANY] * eo,
        out_shape=list(out_shape) + list(ex.out_shape), scratch_shapes=list(scratch_shapes) + list(ex.sems),
        compiler_params=_cp(("arbitrary",) * len(grid)),
    )(*args, *ex.arrays)
    return res[:ko], res[ko:]


def _run_exchange(ex, name):
    ni, no = len(ex.arrays), len(ex.out_shape)

    def body(*refs):
        start, finish = ex.make(refs[:ni], refs[ni:ni + no], refs[ni + no:])
        start()
        finish()

    return pl.pallas_call(body, name=name, in_specs=[_ANY] * ni, out_specs=[_ANY] * no, out_shape=list(ex.out_shape),
                          scratch_shapes=list(ex.sems))(*ex.arrays)


def _proj_fwd(x, norm_g, wp, ex):
    T = x.shape[0]
    tm, tn = min(T, 1024), 1152

    def body(x_ref, g_ref, w_ref, proj_ref, h_ref, hs_ref):
        @pl.when(pl.program_id(1) == 0)
        def _():
            xf = x_ref[...]
            r = lax.rsqrt(jnp.mean(xf * xf, axis=-1, keepdims=True) + EPS)
            h = (xf * r * g_ref[...]).astype(BF16)
            hs_ref[...] = h
            h_ref[...] = h

        proj_ref[...] = _dot(hs_ref[...], w_ref[...])

    return _call(
        body, ex, name="proj_fwd", grid=(T // tm, NP // tn),
        in_specs=[pl.BlockSpec((tm, D), lambda i, j: (i, 0)), _full((1, D)), pl.BlockSpec((D, tn), lambda i, j: (0, j))],
        out_specs=[pl.BlockSpec((tm, tn), lambda i, j: (i, j)), pl.BlockSpec((tm, D), lambda i, j: (i, 0))],
        out_shape=[jax.ShapeDtypeStruct((T, NP), F32), jax.ShapeDtypeStruct((T, D), BF16)],
        scratch_shapes=[pltpu.VMEM((tm, D), BF16)], args=(x, norm_g, wp))


def _gmlp_tile():
    return 256


def _gmlp_fwd(proj, ln_g, ln_b, ws, bst):
    T = proj.shape[0]
    tm = min(T, _gmlp_tile())

    def body(u_ref, v_ref, z_ref, lg_ref, lb_ref, ws_ref, bs_ref, ya_ref, vn_s):
        vg = _gelu(v_ref[...])
        mu = jnp.mean(vg, axis=-1, keepdims=True)
        xc = vg - mu
        rstd = lax.rsqrt(jnp.mean(xc * xc, axis=-1, keepdims=True) + EPS)
        vn_s[...] = (xc * rstd * lg_ref[...] + lb_ref[...]).astype(BF16)
        for c in range(tm // CH):
            rs = slice(c * CH, (c + 1) * CH)
            for g in range(NG):
                cs_ = slice(g * 256, (g + 1) * 256)
                sv = _dot(ws_ref[g], vn_s[rs, cs_]) + bs_ref[g]
                z = z_ref[rs, cs_]
                ya_ref[rs, cs_] = (_gelu(u_ref[rs, cs_]) * sv * (z * _sigmoid(z))).astype(BF16)

    blk = lambda off: pl.BlockSpec((tm, D), lambda i: (i, off // D))
    return pl.pallas_call(
        body, name="gmlp_fwd", grid=(T // tm,),
        in_specs=[blk(OFF_U), blk(OFF_V), blk(OFF_ZA), _full((1, D)), _full((1, D)), _full((NG, CH, CH)), _full((NG, CH, 256))],
        out_specs=pl.BlockSpec((tm, D), lambda i: (i, 0)),
        out_shape=jax.ShapeDtypeStruct((T, D), BF16),
        scratch_shapes=[pltpu.VMEM((tm, D), BF16)],
        compiler_params=_cp(("parallel",)),
    )(proj, proj, proj, ln_g, ln_b, ws, bst)


def _gmlp_bwd(proj, dya, ln_g, ln_b, ws, wst, bst, gsel, ex):
    T = proj.shape[0]
    tm = min(T, _gmlp_tile())

    def body(u_ref, v_ref, z_ref, dy_ref, lg_ref, lb_ref, ws_ref, wst_ref, bs_ref, gsel_ref,
             d_ref, dws_ref, dbs_ref, dln_ref, vn_s, dsv_s, dvn_s):
        du_ref, dv_ref, dz_ref = d_ref.at[:, 0:D], d_ref.at[:, D:2 * D], d_ref.at[:, 2 * D:3 * D]
        @pl.when(pl.program_id(0) == 0)
        def _():
            dws_ref[...] = jnp.zeros_like(dws_ref)
            dbs_ref[...] = jnp.zeros_like(dbs_ref)
            dln_ref[...] = jnp.zeros_like(dln_ref)

        vg, dvg_dv = _gelu_and_grad(v_ref[...])
        mu = jnp.mean(vg, axis=-1, keepdims=True)
        xc = vg - mu
        rstd = lax.rsqrt(jnp.mean(xc * xc, axis=-1, keepdims=True) + EPS)
        vhat = xc * rstd
        vn_s[...] = (vhat * lg_ref[...] + lb_ref[...]).astype(BF16)
        ri = lax.broadcasted_iota(jnp.int32, (CH, CH), 0)
        ci = lax.broadcasted_iota(jnp.int32, (CH, CH), 1)
        tril = (ri >= ci).astype(F32)
        for c in range(tm // CH):
            rs = slice(c * CH, (c + 1) * CH)
            for g in range(NG):
                cs_ = slice(g * 256, (g + 1) * 256)
                vn = vn_s[rs, cs_]
                sv = _dot(ws_ref[g], vn) + bs_ref[g]
                z = z_ref[rs, cs_]
                sz = _sigmoid(z)
                ug, dug_du = _gelu_and_grad(u_ref[rs, cs_])
                dy = dy_ref[rs, cs_].astype(F32)
                t = dy * z * sz
                du_ref[rs, cs_] = (t * sv * dug_du).astype(BF16)
                dz_ref[rs, cs_] = (dy * ug * sv * sz * (1.0 + z * (1.0 - sz))).astype(BF16)
                dsv = (t * ug).astype(BF16)
                dsv_s[rs, cs_] = dsv
                dvn_s[rs, cs_] = _dot(wst_ref[g], dsv)
                dws_ref[g] += _dot_nt(dsv, vn) * tril
            dbs_ref[...] += _dot(dsv_s[rs, :], gsel_ref[...])
        dvn = dvn_s[...]
        dln_ref[0:1, :] += jnp.sum(dvn * vhat, axis=0, keepdims=True)
        dln_ref[1:2, :] += jnp.sum(dvn, axis=0, keepdims=True)
        dvh = dvn * lg_ref[...]
        dvg = rstd * (dvh - jnp.mean(dvh, axis=-1, keepdims=True) - vhat * jnp.mean(dvh * vhat, axis=-1, keepdims=True))
        dv_ref[...] = (dvg * dvg_dv).astype(BF16)

    blk = lambda off: pl.BlockSpec((tm, D), lambda i: (i, off // D))
    row = pl.BlockSpec((tm, D), lambda i: (i, 0))
    return _call(
        body, ex, name="gmlp_bwd", grid=(T // tm,),
        in_specs=[blk(OFF_U), blk(OFF_V), blk(OFF_ZA), row, _full((1, D)), _full((1, D)), _full((NG, CH, CH)),
                  _full((NG, CH, CH)), _full((NG, CH, 256)), _full((D, 128))],
        out_specs=[pl.BlockSpec((tm, 3 * D), lambda i: (i, 0)), _full((NG, CH, CH)), _full((CH, 128)), _full((8, D))],
        out_shape=[jax.ShapeDtypeStruct((T, 3 * D), BF16),
                   jax.ShapeDtypeStruct((NG, CH, CH), F32), jax.ShapeDtypeStruct((CH, 128), F32), jax.ShapeDtypeStruct((8, D), F32)],
        scratch_shapes=[pltpu.VMEM((tm, D), BF16), pltpu.VMEM((tm, D), BF16), pltpu.VMEM((tm, D), F32)],
        args=(proj, proj, proj, dya, ln_g, ln_b, ws, wst, bst, gsel))


def _conv_pre(x, xp, cw_ref, cb_ref):
    row = lax.broadcasted_iota(jnp.int32, (CH, 1), 0)
    shifted = [x]
    for j in (1, 2, 3):
        shifted.append(jnp.where(row >= j, pltpu.roll(x, j, 0), pltpu.roll(xp, j, 0)))
    pre = cb_ref[...] + cw_ref[3:4, :] * x
    for j in (1, 2, 3):
        pre = pre + cw_ref[3 - j:4 - j, :] * shifted[j]
    return pre, shifted


def _split_dot(x, w, parts, w_left=False):
    acc, r = None, x
    for k in range(parts):
        hi = r.astype(BF16)
        d = _dot(w, hi) if w_left else _dot(hi, w)
        acc = d if acc is None else acc + d
        if k + 1 < parts:
            r = r - hi.astype(F32)
    return acc


def _chunk_decays(dt, alog_ref, e_ref, cs_s, cst_s, csx_s):
    a = -jnp.exp(alog_ref[...])
    ri = lax.broadcasted_iota(jnp.int32, (CH, CH), 0)
    ci = lax.broadcasted_iota(jnp.int32, (CH, CH), 1)
    tril = ri >= ci
    cs = _split_dot(dt * a, tril.astype(BF16), 3, w_left=True)
    cs_s[...] = cs
    cst_s[...] = cs.T
    csx_s[...] = _split_dot(cs, e_ref[...], 3)
    return a, tril, ri, ci


def _lmat(cst_s, h, tril):
    rowb = jnp.broadcast_to(cst_s[h:h + 1, :], (CH, CH))
    return jnp.exp(jnp.where(tril, rowb.T - rowb, -jnp.inf))


def _head_pair_rows(v, lane):
    return jnp.concatenate([jnp.where(lane < HD, v, 0.0), jnp.where(lane < HD, 0.0, v)], axis=0).astype(BF16)


def _ssd_fwd(proj, cw, cb, dtb, alog, dskx, sg, e128):
    T = proj.shape[0]
    nc = T // CH

    def body(xbc_ref, zb_ref, dt_ref, cw_ref, cb_ref, dtb_ref, alog_ref, dx_ref, sg_ref, e_ref,
             y_ref, yb_ref, hp_ref, xprev_s, h_s, cs_s, cst_s, csx_s, yz_s):
        @pl.when(pl.program_id(0) == 0)
        def _():
            xprev_s[...] = jnp.zeros_like(xprev_s)
            h_s[...] = jnp.zeros_like(h_s)

        x = xbc_ref[...]
        pre, _ = _conv_pre(x, xprev_s[...], cw_ref, cb_ref)
        xprev_s[...] = x
        xc = pre * _sigmoid(pre)
        dt = _softplus(dt_ref[...] + dtb_ref[...])
        a, tril, _, lane = _chunk_decays(dt, alog_ref, e_ref, cs_s, cst_s, csx_s)
        dt_x = _split_dot(dt, e_ref[...], 2)
        cs_last_x = csx_s[CH - 1:CH, :]
        hp_ref[0] = h_s[...]
        for g in range(NG):
            gs = slice(g * 512, (g + 1) * 512)
            bg = xc[:, DI + g * NST:DI + (g + 1) * NST].astype(BF16)
            cg = xc[:, DI + 512 + g * NST:DI + 512 + (g + 1) * NST].astype(BF16)
            cbm = _dot_nt(cg, bg)
            xg = xc[:, gs]
            xdt = xg * dt_x[:, gs]
            hprev = h_s[:, gs]
            csx = csx_s[:, gs]
            yoff = _dot(cg, hprev.astype(BF16)) * jnp.exp(csx)
            st = _dot_tn(bg, (xdt * jnp.exp(cs_last_x[:, gs] - csx)).astype(BF16))
            h_s[:, gs] = jnp.exp(cs_last_x[:, gs]) * hprev + st
            ssq = jnp.zeros((CH, 1), F32)
            for q in range(4):
                h0 = g * 8 + 2 * q
                ps = slice(q * 128, (q + 1) * 128)
                cols = slice(g * 512 + q * 128, g * 512 + (q + 1) * 128)
                m01 = jnp.concatenate([cbm * _lmat(cst_s, h0, tril), cbm * _lmat(cst_s, h0 + 1, tril)], axis=1).astype(BF16)
                yq = _dot(m01, _head_pair_rows(xdt[:, ps], lane)) + yoff[:, ps] + xg[:, ps] * dx_ref[:, cols]
                y_ref[:, cols] = yq
                z = zb_ref[:, cols]
                yz = yq * z * _sigmoid(z)
                yz_s[:, cols] = yz
                ssq = ssq + jnp.sum(yz * yz, axis=1, keepdims=True)
            rg = lax.rsqrt(ssq * (1.0 / 512.0) + EPS)
            yb_ref[:, gs] = (yz_s[:, gs] * rg * sg_ref[:, gs]).astype(BF16)

    return pl.pallas_call(
        body, name="ssd_fwd", grid=(nc,),
        in_specs=[pl.BlockSpec((CH, CD), lambda c: (c, OFF_XBC // CD)), pl.BlockSpec((CH, DI), lambda c: (c, OFF_ZB // DI)),
                  pl.BlockSpec((CH, DTW), lambda c: (c, OFF_DT // DTW)), _full((4, CD)), _full((1, CD)), _full((1, DTW)),
                  _full((1, DTW)), _full((1, DI)), _full((1, DI)), _full((DTW, DI))],
        out_specs=[pl.BlockSpec((CH, DI), lambda c: (c, 0)), pl.BlockSpec((CH, DI), lambda c: (c, 0)),
                   pl.BlockSpec((1, NST, DI), lambda c: (c, 0, 0))],
        out_shape=[jax.ShapeDtypeStruct((T, DI), F32), jax.ShapeDtypeStruct((T, DI), BF16),
                   jax.ShapeDtypeStruct((nc, NST, DI), F32)],
        scratch_shapes=[pltpu.VMEM((CH, CD), F32), pltpu.VMEM((NST, DI), F32), pltpu.VMEM((CH, CH), F32),
                        pltpu.VMEM((CH, CH), F32), pltpu.VMEM((CH, DI), F32), pltpu.VMEM((CH, DI), F32)],
        compiler_params=_cp(("arbitrary",)),
    )(proj, proj, proj, cw, cb, dtb, alog, dskx, sg, e128)


def _ssd_bwd(proj, y, dyb, hprev_all, cw, cb, dtb, alog, dskx, sg, e128, et128, ex):
    T = proj.shape[0]
    nc = T // CH

    def body(xbc_ref, xbcp_ref, zb_ref, dt_ref, y_ref, dyb_ref, hp_ref, cw_ref, cb_ref, dtb_ref, alog_ref, dx_ref, sg_ref,
             e_ref, et_ref, d_ref, ddt_ref, dcw_ref, dsm_ref, dsg_ref,
             g_s, dpn_s, cs_s, cst_s, csx_s, dy_s, dxdt_s, dxs_s, dsd_s, dxc_s, gh_s):
        dxbc_ref, dzb_ref = d_ref.at[:, 0:CD], d_ref.at[:, CD:CD + DI]
        i = pl.program_id(0)

        @pl.when(i == 0)
        def _():
            g_s[...] = jnp.zeros_like(g_s)
            dpn_s[...] = jnp.zeros_like(dpn_s)
            dcw_ref[...] = jnp.zeros_like(dcw_ref)
            dsm_ref[...] = jnp.zeros_like(dsm_ref)
            dsg_ref[...] = jnp.zeros_like(dsg_ref)

        x = xbc_ref[...]
        xp = jnp.where(i == nc - 1, 0.0, xbcp_ref[...])
        pre, shifted = _conv_pre(x, xp, cw_ref, cb_ref)
        sp = _sigmoid(pre)
        xc = pre * sp
        dtr = dt_ref[...] + dtb_ref[...]
        dt = _softplus(dtr)
        a, tril, ri, lane = _chunk_decays(dt, alog_ref, e_ref, cs_s, cst_s, csx_s)
        et = et_ref[...]
        dt_x = _split_dot(dt, e_ref[...], 2)
        cs_last_x = csx_s[CH - 1:CH, :]

        for g in range(NG):
            gs = slice(g * 512, (g + 1) * 512)
            z = zb_ref[:, gs]
            sz = _sigmoid(z)
            yv = y_ref[:, gs]
            yz = yv * z * sz
            rg = lax.rsqrt(jnp.mean(yz * yz, axis=-1, keepdims=True) + EPS)
            yn = yz * rg
            dyb = dyb_ref[:, gs].astype(F32)
            dsg_ref[0:1, gs] += jnp.sum(dyb * yn, axis=0, keepdims=True)
            dyn = dyb * sg_ref[:, gs]
            dyz = rg * (dyn - yn * jnp.mean(dyn * yn, axis=-1, keepdims=True))
            dy_s[:, gs] = dyz * z * sz
            dzb_ref[:, gs] = (dyz * yv * sz * (1.0 + z * (1.0 - sz))).astype(BF16)

        rsum = jnp.zeros((CH, DTW), F32)
        csum_t = jnp.zeros((DTW, CH), F32)
        for g in range(NG):
            gs = slice(g * 512, (g + 1) * 512)
            bg = xc[:, DI + g * NST:DI + (g + 1) * NST].astype(BF16)
            cg = xc[:, DI + 512 + g * NST:DI + 512 + (g + 1) * NST].astype(BF16)
            cbm = _dot_nt(cg, bg)
            xdt = xc[:, gs] * dt_x[:, gs]
            hprev = hp_ref[0, :, gs]
            hpb = hprev.astype(BF16)
            gn = g_s[:, gs]
            gnb = gn.astype(BF16)
            dy = dy_s[:, gs]
            csx = csx_s[:, gs]
            ecs = jnp.exp(csx)
            dec = jnp.exp(cs_last_x[:, gs] - csx)
            dye = (dy * ecs).astype(BF16)
            dc = _dot_nt(dye, hpb)
            dprev = _dot_tn(cg, dye)
            dxdt_state = dec * _dot(bg, gnb)
            db = _dot_nt((xdt * dec).astype(BF16), gnb)
            dcb = jnp.zeros((CH, CH), F32)
            for q in range(4):
                h0 = g * 8 + 2 * q
                ps = slice(q * 128, (q + 1) * 128)
                dyp = dy[:, ps]
                l0 = _lmat(cst_s, h0, tril)
                l1 = _lmat(cst_s, h0 + 1, tril)
                m0 = cbm * l0
                m1 = cbm * l1
                dm = _dot_nt(dyp.astype(BF16), _head_pair_rows(xdt[:, ps], lane))
                dm0 = dm[:, :CH]
                dm1 = dm[:, CH:]
                dcb = dcb + dm0 * l0 + dm1 * l1
                for hh, qm in ((h0, dm0 * m0), (h0 + 1, dm1 * m1)):
                    rsum = jnp.where(lane == hh, jnp.sum(qm, axis=1, keepdims=True), rsum)
                    csum_t = jnp.where(ri == hh, jnp.sum(qm, axis=0, keepdims=True), csum_t)
                mst = jnp.concatenate([m0, m1], axis=0).astype(BF16)
                d = _dot_tn(mst, _head_pair_rows(dyp, lane))
                dxdt_s[:, g * 512 + q * 128:g * 512 + (q + 1) * 128] = d + dxdt_state[:, ps]
            yoff = _dot(cg, hpb) * ecs
            dsd_s[:, gs] = xdt * dxdt_state
            dxs_s[:, gs] = dy * yoff
            dcbb = dcb.astype(BF16)
            dxc_s[:, DI + 512 + g * NST:DI + 512 + (g + 1) * NST] = dc + _dot(dcbb, bg)
            dxc_s[:, DI + g * NST:DI + (g + 1) * NST] = db + _dot_tn(dcbb, cg)
            gh_s[:, gs] = jnp.broadcast_to(jnp.sum(gn * hprev, axis=0, keepdims=True), (8, 512))
            g_s[:, gs] = dprev + jnp.exp(cs_last_x[:, gs]) * gn

        xs = xc[:, :DI]
        dy = dy_s[...]
        dxdt = dxdt_s[...]
        cs_last = cs_s[CH - 1:CH, :]
        state_e = _split_dot(dsd_s[...], et, 2)
        dcd = 0.125 * jnp.sum(_split_dot(gh_s[...], et, 2), axis=0, keepdims=True) * jnp.exp(cs_last)
        row = lax.broadcasted_iota(jnp.int32, (CH, 1), 0)
        dcs = rsum - csum_t.T + _split_dot(dxs_s[...], et, 2) - state_e
        dcs = dcs + jnp.where(row == CH - 1, jnp.sum(state_e, axis=0, keepdims=True) + dcd, 0.0)
        dda = _split_dot(dcs, (lane >= ri).astype(BF16), 3, w_left=True)
        ddt = dda * a + _dot((dxdt * xs).astype(BF16), et)
        ddtr = jnp.where(lane < NH, ddt * _sigmoid(dtr), 0.0)
        ddt_ref[...] = ddtr.astype(BF16)
        dsm_ref[0:1, :] += jnp.sum(ddtr, axis=0, keepdims=True)
        dsm_ref[1:2, :] += jnp.sum(dda * dt, axis=0, keepdims=True) * a
        dsm_ref[2:3, :] += jnp.sum(_dot((dy * xs).astype(BF16), et), axis=0, keepdims=True)
        dxc_s[:, :DI] = dxdt * dt_x + dy * dx_ref[...]

        dpre = dxc_s[...] * sp * (1.0 + pre * (1.0 - sp))
        dpn = dpn_s[...]
        dcw_ref[4:5, :] += jnp.sum(dpre, axis=0, keepdims=True)
        dxbc = cw_ref[3:4, :] * dpre
        dcw_ref[3:4, :] += jnp.sum(dpre * shifted[0], axis=0, keepdims=True)
        for j in (1, 2, 3):
            dcw_ref[3 - j:4 - j, :] += jnp.sum(dpre * shifted[j], axis=0, keepdims=True)
            ahead = jnp.where(row < CH - j, pltpu.roll(dpre, CH - j, 0), pltpu.roll(dpn, CH - j, 0))
            dxbc = dxbc + cw_ref[3 - j:4 - j, :] * ahead
        dpn_s[...] = dpre
        dxbc_ref[...] = dxbc.astype(BF16)

    rev = lambda c: nc - 1 - c
    return _call(
        body, ex, name="ssd_bwd", grid=(nc,),
        in_specs=[pl.BlockSpec((CH, CD), lambda c: (rev(c), OFF_XBC // CD)),
                  pl.BlockSpec((CH, CD), lambda c: (jnp.maximum(rev(c) - 1, 0), OFF_XBC // CD)),
                  pl.BlockSpec((CH, DI), lambda c: (rev(c), OFF_ZB // DI)),
                  pl.BlockSpec((CH, DTW), lambda c: (rev(c), OFF_DT // DTW)),
                  pl.BlockSpec((CH, DI), lambda c: (rev(c), 0)), pl.BlockSpec((CH, DI), lambda c: (rev(c), 0)),
                  pl.BlockSpec((1, NST, DI), lambda c: (rev(c), 0, 0)),
                  _full((4, CD)), _full((1, CD)), _full((1, DTW)), _full((1, DTW)), _full((1, DI)), _full((1, DI)),
                  _full((DTW, DI)), _full((DI, DTW))],
        out_specs=[pl.BlockSpec((CH, CD + DI), lambda c: (rev(c), 0)),
                   pl.BlockSpec((CH, DTW), lambda c: (rev(c), 0)), _full((8, CD)), _full((8, DTW)), _full((8, DI))],
        out_shape=[jax.ShapeDtypeStruct((T, CD + DI), BF16), jax.ShapeDtypeStruct((T, DTW), BF16),
                   jax.ShapeDtypeStruct((8, CD), F32), jax.ShapeDtypeStruct((8, DTW), F32), jax.ShapeDtypeStruct((8, DI), F32)],
        scratch_shapes=[pltpu.VMEM((NST, DI), F32), pltpu.VMEM((CH, CD), F32), pltpu.VMEM((CH, CH), F32), pltpu.VMEM((CH, CH), F32),
                        pltpu.VMEM((CH, DI), F32), pltpu.VMEM((CH, DI), F32), pltpu.VMEM((CH, DI), F32), pltpu.VMEM((CH, DI), F32),
                        pltpu.VMEM((CH, DI), F32), pltpu.VMEM((CH, CD), F32), pltpu.VMEM((8, DI), F32)],
        args=(proj, proj, proj, proj, y, dyb, hprev_all, cw, cb, dtb, alog, dskx, sg, e128, et128))


def _merge_tile():
    return 256


def _merge_fwd(x, ya, yb, proj, p, tgt, w_oa, w_ob, w_out, w_pg, w_ple, w_pgt, ple_g, fin_g):
    T = x.shape[0]
    tm = min(T, _merge_tile())

    def body(x_ref, ya_ref, yb_ref, ga_ref, gb_ref, p_ref, t_ref, woa, wob, wout, wpg, wple, wpgt, pg_ref, fg_ref,
             dx1_ref, dx1b_ref, oa_ref, ob_ref, mg_ref, hp_ref, dpre_ref, dpe_ref, acc_ref):
        @pl.when(pl.program_id(0) == 0)
        def _():
            acc_ref[...] = jnp.zeros_like(acc_ref)

        oa = _dot(ya_ref[...], woa[...])
        ob = _dot(yb_ref[...], wob[...])
        oa_ref[...] = oa.astype(BF16)
        ob_ref[...] = ob.astype(BF16)
        mg = _sigmoid(ga_ref[...]) * oa + _sigmoid(gb_ref[...]) * ob
        mgb = mg.astype(BF16)
        mg_ref[...] = mgb
        x1 = x_ref[...] + _dot(mgb, wout[...])
        r2 = lax.rsqrt(jnp.mean(x1 * x1, axis=-1, keepdims=True) + EPS)
        xh1 = x1 * r2
        hpb = (xh1 * pg_ref[...]).astype(BF16)
        hp_ref[...] = hpb
        gate = _sigmoid(_dot(hpb, wpg[...]))
        pe = _dot(p_ref[...].astype(BF16), wple[...])
        x2 = x1 + gate * pe
        r3 = lax.rsqrt(jnp.mean(x2 * x2, axis=-1, keepdims=True) + EPS)
        xh2 = x2 * r3
        err = xh2 * fg_ref[...] - t_ref[...]
        acc_ref[2:3, :] += 0.5 * jnp.sum(jnp.mean(err * err, axis=-1, keepdims=True))
        dyo = err * (1.0 / D)
        acc_ref[0:1, :] += jnp.sum(dyo * xh2, axis=0, keepdims=True)
        dn = dyo * fg_ref[...]
        dx2 = r3 * (dn - xh2 * jnp.mean(dn * xh2, axis=-1, keepdims=True))
        dpe_ref[...] = (dx2 * gate).astype(BF16)
        dpre = (dx2 * pe * gate * (1.0 - gate)).astype(BF16)
        dpre_ref[...] = dpre
        dhp = _dot(dpre, wpgt[...])
        acc_ref[1:2, :] += jnp.sum(dhp * xh1, axis=0, keepdims=True)
        dhn = dhp * pg_ref[...]
        dx1 = dx2 + r2 * (dhn - xh1 * jnp.mean(dhn * xh1, axis=-1, keepdims=True))
        dx1_ref[...] = dx1
        dx1b_ref[...] = dx1.astype(BF16)

    row = lambda w: pl.BlockSpec((tm, w), lambda i: (i, 0))
    wsp = lambda s: pl.BlockSpec(s, lambda i: (0, 0), pipeline_mode=pl.Buffered(1))
    return pl.pallas_call(
        body, name="merge_fwd", grid=(T // tm,),
        in_specs=[row(D), row(D), row(DI), pl.BlockSpec((tm, D), lambda i: (i, OFF_GA // D)),
                  pl.BlockSpec((tm, D), lambda i: (i, OFF_GB // D)), row(PLE), row(D),
                  wsp((D, D)), wsp((DI, D)), wsp((D, D)), wsp((D, D)), wsp((PLE, D)), wsp((D, D)), _full((1, D)), _full((1, D))],
        out_specs=[row(D)] * 8 + [_full((8, D))],
        out_shape=[jax.ShapeDtypeStruct((T, D), F32)] + [jax.ShapeDtypeStruct((T, D), BF16)] * 7 + [jax.ShapeDtypeStruct((8, D), F32)],
        compiler_params=_cp(("arbitrary",)),
    )(x, ya, yb, proj, proj, p, tgt, w_oa, w_ob, w_out, w_pg, w_ple, w_pgt, ple_g, fin_g)


def _merge_bwd(dx1, oa, ob, proj, w_outt, w_oat, w_obt):
    T = dx1.shape[0]
    tm = min(T, _merge_tile())

    def body(dx1_ref, oa_ref, ob_ref, ga_ref, gb_ref, woutt, woat, wobt, doa_ref, dob_ref, dya_ref, dyb_ref, dg_ref):
        dmg = _dot(dx1_ref[...].astype(BF16), woutt[...])
        sa = _sigmoid(ga_ref[...])
        sb = _sigmoid(gb_ref[...])
        doa = (dmg * sa).astype(BF16)
        dob = (dmg * sb).astype(BF16)
        doa_ref[...] = doa
        dob_ref[...] = dob
        dg_ref[:, :D] = (dmg * oa_ref[...].astype(F32) * sa * (1.0 - sa)).astype(BF16)
        dg_ref[:, D:] = (dmg * ob_ref[...].astype(F32) * sb * (1.0 - sb)).astype(BF16)
        dya_ref[...] = _dot(doa, woat[...]).astype(BF16)
        dyb_ref[...] = _dot(dob, wobt[...]).astype(BF16)

    row = lambda w: pl.BlockSpec((tm, w), lambda i: (i, 0))
    wsp = lambda s: pl.BlockSpec(s, lambda i: (0, 0), pipeline_mode=pl.Buffered(1))
    return pl.pallas_call(
        body, name="merge_bwd", grid=(T // tm,),
        in_specs=[row(D), row(D), row(D), pl.BlockSpec((tm, D), lambda i: (i, OFF_GA // D)),
                  pl.BlockSpec((tm, D), lambda i: (i, OFF_GB // D)), wsp((D, D)), wsp((D, D)), wsp((D, DI))],
        out_specs=[row(D), row(D), row(D), row(DI), row(2 * D)],
        out_shape=[jax.ShapeDtypeStruct((T, D), BF16)] * 3 + [jax.ShapeDtypeStruct((T, DI), BF16), jax.ShapeDtypeStruct((T, 2 * D), BF16)],
        compiler_params=_cp(("parallel",)),
    )(dx1, oa, ob, proj, proj, w_outt, w_oat, w_obt)


def _wgrad(a, b, name):
    T, K = a.shape
    N = b.shape[1]
    tt, tk, tn = min(T, 1024), min(K, 1024), min(N, 1024)
    nt = T // tt

    def body(a_ref, b_ref, o_ref, acc_s):
        t = pl.program_id(2)

        @pl.when(t == 0)
        def _():
            acc_s[...] = jnp.zeros_like(acc_s)

        acc_s[...] += _dot_tn(a_ref[...].astype(BF16), b_ref[...])

        @pl.when(t == nt - 1)
        def _():
            o_ref[...] = acc_s[...].astype(BF16)

    return pl.pallas_call(
        body, name=name, grid=(K // tk, N // tn, nt),
        in_specs=[pl.BlockSpec((tt, tk), lambda k, n, t: (t, k)), pl.BlockSpec((tt, tn), lambda k, n, t: (t, n))],
        out_specs=pl.BlockSpec((tk, tn), lambda k, n, t: (k, n)),
        out_shape=jax.ShapeDtypeStruct((K, N), BF16),
        scratch_shapes=[pltpu.VMEM((tk, tn), F32)],
        compiler_params=_cp(("parallel", "parallel", "arbitrary")),
    )(a, b)


def _proj_bwd(x, dx1, norm_g, wt, pieces, ddt, ex):
    T = x.shape[0]
    tm = min(T, 1024)
    nk = OFF_DT // D + 1
    starts = [sum(a.shape[1] for a in pieces[:n]) // D for n in range(len(pieces))]
    ranges = [(s, s + a.shape[1] // D) for s, a in zip(starts, pieces)]
    assert ranges[-1][1] == nk - 1
    npc = len(pieces)

    def body(x_ref, dx1_ref, g_ref, w_ref, wdt_ref, *rest):
        piece_refs, ddt_ref, gx_ref, dng_ref, acc_s = rest[:npc], rest[npc], rest[npc + 1], rest[npc + 2], rest[npc + 3]
        i, k = pl.program_id(0), pl.program_id(1)

        @pl.when((i == 0) & (k == 0))
        def _():
            dng_ref[...] = jnp.zeros_like(dng_ref)

        @pl.when(k == 0)
        def _():
            acc_s[...] = jnp.zeros_like(acc_s)

        for ref, (lo, hi) in zip(piece_refs, ranges):
            @pl.when((k >= lo) & (k < hi))
            def _(ref=ref):
                acc_s[...] += _dot(ref[...], w_ref[...])

        @pl.when(k == nk - 1)
        def _():
            dh = acc_s[...] + _dot(ddt_ref[...], wdt_ref[...])
            xf = x_ref[...]
            r = lax.rsqrt(jnp.mean(xf * xf, axis=-1, keepdims=True) + EPS)
            xh = xf * r
            dng_ref[0:1, :] += jnp.sum(dh * xh, axis=0, keepdims=True)
            dxn = dh * g_ref[...]
            gx_ref[...] = dx1_ref[...] + r * (dxn - xh * jnp.mean(dxn * xh, axis=-1, keepdims=True))

    def piece_spec(lo, hi):
        return pl.BlockSpec((tm, D), lambda i, k: (i, jnp.clip(k - lo, 0, hi - lo - 1)))

    row = pl.BlockSpec((tm, D), lambda i, k: (i, 0))
    return _call(
        body, ex, name="proj_bwd", grid=(T // tm, nk),
        in_specs=[row, row, _full((1, D)), pl.BlockSpec((D, D), lambda i, k: (jnp.minimum(k, nk - 2), 0)),
                  pl.BlockSpec((DTW, D), lambda i, k: (OFF_DT // DTW, 0))]
        + [piece_spec(lo, hi) for lo, hi in ranges] + [pl.BlockSpec((tm, DTW), lambda i, k: (i, 0))],
        out_specs=[row, _full((8, D))],
        out_shape=[jax.ShapeDtypeStruct((T, D), F32), jax.ShapeDtypeStruct((8, D), F32)],
        scratch_shapes=[pltpu.VMEM((tm, D), F32)],
        args=(x, dx1, norm_g, wt, wt, *pieces, ddt))


def _elementwise_tile(R, C, limit=1 << 20):
    if R * C * 4 <= limit:
        return R, C
    rows = [t for t in range(16, R, 16) if R % t == 0 and t * C * 4 <= limit]
    if rows:
        return rows[-1], C
    cols = [t for t in range(128, C, 128) if C % t == 0 and R * t * 4 <= limit]
    return R, cols[-1]


def _adamw(w, m, v, parts, name):
    R, C = w.shape
    tr, tc = _elementwise_tile(R, C)
    c1 = 1.0 - ADAM_B1 ** ADAM_STEP
    c2 = 1.0 - ADAM_B2 ** ADAM_STEP
    n = len(parts)
    wspec = pl.BlockSpec((tr, tc), lambda i, j: (i, j))
    slot = lambda k: pl.BlockSpec((None, tr, tc), lambda i, j: (k, i, j))
    part_specs = [slot(p[1]) if isinstance(p, tuple) else wspec for p in parts]
    part_arrays = [p[0] if isinstance(p, tuple) else p for p in parts]

    def body(*refs):
        w_ref, m_ref, v_ref = refs[:3]
        g_ref, d_ref, nm_ref, nv_ref = refs[3 + n:]
        g = refs[3][...].astype(F32)
        for r in refs[4:3 + n]:
            g = g + r[...].astype(F32)
        mm = ADAM_B1 * m_ref[...] + (1.0 - ADAM_B1) * g
        vv = ADAM_B2 * v_ref[...] + (1.0 - ADAM_B2) * (g * g)
        g_ref[...] = g
        nm_ref[...] = mm
        nv_ref[...] = vv
        d_ref[...] = -ADAM_LR * ((mm / c1) / (jnp.sqrt(vv / c2) + ADAM_EPS) + ADAM_WD * w_ref[...])

    return pl.pallas_call(
        body, name=name, grid=(R // tr, C // tc), in_specs=[wspec] * 3 + part_specs, out_specs=[wspec] * 4,
        out_shape=[jax.ShapeDtypeStruct(w.shape, F32)] * 4, compiler_params=_cp(("parallel", "parallel")),
    )(w, m, v, *part_arrays)


def _pair_sum(mine, theirs, core, name):
    Q, R, C = theirs.shape
    tr, tc = _elementwise_tile(R, C)

    def body(core_ref, a_ref, b_ref, o_ref):
        o_ref[...] = (a_ref[...].astype(F32) + b_ref[...].astype(F32)).astype(BF16)

    spec = pl.BlockSpec((None, tr, tc), lambda q, i, j, core_ref: (q, i, j))
    own = pl.BlockSpec((None, tr, tc), lambda q, i, j, core_ref: (2 * q + core_ref[0], i, j))
    return pl.pallas_call(
        body, name=name,
        grid_spec=pltpu.PrefetchScalarGridSpec(num_scalar_prefetch=1, grid=(Q, R // tr, C // tc), in_specs=[own, spec], out_specs=spec),
        out_shape=jax.ShapeDtypeStruct((Q, R, C), BF16), compiler_params=_cp(("parallel", "parallel", "parallel")),
    )(core, mine, theirs)


def _dev_index(px, py, pc):
    return 4 * px + 2 * py + pc


def _mesh_position():
    return lax.axis_index("x"), lax.axis_index("y"), lax.axis_index("c")


def _gather_exchange(blocks):
    n = len(blocks)

    def make(ins, outs, sems):
        send_sems, recv_sems, local_sems = sems
        x, y, c = _mesh_position()
        me, sibling = (x, y, c), (x, y, 1 - c)
        chips = [(1 - x, y), (x, 1 - y), (1 - x, 1 - y)]

        def copy(a, k, block, to, src=None):
            dst = outs[a].at[_dev_index(*block)]
            return pltpu.make_async_remote_copy(src_ref=dst if src is None else src, dst_ref=dst, send_sem=send_sems.at[a, k],
                                                recv_sem=recv_sems.at[a, k], device_id=to, device_id_type=MESH)

        mine = [pltpu.make_async_copy(ins[a], outs[a].at[_dev_index(*me)], local_sems.at[a]) for a in range(n)]
        first = []
        for a in range(n):
            first.append(copy(a, 0, me, sibling, src=ins[a]))
            first += [copy(a, 1 + j, me, (*chip, c), src=ins[a]) for j, chip in enumerate(chips)]

        def start():
            for cp in mine + first:
                cp.start()

        def finish():
            passed = []
            for j, chip in enumerate(chips):
                for a in range(n):
                    copy(a, 1 + j, (*chip, c), me).wait_recv()
                    fwd = copy(a, 4 + j, (*chip, c), sibling)
                    fwd.start()
                    passed.append(fwd)
            for a in range(n):
                copy(a, 0, sibling, me).wait_recv()
                for j, chip in enumerate(chips):
                    copy(a, 4 + j, (*chip, 1 - c), me).wait_recv()
            for cp in first + passed:
                cp.wait_send()
            for cp in mine:
                cp.wait()

        return start, finish

    return _Exchange(list(blocks), [jax.ShapeDtypeStruct((NDEV,) + b.shape, b.dtype) for b in blocks],
                     [pltpu.SemaphoreType.DMA((n, 7)), pltpu.SemaphoreType.DMA((n, 7)), pltpu.SemaphoreType.DMA((n,))], make)


def _start_wait_all(copies, local=()):
    def start():
        for cp in list(local) + list(copies):
            cp.start()

    def finish():
        for cp in copies:
            cp.wait()
        for cp in local:
            cp.wait()

    return start, finish


def _sibling_exchange(grads):
    n = len(grads)

    def make(ins, outs, sems):
        send_sems, recv_sems = sems
        x, y, c = _mesh_position()
        return _start_wait_all([pltpu.make_async_remote_copy(
            src_ref=ins[a].at[2 * q + (1 - c)], dst_ref=outs[a].at[q], send_sem=send_sems.at[a, q], recv_sem=recv_sems.at[a, q],
            device_id=(x, y, 1 - c), device_id_type=MESH) for a in range(n) for q in range(4)])

    return _Exchange(list(grads), [jax.ShapeDtypeStruct((4,) + g.shape[1:], g.dtype) for g in grads],
                     [pltpu.SemaphoreType.DMA((n, 4)), pltpu.SemaphoreType.DMA((n, 4))], make)


def _chips_exchange(partials):
    n = len(partials)

    def make(ins, outs, sems):
        send_sems, recv_sems = sems
        x, y, c = _mesh_position()
        chips = [(1 - x, y), (x, 1 - y), (1 - x, 1 - y)]
        return _start_wait_all([pltpu.make_async_remote_copy(
            src_ref=ins[a].at[2 * px + py], dst_ref=outs[a].at[k], send_sem=send_sems.at[a, k], recv_sem=recv_sems.at[a, k],
            device_id=(px, py, c), device_id_type=MESH) for a in range(n) for k, (px, py) in enumerate(chips)])

    return _Exchange(list(partials), [jax.ShapeDtypeStruct((3,) + g.shape[1:], g.dtype) for g in partials],
                     [pltpu.SemaphoreType.DMA((n, 3)), pltpu.SemaphoreType.DMA((n, 3))], make)


def _direct_gather_exchange(small):
    def make(ins, outs, sems):
        send_sems, recv_sems, local_sem = sems
        x, y, c = _mesh_position()
        slot = outs[0].at[_dev_index(x, y, c)]
        copies = []
        for r in range(1, NDEV):
            peer = (x ^ ((r >> 2) & 1), y ^ ((r >> 1) & 1), c ^ (r & 1))
            copies.append(pltpu.make_async_remote_copy(src_ref=ins[0], dst_ref=slot, send_sem=send_sems.at[r - 1],
                                                       recv_sem=recv_sems.at[r - 1], device_id=peer, device_id_type=MESH))
        return _start_wait_all(copies, local=[pltpu.make_async_copy(ins[0], slot, local_sem)])

    return _Exchange([small], [jax.ShapeDtypeStruct((NDEV,) + small.shape, small.dtype)],
                     [pltpu.SemaphoreType.DMA((7,)), pltpu.SemaphoreType.DMA((7,)), pltpu.SemaphoreType.DMA(())], make)


_W_IN_ROWS = {"u": (0, 1024), "v": (1024, 2048), "za": (2048, 3072), "zb": (3072, 5120), "xbc": (5120, 8192),
              "dt": (8192, 8224), "ga": (8224, 9248), "gb": (9248, 10272)}
_FWD_ORDER = ("xbc", "u", "zb", "v", "za", "ga", "gb", "dt")
_BWD_ORDER = ("xbc", "zb", "u", "v", "za", "ga", "gb", "dt")


def _w_in_t_rows(wt, order):
    assert order[-1] == "dt"
    z = jnp.zeros((NP - NIN, wt.shape[1]), wt.dtype)
    return jnp.concatenate([wt[slice(*_W_IN_ROWS[n])] for n in order] + [z], axis=0)


_SMALL = [("norm_g", 1024), ("ln_a_g", 1024), ("ln_a_b", 1024), ("w_s", 65536), ("b_s", 512), ("conv_w", 12288),
          ("conv_b", 3072), ("dt_bias", 128), ("a_log", 128), ("d_skip", 128), ("ssm_norm_g", 2048), ("ple_norm_g", 1024),
          ("final_g", 1024)]
_SMALL_ROWS = sum(s for _, s in _SMALL) // 128


def _pack_small(d):
    parts = []
    for name, size in _SMALL:
        if name in d:
            f = d[name].reshape(-1).astype(F32)
            parts.append(jnp.pad(f, (0, size - f.shape[0])))
        else:
            parts.append(jnp.zeros((size,), F32))
    return jnp.concatenate(parts).reshape(_SMALL_ROWS, 128)


def _unpack_small(packed, shapes):
    flat = packed.reshape(-1)
    out, off = {}, 0
    for name, size in _SMALL:
        if name in shapes:
            k = math.prod(shapes[name])
            out[name] = flat[off:off + k].reshape(shapes[name])
        off += size
    return out


_WEIGHTS = ["norm_g", "w_in", "ln_a_g", "ln_a_b", "w_s", "b_s", "conv_w", "conv_b", "dt_bias", "a_log", "d_skip", "ssm_norm_g",
            "w_oa", "w_ob", "w_out", "ple_norm_g", "w_pg", "w_ple", "final_g"]
_REPLICATED = ["norm_g", "ln_a_g", "ln_a_b", "w_s", "b_s", "conv_b", "dt_bias", "a_log", "d_skip", "ssm_norm_g", "ple_norm_g", "final_g"]


def _rows_pack(d, dtype):
    return jnp.concatenate([d["w_oa"].reshape(128, D), d["w_ob"].reshape(256, D), d["w_out"].reshape(128, D),
                            d["w_pg"].reshape(128, D), d["w_ple"].reshape(32, D)], axis=0).astype(dtype)


def kernel(x, p, norm_g, w_in, ln_a_g, ln_a_b, w_s, b_s, conv_w, conv_b, dt_bias, a_log, d_skip, ssm_norm_g, w_oa, w_ob, w_out, ple_norm_g, w_pg, w_ple, final_g, loss_target, m_norm_g, m_w_in, m_ln_a_g, m_ln_a_b, m_w_s, m_b_s, m_conv_w, m_conv_b, m_dt_bias, m_a_log, m_d_skip, m_ssm_norm_g, m_w_oa, m_w_ob, m_w_out, m_ple_norm_g, m_w_pg, m_w_ple, m_final_g, v_norm_g, v_w_in, v_ln_a_g, v_ln_a_b, v_w_s, v_b_s, v_conv_w, v_conv_b, v_dt_bias, v_a_log, v_d_skip, v_ssm_norm_g, v_w_oa, v_w_ob, v_w_out, v_ple_norm_g, v_w_pg, v_w_ple, v_final_g):
    args = dict(locals())
    w = {n: args[n] for n in _WEIGHTS}
    m = {n: args["m_" + n] for n in _WEIGHTS}
    v = {n: args["v_" + n] for n in _WEIGHTS}
    T = x.shape[1]
    xi, yi, ci = lax.axis_index("x"), lax.axis_index("y"), lax.axis_index("c")
    me = 4 * xi + 2 * yi + ci

    chip = 2 * xi + yi
    x2, p2, tgt = x.reshape(T, D), p.reshape(T, PLE), loss_target.reshape(T, D)

    norm_g2 = w["norm_g"].reshape(1, D)
    ws = jnp.where(jnp.tril(jnp.ones((CH, CH), bool))[None], w["w_s"].reshape(NG, CH, CH), 0.0).astype(BF16)
    wst = jnp.transpose(ws, (0, 2, 1))
    bst = jnp.broadcast_to(w["b_s"].reshape(NG, CH, 1), (NG, CH, 256))
    ln_g, ln_b = w["ln_a_g"].reshape(1, D), w["ln_a_b"].reshape(1, D)
    cb = w["conv_b"].reshape(1, CD)
    pad32 = lambda a: jnp.pad(a.reshape(1, NH), ((0, 0), (0, DTW - NH)))
    dtb, alog = pad32(w["dt_bias"]), pad32(w["a_log"])
    dskx = jnp.repeat(w["d_skip"].reshape(NH), HD).reshape(1, DI)
    sg = w["ssm_norm_g"].reshape(1, DI)
    ple_g, fin_g = w["ple_norm_g"].reshape(1, D), w["final_g"].reshape(1, D)
    e128 = (jnp.arange(DTW)[:, None] == (jnp.arange(DI)[None, :] // HD)).astype(BF16)
    et128 = e128.T
    gsel = ((jnp.arange(D)[:, None] // 256) == jnp.arange(128)[None, :]).astype(BF16)

    w_in_t = lambda a: jnp.transpose(a.reshape(D, WSH))
    (a_all,) = _run_exchange(_gather_exchange([w_in_t(w["w_in"]).astype(BF16)]), "all_gather_w_in")
    w_in_full_t = a_all.reshape(NIN, D)
    wp = _w_in_t_rows(w_in_full_t, _FWD_ORDER).T
    wt_bwd = _w_in_t_rows(w_in_full_t, _BWD_ORDER)
    core = ci.reshape(1).astype(jnp.int32)
    (proj, h), (r_all, cw_all) = _proj_fwd(x2, norm_g2, wp, _gather_exchange([_rows_pack(w, BF16), w["conv_w"].reshape(4, CD // NDEV)]))
    f_oa = r_all[:, R_OA:R_OB].reshape(D, D)
    f_ob = r_all[:, R_OB:R_OUT].reshape(DI, D)
    f_out = r_all[:, R_OUT:R_PG].reshape(D, D)
    f_pg = r_all[:, R_PG:R_PLE].reshape(D, D)
    f_ple = jnp.transpose(r_all[:, R_PLE:R_ROWS].reshape(NDEV, PLE, 128), (1, 0, 2)).reshape(PLE, D)
    cw = jnp.transpose(cw_all, (1, 0, 2)).reshape(4, CD)

    ya = _gmlp_fwd(proj, ln_g, ln_b, ws, bst)
    y, yb, hprev = _ssd_fwd(proj, cw, cb, dtb, alog, dskx, sg, e128)
    dx1, dx1b, oa, ob, mg, hp, dpre, dpe, acc = _merge_fwd(x2, ya, yb, proj, p2, tgt, f_oa, f_ob, f_out, f_pg, f_ple, f_pg.T, ple_g, fin_g)
    doa, dob, dya, dyb, dgab = _merge_bwd(dx1b, oa, ob, proj, f_out.T, f_oa.T, f_ob.T)
    loss = lax.psum(acc[2, 0], ("x", "y", "c"))

    gple = jnp.transpose(_wgrad(p2, dpe, "wgrad_ple").reshape(PLE, NDEV, 128), (1, 0, 2)).reshape(NDEV, 32, D)
    gr = jnp.concatenate([_wgrad(ya, doa, "wgrad_oa").reshape(NDEV, 128, D), _wgrad(yb, dob, "wgrad_ob").reshape(NDEV, 256, D),
                          _wgrad(mg, dx1b, "wgrad_out").reshape(NDEV, 128, D), _wgrad(hp, dpre, "wgrad_pg").reshape(NDEV, 128, D),
                          gple], axis=1)
    (duvz, dws, dbs, dln), (from_sib_r,) = _gmlp_bwd(proj, dya, ln_g, ln_b, ws, wst, bst, gsel, _sibling_exchange([gr]))
    pr = _pair_sum(gr, from_sib_r, core, "pair_sum_rows")
    (dxz, ddt, dcw, dsm, dsg), (rr,) = _ssd_bwd(proj, y, dyb, hprev, cw, cb, dtb, alog, dskx, sg, e128, et128, _chips_exchange([pr]))

    g_xz = _wgrad(dxz, h, "wgrad_xbc_zb")
    g_w_in_t = jnp.concatenate([_wgrad(duvz, h, "wgrad_u_v_za"), g_xz[CD:], g_xz[:CD], _wgrad(ddt, h, "wgrad_dt")[:NH],
                                _wgrad(dgab, h, "wgrad_ga_gb")], axis=0)
    ga = g_w_in_t.reshape(NDEV, WSH, D)
    (from_sib_a,) = _run_exchange(_sibling_exchange([ga]), "reduce_scatter_sibling_w_in")
    pa = _pair_sum(ga, from_sib_a, core, "pair_sum_w_in")
    (gx, dng), (ra,) = _proj_bwd(x2, dx1, norm_g2, wt_bwd, [dxz, duvz, dgab], ddt, _chips_exchange([pa]))
    pa_own = lax.dynamic_index_in_dim(pa, chip, 0, keepdims=False)
    pr_own = lax.dynamic_index_in_dim(pr, chip, 0, keepdims=False)

    g = {"norm_g": dng[0], "ln_a_g": dln[0], "ln_a_b": dln[1], "w_s": dws, "b_s": dbs[:, :NG].T,
         "conv_w": dcw[0:4], "conv_b": dcw[4], "dt_bias": dsm[0, :NH], "a_log": dsm[1, :NH], "d_skip": dsm[2, :NH],
         "ssm_norm_g": dsg[0], "ple_norm_g": acc[1], "final_g": acc[0]}
    (small_all,) = _run_exchange(_direct_gather_exchange(_pack_small(g)), "all_gather_small_grads")

    out_g, out_d, out_m, out_v = {}, {}, {}, {}
    res = _adamw(w_in_t(w["w_in"]), w_in_t(m["w_in"]), w_in_t(v["w_in"]), [pa_own, (ra, 0), (ra, 1), (ra, 2)], "adamw_w_in")
    for dst, val in zip((out_g, out_d, out_m, out_v), res):
        dst["w_in"] = jnp.transpose(val).reshape(1, D, WSH)
    res = _adamw(_rows_pack(w, F32), _rows_pack(m, F32), _rows_pack(v, F32), [pr_own, (rr, 0), (rr, 1), (rr, 2)], "adamw_rows")
    for dst, val in zip((out_g, out_d, out_m, out_v), res):
        dst["w_oa"] = val[R_OA:R_OB].reshape(1, 128, D)
        dst["w_ob"] = val[R_OB:R_OUT].reshape(1, 256, D)
        dst["w_out"] = val[R_OUT:R_PG].reshape(1, 128, D)
        dst["w_pg"] = val[R_PG:R_PLE].reshape(1, 128, D)
        dst["w_ple"] = val[R_PLE:R_ROWS].reshape(1, PLE, 128)
    shapes = {n: w[n].shape for n in _REPLICATED}
    res = _adamw(_pack_small({n: w[n] for n in _REPLICATED}), _pack_small({n: m[n] for n in _REPLICATED}),
                 _pack_small({n: v[n] for n in _REPLICATED}), [(small_all, k) for k in range(NDEV)], "adamw_small")
    for dst, val in zip((out_g, out_d, out_m, out_v), res):
        dst.update(_unpack_small(val, shapes))
    g_cw_full = _unpack_small(res[0], {"conv_w": (4, CD)})["conv_w"]
    g_cw = lax.dynamic_slice_in_dim(g_cw_full, me * (CD // NDEV), CD // NDEV, axis=1).reshape(12, 128)
    res = _adamw(w["conv_w"].reshape(12, 128), m["conv_w"].reshape(12, 128), v["conv_w"].reshape(12, 128), [g_cw], "adamw_conv_w")
    for dst, val in zip((out_g, out_d, out_m, out_v), res):
        dst["conv_w"] = val.reshape(1, 4, CD // NDEV)

    return (loss, gx.reshape(1, T, D), *[out_g[n] for n in _WEIGHTS], *[out_d[n] for n in _WEIGHTS],
            *[out_m[n] for n in _WEIGHTS], *[out_v[n] for n in _WEIGHTS])
```

```python
import functools
import math
from typing import Callable, NamedTuple

import jax
import jax.numpy as jnp
from jax import lax
from jax.experimental import pallas as pl
from jax.experimental.pallas import tpu as pltpu

F32 = jnp.float32
BF16 = jnp.bfloat16
HI = lax.Precision.HIGHEST
MESH = pl.DeviceIdType.MESH

D = 1024
DI = 2048
CD = 3072
NH = 32
HD = 64
NST = 128
NG = 4
CH = 128
PLE = 256
NIN = 10272
NDEV = 8
WSH = NIN // NDEV
EPS = 1e-6
OFF_XBC, OFF_U, OFF_ZB, OFF_V, OFF_ZA, OFF_GA, OFF_GB, OFF_DT = 0, 3072, 4096, 6144, 7168, 8192, 9216, 10240
NP = 10368
DTW = 128
R_OA, R_OB, R_OUT, R_PG, R_PLE, R_ROWS = 0, 128, 384, 512, 640, 672

ADAM_LR, ADAM_B1, ADAM_B2, ADAM_EPS, ADAM_WD, ADAM_STEP = 0.001, 0.9, 0.999, 1e-08, 0.01, 10

V7X_VMEM_LIMIT = 56 * 1024 * 1024


def _cp(sem=None):
    return pltpu.CompilerParams(dimension_semantics=sem, vmem_limit_bytes=V7X_VMEM_LIMIT)


def _dot(a, b, prec=None):
    return jnp.dot(a, b, preferred_element_type=F32, precision=prec)


def _dot_nt(a, b, prec=None):
    return lax.dot_general(a, b, (((1,), (1,)), ((), ())), preferred_element_type=F32, precision=prec)


def _dot_tn(a, b, prec=None):
    return lax.dot_general(a, b, (((0,), (0,)), ((), ())), preferred_element_type=F32, precision=prec)


def _sigmoid(x):
    return 1.0 / (1.0 + jnp.exp(-x))


def _gelu_and_grad(x):
    c = math.sqrt(2.0 / math.pi)
    x2 = x * x
    t = jnp.tanh(c * (x + 0.044715 * x * x2))
    g = 0.5 * x * (1.0 + t)
    dg = 0.5 * (1.0 + t) + 0.5 * x * (1.0 - t * t) * c * (1.0 + 3.0 * 0.044715 * x2)
    return g, dg


def _gelu(x):
    c = math.sqrt(2.0 / math.pi)
    return 0.5 * x * (1.0 + jnp.tanh(c * (x + 0.044715 * x * x * x)))


def _softplus(x):
    return jnp.maximum(x, 0.0) + jnp.log(1.0 + jnp.exp(-jnp.abs(x)))


def _full(shape):
    n = len(shape)
    return pl.BlockSpec(shape, lambda *_: (0,) * n)


_ANY = pl.BlockSpec(memory_space=pl.ANY)


class _Exchange(NamedTuple):
    arrays: list
    out_shape: list
    sems: list
    make: Callable


def _call(body, ex, *, name, grid, in_specs, out_specs, out_shape, scratch_shapes, args):
    ki, ko, ks = len(in_specs), len(out_specs), len(scratch_shapes)
    ei, eo = len(ex.arrays), len(ex.out_shape)
    last = [g - 1 for g in grid]

    def full_body(*refs):
        r = list(refs)
        ins, eins, r = r[:ki], r[ki:ki + ei], r[ki + ei:]
        outs, eouts, r = r[:ko], r[ko:ko + eo], r[ko + eo:]
        scr, esems = r[:ks], r[ks:]
        start, finish = ex.make(eins, eouts, esems)
        ids = [pl.program_id(a) for a in range(len(grid))]
        is_first = functools.reduce(lambda p, q: p & q, [i == 0 for i in ids])
        is_last = functools.reduce(lambda p, q: p & q, [i == l for i, l in zip(ids, last)])
        pl.when(is_first)(start)
        body(*ins, *outs, *scr)
        pl.when(is_last)(finish)

    res = pl.pallas_call(
        full_body, name=name, grid=grid, in_specs=list(in_specs) + [_ANY] * ei, out_specs=list(out_specs) + [_ANY] * eo,
        out_shape=list(out_shape) + list(ex.out_shape), scratch_shapes=list(scratch_shapes) + list(ex.sems),
        compiler_params=_cp(("arbitrary",) * len(grid)),
    )(*args, *ex.arrays)
    return res[:ko], res[ko:]


def _run_exchange(ex, name):
    ni, no = len(ex.arrays), len(ex.out_shape)

    def body(*refs):
        start, finish = ex.make(refs[:ni], refs[ni:ni + no], refs[ni + no:])
        start()
        finish()

    return pl.pallas_call(body, name=name, in_specs=[_ANY] * ni, out_specs=[_ANY] * no, out_shape=list(ex.out_shape),
                          scratch_shapes=list(ex.sems))(*ex.arrays)


def _proj_fwd(x, norm_g, wp, ex):
    T = x.shape[0]
    tm, tn = min(T, 1024), 1152
    nj = NP // tn
    assert OFF_DT + DTW == NP

    def body(x_ref, g_ref, w_ref, proj_ref, dt_ref, h_ref, hs_ref):
        j = pl.program_id(1)

        @pl.when(j == 0)
        def _():
            xf = x_ref[...]
            r = lax.rsqrt(jnp.mean(xf * xf, axis=-1, keepdims=True) + EPS)
            h = (xf * r * g_ref[...]).astype(BF16)
            hs_ref[...] = h
            h_ref[...] = h

        acc = _dot(hs_ref[...], w_ref[...])
        proj_ref[...] = acc.astype(BF16)

        @pl.when(j == nj - 1)
        def _():
            dt_ref[...] = acc[:, tn - DTW:]

    return _call(
        body, ex, name="proj_fwd", grid=(T // tm, nj),
        in_specs=[pl.BlockSpec((tm, D), lambda i, j: (i, 0)), _full((1, D)), pl.BlockSpec((D, tn), lambda i, j: (0, j))],
        out_specs=[pl.BlockSpec((tm, tn), lambda i, j: (i, j)), pl.BlockSpec((tm, DTW), lambda i, j: (i, 0)),
                   pl.BlockSpec((tm, D), lambda i, j: (i, 0))],
        out_shape=[jax.ShapeDtypeStruct((T, NP), BF16), jax.ShapeDtypeStruct((T, DTW), F32), jax.ShapeDtypeStruct((T, D), BF16)],
        scratch_shapes=[pltpu.VMEM((tm, D), BF16)], args=(x, norm_g, wp))


def _gmlp_tile():
    return 256


def _gmlp_fwd(proj, ln_g, ln_b, ws, bst):
    T = proj.shape[0]
    tm = min(T, _gmlp_tile())

    def body(u_ref, v_ref, z_ref, lg_ref, lb_ref, ws_ref, bs_ref, ya_ref, vn_s):
        vg = _gelu(v_ref[...].astype(F32))
        mu = jnp.mean(vg, axis=-1, keepdims=True)
        xc = vg - mu
        rstd = lax.rsqrt(jnp.mean(xc * xc, axis=-1, keepdims=True) + EPS)
        vn_s[...] = (xc * rstd * lg_ref[...] + lb_ref[...]).astype(BF16)
        for c in range(tm // CH):
            rs = slice(c * CH, (c + 1) * CH)
            for g in range(NG):
                cs_ = slice(g * 256, (g + 1) * 256)
                sv = _dot(ws_ref[g], vn_s[rs, cs_]) + bs_ref[g]
                z = z_ref[rs, cs_].astype(F32)
                ya_ref[rs, cs_] = (_gelu(u_ref[rs, cs_].astype(F32)) * sv * (z * _sigmoid(z))).astype(BF16)

    blk = lambda off: pl.BlockSpec((tm, D), lambda i: (i, off // D))
    return pl.pallas_call(
        body, name="gmlp_fwd", grid=(T // tm,),
        in_specs=[blk(OFF_U), blk(OFF_V), blk(OFF_ZA), _full((1, D)), _full((1, D)), _full((NG, CH, CH)), _full((NG, CH, 256))],
        out_specs=pl.BlockSpec((tm, D), lambda i: (i, 0)),
        out_shape=jax.ShapeDtypeStruct((T, D), BF16),
        scratch_shapes=[pltpu.VMEM((tm, D), BF16)],
        compiler_params=_cp(("parallel",)),
    )(proj, proj, proj, ln_g, ln_b, ws, bst)


def _gmlp_bwd(proj, dya, ln_g, ln_b, ws, wst, bst, gsel, ex):
    T = proj.shape[0]
    tm = min(T, _gmlp_tile())

    def body(u_ref, v_ref, z_ref, dy_ref, lg_ref, lb_ref, ws_ref, wst_ref, bs_ref, gsel_ref,
             d_ref, dws_ref, dbs_ref, dln_ref, vn_s, dsv_s, dvn_s):
        du_ref, dv_ref, dz_ref = d_ref.at[:, 0:D], d_ref.at[:, D:2 * D], d_ref.at[:, 2 * D:3 * D]
        @pl.when(pl.program_id(0) == 0)
        def _():
            dws_ref[...] = jnp.zeros_like(dws_ref)
            dbs_ref[...] = jnp.zeros_like(dbs_ref)
            dln_ref[...] = jnp.zeros_like(dln_ref)

        vg, dvg_dv = _gelu_and_grad(v_ref[...].astype(F32))
        mu = jnp.mean(vg, axis=-1, keepdims=True)
        xc = vg - mu
        rstd = lax.rsqrt(jnp.mean(xc * xc, axis=-1, keepdims=True) + EPS)
        vhat = xc * rstd
        vn_s[...] = (vhat * lg_ref[...] + lb_ref[...]).astype(BF16)
        ri = lax.broadcasted_iota(jnp.int32, (CH, CH), 0)
        ci = lax.broadcasted_iota(jnp.int32, (CH, CH), 1)
        tril = (ri >= ci).astype(F32)
        for c in range(tm // CH):
            rs = slice(c * CH, (c + 1) * CH)
            for g in range(NG):
                cs_ = slice(g * 256, (g + 1) * 256)
                vn = vn_s[rs, cs_]
                sv = _dot(ws_ref[g], vn) + bs_ref[g]
                z = z_ref[rs, cs_].astype(F32)
                sz = _sigmoid(z)
                ug, dug_du = _gelu_and_grad(u_ref[rs, cs_].astype(F32))
                dy = dy_ref[rs, cs_].astype(F32)
                t = dy * z * sz
                du_ref[rs, cs_] = (t * sv * dug_du).astype(BF16)
                dz_ref[rs, cs_] = (dy * ug * sv * sz * (1.0 + z * (1.0 - sz))).astype(BF16)
                dsv = (t * ug).astype(BF16)
                dsv_s[rs, cs_] = dsv
                dvn_s[rs, cs_] = _dot(wst_ref[g], dsv)
                dws_ref[g] += _dot_nt(dsv, vn) * tril
            dbs_ref[...] += _dot(dsv_s[rs, :], gsel_ref[...])
        dvn = dvn_s[...]
        dln_ref[0:1, :] += jnp.sum(dvn * vhat, axis=0, keepdims=True)
        dln_ref[1:2, :] += jnp.sum(dvn, axis=0, keepdims=True)
        dvh = dvn * lg_ref[...]
        dvg = rstd * (dvh - jnp.mean(dvh, axis=-1, keepdims=True) - vhat * jnp.mean(dvh * vhat, axis=-1, keepdims=True))
        dv_ref[...] = (dvg * dvg_dv).astype(BF16)

    blk = lambda off: pl.BlockSpec((tm, D), lambda i: (i, off // D))
    row = pl.BlockSpec((tm, D), lambda i: (i, 0))
    return _call(
        body, ex, name="gmlp_bwd", grid=(T // tm,),
        in_specs=[blk(OFF_U), blk(OFF_V), blk(OFF_ZA), row, _full((1, D)), _full((1, D)), _full((NG, CH, CH)),
                  _full((NG, CH, CH)), _full((NG, CH, 256)), _full((D, 128))],
        out_specs=[pl.BlockSpec((tm, 3 * D), lambda i: (i, 0)), _full((NG, CH, CH)), _full((CH, 128)), _full((8, D))],
        out_shape=[jax.ShapeDtypeStruct((T, 3 * D), BF16),
                   jax.ShapeDtypeStruct((NG, CH, CH), F32), jax.ShapeDtypeStruct((CH, 128), F32), jax.ShapeDtypeStruct((8, D), F32)],
        scratch_shapes=[pltpu.VMEM((tm, D), BF16), pltpu.VMEM((tm, D), BF16), pltpu.VMEM((tm, D), F32)],
        args=(proj, proj, proj, dya, ln_g, ln_b, ws, wst, bst, gsel))


def _conv_pre(x, xp, cw_ref, cb_ref):
    row = lax.broadcasted_iota(jnp.int32, (CH, 1), 0)
    shifted = [x]
    for j in (1, 2, 3):
        shifted.append(jnp.where(row >= j, pltpu.roll(x, j, 0), pltpu.roll(xp, j, 0)))
    pre = cb_ref[...] + cw_ref[3:4, :] * x
    for j in (1, 2, 3):
        pre = pre + cw_ref[3 - j:4 - j, :] * shifted[j]
    return pre, shifted


def _split_dot(x, w, parts, w_left=False):
    acc, r = None, x
    for k in range(parts):
        hi = r.astype(BF16)
        d = _dot(w, hi) if w_left else _dot(hi, w)
        acc = d if acc is None else acc + d
        if k + 1 < parts:
            r = r - hi.astype(F32)
    return acc


def _chunk_decays(dt, alog_ref, e_ref, cs_s, cst_s, csx_s):
    a = -jnp.exp(alog_ref[...])
    ri = lax.broadcasted_iota(jnp.int32, (CH, CH), 0)
    ci = lax.broadcasted_iota(jnp.int32, (CH, CH), 1)
    tril = ri >= ci
    cs = _split_dot(dt * a, tril.astype(BF16), 3, w_left=True)
    cs_s[...] = cs
    cst_s[...] = cs.T
    csx_s[...] = _split_dot(cs, e_ref[...], 3)
    return a, tril, ri, ci


def _lmat(cst_s, h, tril):
    rowb = jnp.broadcast_to(cst_s[h:h + 1, :], (CH, CH))
    return jnp.exp(jnp.where(tril, rowb.T - rowb, -jnp.inf))


def _head_pair_rows(v, lane):
    return jnp.concatenate([jnp.where(lane < HD, v, 0.0), jnp.where(lane < HD, 0.0, v)], axis=0).astype(BF16)


def _ssd_fwd(proj, dtr, cw, cb, dtb, alog, dskx, sg, e128):
    T = proj.shape[0]
    nc = T // CH

    def body(xbc_ref, zb_ref, dt_ref, cw_ref, cb_ref, dtb_ref, alog_ref, dx_ref, sg_ref, e_ref,
             y_ref, yb_ref, hp_ref, xprev_s, h_s, cs_s, cst_s, csx_s, yz_s):
        @pl.when(pl.program_id(0) == 0)
        def _():
            xprev_s[...] = jnp.zeros_like(xprev_s)
            h_s[...] = jnp.zeros_like(h_s)

        x = xbc_ref[...].astype(F32)
        pre, _ = _conv_pre(x, xprev_s[...], cw_ref, cb_ref)
        xprev_s[...] = x
        xc = pre * _sigmoid(pre)
        dt = _softplus(dt_ref[...] + dtb_ref[...])
        a, tril, _, lane = _chunk_decays(dt, alog_ref, e_ref, cs_s, cst_s, csx_s)
        dt_x = _split_dot(dt, e_ref[...], 2)
        cs_last_x = csx_s[CH - 1:CH, :]
        hp_ref[0] = h_s[...]
        for g in range(NG):
            gs = slice(g * 512, (g + 1) * 512)
            bg = xc[:, DI + g * NST:DI + (g + 1) * NST].astype(BF16)
            cg = xc[:, DI + 512 + g * NST:DI + 512 + (g + 1) * NST].astype(BF16)
            cbm = _dot_nt(cg, bg)
            xg = xc[:, gs]
            xdt = xg * dt_x[:, gs]
            hprev = h_s[:, gs]
            csx = csx_s[:, gs]
            yoff = _dot(cg, hprev.astype(BF16)) * jnp.exp(csx)
            st = _dot_tn(bg, (xdt * jnp.exp(cs_last_x[:, gs] - csx)).astype(BF16))
            h_s[:, gs] = jnp.exp(cs_last_x[:, gs]) * hprev + st
            ssq = jnp.zeros((CH, 1), F32)
            for q in range(4):
                h0 = g * 8 + 2 * q
                ps = slice(q * 128, (q + 1) * 128)
                cols = slice(g * 512 + q * 128, g * 512 + (q + 1) * 128)
                m01 = jnp.concatenate([cbm * _lmat(cst_s, h0, tril), cbm * _lmat(cst_s, h0 + 1, tril)], axis=1).astype(BF16)
                yq = _dot(m01, _head_pair_rows(xdt[:, ps], lane)) + yoff[:, ps] + xg[:, ps] * dx_ref[:, cols]
                y_ref[:, cols] = yq
                z = zb_ref[:, cols].astype(F32)
                yz = yq * z * _sigmoid(z)
                yz_s[:, cols] = yz
                ssq = ssq + jnp.sum(yz * yz, axis=1, keepdims=True)
            rg = lax.rsqrt(ssq * (1.0 / 512.0) + EPS)
            yb_ref[:, gs] = (yz_s[:, gs] * rg * sg_ref[:, gs]).astype(BF16)

    return pl.pallas_call(
        body, name="ssd_fwd", grid=(nc,),
        in_specs=[pl.BlockSpec((CH, CD), lambda c: (c, OFF_XBC // CD)), pl.BlockSpec((CH, DI), lambda c: (c, OFF_ZB // DI)),
                  pl.BlockSpec((CH, DTW), lambda c: (c, 0)), _full((4, CD)), _full((1, CD)), _full((1, DTW)),
                  _full((1, DTW)), _full((1, DI)), _full((1, DI)), _full((DTW, DI))],
        out_specs=[pl.BlockSpec((CH, DI), lambda c: (c, 0)), pl.BlockSpec((CH, DI), lambda c: (c, 0)),
                   pl.BlockSpec((1, NST, DI), lambda c: (c, 0, 0))],
        out_shape=[jax.ShapeDtypeStruct((T, DI), F32), jax.ShapeDtypeStruct((T, DI), BF16),
                   jax.ShapeDtypeStruct((nc, NST, DI), F32)],
        scratch_shapes=[pltpu.VMEM((CH, CD), F32), pltpu.VMEM((NST, DI), F32), pltpu.VMEM((CH, CH), F32),
                        pltpu.VMEM((CH, CH), F32), pltpu.VMEM((CH, DI), F32), pltpu.VMEM((CH, DI), F32)],
        compiler_params=_cp(("arbitrary",)),
    )(proj, proj, dtr, cw, cb, dtb, alog, dskx, sg, e128)


def _ssd_bwd(proj, dtr, y, dyb, hprev_all, cw, cb, dtb, alog, dskx, sg, e128, et128, ex):
    T = proj.shape[0]
    nc = T // CH

    def body(xbc_ref, xbcp_ref, zb_ref, dt_ref, y_ref, dyb_ref, hp_ref, cw_ref, cb_ref, dtb_ref, alog_ref, dx_ref, sg_ref,
             e_ref, et_ref, d_ref, ddt_ref, dcw_ref, dsm_ref, dsg_ref,
             g_s, dpn_s, cs_s, cst_s, csx_s, dy_s, dxdt_s, dxs_s, dsd_s, dxc_s, gh_s):
        dxbc_ref, dzb_ref = d_ref.at[:, 0:CD], d_ref.at[:, CD:CD + DI]
        i = pl.program_id(0)

        @pl.when(i == 0)
        def _():
            g_s[...] = jnp.zeros_like(g_s)
            dpn_s[...] = jnp.zeros_like(dpn_s)
            dcw_ref[...] = jnp.zeros_like(dcw_ref)
            dsm_ref[...] = jnp.zeros_like(dsm_ref)
            dsg_ref[...] = jnp.zeros_like(dsg_ref)

        x = xbc_ref[...].astype(F32)
        xp = jnp.where(i == nc - 1, 0.0, xbcp_ref[...].astype(F32))
        pre, shifted = _conv_pre(x, xp, cw_ref, cb_ref)
        sp = _sigmoid(pre)
        xc = pre * sp
        dtr = dt_ref[...] + dtb_ref[...]
        dt = _softplus(dtr)
        a, tril, ri, lane = _chunk_decays(dt, alog_ref, e_ref, cs_s, cst_s, csx_s)
        et = et_ref[...]
        dt_x = _split_dot(dt, e_ref[...], 2)
        cs_last_x = csx_s[CH - 1:CH, :]

        for g in range(NG):
            gs = slice(g * 512, (g + 1) * 512)
            z = zb_ref[:, gs].astype(F32)
            sz = _sigmoid(z)
            yv = y_ref[:, gs]
            yz = yv * z * sz
            rg = lax.rsqrt(jnp.mean(yz * yz, axis=-1, keepdims=True) + EPS)
            yn = yz * rg
            dyb = dyb_ref[:, gs].astype(F32)
            dsg_ref[0:1, gs] += jnp.sum(dyb * yn, axis=0, keepdims=True)
            dyn = dyb * sg_ref[:, gs]
            dyz = rg * (dyn - yn * jnp.mean(dyn * yn, axis=-1, keepdims=True))
            dy_s[:, gs] = dyz * z * sz
            dzb_ref[:, gs] = (dyz * yv * sz * (1.0 + z * (1.0 - sz))).astype(BF16)

        rsum = jnp.zeros((CH, DTW), F32)
        csum_t = jnp.zeros((DTW, CH), F32)
        for g in range(NG):
            gs = slice(g * 512, (g + 1) * 512)
            bg = xc[:, DI + g * NST:DI + (g + 1) * NST].astype(BF16)
            cg = xc[:, DI + 512 + g * NST:DI + 512 + (g + 1) * NST].astype(BF16)
            cbm = _dot_nt(cg, bg)
            xdt = xc[:, gs] * dt_x[:, gs]
            hprev = hp_ref[0, :, gs]
            hpb = hprev.astype(BF16)
            gn = g_s[:, gs]
            gnb = gn.astype(BF16)
            dy = dy_s[:, gs]
            csx = csx_s[:, gs]
            ecs = jnp.exp(csx)
            dec = jnp.exp(cs_last_x[:, gs] - csx)
            dye = (dy * ecs).astype(BF16)
            dc = _dot_nt(dye, hpb)
            dprev = _dot_tn(cg, dye)
            dxdt_state = dec * _dot(bg, gnb)
            db = _dot_nt((xdt * dec).astype(BF16), gnb)
            dcb = jnp.zeros((CH, CH), F32)
            for q in range(4):
                h0 = g * 8 + 2 * q
                ps = slice(q * 128, (q + 1) * 128)
                dyp = dy[:, ps]
                l0 = _lmat(cst_s, h0, tril)
                l1 = _lmat(cst_s, h0 + 1, tril)
                m0 = cbm * l0
                m1 = cbm * l1
                dm = _dot_nt(dyp.astype(BF16), _head_pair_rows(xdt[:, ps], lane))
                dm0 = dm[:, :CH]
                dm1 = dm[:, CH:]
                dcb = dcb + dm0 * l0 + dm1 * l1
                for hh, qm in ((h0, dm0 * m0), (h0 + 1, dm1 * m1)):
                    rsum = jnp.where(lane == hh, jnp.sum(qm, axis=1, keepdims=True), rsum)
                    csum_t = jnp.where(ri == hh, jnp.sum(qm, axis=0, keepdims=True), csum_t)
                mst = jnp.concatenate([m0, m1], axis=0).astype(BF16)
                d = _dot_tn(mst, _head_pair_rows(dyp, lane))
                dxdt_s[:, g * 512 + q * 128:g * 512 + (q + 1) * 128] = d + dxdt_state[:, ps]
            yoff = _dot(cg, hpb) * ecs
            dsd_s[:, gs] = xdt * dxdt_state
            dxs_s[:, gs] = dy * yoff
            dcbb = dcb.astype(BF16)
            dxc_s[:, DI + 512 + g * NST:DI + 512 + (g + 1) * NST] = dc + _dot(dcbb, bg)
            dxc_s[:, DI + g * NST:DI + (g + 1) * NST] = db + _dot_tn(dcbb, cg)
            gh_s[:, gs] = jnp.broadcast_to(jnp.sum(gn * hprev, axis=0, keepdims=True), (8, 512))
            g_s[:, gs] = dprev + jnp.exp(cs_last_x[:, gs]) * gn

        xs = xc[:, :DI]
        dy = dy_s[...]
        dxdt = dxdt_s[...]
        cs_last = cs_s[CH - 1:CH, :]
        state_e = _split_dot(dsd_s[...], et, 2)
        dcd = 0.125 * jnp.sum(_split_dot(gh_s[...], et, 2), axis=0, keepdims=True) * jnp.exp(cs_last)
        row = lax.broadcasted_iota(jnp.int32, (CH, 1), 0)
        dcs = rsum - csum_t.T + _split_dot(dxs_s[...], et, 2) - state_e
        dcs = dcs + jnp.where(row == CH - 1, jnp.sum(state_e, axis=0, keepdims=True) + dcd, 0.0)
        dda = _split_dot(dcs, (lane >= ri).astype(BF16), 3, w_left=True)
        ddt = dda * a + _dot((dxdt * xs).astype(BF16), et)
        ddtr = jnp.where(lane < NH, ddt * _sigmoid(dtr), 0.0)
        ddt_ref[...] = ddtr.astype(BF16)
        dsm_ref[0:1, :] += jnp.sum(ddtr, axis=0, keepdims=True)
        dsm_ref[1:2, :] += jnp.sum(dda * dt, axis=0, keepdims=True) * a
        dsm_ref[2:3, :] += jnp.sum(_dot((dy * xs).astype(BF16), et), axis=0, keepdims=True)
        dxc_s[:, :DI] = dxdt * dt_x + dy * dx_ref[...]

        dpre = dxc_s[...] * sp * (1.0 + pre * (1.0 - sp))
        dpn = dpn_s[...]
        dcw_ref[4:5, :] += jnp.sum(dpre, axis=0, keepdims=True)
        dxbc = cw_ref[3:4, :] * dpre
        dcw_ref[3:4, :] += jnp.sum(dpre * shifted[0], axis=0, keepdims=True)
        for j in (1, 2, 3):
            dcw_ref[3 - j:4 - j, :] += jnp.sum(dpre * shifted[j], axis=0, keepdims=True)
            ahead = jnp.where(row < CH - j, pltpu.roll(dpre, CH - j, 0), pltpu.roll(dpn, CH - j, 0))
            dxbc = dxbc + cw_ref[3 - j:4 - j, :] * ahead
        dpn_s[...] = dpre
        dxbc_ref[...] = dxbc.astype(BF16)

    rev = lambda c: nc - 1 - c
    return _call(
        body, ex, name="ssd_bwd", grid=(nc,),
        in_specs=[pl.BlockSpec((CH, CD), lambda c: (rev(c), OFF_XBC // CD)),
                  pl.BlockSpec((CH, CD), lambda c: (jnp.maximum(rev(c) - 1, 0), OFF_XBC // CD)),
                  pl.BlockSpec((CH, DI), lambda c: (rev(c), OFF_ZB // DI)),
                  pl.BlockSpec((CH, DTW), lambda c: (rev(c), 0)),
                  pl.BlockSpec((CH, DI), lambda c: (rev(c), 0)), pl.BlockSpec((CH, DI), lambda c: (rev(c), 0)),
                  pl.BlockSpec((1, NST, DI), lambda c: (rev(c), 0, 0)),
                  _full((4, CD)), _full((1, CD)), _full((1, DTW)), _full((1, DTW)), _full((1, DI)), _full((1, DI)),
                  _full((DTW, DI)), _full((DI, DTW))],
        out_specs=[pl.BlockSpec((CH, CD + DI), lambda c: (rev(c), 0)),
                   pl.BlockSpec((CH, DTW), lambda c: (rev(c), 0)), _full((8, CD)), _full((8, DTW)), _full((8, DI))],
        out_shape=[jax.ShapeDtypeStruct((T, CD + DI), BF16), jax.ShapeDtypeStruct((T, DTW), BF16),
                   jax.ShapeDtypeStruct((8, CD), F32), jax.ShapeDtypeStruct((8, DTW), F32), jax.ShapeDtypeStruct((8, DI), F32)],
        scratch_shapes=[pltpu.VMEM((NST, DI), F32), pltpu.VMEM((CH, CD), F32), pltpu.VMEM((CH, CH), F32), pltpu.VMEM((CH, CH), F32),
                        pltpu.VMEM((CH, DI), F32), pltpu.VMEM((CH, DI), F32), pltpu.VMEM((CH, DI), F32), pltpu.VMEM((CH, DI), F32),
                        pltpu.VMEM((CH, DI), F32), pltpu.VMEM((CH, CD), F32), pltpu.VMEM((8, DI), F32)],
        args=(proj, proj, proj, dtr, y, dyb, hprev_all, cw, cb, dtb, alog, dskx, sg, e128, et128))


def _merge_tile(backward=False):
    return 512 if backward else 256


def _merge_fwd(x, ya, yb, proj, p, tgt, w_oa, w_ob, w_out, w_pg, w_ple, w_pgt, ple_g, fin_g):
    T = x.shape[0]
    tm = min(T, _merge_tile())

    def body(x_ref, ya_ref, yb_ref, ga_ref, gb_ref, p_ref, t_ref, woa, wob, wout, wpg, wple, wpgt, pg_ref, fg_ref,
             dx1_ref, dx1b_ref, oa_ref, ob_ref, mg_ref, hp_ref, dpre_ref, dpe_ref, acc_ref):
        @pl.when(pl.program_id(0) == 0)
        def _():
            acc_ref[...] = jnp.zeros_like(acc_ref)

        oa = _dot(ya_ref[...], woa[...])
        ob = _dot(yb_ref[...], wob[...])
        oa_ref[...] = oa.astype(BF16)
        ob_ref[...] = ob.astype(BF16)
        mg = _sigmoid(ga_ref[...].astype(F32)) * oa + _sigmoid(gb_ref[...].astype(F32)) * ob
        mgb = mg.astype(BF16)
        mg_ref[...] = mgb
        x1 = x_ref[...] + _dot(mgb, wout[...])
        r2 = lax.rsqrt(jnp.mean(x1 * x1, axis=-1, keepdims=True) + EPS)
        xh1 = x1 * r2
        hpb = (xh1 * pg_ref[...]).astype(BF16)
        hp_ref[...] = hpb
        gate = _sigmoid(_dot(hpb, wpg[...]))
        pe = _dot(p_ref[...].astype(BF16), wple[...])
        x2 = x1 + gate * pe
        r3 = lax.rsqrt(jnp.mean(x2 * x2, axis=-1, keepdims=True) + EPS)
        xh2 = x2 * r3
        err = xh2 * fg_ref[...] - t_ref[...]
        acc_ref[2:3, :] += 0.5 * jnp.sum(jnp.mean(err * err, axis=-1, keepdims=True))
        dyo = err * (1.0 / D)
        acc_ref[0:1, :] += jnp.sum(dyo * xh2, axis=0, keepdims=True)
        dn = dyo * fg_ref[...]
        dx2 = r3 * (dn - xh2 * jnp.mean(dn * xh2, axis=-1, keepdims=True))
        dpe_ref[...] = (dx2 * gate).astype(BF16)
        dpre = (dx2 * pe * gate * (1.0 - gate)).astype(BF16)
        dpre_ref[...] = dpre
        dhp = _dot(dpre, wpgt[...])
        acc_ref[1:2, :] += jnp.sum(dhp * xh1, axis=0, keepdims=True)
        dhn = dhp * pg_ref[...]
        dx1 = dx2 + r2 * (dhn - xh1 * jnp.mean(dhn * xh1, axis=-1, keepdims=True))
        dx1_ref[...] = dx1
        dx1b_ref[...] = dx1.astype(BF16)

    row = lambda w: pl.BlockSpec((tm, w), lambda i: (i, 0))
    wsp = lambda s: pl.BlockSpec(s, lambda i: (0, 0), pipeline_mode=pl.Buffered(1))
    return pl.pallas_call(
        body, name="merge_fwd", grid=(T // tm,),
        in_specs=[row(D), row(D), row(DI), pl.BlockSpec((tm, D), lambda i: (i, OFF_GA // D)),
                  pl.BlockSpec((tm, D), lambda i: (i, OFF_GB // D)), row(PLE), row(D),
                  wsp((D, D)), wsp((DI, D)), wsp((D, D)), wsp((D, D)), wsp((PLE, D)), wsp((D, D)), _full((1, D)), _full((1, D))],
        out_specs=[row(D)] * 8 + [_full((8, D))],
        out_shape=[jax.ShapeDtypeStruct((T, D), F32)] + [jax.ShapeDtypeStruct((T, D), BF16)] * 7 + [jax.ShapeDtypeStruct((8, D), F32)],
        compiler_params=_cp(("arbitrary",)),
    )(x, ya, yb, proj, proj, p, tgt, w_oa, w_ob, w_out, w_pg, w_ple, w_pgt, ple_g, fin_g)


def _merge_bwd(dx1, oa, ob, proj, w_outt, w_oat, w_obt):
    T = dx1.shape[0]
    tm = min(T, _merge_tile(backward=True))

    def body(dx1_ref, oa_ref, ob_ref, ga_ref, gb_ref, woutt, woat, wobt, doa_ref, dob_ref, dya_ref, dyb_ref, dg_ref):
        dmg = _dot(dx1_ref[...].astype(BF16), woutt[...])
        sa = _sigmoid(ga_ref[...].astype(F32))
        sb = _sigmoid(gb_ref[...].astype(F32))
        doa = (dmg * sa).astype(BF16)
        dob = (dmg * sb).astype(BF16)
        doa_ref[...] = doa
        dob_ref[...] = dob
        dg_ref[:, :D] = (dmg * oa_ref[...].astype(F32) * sa * (1.0 - sa)).astype(BF16)
        dg_ref[:, D:] = (dmg * ob_ref[...].astype(F32) * sb * (1.0 - sb)).astype(BF16)
        dya_ref[...] = _dot(doa, woat[...]).astype(BF16)
        dyb_ref[...] = _dot(dob, wobt[...]).astype(BF16)

    row = lambda w: pl.BlockSpec((tm, w), lambda i: (i, 0))
    wsp = lambda s: pl.BlockSpec(s, lambda i: (0, 0), pipeline_mode=pl.Buffered(1))
    return pl.pallas_call(
        body, name="merge_bwd", grid=(T // tm,),
        in_specs=[row(D), row(D), row(D), pl.BlockSpec((tm, D), lambda i: (i, OFF_GA // D)),
                  pl.BlockSpec((tm, D), lambda i: (i, OFF_GB // D)), wsp((D, D)), wsp((D, D)), wsp((D, DI))],
        out_specs=[row(D), row(D), row(D), row(DI), row(2 * D)],
        out_shape=[jax.ShapeDtypeStruct((T, D), BF16)] * 3 + [jax.ShapeDtypeStruct((T, DI), BF16), jax.ShapeDtypeStruct((T, 2 * D), BF16)],
        compiler_params=_cp(("parallel",)),
    )(dx1, oa, ob, proj, proj, w_outt, w_oat, w_obt)


def _wgrad(a, b, name):
    T, K = a.shape
    N = b.shape[1]
    tt, tk, tn = min(T, 1024), min(K, 1024), min(N, 1024)
    nt = T // tt

    def body(a_ref, b_ref, o_ref, acc_s):
        t = pl.program_id(2)

        @pl.when(t == 0)
        def _():
            acc_s[...] = jnp.zeros_like(acc_s)

        acc_s[...] += _dot_tn(a_ref[...].astype(BF16), b_ref[...])

        @pl.when(t == nt - 1)
        def _():
            o_ref[...] = acc_s[...].astype(BF16)

    return pl.pallas_call(
        body, name=name, grid=(K // tk, N // tn, nt),
        in_specs=[pl.BlockSpec((tt, tk), lambda k, n, t: (t, k)), pl.BlockSpec((tt, tn), lambda k, n, t: (t, n))],
        out_specs=pl.BlockSpec((tk, tn), lambda k, n, t: (k, n)),
        out_shape=jax.ShapeDtypeStruct((K, N), BF16),
        scratch_shapes=[pltpu.VMEM((tk, tn), F32)],
        compiler_params=_cp(("parallel", "parallel", "arbitrary")),
    )(a, b)


def _proj_bwd(x, dx1, norm_g, wt, pieces, ddt, ex):
    T = x.shape[0]
    tm = min(T, 1024)
    nk = OFF_DT // D + 1
    starts = [sum(a.shape[1] for a in pieces[:n]) // D for n in range(len(pieces))]
    ranges = [(s, s + a.shape[1] // D) for s, a in zip(starts, pieces)]
    assert ranges[-1][1] == nk - 1
    npc = len(pieces)

    def body(x_ref, dx1_ref, g_ref, w_ref, wdt_ref, *rest):
        piece_refs, ddt_ref, gx_ref, dng_ref, acc_s = rest[:npc], rest[npc], rest[npc + 1], rest[npc + 2], rest[npc + 3]
        i, k = pl.program_id(0), pl.program_id(1)

        @pl.when((i == 0) & (k == 0))
        def _():
            dng_ref[...] = jnp.zeros_like(dng_ref)

        @pl.when(k == 0)
        def _():
            acc_s[...] = jnp.zeros_like(acc_s)

        for ref, (lo, hi) in zip(piece_refs, ranges):
            @pl.when((k >= lo) & (k < hi))
            def _(ref=ref):
                acc_s[...] += _dot(ref[...], w_ref[...])

        @pl.when(k == nk - 1)
        def _():
            dh = acc_s[...] + _dot(ddt_ref[...], wdt_ref[...])
            xf = x_ref[...]
            r = lax.rsqrt(jnp.mean(xf * xf, axis=-1, keepdims=True) + EPS)
            xh = xf * r
            dng_ref[0:1, :] += jnp.sum(dh * xh, axis=0, keepdims=True)
            dxn = dh * g_ref[...]
            gx_ref[...] = dx1_ref[...] + r * (dxn - xh * jnp.mean(dxn * xh, axis=-1, keepdims=True))

    def piece_spec(lo, hi):
        return pl.BlockSpec((tm, D), lambda i, k: (i, jnp.clip(k - lo, 0, hi - lo - 1)))

    row = pl.BlockSpec((tm, D), lambda i, k: (i, 0))
    return _call(
        body, ex, name="proj_bwd", grid=(T // tm, nk),
        in_specs=[row, row, _full((1, D)), pl.BlockSpec((D, D), lambda i, k: (jnp.minimum(k, nk - 2), 0)),
                  pl.BlockSpec((DTW, D), lambda i, k: (OFF_DT // DTW, 0))]
        + [piece_spec(lo, hi) for lo, hi in ranges] + [pl.BlockSpec((tm, DTW), lambda i, k: (i, 0))],
        out_specs=[row, _full((8, D))],
        out_shape=[jax.ShapeDtypeStruct((T, D), F32), jax.ShapeDtypeStruct((8, D), F32)],
        scratch_shapes=[pltpu.VMEM((tm, D), F32)],
        args=(x, dx1, norm_g, wt, wt, *pieces, ddt))


def _elementwise_tile(R, C, limit=1 << 20):
    if R * C * 4 <= limit:
        return R, C
    rows = [t for t in range(16, R, 16) if R % t == 0 and t * C * 4 <= limit]
    if rows:
        return rows[-1], C
    cols = [t for t in range(128, C, 128) if C % t == 0 and R * t * 4 <= limit]
    return R, cols[-1]


def _adamw(w, m, v, parts, name):
    R, C = w.shape
    tr, tc = _elementwise_tile(R, C)
    c1 = 1.0 - ADAM_B1 ** ADAM_STEP
    c2 = 1.0 - ADAM_B2 ** ADAM_STEP
    n = len(parts)
    wspec = pl.BlockSpec((tr, tc), lambda i, j: (i, j))
    slot = lambda k: pl.BlockSpec((None, tr, tc), lambda i, j: (k, i, j))
    part_specs = [slot(p[1]) if isinstance(p, tuple) else wspec for p in parts]
    part_arrays = [p[0] if isinstance(p, tuple) else p for p in parts]

    def body(*refs):
        w_ref, m_ref, v_ref = refs[:3]
        g_ref, d_ref, nm_ref, nv_ref = refs[3 + n:]
        g = refs[3][...].astype(F32)
        for r in refs[4:3 + n]:
            g = g + r[...].astype(F32)
        mm = ADAM_B1 * m_ref[...] + (1.0 - ADAM_B1) * g
        vv = ADAM_B2 * v_ref[...] + (1.0 - ADAM_B2) * (g * g)
        g_ref[...] = g
        nm_ref[...] = mm
        nv_ref[...] = vv
        d_ref[...] = -ADAM_LR * ((mm / c1) / (jnp.sqrt(vv / c2) + ADAM_EPS) + ADAM_WD * w_ref[...])

    return pl.pallas_call(
        body, name=name, grid=(R // tr, C // tc), in_specs=[wspec] * 3 + part_specs, out_specs=[wspec] * 4,
        out_shape=[jax.ShapeDtypeStruct(w.shape, F32)] * 4, compiler_params=_cp(("parallel", "parallel")),
    )(w, m, v, *part_arrays)


def _pair_sum(mine, theirs, core, name):
    Q, R, C = theirs.shape
    tr, tc = _elementwise_tile(R, C)

    def body(core_ref, a_ref, b_ref, o_ref):
        o_ref[...] = (a_ref[...].astype(F32) + b_ref[...].astype(F32)).astype(BF16)

    spec = pl.BlockSpec((None, tr, tc), lambda q, i, j, core_ref: (q, i, j))
    own = pl.BlockSpec((None, tr, tc), lambda q, i, j, core_ref: (2 * q + core_ref[0], i, j))
    return pl.pallas_call(
        body, name=name,
        grid_spec=pltpu.PrefetchScalarGridSpec(num_scalar_prefetch=1, grid=(Q, R // tr, C // tc), in_specs=[own, spec], out_specs=spec),
        out_shape=jax.ShapeDtypeStruct((Q, R, C), BF16), compiler_params=_cp(("parallel", "parallel", "parallel")),
    )(core, mine, theirs)


def _dev_index(px, py, pc):
    return 4 * px + 2 * py + pc


def _mesh_position():
    return lax.axis_index("x"), lax.axis_index("y"), lax.axis_index("c")


def _gather_exchange(blocks):
    n = len(blocks)

    def make(ins, outs, sems):
        send_sems, recv_sems, local_sems = sems
        x, y, c = _mesh_position()
        me, sibling = (x, y, c), (x, y, 1 - c)
        chips = [(1 - x, y), (x, 1 - y), (1 - x, 1 - y)]

        def copy(a, k, block, to, src=None):
            dst = outs[a].at[_dev_index(*block)]
            return pltpu.make_async_remote_copy(src_ref=dst if src is None else src, dst_ref=dst, send_sem=send_sems.at[a, k],
                                                recv_sem=recv_sems.at[a, k], device_id=to, device_id_type=MESH)

        mine = [pltpu.make_async_copy(ins[a], outs[a].at[_dev_index(*me)], local_sems.at[a]) for a in range(n)]
        first = []
        for a in range(n):
            first.append(copy(a, 0, me, sibling, src=ins[a]))
            first += [copy(a, 1 + j, me, (*chip, c), src=ins[a]) for j, chip in enumerate(chips)]

        def start():
            for cp in mine + first:
                cp.start()

        def finish():
            passed = []
            for j, chip in enumerate(chips):
                for a in range(n):
                    copy(a, 1 + j, (*chip, c), me).wait_recv()
                    fwd = copy(a, 4 + j, (*chip, c), sibling)
                    fwd.start()
                    passed.append(fwd)
            for a in range(n):
                copy(a, 0, sibling, me).wait_recv()
                for j, chip in enumerate(chips):
                    copy(a, 4 + j, (*chip, 1 - c), me).wait_recv()
            for cp in first + passed:
                cp.wait_send()
            for cp in mine:
                cp.wait()

        return start, finish

    return _Exchange(list(blocks), [jax.ShapeDtypeStruct((NDEV,) + b.shape, b.dtype) for b in blocks],
                     [pltpu.SemaphoreType.DMA((n, 7)), pltpu.SemaphoreType.DMA((n, 7)), pltpu.SemaphoreType.DMA((n,))], make)


def _start_wait_all(copies, local=()):
    def start():
        for cp in list(local) + list(copies):
            cp.start()

    def finish():
        for cp in copies:
            cp.wait()
        for cp in local:
            cp.wait()

    return start, finish


def _sibling_exchange(grads):
    n = len(grads)

    def make(ins, outs, sems):
        send_sems, recv_sems = sems
        x, y, c = _mesh_position()
        return _start_wait_all([pltpu.make_async_remote_copy(
            src_ref=ins[a].at[2 * q + (1 - c)], dst_ref=outs[a].at[q], send_sem=send_sems.at[a, q], recv_sem=recv_sems.at[a, q],
            device_id=(x, y, 1 - c), device_id_type=MESH) for a in range(n) for q in range(4)])

    return _Exchange(list(grads), [jax.ShapeDtypeStruct((4,) + g.shape[1:], g.dtype) for g in grads],
                     [pltpu.SemaphoreType.DMA((n, 4)), pltpu.SemaphoreType.DMA((n, 4))], make)


def _chips_exchange(partials):
    n = len(partials)

    def make(ins, outs, sems):
        send_sems, recv_sems = sems
        x, y, c = _mesh_position()
        chips = [(1 - x, y), (x, 1 - y), (1 - x, 1 - y)]
        return _start_wait_all([pltpu.make_async_remote_copy(
            src_ref=ins[a].at[2 * px + py], dst_ref=outs[a].at[k], send_sem=send_sems.at[a, k], recv_sem=recv_sems.at[a, k],
            device_id=(px, py, c), device_id_type=MESH) for a in range(n) for k, (px, py) in enumerate(chips)])

    return _Exchange(list(partials), [jax.ShapeDtypeStruct((3,) + g.shape[1:], g.dtype) for g in partials],
                     [pltpu.SemaphoreType.DMA((n, 3)), pltpu.SemaphoreType.DMA((n, 3))], make)


def _direct_gather_exchange(small):
    def make(ins, outs, sems):
        send_sems, recv_sems, local_sem = sems
        x, y, c = _mesh_position()
        slot = outs[0].at[_dev_index(x, y, c)]
        copies = []
        for r in range(1, NDEV):
            peer = (x ^ ((r >> 2) & 1), y ^ ((r >> 1) & 1), c ^ (r & 1))
            copies.append(pltpu.make_async_remote_copy(src_ref=ins[0], dst_ref=slot, send_sem=send_sems.at[r - 1],
                                                       recv_sem=recv_sems.at[r - 1], device_id=peer, device_id_type=MESH))
        return _start_wait_all(copies, local=[pltpu.make_async_copy(ins[0], slot, local_sem)])

    return _Exchange([small], [jax.ShapeDtypeStruct((NDEV,) + small.shape, small.dtype)],
                     [pltpu.SemaphoreType.DMA((7,)), pltpu.SemaphoreType.DMA((7,)), pltpu.SemaphoreType.DMA(())], make)


_W_IN_ROWS = {"u": (0, 1024), "v": (1024, 2048), "za": (2048, 3072), "zb": (3072, 5120), "xbc": (5120, 8192),
              "dt": (8192, 8224), "ga": (8224, 9248), "gb": (9248, 10272)}
_FWD_ORDER = ("xbc", "u", "zb", "v", "za", "ga", "gb", "dt")
_BWD_ORDER = ("xbc", "zb", "u", "v", "za", "ga", "gb", "dt")


def _w_in_t_rows(wt, order):
    assert order[-1] == "dt"
    z = jnp.zeros((NP - NIN, wt.shape[1]), wt.dtype)
    return jnp.concatenate([wt[slice(*_W_IN_ROWS[n])] for n in order] + [z], axis=0)


_SMALL = [("norm_g", 1024), ("ln_a_g", 1024), ("ln_a_b", 1024), ("w_s", 65536), ("b_s", 512), ("conv_w", 12288),
          ("conv_b", 3072), ("dt_bias", 128), ("a_log", 128), ("d_skip", 128), ("ssm_norm_g", 2048), ("ple_norm_g", 1024),
          ("final_g", 1024)]
_SMALL_ROWS = sum(s for _, s in _SMALL) // 128


def _pack_small(d):
    parts = []
    for name, size in _SMALL:
        if name in d:
            f = d[name].reshape(-1).astype(F32)
            parts.append(jnp.pad(f, (0, size - f.shape[0])))
        else:
            parts.append(jnp.zeros((size,), F32))
    return jnp.concatenate(parts).reshape(_SMALL_ROWS, 128)


def _unpack_small(packed, shapes):
    flat = packed.reshape(-1)
    out, off = {}, 0
    for name, size in _SMALL:
        if name in shapes:
            k = math.prod(shapes[name])
            out[name] = flat[off:off + k].reshape(shapes[name])
        off += size
    return out


_WEIGHTS = ["norm_g", "w_in", "ln_a_g", "ln_a_b", "w_s", "b_s", "conv_w", "conv_b", "dt_bias", "a_log", "d_skip", "ssm_norm_g",
            "w_oa", "w_ob", "w_out", "ple_norm_g", "w_pg", "w_ple", "final_g"]
_REPLICATED = ["norm_g", "ln_a_g", "ln_a_b", "w_s", "b_s", "conv_b", "dt_bias", "a_log", "d_skip", "ssm_norm_g", "ple_norm_g", "final_g"]


def _rows_pack(d, dtype):
    return jnp.concatenate([d["w_oa"].reshape(128, D), d["w_ob"].reshape(256, D), d["w_out"].reshape(128, D),
                            d["w_pg"].reshape(128, D), d["w_ple"].reshape(32, D)], axis=0).astype(dtype)


def kernel(x, p, norm_g, w_in, ln_a_g, ln_a_b, w_s, b_s, conv_w, conv_b, dt_bias, a_log, d_skip, ssm_norm_g, w_oa, w_ob, w_out, ple_norm_g, w_pg, w_ple, final_g, loss_target, m_norm_g, m_w_in, m_ln_a_g, m_ln_a_b, m_w_s, m_b_s, m_conv_w, m_conv_b, m_dt_bias, m_a_log, m_d_skip, m_ssm_norm_g, m_w_oa, m_w_ob, m_w_out, m_ple_norm_g, m_w_pg, m_w_ple, m_final_g, v_norm_g, v_w_in, v_ln_a_g, v_ln_a_b, v_w_s, v_b_s, v_conv_w, v_conv_b, v_dt_bias, v_a_log, v_d_skip, v_ssm_norm_g, v_w_oa, v_w_ob, v_w_out, v_ple_norm_g, v_w_pg, v_w_ple, v_final_g):
    args = dict(locals())
    w = {n: args[n] for n in _WEIGHTS}
    m = {n: args["m_" + n] for n in _WEIGHTS}
    v = {n: args["v_" + n] for n in _WEIGHTS}
    T = x.shape[1]
    xi, yi, ci = lax.axis_index("x"), lax.axis_index("y"), lax.axis_index("c")
    me = 4 * xi + 2 * yi + ci

    chip = 2 * xi + yi
    x2, p2, tgt = x.reshape(T, D), p.reshape(T, PLE), loss_target.reshape(T, D)

    norm_g2 = w["norm_g"].reshape(1, D)
    ws = jnp.where(jnp.tril(jnp.ones((CH, CH), bool))[None], w["w_s"].reshape(NG, CH, CH), 0.0).astype(BF16)
    wst = jnp.transpose(ws, (0, 2, 1))
    bst = jnp.broadcast_to(w["b_s"].reshape(NG, CH, 1), (NG, CH, 256))
    ln_g, ln_b = w["ln_a_g"].reshape(1, D), w["ln_a_b"].reshape(1, D)
    cb = w["conv_b"].reshape(1, CD)
    pad32 = lambda a: jnp.pad(a.reshape(1, NH), ((0, 0), (0, DTW - NH)))
    dtb, alog = pad32(w["dt_bias"]), pad32(w["a_log"])
    dskx = jnp.repeat(w["d_skip"].reshape(NH), HD).reshape(1, DI)
    sg = w["ssm_norm_g"].reshape(1, DI)
    ple_g, fin_g = w["ple_norm_g"].reshape(1, D), w["final_g"].reshape(1, D)
    e128 = (jnp.arange(DTW)[:, None] == (jnp.arange(DI)[None, :] // HD)).astype(BF16)
    et128 = e128.T
    gsel = ((jnp.arange(D)[:, None] // 256) == jnp.arange(128)[None, :]).astype(BF16)

    w_in_t = lambda a: jnp.transpose(a.reshape(D, WSH))
    (a_all,) = _run_exchange(_gather_exchange([w_in_t(w["w_in"]).astype(BF16)]), "all_gather_w_in")
    w_in_full_t = a_all.reshape(NIN, D)
    wp = _w_in_t_rows(w_in_full_t, _FWD_ORDER).T
    wt_bwd = _w_in_t_rows(w_in_full_t, _BWD_ORDER)
    core = ci.reshape(1).astype(jnp.int32)
    (proj, dtr, h), (r_all, cw_all) = _proj_fwd(x2, norm_g2, wp, _gather_exchange([_rows_pack(w, BF16), w["conv_w"].reshape(4, CD // NDEV)]))
    f_oa = r_all[:, R_OA:R_OB].reshape(D, D)
    f_ob = r_all[:, R_OB:R_OUT].reshape(DI, D)
    f_out = r_all[:, R_OUT:R_PG].reshape(D, D)
    f_pg = r_all[:, R_PG:R_PLE].reshape(D, D)
    f_ple = jnp.transpose(r_all[:, R_PLE:R_ROWS].reshape(NDEV, PLE, 128), (1, 0, 2)).reshape(PLE, D)
    cw = jnp.transpose(cw_all, (1, 0, 2)).reshape(4, CD)

    ya = _gmlp_fwd(proj, ln_g, ln_b, ws, bst)
    y, yb, hprev = _ssd_fwd(proj, dtr, cw, cb, dtb, alog, dskx, sg, e128)
    dx1, dx1b, oa, ob, mg, hp, dpre, dpe, acc = _merge_fwd(x2, ya, yb, proj, p2, tgt, f_oa, f_ob, f_out, f_pg, f_ple, f_pg.T, ple_g, fin_g)
    doa, dob, dya, dyb, dgab = _merge_bwd(dx1b, oa, ob, proj, f_out.T, f_oa.T, f_ob.T)
    loss = lax.psum(acc[2, 0], ("x", "y", "c"))

    gple = jnp.transpose(_wgrad(p2, dpe, "wgrad_ple").reshape(PLE, NDEV, 128), (1, 0, 2)).reshape(NDEV, 32, D)
    gr = jnp.concatenate([_wgrad(ya, doa, "wgrad_oa").reshape(NDEV, 128, D), _wgrad(yb, dob, "wgrad_ob").reshape(NDEV, 256, D),
                          _wgrad(mg, dx1b, "wgrad_out").reshape(NDEV, 128, D), _wgrad(hp, dpre, "wgrad_pg").reshape(NDEV, 128, D),
                          gple], axis=1)
    (duvz, dws, dbs, dln), (from_sib_r,) = _gmlp_bwd(proj, dya, ln_g, ln_b, ws, wst, bst, gsel, _sibling_exchange([gr]))
    pr = _pair_sum(gr, from_sib_r, core, "pair_sum_rows")
    (dxz, ddt, dcw, dsm, dsg), (rr,) = _ssd_bwd(proj, dtr, y, dyb, hprev, cw, cb, dtb, alog, dskx, sg, e128, et128, _chips_exchange([pr]))

    g_xz = _wgrad(dxz, h, "wgrad_xbc_zb")
    g_w_in_t = jnp.concatenate([_wgrad(duvz, h, "wgrad_u_v_za"), g_xz[CD:], g_xz[:CD], _wgrad(ddt, h, "wgrad_dt")[:NH],
                                _wgrad(dgab, h, "wgrad_ga_gb")], axis=0)
    ga = g_w_in_t.reshape(NDEV, WSH, D)
    (from_sib_a,) = _run_exchange(_sibling_exchange([ga]), "reduce_scatter_sibling_w_in")
    pa = _pair_sum(ga, from_sib_a, core, "pair_sum_w_in")
    (gx, dng), (ra,) = _proj_bwd(x2, dx1, norm_g2, wt_bwd, [dxz, duvz, dgab], ddt, _chips_exchange([pa]))
    pa_own = lax.dynamic_index_in_dim(pa, chip, 0, keepdims=False)
    pr_own = lax.dynamic_index_in_dim(pr, chip, 0, keepdims=False)

    g = {"norm_g": dng[0], "ln_a_g": dln[0], "ln_a_b": dln[1], "w_s": dws, "b_s": dbs[:, :NG].T,
         "conv_w": dcw[0:4], "conv_b": dcw[4], "dt_bias": dsm[0, :NH], "a_log": dsm[1, :NH], "d_skip": dsm[2, :NH],
         "ssm_norm_g": dsg[0], "ple_norm_g": acc[1], "final_g": acc[0]}
    (small_all,) = _run_exchange(_direct_gather_exchange(_pack_small(g)), "all_gather_small_grads")

    out_g, out_d, out_m, out_v = {}, {}, {}, {}
    res = _adamw(w_in_t(w["w_in"]), w_in_t(m["w_in"]), w_in_t(v["w_in"]), [pa_own, (ra, 0), (ra, 1), (ra, 2)], "adamw_w_in")
    for dst, val in zip((out_g, out_d, out_m, out_v), res):
        dst["w_in"] = jnp.transpose(val).reshape(1, D, WSH)
    res = _adamw(_rows_pack(w, F32), _rows_pack(m, F32), _rows_pack(v, F32), [pr_own, (rr, 0), (rr, 1), (rr, 2)], "adamw_rows")
    for dst, val in zip((out_g, out_d, out_m, out_v), res):
        dst["w_oa"] = val[R_OA:R_OB].reshape(1, 128, D)
        dst["w_ob"] = val[R_OB:R_OUT].reshape(1, 256, D)
        dst["w_out"] = val[R_OUT:R_PG].reshape(1, 128, D)
        dst["w_pg"] = val[R_PG:R_PLE].reshape(1, 128, D)
        dst["w_ple"] = val[R_PLE:R_ROWS].reshape(1, PLE, 128)
    shapes = {n: w[n].shape for n in _REPLICATED}
    res = _adamw(_pack_small({n: w[n] for n in _REPLICATED}), _pack_small({n: m[n] for n in _REPLICATED}),
                 _pack_small({n: v[n] for n in _REPLICATED}), [(small_all, k) for k in range(NDEV)], "adamw_small")
    for dst, val in zip((out_g, out_d, out_m, out_v), res):
        dst.update(_unpack_small(val, shapes))
    g_cw_full = _unpack_small(res[0], {"conv_w": (4, CD)})["conv_w"]
    g_cw = lax.dynamic_slice_in_dim(g_cw_full, me * (CD // NDEV), CD // NDEV, axis=1).reshape(12, 128)
    res = _adamw(w["conv_w"].reshape(12, 128), m["conv_w"].reshape(12, 128), v["conv_w"].reshape(12, 128), [g_cw], "adamw_conv_w")
    for dst, val in zip((out_g, out_d, out_m, out_v), res):
        dst["conv_w"] = val.reshape(1, 4, CD // NDEV)

    return (loss, gx.reshape(1, T, D), *[out_g[n] for n in _WEIGHTS], *[out_d[n] for n in _WEIGHTS],
            *[out_m[n] for n in _WEIGHTS], *[out_v[n] for n in _WEIGHTS])
```

```python
import functools
import math
from typing import Callable, NamedTuple

import jax
import jax.numpy as jnp
from jax import lax
from jax.experimental import pallas as pl
from jax.experimental.pallas import tpu as pltpu

F32 = jnp.float32
BF16 = jnp.bfloat16
MESH = pl.DeviceIdType.MESH

D = 1024
DI = 2048
CD = 3072
NH = 32
HD = 64
NST = 128
NG = 4
CH = 128
PLE = 256
NIN = 10272
NDEV = 8
WSH = NIN // NDEV
EPS = 1e-6
OFF_XBC, OFF_U, OFF_ZB, OFF_V, OFF_ZA, OFF_GA, OFF_GB, OFF_DT = 0, 3072, 4096, 6144, 7168, 8192, 9216, 10240
NP = 10368
DTW = 128
R_OA, R_OB, R_OUT, R_PG, R_PLE, R_ROWS = 0, 128, 384, 512, 640, 672

ADAM_LR, ADAM_B1, ADAM_B2, ADAM_EPS, ADAM_WD, ADAM_STEP = 0.001, 0.9, 0.999, 1e-08, 0.01, 10

V7X_VMEM_LIMIT = 56 * 1024 * 1024


def _cp(sem=None):
    return pltpu.CompilerParams(dimension_semantics=sem, vmem_limit_bytes=V7X_VMEM_LIMIT)


def _dot(a, b, prec=None):
    return jnp.dot(a, b, preferred_element_type=F32, precision=prec)


def _dot_nt(a, b, prec=None):
    return lax.dot_general(a, b, (((1,), (1,)), ((), ())), preferred_element_type=F32, precision=prec)


def _dot_tn(a, b, prec=None):
    return lax.dot_general(a, b, (((0,), (0,)), ((), ())), preferred_element_type=F32, precision=prec)


def _sigmoid(x):
    return 1.0 / (1.0 + jnp.exp(-x))


def _gelu_and_grad(x):
    c = math.sqrt(2.0 / math.pi)
    x2 = x * x
    t = jnp.tanh(c * (x + 0.044715 * x * x2))
    g = 0.5 * x * (1.0 + t)
    dg = 0.5 * (1.0 + t) + 0.5 * x * (1.0 - t * t) * c * (1.0 + 3.0 * 0.044715 * x2)
    return g, dg


def _gelu(x):
    c = math.sqrt(2.0 / math.pi)
    return 0.5 * x * (1.0 + jnp.tanh(c * (x + 0.044715 * x * x * x)))


def _softplus(x):
    return jnp.maximum(x, 0.0) + jnp.log(1.0 + jnp.exp(-jnp.abs(x)))


def _full(shape):
    n = len(shape)
    return pl.BlockSpec(shape, lambda *_: (0,) * n)


_ANY = pl.BlockSpec(memory_space=pl.ANY)


class _Exchange(NamedTuple):
    arrays: list
    out_shape: list
    sems: list
    make: Callable


def _call(body, ex, *, name, grid, in_specs, out_specs, out_shape, scratch_shapes, args):
    ki, ko, ks = len(in_specs), len(out_specs), len(scratch_shapes)
    ei, eo = len(ex.arrays), len(ex.out_shape)
    last = [g - 1 for g in grid]

    def full_body(*refs):
        r = list(refs)
        ins, eins, r = r[:ki], r[ki:ki + ei], r[ki + ei:]
        outs, eouts, r = r[:ko], r[ko:ko + eo], r[ko + eo:]
        scr, esems = r[:ks], r[ks:]
        start, finish = ex.make(eins, eouts, esems)
        ids = [pl.program_id(a) for a in range(len(grid))]
        is_first = functools.reduce(lambda p, q: p & q, [i == 0 for i in ids])
        is_last = functools.reduce(lambda p, q: p & q, [i == l for i, l in zip(ids, last)])
        pl.when(is_first)(start)
        body(*ins, *outs, *scr)
        pl.when(is_last)(finish)

    res = pl.pallas_call(
        full_body, name=name, grid=grid, in_specs=list(in_specs) + [_ANY] * ei, out_specs=list(out_specs) + [_ANY] * eo,
        out_shape=list(out_shape) + list(ex.out_shape), scratch_shapes=list(scratch_shapes) + list(ex.sems),
        compiler_params=_cp(("arbitrary",) * len(grid)),
    )(*args, *ex.arrays)
    return res[:ko], res[ko:]


def _run_exchange(ex, name):
    ni, no = len(ex.arrays), len(ex.out_shape)

    def body(*refs):
        start, finish = ex.make(refs[:ni], refs[ni:ni + no], refs[ni + no:])
        start()
        finish()

    return pl.pallas_call(body, name=name, in_specs=[_ANY] * ni, out_specs=[_ANY] * no, out_shape=list(ex.out_shape),
                          scratch_shapes=list(ex.sems))(*ex.arrays)


def _proj_fwd(x, norm_g, wp, ex):
    T = x.shape[0]
    tm, tn = min(T, 1024), 1152
    nj = NP // tn
    assert OFF_DT + DTW == NP

    def body(x_ref, g_ref, w_ref, proj_ref, dt_ref, h_ref, hs_ref):
        j = pl.program_id(1)

        @pl.when(j == 0)
        def _():
            xf = x_ref[...]
            r = lax.rsqrt(jnp.mean(xf * xf, axis=-1, keepdims=True) + EPS)
            h = (xf * r * g_ref[...]).astype(BF16)
            hs_ref[...] = h
            h_ref[...] = h

        acc = _dot(hs_ref[...], w_ref[...])
        proj_ref[...] = acc.astype(BF16)

        @pl.when(j == nj - 1)
        def _():
            dt_ref[...] = acc[:, tn - DTW:]

    return _call(
        body, ex, name="proj_fwd", grid=(T // tm, nj),
        in_specs=[pl.BlockSpec((tm, D), lambda i, j: (i, 0)), _full((1, D)), pl.BlockSpec((D, tn), lambda i, j: (0, j))],
        out_specs=[pl.BlockSpec((tm, tn), lambda i, j: (i, j)), pl.BlockSpec((tm, DTW), lambda i, j: (i, 0)),
                   pl.BlockSpec((tm, D), lambda i, j: (i, 0))],
        out_shape=[jax.ShapeDtypeStruct((T, NP), BF16), jax.ShapeDtypeStruct((T, DTW), F32), jax.ShapeDtypeStruct((T, D), BF16)],
        scratch_shapes=[pltpu.VMEM((tm, D), BF16)], args=(x, norm_g, wp))


def _gmlp_tile():
    return 256


def _gmlp_fwd(proj, ln_g, ln_b, ws, bst):
    T = proj.shape[0]
    tm = min(T, _gmlp_tile())

    def body(u_ref, v_ref, z_ref, lg_ref, lb_ref, ws_ref, bs_ref, ya_ref, vn_s):
        vg = _gelu(v_ref[...].astype(F32))
        mu = jnp.mean(vg, axis=-1, keepdims=True)
        xc = vg - mu
        rstd = lax.rsqrt(jnp.mean(xc * xc, axis=-1, keepdims=True) + EPS)
        vn_s[...] = (xc * rstd * lg_ref[...] + lb_ref[...]).astype(BF16)
        for c in range(tm // CH):
            rs = slice(c * CH, (c + 1) * CH)
            for g in range(NG):
                cs_ = slice(g * 256, (g + 1) * 256)
                sv = _dot(ws_ref[g], vn_s[rs, cs_]) + bs_ref[g]
                z = z_ref[rs, cs_].astype(F32)
                ya_ref[rs, cs_] = (_gelu(u_ref[rs, cs_].astype(F32)) * sv * (z * _sigmoid(z))).astype(BF16)

    blk = lambda off: pl.BlockSpec((tm, D), lambda i: (i, off // D))
    return pl.pallas_call(
        body, name="gmlp_fwd", grid=(T // tm,),
        in_specs=[blk(OFF_U), blk(OFF_V), blk(OFF_ZA), _full((1, D)), _full((1, D)), _full((NG, CH, CH)), _full((NG, CH, 256))],
        out_specs=pl.BlockSpec((tm, D), lambda i: (i, 0)),
        out_shape=jax.ShapeDtypeStruct((T, D), BF16),
        scratch_shapes=[pltpu.VMEM((tm, D), BF16)],
        compiler_params=_cp(("parallel",)),
    )(proj, proj, proj, ln_g, ln_b, ws, bst)


def _gmlp_bwd(proj, dya, ln_g, ln_b, ws, wst, bst, gsel, ex):
    T = proj.shape[0]
    tm = min(T, _gmlp_tile())

    def body(u_ref, v_ref, z_ref, dy_ref, lg_ref, lb_ref, ws_ref, wst_ref, bs_ref, gsel_ref,
             d_ref, dws_ref, dbs_ref, dln_ref, vn_s, dsv_s, dvn_s):
        du_ref, dv_ref, dz_ref = d_ref.at[:, 0:D], d_ref.at[:, D:2 * D], d_ref.at[:, 2 * D:3 * D]
        @pl.when(pl.program_id(0) == 0)
        def _():
            dws_ref[...] = jnp.zeros_like(dws_ref)
            dbs_ref[...] = jnp.zeros_like(dbs_ref)
            dln_ref[...] = jnp.zeros_like(dln_ref)

        vg, dvg_dv = _gelu_and_grad(v_ref[...].astype(F32))
        mu = jnp.mean(vg, axis=-1, keepdims=True)
        xc = vg - mu
        rstd = lax.rsqrt(jnp.mean(xc * xc, axis=-1, keepdims=True) + EPS)
        vhat = xc * rstd
        vn_s[...] = (vhat * lg_ref[...] + lb_ref[...]).astype(BF16)
        ri = lax.broadcasted_iota(jnp.int32, (CH, CH), 0)
        ci = lax.broadcasted_iota(jnp.int32, (CH, CH), 1)
        tril = (ri >= ci).astype(F32)
        for c in range(tm // CH):
            rs = slice(c * CH, (c + 1) * CH)
            for g in range(NG):
                cs_ = slice(g * 256, (g + 1) * 256)
                vn = vn_s[rs, cs_]
                sv = _dot(ws_ref[g], vn) + bs_ref[g]
                z = z_ref[rs, cs_].astype(F32)
                sz = _sigmoid(z)
                ug, dug_du = _gelu_and_grad(u_ref[rs, cs_].astype(F32))
                dy = dy_ref[rs, cs_].astype(F32)
                t = dy * z * sz
                du_ref[rs, cs_] = (t * sv * dug_du).astype(BF16)
                dz_ref[rs, cs_] = (dy * ug * sv * sz * (1.0 + z * (1.0 - sz))).astype(BF16)
                dsv = (t * ug).astype(BF16)
                dsv_s[rs, cs_] = dsv
                dvn_s[rs, cs_] = _dot(wst_ref[g], dsv)
                dws_ref[g] += _dot_nt(dsv, vn) * tril
            dbs_ref[...] += _dot(dsv_s[rs, :], gsel_ref[...])
        dvn = dvn_s[...]
        dln_ref[0:1, :] += jnp.sum(dvn * vhat, axis=0, keepdims=True)
        dln_ref[1:2, :] += jnp.sum(dvn, axis=0, keepdims=True)
        dvh = dvn * lg_ref[...]
        dvg = rstd * (dvh - jnp.mean(dvh, axis=-1, keepdims=True) - vhat * jnp.mean(dvh * vhat, axis=-1, keepdims=True))
        dv_ref[...] = (dvg * dvg_dv).astype(BF16)

    blk = lambda off: pl.BlockSpec((tm, D), lambda i: (i, off // D))
    row = pl.BlockSpec((tm, D), lambda i: (i, 0))
    return _call(
        body, ex, name="gmlp_bwd", grid=(T // tm,),
        in_specs=[blk(OFF_U), blk(OFF_V), blk(OFF_ZA), row, _full((1, D)), _full((1, D)), _full((NG, CH, CH)),
                  _full((NG, CH, CH)), _full((NG, CH, 256)), _full((D, 128))],
        out_specs=[pl.BlockSpec((tm, 3 * D), lambda i: (i, 0)), _full((NG, CH, CH)), _full((CH, 128)), _full((8, D))],
        out_shape=[jax.ShapeDtypeStruct((T, 3 * D), BF16),
                   jax.ShapeDtypeStruct((NG, CH, CH), F32), jax.ShapeDtypeStruct((CH, 128), F32), jax.ShapeDtypeStruct((8, D), F32)],
        scratch_shapes=[pltpu.VMEM((tm, D), BF16), pltpu.VMEM((tm, D), BF16), pltpu.VMEM((tm, D), F32)],
        args=(proj, proj, proj, dya, ln_g, ln_b, ws, wst, bst, gsel))


def _conv_pre(x, xp, cw_ref, cb_ref):
    row = lax.broadcasted_iota(jnp.int32, (CH, 1), 0)
    shifted = [x]
    for j in (1, 2, 3):
        shifted.append(jnp.where(row >= j, pltpu.roll(x, j, 0), pltpu.roll(xp, j, 0)))
    pre = cb_ref[...] + cw_ref[3:4, :] * x
    for j in (1, 2, 3):
        pre = pre + cw_ref[3 - j:4 - j, :] * shifted[j]
    return pre, shifted


def _split_dot(x, w, parts, w_left=False):
    acc, r = None, x
    for k in range(parts):
        hi = r.astype(BF16)
        d = _dot(w, hi) if w_left else _dot(hi, w)
        acc = d if acc is None else acc + d
        if k + 1 < parts:
            r = r - hi.astype(F32)
    return acc


def _chunk_decays(dt, alog_ref, e_ref, cs_s, cst_s, csx_s):
    a = -jnp.exp(alog_ref[...])
    ri = lax.broadcasted_iota(jnp.int32, (CH, CH), 0)
    ci = lax.broadcasted_iota(jnp.int32, (CH, CH), 1)
    tril = ri >= ci
    cs = _split_dot(dt * a, tril.astype(BF16), 3, w_left=True)
    cs_s[...] = cs
    cst_s[...] = cs.T
    csx_s[...] = _split_dot(cs, e_ref[...], 3)
    return a, tril, ri, ci


def _lmat(cst_s, h, tril):
    rowb = jnp.broadcast_to(cst_s[h:h + 1, :], (CH, CH))
    return jnp.exp(jnp.where(tril, rowb.T - rowb, -jnp.inf))


def _head_pair_rows(v, lane):
    return jnp.concatenate([jnp.where(lane < HD, v, 0.0), jnp.where(lane < HD, 0.0, v)], axis=0).astype(BF16)


def _ssd_fwd(proj, dtr, cw, cb, dtb, alog, dskx, sg, e128):
    T = proj.shape[0]
    nc = T // CH

    def body(xbc_ref, zb_ref, dt_ref, cw_ref, cb_ref, dtb_ref, alog_ref, dx_ref, sg_ref, e_ref,
             y_ref, yb_ref, hp_ref, xprev_s, h_s, cs_s, cst_s, csx_s, yz_s):
        @pl.when(pl.program_id(0) == 0)
        def _():
            xprev_s[...] = jnp.zeros_like(xprev_s)
            h_s[...] = jnp.zeros_like(h_s)

        x = xbc_ref[...].astype(F32)
        pre, _ = _conv_pre(x, xprev_s[...], cw_ref, cb_ref)
        xprev_s[...] = x
        xc = pre * _sigmoid(pre)
        dt = _softplus(dt_ref[...] + dtb_ref[...])
        a, tril, _, lane = _chunk_decays(dt, alog_ref, e_ref, cs_s, cst_s, csx_s)
        dt_x = _split_dot(dt, e_ref[...], 2)
        cs_last_x = csx_s[CH - 1:CH, :]
        hp_ref[0] = h_s[...]
        for g in range(NG):
            gs = slice(g * 512, (g + 1) * 512)
            bg = xc[:, DI + g * NST:DI + (g + 1) * NST].astype(BF16)
            cg = xc[:, DI + 512 + g * NST:DI + 512 + (g + 1) * NST].astype(BF16)
            cbm = _dot_nt(cg, bg)
            xg = xc[:, gs]
            xdt = xg * dt_x[:, gs]
            hprev = h_s[:, gs]
            csx = csx_s[:, gs]
            yoff = _dot(cg, hprev.astype(BF16)) * jnp.exp(csx)
            st = _dot_tn(bg, (xdt * jnp.exp(cs_last_x[:, gs] - csx)).astype(BF16))
            h_s[:, gs] = jnp.exp(cs_last_x[:, gs]) * hprev + st
            ssq = jnp.zeros((CH, 1), F32)
            for q in range(4):
                h0 = g * 8 + 2 * q
                ps = slice(q * 128, (q + 1) * 128)
                cols = slice(g * 512 + q * 128, g * 512 + (q + 1) * 128)
                m01 = jnp.concatenate([cbm * _lmat(cst_s, h0, tril), cbm * _lmat(cst_s, h0 + 1, tril)], axis=1).astype(BF16)
                yq = _dot(m01, _head_pair_rows(xdt[:, ps], lane)) + yoff[:, ps] + xg[:, ps] * dx_ref[:, cols]
                y_ref[:, cols] = yq
                z = zb_ref[:, cols].astype(F32)
                yz = yq * z * _sigmoid(z)
                yz_s[:, cols] = yz
                ssq = ssq + jnp.sum(yz * yz, axis=1, keepdims=True)
            rg = lax.rsqrt(ssq * (1.0 / 512.0) + EPS)
            yb_ref[:, gs] = (yz_s[:, gs] * rg * sg_ref[:, gs]).astype(BF16)

    return pl.pallas_call(
        body, name="ssd_fwd", grid=(nc,),
        in_specs=[pl.BlockSpec((CH, CD), lambda c: (c, OFF_XBC // CD)), pl.BlockSpec((CH, DI), lambda c: (c, OFF_ZB // DI)),
                  pl.BlockSpec((CH, DTW), lambda c: (c, 0)), _full((4, CD)), _full((1, CD)), _full((1, DTW)),
                  _full((1, DTW)), _full((1, DI)), _full((1, DI)), _full((DTW, DI))],
        out_specs=[pl.BlockSpec((CH, DI), lambda c: (c, 0)), pl.BlockSpec((CH, DI), lambda c: (c, 0)),
                   pl.BlockSpec((1, NST, DI), lambda c: (c, 0, 0))],
        out_shape=[jax.ShapeDtypeStruct((T, DI), F32), jax.ShapeDtypeStruct((T, DI), BF16),
                   jax.ShapeDtypeStruct((nc, NST, DI), F32)],
        scratch_shapes=[pltpu.VMEM((CH, CD), F32), pltpu.VMEM((NST, DI), F32), pltpu.VMEM((CH, CH), F32),
                        pltpu.VMEM((CH, CH), F32), pltpu.VMEM((CH, DI), F32), pltpu.VMEM((CH, DI), F32)],
        compiler_params=_cp(("arbitrary",)),
    )(proj, proj, dtr, cw, cb, dtb, alog, dskx, sg, e128)


def _ssd_bwd(proj, dtr, y, dyb, hprev_all, cw, cb, dtb, alog, dskx, sg, e128, et128, ex):
    T = proj.shape[0]
    nc = T // CH

    def body(xbc_ref, xbcp_ref, zb_ref, dt_ref, y_ref, dyb_ref, hp_ref, cw_ref, cb_ref, dtb_ref, alog_ref, dx_ref, sg_ref,
             e_ref, et_ref, d_ref, ddt_ref, dcw_ref, dsm_ref, dsg_ref,
             g_s, dpn_s, cs_s, cst_s, csx_s, dy_s, dxdt_s, dxs_s, dsd_s, dxc_s, gh_s):
        dxbc_ref, dzb_ref = d_ref.at[:, 0:CD], d_ref.at[:, CD:CD + DI]
        i = pl.program_id(0)

        @pl.when(i == 0)
        def _():
            g_s[...] = jnp.zeros_like(g_s)
            dpn_s[...] = jnp.zeros_like(dpn_s)
            dcw_ref[...] = jnp.zeros_like(dcw_ref)
            dsm_ref[...] = jnp.zeros_like(dsm_ref)
            dsg_ref[...] = jnp.zeros_like(dsg_ref)

        x = xbc_ref[...].astype(F32)
        xp = jnp.where(i == nc - 1, 0.0, xbcp_ref[...].astype(F32))
        pre, shifted = _conv_pre(x, xp, cw_ref, cb_ref)
        sp = _sigmoid(pre)
        xc = pre * sp
        dtr = dt_ref[...] + dtb_ref[...]
        dt = _softplus(dtr)
        a, tril, ri, lane = _chunk_decays(dt, alog_ref, e_ref, cs_s, cst_s, csx_s)
        et = et_ref[...]
        dt_x = _split_dot(dt, e_ref[...], 2)
        cs_last_x = csx_s[CH - 1:CH, :]

        for g in range(NG):
            gs = slice(g * 512, (g + 1) * 512)
            z = zb_ref[:, gs].astype(F32)
            sz = _sigmoid(z)
            yv = y_ref[:, gs]
            yz = yv * z * sz
            rg = lax.rsqrt(jnp.mean(yz * yz, axis=-1, keepdims=True) + EPS)
            yn = yz * rg
            dyb = dyb_ref[:, gs].astype(F32)
            dsg_ref[0:1, gs] += jnp.sum(dyb * yn, axis=0, keepdims=True)
            dyn = dyb * sg_ref[:, gs]
            dyz = rg * (dyn - yn * jnp.mean(dyn * yn, axis=-1, keepdims=True))
            dy_s[:, gs] = dyz * z * sz
            dzb_ref[:, gs] = (dyz * yv * sz * (1.0 + z * (1.0 - sz))).astype(BF16)

        rsum = jnp.zeros((CH, DTW), F32)
        csum_t = jnp.zeros((DTW, CH), F32)
        for g in range(NG):
            gs = slice(g * 512, (g + 1) * 512)
            bg = xc[:, DI + g * NST:DI + (g + 1) * NST].astype(BF16)
            cg = xc[:, DI + 512 + g * NST:DI + 512 + (g + 1) * NST].astype(BF16)
            cbm = _dot_nt(cg, bg)
            xdt = xc[:, gs] * dt_x[:, gs]
            hprev = hp_ref[0, :, gs]
            hpb = hprev.astype(BF16)
            gn = g_s[:, gs]
            gnb = gn.astype(BF16)
            dy = dy_s[:, gs]
            csx = csx_s[:, gs]
            ecs = jnp.exp(csx)
            dec = jnp.exp(cs_last_x[:, gs] - csx)
            dye = (dy * ecs).astype(BF16)
            dc = _dot_nt(dye, hpb)
            dprev = _dot_tn(cg, dye)
            dxdt_state = dec * _dot(bg, gnb)
            db = _dot_nt((xdt * dec).astype(BF16), gnb)
            dcb = jnp.zeros((CH, CH), F32)
            for q in range(4):
                h0 = g * 8 + 2 * q
                ps = slice(q * 128, (q + 1) * 128)
                dyp = dy[:, ps]
                l0 = _lmat(cst_s, h0, tril)
                l1 = _lmat(cst_s, h0 + 1, tril)
                m0 = cbm * l0
                m1 = cbm * l1
                dm = _dot_nt(dyp.astype(BF16), _head_pair_rows(xdt[:, ps], lane))
                dm0 = dm[:, :CH]
                dm1 = dm[:, CH:]
                dcb = dcb + dm0 * l0 + dm1 * l1
                for hh, qm in ((h0, dm0 * m0), (h0 + 1, dm1 * m1)):
                    rsum = jnp.where(lane == hh, jnp.sum(qm, axis=1, keepdims=True), rsum)
                    csum_t = jnp.where(ri == hh, jnp.sum(qm, axis=0, keepdims=True), csum_t)
                mst = jnp.concatenate([m0, m1], axis=0).astype(BF16)
                d = _dot_tn(mst, _head_pair_rows(dyp, lane))
                dxdt_s[:, g * 512 + q * 128:g * 512 + (q + 1) * 128] = d + dxdt_state[:, ps]
            yoff = _dot(cg, hpb) * ecs
            dsd_s[:, gs] = xdt * dxdt_state
            dxs_s[:, gs] = dy * yoff
            dcbb = dcb.astype(BF16)
            dxc_s[:, DI + 512 + g * NST:DI + 512 + (g + 1) * NST] = dc + _dot(dcbb, bg)
            dxc_s[:, DI + g * NST:DI + (g + 1) * NST] = db + _dot_tn(dcbb, cg)
            gh_s[:, gs] = jnp.broadcast_to(jnp.sum(gn * hprev, axis=0, keepdims=True), (8, 512))
            g_s[:, gs] = dprev + jnp.exp(cs_last_x[:, gs]) * gn

        xs = xc[:, :DI]
        dy = dy_s[...]
        dxdt = dxdt_s[...]
        cs_last = cs_s[CH - 1:CH, :]
        state_e = _split_dot(dsd_s[...], et, 2)
        dcd = 0.125 * jnp.sum(_split_dot(gh_s[...], et, 2), axis=0, keepdims=True) * jnp.exp(cs_last)
        row = lax.broadcasted_iota(jnp.int32, (CH, 1), 0)
        dcs = rsum - csum_t.T + _split_dot(dxs_s[...], et, 2) - state_e
        dcs = dcs + jnp.where(row == CH - 1, jnp.sum(state_e, axis=0, keepdims=True) + dcd, 0.0)
        dda = _split_dot(dcs, (lane >= ri).astype(BF16), 3, w_left=True)
        ddt = dda * a + _dot((dxdt * xs).astype(BF16), et)
        ddtr = jnp.where(lane < NH, ddt * _sigmoid(dtr), 0.0)
        ddt_ref[...] = ddtr.astype(BF16)
        dsm_ref[0:1, :] += jnp.sum(ddtr, axis=0, keepdims=True)
        dsm_ref[1:2, :] += jnp.sum(dda * dt, axis=0, keepdims=True) * a
        dsm_ref[2:3, :] += jnp.sum(_dot((dy * xs).astype(BF16), et), axis=0, keepdims=True)
        dxc_s[:, :DI] = dxdt * dt_x + dy * dx_ref[...]

        dpre = dxc_s[...] * sp * (1.0 + pre * (1.0 - sp))
        dpn = dpn_s[...]
        dcw_ref[4:5, :] += jnp.sum(dpre, axis=0, keepdims=True)
        dxbc = cw_ref[3:4, :] * dpre
        dcw_ref[3:4, :] += jnp.sum(dpre * shifted[0], axis=0, keepdims=True)
        for j in (1, 2, 3):
            dcw_ref[3 - j:4 - j, :] += jnp.sum(dpre * shifted[j], axis=0, keepdims=True)
            ahead = jnp.where(row < CH - j, pltpu.roll(dpre, CH - j, 0), pltpu.roll(dpn, CH - j, 0))
            dxbc = dxbc + cw_ref[3 - j:4 - j, :] * ahead
        dpn_s[...] = dpre
        dxbc_ref[...] = dxbc.astype(BF16)

    rev = lambda c: nc - 1 - c
    return _call(
        body, ex, name="ssd_bwd", grid=(nc,),
        in_specs=[pl.BlockSpec((CH, CD), lambda c: (rev(c), OFF_XBC // CD)),
                  pl.BlockSpec((CH, CD), lambda c: (jnp.maximum(rev(c) - 1, 0), OFF_XBC // CD)),
                  pl.BlockSpec((CH, DI), lambda c: (rev(c), OFF_ZB // DI)),
                  pl.BlockSpec((CH, DTW), lambda c: (rev(c), 0)),
                  pl.BlockSpec((CH, DI), lambda c: (rev(c), 0)), pl.BlockSpec((CH, DI), lambda c: (rev(c), 0)),
                  pl.BlockSpec((1, NST, DI), lambda c: (rev(c), 0, 0)),
                  _full((4, CD)), _full((1, CD)), _full((1, DTW)), _full((1, DTW)), _full((1, DI)), _full((1, DI)),
                  _full((DTW, DI)), _full((DI, DTW))],
        out_specs=[pl.BlockSpec((CH, CD + DI), lambda c: (rev(c), 0)),
                   pl.BlockSpec((CH, DTW), lambda c: (rev(c), 0)), _full((8, CD)), _full((8, DTW)), _full((8, DI))],
        out_shape=[jax.ShapeDtypeStruct((T, CD + DI), BF16), jax.ShapeDtypeStruct((T, DTW), BF16),
                   jax.ShapeDtypeStruct((8, CD), F32), jax.ShapeDtypeStruct((8, DTW), F32), jax.ShapeDtypeStruct((8, DI), F32)],
        scratch_shapes=[pltpu.VMEM((NST, DI), F32), pltpu.VMEM((CH, CD), F32), pltpu.VMEM((CH, CH), F32), pltpu.VMEM((CH, CH), F32),
                        pltpu.VMEM((CH, DI), F32), pltpu.VMEM((CH, DI), F32), pltpu.VMEM((CH, DI), F32), pltpu.VMEM((CH, DI), F32),
                        pltpu.VMEM((CH, DI), F32), pltpu.VMEM((CH, CD), F32), pltpu.VMEM((8, DI), F32)],
        args=(proj, proj, proj, dtr, y, dyb, hprev_all, cw, cb, dtb, alog, dskx, sg, e128, et128))


def _merge_tile(backward=False):
    return 512 if backward else 256


def _merge_fwd(x, ya, yb, proj, p, tgt, w_oa, w_ob, w_out, w_pg, w_ple, w_pgt, ple_g, fin_g):
    T = x.shape[0]
    tm = min(T, _merge_tile())

    def body(x_ref, ya_ref, yb_ref, ga_ref, gb_ref, p_ref, t_ref, woa, wob, wout, wpg, wple, wpgt, pg_ref, fg_ref,
             dx1_ref, dx1b_ref, oa_ref, ob_ref, mg_ref, hp_ref, dpre_ref, dpe_ref, acc_ref):
        @pl.when(pl.program_id(0) == 0)
        def _():
            acc_ref[...] = jnp.zeros_like(acc_ref)

        oa = _dot(ya_ref[...], woa[...])
        ob = _dot(yb_ref[...], wob[...])
        oa_ref[...] = oa.astype(BF16)
        ob_ref[...] = ob.astype(BF16)
        mg = _sigmoid(ga_ref[...].astype(F32)) * oa + _sigmoid(gb_ref[...].astype(F32)) * ob
        mgb = mg.astype(BF16)
        mg_ref[...] = mgb
        x1 = x_ref[...] + _dot(mgb, wout[...])
        r2 = lax.rsqrt(jnp.mean(x1 * x1, axis=-1, keepdims=True) + EPS)
        xh1 = x1 * r2
        hpb = (xh1 * pg_ref[...]).astype(BF16)
        hp_ref[...] = hpb
        gate = _sigmoid(_dot(hpb, wpg[...]))
        pe = _dot(p_ref[...].astype(BF16), wple[...])
        x2 = x1 + gate * pe
        r3 = lax.rsqrt(jnp.mean(x2 * x2, axis=-1, keepdims=True) + EPS)
        xh2 = x2 * r3
        err = xh2 * fg_ref[...] - t_ref[...]
        acc_ref[2:3, :] += 0.5 * jnp.sum(jnp.mean(err * err, axis=-1, keepdims=True))
        dyo = err * (1.0 / D)
        acc_ref[0:1, :] += jnp.sum(dyo * xh2, axis=0, keepdims=True)
        dn = dyo * fg_ref[...]
        dx2 = r3 * (dn - xh2 * jnp.mean(dn * xh2, axis=-1, keepdims=True))
        dpe_ref[...] = (dx2 * gate).astype(BF16)
        dpre = (dx2 * pe * gate * (1.0 - gate)).astype(BF16)
        dpre_ref[...] = dpre
        dhp = _dot(dpre, wpgt[...])
        acc_ref[1:2, :] += jnp.sum(dhp * xh1, axis=0, keepdims=True)
        dhn = dhp * pg_ref[...]
        dx1 = dx2 + r2 * (dhn - xh1 * jnp.mean(dhn * xh1, axis=-1, keepdims=True))
        dx1_ref[...] = dx1
        dx1b_ref[...] = dx1.astype(BF16)

    row = lambda w: pl.BlockSpec((tm, w), lambda i: (i, 0))
    wsp = lambda s: pl.BlockSpec(s, lambda i: (0, 0), pipeline_mode=pl.Buffered(1))
    return pl.pallas_call(
        body, name="merge_fwd", grid=(T // tm,),
        in_specs=[row(D), row(D), row(DI), pl.BlockSpec((tm, D), lambda i: (i, OFF_GA // D)),
                  pl.BlockSpec((tm, D), lambda i: (i, OFF_GB // D)), row(PLE), row(D),
                  wsp((D, D)), wsp((DI, D)), wsp((D, D)), wsp((D, D)), wsp((PLE, D)), wsp((D, D)), _full((1, D)), _full((1, D))],
        out_specs=[row(D)] * 8 + [_full((8, D))],
        out_shape=[jax.ShapeDtypeStruct((T, D), F32)] + [jax.ShapeDtypeStruct((T, D), BF16)] * 7 + [jax.ShapeDtypeStruct((8, D), F32)],
        compiler_params=_cp(("arbitrary",)),
    )(x, ya, yb, proj, proj, p, tgt, w_oa, w_ob, w_out, w_pg, w_ple, w_pgt, ple_g, fin_g)


def _merge_bwd(dx1, oa, ob, proj, w_outt, w_oat, w_obt):
    T = dx1.shape[0]
    tm = min(T, _merge_tile(backward=True))

    def body(dx1_ref, oa_ref, ob_ref, ga_ref, gb_ref, woutt, woat, wobt, doa_ref, dob_ref, dya_ref, dyb_ref, dg_ref):
        dmg = _dot(dx1_ref[...].astype(BF16), woutt[...])
        sa = _sigmoid(ga_ref[...].astype(F32))
        sb = _sigmoid(gb_ref[...].astype(F32))
        doa = (dmg * sa).astype(BF16)
        dob = (dmg * sb).astype(BF16)
        doa_ref[...] = doa
        dob_ref[...] = dob
        dg_ref[:, :D] = (dmg * oa_ref[...].astype(F32) * sa * (1.0 - sa)).astype(BF16)
        dg_ref[:, D:] = (dmg * ob_ref[...].astype(F32) * sb * (1.0 - sb)).astype(BF16)
        dya_ref[...] = _dot(doa, woat[...]).astype(BF16)
        dyb_ref[...] = _dot(dob, wobt[...]).astype(BF16)

    row = lambda w: pl.BlockSpec((tm, w), lambda i: (i, 0))
    wsp = lambda s: pl.BlockSpec(s, lambda i: (0, 0), pipeline_mode=pl.Buffered(1))
    return pl.pallas_call(
        body, name="merge_bwd", grid=(T // tm,),
        in_specs=[row(D), row(D), row(D), pl.BlockSpec((tm, D), lambda i: (i, OFF_GA // D)),
                  pl.BlockSpec((tm, D), lambda i: (i, OFF_GB // D)), wsp((D, D)), wsp((D, D)), wsp((D, DI))],
        out_specs=[row(D), row(D), row(D), row(DI), row(2 * D)],
        out_shape=[jax.ShapeDtypeStruct((T, D), BF16)] * 3 + [jax.ShapeDtypeStruct((T, DI), BF16), jax.ShapeDtypeStruct((T, 2 * D), BF16)],
        compiler_params=_cp(("parallel",)),
    )(dx1, oa, ob, proj, proj, w_outt, w_oat, w_obt)


def _wgrad(a, b, name):
    T, K = a.shape
    N = b.shape[1]
    tt, tk, tn = min(T, 1024), min(K, 1024), min(N, 1024)
    nt = T // tt

    def body(a_ref, b_ref, o_ref, acc_s):
        t = pl.program_id(2)

        @pl.when(t == 0)
        def _():
            acc_s[...] = jnp.zeros_like(acc_s)

        acc_s[...] += _dot_tn(a_ref[...].astype(BF16), b_ref[...])

        @pl.when(t == nt - 1)
        def _():
            o_ref[...] = acc_s[...].astype(BF16)

    return pl.pallas_call(
        body, name=name, grid=(K // tk, N // tn, nt),
        in_specs=[pl.BlockSpec((tt, tk), lambda k, n, t: (t, k)), pl.BlockSpec((tt, tn), lambda k, n, t: (t, n))],
        out_specs=pl.BlockSpec((tk, tn), lambda k, n, t: (k, n)),
        out_shape=jax.ShapeDtypeStruct((K, N), BF16),
        scratch_shapes=[pltpu.VMEM((tk, tn), F32)],
        compiler_params=_cp(("parallel", "parallel", "arbitrary")),
    )(a, b)


def _proj_bwd(x, dx1, norm_g, wt, pieces, ddt, ex):
    T = x.shape[0]
    tm = min(T, 1024)
    nk = OFF_DT // D + 1
    starts = [sum(a.shape[1] for a in pieces[:n]) // D for n in range(len(pieces))]
    ranges = [(s, s + a.shape[1] // D) for s, a in zip(starts, pieces)]
    assert ranges[-1][1] == nk - 1
    npc = len(pieces)

    def body(x_ref, dx1_ref, g_ref, w_ref, wdt_ref, *rest):
        piece_refs, ddt_ref, gx_ref, dng_ref, acc_s = rest[:npc], rest[npc], rest[npc + 1], rest[npc + 2], rest[npc + 3]
        i, k = pl.program_id(0), pl.program_id(1)

        @pl.when((i == 0) & (k == 0))
        def _():
            dng_ref[...] = jnp.zeros_like(dng_ref)

        @pl.when(k == 0)
        def _():
            acc_s[...] = jnp.zeros_like(acc_s)

        for ref, (lo, hi) in zip(piece_refs, ranges):
            @pl.when((k >= lo) & (k < hi))
            def _(ref=ref):
                acc_s[...] += _dot(ref[...], w_ref[...])

        @pl.when(k == nk - 1)
        def _():
            dh = acc_s[...] + _dot(ddt_ref[...], wdt_ref[...])
            xf = x_ref[...]
            r = lax.rsqrt(jnp.mean(xf * xf, axis=-1, keepdims=True) + EPS)
            xh = xf * r
            dng_ref[0:1, :] += jnp.sum(dh * xh, axis=0, keepdims=True)
            dxn = dh * g_ref[...]
            gx_ref[...] = dx1_ref[...] + r * (dxn - xh * jnp.mean(dxn * xh, axis=-1, keepdims=True))

    def piece_spec(lo, hi):
        return pl.BlockSpec((tm, D), lambda i, k: (i, jnp.clip(k - lo, 0, hi - lo - 1)))

    row = pl.BlockSpec((tm, D), lambda i, k: (i, 0))
    return _call(
        body, ex, name="proj_bwd", grid=(T // tm, nk),
        in_specs=[row, row, _full((1, D)), pl.BlockSpec((D, D), lambda i, k: (jnp.minimum(k, nk - 2), 0)),
                  pl.BlockSpec((DTW, D), lambda i, k: (OFF_DT // DTW, 0))]
        + [piece_spec(lo, hi) for lo, hi in ranges] + [pl.BlockSpec((tm, DTW), lambda i, k: (i, 0))],
        out_specs=[row, _full((8, D))],
        out_shape=[jax.ShapeDtypeStruct((T, D), F32), jax.ShapeDtypeStruct((8, D), F32)],
        scratch_shapes=[pltpu.VMEM((tm, D), F32)],
        args=(x, dx1, norm_g, wt, wt, *pieces, ddt))


def _elementwise_tile(R, C, limit=1 << 20):
    if R * C * 4 <= limit:
        return R, C
    rows = [t for t in range(16, R, 16) if R % t == 0 and t * C * 4 <= limit]
    if rows:
        return rows[-1], C
    cols = [t for t in range(128, C, 128) if C % t == 0 and R * t * 4 <= limit]
    return R, cols[-1]


def _adam_update(w, m, v, g):
    c1 = 1.0 - ADAM_B1 ** ADAM_STEP
    c2 = 1.0 - ADAM_B2 ** ADAM_STEP
    mm = ADAM_B1 * m + (1.0 - ADAM_B1) * g
    vv = ADAM_B2 * v + (1.0 - ADAM_B2) * (g * g)
    return -ADAM_LR * ((mm / c1) / (jnp.sqrt(vv / c2) + ADAM_EPS) + ADAM_WD * w), mm, vv


def _adamw(w, m, v, parts, name, part_row=0, block_rows=None):
    R, C = w.shape
    tr, tc = _elementwise_tile(R, C) if block_rows is None else (block_rows, C)
    assert R % tr == 0 and part_row % tr == 0
    first = part_row // tr
    n = len(parts)
    wspec = pl.BlockSpec((tr, tc), lambda i, j: (i, j))
    flat = pl.BlockSpec((tr, tc), lambda i, j: (first + i, j))
    slot = lambda k: pl.BlockSpec((None, tr, tc), lambda i, j: (k, first + i, j))
    part_specs = [slot(p[1]) if isinstance(p, tuple) else flat for p in parts]
    part_arrays = [p[0] if isinstance(p, tuple) else p for p in parts]

    def body(*refs):
        w_ref, m_ref, v_ref = refs[:3]
        g_ref, d_ref, nm_ref, nv_ref = refs[3 + n:]
        g = refs[3][...].astype(F32)
        for r in refs[4:3 + n]:
            g = g + r[...].astype(F32)
        g_ref[...] = g
        d_ref[...], nm_ref[...], nv_ref[...] = _adam_update(w_ref[...], m_ref[...], v_ref[...], g)

    return pl.pallas_call(
        body, name=name, grid=(R // tr, C // tc), in_specs=[wspec] * 3 + part_specs, out_specs=[wspec] * 4,
        out_shape=[jax.ShapeDtypeStruct(w.shape, F32)] * 4, compiler_params=_cp(("parallel", "parallel")),
    )(w, m, v, *part_arrays)


_SMALL_WIDE = {"norm_g": 0, "ln_a_g": 1, "ln_a_b": 2, "ple_norm_g": 3, "final_g": 4, "ssm_norm_g": 5, "conv_b": 7}
_WIDE_CONV_W, _WIDE_ROWS = 10, 24
_SMALL_NARROW = {"w_s": (0, NG * CH, 128), "b_s": (512, NG, 128), "dt_bias": (516, 1, NH), "a_log": (517, 1, NH), "d_skip": (518, 1, NH)}
_NARROW_ROWS = 520
_SMALL_SHAPES = {"norm_g": (1, D), "ln_a_g": (1, D), "ln_a_b": (1, D), "ple_norm_g": (1, D), "final_g": (1, D),
                 "ssm_norm_g": (1, DI), "conv_b": (1, CD), "w_s": (NG * CH, CH), "b_s": (NG, CH), "dt_bias": (1, NH),
                 "a_log": (1, NH), "d_skip": (1, NH)}


def _adamw_small(w, m, v, wide_all, narrow_all):
    names = list(_SMALL_SHAPES)
    n = len(names)

    def body(*refs):
        wr, mr, vr = refs[:n], refs[n:2 * n], refs[2 * n:3 * n]
        wide_ref, narrow_ref = refs[3 * n], refs[3 * n + 1]
        outs = refs[3 * n + 2:]
        gr, dr, nmr, nvr, cw_ref = outs[:n], outs[n:2 * n], outs[2 * n:3 * n], outs[3 * n:4 * n], outs[4 * n]

        def total(ref, rows, lanes):
            acc = ref[0, rows, lanes]
            for d in range(1, NDEV):
                acc = acc + ref[d, rows, lanes]
            return acc

        for k, name in enumerate(names):
            if name in _SMALL_WIDE:
                for part in range(_SMALL_SHAPES[name][1] // D):
                    r = _SMALL_WIDE[name] + part
                    cols = slice(part * D, (part + 1) * D)
                    g = total(wide_ref, slice(r, r + 1), slice(None))
                    gr[k][:, cols] = g
                    dr[k][:, cols], nmr[k][:, cols], nvr[k][:, cols] = _adam_update(wr[k][:, cols], mr[k][:, cols], vr[k][:, cols], g)
            else:
                r, rows, lanes = _SMALL_NARROW[name]
                g = total(narrow_ref, slice(r, r + rows), slice(0, lanes))
                gr[k][...] = g
                dr[k][...], nmr[k][...], nvr[k][...] = _adam_update(wr[k][...], mr[k][...], vr[k][...], g)
        cw_ref[...] = total(wide_ref, slice(_WIDE_CONV_W, _WIDE_CONV_W + 12), slice(None))

    shapes = [jax.ShapeDtypeStruct(_SMALL_SHAPES[k], F32) for k in names]
    specs = [_full(_SMALL_SHAPES[k]) for k in names]
    res = pl.pallas_call(
        body, name="adamw_small", grid=(1,),
        in_specs=specs * 3 + [_full(wide_all.shape), _full(narrow_all.shape)],
        out_specs=specs * 4 + [_full((12, D))], out_shape=shapes * 4 + [jax.ShapeDtypeStruct((12, D), F32)],
        compiler_params=_cp(("arbitrary",)),
    )(*[w[k] for k in names], *[m[k] for k in names], *[v[k] for k in names], wide_all, narrow_all)
    groups = [dict(zip(names, res[q * n:(q + 1) * n])) for q in range(4)]
    return groups[0], groups[1], groups[2], groups[3], res[4 * n]


def _pair_sum(mine, theirs, core, name):
    Q, R, C = theirs.shape
    tr, tc = _elementwise_tile(R, C)

    def body(core_ref, a_ref, b_ref, o_ref):
        o_ref[...] = (a_ref[...].astype(F32) + b_ref[...].astype(F32)).astype(BF16)

    spec = pl.BlockSpec((None, tr, tc), lambda q, i, j, core_ref: (q, i, j))
    own = pl.BlockSpec((None, tr, tc), lambda q, i, j, core_ref: (2 * q + core_ref[0], i, j))
    return pl.pallas_call(
        body, name=name,
        grid_spec=pltpu.PrefetchScalarGridSpec(num_scalar_prefetch=1, grid=(Q, R // tr, C // tc), in_specs=[own, spec], out_specs=spec),
        out_shape=jax.ShapeDtypeStruct((Q, R, C), BF16), compiler_params=_cp(("parallel", "parallel", "parallel")),
    )(core, mine, theirs)


def _dev_index(px, py, pc):
    return 4 * px + 2 * py + pc


def _mesh_position():
    return lax.axis_index("x"), lax.axis_index("y"), lax.axis_index("c")


def _gather_exchange(blocks):
    n = len(blocks)

    def make(ins, outs, sems):
        send_sems, recv_sems, local_sems = sems
        x, y, c = _mesh_position()
        me, sibling = (x, y, c), (x, y, 1 - c)
        chips = [(1 - x, y), (x, 1 - y), (1 - x, 1 - y)]

        def copy(a, k, block, to, src=None):
            dst = outs[a].at[_dev_index(*block)]
            return pltpu.make_async_remote_copy(src_ref=dst if src is None else src, dst_ref=dst, send_sem=send_sems.at[a, k],
                                                recv_sem=recv_sems.at[a, k], device_id=to, device_id_type=MESH)

        mine = [pltpu.make_async_copy(ins[a], outs[a].at[_dev_index(*me)], local_sems.at[a]) for a in range(n)]
        first = []
        for a in range(n):
            first.append(copy(a, 0, me, sibling, src=ins[a]))
            first += [copy(a, 1 + j, me, (*chip, c), src=ins[a]) for j, chip in enumerate(chips)]

        def start():
            for cp in mine + first:
                cp.start()

        def finish():
            passed = []
            for j, chip in enumerate(chips):
                for a in range(n):
                    copy(a, 1 + j, (*chip, c), me).wait_recv()
                    fwd = copy(a, 4 + j, (*chip, c), sibling)
                    fwd.start()
                    passed.append(fwd)
            for a in range(n):
                copy(a, 0, sibling, me).wait_recv()
                for j, chip in enumerate(chips):
                    copy(a, 4 + j, (*chip, 1 - c), me).wait_recv()
            for cp in first + passed:
                cp.wait_send()
            for cp in mine:
                cp.wait()

        return start, finish

    return _Exchange(list(blocks), [jax.ShapeDtypeStruct((NDEV,) + b.shape, b.dtype) for b in blocks],
                     [pltpu.SemaphoreType.DMA((n, 7)), pltpu.SemaphoreType.DMA((n, 7)), pltpu.SemaphoreType.DMA((n,))], make)


def _start_wait_all(copies, local=()):
    def start():
        for cp in list(local) + list(copies):
            cp.start()

    def finish():
        for cp in copies:
            cp.wait()
        for cp in local:
            cp.wait()

    return start, finish


def _sibling_exchange(grads):
    n = len(grads)

    def make(ins, outs, sems):
        send_sems, recv_sems = sems
        x, y, c = _mesh_position()
        return _start_wait_all([pltpu.make_async_remote_copy(
            src_ref=ins[a].at[2 * q + (1 - c)], dst_ref=outs[a].at[q], send_sem=send_sems.at[a, q], recv_sem=recv_sems.at[a, q],
            device_id=(x, y, 1 - c), device_id_type=MESH) for a in range(n) for q in range(4)])

    return _Exchange(list(grads), [jax.ShapeDtypeStruct((4,) + g.shape[1:], g.dtype) for g in grads],
                     [pltpu.SemaphoreType.DMA((n, 4)), pltpu.SemaphoreType.DMA((n, 4))], make)


def _chips_exchange(partials):
    n = len(partials)

    def make(ins, outs, sems):
        send_sems, recv_sems = sems
        x, y, c = _mesh_position()
        chips = [(1 - x, y), (x, 1 - y), (1 - x, 1 - y)]
        return _start_wait_all([pltpu.make_async_remote_copy(
            src_ref=ins[a].at[2 * px + py], dst_ref=outs[a].at[k], send_sem=send_sems.at[a, k], recv_sem=recv_sems.at[a, k],
            device_id=(px, py, c), device_id_type=MESH) for a in range(n) for k, (px, py) in enumerate(chips)])

    return _Exchange(list(partials), [jax.ShapeDtypeStruct((3,) + g.shape[1:], g.dtype) for g in partials],
                     [pltpu.SemaphoreType.DMA((n, 3)), pltpu.SemaphoreType.DMA((n, 3))], make)


def _direct_gather_exchange(smalls):
    n = len(smalls)

    def make(ins, outs, sems):
        send_sems, recv_sems, local_sems = sems
        x, y, c = _mesh_position()
        copies, local = [], []
        for a in range(n):
            slot = outs[a].at[_dev_index(x, y, c)]
            local.append(pltpu.make_async_copy(ins[a], slot, local_sems.at[a]))
            for r in range(1, NDEV):
                peer = (x ^ ((r >> 2) & 1), y ^ ((r >> 1) & 1), c ^ (r & 1))
                copies.append(pltpu.make_async_remote_copy(src_ref=ins[a], dst_ref=slot, send_sem=send_sems.at[a, r - 1],
                                                           recv_sem=recv_sems.at[a, r - 1], device_id=peer, device_id_type=MESH))
        return _start_wait_all(copies, local=local)

    return _Exchange(list(smalls), [jax.ShapeDtypeStruct((NDEV,) + s.shape, s.dtype) for s in smalls],
                     [pltpu.SemaphoreType.DMA((n, 7)), pltpu.SemaphoreType.DMA((n, 7)), pltpu.SemaphoreType.DMA((n,))], make)


_W_IN_ROWS = {"u": (0, 1024), "v": (1024, 2048), "za": (2048, 3072), "zb": (3072, 5120), "xbc": (5120, 8192),
              "dt": (8192, 8224), "ga": (8224, 9248), "gb": (9248, 10272)}
_FWD_ORDER = ("xbc", "u", "zb", "v", "za", "ga", "gb", "dt")
_BWD_ORDER = ("xbc", "zb", "u", "v", "za", "ga", "gb", "dt")


def _w_in_t_rows(wt, order):
    assert order[-1] == "dt"
    z = jnp.zeros((NP - NIN, wt.shape[1]), wt.dtype)
    return jnp.concatenate([wt[slice(*_W_IN_ROWS[n])] for n in order] + [z], axis=0)


_WEIGHTS = ["norm_g", "w_in", "ln_a_g", "ln_a_b", "w_s", "b_s", "conv_w", "conv_b", "dt_bias", "a_log", "d_skip", "ssm_norm_g",
            "w_oa", "w_ob", "w_out", "ple_norm_g", "w_pg", "w_ple", "final_g"]


def _rows_pack(d, dtype):
    return jnp.concatenate([d["w_oa"].reshape(128, D), d["w_ob"].reshape(256, D), d["w_out"].reshape(128, D),
                            d["w_pg"].reshape(128, D), d["w_ple"].reshape(32, D)], axis=0).astype(dtype)


def kernel(x, p, norm_g, w_in, ln_a_g, ln_a_b, w_s, b_s, conv_w, conv_b, dt_bias, a_log, d_skip, ssm_norm_g, w_oa, w_ob, w_out, ple_norm_g, w_pg, w_ple, final_g, loss_target, m_norm_g, m_w_in, m_ln_a_g, m_ln_a_b, m_w_s, m_b_s, m_conv_w, m_conv_b, m_dt_bias, m_a_log, m_d_skip, m_ssm_norm_g, m_w_oa, m_w_ob, m_w_out, m_ple_norm_g, m_w_pg, m_w_ple, m_final_g, v_norm_g, v_w_in, v_ln_a_g, v_ln_a_b, v_w_s, v_b_s, v_conv_w, v_conv_b, v_dt_bias, v_a_log, v_d_skip, v_ssm_norm_g, v_w_oa, v_w_ob, v_w_out, v_ple_norm_g, v_w_pg, v_w_ple, v_final_g):
    args = dict(locals())
    w = {n: args[n] for n in _WEIGHTS}
    m = {n: args["m_" + n] for n in _WEIGHTS}
    v = {n: args["v_" + n] for n in _WEIGHTS}
    T = x.shape[1]
    xi, yi, ci = lax.axis_index("x"), lax.axis_index("y"), lax.axis_index("c")
    me = 4 * xi + 2 * yi + ci

    chip = 2 * xi + yi
    x2, p2, tgt = x.reshape(T, D), p.reshape(T, PLE), loss_target.reshape(T, D)

    norm_g2 = w["norm_g"].reshape(1, D)
    ws = jnp.where(jnp.tril(jnp.ones((CH, CH), bool))[None], w["w_s"].reshape(NG, CH, CH), 0.0).astype(BF16)
    wst = jnp.transpose(ws, (0, 2, 1))
    bst = jnp.broadcast_to(w["b_s"].reshape(NG, CH, 1), (NG, CH, 256))
    ln_g, ln_b = w["ln_a_g"].reshape(1, D), w["ln_a_b"].reshape(1, D)
    cb = w["conv_b"].reshape(1, CD)
    pad32 = lambda a: jnp.pad(a.reshape(1, NH), ((0, 0), (0, DTW - NH)))
    dtb, alog = pad32(w["dt_bias"]), pad32(w["a_log"])
    dskx = jnp.repeat(w["d_skip"].reshape(NH), HD).reshape(1, DI)
    sg = w["ssm_norm_g"].reshape(1, DI)
    ple_g, fin_g = w["ple_norm_g"].reshape(1, D), w["final_g"].reshape(1, D)
    e128 = (jnp.arange(DTW)[:, None] == (jnp.arange(DI)[None, :] // HD)).astype(BF16)
    et128 = e128.T
    gsel = ((jnp.arange(D)[:, None] // 256) == jnp.arange(128)[None, :]).astype(BF16)

    w_in_t = lambda a: jnp.transpose(a.reshape(D, WSH))
    (a_all,) = _run_exchange(_gather_exchange([w_in_t(w["w_in"]).astype(BF16)]), "all_gather_w_in")
    w_in_full_t = a_all.reshape(NIN, D)
    wp = _w_in_t_rows(w_in_full_t, _FWD_ORDER).T
    wt_bwd = _w_in_t_rows(w_in_full_t, _BWD_ORDER)
    core = ci.reshape(1).astype(jnp.int32)
    (proj, dtr, h), (r_all, cw_all) = _proj_fwd(x2, norm_g2, wp, _gather_exchange([_rows_pack(w, BF16), w["conv_w"].reshape(4, CD // NDEV)]))
    f_oa = r_all[:, R_OA:R_OB].reshape(D, D)
    f_ob = r_all[:, R_OB:R_OUT].reshape(DI, D)
    f_out = r_all[:, R_OUT:R_PG].reshape(D, D)
    f_pg = r_all[:, R_PG:R_PLE].reshape(D, D)
    f_ple = jnp.transpose(r_all[:, R_PLE:R_ROWS].reshape(NDEV, PLE, 128), (1, 0, 2)).reshape(PLE, D)
    cw = jnp.transpose(cw_all, (1, 0, 2)).reshape(4, CD)

    ya = _gmlp_fwd(proj, ln_g, ln_b, ws, bst)
    y, yb, hprev = _ssd_fwd(proj, dtr, cw, cb, dtb, alog, dskx, sg, e128)
    dx1, dx1b, oa, ob, mg, hp, dpre, dpe, acc = _merge_fwd(x2, ya, yb, proj, p2, tgt, f_oa, f_ob, f_out, f_pg, f_ple, f_pg.T, ple_g, fin_g)
    doa, dob, dya, dyb, dgab = _merge_bwd(dx1b, oa, ob, proj, f_out.T, f_oa.T, f_ob.T)
    loss = lax.psum(acc[2, 0], ("x", "y", "c"))

    gple = jnp.transpose(_wgrad(p2, dpe, "wgrad_ple").reshape(PLE, NDEV, 128), (1, 0, 2)).reshape(NDEV, 32, D)
    gr = jnp.concatenate([_wgrad(ya, doa, "wgrad_oa").reshape(NDEV, 128, D), _wgrad(yb, dob, "wgrad_ob").reshape(NDEV, 256, D),
                          _wgrad(mg, dx1b, "wgrad_out").reshape(NDEV, 128, D), _wgrad(hp, dpre, "wgrad_pg").reshape(NDEV, 128, D),
                          gple], axis=1)
    (duvz, dws, dbs, dln), (from_sib_r,) = _gmlp_bwd(proj, dya, ln_g, ln_b, ws, wst, bst, gsel, _sibling_exchange([gr]))
    pr = _pair_sum(gr, from_sib_r, core, "pair_sum_rows")
    (dxz, ddt, dcw, dsm, dsg), (rr,) = _ssd_bwd(proj, dtr, y, dyb, hprev, cw, cb, dtb, alog, dskx, sg, e128, et128, _chips_exchange([pr]))

    g_xz = _wgrad(dxz, h, "wgrad_xbc_zb")
    g_w_in_t = jnp.concatenate([_wgrad(duvz, h, "wgrad_u_v_za"), g_xz[CD:], g_xz[:CD], _wgrad(ddt, h, "wgrad_dt")[:NH],
                                _wgrad(dgab, h, "wgrad_ga_gb")], axis=0)
    ga = g_w_in_t.reshape(NDEV, WSH, D)
    (from_sib_a,) = _run_exchange(_sibling_exchange([ga]), "reduce_scatter_sibling_w_in")
    pa = _pair_sum(ga, from_sib_a, core, "pair_sum_w_in")
    (gx, dng), (ra,) = _proj_bwd(x2, dx1, norm_g2, wt_bwd, [dxz, duvz, dgab], ddt, _chips_exchange([pa]))
    pa_own = lax.dynamic_index_in_dim(pa, chip, 0, keepdims=False)
    pr_own = lax.dynamic_index_in_dim(pr, chip, 0, keepdims=False)

    wide = jnp.concatenate([dng[0:1], dln[0:2], acc[1:2], acc[0:1], dsg[0:1].reshape(2, D), dcw[4:5].reshape(3, D),
                            dcw[0:4].reshape(12, D), jnp.zeros((_WIDE_ROWS - 22, D), F32)], axis=0)
    narrow = jnp.concatenate([dws.reshape(NG * CH, CH), dbs[:, :NG].T, dsm[0:3], jnp.zeros((_NARROW_ROWS - 519, 128), F32)], axis=0)
    wide_all, narrow_all = _run_exchange(_direct_gather_exchange([wide, narrow]), "all_gather_small_grads")

    out_g, out_d, out_m, out_v = {}, {}, {}, {}
    outs = (out_g, out_d, out_m, out_v)
    res = _adamw(w_in_t(w["w_in"]), w_in_t(m["w_in"]), w_in_t(v["w_in"]), [pa_own, (ra, 0), (ra, 1), (ra, 2)], "adamw_w_in")
    for dst, val in zip(outs, res):
        dst["w_in"] = jnp.transpose(val).reshape(1, D, WSH)
    parts_r = [pr_own, (rr, 0), (rr, 1), (rr, 2)]
    for name, row, rows in (("w_oa", R_OA, 128), ("w_ob", R_OB, 256), ("w_out", R_OUT, 128), ("w_pg", R_PG, 128)):
        res = _adamw(w[name].reshape(rows, D), m[name].reshape(rows, D), v[name].reshape(rows, D), parts_r, "adamw_" + name,
                     part_row=row, block_rows=128)
        for dst, val in zip(outs, res):
            dst[name] = val.reshape(1, rows, D)
    res = _adamw(w["w_ple"].reshape(32, D), m["w_ple"].reshape(32, D), v["w_ple"].reshape(32, D), parts_r, "adamw_w_ple",
                 part_row=R_PLE, block_rows=32)
    for dst, val in zip(outs, res):
        dst["w_ple"] = val.reshape(1, PLE, 128)
    two_d = lambda d: {n: d[n].reshape(_SMALL_SHAPES[n]) for n in _SMALL_SHAPES}
    *res, g_cw_wide = _adamw_small(two_d(w), two_d(m), two_d(v), wide_all, narrow_all)
    for dst, val in zip(outs, res):
        dst.update({n: val[n].reshape(w[n].shape) for n in _SMALL_SHAPES})
    g_cw = lax.dynamic_slice_in_dim(g_cw_wide.reshape(4, CD), me * (CD // NDEV), CD // NDEV, axis=1).reshape(12, 128)
    res = _adamw(w["conv_w"].reshape(12, 128), m["conv_w"].reshape(12, 128), v["conv_w"].reshape(12, 128), [g_cw], "adamw_conv_w")
    for dst, val in zip((out_g, out_d, out_m, out_v), res):
        dst["conv_w"] = val.reshape(1, 4, CD // NDEV)

    return (loss, gx.reshape(1, T, D), *[out_g[n] for n in _WEIGHTS], *[out_d[n] for n in _WEIGHTS],
            *[out_m[n] for n in _WEIGHTS], *[out_v[n] for n in _WEIGHTS])
```

```python
import functools
import math
from typing import Callable, NamedTuple

import jax
import jax.numpy as jnp
from jax import lax
from jax.experimental import pallas as pl
from jax.experimental.pallas import tpu as pltpu

F32 = jnp.float32
BF16 = jnp.bfloat16
MESH = pl.DeviceIdType.MESH

D = 1024
DI = 2048
CD = 3072
NH = 32
HD = 64
NST = 128
NG = 4
CH = 128
PLE = 256
NIN = 10272
NDEV = 8
WSH = NIN // NDEV
EPS = 1e-6
OFF_XBC, OFF_U, OFF_ZB, OFF_V, OFF_ZA, OFF_GA, OFF_GB, OFF_DT = 0, 3072, 4096, 6144, 7168, 8192, 9216, 10240
NP = 10368
DTW = 128
R_OA, R_OB, R_OUT, R_PG, R_PLE, R_ROWS = 0, 128, 384, 512, 640, 672

ADAM_LR, ADAM_B1, ADAM_B2, ADAM_EPS, ADAM_WD, ADAM_STEP = 0.001, 0.9, 0.999, 1e-08, 0.01, 10

V7X_VMEM_LIMIT = 56 * 1024 * 1024


def _cp(sem=None):
    return pltpu.CompilerParams(dimension_semantics=sem, vmem_limit_bytes=V7X_VMEM_LIMIT)


def _dot(a, b, prec=None):
    return jnp.dot(a, b, preferred_element_type=F32, precision=prec)


def _dot_nt(a, b, prec=None):
    return lax.dot_general(a, b, (((1,), (1,)), ((), ())), preferred_element_type=F32, precision=prec)


def _dot_tn(a, b, prec=None):
    return lax.dot_general(a, b, (((0,), (0,)), ((), ())), preferred_element_type=F32, precision=prec)


def _sigmoid(x):
    return 1.0 / (1.0 + jnp.exp(-x))


def _gelu_and_grad(x):
    c = math.sqrt(2.0 / math.pi)
    x2 = x * x
    t = jnp.tanh(c * (x + 0.044715 * x * x2))
    g = 0.5 * x * (1.0 + t)
    dg = 0.5 * (1.0 + t) + 0.5 * x * (1.0 - t * t) * c * (1.0 + 3.0 * 0.044715 * x2)
    return g, dg


def _gelu(x):
    c = math.sqrt(2.0 / math.pi)
    return 0.5 * x * (1.0 + jnp.tanh(c * (x + 0.044715 * x * x * x)))


def _softplus(x):
    return jnp.maximum(x, 0.0) + jnp.log(1.0 + jnp.exp(-jnp.abs(x)))


def _full(shape):
    n = len(shape)
    return pl.BlockSpec(shape, lambda *_: (0,) * n)


_ANY = pl.BlockSpec(memory_space=pl.ANY)


class _Exchange(NamedTuple):
    arrays: list
    out_shape: list
    sems: list
    make: Callable


def _call(body, ex, *, name, grid, in_specs, out_specs, out_shape, scratch_shapes, args):
    ki, ko, ks = len(in_specs), len(out_specs), len(scratch_shapes)
    ei, eo = len(ex.arrays), len(ex.out_shape)
    last = [g - 1 for g in grid]

    def full_body(*refs):
        r = list(refs)
        ins, eins, r = r[:ki], r[ki:ki + ei], r[ki + ei:]
        outs, eouts, r = r[:ko], r[ko:ko + eo], r[ko + eo:]
        scr, esems = r[:ks], r[ks:]
        start, finish = ex.make(eins, eouts, esems)
        ids = [pl.program_id(a) for a in range(len(grid))]
        is_first = functools.reduce(lambda p, q: p & q, [i == 0 for i in ids])
        is_last = functools.reduce(lambda p, q: p & q, [i == l for i, l in zip(ids, last)])
        pl.when(is_first)(start)
        body(*ins, *outs, *scr)
        pl.when(is_last)(finish)

    res = pl.pallas_call(
        full_body, name=name, grid=grid, in_specs=list(in_specs) + [_ANY] * ei, out_specs=list(out_specs) + [_ANY] * eo,
        out_shape=list(out_shape) + list(ex.out_shape), scratch_shapes=list(scratch_shapes) + list(ex.sems),
        compiler_params=_cp(("arbitrary",) * len(grid)),
    )(*args, *ex.arrays)
    return res[:ko], res[ko:]


def _run_exchange(ex, name):
    ni, no = len(ex.arrays), len(ex.out_shape)

    def body(*refs):
        start, finish = ex.make(refs[:ni], refs[ni:ni + no], refs[ni + no:])
        start()
        finish()

    return pl.pallas_call(body, name=name, in_specs=[_ANY] * ni, out_specs=[_ANY] * no, out_shape=list(ex.out_shape),
                          scratch_shapes=list(ex.sems))(*ex.arrays)


def _proj_fwd(x, norm_g, wp, ex):
    T = x.shape[0]
    tm, tn = min(T, 1024), 1152
    nj = NP // tn
    assert OFF_DT + DTW == NP

    def body(x_ref, g_ref, w_ref, proj_ref, dt_ref, h_ref, hs_ref):
        j = pl.program_id(1)

        @pl.when(j == 0)
        def _():
            xf = x_ref[...]
            r = lax.rsqrt(jnp.mean(xf * xf, axis=-1, keepdims=True) + EPS)
            h = (xf * r * g_ref[...]).astype(BF16)
            hs_ref[...] = h
            h_ref[...] = h

        acc = _dot(hs_ref[...], w_ref[...])
        proj_ref[...] = acc.astype(BF16)

        @pl.when(j == nj - 1)
        def _():
            dt_ref[...] = acc[:, tn - DTW:]

    return _call(
        body, ex, name="proj_fwd", grid=(T // tm, nj),
        in_specs=[pl.BlockSpec((tm, D), lambda i, j: (i, 0)), _full((1, D)), pl.BlockSpec((D, tn), lambda i, j: (0, j))],
        out_specs=[pl.BlockSpec((tm, tn), lambda i, j: (i, j)), pl.BlockSpec((tm, DTW), lambda i, j: (i, 0)),
                   pl.BlockSpec((tm, D), lambda i, j: (i, 0))],
        out_shape=[jax.ShapeDtypeStruct((T, NP), BF16), jax.ShapeDtypeStruct((T, DTW), F32), jax.ShapeDtypeStruct((T, D), BF16)],
        scratch_shapes=[pltpu.VMEM((tm, D), BF16)], args=(x, norm_g, wp))


def _gmlp_tile():
    return 256


def _gmlp_fwd(proj, ln_g, ln_b, ws, bst):
    T = proj.shape[0]
    tm = min(T, _gmlp_tile())

    def body(u_ref, v_ref, z_ref, lg_ref, lb_ref, ws_ref, bs_ref, ya_ref, vn_s):
        vg = _gelu(v_ref[...].astype(F32))
        mu = jnp.mean(vg, axis=-1, keepdims=True)
        xc = vg - mu
        rstd = lax.rsqrt(jnp.mean(xc * xc, axis=-1, keepdims=True) + EPS)
        vn_s[...] = (xc * rstd * lg_ref[...] + lb_ref[...]).astype(BF16)
        for c in range(tm // CH):
            rs = slice(c * CH, (c + 1) * CH)
            for g in range(NG):
                cs_ = slice(g * 256, (g + 1) * 256)
                sv = _dot(ws_ref[g], vn_s[rs, cs_]) + bs_ref[g]
                z = z_ref[rs, cs_].astype(F32)
                ya_ref[rs, cs_] = (_gelu(u_ref[rs, cs_].astype(F32)) * sv * (z * _sigmoid(z))).astype(BF16)

    blk = lambda off: pl.BlockSpec((tm, D), lambda i: (i, off // D))
    return pl.pallas_call(
        body, name="gmlp_fwd", grid=(T // tm,),
        in_specs=[blk(OFF_U), blk(OFF_V), blk(OFF_ZA), _full((1, D)), _full((1, D)), _full((NG, CH, CH)), _full((NG, CH, 256))],
        out_specs=pl.BlockSpec((tm, D), lambda i: (i, 0)),
        out_shape=jax.ShapeDtypeStruct((T, D), BF16),
        scratch_shapes=[pltpu.VMEM((tm, D), BF16)],
        compiler_params=_cp(("parallel",)),
    )(proj, proj, proj, ln_g, ln_b, ws, bst)


def _gmlp_bwd(proj, dya, ln_g, ln_b, ws, wst, bst, gsel, ex):
    T = proj.shape[0]
    tm = min(T, _gmlp_tile())

    def body(u_ref, v_ref, z_ref, dy_ref, lg_ref, lb_ref, ws_ref, wst_ref, bs_ref, gsel_ref,
             d_ref, dws_ref, dbs_ref, dln_ref, vn_s, dsv_s, dvn_s):
        du_ref, dv_ref, dz_ref = d_ref.at[:, 0:D], d_ref.at[:, D:2 * D], d_ref.at[:, 2 * D:3 * D]
        @pl.when(pl.program_id(0) == 0)
        def _():
            dws_ref[...] = jnp.zeros_like(dws_ref)
            dbs_ref[...] = jnp.zeros_like(dbs_ref)
            dln_ref[...] = jnp.zeros_like(dln_ref)

        vg, dvg_dv = _gelu_and_grad(v_ref[...].astype(F32))
        mu = jnp.mean(vg, axis=-1, keepdims=True)
        xc = vg - mu
        rstd = lax.rsqrt(jnp.mean(xc * xc, axis=-1, keepdims=True) + EPS)
        vhat = xc * rstd
        vn_s[...] = (vhat * lg_ref[...] + lb_ref[...]).astype(BF16)
        ri = lax.broadcasted_iota(jnp.int32, (CH, CH), 0)
        ci = lax.broadcasted_iota(jnp.int32, (CH, CH), 1)
        tril = (ri >= ci).astype(F32)
        for c in range(tm // CH):
            rs = slice(c * CH, (c + 1) * CH)
            for g in range(NG):
                cs_ = slice(g * 256, (g + 1) * 256)
                vn = vn_s[rs, cs_]
                sv = _dot(ws_ref[g], vn) + bs_ref[g]
                z = z_ref[rs, cs_].astype(F32)
                sz = _sigmoid(z)
                ug, dug_du = _gelu_and_grad(u_ref[rs, cs_].astype(F32))
                dy = dy_ref[rs, cs_].astype(F32)
                t = dy * z * sz
                du_ref[rs, cs_] = (t * sv * dug_du).astype(BF16)
                dz_ref[rs, cs_] = (dy * ug * sv * sz * (1.0 + z * (1.0 - sz))).astype(BF16)
                dsv = (t * ug).astype(BF16)
                dsv_s[rs, cs_] = dsv
                dvn_s[rs, cs_] = _dot(wst_ref[g], dsv)
                dws_ref[g] += _dot_nt(dsv, vn) * tril
            dbs_ref[...] += _dot(dsv_s[rs, :], gsel_ref[...])
        dvn = dvn_s[...]
        dln_ref[0:1, :] += jnp.sum(dvn * vhat, axis=0, keepdims=True)
        dln_ref[1:2, :] += jnp.sum(dvn, axis=0, keepdims=True)
        dvh = dvn * lg_ref[...]
        dvg = rstd * (dvh - jnp.mean(dvh, axis=-1, keepdims=True) - vhat * jnp.mean(dvh * vhat, axis=-1, keepdims=True))
        dv_ref[...] = (dvg * dvg_dv).astype(BF16)

    blk = lambda off: pl.BlockSpec((tm, D), lambda i: (i, off // D))
    row = pl.BlockSpec((tm, D), lambda i: (i, 0))
    return _call(
        body, ex, name="gmlp_bwd", grid=(T // tm,),
        in_specs=[blk(OFF_U), blk(OFF_V), blk(OFF_ZA), row, _full((1, D)), _full((1, D)), _full((NG, CH, CH)),
                  _full((NG, CH, CH)), _full((NG, CH, 256)), _full((D, 128))],
        out_specs=[pl.BlockSpec((tm, 3 * D), lambda i: (i, 0)), _full((NG, CH, CH)), _full((CH, 128)), _full((8, D))],
        out_shape=[jax.ShapeDtypeStruct((T, 3 * D), BF16),
                   jax.ShapeDtypeStruct((NG, CH, CH), F32), jax.ShapeDtypeStruct((CH, 128), F32), jax.ShapeDtypeStruct((8, D), F32)],
        scratch_shapes=[pltpu.VMEM((tm, D), BF16), pltpu.VMEM((tm, D), BF16), pltpu.VMEM((tm, D), F32)],
        args=(proj, proj, proj, dya, ln_g, ln_b, ws, wst, bst, gsel))


def _conv_pre(x, xp, cw_ref, cb_ref):
    row = lax.broadcasted_iota(jnp.int32, (CH, 1), 0)
    pre = cb_ref[...] + cw_ref[3:4, :] * x
    for j in (1, 2, 3):
        pre = pre + cw_ref[3 - j:4 - j, :] * jnp.where(row >= j, pltpu.roll(x, j, 0), pltpu.roll(xp, j, 0))
    return pre


def _split_dot(x, w, parts, w_left=False):
    acc, r = None, x
    for k in range(parts):
        hi = r.astype(BF16)
        d = _dot(w, hi) if w_left else _dot(hi, w)
        acc = d if acc is None else acc + d
        if k + 1 < parts:
            r = r - hi.astype(F32)
    return acc


def _chunk_decays(dt, alog_ref, e_ref, cs_s, cst_s, csx_s):
    a = -jnp.exp(alog_ref[...])
    ri = lax.broadcasted_iota(jnp.int32, (CH, CH), 0)
    ci = lax.broadcasted_iota(jnp.int32, (CH, CH), 1)
    tril = ri >= ci
    cs = _split_dot(dt * a, tril.astype(BF16), 3, w_left=True)
    cs_s[...] = cs
    cst_s[...] = cs.T
    csx_s[...] = _split_dot(cs, e_ref[...], 3)
    return a, tril, ri, ci


def _lmat(cst_s, h, tril):
    rowb = jnp.broadcast_to(cst_s[h:h + 1, :], (CH, CH))
    return jnp.exp(jnp.where(tril, rowb.T - rowb, -jnp.inf))


def _head_pair_rows(v, lane):
    return jnp.concatenate([jnp.where(lane < HD, v, 0.0), jnp.where(lane < HD, 0.0, v)], axis=0).astype(BF16)


def _ssd_fwd(proj, dtr, cw, cb, dtb, alog, dskx, sg, e128):
    T = proj.shape[0]
    nc = T // CH

    def body(xbc_ref, zb_ref, dt_ref, cw_ref, cb_ref, dtb_ref, alog_ref, dx_ref, sg_ref, e_ref,
             y_ref, yb_ref, hp_ref, pre_ref, xprev_s, h_s, cs_s, cst_s, csx_s, yz_s):
        @pl.when(pl.program_id(0) == 0)
        def _():
            xprev_s[...] = jnp.zeros_like(xprev_s)
            h_s[...] = jnp.zeros_like(h_s)

        x = xbc_ref[...].astype(F32)
        pre = _conv_pre(x, xprev_s[...], cw_ref, cb_ref)
        pre_ref[...] = pre
        xprev_s[...] = x
        xc = pre * _sigmoid(pre)
        dt = _softplus(dt_ref[...] + dtb_ref[...])
        a, tril, _, lane = _chunk_decays(dt, alog_ref, e_ref, cs_s, cst_s, csx_s)
        dt_x = _split_dot(dt, e_ref[...], 2)
        cs_last_x = csx_s[CH - 1:CH, :]
        hp_ref[0] = h_s[...]
        for g in range(NG):
            gs = slice(g * 512, (g + 1) * 512)
            bg = xc[:, DI + g * NST:DI + (g + 1) * NST].astype(BF16)
            cg = xc[:, DI + 512 + g * NST:DI + 512 + (g + 1) * NST].astype(BF16)
            cbm = _dot_nt(cg, bg)
            xg = xc[:, gs]
            xdt = xg * dt_x[:, gs]
            hprev = h_s[:, gs]
            csx = csx_s[:, gs]
            yoff = _dot(cg, hprev.astype(BF16)) * jnp.exp(csx)
            st = _dot_tn(bg, (xdt * jnp.exp(cs_last_x[:, gs] - csx)).astype(BF16))
            h_s[:, gs] = jnp.exp(cs_last_x[:, gs]) * hprev + st
            ssq = jnp.zeros((CH, 1), F32)
            for q in range(4):
                h0 = g * 8 + 2 * q
                ps = slice(q * 128, (q + 1) * 128)
                cols = slice(g * 512 + q * 128, g * 512 + (q + 1) * 128)
                m01 = jnp.concatenate([cbm * _lmat(cst_s, h0, tril), cbm * _lmat(cst_s, h0 + 1, tril)], axis=1).astype(BF16)
                yq = _dot(m01, _head_pair_rows(xdt[:, ps], lane)) + yoff[:, ps] + xg[:, ps] * dx_ref[:, cols]
                y_ref[:, cols] = yq
                z = zb_ref[:, cols].astype(F32)
                yz = yq * z * _sigmoid(z)
                yz_s[:, cols] = yz
                ssq = ssq + jnp.sum(yz * yz, axis=1, keepdims=True)
            rg = lax.rsqrt(ssq * (1.0 / 512.0) + EPS)
            yb_ref[:, gs] = (yz_s[:, gs] * rg * sg_ref[:, gs]).astype(BF16)

    return pl.pallas_call(
        body, name="ssd_fwd", grid=(nc,),
        in_specs=[pl.BlockSpec((CH, CD), lambda c: (c, OFF_XBC // CD)), pl.BlockSpec((CH, DI), lambda c: (c, OFF_ZB // DI)),
                  pl.BlockSpec((CH, DTW), lambda c: (c, 0)), _full((4, CD)), _full((1, CD)), _full((1, DTW)),
                  _full((1, DTW)), _full((1, DI)), _full((1, DI)), _full((DTW, DI))],
        out_specs=[pl.BlockSpec((CH, DI), lambda c: (c, 0)), pl.BlockSpec((CH, DI), lambda c: (c, 0)),
                   pl.BlockSpec((1, NST, DI), lambda c: (c, 0, 0)), pl.BlockSpec((CH, CD), lambda c: (c, 0))],
        out_shape=[jax.ShapeDtypeStruct((T, DI), F32), jax.ShapeDtypeStruct((T, DI), BF16),
                   jax.ShapeDtypeStruct((nc, NST, DI), F32), jax.ShapeDtypeStruct((T, CD), F32)],
        scratch_shapes=[pltpu.VMEM((CH, CD), F32), pltpu.VMEM((NST, DI), F32), pltpu.VMEM((CH, CH), F32),
                        pltpu.VMEM((CH, CH), F32), pltpu.VMEM((CH, DI), F32), pltpu.VMEM((CH, DI), F32)],
        compiler_params=_cp(("arbitrary",)),
    )(proj, proj, dtr, cw, cb, dtb, alog, dskx, sg, e128)


def _ssd_bwd(proj, pre_all, dtr, y, dyb, hprev_all, cw, dtb, alog, dskx, sg, e128, et128, ex):
    T = proj.shape[0]
    nc = T // CH

    def body(xbc_ref, pre_ref, zb_ref, dt_ref, y_ref, dyb_ref, hp_ref, cw_ref, dtb_ref, alog_ref, dx_ref, sg_ref,
             e_ref, et_ref, d_ref, ddt_ref, dcw_ref, dsm_ref, dsg_ref,
             g_s, dpn_s, cs_s, cst_s, csx_s, dy_s, dxdt_s, dxs_s, dsd_s, dxc_s, gh_s):
        dxbc_ref, dzb_ref = d_ref.at[:, 0:CD], d_ref.at[:, CD:CD + DI]
        i = pl.program_id(0)

        @pl.when(i == 0)
        def _():
            g_s[...] = jnp.zeros_like(g_s)
            dpn_s[...] = jnp.zeros_like(dpn_s)
            dcw_ref[...] = jnp.zeros_like(dcw_ref)
            dsm_ref[...] = jnp.zeros_like(dsm_ref)
            dsg_ref[...] = jnp.zeros_like(dsg_ref)

        pre = pre_ref[...]
        sp = _sigmoid(pre)
        xc = pre * sp
        dtr = dt_ref[...] + dtb_ref[...]
        dt = _softplus(dtr)
        a, tril, ri, lane = _chunk_decays(dt, alog_ref, e_ref, cs_s, cst_s, csx_s)
        et = et_ref[...]
        dt_x = _split_dot(dt, e_ref[...], 2)
        cs_last_x = csx_s[CH - 1:CH, :]

        for g in range(NG):
            gs = slice(g * 512, (g + 1) * 512)
            z = zb_ref[:, gs].astype(F32)
            sz = _sigmoid(z)
            yv = y_ref[:, gs]
            yz = yv * z * sz
            rg = lax.rsqrt(jnp.mean(yz * yz, axis=-1, keepdims=True) + EPS)
            yn = yz * rg
            dyb = dyb_ref[:, gs].astype(F32)
            dsg_ref[0:1, gs] += jnp.sum(dyb * yn, axis=0, keepdims=True)
            dyn = dyb * sg_ref[:, gs]
            dyz = rg * (dyn - yn * jnp.mean(dyn * yn, axis=-1, keepdims=True))
            dy_s[:, gs] = dyz * z * sz
            dzb_ref[:, gs] = (dyz * yv * sz * (1.0 + z * (1.0 - sz))).astype(BF16)

        rsum = jnp.zeros((CH, DTW), F32)
        csum_t = jnp.zeros((DTW, CH), F32)
        for g in range(NG):
            gs = slice(g * 512, (g + 1) * 512)
            bg = xc[:, DI + g * NST:DI + (g + 1) * NST].astype(BF16)
            cg = xc[:, DI + 512 + g * NST:DI + 512 + (g + 1) * NST].astype(BF16)
            cbm = _dot_nt(cg, bg)
            xdt = xc[:, gs] * dt_x[:, gs]
            hprev = hp_ref[0, :, gs]
            hpb = hprev.astype(BF16)
            gn = g_s[:, gs]
            gnb = gn.astype(BF16)
            dy = dy_s[:, gs]
            csx = csx_s[:, gs]
            ecs = jnp.exp(csx)
            dec = jnp.exp(cs_last_x[:, gs] - csx)
            dye = (dy * ecs).astype(BF16)
            dc = _dot_nt(dye, hpb)
            dprev = _dot_tn(cg, dye)
            dxdt_state = dec * _dot(bg, gnb)
            db = _dot_nt((xdt * dec).astype(BF16), gnb)
            dcb = jnp.zeros((CH, CH), F32)
            for q in range(4):
                h0 = g * 8 + 2 * q
                ps = slice(q * 128, (q + 1) * 128)
                dyp = dy[:, ps]
                l0 = _lmat(cst_s, h0, tril)
                l1 = _lmat(cst_s, h0 + 1, tril)
                m0 = cbm * l0
                m1 = cbm * l1
                dm = _dot_nt(dyp.astype(BF16), _head_pair_rows(xdt[:, ps], lane))
                dm0 = dm[:, :CH]
                dm1 = dm[:, CH:]
                dcb = dcb + dm0 * l0 + dm1 * l1
                for hh, qm in ((h0, dm0 * m0), (h0 + 1, dm1 * m1)):
                    rsum = jnp.where(lane == hh, jnp.sum(qm, axis=1, keepdims=True), rsum)
                    csum_t = jnp.where(ri == hh, jnp.sum(qm, axis=0, keepdims=True), csum_t)
                mst = jnp.concatenate([m0, m1], axis=0).astype(BF16)
                d = _dot_tn(mst, _head_pair_rows(dyp, lane))
                dxdt_s[:, g * 512 + q * 128:g * 512 + (q + 1) * 128] = d + dxdt_state[:, ps]
            yoff = _dot(cg, hpb) * ecs
            dsd_s[:, gs] = xdt * dxdt_state
            dxs_s[:, gs] = dy * yoff
            dcbb = dcb.astype(BF16)
            dxc_s[:, DI + 512 + g * NST:DI + 512 + (g + 1) * NST] = dc + _dot(dcbb, bg)
            dxc_s[:, DI + g * NST:DI + (g + 1) * NST] = db + _dot_tn(dcbb, cg)
            gh_s[:, gs] = jnp.broadcast_to(jnp.sum(gn * hprev, axis=0, keepdims=True), (8, 512))
            g_s[:, gs] = dprev + jnp.exp(cs_last_x[:, gs]) * gn

        xs = xc[:, :DI]
        dy = dy_s[...]
        dxdt = dxdt_s[...]
        cs_last = cs_s[CH - 1:CH, :]
        state_e = _split_dot(dsd_s[...], et, 2)
        dcd = 0.125 * jnp.sum(_split_dot(gh_s[...], et, 2), axis=0, keepdims=True) * jnp.exp(cs_last)
        row = lax.broadcasted_iota(jnp.int32, (CH, 1), 0)
        dcs = rsum - csum_t.T + _split_dot(dxs_s[...], et, 2) - state_e
        dcs = dcs + jnp.where(row == CH - 1, jnp.sum(state_e, axis=0, keepdims=True) + dcd, 0.0)
        dda = _split_dot(dcs, (lane >= ri).astype(BF16), 3, w_left=True)
        ddt = dda * a + _dot((dxdt * xs).astype(BF16), et)
        ddtr = jnp.where(lane < NH, ddt * _sigmoid(dtr), 0.0)
        ddt_ref[...] = ddtr.astype(BF16)
        dsm_ref[0:1, :] += jnp.sum(ddtr, axis=0, keepdims=True)
        dsm_ref[1:2, :] += jnp.sum(dda * dt, axis=0, keepdims=True) * a
        dsm_ref[2:3, :] += jnp.sum(_dot((dy * xs).astype(BF16), et), axis=0, keepdims=True)
        dxc_s[:, :DI] = dxdt * dt_x + dy * dx_ref[...]

        dpre = dxc_s[...] * sp * (1.0 + pre * (1.0 - sp))
        dpn = dpn_s[...]
        x = xbc_ref[...].astype(F32)
        dcw_ref[4:5, :] += jnp.sum(dpre, axis=0, keepdims=True)
        dxbc = cw_ref[3:4, :] * dpre
        dcw_ref[3:4, :] += jnp.sum(dpre * x, axis=0, keepdims=True)
        for j in (1, 2, 3):
            ahead = jnp.where(row < CH - j, pltpu.roll(dpre, CH - j, 0), pltpu.roll(dpn, CH - j, 0))
            dcw_ref[3 - j:4 - j, :] += jnp.sum(ahead * x, axis=0, keepdims=True)
            dxbc = dxbc + cw_ref[3 - j:4 - j, :] * ahead
        dpn_s[...] = dpre
        dxbc_ref[...] = dxbc.astype(BF16)

    rev = lambda c: nc - 1 - c
    return _call(
        body, ex, name="ssd_bwd", grid=(nc,),
        in_specs=[pl.BlockSpec((CH, CD), lambda c: (rev(c), OFF_XBC // CD)),
                  pl.BlockSpec((CH, CD), lambda c: (rev(c), 0)),
                  pl.BlockSpec((CH, DI), lambda c: (rev(c), OFF_ZB // DI)),
                  pl.BlockSpec((CH, DTW), lambda c: (rev(c), 0)),
                  pl.BlockSpec((CH, DI), lambda c: (rev(c), 0)), pl.BlockSpec((CH, DI), lambda c: (rev(c), 0)),
                  pl.BlockSpec((1, NST, DI), lambda c: (rev(c), 0, 0)),
                  _full((4, CD)), _full((1, DTW)), _full((1, DTW)), _full((1, DI)), _full((1, DI)),
                  _full((DTW, DI)), _full((DI, DTW))],
        out_specs=[pl.BlockSpec((CH, CD + DI), lambda c: (rev(c), 0)),
                   pl.BlockSpec((CH, DTW), lambda c: (rev(c), 0)), _full((8, CD)), _full((8, DTW)), _full((8, DI))],
        out_shape=[jax.ShapeDtypeStruct((T, CD + DI), BF16), jax.ShapeDtypeStruct((T, DTW), BF16),
                   jax.ShapeDtypeStruct((8, CD), F32), jax.ShapeDtypeStruct((8, DTW), F32), jax.ShapeDtypeStruct((8, DI), F32)],
        scratch_shapes=[pltpu.VMEM((NST, DI), F32), pltpu.VMEM((CH, CD), F32), pltpu.VMEM((CH, CH), F32), pltpu.VMEM((CH, CH), F32),
                        pltpu.VMEM((CH, DI), F32), pltpu.VMEM((CH, DI), F32), pltpu.VMEM((CH, DI), F32), pltpu.VMEM((CH, DI), F32),
                        pltpu.VMEM((CH, DI), F32), pltpu.VMEM((CH, CD), F32), pltpu.VMEM((8, DI), F32)],
        args=(proj, pre_all, proj, dtr, y, dyb, hprev_all, cw, dtb, alog, dskx, sg, e128, et128))


def _merge_tile(backward=False):
    return 512 if backward else 256


def _merge_fwd(x, ya, yb, proj, p, tgt, w_oa, w_ob, w_out, w_pg, w_ple, w_pgt, ple_g, fin_g):
    T = x.shape[0]
    tm = min(T, _merge_tile())

    def body(x_ref, ya_ref, yb_ref, ga_ref, gb_ref, p_ref, t_ref, woa, wob, wout, wpg, wple, wpgt, pg_ref, fg_ref,
             dx1_ref, dx1b_ref, oa_ref, ob_ref, mg_ref, hp_ref, dpre_ref, dpe_ref, acc_ref):
        @pl.when(pl.program_id(0) == 0)
        def _():
            acc_ref[...] = jnp.zeros_like(acc_ref)

        oa = _dot(ya_ref[...], woa[...])
        ob = _dot(yb_ref[...], wob[...])
        oa_ref[...] = oa.astype(BF16)
        ob_ref[...] = ob.astype(BF16)
        mg = _sigmoid(ga_ref[...].astype(F32)) * oa + _sigmoid(gb_ref[...].astype(F32)) * ob
        mgb = mg.astype(BF16)
        mg_ref[...] = mgb
        x1 = x_ref[...] + _dot(mgb, wout[...])
        r2 = lax.rsqrt(jnp.mean(x1 * x1, axis=-1, keepdims=True) + EPS)
        xh1 = x1 * r2
        hpb = (xh1 * pg_ref[...]).astype(BF16)
        hp_ref[...] = hpb
        gate = _sigmoid(_dot(hpb, wpg[...]))
        pe = _dot(p_ref[...].astype(BF16), wple[...])
        x2 = x1 + gate * pe
        r3 = lax.rsqrt(jnp.mean(x2 * x2, axis=-1, keepdims=True) + EPS)
        xh2 = x2 * r3
        err = xh2 * fg_ref[...] - t_ref[...]
        acc_ref[2:3, :] += 0.5 * jnp.sum(jnp.mean(err * err, axis=-1, keepdims=True))
        dyo = err * (1.0 / D)
        acc_ref[0:1, :] += jnp.sum(dyo * xh2, axis=0, keepdims=True)
        dn = dyo * fg_ref[...]
        dx2 = r3 * (dn - xh2 * jnp.mean(dn * xh2, axis=-1, keepdims=True))
        dpe_ref[...] = (dx2 * gate).astype(BF16)
        dpre = (dx2 * pe * gate * (1.0 - gate)).astype(BF16)
        dpre_ref[...] = dpre
        dhp = _dot(dpre, wpgt[...])
        acc_ref[1:2, :] += jnp.sum(dhp * xh1, axis=0, keepdims=True)
        dhn = dhp * pg_ref[...]
        dx1 = dx2 + r2 * (dhn - xh1 * jnp.mean(dhn * xh1, axis=-1, keepdims=True))
        dx1_ref[...] = dx1
        dx1b_ref[...] = dx1.astype(BF16)

    row = lambda w: pl.BlockSpec((tm, w), lambda i: (i, 0))
    wsp = lambda s: pl.BlockSpec(s, lambda i: (0, 0), pipeline_mode=pl.Buffered(1))
    return pl.pallas_call(
        body, name="merge_fwd", grid=(T // tm,),
        in_specs=[row(D), row(D), row(DI), pl.BlockSpec((tm, D), lambda i: (i, OFF_GA // D)),
                  pl.BlockSpec((tm, D), lambda i: (i, OFF_GB // D)), row(PLE), row(D),
                  wsp((D, D)), wsp((DI, D)), wsp((D, D)), wsp((D, D)), wsp((PLE, D)), wsp((D, D)), _full((1, D)), _full((1, D))],
        out_specs=[row(D)] * 8 + [_full((8, D))],
        out_shape=[jax.ShapeDtypeStruct((T, D), F32)] + [jax.ShapeDtypeStruct((T, D), BF16)] * 7 + [jax.ShapeDtypeStruct((8, D), F32)],
        compiler_params=_cp(("arbitrary",)),
    )(x, ya, yb, proj, proj, p, tgt, w_oa, w_ob, w_out, w_pg, w_ple, w_pgt, ple_g, fin_g)


def _merge_bwd(dx1, oa, ob, proj, w_outt, w_oat, w_obt):
    T = dx1.shape[0]
    tm = min(T, _merge_tile(backward=True))

    def body(dx1_ref, oa_ref, ob_ref, ga_ref, gb_ref, woutt, woat, wobt, doa_ref, dob_ref, dya_ref, dyb_ref, dg_ref):
        dmg = _dot(dx1_ref[...].astype(BF16), woutt[...])
        sa = _sigmoid(ga_ref[...].astype(F32))
        sb = _sigmoid(gb_ref[...].astype(F32))
        doa = (dmg * sa).astype(BF16)
        dob = (dmg * sb).astype(BF16)
        doa_ref[...] = doa
        dob_ref[...] = dob
        dg_ref[:, :D] = (dmg * oa_ref[...].astype(F32) * sa * (1.0 - sa)).astype(BF16)
        dg_ref[:, D:] = (dmg * ob_ref[...].astype(F32) * sb * (1.0 - sb)).astype(BF16)
        dya_ref[...] = _dot(doa, woat[...]).astype(BF16)
        dyb_ref[...] = _dot(dob, wobt[...]).astype(BF16)

    row = lambda w: pl.BlockSpec((tm, w), lambda i: (i, 0))
    wsp = lambda s: pl.BlockSpec(s, lambda i: (0, 0), pipeline_mode=pl.Buffered(1))
    return pl.pallas_call(
        body, name="merge_bwd", grid=(T // tm,),
        in_specs=[row(D), row(D), row(D), pl.BlockSpec((tm, D), lambda i: (i, OFF_GA // D)),
                  pl.BlockSpec((tm, D), lambda i: (i, OFF_GB // D)), wsp((D, D)), wsp((D, D)), wsp((D, DI))],
        out_specs=[row(D), row(D), row(D), row(DI), row(2 * D)],
        out_shape=[jax.ShapeDtypeStruct((T, D), BF16)] * 3 + [jax.ShapeDtypeStruct((T, DI), BF16), jax.ShapeDtypeStruct((T, 2 * D), BF16)],
        compiler_params=_cp(("parallel",)),
    )(dx1, oa, ob, proj, proj, w_outt, w_oat, w_obt)


def _wgrad(a, b, name):
    T, K = a.shape
    N = b.shape[1]
    tt, tk, tn = min(T, 1024), min(K, 1024), min(N, 1024)
    nt = T // tt

    def body(a_ref, b_ref, o_ref, acc_s):
        t = pl.program_id(2)

        @pl.when(t == 0)
        def _():
            acc_s[...] = jnp.zeros_like(acc_s)

        acc_s[...] += _dot_tn(a_ref[...].astype(BF16), b_ref[...])

        @pl.when(t == nt - 1)
        def _():
            o_ref[...] = acc_s[...].astype(BF16)

    return pl.pallas_call(
        body, name=name, grid=(K // tk, N // tn, nt),
        in_specs=[pl.BlockSpec((tt, tk), lambda k, n, t: (t, k)), pl.BlockSpec((tt, tn), lambda k, n, t: (t, n))],
        out_specs=pl.BlockSpec((tk, tn), lambda k, n, t: (k, n)),
        out_shape=jax.ShapeDtypeStruct((K, N), BF16),
        scratch_shapes=[pltpu.VMEM((tk, tn), F32)],
        compiler_params=_cp(("parallel", "parallel", "arbitrary")),
    )(a, b)


def _proj_bwd(x, dx1, norm_g, wt, pieces, ddt, ex):
    T = x.shape[0]
    tm = min(T, 1024)
    nk = OFF_DT // D + 1
    starts = [sum(a.shape[1] for a in pieces[:n]) // D for n in range(len(pieces))]
    ranges = [(s, s + a.shape[1] // D) for s, a in zip(starts, pieces)]
    assert ranges[-1][1] == nk - 1
    npc = len(pieces)

    def body(x_ref, dx1_ref, g_ref, w_ref, wdt_ref, *rest):
        piece_refs, ddt_ref, gx_ref, dng_ref, acc_s = rest[:npc], rest[npc], rest[npc + 1], rest[npc + 2], rest[npc + 3]
        i, k = pl.program_id(0), pl.program_id(1)

        @pl.when((i == 0) & (k == 0))
        def _():
            dng_ref[...] = jnp.zeros_like(dng_ref)

        @pl.when(k == 0)
        def _():
            acc_s[...] = jnp.zeros_like(acc_s)

        for ref, (lo, hi) in zip(piece_refs, ranges):
            @pl.when((k >= lo) & (k < hi))
            def _(ref=ref):
                acc_s[...] += _dot(ref[...], w_ref[...])

        @pl.when(k == nk - 1)
        def _():
            dh = acc_s[...] + _dot(ddt_ref[...], wdt_ref[...])
            xf = x_ref[...]
            r = lax.rsqrt(jnp.mean(xf * xf, axis=-1, keepdims=True) + EPS)
            xh = xf * r
            dng_ref[0:1, :] += jnp.sum(dh * xh, axis=0, keepdims=True)
            dxn = dh * g_ref[...]
            gx_ref[...] = dx1_ref[...] + r * (dxn - xh * jnp.mean(dxn * xh, axis=-1, keepdims=True))

    def piece_spec(lo, hi):
        return pl.BlockSpec((tm, D), lambda i, k: (i, jnp.clip(k - lo, 0, hi - lo - 1)))

    row = pl.BlockSpec((tm, D), lambda i, k: (i, 0))
    return _call(
        body, ex, name="proj_bwd", grid=(T // tm, nk),
        in_specs=[row, row, _full((1, D)), pl.BlockSpec((D, D), lambda i, k: (jnp.minimum(k, nk - 2), 0)),
                  pl.BlockSpec((DTW, D), lambda i, k: (OFF_DT // DTW, 0))]
        + [piece_spec(lo, hi) for lo, hi in ranges] + [pl.BlockSpec((tm, DTW), lambda i, k: (i, 0))],
        out_specs=[row, _full((8, D))],
        out_shape=[jax.ShapeDtypeStruct((T, D), F32), jax.ShapeDtypeStruct((8, D), F32)],
        scratch_shapes=[pltpu.VMEM((tm, D), F32)],
        args=(x, dx1, norm_g, wt, wt, *pieces, ddt))


def _elementwise_tile(R, C, limit=1 << 20):
    if R * C * 4 <= limit:
        return R, C
    rows = [t for t in range(16, R, 16) if R % t == 0 and t * C * 4 <= limit]
    if rows:
        return rows[-1], C
    cols = [t for t in range(128, C, 128) if C % t == 0 and R * t * 4 <= limit]
    return R, cols[-1]


def _adam_update(w, m, v, g):
    c1 = 1.0 - ADAM_B1 ** ADAM_STEP
    c2 = 1.0 - ADAM_B2 ** ADAM_STEP
    mm = ADAM_B1 * m + (1.0 - ADAM_B1) * g
    vv = ADAM_B2 * v + (1.0 - ADAM_B2) * (g * g)
    return -ADAM_LR * ((mm / c1) / (jnp.sqrt(vv / c2) + ADAM_EPS) + ADAM_WD * w), mm, vv


def _adamw(w, m, v, parts, name, part_row=0, block_rows=None):
    R, C = w.shape
    tr, tc = _elementwise_tile(R, C) if block_rows is None else (block_rows, C)
    assert R % tr == 0 and part_row % tr == 0
    first = part_row // tr
    n = len(parts)
    wspec = pl.BlockSpec((tr, tc), lambda i, j: (i, j))
    flat = pl.BlockSpec((tr, tc), lambda i, j: (first + i, j))
    slot = lambda k: pl.BlockSpec((None, tr, tc), lambda i, j: (k, first + i, j))
    part_specs = [slot(p[1]) if isinstance(p, tuple) else flat for p in parts]
    part_arrays = [p[0] if isinstance(p, tuple) else p for p in parts]

    def body(*refs):
        w_ref, m_ref, v_ref = refs[:3]
        g_ref, d_ref, nm_ref, nv_ref = refs[3 + n:]
        g = refs[3][...].astype(F32)
        for r in refs[4:3 + n]:
            g = g + r[...].astype(F32)
        g_ref[...] = g
        d_ref[...], nm_ref[...], nv_ref[...] = _adam_update(w_ref[...], m_ref[...], v_ref[...], g)

    return pl.pallas_call(
        body, name=name, grid=(R // tr, C // tc), in_specs=[wspec] * 3 + part_specs, out_specs=[wspec] * 4,
        out_shape=[jax.ShapeDtypeStruct(w.shape, F32)] * 4, compiler_params=_cp(("parallel", "parallel")),
    )(w, m, v, *part_arrays)


_SMALL_WIDE = {"norm_g": 0, "ln_a_g": 8, "ln_a_b": 9, "final_g": 16, "ple_norm_g": 17, "ssm_norm_g": 24, "conv_b": 52}
_WIDE_CONV_W, _WIDE_ROWS = 40, 64
_SMALL_NARROW = {"w_s": (0, NG * CH, 128), "b_s": (512, NG, 128), "dt_bias": (520, 1, NH), "a_log": (521, 1, NH),
                 "d_skip": (522, 1, NH)}
_NARROW_ROWS = 528
_SMALL_SHAPES = {"norm_g": (1, D), "ln_a_g": (1, D), "ln_a_b": (1, D), "ple_norm_g": (1, D), "final_g": (1, D),
                 "ssm_norm_g": (1, DI), "conv_b": (1, CD), "w_s": (NG * CH, CH), "b_s": (NG, CH), "dt_bias": (1, NH),
                 "a_log": (1, NH), "d_skip": (1, NH)}


def _adamw_small(w, m, v, wide_all, narrow_all):
    names = list(_SMALL_SHAPES)
    n = len(names)

    def body(*refs):
        wr, mr, vr = refs[:n], refs[n:2 * n], refs[2 * n:3 * n]
        wide_ref, narrow_ref = refs[3 * n], refs[3 * n + 1]
        outs = refs[3 * n + 2:]
        gr, dr, nmr, nvr, cw_ref = outs[:n], outs[n:2 * n], outs[2 * n:3 * n], outs[3 * n:4 * n], outs[4 * n]

        def total(ref, rows, lanes):
            acc = ref[0, rows, lanes]
            for d in range(1, NDEV):
                acc = acc + ref[d, rows, lanes]
            return acc

        for k, name in enumerate(names):
            if name in _SMALL_WIDE:
                for part in range(_SMALL_SHAPES[name][1] // D):
                    r = _SMALL_WIDE[name] + part
                    cols = slice(part * D, (part + 1) * D)
                    g = total(wide_ref, slice(r, r + 1), slice(None))
                    gr[k][:, cols] = g
                    dr[k][:, cols], nmr[k][:, cols], nvr[k][:, cols] = _adam_update(wr[k][:, cols], mr[k][:, cols], vr[k][:, cols], g)
            else:
                r, rows, lanes = _SMALL_NARROW[name]
                g = total(narrow_ref, slice(r, r + rows), slice(0, lanes))
                gr[k][...] = g
                dr[k][...], nmr[k][...], nvr[k][...] = _adam_update(wr[k][...], mr[k][...], vr[k][...], g)
        cw_ref[...] = total(wide_ref, slice(_WIDE_CONV_W, _WIDE_CONV_W + 12), slice(None))

    shapes = [jax.ShapeDtypeStruct(_SMALL_SHAPES[k], F32) for k in names]
    specs = [_full(_SMALL_SHAPES[k]) for k in names]
    res = pl.pallas_call(
        body, name="adamw_small", grid=(1,),
        in_specs=specs * 3 + [_full(wide_all.shape), _full(narrow_all.shape)],
        out_specs=specs * 4 + [_full((12, D))], out_shape=shapes * 4 + [jax.ShapeDtypeStruct((12, D), F32)],
        compiler_params=_cp(("arbitrary",)),
    )(*[w[k] for k in names], *[m[k] for k in names], *[v[k] for k in names], wide_all, narrow_all)
    groups = [dict(zip(names, res[q * n:(q + 1) * n])) for q in range(4)]
    return groups[0], groups[1], groups[2], groups[3], res[4 * n]


def _pair_sum(mine, theirs, core, name):
    Q, R, C = theirs.shape
    tr, tc = _elementwise_tile(R, C)

    def body(core_ref, a_ref, b_ref, o_ref):
        o_ref[...] = (a_ref[...].astype(F32) + b_ref[...].astype(F32)).astype(BF16)

    spec = pl.BlockSpec((None, tr, tc), lambda q, i, j, core_ref: (q, i, j))
    own = pl.BlockSpec((None, tr, tc), lambda q, i, j, core_ref: (2 * q + core_ref[0], i, j))
    return pl.pallas_call(
        body, name=name,
        grid_spec=pltpu.PrefetchScalarGridSpec(num_scalar_prefetch=1, grid=(Q, R // tr, C // tc), in_specs=[own, spec], out_specs=spec),
        out_shape=jax.ShapeDtypeStruct((Q, R, C), BF16), compiler_params=_cp(("parallel", "parallel", "parallel")),
    )(core, mine, theirs)


def _dev_index(px, py, pc):
    return 4 * px + 2 * py + pc


def _mesh_position():
    return lax.axis_index("x"), lax.axis_index("y"), lax.axis_index("c")


def _gather_exchange(blocks):
    n = len(blocks)

    def make(ins, outs, sems):
        send_sems, recv_sems, local_sems = sems
        x, y, c = _mesh_position()
        me, sibling = (x, y, c), (x, y, 1 - c)
        chips = [(1 - x, y), (x, 1 - y), (1 - x, 1 - y)]

        def copy(a, k, block, to, src=None):
            dst = outs[a].at[_dev_index(*block)]
            return pltpu.make_async_remote_copy(src_ref=dst if src is None else src, dst_ref=dst, send_sem=send_sems.at[a, k],
                                                recv_sem=recv_sems.at[a, k], device_id=to, device_id_type=MESH)

        mine = [pltpu.make_async_copy(ins[a], outs[a].at[_dev_index(*me)], local_sems.at[a]) for a in range(n)]
        first = []
        for a in range(n):
            first.append(copy(a, 0, me, sibling, src=ins[a]))
            first += [copy(a, 1 + j, me, (*chip, c), src=ins[a]) for j, chip in enumerate(chips)]

        def start():
            for cp in mine + first:
                cp.start()

        def finish():
            passed = []
            for j, chip in enumerate(chips):
                for a in range(n):
                    copy(a, 1 + j, (*chip, c), me).wait_recv()
                    fwd = copy(a, 4 + j, (*chip, c), sibling)
                    fwd.start()
                    passed.append(fwd)
            for a in range(n):
                copy(a, 0, sibling, me).wait_recv()
                for j, chip in enumerate(chips):
                    copy(a, 4 + j, (*chip, 1 - c), me).wait_recv()
            for cp in first + passed:
                cp.wait_send()
            for cp in mine:
                cp.wait()

        return start, finish

    return _Exchange(list(blocks), [jax.ShapeDtypeStruct((NDEV,) + b.shape, b.dtype) for b in blocks],
                     [pltpu.SemaphoreType.DMA((n, 7)), pltpu.SemaphoreType.DMA((n, 7)), pltpu.SemaphoreType.DMA((n,))], make)


def _start_wait_all(copies, local=()):
    def start():
        for cp in list(local) + list(copies):
            cp.start()

    def finish():
        for cp in copies:
            cp.wait()
        for cp in local:
            cp.wait()

    return start, finish


def _sibling_exchange(grads):
    n = len(grads)

    def make(ins, outs, sems):
        send_sems, recv_sems = sems
        x, y, c = _mesh_position()
        return _start_wait_all([pltpu.make_async_remote_copy(
            src_ref=ins[a].at[2 * q + (1 - c)], dst_ref=outs[a].at[q], send_sem=send_sems.at[a, q], recv_sem=recv_sems.at[a, q],
            device_id=(x, y, 1 - c), device_id_type=MESH) for a in range(n) for q in range(4)])

    return _Exchange(list(grads), [jax.ShapeDtypeStruct((4,) + g.shape[1:], g.dtype) for g in grads],
                     [pltpu.SemaphoreType.DMA((n, 4)), pltpu.SemaphoreType.DMA((n, 4))], make)


def _chips_exchange(partials):
    n = len(partials)

    def make(ins, outs, sems):
        send_sems, recv_sems = sems
        x, y, c = _mesh_position()
        chips = [(1 - x, y), (x, 1 - y), (1 - x, 1 - y)]
        return _start_wait_all([pltpu.make_async_remote_copy(
            src_ref=ins[a].at[2 * px + py], dst_ref=outs[a].at[k], send_sem=send_sems.at[a, k], recv_sem=recv_sems.at[a, k],
            device_id=(px, py, c), device_id_type=MESH) for a in range(n) for k, (px, py) in enumerate(chips)])

    return _Exchange(list(partials), [jax.ShapeDtypeStruct((3,) + g.shape[1:], g.dtype) for g in partials],
                     [pltpu.SemaphoreType.DMA((n, 3)), pltpu.SemaphoreType.DMA((n, 3))], make)


def _direct_gather_exchange(smalls):
    n = len(smalls)

    def make(ins, outs, sems):
        send_sems, recv_sems, local_sems = sems
        x, y, c = _mesh_position()
        copies, local = [], []
        for a in range(n):
            slot = outs[a].at[_dev_index(x, y, c)]
            local.append(pltpu.make_async_copy(ins[a], slot, local_sems.at[a]))
            for r in range(1, NDEV):
                peer = (x ^ ((r >> 2) & 1), y ^ ((r >> 1) & 1), c ^ (r & 1))
                copies.append(pltpu.make_async_remote_copy(src_ref=ins[a], dst_ref=slot, send_sem=send_sems.at[a, r - 1],
                                                           recv_sem=recv_sems.at[a, r - 1], device_id=peer, device_id_type=MESH))
        return _start_wait_all(copies, local=local)

    return _Exchange(list(smalls), [jax.ShapeDtypeStruct((NDEV,) + s.shape, s.dtype) for s in smalls],
                     [pltpu.SemaphoreType.DMA((n, 7)), pltpu.SemaphoreType.DMA((n, 7)), pltpu.SemaphoreType.DMA((n,))], make)


_W_IN_ROWS = {"u": (0, 1024), "v": (1024, 2048), "za": (2048, 3072), "zb": (3072, 5120), "xbc": (5120, 8192),
              "dt": (8192, 8224), "ga": (8224, 9248), "gb": (9248, 10272)}
_FWD_ORDER = ("xbc", "u", "zb", "v", "za", "ga", "gb", "dt")
_BWD_ORDER = ("xbc", "zb", "u", "v", "za", "ga", "gb", "dt")


def _w_in_t_rows(wt, order):
    assert order[-1] == "dt"
    z = jnp.zeros((NP - NIN, wt.shape[1]), wt.dtype)
    return jnp.concatenate([wt[slice(*_W_IN_ROWS[n])] for n in order] + [z], axis=0)


_WEIGHTS = ["norm_g", "w_in", "ln_a_g", "ln_a_b", "w_s", "b_s", "conv_w", "conv_b", "dt_bias", "a_log", "d_skip", "ssm_norm_g",
            "w_oa", "w_ob", "w_out", "ple_norm_g", "w_pg", "w_ple", "final_g"]


def _rows_pack(d, dtype):
    return jnp.concatenate([d["w_oa"].reshape(128, D), d["w_ob"].reshape(256, D), d["w_out"].reshape(128, D),
                            d["w_pg"].reshape(128, D), d["w_ple"].reshape(32, D)], axis=0).astype(dtype)


def kernel(x, p, norm_g, w_in, ln_a_g, ln_a_b, w_s, b_s, conv_w, conv_b, dt_bias, a_log, d_skip, ssm_norm_g, w_oa, w_ob, w_out, ple_norm_g, w_pg, w_ple, final_g, loss_target, m_norm_g, m_w_in, m_ln_a_g, m_ln_a_b, m_w_s, m_b_s, m_conv_w, m_conv_b, m_dt_bias, m_a_log, m_d_skip, m_ssm_norm_g, m_w_oa, m_w_ob, m_w_out, m_ple_norm_g, m_w_pg, m_w_ple, m_final_g, v_norm_g, v_w_in, v_ln_a_g, v_ln_a_b, v_w_s, v_b_s, v_conv_w, v_conv_b, v_dt_bias, v_a_log, v_d_skip, v_ssm_norm_g, v_w_oa, v_w_ob, v_w_out, v_ple_norm_g, v_w_pg, v_w_ple, v_final_g):
    args = dict(locals())
    w = {n: args[n] for n in _WEIGHTS}
    m = {n: args["m_" + n] for n in _WEIGHTS}
    v = {n: args["v_" + n] for n in _WEIGHTS}
    T = x.shape[1]
    xi, yi, ci = lax.axis_index("x"), lax.axis_index("y"), lax.axis_index("c")
    me = 4 * xi + 2 * yi + ci

    chip = 2 * xi + yi
    x2, p2, tgt = x.reshape(T, D), p.reshape(T, PLE), loss_target.reshape(T, D)

    norm_g2 = w["norm_g"].reshape(1, D)
    ws = jnp.where(jnp.tril(jnp.ones((CH, CH), bool))[None], w["w_s"].reshape(NG, CH, CH), 0.0).astype(BF16)
    wst = jnp.transpose(ws, (0, 2, 1))
    bst = jnp.broadcast_to(w["b_s"].reshape(NG, CH, 1), (NG, CH, 256))
    ln_g, ln_b = w["ln_a_g"].reshape(1, D), w["ln_a_b"].reshape(1, D)
    cb = w["conv_b"].reshape(1, CD)
    pad32 = lambda a: jnp.pad(a.reshape(1, NH), ((0, 0), (0, DTW - NH)))
    dtb, alog = pad32(w["dt_bias"]), pad32(w["a_log"])
    dskx = jnp.repeat(w["d_skip"].reshape(NH), HD).reshape(1, DI)
    sg = w["ssm_norm_g"].reshape(1, DI)
    ple_g, fin_g = w["ple_norm_g"].reshape(1, D), w["final_g"].reshape(1, D)
    e128 = (jnp.arange(DTW)[:, None] == (jnp.arange(DI)[None, :] // HD)).astype(BF16)
    et128 = e128.T
    gsel = ((jnp.arange(D)[:, None] // 256) == jnp.arange(128)[None, :]).astype(BF16)

    w_in_t = lambda a: jnp.transpose(a.reshape(D, WSH))
    (a_all,) = _run_exchange(_gather_exchange([w_in_t(w["w_in"]).astype(BF16)]), "all_gather_w_in")
    w_in_full_t = a_all.reshape(NIN, D)
    wp = _w_in_t_rows(w_in_full_t, _FWD_ORDER).T
    wt_bwd = _w_in_t_rows(w_in_full_t, _BWD_ORDER)
    core = ci.reshape(1).astype(jnp.int32)
    (proj, dtr, h), (r_all, cw_all) = _proj_fwd(x2, norm_g2, wp, _gather_exchange([_rows_pack(w, BF16), w["conv_w"].reshape(4, CD // NDEV)]))
    f_oa = r_all[:, R_OA:R_OB].reshape(D, D)
    f_ob = r_all[:, R_OB:R_OUT].reshape(DI, D)
    f_out = r_all[:, R_OUT:R_PG].reshape(D, D)
    f_pg = r_all[:, R_PG:R_PLE].reshape(D, D)
    f_ple = jnp.transpose(r_all[:, R_PLE:R_ROWS].reshape(NDEV, PLE, 128), (1, 0, 2)).reshape(PLE, D)
    cw = jnp.transpose(cw_all, (1, 0, 2)).reshape(4, CD)

    ya = _gmlp_fwd(proj, ln_g, ln_b, ws, bst)
    y, yb, hprev, pre_all = _ssd_fwd(proj, dtr, cw, cb, dtb, alog, dskx, sg, e128)
    dx1, dx1b, oa, ob, mg, hp, dpre, dpe, acc = _merge_fwd(x2, ya, yb, proj, p2, tgt, f_oa, f_ob, f_out, f_pg, f_ple, f_pg.T, ple_g, fin_g)
    doa, dob, dya, dyb, dgab = _merge_bwd(dx1b, oa, ob, proj, f_out.T, f_oa.T, f_ob.T)
    loss = lax.psum(acc[2, 0], ("x", "y", "c"))

    gple = jnp.transpose(_wgrad(p2, dpe, "wgrad_ple").reshape(PLE, NDEV, 128), (1, 0, 2)).reshape(NDEV, 32, D)
    gr = jnp.concatenate([_wgrad(ya, doa, "wgrad_oa").reshape(NDEV, 128, D), _wgrad(yb, dob, "wgrad_ob").reshape(NDEV, 256, D),
                          _wgrad(mg, dx1b, "wgrad_out").reshape(NDEV, 128, D), _wgrad(hp, dpre, "wgrad_pg").reshape(NDEV, 128, D),
                          gple], axis=1)
    (duvz, dws, dbs, dln), (from_sib_r,) = _gmlp_bwd(proj, dya, ln_g, ln_b, ws, wst, bst, gsel, _sibling_exchange([gr]))
    pr = _pair_sum(gr, from_sib_r, core, "pair_sum_rows")
    (dxz, ddt, dcw, dsm, dsg), (rr,) = _ssd_bwd(proj, pre_all, dtr, y, dyb, hprev, cw, dtb, alog, dskx, sg, e128, et128, _chips_exchange([pr]))

    g_xz = _wgrad(dxz, h, "wgrad_xbc_zb")
    g_w_in_t = jnp.concatenate([_wgrad(duvz, h, "wgrad_u_v_za"), g_xz[CD:], g_xz[:CD], _wgrad(ddt, h, "wgrad_dt")[:NH],
                                _wgrad(dgab, h, "wgrad_ga_gb")], axis=0)
    ga = g_w_in_t.reshape(NDEV, WSH, D)
    (from_sib_a,) = _run_exchange(_sibling_exchange([ga]), "reduce_scatter_sibling_w_in")
    pa = _pair_sum(ga, from_sib_a, core, "pair_sum_w_in")
    (gx, dng), (ra,) = _proj_bwd(x2, dx1, norm_g2, wt_bwd, [dxz, duvz, dgab], ddt, _chips_exchange([pa]))
    pa_own = lax.dynamic_index_in_dim(pa, chip, 0, keepdims=False)
    pr_own = lax.dynamic_index_in_dim(pr, chip, 0, keepdims=False)

    wide = jnp.concatenate([dng, dln, acc, dsg.reshape(16, D), dcw.reshape(24, D)], axis=0)
    narrow = jnp.concatenate([dws.reshape(NG * CH, CH), jnp.pad(dbs[:, :NG].T, ((0, 8 - NG), (0, 0))), dsm], axis=0)
    wide_all, narrow_all = _run_exchange(_direct_gather_exchange([wide, narrow]), "all_gather_small_grads")

    out_g, out_d, out_m, out_v = {}, {}, {}, {}
    outs = (out_g, out_d, out_m, out_v)
    res = _adamw(w_in_t(w["w_in"]), w_in_t(m["w_in"]), w_in_t(v["w_in"]), [pa_own, (ra, 0), (ra, 1), (ra, 2)], "adamw_w_in")
    for dst, val in zip(outs, res):
        dst["w_in"] = jnp.transpose(val).reshape(1, D, WSH)
    parts_r = [pr_own, (rr, 0), (rr, 1), (rr, 2)]
    for name, row, rows in (("w_oa", R_OA, 128), ("w_ob", R_OB, 256), ("w_out", R_OUT, 128), ("w_pg", R_PG, 128)):
        res = _adamw(w[name].reshape(rows, D), m[name].reshape(rows, D), v[name].reshape(rows, D), parts_r, "adamw_" + name,
                     part_row=row, block_rows=128)
        for dst, val in zip(outs, res):
            dst[name] = val.reshape(1, rows, D)
    res = _adamw(w["w_ple"].reshape(32, D), m["w_ple"].reshape(32, D), v["w_ple"].reshape(32, D), parts_r, "adamw_w_ple",
                 part_row=R_PLE, block_rows=32)
    for dst, val in zip(outs, res):
        dst["w_ple"] = val.reshape(1, PLE, 128)
    two_d = lambda d: {n: d[n].reshape(_SMALL_SHAPES[n]) for n in _SMALL_SHAPES}
    *res, g_cw_wide = _adamw_small(two_d(w), two_d(m), two_d(v), wide_all, narrow_all)
    for dst, val in zip(outs, res):
        dst.update({n: val[n].reshape(w[n].shape) for n in _SMALL_SHAPES})
    g_cw = lax.dynamic_slice_in_dim(g_cw_wide.reshape(4, CD), me * (CD // NDEV), CD // NDEV, axis=1).reshape(12, 128)
    res = _adamw(w["conv_w"].reshape(12, 128), m["conv_w"].reshape(12, 128), v["conv_w"].reshape(12, 128), [g_cw], "adamw_conv_w")
    for dst, val in zip((out_g, out_d, out_m, out_v), res):
        dst["conv_w"] = val.reshape(1, 4, CD // NDEV)

    return (loss, gx.reshape(1, T, D), *[out_g[n] for n in _WEIGHTS], *[out_d[n] for n in _WEIGHTS],
            *[out_m[n] for n in _WEIGHTS], *[out_v[n] for n in _WEIGHTS])
```

```python
import functools
import math
from typing import Callable, NamedTuple

import jax
import jax.numpy as jnp
from jax import lax
from jax.experimental import pallas as pl
from jax.experimental.pallas import tpu as pltpu

F32 = jnp.float32
BF16 = jnp.bfloat16
MESH = pl.DeviceIdType.MESH

D = 1024
DI = 2048
CD = 3072
NH = 32
HD = 64
NST = 128
NG = 4
CH = 128
PLE = 256
NIN = 10272
NDEV = 8
WSH = NIN // NDEV
EPS = 1e-6
OFF_XBC, OFF_U, OFF_ZB, OFF_V, OFF_ZA, OFF_GA, OFF_GB, OFF_DT = 0, 3072, 4096, 6144, 7168, 8192, 9216, 10240
NP = 10368
DTW = 128
R_OA, R_OB, R_OUT, R_PG, R_PLE, R_ROWS = 0, 128, 384, 512, 640, 672

ADAM_LR, ADAM_B1, ADAM_B2, ADAM_EPS, ADAM_WD, ADAM_STEP = 0.001, 0.9, 0.999, 1e-08, 0.01, 10

V7X_VMEM_LIMIT = 56 * 1024 * 1024


def _cp(sem=None):
    return pltpu.CompilerParams(dimension_semantics=sem, vmem_limit_bytes=V7X_VMEM_LIMIT)


def _dot(a, b, prec=None):
    return jnp.dot(a, b, preferred_element_type=F32, precision=prec)


def _dot_nt(a, b, prec=None):
    return lax.dot_general(a, b, (((1,), (1,)), ((), ())), preferred_element_type=F32, precision=prec)


def _dot_tn(a, b, prec=None):
    return lax.dot_general(a, b, (((0,), (0,)), ((), ())), preferred_element_type=F32, precision=prec)


def _sigmoid(x):
    return 1.0 / (1.0 + jnp.exp(-x))


def _gelu_and_grad(x):
    c = math.sqrt(2.0 / math.pi)
    x2 = x * x
    t = jnp.tanh(c * (x + 0.044715 * x * x2))
    g = 0.5 * x * (1.0 + t)
    dg = 0.5 * (1.0 + t) + 0.5 * x * (1.0 - t * t) * c * (1.0 + 3.0 * 0.044715 * x2)
    return g, dg


def _gelu(x):
    c = math.sqrt(2.0 / math.pi)
    return 0.5 * x * (1.0 + jnp.tanh(c * (x + 0.044715 * x * x * x)))


def _softplus(x):
    return jnp.maximum(x, 0.0) + jnp.log(1.0 + jnp.exp(-jnp.abs(x)))


def _full(shape):
    n = len(shape)
    return pl.BlockSpec(shape, lambda *_: (0,) * n)


_ANY = pl.BlockSpec(memory_space=pl.ANY)


class _Exchange(NamedTuple):
    arrays: list
    out_shape: list
    sems: list
    make: Callable


def _call(body, ex, *, name, grid, in_specs, out_specs, out_shape, scratch_shapes, args):
    ki, ko, ks = len(in_specs), len(out_specs), len(scratch_shapes)
    ei, eo = len(ex.arrays), len(ex.out_shape)
    last = [g - 1 for g in grid]

    def full_body(*refs):
        r = list(refs)
        ins, eins, r = r[:ki], r[ki:ki + ei], r[ki + ei:]
        outs, eouts, r = r[:ko], r[ko:ko + eo], r[ko + eo:]
        scr, esems = r[:ks], r[ks:]
        start, finish = ex.make(eins, eouts, esems)
        ids = [pl.program_id(a) for a in range(len(grid))]
        is_first = functools.reduce(lambda p, q: p & q, [i == 0 for i in ids])
        is_last = functools.reduce(lambda p, q: p & q, [i == l for i, l in zip(ids, last)])
        pl.when(is_first)(start)
        body(*ins, *outs, *scr)
        pl.when(is_last)(finish)

    res = pl.pallas_call(
        full_body, name=name, grid=grid, in_specs=list(in_specs) + [_ANY] * ei, out_specs=list(out_specs) + [_ANY] * eo,
        out_shape=list(out_shape) + list(ex.out_shape), scratch_shapes=list(scratch_shapes) + list(ex.sems),
        compiler_params=_cp(("arbitrary",) * len(grid)),
    )(*args, *ex.arrays)
    return res[:ko], res[ko:]


def _run_exchange(ex, name):
    ni, no = len(ex.arrays), len(ex.out_shape)

    def body(*refs):
        start, finish = ex.make(refs[:ni], refs[ni:ni + no], refs[ni + no:])
        start()
        finish()

    return pl.pallas_call(body, name=name, in_specs=[_ANY] * ni, out_specs=[_ANY] * no, out_shape=list(ex.out_shape),
                          scratch_shapes=list(ex.sems))(*ex.arrays)


def _proj_fwd(x, norm_g, wp, ex):
    T = x.shape[0]
    tm, tn = min(T, 1024), 1152
    nj = NP // tn
    assert OFF_DT + DTW == NP

    def body(x_ref, g_ref, w_ref, proj_ref, dt_ref, h_ref, hs_ref):
        j = pl.program_id(1)

        @pl.when(j == 0)
        def _():
            xf = x_ref[...]
            r = lax.rsqrt(jnp.mean(xf * xf, axis=-1, keepdims=True) + EPS)
            h = (xf * r * g_ref[...]).astype(BF16)
            hs_ref[...] = h
            h_ref[...] = h

        acc = _dot_nt(hs_ref[...], w_ref[...])
        proj_ref[...] = acc.astype(BF16)

        @pl.when(j == nj - 1)
        def _():
            dt_ref[...] = acc[:, tn - DTW:]

    return _call(
        body, ex, name="proj_fwd", grid=(T // tm, nj),
        in_specs=[pl.BlockSpec((tm, D), lambda i, j: (i, 0)), _full((1, D)), pl.BlockSpec((tn, D), lambda i, j: (j, 0))],
        out_specs=[pl.BlockSpec((tm, tn), lambda i, j: (i, j)), pl.BlockSpec((tm, DTW), lambda i, j: (i, 0)),
                   pl.BlockSpec((tm, D), lambda i, j: (i, 0))],
        out_shape=[jax.ShapeDtypeStruct((T, NP), BF16), jax.ShapeDtypeStruct((T, DTW), F32), jax.ShapeDtypeStruct((T, D), BF16)],
        scratch_shapes=[pltpu.VMEM((tm, D), BF16)], args=(x, norm_g, wp))


def _gmlp_tile():
    return 256


def _gmlp_fwd(proj, ln_g, ln_b, ws, bst):
    T = proj.shape[0]
    tm = min(T, _gmlp_tile())

    def body(u_ref, v_ref, z_ref, lg_ref, lb_ref, ws_ref, bs_ref, ya_ref, vn_s):
        vg = _gelu(v_ref[...].astype(F32))
        mu = jnp.mean(vg, axis=-1, keepdims=True)
        xc = vg - mu
        rstd = lax.rsqrt(jnp.mean(xc * xc, axis=-1, keepdims=True) + EPS)
        vn_s[...] = (xc * rstd * lg_ref[...] + lb_ref[...]).astype(BF16)
        for c in range(tm // CH):
            rs = slice(c * CH, (c + 1) * CH)
            for g in range(NG):
                cs_ = slice(g * 256, (g + 1) * 256)
                sv = _dot(ws_ref[g], vn_s[rs, cs_]) + bs_ref[g]
                z = z_ref[rs, cs_].astype(F32)
                ya_ref[rs, cs_] = (_gelu(u_ref[rs, cs_].astype(F32)) * sv * (z * _sigmoid(z))).astype(BF16)

    blk = lambda off: pl.BlockSpec((tm, D), lambda i: (i, off // D))
    return pl.pallas_call(
        body, name="gmlp_fwd", grid=(T // tm,),
        in_specs=[blk(OFF_U), blk(OFF_V), blk(OFF_ZA), _full((1, D)), _full((1, D)), _full((NG, CH, CH)), _full((NG, CH, 256))],
        out_specs=pl.BlockSpec((tm, D), lambda i: (i, 0)),
        out_shape=jax.ShapeDtypeStruct((T, D), BF16),
        scratch_shapes=[pltpu.VMEM((tm, D), BF16)],
        compiler_params=_cp(("parallel",)),
    )(proj, proj, proj, ln_g, ln_b, ws, bst)


def _gmlp_bwd(proj, dya, ln_g, ln_b, ws, wst, bst, gsel, ex):
    T = proj.shape[0]
    tm = min(T, _gmlp_tile())

    def body(u_ref, v_ref, z_ref, dy_ref, lg_ref, lb_ref, ws_ref, wst_ref, bs_ref, gsel_ref,
             d_ref, dws_ref, dbs_ref, dln_ref, vn_s, dsv_s, dvn_s):
        du_ref, dv_ref, dz_ref = d_ref.at[:, 0:D], d_ref.at[:, D:2 * D], d_ref.at[:, 2 * D:3 * D]
        @pl.when(pl.program_id(0) == 0)
        def _():
            dws_ref[...] = jnp.zeros_like(dws_ref)
            dbs_ref[...] = jnp.zeros_like(dbs_ref)
            dln_ref[...] = jnp.zeros_like(dln_ref)

        vg, dvg_dv = _gelu_and_grad(v_ref[...].astype(F32))
        mu = jnp.mean(vg, axis=-1, keepdims=True)
        xc = vg - mu
        rstd = lax.rsqrt(jnp.mean(xc * xc, axis=-1, keepdims=True) + EPS)
        vhat = xc * rstd
        vn_s[...] = (vhat * lg_ref[...] + lb_ref[...]).astype(BF16)
        ri = lax.broadcasted_iota(jnp.int32, (CH, CH), 0)
        ci = lax.broadcasted_iota(jnp.int32, (CH, CH), 1)
        tril = (ri >= ci).astype(F32)
        for c in range(tm // CH):
            rs = slice(c * CH, (c + 1) * CH)
            for g in range(NG):
                cs_ = slice(g * 256, (g + 1) * 256)
                vn = vn_s[rs, cs_]
                sv = _dot(ws_ref[g], vn) + bs_ref[g]
                z = z_ref[rs, cs_].astype(F32)
                sz = _sigmoid(z)
                ug, dug_du = _gelu_and_grad(u_ref[rs, cs_].astype(F32))
                dy = dy_ref[rs, cs_].astype(F32)
                t = dy * z * sz
                du_ref[rs, cs_] = (t * sv * dug_du).astype(BF16)
                dz_ref[rs, cs_] = (dy * ug * sv * sz * (1.0 + z * (1.0 - sz))).astype(BF16)
                dsv = (t * ug).astype(BF16)
                dsv_s[rs, cs_] = dsv
                dvn_s[rs, cs_] = _dot(wst_ref[g], dsv)
                dws_ref[g] += _dot_nt(dsv, vn) * tril
            dbs_ref[...] += _dot(dsv_s[rs, :], gsel_ref[...])
        dvn = dvn_s[...]
        dln_ref[0:1, :] += jnp.sum(dvn * vhat, axis=0, keepdims=True)
        dln_ref[1:2, :] += jnp.sum(dvn, axis=0, keepdims=True)
        dvh = dvn * lg_ref[...]
        dvg = rstd * (dvh - jnp.mean(dvh, axis=-1, keepdims=True) - vhat * jnp.mean(dvh * vhat, axis=-1, keepdims=True))
        dv_ref[...] = (dvg * dvg_dv).astype(BF16)

    blk = lambda off: pl.BlockSpec((tm, D), lambda i: (i, off // D))
    row = pl.BlockSpec((tm, D), lambda i: (i, 0))
    return _call(
        body, ex, name="gmlp_bwd", grid=(T // tm,),
        in_specs=[blk(OFF_U), blk(OFF_V), blk(OFF_ZA), row, _full((1, D)), _full((1, D)), _full((NG, CH, CH)),
                  _full((NG, CH, CH)), _full((NG, CH, 256)), _full((D, 128))],
        out_specs=[pl.BlockSpec((tm, 3 * D), lambda i: (i, 0)), _full((NG, CH, CH)), _full((CH, 128)), _full((8, D))],
        out_shape=[jax.ShapeDtypeStruct((T, 3 * D), BF16),
                   jax.ShapeDtypeStruct((NG, CH, CH), F32), jax.ShapeDtypeStruct((CH, 128), F32), jax.ShapeDtypeStruct((8, D), F32)],
        scratch_shapes=[pltpu.VMEM((tm, D), BF16), pltpu.VMEM((tm, D), BF16), pltpu.VMEM((tm, D), F32)],
        args=(proj, proj, proj, dya, ln_g, ln_b, ws, wst, bst, gsel))


def _conv_pre(x, xp, cw_ref, cb_ref):
    row = lax.broadcasted_iota(jnp.int32, (CH, 1), 0)
    pre = cb_ref[...] + cw_ref[3:4, :] * x
    for j in (1, 2, 3):
        pre = pre + cw_ref[3 - j:4 - j, :] * jnp.where(row >= j, pltpu.roll(x, j, 0), pltpu.roll(xp, j, 0))
    return pre


def _split_dot(x, w, parts, w_left=False):
    acc, r = None, x
    for k in range(parts):
        hi = r.astype(BF16)
        d = _dot(w, hi) if w_left else _dot(hi, w)
        acc = d if acc is None else acc + d
        if k + 1 < parts:
            r = r - hi.astype(F32)
    return acc


def _chunk_decays(dt, alog_ref, e_ref, cs_s, cst_s, csx_s):
    a = -jnp.exp(alog_ref[...])
    ri = lax.broadcasted_iota(jnp.int32, (CH, CH), 0)
    ci = lax.broadcasted_iota(jnp.int32, (CH, CH), 1)
    tril = ri >= ci
    cs = _split_dot(dt * a, tril.astype(BF16), 3, w_left=True)
    cs_s[...] = cs
    cst_s[...] = cs.T
    csx_s[...] = _split_dot(cs, e_ref[...], 3)
    return a, tril, ri, ci


def _lmat(cst_s, h, tril):
    rowb = jnp.broadcast_to(cst_s[h:h + 1, :], (CH, CH))
    return jnp.exp(jnp.where(tril, rowb.T - rowb, -jnp.inf))


def _head_pair_rows(v, lane):
    return jnp.concatenate([jnp.where(lane < HD, v, 0.0), jnp.where(lane < HD, 0.0, v)], axis=0).astype(BF16)


def _ssd_fwd(proj, dtr, cw, cb, dtb, alog, dskx, sg, e128):
    T = proj.shape[0]
    nc = T // CH

    def body(xbc_ref, zb_ref, dt_ref, cw_ref, cb_ref, dtb_ref, alog_ref, dx_ref, sg_ref, e_ref,
             y_ref, yb_ref, hp_ref, pre_ref, xprev_s, h_s, cs_s, cst_s, csx_s, yz_s):
        @pl.when(pl.program_id(0) == 0)
        def _():
            xprev_s[...] = jnp.zeros_like(xprev_s)
            h_s[...] = jnp.zeros_like(h_s)

        x = xbc_ref[...].astype(F32)
        pre = _conv_pre(x, xprev_s[...], cw_ref, cb_ref)
        pre_ref[...] = pre
        xprev_s[...] = x
        xc = pre * _sigmoid(pre)
        dt = _softplus(dt_ref[...] + dtb_ref[...])
        a, tril, _, lane = _chunk_decays(dt, alog_ref, e_ref, cs_s, cst_s, csx_s)
        dt_x = _split_dot(dt, e_ref[...], 2)
        cs_last_x = csx_s[CH - 1:CH, :]
        hp_ref[0] = h_s[...]
        for g in range(NG):
            gs = slice(g * 512, (g + 1) * 512)
            bg = xc[:, DI + g * NST:DI + (g + 1) * NST].astype(BF16)
            cg = xc[:, DI + 512 + g * NST:DI + 512 + (g + 1) * NST].astype(BF16)
            cbm = _dot_nt(cg, bg)
            xg = xc[:, gs]
            xdt = xg * dt_x[:, gs]
            hprev = h_s[:, gs]
            csx = csx_s[:, gs]
            yoff = _dot(cg, hprev.astype(BF16)) * jnp.exp(csx)
            st = _dot_tn(bg, (xdt * jnp.exp(cs_last_x[:, gs] - csx)).astype(BF16))
            h_s[:, gs] = jnp.exp(cs_last_x[:, gs]) * hprev + st
            ssq = jnp.zeros((CH, 1), F32)
            for q in range(4):
                h0 = g * 8 + 2 * q
                ps = slice(q * 128, (q + 1) * 128)
                cols = slice(g * 512 + q * 128, g * 512 + (q + 1) * 128)
                m01 = jnp.concatenate([cbm * _lmat(cst_s, h0, tril), cbm * _lmat(cst_s, h0 + 1, tril)], axis=1).astype(BF16)
                yq = _dot(m01, _head_pair_rows(xdt[:, ps], lane)) + yoff[:, ps] + xg[:, ps] * dx_ref[:, cols]
                y_ref[:, cols] = yq
                z = zb_ref[:, cols].astype(F32)
                yz = yq * z * _sigmoid(z)
                yz_s[:, cols] = yz
                ssq = ssq + jnp.sum(yz * yz, axis=1, keepdims=True)
            rg = lax.rsqrt(ssq * (1.0 / 512.0) + EPS)
            yb_ref[:, gs] = (yz_s[:, gs] * rg * sg_ref[:, gs]).astype(BF16)

    return pl.pallas_call(
        body, name="ssd_fwd", grid=(nc,),
        in_specs=[pl.BlockSpec((CH, CD), lambda c: (c, OFF_XBC // CD)), pl.BlockSpec((CH, DI), lambda c: (c, OFF_ZB // DI)),
                  pl.BlockSpec((CH, DTW), lambda c: (c, 0)), _full((4, CD)), _full((1, CD)), _full((1, DTW)),
                  _full((1, DTW)), _full((1, DI)), _full((1, DI)), _full((DTW, DI))],
        out_specs=[pl.BlockSpec((CH, DI), lambda c: (c, 0)), pl.BlockSpec((CH, DI), lambda c: (c, 0)),
                   pl.BlockSpec((1, NST, DI), lambda c: (c, 0, 0)), pl.BlockSpec((CH, CD), lambda c: (c, 0))],
        out_shape=[jax.ShapeDtypeStruct((T, DI), F32), jax.ShapeDtypeStruct((T, DI), BF16),
                   jax.ShapeDtypeStruct((nc, NST, DI), F32), jax.ShapeDtypeStruct((T, CD), F32)],
        scratch_shapes=[pltpu.VMEM((CH, CD), F32), pltpu.VMEM((NST, DI), F32), pltpu.VMEM((CH, CH), F32),
                        pltpu.VMEM((CH, CH), F32), pltpu.VMEM((CH, DI), F32), pltpu.VMEM((CH, DI), F32)],
        compiler_params=_cp(("arbitrary",)),
    )(proj, proj, dtr, cw, cb, dtb, alog, dskx, sg, e128)


def _ssd_bwd(proj, pre_all, dtr, y, dyb, hprev_all, cw, dtb, alog, dskx, sg, e128, et128, ex):
    T = proj.shape[0]
    nc = T // CH

    def body(xbc_ref, pre_ref, zb_ref, dt_ref, y_ref, dyb_ref, hp_ref, cw_ref, dtb_ref, alog_ref, dx_ref, sg_ref,
             e_ref, et_ref, d_ref, ddt_ref, dcw_ref, dsm_ref, dsg_ref,
             g_s, dpn_s, cs_s, cst_s, csx_s, dy_s, dxdt_s, dxs_s, dsd_s, dxc_s, gh_s):
        dxbc_ref, dzb_ref = d_ref.at[:, 0:CD], d_ref.at[:, CD:CD + DI]
        i = pl.program_id(0)

        @pl.when(i == 0)
        def _():
            g_s[...] = jnp.zeros_like(g_s)
            dpn_s[...] = jnp.zeros_like(dpn_s)
            dcw_ref[...] = jnp.zeros_like(dcw_ref)
            dsm_ref[...] = jnp.zeros_like(dsm_ref)
            dsg_ref[...] = jnp.zeros_like(dsg_ref)

        pre = pre_ref[...]
        sp = _sigmoid(pre)
        xc = pre * sp
        dtr = dt_ref[...] + dtb_ref[...]
        dt = _softplus(dtr)
        a, tril, ri, lane = _chunk_decays(dt, alog_ref, e_ref, cs_s, cst_s, csx_s)
        et = et_ref[...]
        dt_x = _split_dot(dt, e_ref[...], 2)
        cs_last_x = csx_s[CH - 1:CH, :]

        for g in range(NG):
            gs = slice(g * 512, (g + 1) * 512)
            z = zb_ref[:, gs].astype(F32)
            sz = _sigmoid(z)
            yv = y_ref[:, gs]
            yz = yv * z * sz
            rg = lax.rsqrt(jnp.mean(yz * yz, axis=-1, keepdims=True) + EPS)
            yn = yz * rg
            dyb = dyb_ref[:, gs].astype(F32)
            dsg_ref[0:1, gs] += jnp.sum(dyb * yn, axis=0, keepdims=True)
            dyn = dyb * sg_ref[:, gs]
            dyz = rg * (dyn - yn * jnp.mean(dyn * yn, axis=-1, keepdims=True))
            dy_s[:, gs] = dyz * z * sz
            dzb_ref[:, gs] = (dyz * yv * sz * (1.0 + z * (1.0 - sz))).astype(BF16)

        rsum = jnp.zeros((CH, DTW), F32)
        csum_t = jnp.zeros((DTW, CH), F32)
        for g in range(NG):
            gs = slice(g * 512, (g + 1) * 512)
            bg = xc[:, DI + g * NST:DI + (g + 1) * NST].astype(BF16)
            cg = xc[:, DI + 512 + g * NST:DI + 512 + (g + 1) * NST].astype(BF16)
            cbm = _dot_nt(cg, bg)
            xdt = xc[:, gs] * dt_x[:, gs]
            hprev = hp_ref[0, :, gs]
            hpb = hprev.astype(BF16)
            gn = g_s[:, gs]
            gnb = gn.astype(BF16)
            dy = dy_s[:, gs]
            csx = csx_s[:, gs]
            ecs = jnp.exp(csx)
            dec = jnp.exp(cs_last_x[:, gs] - csx)
            dye = (dy * ecs).astype(BF16)
            dc = _dot_nt(dye, hpb)
            dprev = _dot_tn(cg, dye)
            dxdt_state = dec * _dot(bg, gnb)
            db = _dot_nt((xdt * dec).astype(BF16), gnb)
            dcb = jnp.zeros((CH, CH), F32)
            for q in range(4):
                h0 = g * 8 + 2 * q
                ps = slice(q * 128, (q + 1) * 128)
                dyp = dy[:, ps]
                l0 = _lmat(cst_s, h0, tril)
                l1 = _lmat(cst_s, h0 + 1, tril)
                m0 = cbm * l0
                m1 = cbm * l1
                dm = _dot_nt(dyp.astype(BF16), _head_pair_rows(xdt[:, ps], lane))
                dm0 = dm[:, :CH]
                dm1 = dm[:, CH:]
                dcb = dcb + dm0 * l0 + dm1 * l1
                for hh, qm in ((h0, dm0 * m0), (h0 + 1, dm1 * m1)):
                    rsum = jnp.where(lane == hh, jnp.sum(qm, axis=1, keepdims=True), rsum)
                    csum_t = jnp.where(ri == hh, jnp.sum(qm, axis=0, keepdims=True), csum_t)
                mst = jnp.concatenate([m0, m1], axis=0).astype(BF16)
                d = _dot_tn(mst, _head_pair_rows(dyp, lane))
                dxdt_s[:, g * 512 + q * 128:g * 512 + (q + 1) * 128] = d + dxdt_state[:, ps]
            yoff = _dot(cg, hpb) * ecs
            dsd_s[:, gs] = xdt * dxdt_state
            dxs_s[:, gs] = dy * yoff
            dcbb = dcb.astype(BF16)
            dxc_s[:, DI + 512 + g * NST:DI + 512 + (g + 1) * NST] = dc + _dot(dcbb, bg)
            dxc_s[:, DI + g * NST:DI + (g + 1) * NST] = db + _dot_tn(dcbb, cg)
            gh_s[:, gs] = jnp.broadcast_to(jnp.sum(gn * hprev, axis=0, keepdims=True), (8, 512))
            g_s[:, gs] = dprev + jnp.exp(cs_last_x[:, gs]) * gn

        xs = xc[:, :DI]
        dy = dy_s[...]
        dxdt = dxdt_s[...]
        cs_last = cs_s[CH - 1:CH, :]
        state_e = _split_dot(dsd_s[...], et, 2)
        dcd = 0.125 * jnp.sum(_split_dot(gh_s[...], et, 2), axis=0, keepdims=True) * jnp.exp(cs_last)
        row = lax.broadcasted_iota(jnp.int32, (CH, 1), 0)
        dcs = rsum - csum_t.T + _split_dot(dxs_s[...], et, 2) - state_e
        dcs = dcs + jnp.where(row == CH - 1, jnp.sum(state_e, axis=0, keepdims=True) + dcd, 0.0)
        dda = _split_dot(dcs, (lane >= ri).astype(BF16), 3, w_left=True)
        ddt = dda * a + _dot((dxdt * xs).astype(BF16), et)
        ddtr = jnp.where(lane < NH, ddt * _sigmoid(dtr), 0.0)
        ddt_ref[...] = ddtr.astype(BF16)
        dsm_ref[0:1, :] += jnp.sum(ddtr, axis=0, keepdims=True)
        dsm_ref[1:2, :] += jnp.sum(dda * dt, axis=0, keepdims=True) * a
        dsm_ref[2:3, :] += jnp.sum(_dot((dy * xs).astype(BF16), et), axis=0, keepdims=True)
        dxc_s[:, :DI] = dxdt * dt_x + dy * dx_ref[...]

        dpre = dxc_s[...] * sp * (1.0 + pre * (1.0 - sp))
        dpn = dpn_s[...]
        x = xbc_ref[...].astype(F32)
        dcw_ref[4:5, :] += jnp.sum(dpre, axis=0, keepdims=True)
        dxbc = cw_ref[3:4, :] * dpre
        dcw_ref[3:4, :] += jnp.sum(dpre * x, axis=0, keepdims=True)
        for j in (1, 2, 3):
            ahead = jnp.where(row < CH - j, pltpu.roll(dpre, CH - j, 0), pltpu.roll(dpn, CH - j, 0))
            dcw_ref[3 - j:4 - j, :] += jnp.sum(ahead * x, axis=0, keepdims=True)
            dxbc = dxbc + cw_ref[3 - j:4 - j, :] * ahead
        dpn_s[...] = dpre
        dxbc_ref[...] = dxbc.astype(BF16)

    rev = lambda c: nc - 1 - c
    return _call(
        body, ex, name="ssd_bwd", grid=(nc,),
        in_specs=[pl.BlockSpec((CH, CD), lambda c: (rev(c), OFF_XBC // CD)),
                  pl.BlockSpec((CH, CD), lambda c: (rev(c), 0)),
                  pl.BlockSpec((CH, DI), lambda c: (rev(c), OFF_ZB // DI)),
                  pl.BlockSpec((CH, DTW), lambda c: (rev(c), 0)),
                  pl.BlockSpec((CH, DI), lambda c: (rev(c), 0)), pl.BlockSpec((CH, DI), lambda c: (rev(c), 0)),
                  pl.BlockSpec((1, NST, DI), lambda c: (rev(c), 0, 0)),
                  _full((4, CD)), _full((1, DTW)), _full((1, DTW)), _full((1, DI)), _full((1, DI)),
                  _full((DTW, DI)), _full((DI, DTW))],
        out_specs=[pl.BlockSpec((CH, CD + DI), lambda c: (rev(c), 0)),
                   pl.BlockSpec((CH, DTW), lambda c: (rev(c), 0)), _full((8, CD)), _full((8, DTW)), _full((8, DI))],
        out_shape=[jax.ShapeDtypeStruct((T, CD + DI), BF16), jax.ShapeDtypeStruct((T, DTW), BF16),
                   jax.ShapeDtypeStruct((8, CD), F32), jax.ShapeDtypeStruct((8, DTW), F32), jax.ShapeDtypeStruct((8, DI), F32)],
        scratch_shapes=[pltpu.VMEM((NST, DI), F32), pltpu.VMEM((CH, CD), F32), pltpu.VMEM((CH, CH), F32), pltpu.VMEM((CH, CH), F32),
                        pltpu.VMEM((CH, DI), F32), pltpu.VMEM((CH, DI), F32), pltpu.VMEM((CH, DI), F32), pltpu.VMEM((CH, DI), F32),
                        pltpu.VMEM((CH, DI), F32), pltpu.VMEM((CH, CD), F32), pltpu.VMEM((8, DI), F32)],
        args=(proj, pre_all, proj, dtr, y, dyb, hprev_all, cw, dtb, alog, dskx, sg, e128, et128))


def _merge_tile():
    return 256


def _merge(x, ya, yb, proj, p, tgt, w_oa, w_ob, w_out, w_pg, w_ple, ple_g, fin_g):
    T = x.shape[0]
    tm = min(T, _merge_tile())

    def body(x_ref, ya_ref, yb_ref, ga_ref, gb_ref, p_ref, t_ref, woa, wob, wout, wpg, wple, pg_ref, fg_ref,
             dx1_ref, dx1b_ref, mg_ref, hp_ref, dpre_ref, dpe_ref, doa_ref, dob_ref, dya_ref, dyb_ref, dg_ref, acc_ref):
        @pl.when(pl.program_id(0) == 0)
        def _():
            acc_ref[...] = jnp.zeros_like(acc_ref)

        oa = _dot(ya_ref[...], woa[...])
        ob = _dot(yb_ref[...], wob[...])
        sa = _sigmoid(ga_ref[...].astype(F32))
        sb = _sigmoid(gb_ref[...].astype(F32))
        mg = sa * oa + sb * ob
        mgb = mg.astype(BF16)
        mg_ref[...] = mgb
        x1 = x_ref[...] + _dot(mgb, wout[...])
        r2 = lax.rsqrt(jnp.mean(x1 * x1, axis=-1, keepdims=True) + EPS)
        xh1 = x1 * r2
        hpb = (xh1 * pg_ref[...]).astype(BF16)
        hp_ref[...] = hpb
        gate = _sigmoid(_dot(hpb, wpg[...]))
        pe = _dot(p_ref[...].astype(BF16), wple[...])
        x2 = x1 + gate * pe
        r3 = lax.rsqrt(jnp.mean(x2 * x2, axis=-1, keepdims=True) + EPS)
        xh2 = x2 * r3
        err = xh2 * fg_ref[...] - t_ref[...]
        acc_ref[2:3, :] += 0.5 * jnp.sum(jnp.mean(err * err, axis=-1, keepdims=True))
        dyo = err * (1.0 / D)
        acc_ref[0:1, :] += jnp.sum(dyo * xh2, axis=0, keepdims=True)
        dn = dyo * fg_ref[...]
        dx2 = r3 * (dn - xh2 * jnp.mean(dn * xh2, axis=-1, keepdims=True))
        dpe_ref[...] = (dx2 * gate).astype(BF16)
        dpre = (dx2 * pe * gate * (1.0 - gate)).astype(BF16)
        dpre_ref[...] = dpre
        dhp = _dot_nt(dpre, wpg[...])
        acc_ref[1:2, :] += jnp.sum(dhp * xh1, axis=0, keepdims=True)
        dhn = dhp * pg_ref[...]
        dx1 = dx2 + r2 * (dhn - xh1 * jnp.mean(dhn * xh1, axis=-1, keepdims=True))
        dx1_ref[...] = dx1
        dx1b = dx1.astype(BF16)
        dx1b_ref[...] = dx1b
        dmg = _dot_nt(dx1b, wout[...])
        doa = (dmg * sa).astype(BF16)
        dob = (dmg * sb).astype(BF16)
        doa_ref[...] = doa
        dob_ref[...] = dob
        dg_ref[:, :D] = (dmg * oa * sa * (1.0 - sa)).astype(BF16)
        dg_ref[:, D:] = (dmg * ob * sb * (1.0 - sb)).astype(BF16)
        dya_ref[...] = _dot_nt(doa, woa[...]).astype(BF16)
        dyb_ref[...] = _dot_nt(dob, wob[...]).astype(BF16)

    row = lambda w: pl.BlockSpec((tm, w), lambda i: (i, 0))
    wsp = lambda s: pl.BlockSpec(s, lambda i: (0, 0), pipeline_mode=pl.Buffered(1))
    return pl.pallas_call(
        body, name="merge", grid=(T // tm,),
        in_specs=[row(D), row(D), row(DI), pl.BlockSpec((tm, D), lambda i: (i, OFF_GA // D)),
                  pl.BlockSpec((tm, D), lambda i: (i, OFF_GB // D)), row(PLE), row(D),
                  wsp((D, D)), wsp((DI, D)), wsp((D, D)), wsp((D, D)), wsp((PLE, D)), _full((1, D)), _full((1, D))],
        out_specs=[row(D)] * 9 + [row(DI), row(2 * D), _full((8, D))],
        out_shape=[jax.ShapeDtypeStruct((T, D), F32)] + [jax.ShapeDtypeStruct((T, D), BF16)] * 8
        + [jax.ShapeDtypeStruct((T, DI), BF16), jax.ShapeDtypeStruct((T, 2 * D), BF16), jax.ShapeDtypeStruct((8, D), F32)],
        compiler_params=_cp(("arbitrary",)),
    )(x, ya, yb, proj, proj, p, tgt, w_oa, w_ob, w_out, w_pg, w_ple, ple_g, fin_g)


def _wgrad(a, b, name):
    T, K = a.shape
    N = b.shape[1]
    tt, tk, tn = min(T, 1024), min(K, 1024), min(N, 1024)
    nt = T // tt

    def body(a_ref, b_ref, o_ref, acc_s):
        t = pl.program_id(2)

        @pl.when(t == 0)
        def _():
            acc_s[...] = jnp.zeros_like(acc_s)

        acc_s[...] += _dot_tn(a_ref[...].astype(BF16), b_ref[...])

        @pl.when(t == nt - 1)
        def _():
            o_ref[...] = acc_s[...].astype(BF16)

    return pl.pallas_call(
        body, name=name, grid=(K // tk, N // tn, nt),
        in_specs=[pl.BlockSpec((tt, tk), lambda k, n, t: (t, k)), pl.BlockSpec((tt, tn), lambda k, n, t: (t, n))],
        out_specs=pl.BlockSpec((tk, tn), lambda k, n, t: (k, n)),
        out_shape=jax.ShapeDtypeStruct((K, N), BF16),
        scratch_shapes=[pltpu.VMEM((tk, tn), F32)],
        compiler_params=_cp(("parallel", "parallel", "arbitrary")),
    )(a, b)


def _proj_bwd(x, dx1, norm_g, wt, pieces, ddt, ex):
    T = x.shape[0]
    tm = min(T, 1024)
    nk = OFF_DT // D + 1
    starts = [sum(a.shape[1] for a in pieces[:n]) // D for n in range(len(pieces))]
    ranges = [(s, s + a.shape[1] // D) for s, a in zip(starts, pieces)]
    assert ranges[-1][1] == nk - 1
    npc = len(pieces)

    def body(x_ref, dx1_ref, g_ref, w_ref, wdt_ref, *rest):
        piece_refs, ddt_ref, gx_ref, dng_ref, acc_s = rest[:npc], rest[npc], rest[npc + 1], rest[npc + 2], rest[npc + 3]
        i, k = pl.program_id(0), pl.program_id(1)

        @pl.when((i == 0) & (k == 0))
        def _():
            dng_ref[...] = jnp.zeros_like(dng_ref)

        @pl.when(k == 0)
        def _():
            acc_s[...] = jnp.zeros_like(acc_s)

        for ref, (lo, hi) in zip(piece_refs, ranges):
            @pl.when((k >= lo) & (k < hi))
            def _(ref=ref):
                acc_s[...] += _dot(ref[...], w_ref[...])

        @pl.when(k == nk - 1)
        def _():
            dh = acc_s[...] + _dot(ddt_ref[...], wdt_ref[...])
            xf = x_ref[...]
            r = lax.rsqrt(jnp.mean(xf * xf, axis=-1, keepdims=True) + EPS)
            xh = xf * r
            dng_ref[0:1, :] += jnp.sum(dh * xh, axis=0, keepdims=True)
            dxn = dh * g_ref[...]
            gx_ref[...] = dx1_ref[...] + r * (dxn - xh * jnp.mean(dxn * xh, axis=-1, keepdims=True))

    def piece_spec(lo, hi):
        return pl.BlockSpec((tm, D), lambda i, k: (i, jnp.clip(k - lo, 0, hi - lo - 1)))

    row = pl.BlockSpec((tm, D), lambda i, k: (i, 0))
    return _call(
        body, ex, name="proj_bwd", grid=(T // tm, nk),
        in_specs=[row, row, _full((1, D)), pl.BlockSpec((D, D), lambda i, k: (jnp.minimum(k, nk - 2), 0)),
                  pl.BlockSpec((DTW, D), lambda i, k: (OFF_DT // DTW, 0))]
        + [piece_spec(lo, hi) for lo, hi in ranges] + [pl.BlockSpec((tm, DTW), lambda i, k: (i, 0))],
        out_specs=[row, _full((8, D))],
        out_shape=[jax.ShapeDtypeStruct((T, D), F32), jax.ShapeDtypeStruct((8, D), F32)],
        scratch_shapes=[pltpu.VMEM((tm, D), F32)],
        args=(x, dx1, norm_g, wt, wt, *pieces, ddt))


def _elementwise_tile(R, C, limit=1 << 20):
    if R * C * 4 <= limit:
        return R, C
    rows = [t for t in range(16, R, 16) if R % t == 0 and t * C * 4 <= limit]
    if rows:
        return rows[-1], C
    cols = [t for t in range(128, C, 128) if C % t == 0 and R * t * 4 <= limit]
    return R, cols[-1]


def _adam_update(w, m, v, g):
    c1 = 1.0 - ADAM_B1 ** ADAM_STEP
    c2 = 1.0 - ADAM_B2 ** ADAM_STEP
    mm = ADAM_B1 * m + (1.0 - ADAM_B1) * g
    vv = ADAM_B2 * v + (1.0 - ADAM_B2) * (g * g)
    return -ADAM_LR * ((mm / c1) / (jnp.sqrt(vv / c2) + ADAM_EPS) + ADAM_WD * w), mm, vv


def _adamw(w, m, v, parts, name, part_row=0, block_rows=None):
    R, C = w.shape
    tr, tc = _elementwise_tile(R, C) if block_rows is None else (block_rows, C)
    assert R % tr == 0 and part_row % tr == 0
    first = part_row // tr
    n = len(parts)
    wspec = pl.BlockSpec((tr, tc), lambda i, j: (i, j))
    flat = pl.BlockSpec((tr, tc), lambda i, j: (first + i, j))
    slot = lambda k: pl.BlockSpec((None, tr, tc), lambda i, j: (k, first + i, j))
    part_specs = [slot(p[1]) if isinstance(p, tuple) else flat for p in parts]
    part_arrays = [p[0] if isinstance(p, tuple) else p for p in parts]

    def body(*refs):
        w_ref, m_ref, v_ref = refs[:3]
        g_ref, d_ref, nm_ref, nv_ref = refs[3 + n:]
        g = refs[3][...].astype(F32)
        for r in refs[4:3 + n]:
            g = g + r[...].astype(F32)
        g_ref[...] = g
        d_ref[...], nm_ref[...], nv_ref[...] = _adam_update(w_ref[...], m_ref[...], v_ref[...], g)

    return pl.pallas_call(
        body, name=name, grid=(R // tr, C // tc), in_specs=[wspec] * 3 + part_specs, out_specs=[wspec] * 4,
        out_shape=[jax.ShapeDtypeStruct(w.shape, F32)] * 4, compiler_params=_cp(("parallel", "parallel")),
    )(w, m, v, *part_arrays)


_SMALL_WIDE = {"ln_a_g": 0, "ln_a_b": 1, "final_g": 8, "ple_norm_g": 9, "ssm_norm_g": 16, "conv_b": 44}
_WIDE_CONV_W, _WIDE_ROWS = 32, 56
_SMALL_NARROW = {"w_s": (0, NG * CH, 128), "b_s": (512, NG, 128), "dt_bias": (520, 1, NH), "a_log": (521, 1, NH),
                 "d_skip": (522, 1, NH)}
_NARROW_ROWS = 528
_SMALL_SHAPES = {"norm_g": (1, D), "ln_a_g": (1, D), "ln_a_b": (1, D), "ple_norm_g": (1, D), "final_g": (1, D),
                 "ssm_norm_g": (1, DI), "conv_b": (1, CD), "w_s": (NG * CH, CH), "b_s": (NG, CH), "dt_bias": (1, NH),
                 "a_log": (1, NH), "d_skip": (1, NH)}


def _adamw_small(w, m, v, wide_all, narrow_all, norm_all):
    names = list(_SMALL_SHAPES)
    n = len(names)

    def body(*refs):
        wr, mr, vr = refs[:n], refs[n:2 * n], refs[2 * n:3 * n]
        wide_ref, narrow_ref, norm_ref = refs[3 * n:3 * n + 3]
        outs = refs[3 * n + 3:]
        gr, dr, nmr, nvr, cw_ref = outs[:n], outs[n:2 * n], outs[2 * n:3 * n], outs[3 * n:4 * n], outs[4 * n]

        def total(ref, rows, lanes):
            acc = ref[0, rows, lanes]
            for d in range(1, NDEV):
                acc = acc + ref[d, rows, lanes]
            return acc

        for k, name in enumerate(names):
            if name in _SMALL_WIDE or name == "norm_g":
                for part in range(_SMALL_SHAPES[name][1] // D):
                    pack, r = (norm_ref, 0) if name == "norm_g" else (wide_ref, _SMALL_WIDE[name] + part)
                    cols = slice(part * D, (part + 1) * D)
                    g = total(pack, slice(r, r + 1), slice(None))
                    gr[k][:, cols] = g
                    dr[k][:, cols], nmr[k][:, cols], nvr[k][:, cols] = _adam_update(wr[k][:, cols], mr[k][:, cols], vr[k][:, cols], g)
            else:
                r, rows, lanes = _SMALL_NARROW[name]
                g = total(narrow_ref, slice(r, r + rows), slice(0, lanes))
                gr[k][...] = g
                dr[k][...], nmr[k][...], nvr[k][...] = _adam_update(wr[k][...], mr[k][...], vr[k][...], g)
        cw_ref[...] = total(wide_ref, slice(_WIDE_CONV_W, _WIDE_CONV_W + 12), slice(None))

    shapes = [jax.ShapeDtypeStruct(_SMALL_SHAPES[k], F32) for k in names]
    specs = [_full(_SMALL_SHAPES[k]) for k in names]
    res = pl.pallas_call(
        body, name="adamw_small", grid=(1,),
        in_specs=specs * 3 + [_full(wide_all.shape), _full(narrow_all.shape), _full(norm_all.shape)],
        out_specs=specs * 4 + [_full((12, D))], out_shape=shapes * 4 + [jax.ShapeDtypeStruct((12, D), F32)],
        compiler_params=_cp(("arbitrary",)),
    )(*[w[k] for k in names], *[m[k] for k in names], *[v[k] for k in names], wide_all, narrow_all, norm_all)
    groups = [dict(zip(names, res[q * n:(q + 1) * n])) for q in range(4)]
    return groups[0], groups[1], groups[2], groups[3], res[4 * n]


def _pair_sum(mine, theirs, core, name):
    Q, R, C = theirs.shape
    tr, tc = _elementwise_tile(R, C)

    def body(core_ref, a_ref, b_ref, o_ref):
        o_ref[...] = (a_ref[...].astype(F32) + b_ref[...].astype(F32)).astype(BF16)

    spec = pl.BlockSpec((None, tr, tc), lambda q, i, j, core_ref: (q, i, j))
    own = pl.BlockSpec((None, tr, tc), lambda q, i, j, core_ref: (2 * q + core_ref[0], i, j))
    return pl.pallas_call(
        body, name=name,
        grid_spec=pltpu.PrefetchScalarGridSpec(num_scalar_prefetch=1, grid=(Q, R // tr, C // tc), in_specs=[own, spec], out_specs=spec),
        out_shape=jax.ShapeDtypeStruct((Q, R, C), BF16), compiler_params=_cp(("parallel", "parallel", "parallel")),
    )(core, mine, theirs)


def _dev_index(px, py, pc):
    return 4 * px + 2 * py + pc


def _mesh_position():
    return lax.axis_index("x"), lax.axis_index("y"), lax.axis_index("c")


def _gather_exchange(blocks):
    n = len(blocks)

    def make(ins, outs, sems):
        send_sems, recv_sems, local_sems = sems
        x, y, c = _mesh_position()
        me, sibling = (x, y, c), (x, y, 1 - c)
        chips = [(1 - x, y), (x, 1 - y), (1 - x, 1 - y)]

        def copy(a, k, block, to, src=None):
            dst = outs[a].at[_dev_index(*block)]
            return pltpu.make_async_remote_copy(src_ref=dst if src is None else src, dst_ref=dst, send_sem=send_sems.at[a, k],
                                                recv_sem=recv_sems.at[a, k], device_id=to, device_id_type=MESH)

        mine = [pltpu.make_async_copy(ins[a], outs[a].at[_dev_index(*me)], local_sems.at[a]) for a in range(n)]
        first = []
        for a in range(n):
            first.append(copy(a, 0, me, sibling, src=ins[a]))
            first += [copy(a, 1 + j, me, (*chip, c), src=ins[a]) for j, chip in enumerate(chips)]

        def start():
            for cp in mine + first:
                cp.start()

        def finish():
            passed = []
            for j, chip in enumerate(chips):
                for a in range(n):
                    copy(a, 1 + j, (*chip, c), me).wait_recv()
                    fwd = copy(a, 4 + j, (*chip, c), sibling)
                    fwd.start()
                    passed.append(fwd)
            for a in range(n):
                copy(a, 0, sibling, me).wait_recv()
                for j, chip in enumerate(chips):
                    copy(a, 4 + j, (*chip, 1 - c), me).wait_recv()
            for cp in first + passed:
                cp.wait_send()
            for cp in mine:
                cp.wait()

        return start, finish

    return _Exchange(list(blocks), [jax.ShapeDtypeStruct((NDEV,) + b.shape, b.dtype) for b in blocks],
                     [pltpu.SemaphoreType.DMA((n, 7)), pltpu.SemaphoreType.DMA((n, 7)), pltpu.SemaphoreType.DMA((n,))], make)


def _combine(*exchanges):
    def make(ins, outs, sems):
        pairs = []
        for e in exchanges:
            ni, no, ns = len(e.arrays), len(e.out_shape), len(e.sems)
            pairs.append(e.make(ins[:ni], outs[:no], sems[:ns]))
            ins, outs, sems = ins[ni:], outs[no:], sems[ns:]

        def start():
            for s, _ in pairs:
                s()

        def finish():
            for _, f in pairs:
                f()

        return start, finish

    return _Exchange(sum((list(e.arrays) for e in exchanges), []), sum((list(e.out_shape) for e in exchanges), []),
                     sum((list(e.sems) for e in exchanges), []), make)


def _start_wait_all(copies, local=()):
    def start():
        for cp in list(local) + list(copies):
            cp.start()

    def finish():
        for cp in copies:
            cp.wait()
        for cp in local:
            cp.wait()

    return start, finish


def _sibling_exchange(grads):
    n = len(grads)

    def make(ins, outs, sems):
        send_sems, recv_sems = sems
        x, y, c = _mesh_position()
        return _start_wait_all([pltpu.make_async_remote_copy(
            src_ref=ins[a].at[2 * q + (1 - c)], dst_ref=outs[a].at[q], send_sem=send_sems.at[a, q], recv_sem=recv_sems.at[a, q],
            device_id=(x, y, 1 - c), device_id_type=MESH) for a in range(n) for q in range(4)])

    return _Exchange(list(grads), [jax.ShapeDtypeStruct((4,) + g.shape[1:], g.dtype) for g in grads],
                     [pltpu.SemaphoreType.DMA((n, 4)), pltpu.SemaphoreType.DMA((n, 4))], make)


def _chips_exchange(partials):
    n = len(partials)

    def make(ins, outs, sems):
        send_sems, recv_sems = sems
        x, y, c = _mesh_position()
        chips = [(1 - x, y), (x, 1 - y), (1 - x, 1 - y)]
        return _start_wait_all([pltpu.make_async_remote_copy(
            src_ref=ins[a].at[2 * px + py], dst_ref=outs[a].at[k], send_sem=send_sems.at[a, k], recv_sem=recv_sems.at[a, k],
            device_id=(px, py, c), device_id_type=MESH) for a in range(n) for k, (px, py) in enumerate(chips)])

    return _Exchange(list(partials), [jax.ShapeDtypeStruct((3,) + g.shape[1:], g.dtype) for g in partials],
                     [pltpu.SemaphoreType.DMA((n, 3)), pltpu.SemaphoreType.DMA((n, 3))], make)


def _direct_gather_exchange(smalls):
    n = len(smalls)

    def make(ins, outs, sems):
        send_sems, recv_sems, local_sems = sems
        x, y, c = _mesh_position()
        copies, local = [], []
        for a in range(n):
            slot = outs[a].at[_dev_index(x, y, c)]
            local.append(pltpu.make_async_copy(ins[a], slot, local_sems.at[a]))
            for r in range(1, NDEV):
                peer = (x ^ ((r >> 2) & 1), y ^ ((r >> 1) & 1), c ^ (r & 1))
                copies.append(pltpu.make_async_remote_copy(src_ref=ins[a], dst_ref=slot, send_sem=send_sems.at[a, r - 1],
                                                           recv_sem=recv_sems.at[a, r - 1], device_id=peer, device_id_type=MESH))
        return _start_wait_all(copies, local=local)

    return _Exchange(list(smalls), [jax.ShapeDtypeStruct((NDEV,) + s.shape, s.dtype) for s in smalls],
                     [pltpu.SemaphoreType.DMA((n, 7)), pltpu.SemaphoreType.DMA((n, 7)), pltpu.SemaphoreType.DMA((n,))], make)


_W_IN_ROWS = {"u": (0, 1024), "v": (1024, 2048), "za": (2048, 3072), "zb": (3072, 5120), "xbc": (5120, 8192),
              "dt": (8192, 8224), "ga": (8224, 9248), "gb": (9248, 10272)}
_FWD_ORDER = ("xbc", "u", "zb", "v", "za", "ga", "gb", "dt")
_BWD_ORDER = ("xbc", "zb", "u", "v", "za", "ga", "gb", "dt")


def _w_in_t_rows(wt, order):
    assert order[-1] == "dt"
    z = jnp.zeros((NP - NIN, wt.shape[1]), wt.dtype)
    return jnp.concatenate([wt[slice(*_W_IN_ROWS[n])] for n in order] + [z], axis=0)


_WEIGHTS = ["norm_g", "w_in", "ln_a_g", "ln_a_b", "w_s", "b_s", "conv_w", "conv_b", "dt_bias", "a_log", "d_skip", "ssm_norm_g",
            "w_oa", "w_ob", "w_out", "ple_norm_g", "w_pg", "w_ple", "final_g"]


def _rows_pack(d, dtype):
    return jnp.concatenate([d["w_oa"].reshape(128, D), d["w_ob"].reshape(256, D), d["w_out"].reshape(128, D),
                            d["w_pg"].reshape(128, D), d["w_ple"].reshape(32, D)], axis=0).astype(dtype)


def kernel(x, p, norm_g, w_in, ln_a_g, ln_a_b, w_s, b_s, conv_w, conv_b, dt_bias, a_log, d_skip, ssm_norm_g, w_oa, w_ob, w_out, ple_norm_g, w_pg, w_ple, final_g, loss_target, m_norm_g, m_w_in, m_ln_a_g, m_ln_a_b, m_w_s, m_b_s, m_conv_w, m_conv_b, m_dt_bias, m_a_log, m_d_skip, m_ssm_norm_g, m_w_oa, m_w_ob, m_w_out, m_ple_norm_g, m_w_pg, m_w_ple, m_final_g, v_norm_g, v_w_in, v_ln_a_g, v_ln_a_b, v_w_s, v_b_s, v_conv_w, v_conv_b, v_dt_bias, v_a_log, v_d_skip, v_ssm_norm_g, v_w_oa, v_w_ob, v_w_out, v_ple_norm_g, v_w_pg, v_w_ple, v_final_g):
    args = dict(locals())
    w = {n: args[n] for n in _WEIGHTS}
    m = {n: args["m_" + n] for n in _WEIGHTS}
    v = {n: args["v_" + n] for n in _WEIGHTS}
    T = x.shape[1]
    xi, yi, ci = lax.axis_index("x"), lax.axis_index("y"), lax.axis_index("c")
    me = 4 * xi + 2 * yi + ci

    chip = 2 * xi + yi
    x2, p2, tgt = x.reshape(T, D), p.reshape(T, PLE), loss_target.reshape(T, D)

    norm_g2 = w["norm_g"].reshape(1, D)
    ws = jnp.where(jnp.tril(jnp.ones((CH, CH), bool))[None], w["w_s"].reshape(NG, CH, CH), 0.0).astype(BF16)
    wst = jnp.transpose(ws, (0, 2, 1))
    bst = jnp.broadcast_to(w["b_s"].reshape(NG, CH, 1), (NG, CH, 256))
    ln_g, ln_b = w["ln_a_g"].reshape(1, D), w["ln_a_b"].reshape(1, D)
    cb = w["conv_b"].reshape(1, CD)
    pad32 = lambda a: jnp.pad(a.reshape(1, NH), ((0, 0), (0, DTW - NH)))
    dtb, alog = pad32(w["dt_bias"]), pad32(w["a_log"])
    dskx = jnp.repeat(w["d_skip"].reshape(NH), HD).reshape(1, DI)
    sg = w["ssm_norm_g"].reshape(1, DI)
    ple_g, fin_g = w["ple_norm_g"].reshape(1, D), w["final_g"].reshape(1, D)
    e128 = (jnp.arange(DTW)[:, None] == (jnp.arange(DI)[None, :] // HD)).astype(BF16)
    et128 = e128.T
    gsel = ((jnp.arange(D)[:, None] // 256) == jnp.arange(128)[None, :]).astype(BF16)

    w_in_t = lambda a: jnp.transpose(a.reshape(D, WSH))
    (a_all,) = _run_exchange(_gather_exchange([w_in_t(w["w_in"]).astype(BF16)]), "all_gather_w_in")
    w_in_full_t = a_all.reshape(NIN, D)
    wp = _w_in_t_rows(w_in_full_t, _FWD_ORDER)
    wt_bwd = _w_in_t_rows(w_in_full_t, _BWD_ORDER)
    core = ci.reshape(1).astype(jnp.int32)
    (proj, dtr, h), (r_all, cw_all) = _proj_fwd(x2, norm_g2, wp, _gather_exchange([_rows_pack(w, BF16), w["conv_w"].reshape(4, CD // NDEV)]))
    f_oa = r_all[:, R_OA:R_OB].reshape(D, D)
    f_ob = r_all[:, R_OB:R_OUT].reshape(DI, D)
    f_out = r_all[:, R_OUT:R_PG].reshape(D, D)
    f_pg = r_all[:, R_PG:R_PLE].reshape(D, D)
    f_ple = jnp.transpose(r_all[:, R_PLE:R_ROWS].reshape(NDEV, PLE, 128), (1, 0, 2)).reshape(PLE, D)
    cw = jnp.transpose(cw_all, (1, 0, 2)).reshape(4, CD)

    ya = _gmlp_fwd(proj, ln_g, ln_b, ws, bst)
    y, yb, hprev, pre_all = _ssd_fwd(proj, dtr, cw, cb, dtb, alog, dskx, sg, e128)
    dx1, dx1b, mg, hp, dpre, dpe, doa, dob, dya, dyb, dgab, acc = _merge(x2, ya, yb, proj, p2, tgt, f_oa, f_ob, f_out, f_pg, f_ple,
                                                                         ple_g, fin_g)
    loss = lax.psum(acc[2, 0], ("x", "y", "c"))

    gple = jnp.transpose(_wgrad(p2, dpe, "wgrad_ple").reshape(PLE, NDEV, 128), (1, 0, 2)).reshape(NDEV, 32, D)
    gr = jnp.concatenate([_wgrad(ya, doa, "wgrad_oa").reshape(NDEV, 128, D), _wgrad(yb, dob, "wgrad_ob").reshape(NDEV, 256, D),
                          _wgrad(mg, dx1b, "wgrad_out").reshape(NDEV, 128, D), _wgrad(hp, dpre, "wgrad_pg").reshape(NDEV, 128, D),
                          gple], axis=1)
    (duvz, dws, dbs, dln), (from_sib_r,) = _gmlp_bwd(proj, dya, ln_g, ln_b, ws, wst, bst, gsel, _sibling_exchange([gr]))
    pr = _pair_sum(gr, from_sib_r, core, "pair_sum_rows")
    (dxz, ddt, dcw, dsm, dsg), (rr,) = _ssd_bwd(proj, pre_all, dtr, y, dyb, hprev, cw, dtb, alog, dskx, sg, e128, et128, _chips_exchange([pr]))

    g_xz = _wgrad(dxz, h, "wgrad_xbc_zb")
    g_w_in_t = jnp.concatenate([_wgrad(duvz, h, "wgrad_u_v_za"), g_xz[CD:], g_xz[:CD], _wgrad(ddt, h, "wgrad_dt")[:NH],
                                _wgrad(dgab, h, "wgrad_ga_gb")], axis=0)
    ga = g_w_in_t.reshape(NDEV, WSH, D)
    (from_sib_a,) = _run_exchange(_sibling_exchange([ga]), "reduce_scatter_sibling_w_in")
    pa = _pair_sum(ga, from_sib_a, core, "pair_sum_w_in")
    wide = jnp.concatenate([dln, acc, dsg.reshape(16, D), dcw.reshape(24, D)], axis=0)
    narrow = jnp.concatenate([dws.reshape(NG * CH, CH), jnp.pad(dbs[:, :NG].T, ((0, 8 - NG), (0, 0))), dsm], axis=0)
    (gx, dng), (ra, wide_all, narrow_all) = _proj_bwd(x2, dx1, norm_g2, wt_bwd, [dxz, duvz, dgab], ddt,
                                                      _combine(_chips_exchange([pa]), _direct_gather_exchange([wide, narrow])))
    (norm_all,) = _run_exchange(_direct_gather_exchange([dng]), "all_gather_d_norm_g")
    pa_own = lax.dynamic_index_in_dim(pa, chip, 0, keepdims=False)
    pr_own = lax.dynamic_index_in_dim(pr, chip, 0, keepdims=False)

    out_g, out_d, out_m, out_v = {}, {}, {}, {}
    outs = (out_g, out_d, out_m, out_v)
    res = _adamw(w_in_t(w["w_in"]), w_in_t(m["w_in"]), w_in_t(v["w_in"]), [pa_own, (ra, 0), (ra, 1), (ra, 2)], "adamw_w_in")
    for dst, val in zip(outs, res):
        dst["w_in"] = jnp.transpose(val).reshape(1, D, WSH)
    parts_r = [pr_own, (rr, 0), (rr, 1), (rr, 2)]
    for name, row, rows in (("w_oa", R_OA, 128), ("w_ob", R_OB, 256), ("w_out", R_OUT, 128), ("w_pg", R_PG, 128)):
        res = _adamw(w[name].reshape(rows, D), m[name].reshape(rows, D), v[name].reshape(rows, D), parts_r, "adamw_" + name,
                     part_row=row, block_rows=128)
        for dst, val in zip(outs, res):
            dst[name] = val.reshape(1, rows, D)
    res = _adamw(w["w_ple"].reshape(32, D), m["w_ple"].reshape(32, D), v["w_ple"].reshape(32, D), parts_r, "adamw_w_ple",
                 part_row=R_PLE, block_rows=32)
    for dst, val in zip(outs, res):
        dst["w_ple"] = val.reshape(1, PLE, 128)
    two_d = lambda d: {n: d[n].reshape(_SMALL_SHAPES[n]) for n in _SMALL_SHAPES}
    *res, g_cw_wide = _adamw_small(two_d(w), two_d(m), two_d(v), wide_all, narrow_all, norm_all)
    for dst, val in zip(outs, res):
        dst.update({n: val[n].reshape(w[n].shape) for n in _SMALL_SHAPES})
    g_cw = lax.dynamic_slice_in_dim(g_cw_wide.reshape(4, CD), me * (CD // NDEV), CD // NDEV, axis=1).reshape(12, 128)
    res = _adamw(w["conv_w"].reshape(12, 128), m["conv_w"].reshape(12, 128), v["conv_w"].reshape(12, 128), [g_cw], "adamw_conv_w")
    for dst, val in zip((out_g, out_d, out_m, out_v), res):
        dst["conv_w"] = val.reshape(1, 4, CD // NDEV)

    return (loss, gx.reshape(1, T, D), *[out_g[n] for n in _WEIGHTS], *[out_d[n] for n in _WEIGHTS],
            *[out_m[n] for n in _WEIGHTS], *[out_v[n] for n in _WEIGHTS])
```

```python
import functools
import math
from typing import Callable, NamedTuple

import jax
import jax.numpy as jnp
from jax import lax
from jax.experimental import pallas as pl
from jax.experimental.pallas import tpu as pltpu

F32 = jnp.float32
BF16 = jnp.bfloat16
MESH = pl.DeviceIdType.MESH

D = 1024
DI = 2048
CD = 3072
NH = 32
HD = 64
NST = 128
NG = 4
CH = 128
PLE = 256
NIN = 10272
NDEV = 8
WSH = NIN // NDEV
EPS = 1e-6
OFF_XBC, OFF_U, OFF_ZB, OFF_V, OFF_ZA, OFF_GA, OFF_GB, OFF_DT = 0, 3072, 4096, 6144, 7168, 8192, 9216, 10240
NP = 10368
DTW = 128
R_OA, R_OB, R_OUT, R_PG, R_PLE, R_ROWS = 0, 128, 384, 512, 640, 672

ADAM_LR, ADAM_B1, ADAM_B2, ADAM_EPS, ADAM_WD, ADAM_STEP = 0.001, 0.9, 0.999, 1e-08, 0.01, 10

V7X_VMEM_LIMIT = 56 * 1024 * 1024


def _cp(sem=None):
    return pltpu.CompilerParams(dimension_semantics=sem, vmem_limit_bytes=V7X_VMEM_LIMIT)


def _dot(a, b, prec=None):
    return jnp.dot(a, b, preferred_element_type=F32, precision=prec)


def _dot_nt(a, b, prec=None):
    return lax.dot_general(a, b, (((1,), (1,)), ((), ())), preferred_element_type=F32, precision=prec)


def _dot_tn(a, b, prec=None):
    return lax.dot_general(a, b, (((0,), (0,)), ((), ())), preferred_element_type=F32, precision=prec)


def _sigmoid(x):
    return 1.0 / (1.0 + jnp.exp(-x))


def _gelu_and_grad(x):
    c = math.sqrt(2.0 / math.pi)
    x2 = x * x
    t = jnp.tanh(c * (x + 0.044715 * x * x2))
    g = 0.5 * x * (1.0 + t)
    dg = 0.5 * (1.0 + t) + 0.5 * x * (1.0 - t * t) * c * (1.0 + 3.0 * 0.044715 * x2)
    return g, dg


def _gelu(x):
    c = math.sqrt(2.0 / math.pi)
    return 0.5 * x * (1.0 + jnp.tanh(c * (x + 0.044715 * x * x * x)))


def _softplus(x):
    return jnp.maximum(x, 0.0) + jnp.log(1.0 + jnp.exp(-jnp.abs(x)))


def _full(shape):
    n = len(shape)
    return pl.BlockSpec(shape, lambda *_: (0,) * n)


_ANY = pl.BlockSpec(memory_space=pl.ANY)


class _Exchange(NamedTuple):
    arrays: list
    out_shape: list
    sems: list
    make: Callable


def _call(body, ex, *, name, grid, in_specs, out_specs, out_shape, scratch_shapes, args):
    ki, ko, ks = len(in_specs), len(out_specs), len(scratch_shapes)
    ei, eo = len(ex.arrays), len(ex.out_shape)
    last = [g - 1 for g in grid]

    def full_body(*refs):
        r = list(refs)
        ins, eins, r = r[:ki], r[ki:ki + ei], r[ki + ei:]
        outs, eouts, r = r[:ko], r[ko:ko + eo], r[ko + eo:]
        scr, esems = r[:ks], r[ks:]
        start, finish = ex.make(eins, eouts, esems)
        ids = [pl.program_id(a) for a in range(len(grid))]
        is_first = functools.reduce(lambda p, q: p & q, [i == 0 for i in ids])
        is_last = functools.reduce(lambda p, q: p & q, [i == l for i, l in zip(ids, last)])
        pl.when(is_first)(start)
        body(*ins, *outs, *scr)
        pl.when(is_last)(finish)

    res = pl.pallas_call(
        full_body, name=name, grid=grid, in_specs=list(in_specs) + [_ANY] * ei, out_specs=list(out_specs) + [_ANY] * eo,
        out_shape=list(out_shape) + list(ex.out_shape), scratch_shapes=list(scratch_shapes) + list(ex.sems),
        compiler_params=_cp(("arbitrary",) * len(grid)),
    )(*args, *ex.arrays)
    return res[:ko], res[ko:]


def _run_exchange(ex, name):
    ni, no = len(ex.arrays), len(ex.out_shape)

    def body(*refs):
        start, finish = ex.make(refs[:ni], refs[ni:ni + no], refs[ni + no:])
        start()
        finish()

    return pl.pallas_call(body, name=name, in_specs=[_ANY] * ni, out_specs=[_ANY] * no, out_shape=list(ex.out_shape),
                          scratch_shapes=list(ex.sems))(*ex.arrays)


def _proj_fwd(x, norm_g, wp, ex):
    T = x.shape[0]
    tm, tn = min(T, 1024), 1152
    nj = NP // tn
    assert OFF_DT + DTW == NP

    def body(x_ref, g_ref, w_ref, proj_ref, dt_ref, h_ref, hs_ref):
        j = pl.program_id(1)

        @pl.when(j == 0)
        def _():
            xf = x_ref[...]
            r = lax.rsqrt(jnp.mean(xf * xf, axis=-1, keepdims=True) + EPS)
            h = (xf * r * g_ref[...]).astype(BF16)
            hs_ref[...] = h
            h_ref[...] = h

        acc = _dot_nt(hs_ref[...], w_ref[...])
        proj_ref[...] = acc.astype(BF16)

        @pl.when(j == nj - 1)
        def _():
            dt_ref[...] = acc[:, tn - DTW:]

    return _call(
        body, ex, name="proj_fwd", grid=(T // tm, nj),
        in_specs=[pl.BlockSpec((tm, D), lambda i, j: (i, 0)), _full((1, D)), pl.BlockSpec((tn, D), lambda i, j: (j, 0))],
        out_specs=[pl.BlockSpec((tm, tn), lambda i, j: (i, j)), pl.BlockSpec((tm, DTW), lambda i, j: (i, 0)),
                   pl.BlockSpec((tm, D), lambda i, j: (i, 0))],
        out_shape=[jax.ShapeDtypeStruct((T, NP), BF16), jax.ShapeDtypeStruct((T, DTW), F32), jax.ShapeDtypeStruct((T, D), BF16)],
        scratch_shapes=[pltpu.VMEM((tm, D), BF16)], args=(x, norm_g, wp))


def _gmlp_tile():
    return 256


def _gmlp_fwd(proj, ln_g, ln_b, ws, bst):
    T = proj.shape[0]
    tm = min(T, _gmlp_tile())

    def body(u_ref, v_ref, z_ref, lg_ref, lb_ref, ws_ref, bs_ref, ya_ref, vn_s):
        vg = _gelu(v_ref[...].astype(F32))
        mu = jnp.mean(vg, axis=-1, keepdims=True)
        xc = vg - mu
        rstd = lax.rsqrt(jnp.mean(xc * xc, axis=-1, keepdims=True) + EPS)
        vn_s[...] = (xc * rstd * lg_ref[...] + lb_ref[...]).astype(BF16)
        for c in range(tm // CH):
            rs = slice(c * CH, (c + 1) * CH)
            for g in range(NG):
                cs_ = slice(g * 256, (g + 1) * 256)
                sv = _dot(ws_ref[g], vn_s[rs, cs_]) + bs_ref[g]
                z = z_ref[rs, cs_].astype(F32)
                ya_ref[rs, cs_] = (_gelu(u_ref[rs, cs_].astype(F32)) * sv * (z * _sigmoid(z))).astype(BF16)

    blk = lambda off: pl.BlockSpec((tm, D), lambda i: (i, off // D))
    return pl.pallas_call(
        body, name="gmlp_fwd", grid=(T // tm,),
        in_specs=[blk(OFF_U), blk(OFF_V), blk(OFF_ZA), _full((1, D)), _full((1, D)), _full((NG, CH, CH)), _full((NG, CH, 256))],
        out_specs=pl.BlockSpec((tm, D), lambda i: (i, 0)),
        out_shape=jax.ShapeDtypeStruct((T, D), BF16),
        scratch_shapes=[pltpu.VMEM((tm, D), BF16)],
        compiler_params=_cp(("parallel",)),
    )(proj, proj, proj, ln_g, ln_b, ws, bst)


def _gmlp_bwd(proj, dya, ln_g, ln_b, ws, wst, bst, gsel, ex):
    T = proj.shape[0]
    tm = min(T, _gmlp_tile())

    def body(u_ref, v_ref, z_ref, dy_ref, lg_ref, lb_ref, ws_ref, wst_ref, bs_ref, gsel_ref,
             d_ref, dws_ref, dbs_ref, dln_ref, vn_s, dsv_s, dvn_s):
        du_ref, dv_ref, dz_ref = d_ref.at[:, 0:D], d_ref.at[:, D:2 * D], d_ref.at[:, 2 * D:3 * D]
        @pl.when(pl.program_id(0) == 0)
        def _():
            dws_ref[...] = jnp.zeros_like(dws_ref)
            dbs_ref[...] = jnp.zeros_like(dbs_ref)
            dln_ref[...] = jnp.zeros_like(dln_ref)

        vg, dvg_dv = _gelu_and_grad(v_ref[...].astype(F32))
        mu = jnp.mean(vg, axis=-1, keepdims=True)
        xc = vg - mu
        rstd = lax.rsqrt(jnp.mean(xc * xc, axis=-1, keepdims=True) + EPS)
        vhat = xc * rstd
        vn_s[...] = (vhat * lg_ref[...] + lb_ref[...]).astype(BF16)
        ri = lax.broadcasted_iota(jnp.int32, (CH, CH), 0)
        ci = lax.broadcasted_iota(jnp.int32, (CH, CH), 1)
        tril = (ri >= ci).astype(F32)
        for c in range(tm // CH):
            rs = slice(c * CH, (c + 1) * CH)
            for g in range(NG):
                cs_ = slice(g * 256, (g + 1) * 256)
                vn = vn_s[rs, cs_]
                sv = _dot(ws_ref[g], vn) + bs_ref[g]
                z = z_ref[rs, cs_].astype(F32)
                sz = _sigmoid(z)
                ug, dug_du = _gelu_and_grad(u_ref[rs, cs_].astype(F32))
                dy = dy_ref[rs, cs_].astype(F32)
                t = dy * z * sz
                du_ref[rs, cs_] = (t * sv * dug_du).astype(BF16)
                dz_ref[rs, cs_] = (dy * ug * sv * sz * (1.0 + z * (1.0 - sz))).astype(BF16)
                dsv = (t * ug).astype(BF16)
                dsv_s[rs, cs_] = dsv
                dvn_s[rs, cs_] = _dot(wst_ref[g], dsv)
                dws_ref[g] += _dot_nt(dsv, vn) * tril
            dbs_ref[...] += _dot(dsv_s[rs, :], gsel_ref[...])
        dvn = dvn_s[...]
        dln_ref[0:1, :] += jnp.sum(dvn * vhat, axis=0, keepdims=True)
        dln_ref[1:2, :] += jnp.sum(dvn, axis=0, keepdims=True)
        dvh = dvn * lg_ref[...]
        dvg = rstd * (dvh - jnp.mean(dvh, axis=-1, keepdims=True) - vhat * jnp.mean(dvh * vhat, axis=-1, keepdims=True))
        dv_ref[...] = (dvg * dvg_dv).astype(BF16)

    blk = lambda off: pl.BlockSpec((tm, D), lambda i: (i, off // D))
    row = pl.BlockSpec((tm, D), lambda i: (i, 0))
    return _call(
        body, ex, name="gmlp_bwd", grid=(T // tm,),
        in_specs=[blk(OFF_U), blk(OFF_V), blk(OFF_ZA), row, _full((1, D)), _full((1, D)), _full((NG, CH, CH)),
                  _full((NG, CH, CH)), _full((NG, CH, 256)), _full((D, 128))],
        out_specs=[pl.BlockSpec((tm, 3 * D), lambda i: (i, 0)), _full((NG, CH, CH)), _full((CH, 128)), _full((8, D))],
        out_shape=[jax.ShapeDtypeStruct((T, 3 * D), BF16),
                   jax.ShapeDtypeStruct((NG, CH, CH), F32), jax.ShapeDtypeStruct((CH, 128), F32), jax.ShapeDtypeStruct((8, D), F32)],
        scratch_shapes=[pltpu.VMEM((tm, D), BF16), pltpu.VMEM((tm, D), BF16), pltpu.VMEM((tm, D), F32)],
        args=(proj, proj, proj, dya, ln_g, ln_b, ws, wst, bst, gsel))


def _shift_rows(cur, edge, j, down):
    r8 = lax.broadcasted_iota(jnp.int32, (8, 1), 0)
    if down:
        body = pltpu.roll(cur, j, 0)
        return jnp.concatenate([jnp.where(r8 >= j, body[0:8], pltpu.roll(edge, j, 0)), body[8:]], axis=0)
    body = pltpu.roll(cur, CH - j, 0)
    return jnp.concatenate([body[:CH - 8], jnp.where(r8 < 8 - j, body[CH - 8:], pltpu.roll(edge, 8 - j, 0))], axis=0)


def _conv_pre(x, x_before, cw_ref, cb_ref):
    pre = cb_ref[...] + cw_ref[3:4, :] * x
    for j in (1, 2, 3):
        pre = pre + cw_ref[3 - j:4 - j, :] * _shift_rows(x, x_before, j, down=True)
    return pre


def _split_dot(x, w, parts, w_left=False):
    acc, r = None, x
    for k in range(parts):
        hi = r.astype(BF16)
        d = _dot(w, hi) if w_left else _dot(hi, w)
        acc = d if acc is None else acc + d
        if k + 1 < parts:
            r = r - hi.astype(F32)
    return acc


def _chunk_decays(dt, alog_ref, e_ref, cs_s, cst_s, csx_s):
    a = -jnp.exp(alog_ref[...])
    ri = lax.broadcasted_iota(jnp.int32, (CH, CH), 0)
    ci = lax.broadcasted_iota(jnp.int32, (CH, CH), 1)
    tril = ri >= ci
    cs = _split_dot(dt * a, tril.astype(BF16), 3, w_left=True)
    cs_s[...] = cs
    cst_s[...] = cs.T
    csx_s[...] = _split_dot(cs, e_ref[...], 3)
    return a, tril, ri, ci


def _lmat(cst_s, h, tril):
    rowb = jnp.broadcast_to(cst_s[h:h + 1, :], (CH, CH))
    return jnp.exp(jnp.where(tril, rowb.T - rowb, -jnp.inf))


def _head_pair_rows(v, lane):
    return jnp.concatenate([jnp.where(lane < HD, v, 0.0), jnp.where(lane < HD, 0.0, v)], axis=0).astype(BF16)


def _ssd_fwd(proj, dtr, cw, cb, dtb, alog, dskx, sg, e128):
    T = proj.shape[0]
    nc = T // CH

    def body(xbc_ref, zb_ref, dt_ref, cw_ref, cb_ref, dtb_ref, alog_ref, dx_ref, sg_ref, e_ref,
             y_ref, yb_ref, hp_ref, pre_ref, xprev_s, h_s, cs_s, cst_s, csx_s, yz_s):
        @pl.when(pl.program_id(0) == 0)
        def _():
            xprev_s[...] = jnp.zeros_like(xprev_s)
            h_s[...] = jnp.zeros_like(h_s)

        x = xbc_ref[...].astype(F32)
        pre = _conv_pre(x, xprev_s[...], cw_ref, cb_ref)
        pre_ref[...] = pre
        xprev_s[...] = x[CH - 8:]
        xc = pre * _sigmoid(pre)
        dt = _softplus(dt_ref[...] + dtb_ref[...])
        a, tril, _, lane = _chunk_decays(dt, alog_ref, e_ref, cs_s, cst_s, csx_s)
        dt_x = _split_dot(dt, e_ref[...], 2)
        cs_last_x = csx_s[CH - 1:CH, :]
        hp_ref[0] = h_s[...]
        for g in range(NG):
            gs = slice(g * 512, (g + 1) * 512)
            bg = xc[:, DI + g * NST:DI + (g + 1) * NST].astype(BF16)
            cg = xc[:, DI + 512 + g * NST:DI + 512 + (g + 1) * NST].astype(BF16)
            cbm = _dot_nt(cg, bg)
            xg = xc[:, gs]
            xdt = xg * dt_x[:, gs]
            hprev = h_s[:, gs]
            csx = csx_s[:, gs]
            yoff = _dot(cg, hprev.astype(BF16)) * jnp.exp(csx)
            st = _dot_tn(bg, (xdt * jnp.exp(cs_last_x[:, gs] - csx)).astype(BF16))
            h_s[:, gs] = jnp.exp(cs_last_x[:, gs]) * hprev + st
            ssq = jnp.zeros((CH, 1), F32)
            for q in range(4):
                h0 = g * 8 + 2 * q
                ps = slice(q * 128, (q + 1) * 128)
                cols = slice(g * 512 + q * 128, g * 512 + (q + 1) * 128)
                m01 = jnp.concatenate([cbm * _lmat(cst_s, h0, tril), cbm * _lmat(cst_s, h0 + 1, tril)], axis=1).astype(BF16)
                yq = _dot(m01, _head_pair_rows(xdt[:, ps], lane)) + yoff[:, ps] + xg[:, ps] * dx_ref[:, cols]
                y_ref[:, cols] = yq
                z = zb_ref[:, cols].astype(F32)
                yz = yq * z * _sigmoid(z)
                yz_s[:, cols] = yz
                ssq = ssq + jnp.sum(yz * yz, axis=1, keepdims=True)
            rg = lax.rsqrt(ssq * (1.0 / 512.0) + EPS)
            yb_ref[:, gs] = (yz_s[:, gs] * rg * sg_ref[:, gs]).astype(BF16)

    return pl.pallas_call(
        body, name="ssd_fwd", grid=(nc,),
        in_specs=[pl.BlockSpec((CH, CD), lambda c: (c, OFF_XBC // CD)), pl.BlockSpec((CH, DI), lambda c: (c, OFF_ZB // DI)),
                  pl.BlockSpec((CH, DTW), lambda c: (c, 0)), _full((4, CD)), _full((1, CD)), _full((1, DTW)),
                  _full((1, DTW)), _full((1, DI)), _full((1, DI)), _full((DTW, DI))],
        out_specs=[pl.BlockSpec((CH, DI), lambda c: (c, 0)), pl.BlockSpec((CH, DI), lambda c: (c, 0)),
                   pl.BlockSpec((1, NST, DI), lambda c: (c, 0, 0)), pl.BlockSpec((CH, CD), lambda c: (c, 0))],
        out_shape=[jax.ShapeDtypeStruct((T, DI), F32), jax.ShapeDtypeStruct((T, DI), BF16),
                   jax.ShapeDtypeStruct((nc, NST, DI), F32), jax.ShapeDtypeStruct((T, CD), F32)],
        scratch_shapes=[pltpu.VMEM((8, CD), F32), pltpu.VMEM((NST, DI), F32), pltpu.VMEM((CH, CH), F32),
                        pltpu.VMEM((CH, CH), F32), pltpu.VMEM((CH, DI), F32), pltpu.VMEM((CH, DI), F32)],
        compiler_params=_cp(("arbitrary",)),
    )(proj, proj, dtr, cw, cb, dtb, alog, dskx, sg, e128)


def _ssd_bwd(proj, pre_all, dtr, y, dyb, hprev_all, cw, dtb, alog, dskx, sg, e128, et128, ex):
    T = proj.shape[0]
    nc = T // CH

    def body(xbc_ref, pre_ref, zb_ref, dt_ref, y_ref, dyb_ref, hp_ref, cw_ref, dtb_ref, alog_ref, dx_ref, sg_ref,
             e_ref, et_ref, d_ref, ddt_ref, dcw_ref, dsm_ref, dsg_ref,
             g_s, dpn_s, cs_s, cst_s, csx_s, dy_s, dxdt_s, dxs_s, dsd_s, dxc_s, gh_s):
        dxbc_ref, dzb_ref = d_ref.at[:, 0:CD], d_ref.at[:, CD:CD + DI]
        i = pl.program_id(0)

        @pl.when(i == 0)
        def _():
            g_s[...] = jnp.zeros_like(g_s)
            dpn_s[...] = jnp.zeros_like(dpn_s)
            dcw_ref[...] = jnp.zeros_like(dcw_ref)
            dsm_ref[...] = jnp.zeros_like(dsm_ref)
            dsg_ref[...] = jnp.zeros_like(dsg_ref)

        pre = pre_ref[...]
        sp = _sigmoid(pre)
        xc = pre * sp
        dtr = dt_ref[...] + dtb_ref[...]
        dt = _softplus(dtr)
        a, tril, ri, lane = _chunk_decays(dt, alog_ref, e_ref, cs_s, cst_s, csx_s)
        et = et_ref[...]
        dt_x = _split_dot(dt, e_ref[...], 2)
        cs_last_x = csx_s[CH - 1:CH, :]

        for g in range(NG):
            gs = slice(g * 512, (g + 1) * 512)
            z = zb_ref[:, gs].astype(F32)
            sz = _sigmoid(z)
            yv = y_ref[:, gs]
            yz = yv * z * sz
            rg = lax.rsqrt(jnp.mean(yz * yz, axis=-1, keepdims=True) + EPS)
            yn = yz * rg
            dyb = dyb_ref[:, gs].astype(F32)
            dsg_ref[0:1, gs] += jnp.sum(dyb * yn, axis=0, keepdims=True)
            dyn = dyb * sg_ref[:, gs]
            dyz = rg * (dyn - yn * jnp.mean(dyn * yn, axis=-1, keepdims=True))
            dy_s[:, gs] = dyz * z * sz
            dzb_ref[:, gs] = (dyz * yv * sz * (1.0 + z * (1.0 - sz))).astype(BF16)

        rsum = jnp.zeros((CH, DTW), F32)
        csum_t = jnp.zeros((DTW, CH), F32)
        for g in range(NG):
            gs = slice(g * 512, (g + 1) * 512)
            bg = xc[:, DI + g * NST:DI + (g + 1) * NST].astype(BF16)
            cg = xc[:, DI + 512 + g * NST:DI + 512 + (g + 1) * NST].astype(BF16)
            cbm = _dot_nt(cg, bg)
            xdt = xc[:, gs] * dt_x[:, gs]
            hprev = hp_ref[0, :, gs]
            hpb = hprev.astype(BF16)
            gn = g_s[:, gs]
            gnb = gn.astype(BF16)
            dy = dy_s[:, gs]
            csx = csx_s[:, gs]
            ecs = jnp.exp(csx)
            dec = jnp.exp(cs_last_x[:, gs] - csx)
            dye = (dy * ecs).astype(BF16)
            dc = _dot_nt(dye, hpb)
            dprev = _dot_tn(cg, dye)
            dxdt_state = dec * _dot(bg, gnb)
            db = _dot_nt((xdt * dec).astype(BF16), gnb)
            dcb = jnp.zeros((CH, CH), F32)
            for q in range(4):
                h0 = g * 8 + 2 * q
                ps = slice(q * 128, (q + 1) * 128)
                dyp = dy[:, ps]
                l0 = _lmat(cst_s, h0, tril)
                l1 = _lmat(cst_s, h0 + 1, tril)
                m0 = cbm * l0
                m1 = cbm * l1
                dm = _dot_nt(dyp.astype(BF16), _head_pair_rows(xdt[:, ps], lane))
                dm0 = dm[:, :CH]
                dm1 = dm[:, CH:]
                dcb = dcb + dm0 * l0 + dm1 * l1
                for hh, qm in ((h0, dm0 * m0), (h0 + 1, dm1 * m1)):
                    rsum = jnp.where(lane == hh, jnp.sum(qm, axis=1, keepdims=True), rsum)
                    csum_t = jnp.where(ri == hh, jnp.sum(qm, axis=0, keepdims=True), csum_t)
                mst = jnp.concatenate([m0, m1], axis=0).astype(BF16)
                d = _dot_tn(mst, _head_pair_rows(dyp, lane))
                dxdt_s[:, g * 512 + q * 128:g * 512 + (q + 1) * 128] = d + dxdt_state[:, ps]
            yoff = _dot(cg, hpb) * ecs
            dsd_s[:, gs] = xdt * dxdt_state
            dxs_s[:, gs] = dy * yoff
            dcbb = dcb.astype(BF16)
            dxc_s[:, DI + 512 + g * NST:DI + 512 + (g + 1) * NST] = dc + _dot(dcbb, bg)
            dxc_s[:, DI + g * NST:DI + (g + 1) * NST] = db + _dot_tn(dcbb, cg)
            gh_s[:, gs] = jnp.broadcast_to(jnp.sum(gn * hprev, axis=0, keepdims=True), (8, 512))
            g_s[:, gs] = dprev + jnp.exp(cs_last_x[:, gs]) * gn

        xs = xc[:, :DI]
        dy = dy_s[...]
        dxdt = dxdt_s[...]
        cs_last = cs_s[CH - 1:CH, :]
        state_e = _split_dot(dsd_s[...], et, 2)
        dcd = 0.125 * jnp.sum(_split_dot(gh_s[...], et, 2), axis=0, keepdims=True) * jnp.exp(cs_last)
        row = lax.broadcasted_iota(jnp.int32, (CH, 1), 0)
        dcs = rsum - csum_t.T + _split_dot(dxs_s[...], et, 2) - state_e
        dcs = dcs + jnp.where(row == CH - 1, jnp.sum(state_e, axis=0, keepdims=True) + dcd, 0.0)
        dda = _split_dot(dcs, (lane >= ri).astype(BF16), 3, w_left=True)
        ddt = dda * a + _dot((dxdt * xs).astype(BF16), et)
        ddtr = jnp.where(lane < NH, ddt * _sigmoid(dtr), 0.0)
        ddt_ref[...] = ddtr.astype(BF16)
        dsm_ref[0:1, :] += jnp.sum(ddtr, axis=0, keepdims=True)
        dsm_ref[1:2, :] += jnp.sum(dda * dt, axis=0, keepdims=True) * a
        dsm_ref[2:3, :] += jnp.sum(_dot((dy * xs).astype(BF16), et), axis=0, keepdims=True)
        dxc_s[:, :DI] = dxdt * dt_x + dy * dx_ref[...]

        dpre = dxc_s[...] * sp * (1.0 + pre * (1.0 - sp))
        dpn = dpn_s[...]
        x = xbc_ref[...].astype(F32)
        dcw_ref[4:5, :] += jnp.sum(dpre, axis=0, keepdims=True)
        dxbc = cw_ref[3:4, :] * dpre
        dcw_ref[3:4, :] += jnp.sum(dpre * x, axis=0, keepdims=True)
        for j in (1, 2, 3):
            ahead = _shift_rows(dpre, dpn, j, down=False)
            dcw_ref[3 - j:4 - j, :] += jnp.sum(ahead * x, axis=0, keepdims=True)
            dxbc = dxbc + cw_ref[3 - j:4 - j, :] * ahead
        dpn_s[...] = dpre[0:8]
        dxbc_ref[...] = dxbc.astype(BF16)

    rev = lambda c: nc - 1 - c
    return _call(
        body, ex, name="ssd_bwd", grid=(nc,),
        in_specs=[pl.BlockSpec((CH, CD), lambda c: (rev(c), OFF_XBC // CD)),
                  pl.BlockSpec((CH, CD), lambda c: (rev(c), 0)),
                  pl.BlockSpec((CH, DI), lambda c: (rev(c), OFF_ZB // DI)),
                  pl.BlockSpec((CH, DTW), lambda c: (rev(c), 0)),
                  pl.BlockSpec((CH, DI), lambda c: (rev(c), 0)), pl.BlockSpec((CH, DI), lambda c: (rev(c), 0)),
                  pl.BlockSpec((1, NST, DI), lambda c: (rev(c), 0, 0)),
                  _full((4, CD)), _full((1, DTW)), _full((1, DTW)), _full((1, DI)), _full((1, DI)),
                  _full((DTW, DI)), _full((DI, DTW))],
        out_specs=[pl.BlockSpec((CH, CD + DI), lambda c: (rev(c), 0)),
                   pl.BlockSpec((CH, DTW), lambda c: (rev(c), 0)), _full((8, CD)), _full((8, DTW)), _full((8, DI))],
        out_shape=[jax.ShapeDtypeStruct((T, CD + DI), BF16), jax.ShapeDtypeStruct((T, DTW), BF16),
                   jax.ShapeDtypeStruct((8, CD), F32), jax.ShapeDtypeStruct((8, DTW), F32), jax.ShapeDtypeStruct((8, DI), F32)],
        scratch_shapes=[pltpu.VMEM((NST, DI), F32), pltpu.VMEM((8, CD), F32), pltpu.VMEM((CH, CH), F32), pltpu.VMEM((CH, CH), F32),
                        pltpu.VMEM((CH, DI), F32), pltpu.VMEM((CH, DI), F32), pltpu.VMEM((CH, DI), F32), pltpu.VMEM((CH, DI), F32),
                        pltpu.VMEM((CH, DI), F32), pltpu.VMEM((CH, CD), F32), pltpu.VMEM((8, DI), F32)],
        args=(proj, pre_all, proj, dtr, y, dyb, hprev_all, cw, dtb, alog, dskx, sg, e128, et128))


def _merge_tile():
    return 256


def _merge(x, ya, yb, proj, p, tgt, w_oa, w_ob, w_out, w_pg, w_ple, ple_g, fin_g):
    T = x.shape[0]
    tm = min(T, _merge_tile())

    def body(x_ref, ya_ref, yb_ref, ga_ref, gb_ref, p_ref, t_ref, woa, wob, wout, wpg, wple, pg_ref, fg_ref,
             dx1_ref, dx1b_ref, mg_ref, hp_ref, dpre_ref, dpe_ref, doa_ref, dob_ref, dya_ref, dyb_ref, dg_ref, acc_ref):
        @pl.when(pl.program_id(0) == 0)
        def _():
            acc_ref[...] = jnp.zeros_like(acc_ref)

        oa = _dot(ya_ref[...], woa[...])
        ob = _dot(yb_ref[...], wob[...])
        sa = _sigmoid(ga_ref[...].astype(F32))
        sb = _sigmoid(gb_ref[...].astype(F32))
        mg = sa * oa + sb * ob
        mgb = mg.astype(BF16)
        mg_ref[...] = mgb
        x1 = x_ref[...] + _dot(mgb, wout[...])
        r2 = lax.rsqrt(jnp.mean(x1 * x1, axis=-1, keepdims=True) + EPS)
        xh1 = x1 * r2
        hpb = (xh1 * pg_ref[...]).astype(BF16)
        hp_ref[...] = hpb
        gate = _sigmoid(_dot(hpb, wpg[...]))
        pe = _dot(p_ref[...].astype(BF16), wple[...])
        x2 = x1 + gate * pe
        r3 = lax.rsqrt(jnp.mean(x2 * x2, axis=-1, keepdims=True) + EPS)
        xh2 = x2 * r3
        err = xh2 * fg_ref[...] - t_ref[...]
        acc_ref[2:3, :] += 0.5 * jnp.sum(jnp.mean(err * err, axis=-1, keepdims=True))
        dyo = err * (1.0 / D)
        acc_ref[0:1, :] += jnp.sum(dyo * xh2, axis=0, keepdims=True)
        dn = dyo * fg_ref[...]
        dx2 = r3 * (dn - xh2 * jnp.mean(dn * xh2, axis=-1, keepdims=True))
        dpe_ref[...] = (dx2 * gate).astype(BF16)
        dpre = (dx2 * pe * gate * (1.0 - gate)).astype(BF16)
        dpre_ref[...] = dpre
        dhp = _dot_nt(dpre, wpg[...])
        acc_ref[1:2, :] += jnp.sum(dhp * xh1, axis=0, keepdims=True)
        dhn = dhp * pg_ref[...]
        dx1 = dx2 + r2 * (dhn - xh1 * jnp.mean(dhn * xh1, axis=-1, keepdims=True))
        dx1_ref[...] = dx1
        dx1b = dx1.astype(BF16)
        dx1b_ref[...] = dx1b
        dmg = _dot_nt(dx1b, wout[...])
        doa = (dmg * sa).astype(BF16)
        dob = (dmg * sb).astype(BF16)
        doa_ref[...] = doa
        dob_ref[...] = dob
        dg_ref[:, :D] = (dmg * oa * sa * (1.0 - sa)).astype(BF16)
        dg_ref[:, D:] = (dmg * ob * sb * (1.0 - sb)).astype(BF16)
        dya_ref[...] = _dot_nt(doa, woa[...]).astype(BF16)
        dyb_ref[...] = _dot_nt(dob, wob[...]).astype(BF16)

    row = lambda w: pl.BlockSpec((tm, w), lambda i: (i, 0))
    wsp = lambda s: pl.BlockSpec(s, lambda i: (0, 0), pipeline_mode=pl.Buffered(1))
    return pl.pallas_call(
        body, name="merge", grid=(T // tm,),
        in_specs=[row(D), row(D), row(DI), pl.BlockSpec((tm, D), lambda i: (i, OFF_GA // D)),
                  pl.BlockSpec((tm, D), lambda i: (i, OFF_GB // D)), row(PLE), row(D),
                  wsp((D, D)), wsp((DI, D)), wsp((D, D)), wsp((D, D)), wsp((PLE, D)), _full((1, D)), _full((1, D))],
        out_specs=[row(D)] * 9 + [row(DI), row(2 * D), _full((8, D))],
        out_shape=[jax.ShapeDtypeStruct((T, D), F32)] + [jax.ShapeDtypeStruct((T, D), BF16)] * 8
        + [jax.ShapeDtypeStruct((T, DI), BF16), jax.ShapeDtypeStruct((T, 2 * D), BF16), jax.ShapeDtypeStruct((8, D), F32)],
        compiler_params=_cp(("arbitrary",)),
    )(x, ya, yb, proj, proj, p, tgt, w_oa, w_ob, w_out, w_pg, w_ple, ple_g, fin_g)


def _wgrad(a, b, name):
    T, K = a.shape
    N = b.shape[1]
    tt, tk, tn = min(T, 2048), min(K, 1024), min(N, 1024)
    nt = T // tt

    def body(a_ref, b_ref, o_ref, acc_s):
        t = pl.program_id(2)

        @pl.when(t == 0)
        def _():
            acc_s[...] = jnp.zeros_like(acc_s)

        acc_s[...] += _dot_tn(a_ref[...].astype(BF16), b_ref[...])

        @pl.when(t == nt - 1)
        def _():
            o_ref[...] = acc_s[...].astype(BF16)

    return pl.pallas_call(
        body, name=name, grid=(K // tk, N // tn, nt),
        in_specs=[pl.BlockSpec((tt, tk), lambda k, n, t: (t, k)), pl.BlockSpec((tt, tn), lambda k, n, t: (t, n))],
        out_specs=pl.BlockSpec((tk, tn), lambda k, n, t: (k, n)),
        out_shape=jax.ShapeDtypeStruct((K, N), BF16),
        scratch_shapes=[pltpu.VMEM((tk, tn), F32)],
        compiler_params=_cp(("parallel", "parallel", "arbitrary")),
    )(a, b)


def _proj_bwd(x, dx1, norm_g, wt, pieces, ddt, ex):
    T = x.shape[0]
    tm = min(T, 1024)
    nk = OFF_DT // D + 1
    starts = [sum(a.shape[1] for a in pieces[:n]) // D for n in range(len(pieces))]
    ranges = [(s, s + a.shape[1] // D) for s, a in zip(starts, pieces)]
    assert ranges[-1][1] == nk - 1
    npc = len(pieces)

    def body(x_ref, dx1_ref, g_ref, w_ref, wdt_ref, *rest):
        piece_refs, ddt_ref, gx_ref, dng_ref, acc_s = rest[:npc], rest[npc], rest[npc + 1], rest[npc + 2], rest[npc + 3]
        i, k = pl.program_id(0), pl.program_id(1)

        @pl.when((i == 0) & (k == 0))
        def _():
            dng_ref[...] = jnp.zeros_like(dng_ref)

        @pl.when(k == 0)
        def _():
            acc_s[...] = jnp.zeros_like(acc_s)

        for ref, (lo, hi) in zip(piece_refs, ranges):
            @pl.when((k >= lo) & (k < hi))
            def _(ref=ref):
                acc_s[...] += _dot(ref[...], w_ref[...])

        @pl.when(k == nk - 1)
        def _():
            dh = acc_s[...] + _dot(ddt_ref[...], wdt_ref[...])
            xf = x_ref[...]
            r = lax.rsqrt(jnp.mean(xf * xf, axis=-1, keepdims=True) + EPS)
            xh = xf * r
            dng_ref[0:1, :] += jnp.sum(dh * xh, axis=0, keepdims=True)
            dxn = dh * g_ref[...]
            gx_ref[...] = dx1_ref[...] + r * (dxn - xh * jnp.mean(dxn * xh, axis=-1, keepdims=True))

    def piece_spec(lo, hi):
        return pl.BlockSpec((tm, D), lambda i, k: (i, jnp.clip(k - lo, 0, hi - lo - 1)))

    row = pl.BlockSpec((tm, D), lambda i, k: (i, 0))
    return _call(
        body, ex, name="proj_bwd", grid=(T // tm, nk),
        in_specs=[row, row, _full((1, D)), pl.BlockSpec((D, D), lambda i, k: (jnp.minimum(k, nk - 2), 0)),
                  pl.BlockSpec((DTW, D), lambda i, k: (OFF_DT // DTW, 0))]
        + [piece_spec(lo, hi) for lo, hi in ranges] + [pl.BlockSpec((tm, DTW), lambda i, k: (i, 0))],
        out_specs=[row, _full((8, D))],
        out_shape=[jax.ShapeDtypeStruct((T, D), F32), jax.ShapeDtypeStruct((8, D), F32)],
        scratch_shapes=[pltpu.VMEM((tm, D), F32)],
        args=(x, dx1, norm_g, wt, wt, *pieces, ddt))


def _elementwise_tile(R, C, limit=1 << 20):
    if R * C * 4 <= limit:
        return R, C
    rows = [t for t in range(16, R, 16) if R % t == 0 and t * C * 4 <= limit]
    if rows:
        return rows[-1], C
    cols = [t for t in range(128, C, 128) if C % t == 0 and R * t * 4 <= limit]
    return R, cols[-1]


def _adam_update(w, m, v, g):
    c1 = 1.0 - ADAM_B1 ** ADAM_STEP
    c2 = 1.0 - ADAM_B2 ** ADAM_STEP
    mm = ADAM_B1 * m + (1.0 - ADAM_B1) * g
    vv = ADAM_B2 * v + (1.0 - ADAM_B2) * (g * g)
    return -ADAM_LR * ((mm / c1) / (jnp.sqrt(vv / c2) + ADAM_EPS) + ADAM_WD * w), mm, vv


def _adamw(w, m, v, parts, name, part_row=0, block_rows=None):
    R, C = w.shape
    tr, tc = _elementwise_tile(R, C) if block_rows is None else (block_rows, C)
    assert R % tr == 0 and part_row % tr == 0
    first = part_row // tr
    n = len(parts)
    wspec = pl.BlockSpec((tr, tc), lambda i, j: (i, j))
    flat = pl.BlockSpec((tr, tc), lambda i, j: (first + i, j))
    slot = lambda k: pl.BlockSpec((None, tr, tc), lambda i, j: (k, first + i, j))
    part_specs = [slot(p[1]) if isinstance(p, tuple) else flat for p in parts]
    part_arrays = [p[0] if isinstance(p, tuple) else p for p in parts]

    def body(*refs):
        w_ref, m_ref, v_ref = refs[:3]
        g_ref, d_ref, nm_ref, nv_ref = refs[3 + n:]
        g = refs[3][...].astype(F32)
        for r in refs[4:3 + n]:
            g = g + r[...].astype(F32)
        g_ref[...] = g
        d_ref[...], nm_ref[...], nv_ref[...] = _adam_update(w_ref[...], m_ref[...], v_ref[...], g)

    return pl.pallas_call(
        body, name=name, grid=(R // tr, C // tc), in_specs=[wspec] * 3 + part_specs, out_specs=[wspec] * 4,
        out_shape=[jax.ShapeDtypeStruct(w.shape, F32)] * 4, compiler_params=_cp(("parallel", "parallel")),
    )(w, m, v, *part_arrays)


_SMALL_WIDE = {"ln_a_g": 0, "ln_a_b": 1, "final_g": 8, "ple_norm_g": 9, "ssm_norm_g": 16, "conv_b": 44}
_WIDE_CONV_W, _WIDE_ROWS = 32, 56
_WIDE_LOSS = 10
_SMALL_NARROW = {"w_s": (0, NG * CH, 128), "b_s": (512, NG, 128), "dt_bias": (520, 1, NH), "a_log": (521, 1, NH),
                 "d_skip": (522, 1, NH)}
_NARROW_ROWS = 528
_SMALL_SHAPES = {"norm_g": (1, D), "ln_a_g": (1, D), "ln_a_b": (1, D), "ple_norm_g": (1, D), "final_g": (1, D),
                 "ssm_norm_g": (1, DI), "conv_b": (1, CD), "w_s": (NG * CH, CH), "b_s": (NG, CH), "dt_bias": (1, NH),
                 "a_log": (1, NH), "d_skip": (1, NH)}


def _adamw_small(w, m, v, wide_all, narrow_all, norm_all):
    names = list(_SMALL_SHAPES)
    n = len(names)

    def body(*refs):
        wr, mr, vr = refs[:n], refs[n:2 * n], refs[2 * n:3 * n]
        wide_ref, narrow_ref, norm_ref = refs[3 * n:3 * n + 3]
        outs = refs[3 * n + 3:]
        gr, dr, nmr, nvr, cw_ref, loss_ref = outs[:n], outs[n:2 * n], outs[2 * n:3 * n], outs[3 * n:4 * n], outs[4 * n], outs[4 * n + 1]

        def total(ref, rows, lanes):
            acc = ref[0, rows, lanes]
            for d in range(1, NDEV):
                acc = acc + ref[d, rows, lanes]
            return acc

        for k, name in enumerate(names):
            if name in _SMALL_WIDE or name == "norm_g":
                for part in range(_SMALL_SHAPES[name][1] // D):
                    pack, r = (norm_ref, 0) if name == "norm_g" else (wide_ref, _SMALL_WIDE[name] + part)
                    cols = slice(part * D, (part + 1) * D)
                    g = total(pack, slice(r, r + 1), slice(None))
                    gr[k][:, cols] = g
                    dr[k][:, cols], nmr[k][:, cols], nvr[k][:, cols] = _adam_update(wr[k][:, cols], mr[k][:, cols], vr[k][:, cols], g)
            else:
                r, rows, lanes = _SMALL_NARROW[name]
                g = total(narrow_ref, slice(r, r + rows), slice(0, lanes))
                gr[k][...] = g
                dr[k][...], nmr[k][...], nvr[k][...] = _adam_update(wr[k][...], mr[k][...], vr[k][...], g)
        cw_ref[...] = total(wide_ref, slice(_WIDE_CONV_W, _WIDE_CONV_W + 12), slice(None))
        loss_ref[...] = total(wide_ref, slice(_WIDE_LOSS, _WIDE_LOSS + 1), slice(None))

    shapes = [jax.ShapeDtypeStruct(_SMALL_SHAPES[k], F32) for k in names]
    specs = [_full(_SMALL_SHAPES[k]) for k in names]
    res = pl.pallas_call(
        body, name="adamw_small", grid=(1,),
        in_specs=specs * 3 + [_full(wide_all.shape), _full(narrow_all.shape), _full(norm_all.shape)],
        out_specs=specs * 4 + [_full((12, D)), _full((1, D))],
        out_shape=shapes * 4 + [jax.ShapeDtypeStruct((12, D), F32), jax.ShapeDtypeStruct((1, D), F32)],
        compiler_params=_cp(("arbitrary",)),
    )(*[w[k] for k in names], *[m[k] for k in names], *[v[k] for k in names], wide_all, narrow_all, norm_all)
    groups = [dict(zip(names, res[q * n:(q + 1) * n])) for q in range(4)]
    return groups[0], groups[1], groups[2], groups[3], res[4 * n], res[4 * n + 1][0, 0]


def _pair_sum(mine, theirs, core, name):
    Q, R, C = theirs.shape
    tr, tc = _elementwise_tile(R, C)

    def body(core_ref, a_ref, b_ref, o_ref):
        o_ref[...] = (a_ref[...].astype(F32) + b_ref[...].astype(F32)).astype(BF16)

    spec = pl.BlockSpec((None, tr, tc), lambda q, i, j, core_ref: (q, i, j))
    own = pl.BlockSpec((None, tr, tc), lambda q, i, j, core_ref: (2 * q + core_ref[0], i, j))
    return pl.pallas_call(
        body, name=name,
        grid_spec=pltpu.PrefetchScalarGridSpec(num_scalar_prefetch=1, grid=(Q, R // tr, C // tc), in_specs=[own, spec], out_specs=spec),
        out_shape=jax.ShapeDtypeStruct((Q, R, C), BF16), compiler_params=_cp(("parallel", "parallel", "parallel")),
    )(core, mine, theirs)


def _dev_index(px, py, pc):
    return 4 * px + 2 * py + pc


def _mesh_position():
    return lax.axis_index("x"), lax.axis_index("y"), lax.axis_index("c")


def _gather_exchange(blocks):
    n = len(blocks)

    def make(ins, outs, sems):
        send_sems, recv_sems, local_sems = sems
        x, y, c = _mesh_position()
        me, sibling = (x, y, c), (x, y, 1 - c)
        chips = [(1 - x, y), (x, 1 - y), (1 - x, 1 - y)]

        def copy(a, k, block, to, src=None):
            dst = outs[a].at[_dev_index(*block)]
            return pltpu.make_async_remote_copy(src_ref=dst if src is None else src, dst_ref=dst, send_sem=send_sems.at[a, k],
                                                recv_sem=recv_sems.at[a, k], device_id=to, device_id_type=MESH)

        mine = [pltpu.make_async_copy(ins[a], outs[a].at[_dev_index(*me)], local_sems.at[a]) for a in range(n)]
        first = []
        for a in range(n):
            first.append(copy(a, 0, me, sibling, src=ins[a]))
            first += [copy(a, 1 + j, me, (*chip, c), src=ins[a]) for j, chip in enumerate(chips)]

        def start():
            for cp in mine + first:
                cp.start()

        def finish():
            passed = []
            for j, chip in enumerate(chips):
                for a in range(n):
                    copy(a, 1 + j, (*chip, c), me).wait_recv()
                    fwd = copy(a, 4 + j, (*chip, c), sibling)
                    fwd.start()
                    passed.append(fwd)
            for a in range(n):
                copy(a, 0, sibling, me).wait_recv()
                for j, chip in enumerate(chips):
                    copy(a, 4 + j, (*chip, 1 - c), me).wait_recv()
            for cp in first + passed:
                cp.wait_send()
            for cp in mine:
                cp.wait()

        return start, finish

    return _Exchange(list(blocks), [jax.ShapeDtypeStruct((NDEV,) + b.shape, b.dtype) for b in blocks],
                     [pltpu.SemaphoreType.DMA((n, 7)), pltpu.SemaphoreType.DMA((n, 7)), pltpu.SemaphoreType.DMA((n,))], make)


def _combine(*exchanges):
    def make(ins, outs, sems):
        pairs = []
        for e in exchanges:
            ni, no, ns = len(e.arrays), len(e.out_shape), len(e.sems)
            pairs.append(e.make(ins[:ni], outs[:no], sems[:ns]))
            ins, outs, sems = ins[ni:], outs[no:], sems[ns:]

        def start():
            for s, _ in pairs:
                s()

        def finish():
            for _, f in pairs:
                f()

        return start, finish

    return _Exchange(sum((list(e.arrays) for e in exchanges), []), sum((list(e.out_shape) for e in exchanges), []),
                     sum((list(e.sems) for e in exchanges), []), make)


def _start_wait_all(copies, local=()):
    def start():
        for cp in list(local) + list(copies):
            cp.start()

    def finish():
        for cp in copies:
            cp.wait()
        for cp in local:
            cp.wait()

    return start, finish


def _sibling_exchange(grads):
    n = len(grads)

    def make(ins, outs, sems):
        send_sems, recv_sems = sems
        x, y, c = _mesh_position()
        return _start_wait_all([pltpu.make_async_remote_copy(
            src_ref=ins[a].at[2 * q + (1 - c)], dst_ref=outs[a].at[q], send_sem=send_sems.at[a, q], recv_sem=recv_sems.at[a, q],
            device_id=(x, y, 1 - c), device_id_type=MESH) for a in range(n) for q in range(4)])

    return _Exchange(list(grads), [jax.ShapeDtypeStruct((4,) + g.shape[1:], g.dtype) for g in grads],
                     [pltpu.SemaphoreType.DMA((n, 4)), pltpu.SemaphoreType.DMA((n, 4))], make)


def _chips_exchange(partials):
    n = len(partials)

    def make(ins, outs, sems):
        send_sems, recv_sems = sems
        x, y, c = _mesh_position()
        chips = [(1 - x, y), (x, 1 - y), (1 - x, 1 - y)]
        return _start_wait_all([pltpu.make_async_remote_copy(
            src_ref=ins[a].at[2 * px + py], dst_ref=outs[a].at[k], send_sem=send_sems.at[a, k], recv_sem=recv_sems.at[a, k],
            device_id=(px, py, c), device_id_type=MESH) for a in range(n) for k, (px, py) in enumerate(chips)])

    return _Exchange(list(partials), [jax.ShapeDtypeStruct((3,) + g.shape[1:], g.dtype) for g in partials],
                     [pltpu.SemaphoreType.DMA((n, 3)), pltpu.SemaphoreType.DMA((n, 3))], make)


def _direct_gather_exchange(smalls):
    n = len(smalls)

    def make(ins, outs, sems):
        send_sems, recv_sems, local_sems = sems
        x, y, c = _mesh_position()
        copies, local = [], []
        for a in range(n):
            slot = outs[a].at[_dev_index(x, y, c)]
            local.append(pltpu.make_async_copy(ins[a], slot, local_sems.at[a]))
            for r in range(1, NDEV):
                peer = (x ^ ((r >> 2) & 1), y ^ ((r >> 1) & 1), c ^ (r & 1))
                copies.append(pltpu.make_async_remote_copy(src_ref=ins[a], dst_ref=slot, send_sem=send_sems.at[a, r - 1],
                                                           recv_sem=recv_sems.at[a, r - 1], device_id=peer, device_id_type=MESH))
        return _start_wait_all(copies, local=local)

    return _Exchange(list(smalls), [jax.ShapeDtypeStruct((NDEV,) + s.shape, s.dtype) for s in smalls],
                     [pltpu.SemaphoreType.DMA((n, 7)), pltpu.SemaphoreType.DMA((n, 7)), pltpu.SemaphoreType.DMA((n,))], make)


_W_IN_ROWS = {"u": (0, 1024), "v": (1024, 2048), "za": (2048, 3072), "zb": (3072, 5120), "xbc": (5120, 8192),
              "dt": (8192, 8224), "ga": (8224, 9248), "gb": (9248, 10272)}
_FWD_ORDER = ("xbc", "u", "zb", "v", "za", "ga", "gb", "dt")
_BWD_ORDER = ("xbc", "zb", "u", "v", "za", "ga", "gb", "dt")


def _w_in_t_rows(wt, order):
    assert order[-1] == "dt"
    z = jnp.zeros((NP - NIN, wt.shape[1]), wt.dtype)
    return jnp.concatenate([wt[slice(*_W_IN_ROWS[n])] for n in order] + [z], axis=0)


_WEIGHTS = ["norm_g", "w_in", "ln_a_g", "ln_a_b", "w_s", "b_s", "conv_w", "conv_b", "dt_bias", "a_log", "d_skip", "ssm_norm_g",
            "w_oa", "w_ob", "w_out", "ple_norm_g", "w_pg", "w_ple", "final_g"]


def _rows_pack(d, dtype):
    return jnp.concatenate([d["w_oa"].reshape(128, D), d["w_ob"].reshape(256, D), d["w_out"].reshape(128, D),
                            d["w_pg"].reshape(128, D), d["w_ple"].reshape(32, D)], axis=0).astype(dtype)


def kernel(x, p, norm_g, w_in, ln_a_g, ln_a_b, w_s, b_s, conv_w, conv_b, dt_bias, a_log, d_skip, ssm_norm_g, w_oa, w_ob, w_out, ple_norm_g, w_pg, w_ple, final_g, loss_target, m_norm_g, m_w_in, m_ln_a_g, m_ln_a_b, m_w_s, m_b_s, m_conv_w, m_conv_b, m_dt_bias, m_a_log, m_d_skip, m_ssm_norm_g, m_w_oa, m_w_ob, m_w_out, m_ple_norm_g, m_w_pg, m_w_ple, m_final_g, v_norm_g, v_w_in, v_ln_a_g, v_ln_a_b, v_w_s, v_b_s, v_conv_w, v_conv_b, v_dt_bias, v_a_log, v_d_skip, v_ssm_norm_g, v_w_oa, v_w_ob, v_w_out, v_ple_norm_g, v_w_pg, v_w_ple, v_final_g):
    args = dict(locals())
    w = {n: args[n] for n in _WEIGHTS}
    m = {n: args["m_" + n] for n in _WEIGHTS}
    v = {n: args["v_" + n] for n in _WEIGHTS}
    T = x.shape[1]
    xi, yi, ci = lax.axis_index("x"), lax.axis_index("y"), lax.axis_index("c")
    me = 4 * xi + 2 * yi + ci

    chip = 2 * xi + yi
    x2, p2, tgt = x.reshape(T, D), p.reshape(T, PLE), loss_target.reshape(T, D)

    norm_g2 = w["norm_g"].reshape(1, D)
    ws = jnp.where(jnp.tril(jnp.ones((CH, CH), bool))[None], w["w_s"].reshape(NG, CH, CH), 0.0).astype(BF16)
    wst = jnp.transpose(ws, (0, 2, 1))
    bst = jnp.broadcast_to(w["b_s"].reshape(NG, CH, 1), (NG, CH, 256))
    ln_g, ln_b = w["ln_a_g"].reshape(1, D), w["ln_a_b"].reshape(1, D)
    cb = w["conv_b"].reshape(1, CD)
    pad32 = lambda a: jnp.pad(a.reshape(1, NH), ((0, 0), (0, DTW - NH)))
    dtb, alog = pad32(w["dt_bias"]), pad32(w["a_log"])
    dskx = jnp.repeat(w["d_skip"].reshape(NH), HD).reshape(1, DI)
    sg = w["ssm_norm_g"].reshape(1, DI)
    ple_g, fin_g = w["ple_norm_g"].reshape(1, D), w["final_g"].reshape(1, D)
    e128 = (jnp.arange(DTW)[:, None] == (jnp.arange(DI)[None, :] // HD)).astype(BF16)
    et128 = e128.T
    gsel = ((jnp.arange(D)[:, None] // 256) == jnp.arange(128)[None, :]).astype(BF16)

    w_in_t = lambda a: jnp.transpose(a.reshape(D, WSH))
    (a_all,) = _run_exchange(_gather_exchange([w_in_t(w["w_in"]).astype(BF16)]), "all_gather_w_in")
    w_in_full_t = a_all.reshape(NIN, D)
    wp = _w_in_t_rows(w_in_full_t, _FWD_ORDER)
    wt_bwd = _w_in_t_rows(w_in_full_t, _BWD_ORDER)
    core = ci.reshape(1).astype(jnp.int32)
    (proj, dtr, h), (r_all, cw_all) = _proj_fwd(x2, norm_g2, wp, _gather_exchange([_rows_pack(w, BF16), w["conv_w"].reshape(4, CD // NDEV)]))
    f_oa = r_all[:, R_OA:R_OB].reshape(D, D)
    f_ob = r_all[:, R_OB:R_OUT].reshape(DI, D)
    f_out = r_all[:, R_OUT:R_PG].reshape(D, D)
    f_pg = r_all[:, R_PG:R_PLE].reshape(D, D)
    f_ple = jnp.transpose(r_all[:, R_PLE:R_ROWS].reshape(NDEV, PLE, 128), (1, 0, 2)).reshape(PLE, D)
    cw = jnp.transpose(cw_all, (1, 0, 2)).reshape(4, CD)

    ya = _gmlp_fwd(proj, ln_g, ln_b, ws, bst)
    y, yb, hprev, pre_all = _ssd_fwd(proj, dtr, cw, cb, dtb, alog, dskx, sg, e128)
    dx1, dx1b, mg, hp, dpre, dpe, doa, dob, dya, dyb, dgab, acc = _merge(x2, ya, yb, proj, p2, tgt, f_oa, f_ob, f_out, f_pg, f_ple,
                                                                         ple_g, fin_g)

    gple = jnp.transpose(_wgrad(p2, dpe, "wgrad_ple").reshape(PLE, NDEV, 128), (1, 0, 2)).reshape(NDEV, 32, D)
    gr = jnp.concatenate([_wgrad(ya, doa, "wgrad_oa").reshape(NDEV, 128, D), _wgrad(yb, dob, "wgrad_ob").reshape(NDEV, 256, D),
                          _wgrad(mg, dx1b, "wgrad_out").reshape(NDEV, 128, D), _wgrad(hp, dpre, "wgrad_pg").reshape(NDEV, 128, D),
                          gple], axis=1)
    (duvz, dws, dbs, dln), (from_sib_r,) = _gmlp_bwd(proj, dya, ln_g, ln_b, ws, wst, bst, gsel, _sibling_exchange([gr]))
    pr = _pair_sum(gr, from_sib_r, core, "pair_sum_rows")
    (dxz, ddt, dcw, dsm, dsg), (rr,) = _ssd_bwd(proj, pre_all, dtr, y, dyb, hprev, cw, dtb, alog, dskx, sg, e128, et128, _chips_exchange([pr]))

    g_xz = _wgrad(dxz, h, "wgrad_xbc_zb")
    g_w_in_t = jnp.concatenate([_wgrad(duvz, h, "wgrad_u_v_za"), g_xz[CD:], g_xz[:CD], _wgrad(ddt, h, "wgrad_dt")[:NH],
                                _wgrad(dgab, h, "wgrad_ga_gb")], axis=0)
    ga = g_w_in_t.reshape(NDEV, WSH, D)
    (from_sib_a,) = _run_exchange(_sibling_exchange([ga]), "reduce_scatter_sibling_w_in")
    pa = _pair_sum(ga, from_sib_a, core, "pair_sum_w_in")
    wide = jnp.concatenate([dln, acc, dsg.reshape(16, D), dcw.reshape(24, D)], axis=0)
    narrow = jnp.concatenate([dws.reshape(NG * CH, CH), jnp.pad(dbs[:, :NG].T, ((0, 8 - NG), (0, 0))), dsm], axis=0)
    (gx, dng), (ra, wide_all, narrow_all) = _proj_bwd(x2, dx1, norm_g2, wt_bwd, [dxz, duvz, dgab], ddt,
                                                      _combine(_chips_exchange([pa]), _direct_gather_exchange([wide, narrow])))
    (norm_all,) = _run_exchange(_direct_gather_exchange([dng]), "all_gather_d_norm_g")
    pa_own = lax.dynamic_index_in_dim(pa, chip, 0, keepdims=False)
    pr_own = lax.dynamic_index_in_dim(pr, chip, 0, keepdims=False)

    out_g, out_d, out_m, out_v = {}, {}, {}, {}
    outs = (out_g, out_d, out_m, out_v)
    res = _adamw(w_in_t(w["w_in"]), w_in_t(m["w_in"]), w_in_t(v["w_in"]), [pa_own, (ra, 0), (ra, 1), (ra, 2)], "adamw_w_in")
    for dst, val in zip(outs, res):
        dst["w_in"] = jnp.transpose(val).reshape(1, D, WSH)
    parts_r = [pr_own, (rr, 0), (rr, 1), (rr, 2)]
    for name, row, rows in (("w_oa", R_OA, 128), ("w_ob", R_OB, 256), ("w_out", R_OUT, 128), ("w_pg", R_PG, 128)):
        res = _adamw(w[name].reshape(rows, D), m[name].reshape(rows, D), v[name].reshape(rows, D), parts_r, "adamw_" + name,
                     part_row=row, block_rows=128)
        for dst, val in zip(outs, res):
            dst[name] = val.reshape(1, rows, D)
    res = _adamw(w["w_ple"].reshape(32, D), m["w_ple"].reshape(32, D), v["w_ple"].reshape(32, D), parts_r, "adamw_w_ple",
                 part_row=R_PLE, block_rows=32)
    for dst, val in zip(outs, res):
        dst["w_ple"] = val.reshape(1, PLE, 128)
    two_d = lambda d: {n: d[n].reshape(_SMALL_SHAPES[n]) for n in _SMALL_SHAPES}
    *res, g_cw_wide, loss = _adamw_small(two_d(w), two_d(m), two_d(v), wide_all, narrow_all, norm_all)
    for dst, val in zip(outs, res):
        dst.update({n: val[n].reshape(w[n].shape) for n in _SMALL_SHAPES})
    g_cw = lax.dynamic_slice_in_dim(g_cw_wide.reshape(4, CD), me * (CD // NDEV), CD // NDEV, axis=1).reshape(12, 128)
    res = _adamw(w["conv_w"].reshape(12, 128), m["conv_w"].reshape(12, 128), v["conv_w"].reshape(12, 128), [g_cw], "adamw_conv_w")
    for dst, val in zip((out_g, out_d, out_m, out_v), res):
        dst["conv_w"] = val.reshape(1, 4, CD // NDEV)

    return (loss, gx.reshape(1, T, D), *[out_g[n] for n in _WEIGHTS], *[out_d[n] for n in _WEIGHTS],
            *[out_m[n] for n in _WEIGHTS], *[out_v[n] for n in _WEIGHTS])
```

```python
import functools
import math
from typing import Callable, NamedTuple

import jax
import jax.numpy as jnp
from jax import lax
from jax.experimental import pallas as pl
from jax.experimental.pallas import tpu as pltpu

F32 = jnp.float32
BF16 = jnp.bfloat16
MESH = pl.DeviceIdType.MESH

D = 1024
DI = 2048
CD = 3072
NH = 32
HD = 64
NST = 128
NG = 4
CH = 128
PLE = 256
NIN = 10272
NDEV = 8
WSH = NIN // NDEV
EPS = 1e-6
OFF_XBC, OFF_U, OFF_ZB, OFF_V, OFF_ZA, OFF_GA, OFF_GB, OFF_DT = 0, 3072, 4096, 6144, 7168, 8192, 9216, 10240
NP = 10368
DTW = 128
R_OA, R_OB, R_OUT, R_PG, R_PLE, R_ROWS = 0, 128, 384, 512, 640, 672

ADAM_LR, ADAM_B1, ADAM_B2, ADAM_EPS, ADAM_WD, ADAM_STEP = 0.001, 0.9, 0.999, 1e-08, 0.01, 10

V7X_VMEM_LIMIT = 56 * 1024 * 1024


def _cp(sem=None):
    return pltpu.CompilerParams(dimension_semantics=sem, vmem_limit_bytes=V7X_VMEM_LIMIT)


def _dot(a, b, prec=None):
    return jnp.dot(a, b, preferred_element_type=F32, precision=prec)


def _dot_nt(a, b, prec=None):
    return lax.dot_general(a, b, (((1,), (1,)), ((), ())), preferred_element_type=F32, precision=prec)


def _dot_tn(a, b, prec=None):
    return lax.dot_general(a, b, (((0,), (0,)), ((), ())), preferred_element_type=F32, precision=prec)


def _sigmoid(x):
    return 1.0 / (1.0 + jnp.exp(-x))


def _gelu_and_grad(x):
    c = math.sqrt(2.0 / math.pi)
    x2 = x * x
    t = jnp.tanh(c * (x + 0.044715 * x * x2))
    g = 0.5 * x * (1.0 + t)
    dg = 0.5 * (1.0 + t) + 0.5 * x * (1.0 - t * t) * c * (1.0 + 3.0 * 0.044715 * x2)
    return g, dg


def _gelu(x):
    c = math.sqrt(2.0 / math.pi)
    return 0.5 * x * (1.0 + jnp.tanh(c * (x + 0.044715 * x * x * x)))


def _softplus(x):
    return jnp.maximum(x, 0.0) + jnp.log(1.0 + jnp.exp(-jnp.abs(x)))


def _full(shape):
    n = len(shape)
    return pl.BlockSpec(shape, lambda *_: (0,) * n)


_ANY = pl.BlockSpec(memory_space=pl.ANY)


class _Exchange(NamedTuple):
    arrays: list
    out_shape: list
    sems: list
    make: Callable


def _call(body, ex, *, name, grid, in_specs, out_specs, out_shape, scratch_shapes, args):
    ki, ko, ks = len(in_specs), len(out_specs), len(scratch_shapes)
    ei, eo = len(ex.arrays), len(ex.out_shape)
    last = [g - 1 for g in grid]

    def full_body(*refs):
        r = list(refs)
        ins, eins, r = r[:ki], r[ki:ki + ei], r[ki + ei:]
        outs, eouts, r = r[:ko], r[ko:ko + eo], r[ko + eo:]
        scr, esems = r[:ks], r[ks:]
        start, finish = ex.make(eins, eouts, esems)
        ids = [pl.program_id(a) for a in range(len(grid))]
        is_first = functools.reduce(lambda p, q: p & q, [i == 0 for i in ids])
        is_last = functools.reduce(lambda p, q: p & q, [i == l for i, l in zip(ids, last)])
        pl.when(is_first)(start)
        body(*ins, *outs, *scr)
        pl.when(is_last)(finish)

    res = pl.pallas_call(
        full_body, name=name, grid=grid, in_specs=list(in_specs) + [_ANY] * ei, out_specs=list(out_specs) + [_ANY] * eo,
        out_shape=list(out_shape) + list(ex.out_shape), scratch_shapes=list(scratch_shapes) + list(ex.sems),
        compiler_params=_cp(("arbitrary",) * len(grid)),
    )(*args, *ex.arrays)
    return res[:ko], res[ko:]


def _run_exchange(ex, name):
    ni, no = len(ex.arrays), len(ex.out_shape)

    def body(*refs):
        start, finish = ex.make(refs[:ni], refs[ni:ni + no], refs[ni + no:])
        start()
        finish()

    return pl.pallas_call(body, name=name, in_specs=[_ANY] * ni, out_specs=[_ANY] * no, out_shape=list(ex.out_shape),
                          scratch_shapes=list(ex.sems))(*ex.arrays)


def _proj_fwd(x, norm_g, wp, ex):
    T = x.shape[0]
    tm, tn = min(T, 1024), 2048
    nj = OFF_DT // tn
    assert OFF_DT % tn == 0 and OFF_DT + DTW == NP

    def body(x_ref, g_ref, w_ref, wdt_ref, proj_ref, dt_ref, h_ref, hs_ref):
        j = pl.program_id(1)

        @pl.when(j == 0)
        def _():
            xf = x_ref[...]
            r = lax.rsqrt(jnp.mean(xf * xf, axis=-1, keepdims=True) + EPS)
            h = (xf * r * g_ref[...]).astype(BF16)
            hs_ref[...] = h
            h_ref[...] = h

        proj_ref[...] = _dot_nt(hs_ref[...], w_ref[...]).astype(BF16)

        @pl.when(j == nj - 1)
        def _():
            dt_ref[...] = _dot_nt(hs_ref[...], wdt_ref[...])

    return _call(
        body, ex, name="proj_fwd", grid=(T // tm, nj),
        in_specs=[pl.BlockSpec((tm, D), lambda i, j: (i, 0)), _full((1, D)), pl.BlockSpec((tn, D), lambda i, j: (j, 0)),
                  pl.BlockSpec((DTW, D), lambda i, j: (OFF_DT // DTW, 0))],
        out_specs=[pl.BlockSpec((tm, tn), lambda i, j: (i, j)), pl.BlockSpec((tm, DTW), lambda i, j: (i, 0)),
                   pl.BlockSpec((tm, D), lambda i, j: (i, 0))],
        out_shape=[jax.ShapeDtypeStruct((T, OFF_DT), BF16), jax.ShapeDtypeStruct((T, DTW), F32), jax.ShapeDtypeStruct((T, D), BF16)],
        scratch_shapes=[pltpu.VMEM((tm, D), BF16)], args=(x, norm_g, wp, wp))


def _gmlp_tile():
    return 256


def _gmlp_fwd(proj, ln_g, ln_b, ws, bst):
    T = proj.shape[0]
    tm = min(T, _gmlp_tile())

    def body(u_ref, v_ref, z_ref, lg_ref, lb_ref, ws_ref, bs_ref, ya_ref, vn_s):
        vg = _gelu(v_ref[...].astype(F32))
        mu = jnp.mean(vg, axis=-1, keepdims=True)
        xc = vg - mu
        rstd = lax.rsqrt(jnp.mean(xc * xc, axis=-1, keepdims=True) + EPS)
        vn_s[...] = (xc * rstd * lg_ref[...] + lb_ref[...]).astype(BF16)
        for c in range(tm // CH):
            rs = slice(c * CH, (c + 1) * CH)
            for g in range(NG):
                cs_ = slice(g * 256, (g + 1) * 256)
                sv = _dot(ws_ref[g], vn_s[rs, cs_]) + bs_ref[g]
                z = z_ref[rs, cs_].astype(F32)
                ya_ref[rs, cs_] = (_gelu(u_ref[rs, cs_].astype(F32)) * sv * (z * _sigmoid(z))).astype(BF16)

    blk = lambda off: pl.BlockSpec((tm, D), lambda i: (i, off // D))
    return pl.pallas_call(
        body, name="gmlp_fwd", grid=(T // tm,),
        in_specs=[blk(OFF_U), blk(OFF_V), blk(OFF_ZA), _full((1, D)), _full((1, D)), _full((NG, CH, CH)), _full((NG, CH, 256))],
        out_specs=pl.BlockSpec((tm, D), lambda i: (i, 0)),
        out_shape=jax.ShapeDtypeStruct((T, D), BF16),
        scratch_shapes=[pltpu.VMEM((tm, D), BF16)],
        compiler_params=_cp(("parallel",)),
    )(proj, proj, proj, ln_g, ln_b, ws, bst)


def _gmlp_bwd(proj, dya, ln_g, ln_b, ws, wst, bst, gsel, ex):
    T = proj.shape[0]
    tm = min(T, _gmlp_tile())

    def body(u_ref, v_ref, z_ref, dy_ref, lg_ref, lb_ref, ws_ref, wst_ref, bs_ref, gsel_ref,
             d_ref, dws_ref, dbs_ref, dln_ref, vn_s, dsv_s, dvn_s):
        du_ref, dv_ref, dz_ref = d_ref.at[:, 0:D], d_ref.at[:, D:2 * D], d_ref.at[:, 2 * D:3 * D]
        @pl.when(pl.program_id(0) == 0)
        def _():
            dws_ref[...] = jnp.zeros_like(dws_ref)
            dbs_ref[...] = jnp.zeros_like(dbs_ref)
            dln_ref[...] = jnp.zeros_like(dln_ref)

        vg, dvg_dv = _gelu_and_grad(v_ref[...].astype(F32))
        mu = jnp.mean(vg, axis=-1, keepdims=True)
        xc = vg - mu
        rstd = lax.rsqrt(jnp.mean(xc * xc, axis=-1, keepdims=True) + EPS)
        vhat = xc * rstd
        vn_s[...] = (vhat * lg_ref[...] + lb_ref[...]).astype(BF16)
        ri = lax.broadcasted_iota(jnp.int32, (CH, CH), 0)
        ci = lax.broadcasted_iota(jnp.int32, (CH, CH), 1)
        tril = (ri >= ci).astype(F32)
        for c in range(tm // CH):
            rs = slice(c * CH, (c + 1) * CH)
            for g in range(NG):
                cs_ = slice(g * 256, (g + 1) * 256)
                vn = vn_s[rs, cs_]
                sv = _dot(ws_ref[g], vn) + bs_ref[g]
                z = z_ref[rs, cs_].astype(F32)
                sz = _sigmoid(z)
                ug, dug_du = _gelu_and_grad(u_ref[rs, cs_].astype(F32))
                dy = dy_ref[rs, cs_].astype(F32)
                t = dy * z * sz
                du_ref[rs, cs_] = (t * sv * dug_du).astype(BF16)
                dz_ref[rs, cs_] = (dy * ug * sv * sz * (1.0 + z * (1.0 - sz))).astype(BF16)
                dsv = (t * ug).astype(BF16)
                dsv_s[rs, cs_] = dsv
                dvn_s[rs, cs_] = _dot(wst_ref[g], dsv)
                dws_ref[g] += _dot_nt(dsv, vn) * tril
            dbs_ref[...] += _dot(dsv_s[rs, :], gsel_ref[...])
        dvn = dvn_s[...]
        dln_ref[0:1, :] += jnp.sum(dvn * vhat, axis=0, keepdims=True)
        dln_ref[1:2, :] += jnp.sum(dvn, axis=0, keepdims=True)
        dvh = dvn * lg_ref[...]
        dvg = rstd * (dvh - jnp.mean(dvh, axis=-1, keepdims=True) - vhat * jnp.mean(dvh * vhat, axis=-1, keepdims=True))
        dv_ref[...] = (dvg * dvg_dv).astype(BF16)

    blk = lambda off: pl.BlockSpec((tm, D), lambda i: (i, off // D))
    row = pl.BlockSpec((tm, D), lambda i: (i, 0))
    return _call(
        body, ex, name="gmlp_bwd", grid=(T // tm,),
        in_specs=[blk(OFF_U), blk(OFF_V), blk(OFF_ZA), row, _full((1, D)), _full((1, D)), _full((NG, CH, CH)),
                  _full((NG, CH, CH)), _full((NG, CH, 256)), _full((D, 128))],
        out_specs=[pl.BlockSpec((tm, 3 * D), lambda i: (i, 0)), _full((NG, CH, CH)), _full((CH, 128)), _full((8, D))],
        out_shape=[jax.ShapeDtypeStruct((T, 3 * D), BF16),
                   jax.ShapeDtypeStruct((NG, CH, CH), F32), jax.ShapeDtypeStruct((CH, 128), F32), jax.ShapeDtypeStruct((8, D), F32)],
        scratch_shapes=[pltpu.VMEM((tm, D), BF16), pltpu.VMEM((tm, D), BF16), pltpu.VMEM((tm, D), F32)],
        args=(proj, proj, proj, dya, ln_g, ln_b, ws, wst, bst, gsel))


def _shift_rows(cur, edge, j, down):
    r8 = lax.broadcasted_iota(jnp.int32, (8, 1), 0)
    if down:
        body = pltpu.roll(cur, j, 0)
        return jnp.concatenate([jnp.where(r8 >= j, body[0:8], pltpu.roll(edge, j, 0)), body[8:]], axis=0)
    body = pltpu.roll(cur, CH - j, 0)
    return jnp.concatenate([body[:CH - 8], jnp.where(r8 < 8 - j, body[CH - 8:], pltpu.roll(edge, 8 - j, 0))], axis=0)


def _conv_pre(x, x_before, cw_ref, cb_ref):
    pre = cb_ref[...] + cw_ref[3:4, :] * x
    for j in (1, 2, 3):
        pre = pre + cw_ref[3 - j:4 - j, :] * _shift_rows(x, x_before, j, down=True)
    return pre


def _split_dot(x, w, parts, w_left=False):
    acc, r = None, x
    for k in range(parts):
        hi = r.astype(BF16)
        d = _dot(w, hi) if w_left else _dot(hi, w)
        acc = d if acc is None else acc + d
        if k + 1 < parts:
            r = r - hi.astype(F32)
    return acc


def _chunk_decays(dt, alog_ref, e_ref, cs_s, cst_s, csx_s):
    a = -jnp.exp(alog_ref[...])
    ri = lax.broadcasted_iota(jnp.int32, (CH, CH), 0)
    ci = lax.broadcasted_iota(jnp.int32, (CH, CH), 1)
    tril = ri >= ci
    cs = _split_dot(dt * a, tril.astype(BF16), 3, w_left=True)
    cs_s[...] = cs
    cst_s[...] = cs.T
    csx_s[...] = _split_dot(cs, e_ref[...], 3)
    return a, tril, ri, ci


def _lmat(cst_s, h, tril):
    rowb = jnp.broadcast_to(cst_s[h:h + 1, :], (CH, CH))
    return jnp.exp(jnp.where(tril, rowb.T - rowb, -jnp.inf))


def _head_pair_rows(v, lane):
    return jnp.concatenate([jnp.where(lane < HD, v, 0.0), jnp.where(lane < HD, 0.0, v)], axis=0).astype(BF16)


def _ssd_fwd(proj, dtr, cw, cb, dtb, alog, dskx, sg, e128):
    T = proj.shape[0]
    nc = T // CH

    def body(xbc_ref, zb_ref, dt_ref, cw_ref, cb_ref, dtb_ref, alog_ref, dx_ref, sg_ref, e_ref,
             y_ref, yb_ref, hp_ref, pre_ref, xprev_s, h_s, cs_s, cst_s, csx_s, yz_s):
        @pl.when(pl.program_id(0) == 0)
        def _():
            xprev_s[...] = jnp.zeros_like(xprev_s)
            h_s[...] = jnp.zeros_like(h_s)

        x = xbc_ref[...].astype(F32)
        pre = _conv_pre(x, xprev_s[...], cw_ref, cb_ref)
        pre_ref[...] = pre
        xprev_s[...] = x[CH - 8:]
        xc = pre * _sigmoid(pre)
        dt = _softplus(dt_ref[...] + dtb_ref[...])
        a, tril, _, lane = _chunk_decays(dt, alog_ref, e_ref, cs_s, cst_s, csx_s)
        dt_x = _split_dot(dt, e_ref[...], 2)
        cs_last_x = csx_s[CH - 1:CH, :]
        hp_ref[0] = h_s[...]
        for g in range(NG):
            gs = slice(g * 512, (g + 1) * 512)
            bg = xc[:, DI + g * NST:DI + (g + 1) * NST].astype(BF16)
            cg = xc[:, DI + 512 + g * NST:DI + 512 + (g + 1) * NST].astype(BF16)
            cbm = _dot_nt(cg, bg)
            xg = xc[:, gs]
            xdt = xg * dt_x[:, gs]
            hprev = h_s[:, gs]
            csx = csx_s[:, gs]
            yoff = _dot(cg, hprev.astype(BF16)) * jnp.exp(csx)
            st = _dot_tn(bg, (xdt * jnp.exp(cs_last_x[:, gs] - csx)).astype(BF16))
            h_s[:, gs] = jnp.exp(cs_last_x[:, gs]) * hprev + st
            ssq = jnp.zeros((CH, 1), F32)
            for q in range(4):
                h0 = g * 8 + 2 * q
                ps = slice(q * 128, (q + 1) * 128)
                cols = slice(g * 512 + q * 128, g * 512 + (q + 1) * 128)
                m01 = jnp.concatenate([cbm * _lmat(cst_s, h0, tril), cbm * _lmat(cst_s, h0 + 1, tril)], axis=1).astype(BF16)
                yq = _dot(m01, _head_pair_rows(xdt[:, ps], lane)) + yoff[:, ps] + xg[:, ps] * dx_ref[:, cols]
                y_ref[:, cols] = yq
                z = zb_ref[:, cols].astype(F32)
                yz = yq * z * _sigmoid(z)
                yz_s[:, cols] = yz
                ssq = ssq + jnp.sum(yz * yz, axis=1, keepdims=True)
            rg = lax.rsqrt(ssq * (1.0 / 512.0) + EPS)
            yb_ref[:, gs] = (yz_s[:, gs] * rg * sg_ref[:, gs]).astype(BF16)

    return pl.pallas_call(
        body, name="ssd_fwd", grid=(nc,),
        in_specs=[pl.BlockSpec((CH, CD), lambda c: (c, OFF_XBC // CD)), pl.BlockSpec((CH, DI), lambda c: (c, OFF_ZB // DI)),
                  pl.BlockSpec((CH, DTW), lambda c: (c, 0)), _full((4, CD)), _full((1, CD)), _full((1, DTW)),
                  _full((1, DTW)), _full((1, DI)), _full((1, DI)), _full((DTW, DI))],
        out_specs=[pl.BlockSpec((CH, DI), lambda c: (c, 0)), pl.BlockSpec((CH, DI), lambda c: (c, 0)),
                   pl.BlockSpec((1, NST, DI), lambda c: (c, 0, 0)), pl.BlockSpec((CH, CD), lambda c: (c, 0))],
        out_shape=[jax.ShapeDtypeStruct((T, DI), F32), jax.ShapeDtypeStruct((T, DI), BF16),
                   jax.ShapeDtypeStruct((nc, NST, DI), F32), jax.ShapeDtypeStruct((T, CD), F32)],
        scratch_shapes=[pltpu.VMEM((8, CD), F32), pltpu.VMEM((NST, DI), F32), pltpu.VMEM((CH, CH), F32),
                        pltpu.VMEM((CH, CH), F32), pltpu.VMEM((CH, DI), F32), pltpu.VMEM((CH, DI), F32)],
        compiler_params=_cp(("arbitrary",)),
    )(proj, proj, dtr, cw, cb, dtb, alog, dskx, sg, e128)


def _ssd_bwd(proj, pre_all, dtr, y, dyb, hprev_all, cw, dtb, alog, dskx, sg, e128, et128, ex):
    T = proj.shape[0]
    nc = T // CH

    def body(xbc_ref, pre_ref, zb_ref, dt_ref, y_ref, dyb_ref, hp_ref, cw_ref, dtb_ref, alog_ref, dx_ref, sg_ref,
             e_ref, et_ref, d_ref, ddt_ref, dcw_ref, dsm_ref, dsg_ref,
             g_s, dpn_s, cs_s, cst_s, csx_s, dy_s, dxdt_s, dxs_s, dsd_s, dxc_s, gh_s):
        dxbc_ref, dzb_ref = d_ref.at[:, 0:CD], d_ref.at[:, CD:CD + DI]
        i = pl.program_id(0)

        @pl.when(i == 0)
        def _():
            g_s[...] = jnp.zeros_like(g_s)
            dpn_s[...] = jnp.zeros_like(dpn_s)
            dcw_ref[...] = jnp.zeros_like(dcw_ref)
            dsm_ref[...] = jnp.zeros_like(dsm_ref)
            dsg_ref[...] = jnp.zeros_like(dsg_ref)

        pre = pre_ref[...]
        sp = _sigmoid(pre)
        xc = pre * sp
        dtr = dt_ref[...] + dtb_ref[...]
        dt = _softplus(dtr)
        a, tril, ri, lane = _chunk_decays(dt, alog_ref, e_ref, cs_s, cst_s, csx_s)
        et = et_ref[...]
        dt_x = _split_dot(dt, e_ref[...], 2)
        cs_last_x = csx_s[CH - 1:CH, :]

        for g in range(NG):
            gs = slice(g * 512, (g + 1) * 512)
            z = zb_ref[:, gs].astype(F32)
            sz = _sigmoid(z)
            yv = y_ref[:, gs]
            yz = yv * z * sz
            rg = lax.rsqrt(jnp.mean(yz * yz, axis=-1, keepdims=True) + EPS)
            yn = yz * rg
            dyb = dyb_ref[:, gs].astype(F32)
            dsg_ref[0:1, gs] += jnp.sum(dyb * yn, axis=0, keepdims=True)
            dyn = dyb * sg_ref[:, gs]
            dyz = rg * (dyn - yn * jnp.mean(dyn * yn, axis=-1, keepdims=True))
            dy_s[:, gs] = dyz * z * sz
            dzb_ref[:, gs] = (dyz * yv * sz * (1.0 + z * (1.0 - sz))).astype(BF16)

        rsum = jnp.zeros((CH, DTW), F32)
        csum_t = jnp.zeros((DTW, CH), F32)
        for g in range(NG):
            gs = slice(g * 512, (g + 1) * 512)
            bg = xc[:, DI + g * NST:DI + (g + 1) * NST].astype(BF16)
            cg = xc[:, DI + 512 + g * NST:DI + 512 + (g + 1) * NST].astype(BF16)
            cbm = _dot_nt(cg, bg)
            xdt = xc[:, gs] * dt_x[:, gs]
            hprev = hp_ref[0, :, gs]
            hpb = hprev.astype(BF16)
            gn = g_s[:, gs]
            gnb = gn.astype(BF16)
            dy = dy_s[:, gs]
            csx = csx_s[:, gs]
            ecs = jnp.exp(csx)
            dec = jnp.exp(cs_last_x[:, gs] - csx)
            dye = (dy * ecs).astype(BF16)
            dc = _dot_nt(dye, hpb)
            dprev = _dot_tn(cg, dye)
            dxdt_state = dec * _dot(bg, gnb)
            db = _dot_nt((xdt * dec).astype(BF16), gnb)
            dcb = jnp.zeros((CH, CH), F32)
            for q in range(4):
                h0 = g * 8 + 2 * q
                ps = slice(q * 128, (q + 1) * 128)
                dyp = dy[:, ps]
                l0 = _lmat(cst_s, h0, tril)
                l1 = _lmat(cst_s, h0 + 1, tril)
                m0 = cbm * l0
                m1 = cbm * l1
                dm = _dot_nt(dyp.astype(BF16), _head_pair_rows(xdt[:, ps], lane))
                dm0 = dm[:, :CH]
                dm1 = dm[:, CH:]
                dcb = dcb + dm0 * l0 + dm1 * l1
                for hh, qm in ((h0, dm0 * m0), (h0 + 1, dm1 * m1)):
                    rsum = jnp.where(lane == hh, jnp.sum(qm, axis=1, keepdims=True), rsum)
                    csum_t = jnp.where(ri == hh, jnp.sum(qm, axis=0, keepdims=True), csum_t)
                mst = jnp.concatenate([m0, m1], axis=0).astype(BF16)
                d = _dot_tn(mst, _head_pair_rows(dyp, lane))
                dxdt_s[:, g * 512 + q * 128:g * 512 + (q + 1) * 128] = d + dxdt_state[:, ps]
            yoff = _dot(cg, hpb) * ecs
            dsd_s[:, gs] = xdt * dxdt_state
            dxs_s[:, gs] = dy * yoff
            dcbb = dcb.astype(BF16)
            dxc_s[:, DI + 512 + g * NST:DI + 512 + (g + 1) * NST] = dc + _dot(dcbb, bg)
            dxc_s[:, DI + g * NST:DI + (g + 1) * NST] = db + _dot_tn(dcbb, cg)
            gh_s[:, gs] = jnp.broadcast_to(jnp.sum(gn * hprev, axis=0, keepdims=True), (8, 512))
            g_s[:, gs] = dprev + jnp.exp(cs_last_x[:, gs]) * gn

        xs = xc[:, :DI]
        dy = dy_s[...]
        dxdt = dxdt_s[...]
        cs_last = cs_s[CH - 1:CH, :]
        state_e = _split_dot(dsd_s[...], et, 2)
        dcd = 0.125 * jnp.sum(_split_dot(gh_s[...], et, 2), axis=0, keepdims=True) * jnp.exp(cs_last)
        row = lax.broadcasted_iota(jnp.int32, (CH, 1), 0)
        dcs = rsum - csum_t.T + _split_dot(dxs_s[...], et, 2) - state_e
        dcs = dcs + jnp.where(row == CH - 1, jnp.sum(state_e, axis=0, keepdims=True) + dcd, 0.0)
        dda = _split_dot(dcs, (lane >= ri).astype(BF16), 3, w_left=True)
        ddt = dda * a + _dot((dxdt * xs).astype(BF16), et)
        ddtr = jnp.where(lane < NH, ddt * _sigmoid(dtr), 0.0)
        ddt_ref[...] = ddtr.astype(BF16)
        dsm_ref[0:1, :] += jnp.sum(ddtr, axis=0, keepdims=True)
        dsm_ref[1:2, :] += jnp.sum(dda * dt, axis=0, keepdims=True) * a
        dsm_ref[2:3, :] += jnp.sum(_dot((dy * xs).astype(BF16), et), axis=0, keepdims=True)
        dxc_s[:, :DI] = dxdt * dt_x + dy * dx_ref[...]

        dpre = dxc_s[...] * sp * (1.0 + pre * (1.0 - sp))
        dpn = dpn_s[...]
        x = xbc_ref[...].astype(F32)
        dcw_ref[4:5, :] += jnp.sum(dpre, axis=0, keepdims=True)
        dxbc = cw_ref[3:4, :] * dpre
        dcw_ref[3:4, :] += jnp.sum(dpre * x, axis=0, keepdims=True)
        for j in (1, 2, 3):
            ahead = _shift_rows(dpre, dpn, j, down=False)
            dcw_ref[3 - j:4 - j, :] += jnp.sum(ahead * x, axis=0, keepdims=True)
            dxbc = dxbc + cw_ref[3 - j:4 - j, :] * ahead
        dpn_s[...] = dpre[0:8]
        dxbc_ref[...] = dxbc.astype(BF16)

    rev = lambda c: nc - 1 - c
    return _call(
        body, ex, name="ssd_bwd", grid=(nc,),
        in_specs=[pl.BlockSpec((CH, CD), lambda c: (rev(c), OFF_XBC // CD)),
                  pl.BlockSpec((CH, CD), lambda c: (rev(c), 0)),
                  pl.BlockSpec((CH, DI), lambda c: (rev(c), OFF_ZB // DI)),
                  pl.BlockSpec((CH, DTW), lambda c: (rev(c), 0)),
                  pl.BlockSpec((CH, DI), lambda c: (rev(c), 0)), pl.BlockSpec((CH, DI), lambda c: (rev(c), 0)),
                  pl.BlockSpec((1, NST, DI), lambda c: (rev(c), 0, 0)),
                  _full((4, CD)), _full((1, DTW)), _full((1, DTW)), _full((1, DI)), _full((1, DI)),
                  _full((DTW, DI)), _full((DI, DTW))],
        out_specs=[pl.BlockSpec((CH, CD + DI), lambda c: (rev(c), 0)),
                   pl.BlockSpec((CH, DTW), lambda c: (rev(c), 0)), _full((8, CD)), _full((8, DTW)), _full((8, DI))],
        out_shape=[jax.ShapeDtypeStruct((T, CD + DI), BF16), jax.ShapeDtypeStruct((T, DTW), BF16),
                   jax.ShapeDtypeStruct((8, CD), F32), jax.ShapeDtypeStruct((8, DTW), F32), jax.ShapeDtypeStruct((8, DI), F32)],
        scratch_shapes=[pltpu.VMEM((NST, DI), F32), pltpu.VMEM((8, CD), F32), pltpu.VMEM((CH, CH), F32), pltpu.VMEM((CH, CH), F32),
                        pltpu.VMEM((CH, DI), F32), pltpu.VMEM((CH, DI), F32), pltpu.VMEM((CH, DI), F32), pltpu.VMEM((CH, DI), F32),
                        pltpu.VMEM((CH, DI), F32), pltpu.VMEM((CH, CD), F32), pltpu.VMEM((8, DI), F32)],
        args=(proj, pre_all, proj, dtr, y, dyb, hprev_all, cw, dtb, alog, dskx, sg, e128, et128))


def _merge_tile():
    return 256


def _merge(x, ya, yb, proj, p, tgt, w_oa, w_ob, w_out, w_pg, w_ple, ple_g, fin_g):
    T = x.shape[0]
    tm = min(T, _merge_tile())

    def body(x_ref, ya_ref, yb_ref, ga_ref, gb_ref, p_ref, t_ref, woa, wob, wout, wpg, wple, pg_ref, fg_ref,
             dx1_ref, dx1b_ref, mg_ref, hp_ref, dpre_ref, dpe_ref, doa_ref, dob_ref, dya_ref, dyb_ref, dg_ref, acc_ref):
        @pl.when(pl.program_id(0) == 0)
        def _():
            acc_ref[...] = jnp.zeros_like(acc_ref)

        oa = _dot(ya_ref[...], woa[...])
        ob = _dot(yb_ref[...], wob[...])
        sa = _sigmoid(ga_ref[...].astype(F32))
        sb = _sigmoid(gb_ref[...].astype(F32))
        mg = sa * oa + sb * ob
        mgb = mg.astype(BF16)
        mg_ref[...] = mgb
        x1 = x_ref[...] + _dot(mgb, wout[...])
        r2 = lax.rsqrt(jnp.mean(x1 * x1, axis=-1, keepdims=True) + EPS)
        xh1 = x1 * r2
        hpb = (xh1 * pg_ref[...]).astype(BF16)
        hp_ref[...] = hpb
        gate = _sigmoid(_dot(hpb, wpg[...]))
        pe = _dot(p_ref[...].astype(BF16), wple[...])
        x2 = x1 + gate * pe
        r3 = lax.rsqrt(jnp.mean(x2 * x2, axis=-1, keepdims=True) + EPS)
        xh2 = x2 * r3
        err = xh2 * fg_ref[...] - t_ref[...]
        acc_ref[2:3, :] += 0.5 * jnp.sum(jnp.mean(err * err, axis=-1, keepdims=True))
        dyo = err * (1.0 / D)
        acc_ref[0:1, :] += jnp.sum(dyo * xh2, axis=0, keepdims=True)
        dn = dyo * fg_ref[...]
        dx2 = r3 * (dn - xh2 * jnp.mean(dn * xh2, axis=-1, keepdims=True))
        dpe_ref[...] = (dx2 * gate).astype(BF16)
        dpre = (dx2 * pe * gate * (1.0 - gate)).astype(BF16)
        dpre_ref[...] = dpre
        dhp = _dot_nt(dpre, wpg[...])
        acc_ref[1:2, :] += jnp.sum(dhp * xh1, axis=0, keepdims=True)
        dhn = dhp * pg_ref[...]
        dx1 = dx2 + r2 * (dhn - xh1 * jnp.mean(dhn * xh1, axis=-1, keepdims=True))
        dx1_ref[...] = dx1
        dx1b = dx1.astype(BF16)
        dx1b_ref[...] = dx1b
        dmg = _dot_nt(dx1b, wout[...])
        doa = (dmg * sa).astype(BF16)
        dob = (dmg * sb).astype(BF16)
        doa_ref[...] = doa
        dob_ref[...] = dob
        dg_ref[:, :D] = (dmg * oa * sa * (1.0 - sa)).astype(BF16)
        dg_ref[:, D:] = (dmg * ob * sb * (1.0 - sb)).astype(BF16)
        dya_ref[...] = _dot_nt(doa, woa[...]).astype(BF16)
        dyb_ref[...] = _dot_nt(dob, wob[...]).astype(BF16)

    row = lambda w: pl.BlockSpec((tm, w), lambda i: (i, 0))
    wsp = lambda s: pl.BlockSpec(s, lambda i: (0, 0), pipeline_mode=pl.Buffered(1))
    return pl.pallas_call(
        body, name="merge", grid=(T // tm,),
        in_specs=[row(D), row(D), row(DI), pl.BlockSpec((tm, D), lambda i: (i, OFF_GA // D)),
                  pl.BlockSpec((tm, D), lambda i: (i, OFF_GB // D)), row(PLE), row(D),
                  wsp((D, D)), wsp((DI, D)), wsp((D, D)), wsp((D, D)), wsp((PLE, D)), _full((1, D)), _full((1, D))],
        out_specs=[row(D)] * 9 + [row(DI), row(2 * D), _full((8, D))],
        out_shape=[jax.ShapeDtypeStruct((T, D), F32)] + [jax.ShapeDtypeStruct((T, D), BF16)] * 8
        + [jax.ShapeDtypeStruct((T, DI), BF16), jax.ShapeDtypeStruct((T, 2 * D), BF16), jax.ShapeDtypeStruct((8, D), F32)],
        compiler_params=_cp(("arbitrary",)),
    )(x, ya, yb, proj, proj, p, tgt, w_oa, w_ob, w_out, w_pg, w_ple, ple_g, fin_g)


def _wgrad(a, b, name):
    T, K = a.shape
    N = b.shape[1]
    tt, tk, tn = min(T, 2048), min(K, 1024), min(N, 1024)
    nt = T // tt

    def body(a_ref, b_ref, o_ref, acc_s):
        t = pl.program_id(2)

        @pl.when(t == 0)
        def _():
            acc_s[...] = jnp.zeros_like(acc_s)

        acc_s[...] += _dot_tn(a_ref[...].astype(BF16), b_ref[...])

        @pl.when(t == nt - 1)
        def _():
            o_ref[...] = acc_s[...].astype(BF16)

    return pl.pallas_call(
        body, name=name, grid=(K // tk, N // tn, nt),
        in_specs=[pl.BlockSpec((tt, tk), lambda k, n, t: (t, k)), pl.BlockSpec((tt, tn), lambda k, n, t: (t, n))],
        out_specs=pl.BlockSpec((tk, tn), lambda k, n, t: (k, n)),
        out_shape=jax.ShapeDtypeStruct((K, N), BF16),
        scratch_shapes=[pltpu.VMEM((tk, tn), F32)],
        compiler_params=_cp(("parallel", "parallel", "arbitrary")),
    )(a, b)


def _proj_bwd(x, dx1, norm_g, wt, pieces, ddt, ex):
    T = x.shape[0]
    tm = min(T, 1024)
    nk = OFF_DT // D + 1
    starts = [sum(a.shape[1] for a in pieces[:n]) // D for n in range(len(pieces))]
    ranges = [(s, s + a.shape[1] // D) for s, a in zip(starts, pieces)]
    assert ranges[-1][1] == nk - 1
    npc = len(pieces)

    def body(x_ref, dx1_ref, g_ref, w_ref, wdt_ref, *rest):
        piece_refs, ddt_ref, gx_ref, dng_ref, acc_s = rest[:npc], rest[npc], rest[npc + 1], rest[npc + 2], rest[npc + 3]
        i, k = pl.program_id(0), pl.program_id(1)

        @pl.when((i == 0) & (k == 0))
        def _():
            dng_ref[...] = jnp.zeros_like(dng_ref)

        @pl.when(k == 0)
        def _():
            acc_s[...] = jnp.zeros_like(acc_s)

        for ref, (lo, hi) in zip(piece_refs, ranges):
            @pl.when((k >= lo) & (k < hi))
            def _(ref=ref):
                acc_s[...] += _dot(ref[...], w_ref[...])

        @pl.when(k == nk - 1)
        def _():
            dh = acc_s[...] + _dot(ddt_ref[...], wdt_ref[...])
            xf = x_ref[...]
            r = lax.rsqrt(jnp.mean(xf * xf, axis=-1, keepdims=True) + EPS)
            xh = xf * r
            dng_ref[0:1, :] += jnp.sum(dh * xh, axis=0, keepdims=True)
            dxn = dh * g_ref[...]
            gx_ref[...] = dx1_ref[...] + r * (dxn - xh * jnp.mean(dxn * xh, axis=-1, keepdims=True))

    def piece_spec(lo, hi):
        return pl.BlockSpec((tm, D), lambda i, k: (i, jnp.clip(k - lo, 0, hi - lo - 1)))

    row = pl.BlockSpec((tm, D), lambda i, k: (i, 0))
    return _call(
        body, ex, name="proj_bwd", grid=(T // tm, nk),
        in_specs=[row, row, _full((1, D)), pl.BlockSpec((D, D), lambda i, k: (jnp.minimum(k, nk - 2), 0)),
                  pl.BlockSpec((DTW, D), lambda i, k: (OFF_DT // DTW, 0))]
        + [piece_spec(lo, hi) for lo, hi in ranges] + [pl.BlockSpec((tm, DTW), lambda i, k: (i, 0))],
        out_specs=[row, _full((8, D))],
        out_shape=[jax.ShapeDtypeStruct((T, D), F32), jax.ShapeDtypeStruct((8, D), F32)],
        scratch_shapes=[pltpu.VMEM((tm, D), F32)],
        args=(x, dx1, norm_g, wt, wt, *pieces, ddt))


def _elementwise_tile(R, C, limit=1 << 20):
    if R * C * 4 <= limit:
        return R, C
    rows = [t for t in range(16, R, 16) if R % t == 0 and t * C * 4 <= limit]
    if rows:
        return rows[-1], C
    cols = [t for t in range(128, C, 128) if C % t == 0 and R * t * 4 <= limit]
    return R, cols[-1]


def _adam_update(w, m, v, g):
    c1 = 1.0 - ADAM_B1 ** ADAM_STEP
    c2 = 1.0 - ADAM_B2 ** ADAM_STEP
    mm = ADAM_B1 * m + (1.0 - ADAM_B1) * g
    vv = ADAM_B2 * v + (1.0 - ADAM_B2) * (g * g)
    return -ADAM_LR * ((mm / c1) / (jnp.sqrt(vv / c2) + ADAM_EPS) + ADAM_WD * w), mm, vv


def _adamw(w, m, v, parts, name, part_row=0, block_rows=None):
    R, C = w.shape
    tr, tc = _elementwise_tile(R, C) if block_rows is None else (block_rows, C)
    assert R % tr == 0 and part_row % tr == 0
    first = part_row // tr
    n = len(parts)
    wspec = pl.BlockSpec((tr, tc), lambda i, j: (i, j))
    flat = pl.BlockSpec((tr, tc), lambda i, j: (first + i, j))
    slot = lambda k: pl.BlockSpec((None, tr, tc), lambda i, j: (k, first + i, j))
    part_specs = [slot(p[1]) if isinstance(p, tuple) else flat for p in parts]
    part_arrays = [p[0] if isinstance(p, tuple) else p for p in parts]

    def body(*refs):
        w_ref, m_ref, v_ref = refs[:3]
        g_ref, d_ref, nm_ref, nv_ref = refs[3 + n:]
        g = refs[3][...].astype(F32)
        for r in refs[4:3 + n]:
            g = g + r[...].astype(F32)
        g_ref[...] = g
        d_ref[...], nm_ref[...], nv_ref[...] = _adam_update(w_ref[...], m_ref[...], v_ref[...], g)

    return pl.pallas_call(
        body, name=name, grid=(R // tr, C // tc), in_specs=[wspec] * 3 + part_specs, out_specs=[wspec] * 4,
        out_shape=[jax.ShapeDtypeStruct(w.shape, F32)] * 4, compiler_params=_cp(("parallel", "parallel")),
    )(w, m, v, *part_arrays)


_SMALL_WIDE = {"ln_a_g": 0, "ln_a_b": 1, "final_g": 8, "ple_norm_g": 9, "ssm_norm_g": 16, "conv_b": 44}
_WIDE_CONV_W, _WIDE_ROWS = 32, 56
_WIDE_LOSS = 10
_SMALL_NARROW = {"w_s": (0, NG * CH, 128), "b_s": (512, NG, 128), "dt_bias": (520, 1, NH), "a_log": (521, 1, NH),
                 "d_skip": (522, 1, NH)}
_NARROW_ROWS = 528
_SMALL_SHAPES = {"norm_g": (1, D), "ln_a_g": (1, D), "ln_a_b": (1, D), "ple_norm_g": (1, D), "final_g": (1, D),
                 "ssm_norm_g": (1, DI), "conv_b": (1, CD), "w_s": (NG * CH, CH), "b_s": (NG, CH), "dt_bias": (1, NH),
                 "a_log": (1, NH), "d_skip": (1, NH)}


def _adamw_small(w, m, v, wide_all, narrow_all, norm_all):
    names = list(_SMALL_SHAPES)
    n = len(names)

    def body(*refs):
        wr, mr, vr = refs[:n], refs[n:2 * n], refs[2 * n:3 * n]
        wide_ref, narrow_ref, norm_ref = refs[3 * n:3 * n + 3]
        outs = refs[3 * n + 3:]
        gr, dr, nmr, nvr, cw_ref, loss_ref = outs[:n], outs[n:2 * n], outs[2 * n:3 * n], outs[3 * n:4 * n], outs[4 * n], outs[4 * n + 1]

        def total(ref, rows, lanes):
            acc = ref[0, rows, lanes]
            for d in range(1, NDEV):
                acc = acc + ref[d, rows, lanes]
            return acc

        for k, name in enumerate(names):
            if name in _SMALL_WIDE or name == "norm_g":
                for part in range(_SMALL_SHAPES[name][1] // D):
                    pack, r = (norm_ref, 0) if name == "norm_g" else (wide_ref, _SMALL_WIDE[name] + part)
                    cols = slice(part * D, (part + 1) * D)
                    g = total(pack, slice(r, r + 1), slice(None))
                    gr[k][:, cols] = g
                    dr[k][:, cols], nmr[k][:, cols], nvr[k][:, cols] = _adam_update(wr[k][:, cols], mr[k][:, cols], vr[k][:, cols], g)
            else:
                r, rows, lanes = _SMALL_NARROW[name]
                g = total(narrow_ref, slice(r, r + rows), slice(0, lanes))
                gr[k][...] = g
                dr[k][...], nmr[k][...], nvr[k][...] = _adam_update(wr[k][...], mr[k][...], vr[k][...], g)
        cw_ref[...] = total(wide_ref, slice(_WIDE_CONV_W, _WIDE_CONV_W + 12), slice(None))
        loss_ref[...] = total(wide_ref, slice(_WIDE_LOSS, _WIDE_LOSS + 1), slice(None))

    shapes = [jax.ShapeDtypeStruct(_SMALL_SHAPES[k], F32) for k in names]
    specs = [_full(_SMALL_SHAPES[k]) for k in names]
    res = pl.pallas_call(
        body, name="adamw_small", grid=(1,),
        in_specs=specs * 3 + [_full(wide_all.shape), _full(narrow_all.shape), _full(norm_all.shape)],
        out_specs=specs * 4 + [_full((12, D)), _full((1, D))],
        out_shape=shapes * 4 + [jax.ShapeDtypeStruct((12, D), F32), jax.ShapeDtypeStruct((1, D), F32)],
        compiler_params=_cp(("arbitrary",)),
    )(*[w[k] for k in names], *[m[k] for k in names], *[v[k] for k in names], wide_all, narrow_all, norm_all)
    groups = [dict(zip(names, res[q * n:(q + 1) * n])) for q in range(4)]
    return groups[0], groups[1], groups[2], groups[3], res[4 * n], res[4 * n + 1][0, 0]


def _pair_sum(mine, theirs, core, name):
    Q, R, C = theirs.shape
    tr, tc = _elementwise_tile(R, C)

    def body(core_ref, a_ref, b_ref, o_ref):
        o_ref[...] = (a_ref[...].astype(F32) + b_ref[...].astype(F32)).astype(BF16)

    spec = pl.BlockSpec((None, tr, tc), lambda q, i, j, core_ref: (q, i, j))
    own = pl.BlockSpec((None, tr, tc), lambda q, i, j, core_ref: (2 * q + core_ref[0], i, j))
    return pl.pallas_call(
        body, name=name,
        grid_spec=pltpu.PrefetchScalarGridSpec(num_scalar_prefetch=1, grid=(Q, R // tr, C // tc), in_specs=[own, spec], out_specs=spec),
        out_shape=jax.ShapeDtypeStruct((Q, R, C), BF16), compiler_params=_cp(("parallel", "parallel", "parallel")),
    )(core, mine, theirs)


def _dev_index(px, py, pc):
    return 4 * px + 2 * py + pc


def _mesh_position():
    return lax.axis_index("x"), lax.axis_index("y"), lax.axis_index("c")


def _gather_exchange(blocks):
    n = len(blocks)

    def make(ins, outs, sems):
        send_sems, recv_sems, local_sems = sems
        x, y, c = _mesh_position()
        me, sibling = (x, y, c), (x, y, 1 - c)
        chips = [(1 - x, y), (x, 1 - y), (1 - x, 1 - y)]

        def copy(a, k, block, to, src=None):
            dst = outs[a].at[_dev_index(*block)]
            return pltpu.make_async_remote_copy(src_ref=dst if src is None else src, dst_ref=dst, send_sem=send_sems.at[a, k],
                                                recv_sem=recv_sems.at[a, k], device_id=to, device_id_type=MESH)

        mine = [pltpu.make_async_copy(ins[a], outs[a].at[_dev_index(*me)], local_sems.at[a]) for a in range(n)]
        first = []
        for a in range(n):
            first.append(copy(a, 0, me, sibling, src=ins[a]))
            first += [copy(a, 1 + j, me, (*chip, c), src=ins[a]) for j, chip in enumerate(chips)]

        def start():
            for cp in mine + first:
                cp.start()

        def finish():
            passed = []
            for j, chip in enumerate(chips):
                for a in range(n):
                    copy(a, 1 + j, (*chip, c), me).wait_recv()
                    fwd = copy(a, 4 + j, (*chip, c), sibling)
                    fwd.start()
                    passed.append(fwd)
            for a in range(n):
                copy(a, 0, sibling, me).wait_recv()
                for j, chip in enumerate(chips):
                    copy(a, 4 + j, (*chip, 1 - c), me).wait_recv()
            for cp in first + passed:
                cp.wait_send()
            for cp in mine:
                cp.wait()

        return start, finish

    return _Exchange(list(blocks), [jax.ShapeDtypeStruct((NDEV,) + b.shape, b.dtype) for b in blocks],
                     [pltpu.SemaphoreType.DMA((n, 7)), pltpu.SemaphoreType.DMA((n, 7)), pltpu.SemaphoreType.DMA((n,))], make)


def _combine(*exchanges):
    def make(ins, outs, sems):
        pairs = []
        for e in exchanges:
            ni, no, ns = len(e.arrays), len(e.out_shape), len(e.sems)
            pairs.append(e.make(ins[:ni], outs[:no], sems[:ns]))
            ins, outs, sems = ins[ni:], outs[no:], sems[ns:]

        def start():
            for s, _ in pairs:
                s()

        def finish():
            for _, f in pairs:
                f()

        return start, finish

    return _Exchange(sum((list(e.arrays) for e in exchanges), []), sum((list(e.out_shape) for e in exchanges), []),
                     sum((list(e.sems) for e in exchanges), []), make)


def _start_wait_all(copies, local=()):
    def start():
        for cp in list(local) + list(copies):
            cp.start()

    def finish():
        for cp in copies:
            cp.wait()
        for cp in local:
            cp.wait()

    return start, finish


def _sibling_exchange(grads):
    n = len(grads)

    def make(ins, outs, sems):
        send_sems, recv_sems = sems
        x, y, c = _mesh_position()
        return _start_wait_all([pltpu.make_async_remote_copy(
            src_ref=ins[a].at[2 * q + (1 - c)], dst_ref=outs[a].at[q], send_sem=send_sems.at[a, q], recv_sem=recv_sems.at[a, q],
            device_id=(x, y, 1 - c), device_id_type=MESH) for a in range(n) for q in range(4)])

    return _Exchange(list(grads), [jax.ShapeDtypeStruct((4,) + g.shape[1:], g.dtype) for g in grads],
                     [pltpu.SemaphoreType.DMA((n, 4)), pltpu.SemaphoreType.DMA((n, 4))], make)


def _chips_exchange(partials):
    n = len(partials)

    def make(ins, outs, sems):
        send_sems, recv_sems = sems
        x, y, c = _mesh_position()
        chips = [(1 - x, y), (x, 1 - y), (1 - x, 1 - y)]
        return _start_wait_all([pltpu.make_async_remote_copy(
            src_ref=ins[a].at[2 * px + py], dst_ref=outs[a].at[k], send_sem=send_sems.at[a, k], recv_sem=recv_sems.at[a, k],
            device_id=(px, py, c), device_id_type=MESH) for a in range(n) for k, (px, py) in enumerate(chips)])

    return _Exchange(list(partials), [jax.ShapeDtypeStruct((3,) + g.shape[1:], g.dtype) for g in partials],
                     [pltpu.SemaphoreType.DMA((n, 3)), pltpu.SemaphoreType.DMA((n, 3))], make)


def _direct_gather_exchange(smalls):
    n = len(smalls)

    def make(ins, outs, sems):
        send_sems, recv_sems, local_sems = sems
        x, y, c = _mesh_position()
        copies, local = [], []
        for a in range(n):
            slot = outs[a].at[_dev_index(x, y, c)]
            local.append(pltpu.make_async_copy(ins[a], slot, local_sems.at[a]))
            for r in range(1, NDEV):
                peer = (x ^ ((r >> 2) & 1), y ^ ((r >> 1) & 1), c ^ (r & 1))
                copies.append(pltpu.make_async_remote_copy(src_ref=ins[a], dst_ref=slot, send_sem=send_sems.at[a, r - 1],
                                                           recv_sem=recv_sems.at[a, r - 1], device_id=peer, device_id_type=MESH))
        return _start_wait_all(copies, local=local)

    return _Exchange(list(smalls), [jax.ShapeDtypeStruct((NDEV,) + s.shape, s.dtype) for s in smalls],
                     [pltpu.SemaphoreType.DMA((n, 7)), pltpu.SemaphoreType.DMA((n, 7)), pltpu.SemaphoreType.DMA((n,))], make)


_W_IN_ROWS = {"u": (0, 1024), "v": (1024, 2048), "za": (2048, 3072), "zb": (3072, 5120), "xbc": (5120, 8192),
              "dt": (8192, 8224), "ga": (8224, 9248), "gb": (9248, 10272)}
_FWD_ORDER = ("xbc", "u", "zb", "v", "za", "ga", "gb", "dt")
_BWD_ORDER = ("xbc", "zb", "u", "v", "za", "ga", "gb", "dt")


def _w_in_t_rows(wt, order):
    assert order[-1] == "dt"
    z = jnp.zeros((NP - NIN, wt.shape[1]), wt.dtype)
    return jnp.concatenate([wt[slice(*_W_IN_ROWS[n])] for n in order] + [z], axis=0)


_WEIGHTS = ["norm_g", "w_in", "ln_a_g", "ln_a_b", "w_s", "b_s", "conv_w", "conv_b", "dt_bias", "a_log", "d_skip", "ssm_norm_g",
            "w_oa", "w_ob", "w_out", "ple_norm_g", "w_pg", "w_ple", "final_g"]


def _rows_pack(d, dtype):
    return jnp.concatenate([d["w_oa"].reshape(128, D), d["w_ob"].reshape(256, D), d["w_out"].reshape(128, D),
                            d["w_pg"].reshape(128, D), d["w_ple"].reshape(32, D)], axis=0).astype(dtype)


def kernel(x, p, norm_g, w_in, ln_a_g, ln_a_b, w_s, b_s, conv_w, conv_b, dt_bias, a_log, d_skip, ssm_norm_g, w_oa, w_ob, w_out, ple_norm_g, w_pg, w_ple, final_g, loss_target, m_norm_g, m_w_in, m_ln_a_g, m_ln_a_b, m_w_s, m_b_s, m_conv_w, m_conv_b, m_dt_bias, m_a_log, m_d_skip, m_ssm_norm_g, m_w_oa, m_w_ob, m_w_out, m_ple_norm_g, m_w_pg, m_w_ple, m_final_g, v_norm_g, v_w_in, v_ln_a_g, v_ln_a_b, v_w_s, v_b_s, v_conv_w, v_conv_b, v_dt_bias, v_a_log, v_d_skip, v_ssm_norm_g, v_w_oa, v_w_ob, v_w_out, v_ple_norm_g, v_w_pg, v_w_ple, v_final_g):
    args = dict(locals())
    w = {n: args[n] for n in _WEIGHTS}
    m = {n: args["m_" + n] for n in _WEIGHTS}
    v = {n: args["v_" + n] for n in _WEIGHTS}
    T = x.shape[1]
    xi, yi, ci = lax.axis_index("x"), lax.axis_index("y"), lax.axis_index("c")
    me = 4 * xi + 2 * yi + ci

    chip = 2 * xi + yi
    x2, p2, tgt = x.reshape(T, D), p.reshape(T, PLE), loss_target.reshape(T, D)

    norm_g2 = w["norm_g"].reshape(1, D)
    ws = jnp.where(jnp.tril(jnp.ones((CH, CH), bool))[None], w["w_s"].reshape(NG, CH, CH), 0.0).astype(BF16)
    wst = jnp.transpose(ws, (0, 2, 1))
    bst = jnp.broadcast_to(w["b_s"].reshape(NG, CH, 1), (NG, CH, 256))
    ln_g, ln_b = w["ln_a_g"].reshape(1, D), w["ln_a_b"].reshape(1, D)
    cb = w["conv_b"].reshape(1, CD)
    pad32 = lambda a: jnp.pad(a.reshape(1, NH), ((0, 0), (0, DTW - NH)))
    dtb, alog = pad32(w["dt_bias"]), pad32(w["a_log"])
    dskx = jnp.repeat(w["d_skip"].reshape(NH), HD).reshape(1, DI)
    sg = w["ssm_norm_g"].reshape(1, DI)
    ple_g, fin_g = w["ple_norm_g"].reshape(1, D), w["final_g"].reshape(1, D)
    e128 = (jnp.arange(DTW)[:, None] == (jnp.arange(DI)[None, :] // HD)).astype(BF16)
    et128 = e128.T
    gsel = ((jnp.arange(D)[:, None] // 256) == jnp.arange(128)[None, :]).astype(BF16)

    w_in_t = lambda a: jnp.transpose(a.reshape(D, WSH))
    (a_all,) = _run_exchange(_gather_exchange([w_in_t(w["w_in"]).astype(BF16)]), "all_gather_w_in")
    w_in_full_t = a_all.reshape(NIN, D)
    wp = _w_in_t_rows(w_in_full_t, _FWD_ORDER)
    wt_bwd = _w_in_t_rows(w_in_full_t, _BWD_ORDER)
    core = ci.reshape(1).astype(jnp.int32)
    (proj, dtr, h), (r_all, cw_all) = _proj_fwd(x2, norm_g2, wp, _gather_exchange([_rows_pack(w, BF16), w["conv_w"].reshape(4, CD // NDEV)]))
    f_oa = r_all[:, R_OA:R_OB].reshape(D, D)
    f_ob = r_all[:, R_OB:R_OUT].reshape(DI, D)
    f_out = r_all[:, R_OUT:R_PG].reshape(D, D)
    f_pg = r_all[:, R_PG:R_PLE].reshape(D, D)
    f_ple = jnp.transpose(r_all[:, R_PLE:R_ROWS].reshape(NDEV, PLE, 128), (1, 0, 2)).reshape(PLE, D)
    cw = jnp.transpose(cw_all, (1, 0, 2)).reshape(4, CD)

    ya = _gmlp_fwd(proj, ln_g, ln_b, ws, bst)
    y, yb, hprev, pre_all = _ssd_fwd(proj, dtr, cw, cb, dtb, alog, dskx, sg, e128)
    dx1, dx1b, mg, hp, dpre, dpe, doa, dob, dya, dyb, dgab, acc = _merge(x2, ya, yb, proj, p2, tgt, f_oa, f_ob, f_out, f_pg, f_ple,
                                                                         ple_g, fin_g)

    gple = jnp.transpose(_wgrad(p2, dpe, "wgrad_ple").reshape(PLE, NDEV, 128), (1, 0, 2)).reshape(NDEV, 32, D)
    gr = jnp.concatenate([_wgrad(ya, doa, "wgrad_oa").reshape(NDEV, 128, D), _wgrad(yb, dob, "wgrad_ob").reshape(NDEV, 256, D),
                          _wgrad(mg, dx1b, "wgrad_out").reshape(NDEV, 128, D), _wgrad(hp, dpre, "wgrad_pg").reshape(NDEV, 128, D),
                          gple], axis=1)
    (duvz, dws, dbs, dln), (from_sib_r,) = _gmlp_bwd(proj, dya, ln_g, ln_b, ws, wst, bst, gsel, _sibling_exchange([gr]))
    pr = _pair_sum(gr, from_sib_r, core, "pair_sum_rows")
    (dxz, ddt, dcw, dsm, dsg), (rr,) = _ssd_bwd(proj, pre_all, dtr, y, dyb, hprev, cw, dtb, alog, dskx, sg, e128, et128, _chips_exchange([pr]))

    g_xz = _wgrad(dxz, h, "wgrad_xbc_zb")
    g_w_in_t = jnp.concatenate([_wgrad(duvz, h, "wgrad_u_v_za"), g_xz[CD:], g_xz[:CD], _wgrad(ddt, h, "wgrad_dt")[:NH],
                                _wgrad(dgab, h, "wgrad_ga_gb")], axis=0)
    ga = g_w_in_t.reshape(NDEV, WSH, D)
    (from_sib_a,) = _run_exchange(_sibling_exchange([ga]), "reduce_scatter_sibling_w_in")
    pa = _pair_sum(ga, from_sib_a, core, "pair_sum_w_in")
    wide = jnp.concatenate([dln, acc, dsg.reshape(16, D), dcw.reshape(24, D)], axis=0)
    narrow = jnp.concatenate([dws.reshape(NG * CH, CH), jnp.pad(dbs[:, :NG].T, ((0, 8 - NG), (0, 0))), dsm], axis=0)
    (gx, dng), (ra, wide_all, narrow_all) = _proj_bwd(x2, dx1, norm_g2, wt_bwd, [dxz, duvz, dgab], ddt,
                                                      _combine(_chips_exchange([pa]), _direct_gather_exchange([wide, narrow])))
    (norm_all,) = _run_exchange(_direct_gather_exchange([dng]), "all_gather_d_norm_g")
    pa_own = lax.dynamic_index_in_dim(pa, chip, 0, keepdims=False)
    pr_own = lax.dynamic_index_in_dim(pr, chip, 0, keepdims=False)

    out_g, out_d, out_m, out_v = {}, {}, {}, {}
    outs = (out_g, out_d, out_m, out_v)
    res = _adamw(w_in_t(w["w_in"]), w_in_t(m["w_in"]), w_in_t(v["w_in"]), [pa_own, (ra, 0), (ra, 1), (ra, 2)], "adamw_w_in")
    for dst, val in zip(outs, res):
        dst["w_in"] = jnp.transpose(val).reshape(1, D, WSH)
    parts_r = [pr_own, (rr, 0), (rr, 1), (rr, 2)]
    for name, row, rows in (("w_oa", R_OA, 128), ("w_ob", R_OB, 256), ("w_out", R_OUT, 128), ("w_pg", R_PG, 128)):
        res = _adamw(w[name].reshape(rows, D), m[name].reshape(rows, D), v[name].reshape(rows, D), parts_r, "adamw_" + name,
                     part_row=row, block_rows=128)
        for dst, val in zip(outs, res):
            dst[name] = val.reshape(1, rows, D)
    res = _adamw(w["w_ple"].reshape(32, D), m["w_ple"].reshape(32, D), v["w_ple"].reshape(32, D), parts_r, "adamw_w_ple",
                 part_row=R_PLE, block_rows=32)
    for dst, val in zip(outs, res):
        dst["w_ple"] = val.reshape(1, PLE, 128)
    two_d = lambda d: {n: d[n].reshape(_SMALL_SHAPES[n]) for n in _SMALL_SHAPES}
    *res, g_cw_wide, loss = _adamw_small(two_d(w), two_d(m), two_d(v), wide_all, narrow_all, norm_all)
    for dst, val in zip(outs, res):
        dst.update({n: val[n].reshape(w[n].shape) for n in _SMALL_SHAPES})
    g_cw = lax.dynamic_slice_in_dim(g_cw_wide.reshape(4, CD), me * (CD // NDEV), CD // NDEV, axis=1).reshape(12, 128)
    res = _adamw(w["conv_w"].reshape(12, 128), m["conv_w"].reshape(12, 128), v["conv_w"].reshape(12, 128), [g_cw], "adamw_conv_w")
    for dst, val in zip((out_g, out_d, out_m, out_v), res):
        dst["conv_w"] = val.reshape(1, 4, CD // NDEV)

    return (loss, gx.reshape(1, T, D), *[out_g[n] for n in _WEIGHTS], *[out_d[n] for n in _WEIGHTS],
            *[out_m[n] for n in _WEIGHTS], *[out_v[n] for n in _WEIGHTS])
```

```python
import functools
import math
from typing import Callable, NamedTuple

import jax
import jax.numpy as jnp
from jax import lax
from jax.experimental import pallas as pl
from jax.experimental.pallas import tpu as pltpu

F32 = jnp.float32
BF16 = jnp.bfloat16
MESH = pl.DeviceIdType.MESH

D = 1024
DI = 2048
CD = 3072
NH = 32
HD = 64
NST = 128
NG = 4
CH = 128
PLE = 256
NIN = 10272
NDEV = 8
WSH = NIN // NDEV
EPS = 1e-6
OFF_XBC, OFF_U, OFF_ZB, OFF_V, OFF_ZA, OFF_GA, OFF_GB, OFF_DT = 0, 3072, 4096, 6144, 7168, 8192, 9216, 10240
NP = 10368
DTW = 128
R_OA, R_OB, R_OUT, R_PG, R_PLE, R_ROWS = 0, 128, 384, 512, 640, 672

ADAM_LR, ADAM_B1, ADAM_B2, ADAM_EPS, ADAM_WD, ADAM_STEP = 0.001, 0.9, 0.999, 1e-08, 0.01, 10

V7X_VMEM_LIMIT = 56 * 1024 * 1024


def _cp(sem=None):
    return pltpu.CompilerParams(dimension_semantics=sem, vmem_limit_bytes=V7X_VMEM_LIMIT)


def _dot(a, b, prec=None):
    return jnp.dot(a, b, preferred_element_type=F32, precision=prec)


def _dot_nt(a, b, prec=None):
    return lax.dot_general(a, b, (((1,), (1,)), ((), ())), preferred_element_type=F32, precision=prec)


def _dot_tn(a, b, prec=None):
    return lax.dot_general(a, b, (((0,), (0,)), ((), ())), preferred_element_type=F32, precision=prec)


def _sigmoid(x):
    return 1.0 / (1.0 + jnp.exp(-x))


def _gelu_and_grad(x):
    c = math.sqrt(2.0 / math.pi)
    x2 = x * x
    t = jnp.tanh(c * (x + 0.044715 * x * x2))
    g = 0.5 * x * (1.0 + t)
    dg = 0.5 * (1.0 + t) + 0.5 * x * (1.0 - t * t) * c * (1.0 + 3.0 * 0.044715 * x2)
    return g, dg


def _gelu(x):
    c = math.sqrt(2.0 / math.pi)
    return 0.5 * x * (1.0 + jnp.tanh(c * (x + 0.044715 * x * x * x)))


def _softplus(x):
    return jnp.maximum(x, 0.0) + jnp.log(1.0 + jnp.exp(-jnp.abs(x)))


def _full(shape):
    n = len(shape)
    return pl.BlockSpec(shape, lambda *_: (0,) * n)


_ANY = pl.BlockSpec(memory_space=pl.ANY)


class _Exchange(NamedTuple):
    arrays: list
    out_shape: list
    sems: list
    make: Callable


def _call(body, ex, *, name, grid, in_specs, out_specs, out_shape, scratch_shapes, args):
    ki, ko, ks = len(in_specs), len(out_specs), len(scratch_shapes)
    ei, eo = len(ex.arrays), len(ex.out_shape)
    last = [g - 1 for g in grid]

    def full_body(*refs):
        r = list(refs)
        ins, eins, r = r[:ki], r[ki:ki + ei], r[ki + ei:]
        outs, eouts, r = r[:ko], r[ko:ko + eo], r[ko + eo:]
        scr, esems = r[:ks], r[ks:]
        start, finish = ex.make(eins, eouts, esems)
        ids = [pl.program_id(a) for a in range(len(grid))]
        is_first = functools.reduce(lambda p, q: p & q, [i == 0 for i in ids])
        is_last = functools.reduce(lambda p, q: p & q, [i == l for i, l in zip(ids, last)])
        pl.when(is_first)(start)
        body(*ins, *outs, *scr)
        pl.when(is_last)(finish)

    res = pl.pallas_call(
        full_body, name=name, grid=grid, in_specs=list(in_specs) + [_ANY] * ei, out_specs=list(out_specs) + [_ANY] * eo,
        out_shape=list(out_shape) + list(ex.out_shape), scratch_shapes=list(scratch_shapes) + list(ex.sems),
        compiler_params=_cp(("arbitrary",) * len(grid)),
    )(*args, *ex.arrays)
    return res[:ko], res[ko:]


def _run_exchange(ex, name):
    ni, no = len(ex.arrays), len(ex.out_shape)

    def body(*refs):
        start, finish = ex.make(refs[:ni], refs[ni:ni + no], refs[ni + no:])
        start()
        finish()

    return pl.pallas_call(body, name=name, in_specs=[_ANY] * ni, out_specs=[_ANY] * no, out_shape=list(ex.out_shape),
                          scratch_shapes=list(ex.sems))(*ex.arrays)


def _proj_fwd(x, norm_g, wp, ex):
    T = x.shape[0]
    tm, tn = min(T, 1024), 2048
    nj = OFF_DT // tn
    assert OFF_DT % tn == 0 and OFF_DT + DTW == NP

    def body(x_ref, g_ref, w_ref, wdt_ref, proj_ref, dt_ref, h_ref, hs_ref):
        j = pl.program_id(1)

        @pl.when(j == 0)
        def _():
            xf = x_ref[...]
            r = lax.rsqrt(jnp.mean(xf * xf, axis=-1, keepdims=True) + EPS)
            h = (xf * r * g_ref[...]).astype(BF16)
            hs_ref[...] = h
            h_ref[...] = h

        proj_ref[...] = _dot_nt(hs_ref[...], w_ref[...]).astype(BF16)

        @pl.when(j == nj - 1)
        def _():
            dt_ref[...] = _dot_nt(hs_ref[...], wdt_ref[...])

    return _call(
        body, ex, name="proj_fwd", grid=(T // tm, nj),
        in_specs=[pl.BlockSpec((tm, D), lambda i, j: (i, 0)), _full((1, D)), pl.BlockSpec((tn, D), lambda i, j: (j, 0)),
                  pl.BlockSpec((DTW, D), lambda i, j: (OFF_DT // DTW, 0))],
        out_specs=[pl.BlockSpec((tm, tn), lambda i, j: (i, j)), pl.BlockSpec((tm, DTW), lambda i, j: (i, 0)),
                   pl.BlockSpec((tm, D), lambda i, j: (i, 0))],
        out_shape=[jax.ShapeDtypeStruct((T, OFF_DT), BF16), jax.ShapeDtypeStruct((T, DTW), F32), jax.ShapeDtypeStruct((T, D), BF16)],
        scratch_shapes=[pltpu.VMEM((tm, D), BF16)], args=(x, norm_g, wp, wp))


def _gmlp_tile():
    return 256


def _gmlp_fwd(proj, ln_g, ln_b, ws, bst):
    T = proj.shape[0]
    tm = min(T, _gmlp_tile())

    def body(u_ref, v_ref, z_ref, lg_ref, lb_ref, ws_ref, bs_ref, ya_ref, vn_s):
        vg = _gelu(v_ref[...].astype(F32))
        mu = jnp.mean(vg, axis=-1, keepdims=True)
        xc = vg - mu
        rstd = lax.rsqrt(jnp.mean(xc * xc, axis=-1, keepdims=True) + EPS)
        vn_s[...] = (xc * rstd * lg_ref[...] + lb_ref[...]).astype(BF16)
        for c in range(tm // CH):
            rs = slice(c * CH, (c + 1) * CH)
            for g in range(NG):
                cs_ = slice(g * 256, (g + 1) * 256)
                sv = _dot(ws_ref[g], vn_s[rs, cs_]) + bs_ref[g]
                z = z_ref[rs, cs_].astype(F32)
                ya_ref[rs, cs_] = (_gelu(u_ref[rs, cs_].astype(F32)) * sv * (z * _sigmoid(z))).astype(BF16)

    blk = lambda off: pl.BlockSpec((tm, D), lambda i: (i, off // D))
    return pl.pallas_call(
        body, name="gmlp_fwd", grid=(T // tm,),
        in_specs=[blk(OFF_U), blk(OFF_V), blk(OFF_ZA), _full((1, D)), _full((1, D)), _full((NG, CH, CH)), _full((NG, CH, 256))],
        out_specs=pl.BlockSpec((tm, D), lambda i: (i, 0)),
        out_shape=jax.ShapeDtypeStruct((T, D), BF16),
        scratch_shapes=[pltpu.VMEM((tm, D), BF16)],
        compiler_params=_cp(("parallel",)),
    )(proj, proj, proj, ln_g, ln_b, ws, bst)


def _gmlp_bwd(proj, dya, ln_g, ln_b, ws, wst, bst, gsel, ex):
    T = proj.shape[0]
    tm = min(T, _gmlp_tile())

    def body(u_ref, v_ref, z_ref, dy_ref, lg_ref, lb_ref, ws_ref, wst_ref, bs_ref, gsel_ref,
             d_ref, dws_ref, dbs_ref, dln_ref, vn_s, dsv_s, dvn_s):
        du_ref, dv_ref, dz_ref = d_ref.at[:, 0:D], d_ref.at[:, D:2 * D], d_ref.at[:, 2 * D:3 * D]
        @pl.when(pl.program_id(0) == 0)
        def _():
            dws_ref[...] = jnp.zeros_like(dws_ref)
            dbs_ref[...] = jnp.zeros_like(dbs_ref)
            dln_ref[...] = jnp.zeros_like(dln_ref)

        vg, dvg_dv = _gelu_and_grad(v_ref[...].astype(F32))
        mu = jnp.mean(vg, axis=-1, keepdims=True)
        xc = vg - mu
        rstd = lax.rsqrt(jnp.mean(xc * xc, axis=-1, keepdims=True) + EPS)
        vhat = xc * rstd
        vn_s[...] = (vhat * lg_ref[...] + lb_ref[...]).astype(BF16)
        ri = lax.broadcasted_iota(jnp.int32, (CH, CH), 0)
        ci = lax.broadcasted_iota(jnp.int32, (CH, CH), 1)
        tril = (ri >= ci).astype(F32)
        for c in range(tm // CH):
            rs = slice(c * CH, (c + 1) * CH)
            for g in range(NG):
                cs_ = slice(g * 256, (g + 1) * 256)
                vn = vn_s[rs, cs_]
                sv = _dot(ws_ref[g], vn) + bs_ref[g]
                z = z_ref[rs, cs_].astype(F32)
                sz = _sigmoid(z)
                ug, dug_du = _gelu_and_grad(u_ref[rs, cs_].astype(F32))
                dy = dy_ref[rs, cs_].astype(F32)
                t = dy * z * sz
                du_ref[rs, cs_] = (t * sv * dug_du).astype(BF16)
                dz_ref[rs, cs_] = (dy * ug * sv * sz * (1.0 + z * (1.0 - sz))).astype(BF16)
                dsv = (t * ug).astype(BF16)
                dsv_s[rs, cs_] = dsv
                dvn_s[rs, cs_] = _dot(wst_ref[g], dsv)
                dws_ref[g] += _dot_nt(dsv, vn) * tril
            dbs_ref[...] += _dot(dsv_s[rs, :], gsel_ref[...])
        dvn = dvn_s[...]
        dln_ref[0:1, :] += jnp.sum(dvn * vhat, axis=0, keepdims=True)
        dln_ref[1:2, :] += jnp.sum(dvn, axis=0, keepdims=True)
        dvh = dvn * lg_ref[...]
        dvg = rstd * (dvh - jnp.mean(dvh, axis=-1, keepdims=True) - vhat * jnp.mean(dvh * vhat, axis=-1, keepdims=True))
        dv_ref[...] = (dvg * dvg_dv).astype(BF16)

    blk = lambda off: pl.BlockSpec((tm, D), lambda i: (i, off // D))
    row = pl.BlockSpec((tm, D), lambda i: (i, 0))
    return _call(
        body, ex, name="gmlp_bwd", grid=(T // tm,),
        in_specs=[blk(OFF_U), blk(OFF_V), blk(OFF_ZA), row, _full((1, D)), _full((1, D)), _full((NG, CH, CH)),
                  _full((NG, CH, CH)), _full((NG, CH, 256)), _full((D, 128))],
        out_specs=[pl.BlockSpec((tm, 3 * D), lambda i: (i, 0)), _full((NG, CH, CH)), _full((CH, 128)), _full((8, D))],
        out_shape=[jax.ShapeDtypeStruct((T, 3 * D), BF16),
                   jax.ShapeDtypeStruct((NG, CH, CH), F32), jax.ShapeDtypeStruct((CH, 128), F32), jax.ShapeDtypeStruct((8, D), F32)],
        scratch_shapes=[pltpu.VMEM((tm, D), BF16), pltpu.VMEM((tm, D), BF16), pltpu.VMEM((tm, D), F32)],
        args=(proj, proj, proj, dya, ln_g, ln_b, ws, wst, bst, gsel))


def _shift_rows(cur, edge, j, down):
    r8 = lax.broadcasted_iota(jnp.int32, (8, 1), 0)
    if down:
        body = pltpu.roll(cur, j, 0)
        return jnp.concatenate([jnp.where(r8 >= j, body[0:8], pltpu.roll(edge, j, 0)), body[8:]], axis=0)
    body = pltpu.roll(cur, CH - j, 0)
    return jnp.concatenate([body[:CH - 8], jnp.where(r8 < 8 - j, body[CH - 8:], pltpu.roll(edge, 8 - j, 0))], axis=0)


def _conv_pre(x, x_before, cw_ref, cb_ref):
    pre = cb_ref[...] + cw_ref[3:4, :] * x
    for j in (1, 2, 3):
        pre = pre + cw_ref[3 - j:4 - j, :] * _shift_rows(x, x_before, j, down=True)
    return pre


def _split_dot(x, w, parts, w_left=False):
    acc, r = None, x
    for k in range(parts):
        hi = r.astype(BF16)
        d = _dot(w, hi) if w_left else _dot(hi, w)
        acc = d if acc is None else acc + d
        if k + 1 < parts:
            r = r - hi.astype(F32)
    return acc


def _chunk_decays(dt, alog_ref, e_ref, cs_s, cst_s, csx_s):
    a = -jnp.exp(alog_ref[...])
    ri = lax.broadcasted_iota(jnp.int32, (CH, CH), 0)
    ci = lax.broadcasted_iota(jnp.int32, (CH, CH), 1)
    tril = ri >= ci
    cs = _split_dot(dt * a, tril.astype(BF16), 3, w_left=True)
    cs_s[...] = cs
    cst_s[...] = cs.T
    csx_s[...] = _split_dot(cs, e_ref[...], 3)
    return a, tril, ri, ci


def _lmat(cst_s, h, tril):
    rowb = jnp.broadcast_to(cst_s[h:h + 1, :], (CH, CH))
    return jnp.exp(jnp.where(tril, rowb.T - rowb, -jnp.inf))


def _head_pair_rows(v, lane):
    return jnp.concatenate([jnp.where(lane < HD, v, 0.0), jnp.where(lane < HD, 0.0, v)], axis=0).astype(BF16)


def _ssd_fwd(proj, dtr, cw, cb, dtb, alog, dskx, sg, e128):
    T = proj.shape[0]
    nc = T // CH

    def body(xbc_ref, zb_ref, dt_ref, cw_ref, cb_ref, dtb_ref, alog_ref, dx_ref, sg_ref, e_ref,
             y_ref, yb_ref, hp_ref, pre_ref, xprev_s, h_s, cs_s, cst_s, csx_s, yz_s):
        @pl.when(pl.program_id(0) == 0)
        def _():
            xprev_s[...] = jnp.zeros_like(xprev_s)
            h_s[...] = jnp.zeros_like(h_s)

        x = xbc_ref[...].astype(F32)
        pre = _conv_pre(x, xprev_s[...], cw_ref, cb_ref)
        pre_ref[...] = pre
        xprev_s[...] = x[CH - 8:]
        xc = pre * _sigmoid(pre)
        dt = _softplus(dt_ref[...] + dtb_ref[...])
        a, tril, _, lane = _chunk_decays(dt, alog_ref, e_ref, cs_s, cst_s, csx_s)
        dt_x = _split_dot(dt, e_ref[...], 2)
        cs_last_x = csx_s[CH - 1:CH, :]
        hp_ref[0] = h_s[...]
        for g in range(NG):
            gs = slice(g * 512, (g + 1) * 512)
            bg = xc[:, DI + g * NST:DI + (g + 1) * NST].astype(BF16)
            cg = xc[:, DI + 512 + g * NST:DI + 512 + (g + 1) * NST].astype(BF16)
            cbm = _dot_nt(cg, bg)
            xg = xc[:, gs]
            xdt = xg * dt_x[:, gs]
            hprev = h_s[:, gs]
            csx = csx_s[:, gs]
            yoff = _dot(cg, hprev.astype(BF16)) * jnp.exp(csx)
            st = _dot_tn(bg, (xdt * jnp.exp(cs_last_x[:, gs] - csx)).astype(BF16))
            h_s[:, gs] = jnp.exp(cs_last_x[:, gs]) * hprev + st
            ssq = jnp.zeros((CH, 1), F32)
            for q in range(4):
                h0 = g * 8 + 2 * q
                ps = slice(q * 128, (q + 1) * 128)
                cols = slice(g * 512 + q * 128, g * 512 + (q + 1) * 128)
                m01 = jnp.concatenate([cbm * _lmat(cst_s, h0, tril), cbm * _lmat(cst_s, h0 + 1, tril)], axis=1).astype(BF16)
                yq = _dot(m01, _head_pair_rows(xdt[:, ps], lane)) + yoff[:, ps] + xg[:, ps] * dx_ref[:, cols]
                y_ref[:, cols] = yq
                z = zb_ref[:, cols].astype(F32)
                yz = yq * z * _sigmoid(z)
                yz_s[:, cols] = yz
                ssq = ssq + jnp.sum(yz * yz, axis=1, keepdims=True)
            rg = lax.rsqrt(ssq * (1.0 / 512.0) + EPS)
            yb_ref[:, gs] = (yz_s[:, gs] * rg * sg_ref[:, gs]).astype(BF16)

    return pl.pallas_call(
        body, name="ssd_fwd", grid=(nc,),
        in_specs=[pl.BlockSpec((CH, CD), lambda c: (c, OFF_XBC // CD)), pl.BlockSpec((CH, DI), lambda c: (c, OFF_ZB // DI)),
                  pl.BlockSpec((CH, DTW), lambda c: (c, 0)), _full((4, CD)), _full((1, CD)), _full((1, DTW)),
                  _full((1, DTW)), _full((1, DI)), _full((1, DI)), _full((DTW, DI))],
        out_specs=[pl.BlockSpec((CH, DI), lambda c: (c, 0)), pl.BlockSpec((CH, DI), lambda c: (c, 0)),
                   pl.BlockSpec((1, NST, DI), lambda c: (c, 0, 0)), pl.BlockSpec((CH, CD), lambda c: (c, 0))],
        out_shape=[jax.ShapeDtypeStruct((T, DI), F32), jax.ShapeDtypeStruct((T, DI), BF16),
                   jax.ShapeDtypeStruct((nc, NST, DI), F32), jax.ShapeDtypeStruct((T, CD), F32)],
        scratch_shapes=[pltpu.VMEM((8, CD), F32), pltpu.VMEM((NST, DI), F32), pltpu.VMEM((CH, CH), F32),
                        pltpu.VMEM((CH, CH), F32), pltpu.VMEM((CH, DI), F32), pltpu.VMEM((CH, DI), F32)],
        compiler_params=_cp(("arbitrary",)),
    )(proj, proj, dtr, cw, cb, dtb, alog, dskx, sg, e128)


def _ssd_bwd(proj, pre_all, dtr, y, dyb, hprev_all, cw, dtb, alog, dskx, sg, e128, et128, ex):
    T = proj.shape[0]
    nc = T // CH

    def body(xbc_ref, pre_ref, zb_ref, dt_ref, y_ref, dyb_ref, hp_ref, cw_ref, dtb_ref, alog_ref, dx_ref, sg_ref,
             e_ref, et_ref, d_ref, ddt_ref, dcw_ref, dsm_ref, dsg_ref,
             g_s, dpn_s, cs_s, cst_s, csx_s, dy_s, dxdt_s, dxs_s, dsd_s, dxc_s, gh_s):
        dxbc_ref, dzb_ref = d_ref.at[:, 0:CD], d_ref.at[:, CD:CD + DI]
        i = pl.program_id(0)

        @pl.when(i == 0)
        def _():
            g_s[...] = jnp.zeros_like(g_s)
            dpn_s[...] = jnp.zeros_like(dpn_s)
            dcw_ref[...] = jnp.zeros_like(dcw_ref)
            dsm_ref[...] = jnp.zeros_like(dsm_ref)
            dsg_ref[...] = jnp.zeros_like(dsg_ref)

        pre = pre_ref[...]
        sp = _sigmoid(pre)
        xc = pre * sp
        dtr = dt_ref[...] + dtb_ref[...]
        dt = _softplus(dtr)
        a, tril, ri, lane = _chunk_decays(dt, alog_ref, e_ref, cs_s, cst_s, csx_s)
        et = et_ref[...]
        dt_x = _split_dot(dt, e_ref[...], 2)
        cs_last_x = csx_s[CH - 1:CH, :]

        for g in range(NG):
            gs = slice(g * 512, (g + 1) * 512)
            z = zb_ref[:, gs].astype(F32)
            sz = _sigmoid(z)
            yv = y_ref[:, gs]
            yz = yv * z * sz
            rg = lax.rsqrt(jnp.mean(yz * yz, axis=-1, keepdims=True) + EPS)
            yn = yz * rg
            dyb = dyb_ref[:, gs].astype(F32)
            dsg_ref[0:1, gs] += jnp.sum(dyb * yn, axis=0, keepdims=True)
            dyn = dyb * sg_ref[:, gs]
            dyz = rg * (dyn - yn * jnp.mean(dyn * yn, axis=-1, keepdims=True))
            dy_s[:, gs] = dyz * z * sz
            dzb_ref[:, gs] = (dyz * yv * sz * (1.0 + z * (1.0 - sz))).astype(BF16)

        rsum = jnp.zeros((CH, DTW), F32)
        csum_t = jnp.zeros((DTW, CH), F32)
        for g in range(NG):
            gs = slice(g * 512, (g + 1) * 512)
            bg = xc[:, DI + g * NST:DI + (g + 1) * NST].astype(BF16)
            cg = xc[:, DI + 512 + g * NST:DI + 512 + (g + 1) * NST].astype(BF16)
            cbm = _dot_nt(cg, bg)
            xdt = xc[:, gs] * dt_x[:, gs]
            hprev = hp_ref[0, :, gs]
            hpb = hprev.astype(BF16)
            gn = g_s[:, gs]
            gnb = gn.astype(BF16)
            dy = dy_s[:, gs]
            csx = csx_s[:, gs]
            ecs = jnp.exp(csx)
            dec = jnp.exp(cs_last_x[:, gs] - csx)
            dye = (dy * ecs).astype(BF16)
            dc = _dot_nt(dye, hpb)
            dprev = _dot_tn(cg, dye)
            dxdt_state = dec * _dot(bg, gnb)
            db = _dot_nt((xdt * dec).astype(BF16), gnb)
            dcb = jnp.zeros((CH, CH), F32)
            for q in range(4):
                h0 = g * 8 + 2 * q
                ps = slice(q * 128, (q + 1) * 128)
                dyp = dy[:, ps]
                l0 = _lmat(cst_s, h0, tril)
                l1 = _lmat(cst_s, h0 + 1, tril)
                m0 = cbm * l0
                m1 = cbm * l1
                dm = _dot_nt(dyp.astype(BF16), _head_pair_rows(xdt[:, ps], lane))
                dm0 = dm[:, :CH]
                dm1 = dm[:, CH:]
                dcb = dcb + dm0 * l0 + dm1 * l1
                for hh, qm in ((h0, dm0 * m0), (h0 + 1, dm1 * m1)):
                    rsum = jnp.where(lane == hh, jnp.sum(qm, axis=1, keepdims=True), rsum)
                    csum_t = jnp.where(ri == hh, jnp.sum(qm, axis=0, keepdims=True), csum_t)
                mst = jnp.concatenate([m0, m1], axis=0).astype(BF16)
                d = _dot_tn(mst, _head_pair_rows(dyp, lane))
                dxdt_s[:, g * 512 + q * 128:g * 512 + (q + 1) * 128] = d + dxdt_state[:, ps]
            yoff = _dot(cg, hpb) * ecs
            dsd_s[:, gs] = xdt * dxdt_state
            dxs_s[:, gs] = dy * yoff
            dcbb = dcb.astype(BF16)
            dxc_s[:, DI + 512 + g * NST:DI + 512 + (g + 1) * NST] = dc + _dot(dcbb, bg)
            dxc_s[:, DI + g * NST:DI + (g + 1) * NST] = db + _dot_tn(dcbb, cg)
            gh_s[:, gs] = jnp.broadcast_to(jnp.sum(gn * hprev, axis=0, keepdims=True), (8, 512))
            g_s[:, gs] = dprev + jnp.exp(cs_last_x[:, gs]) * gn

        xs = xc[:, :DI]
        dy = dy_s[...]
        dxdt = dxdt_s[...]
        cs_last = cs_s[CH - 1:CH, :]
        state_e = _split_dot(dsd_s[...], et, 2)
        dcd = 0.125 * jnp.sum(_split_dot(gh_s[...], et, 2), axis=0, keepdims=True) * jnp.exp(cs_last)
        row = lax.broadcasted_iota(jnp.int32, (CH, 1), 0)
        dcs = rsum - csum_t.T + _split_dot(dxs_s[...], et, 2) - state_e
        dcs = dcs + jnp.where(row == CH - 1, jnp.sum(state_e, axis=0, keepdims=True) + dcd, 0.0)
        dda = _split_dot(dcs, (lane >= ri).astype(BF16), 3, w_left=True)
        ddt = dda * a + _dot((dxdt * xs).astype(BF16), et)
        ddtr = jnp.where(lane < NH, ddt * _sigmoid(dtr), 0.0)
        ddt_ref[...] = ddtr.astype(BF16)
        dsm_ref[0:1, :] += jnp.sum(ddtr, axis=0, keepdims=True)
        dsm_ref[1:2, :] += jnp.sum(dda * dt, axis=0, keepdims=True) * a
        dsm_ref[2:3, :] += jnp.sum(_dot((dy * xs).astype(BF16), et), axis=0, keepdims=True)
        dxc_s[:, :DI] = dxdt * dt_x + dy * dx_ref[...]

        dpre = dxc_s[...] * sp * (1.0 + pre * (1.0 - sp))
        dpn = dpn_s[...]
        x = xbc_ref[...].astype(F32)
        dcw_ref[4:5, :] += jnp.sum(dpre, axis=0, keepdims=True)
        dxbc = cw_ref[3:4, :] * dpre
        dcw_ref[3:4, :] += jnp.sum(dpre * x, axis=0, keepdims=True)
        for j in (1, 2, 3):
            ahead = _shift_rows(dpre, dpn, j, down=False)
            dcw_ref[3 - j:4 - j, :] += jnp.sum(ahead * x, axis=0, keepdims=True)
            dxbc = dxbc + cw_ref[3 - j:4 - j, :] * ahead
        dpn_s[...] = dpre[0:8]
        dxbc_ref[...] = dxbc.astype(BF16)

    rev = lambda c: nc - 1 - c
    return _call(
        body, ex, name="ssd_bwd", grid=(nc,),
        in_specs=[pl.BlockSpec((CH, CD), lambda c: (rev(c), OFF_XBC // CD)),
                  pl.BlockSpec((CH, CD), lambda c: (rev(c), 0)),
                  pl.BlockSpec((CH, DI), lambda c: (rev(c), OFF_ZB // DI)),
                  pl.BlockSpec((CH, DTW), lambda c: (rev(c), 0)),
                  pl.BlockSpec((CH, DI), lambda c: (rev(c), 0)), pl.BlockSpec((CH, DI), lambda c: (rev(c), 0)),
                  pl.BlockSpec((1, NST, DI), lambda c: (rev(c), 0, 0)),
                  _full((4, CD)), _full((1, DTW)), _full((1, DTW)), _full((1, DI)), _full((1, DI)),
                  _full((DTW, DI)), _full((DI, DTW))],
        out_specs=[pl.BlockSpec((CH, CD + DI), lambda c: (rev(c), 0)),
                   pl.BlockSpec((CH, DTW), lambda c: (rev(c), 0)), _full((8, CD)), _full((8, DTW)), _full((8, DI))],
        out_shape=[jax.ShapeDtypeStruct((T, CD + DI), BF16), jax.ShapeDtypeStruct((T, DTW), BF16),
                   jax.ShapeDtypeStruct((8, CD), F32), jax.ShapeDtypeStruct((8, DTW), F32), jax.ShapeDtypeStruct((8, DI), F32)],
        scratch_shapes=[pltpu.VMEM((NST, DI), F32), pltpu.VMEM((8, CD), F32), pltpu.VMEM((CH, CH), F32), pltpu.VMEM((CH, CH), F32),
                        pltpu.VMEM((CH, DI), F32), pltpu.VMEM((CH, DI), F32), pltpu.VMEM((CH, DI), F32), pltpu.VMEM((CH, DI), F32),
                        pltpu.VMEM((CH, DI), F32), pltpu.VMEM((CH, CD), F32), pltpu.VMEM((8, DI), F32)],
        args=(proj, pre_all, proj, dtr, y, dyb, hprev_all, cw, dtb, alog, dskx, sg, e128, et128))


def _merge_tile():
    return 256


def _merge(x, ya, yb, proj, p, tgt, w_oa, w_ob, w_out, w_pg, w_ple, ple_g, fin_g):
    T = x.shape[0]
    tm = min(T, _merge_tile())

    def body(x_ref, ya_ref, yb_ref, ga_ref, gb_ref, p_ref, t_ref, woa, wob, wout, wpg, wple, pg_ref, fg_ref,
             dx1_ref, dx1b_ref, mg_ref, hp_ref, dpre_ref, dpe_ref, doa_ref, dob_ref, dya_ref, dyb_ref, dg_ref, acc_ref):
        @pl.when(pl.program_id(0) == 0)
        def _():
            acc_ref[...] = jnp.zeros_like(acc_ref)

        oa = _dot(ya_ref[...], woa[...])
        ob = _dot(yb_ref[...], wob[...])
        sa = _sigmoid(ga_ref[...].astype(F32))
        sb = _sigmoid(gb_ref[...].astype(F32))
        mg = sa * oa + sb * ob
        mgb = mg.astype(BF16)
        mg_ref[...] = mgb
        x1 = x_ref[...] + _dot(mgb, wout[...])
        r2 = lax.rsqrt(jnp.mean(x1 * x1, axis=-1, keepdims=True) + EPS)
        xh1 = x1 * r2
        hpb = (xh1 * pg_ref[...]).astype(BF16)
        hp_ref[...] = hpb
        gate = _sigmoid(_dot(hpb, wpg[...]))
        pe = _dot(p_ref[...].astype(BF16), wple[...])
        x2 = x1 + gate * pe
        r3 = lax.rsqrt(jnp.mean(x2 * x2, axis=-1, keepdims=True) + EPS)
        xh2 = x2 * r3
        err = xh2 * fg_ref[...] - t_ref[...]
        acc_ref[2:3, :] += 0.5 * jnp.sum(jnp.mean(err * err, axis=-1, keepdims=True))
        dyo = err * (1.0 / D)
        acc_ref[0:1, :] += jnp.sum(dyo * xh2, axis=0, keepdims=True)
        dn = dyo * fg_ref[...]
        dx2 = r3 * (dn - xh2 * jnp.mean(dn * xh2, axis=-1, keepdims=True))
        dpe_ref[...] = (dx2 * gate).astype(BF16)
        dpre = (dx2 * pe * gate * (1.0 - gate)).astype(BF16)
        dpre_ref[...] = dpre
        dhp = _dot_nt(dpre, wpg[...])
        acc_ref[1:2, :] += jnp.sum(dhp * xh1, axis=0, keepdims=True)
        dhn = dhp * pg_ref[...]
        dx1 = dx2 + r2 * (dhn - xh1 * jnp.mean(dhn * xh1, axis=-1, keepdims=True))
        dx1_ref[...] = dx1
        dx1b = dx1.astype(BF16)
        dx1b_ref[...] = dx1b
        dmg = _dot_nt(dx1b, wout[...])
        doa = (dmg * sa).astype(BF16)
        dob = (dmg * sb).astype(BF16)
        doa_ref[...] = doa
        dob_ref[...] = dob
        dg_ref[:, :D] = (dmg * oa * sa * (1.0 - sa)).astype(BF16)
        dg_ref[:, D:] = (dmg * ob * sb * (1.0 - sb)).astype(BF16)
        dya_ref[...] = _dot_nt(doa, woa[...]).astype(BF16)
        dyb_ref[...] = _dot_nt(dob, wob[...]).astype(BF16)

    row = lambda w: pl.BlockSpec((tm, w), lambda i: (i, 0))
    wsp = lambda s: pl.BlockSpec(s, lambda i: (0, 0), pipeline_mode=pl.Buffered(1))
    return pl.pallas_call(
        body, name="merge", grid=(T // tm,),
        in_specs=[row(D), row(D), row(DI), pl.BlockSpec((tm, D), lambda i: (i, OFF_GA // D)),
                  pl.BlockSpec((tm, D), lambda i: (i, OFF_GB // D)), row(PLE), row(D),
                  wsp((D, D)), wsp((DI, D)), wsp((D, D)), wsp((D, D)), wsp((PLE, D)), _full((1, D)), _full((1, D))],
        out_specs=[row(D)] * 9 + [row(DI), row(2 * D), _full((8, D))],
        out_shape=[jax.ShapeDtypeStruct((T, D), F32)] + [jax.ShapeDtypeStruct((T, D), BF16)] * 8
        + [jax.ShapeDtypeStruct((T, DI), BF16), jax.ShapeDtypeStruct((T, 2 * D), BF16), jax.ShapeDtypeStruct((8, D), F32)],
        compiler_params=_cp(("arbitrary",)),
    )(x, ya, yb, proj, proj, p, tgt, w_oa, w_ob, w_out, w_pg, w_ple, ple_g, fin_g)


def _wgrad(a, b, name):
    T, K = a.shape
    N = b.shape[1]
    tt, tk, tn = min(T, 2048), min(K, 1024), min(N, 1024)
    nt = T // tt

    def body(a_ref, b_ref, o_ref, acc_s):
        t = pl.program_id(2)

        @pl.when(t == 0)
        def _():
            acc_s[...] = jnp.zeros_like(acc_s)

        acc_s[...] += _dot_tn(a_ref[...].astype(BF16), b_ref[...])

        @pl.when(t == nt - 1)
        def _():
            o_ref[...] = acc_s[...].astype(BF16)

    return pl.pallas_call(
        body, name=name, grid=(K // tk, N // tn, nt),
        in_specs=[pl.BlockSpec((tt, tk), lambda k, n, t: (t, k)), pl.BlockSpec((tt, tn), lambda k, n, t: (t, n))],
        out_specs=pl.BlockSpec((tk, tn), lambda k, n, t: (k, n)),
        out_shape=jax.ShapeDtypeStruct((K, N), BF16),
        scratch_shapes=[pltpu.VMEM((tk, tn), F32)],
        compiler_params=_cp(("parallel", "parallel", "arbitrary")),
    )(a, b)


def _proj_bwd(x, dx1, norm_g, wt, pieces, ddt, ex):
    T = x.shape[0]
    tm = min(T, 1024)
    nk = OFF_DT // D + 1
    starts = [sum(a.shape[1] for a in pieces[:n]) // D for n in range(len(pieces))]
    ranges = [(s, s + a.shape[1] // D) for s, a in zip(starts, pieces)]
    assert ranges[-1][1] == nk - 1
    npc = len(pieces)

    def body(x_ref, dx1_ref, g_ref, w_ref, wdt_ref, *rest):
        piece_refs, ddt_ref, gx_ref, dng_ref, acc_s = rest[:npc], rest[npc], rest[npc + 1], rest[npc + 2], rest[npc + 3]
        i, k = pl.program_id(0), pl.program_id(1)

        @pl.when((i == 0) & (k == 0))
        def _():
            dng_ref[...] = jnp.zeros_like(dng_ref)

        @pl.when(k == 0)
        def _():
            acc_s[...] = jnp.zeros_like(acc_s)

        for ref, (lo, hi) in zip(piece_refs, ranges):
            @pl.when((k >= lo) & (k < hi))
            def _(ref=ref):
                acc_s[...] += _dot(ref[...], w_ref[...])

        @pl.when(k == nk - 1)
        def _():
            dh = acc_s[...] + _dot(ddt_ref[...], wdt_ref[...])
            xf = x_ref[...]
            r = lax.rsqrt(jnp.mean(xf * xf, axis=-1, keepdims=True) + EPS)
            xh = xf * r
            dng_ref[0:1, :] += jnp.sum(dh * xh, axis=0, keepdims=True)
            dxn = dh * g_ref[...]
            gx_ref[...] = dx1_ref[...] + r * (dxn - xh * jnp.mean(dxn * xh, axis=-1, keepdims=True))

    def piece_spec(lo, hi):
        return pl.BlockSpec((tm, D), lambda i, k: (i, jnp.clip(k - lo, 0, hi - lo - 1)))

    row = pl.BlockSpec((tm, D), lambda i, k: (i, 0))
    return _call(
        body, ex, name="proj_bwd", grid=(T // tm, nk),
        in_specs=[row, row, _full((1, D)), pl.BlockSpec((D, D), lambda i, k: (jnp.minimum(k, nk - 2), 0)),
                  pl.BlockSpec((DTW, D), lambda i, k: (OFF_DT // DTW, 0))]
        + [piece_spec(lo, hi) for lo, hi in ranges] + [pl.BlockSpec((tm, DTW), lambda i, k: (i, 0))],
        out_specs=[row, _full((8, D))],
        out_shape=[jax.ShapeDtypeStruct((T, D), F32), jax.ShapeDtypeStruct((8, D), F32)],
        scratch_shapes=[pltpu.VMEM((tm, D), F32)],
        args=(x, dx1, norm_g, wt, wt, *pieces, ddt))


def _elementwise_tile(R, C, limit=1 << 20):
    if R * C * 4 <= limit:
        return R, C
    rows = [t for t in range(16, R, 16) if R % t == 0 and t * C * 4 <= limit]
    if rows:
        return rows[-1], C
    cols = [t for t in range(128, C, 128) if C % t == 0 and R * t * 4 <= limit]
    return R, cols[-1]


def _adam_update(w, m, v, g):
    c1 = 1.0 - ADAM_B1 ** ADAM_STEP
    c2 = 1.0 - ADAM_B2 ** ADAM_STEP
    mm = ADAM_B1 * m + (1.0 - ADAM_B1) * g
    vv = ADAM_B2 * v + (1.0 - ADAM_B2) * (g * g)
    return -ADAM_LR * ((mm / c1) / (jnp.sqrt(vv / c2) + ADAM_EPS) + ADAM_WD * w), mm, vv


def _adamw(w, m, v, parts, name, part_row=0, block_rows=None):
    R, C = w.shape
    tr, tc = _elementwise_tile(R, C) if block_rows is None else (block_rows, C)
    assert R % tr == 0 and part_row % tr == 0
    first = part_row // tr
    n = len(parts)
    wspec = pl.BlockSpec((tr, tc), lambda i, j: (i, j))
    flat = pl.BlockSpec((tr, tc), lambda i, j: (first + i, j))
    slot = lambda k: pl.BlockSpec((None, tr, tc), lambda i, j: (k, first + i, j))
    part_specs = [slot(p[1]) if isinstance(p, tuple) else flat for p in parts]
    part_arrays = [p[0] if isinstance(p, tuple) else p for p in parts]

    def body(*refs):
        w_ref, m_ref, v_ref = refs[:3]
        g_ref, d_ref, nm_ref, nv_ref = refs[3 + n:]
        g = refs[3][...].astype(F32)
        for r in refs[4:3 + n]:
            g = g + r[...].astype(F32)
        g_ref[...] = g
        d_ref[...], nm_ref[...], nv_ref[...] = _adam_update(w_ref[...], m_ref[...], v_ref[...], g)

    return pl.pallas_call(
        body, name=name, grid=(R // tr, C // tc), in_specs=[wspec] * 3 + part_specs, out_specs=[wspec] * 4,
        out_shape=[jax.ShapeDtypeStruct(w.shape, F32)] * 4, compiler_params=_cp(("parallel", "parallel")),
    )(w, m, v, *part_arrays)


_SMALL_WIDE = {"ln_a_g": 0, "ln_a_b": 1, "final_g": 8, "ple_norm_g": 9, "ssm_norm_g": 16, "conv_b": 44}
_WIDE_CONV_W, _WIDE_ROWS = 32, 56
_WIDE_LOSS = 10
_SMALL_NARROW = {"w_s": (0, NG * CH, 128), "b_s": (512, NG, 128), "dt_bias": (520, 1, NH), "a_log": (521, 1, NH),
                 "d_skip": (522, 1, NH)}
_NARROW_ROWS = 528
_SMALL_SHAPES = {"norm_g": (1, D), "ln_a_g": (1, D), "ln_a_b": (1, D), "ple_norm_g": (1, D), "final_g": (1, D),
                 "ssm_norm_g": (1, DI), "conv_b": (1, CD), "w_s": (NG * CH, CH), "b_s": (NG, CH), "dt_bias": (1, NH),
                 "a_log": (1, NH), "d_skip": (1, NH)}


def _adamw_small(w, m, v, wide_all, narrow_all, norm_all):
    names = list(_SMALL_SHAPES)
    n = len(names)

    def body(*refs):
        wr, mr, vr = refs[:n], refs[n:2 * n], refs[2 * n:3 * n]
        wide_ref, narrow_ref, norm_ref = refs[3 * n:3 * n + 3]
        outs = refs[3 * n + 3:]
        gr, dr, nmr, nvr, cw_ref, loss_ref = outs[:n], outs[n:2 * n], outs[2 * n:3 * n], outs[3 * n:4 * n], outs[4 * n], outs[4 * n + 1]

        def total(ref, rows, lanes):
            acc = ref[0, rows, lanes]
            for d in range(1, NDEV):
                acc = acc + ref[d, rows, lanes]
            return acc

        for k, name in enumerate(names):
            if name in _SMALL_WIDE or name == "norm_g":
                for part in range(_SMALL_SHAPES[name][1] // D):
                    pack, r = (norm_ref, 0) if name == "norm_g" else (wide_ref, _SMALL_WIDE[name] + part)
                    cols = slice(part * D, (part + 1) * D)
                    g = total(pack, slice(r, r + 1), slice(None))
                    gr[k][:, cols] = g
                    dr[k][:, cols], nmr[k][:, cols], nvr[k][:, cols] = _adam_update(wr[k][:, cols], mr[k][:, cols], vr[k][:, cols], g)
            else:
                r, rows, lanes = _SMALL_NARROW[name]
                g = total(narrow_ref, slice(r, r + rows), slice(0, lanes))
                gr[k][...] = g
                dr[k][...], nmr[k][...], nvr[k][...] = _adam_update(wr[k][...], mr[k][...], vr[k][...], g)
        cw_ref[...] = total(wide_ref, slice(_WIDE_CONV_W, _WIDE_CONV_W + 12), slice(None))
        loss_ref[...] = total(wide_ref, slice(_WIDE_LOSS, _WIDE_LOSS + 1), slice(None))

    shapes = [jax.ShapeDtypeStruct(_SMALL_SHAPES[k], F32) for k in names]
    specs = [_full(_SMALL_SHAPES[k]) for k in names]
    res = pl.pallas_call(
        body, name="adamw_small", grid=(1,),
        in_specs=specs * 3 + [_full(wide_all.shape), _full(narrow_all.shape), _full(norm_all.shape)],
        out_specs=specs * 4 + [_full((12, D)), _full((1, D))],
        out_shape=shapes * 4 + [jax.ShapeDtypeStruct((12, D), F32), jax.ShapeDtypeStruct((1, D), F32)],
        compiler_params=_cp(("arbitrary",)),
    )(*[w[k] for k in names], *[m[k] for k in names], *[v[k] for k in names], wide_all, narrow_all, norm_all)
    groups = [dict(zip(names, res[q * n:(q + 1) * n])) for q in range(4)]
    return groups[0], groups[1], groups[2], groups[3], res[4 * n], res[4 * n + 1][0, 0]


def _pair_sum(mine, theirs, core, name):
    Q, R, C = theirs.shape
    tr, tc = _elementwise_tile(R, C)

    def body(core_ref, a_ref, b_ref, o_ref):
        o_ref[...] = (a_ref[...].astype(F32) + b_ref[...].astype(F32)).astype(BF16)

    spec = pl.BlockSpec((None, tr, tc), lambda q, i, j, core_ref: (q, i, j))
    own = pl.BlockSpec((None, tr, tc), lambda q, i, j, core_ref: (2 * q + core_ref[0], i, j))
    return pl.pallas_call(
        body, name=name,
        grid_spec=pltpu.PrefetchScalarGridSpec(num_scalar_prefetch=1, grid=(Q, R // tr, C // tc), in_specs=[own, spec], out_specs=spec),
        out_shape=jax.ShapeDtypeStruct((Q, R, C), BF16), compiler_params=_cp(("parallel", "parallel", "parallel")),
    )(core, mine, theirs)


def _dev_index(px, py, pc):
    return 4 * px + 2 * py + pc


def _mesh_position():
    return lax.axis_index("x"), lax.axis_index("y"), lax.axis_index("c")


def _gather_exchange(blocks):
    n = len(blocks)

    def make(ins, outs, sems):
        send_sems, recv_sems, local_sems = sems
        x, y, c = _mesh_position()
        me, sibling = (x, y, c), (x, y, 1 - c)
        chips = [(1 - x, y), (x, 1 - y), (1 - x, 1 - y)]

        def copy(a, k, block, to, src=None):
            dst = outs[a].at[_dev_index(*block)]
            return pltpu.make_async_remote_copy(src_ref=dst if src is None else src, dst_ref=dst, send_sem=send_sems.at[a, k],
                                                recv_sem=recv_sems.at[a, k], device_id=to, device_id_type=MESH)

        mine = [pltpu.make_async_copy(ins[a], outs[a].at[_dev_index(*me)], local_sems.at[a]) for a in range(n)]
        first = []
        for a in range(n):
            first.append(copy(a, 0, me, sibling, src=ins[a]))
            first += [copy(a, 1 + j, me, (*chip, c), src=ins[a]) for j, chip in enumerate(chips)]

        def start():
            for cp in mine + first:
                cp.start()

        def finish():
            passed = []
            for j, chip in enumerate(chips):
                for a in range(n):
                    copy(a, 1 + j, (*chip, c), me).wait_recv()
                    fwd = copy(a, 4 + j, (*chip, c), sibling)
                    fwd.start()
                    passed.append(fwd)
            for a in range(n):
                copy(a, 0, sibling, me).wait_recv()
                for j, chip in enumerate(chips):
                    copy(a, 4 + j, (*chip, 1 - c), me).wait_recv()
            for cp in first + passed:
                cp.wait_send()
            for cp in mine:
                cp.wait()

        return start, finish

    return _Exchange(list(blocks), [jax.ShapeDtypeStruct((NDEV,) + b.shape, b.dtype) for b in blocks],
                     [pltpu.SemaphoreType.DMA((n, 7)), pltpu.SemaphoreType.DMA((n, 7)), pltpu.SemaphoreType.DMA((n,))], make)


def _relay_gather_exchange(blocks):
    n = len(blocks)

    def make(ins, outs, sems):
        send_sems, recv_sems, local_sems = sems
        x, y, c = _mesh_position()
        me, sibling = (x, y, c), (x, y, 1 - c)
        x_nbr, y_nbr, diag = (1 - x, y), (x, 1 - y), (1 - x, 1 - y)
        relay_from = (jnp.where(c == 0, x, 1 - x), jnp.where(c == 0, 1 - y, y))
        relay_to = (jnp.where(c == 0, 1 - x, x), jnp.where(c == 0, y, 1 - y))

        def copy(a, k, block, to, src=None):
            dst = outs[a].at[_dev_index(*block)]
            return pltpu.make_async_remote_copy(src_ref=dst if src is None else src, dst_ref=dst, send_sem=send_sems.at[a, k],
                                                recv_sem=recv_sems.at[a, k], device_id=to, device_id_type=MESH)

        mine = [pltpu.make_async_copy(ins[a], outs[a].at[_dev_index(*me)], local_sems.at[a]) for a in range(n)]
        first = []
        for a in range(n):
            first += [copy(a, 0, me, sibling, src=ins[a]), copy(a, 1, me, (*x_nbr, c), src=ins[a]), copy(a, 2, me, (*y_nbr, c), src=ins[a])]

        def start():
            for cp in mine + first:
                cp.start()

        def finish():
            later = []
            for a in range(n):
                copy(a, 1, (*x_nbr, c), me).wait_recv()
                copy(a, 2, (*y_nbr, c), me).wait_recv()
                later.append(copy(a, 3, (*relay_from, c), (*relay_to, c)))
                later += [copy(a, 4, (*x_nbr, c), sibling), copy(a, 5, (*y_nbr, c), sibling)]
                for cp in later[-3:]:
                    cp.start()
            for a in range(n):
                copy(a, 3, (*diag, c), me).wait_recv()
                later.append(copy(a, 6, (*diag, c), sibling))
                later[-1].start()
            for a in range(n):
                copy(a, 0, sibling, me).wait_recv()
                for k, chip in ((4, x_nbr), (5, y_nbr), (6, diag)):
                    copy(a, k, (*chip, 1 - c), me).wait_recv()
            for cp in first + later:
                cp.wait_send()
            for cp in mine:
                cp.wait()

        return start, finish

    return _Exchange(list(blocks), [jax.ShapeDtypeStruct((NDEV,) + b.shape, b.dtype) for b in blocks],
                     [pltpu.SemaphoreType.DMA((n, 7)), pltpu.SemaphoreType.DMA((n, 7)), pltpu.SemaphoreType.DMA((n,))], make)


def _combine(*exchanges):
    def make(ins, outs, sems):
        pairs = []
        for e in exchanges:
            ni, no, ns = len(e.arrays), len(e.out_shape), len(e.sems)
            pairs.append(e.make(ins[:ni], outs[:no], sems[:ns]))
            ins, outs, sems = ins[ni:], outs[no:], sems[ns:]

        def start():
            for s, _ in pairs:
                s()

        def finish():
            for _, f in pairs:
                f()

        return start, finish

    return _Exchange(sum((list(e.arrays) for e in exchanges), []), sum((list(e.out_shape) for e in exchanges), []),
                     sum((list(e.sems) for e in exchanges), []), make)


def _start_wait_all(copies, local=()):
    def start():
        for cp in list(local) + list(copies):
            cp.start()

    def finish():
        for cp in copies:
            cp.wait()
        for cp in local:
            cp.wait()

    return start, finish


def _sibling_exchange(grads):
    n = len(grads)

    def make(ins, outs, sems):
        send_sems, recv_sems = sems
        x, y, c = _mesh_position()
        return _start_wait_all([pltpu.make_async_remote_copy(
            src_ref=ins[a].at[2 * q + (1 - c)], dst_ref=outs[a].at[q], send_sem=send_sems.at[a, q], recv_sem=recv_sems.at[a, q],
            device_id=(x, y, 1 - c), device_id_type=MESH) for a in range(n) for q in range(4)])

    return _Exchange(list(grads), [jax.ShapeDtypeStruct((4,) + g.shape[1:], g.dtype) for g in grads],
                     [pltpu.SemaphoreType.DMA((n, 4)), pltpu.SemaphoreType.DMA((n, 4))], make)


def _chips_exchange(partials):
    n = len(partials)

    def make(ins, outs, sems):
        send_sems, recv_sems = sems
        x, y, c = _mesh_position()
        chips = [(1 - x, y), (x, 1 - y), (1 - x, 1 - y)]
        return _start_wait_all([pltpu.make_async_remote_copy(
            src_ref=ins[a].at[2 * px + py], dst_ref=outs[a].at[k], send_sem=send_sems.at[a, k], recv_sem=recv_sems.at[a, k],
            device_id=(px, py, c), device_id_type=MESH) for a in range(n) for k, (px, py) in enumerate(chips)])

    return _Exchange(list(partials), [jax.ShapeDtypeStruct((3,) + g.shape[1:], g.dtype) for g in partials],
                     [pltpu.SemaphoreType.DMA((n, 3)), pltpu.SemaphoreType.DMA((n, 3))], make)


def _direct_gather_exchange(smalls):
    n = len(smalls)

    def make(ins, outs, sems):
        send_sems, recv_sems, local_sems = sems
        x, y, c = _mesh_position()
        copies, local = [], []
        for a in range(n):
            slot = outs[a].at[_dev_index(x, y, c)]
            local.append(pltpu.make_async_copy(ins[a], slot, local_sems.at[a]))
            for r in range(1, NDEV):
                peer = (x ^ ((r >> 2) & 1), y ^ ((r >> 1) & 1), c ^ (r & 1))
                copies.append(pltpu.make_async_remote_copy(src_ref=ins[a], dst_ref=slot, send_sem=send_sems.at[a, r - 1],
                                                           recv_sem=recv_sems.at[a, r - 1], device_id=peer, device_id_type=MESH))
        return _start_wait_all(copies, local=local)

    return _Exchange(list(smalls), [jax.ShapeDtypeStruct((NDEV,) + s.shape, s.dtype) for s in smalls],
                     [pltpu.SemaphoreType.DMA((n, 7)), pltpu.SemaphoreType.DMA((n, 7)), pltpu.SemaphoreType.DMA((n,))], make)


_W_IN_ROWS = {"u": (0, 1024), "v": (1024, 2048), "za": (2048, 3072), "zb": (3072, 5120), "xbc": (5120, 8192),
              "dt": (8192, 8224), "ga": (8224, 9248), "gb": (9248, 10272)}
_FWD_ORDER = ("xbc", "u", "zb", "v", "za", "ga", "gb", "dt")
_BWD_ORDER = ("xbc", "zb", "u", "v", "za", "ga", "gb", "dt")


def _w_in_t_rows(wt, order):
    assert order[-1] == "dt"
    z = jnp.zeros((NP - NIN, wt.shape[1]), wt.dtype)
    return jnp.concatenate([wt[slice(*_W_IN_ROWS[n])] for n in order] + [z], axis=0)


_WEIGHTS = ["norm_g", "w_in", "ln_a_g", "ln_a_b", "w_s", "b_s", "conv_w", "conv_b", "dt_bias", "a_log", "d_skip", "ssm_norm_g",
            "w_oa", "w_ob", "w_out", "ple_norm_g", "w_pg", "w_ple", "final_g"]


def _rows_pack(d, dtype):
    return jnp.concatenate([d["w_oa"].reshape(128, D), d["w_ob"].reshape(256, D), d["w_out"].reshape(128, D),
                            d["w_pg"].reshape(128, D), d["w_ple"].reshape(32, D)], axis=0).astype(dtype)


def kernel(x, p, norm_g, w_in, ln_a_g, ln_a_b, w_s, b_s, conv_w, conv_b, dt_bias, a_log, d_skip, ssm_norm_g, w_oa, w_ob, w_out, ple_norm_g, w_pg, w_ple, final_g, loss_target, m_norm_g, m_w_in, m_ln_a_g, m_ln_a_b, m_w_s, m_b_s, m_conv_w, m_conv_b, m_dt_bias, m_a_log, m_d_skip, m_ssm_norm_g, m_w_oa, m_w_ob, m_w_out, m_ple_norm_g, m_w_pg, m_w_ple, m_final_g, v_norm_g, v_w_in, v_ln_a_g, v_ln_a_b, v_w_s, v_b_s, v_conv_w, v_conv_b, v_dt_bias, v_a_log, v_d_skip, v_ssm_norm_g, v_w_oa, v_w_ob, v_w_out, v_ple_norm_g, v_w_pg, v_w_ple, v_final_g):
    args = dict(locals())
    w = {n: args[n] for n in _WEIGHTS}
    m = {n: args["m_" + n] for n in _WEIGHTS}
    v = {n: args["v_" + n] for n in _WEIGHTS}
    T = x.shape[1]
    xi, yi, ci = lax.axis_index("x"), lax.axis_index("y"), lax.axis_index("c")
    me = 4 * xi + 2 * yi + ci

    chip = 2 * xi + yi
    x2, p2, tgt = x.reshape(T, D), p.reshape(T, PLE), loss_target.reshape(T, D)

    norm_g2 = w["norm_g"].reshape(1, D)
    ws = jnp.where(jnp.tril(jnp.ones((CH, CH), bool))[None], w["w_s"].reshape(NG, CH, CH), 0.0).astype(BF16)
    wst = jnp.transpose(ws, (0, 2, 1))
    bst = jnp.broadcast_to(w["b_s"].reshape(NG, CH, 1), (NG, CH, 256))
    ln_g, ln_b = w["ln_a_g"].reshape(1, D), w["ln_a_b"].reshape(1, D)
    cb = w["conv_b"].reshape(1, CD)
    pad32 = lambda a: jnp.pad(a.reshape(1, NH), ((0, 0), (0, DTW - NH)))
    dtb, alog = pad32(w["dt_bias"]), pad32(w["a_log"])
    dskx = jnp.repeat(w["d_skip"].reshape(NH), HD).reshape(1, DI)
    sg = w["ssm_norm_g"].reshape(1, DI)
    ple_g, fin_g = w["ple_norm_g"].reshape(1, D), w["final_g"].reshape(1, D)
    e128 = (jnp.arange(DTW)[:, None] == (jnp.arange(DI)[None, :] // HD)).astype(BF16)
    et128 = e128.T
    gsel = ((jnp.arange(D)[:, None] // 256) == jnp.arange(128)[None, :]).astype(BF16)

    w_in_t = lambda a: jnp.transpose(a.reshape(D, WSH))
    (a_all,) = _run_exchange(_relay_gather_exchange([w_in_t(w["w_in"]).astype(BF16)]), "all_gather_w_in")
    w_in_full_t = a_all.reshape(NIN, D)
    wp = _w_in_t_rows(w_in_full_t, _FWD_ORDER)
    wt_bwd = _w_in_t_rows(w_in_full_t, _BWD_ORDER)
    core = ci.reshape(1).astype(jnp.int32)
    (proj, dtr, h), (r_all, cw_all) = _proj_fwd(x2, norm_g2, wp, _gather_exchange([_rows_pack(w, BF16), w["conv_w"].reshape(4, CD // NDEV)]))
    f_oa = r_all[:, R_OA:R_OB].reshape(D, D)
    f_ob = r_all[:, R_OB:R_OUT].reshape(DI, D)
    f_out = r_all[:, R_OUT:R_PG].reshape(D, D)
    f_pg = r_all[:, R_PG:R_PLE].reshape(D, D)
    f_ple = jnp.transpose(r_all[:, R_PLE:R_ROWS].reshape(NDEV, PLE, 128), (1, 0, 2)).reshape(PLE, D)
    cw = jnp.transpose(cw_all, (1, 0, 2)).reshape(4, CD)

    ya = _gmlp_fwd(proj, ln_g, ln_b, ws, bst)
    y, yb, hprev, pre_all = _ssd_fwd(proj, dtr, cw, cb, dtb, alog, dskx, sg, e128)
    dx1, dx1b, mg, hp, dpre, dpe, doa, dob, dya, dyb, dgab, acc = _merge(x2, ya, yb, proj, p2, tgt, f_oa, f_ob, f_out, f_pg, f_ple,
                                                                         ple_g, fin_g)

    gple = jnp.transpose(_wgrad(p2, dpe, "wgrad_ple").reshape(PLE, NDEV, 128), (1, 0, 2)).reshape(NDEV, 32, D)
    gr = jnp.concatenate([_wgrad(ya, doa, "wgrad_oa").reshape(NDEV, 128, D), _wgrad(yb, dob, "wgrad_ob").reshape(NDEV, 256, D),
                          _wgrad(mg, dx1b, "wgrad_out").reshape(NDEV, 128, D), _wgrad(hp, dpre, "wgrad_pg").reshape(NDEV, 128, D),
                          gple], axis=1)
    (duvz, dws, dbs, dln), (from_sib_r,) = _gmlp_bwd(proj, dya, ln_g, ln_b, ws, wst, bst, gsel, _sibling_exchange([gr]))
    pr = _pair_sum(gr, from_sib_r, core, "pair_sum_rows")
    (dxz, ddt, dcw, dsm, dsg), (rr,) = _ssd_bwd(proj, pre_all, dtr, y, dyb, hprev, cw, dtb, alog, dskx, sg, e128, et128, _chips_exchange([pr]))

    g_xz = _wgrad(dxz, h, "wgrad_xbc_zb")
    g_w_in_t = jnp.concatenate([_wgrad(duvz, h, "wgrad_u_v_za"), g_xz[CD:], g_xz[:CD], _wgrad(ddt, h, "wgrad_dt")[:NH],
                                _wgrad(dgab, h, "wgrad_ga_gb")], axis=0)
    ga = g_w_in_t.reshape(NDEV, WSH, D)
    (from_sib_a,) = _run_exchange(_sibling_exchange([ga]), "reduce_scatter_sibling_w_in")
    pa = _pair_sum(ga, from_sib_a, core, "pair_sum_w_in")
    wide = jnp.concatenate([dln, acc, dsg.reshape(16, D), dcw.reshape(24, D)], axis=0)
    narrow = jnp.concatenate([dws.reshape(NG * CH, CH), jnp.pad(dbs[:, :NG].T, ((0, 8 - NG), (0, 0))), dsm], axis=0)
    (gx, dng), (ra, wide_all, narrow_all) = _proj_bwd(x2, dx1, norm_g2, wt_bwd, [dxz, duvz, dgab], ddt,
                                                      _combine(_chips_exchange([pa]), _direct_gather_exchange([wide, narrow])))
    (norm_all,) = _run_exchange(_direct_gather_exchange([dng]), "all_gather_d_norm_g")
    pa_own = lax.dynamic_index_in_dim(pa, chip, 0, keepdims=False)
    pr_own = lax.dynamic_index_in_dim(pr, chip, 0, keepdims=False)

    out_g, out_d, out_m, out_v = {}, {}, {}, {}
    outs = (out_g, out_d, out_m, out_v)
    res = _adamw(w_in_t(w["w_in"]), w_in_t(m["w_in"]), w_in_t(v["w_in"]), [pa_own, (ra, 0), (ra, 1), (ra, 2)], "adamw_w_in")
    for dst, val in zip(outs, res):
        dst["w_in"] = jnp.transpose(val).reshape(1, D, WSH)
    parts_r = [pr_own, (rr, 0), (rr, 1), (rr, 2)]
    for name, row, rows in (("w_oa", R_OA, 128), ("w_ob", R_OB, 256), ("w_out", R_OUT, 128), ("w_pg", R_PG, 128)):
        res = _adamw(w[name].reshape(rows, D), m[name].reshape(rows, D), v[name].reshape(rows, D), parts_r, "adamw_" + name,
                     part_row=row, block_rows=128)
        for dst, val in zip(outs, res):
            dst[name] = val.reshape(1, rows, D)
    res = _adamw(w["w_ple"].reshape(32, D), m["w_ple"].reshape(32, D), v["w_ple"].reshape(32, D), parts_r, "adamw_w_ple",
                 part_row=R_PLE, block_rows=32)
    for dst, val in zip(outs, res):
        dst["w_ple"] = val.reshape(1, PLE, 128)
    two_d = lambda d: {n: d[n].reshape(_SMALL_SHAPES[n]) for n in _SMALL_SHAPES}
    *res, g_cw_wide, loss = _adamw_small(two_d(w), two_d(m), two_d(v), wide_all, narrow_all, norm_all)
    for dst, val in zip(outs, res):
        dst.update({n: val[n].reshape(w[n].shape) for n in _SMALL_SHAPES})
    g_cw = lax.dynamic_slice_in_dim(g_cw_wide.reshape(4, CD), me * (CD // NDEV), CD // NDEV, axis=1).reshape(12, 128)
    res = _adamw(w["conv_w"].reshape(12, 128), m["conv_w"].reshape(12, 128), v["conv_w"].reshape(12, 128), [g_cw], "adamw_conv_w")
    for dst, val in zip((out_g, out_d, out_m, out_v), res):
        dst["conv_w"] = val.reshape(1, 4, CD // NDEV)

    return (loss, gx.reshape(1, T, D), *[out_g[n] for n in _WEIGHTS], *[out_d[n] for n in _WEIGHTS],
            *[out_m[n] for n in _WEIGHTS], *[out_v[n] for n in _WEIGHTS])
```

```python
import functools
import math
from typing import Callable, NamedTuple

import jax
import jax.numpy as jnp
from jax import lax
from jax.experimental import pallas as pl
from jax.experimental.pallas import tpu as pltpu

F32 = jnp.float32
BF16 = jnp.bfloat16
MESH = pl.DeviceIdType.MESH

D = 1024
DI = 2048
CD = 3072
NH = 32
HD = 64
NST = 128
NG = 4
CH = 128
PLE = 256
NIN = 10272
NDEV = 8
WSH = NIN // NDEV
EPS = 1e-6
OFF_XBC, OFF_U, OFF_ZB, OFF_V, OFF_ZA, OFF_GA, OFF_GB, OFF_DT = 0, 3072, 4096, 6144, 7168, 8192, 9216, 10240
NP = 10368
DTW = 128
R_OA, R_OB, R_OUT, R_PG, R_PLE, R_ROWS = 0, 128, 384, 512, 640, 672

ADAM_LR, ADAM_B1, ADAM_B2, ADAM_EPS, ADAM_WD, ADAM_STEP = 0.001, 0.9, 0.999, 1e-08, 0.01, 10

V7X_VMEM_LIMIT = 56 * 1024 * 1024


def _cp(sem=None):
    return pltpu.CompilerParams(dimension_semantics=sem, vmem_limit_bytes=V7X_VMEM_LIMIT)


def _dot(a, b, prec=None):
    return jnp.dot(a, b, preferred_element_type=F32, precision=prec)


def _dot_nt(a, b, prec=None):
    return lax.dot_general(a, b, (((1,), (1,)), ((), ())), preferred_element_type=F32, precision=prec)


def _dot_tn(a, b, prec=None):
    return lax.dot_general(a, b, (((0,), (0,)), ((), ())), preferred_element_type=F32, precision=prec)


def _sigmoid(x):
    return 1.0 / (1.0 + jnp.exp(-x))


def _gelu_and_grad(x):
    c = math.sqrt(2.0 / math.pi)
    x2 = x * x
    t = jnp.tanh(c * (x + 0.044715 * x * x2))
    g = 0.5 * x * (1.0 + t)
    dg = 0.5 * (1.0 + t) + 0.5 * x * (1.0 - t * t) * c * (1.0 + 3.0 * 0.044715 * x2)
    return g, dg


def _gelu(x):
    c = math.sqrt(2.0 / math.pi)
    return 0.5 * x * (1.0 + jnp.tanh(c * (x + 0.044715 * x * x * x)))


def _softplus(x):
    return jnp.maximum(x, 0.0) + jnp.log(1.0 + jnp.exp(-jnp.abs(x)))


def _full(shape):
    n = len(shape)
    return pl.BlockSpec(shape, lambda *_: (0,) * n)


_ANY = pl.BlockSpec(memory_space=pl.ANY)


class _Exchange(NamedTuple):
    arrays: list
    out_shape: list
    sems: list
    make: Callable


def _call(body, ex, *, name, grid, in_specs, out_specs, out_shape, scratch_shapes, args):
    ki, ko, ks = len(in_specs), len(out_specs), len(scratch_shapes)
    ei, eo = len(ex.arrays), len(ex.out_shape)
    last = [g - 1 for g in grid]

    def full_body(*refs):
        r = list(refs)
        ins, eins, r = r[:ki], r[ki:ki + ei], r[ki + ei:]
        outs, eouts, r = r[:ko], r[ko:ko + eo], r[ko + eo:]
        scr, esems = r[:ks], r[ks:]
        start, finish = ex.make(eins, eouts, esems)
        ids = [pl.program_id(a) for a in range(len(grid))]
        is_first = functools.reduce(lambda p, q: p & q, [i == 0 for i in ids])
        is_last = functools.reduce(lambda p, q: p & q, [i == l for i, l in zip(ids, last)])
        pl.when(is_first)(start)
        body(*ins, *outs, *scr)
        pl.when(is_last)(finish)

    res = pl.pallas_call(
        full_body, name=name, grid=grid, in_specs=list(in_specs) + [_ANY] * ei, out_specs=list(out_specs) + [_ANY] * eo,
        out_shape=list(out_shape) + list(ex.out_shape), scratch_shapes=list(scratch_shapes) + list(ex.sems),
        compiler_params=_cp(("arbitrary",) * len(grid)),
    )(*args, *ex.arrays)
    return res[:ko], res[ko:]


def _run_exchange(ex, name):
    ni, no = len(ex.arrays), len(ex.out_shape)

    def body(*refs):
        start, finish = ex.make(refs[:ni], refs[ni:ni + no], refs[ni + no:])
        start()
        finish()

    return pl.pallas_call(body, name=name, in_specs=[_ANY] * ni, out_specs=[_ANY] * no, out_shape=list(ex.out_shape),
                          scratch_shapes=list(ex.sems))(*ex.arrays)


def _proj_fwd(x, norm_g, wp, ex):
    T = x.shape[0]
    tm, tn = min(T, 1024), 2048
    nj = OFF_DT // tn
    assert OFF_DT % tn == 0 and OFF_DT + DTW == NP

    def body(x_ref, g_ref, w_ref, wdt_ref, proj_ref, dt_ref, h_ref, hs_ref):
        j = pl.program_id(1)

        @pl.when(j == 0)
        def _():
            xf = x_ref[...]
            r = lax.rsqrt(jnp.mean(xf * xf, axis=-1, keepdims=True) + EPS)
            h = (xf * r * g_ref[...]).astype(BF16)
            hs_ref[...] = h
            h_ref[...] = h

        proj_ref[...] = _dot_nt(hs_ref[...], w_ref[...]).astype(BF16)

        @pl.when(j == nj - 1)
        def _():
            dt_ref[...] = _dot_nt(hs_ref[...], wdt_ref[...])

    return _call(
        body, ex, name="proj_fwd", grid=(T // tm, nj),
        in_specs=[pl.BlockSpec((tm, D), lambda i, j: (i, 0)), _full((1, D)), pl.BlockSpec((tn, D), lambda i, j: (j, 0)),
                  pl.BlockSpec((DTW, D), lambda i, j: (OFF_DT // DTW, 0))],
        out_specs=[pl.BlockSpec((tm, tn), lambda i, j: (i, j)), pl.BlockSpec((tm, DTW), lambda i, j: (i, 0)),
                   pl.BlockSpec((tm, D), lambda i, j: (i, 0))],
        out_shape=[jax.ShapeDtypeStruct((T, OFF_DT), BF16), jax.ShapeDtypeStruct((T, DTW), F32), jax.ShapeDtypeStruct((T, D), BF16)],
        scratch_shapes=[pltpu.VMEM((tm, D), BF16)], args=(x, norm_g, wp, wp))


def _gmlp_tile():
    return 256


def _gmlp_fwd(proj, ln_g, ln_b, ws, bst):
    T = proj.shape[0]
    tm = min(T, _gmlp_tile())

    def body(u_ref, v_ref, z_ref, lg_ref, lb_ref, ws_ref, bs_ref, ya_ref, vn_s):
        vg = _gelu(v_ref[...].astype(F32))
        mu = jnp.mean(vg, axis=-1, keepdims=True)
        xc = vg - mu
        rstd = lax.rsqrt(jnp.mean(xc * xc, axis=-1, keepdims=True) + EPS)
        vn_s[...] = (xc * rstd * lg_ref[...] + lb_ref[...]).astype(BF16)
        for c in range(tm // CH):
            rs = slice(c * CH, (c + 1) * CH)
            for g in range(NG):
                cs_ = slice(g * 256, (g + 1) * 256)
                sv = _dot(ws_ref[g], vn_s[rs, cs_]) + bs_ref[g]
                z = z_ref[rs, cs_].astype(F32)
                ya_ref[rs, cs_] = (_gelu(u_ref[rs, cs_].astype(F32)) * sv * (z * _sigmoid(z))).astype(BF16)

    blk = lambda off: pl.BlockSpec((tm, D), lambda i: (i, off // D))
    return pl.pallas_call(
        body, name="gmlp_fwd", grid=(T // tm,),
        in_specs=[blk(OFF_U), blk(OFF_V), blk(OFF_ZA), _full((1, D)), _full((1, D)), _full((NG, CH, CH)), _full((NG, CH, 256))],
        out_specs=pl.BlockSpec((tm, D), lambda i: (i, 0)),
        out_shape=jax.ShapeDtypeStruct((T, D), BF16),
        scratch_shapes=[pltpu.VMEM((tm, D), BF16)],
        compiler_params=_cp(("parallel",)),
    )(proj, proj, proj, ln_g, ln_b, ws, bst)


def _gmlp_bwd(proj, dya, ln_g, ln_b, ws, wst, bst, gsel, ex):
    T = proj.shape[0]
    tm = min(T, _gmlp_tile())

    def body(u_ref, v_ref, z_ref, dy_ref, lg_ref, lb_ref, ws_ref, wst_ref, bs_ref, gsel_ref,
             d_ref, dws_ref, dbs_ref, dln_ref, vn_s, dsv_s, dvn_s):
        du_ref, dv_ref, dz_ref = d_ref.at[:, 0:D], d_ref.at[:, D:2 * D], d_ref.at[:, 2 * D:3 * D]
        @pl.when(pl.program_id(0) == 0)
        def _():
            dws_ref[...] = jnp.zeros_like(dws_ref)
            dbs_ref[...] = jnp.zeros_like(dbs_ref)
            dln_ref[...] = jnp.zeros_like(dln_ref)

        vg, dvg_dv = _gelu_and_grad(v_ref[...].astype(F32))
        mu = jnp.mean(vg, axis=-1, keepdims=True)
        xc = vg - mu
        rstd = lax.rsqrt(jnp.mean(xc * xc, axis=-1, keepdims=True) + EPS)
        vhat = xc * rstd
        vn_s[...] = (vhat * lg_ref[...] + lb_ref[...]).astype(BF16)
        ri = lax.broadcasted_iota(jnp.int32, (CH, CH), 0)
        ci = lax.broadcasted_iota(jnp.int32, (CH, CH), 1)
        tril = (ri >= ci).astype(F32)
        for c in range(tm // CH):
            rs = slice(c * CH, (c + 1) * CH)
            for g in range(NG):
                cs_ = slice(g * 256, (g + 1) * 256)
                vn = vn_s[rs, cs_]
                sv = _dot(ws_ref[g], vn) + bs_ref[g]
                z = z_ref[rs, cs_].astype(F32)
                sz = _sigmoid(z)
                ug, dug_du = _gelu_and_grad(u_ref[rs, cs_].astype(F32))
                dy = dy_ref[rs, cs_].astype(F32)
                t = dy * z * sz
                du_ref[rs, cs_] = (t * sv * dug_du).astype(BF16)
                dz_ref[rs, cs_] = (dy * ug * sv * sz * (1.0 + z * (1.0 - sz))).astype(BF16)
                dsv = (t * ug).astype(BF16)
                dsv_s[rs, cs_] = dsv
                dvn_s[rs, cs_] = _dot(wst_ref[g], dsv)
                dws_ref[g] += _dot_nt(dsv, vn) * tril
            dbs_ref[...] += _dot(dsv_s[rs, :], gsel_ref[...])
        dvn = dvn_s[...]
        dln_ref[0:1, :] += jnp.sum(dvn * vhat, axis=0, keepdims=True)
        dln_ref[1:2, :] += jnp.sum(dvn, axis=0, keepdims=True)
        dvh = dvn * lg_ref[...]
        dvg = rstd * (dvh - jnp.mean(dvh, axis=-1, keepdims=True) - vhat * jnp.mean(dvh * vhat, axis=-1, keepdims=True))
        dv_ref[...] = (dvg * dvg_dv).astype(BF16)

    blk = lambda off: pl.BlockSpec((tm, D), lambda i: (i, off // D))
    row = pl.BlockSpec((tm, D), lambda i: (i, 0))
    return _call(
        body, ex, name="gmlp_bwd", grid=(T // tm,),
        in_specs=[blk(OFF_U), blk(OFF_V), blk(OFF_ZA), row, _full((1, D)), _full((1, D)), _full((NG, CH, CH)),
                  _full((NG, CH, CH)), _full((NG, CH, 256)), _full((D, 128))],
        out_specs=[pl.BlockSpec((tm, 3 * D), lambda i: (i, 0)), _full((NG, CH, CH)), _full((CH, 128)), _full((8, D))],
        out_shape=[jax.ShapeDtypeStruct((T, 3 * D), BF16),
                   jax.ShapeDtypeStruct((NG, CH, CH), F32), jax.ShapeDtypeStruct((CH, 128), F32), jax.ShapeDtypeStruct((8, D), F32)],
        scratch_shapes=[pltpu.VMEM((tm, D), BF16), pltpu.VMEM((tm, D), BF16), pltpu.VMEM((tm, D), F32)],
        args=(proj, proj, proj, dya, ln_g, ln_b, ws, wst, bst, gsel))


def _shift_rows(cur, edge, j, down):
    r8 = lax.broadcasted_iota(jnp.int32, (8, 1), 0)
    if down:
        body = pltpu.roll(cur, j, 0)
        return jnp.concatenate([jnp.where(r8 >= j, body[0:8], pltpu.roll(edge, j, 0)), body[8:]], axis=0)
    body = pltpu.roll(cur, CH - j, 0)
    return jnp.concatenate([body[:CH - 8], jnp.where(r8 < 8 - j, body[CH - 8:], pltpu.roll(edge, 8 - j, 0))], axis=0)


def _conv_pre(x, x_before, cw_ref, cb_ref):
    pre = cb_ref[...] + cw_ref[3:4, :] * x
    for j in (1, 2, 3):
        pre = pre + cw_ref[3 - j:4 - j, :] * _shift_rows(x, x_before, j, down=True)
    return pre


def _split_dot(x, w, parts, w_left=False):
    acc, r = None, x
    for k in range(parts):
        hi = r.astype(BF16)
        d = _dot(w, hi) if w_left else _dot(hi, w)
        acc = d if acc is None else acc + d
        if k + 1 < parts:
            r = r - hi.astype(F32)
    return acc


def _chunk_decays(dt, alog_ref, e_ref, cs_s, cst_s, csx_s):
    a = -jnp.exp(alog_ref[...])
    ri = lax.broadcasted_iota(jnp.int32, (CH, CH), 0)
    ci = lax.broadcasted_iota(jnp.int32, (CH, CH), 1)
    tril = ri >= ci
    cs = _split_dot(dt * a, tril.astype(BF16), 3, w_left=True)
    cs_s[...] = cs
    cst_s[...] = cs.T
    csx_s[...] = _split_dot(cs, e_ref[...], 3)
    return a, tril, ri, ci


def _lmat(cst_s, h, tril):
    rowb = jnp.broadcast_to(cst_s[h:h + 1, :], (CH, CH))
    return jnp.exp(jnp.where(tril, rowb.T - rowb, -jnp.inf))


def _head_pair_rows(v, lane):
    return jnp.concatenate([jnp.where(lane < HD, v, 0.0), jnp.where(lane < HD, 0.0, v)], axis=0).astype(BF16)


def _ssd_fwd(proj, dtr, cw, cb, dtb, alog, dskx, sg, e128):
    T = proj.shape[0]
    nc = T // CH

    def body(xbc_ref, zb_ref, dt_ref, cw_ref, cb_ref, dtb_ref, alog_ref, dx_ref, sg_ref, e_ref,
             y_ref, yb_ref, hp_ref, pre_ref, xprev_s, h_s, cs_s, cst_s, csx_s, yz_s):
        @pl.when(pl.program_id(0) == 0)
        def _():
            xprev_s[...] = jnp.zeros_like(xprev_s)
            h_s[...] = jnp.zeros_like(h_s)

        x = xbc_ref[...].astype(F32)
        pre = _conv_pre(x, xprev_s[...], cw_ref, cb_ref)
        pre_ref[...] = pre
        xprev_s[...] = x[CH - 8:]
        xc = pre * _sigmoid(pre)
        dt = _softplus(dt_ref[...] + dtb_ref[...])
        a, tril, _, lane = _chunk_decays(dt, alog_ref, e_ref, cs_s, cst_s, csx_s)
        dt_x = _split_dot(dt, e_ref[...], 2)
        cs_last_x = csx_s[CH - 1:CH, :]
        hp_ref[0] = h_s[...]
        for g in range(NG):
            gs = slice(g * 512, (g + 1) * 512)
            bg = xc[:, DI + g * NST:DI + (g + 1) * NST].astype(BF16)
            cg = xc[:, DI + 512 + g * NST:DI + 512 + (g + 1) * NST].astype(BF16)
            cbm = _dot_nt(cg, bg)
            xg = xc[:, gs]
            xdt = xg * dt_x[:, gs]
            hprev = h_s[:, gs]
            csx = csx_s[:, gs]
            yoff = _dot(cg, hprev.astype(BF16)) * jnp.exp(csx)
            st = _dot_tn(bg, (xdt * jnp.exp(cs_last_x[:, gs] - csx)).astype(BF16))
            h_s[:, gs] = jnp.exp(cs_last_x[:, gs]) * hprev + st
            ssq = jnp.zeros((CH, 1), F32)
            for q in range(4):
                h0 = g * 8 + 2 * q
                ps = slice(q * 128, (q + 1) * 128)
                cols = slice(g * 512 + q * 128, g * 512 + (q + 1) * 128)
                m01 = jnp.concatenate([cbm * _lmat(cst_s, h0, tril), cbm * _lmat(cst_s, h0 + 1, tril)], axis=1).astype(BF16)
                yq = _dot(m01, _head_pair_rows(xdt[:, ps], lane)) + yoff[:, ps] + xg[:, ps] * dx_ref[:, cols]
                y_ref[:, cols] = yq
                z = zb_ref[:, cols].astype(F32)
                yz = yq * z * _sigmoid(z)
                yz_s[:, cols] = yz
                ssq = ssq + jnp.sum(yz * yz, axis=1, keepdims=True)
            rg = lax.rsqrt(ssq * (1.0 / 512.0) + EPS)
            yb_ref[:, gs] = (yz_s[:, gs] * rg * sg_ref[:, gs]).astype(BF16)

    return pl.pallas_call(
        body, name="ssd_fwd", grid=(nc,),
        in_specs=[pl.BlockSpec((CH, CD), lambda c: (c, OFF_XBC // CD)), pl.BlockSpec((CH, DI), lambda c: (c, OFF_ZB // DI)),
                  pl.BlockSpec((CH, DTW), lambda c: (c, 0)), _full((4, CD)), _full((1, CD)), _full((1, DTW)),
                  _full((1, DTW)), _full((1, DI)), _full((1, DI)), _full((DTW, DI))],
        out_specs=[pl.BlockSpec((CH, DI), lambda c: (c, 0)), pl.BlockSpec((CH, DI), lambda c: (c, 0)),
                   pl.BlockSpec((1, NST, DI), lambda c: (c, 0, 0)), pl.BlockSpec((CH, CD), lambda c: (c, 0))],
        out_shape=[jax.ShapeDtypeStruct((T, DI), F32), jax.ShapeDtypeStruct((T, DI), BF16),
                   jax.ShapeDtypeStruct((nc, NST, DI), F32), jax.ShapeDtypeStruct((T, CD), F32)],
        scratch_shapes=[pltpu.VMEM((8, CD), F32), pltpu.VMEM((NST, DI), F32), pltpu.VMEM((CH, CH), F32),
                        pltpu.VMEM((CH, CH), F32), pltpu.VMEM((CH, DI), F32), pltpu.VMEM((CH, DI), F32)],
        compiler_params=_cp(("arbitrary",)),
    )(proj, proj, dtr, cw, cb, dtb, alog, dskx, sg, e128)


def _ssd_bwd(proj, pre_all, dtr, y, dyb, hprev_all, cw, dtb, alog, dskx, sg, e128, et128, ex):
    T = proj.shape[0]
    nc = T // CH

    def body(xbc_ref, pre_ref, zb_ref, dt_ref, y_ref, dyb_ref, hp_ref, cw_ref, dtb_ref, alog_ref, dx_ref, sg_ref,
             e_ref, et_ref, d_ref, ddt_ref, dcw_ref, dsm_ref, dsg_ref,
             g_s, dpn_s, cs_s, cst_s, csx_s, dy_s, dxdt_s, dxs_s, dsd_s, dxc_s, gh_s):
        dxbc_ref, dzb_ref = d_ref.at[:, 0:CD], d_ref.at[:, CD:CD + DI]
        i = pl.program_id(0)

        @pl.when(i == 0)
        def _():
            g_s[...] = jnp.zeros_like(g_s)
            dpn_s[...] = jnp.zeros_like(dpn_s)
            dcw_ref[...] = jnp.zeros_like(dcw_ref)
            dsm_ref[...] = jnp.zeros_like(dsm_ref)
            dsg_ref[...] = jnp.zeros_like(dsg_ref)

        pre = pre_ref[...]
        sp = _sigmoid(pre)
        xc = pre * sp
        dtr = dt_ref[...] + dtb_ref[...]
        dt = _softplus(dtr)
        a, tril, ri, lane = _chunk_decays(dt, alog_ref, e_ref, cs_s, cst_s, csx_s)
        et = et_ref[...]
        dt_x = _split_dot(dt, e_ref[...], 2)
        cs_last_x = csx_s[CH - 1:CH, :]

        for g in range(NG):
            gs = slice(g * 512, (g + 1) * 512)
            z = zb_ref[:, gs].astype(F32)
            sz = _sigmoid(z)
            yv = y_ref[:, gs]
            yz = yv * z * sz
            rg = lax.rsqrt(jnp.mean(yz * yz, axis=-1, keepdims=True) + EPS)
            yn = yz * rg
            dyb = dyb_ref[:, gs].astype(F32)
            dsg_ref[0:1, gs] += jnp.sum(dyb * yn, axis=0, keepdims=True)
            dyn = dyb * sg_ref[:, gs]
            dyz = rg * (dyn - yn * jnp.mean(dyn * yn, axis=-1, keepdims=True))
            dy_s[:, gs] = dyz * z * sz
            dzb_ref[:, gs] = (dyz * yv * sz * (1.0 + z * (1.0 - sz))).astype(BF16)

        rsum = jnp.zeros((CH, DTW), F32)
        csum_t = jnp.zeros((DTW, CH), F32)
        for g in range(NG):
            gs = slice(g * 512, (g + 1) * 512)
            bg = xc[:, DI + g * NST:DI + (g + 1) * NST].astype(BF16)
            cg = xc[:, DI + 512 + g * NST:DI + 512 + (g + 1) * NST].astype(BF16)
            cbm = _dot_nt(cg, bg)
            xdt = xc[:, gs] * dt_x[:, gs]
            hprev = hp_ref[0, :, gs]
            hpb = hprev.astype(BF16)
            gn = g_s[:, gs]
            gnb = gn.astype(BF16)
            dy = dy_s[:, gs]
            csx = csx_s[:, gs]
            ecs = jnp.exp(csx)
            dec = jnp.exp(cs_last_x[:, gs] - csx)
            dye = (dy * ecs).astype(BF16)
            dc = _dot_nt(dye, hpb)
            dprev = _dot_tn(cg, dye)
            dxdt_state = dec * _dot(bg, gnb)
            db = _dot_nt((xdt * dec).astype(BF16), gnb)
            dcb = jnp.zeros((CH, CH), F32)
            for q in range(4):
                h0 = g * 8 + 2 * q
                ps = slice(q * 128, (q + 1) * 128)
                dyp = dy[:, ps]
                l0 = _lmat(cst_s, h0, tril)
                l1 = _lmat(cst_s, h0 + 1, tril)
                m0 = cbm * l0
                m1 = cbm * l1
                dm = _dot_nt(dyp.astype(BF16), _head_pair_rows(xdt[:, ps], lane))
                dm0 = dm[:, :CH]
                dm1 = dm[:, CH:]
                dcb = dcb + dm0 * l0 + dm1 * l1
                for hh, qm in ((h0, dm0 * m0), (h0 + 1, dm1 * m1)):
                    rsum = jnp.where(lane == hh, jnp.sum(qm, axis=1, keepdims=True), rsum)
                    csum_t = jnp.where(ri == hh, jnp.sum(qm, axis=0, keepdims=True), csum_t)
                mst = jnp.concatenate([m0, m1], axis=0).astype(BF16)
                d = _dot_tn(mst, _head_pair_rows(dyp, lane))
                dxdt_s[:, g * 512 + q * 128:g * 512 + (q + 1) * 128] = d + dxdt_state[:, ps]
            yoff = _dot(cg, hpb) * ecs
            dsd_s[:, gs] = xdt * dxdt_state
            dxs_s[:, gs] = dy * yoff
            dcbb = dcb.astype(BF16)
            dxc_s[:, DI + 512 + g * NST:DI + 512 + (g + 1) * NST] = dc + _dot(dcbb, bg)
            dxc_s[:, DI + g * NST:DI + (g + 1) * NST] = db + _dot_tn(dcbb, cg)
            gh_s[:, gs] = jnp.broadcast_to(jnp.sum(gn * hprev, axis=0, keepdims=True), (8, 512))
            g_s[:, gs] = dprev + jnp.exp(cs_last_x[:, gs]) * gn

        xs = xc[:, :DI]
        dy = dy_s[...]
        dxdt = dxdt_s[...]
        cs_last = cs_s[CH - 1:CH, :]
        state_e = _split_dot(dsd_s[...], et, 2)
        dcd = 0.125 * jnp.sum(_split_dot(gh_s[...], et, 2), axis=0, keepdims=True) * jnp.exp(cs_last)
        row = lax.broadcasted_iota(jnp.int32, (CH, 1), 0)
        dcs = rsum - csum_t.T + _split_dot(dxs_s[...], et, 2) - state_e
        dcs = dcs + jnp.where(row == CH - 1, jnp.sum(state_e, axis=0, keepdims=True) + dcd, 0.0)
        dda = _split_dot(dcs, (lane >= ri).astype(BF16), 3, w_left=True)
        ddt = dda * a + _dot((dxdt * xs).astype(BF16), et)
        ddtr = jnp.where(lane < NH, ddt * _sigmoid(dtr), 0.0)
        ddt_ref[...] = ddtr.astype(BF16)
        dsm_ref[0:1, :] += jnp.sum(ddtr, axis=0, keepdims=True)
        dsm_ref[1:2, :] += jnp.sum(dda * dt, axis=0, keepdims=True) * a
        dsm_ref[2:3, :] += jnp.sum(_dot((dy * xs).astype(BF16), et), axis=0, keepdims=True)
        dxc_s[:, :DI] = dxdt * dt_x + dy * dx_ref[...]

        dpre = dxc_s[...] * sp * (1.0 + pre * (1.0 - sp))
        dpn = dpn_s[...]
        x = xbc_ref[...].astype(F32)
        dcw_ref[4:5, :] += jnp.sum(dpre, axis=0, keepdims=True)
        dxbc = cw_ref[3:4, :] * dpre
        dcw_ref[3:4, :] += jnp.sum(dpre * x, axis=0, keepdims=True)
        for j in (1, 2, 3):
            ahead = _shift_rows(dpre, dpn, j, down=False)
            dcw_ref[3 - j:4 - j, :] += jnp.sum(ahead * x, axis=0, keepdims=True)
            dxbc = dxbc + cw_ref[3 - j:4 - j, :] * ahead
        dpn_s[...] = dpre[0:8]
        dxbc_ref[...] = dxbc.astype(BF16)

    rev = lambda c: nc - 1 - c
    return _call(
        body, ex, name="ssd_bwd", grid=(nc,),
        in_specs=[pl.BlockSpec((CH, CD), lambda c: (rev(c), OFF_XBC // CD)),
                  pl.BlockSpec((CH, CD), lambda c: (rev(c), 0)),
                  pl.BlockSpec((CH, DI), lambda c: (rev(c), OFF_ZB // DI)),
                  pl.BlockSpec((CH, DTW), lambda c: (rev(c), 0)),
                  pl.BlockSpec((CH, DI), lambda c: (rev(c), 0)), pl.BlockSpec((CH, DI), lambda c: (rev(c), 0)),
                  pl.BlockSpec((1, NST, DI), lambda c: (rev(c), 0, 0)),
                  _full((4, CD)), _full((1, DTW)), _full((1, DTW)), _full((1, DI)), _full((1, DI)),
                  _full((DTW, DI)), _full((DI, DTW))],
        out_specs=[pl.BlockSpec((CH, CD + DI), lambda c: (rev(c), 0)),
                   pl.BlockSpec((CH, DTW), lambda c: (rev(c), 0)), _full((8, CD)), _full((8, DTW)), _full((8, DI))],
        out_shape=[jax.ShapeDtypeStruct((T, CD + DI), BF16), jax.ShapeDtypeStruct((T, DTW), BF16),
                   jax.ShapeDtypeStruct((8, CD), F32), jax.ShapeDtypeStruct((8, DTW), F32), jax.ShapeDtypeStruct((8, DI), F32)],
        scratch_shapes=[pltpu.VMEM((NST, DI), F32), pltpu.VMEM((8, CD), F32), pltpu.VMEM((CH, CH), F32), pltpu.VMEM((CH, CH), F32),
                        pltpu.VMEM((CH, DI), F32), pltpu.VMEM((CH, DI), F32), pltpu.VMEM((CH, DI), F32), pltpu.VMEM((CH, DI), F32),
                        pltpu.VMEM((CH, DI), F32), pltpu.VMEM((CH, CD), F32), pltpu.VMEM((8, DI), F32)],
        args=(proj, pre_all, proj, dtr, y, dyb, hprev_all, cw, dtb, alog, dskx, sg, e128, et128))


def _merge_tile():
    return 256


def _merge(x, ya, yb, proj, p, tgt, w_oa, w_ob, w_out, w_pg, w_ple, ple_g, fin_g):
    T = x.shape[0]
    tm = min(T, _merge_tile())

    def body(x_ref, ya_ref, yb_ref, ga_ref, gb_ref, p_ref, t_ref, woa, wob, wout, wpg, wple, pg_ref, fg_ref,
             dx1_ref, dx1b_ref, mg_ref, hp_ref, dpre_ref, dpe_ref, doa_ref, dob_ref, dya_ref, dyb_ref, dg_ref, acc_ref):
        @pl.when(pl.program_id(0) == 0)
        def _():
            acc_ref[...] = jnp.zeros_like(acc_ref)

        oa = _dot(ya_ref[...], woa[...])
        ob = _dot(yb_ref[...], wob[...])
        sa = _sigmoid(ga_ref[...].astype(F32))
        sb = _sigmoid(gb_ref[...].astype(F32))
        mg = sa * oa + sb * ob
        mgb = mg.astype(BF16)
        mg_ref[...] = mgb
        x1 = x_ref[...] + _dot(mgb, wout[...])
        r2 = lax.rsqrt(jnp.mean(x1 * x1, axis=-1, keepdims=True) + EPS)
        xh1 = x1 * r2
        hpb = (xh1 * pg_ref[...]).astype(BF16)
        hp_ref[...] = hpb
        gate = _sigmoid(_dot(hpb, wpg[...]))
        pe = _dot(p_ref[...].astype(BF16), wple[...])
        x2 = x1 + gate * pe
        r3 = lax.rsqrt(jnp.mean(x2 * x2, axis=-1, keepdims=True) + EPS)
        xh2 = x2 * r3
        err = xh2 * fg_ref[...] - t_ref[...]
        acc_ref[2:3, :] += 0.5 * jnp.sum(jnp.mean(err * err, axis=-1, keepdims=True))
        dyo = err * (1.0 / D)
        acc_ref[0:1, :] += jnp.sum(dyo * xh2, axis=0, keepdims=True)
        dn = dyo * fg_ref[...]
        dx2 = r3 * (dn - xh2 * jnp.mean(dn * xh2, axis=-1, keepdims=True))
        dpe_ref[...] = (dx2 * gate).astype(BF16)
        dpre = (dx2 * pe * gate * (1.0 - gate)).astype(BF16)
        dpre_ref[...] = dpre
        dhp = _dot_nt(dpre, wpg[...])
        acc_ref[1:2, :] += jnp.sum(dhp * xh1, axis=0, keepdims=True)
        dhn = dhp * pg_ref[...]
        dx1 = dx2 + r2 * (dhn - xh1 * jnp.mean(dhn * xh1, axis=-1, keepdims=True))
        dx1_ref[...] = dx1
        dx1b = dx1.astype(BF16)
        dx1b_ref[...] = dx1b
        dmg = _dot_nt(dx1b, wout[...])
        doa = (dmg * sa).astype(BF16)
        dob = (dmg * sb).astype(BF16)
        doa_ref[...] = doa
        dob_ref[...] = dob
        dg_ref[:, :D] = (dmg * oa * sa * (1.0 - sa)).astype(BF16)
        dg_ref[:, D:] = (dmg * ob * sb * (1.0 - sb)).astype(BF16)
        dya_ref[...] = _dot_nt(doa, woa[...]).astype(BF16)
        dyb_ref[...] = _dot_nt(dob, wob[...]).astype(BF16)

    row = lambda w: pl.BlockSpec((tm, w), lambda i: (i, 0))
    wsp = lambda s: pl.BlockSpec(s, lambda i: (0, 0), pipeline_mode=pl.Buffered(1))
    return pl.pallas_call(
        body, name="merge", grid=(T // tm,),
        in_specs=[row(D), row(D), row(DI), pl.BlockSpec((tm, D), lambda i: (i, OFF_GA // D)),
                  pl.BlockSpec((tm, D), lambda i: (i, OFF_GB // D)), row(PLE), row(D),
                  wsp((D, D)), wsp((DI, D)), wsp((D, D)), wsp((D, D)), wsp((PLE, D)), _full((1, D)), _full((1, D))],
        out_specs=[row(D)] * 9 + [row(DI), row(2 * D), _full((8, D))],
        out_shape=[jax.ShapeDtypeStruct((T, D), F32)] + [jax.ShapeDtypeStruct((T, D), BF16)] * 8
        + [jax.ShapeDtypeStruct((T, DI), BF16), jax.ShapeDtypeStruct((T, 2 * D), BF16), jax.ShapeDtypeStruct((8, D), F32)],
        compiler_params=_cp(("arbitrary",)),
    )(x, ya, yb, proj, proj, p, tgt, w_oa, w_ob, w_out, w_pg, w_ple, ple_g, fin_g)


def _wgrad(a, b, name):
    T, K = a.shape
    N = b.shape[1]
    tt, tk, tn = min(T, 2048), min(K, 1024), min(N, 1024)
    nt = T // tt

    def body(a_ref, b_ref, o_ref, acc_s):
        t = pl.program_id(2)

        @pl.when(t == 0)
        def _():
            acc_s[...] = jnp.zeros_like(acc_s)

        acc_s[...] += _dot_tn(a_ref[...].astype(BF16), b_ref[...])

        @pl.when(t == nt - 1)
        def _():
            o_ref[...] = acc_s[...].astype(BF16)

    return pl.pallas_call(
        body, name=name, grid=(K // tk, N // tn, nt),
        in_specs=[pl.BlockSpec((tt, tk), lambda k, n, t: (t, k)), pl.BlockSpec((tt, tn), lambda k, n, t: (t, n))],
        out_specs=pl.BlockSpec((tk, tn), lambda k, n, t: (k, n)),
        out_shape=jax.ShapeDtypeStruct((K, N), BF16),
        scratch_shapes=[pltpu.VMEM((tk, tn), F32)],
        compiler_params=_cp(("parallel", "parallel", "arbitrary")),
    )(a, b)


def _proj_bwd(x, dx1, norm_g, wt, pieces, ddt, ex):
    T = x.shape[0]
    tm = min(T, 1024)
    nk = OFF_DT // D + 1
    starts = [sum(a.shape[1] for a in pieces[:n]) // D for n in range(len(pieces))]
    ranges = [(s, s + a.shape[1] // D) for s, a in zip(starts, pieces)]
    assert ranges[-1][1] == nk - 1
    npc = len(pieces)

    def body(x_ref, dx1_ref, g_ref, w_ref, wdt_ref, *rest):
        piece_refs, ddt_ref, gx_ref, dng_ref, acc_s = rest[:npc], rest[npc], rest[npc + 1], rest[npc + 2], rest[npc + 3]
        i, k = pl.program_id(0), pl.program_id(1)

        @pl.when((i == 0) & (k == 0))
        def _():
            dng_ref[...] = jnp.zeros_like(dng_ref)

        @pl.when(k == 0)
        def _():
            acc_s[...] = jnp.zeros_like(acc_s)

        for ref, (lo, hi) in zip(piece_refs, ranges):
            @pl.when((k >= lo) & (k < hi))
            def _(ref=ref):
                acc_s[...] += _dot(ref[...], w_ref[...])

        @pl.when(k == nk - 1)
        def _():
            dh = acc_s[...] + _dot(ddt_ref[...], wdt_ref[...])
            xf = x_ref[...]
            r = lax.rsqrt(jnp.mean(xf * xf, axis=-1, keepdims=True) + EPS)
            xh = xf * r
            dng_ref[0:1, :] += jnp.sum(dh * xh, axis=0, keepdims=True)
            dxn = dh * g_ref[...]
            gx_ref[...] = dx1_ref[...] + r * (dxn - xh * jnp.mean(dxn * xh, axis=-1, keepdims=True))

    def piece_spec(lo, hi):
        return pl.BlockSpec((tm, D), lambda i, k: (i, jnp.clip(k - lo, 0, hi - lo - 1)))

    row = pl.BlockSpec((tm, D), lambda i, k: (i, 0))
    return _call(
        body, ex, name="proj_bwd", grid=(T // tm, nk),
        in_specs=[row, row, _full((1, D)), pl.BlockSpec((D, D), lambda i, k: (jnp.minimum(k, nk - 2), 0)),
                  pl.BlockSpec((DTW, D), lambda i, k: (OFF_DT // DTW, 0))]
        + [piece_spec(lo, hi) for lo, hi in ranges] + [pl.BlockSpec((tm, DTW), lambda i, k: (i, 0))],
        out_specs=[row, _full((8, D))],
        out_shape=[jax.ShapeDtypeStruct((T, D), F32), jax.ShapeDtypeStruct((8, D), F32)],
        scratch_shapes=[pltpu.VMEM((tm, D), F32)],
        args=(x, dx1, norm_g, wt, wt, *pieces, ddt))


def _elementwise_tile(R, C, limit=1 << 20):
    if R * C * 4 <= limit:
        return R, C
    rows = [t for t in range(16, R, 16) if R % t == 0 and t * C * 4 <= limit]
    if rows:
        return rows[-1], C
    cols = [t for t in range(128, C, 128) if C % t == 0 and R * t * 4 <= limit]
    return R, cols[-1]


def _adam_update(w, m, v, g):
    c1 = 1.0 - ADAM_B1 ** ADAM_STEP
    c2 = 1.0 - ADAM_B2 ** ADAM_STEP
    mm = ADAM_B1 * m + (1.0 - ADAM_B1) * g
    vv = ADAM_B2 * v + (1.0 - ADAM_B2) * (g * g)
    return -ADAM_LR * ((mm / c1) / (jnp.sqrt(vv / c2) + ADAM_EPS) + ADAM_WD * w), mm, vv


def _adamw(w, m, v, parts, name, part_row=0, block_rows=None):
    R, C = w.shape
    tr, tc = _elementwise_tile(R, C) if block_rows is None else (block_rows, C)
    assert R % tr == 0 and part_row % tr == 0
    first = part_row // tr
    n = len(parts)
    wspec = pl.BlockSpec((tr, tc), lambda i, j: (i, j))
    flat = pl.BlockSpec((tr, tc), lambda i, j: (first + i, j))
    slot = lambda k: pl.BlockSpec((None, tr, tc), lambda i, j: (k, first + i, j))
    part_specs = [slot(p[1]) if isinstance(p, tuple) else flat for p in parts]
    part_arrays = [p[0] if isinstance(p, tuple) else p for p in parts]

    def body(*refs):
        w_ref, m_ref, v_ref = refs[:3]
        g_ref, d_ref, nm_ref, nv_ref = refs[3 + n:]
        g = refs[3][...].astype(F32)
        for r in refs[4:3 + n]:
            g = g + r[...].astype(F32)
        g_ref[...] = g
        d_ref[...], nm_ref[...], nv_ref[...] = _adam_update(w_ref[...], m_ref[...], v_ref[...], g)

    return pl.pallas_call(
        body, name=name, grid=(R // tr, C // tc), in_specs=[wspec] * 3 + part_specs, out_specs=[wspec] * 4,
        out_shape=[jax.ShapeDtypeStruct(w.shape, F32)] * 4, compiler_params=_cp(("parallel", "parallel")),
    )(w, m, v, *part_arrays)


_SMALL_WIDE = {"ln_a_g": 0, "ln_a_b": 1, "final_g": 8, "ple_norm_g": 9, "ssm_norm_g": 16, "conv_b": 44}
_WIDE_CONV_W, _WIDE_ROWS = 32, 56
_WIDE_LOSS = 10
_SMALL_NARROW = {"w_s": (0, NG * CH, 128), "b_s": (512, NG, 128), "dt_bias": (520, 1, NH), "a_log": (521, 1, NH),
                 "d_skip": (522, 1, NH)}
_NARROW_ROWS = 528
_SMALL_SHAPES = {"norm_g": (1, D), "ln_a_g": (1, D), "ln_a_b": (1, D), "ple_norm_g": (1, D), "final_g": (1, D),
                 "ssm_norm_g": (1, DI), "conv_b": (1, CD), "w_s": (NG * CH, CH), "b_s": (NG, CH), "dt_bias": (1, NH),
                 "a_log": (1, NH), "d_skip": (1, NH)}


def _adamw_small(w, m, v, wide_all, narrow_all, norm_all):
    names = list(_SMALL_SHAPES)
    n = len(names)

    def body(*refs):
        wr, mr, vr = refs[:n], refs[n:2 * n], refs[2 * n:3 * n]
        wide_ref, narrow_ref, norm_ref = refs[3 * n:3 * n + 3]
        outs = refs[3 * n + 3:]
        gr, dr, nmr, nvr, cw_ref, loss_ref = outs[:n], outs[n:2 * n], outs[2 * n:3 * n], outs[3 * n:4 * n], outs[4 * n], outs[4 * n + 1]

        def total(ref, rows, lanes):
            acc = ref[0, rows, lanes]
            for d in range(1, NDEV):
                acc = acc + ref[d, rows, lanes]
            return acc

        for k, name in enumerate(names):
            if name in _SMALL_WIDE or name == "norm_g":
                for part in range(_SMALL_SHAPES[name][1] // D):
                    pack, r = (norm_ref, 0) if name == "norm_g" else (wide_ref, _SMALL_WIDE[name] + part)
                    cols = slice(part * D, (part + 1) * D)
                    g = total(pack, slice(r, r + 1), slice(None))
                    gr[k][:, cols] = g
                    dr[k][:, cols], nmr[k][:, cols], nvr[k][:, cols] = _adam_update(wr[k][:, cols], mr[k][:, cols], vr[k][:, cols], g)
            else:
                r, rows, lanes = _SMALL_NARROW[name]
                g = total(narrow_ref, slice(r, r + rows), slice(0, lanes))
                gr[k][...] = g
                dr[k][...], nmr[k][...], nvr[k][...] = _adam_update(wr[k][...], mr[k][...], vr[k][...], g)
        cw_ref[...] = total(wide_ref, slice(_WIDE_CONV_W, _WIDE_CONV_W + 12), slice(None))
        loss_ref[...] = total(wide_ref, slice(_WIDE_LOSS, _WIDE_LOSS + 1), slice(None))

    shapes = [jax.ShapeDtypeStruct(_SMALL_SHAPES[k], F32) for k in names]
    specs = [_full(_SMALL_SHAPES[k]) for k in names]
    res = pl.pallas_call(
        body, name="adamw_small", grid=(1,),
        in_specs=specs * 3 + [_full(wide_all.shape), _full(narrow_all.shape), _full(norm_all.shape)],
        out_specs=specs * 4 + [_full((12, D)), _full((1, D))],
        out_shape=shapes * 4 + [jax.ShapeDtypeStruct((12, D), F32), jax.ShapeDtypeStruct((1, D), F32)],
        compiler_params=_cp(("arbitrary",)),
    )(*[w[k] for k in names], *[m[k] for k in names], *[v[k] for k in names], wide_all, narrow_all, norm_all)
    groups = [dict(zip(names, res[q * n:(q + 1) * n])) for q in range(4)]
    return groups[0], groups[1], groups[2], groups[3], res[4 * n], res[4 * n + 1][0, 0]


def _dev_index(px, py, pc):
    return 4 * px + 2 * py + pc


def _mesh_position():
    return lax.axis_index("x"), lax.axis_index("y"), lax.axis_index("c")


def _gather_exchange(blocks):
    n = len(blocks)

    def make(ins, outs, sems):
        send_sems, recv_sems, local_sems = sems
        x, y, c = _mesh_position()
        me, sibling = (x, y, c), (x, y, 1 - c)
        chips = [(1 - x, y), (x, 1 - y), (1 - x, 1 - y)]

        def copy(a, k, block, to, src=None):
            dst = outs[a].at[_dev_index(*block)]
            return pltpu.make_async_remote_copy(src_ref=dst if src is None else src, dst_ref=dst, send_sem=send_sems.at[a, k],
                                                recv_sem=recv_sems.at[a, k], device_id=to, device_id_type=MESH)

        mine = [pltpu.make_async_copy(ins[a], outs[a].at[_dev_index(*me)], local_sems.at[a]) for a in range(n)]
        first = []
        for a in range(n):
            first.append(copy(a, 0, me, sibling, src=ins[a]))
            first += [copy(a, 1 + j, me, (*chip, c), src=ins[a]) for j, chip in enumerate(chips)]

        def start():
            for cp in mine + first:
                cp.start()

        def finish():
            passed = []
            for j, chip in enumerate(chips):
                for a in range(n):
                    copy(a, 1 + j, (*chip, c), me).wait_recv()
                    fwd = copy(a, 4 + j, (*chip, c), sibling)
                    fwd.start()
                    passed.append(fwd)
            for a in range(n):
                copy(a, 0, sibling, me).wait_recv()
                for j, chip in enumerate(chips):
                    copy(a, 4 + j, (*chip, 1 - c), me).wait_recv()
            for cp in first + passed:
                cp.wait_send()
            for cp in mine:
                cp.wait()

        return start, finish

    return _Exchange(list(blocks), [jax.ShapeDtypeStruct((NDEV,) + b.shape, b.dtype) for b in blocks],
                     [pltpu.SemaphoreType.DMA((n, 7)), pltpu.SemaphoreType.DMA((n, 7)), pltpu.SemaphoreType.DMA((n,))], make)


def _relay_gather_exchange(blocks):
    n = len(blocks)

    def make(ins, outs, sems):
        send_sems, recv_sems, local_sems = sems
        x, y, c = _mesh_position()
        me, sibling = (x, y, c), (x, y, 1 - c)
        x_nbr, y_nbr, diag = (1 - x, y), (x, 1 - y), (1 - x, 1 - y)
        relay_from = (jnp.where(c == 0, x, 1 - x), jnp.where(c == 0, 1 - y, y))
        relay_to = (jnp.where(c == 0, 1 - x, x), jnp.where(c == 0, y, 1 - y))

        def copy(a, k, block, to, src=None):
            dst = outs[a].at[_dev_index(*block)]
            return pltpu.make_async_remote_copy(src_ref=dst if src is None else src, dst_ref=dst, send_sem=send_sems.at[a, k],
                                                recv_sem=recv_sems.at[a, k], device_id=to, device_id_type=MESH)

        mine = [pltpu.make_async_copy(ins[a], outs[a].at[_dev_index(*me)], local_sems.at[a]) for a in range(n)]
        first = []
        for a in range(n):
            first += [copy(a, 0, me, sibling, src=ins[a]), copy(a, 1, me, (*x_nbr, c), src=ins[a]), copy(a, 2, me, (*y_nbr, c), src=ins[a])]

        def start():
            for cp in mine + first:
                cp.start()

        def finish():
            later = []
            for a in range(n):
                copy(a, 1, (*x_nbr, c), me).wait_recv()
                copy(a, 2, (*y_nbr, c), me).wait_recv()
                later.append(copy(a, 3, (*relay_from, c), (*relay_to, c)))
                later += [copy(a, 4, (*x_nbr, c), sibling), copy(a, 5, (*y_nbr, c), sibling)]
                for cp in later[-3:]:
                    cp.start()
            for a in range(n):
                copy(a, 3, (*diag, c), me).wait_recv()
                later.append(copy(a, 6, (*diag, c), sibling))
                later[-1].start()
            for a in range(n):
                copy(a, 0, sibling, me).wait_recv()
                for k, chip in ((4, x_nbr), (5, y_nbr), (6, diag)):
                    copy(a, k, (*chip, 1 - c), me).wait_recv()
            for cp in first + later:
                cp.wait_send()
            for cp in mine:
                cp.wait()

        return start, finish

    return _Exchange(list(blocks), [jax.ShapeDtypeStruct((NDEV,) + b.shape, b.dtype) for b in blocks],
                     [pltpu.SemaphoreType.DMA((n, 7)), pltpu.SemaphoreType.DMA((n, 7)), pltpu.SemaphoreType.DMA((n,))], make)


def _combine(*exchanges):
    def make(ins, outs, sems):
        pairs = []
        for e in exchanges:
            ni, no, ns = len(e.arrays), len(e.out_shape), len(e.sems)
            pairs.append(e.make(ins[:ni], outs[:no], sems[:ns]))
            ins, outs, sems = ins[ni:], outs[no:], sems[ns:]

        def start():
            for s, _ in pairs:
                s()

        def finish():
            for _, f in pairs:
                f()

        return start, finish

    return _Exchange(sum((list(e.arrays) for e in exchanges), []), sum((list(e.out_shape) for e in exchanges), []),
                     sum((list(e.sems) for e in exchanges), []), make)


def _start_wait_all(copies, local=()):
    def start():
        for cp in list(local) + list(copies):
            cp.start()

    def finish():
        for cp in copies:
            cp.wait()
        for cp in local:
            cp.wait()

    return start, finish


def _no_exchange():
    return _Exchange([], [], [], lambda ins, outs, sems: (lambda: None, lambda: None))


def _direct_exchange(grads):
    n = len(grads)

    def make(ins, outs, sems):
        send_sems, recv_sems = sems
        x, y, c = _mesh_position()
        copies = []
        for a in range(n):
            for r in range(1, NDEV):
                peer = (x ^ ((r >> 2) & 1), y ^ ((r >> 1) & 1), c ^ (r & 1))
                copies.append(pltpu.make_async_remote_copy(
                    src_ref=ins[a].at[_dev_index(*peer)], dst_ref=outs[a].at[r - 1], send_sem=send_sems.at[a, r - 1],
                    recv_sem=recv_sems.at[a, r - 1], device_id=peer, device_id_type=MESH))
        return _start_wait_all(copies)

    return _Exchange(list(grads), [jax.ShapeDtypeStruct((NDEV - 1,) + g.shape[1:], g.dtype) for g in grads],
                     [pltpu.SemaphoreType.DMA((n, NDEV - 1)), pltpu.SemaphoreType.DMA((n, NDEV - 1))], make)


def _direct_gather_exchange(smalls):
    n = len(smalls)

    def make(ins, outs, sems):
        send_sems, recv_sems, local_sems = sems
        x, y, c = _mesh_position()
        copies, local = [], []
        for a in range(n):
            slot = outs[a].at[_dev_index(x, y, c)]
            local.append(pltpu.make_async_copy(ins[a], slot, local_sems.at[a]))
            for r in range(1, NDEV):
                peer = (x ^ ((r >> 2) & 1), y ^ ((r >> 1) & 1), c ^ (r & 1))
                copies.append(pltpu.make_async_remote_copy(src_ref=ins[a], dst_ref=slot, send_sem=send_sems.at[a, r - 1],
                                                           recv_sem=recv_sems.at[a, r - 1], device_id=peer, device_id_type=MESH))
        return _start_wait_all(copies, local=local)

    return _Exchange(list(smalls), [jax.ShapeDtypeStruct((NDEV,) + s.shape, s.dtype) for s in smalls],
                     [pltpu.SemaphoreType.DMA((n, 7)), pltpu.SemaphoreType.DMA((n, 7)), pltpu.SemaphoreType.DMA((n,))], make)


_W_IN_ROWS = {"u": (0, 1024), "v": (1024, 2048), "za": (2048, 3072), "zb": (3072, 5120), "xbc": (5120, 8192),
              "dt": (8192, 8224), "ga": (8224, 9248), "gb": (9248, 10272)}
_FWD_ORDER = ("xbc", "u", "zb", "v", "za", "ga", "gb", "dt")
_BWD_ORDER = ("xbc", "zb", "u", "v", "za", "ga", "gb", "dt")


def _w_in_t_rows(wt, order):
    assert order[-1] == "dt"
    z = jnp.zeros((NP - NIN, wt.shape[1]), wt.dtype)
    return jnp.concatenate([wt[slice(*_W_IN_ROWS[n])] for n in order] + [z], axis=0)


_WEIGHTS = ["norm_g", "w_in", "ln_a_g", "ln_a_b", "w_s", "b_s", "conv_w", "conv_b", "dt_bias", "a_log", "d_skip", "ssm_norm_g",
            "w_oa", "w_ob", "w_out", "ple_norm_g", "w_pg", "w_ple", "final_g"]


def _rows_pack(d, dtype):
    return jnp.concatenate([d["w_oa"].reshape(128, D), d["w_ob"].reshape(256, D), d["w_out"].reshape(128, D),
                            d["w_pg"].reshape(128, D), d["w_ple"].reshape(32, D)], axis=0).astype(dtype)


def kernel(x, p, norm_g, w_in, ln_a_g, ln_a_b, w_s, b_s, conv_w, conv_b, dt_bias, a_log, d_skip, ssm_norm_g, w_oa, w_ob, w_out, ple_norm_g, w_pg, w_ple, final_g, loss_target, m_norm_g, m_w_in, m_ln_a_g, m_ln_a_b, m_w_s, m_b_s, m_conv_w, m_conv_b, m_dt_bias, m_a_log, m_d_skip, m_ssm_norm_g, m_w_oa, m_w_ob, m_w_out, m_ple_norm_g, m_w_pg, m_w_ple, m_final_g, v_norm_g, v_w_in, v_ln_a_g, v_ln_a_b, v_w_s, v_b_s, v_conv_w, v_conv_b, v_dt_bias, v_a_log, v_d_skip, v_ssm_norm_g, v_w_oa, v_w_ob, v_w_out, v_ple_norm_g, v_w_pg, v_w_ple, v_final_g):
    args = dict(locals())
    w = {n: args[n] for n in _WEIGHTS}
    m = {n: args["m_" + n] for n in _WEIGHTS}
    v = {n: args["v_" + n] for n in _WEIGHTS}
    T = x.shape[1]
    xi, yi, ci = lax.axis_index("x"), lax.axis_index("y"), lax.axis_index("c")
    me = 4 * xi + 2 * yi + ci
    x2, p2, tgt = x.reshape(T, D), p.reshape(T, PLE), loss_target.reshape(T, D)

    norm_g2 = w["norm_g"].reshape(1, D)
    ws = jnp.where(jnp.tril(jnp.ones((CH, CH), bool))[None], w["w_s"].reshape(NG, CH, CH), 0.0).astype(BF16)
    wst = jnp.transpose(ws, (0, 2, 1))
    bst = jnp.broadcast_to(w["b_s"].reshape(NG, CH, 1), (NG, CH, 256))
    ln_g, ln_b = w["ln_a_g"].reshape(1, D), w["ln_a_b"].reshape(1, D)
    cb = w["conv_b"].reshape(1, CD)
    pad32 = lambda a: jnp.pad(a.reshape(1, NH), ((0, 0), (0, DTW - NH)))
    dtb, alog = pad32(w["dt_bias"]), pad32(w["a_log"])
    dskx = jnp.repeat(w["d_skip"].reshape(NH), HD).reshape(1, DI)
    sg = w["ssm_norm_g"].reshape(1, DI)
    ple_g, fin_g = w["ple_norm_g"].reshape(1, D), w["final_g"].reshape(1, D)
    e128 = (jnp.arange(DTW)[:, None] == (jnp.arange(DI)[None, :] // HD)).astype(BF16)
    et128 = e128.T
    gsel = ((jnp.arange(D)[:, None] // 256) == jnp.arange(128)[None, :]).astype(BF16)

    w_in_t = lambda a: jnp.transpose(a.reshape(D, WSH))
    (a_all,) = _run_exchange(_relay_gather_exchange([w_in_t(w["w_in"]).astype(BF16)]), "all_gather_w_in")
    w_in_full_t = a_all.reshape(NIN, D)
    wp = _w_in_t_rows(w_in_full_t, _FWD_ORDER)
    wt_bwd = _w_in_t_rows(w_in_full_t, _BWD_ORDER)
    (proj, dtr, h), (r_all, cw_all) = _proj_fwd(x2, norm_g2, wp, _gather_exchange([_rows_pack(w, BF16), w["conv_w"].reshape(4, CD // NDEV)]))
    f_oa = r_all[:, R_OA:R_OB].reshape(D, D)
    f_ob = r_all[:, R_OB:R_OUT].reshape(DI, D)
    f_out = r_all[:, R_OUT:R_PG].reshape(D, D)
    f_pg = r_all[:, R_PG:R_PLE].reshape(D, D)
    f_ple = jnp.transpose(r_all[:, R_PLE:R_ROWS].reshape(NDEV, PLE, 128), (1, 0, 2)).reshape(PLE, D)
    cw = jnp.transpose(cw_all, (1, 0, 2)).reshape(4, CD)

    ya = _gmlp_fwd(proj, ln_g, ln_b, ws, bst)
    y, yb, hprev, pre_all = _ssd_fwd(proj, dtr, cw, cb, dtb, alog, dskx, sg, e128)
    dx1, dx1b, mg, hp, dpre, dpe, doa, dob, dya, dyb, dgab, acc = _merge(x2, ya, yb, proj, p2, tgt, f_oa, f_ob, f_out, f_pg, f_ple,
                                                                         ple_g, fin_g)

    gple = jnp.transpose(_wgrad(p2, dpe, "wgrad_ple").reshape(PLE, NDEV, 128), (1, 0, 2)).reshape(NDEV, 32, D)
    gr = jnp.concatenate([_wgrad(ya, doa, "wgrad_oa").reshape(NDEV, 128, D), _wgrad(yb, dob, "wgrad_ob").reshape(NDEV, 256, D),
                          _wgrad(mg, dx1b, "wgrad_out").reshape(NDEV, 128, D), _wgrad(hp, dpre, "wgrad_pg").reshape(NDEV, 128, D),
                          gple], axis=1)
    (duvz, dws, dbs, dln), _ = _gmlp_bwd(proj, dya, ln_g, ln_b, ws, wst, bst, gsel, _no_exchange())
    (dxz, ddt, dcw, dsm, dsg), (rr,) = _ssd_bwd(proj, pre_all, dtr, y, dyb, hprev, cw, dtb, alog, dskx, sg, e128, et128, _direct_exchange([gr]))

    g_xz = _wgrad(dxz, h, "wgrad_xbc_zb")
    g_w_in_t = jnp.concatenate([_wgrad(duvz, h, "wgrad_u_v_za"), g_xz[CD:], g_xz[:CD], _wgrad(ddt, h, "wgrad_dt")[:NH],
                                _wgrad(dgab, h, "wgrad_ga_gb")], axis=0)
    ga = g_w_in_t.reshape(NDEV, WSH, D)
    wide = jnp.concatenate([dln, acc, dsg.reshape(16, D), dcw.reshape(24, D)], axis=0)
    narrow = jnp.concatenate([dws.reshape(NG * CH, CH), jnp.pad(dbs[:, :NG].T, ((0, 8 - NG), (0, 0))), dsm], axis=0)
    (gx, dng), (ra, wide_all, narrow_all) = _proj_bwd(x2, dx1, norm_g2, wt_bwd, [dxz, duvz, dgab], ddt,
                                                      _combine(_direct_exchange([ga]), _direct_gather_exchange([wide, narrow])))
    (norm_all,) = _run_exchange(_direct_gather_exchange([dng]), "all_gather_d_norm_g")
    ga_own = lax.dynamic_index_in_dim(ga, me, 0, keepdims=False)
    gr_own = lax.dynamic_index_in_dim(gr, me, 0, keepdims=False)

    out_g, out_d, out_m, out_v = {}, {}, {}, {}
    outs = (out_g, out_d, out_m, out_v)
    res = _adamw(w_in_t(w["w_in"]), w_in_t(m["w_in"]), w_in_t(v["w_in"]), [ga_own] + [(ra, k) for k in range(NDEV - 1)], "adamw_w_in")
    for dst, val in zip(outs, res):
        dst["w_in"] = jnp.transpose(val).reshape(1, D, WSH)
    parts_r = [gr_own] + [(rr, k) for k in range(NDEV - 1)]
    for name, row, rows in (("w_oa", R_OA, 128), ("w_ob", R_OB, 256), ("w_out", R_OUT, 128), ("w_pg", R_PG, 128)):
        res = _adamw(w[name].reshape(rows, D), m[name].reshape(rows, D), v[name].reshape(rows, D), parts_r, "adamw_" + name,
                     part_row=row, block_rows=128)
        for dst, val in zip(outs, res):
            dst[name] = val.reshape(1, rows, D)
    res = _adamw(w["w_ple"].reshape(32, D), m["w_ple"].reshape(32, D), v["w_ple"].reshape(32, D), parts_r, "adamw_w_ple",
                 part_row=R_PLE, block_rows=32)
    for dst, val in zip(outs, res):
        dst["w_ple"] = val.reshape(1, PLE, 128)
    two_d = lambda d: {n: d[n].reshape(_SMALL_SHAPES[n]) for n in _SMALL_SHAPES}
    *res, g_cw_wide, loss = _adamw_small(two_d(w), two_d(m), two_d(v), wide_all, narrow_all, norm_all)
    for dst, val in zip(outs, res):
        dst.update({n: val[n].reshape(w[n].shape) for n in _SMALL_SHAPES})
    g_cw = lax.dynamic_slice_in_dim(g_cw_wide.reshape(4, CD), me * (CD // NDEV), CD // NDEV, axis=1).reshape(12, 128)
    res = _adamw(w["conv_w"].reshape(12, 128), m["conv_w"].reshape(12, 128), v["conv_w"].reshape(12, 128), [g_cw], "adamw_conv_w")
    for dst, val in zip((out_g, out_d, out_m, out_v), res):
        dst["conv_w"] = val.reshape(1, 4, CD // NDEV)

    return (loss, gx.reshape(1, T, D), *[out_g[n] for n in _WEIGHTS], *[out_d[n] for n in _WEIGHTS],
            *[out_m[n] for n in _WEIGHTS], *[out_v[n] for n in _WEIGHTS])
```

```python
import functools
import math
from typing import Callable, NamedTuple

import jax
import jax.numpy as jnp
from jax import lax
from jax.experimental import pallas as pl
from jax.experimental.pallas import tpu as pltpu

F32 = jnp.float32
BF16 = jnp.bfloat16
MESH = pl.DeviceIdType.MESH

D = 1024
DI = 2048
CD = 3072
NH = 32
HD = 64
NST = 128
NG = 4
CH = 128
PLE = 256
NIN = 10272
NDEV = 8
WSH = NIN // NDEV
EPS = 1e-6
OFF_XBC, OFF_U, OFF_ZB, OFF_V, OFF_ZA, OFF_GA, OFF_GB, OFF_DT = 0, 3072, 4096, 6144, 7168, 8192, 9216, 10240
NP = 10368
DTW = 128
R_OA, R_OB, R_OUT, R_PG, R_PLE, R_ROWS = 0, 128, 384, 512, 640, 672

ADAM_LR, ADAM_B1, ADAM_B2, ADAM_EPS, ADAM_WD, ADAM_STEP = 0.001, 0.9, 0.999, 1e-08, 0.01, 10

V7X_VMEM_LIMIT = 56 * 1024 * 1024


def _cp(sem=None):
    return pltpu.CompilerParams(dimension_semantics=sem, vmem_limit_bytes=V7X_VMEM_LIMIT)


def _dot(a, b, prec=None):
    return jnp.dot(a, b, preferred_element_type=F32, precision=prec)


def _dot_nt(a, b, prec=None):
    return lax.dot_general(a, b, (((1,), (1,)), ((), ())), preferred_element_type=F32, precision=prec)


def _dot_tn(a, b, prec=None):
    return lax.dot_general(a, b, (((0,), (0,)), ((), ())), preferred_element_type=F32, precision=prec)


def _sigmoid(x):
    return 1.0 / (1.0 + jnp.exp(-x))


def _gelu_and_grad(x):
    c = math.sqrt(2.0 / math.pi)
    x2 = x * x
    t = jnp.tanh(c * (x + 0.044715 * x * x2))
    g = 0.5 * x * (1.0 + t)
    dg = 0.5 * (1.0 + t) + 0.5 * x * (1.0 - t * t) * c * (1.0 + 3.0 * 0.044715 * x2)
    return g, dg


def _gelu(x):
    c = math.sqrt(2.0 / math.pi)
    return 0.5 * x * (1.0 + jnp.tanh(c * (x + 0.044715 * x * x * x)))


def _softplus(x):
    return jnp.maximum(x, 0.0) + jnp.log(1.0 + jnp.exp(-jnp.abs(x)))


def _full(shape):
    n = len(shape)
    return pl.BlockSpec(shape, lambda *_: (0,) * n)


_ANY = pl.BlockSpec(memory_space=pl.ANY)


class _Exchange(NamedTuple):
    arrays: list
    out_shape: list
    sems: list
    make: Callable


def _call(body, ex, *, name, grid, in_specs, out_specs, out_shape, scratch_shapes, args):
    ki, ko, ks = len(in_specs), len(out_specs), len(scratch_shapes)
    ei, eo = len(ex.arrays), len(ex.out_shape)
    last = [g - 1 for g in grid]

    def full_body(*refs):
        r = list(refs)
        ins, eins, r = r[:ki], r[ki:ki + ei], r[ki + ei:]
        outs, eouts, r = r[:ko], r[ko:ko + eo], r[ko + eo:]
        scr, esems = r[:ks], r[ks:]
        start, finish = ex.make(eins, eouts, esems)
        ids = [pl.program_id(a) for a in range(len(grid))]
        is_first = functools.reduce(lambda p, q: p & q, [i == 0 for i in ids])
        is_last = functools.reduce(lambda p, q: p & q, [i == l for i, l in zip(ids, last)])
        pl.when(is_first)(start)
        body(*ins, *outs, *scr)
        pl.when(is_last)(finish)

    res = pl.pallas_call(
        full_body, name=name, grid=grid, in_specs=list(in_specs) + [_ANY] * ei, out_specs=list(out_specs) + [_ANY] * eo,
        out_shape=list(out_shape) + list(ex.out_shape), scratch_shapes=list(scratch_shapes) + list(ex.sems),
        compiler_params=_cp(("arbitrary",) * len(grid)),
    )(*args, *ex.arrays)
    return res[:ko], res[ko:]


def _run_exchange(ex, name):
    ni, no = len(ex.arrays), len(ex.out_shape)

    def body(*refs):
        start, finish = ex.make(refs[:ni], refs[ni:ni + no], refs[ni + no:])
        start()
        finish()

    return pl.pallas_call(body, name=name, in_specs=[_ANY] * ni, out_specs=[_ANY] * no, out_shape=list(ex.out_shape),
                          scratch_shapes=list(ex.sems))(*ex.arrays)


def _proj_fwd(x, norm_g, wp, ex):
    T = x.shape[0]
    tm, tn = min(T, 1024), 2048
    nj = OFF_DT // tn
    assert OFF_DT % tn == 0 and OFF_DT + DTW == NP

    def body(x_ref, g_ref, w_ref, wdt_ref, proj_ref, dt_ref, h_ref, hs_ref):
        j = pl.program_id(1)

        @pl.when(j == 0)
        def _():
            xf = x_ref[...]
            r = lax.rsqrt(jnp.mean(xf * xf, axis=-1, keepdims=True) + EPS)
            h = (xf * r * g_ref[...]).astype(BF16)
            hs_ref[...] = h
            h_ref[...] = h

        proj_ref[...] = _dot_nt(hs_ref[...], w_ref[...]).astype(BF16)

        @pl.when(j == nj - 1)
        def _():
            dt_ref[...] = _dot_nt(hs_ref[...], wdt_ref[...])

    return _call(
        body, ex, name="proj_fwd", grid=(T // tm, nj),
        in_specs=[pl.BlockSpec((tm, D), lambda i, j: (i, 0)), _full((1, D)), pl.BlockSpec((tn, D), lambda i, j: (j, 0)),
                  pl.BlockSpec((DTW, D), lambda i, j: (OFF_DT // DTW, 0))],
        out_specs=[pl.BlockSpec((tm, tn), lambda i, j: (i, j)), pl.BlockSpec((tm, DTW), lambda i, j: (i, 0)),
                   pl.BlockSpec((tm, D), lambda i, j: (i, 0))],
        out_shape=[jax.ShapeDtypeStruct((T, OFF_DT), BF16), jax.ShapeDtypeStruct((T, DTW), F32), jax.ShapeDtypeStruct((T, D), BF16)],
        scratch_shapes=[pltpu.VMEM((tm, D), BF16)], args=(x, norm_g, wp, wp))


def _gmlp_tile():
    return 256


def _gmlp_fwd_tile(u_ref, v_ref, z_ref, lg_ref, lb_ref, ws_ref, bs_ref, ya_ref, vn_s):
    tm = u_ref.shape[0]
    vg = _gelu(v_ref[...].astype(F32))
    mu = jnp.mean(vg, axis=-1, keepdims=True)
    xc = vg - mu
    rstd = lax.rsqrt(jnp.mean(xc * xc, axis=-1, keepdims=True) + EPS)
    vn_s[...] = (xc * rstd * lg_ref[...] + lb_ref[...]).astype(BF16)
    for c in range(tm // CH):
        rs = slice(c * CH, (c + 1) * CH)
        for g in range(NG):
            cs_ = slice(g * 256, (g + 1) * 256)
            sv = _dot(ws_ref[g], vn_s[rs, cs_]) + bs_ref[g]
            z = z_ref[rs, cs_].astype(F32)
            ya_ref[rs, cs_] = (_gelu(u_ref[rs, cs_].astype(F32)) * sv * (z * _sigmoid(z))).astype(BF16)


def _gmlp_bwd(proj, dya, ln_g, ln_b, ws, wst, bst, gsel, ex):
    T = proj.shape[0]
    tm = min(T, _gmlp_tile())

    def body(u_ref, v_ref, z_ref, dy_ref, lg_ref, lb_ref, ws_ref, wst_ref, bs_ref, gsel_ref,
             d_ref, dws_ref, dbs_ref, dln_ref, vn_s, dsv_s, dvn_s):
        du_ref, dv_ref, dz_ref = d_ref.at[:, 0:D], d_ref.at[:, D:2 * D], d_ref.at[:, 2 * D:3 * D]
        @pl.when(pl.program_id(0) == 0)
        def _():
            dws_ref[...] = jnp.zeros_like(dws_ref)
            dbs_ref[...] = jnp.zeros_like(dbs_ref)
            dln_ref[...] = jnp.zeros_like(dln_ref)

        vg, dvg_dv = _gelu_and_grad(v_ref[...].astype(F32))
        mu = jnp.mean(vg, axis=-1, keepdims=True)
        xc = vg - mu
        rstd = lax.rsqrt(jnp.mean(xc * xc, axis=-1, keepdims=True) + EPS)
        vhat = xc * rstd
        vn_s[...] = (vhat * lg_ref[...] + lb_ref[...]).astype(BF16)
        ri = lax.broadcasted_iota(jnp.int32, (CH, CH), 0)
        ci = lax.broadcasted_iota(jnp.int32, (CH, CH), 1)
        tril = (ri >= ci).astype(F32)
        for c in range(tm // CH):
            rs = slice(c * CH, (c + 1) * CH)
            for g in range(NG):
                cs_ = slice(g * 256, (g + 1) * 256)
                vn = vn_s[rs, cs_]
                sv = _dot(ws_ref[g], vn) + bs_ref[g]
                z = z_ref[rs, cs_].astype(F32)
                sz = _sigmoid(z)
                ug, dug_du = _gelu_and_grad(u_ref[rs, cs_].astype(F32))
                dy = dy_ref[rs, cs_].astype(F32)
                zs = z * sz
                t = dy * zs
                dsv_f = dy * sv
                du_ref[rs, cs_] = (zs * dsv_f * dug_du).astype(BF16)
                dz_ref[rs, cs_] = (dsv_f * ug * (sz + zs * (1.0 - sz))).astype(BF16)
                dsv = (t * ug).astype(BF16)
                dsv_s[rs, cs_] = dsv
                dvn_s[rs, cs_] = _dot(wst_ref[g], dsv)
                dws_ref[g] += _dot_nt(dsv, vn) * tril
            dbs_ref[...] += _dot(dsv_s[rs, :], gsel_ref[...])
        dvn = dvn_s[...]
        dln_ref[0:1, :] += jnp.sum(dvn * vhat, axis=0, keepdims=True)
        dln_ref[1:2, :] += jnp.sum(dvn, axis=0, keepdims=True)
        dvh = dvn * lg_ref[...]
        dvg = rstd * (dvh - jnp.mean(dvh, axis=-1, keepdims=True) - vhat * jnp.mean(dvh * vhat, axis=-1, keepdims=True))
        dv_ref[...] = (dvg * dvg_dv).astype(BF16)

    blk = lambda off: pl.BlockSpec((tm, D), lambda i: (i, off // D))
    row = pl.BlockSpec((tm, D), lambda i: (i, 0))
    return _call(
        body, ex, name="gmlp_bwd", grid=(T // tm,),
        in_specs=[blk(OFF_U), blk(OFF_V), blk(OFF_ZA), row, _full((1, D)), _full((1, D)), _full((NG, CH, CH)),
                  _full((NG, CH, CH)), _full((NG, CH, 256)), _full((D, 128))],
        out_specs=[pl.BlockSpec((tm, 3 * D), lambda i: (i, 0)), _full((NG, CH, CH)), _full((CH, 128)), _full((8, D))],
        out_shape=[jax.ShapeDtypeStruct((T, 3 * D), BF16),
                   jax.ShapeDtypeStruct((NG, CH, CH), F32), jax.ShapeDtypeStruct((CH, 128), F32), jax.ShapeDtypeStruct((8, D), F32)],
        scratch_shapes=[pltpu.VMEM((tm, D), BF16), pltpu.VMEM((tm, D), BF16), pltpu.VMEM((tm, D), F32)],
        args=(proj, proj, proj, dya, ln_g, ln_b, ws, wst, bst, gsel))


def _shift_rows(cur, edge, j, down):
    r8 = lax.broadcasted_iota(jnp.int32, (8, 1), 0)
    if down:
        body = pltpu.roll(cur, j, 0)
        return jnp.concatenate([jnp.where(r8 >= j, body[0:8], pltpu.roll(edge, j, 0)), body[8:]], axis=0)
    body = pltpu.roll(cur, CH - j, 0)
    return jnp.concatenate([body[:CH - 8], jnp.where(r8 < 8 - j, body[CH - 8:], pltpu.roll(edge, 8 - j, 0))], axis=0)


def _conv_pre(x, x_before, cw_ref, cb_ref):
    pre = cb_ref[...] + cw_ref[3:4, :] * x
    for j in (1, 2, 3):
        pre = pre + cw_ref[3 - j:4 - j, :] * _shift_rows(x, x_before, j, down=True)
    return pre


def _split_dot(x, w, parts, w_left=False):
    acc, r = None, x
    for k in range(parts):
        hi = r.astype(BF16)
        d = _dot(w, hi) if w_left else _dot(hi, w)
        acc = d if acc is None else acc + d
        if k + 1 < parts:
            r = r - hi.astype(F32)
    return acc


def _chunk_decays(dt, alog_ref, e_ref, cs_s, cst_s, csx_s):
    a = -jnp.exp(alog_ref[...])
    ri = lax.broadcasted_iota(jnp.int32, (CH, CH), 0)
    ci = lax.broadcasted_iota(jnp.int32, (CH, CH), 1)
    tril = ri >= ci
    cs = _split_dot(dt * a, tril.astype(BF16), 3, w_left=True)
    cs_s[...] = cs
    cst_s[...] = cs.T
    csx_s[...] = _split_dot(cs, e_ref[...], 3)
    return a, tril, ri, ci


def _lmat(cst_s, h, tril):
    rowb = jnp.broadcast_to(cst_s[h:h + 1, :], (CH, CH))
    return jnp.exp(jnp.where(tril, rowb.T - rowb, -jnp.inf))


def _head_pair_rows(v, lane):
    return jnp.concatenate([jnp.where(lane < HD, v, 0.0), jnp.where(lane < HD, 0.0, v)], axis=0).astype(BF16)


def _ssd_fwd(proj, dtr, cw, cb, dtb, alog, dskx, sg, e128):
    T = proj.shape[0]
    nc = T // CH

    def body(xbc_ref, zb_ref, dt_ref, cw_ref, cb_ref, dtb_ref, alog_ref, dx_ref, sg_ref, e_ref,
             y_ref, yb_ref, hp_ref, pre_ref, xprev_s, h_s, cs_s, cst_s, csx_s, yz_s):
        @pl.when(pl.program_id(0) == 0)
        def _():
            xprev_s[...] = jnp.zeros_like(xprev_s)
            h_s[...] = jnp.zeros_like(h_s)

        x = xbc_ref[...].astype(F32)
        pre = _conv_pre(x, xprev_s[...], cw_ref, cb_ref)
        pre_ref[...] = pre
        xprev_s[...] = x[CH - 8:]
        xc = pre * _sigmoid(pre)
        dt = _softplus(dt_ref[...] + dtb_ref[...])
        a, tril, _, lane = _chunk_decays(dt, alog_ref, e_ref, cs_s, cst_s, csx_s)
        dt_x = _split_dot(dt, e_ref[...], 2)
        cs_last_x = csx_s[CH - 1:CH, :]
        hp_ref[0] = h_s[...]
        for g in range(NG):
            gs = slice(g * 512, (g + 1) * 512)
            bg = xc[:, DI + g * NST:DI + (g + 1) * NST].astype(BF16)
            cg = xc[:, DI + 512 + g * NST:DI + 512 + (g + 1) * NST].astype(BF16)
            cbm = _dot_nt(cg, bg)
            xg = xc[:, gs]
            xdt = xg * dt_x[:, gs]
            hprev = h_s[:, gs]
            csx = csx_s[:, gs]
            yoff = _dot(cg, hprev.astype(BF16)) * jnp.exp(csx)
            st = _dot_tn(bg, (xdt * jnp.exp(cs_last_x[:, gs] - csx)).astype(BF16))
            h_s[:, gs] = jnp.exp(cs_last_x[:, gs]) * hprev + st
            ssq = jnp.zeros((CH, 1), F32)
            for q in range(4):
                h0 = g * 8 + 2 * q
                ps = slice(q * 128, (q + 1) * 128)
                cols = slice(g * 512 + q * 128, g * 512 + (q + 1) * 128)
                m01 = jnp.concatenate([cbm * _lmat(cst_s, h0, tril), cbm * _lmat(cst_s, h0 + 1, tril)], axis=1).astype(BF16)
                yq = _dot(m01, _head_pair_rows(xdt[:, ps], lane)) + yoff[:, ps] + xg[:, ps] * dx_ref[:, cols]
                y_ref[:, cols] = yq
                z = zb_ref[:, cols].astype(F32)
                yz = yq * z * _sigmoid(z)
                yz_s[:, cols] = yz
                ssq = ssq + jnp.sum(yz * yz, axis=1, keepdims=True)
            rg = lax.rsqrt(ssq * (1.0 / 512.0) + EPS)
            yb_ref[:, gs] = (yz_s[:, gs] * rg * sg_ref[:, gs]).astype(BF16)

    return pl.pallas_call(
        body, name="ssd_fwd", grid=(nc,),
        in_specs=[pl.BlockSpec((CH, CD), lambda c: (c, OFF_XBC // CD)), pl.BlockSpec((CH, DI), lambda c: (c, OFF_ZB // DI)),
                  pl.BlockSpec((CH, DTW), lambda c: (c, 0)), _full((4, CD)), _full((1, CD)), _full((1, DTW)),
                  _full((1, DTW)), _full((1, DI)), _full((1, DI)), _full((DTW, DI))],
        out_specs=[pl.BlockSpec((CH, DI), lambda c: (c, 0)), pl.BlockSpec((CH, DI), lambda c: (c, 0)),
                   pl.BlockSpec((1, NST, DI), lambda c: (c, 0, 0)), pl.BlockSpec((CH, CD), lambda c: (c, 0))],
        out_shape=[jax.ShapeDtypeStruct((T, DI), F32), jax.ShapeDtypeStruct((T, DI), BF16),
                   jax.ShapeDtypeStruct((nc, NST, DI), F32), jax.ShapeDtypeStruct((T, CD), F32)],
        scratch_shapes=[pltpu.VMEM((8, CD), F32), pltpu.VMEM((NST, DI), F32), pltpu.VMEM((CH, CH), F32),
                        pltpu.VMEM((CH, CH), F32), pltpu.VMEM((CH, DI), F32), pltpu.VMEM((CH, DI), F32)],
        compiler_params=_cp(("arbitrary",)),
    )(proj, proj, dtr, cw, cb, dtb, alog, dskx, sg, e128)


def _ssd_bwd(proj, pre_all, dtr, y, dyb, hprev_all, cw, dtb, alog, dskx, sg, e128, et128, ex):
    T = proj.shape[0]
    nc = T // CH

    def body(xbc_ref, pre_ref, zb_ref, dt_ref, y_ref, dyb_ref, hp_ref, cw_ref, dtb_ref, alog_ref, dx_ref, sg_ref,
             e_ref, et_ref, d_ref, ddt_ref, dcw_ref, dsm_ref, dsg_ref,
             g_s, dpn_s, cs_s, cst_s, csx_s, dy_s, dxdt_s, dxs_s, dsd_s, dxc_s, gh_s):
        dxbc_ref, dzb_ref = d_ref.at[:, 0:CD], d_ref.at[:, CD:CD + DI]
        i = pl.program_id(0)

        @pl.when(i == 0)
        def _():
            g_s[...] = jnp.zeros_like(g_s)
            dpn_s[...] = jnp.zeros_like(dpn_s)
            dcw_ref[...] = jnp.zeros_like(dcw_ref)
            dsm_ref[...] = jnp.zeros_like(dsm_ref)
            dsg_ref[...] = jnp.zeros_like(dsg_ref)

        pre = pre_ref[...]
        sp = _sigmoid(pre)
        xc = pre * sp
        dtr = dt_ref[...] + dtb_ref[...]
        dt = _softplus(dtr)
        a, tril, ri, lane = _chunk_decays(dt, alog_ref, e_ref, cs_s, cst_s, csx_s)
        et = et_ref[...]
        dt_x = _split_dot(dt, e_ref[...], 2)
        cs_last_x = csx_s[CH - 1:CH, :]

        for g in range(NG):
            gs = slice(g * 512, (g + 1) * 512)
            z = zb_ref[:, gs].astype(F32)
            sz = _sigmoid(z)
            yv = y_ref[:, gs]
            zs = z * sz
            yz = yv * zs
            rg = lax.rsqrt(jnp.mean(yz * yz, axis=-1, keepdims=True) + EPS)
            yn = yz * rg
            dyb = dyb_ref[:, gs].astype(F32)
            dsg_ref[0:1, gs] += jnp.sum(dyb * yn, axis=0, keepdims=True)
            dyn = dyb * sg_ref[:, gs]
            dyz = rg * (dyn - yn * jnp.mean(dyn * yn, axis=-1, keepdims=True))
            dy_s[:, gs] = dyz * zs
            dzb_ref[:, gs] = (dyz * yv * (sz + zs * (1.0 - sz))).astype(BF16)

        rsum = jnp.zeros((CH, DTW), F32)
        csum_t = jnp.zeros((DTW, CH), F32)
        for g in range(NG):
            gs = slice(g * 512, (g + 1) * 512)
            bg = xc[:, DI + g * NST:DI + (g + 1) * NST].astype(BF16)
            cg = xc[:, DI + 512 + g * NST:DI + 512 + (g + 1) * NST].astype(BF16)
            cbm = _dot_nt(cg, bg)
            xdt = xc[:, gs] * dt_x[:, gs]
            hprev = hp_ref[0, :, gs]
            hpb = hprev.astype(BF16)
            gn = g_s[:, gs]
            gnb = gn.astype(BF16)
            dy = dy_s[:, gs]
            csx = csx_s[:, gs]
            ecs = jnp.exp(csx)
            dec = jnp.exp(cs_last_x[:, gs] - csx)
            dye = (dy * ecs).astype(BF16)
            dc = _dot_nt(dye, hpb)
            dprev = _dot_tn(cg, dye)
            dxdt_state = dec * _dot(bg, gnb)
            db = _dot_nt((xdt * dec).astype(BF16), gnb)
            dcb = jnp.zeros((CH, CH), F32)
            for q in range(4):
                h0 = g * 8 + 2 * q
                ps = slice(q * 128, (q + 1) * 128)
                dyp = dy[:, ps]
                l0 = _lmat(cst_s, h0, tril)
                l1 = _lmat(cst_s, h0 + 1, tril)
                m0 = cbm * l0
                m1 = cbm * l1
                dm = _dot_nt(dyp.astype(BF16), _head_pair_rows(xdt[:, ps], lane))
                dm0 = dm[:, :CH]
                dm1 = dm[:, CH:]
                dcb = dcb + dm0 * l0 + dm1 * l1
                for hh, qm in ((h0, dm0 * m0), (h0 + 1, dm1 * m1)):
                    rsum = jnp.where(lane == hh, jnp.sum(qm, axis=1, keepdims=True), rsum)
                    csum_t = jnp.where(ri == hh, jnp.sum(qm, axis=0, keepdims=True), csum_t)
                mst = jnp.concatenate([m0, m1], axis=0).astype(BF16)
                d = _dot_tn(mst, _head_pair_rows(dyp, lane))
                dxdt_s[:, g * 512 + q * 128:g * 512 + (q + 1) * 128] = d + dxdt_state[:, ps]
            yoff = _dot(cg, hpb) * ecs
            dsd_s[:, gs] = xdt * dxdt_state
            dxs_s[:, gs] = dy * yoff
            dcbb = dcb.astype(BF16)
            dxc_s[:, DI + 512 + g * NST:DI + 512 + (g + 1) * NST] = dc + _dot(dcbb, bg)
            dxc_s[:, DI + g * NST:DI + (g + 1) * NST] = db + _dot_tn(dcbb, cg)
            gh_s[:, gs] = jnp.broadcast_to(jnp.sum(gn * hprev, axis=0, keepdims=True), (8, 512))
            g_s[:, gs] = dprev + jnp.exp(cs_last_x[:, gs]) * gn

        xs = xc[:, :DI]
        dy = dy_s[...]
        dxdt = dxdt_s[...]
        cs_last = cs_s[CH - 1:CH, :]
        state_e = _split_dot(dsd_s[...], et, 2)
        dcd = 0.125 * jnp.sum(_split_dot(gh_s[...], et, 2), axis=0, keepdims=True) * jnp.exp(cs_last)
        row = lax.broadcasted_iota(jnp.int32, (CH, 1), 0)
        dcs = rsum - csum_t.T + _split_dot(dxs_s[...], et, 2) - state_e
        dcs = dcs + jnp.where(row == CH - 1, jnp.sum(state_e, axis=0, keepdims=True) + dcd, 0.0)
        dda = _split_dot(dcs, (lane >= ri).astype(BF16), 3, w_left=True)
        ddt = dda * a + _dot((dxdt * xs).astype(BF16), et)
        ddtr = jnp.where(lane < NH, ddt * _sigmoid(dtr), 0.0)
        ddt_ref[...] = ddtr.astype(BF16)
        dsm_ref[0:1, :] += jnp.sum(ddtr, axis=0, keepdims=True)
        dsm_ref[1:2, :] += jnp.sum(dda * dt, axis=0, keepdims=True) * a
        dsm_ref[2:3, :] += jnp.sum(_dot((dy * xs).astype(BF16), et), axis=0, keepdims=True)
        dxc_s[:, :DI] = dxdt * dt_x + dy * dx_ref[...]

        dpre = dxc_s[...] * (sp + xc * (1.0 - sp))
        dpn = dpn_s[...]
        x = xbc_ref[...].astype(F32)
        dcw_ref[4:5, :] += jnp.sum(dpre, axis=0, keepdims=True)
        dxbc = cw_ref[3:4, :] * dpre
        dcw_ref[3:4, :] += jnp.sum(dpre * x, axis=0, keepdims=True)
        for j in (1, 2, 3):
            ahead = _shift_rows(dpre, dpn, j, down=False)
            dcw_ref[3 - j:4 - j, :] += jnp.sum(ahead * x, axis=0, keepdims=True)
            dxbc = dxbc + cw_ref[3 - j:4 - j, :] * ahead
        dpn_s[...] = dpre[0:8]
        dxbc_ref[...] = dxbc.astype(BF16)

    rev = lambda c: nc - 1 - c
    return _call(
        body, ex, name="ssd_bwd", grid=(nc,),
        in_specs=[pl.BlockSpec((CH, CD), lambda c: (rev(c), OFF_XBC // CD)),
                  pl.BlockSpec((CH, CD), lambda c: (rev(c), 0)),
                  pl.BlockSpec((CH, DI), lambda c: (rev(c), OFF_ZB // DI)),
                  pl.BlockSpec((CH, DTW), lambda c: (rev(c), 0)),
                  pl.BlockSpec((CH, DI), lambda c: (rev(c), 0)), pl.BlockSpec((CH, DI), lambda c: (rev(c), 0)),
                  pl.BlockSpec((1, NST, DI), lambda c: (rev(c), 0, 0)),
                  _full((4, CD)), _full((1, DTW)), _full((1, DTW)), _full((1, DI)), _full((1, DI)),
                  _full((DTW, DI)), _full((DI, DTW))],
        out_specs=[pl.BlockSpec((CH, CD + DI), lambda c: (rev(c), 0)),
                   pl.BlockSpec((CH, DTW), lambda c: (rev(c), 0)), _full((8, CD)), _full((8, DTW)), _full((8, DI))],
        out_shape=[jax.ShapeDtypeStruct((T, CD + DI), BF16), jax.ShapeDtypeStruct((T, DTW), BF16),
                   jax.ShapeDtypeStruct((8, CD), F32), jax.ShapeDtypeStruct((8, DTW), F32), jax.ShapeDtypeStruct((8, DI), F32)],
        scratch_shapes=[pltpu.VMEM((NST, DI), F32), pltpu.VMEM((8, CD), F32), pltpu.VMEM((CH, CH), F32), pltpu.VMEM((CH, CH), F32),
                        pltpu.VMEM((CH, DI), F32), pltpu.VMEM((CH, DI), F32), pltpu.VMEM((CH, DI), F32), pltpu.VMEM((CH, DI), F32),
                        pltpu.VMEM((CH, DI), F32), pltpu.VMEM((CH, CD), F32), pltpu.VMEM((8, DI), F32)],
        args=(proj, pre_all, proj, dtr, y, dyb, hprev_all, cw, dtb, alog, dskx, sg, e128, et128))


def _merge_tile():
    return 256


def _merge(x, yb, proj, p, tgt, gmlp, w_oa, w_ob, w_out, w_pg, w_ple, ple_g, fin_g):
    T = x.shape[0]
    tm = min(T, _merge_tile())

    def body(x_ref, u_ref, v_ref, za_ref, yb_ref, ga_ref, gb_ref, p_ref, t_ref, lg_ref, lb_ref, ws_ref, bs_ref,
             woa, wob, wout, wpg, wple, pg_ref, fg_ref,
             dx1_ref, dx1b_ref, ya_ref, mg_ref, hp_ref, dpre_ref, dpe_ref, doa_ref, dob_ref, dya_ref, dyb_ref, dg_ref, acc_ref, vn_s):
        @pl.when(pl.program_id(0) == 0)
        def _():
            acc_ref[...] = jnp.zeros_like(acc_ref)

        _gmlp_fwd_tile(u_ref, v_ref, za_ref, lg_ref, lb_ref, ws_ref, bs_ref, ya_ref, vn_s)
        oa = _dot(ya_ref[...], woa[...])
        ob = _dot(yb_ref[...], wob[...])
        sa = _sigmoid(ga_ref[...].astype(F32))
        sb = _sigmoid(gb_ref[...].astype(F32))
        mg = sa * oa + sb * ob
        mgb = mg.astype(BF16)
        mg_ref[...] = mgb
        x1 = x_ref[...] + _dot(mgb, wout[...])
        r2 = lax.rsqrt(jnp.mean(x1 * x1, axis=-1, keepdims=True) + EPS)
        xh1 = x1 * r2
        hpb = (xh1 * pg_ref[...]).astype(BF16)
        hp_ref[...] = hpb
        gate = _sigmoid(_dot(hpb, wpg[...]))
        pe = _dot(p_ref[...].astype(BF16), wple[...])
        x2 = x1 + gate * pe
        r3 = lax.rsqrt(jnp.mean(x2 * x2, axis=-1, keepdims=True) + EPS)
        xh2 = x2 * r3
        err = xh2 * fg_ref[...] - t_ref[...]
        acc_ref[2:3, :] += 0.5 * jnp.sum(jnp.mean(err * err, axis=-1, keepdims=True))
        dyo = err * (1.0 / D)
        acc_ref[0:1, :] += jnp.sum(dyo * xh2, axis=0, keepdims=True)
        dn = dyo * fg_ref[...]
        dx2 = r3 * (dn - xh2 * jnp.mean(dn * xh2, axis=-1, keepdims=True))
        dpe_ref[...] = (dx2 * gate).astype(BF16)
        dpre = (dx2 * pe * gate * (1.0 - gate)).astype(BF16)
        dpre_ref[...] = dpre
        dhp = _dot_nt(dpre, wpg[...])
        acc_ref[1:2, :] += jnp.sum(dhp * xh1, axis=0, keepdims=True)
        dhn = dhp * pg_ref[...]
        dx1 = dx2 + r2 * (dhn - xh1 * jnp.mean(dhn * xh1, axis=-1, keepdims=True))
        dx1_ref[...] = dx1
        dx1b = dx1.astype(BF16)
        dx1b_ref[...] = dx1b
        dmg = _dot_nt(dx1b, wout[...])
        doa = (dmg * sa).astype(BF16)
        dob = (dmg * sb).astype(BF16)
        doa_ref[...] = doa
        dob_ref[...] = dob
        dg_ref[:, :D] = (dmg * oa * sa * (1.0 - sa)).astype(BF16)
        dg_ref[:, D:] = (dmg * ob * sb * (1.0 - sb)).astype(BF16)
        dya_ref[...] = _dot_nt(doa, woa[...]).astype(BF16)
        dyb_ref[...] = _dot_nt(dob, wob[...]).astype(BF16)

    row = lambda w: pl.BlockSpec((tm, w), lambda i: (i, 0))
    blk = lambda off: pl.BlockSpec((tm, D), lambda i: (i, off // D))
    wsp = lambda s: pl.BlockSpec(s, lambda i: (0, 0), pipeline_mode=pl.Buffered(1))
    return pl.pallas_call(
        body, name="merge", grid=(T // tm,),
        in_specs=[row(D), blk(OFF_U), blk(OFF_V), blk(OFF_ZA), row(DI), blk(OFF_GA), blk(OFF_GB), row(PLE), row(D),
                  _full((1, D)), _full((1, D)), _full((NG, CH, CH)), _full((NG, CH, 256)),
                  wsp((D, D)), wsp((DI, D)), wsp((D, D)), wsp((D, D)), wsp((PLE, D)), _full((1, D)), _full((1, D))],
        out_specs=[row(D)] * 10 + [row(DI), row(2 * D), _full((8, D))],
        out_shape=[jax.ShapeDtypeStruct((T, D), F32)] + [jax.ShapeDtypeStruct((T, D), BF16)] * 9
        + [jax.ShapeDtypeStruct((T, DI), BF16), jax.ShapeDtypeStruct((T, 2 * D), BF16), jax.ShapeDtypeStruct((8, D), F32)],
        scratch_shapes=[pltpu.VMEM((tm, D), BF16)],
        compiler_params=_cp(("arbitrary",)),
    )(x, proj, proj, proj, yb, proj, proj, p, tgt, *gmlp, w_oa, w_ob, w_out, w_pg, w_ple, ple_g, fin_g)


def _wgrad(a, b, name):
    T, K = a.shape
    N = b.shape[1]
    tt, tk, tn = min(T, 2048), min(K, 1024), min(N, 1024)
    nt = T // tt

    def body(a_ref, b_ref, o_ref, acc_s):
        t = pl.program_id(2)

        @pl.when(t == 0)
        def _():
            acc_s[...] = jnp.zeros_like(acc_s)

        acc_s[...] += _dot_tn(a_ref[...].astype(BF16), b_ref[...])

        @pl.when(t == nt - 1)
        def _():
            o_ref[...] = acc_s[...].astype(BF16)

    return pl.pallas_call(
        body, name=name, grid=(K // tk, N // tn, nt),
        in_specs=[pl.BlockSpec((tt, tk), lambda k, n, t: (t, k)), pl.BlockSpec((tt, tn), lambda k, n, t: (t, n))],
        out_specs=pl.BlockSpec((tk, tn), lambda k, n, t: (k, n)),
        out_shape=jax.ShapeDtypeStruct((K, N), BF16),
        scratch_shapes=[pltpu.VMEM((tk, tn), F32)],
        compiler_params=_cp(("parallel", "parallel", "arbitrary")),
    )(a, b)


def _proj_bwd(x, dx1, norm_g, wt, pieces, ddt, ex):
    T = x.shape[0]
    tm = min(T, 1024)
    nk = OFF_DT // D + 1
    starts = [sum(a.shape[1] for a in pieces[:n]) // D for n in range(len(pieces))]
    ranges = [(s, s + a.shape[1] // D) for s, a in zip(starts, pieces)]
    assert ranges[-1][1] == nk - 1
    npc = len(pieces)

    def body(x_ref, dx1_ref, g_ref, w_ref, wdt_ref, *rest):
        piece_refs, ddt_ref, gx_ref, dng_ref, acc_s = rest[:npc], rest[npc], rest[npc + 1], rest[npc + 2], rest[npc + 3]
        i, k = pl.program_id(0), pl.program_id(1)

        @pl.when((i == 0) & (k == 0))
        def _():
            dng_ref[...] = jnp.zeros_like(dng_ref)

        @pl.when(k == 0)
        def _():
            acc_s[...] = jnp.zeros_like(acc_s)

        for ref, (lo, hi) in zip(piece_refs, ranges):
            @pl.when((k >= lo) & (k < hi))
            def _(ref=ref):
                acc_s[...] += _dot(ref[...], w_ref[...])

        @pl.when(k == nk - 1)
        def _():
            dh = acc_s[...] + _dot(ddt_ref[...], wdt_ref[...])
            xf = x_ref[...]
            r = lax.rsqrt(jnp.mean(xf * xf, axis=-1, keepdims=True) + EPS)
            xh = xf * r
            dng_ref[0:1, :] += jnp.sum(dh * xh, axis=0, keepdims=True)
            dxn = dh * g_ref[...]
            gx_ref[...] = dx1_ref[...] + r * (dxn - xh * jnp.mean(dxn * xh, axis=-1, keepdims=True))

    def piece_spec(lo, hi):
        return pl.BlockSpec((tm, D), lambda i, k: (i, jnp.clip(k - lo, 0, hi - lo - 1)))

    row = pl.BlockSpec((tm, D), lambda i, k: (i, 0))
    return _call(
        body, ex, name="proj_bwd", grid=(T // tm, nk),
        in_specs=[row, row, _full((1, D)), pl.BlockSpec((D, D), lambda i, k: (jnp.minimum(k, nk - 2), 0)),
                  pl.BlockSpec((DTW, D), lambda i, k: (OFF_DT // DTW, 0))]
        + [piece_spec(lo, hi) for lo, hi in ranges] + [pl.BlockSpec((tm, DTW), lambda i, k: (i, 0))],
        out_specs=[row, _full((8, D))],
        out_shape=[jax.ShapeDtypeStruct((T, D), F32), jax.ShapeDtypeStruct((8, D), F32)],
        scratch_shapes=[pltpu.VMEM((tm, D), F32)],
        args=(x, dx1, norm_g, wt, wt, *pieces, ddt))


def _elementwise_tile(R, C, limit=1 << 20):
    if R * C * 4 <= limit:
        return R, C
    rows = [t for t in range(16, R, 16) if R % t == 0 and t * C * 4 <= limit]
    if rows:
        return rows[-1], C
    cols = [t for t in range(128, C, 128) if C % t == 0 and R * t * 4 <= limit]
    return R, cols[-1]


def _adam_update(w, m, v, g):
    c1 = 1.0 - ADAM_B1 ** ADAM_STEP
    c2 = 1.0 - ADAM_B2 ** ADAM_STEP
    mm = ADAM_B1 * m + (1.0 - ADAM_B1) * g
    vv = ADAM_B2 * v + (1.0 - ADAM_B2) * (g * g)
    return -ADAM_LR * ((mm / c1) / (jnp.sqrt(vv / c2) + ADAM_EPS) + ADAM_WD * w), mm, vv


def _adamw(w, m, v, parts, name, part_row=0, block_rows=None):
    R, C = w.shape
    tr, tc = _elementwise_tile(R, C) if block_rows is None else (block_rows, C)
    assert R % tr == 0 and part_row % tr == 0
    first = part_row // tr
    n = len(parts)
    wspec = pl.BlockSpec((tr, tc), lambda i, j: (i, j))
    flat = pl.BlockSpec((tr, tc), lambda i, j: (first + i, j))
    slot = lambda k: pl.BlockSpec((None, tr, tc), lambda i, j: (k, first + i, j))
    part_specs = [slot(p[1]) if isinstance(p, tuple) else flat for p in parts]
    part_arrays = [p[0] if isinstance(p, tuple) else p for p in parts]

    def body(*refs):
        w_ref, m_ref, v_ref = refs[:3]
        g_ref, d_ref, nm_ref, nv_ref = refs[3 + n:]
        g = refs[3][...].astype(F32)
        for r in refs[4:3 + n]:
            g = g + r[...].astype(F32)
        g_ref[...] = g
        d_ref[...], nm_ref[...], nv_ref[...] = _adam_update(w_ref[...], m_ref[...], v_ref[...], g)

    return pl.pallas_call(
        body, name=name, grid=(R // tr, C // tc), in_specs=[wspec] * 3 + part_specs, out_specs=[wspec] * 4,
        out_shape=[jax.ShapeDtypeStruct(w.shape, F32)] * 4, compiler_params=_cp(("parallel", "parallel")),
    )(w, m, v, *part_arrays)


_SMALL_WIDE = {"ln_a_g": 0, "ln_a_b": 1, "final_g": 8, "ple_norm_g": 9, "ssm_norm_g": 16, "conv_b": 44}
_WIDE_CONV_W, _WIDE_ROWS = 32, 56
_WIDE_LOSS = 10
_SMALL_NARROW = {"w_s": (0, NG * CH, 128), "b_s": (512, NG, 128), "dt_bias": (520, 1, NH), "a_log": (521, 1, NH),
                 "d_skip": (522, 1, NH)}
_NARROW_ROWS = 528
_SMALL_SHAPES = {"norm_g": (1, D), "ln_a_g": (1, D), "ln_a_b": (1, D), "ple_norm_g": (1, D), "final_g": (1, D),
                 "ssm_norm_g": (1, DI), "conv_b": (1, CD), "w_s": (NG * CH, CH), "b_s": (NG, CH), "dt_bias": (1, NH),
                 "a_log": (1, NH), "d_skip": (1, NH)}


def _adamw_small(w, m, v, wide_all, narrow_all, norm_all):
    names = list(_SMALL_SHAPES)
    n = len(names)

    def body(*refs):
        wr, mr, vr = refs[:n], refs[n:2 * n], refs[2 * n:3 * n]
        wide_ref, narrow_ref, norm_ref = refs[3 * n:3 * n + 3]
        outs = refs[3 * n + 3:]
        gr, dr, nmr, nvr, cw_ref, loss_ref = outs[:n], outs[n:2 * n], outs[2 * n:3 * n], outs[3 * n:4 * n], outs[4 * n], outs[4 * n + 1]

        def total(ref, rows, lanes):
            acc = ref[0, rows, lanes]
            for d in range(1, NDEV):
                acc = acc + ref[d, rows, lanes]
            return acc

        for k, name in enumerate(names):
            if name in _SMALL_WIDE or name == "norm_g":
                for part in range(_SMALL_SHAPES[name][1] // D):
                    pack, r = (norm_ref, 0) if name == "norm_g" else (wide_ref, _SMALL_WIDE[name] + part)
                    cols = slice(part * D, (part + 1) * D)
                    g = total(pack, slice(r, r + 1), slice(None))
                    gr[k][:, cols] = g
                    dr[k][:, cols], nmr[k][:, cols], nvr[k][:, cols] = _adam_update(wr[k][:, cols], mr[k][:, cols], vr[k][:, cols], g)
            else:
                r, rows, lanes = _SMALL_NARROW[name]
                g = total(narrow_ref, slice(r, r + rows), slice(0, lanes))
                gr[k][...] = g
                dr[k][...], nmr[k][...], nvr[k][...] = _adam_update(wr[k][...], mr[k][...], vr[k][...], g)
        cw_ref[...] = total(wide_ref, slice(_WIDE_CONV_W, _WIDE_CONV_W + 12), slice(None))
        loss_ref[...] = total(wide_ref, slice(_WIDE_LOSS, _WIDE_LOSS + 1), slice(None))

    shapes = [jax.ShapeDtypeStruct(_SMALL_SHAPES[k], F32) for k in names]
    specs = [_full(_SMALL_SHAPES[k]) for k in names]
    res = pl.pallas_call(
        body, name="adamw_small", grid=(1,),
        in_specs=specs * 3 + [_full(wide_all.shape), _full(narrow_all.shape), _full(norm_all.shape)],
        out_specs=specs * 4 + [_full((12, D)), _full((1, D))],
        out_shape=shapes * 4 + [jax.ShapeDtypeStruct((12, D), F32), jax.ShapeDtypeStruct((1, D), F32)],
        compiler_params=_cp(("arbitrary",)),
    )(*[w[k] for k in names], *[m[k] for k in names], *[v[k] for k in names], wide_all, narrow_all, norm_all)
    groups = [dict(zip(names, res[q * n:(q + 1) * n])) for q in range(4)]
    return groups[0], groups[1], groups[2], groups[3], res[4 * n], res[4 * n + 1][0, 0]


def _dev_index(px, py, pc):
    return 4 * px + 2 * py + pc


def _mesh_position():
    return lax.axis_index("x"), lax.axis_index("y"), lax.axis_index("c")


def _gather_exchange(blocks):
    n = len(blocks)

    def make(ins, outs, sems):
        send_sems, recv_sems, local_sems = sems
        x, y, c = _mesh_position()
        me, sibling = (x, y, c), (x, y, 1 - c)
        chips = [(1 - x, y), (x, 1 - y), (1 - x, 1 - y)]

        def copy(a, k, block, to, src=None):
            dst = outs[a].at[_dev_index(*block)]
            return pltpu.make_async_remote_copy(src_ref=dst if src is None else src, dst_ref=dst, send_sem=send_sems.at[a, k],
                                                recv_sem=recv_sems.at[a, k], device_id=to, device_id_type=MESH)

        mine = [pltpu.make_async_copy(ins[a], outs[a].at[_dev_index(*me)], local_sems.at[a]) for a in range(n)]
        first = []
        for a in range(n):
            first.append(copy(a, 0, me, sibling, src=ins[a]))
            first += [copy(a, 1 + j, me, (*chip, c), src=ins[a]) for j, chip in enumerate(chips)]

        def start():
            for cp in mine + first:
                cp.start()

        def finish():
            passed = []
            for j, chip in enumerate(chips):
                for a in range(n):
                    copy(a, 1 + j, (*chip, c), me).wait_recv()
                    fwd = copy(a, 4 + j, (*chip, c), sibling)
                    fwd.start()
                    passed.append(fwd)
            for a in range(n):
                copy(a, 0, sibling, me).wait_recv()
                for j, chip in enumerate(chips):
                    copy(a, 4 + j, (*chip, 1 - c), me).wait_recv()
            for cp in first + passed:
                cp.wait_send()
            for cp in mine:
                cp.wait()

        return start, finish

    return _Exchange(list(blocks), [jax.ShapeDtypeStruct((NDEV,) + b.shape, b.dtype) for b in blocks],
                     [pltpu.SemaphoreType.DMA((n, 7)), pltpu.SemaphoreType.DMA((n, 7)), pltpu.SemaphoreType.DMA((n,))], make)


def _relay_gather_exchange(blocks):
    n = len(blocks)

    def make(ins, outs, sems):
        send_sems, recv_sems, local_sems = sems
        x, y, c = _mesh_position()
        me, sibling = (x, y, c), (x, y, 1 - c)
        x_nbr, y_nbr, diag = (1 - x, y), (x, 1 - y), (1 - x, 1 - y)
        relay_from = (jnp.where(c == 0, x, 1 - x), jnp.where(c == 0, 1 - y, y))
        relay_to = (jnp.where(c == 0, 1 - x, x), jnp.where(c == 0, y, 1 - y))

        def copy(a, k, block, to, src=None):
            dst = outs[a].at[_dev_index(*block)]
            return pltpu.make_async_remote_copy(src_ref=dst if src is None else src, dst_ref=dst, send_sem=send_sems.at[a, k],
                                                recv_sem=recv_sems.at[a, k], device_id=to, device_id_type=MESH)

        mine = [pltpu.make_async_copy(ins[a], outs[a].at[_dev_index(*me)], local_sems.at[a]) for a in range(n)]
        first = []
        for a in range(n):
            first += [copy(a, 0, me, sibling, src=ins[a]), copy(a, 1, me, (*x_nbr, c), src=ins[a]), copy(a, 2, me, (*y_nbr, c), src=ins[a])]

        def start():
            for cp in mine + first:
                cp.start()

        def finish():
            later = []
            for a in range(n):
                copy(a, 1, (*x_nbr, c), me).wait_recv()
                copy(a, 2, (*y_nbr, c), me).wait_recv()
                later.append(copy(a, 3, (*relay_from, c), (*relay_to, c)))
                later += [copy(a, 4, (*x_nbr, c), sibling), copy(a, 5, (*y_nbr, c), sibling)]
                for cp in later[-3:]:
                    cp.start()
            for a in range(n):
                copy(a, 3, (*diag, c), me).wait_recv()
                later.append(copy(a, 6, (*diag, c), sibling))
                later[-1].start()
            for a in range(n):
                copy(a, 0, sibling, me).wait_recv()
                for k, chip in ((4, x_nbr), (5, y_nbr), (6, diag)):
                    copy(a, k, (*chip, 1 - c), me).wait_recv()
            for cp in first + later:
                cp.wait_send()
            for cp in mine:
                cp.wait()

        return start, finish

    return _Exchange(list(blocks), [jax.ShapeDtypeStruct((NDEV,) + b.shape, b.dtype) for b in blocks],
                     [pltpu.SemaphoreType.DMA((n, 7)), pltpu.SemaphoreType.DMA((n, 7)), pltpu.SemaphoreType.DMA((n,))], make)


def _combine(*exchanges):
    def make(ins, outs, sems):
        pairs = []
        for e in exchanges:
            ni, no, ns = len(e.arrays), len(e.out_shape), len(e.sems)
            pairs.append(e.make(ins[:ni], outs[:no], sems[:ns]))
            ins, outs, sems = ins[ni:], outs[no:], sems[ns:]

        def start():
            for s, _ in pairs:
                s()

        def finish():
            for _, f in pairs:
                f()

        return start, finish

    return _Exchange(sum((list(e.arrays) for e in exchanges), []), sum((list(e.out_shape) for e in exchanges), []),
                     sum((list(e.sems) for e in exchanges), []), make)


def _start_wait_all(copies, local=()):
    def start():
        for cp in list(local) + list(copies):
            cp.start()

    def finish():
        for cp in copies:
            cp.wait()
        for cp in local:
            cp.wait()

    return start, finish


def _no_exchange():
    return _Exchange([], [], [], lambda ins, outs, sems: (lambda: None, lambda: None))


def _direct_exchange(grads):
    n = len(grads)

    def make(ins, outs, sems):
        send_sems, recv_sems = sems
        x, y, c = _mesh_position()
        copies = []
        for a in range(n):
            for r in range(1, NDEV):
                peer = (x ^ ((r >> 2) & 1), y ^ ((r >> 1) & 1), c ^ (r & 1))
                copies.append(pltpu.make_async_remote_copy(
                    src_ref=ins[a].at[_dev_index(*peer)], dst_ref=outs[a].at[r - 1], send_sem=send_sems.at[a, r - 1],
                    recv_sem=recv_sems.at[a, r - 1], device_id=peer, device_id_type=MESH))
        return _start_wait_all(copies)

    return _Exchange(list(grads), [jax.ShapeDtypeStruct((NDEV - 1,) + g.shape[1:], g.dtype) for g in grads],
                     [pltpu.SemaphoreType.DMA((n, NDEV - 1)), pltpu.SemaphoreType.DMA((n, NDEV - 1))], make)


def _direct_gather_exchange(smalls):
    n = len(smalls)

    def make(ins, outs, sems):
        send_sems, recv_sems, local_sems = sems
        x, y, c = _mesh_position()
        copies, local = [], []
        for a in range(n):
            slot = outs[a].at[_dev_index(x, y, c)]
            local.append(pltpu.make_async_copy(ins[a], slot, local_sems.at[a]))
            for r in range(1, NDEV):
                peer = (x ^ ((r >> 2) & 1), y ^ ((r >> 1) & 1), c ^ (r & 1))
                copies.append(pltpu.make_async_remote_copy(src_ref=ins[a], dst_ref=slot, send_sem=send_sems.at[a, r - 1],
                                                           recv_sem=recv_sems.at[a, r - 1], device_id=peer, device_id_type=MESH))
        return _start_wait_all(copies, local=local)

    return _Exchange(list(smalls), [jax.ShapeDtypeStruct((NDEV,) + s.shape, s.dtype) for s in smalls],
                     [pltpu.SemaphoreType.DMA((n, 7)), pltpu.SemaphoreType.DMA((n, 7)), pltpu.SemaphoreType.DMA((n,))], make)


_W_IN_ROWS = {"u": (0, 1024), "v": (1024, 2048), "za": (2048, 3072), "zb": (3072, 5120), "xbc": (5120, 8192),
              "dt": (8192, 8224), "ga": (8224, 9248), "gb": (9248, 10272)}
_FWD_ORDER = ("xbc", "u", "zb", "v", "za", "ga", "gb", "dt")
_BWD_ORDER = ("xbc", "zb", "u", "v", "za", "ga", "gb", "dt")


def _w_in_t_rows(wt, order):
    assert order[-1] == "dt"
    z = jnp.zeros((NP - NIN, wt.shape[1]), wt.dtype)
    return jnp.concatenate([wt[slice(*_W_IN_ROWS[n])] for n in order] + [z], axis=0)


_WEIGHTS = ["norm_g", "w_in", "ln_a_g", "ln_a_b", "w_s", "b_s", "conv_w", "conv_b", "dt_bias", "a_log", "d_skip", "ssm_norm_g",
            "w_oa", "w_ob", "w_out", "ple_norm_g", "w_pg", "w_ple", "final_g"]


def _rows_pack(d, dtype):
    return jnp.concatenate([d["w_oa"].reshape(128, D), d["w_ob"].reshape(256, D), d["w_out"].reshape(128, D),
                            d["w_pg"].reshape(128, D), d["w_ple"].reshape(32, D)], axis=0).astype(dtype)


def kernel(x, p, norm_g, w_in, ln_a_g, ln_a_b, w_s, b_s, conv_w, conv_b, dt_bias, a_log, d_skip, ssm_norm_g, w_oa, w_ob, w_out, ple_norm_g, w_pg, w_ple, final_g, loss_target, m_norm_g, m_w_in, m_ln_a_g, m_ln_a_b, m_w_s, m_b_s, m_conv_w, m_conv_b, m_dt_bias, m_a_log, m_d_skip, m_ssm_norm_g, m_w_oa, m_w_ob, m_w_out, m_ple_norm_g, m_w_pg, m_w_ple, m_final_g, v_norm_g, v_w_in, v_ln_a_g, v_ln_a_b, v_w_s, v_b_s, v_conv_w, v_conv_b, v_dt_bias, v_a_log, v_d_skip, v_ssm_norm_g, v_w_oa, v_w_ob, v_w_out, v_ple_norm_g, v_w_pg, v_w_ple, v_final_g):
    args = dict(locals())
    w = {n: args[n] for n in _WEIGHTS}
    m = {n: args["m_" + n] for n in _WEIGHTS}
    v = {n: args["v_" + n] for n in _WEIGHTS}
    T = x.shape[1]
    xi, yi, ci = lax.axis_index("x"), lax.axis_index("y"), lax.axis_index("c")
    me = 4 * xi + 2 * yi + ci
    x2, p2, tgt = x.reshape(T, D), p.reshape(T, PLE), loss_target.reshape(T, D)

    norm_g2 = w["norm_g"].reshape(1, D)
    ws = jnp.where(jnp.tril(jnp.ones((CH, CH), bool))[None], w["w_s"].reshape(NG, CH, CH), 0.0).astype(BF16)
    wst = jnp.transpose(ws, (0, 2, 1))
    bst = jnp.broadcast_to(w["b_s"].reshape(NG, CH, 1), (NG, CH, 256))
    ln_g, ln_b = w["ln_a_g"].reshape(1, D), w["ln_a_b"].reshape(1, D)
    cb = w["conv_b"].reshape(1, CD)
    pad32 = lambda a: jnp.pad(a.reshape(1, NH), ((0, 0), (0, DTW - NH)))
    dtb, alog = pad32(w["dt_bias"]), pad32(w["a_log"])
    dskx = jnp.repeat(w["d_skip"].reshape(NH), HD).reshape(1, DI)
    sg = w["ssm_norm_g"].reshape(1, DI)
    ple_g, fin_g = w["ple_norm_g"].reshape(1, D), w["final_g"].reshape(1, D)
    e128 = (jnp.arange(DTW)[:, None] == (jnp.arange(DI)[None, :] // HD)).astype(BF16)
    et128 = e128.T
    gsel = ((jnp.arange(D)[:, None] // 256) == jnp.arange(128)[None, :]).astype(BF16)

    w_in_t = lambda a: jnp.transpose(a.reshape(D, WSH))
    (a_all,) = _run_exchange(_relay_gather_exchange([w_in_t(w["w_in"]).astype(BF16)]), "all_gather_w_in")
    w_in_full_t = a_all.reshape(NIN, D)
    wp = _w_in_t_rows(w_in_full_t, _FWD_ORDER)
    wt_bwd = _w_in_t_rows(w_in_full_t, _BWD_ORDER)
    (proj, dtr, h), (r_all, cw_all) = _proj_fwd(x2, norm_g2, wp, _gather_exchange([_rows_pack(w, BF16), w["conv_w"].reshape(4, CD // NDEV)]))
    f_oa = r_all[:, R_OA:R_OB].reshape(D, D)
    f_ob = r_all[:, R_OB:R_OUT].reshape(DI, D)
    f_out = r_all[:, R_OUT:R_PG].reshape(D, D)
    f_pg = r_all[:, R_PG:R_PLE].reshape(D, D)
    f_ple = jnp.transpose(r_all[:, R_PLE:R_ROWS].reshape(NDEV, PLE, 128), (1, 0, 2)).reshape(PLE, D)
    cw = jnp.transpose(cw_all, (1, 0, 2)).reshape(4, CD)

    y, yb, hprev, pre_all = _ssd_fwd(proj, dtr, cw, cb, dtb, alog, dskx, sg, e128)
    dx1, dx1b, ya, mg, hp, dpre, dpe, doa, dob, dya, dyb, dgab, acc = _merge(
        x2, yb, proj, p2, tgt, (ln_g, ln_b, ws, bst), f_oa, f_ob, f_out, f_pg, f_ple, ple_g, fin_g)

    gple = jnp.transpose(_wgrad(p2, dpe, "wgrad_ple").reshape(PLE, NDEV, 128), (1, 0, 2)).reshape(NDEV, 32, D)
    gr = jnp.concatenate([_wgrad(ya, doa, "wgrad_oa").reshape(NDEV, 128, D), _wgrad(yb, dob, "wgrad_ob").reshape(NDEV, 256, D),
                          _wgrad(mg, dx1b, "wgrad_out").reshape(NDEV, 128, D), _wgrad(hp, dpre, "wgrad_pg").reshape(NDEV, 128, D),
                          gple], axis=1)
    (duvz, dws, dbs, dln), _ = _gmlp_bwd(proj, dya, ln_g, ln_b, ws, wst, bst, gsel, _no_exchange())
    (dxz, ddt, dcw, dsm, dsg), (rr,) = _ssd_bwd(proj, pre_all, dtr, y, dyb, hprev, cw, dtb, alog, dskx, sg, e128, et128, _direct_exchange([gr]))

    g_xz = _wgrad(dxz, h, "wgrad_xbc_zb")
    g_w_in_t = jnp.concatenate([_wgrad(duvz, h, "wgrad_u_v_za"), g_xz[CD:], g_xz[:CD], _wgrad(ddt, h, "wgrad_dt")[:NH],
                                _wgrad(dgab, h, "wgrad_ga_gb")], axis=0)
    ga = g_w_in_t.reshape(NDEV, WSH, D)
    wide = jnp.concatenate([dln, acc, dsg.reshape(16, D), dcw.reshape(24, D)], axis=0)
    narrow = jnp.concatenate([dws.reshape(NG * CH, CH), jnp.pad(dbs[:, :NG].T, ((0, 8 - NG), (0, 0))), dsm], axis=0)
    (gx, dng), (ra, wide_all, narrow_all) = _proj_bwd(x2, dx1, norm_g2, wt_bwd, [dxz, duvz, dgab], ddt,
                                                      _combine(_direct_exchange([ga]), _direct_gather_exchange([wide, narrow])))
    (norm_all,) = _run_exchange(_direct_gather_exchange([dng]), "all_gather_d_norm_g")
    ga_own = lax.dynamic_index_in_dim(ga, me, 0, keepdims=False)
    gr_own = lax.dynamic_index_in_dim(gr, me, 0, keepdims=False)

    out_g, out_d, out_m, out_v = {}, {}, {}, {}
    outs = (out_g, out_d, out_m, out_v)
    res = _adamw(w_in_t(w["w_in"]), w_in_t(m["w_in"]), w_in_t(v["w_in"]), [ga_own] + [(ra, k) for k in range(NDEV - 1)], "adamw_w_in")
    for dst, val in zip(outs, res):
        dst["w_in"] = jnp.transpose(val).reshape(1, D, WSH)
    parts_r = [gr_own] + [(rr, k) for k in range(NDEV - 1)]
    for name, row, rows in (("w_oa", R_OA, 128), ("w_ob", R_OB, 256), ("w_out", R_OUT, 128), ("w_pg", R_PG, 128)):
        res = _adamw(w[name].reshape(rows, D), m[name].reshape(rows, D), v[name].reshape(rows, D), parts_r, "adamw_" + name,
                     part_row=row, block_rows=128)
        for dst, val in zip(outs, res):
            dst[name] = val.reshape(1, rows, D)
    res = _adamw(w["w_ple"].reshape(32, D), m["w_ple"].reshape(32, D), v["w_ple"].reshape(32, D), parts_r, "adamw_w_ple",
                 part_row=R_PLE, block_rows=32)
    for dst, val in zip(outs, res):
        dst["w_ple"] = val.reshape(1, PLE, 128)
    two_d = lambda d: {n: d[n].reshape(_SMALL_SHAPES[n]) for n in _SMALL_SHAPES}
    *res, g_cw_wide, loss = _adamw_small(two_d(w), two_d(m), two_d(v), wide_all, narrow_all, norm_all)
    for dst, val in zip(outs, res):
        dst.update({n: val[n].reshape(w[n].shape) for n in _SMALL_SHAPES})
    g_cw = lax.dynamic_slice_in_dim(g_cw_wide.reshape(4, CD), me * (CD // NDEV), CD // NDEV, axis=1).reshape(12, 128)
    res = _adamw(w["conv_w"].reshape(12, 128), m["conv_w"].reshape(12, 128), v["conv_w"].reshape(12, 128), [g_cw], "adamw_conv_w")
    for dst, val in zip((out_g, out_d, out_m, out_v), res):
        dst["conv_w"] = val.reshape(1, 4, CD // NDEV)

    return (loss, gx.reshape(1, T, D), *[out_g[n] for n in _WEIGHTS], *[out_d[n] for n in _WEIGHTS],
            *[out_m[n] for n in _WEIGHTS], *[out_v[n] for n in _WEIGHTS])
```

```python
import functools
import math
from typing import Callable, NamedTuple

import jax
import jax.numpy as jnp
from jax import lax
from jax.experimental import pallas as pl
from jax.experimental.pallas import tpu as pltpu

F32 = jnp.float32
BF16 = jnp.bfloat16
MESH = pl.DeviceIdType.MESH

D = 1024
DI = 2048
CD = 3072
NH = 32
HD = 64
NST = 128
NG = 4
CH = 128
PLE = 256
NIN = 10272
NDEV = 8
WSH = NIN // NDEV
EPS = 1e-6
OFF_XBC, OFF_U, OFF_ZB, OFF_V, OFF_ZA, OFF_GA, OFF_GB, OFF_DT = 0, 3072, 4096, 6144, 7168, 8192, 9216, 10240
NP = 10368
DTW = 128
R_OA, R_OB, R_OUT, R_PG, R_PLE, R_ROWS = 0, 128, 384, 512, 640, 672

ADAM_LR, ADAM_B1, ADAM_B2, ADAM_EPS, ADAM_WD, ADAM_STEP = 0.001, 0.9, 0.999, 1e-08, 0.01, 10

V7X_VMEM_LIMIT = 56 * 1024 * 1024


def _cp(sem=None):
    return pltpu.CompilerParams(dimension_semantics=sem, vmem_limit_bytes=V7X_VMEM_LIMIT)


def _dot(a, b, prec=None):
    return jnp.dot(a, b, preferred_element_type=F32, precision=prec)


def _dot_nt(a, b, prec=None):
    return lax.dot_general(a, b, (((1,), (1,)), ((), ())), preferred_element_type=F32, precision=prec)


def _dot_tn(a, b, prec=None):
    return lax.dot_general(a, b, (((0,), (0,)), ((), ())), preferred_element_type=F32, precision=prec)


def _sigmoid(x):
    return 1.0 / (1.0 + jnp.exp(-x))


def _gelu_and_grad(x):
    c = math.sqrt(2.0 / math.pi)
    x2 = x * x
    t = jnp.tanh(c * (x + 0.044715 * x * x2))
    g = 0.5 * x * (1.0 + t)
    dg = 0.5 * (1.0 + t) + 0.5 * x * (1.0 - t * t) * c * (1.0 + 3.0 * 0.044715 * x2)
    return g, dg


def _gelu(x):
    c = math.sqrt(2.0 / math.pi)
    return 0.5 * x * (1.0 + jnp.tanh(c * (x + 0.044715 * x * x * x)))


def _softplus(x):
    return jnp.maximum(x, 0.0) + jnp.log(1.0 + jnp.exp(-jnp.abs(x)))


def _full(shape):
    n = len(shape)
    return pl.BlockSpec(shape, lambda *_: (0,) * n)


_ANY = pl.BlockSpec(memory_space=pl.ANY)


class _Exchange(NamedTuple):
    arrays: list
    out_shape: list
    sems: list
    make: Callable


def _call(body, ex, *, name, grid, in_specs, out_specs, out_shape, scratch_shapes, args):
    ki, ko, ks = len(in_specs), len(out_specs), len(scratch_shapes)
    ei, eo = len(ex.arrays), len(ex.out_shape)
    last = [g - 1 for g in grid]

    def full_body(*refs):
        r = list(refs)
        ins, eins, r = r[:ki], r[ki:ki + ei], r[ki + ei:]
        outs, eouts, r = r[:ko], r[ko:ko + eo], r[ko + eo:]
        scr, esems = r[:ks], r[ks:]
        start, finish = ex.make(eins, eouts, esems)
        ids = [pl.program_id(a) for a in range(len(grid))]
        is_first = functools.reduce(lambda p, q: p & q, [i == 0 for i in ids])
        is_last = functools.reduce(lambda p, q: p & q, [i == l for i, l in zip(ids, last)])
        pl.when(is_first)(start)
        body(*ins, *outs, *scr)
        pl.when(is_last)(finish)

    res = pl.pallas_call(
        full_body, name=name, grid=grid, in_specs=list(in_specs) + [_ANY] * ei, out_specs=list(out_specs) + [_ANY] * eo,
        out_shape=list(out_shape) + list(ex.out_shape), scratch_shapes=list(scratch_shapes) + list(ex.sems),
        compiler_params=_cp(("arbitrary",) * len(grid)),
    )(*args, *ex.arrays)
    return res[:ko], res[ko:]


def _run_exchange(ex, name):
    ni, no = len(ex.arrays), len(ex.out_shape)

    def body(*refs):
        start, finish = ex.make(refs[:ni], refs[ni:ni + no], refs[ni + no:])
        start()
        finish()

    return pl.pallas_call(body, name=name, in_specs=[_ANY] * ni, out_specs=[_ANY] * no, out_shape=list(ex.out_shape),
                          scratch_shapes=list(ex.sems))(*ex.arrays)


def _proj_fwd(x, norm_g, wp, ex):
    T = x.shape[0]
    tm, tn = min(T, 1024), 2048
    nj = OFF_DT // tn
    assert OFF_DT % tn == 0 and OFF_DT + DTW == NP

    def body(x_ref, g_ref, w_ref, wdt_ref, proj_ref, dt_ref, h_ref, hs_ref):
        j = pl.program_id(1)

        @pl.when(j == 0)
        def _():
            xf = x_ref[...]
            r = lax.rsqrt(jnp.mean(xf * xf, axis=-1, keepdims=True) + EPS)
            h = (xf * r * g_ref[...]).astype(BF16)
            hs_ref[...] = h
            h_ref[...] = h

        proj_ref[...] = _dot_nt(hs_ref[...], w_ref[...]).astype(BF16)

        @pl.when(j == nj - 1)
        def _():
            dt_ref[...] = _dot_nt(hs_ref[...], wdt_ref[...])

    return _call(
        body, ex, name="proj_fwd", grid=(T // tm, nj),
        in_specs=[pl.BlockSpec((tm, D), lambda i, j: (i, 0)), _full((1, D)), pl.BlockSpec((tn, D), lambda i, j: (j, 0)),
                  pl.BlockSpec((DTW, D), lambda i, j: (OFF_DT // DTW, 0))],
        out_specs=[pl.BlockSpec((tm, tn), lambda i, j: (i, j)), pl.BlockSpec((tm, DTW), lambda i, j: (i, 0)),
                   pl.BlockSpec((tm, D), lambda i, j: (i, 0))],
        out_shape=[jax.ShapeDtypeStruct((T, OFF_DT), BF16), jax.ShapeDtypeStruct((T, DTW), F32), jax.ShapeDtypeStruct((T, D), BF16)],
        scratch_shapes=[pltpu.VMEM((tm, D), BF16)], args=(x, norm_g, wp, wp))


def _gmlp_tile():
    return 256


def _gmlp_fwd_tile(u_ref, v_ref, z_ref, lg_ref, lb_ref, ws_ref, bs_ref, ya_ref, vn_s):
    tm = u_ref.shape[0]
    vg = _gelu(v_ref[...].astype(F32))
    mu = jnp.mean(vg, axis=-1, keepdims=True)
    xc = vg - mu
    rstd = lax.rsqrt(jnp.mean(xc * xc, axis=-1, keepdims=True) + EPS)
    vn_s[...] = (xc * rstd * lg_ref[...] + lb_ref[...]).astype(BF16)
    for c in range(tm // CH):
        rs = slice(c * CH, (c + 1) * CH)
        for g in range(NG):
            cs_ = slice(g * 256, (g + 1) * 256)
            sv = _dot(ws_ref[g], vn_s[rs, cs_]) + bs_ref[g]
            z = z_ref[rs, cs_].astype(F32)
            ya_ref[rs, cs_] = (_gelu(u_ref[rs, cs_].astype(F32)) * sv * (z * _sigmoid(z))).astype(BF16)


def _gmlp_bwd(proj, dya, ln_g, ln_b, ws, wst, bst, gsel, ex):
    T = proj.shape[0]
    tm = min(T, _gmlp_tile())

    def body(u_ref, v_ref, z_ref, dy_ref, lg_ref, lb_ref, ws_ref, wst_ref, bs_ref, gsel_ref,
             d_ref, dws_ref, dbs_ref, dln_ref, vn_s, dsv_s, dvn_s):
        du_ref, dv_ref, dz_ref = d_ref.at[:, 0:D], d_ref.at[:, D:2 * D], d_ref.at[:, 2 * D:3 * D]
        @pl.when(pl.program_id(0) == 0)
        def _():
            dws_ref[...] = jnp.zeros_like(dws_ref)
            dbs_ref[...] = jnp.zeros_like(dbs_ref)
            dln_ref[...] = jnp.zeros_like(dln_ref)

        vg, dvg_dv = _gelu_and_grad(v_ref[...].astype(F32))
        mu = jnp.mean(vg, axis=-1, keepdims=True)
        xc = vg - mu
        rstd = lax.rsqrt(jnp.mean(xc * xc, axis=-1, keepdims=True) + EPS)
        vhat = xc * rstd
        vn_s[...] = (vhat * lg_ref[...] + lb_ref[...]).astype(BF16)
        ri = lax.broadcasted_iota(jnp.int32, (CH, CH), 0)
        ci = lax.broadcasted_iota(jnp.int32, (CH, CH), 1)
        tril = (ri >= ci).astype(F32)
        for c in range(tm // CH):
            rs = slice(c * CH, (c + 1) * CH)
            for g in range(NG):
                cs_ = slice(g * 256, (g + 1) * 256)
                vn = vn_s[rs, cs_]
                sv = _dot(ws_ref[g], vn) + bs_ref[g]
                z = z_ref[rs, cs_].astype(F32)
                sz = _sigmoid(z)
                ug, dug_du = _gelu_and_grad(u_ref[rs, cs_].astype(F32))
                dy = dy_ref[rs, cs_].astype(F32)
                zs = z * sz
                t = dy * zs
                dsv_f = dy * sv
                du_ref[rs, cs_] = (zs * dsv_f * dug_du).astype(BF16)
                dz_ref[rs, cs_] = (dsv_f * ug * (sz + zs * (1.0 - sz))).astype(BF16)
                dsv = (t * ug).astype(BF16)
                dsv_s[rs, cs_] = dsv
                dvn_s[rs, cs_] = _dot(wst_ref[g], dsv)
                dws_ref[g] += _dot_nt(dsv, vn) * tril
            dbs_ref[...] += _dot(dsv_s[rs, :], gsel_ref[...])
        dvn = dvn_s[...]
        dln_ref[0:1, :] += jnp.sum(dvn * vhat, axis=0, keepdims=True)
        dln_ref[1:2, :] += jnp.sum(dvn, axis=0, keepdims=True)
        dvh = dvn * lg_ref[...]
        dvg = rstd * (dvh - jnp.mean(dvh, axis=-1, keepdims=True) - vhat * jnp.mean(dvh * vhat, axis=-1, keepdims=True))
        dv_ref[...] = (dvg * dvg_dv).astype(BF16)

    blk = lambda off: pl.BlockSpec((tm, D), lambda i: (i, off // D))
    row = pl.BlockSpec((tm, D), lambda i: (i, 0))
    return _call(
        body, ex, name="gmlp_bwd", grid=(T // tm,),
        in_specs=[blk(OFF_U), blk(OFF_V), blk(OFF_ZA), row, _full((1, D)), _full((1, D)), _full((NG, CH, CH)),
                  _full((NG, CH, CH)), _full((NG, CH, 256)), _full((D, 128))],
        out_specs=[pl.BlockSpec((tm, 3 * D), lambda i: (i, 0)), _full((NG, CH, CH)), _full((CH, 128)), _full((8, D))],
        out_shape=[jax.ShapeDtypeStruct((T, 3 * D), BF16),
                   jax.ShapeDtypeStruct((NG, CH, CH), F32), jax.ShapeDtypeStruct((CH, 128), F32), jax.ShapeDtypeStruct((8, D), F32)],
        scratch_shapes=[pltpu.VMEM((tm, D), BF16), pltpu.VMEM((tm, D), BF16), pltpu.VMEM((tm, D), F32)],
        args=(proj, proj, proj, dya, ln_g, ln_b, ws, wst, bst, gsel))


def _shift_matrix(down):
    t = jnp.arange(CH)[None, :, None]
    j = jnp.arange(1, 4)[:, None, None]
    col = jnp.arange(2 * CH)[None, None, :]
    src = CH + t - j if down else t + j
    return (col == src).astype(BF16).reshape(3 * CH, 2 * CH)


def _conv_pre(x, moved, cw_ref, cb_ref):
    pre = cb_ref[...] + cw_ref[3:4, :] * x
    for j in (1, 2, 3):
        pre = pre + cw_ref[3 - j:4 - j, :] * moved[(j - 1) * CH:j * CH]
    return pre


def _split_dot(x, w, parts, w_left=False):
    acc, r = None, x
    for k in range(parts):
        hi = r.astype(BF16)
        d = _dot(w, hi) if w_left else _dot(hi, w)
        acc = d if acc is None else acc + d
        if k + 1 < parts:
            r = r - hi.astype(F32)
    return acc


def _chunk_decays(dt, alog_ref, e_ref, cs_s, cst_s, csx_s):
    a = -jnp.exp(alog_ref[...])
    ri = lax.broadcasted_iota(jnp.int32, (CH, CH), 0)
    ci = lax.broadcasted_iota(jnp.int32, (CH, CH), 1)
    tril = ri >= ci
    cs = _split_dot(dt * a, tril.astype(BF16), 3, w_left=True)
    cs_s[...] = cs
    cst_s[...] = cs.T
    csx_s[...] = _split_dot(cs, e_ref[...], 3)
    return a, tril, ri, ci


def _lmat(cst_s, h, tril):
    rowb = jnp.broadcast_to(cst_s[h:h + 1, :], (CH, CH))
    return jnp.exp(jnp.where(tril, rowb.T - rowb, -jnp.inf))


def _head_pair_rows(v, lane):
    return jnp.concatenate([jnp.where(lane < HD, v, 0.0), jnp.where(lane < HD, 0.0, v)], axis=0).astype(BF16)


def _ssd_fwd(proj, dtr, cw, cb, dtb, alog, dskx, sg, e128):
    T = proj.shape[0]
    nc = T // CH

    def body(xbc_ref, zb_ref, dt_ref, cw_ref, cb_ref, dtb_ref, alog_ref, dx_ref, sg_ref, e_ref, shift_ref,
             y_ref, yb_ref, hp_ref, pre_ref, xx_s, h_s, cs_s, cst_s, csx_s, yz_s):
        @pl.when(pl.program_id(0) == 0)
        def _():
            xx_s[...] = jnp.zeros_like(xx_s)
            h_s[...] = jnp.zeros_like(h_s)

        xx_s[CH:, :] = xbc_ref[...]
        moved = _dot(shift_ref[...], xx_s[...])
        xx_s[CH - 16:CH, :] = xbc_ref[CH - 16:, :]
        x = xbc_ref[...].astype(F32)
        pre = _conv_pre(x, moved, cw_ref, cb_ref)
        pre_ref[...] = pre
        xc = pre * _sigmoid(pre)
        dt = _softplus(dt_ref[...] + dtb_ref[...])
        a, tril, _, lane = _chunk_decays(dt, alog_ref, e_ref, cs_s, cst_s, csx_s)
        dt_x = _split_dot(dt, e_ref[...], 2)
        cs_last_x = csx_s[CH - 1:CH, :]
        hp_ref[0] = h_s[...]
        for g in range(NG):
            gs = slice(g * 512, (g + 1) * 512)
            bg = xc[:, DI + g * NST:DI + (g + 1) * NST].astype(BF16)
            cg = xc[:, DI + 512 + g * NST:DI + 512 + (g + 1) * NST].astype(BF16)
            cbm = _dot_nt(cg, bg)
            xg = xc[:, gs]
            xdt = xg * dt_x[:, gs]
            hprev = h_s[:, gs]
            csx = csx_s[:, gs]
            yoff = _dot(cg, hprev.astype(BF16)) * jnp.exp(csx)
            st = _dot_tn(bg, (xdt * jnp.exp(cs_last_x[:, gs] - csx)).astype(BF16))
            h_s[:, gs] = jnp.exp(cs_last_x[:, gs]) * hprev + st
            ssq = jnp.zeros((CH, 1), F32)
            for q in range(4):
                h0 = g * 8 + 2 * q
                ps = slice(q * 128, (q + 1) * 128)
                cols = slice(g * 512 + q * 128, g * 512 + (q + 1) * 128)
                m01 = jnp.concatenate([cbm * _lmat(cst_s, h0, tril), cbm * _lmat(cst_s, h0 + 1, tril)], axis=1).astype(BF16)
                yq = _dot(m01, _head_pair_rows(xdt[:, ps], lane)) + yoff[:, ps] + xg[:, ps] * dx_ref[:, cols]
                y_ref[:, cols] = yq
                z = zb_ref[:, cols].astype(F32)
                yz = yq * z * _sigmoid(z)
                yz_s[:, cols] = yz
                ssq = ssq + jnp.sum(yz * yz, axis=1, keepdims=True)
            rg = lax.rsqrt(ssq * (1.0 / 512.0) + EPS)
            yb_ref[:, gs] = (yz_s[:, gs] * rg * sg_ref[:, gs]).astype(BF16)

    return pl.pallas_call(
        body, name="ssd_fwd", grid=(nc,),
        in_specs=[pl.BlockSpec((CH, CD), lambda c: (c, OFF_XBC // CD)), pl.BlockSpec((CH, DI), lambda c: (c, OFF_ZB // DI)),
                  pl.BlockSpec((CH, DTW), lambda c: (c, 0)), _full((4, CD)), _full((1, CD)), _full((1, DTW)),
                  _full((1, DTW)), _full((1, DI)), _full((1, DI)), _full((DTW, DI)), _full((3 * CH, 2 * CH))],
        out_specs=[pl.BlockSpec((CH, DI), lambda c: (c, 0)), pl.BlockSpec((CH, DI), lambda c: (c, 0)),
                   pl.BlockSpec((1, NST, DI), lambda c: (c, 0, 0)), pl.BlockSpec((CH, CD), lambda c: (c, 0))],
        out_shape=[jax.ShapeDtypeStruct((T, DI), F32), jax.ShapeDtypeStruct((T, DI), BF16),
                   jax.ShapeDtypeStruct((nc, NST, DI), F32), jax.ShapeDtypeStruct((T, CD), F32)],
        scratch_shapes=[pltpu.VMEM((2 * CH, CD), BF16), pltpu.VMEM((NST, DI), F32), pltpu.VMEM((CH, CH), F32),
                        pltpu.VMEM((CH, CH), F32), pltpu.VMEM((CH, DI), F32), pltpu.VMEM((CH, DI), F32)],
        compiler_params=_cp(("arbitrary",)),
    )(proj, proj, dtr, cw, cb, dtb, alog, dskx, sg, e128, _shift_matrix(down=True))


def _ssd_bwd(proj, pre_all, dtr, y, dyb, hprev_all, cw, dtb, alog, dskx, sg, e128, et128, ex):
    T = proj.shape[0]
    nc = T // CH

    def body(xbc_ref, pre_ref, zb_ref, dt_ref, y_ref, dyb_ref, hp_ref, cw_ref, dtb_ref, alog_ref, dx_ref, sg_ref,
             e_ref, et_ref, shift_ref, d_ref, ddt_ref, dcw_ref, dsm_ref, dsg_ref,
             g_s, dd_s, cs_s, cst_s, csx_s, dy_s, dxdt_s, dxs_s, dsd_s, dxc_s, gh_s):
        dxbc_ref, dzb_ref = d_ref.at[:, 0:CD], d_ref.at[:, CD:CD + DI]
        i = pl.program_id(0)

        @pl.when(i == 0)
        def _():
            g_s[...] = jnp.zeros_like(g_s)
            dd_s[...] = jnp.zeros_like(dd_s)
            dcw_ref[...] = jnp.zeros_like(dcw_ref)
            dsm_ref[...] = jnp.zeros_like(dsm_ref)
            dsg_ref[...] = jnp.zeros_like(dsg_ref)

        pre = pre_ref[...]
        sp = _sigmoid(pre)
        xc = pre * sp
        dtr = dt_ref[...] + dtb_ref[...]
        dt = _softplus(dtr)
        a, tril, ri, lane = _chunk_decays(dt, alog_ref, e_ref, cs_s, cst_s, csx_s)
        et = et_ref[...]
        dt_x = _split_dot(dt, e_ref[...], 2)
        cs_last_x = csx_s[CH - 1:CH, :]

        for g in range(NG):
            gs = slice(g * 512, (g + 1) * 512)
            z = zb_ref[:, gs].astype(F32)
            sz = _sigmoid(z)
            yv = y_ref[:, gs]
            zs = z * sz
            yz = yv * zs
            rg = lax.rsqrt(jnp.mean(yz * yz, axis=-1, keepdims=True) + EPS)
            yn = yz * rg
            dyb = dyb_ref[:, gs].astype(F32)
            dsg_ref[0:1, gs] += jnp.sum(dyb * yn, axis=0, keepdims=True)
            dyn = dyb * sg_ref[:, gs]
            dyz = rg * (dyn - yn * jnp.mean(dyn * yn, axis=-1, keepdims=True))
            dy_s[:, gs] = dyz * zs
            dzb_ref[:, gs] = (dyz * yv * (sz + zs * (1.0 - sz))).astype(BF16)

        rsum = jnp.zeros((CH, DTW), F32)
        csum_t = jnp.zeros((DTW, CH), F32)
        for g in range(NG):
            gs = slice(g * 512, (g + 1) * 512)
            bg = xc[:, DI + g * NST:DI + (g + 1) * NST].astype(BF16)
            cg = xc[:, DI + 512 + g * NST:DI + 512 + (g + 1) * NST].astype(BF16)
            cbm = _dot_nt(cg, bg)
            xdt = xc[:, gs] * dt_x[:, gs]
            hprev = hp_ref[0, :, gs]
            hpb = hprev.astype(BF16)
            gn = g_s[:, gs]
            gnb = gn.astype(BF16)
            dy = dy_s[:, gs]
            csx = csx_s[:, gs]
            ecs = jnp.exp(csx)
            dec = jnp.exp(cs_last_x[:, gs] - csx)
            dye = (dy * ecs).astype(BF16)
            dc = _dot_nt(dye, hpb)
            dprev = _dot_tn(cg, dye)
            dxdt_state = dec * _dot(bg, gnb)
            db = _dot_nt((xdt * dec).astype(BF16), gnb)
            dcb = jnp.zeros((CH, CH), F32)
            for q in range(4):
                h0 = g * 8 + 2 * q
                ps = slice(q * 128, (q + 1) * 128)
                dyp = dy[:, ps]
                l0 = _lmat(cst_s, h0, tril)
                l1 = _lmat(cst_s, h0 + 1, tril)
                m0 = cbm * l0
                m1 = cbm * l1
                dm = _dot_nt(dyp.astype(BF16), _head_pair_rows(xdt[:, ps], lane))
                dm0 = dm[:, :CH]
                dm1 = dm[:, CH:]
                dcb = dcb + dm0 * l0 + dm1 * l1
                for hh, qm in ((h0, dm0 * m0), (h0 + 1, dm1 * m1)):
                    rsum = jnp.where(lane == hh, jnp.sum(qm, axis=1, keepdims=True), rsum)
                    csum_t = jnp.where(ri == hh, jnp.sum(qm, axis=0, keepdims=True), csum_t)
                mst = jnp.concatenate([m0, m1], axis=0).astype(BF16)
                d = _dot_tn(mst, _head_pair_rows(dyp, lane))
                dxdt_s[:, g * 512 + q * 128:g * 512 + (q + 1) * 128] = d + dxdt_state[:, ps]
            yoff = _dot(cg, hpb) * ecs
            dsd_s[:, gs] = xdt * dxdt_state
            dxs_s[:, gs] = dy * yoff
            dcbb = dcb.astype(BF16)
            dxc_s[:, DI + 512 + g * NST:DI + 512 + (g + 1) * NST] = dc + _dot(dcbb, bg)
            dxc_s[:, DI + g * NST:DI + (g + 1) * NST] = db + _dot_tn(dcbb, cg)
            gh_s[:, gs] = jnp.broadcast_to(jnp.sum(gn * hprev, axis=0, keepdims=True), (8, 512))
            g_s[:, gs] = dprev + jnp.exp(cs_last_x[:, gs]) * gn

        xs = xc[:, :DI]
        dy = dy_s[...]
        dxdt = dxdt_s[...]
        cs_last = cs_s[CH - 1:CH, :]
        state_e = _split_dot(dsd_s[...], et, 2)
        dcd = 0.125 * jnp.sum(_split_dot(gh_s[...], et, 2), axis=0, keepdims=True) * jnp.exp(cs_last)
        row = lax.broadcasted_iota(jnp.int32, (CH, 1), 0)
        dcs = rsum - csum_t.T + _split_dot(dxs_s[...], et, 2) - state_e
        dcs = dcs + jnp.where(row == CH - 1, jnp.sum(state_e, axis=0, keepdims=True) + dcd, 0.0)
        dda = _split_dot(dcs, (lane >= ri).astype(BF16), 3, w_left=True)
        ddt = dda * a + _dot((dxdt * xs).astype(BF16), et)
        ddtr = jnp.where(lane < NH, ddt * _sigmoid(dtr), 0.0)
        ddt_ref[...] = ddtr.astype(BF16)
        dsm_ref[0:1, :] += jnp.sum(ddtr, axis=0, keepdims=True)
        dsm_ref[1:2, :] += jnp.sum(dda * dt, axis=0, keepdims=True) * a
        dsm_ref[2:3, :] += jnp.sum(_dot((dy * xs).astype(BF16), et), axis=0, keepdims=True)
        dxc_s[:, :DI] = dxdt * dt_x + dy * dx_ref[...]

        dpre = dxc_s[...] * (sp + xc * (1.0 - sp))
        dpre_b = dpre.astype(BF16)
        dd_s[:CH, :] = dpre_b
        moved = _dot(shift_ref[...], dd_s[...])
        dd_s[CH:CH + 16, :] = dpre_b[:16]
        x = xbc_ref[...].astype(F32)
        dcw_ref[4:5, :] += jnp.sum(dpre, axis=0, keepdims=True)
        dxbc = cw_ref[3:4, :] * dpre
        dcw_ref[3:4, :] += jnp.sum(dpre * x, axis=0, keepdims=True)
        for j in (1, 2, 3):
            ahead = moved[(j - 1) * CH:j * CH]
            dcw_ref[3 - j:4 - j, :] += jnp.sum(ahead * x, axis=0, keepdims=True)
            dxbc = dxbc + cw_ref[3 - j:4 - j, :] * ahead
        dxbc_ref[...] = dxbc.astype(BF16)

    rev = lambda c: nc - 1 - c
    return _call(
        body, ex, name="ssd_bwd", grid=(nc,),
        in_specs=[pl.BlockSpec((CH, CD), lambda c: (rev(c), OFF_XBC // CD)),
                  pl.BlockSpec((CH, CD), lambda c: (rev(c), 0)),
                  pl.BlockSpec((CH, DI), lambda c: (rev(c), OFF_ZB // DI)),
                  pl.BlockSpec((CH, DTW), lambda c: (rev(c), 0)),
                  pl.BlockSpec((CH, DI), lambda c: (rev(c), 0)), pl.BlockSpec((CH, DI), lambda c: (rev(c), 0)),
                  pl.BlockSpec((1, NST, DI), lambda c: (rev(c), 0, 0)),
                  _full((4, CD)), _full((1, DTW)), _full((1, DTW)), _full((1, DI)), _full((1, DI)),
                  _full((DTW, DI)), _full((DI, DTW)), _full((3 * CH, 2 * CH))],
        out_specs=[pl.BlockSpec((CH, CD + DI), lambda c: (rev(c), 0)),
                   pl.BlockSpec((CH, DTW), lambda c: (rev(c), 0)), _full((8, CD)), _full((8, DTW)), _full((8, DI))],
        out_shape=[jax.ShapeDtypeStruct((T, CD + DI), BF16), jax.ShapeDtypeStruct((T, DTW), BF16),
                   jax.ShapeDtypeStruct((8, CD), F32), jax.ShapeDtypeStruct((8, DTW), F32), jax.ShapeDtypeStruct((8, DI), F32)],
        scratch_shapes=[pltpu.VMEM((NST, DI), F32), pltpu.VMEM((2 * CH, CD), BF16), pltpu.VMEM((CH, CH), F32), pltpu.VMEM((CH, CH), F32),
                        pltpu.VMEM((CH, DI), F32), pltpu.VMEM((CH, DI), F32), pltpu.VMEM((CH, DI), F32), pltpu.VMEM((CH, DI), F32),
                        pltpu.VMEM((CH, DI), F32), pltpu.VMEM((CH, CD), F32), pltpu.VMEM((8, DI), F32)],
        args=(proj, pre_all, proj, dtr, y, dyb, hprev_all, cw, dtb, alog, dskx, sg, e128, et128, _shift_matrix(down=False)))


def _merge_tile():
    return 256


def _merge(x, yb, proj, p, tgt, gmlp, w_oa, w_ob, w_out, w_pg, w_ple, ple_g, fin_g):
    T = x.shape[0]
    tm = min(T, _merge_tile())

    def body(x_ref, u_ref, v_ref, za_ref, yb_ref, ga_ref, gb_ref, p_ref, t_ref, lg_ref, lb_ref, ws_ref, bs_ref,
             woa, wob, wout, wpg, wple, pg_ref, fg_ref,
             dx1_ref, dx1b_ref, ya_ref, mg_ref, hp_ref, dpre_ref, dpe_ref, doa_ref, dob_ref, dya_ref, dyb_ref, dg_ref, acc_ref, vn_s):
        @pl.when(pl.program_id(0) == 0)
        def _():
            acc_ref[...] = jnp.zeros_like(acc_ref)

        _gmlp_fwd_tile(u_ref, v_ref, za_ref, lg_ref, lb_ref, ws_ref, bs_ref, ya_ref, vn_s)
        oa = _dot(ya_ref[...], woa[...])
        ob = _dot(yb_ref[...], wob[...])
        sa = _sigmoid(ga_ref[...].astype(F32))
        sb = _sigmoid(gb_ref[...].astype(F32))
        mg = sa * oa + sb * ob
        mgb = mg.astype(BF16)
        mg_ref[...] = mgb
        x1 = x_ref[...] + _dot(mgb, wout[...])
        r2 = lax.rsqrt(jnp.mean(x1 * x1, axis=-1, keepdims=True) + EPS)
        xh1 = x1 * r2
        hpb = (xh1 * pg_ref[...]).astype(BF16)
        hp_ref[...] = hpb
        gate = _sigmoid(_dot(hpb, wpg[...]))
        pe = _dot(p_ref[...].astype(BF16), wple[...])
        x2 = x1 + gate * pe
        r3 = lax.rsqrt(jnp.mean(x2 * x2, axis=-1, keepdims=True) + EPS)
        xh2 = x2 * r3
        err = xh2 * fg_ref[...] - t_ref[...]
        acc_ref[2:3, :] += 0.5 * jnp.sum(jnp.mean(err * err, axis=-1, keepdims=True))
        dyo = err * (1.0 / D)
        acc_ref[0:1, :] += jnp.sum(dyo * xh2, axis=0, keepdims=True)
        dn = dyo * fg_ref[...]
        dx2 = r3 * (dn - xh2 * jnp.mean(dn * xh2, axis=-1, keepdims=True))
        dpe_ref[...] = (dx2 * gate).astype(BF16)
        dpre = (dx2 * pe * gate * (1.0 - gate)).astype(BF16)
        dpre_ref[...] = dpre
        dhp = _dot_nt(dpre, wpg[...])
        acc_ref[1:2, :] += jnp.sum(dhp * xh1, axis=0, keepdims=True)
        dhn = dhp * pg_ref[...]
        dx1 = dx2 + r2 * (dhn - xh1 * jnp.mean(dhn * xh1, axis=-1, keepdims=True))
        dx1_ref[...] = dx1
        dx1b = dx1.astype(BF16)
        dx1b_ref[...] = dx1b
        dmg = _dot_nt(dx1b, wout[...])
        doa = (dmg * sa).astype(BF16)
        dob = (dmg * sb).astype(BF16)
        doa_ref[...] = doa
        dob_ref[...] = dob
        dg_ref[:, :D] = (dmg * oa * sa * (1.0 - sa)).astype(BF16)
        dg_ref[:, D:] = (dmg * ob * sb * (1.0 - sb)).astype(BF16)
        dya_ref[...] = _dot_nt(doa, woa[...]).astype(BF16)
        dyb_ref[...] = _dot_nt(dob, wob[...]).astype(BF16)

    row = lambda w: pl.BlockSpec((tm, w), lambda i: (i, 0))
    blk = lambda off: pl.BlockSpec((tm, D), lambda i: (i, off // D))
    wsp = lambda s: pl.BlockSpec(s, lambda i: (0, 0), pipeline_mode=pl.Buffered(1))
    return pl.pallas_call(
        body, name="merge", grid=(T // tm,),
        in_specs=[row(D), blk(OFF_U), blk(OFF_V), blk(OFF_ZA), row(DI), blk(OFF_GA), blk(OFF_GB), row(PLE), row(D),
                  _full((1, D)), _full((1, D)), _full((NG, CH, CH)), _full((NG, CH, 256)),
                  wsp((D, D)), wsp((DI, D)), wsp((D, D)), wsp((D, D)), wsp((PLE, D)), _full((1, D)), _full((1, D))],
        out_specs=[row(D)] * 10 + [row(DI), row(2 * D), _full((8, D))],
        out_shape=[jax.ShapeDtypeStruct((T, D), F32)] + [jax.ShapeDtypeStruct((T, D), BF16)] * 9
        + [jax.ShapeDtypeStruct((T, DI), BF16), jax.ShapeDtypeStruct((T, 2 * D), BF16), jax.ShapeDtypeStruct((8, D), F32)],
        scratch_shapes=[pltpu.VMEM((tm, D), BF16)],
        compiler_params=_cp(("arbitrary",)),
    )(x, proj, proj, proj, yb, proj, proj, p, tgt, *gmlp, w_oa, w_ob, w_out, w_pg, w_ple, ple_g, fin_g)


def _wgrad(a, b, name):
    T, K = a.shape
    N = b.shape[1]
    tt, tk, tn = min(T, 2048), min(K, 1024), min(N, 1024)
    nt = T // tt

    def body(a_ref, b_ref, o_ref, acc_s):
        t = pl.program_id(2)

        @pl.when(t == 0)
        def _():
            acc_s[...] = jnp.zeros_like(acc_s)

        acc_s[...] += _dot_tn(a_ref[...].astype(BF16), b_ref[...])

        @pl.when(t == nt - 1)
        def _():
            o_ref[...] = acc_s[...].astype(BF16)

    return pl.pallas_call(
        body, name=name, grid=(K // tk, N // tn, nt),
        in_specs=[pl.BlockSpec((tt, tk), lambda k, n, t: (t, k)), pl.BlockSpec((tt, tn), lambda k, n, t: (t, n))],
        out_specs=pl.BlockSpec((tk, tn), lambda k, n, t: (k, n)),
        out_shape=jax.ShapeDtypeStruct((K, N), BF16),
        scratch_shapes=[pltpu.VMEM((tk, tn), F32)],
        compiler_params=_cp(("parallel", "parallel", "arbitrary")),
    )(a, b)


def _proj_bwd(x, dx1, norm_g, wt, pieces, ddt, ex):
    T = x.shape[0]
    tm = min(T, 1024)
    nk = OFF_DT // D + 1
    starts = [sum(a.shape[1] for a in pieces[:n]) // D for n in range(len(pieces))]
    ranges = [(s, s + a.shape[1] // D) for s, a in zip(starts, pieces)]
    assert ranges[-1][1] == nk - 1
    npc = len(pieces)

    def body(x_ref, dx1_ref, g_ref, w_ref, wdt_ref, *rest):
        piece_refs, ddt_ref, gx_ref, dng_ref, acc_s = rest[:npc], rest[npc], rest[npc + 1], rest[npc + 2], rest[npc + 3]
        i, k = pl.program_id(0), pl.program_id(1)

        @pl.when((i == 0) & (k == 0))
        def _():
            dng_ref[...] = jnp.zeros_like(dng_ref)

        @pl.when(k == 0)
        def _():
            acc_s[...] = jnp.zeros_like(acc_s)

        for ref, (lo, hi) in zip(piece_refs, ranges):
            @pl.when((k >= lo) & (k < hi))
            def _(ref=ref):
                acc_s[...] += _dot(ref[...], w_ref[...])

        @pl.when(k == nk - 1)
        def _():
            dh = acc_s[...] + _dot(ddt_ref[...], wdt_ref[...])
            xf = x_ref[...]
            r = lax.rsqrt(jnp.mean(xf * xf, axis=-1, keepdims=True) + EPS)
            xh = xf * r
            dng_ref[0:1, :] += jnp.sum(dh * xh, axis=0, keepdims=True)
            dxn = dh * g_ref[...]
            gx_ref[...] = dx1_ref[...] + r * (dxn - xh * jnp.mean(dxn * xh, axis=-1, keepdims=True))

    def piece_spec(lo, hi):
        return pl.BlockSpec((tm, D), lambda i, k: (i, jnp.clip(k - lo, 0, hi - lo - 1)))

    row = pl.BlockSpec((tm, D), lambda i, k: (i, 0))
    return _call(
        body, ex, name="proj_bwd", grid=(T // tm, nk),
        in_specs=[row, row, _full((1, D)), pl.BlockSpec((D, D), lambda i, k: (jnp.minimum(k, nk - 2), 0)),
                  pl.BlockSpec((DTW, D), lambda i, k: (OFF_DT // DTW, 0))]
        + [piece_spec(lo, hi) for lo, hi in ranges] + [pl.BlockSpec((tm, DTW), lambda i, k: (i, 0))],
        out_specs=[row, _full((8, D))],
        out_shape=[jax.ShapeDtypeStruct((T, D), F32), jax.ShapeDtypeStruct((8, D), F32)],
        scratch_shapes=[pltpu.VMEM((tm, D), F32)],
        args=(x, dx1, norm_g, wt, wt, *pieces, ddt))


def _elementwise_tile(R, C, limit=1 << 20):
    if R * C * 4 <= limit:
        return R, C
    rows = [t for t in range(16, R, 16) if R % t == 0 and t * C * 4 <= limit]
    if rows:
        return rows[-1], C
    cols = [t for t in range(128, C, 128) if C % t == 0 and R * t * 4 <= limit]
    return R, cols[-1]


def _adam_update(w, m, v, g):
    c1 = 1.0 - ADAM_B1 ** ADAM_STEP
    c2 = 1.0 - ADAM_B2 ** ADAM_STEP
    mm = ADAM_B1 * m + (1.0 - ADAM_B1) * g
    vv = ADAM_B2 * v + (1.0 - ADAM_B2) * (g * g)
    return -ADAM_LR * ((mm / c1) / (jnp.sqrt(vv / c2) + ADAM_EPS) + ADAM_WD * w), mm, vv


def _adamw(w, m, v, parts, name, part_row=0, block_rows=None):
    R, C = w.shape
    tr, tc = _elementwise_tile(R, C) if block_rows is None else (block_rows, C)
    assert R % tr == 0 and part_row % tr == 0
    first = part_row // tr
    n = len(parts)
    wspec = pl.BlockSpec((tr, tc), lambda i, j: (i, j))
    flat = pl.BlockSpec((tr, tc), lambda i, j: (first + i, j))
    slot = lambda k: pl.BlockSpec((None, tr, tc), lambda i, j: (k, first + i, j))
    part_specs = [slot(p[1]) if isinstance(p, tuple) else flat for p in parts]
    part_arrays = [p[0] if isinstance(p, tuple) else p for p in parts]

    def body(*refs):
        w_ref, m_ref, v_ref = refs[:3]
        g_ref, d_ref, nm_ref, nv_ref = refs[3 + n:]
        g = refs[3][...].astype(F32)
        for r in refs[4:3 + n]:
            g = g + r[...].astype(F32)
        g_ref[...] = g
        d_ref[...], nm_ref[...], nv_ref[...] = _adam_update(w_ref[...], m_ref[...], v_ref[...], g)

    return pl.pallas_call(
        body, name=name, grid=(R // tr, C // tc), in_specs=[wspec] * 3 + part_specs, out_specs=[wspec] * 4,
        out_shape=[jax.ShapeDtypeStruct(w.shape, F32)] * 4, compiler_params=_cp(("parallel", "parallel")),
    )(w, m, v, *part_arrays)


_SMALL_WIDE = {"ln_a_g": 0, "ln_a_b": 1, "final_g": 8, "ple_norm_g": 9, "ssm_norm_g": 16, "conv_b": 44}
_WIDE_CONV_W, _WIDE_ROWS = 32, 56
_WIDE_LOSS = 10
_SMALL_NARROW = {"w_s": (0, NG * CH, 128), "b_s": (512, NG, 128), "dt_bias": (520, 1, NH), "a_log": (521, 1, NH),
                 "d_skip": (522, 1, NH)}
_NARROW_ROWS = 528
_SMALL_SHAPES = {"norm_g": (1, D), "ln_a_g": (1, D), "ln_a_b": (1, D), "ple_norm_g": (1, D), "final_g": (1, D),
                 "ssm_norm_g": (1, DI), "conv_b": (1, CD), "w_s": (NG * CH, CH), "b_s": (NG, CH), "dt_bias": (1, NH),
                 "a_log": (1, NH), "d_skip": (1, NH)}


def _adamw_small(w, m, v, wide_all, narrow_all, norm_all):
    names = list(_SMALL_SHAPES)
    n = len(names)

    def body(*refs):
        wr, mr, vr = refs[:n], refs[n:2 * n], refs[2 * n:3 * n]
        wide_ref, narrow_ref, norm_ref = refs[3 * n:3 * n + 3]
        outs = refs[3 * n + 3:]
        gr, dr, nmr, nvr, cw_ref, loss_ref = outs[:n], outs[n:2 * n], outs[2 * n:3 * n], outs[3 * n:4 * n], outs[4 * n], outs[4 * n + 1]

        def total(ref, rows, lanes):
            acc = ref[0, rows, lanes]
            for d in range(1, NDEV):
                acc = acc + ref[d, rows, lanes]
            return acc

        for k, name in enumerate(names):
            if name in _SMALL_WIDE or name == "norm_g":
                for part in range(_SMALL_SHAPES[name][1] // D):
                    pack, r = (norm_ref, 0) if name == "norm_g" else (wide_ref, _SMALL_WIDE[name] + part)
                    cols = slice(part * D, (part + 1) * D)
                    g = total(pack, slice(r, r + 1), slice(None))
                    gr[k][:, cols] = g
                    dr[k][:, cols], nmr[k][:, cols], nvr[k][:, cols] = _adam_update(wr[k][:, cols], mr[k][:, cols], vr[k][:, cols], g)
            else:
                r, rows, lanes = _SMALL_NARROW[name]
                g = total(narrow_ref, slice(r, r + rows), slice(0, lanes))
                gr[k][...] = g
                dr[k][...], nmr[k][...], nvr[k][...] = _adam_update(wr[k][...], mr[k][...], vr[k][...], g)
        cw_ref[...] = total(wide_ref, slice(_WIDE_CONV_W, _WIDE_CONV_W + 12), slice(None))
        loss_ref[...] = total(wide_ref, slice(_WIDE_LOSS, _WIDE_LOSS + 1), slice(None))

    shapes = [jax.ShapeDtypeStruct(_SMALL_SHAPES[k], F32) for k in names]
    specs = [_full(_SMALL_SHAPES[k]) for k in names]
    res = pl.pallas_call(
        body, name="adamw_small", grid=(1,),
        in_specs=specs * 3 + [_full(wide_all.shape), _full(narrow_all.shape), _full(norm_all.shape)],
        out_specs=specs * 4 + [_full((12, D)), _full((1, D))],
        out_shape=shapes * 4 + [jax.ShapeDtypeStruct((12, D), F32), jax.ShapeDtypeStruct((1, D), F32)],
        compiler_params=_cp(("arbitrary",)),
    )(*[w[k] for k in names], *[m[k] for k in names], *[v[k] for k in names], wide_all, narrow_all, norm_all)
    groups = [dict(zip(names, res[q * n:(q + 1) * n])) for q in range(4)]
    return groups[0], groups[1], groups[2], groups[3], res[4 * n], res[4 * n + 1][0, 0]


def _dev_index(px, py, pc):
    return 4 * px + 2 * py + pc


def _mesh_position():
    return lax.axis_index("x"), lax.axis_index("y"), lax.axis_index("c")


def _gather_exchange(blocks):
    n = len(blocks)

    def make(ins, outs, sems):
        send_sems, recv_sems, local_sems = sems
        x, y, c = _mesh_position()
        me, sibling = (x, y, c), (x, y, 1 - c)
        chips = [(1 - x, y), (x, 1 - y), (1 - x, 1 - y)]

        def copy(a, k, block, to, src=None):
            dst = outs[a].at[_dev_index(*block)]
            return pltpu.make_async_remote_copy(src_ref=dst if src is None else src, dst_ref=dst, send_sem=send_sems.at[a, k],
                                                recv_sem=recv_sems.at[a, k], device_id=to, device_id_type=MESH)

        mine = [pltpu.make_async_copy(ins[a], outs[a].at[_dev_index(*me)], local_sems.at[a]) for a in range(n)]
        first = []
        for a in range(n):
            first.append(copy(a, 0, me, sibling, src=ins[a]))
            first += [copy(a, 1 + j, me, (*chip, c), src=ins[a]) for j, chip in enumerate(chips)]

        def start():
            for cp in mine + first:
                cp.start()

        def finish():
            passed = []
            for j, chip in enumerate(chips):
                for a in range(n):
                    copy(a, 1 + j, (*chip, c), me).wait_recv()
                    fwd = copy(a, 4 + j, (*chip, c), sibling)
                    fwd.start()
                    passed.append(fwd)
            for a in range(n):
                copy(a, 0, sibling, me).wait_recv()
                for j, chip in enumerate(chips):
                    copy(a, 4 + j, (*chip, 1 - c), me).wait_recv()
            for cp in first + passed:
                cp.wait_send()
            for cp in mine:
                cp.wait()

        return start, finish

    return _Exchange(list(blocks), [jax.ShapeDtypeStruct((NDEV,) + b.shape, b.dtype) for b in blocks],
                     [pltpu.SemaphoreType.DMA((n, 7)), pltpu.SemaphoreType.DMA((n, 7)), pltpu.SemaphoreType.DMA((n,))], make)


def _relay_gather_exchange(blocks):
    n = len(blocks)

    def make(ins, outs, sems):
        send_sems, recv_sems, local_sems = sems
        x, y, c = _mesh_position()
        me, sibling = (x, y, c), (x, y, 1 - c)
        x_nbr, y_nbr, diag = (1 - x, y), (x, 1 - y), (1 - x, 1 - y)
        relay_from = (jnp.where(c == 0, x, 1 - x), jnp.where(c == 0, 1 - y, y))
        relay_to = (jnp.where(c == 0, 1 - x, x), jnp.where(c == 0, y, 1 - y))

        def copy(a, k, block, to, src=None):
            dst = outs[a].at[_dev_index(*block)]
            return pltpu.make_async_remote_copy(src_ref=dst if src is None else src, dst_ref=dst, send_sem=send_sems.at[a, k],
                                                recv_sem=recv_sems.at[a, k], device_id=to, device_id_type=MESH)

        mine = [pltpu.make_async_copy(ins[a], outs[a].at[_dev_index(*me)], local_sems.at[a]) for a in range(n)]
        first = []
        for a in range(n):
            first += [copy(a, 0, me, sibling, src=ins[a]), copy(a, 1, me, (*x_nbr, c), src=ins[a]), copy(a, 2, me, (*y_nbr, c), src=ins[a])]

        def start():
            for cp in mine + first:
                cp.start()

        def finish():
            later = []
            for a in range(n):
                copy(a, 1, (*x_nbr, c), me).wait_recv()
                copy(a, 2, (*y_nbr, c), me).wait_recv()
                later.append(copy(a, 3, (*relay_from, c), (*relay_to, c)))
                later += [copy(a, 4, (*x_nbr, c), sibling), copy(a, 5, (*y_nbr, c), sibling)]
                for cp in later[-3:]:
                    cp.start()
            for a in range(n):
                copy(a, 3, (*diag, c), me).wait_recv()
                later.append(copy(a, 6, (*diag, c), sibling))
                later[-1].start()
            for a in range(n):
                copy(a, 0, sibling, me).wait_recv()
                for k, chip in ((4, x_nbr), (5, y_nbr), (6, diag)):
                    copy(a, k, (*chip, 1 - c), me).wait_recv()
            for cp in first + later:
                cp.wait_send()
            for cp in mine:
                cp.wait()

        return start, finish

    return _Exchange(list(blocks), [jax.ShapeDtypeStruct((NDEV,) + b.shape, b.dtype) for b in blocks],
                     [pltpu.SemaphoreType.DMA((n, 7)), pltpu.SemaphoreType.DMA((n, 7)), pltpu.SemaphoreType.DMA((n,))], make)


def _combine(*exchanges):
    def make(ins, outs, sems):
        pairs = []
        for e in exchanges:
            ni, no, ns = len(e.arrays), len(e.out_shape), len(e.sems)
            pairs.append(e.make(ins[:ni], outs[:no], sems[:ns]))
            ins, outs, sems = ins[ni:], outs[no:], sems[ns:]

        def start():
            for s, _ in pairs:
                s()

        def finish():
            for _, f in pairs:
                f()

        return start, finish

    return _Exchange(sum((list(e.arrays) for e in exchanges), []), sum((list(e.out_shape) for e in exchanges), []),
                     sum((list(e.sems) for e in exchanges), []), make)


def _start_wait_all(copies, local=()):
    def start():
        for cp in list(local) + list(copies):
            cp.start()

    def finish():
        for cp in copies:
            cp.wait()
        for cp in local:
            cp.wait()

    return start, finish


def _no_exchange():
    return _Exchange([], [], [], lambda ins, outs, sems: (lambda: None, lambda: None))


def _direct_exchange(grads):
    n = len(grads)

    def make(ins, outs, sems):
        send_sems, recv_sems = sems
        x, y, c = _mesh_position()
        copies = []
        for a in range(n):
            for r in range(1, NDEV):
                peer = (x ^ ((r >> 2) & 1), y ^ ((r >> 1) & 1), c ^ (r & 1))
                copies.append(pltpu.make_async_remote_copy(
                    src_ref=ins[a].at[_dev_index(*peer)], dst_ref=outs[a].at[r - 1], send_sem=send_sems.at[a, r - 1],
                    recv_sem=recv_sems.at[a, r - 1], device_id=peer, device_id_type=MESH))
        return _start_wait_all(copies)

    return _Exchange(list(grads), [jax.ShapeDtypeStruct((NDEV - 1,) + g.shape[1:], g.dtype) for g in grads],
                     [pltpu.SemaphoreType.DMA((n, NDEV - 1)), pltpu.SemaphoreType.DMA((n, NDEV - 1))], make)


def _direct_gather_exchange(smalls):
    n = len(smalls)

    def make(ins, outs, sems):
        send_sems, recv_sems, local_sems = sems
        x, y, c = _mesh_position()
        copies, local = [], []
        for a in range(n):
            slot = outs[a].at[_dev_index(x, y, c)]
            local.append(pltpu.make_async_copy(ins[a], slot, local_sems.at[a]))
            for r in range(1, NDEV):
                peer = (x ^ ((r >> 2) & 1), y ^ ((r >> 1) & 1), c ^ (r & 1))
                copies.append(pltpu.make_async_remote_copy(src_ref=ins[a], dst_ref=slot, send_sem=send_sems.at[a, r - 1],
                                                           recv_sem=recv_sems.at[a, r - 1], device_id=peer, device_id_type=MESH))
        return _start_wait_all(copies, local=local)

    return _Exchange(list(smalls), [jax.ShapeDtypeStruct((NDEV,) + s.shape, s.dtype) for s in smalls],
                     [pltpu.SemaphoreType.DMA((n, 7)), pltpu.SemaphoreType.DMA((n, 7)), pltpu.SemaphoreType.DMA((n,))], make)


_W_IN_ROWS = {"u": (0, 1024), "v": (1024, 2048), "za": (2048, 3072), "zb": (3072, 5120), "xbc": (5120, 8192),
              "dt": (8192, 8224), "ga": (8224, 9248), "gb": (9248, 10272)}
_FWD_ORDER = ("xbc", "u", "zb", "v", "za", "ga", "gb", "dt")
_BWD_ORDER = ("xbc", "zb", "u", "v", "za", "ga", "gb", "dt")


def _w_in_t_rows(wt, order):
    assert order[-1] == "dt"
    z = jnp.zeros((NP - NIN, wt.shape[1]), wt.dtype)
    return jnp.concatenate([wt[slice(*_W_IN_ROWS[n])] for n in order] + [z], axis=0)


_WEIGHTS = ["norm_g", "w_in", "ln_a_g", "ln_a_b", "w_s", "b_s", "conv_w", "conv_b", "dt_bias", "a_log", "d_skip", "ssm_norm_g",
            "w_oa", "w_ob", "w_out", "ple_norm_g", "w_pg", "w_ple", "final_g"]


def _rows_pack(d, dtype):
    return jnp.concatenate([d["w_oa"].reshape(128, D), d["w_ob"].reshape(256, D), d["w_out"].reshape(128, D),
                            d["w_pg"].reshape(128, D), d["w_ple"].reshape(32, D)], axis=0).astype(dtype)


def kernel(x, p, norm_g, w_in, ln_a_g, ln_a_b, w_s, b_s, conv_w, conv_b, dt_bias, a_log, d_skip, ssm_norm_g, w_oa, w_ob, w_out, ple_norm_g, w_pg, w_ple, final_g, loss_target, m_norm_g, m_w_in, m_ln_a_g, m_ln_a_b, m_w_s, m_b_s, m_conv_w, m_conv_b, m_dt_bias, m_a_log, m_d_skip, m_ssm_norm_g, m_w_oa, m_w_ob, m_w_out, m_ple_norm_g, m_w_pg, m_w_ple, m_final_g, v_norm_g, v_w_in, v_ln_a_g, v_ln_a_b, v_w_s, v_b_s, v_conv_w, v_conv_b, v_dt_bias, v_a_log, v_d_skip, v_ssm_norm_g, v_w_oa, v_w_ob, v_w_out, v_ple_norm_g, v_w_pg, v_w_ple, v_final_g):
    args = dict(locals())
    w = {n: args[n] for n in _WEIGHTS}
    m = {n: args["m_" + n] for n in _WEIGHTS}
    v = {n: args["v_" + n] for n in _WEIGHTS}
    T = x.shape[1]
    xi, yi, ci = lax.axis_index("x"), lax.axis_index("y"), lax.axis_index("c")
    me = 4 * xi + 2 * yi + ci
    x2, p2, tgt = x.reshape(T, D), p.reshape(T, PLE), loss_target.reshape(T, D)

    norm_g2 = w["norm_g"].reshape(1, D)
    ws = jnp.where(jnp.tril(jnp.ones((CH, CH), bool))[None], w["w_s"].reshape(NG, CH, CH), 0.0).astype(BF16)
    wst = jnp.transpose(ws, (0, 2, 1))
    bst = jnp.broadcast_to(w["b_s"].reshape(NG, CH, 1), (NG, CH, 256))
    ln_g, ln_b = w["ln_a_g"].reshape(1, D), w["ln_a_b"].reshape(1, D)
    cb = w["conv_b"].reshape(1, CD)
    pad32 = lambda a: jnp.pad(a.reshape(1, NH), ((0, 0), (0, DTW - NH)))
    dtb, alog = pad32(w["dt_bias"]), pad32(w["a_log"])
    dskx = jnp.repeat(w["d_skip"].reshape(NH), HD).reshape(1, DI)
    sg = w["ssm_norm_g"].reshape(1, DI)
    ple_g, fin_g = w["ple_norm_g"].reshape(1, D), w["final_g"].reshape(1, D)
    e128 = (jnp.arange(DTW)[:, None] == (jnp.arange(DI)[None, :] // HD)).astype(BF16)
    et128 = e128.T
    gsel = ((jnp.arange(D)[:, None] // 256) == jnp.arange(128)[None, :]).astype(BF16)

    w_in_t = lambda a: jnp.transpose(a.reshape(D, WSH))
    (a_all,) = _run_exchange(_relay_gather_exchange([w_in_t(w["w_in"]).astype(BF16)]), "all_gather_w_in")
    w_in_full_t = a_all.reshape(NIN, D)
    wp = _w_in_t_rows(w_in_full_t, _FWD_ORDER)
    wt_bwd = _w_in_t_rows(w_in_full_t, _BWD_ORDER)
    (proj, dtr, h), (r_all, cw_all) = _proj_fwd(x2, norm_g2, wp, _gather_exchange([_rows_pack(w, BF16), w["conv_w"].reshape(4, CD // NDEV)]))
    f_oa = r_all[:, R_OA:R_OB].reshape(D, D)
    f_ob = r_all[:, R_OB:R_OUT].reshape(DI, D)
    f_out = r_all[:, R_OUT:R_PG].reshape(D, D)
    f_pg = r_all[:, R_PG:R_PLE].reshape(D, D)
    f_ple = jnp.transpose(r_all[:, R_PLE:R_ROWS].reshape(NDEV, PLE, 128), (1, 0, 2)).reshape(PLE, D)
    cw = jnp.transpose(cw_all, (1, 0, 2)).reshape(4, CD)

    y, yb, hprev, pre_all = _ssd_fwd(proj, dtr, cw, cb, dtb, alog, dskx, sg, e128)
    dx1, dx1b, ya, mg, hp, dpre, dpe, doa, dob, dya, dyb, dgab, acc = _merge(
        x2, yb, proj, p2, tgt, (ln_g, ln_b, ws, bst), f_oa, f_ob, f_out, f_pg, f_ple, ple_g, fin_g)

    gple = jnp.transpose(_wgrad(p2, dpe, "wgrad_ple").reshape(PLE, NDEV, 128), (1, 0, 2)).reshape(NDEV, 32, D)
    gr = jnp.concatenate([_wgrad(ya, doa, "wgrad_oa").reshape(NDEV, 128, D), _wgrad(yb, dob, "wgrad_ob").reshape(NDEV, 256, D),
                          _wgrad(mg, dx1b, "wgrad_out").reshape(NDEV, 128, D), _wgrad(hp, dpre, "wgrad_pg").reshape(NDEV, 128, D),
                          gple], axis=1)
    (duvz, dws, dbs, dln), _ = _gmlp_bwd(proj, dya, ln_g, ln_b, ws, wst, bst, gsel, _no_exchange())
    (dxz, ddt, dcw, dsm, dsg), (rr,) = _ssd_bwd(proj, pre_all, dtr, y, dyb, hprev, cw, dtb, alog, dskx, sg, e128, et128, _direct_exchange([gr]))

    g_xz = _wgrad(dxz, h, "wgrad_xbc_zb")
    g_w_in_t = jnp.concatenate([_wgrad(duvz, h, "wgrad_u_v_za"), g_xz[CD:], g_xz[:CD], _wgrad(ddt, h, "wgrad_dt")[:NH],
                                _wgrad(dgab, h, "wgrad_ga_gb")], axis=0)
    ga = g_w_in_t.reshape(NDEV, WSH, D)
    wide = jnp.concatenate([dln, acc, dsg.reshape(16, D), dcw.reshape(24, D)], axis=0)
    narrow = jnp.concatenate([dws.reshape(NG * CH, CH), jnp.pad(dbs[:, :NG].T, ((0, 8 - NG), (0, 0))), dsm], axis=0)
    (gx, dng), (ra, wide_all, narrow_all) = _proj_bwd(x2, dx1, norm_g2, wt_bwd, [dxz, duvz, dgab], ddt,
                                                      _combine(_direct_exchange([ga]), _direct_gather_exchange([wide, narrow])))
    (norm_all,) = _run_exchange(_direct_gather_exchange([dng]), "all_gather_d_norm_g")
    ga_own = lax.dynamic_index_in_dim(ga, me, 0, keepdims=False)
    gr_own = lax.dynamic_index_in_dim(gr, me, 0, keepdims=False)

    out_g, out_d, out_m, out_v = {}, {}, {}, {}
    outs = (out_g, out_d, out_m, out_v)
    res = _adamw(w_in_t(w["w_in"]), w_in_t(m["w_in"]), w_in_t(v["w_in"]), [ga_own] + [(ra, k) for k in range(NDEV - 1)], "adamw_w_in")
    for dst, val in zip(outs, res):
        dst["w_in"] = jnp.transpose(val).reshape(1, D, WSH)
    parts_r = [gr_own] + [(rr, k) for k in range(NDEV - 1)]
    for name, row, rows in (("w_oa", R_OA, 128), ("w_ob", R_OB, 256), ("w_out", R_OUT, 128), ("w_pg", R_PG, 128)):
        res = _adamw(w[name].reshape(rows, D), m[name].reshape(rows, D), v[name].reshape(rows, D), parts_r, "adamw_" + name,
                     part_row=row, block_rows=128)
        for dst, val in zip(outs, res):
            dst[name] = val.reshape(1, rows, D)
    res = _adamw(w["w_ple"].reshape(32, D), m["w_ple"].reshape(32, D), v["w_ple"].reshape(32, D), parts_r, "adamw_w_ple",
                 part_row=R_PLE, block_rows=32)
    for dst, val in zip(outs, res):
        dst["w_ple"] = val.reshape(1, PLE, 128)
    two_d = lambda d: {n: d[n].reshape(_SMALL_SHAPES[n]) for n in _SMALL_SHAPES}
    *res, g_cw_wide, loss = _adamw_small(two_d(w), two_d(m), two_d(v), wide_all, narrow_all, norm_all)
    for dst, val in zip(outs, res):
        dst.update({n: val[n].reshape(w[n].shape) for n in _SMALL_SHAPES})
    g_cw = lax.dynamic_slice_in_dim(g_cw_wide.reshape(4, CD), me * (CD // NDEV), CD // NDEV, axis=1).reshape(12, 128)
    res = _adamw(w["conv_w"].reshape(12, 128), m["conv_w"].reshape(12, 128), v["conv_w"].reshape(12, 128), [g_cw], "adamw_conv_w")
    for dst, val in zip((out_g, out_d, out_m, out_v), res):
        dst["conv_w"] = val.reshape(1, 4, CD // NDEV)

    return (loss, gx.reshape(1, T, D), *[out_g[n] for n in _WEIGHTS], *[out_d[n] for n in _WEIGHTS],
            *[out_m[n] for n in _WEIGHTS], *[out_v[n] for n in _WEIGHTS])
```

```python
import functools
import math
from typing import Callable, NamedTuple

import jax
import jax.numpy as jnp
from jax import lax
from jax.experimental import pallas as pl
from jax.experimental.pallas import tpu as pltpu

F32 = jnp.float32
BF16 = jnp.bfloat16
MESH = pl.DeviceIdType.MESH

D = 1024
DI = 2048
CD = 3072
NH = 32
HD = 64
NST = 128
NG = 4
CH = 128
PLE = 256
NIN = 10272
NDEV = 8
WSH = NIN // NDEV
EPS = 1e-6
OFF_XBC, OFF_U, OFF_ZB, OFF_V, OFF_ZA, OFF_GA, OFF_GB, OFF_DT = 0, 3072, 4096, 6144, 7168, 8192, 9216, 10240
NP = 10368
DTW = 128
R_OA, R_OB, R_OUT, R_PG, R_PLE, R_ROWS = 0, 128, 384, 512, 640, 672

ADAM_LR, ADAM_B1, ADAM_B2, ADAM_EPS, ADAM_WD, ADAM_STEP = 0.001, 0.9, 0.999, 1e-08, 0.01, 10

V7X_VMEM_LIMIT = 56 * 1024 * 1024


def _cp(sem=None):
    return pltpu.CompilerParams(dimension_semantics=sem, vmem_limit_bytes=V7X_VMEM_LIMIT)


def _dot(a, b, prec=None):
    return jnp.dot(a, b, preferred_element_type=F32, precision=prec)


def _dot_nt(a, b, prec=None):
    return lax.dot_general(a, b, (((1,), (1,)), ((), ())), preferred_element_type=F32, precision=prec)


def _dot_tn(a, b, prec=None):
    return lax.dot_general(a, b, (((0,), (0,)), ((), ())), preferred_element_type=F32, precision=prec)


def _sigmoid(x):
    return 1.0 / (1.0 + jnp.exp(-x))


def _gelu_and_grad(x):
    c = math.sqrt(2.0 / math.pi)
    x2 = x * x
    t = jnp.tanh(c * (x + 0.044715 * x * x2))
    g = 0.5 * x * (1.0 + t)
    dg = 0.5 * (1.0 + t) + 0.5 * x * (1.0 - t * t) * c * (1.0 + 3.0 * 0.044715 * x2)
    return g, dg


def _gelu(x):
    c = math.sqrt(2.0 / math.pi)
    return 0.5 * x * (1.0 + jnp.tanh(c * (x + 0.044715 * x * x * x)))


def _softplus(x):
    return jnp.maximum(x, 0.0) + jnp.log(1.0 + jnp.exp(-jnp.abs(x)))


def _full(shape):
    n = len(shape)
    return pl.BlockSpec(shape, lambda *_: (0,) * n)


_ANY = pl.BlockSpec(memory_space=pl.ANY)


class _Exchange(NamedTuple):
    arrays: list
    out_shape: list
    sems: list
    make: Callable


def _call(body, ex, *, name, grid, in_specs, out_specs, out_shape, scratch_shapes, args):
    ki, ko, ks = len(in_specs), len(out_specs), len(scratch_shapes)
    ei, eo = len(ex.arrays), len(ex.out_shape)
    last = [g - 1 for g in grid]

    def full_body(*refs):
        r = list(refs)
        ins, eins, r = r[:ki], r[ki:ki + ei], r[ki + ei:]
        outs, eouts, r = r[:ko], r[ko:ko + eo], r[ko + eo:]
        scr, esems = r[:ks], r[ks:]
        start, finish = ex.make(eins, eouts, esems)
        ids = [pl.program_id(a) for a in range(len(grid))]
        is_first = functools.reduce(lambda p, q: p & q, [i == 0 for i in ids])
        is_last = functools.reduce(lambda p, q: p & q, [i == l for i, l in zip(ids, last)])
        pl.when(is_first)(start)
        body(*ins, *outs, *scr)
        pl.when(is_last)(finish)

    res = pl.pallas_call(
        full_body, name=name, grid=grid, in_specs=list(in_specs) + [_ANY] * ei, out_specs=list(out_specs) + [_ANY] * eo,
        out_shape=list(out_shape) + list(ex.out_shape), scratch_shapes=list(scratch_shapes) + list(ex.sems),
        compiler_params=_cp(("arbitrary",) * len(grid)),
    )(*args, *ex.arrays)
    return res[:ko], res[ko:]


def _run_exchange(ex, name):
    ni, no = len(ex.arrays), len(ex.out_shape)

    def body(*refs):
        start, finish = ex.make(refs[:ni], refs[ni:ni + no], refs[ni + no:])
        start()
        finish()

    return pl.pallas_call(body, name=name, in_specs=[_ANY] * ni, out_specs=[_ANY] * no, out_shape=list(ex.out_shape),
                          scratch_shapes=list(ex.sems))(*ex.arrays)


def _proj_fwd(x, norm_g, wp, ex):
    T = x.shape[0]
    tm, tn = min(T, 1024), 2048
    nj = OFF_DT // tn
    assert OFF_DT % tn == 0 and OFF_DT + DTW == NP

    def body(x_ref, g_ref, w_ref, wdt_ref, proj_ref, dt_ref, h_ref, hs_ref):
        j = pl.program_id(1)

        @pl.when(j == 0)
        def _():
            xf = x_ref[...]
            r = lax.rsqrt(jnp.mean(xf * xf, axis=-1, keepdims=True) + EPS)
            h = (xf * r * g_ref[...]).astype(BF16)
            hs_ref[...] = h
            h_ref[...] = h

        proj_ref[...] = _dot_nt(hs_ref[...], w_ref[...]).astype(BF16)

        @pl.when(j == nj - 1)
        def _():
            dt_ref[...] = _dot_nt(hs_ref[...], wdt_ref[...])

    return _call(
        body, ex, name="proj_fwd", grid=(T // tm, nj),
        in_specs=[pl.BlockSpec((tm, D), lambda i, j: (i, 0)), _full((1, D)), pl.BlockSpec((tn, D), lambda i, j: (j, 0)),
                  pl.BlockSpec((DTW, D), lambda i, j: (OFF_DT // DTW, 0))],
        out_specs=[pl.BlockSpec((tm, tn), lambda i, j: (i, j)), pl.BlockSpec((tm, DTW), lambda i, j: (i, 0)),
                   pl.BlockSpec((tm, D), lambda i, j: (i, 0))],
        out_shape=[jax.ShapeDtypeStruct((T, OFF_DT), BF16), jax.ShapeDtypeStruct((T, DTW), F32), jax.ShapeDtypeStruct((T, D), BF16)],
        scratch_shapes=[pltpu.VMEM((tm, D), BF16)], args=(x, norm_g, wp, wp))


def _gmlp_tile():
    return 256


def _gmlp_fwd_tile(u_ref, v_ref, z_ref, lg_ref, lb_ref, ws_ref, bs_ref, ya_ref, vn_s):
    tm = u_ref.shape[0]
    vg = _gelu(v_ref[...].astype(F32))
    mu = jnp.mean(vg, axis=-1, keepdims=True)
    xc = vg - mu
    rstd = lax.rsqrt(jnp.mean(xc * xc, axis=-1, keepdims=True) + EPS)
    vn_s[...] = (xc * rstd * lg_ref[...] + lb_ref[...]).astype(BF16)
    for c in range(tm // CH):
        rs = slice(c * CH, (c + 1) * CH)
        for g in range(NG):
            cs_ = slice(g * 256, (g + 1) * 256)
            sv = _dot(ws_ref[g], vn_s[rs, cs_]) + bs_ref[g]
            z = z_ref[rs, cs_].astype(F32)
            ya_ref[rs, cs_] = (_gelu(u_ref[rs, cs_].astype(F32)) * sv * (z * _sigmoid(z))).astype(BF16)


def _gmlp_bwd(proj, dya, ln_g, ln_b, ws, wst, bst, gsel, ex):
    T = proj.shape[0]
    tm = min(T, _gmlp_tile())

    def body(u_ref, v_ref, z_ref, dy_ref, lg_ref, lb_ref, ws_ref, wst_ref, bs_ref, gsel_ref,
             d_ref, dws_ref, dbs_ref, dln_ref, vn_s, dsv_s, dvn_s):
        du_ref, dv_ref, dz_ref = d_ref.at[:, 0:D], d_ref.at[:, D:2 * D], d_ref.at[:, 2 * D:3 * D]
        @pl.when(pl.program_id(0) == 0)
        def _():
            dws_ref[...] = jnp.zeros_like(dws_ref)
            dbs_ref[...] = jnp.zeros_like(dbs_ref)
            dln_ref[...] = jnp.zeros_like(dln_ref)

        vg, dvg_dv = _gelu_and_grad(v_ref[...].astype(F32))
        mu = jnp.mean(vg, axis=-1, keepdims=True)
        xc = vg - mu
        rstd = lax.rsqrt(jnp.mean(xc * xc, axis=-1, keepdims=True) + EPS)
        vhat = xc * rstd
        vn_s[...] = (vhat * lg_ref[...] + lb_ref[...]).astype(BF16)
        ri = lax.broadcasted_iota(jnp.int32, (CH, CH), 0)
        ci = lax.broadcasted_iota(jnp.int32, (CH, CH), 1)
        tril = (ri >= ci).astype(F32)
        for c in range(tm // CH):
            rs = slice(c * CH, (c + 1) * CH)
            for g in range(NG):
                cs_ = slice(g * 256, (g + 1) * 256)
                vn = vn_s[rs, cs_]
                sv = _dot(ws_ref[g], vn) + bs_ref[g]
                z = z_ref[rs, cs_].astype(F32)
                sz = _sigmoid(z)
                ug, dug_du = _gelu_and_grad(u_ref[rs, cs_].astype(F32))
                dy = dy_ref[rs, cs_].astype(F32)
                zs = z * sz
                t = dy * zs
                dsv_f = dy * sv
                du_ref[rs, cs_] = (zs * dsv_f * dug_du).astype(BF16)
                dz_ref[rs, cs_] = (dsv_f * ug * (sz + zs * (1.0 - sz))).astype(BF16)
                dsv = (t * ug).astype(BF16)
                dsv_s[rs, cs_] = dsv
                dvn_s[rs, cs_] = _dot(wst_ref[g], dsv)
                dws_ref[g] += _dot_nt(dsv, vn) * tril
            dbs_ref[...] += _dot(dsv_s[rs, :], gsel_ref[...])
        dvn = dvn_s[...]
        dln_ref[0:1, :] += jnp.sum(dvn * vhat, axis=0, keepdims=True)
        dln_ref[1:2, :] += jnp.sum(dvn, axis=0, keepdims=True)
        dvh = dvn * lg_ref[...]
        dvg = rstd * (dvh - jnp.mean(dvh, axis=-1, keepdims=True) - vhat * jnp.mean(dvh * vhat, axis=-1, keepdims=True))
        dv_ref[...] = (dvg * dvg_dv).astype(BF16)

    blk = lambda off: pl.BlockSpec((tm, D), lambda i: (i, off // D))
    row = pl.BlockSpec((tm, D), lambda i: (i, 0))
    return _call(
        body, ex, name="gmlp_bwd", grid=(T // tm,),
        in_specs=[blk(OFF_U), blk(OFF_V), blk(OFF_ZA), row, _full((1, D)), _full((1, D)), _full((NG, CH, CH)),
                  _full((NG, CH, CH)), _full((NG, CH, 256)), _full((D, 128))],
        out_specs=[pl.BlockSpec((tm, 3 * D), lambda i: (i, 0)), _full((NG, CH, CH)), _full((CH, 128)), _full((8, D))],
        out_shape=[jax.ShapeDtypeStruct((T, 3 * D), BF16),
                   jax.ShapeDtypeStruct((NG, CH, CH), F32), jax.ShapeDtypeStruct((CH, 128), F32), jax.ShapeDtypeStruct((8, D), F32)],
        scratch_shapes=[pltpu.VMEM((tm, D), BF16), pltpu.VMEM((tm, D), BF16), pltpu.VMEM((tm, D), F32)],
        args=(proj, proj, proj, dya, ln_g, ln_b, ws, wst, bst, gsel))


def _shift_matrix(down):
    t = jnp.arange(CH)[None, :, None]
    j = jnp.arange(1, 4)[:, None, None]
    col = jnp.arange(2 * CH)[None, None, :]
    src = CH + t - j if down else t + j
    return (col == src).astype(BF16).reshape(3 * CH, 2 * CH)


def _conv_pre(x, moved, cw_ref, cb_ref):
    pre = cb_ref[...] + cw_ref[3:4, :] * x
    for j in (1, 2, 3):
        pre = pre + cw_ref[3 - j:4 - j, :] * moved[(j - 1) * CH:j * CH]
    return pre


def _split_dot(x, w, parts, w_left=False):
    acc, r = None, x
    for k in range(parts):
        hi = r.astype(BF16)
        d = _dot(w, hi) if w_left else _dot(hi, w)
        acc = d if acc is None else acc + d
        if k + 1 < parts:
            r = r - hi.astype(F32)
    return acc


def _chunk_decays(dt, alog_ref, e_ref, cs_s, cst_s, csx_s):
    a = -jnp.exp(alog_ref[...])
    ri = lax.broadcasted_iota(jnp.int32, (CH, CH), 0)
    ci = lax.broadcasted_iota(jnp.int32, (CH, CH), 1)
    tril = ri >= ci
    cs = _split_dot(dt * a, tril.astype(BF16), 3, w_left=True)
    cs_s[...] = cs
    cst_s[...] = cs.T
    csx_s[...] = _split_dot(cs, e_ref[...], 3)
    return a, tril, ri, ci


def _lmat(cst_s, h, tril):
    rowb = jnp.broadcast_to(cst_s[h:h + 1, :], (CH, CH))
    return jnp.exp(jnp.where(tril, rowb.T - rowb, -jnp.inf))


def _head_pair_rows(v, lane):
    return jnp.concatenate([jnp.where(lane < HD, v, 0.0), jnp.where(lane < HD, 0.0, v)], axis=0).astype(BF16)


def _ssd_fwd(proj, dtr, cw, cb, dtb, alog, dskx, sg, e128):
    T = proj.shape[0]
    nc = T // CH

    def body(xbc_ref, zb_ref, dt_ref, cw_ref, cb_ref, dtb_ref, alog_ref, dx_ref, sg_ref, e_ref, shift_ref,
             y_ref, yb_ref, hp_ref, pre_ref, xx_s, h_s, cs_s, cst_s, csx_s, yz_s):
        @pl.when(pl.program_id(0) == 0)
        def _():
            xx_s[...] = jnp.zeros_like(xx_s)
            h_s[...] = jnp.zeros_like(h_s)

        xx_s[CH:, :] = xbc_ref[...]
        moved = _dot(shift_ref[...], xx_s[...])
        xx_s[CH - 16:CH, :] = xbc_ref[CH - 16:, :]
        x = xbc_ref[...].astype(F32)
        pre = _conv_pre(x, moved, cw_ref, cb_ref)
        pre_ref[...] = pre
        xc = pre * _sigmoid(pre)
        dt = _softplus(dt_ref[...] + dtb_ref[...])
        a, tril, _, lane = _chunk_decays(dt, alog_ref, e_ref, cs_s, cst_s, csx_s)
        dt_x = _split_dot(dt, e_ref[...], 2)
        cs_last_x = csx_s[CH - 1:CH, :]
        hp_ref[0] = h_s[...]
        for g in range(NG):
            gs = slice(g * 512, (g + 1) * 512)
            bg = xc[:, DI + g * NST:DI + (g + 1) * NST].astype(BF16)
            cg = xc[:, DI + 512 + g * NST:DI + 512 + (g + 1) * NST].astype(BF16)
            cbm = _dot_nt(cg, bg)
            xg = xc[:, gs]
            xdt = xg * dt_x[:, gs]
            hprev = h_s[:, gs]
            csx = csx_s[:, gs]
            yoff = _dot(cg, hprev.astype(BF16)) * jnp.exp(csx)
            st = _dot_tn(bg, (xdt * jnp.exp(cs_last_x[:, gs] - csx)).astype(BF16))
            h_s[:, gs] = jnp.exp(cs_last_x[:, gs]) * hprev + st
            ssq = jnp.zeros((CH, 1), F32)
            for q in range(4):
                h0 = g * 8 + 2 * q
                ps = slice(q * 128, (q + 1) * 128)
                cols = slice(g * 512 + q * 128, g * 512 + (q + 1) * 128)
                m01 = jnp.concatenate([cbm * _lmat(cst_s, h0, tril), cbm * _lmat(cst_s, h0 + 1, tril)], axis=1).astype(BF16)
                yq = _dot(m01, _head_pair_rows(xdt[:, ps], lane)) + yoff[:, ps] + xg[:, ps] * dx_ref[:, cols]
                y_ref[:, cols] = yq
                z = zb_ref[:, cols].astype(F32)
                yz = yq * z * _sigmoid(z)
                yz_s[:, cols] = yz
                ssq = ssq + jnp.sum(yz * yz, axis=1, keepdims=True)
            rg = lax.rsqrt(ssq * (1.0 / 512.0) + EPS)
            yb_ref[:, gs] = (yz_s[:, gs] * rg * sg_ref[:, gs]).astype(BF16)

    return pl.pallas_call(
        body, name="ssd_fwd", grid=(nc,),
        in_specs=[pl.BlockSpec((CH, CD), lambda c: (c, OFF_XBC // CD)), pl.BlockSpec((CH, DI), lambda c: (c, OFF_ZB // DI)),
                  pl.BlockSpec((CH, DTW), lambda c: (c, 0)), _full((4, CD)), _full((1, CD)), _full((1, DTW)),
                  _full((1, DTW)), _full((1, DI)), _full((1, DI)), _full((DTW, DI)), _full((3 * CH, 2 * CH))],
        out_specs=[pl.BlockSpec((CH, DI), lambda c: (c, 0)), pl.BlockSpec((CH, DI), lambda c: (c, 0)),
                   pl.BlockSpec((1, NST, DI), lambda c: (c, 0, 0)), pl.BlockSpec((CH, CD), lambda c: (c, 0))],
        out_shape=[jax.ShapeDtypeStruct((T, DI), F32), jax.ShapeDtypeStruct((T, DI), BF16),
                   jax.ShapeDtypeStruct((nc, NST, DI), F32), jax.ShapeDtypeStruct((T, CD), F32)],
        scratch_shapes=[pltpu.VMEM((2 * CH, CD), BF16), pltpu.VMEM((NST, DI), F32), pltpu.VMEM((CH, CH), F32),
                        pltpu.VMEM((CH, CH), F32), pltpu.VMEM((CH, DI), F32), pltpu.VMEM((CH, DI), F32)],
        compiler_params=_cp(("arbitrary",)),
    )(proj, proj, dtr, cw, cb, dtb, alog, dskx, sg, e128, _shift_matrix(down=True))


def _ssd_bwd(proj, pre_all, dtr, y, dyb, hprev_all, cw, dtb, alog, dskx, sg, e128, et128, ex):
    T = proj.shape[0]
    nc = T // CH

    def body(xbc_ref, pre_ref, zb_ref, dt_ref, y_ref, dyb_ref, hp_ref, cw_ref, dtb_ref, alog_ref, dx_ref, sg_ref,
             e_ref, et_ref, shift_ref, d_ref, ddt_ref, dcw_ref, dsm_ref, dsg_ref,
             g_s, dd_s, cs_s, cst_s, csx_s, dy_s, dxdt_s, dxs_s, dsd_s, dxc_s, gh_s):
        dxbc_ref, dzb_ref = d_ref.at[:, 0:CD], d_ref.at[:, CD:CD + DI]
        i = pl.program_id(0)

        @pl.when(i == 0)
        def _():
            g_s[...] = jnp.zeros_like(g_s)
            dd_s[...] = jnp.zeros_like(dd_s)
            dcw_ref[...] = jnp.zeros_like(dcw_ref)
            dsm_ref[...] = jnp.zeros_like(dsm_ref)
            dsg_ref[...] = jnp.zeros_like(dsg_ref)

        pre = pre_ref[...]
        sp = _sigmoid(pre)
        xc = pre * sp
        dtr = dt_ref[...] + dtb_ref[...]
        dt = _softplus(dtr)
        a, tril, ri, lane = _chunk_decays(dt, alog_ref, e_ref, cs_s, cst_s, csx_s)
        et = et_ref[...]
        dt_x = _split_dot(dt, e_ref[...], 2)
        cs_last_x = csx_s[CH - 1:CH, :]

        for g in range(NG):
            gs = slice(g * 512, (g + 1) * 512)
            z = zb_ref[:, gs].astype(F32)
            sz = _sigmoid(z)
            yv = y_ref[:, gs]
            zs = z * sz
            yz = yv * zs
            rg = lax.rsqrt(jnp.mean(yz * yz, axis=-1, keepdims=True) + EPS)
            yn = yz * rg
            dyb = dyb_ref[:, gs].astype(F32)
            dsg_ref[0:1, gs] += jnp.sum(dyb * yn, axis=0, keepdims=True)
            dyn = dyb * sg_ref[:, gs]
            dyz = rg * (dyn - yn * jnp.mean(dyn * yn, axis=-1, keepdims=True))
            dy_s[:, gs] = dyz * zs
            dzb_ref[:, gs] = (dyz * yv * (sz + zs * (1.0 - sz))).astype(BF16)

        rsum = jnp.zeros((CH, DTW), F32)
        csum_t = jnp.zeros((DTW, CH), F32)
        for g in range(NG):
            gs = slice(g * 512, (g + 1) * 512)
            bg = xc[:, DI + g * NST:DI + (g + 1) * NST].astype(BF16)
            cg = xc[:, DI + 512 + g * NST:DI + 512 + (g + 1) * NST].astype(BF16)
            cbm = _dot_nt(cg, bg)
            xdt = xc[:, gs] * dt_x[:, gs]
            hprev = hp_ref[0, :, gs]
            hpb = hprev.astype(BF16)
            gn = g_s[:, gs]
            gnb = gn.astype(BF16)
            dy = dy_s[:, gs]
            csx = csx_s[:, gs]
            ecs = jnp.exp(csx)
            dec = jnp.exp(cs_last_x[:, gs] - csx)
            dye = (dy * ecs).astype(BF16)
            dc = _dot_nt(dye, hpb)
            dprev = _dot_tn(cg, dye)
            dxdt_state = dec * _dot(bg, gnb)
            db = _dot_nt((xdt * dec).astype(BF16), gnb)
            dcb = jnp.zeros((CH, CH), F32)
            for q in range(4):
                h0 = g * 8 + 2 * q
                ps = slice(q * 128, (q + 1) * 128)
                dyp = dy[:, ps]
                l0 = _lmat(cst_s, h0, tril)
                l1 = _lmat(cst_s, h0 + 1, tril)
                m0 = cbm * l0
                m1 = cbm * l1
                dm = _dot_nt(dyp.astype(BF16), _head_pair_rows(xdt[:, ps], lane))
                dm0 = dm[:, :CH]
                dm1 = dm[:, CH:]
                dcb = dcb + dm0 * l0 + dm1 * l1
                for hh, qm in ((h0, dm0 * m0), (h0 + 1, dm1 * m1)):
                    rsum = jnp.where(lane == hh, jnp.sum(qm, axis=1, keepdims=True), rsum)
                    csum_t = jnp.where(ri == hh, jnp.sum(qm, axis=0, keepdims=True), csum_t)
                mst = jnp.concatenate([m0, m1], axis=0).astype(BF16)
                d = _dot_tn(mst, _head_pair_rows(dyp, lane))
                dxdt_s[:, g * 512 + q * 128:g * 512 + (q + 1) * 128] = d + dxdt_state[:, ps]
            yoff = _dot(cg, hpb) * ecs
            dsd_s[:, gs] = xdt * dxdt_state
            dxs_s[:, gs] = dy * yoff
            dcbb = dcb.astype(BF16)
            dxc_s[:, DI + 512 + g * NST:DI + 512 + (g + 1) * NST] = dc + _dot(dcbb, bg)
            dxc_s[:, DI + g * NST:DI + (g + 1) * NST] = db + _dot_tn(dcbb, cg)
            gh_s[:, gs] = jnp.broadcast_to(jnp.sum(gn * hprev, axis=0, keepdims=True), (8, 512))
            g_s[:, gs] = dprev + jnp.exp(cs_last_x[:, gs]) * gn

        xs = xc[:, :DI]
        dy = dy_s[...]
        dxdt = dxdt_s[...]
        cs_last = cs_s[CH - 1:CH, :]
        state_e = _split_dot(dsd_s[...], et, 2)
        dcd = 0.125 * jnp.sum(_split_dot(gh_s[...], et, 2), axis=0, keepdims=True) * jnp.exp(cs_last)
        row = lax.broadcasted_iota(jnp.int32, (CH, 1), 0)
        dcs = rsum - csum_t.T + _split_dot(dxs_s[...], et, 2) - state_e
        dcs = dcs + jnp.where(row == CH - 1, jnp.sum(state_e, axis=0, keepdims=True) + dcd, 0.0)
        dda = _split_dot(dcs, (lane >= ri).astype(BF16), 3, w_left=True)
        ddt = dda * a + _dot((dxdt * xs).astype(BF16), et)
        ddtr = jnp.where(lane < NH, ddt * _sigmoid(dtr), 0.0)
        ddt_ref[...] = ddtr.astype(BF16)
        dsm_ref[0:1, :] += jnp.sum(ddtr, axis=0, keepdims=True)
        dsm_ref[1:2, :] += jnp.sum(dda * dt, axis=0, keepdims=True) * a
        dsm_ref[2:3, :] += jnp.sum(_dot((dy * xs).astype(BF16), et), axis=0, keepdims=True)
        dxc_s[:, :DI] = dxdt * dt_x + dy * dx_ref[...]

        dpre = dxc_s[...] * (sp + xc * (1.0 - sp))
        dpre_b = dpre.astype(BF16)
        dd_s[:CH, :] = dpre_b
        moved = _dot(shift_ref[...], dd_s[...])
        dd_s[CH:CH + 16, :] = dpre_b[:16]
        x = xbc_ref[...].astype(F32)
        dcw_ref[4:5, :] += jnp.sum(dpre, axis=0, keepdims=True)
        dxbc = cw_ref[3:4, :] * dpre
        dcw_ref[3:4, :] += jnp.sum(dpre * x, axis=0, keepdims=True)
        for j in (1, 2, 3):
            ahead = moved[(j - 1) * CH:j * CH]
            dcw_ref[3 - j:4 - j, :] += jnp.sum(ahead * x, axis=0, keepdims=True)
            dxbc = dxbc + cw_ref[3 - j:4 - j, :] * ahead
        dxbc_ref[...] = dxbc.astype(BF16)

    rev = lambda c: nc - 1 - c
    return _call(
        body, ex, name="ssd_bwd", grid=(nc,),
        in_specs=[pl.BlockSpec((CH, CD), lambda c: (rev(c), OFF_XBC // CD)),
                  pl.BlockSpec((CH, CD), lambda c: (rev(c), 0)),
                  pl.BlockSpec((CH, DI), lambda c: (rev(c), OFF_ZB // DI)),
                  pl.BlockSpec((CH, DTW), lambda c: (rev(c), 0)),
                  pl.BlockSpec((CH, DI), lambda c: (rev(c), 0)), pl.BlockSpec((CH, DI), lambda c: (rev(c), 0)),
                  pl.BlockSpec((1, NST, DI), lambda c: (rev(c), 0, 0)),
                  _full((4, CD)), _full((1, DTW)), _full((1, DTW)), _full((1, DI)), _full((1, DI)),
                  _full((DTW, DI)), _full((DI, DTW)), _full((3 * CH, 2 * CH))],
        out_specs=[pl.BlockSpec((CH, CD + DI), lambda c: (rev(c), 0)),
                   pl.BlockSpec((CH, DTW), lambda c: (rev(c), 0)), _full((8, CD)), _full((8, DTW)), _full((8, DI))],
        out_shape=[jax.ShapeDtypeStruct((T, CD + DI), BF16), jax.ShapeDtypeStruct((T, DTW), BF16),
                   jax.ShapeDtypeStruct((8, CD), F32), jax.ShapeDtypeStruct((8, DTW), F32), jax.ShapeDtypeStruct((8, DI), F32)],
        scratch_shapes=[pltpu.VMEM((NST, DI), F32), pltpu.VMEM((2 * CH, CD), BF16), pltpu.VMEM((CH, CH), F32), pltpu.VMEM((CH, CH), F32),
                        pltpu.VMEM((CH, DI), F32), pltpu.VMEM((CH, DI), F32), pltpu.VMEM((CH, DI), F32), pltpu.VMEM((CH, DI), F32),
                        pltpu.VMEM((CH, DI), F32), pltpu.VMEM((CH, CD), F32), pltpu.VMEM((8, DI), F32)],
        args=(proj, pre_all, proj, dtr, y, dyb, hprev_all, cw, dtb, alog, dskx, sg, e128, et128, _shift_matrix(down=False)))


def _merge_tile():
    return 256


def _merge(x, yb, proj, p, tgt, gmlp, w_oa, w_ob, w_out, w_pg, w_ple, ple_g, fin_g):
    T = x.shape[0]
    tm = min(T, _merge_tile())

    def body(x_ref, u_ref, v_ref, za_ref, yb_ref, ga_ref, gb_ref, p_ref, t_ref, lg_ref, lb_ref, ws_ref, bs_ref,
             woa, wob, wout, wpg, wple, pg_ref, fg_ref,
             dx1_ref, dx1b_ref, ya_ref, mg_ref, hp_ref, dpre_ref, dpe_ref, doa_ref, dob_ref, dya_ref, dyb_ref, dg_ref, acc_ref, vn_s):
        @pl.when(pl.program_id(0) == 0)
        def _():
            acc_ref[...] = jnp.zeros_like(acc_ref)

        _gmlp_fwd_tile(u_ref, v_ref, za_ref, lg_ref, lb_ref, ws_ref, bs_ref, ya_ref, vn_s)
        oa = _dot(ya_ref[...], woa[...])
        ob = _dot(yb_ref[...], wob[...])
        sa = _sigmoid(ga_ref[...].astype(F32))
        sb = _sigmoid(gb_ref[...].astype(F32))
        mg = sa * oa + sb * ob
        mgb = mg.astype(BF16)
        mg_ref[...] = mgb
        x1 = x_ref[...] + _dot(mgb, wout[...])
        r2 = lax.rsqrt(jnp.mean(x1 * x1, axis=-1, keepdims=True) + EPS)
        xh1 = x1 * r2
        hpb = (xh1 * pg_ref[...]).astype(BF16)
        hp_ref[...] = hpb
        gate = _sigmoid(_dot(hpb, wpg[...]))
        pe = _dot(p_ref[...].astype(BF16), wple[...])
        x2 = x1 + gate * pe
        r3 = lax.rsqrt(jnp.mean(x2 * x2, axis=-1, keepdims=True) + EPS)
        xh2 = x2 * r3
        err = xh2 * fg_ref[...] - t_ref[...]
        acc_ref[2:3, :] += 0.5 * jnp.sum(jnp.mean(err * err, axis=-1, keepdims=True))
        dyo = err * (1.0 / D)
        acc_ref[0:1, :] += jnp.sum(dyo * xh2, axis=0, keepdims=True)
        dn = dyo * fg_ref[...]
        dx2 = r3 * (dn - xh2 * jnp.mean(dn * xh2, axis=-1, keepdims=True))
        dpe_ref[...] = (dx2 * gate).astype(BF16)
        dpre = (dx2 * pe * gate * (1.0 - gate)).astype(BF16)
        dpre_ref[...] = dpre
        dhp = _dot_nt(dpre, wpg[...])
        acc_ref[1:2, :] += jnp.sum(dhp * xh1, axis=0, keepdims=True)
        dhn = dhp * pg_ref[...]
        dx1 = dx2 + r2 * (dhn - xh1 * jnp.mean(dhn * xh1, axis=-1, keepdims=True))
        dx1_ref[...] = dx1
        dx1b = dx1.astype(BF16)
        dx1b_ref[...] = dx1b
        dmg = _dot_nt(dx1b, wout[...])
        doa = (dmg * sa).astype(BF16)
        dob = (dmg * sb).astype(BF16)
        doa_ref[...] = doa
        dob_ref[...] = dob
        dg_ref[:, :D] = (dmg * oa * sa * (1.0 - sa)).astype(BF16)
        dg_ref[:, D:] = (dmg * ob * sb * (1.0 - sb)).astype(BF16)
        dya_ref[...] = _dot_nt(doa, woa[...]).astype(BF16)
        dyb_ref[...] = _dot_nt(dob, wob[...]).astype(BF16)

    row = lambda w: pl.BlockSpec((tm, w), lambda i: (i, 0))
    blk = lambda off: pl.BlockSpec((tm, D), lambda i: (i, off // D))
    wsp = lambda s: pl.BlockSpec(s, lambda i: (0, 0), pipeline_mode=pl.Buffered(1))
    return pl.pallas_call(
        body, name="merge", grid=(T // tm,),
        in_specs=[row(D), blk(OFF_U), blk(OFF_V), blk(OFF_ZA), row(DI), blk(OFF_GA), blk(OFF_GB), row(PLE), row(D),
                  _full((1, D)), _full((1, D)), _full((NG, CH, CH)), _full((NG, CH, 256)),
                  wsp((D, D)), wsp((DI, D)), wsp((D, D)), wsp((D, D)), wsp((PLE, D)), _full((1, D)), _full((1, D))],
        out_specs=[row(D)] * 10 + [row(DI), row(2 * D), _full((8, D))],
        out_shape=[jax.ShapeDtypeStruct((T, D), F32)] + [jax.ShapeDtypeStruct((T, D), BF16)] * 9
        + [jax.ShapeDtypeStruct((T, DI), BF16), jax.ShapeDtypeStruct((T, 2 * D), BF16), jax.ShapeDtypeStruct((8, D), F32)],
        scratch_shapes=[pltpu.VMEM((tm, D), BF16)],
        compiler_params=_cp(("arbitrary",)),
    )(x, proj, proj, proj, yb, proj, proj, p, tgt, *gmlp, w_oa, w_ob, w_out, w_pg, w_ple, ple_g, fin_g)


def _wgrad(a, b, name):
    T, K = a.shape
    N = b.shape[1]
    tt, tk, tn = min(T, 2048), min(K, 1024), min(N, 1024)
    nt = T // tt

    def body(a_ref, b_ref, o_ref, acc_s):
        t = pl.program_id(2)

        @pl.when(t == 0)
        def _():
            acc_s[...] = jnp.zeros_like(acc_s)

        acc_s[...] += _dot_tn(a_ref[...].astype(BF16), b_ref[...])

        @pl.when(t == nt - 1)
        def _():
            o_ref[...] = acc_s[...].astype(BF16)

    return pl.pallas_call(
        body, name=name, grid=(K // tk, N // tn, nt),
        in_specs=[pl.BlockSpec((tt, tk), lambda k, n, t: (t, k)), pl.BlockSpec((tt, tn), lambda k, n, t: (t, n))],
        out_specs=pl.BlockSpec((tk, tn), lambda k, n, t: (k, n)),
        out_shape=jax.ShapeDtypeStruct((K, N), BF16),
        scratch_shapes=[pltpu.VMEM((tk, tn), F32)],
        compiler_params=_cp(("parallel", "parallel", "arbitrary")),
    )(a, b)


def _proj_bwd(x, dx1, norm_g, wt, pieces, ddt, ex):
    T = x.shape[0]
    tm = min(T, 1024)
    nk = OFF_DT // D + 1
    starts = [sum(a.shape[1] for a in pieces[:n]) // D for n in range(len(pieces))]
    ranges = [(s, s + a.shape[1] // D) for s, a in zip(starts, pieces)]
    assert ranges[-1][1] == nk - 1
    npc = len(pieces)

    def body(x_ref, dx1_ref, g_ref, w_ref, wdt_ref, *rest):
        piece_refs, ddt_ref, gx_ref, dng_ref, acc_s = rest[:npc], rest[npc], rest[npc + 1], rest[npc + 2], rest[npc + 3]
        i, k = pl.program_id(0), pl.program_id(1)

        @pl.when((i == 0) & (k == 0))
        def _():
            dng_ref[...] = jnp.zeros_like(dng_ref)

        @pl.when(k == 0)
        def _():
            acc_s[...] = jnp.zeros_like(acc_s)

        for ref, (lo, hi) in zip(piece_refs, ranges):
            @pl.when((k >= lo) & (k < hi))
            def _(ref=ref):
                acc_s[...] += _dot(ref[...], w_ref[...])

        @pl.when(k == nk - 1)
        def _():
            dh = acc_s[...] + _dot(ddt_ref[...], wdt_ref[...])
            xf = x_ref[...]
            r = lax.rsqrt(jnp.mean(xf * xf, axis=-1, keepdims=True) + EPS)
            xh = xf * r
            dng_ref[0:1, :] += jnp.sum(dh * xh, axis=0, keepdims=True)
            dxn = dh * g_ref[...]
            gx_ref[...] = dx1_ref[...] + r * (dxn - xh * jnp.mean(dxn * xh, axis=-1, keepdims=True))

    def piece_spec(lo, hi):
        return pl.BlockSpec((tm, D), lambda i, k: (i, jnp.clip(k - lo, 0, hi - lo - 1)))

    row = pl.BlockSpec((tm, D), lambda i, k: (i, 0))
    return _call(
        body, ex, name="proj_bwd", grid=(T // tm, nk),
        in_specs=[row, row, _full((1, D)), pl.BlockSpec((D, D), lambda i, k: (jnp.minimum(k, nk - 2), 0)),
                  pl.BlockSpec((DTW, D), lambda i, k: (OFF_DT // DTW, 0))]
        + [piece_spec(lo, hi) for lo, hi in ranges] + [pl.BlockSpec((tm, DTW), lambda i, k: (i, 0))],
        out_specs=[row, _full((8, D))],
        out_shape=[jax.ShapeDtypeStruct((T, D), F32), jax.ShapeDtypeStruct((8, D), F32)],
        scratch_shapes=[pltpu.VMEM((tm, D), F32)],
        args=(x, dx1, norm_g, wt, wt, *pieces, ddt))


def _elementwise_tile(R, C, limit=1 << 20):
    if R * C * 4 <= limit:
        return R, C
    rows = [t for t in range(16, R, 16) if R % t == 0 and t * C * 4 <= limit]
    if rows:
        return rows[-1], C
    cols = [t for t in range(128, C, 128) if C % t == 0 and R * t * 4 <= limit]
    return R, cols[-1]


def _adam_update(w, m, v, g):
    c1 = 1.0 - ADAM_B1 ** ADAM_STEP
    c2 = 1.0 - ADAM_B2 ** ADAM_STEP
    mm = ADAM_B1 * m + (1.0 - ADAM_B1) * g
    vv = ADAM_B2 * v + (1.0 - ADAM_B2) * (g * g)
    return -ADAM_LR * ((mm / c1) / (jnp.sqrt(vv / c2) + ADAM_EPS) + ADAM_WD * w), mm, vv


def _sum_parts(parts, name):
    first = parts[0][0] if isinstance(parts[0], tuple) else parts[0]
    R, C = first.shape[-2:]
    tr, tc = _elementwise_tile(R, C)
    flat = pl.BlockSpec((tr, tc), lambda i, j: (i, j))
    slot = lambda k: pl.BlockSpec((None, tr, tc), lambda i, j: (k, i, j))
    specs = [slot(p[1]) if isinstance(p, tuple) else flat for p in parts]
    arrays = [p[0] if isinstance(p, tuple) else p for p in parts]

    def body(*refs):
        g = refs[0][...].astype(F32)
        for r in refs[1:-1]:
            g = g + r[...].astype(F32)
        refs[-1][...] = g

    return pl.pallas_call(body, name=name, grid=(R // tr, C // tc), in_specs=specs, out_specs=flat,
                          out_shape=jax.ShapeDtypeStruct((R, C), F32), compiler_params=_cp(("parallel", "parallel")))(*arrays)


def _adamw_tiles(w, m, v, g, name):
    R = w.shape[0]
    tr = 107
    assert R % tr == 0
    spec = pl.BlockSpec((tr, 8, 128), lambda i: (i, 0, 0))

    def body(w_ref, m_ref, v_ref, g_in, g_ref, d_ref, nm_ref, nv_ref):
        g = g_in[...]
        g_ref[...] = g
        d_ref[...], nm_ref[...], nv_ref[...] = _adam_update(w_ref[...], m_ref[...], v_ref[...], g)

    return pl.pallas_call(body, name=name, grid=(R // tr,), in_specs=[spec] * 4, out_specs=[spec] * 4,
                          out_shape=[jax.ShapeDtypeStruct(w.shape, F32)] * 4, compiler_params=_cp(("parallel",)))(w, m, v, g)


def _adamw(w, m, v, parts, name, part_row=0, block_rows=None):
    R, C = w.shape
    tr, tc = _elementwise_tile(R, C) if block_rows is None else (block_rows, C)
    assert R % tr == 0 and part_row % tr == 0
    first = part_row // tr
    n = len(parts)
    wspec = pl.BlockSpec((tr, tc), lambda i, j: (i, j))
    flat = pl.BlockSpec((tr, tc), lambda i, j: (first + i, j))
    slot = lambda k: pl.BlockSpec((None, tr, tc), lambda i, j: (k, first + i, j))
    part_specs = [slot(p[1]) if isinstance(p, tuple) else flat for p in parts]
    part_arrays = [p[0] if isinstance(p, tuple) else p for p in parts]

    def body(*refs):
        w_ref, m_ref, v_ref = refs[:3]
        g_ref, d_ref, nm_ref, nv_ref = refs[3 + n:]
        g = refs[3][...].astype(F32)
        for r in refs[4:3 + n]:
            g = g + r[...].astype(F32)
        g_ref[...] = g
        d_ref[...], nm_ref[...], nv_ref[...] = _adam_update(w_ref[...], m_ref[...], v_ref[...], g)

    return pl.pallas_call(
        body, name=name, grid=(R // tr, C // tc), in_specs=[wspec] * 3 + part_specs, out_specs=[wspec] * 4,
        out_shape=[jax.ShapeDtypeStruct(w.shape, F32)] * 4, compiler_params=_cp(("parallel", "parallel")),
    )(w, m, v, *part_arrays)


_SMALL_WIDE = {"ln_a_g": 0, "ln_a_b": 1, "final_g": 8, "ple_norm_g": 9, "ssm_norm_g": 16, "conv_b": 44}
_WIDE_CONV_W, _WIDE_ROWS = 32, 56
_WIDE_LOSS = 10
_SMALL_NARROW = {"w_s": (0, NG * CH, 128), "b_s": (512, NG, 128), "dt_bias": (520, 1, NH), "a_log": (521, 1, NH),
                 "d_skip": (522, 1, NH)}
_NARROW_ROWS = 528
_SMALL_SHAPES = {"norm_g": (1, D), "ln_a_g": (1, D), "ln_a_b": (1, D), "ple_norm_g": (1, D), "final_g": (1, D),
                 "ssm_norm_g": (1, DI), "conv_b": (1, CD), "w_s": (NG * CH, CH), "b_s": (NG, CH), "dt_bias": (1, NH),
                 "a_log": (1, NH), "d_skip": (1, NH)}


def _adamw_small(w, m, v, wide_all, narrow_all, norm_all):
    names = list(_SMALL_SHAPES)
    n = len(names)

    def body(*refs):
        wr, mr, vr = refs[:n], refs[n:2 * n], refs[2 * n:3 * n]
        wide_ref, narrow_ref, norm_ref = refs[3 * n:3 * n + 3]
        outs = refs[3 * n + 3:]
        gr, dr, nmr, nvr, cw_ref, loss_ref = outs[:n], outs[n:2 * n], outs[2 * n:3 * n], outs[3 * n:4 * n], outs[4 * n], outs[4 * n + 1]

        def total(ref, rows, lanes):
            acc = ref[0, rows, lanes]
            for d in range(1, NDEV):
                acc = acc + ref[d, rows, lanes]
            return acc

        for k, name in enumerate(names):
            if name in _SMALL_WIDE or name == "norm_g":
                for part in range(_SMALL_SHAPES[name][1] // D):
                    pack, r = (norm_ref, 0) if name == "norm_g" else (wide_ref, _SMALL_WIDE[name] + part)
                    cols = slice(part * D, (part + 1) * D)
                    g = total(pack, slice(r, r + 1), slice(None))
                    gr[k][:, cols] = g
                    dr[k][:, cols], nmr[k][:, cols], nvr[k][:, cols] = _adam_update(wr[k][:, cols], mr[k][:, cols], vr[k][:, cols], g)
            else:
                r, rows, lanes = _SMALL_NARROW[name]
                g = total(narrow_ref, slice(r, r + rows), slice(0, lanes))
                gr[k][...] = g
                dr[k][...], nmr[k][...], nvr[k][...] = _adam_update(wr[k][...], mr[k][...], vr[k][...], g)
        cw_ref[...] = total(wide_ref, slice(_WIDE_CONV_W, _WIDE_CONV_W + 12), slice(None))
        loss_ref[...] = total(wide_ref, slice(_WIDE_LOSS, _WIDE_LOSS + 1), slice(None))

    shapes = [jax.ShapeDtypeStruct(_SMALL_SHAPES[k], F32) for k in names]
    specs = [_full(_SMALL_SHAPES[k]) for k in names]
    res = pl.pallas_call(
        body, name="adamw_small", grid=(1,),
        in_specs=specs * 3 + [_full(wide_all.shape), _full(narrow_all.shape), _full(norm_all.shape)],
        out_specs=specs * 4 + [_full((12, D)), _full((1, D))],
        out_shape=shapes * 4 + [jax.ShapeDtypeStruct((12, D), F32), jax.ShapeDtypeStruct((1, D), F32)],
        compiler_params=_cp(("arbitrary",)),
    )(*[w[k] for k in names], *[m[k] for k in names], *[v[k] for k in names], wide_all, narrow_all, norm_all)
    groups = [dict(zip(names, res[q * n:(q + 1) * n])) for q in range(4)]
    return groups[0], groups[1], groups[2], groups[3], res[4 * n], res[4 * n + 1][0, 0]


def _dev_index(px, py, pc):
    return 4 * px + 2 * py + pc


def _mesh_position():
    return lax.axis_index("x"), lax.axis_index("y"), lax.axis_index("c")


def _gather_exchange(blocks):
    n = len(blocks)

    def make(ins, outs, sems):
        send_sems, recv_sems, local_sems = sems
        x, y, c = _mesh_position()
        me, sibling = (x, y, c), (x, y, 1 - c)
        chips = [(1 - x, y), (x, 1 - y), (1 - x, 1 - y)]

        def copy(a, k, block, to, src=None):
            dst = outs[a].at[_dev_index(*block)]
            return pltpu.make_async_remote_copy(src_ref=dst if src is None else src, dst_ref=dst, send_sem=send_sems.at[a, k],
                                                recv_sem=recv_sems.at[a, k], device_id=to, device_id_type=MESH)

        mine = [pltpu.make_async_copy(ins[a], outs[a].at[_dev_index(*me)], local_sems.at[a]) for a in range(n)]
        first = []
        for a in range(n):
            first.append(copy(a, 0, me, sibling, src=ins[a]))
            first += [copy(a, 1 + j, me, (*chip, c), src=ins[a]) for j, chip in enumerate(chips)]

        def start():
            for cp in mine + first:
                cp.start()

        def finish():
            passed = []
            for j, chip in enumerate(chips):
                for a in range(n):
                    copy(a, 1 + j, (*chip, c), me).wait_recv()
                    fwd = copy(a, 4 + j, (*chip, c), sibling)
                    fwd.start()
                    passed.append(fwd)
            for a in range(n):
                copy(a, 0, sibling, me).wait_recv()
                for j, chip in enumerate(chips):
                    copy(a, 4 + j, (*chip, 1 - c), me).wait_recv()
            for cp in first + passed:
                cp.wait_send()
            for cp in mine:
                cp.wait()

        return start, finish

    return _Exchange(list(blocks), [jax.ShapeDtypeStruct((NDEV,) + b.shape, b.dtype) for b in blocks],
                     [pltpu.SemaphoreType.DMA((n, 7)), pltpu.SemaphoreType.DMA((n, 7)), pltpu.SemaphoreType.DMA((n,))], make)


def _relay_gather_exchange(blocks):
    n = len(blocks)

    def make(ins, outs, sems):
        send_sems, recv_sems, local_sems = sems
        x, y, c = _mesh_position()
        me, sibling = (x, y, c), (x, y, 1 - c)
        x_nbr, y_nbr, diag = (1 - x, y), (x, 1 - y), (1 - x, 1 - y)
        relay_from = (jnp.where(c == 0, x, 1 - x), jnp.where(c == 0, 1 - y, y))
        relay_to = (jnp.where(c == 0, 1 - x, x), jnp.where(c == 0, y, 1 - y))

        def copy(a, k, block, to, src=None):
            dst = outs[a].at[_dev_index(*block)]
            return pltpu.make_async_remote_copy(src_ref=dst if src is None else src, dst_ref=dst, send_sem=send_sems.at[a, k],
                                                recv_sem=recv_sems.at[a, k], device_id=to, device_id_type=MESH)

        mine = [pltpu.make_async_copy(ins[a], outs[a].at[_dev_index(*me)], local_sems.at[a]) for a in range(n)]
        first = []
        for a in range(n):
            first += [copy(a, 0, me, sibling, src=ins[a]), copy(a, 1, me, (*x_nbr, c), src=ins[a]), copy(a, 2, me, (*y_nbr, c), src=ins[a])]

        def start():
            for cp in mine + first:
                cp.start()

        def finish():
            later = []
            for a in range(n):
                copy(a, 1, (*x_nbr, c), me).wait_recv()
                copy(a, 2, (*y_nbr, c), me).wait_recv()
                later.append(copy(a, 3, (*relay_from, c), (*relay_to, c)))
                later += [copy(a, 4, (*x_nbr, c), sibling), copy(a, 5, (*y_nbr, c), sibling)]
                for cp in later[-3:]:
                    cp.start()
            for a in range(n):
                copy(a, 3, (*diag, c), me).wait_recv()
                later.append(copy(a, 6, (*diag, c), sibling))
                later[-1].start()
            for a in range(n):
                copy(a, 0, sibling, me).wait_recv()
                for k, chip in ((4, x_nbr), (5, y_nbr), (6, diag)):
                    copy(a, k, (*chip, 1 - c), me).wait_recv()
            for cp in first + later:
                cp.wait_send()
            for cp in mine:
                cp.wait()

        return start, finish

    return _Exchange(list(blocks), [jax.ShapeDtypeStruct((NDEV,) + b.shape, b.dtype) for b in blocks],
                     [pltpu.SemaphoreType.DMA((n, 7)), pltpu.SemaphoreType.DMA((n, 7)), pltpu.SemaphoreType.DMA((n,))], make)


def _combine(*exchanges):
    def make(ins, outs, sems):
        pairs = []
        for e in exchanges:
            ni, no, ns = len(e.arrays), len(e.out_shape), len(e.sems)
            pairs.append(e.make(ins[:ni], outs[:no], sems[:ns]))
            ins, outs, sems = ins[ni:], outs[no:], sems[ns:]

        def start():
            for s, _ in pairs:
                s()

        def finish():
            for _, f in pairs:
                f()

        return start, finish

    return _Exchange(sum((list(e.arrays) for e in exchanges), []), sum((list(e.out_shape) for e in exchanges), []),
                     sum((list(e.sems) for e in exchanges), []), make)


def _start_wait_all(copies, local=()):
    def start():
        for cp in list(local) + list(copies):
            cp.start()

    def finish():
        for cp in copies:
            cp.wait()
        for cp in local:
            cp.wait()

    return start, finish


def _no_exchange():
    return _Exchange([], [], [], lambda ins, outs, sems: (lambda: None, lambda: None))


def _direct_exchange(grads):
    n = len(grads)

    def make(ins, outs, sems):
        send_sems, recv_sems = sems
        x, y, c = _mesh_position()
        copies = []
        for a in range(n):
            for r in range(1, NDEV):
                peer = (x ^ ((r >> 2) & 1), y ^ ((r >> 1) & 1), c ^ (r & 1))
                copies.append(pltpu.make_async_remote_copy(
                    src_ref=ins[a].at[_dev_index(*peer)], dst_ref=outs[a].at[r - 1], send_sem=send_sems.at[a, r - 1],
                    recv_sem=recv_sems.at[a, r - 1], device_id=peer, device_id_type=MESH))
        return _start_wait_all(copies)

    return _Exchange(list(grads), [jax.ShapeDtypeStruct((NDEV - 1,) + g.shape[1:], g.dtype) for g in grads],
                     [pltpu.SemaphoreType.DMA((n, NDEV - 1)), pltpu.SemaphoreType.DMA((n, NDEV - 1))], make)


def _direct_gather_exchange(smalls):
    n = len(smalls)

    def make(ins, outs, sems):
        send_sems, recv_sems, local_sems = sems
        x, y, c = _mesh_position()
        copies, local = [], []
        for a in range(n):
            slot = outs[a].at[_dev_index(x, y, c)]
            local.append(pltpu.make_async_copy(ins[a], slot, local_sems.at[a]))
            for r in range(1, NDEV):
                peer = (x ^ ((r >> 2) & 1), y ^ ((r >> 1) & 1), c ^ (r & 1))
                copies.append(pltpu.make_async_remote_copy(src_ref=ins[a], dst_ref=slot, send_sem=send_sems.at[a, r - 1],
                                                           recv_sem=recv_sems.at[a, r - 1], device_id=peer, device_id_type=MESH))
        return _start_wait_all(copies, local=local)

    return _Exchange(list(smalls), [jax.ShapeDtypeStruct((NDEV,) + s.shape, s.dtype) for s in smalls],
                     [pltpu.SemaphoreType.DMA((n, 7)), pltpu.SemaphoreType.DMA((n, 7)), pltpu.SemaphoreType.DMA((n,))], make)


_W_IN_ROWS = {"u": (0, 1024), "v": (1024, 2048), "za": (2048, 3072), "zb": (3072, 5120), "xbc": (5120, 8192),
              "dt": (8192, 8224), "ga": (8224, 9248), "gb": (9248, 10272)}
_FWD_ORDER = ("xbc", "u", "zb", "v", "za", "ga", "gb", "dt")
_BWD_ORDER = ("xbc", "zb", "u", "v", "za", "ga", "gb", "dt")


def _w_in_t_rows(wt, order):
    assert order[-1] == "dt"
    z = jnp.zeros((NP - NIN, wt.shape[1]), wt.dtype)
    return jnp.concatenate([wt[slice(*_W_IN_ROWS[n])] for n in order] + [z], axis=0)


_WEIGHTS = ["norm_g", "w_in", "ln_a_g", "ln_a_b", "w_s", "b_s", "conv_w", "conv_b", "dt_bias", "a_log", "d_skip", "ssm_norm_g",
            "w_oa", "w_ob", "w_out", "ple_norm_g", "w_pg", "w_ple", "final_g"]


def _rows_pack(d, dtype):
    return jnp.concatenate([d["w_oa"].reshape(128, D), d["w_ob"].reshape(256, D), d["w_out"].reshape(128, D),
                            d["w_pg"].reshape(128, D), d["w_ple"].reshape(32, D)], axis=0).astype(dtype)


def kernel(x, p, norm_g, w_in, ln_a_g, ln_a_b, w_s, b_s, conv_w, conv_b, dt_bias, a_log, d_skip, ssm_norm_g, w_oa, w_ob, w_out, ple_norm_g, w_pg, w_ple, final_g, loss_target, m_norm_g, m_w_in, m_ln_a_g, m_ln_a_b, m_w_s, m_b_s, m_conv_w, m_conv_b, m_dt_bias, m_a_log, m_d_skip, m_ssm_norm_g, m_w_oa, m_w_ob, m_w_out, m_ple_norm_g, m_w_pg, m_w_ple, m_final_g, v_norm_g, v_w_in, v_ln_a_g, v_ln_a_b, v_w_s, v_b_s, v_conv_w, v_conv_b, v_dt_bias, v_a_log, v_d_skip, v_ssm_norm_g, v_w_oa, v_w_ob, v_w_out, v_ple_norm_g, v_w_pg, v_w_ple, v_final_g):
    args = dict(locals())
    w = {n: args[n] for n in _WEIGHTS}
    m = {n: args["m_" + n] for n in _WEIGHTS}
    v = {n: args["v_" + n] for n in _WEIGHTS}
    T = x.shape[1]
    xi, yi, ci = lax.axis_index("x"), lax.axis_index("y"), lax.axis_index("c")
    me = 4 * xi + 2 * yi + ci
    x2, p2, tgt = x.reshape(T, D), p.reshape(T, PLE), loss_target.reshape(T, D)

    norm_g2 = w["norm_g"].reshape(1, D)
    ws = jnp.where(jnp.tril(jnp.ones((CH, CH), bool))[None], w["w_s"].reshape(NG, CH, CH), 0.0).astype(BF16)
    wst = jnp.transpose(ws, (0, 2, 1))
    bst = jnp.broadcast_to(w["b_s"].reshape(NG, CH, 1), (NG, CH, 256))
    ln_g, ln_b = w["ln_a_g"].reshape(1, D), w["ln_a_b"].reshape(1, D)
    cb = w["conv_b"].reshape(1, CD)
    pad32 = lambda a: jnp.pad(a.reshape(1, NH), ((0, 0), (0, DTW - NH)))
    dtb, alog = pad32(w["dt_bias"]), pad32(w["a_log"])
    dskx = jnp.repeat(w["d_skip"].reshape(NH), HD).reshape(1, DI)
    sg = w["ssm_norm_g"].reshape(1, DI)
    ple_g, fin_g = w["ple_norm_g"].reshape(1, D), w["final_g"].reshape(1, D)
    e128 = (jnp.arange(DTW)[:, None] == (jnp.arange(DI)[None, :] // HD)).astype(BF16)
    et128 = e128.T
    gsel = ((jnp.arange(D)[:, None] // 256) == jnp.arange(128)[None, :]).astype(BF16)

    w_in_t = lambda a: jnp.transpose(a.reshape(D, WSH))
    (a_all,) = _run_exchange(_relay_gather_exchange([w_in_t(w["w_in"]).astype(BF16)]), "all_gather_w_in")
    w_in_full_t = a_all.reshape(NIN, D)
    wp = _w_in_t_rows(w_in_full_t, _FWD_ORDER)
    wt_bwd = _w_in_t_rows(w_in_full_t, _BWD_ORDER)
    (proj, dtr, h), (r_all, cw_all) = _proj_fwd(x2, norm_g2, wp, _gather_exchange([_rows_pack(w, BF16), w["conv_w"].reshape(4, CD // NDEV)]))
    f_oa = r_all[:, R_OA:R_OB].reshape(D, D)
    f_ob = r_all[:, R_OB:R_OUT].reshape(DI, D)
    f_out = r_all[:, R_OUT:R_PG].reshape(D, D)
    f_pg = r_all[:, R_PG:R_PLE].reshape(D, D)
    f_ple = jnp.transpose(r_all[:, R_PLE:R_ROWS].reshape(NDEV, PLE, 128), (1, 0, 2)).reshape(PLE, D)
    cw = jnp.transpose(cw_all, (1, 0, 2)).reshape(4, CD)

    y, yb, hprev, pre_all = _ssd_fwd(proj, dtr, cw, cb, dtb, alog, dskx, sg, e128)
    dx1, dx1b, ya, mg, hp, dpre, dpe, doa, dob, dya, dyb, dgab, acc = _merge(
        x2, yb, proj, p2, tgt, (ln_g, ln_b, ws, bst), f_oa, f_ob, f_out, f_pg, f_ple, ple_g, fin_g)

    gple = jnp.transpose(_wgrad(p2, dpe, "wgrad_ple").reshape(PLE, NDEV, 128), (1, 0, 2)).reshape(NDEV, 32, D)
    gr = jnp.concatenate([_wgrad(ya, doa, "wgrad_oa").reshape(NDEV, 128, D), _wgrad(yb, dob, "wgrad_ob").reshape(NDEV, 256, D),
                          _wgrad(mg, dx1b, "wgrad_out").reshape(NDEV, 128, D), _wgrad(hp, dpre, "wgrad_pg").reshape(NDEV, 128, D),
                          gple], axis=1)
    (duvz, dws, dbs, dln), _ = _gmlp_bwd(proj, dya, ln_g, ln_b, ws, wst, bst, gsel, _no_exchange())
    (dxz, ddt, dcw, dsm, dsg), (rr,) = _ssd_bwd(proj, pre_all, dtr, y, dyb, hprev, cw, dtb, alog, dskx, sg, e128, et128, _direct_exchange([gr]))

    g_xz = _wgrad(dxz, h, "wgrad_xbc_zb")
    g_w_in_t = jnp.concatenate([_wgrad(duvz, h, "wgrad_u_v_za"), g_xz[CD:], g_xz[:CD], _wgrad(ddt, h, "wgrad_dt")[:NH],
                                _wgrad(dgab, h, "wgrad_ga_gb")], axis=0)
    ga = g_w_in_t.reshape(NDEV, WSH, D)
    wide = jnp.concatenate([dln, acc, dsg.reshape(16, D), dcw.reshape(24, D)], axis=0)
    narrow = jnp.concatenate([dws.reshape(NG * CH, CH), jnp.pad(dbs[:, :NG].T, ((0, 8 - NG), (0, 0))), dsm], axis=0)
    (gx, dng), (ra, wide_all, narrow_all) = _proj_bwd(x2, dx1, norm_g2, wt_bwd, [dxz, duvz, dgab], ddt,
                                                      _combine(_direct_exchange([ga]), _direct_gather_exchange([wide, narrow])))
    (norm_all,) = _run_exchange(_direct_gather_exchange([dng]), "all_gather_d_norm_g")
    ga_own = lax.dynamic_index_in_dim(ga, me, 0, keepdims=False)
    gr_own = lax.dynamic_index_in_dim(gr, me, 0, keepdims=False)

    out_g, out_d, out_m, out_v = {}, {}, {}, {}
    outs = (out_g, out_d, out_m, out_v)
    tiles = lambda a: jnp.transpose(a.reshape(8, 128, WSH), (2, 0, 1))
    g2 = _sum_parts([ga_own] + [(ra, k) for k in range(NDEV - 1)], "sum_w_in")
    res = _adamw_tiles(tiles(w["w_in"]), tiles(m["w_in"]), tiles(v["w_in"]), g2.reshape(WSH, 8, 128), "adamw_w_in")
    for dst, val in zip(outs, res):
        dst["w_in"] = jnp.transpose(val, (1, 2, 0)).reshape(1, D, WSH)
    parts_r = [gr_own] + [(rr, k) for k in range(NDEV - 1)]
    for name, row, rows in (("w_oa", R_OA, 128), ("w_ob", R_OB, 256), ("w_out", R_OUT, 128), ("w_pg", R_PG, 128)):
        res = _adamw(w[name].reshape(rows, D), m[name].reshape(rows, D), v[name].reshape(rows, D), parts_r, "adamw_" + name,
                     part_row=row, block_rows=128)
        for dst, val in zip(outs, res):
            dst[name] = val.reshape(1, rows, D)
    res = _adamw(w["w_ple"].reshape(32, D), m["w_ple"].reshape(32, D), v["w_ple"].reshape(32, D), parts_r, "adamw_w_ple",
                 part_row=R_PLE, block_rows=32)
    for dst, val in zip(outs, res):
        dst["w_ple"] = val.reshape(1, PLE, 128)
    two_d = lambda d: {n: d[n].reshape(_SMALL_SHAPES[n]) for n in _SMALL_SHAPES}
    *res, g_cw_wide, loss = _adamw_small(two_d(w), two_d(m), two_d(v), wide_all, narrow_all, norm_all)
    for dst, val in zip(outs, res):
        dst.update({n: val[n].reshape(w[n].shape) for n in _SMALL_SHAPES})
    g_cw = lax.dynamic_slice_in_dim(g_cw_wide.reshape(4, CD), me * (CD // NDEV), CD // NDEV, axis=1).reshape(12, 128)
    res = _adamw(w["conv_w"].reshape(12, 128), m["conv_w"].reshape(12, 128), v["conv_w"].reshape(12, 128), [g_cw], "adamw_conv_w")
    for dst, val in zip((out_g, out_d, out_m, out_v), res):
        dst["conv_w"] = val.reshape(1, 4, CD // NDEV)

    return (loss, gx.reshape(1, T, D), *[out_g[n] for n in _WEIGHTS], *[out_d[n] for n in _WEIGHTS],
            *[out_m[n] for n in _WEIGHTS], *[out_v[n] for n in _WEIGHTS])
```

```python
import functools
import math
from typing import Callable, NamedTuple

import jax
import jax.numpy as jnp
from jax import lax
from jax.experimental import pallas as pl
from jax.experimental.pallas import tpu as pltpu

F32 = jnp.float32
BF16 = jnp.bfloat16
MESH = pl.DeviceIdType.MESH

D = 1024
DI = 2048
CD = 3072
NH = 32
HD = 64
NST = 128
NG = 4
CH = 128
PLE = 256
NIN = 10272
NDEV = 8
WSH = NIN // NDEV
EPS = 1e-6
OFF_XBC, OFF_ZB, OFF_U, OFF_V, OFF_ZA, OFF_GA, OFF_GB, OFF_DT = 0, 3072, 5120, 6144, 7168, 8192, 9216, 10240
NP = 10368
DTW = 128
R_OA, R_OB, R_OUT, R_PG, R_PLE, R_ROWS = 0, 128, 384, 512, 640, 672

ADAM_LR, ADAM_B1, ADAM_B2, ADAM_EPS, ADAM_WD, ADAM_STEP = 0.001, 0.9, 0.999, 1e-08, 0.01, 10

V7X_VMEM_LIMIT = 56 * 1024 * 1024


def _cp(sem=None):
    return pltpu.CompilerParams(dimension_semantics=sem, vmem_limit_bytes=V7X_VMEM_LIMIT)


def _dot(a, b, prec=None):
    return jnp.dot(a, b, preferred_element_type=F32, precision=prec)


def _dot_nt(a, b, prec=None):
    return lax.dot_general(a, b, (((1,), (1,)), ((), ())), preferred_element_type=F32, precision=prec)


def _dot_tn(a, b, prec=None):
    return lax.dot_general(a, b, (((0,), (0,)), ((), ())), preferred_element_type=F32, precision=prec)


def _sigmoid(x):
    return 1.0 / (1.0 + jnp.exp(-x))


def _gelu_and_grad(x):
    c = math.sqrt(2.0 / math.pi)
    x2 = x * x
    t = jnp.tanh(c * (x + 0.044715 * x * x2))
    g = 0.5 * x * (1.0 + t)
    dg = 0.5 * (1.0 + t) + 0.5 * x * (1.0 - t * t) * c * (1.0 + 3.0 * 0.044715 * x2)
    return g, dg


def _gelu(x):
    c = math.sqrt(2.0 / math.pi)
    return 0.5 * x * (1.0 + jnp.tanh(c * (x + 0.044715 * x * x * x)))


def _softplus(x):
    return jnp.maximum(x, 0.0) + jnp.log(1.0 + jnp.exp(-jnp.abs(x)))


def _full(shape):
    n = len(shape)
    return pl.BlockSpec(shape, lambda *_: (0,) * n)


_ANY = pl.BlockSpec(memory_space=pl.ANY)


class _Exchange(NamedTuple):
    arrays: list
    out_shape: list
    sems: list
    make: Callable


def _call(body, ex, *, name, grid, in_specs, out_specs, out_shape, scratch_shapes, args):
    ki, ko, ks = len(in_specs), len(out_specs), len(scratch_shapes)
    ei, eo = len(ex.arrays), len(ex.out_shape)
    last = [g - 1 for g in grid]

    def full_body(*refs):
        r = list(refs)
        ins, eins, r = r[:ki], r[ki:ki + ei], r[ki + ei:]
        outs, eouts, r = r[:ko], r[ko:ko + eo], r[ko + eo:]
        scr, esems = r[:ks], r[ks:]
        start, finish = ex.make(eins, eouts, esems)
        ids = [pl.program_id(a) for a in range(len(grid))]
        is_first = functools.reduce(lambda p, q: p & q, [i == 0 for i in ids])
        is_last = functools.reduce(lambda p, q: p & q, [i == l for i, l in zip(ids, last)])
        pl.when(is_first)(start)
        body(*ins, *outs, *scr)
        pl.when(is_last)(finish)

    res = pl.pallas_call(
        full_body, name=name, grid=grid, in_specs=list(in_specs) + [_ANY] * ei, out_specs=list(out_specs) + [_ANY] * eo,
        out_shape=list(out_shape) + list(ex.out_shape), scratch_shapes=list(scratch_shapes) + list(ex.sems),
        compiler_params=_cp(("arbitrary",) * len(grid)),
    )(*args, *ex.arrays)
    return res[:ko], res[ko:]


def _run_exchange(ex, name):
    ni, no = len(ex.arrays), len(ex.out_shape)

    def body(*refs):
        start, finish = ex.make(refs[:ni], refs[ni:ni + no], refs[ni + no:])
        start()
        finish()

    return pl.pallas_call(body, name=name, in_specs=[_ANY] * ni, out_specs=[_ANY] * no, out_shape=list(ex.out_shape),
                          scratch_shapes=list(ex.sems))(*ex.arrays)


def _proj_fwd(x, norm_g, wp, ex):
    T = x.shape[0]
    tm, tn = min(T, 1024), 2048
    nj = OFF_DT // tn
    assert OFF_DT % tn == 0 and OFF_DT + DTW == NP

    def body(x_ref, g_ref, w_ref, wdt_ref, proj_ref, dt_ref, h_ref, hs_ref):
        j = pl.program_id(1)

        @pl.when(j == 0)
        def _():
            xf = x_ref[...]
            r = lax.rsqrt(jnp.mean(xf * xf, axis=-1, keepdims=True) + EPS)
            h = (xf * r * g_ref[...]).astype(BF16)
            hs_ref[...] = h
            h_ref[...] = h

        proj_ref[...] = _dot_nt(hs_ref[...], w_ref[...]).astype(BF16)

        @pl.when(j == nj - 1)
        def _():
            dt_ref[...] = _dot_nt(hs_ref[...], wdt_ref[...])

    return _call(
        body, ex, name="proj_fwd", grid=(T // tm, nj),
        in_specs=[pl.BlockSpec((tm, D), lambda i, j: (i, 0)), _full((1, D)), pl.BlockSpec((tn, D), lambda i, j: (j, 0)),
                  pl.BlockSpec((DTW, D), lambda i, j: (OFF_DT // DTW, 0))],
        out_specs=[pl.BlockSpec((tm, tn), lambda i, j: (i, j)), pl.BlockSpec((tm, DTW), lambda i, j: (i, 0)),
                   pl.BlockSpec((tm, D), lambda i, j: (i, 0))],
        out_shape=[jax.ShapeDtypeStruct((T, OFF_DT), BF16), jax.ShapeDtypeStruct((T, DTW), F32), jax.ShapeDtypeStruct((T, D), BF16)],
        scratch_shapes=[pltpu.VMEM((tm, D), BF16)], args=(x, norm_g, wp, wp))


def _gmlp_tile():
    return 256


def _gmlp_fwd_tile(u_ref, v_ref, z_ref, lg_ref, lb_ref, ws_ref, bs_ref, ya_ref, vn_s):
    tm = u_ref.shape[0]
    vg = _gelu(v_ref[...].astype(F32))
    mu = jnp.mean(vg, axis=-1, keepdims=True)
    xc = vg - mu
    rstd = lax.rsqrt(jnp.mean(xc * xc, axis=-1, keepdims=True) + EPS)
    vn_s[...] = (xc * rstd * lg_ref[...] + lb_ref[...]).astype(BF16)
    for c in range(tm // CH):
        rs = slice(c * CH, (c + 1) * CH)
        for g in range(NG):
            cs_ = slice(g * 256, (g + 1) * 256)
            sv = _dot(ws_ref[g], vn_s[rs, cs_]) + bs_ref[g]
            z = z_ref[rs, cs_].astype(F32)
            ya_ref[rs, cs_] = (_gelu(u_ref[rs, cs_].astype(F32)) * sv * (z * _sigmoid(z))).astype(BF16)


def _gmlp_bwd(proj, dya, ln_g, ln_b, ws, wst, bst, gsel, ex):
    T = proj.shape[0]
    tm = min(T, _gmlp_tile())

    def body(u_ref, v_ref, z_ref, dy_ref, lg_ref, lb_ref, ws_ref, wst_ref, bs_ref, gsel_ref,
             d_ref, dws_ref, dbs_ref, dln_ref, vn_s, dsv_s, dvn_s):
        du_ref, dv_ref, dz_ref = d_ref.at[:, 0:D], d_ref.at[:, D:2 * D], d_ref.at[:, 2 * D:3 * D]
        @pl.when(pl.program_id(0) == 0)
        def _():
            dws_ref[...] = jnp.zeros_like(dws_ref)
            dbs_ref[...] = jnp.zeros_like(dbs_ref)
            dln_ref[...] = jnp.zeros_like(dln_ref)

        vg, dvg_dv = _gelu_and_grad(v_ref[...].astype(F32))
        mu = jnp.mean(vg, axis=-1, keepdims=True)
        xc = vg - mu
        rstd = lax.rsqrt(jnp.mean(xc * xc, axis=-1, keepdims=True) + EPS)
        vhat = xc * rstd
        vn_s[...] = (vhat * lg_ref[...] + lb_ref[...]).astype(BF16)
        ri = lax.broadcasted_iota(jnp.int32, (CH, CH), 0)
        ci = lax.broadcasted_iota(jnp.int32, (CH, CH), 1)
        tril = (ri >= ci).astype(F32)
        for c in range(tm // CH):
            rs = slice(c * CH, (c + 1) * CH)
            for g in range(NG):
                cs_ = slice(g * 256, (g + 1) * 256)
                vn = vn_s[rs, cs_]
                sv = _dot(ws_ref[g], vn) + bs_ref[g]
                z = z_ref[rs, cs_].astype(F32)
                sz = _sigmoid(z)
                ug, dug_du = _gelu_and_grad(u_ref[rs, cs_].astype(F32))
                dy = dy_ref[rs, cs_].astype(F32)
                zs = z * sz
                t = dy * zs
                dsv_f = dy * sv
                du_ref[rs, cs_] = (zs * dsv_f * dug_du).astype(BF16)
                dz_ref[rs, cs_] = (dsv_f * ug * (sz + zs * (1.0 - sz))).astype(BF16)
                dsv = (t * ug).astype(BF16)
                dsv_s[rs, cs_] = dsv
                dvn_s[rs, cs_] = _dot(wst_ref[g], dsv)
                dws_ref[g] += _dot_nt(dsv, vn) * tril
            dbs_ref[...] += _dot(dsv_s[rs, :], gsel_ref[...])
        dvn = dvn_s[...]
        dln_ref[0:1, :] += jnp.sum(dvn * vhat, axis=0, keepdims=True)
        dln_ref[1:2, :] += jnp.sum(dvn, axis=0, keepdims=True)
        dvh = dvn * lg_ref[...]
        dvg = rstd * (dvh - jnp.mean(dvh, axis=-1, keepdims=True) - vhat * jnp.mean(dvh * vhat, axis=-1, keepdims=True))
        dv_ref[...] = (dvg * dvg_dv).astype(BF16)

    blk = lambda off: pl.BlockSpec((tm, D), lambda i: (i, off // D))
    row = pl.BlockSpec((tm, D), lambda i: (i, 0))
    return _call(
        body, ex, name="gmlp_bwd", grid=(T // tm,),
        in_specs=[blk(OFF_U), blk(OFF_V), blk(OFF_ZA), row, _full((1, D)), _full((1, D)), _full((NG, CH, CH)),
                  _full((NG, CH, CH)), _full((NG, CH, 256)), _full((D, 128))],
        out_specs=[pl.BlockSpec((tm, 3 * D), lambda i: (i, 0)), _full((NG, CH, CH)), _full((CH, 128)), _full((8, D))],
        out_shape=[jax.ShapeDtypeStruct((T, 3 * D), BF16),
                   jax.ShapeDtypeStruct((NG, CH, CH), F32), jax.ShapeDtypeStruct((CH, 128), F32), jax.ShapeDtypeStruct((8, D), F32)],
        scratch_shapes=[pltpu.VMEM((tm, D), BF16), pltpu.VMEM((tm, D), BF16), pltpu.VMEM((tm, D), F32)],
        args=(proj, proj, proj, dya, ln_g, ln_b, ws, wst, bst, gsel))


def _zb_cols(zb0_ref, zb1_ref, cols):
    ref = zb0_ref if cols.start < D else zb1_ref
    return ref[:, cols.start % D:cols.start % D + (cols.stop - cols.start)]


def _shift_matrix(down):
    t = jnp.arange(CH)[None, :, None]
    j = jnp.arange(1, 4)[:, None, None]
    col = jnp.arange(2 * CH)[None, None, :]
    src = CH + t - j if down else t + j
    return (col == src).astype(BF16).reshape(3 * CH, 2 * CH)


def _conv_pre(x, moved, cw_ref, cb_ref):
    pre = cb_ref[...] + cw_ref[3:4, :] * x
    for j in (1, 2, 3):
        pre = pre + cw_ref[3 - j:4 - j, :] * moved[(j - 1) * CH:j * CH]
    return pre


def _split_dot(x, w, parts, w_left=False):
    acc, r = None, x
    for k in range(parts):
        hi = r.astype(BF16)
        d = _dot(w, hi) if w_left else _dot(hi, w)
        acc = d if acc is None else acc + d
        if k + 1 < parts:
            r = r - hi.astype(F32)
    return acc


def _chunk_decays(dt, alog_ref, e_ref, cs_s, cst_s, csx_s):
    a = -jnp.exp(alog_ref[...])
    ri = lax.broadcasted_iota(jnp.int32, (CH, CH), 0)
    ci = lax.broadcasted_iota(jnp.int32, (CH, CH), 1)
    tril = ri >= ci
    cs = _split_dot(dt * a, tril.astype(BF16), 3, w_left=True)
    cs_s[...] = cs
    cst_s[...] = cs.T
    csx_s[...] = _split_dot(cs, e_ref[...], 3)
    return a, tril, ri, ci


def _lmat(cst_s, h, tril):
    rowb = jnp.broadcast_to(cst_s[h:h + 1, :], (CH, CH))
    return jnp.exp(jnp.where(tril, rowb.T - rowb, -jnp.inf))


def _head_pair_rows(v, lane):
    return jnp.concatenate([jnp.where(lane < HD, v, 0.0), jnp.where(lane < HD, 0.0, v)], axis=0).astype(BF16)


def _ssd_fwd(proj, dtr, cw, cb, dtb, alog, dskx, sg, e128):
    T = proj.shape[0]
    nc = T // CH

    def body(xbc_ref, zb0_ref, zb1_ref, dt_ref, cw_ref, cb_ref, dtb_ref, alog_ref, dx_ref, sg_ref, e_ref, shift_ref,
             y_ref, yb_ref, hp_ref, pre_ref, xx_s, h_s, cs_s, cst_s, csx_s, yz_s):
        @pl.when(pl.program_id(0) == 0)
        def _():
            xx_s[...] = jnp.zeros_like(xx_s)
            h_s[...] = jnp.zeros_like(h_s)

        xx_s[CH:, :] = xbc_ref[...]
        moved = _dot(shift_ref[...], xx_s[...])
        xx_s[CH - 16:CH, :] = xbc_ref[CH - 16:, :]
        x = xbc_ref[...].astype(F32)
        pre = _conv_pre(x, moved, cw_ref, cb_ref)
        pre_ref[...] = pre
        xc = pre * _sigmoid(pre)
        dt = _softplus(dt_ref[...] + dtb_ref[...])
        a, tril, _, lane = _chunk_decays(dt, alog_ref, e_ref, cs_s, cst_s, csx_s)
        dt_x = _split_dot(dt, e_ref[...], 2)
        cs_last_x = csx_s[CH - 1:CH, :]
        hp_ref[0] = h_s[...]
        for g in range(NG):
            gs = slice(g * 512, (g + 1) * 512)
            bg = xc[:, DI + g * NST:DI + (g + 1) * NST].astype(BF16)
            cg = xc[:, DI + 512 + g * NST:DI + 512 + (g + 1) * NST].astype(BF16)
            cbm = _dot_nt(cg, bg)
            xg = xc[:, gs]
            xdt = xg * dt_x[:, gs]
            hprev = h_s[:, gs]
            csx = csx_s[:, gs]
            yoff = _dot(cg, hprev.astype(BF16)) * jnp.exp(csx)
            st = _dot_tn(bg, (xdt * jnp.exp(cs_last_x[:, gs] - csx)).astype(BF16))
            h_s[:, gs] = jnp.exp(cs_last_x[:, gs]) * hprev + st
            ssq = jnp.zeros((CH, 1), F32)
            for q in range(4):
                h0 = g * 8 + 2 * q
                ps = slice(q * 128, (q + 1) * 128)
                cols = slice(g * 512 + q * 128, g * 512 + (q + 1) * 128)
                m01 = jnp.concatenate([cbm * _lmat(cst_s, h0, tril), cbm * _lmat(cst_s, h0 + 1, tril)], axis=1).astype(BF16)
                yq = _dot(m01, _head_pair_rows(xdt[:, ps], lane)) + yoff[:, ps] + xg[:, ps] * dx_ref[:, cols]
                y_ref[:, cols] = yq
                z = _zb_cols(zb0_ref, zb1_ref, cols).astype(F32)
                yz = yq * z * _sigmoid(z)
                yz_s[:, cols] = yz
                ssq = ssq + jnp.sum(yz * yz, axis=1, keepdims=True)
            rg = lax.rsqrt(ssq * (1.0 / 512.0) + EPS)
            yb_ref[:, gs] = (yz_s[:, gs] * rg * sg_ref[:, gs]).astype(BF16)

    return pl.pallas_call(
        body, name="ssd_fwd", grid=(nc,),
        in_specs=[pl.BlockSpec((CH, CD), lambda c: (c, OFF_XBC // CD)), pl.BlockSpec((CH, D), lambda c: (c, OFF_ZB // D)),
                  pl.BlockSpec((CH, D), lambda c: (c, OFF_ZB // D + 1)),
                  pl.BlockSpec((CH, DTW), lambda c: (c, 0)), _full((4, CD)), _full((1, CD)), _full((1, DTW)),
                  _full((1, DTW)), _full((1, DI)), _full((1, DI)), _full((DTW, DI)), _full((3 * CH, 2 * CH))],
        out_specs=[pl.BlockSpec((CH, DI), lambda c: (c, 0)), pl.BlockSpec((CH, DI), lambda c: (c, 0)),
                   pl.BlockSpec((1, NST, DI), lambda c: (c, 0, 0)), pl.BlockSpec((CH, CD), lambda c: (c, 0))],
        out_shape=[jax.ShapeDtypeStruct((T, DI), F32), jax.ShapeDtypeStruct((T, DI), BF16),
                   jax.ShapeDtypeStruct((nc, NST, DI), F32), jax.ShapeDtypeStruct((T, CD), F32)],
        scratch_shapes=[pltpu.VMEM((2 * CH, CD), BF16), pltpu.VMEM((NST, DI), F32), pltpu.VMEM((CH, CH), F32),
                        pltpu.VMEM((CH, CH), F32), pltpu.VMEM((CH, DI), F32), pltpu.VMEM((CH, DI), F32)],
        compiler_params=_cp(("arbitrary",)),
    )(proj, proj, proj, dtr, cw, cb, dtb, alog, dskx, sg, e128, _shift_matrix(down=True))


def _ssd_bwd(proj, pre_all, dtr, y, dyb, hprev_all, cw, dtb, alog, dskx, sg, e128, et128, ex):
    T = proj.shape[0]
    nc = T // CH

    def body(xbc_ref, pre_ref, zb0_ref, zb1_ref, dt_ref, y_ref, dyb_ref, hp_ref, cw_ref, dtb_ref, alog_ref, dx_ref, sg_ref,
             e_ref, et_ref, shift_ref, d_ref, ddt_ref, dcw_ref, dsm_ref, dsg_ref,
             g_s, dd_s, cs_s, cst_s, csx_s, dy_s, dxdt_s, dxs_s, dsd_s, dxc_s, gh_s):
        dxbc_ref, dzb_ref = d_ref.at[:, 0:CD], d_ref.at[:, CD:CD + DI]
        i = pl.program_id(0)

        @pl.when(i == 0)
        def _():
            g_s[...] = jnp.zeros_like(g_s)
            dd_s[...] = jnp.zeros_like(dd_s)
            dcw_ref[...] = jnp.zeros_like(dcw_ref)
            dsm_ref[...] = jnp.zeros_like(dsm_ref)
            dsg_ref[...] = jnp.zeros_like(dsg_ref)

        pre = pre_ref[...]
        sp = _sigmoid(pre)
        xc = pre * sp
        dtr = dt_ref[...] + dtb_ref[...]
        dt = _softplus(dtr)
        a, tril, ri, lane = _chunk_decays(dt, alog_ref, e_ref, cs_s, cst_s, csx_s)
        et = et_ref[...]
        dt_x = _split_dot(dt, e_ref[...], 2)
        cs_last_x = csx_s[CH - 1:CH, :]

        for g in range(NG):
            gs = slice(g * 512, (g + 1) * 512)
            z = _zb_cols(zb0_ref, zb1_ref, gs).astype(F32)
            sz = _sigmoid(z)
            yv = y_ref[:, gs]
            zs = z * sz
            yz = yv * zs
            rg = lax.rsqrt(jnp.mean(yz * yz, axis=-1, keepdims=True) + EPS)
            yn = yz * rg
            dyb = dyb_ref[:, gs].astype(F32)
            dsg_ref[0:1, gs] += jnp.sum(dyb * yn, axis=0, keepdims=True)
            dyn = dyb * sg_ref[:, gs]
            dyz = rg * (dyn - yn * jnp.mean(dyn * yn, axis=-1, keepdims=True))
            dy_s[:, gs] = dyz * zs
            dzb_ref[:, gs] = (dyz * yv * (sz + zs * (1.0 - sz))).astype(BF16)

        rsum = jnp.zeros((CH, DTW), F32)
        csum_t = jnp.zeros((DTW, CH), F32)
        for g in range(NG):
            gs = slice(g * 512, (g + 1) * 512)
            bg = xc[:, DI + g * NST:DI + (g + 1) * NST].astype(BF16)
            cg = xc[:, DI + 512 + g * NST:DI + 512 + (g + 1) * NST].astype(BF16)
            cbm = _dot_nt(cg, bg)
            xdt = xc[:, gs] * dt_x[:, gs]
            hprev = hp_ref[0, :, gs]
            hpb = hprev.astype(BF16)
            gn = g_s[:, gs]
            gnb = gn.astype(BF16)
            dy = dy_s[:, gs]
            csx = csx_s[:, gs]
            ecs = jnp.exp(csx)
            dec = jnp.exp(cs_last_x[:, gs] - csx)
            dye = (dy * ecs).astype(BF16)
            dc = _dot_nt(dye, hpb)
            dprev = _dot_tn(cg, dye)
            dxdt_state = dec * _dot(bg, gnb)
            db = _dot_nt((xdt * dec).astype(BF16), gnb)
            dcb = jnp.zeros((CH, CH), F32)
            for q in range(4):
                h0 = g * 8 + 2 * q
                ps = slice(q * 128, (q + 1) * 128)
                dyp = dy[:, ps]
                l0 = _lmat(cst_s, h0, tril)
                l1 = _lmat(cst_s, h0 + 1, tril)
                m0 = cbm * l0
                m1 = cbm * l1
                dm = _dot_nt(dyp.astype(BF16), _head_pair_rows(xdt[:, ps], lane))
                dm0 = dm[:, :CH]
                dm1 = dm[:, CH:]
                dcb = dcb + dm0 * l0 + dm1 * l1
                for hh, qm in ((h0, dm0 * m0), (h0 + 1, dm1 * m1)):
                    rsum = jnp.where(lane == hh, jnp.sum(qm, axis=1, keepdims=True), rsum)
                    csum_t = jnp.where(ri == hh, jnp.sum(qm, axis=0, keepdims=True), csum_t)
                mst = jnp.concatenate([m0, m1], axis=0).astype(BF16)
                d = _dot_tn(mst, _head_pair_rows(dyp, lane))
                dxdt_s[:, g * 512 + q * 128:g * 512 + (q + 1) * 128] = d + dxdt_state[:, ps]
            yoff = _dot(cg, hpb) * ecs
            dsd_s[:, gs] = xdt * dxdt_state
            dxs_s[:, gs] = dy * yoff
            dcbb = dcb.astype(BF16)
            dxc_s[:, DI + 512 + g * NST:DI + 512 + (g + 1) * NST] = dc + _dot(dcbb, bg)
            dxc_s[:, DI + g * NST:DI + (g + 1) * NST] = db + _dot_tn(dcbb, cg)
            gh_s[:, gs] = jnp.broadcast_to(jnp.sum(gn * hprev, axis=0, keepdims=True), (8, 512))
            g_s[:, gs] = dprev + jnp.exp(cs_last_x[:, gs]) * gn

        xs = xc[:, :DI]
        dy = dy_s[...]
        dxdt = dxdt_s[...]
        cs_last = cs_s[CH - 1:CH, :]
        state_e = _split_dot(dsd_s[...], et, 2)
        dcd = 0.125 * jnp.sum(_split_dot(gh_s[...], et, 2), axis=0, keepdims=True) * jnp.exp(cs_last)
        row = lax.broadcasted_iota(jnp.int32, (CH, 1), 0)
        dcs = rsum - csum_t.T + _split_dot(dxs_s[...], et, 2) - state_e
        dcs = dcs + jnp.where(row == CH - 1, jnp.sum(state_e, axis=0, keepdims=True) + dcd, 0.0)
        dda = _split_dot(dcs, (lane >= ri).astype(BF16), 3, w_left=True)
        ddt = dda * a + _dot((dxdt * xs).astype(BF16), et)
        ddtr = jnp.where(lane < NH, ddt * _sigmoid(dtr), 0.0)
        ddt_ref[...] = ddtr.astype(BF16)
        dsm_ref[0:1, :] += jnp.sum(ddtr, axis=0, keepdims=True)
        dsm_ref[1:2, :] += jnp.sum(dda * dt, axis=0, keepdims=True) * a
        dsm_ref[2:3, :] += jnp.sum(_dot((dy * xs).astype(BF16), et), axis=0, keepdims=True)
        dxc_s[:, :DI] = dxdt * dt_x + dy * dx_ref[...]

        dpre = dxc_s[...] * (sp + xc * (1.0 - sp))
        dpre_b = dpre.astype(BF16)
        dd_s[:CH, :] = dpre_b
        moved = _dot(shift_ref[...], dd_s[...])
        dd_s[CH:CH + 16, :] = dpre_b[:16]
        x = xbc_ref[...].astype(F32)
        dcw_ref[4:5, :] += jnp.sum(dpre, axis=0, keepdims=True)
        dxbc = cw_ref[3:4, :] * dpre
        dcw_ref[3:4, :] += jnp.sum(dpre * x, axis=0, keepdims=True)
        for j in (1, 2, 3):
            ahead = moved[(j - 1) * CH:j * CH]
            dcw_ref[3 - j:4 - j, :] += jnp.sum(ahead * x, axis=0, keepdims=True)
            dxbc = dxbc + cw_ref[3 - j:4 - j, :] * ahead
        dxbc_ref[...] = dxbc.astype(BF16)

    rev = lambda c: nc - 1 - c
    return _call(
        body, ex, name="ssd_bwd", grid=(nc,),
        in_specs=[pl.BlockSpec((CH, CD), lambda c: (rev(c), OFF_XBC // CD)),
                  pl.BlockSpec((CH, CD), lambda c: (rev(c), 0)),
                  pl.BlockSpec((CH, D), lambda c: (rev(c), OFF_ZB // D)), pl.BlockSpec((CH, D), lambda c: (rev(c), OFF_ZB // D + 1)),
                  pl.BlockSpec((CH, DTW), lambda c: (rev(c), 0)),
                  pl.BlockSpec((CH, DI), lambda c: (rev(c), 0)), pl.BlockSpec((CH, DI), lambda c: (rev(c), 0)),
                  pl.BlockSpec((1, NST, DI), lambda c: (rev(c), 0, 0)),
                  _full((4, CD)), _full((1, DTW)), _full((1, DTW)), _full((1, DI)), _full((1, DI)),
                  _full((DTW, DI)), _full((DI, DTW)), _full((3 * CH, 2 * CH))],
        out_specs=[pl.BlockSpec((CH, CD + DI), lambda c: (rev(c), 0)),
                   pl.BlockSpec((CH, DTW), lambda c: (rev(c), 0)), _full((8, CD)), _full((8, DTW)), _full((8, DI))],
        out_shape=[jax.ShapeDtypeStruct((T, CD + DI), BF16), jax.ShapeDtypeStruct((T, DTW), BF16),
                   jax.ShapeDtypeStruct((8, CD), F32), jax.ShapeDtypeStruct((8, DTW), F32), jax.ShapeDtypeStruct((8, DI), F32)],
        scratch_shapes=[pltpu.VMEM((NST, DI), F32), pltpu.VMEM((2 * CH, CD), BF16), pltpu.VMEM((CH, CH), F32), pltpu.VMEM((CH, CH), F32),
                        pltpu.VMEM((CH, DI), F32), pltpu.VMEM((CH, DI), F32), pltpu.VMEM((CH, DI), F32), pltpu.VMEM((CH, DI), F32),
                        pltpu.VMEM((CH, DI), F32), pltpu.VMEM((CH, CD), F32), pltpu.VMEM((8, DI), F32)],
        args=(proj, pre_all, proj, proj, dtr, y, dyb, hprev_all, cw, dtb, alog, dskx, sg, e128, et128, _shift_matrix(down=False)))


def _merge_tile():
    return 256


def _merge(x, yb, proj, p, tgt, gmlp, w_oa, w_ob, w_out, w_pg, w_ple, ple_g, fin_g):
    T = x.shape[0]
    tm = min(T, _merge_tile())

    def body(x_ref, u_ref, v_ref, za_ref, yb_ref, ga_ref, gb_ref, p_ref, t_ref, lg_ref, lb_ref, ws_ref, bs_ref,
             woa, wob, wout, wpg, wple, pg_ref, fg_ref,
             dx1_ref, dx1b_ref, ya_ref, mg_ref, hp_ref, dpre_ref, dpe_ref, doa_ref, dob_ref, dya_ref, dyb_ref, dg_ref, acc_ref, vn_s):
        @pl.when(pl.program_id(0) == 0)
        def _():
            acc_ref[...] = jnp.zeros_like(acc_ref)

        _gmlp_fwd_tile(u_ref, v_ref, za_ref, lg_ref, lb_ref, ws_ref, bs_ref, ya_ref, vn_s)
        oa = _dot(ya_ref[...], woa[...])
        ob = _dot(yb_ref[...], wob[...])
        sa = _sigmoid(ga_ref[...].astype(F32))
        sb = _sigmoid(gb_ref[...].astype(F32))
        mg = sa * oa + sb * ob
        mgb = mg.astype(BF16)
        mg_ref[...] = mgb
        x1 = x_ref[...] + _dot(mgb, wout[...])
        r2 = lax.rsqrt(jnp.mean(x1 * x1, axis=-1, keepdims=True) + EPS)
        xh1 = x1 * r2
        hpb = (xh1 * pg_ref[...]).astype(BF16)
        hp_ref[...] = hpb
        gate = _sigmoid(_dot(hpb, wpg[...]))
        pe = _dot(p_ref[...].astype(BF16), wple[...])
        x2 = x1 + gate * pe
        r3 = lax.rsqrt(jnp.mean(x2 * x2, axis=-1, keepdims=True) + EPS)
        xh2 = x2 * r3
        err = xh2 * fg_ref[...] - t_ref[...]
        acc_ref[2:3, :] += 0.5 * jnp.sum(jnp.mean(err * err, axis=-1, keepdims=True))
        dyo = err * (1.0 / D)
        acc_ref[0:1, :] += jnp.sum(dyo * xh2, axis=0, keepdims=True)
        dn = dyo * fg_ref[...]
        dx2 = r3 * (dn - xh2 * jnp.mean(dn * xh2, axis=-1, keepdims=True))
        dpe_ref[...] = (dx2 * gate).astype(BF16)
        dpre = (dx2 * pe * gate * (1.0 - gate)).astype(BF16)
        dpre_ref[...] = dpre
        dhp = _dot_nt(dpre, wpg[...])
        acc_ref[1:2, :] += jnp.sum(dhp * xh1, axis=0, keepdims=True)
        dhn = dhp * pg_ref[...]
        dx1 = dx2 + r2 * (dhn - xh1 * jnp.mean(dhn * xh1, axis=-1, keepdims=True))
        dx1_ref[...] = dx1
        dx1b = dx1.astype(BF16)
        dx1b_ref[...] = dx1b
        dmg = _dot_nt(dx1b, wout[...])
        doa = (dmg * sa).astype(BF16)
        dob = (dmg * sb).astype(BF16)
        doa_ref[...] = doa
        dob_ref[...] = dob
        dg_ref[:, :D] = (dmg * oa * sa * (1.0 - sa)).astype(BF16)
        dg_ref[:, D:] = (dmg * ob * sb * (1.0 - sb)).astype(BF16)
        dya_ref[...] = _dot_nt(doa, woa[...]).astype(BF16)
        dyb_ref[...] = _dot_nt(dob, wob[...]).astype(BF16)

    row = lambda w: pl.BlockSpec((tm, w), lambda i: (i, 0))
    blk = lambda off: pl.BlockSpec((tm, D), lambda i: (i, off // D))
    wsp = lambda s: pl.BlockSpec(s, lambda i: (0, 0), pipeline_mode=pl.Buffered(1))
    return pl.pallas_call(
        body, name="merge", grid=(T // tm,),
        in_specs=[row(D), blk(OFF_U), blk(OFF_V), blk(OFF_ZA), row(DI), blk(OFF_GA), blk(OFF_GB), row(PLE), row(D),
                  _full((1, D)), _full((1, D)), _full((NG, CH, CH)), _full((NG, CH, 256)),
                  wsp((D, D)), wsp((DI, D)), wsp((D, D)), wsp((D, D)), wsp((PLE, D)), _full((1, D)), _full((1, D))],
        out_specs=[row(D)] * 10 + [row(DI), row(2 * D), _full((8, D))],
        out_shape=[jax.ShapeDtypeStruct((T, D), F32)] + [jax.ShapeDtypeStruct((T, D), BF16)] * 9
        + [jax.ShapeDtypeStruct((T, DI), BF16), jax.ShapeDtypeStruct((T, 2 * D), BF16), jax.ShapeDtypeStruct((8, D), F32)],
        scratch_shapes=[pltpu.VMEM((tm, D), BF16)],
        compiler_params=_cp(("arbitrary",)),
    )(x, proj, proj, proj, yb, proj, proj, p, tgt, *gmlp, w_oa, w_ob, w_out, w_pg, w_ple, ple_g, fin_g)


def _wgrad(a, b, name):
    T, K = a.shape
    N = b.shape[1]
    tt, tk, tn = min(T, 2048), min(K, 1024), min(N, 1024)
    nt = T // tt

    def body(a_ref, b_ref, o_ref, acc_s):
        t = pl.program_id(2)

        @pl.when(t == 0)
        def _():
            acc_s[...] = jnp.zeros_like(acc_s)

        acc_s[...] += _dot_tn(a_ref[...].astype(BF16), b_ref[...])

        @pl.when(t == nt - 1)
        def _():
            o_ref[...] = acc_s[...].astype(BF16)

    return pl.pallas_call(
        body, name=name, grid=(K // tk, N // tn, nt),
        in_specs=[pl.BlockSpec((tt, tk), lambda k, n, t: (t, k)), pl.BlockSpec((tt, tn), lambda k, n, t: (t, n))],
        out_specs=pl.BlockSpec((tk, tn), lambda k, n, t: (k, n)),
        out_shape=jax.ShapeDtypeStruct((K, N), BF16),
        scratch_shapes=[pltpu.VMEM((tk, tn), F32)],
        compiler_params=_cp(("parallel", "parallel", "arbitrary")),
    )(a, b)


def _proj_bwd(x, dx1, norm_g, wt, pieces, ddt, ex):
    T = x.shape[0]
    tm = min(T, 1024)
    nk = OFF_DT // D + 1
    starts = [sum(a.shape[1] for a in pieces[:n]) // D for n in range(len(pieces))]
    ranges = [(s, s + a.shape[1] // D) for s, a in zip(starts, pieces)]
    assert ranges[-1][1] == nk - 1
    npc = len(pieces)

    def body(x_ref, dx1_ref, g_ref, w_ref, wdt_ref, *rest):
        piece_refs, ddt_ref, gx_ref, dng_ref, acc_s = rest[:npc], rest[npc], rest[npc + 1], rest[npc + 2], rest[npc + 3]
        i, k = pl.program_id(0), pl.program_id(1)

        @pl.when((i == 0) & (k == 0))
        def _():
            dng_ref[...] = jnp.zeros_like(dng_ref)

        @pl.when(k == 0)
        def _():
            acc_s[...] = jnp.zeros_like(acc_s)

        for ref, (lo, hi) in zip(piece_refs, ranges):
            @pl.when((k >= lo) & (k < hi))
            def _(ref=ref):
                acc_s[...] += _dot(ref[...], w_ref[...])

        @pl.when(k == nk - 1)
        def _():
            dh = acc_s[...] + _dot(ddt_ref[...], wdt_ref[...])
            xf = x_ref[...]
            r = lax.rsqrt(jnp.mean(xf * xf, axis=-1, keepdims=True) + EPS)
            xh = xf * r
            dng_ref[0:1, :] += jnp.sum(dh * xh, axis=0, keepdims=True)
            dxn = dh * g_ref[...]
            gx_ref[...] = dx1_ref[...] + r * (dxn - xh * jnp.mean(dxn * xh, axis=-1, keepdims=True))

    def piece_spec(lo, hi):
        return pl.BlockSpec((tm, D), lambda i, k: (i, jnp.clip(k - lo, 0, hi - lo - 1)))

    row = pl.BlockSpec((tm, D), lambda i, k: (i, 0))
    return _call(
        body, ex, name="proj_bwd", grid=(T // tm, nk),
        in_specs=[row, row, _full((1, D)), pl.BlockSpec((D, D), lambda i, k: (jnp.minimum(k, nk - 2), 0)),
                  pl.BlockSpec((DTW, D), lambda i, k: (OFF_DT // DTW, 0))]
        + [piece_spec(lo, hi) for lo, hi in ranges] + [pl.BlockSpec((tm, DTW), lambda i, k: (i, 0))],
        out_specs=[row, _full((8, D))],
        out_shape=[jax.ShapeDtypeStruct((T, D), F32), jax.ShapeDtypeStruct((8, D), F32)],
        scratch_shapes=[pltpu.VMEM((tm, D), F32)],
        args=(x, dx1, norm_g, wt, wt, *pieces, ddt))


def _elementwise_tile(R, C, limit=1 << 20):
    if R * C * 4 <= limit:
        return R, C
    rows = [t for t in range(16, R, 16) if R % t == 0 and t * C * 4 <= limit]
    if rows:
        return rows[-1], C
    cols = [t for t in range(128, C, 128) if C % t == 0 and R * t * 4 <= limit]
    return R, cols[-1]


def _adam_update(w, m, v, g):
    c1 = 1.0 - ADAM_B1 ** ADAM_STEP
    c2 = 1.0 - ADAM_B2 ** ADAM_STEP
    mm = ADAM_B1 * m + (1.0 - ADAM_B1) * g
    vv = ADAM_B2 * v + (1.0 - ADAM_B2) * (g * g)
    return -ADAM_LR * ((mm / c1) / (jnp.sqrt(vv / c2) + ADAM_EPS) + ADAM_WD * w), mm, vv


def _sum_parts(parts, name):
    first = parts[0][0] if isinstance(parts[0], tuple) else parts[0]
    R, C = first.shape[-2:]
    tr, tc = _elementwise_tile(R, C)
    flat = pl.BlockSpec((tr, tc), lambda i, j: (i, j))
    slot = lambda k: pl.BlockSpec((None, tr, tc), lambda i, j: (k, i, j))
    specs = [slot(p[1]) if isinstance(p, tuple) else flat for p in parts]
    arrays = [p[0] if isinstance(p, tuple) else p for p in parts]

    def body(*refs):
        g = refs[0][...].astype(F32)
        for r in refs[1:-1]:
            g = g + r[...].astype(F32)
        refs[-1][...] = g

    return pl.pallas_call(body, name=name, grid=(R // tr, C // tc), in_specs=specs, out_specs=flat,
                          out_shape=jax.ShapeDtypeStruct((R, C), F32), compiler_params=_cp(("parallel", "parallel")))(*arrays)


def _adamw_tiles(w, m, v, g, name):
    R = w.shape[0]
    tr = 107
    assert R % tr == 0
    spec = pl.BlockSpec((tr, 8, 128), lambda i: (i, 0, 0))

    def body(w_ref, m_ref, v_ref, g_in, g_ref, d_ref, nm_ref, nv_ref):
        g = g_in[...]
        g_ref[...] = g
        d_ref[...], nm_ref[...], nv_ref[...] = _adam_update(w_ref[...], m_ref[...], v_ref[...], g)

    return pl.pallas_call(body, name=name, grid=(R // tr,), in_specs=[spec] * 4, out_specs=[spec] * 4,
                          out_shape=[jax.ShapeDtypeStruct(w.shape, F32)] * 4, compiler_params=_cp(("parallel",)))(w, m, v, g)


def _adamw(w, m, v, parts, name, part_row=0, block_rows=None):
    R, C = w.shape
    tr, tc = _elementwise_tile(R, C) if block_rows is None else (block_rows, C)
    assert R % tr == 0 and part_row % tr == 0
    first = part_row // tr
    n = len(parts)
    wspec = pl.BlockSpec((tr, tc), lambda i, j: (i, j))
    flat = pl.BlockSpec((tr, tc), lambda i, j: (first + i, j))
    slot = lambda k: pl.BlockSpec((None, tr, tc), lambda i, j: (k, first + i, j))
    part_specs = [slot(p[1]) if isinstance(p, tuple) else flat for p in parts]
    part_arrays = [p[0] if isinstance(p, tuple) else p for p in parts]

    def body(*refs):
        w_ref, m_ref, v_ref = refs[:3]
        g_ref, d_ref, nm_ref, nv_ref = refs[3 + n:]
        g = refs[3][...].astype(F32)
        for r in refs[4:3 + n]:
            g = g + r[...].astype(F32)
        g_ref[...] = g
        d_ref[...], nm_ref[...], nv_ref[...] = _adam_update(w_ref[...], m_ref[...], v_ref[...], g)

    return pl.pallas_call(
        body, name=name, grid=(R // tr, C // tc), in_specs=[wspec] * 3 + part_specs, out_specs=[wspec] * 4,
        out_shape=[jax.ShapeDtypeStruct(w.shape, F32)] * 4, compiler_params=_cp(("parallel", "parallel")),
    )(w, m, v, *part_arrays)


_SMALL_WIDE = {"ln_a_g": 0, "ln_a_b": 1, "final_g": 8, "ple_norm_g": 9, "ssm_norm_g": 16, "conv_b": 44}
_WIDE_CONV_W, _WIDE_ROWS = 32, 56
_WIDE_LOSS = 10
_SMALL_NARROW = {"w_s": (0, NG * CH, 128), "b_s": (512, NG, 128), "dt_bias": (520, 1, NH), "a_log": (521, 1, NH),
                 "d_skip": (522, 1, NH)}
_NARROW_ROWS = 528
_SMALL_SHAPES = {"norm_g": (1, D), "ln_a_g": (1, D), "ln_a_b": (1, D), "ple_norm_g": (1, D), "final_g": (1, D),
                 "ssm_norm_g": (1, DI), "conv_b": (1, CD), "w_s": (NG * CH, CH), "b_s": (NG, CH), "dt_bias": (1, NH),
                 "a_log": (1, NH), "d_skip": (1, NH)}


def _adamw_small(w, m, v, wide_all, narrow_all, norm_all):
    names = list(_SMALL_SHAPES)
    n = len(names)

    def body(*refs):
        wr, mr, vr = refs[:n], refs[n:2 * n], refs[2 * n:3 * n]
        wide_ref, narrow_ref, norm_ref = refs[3 * n:3 * n + 3]
        outs = refs[3 * n + 3:]
        gr, dr, nmr, nvr, cw_ref, loss_ref = outs[:n], outs[n:2 * n], outs[2 * n:3 * n], outs[3 * n:4 * n], outs[4 * n], outs[4 * n + 1]

        def total(ref, rows, lanes):
            acc = ref[0, rows, lanes]
            for d in range(1, NDEV):
                acc = acc + ref[d, rows, lanes]
            return acc

        for k, name in enumerate(names):
            if name in _SMALL_WIDE or name == "norm_g":
                for part in range(_SMALL_SHAPES[name][1] // D):
                    pack, r = (norm_ref, 0) if name == "norm_g" else (wide_ref, _SMALL_WIDE[name] + part)
                    cols = slice(part * D, (part + 1) * D)
                    g = total(pack, slice(r, r + 1), slice(None))
                    gr[k][:, cols] = g
                    dr[k][:, cols], nmr[k][:, cols], nvr[k][:, cols] = _adam_update(wr[k][:, cols], mr[k][:, cols], vr[k][:, cols], g)
            else:
                r, rows, lanes = _SMALL_NARROW[name]
                g = total(narrow_ref, slice(r, r + rows), slice(0, lanes))
                gr[k][...] = g
                dr[k][...], nmr[k][...], nvr[k][...] = _adam_update(wr[k][...], mr[k][...], vr[k][...], g)
        cw_ref[...] = total(wide_ref, slice(_WIDE_CONV_W, _WIDE_CONV_W + 12), slice(None))
        loss_ref[...] = total(wide_ref, slice(_WIDE_LOSS, _WIDE_LOSS + 1), slice(None))

    shapes = [jax.ShapeDtypeStruct(_SMALL_SHAPES[k], F32) for k in names]
    specs = [_full(_SMALL_SHAPES[k]) for k in names]
    res = pl.pallas_call(
        body, name="adamw_small", grid=(1,),
        in_specs=specs * 3 + [_full(wide_all.shape), _full(narrow_all.shape), _full(norm_all.shape)],
        out_specs=specs * 4 + [_full((12, D)), _full((1, D))],
        out_shape=shapes * 4 + [jax.ShapeDtypeStruct((12, D), F32), jax.ShapeDtypeStruct((1, D), F32)],
        compiler_params=_cp(("arbitrary",)),
    )(*[w[k] for k in names], *[m[k] for k in names], *[v[k] for k in names], wide_all, narrow_all, norm_all)
    groups = [dict(zip(names, res[q * n:(q + 1) * n])) for q in range(4)]
    return groups[0], groups[1], groups[2], groups[3], res[4 * n], res[4 * n + 1][0, 0]


def _dev_index(px, py, pc):
    return 4 * px + 2 * py + pc


def _mesh_position():
    return lax.axis_index("x"), lax.axis_index("y"), lax.axis_index("c")


def _gather_exchange(blocks):
    n = len(blocks)

    def make(ins, outs, sems):
        send_sems, recv_sems, local_sems = sems
        x, y, c = _mesh_position()
        me, sibling = (x, y, c), (x, y, 1 - c)
        chips = [(1 - x, y), (x, 1 - y), (1 - x, 1 - y)]

        def copy(a, k, block, to, src=None):
            dst = outs[a].at[_dev_index(*block)]
            return pltpu.make_async_remote_copy(src_ref=dst if src is None else src, dst_ref=dst, send_sem=send_sems.at[a, k],
                                                recv_sem=recv_sems.at[a, k], device_id=to, device_id_type=MESH)

        mine = [pltpu.make_async_copy(ins[a], outs[a].at[_dev_index(*me)], local_sems.at[a]) for a in range(n)]
        first = []
        for a in range(n):
            first.append(copy(a, 0, me, sibling, src=ins[a]))
            first += [copy(a, 1 + j, me, (*chip, c), src=ins[a]) for j, chip in enumerate(chips)]

        def start():
            for cp in mine + first:
                cp.start()

        def finish():
            passed = []
            for j, chip in enumerate(chips):
                for a in range(n):
                    copy(a, 1 + j, (*chip, c), me).wait_recv()
                    fwd = copy(a, 4 + j, (*chip, c), sibling)
                    fwd.start()
                    passed.append(fwd)
            for a in range(n):
                copy(a, 0, sibling, me).wait_recv()
                for j, chip in enumerate(chips):
                    copy(a, 4 + j, (*chip, 1 - c), me).wait_recv()
            for cp in first + passed:
                cp.wait_send()
            for cp in mine:
                cp.wait()

        return start, finish

    return _Exchange(list(blocks), [jax.ShapeDtypeStruct((NDEV,) + b.shape, b.dtype) for b in blocks],
                     [pltpu.SemaphoreType.DMA((n, 7)), pltpu.SemaphoreType.DMA((n, 7)), pltpu.SemaphoreType.DMA((n,))], make)


def _relay_gather_exchange(blocks):
    n = len(blocks)

    def make(ins, outs, sems):
        send_sems, recv_sems, local_sems = sems
        x, y, c = _mesh_position()
        me, sibling = (x, y, c), (x, y, 1 - c)
        x_nbr, y_nbr, diag = (1 - x, y), (x, 1 - y), (1 - x, 1 - y)
        relay_from = (jnp.where(c == 0, x, 1 - x), jnp.where(c == 0, 1 - y, y))
        relay_to = (jnp.where(c == 0, 1 - x, x), jnp.where(c == 0, y, 1 - y))

        def copy(a, k, block, to, src=None):
            dst = outs[a].at[_dev_index(*block)]
            return pltpu.make_async_remote_copy(src_ref=dst if src is None else src, dst_ref=dst, send_sem=send_sems.at[a, k],
                                                recv_sem=recv_sems.at[a, k], device_id=to, device_id_type=MESH)

        mine = [pltpu.make_async_copy(ins[a], outs[a].at[_dev_index(*me)], local_sems.at[a]) for a in range(n)]
        first = []
        for a in range(n):
            first += [copy(a, 0, me, sibling, src=ins[a]), copy(a, 1, me, (*x_nbr, c), src=ins[a]), copy(a, 2, me, (*y_nbr, c), src=ins[a])]

        def start():
            for cp in mine + first:
                cp.start()

        def finish():
            later = []
            for a in range(n):
                copy(a, 1, (*x_nbr, c), me).wait_recv()
                copy(a, 2, (*y_nbr, c), me).wait_recv()
                later.append(copy(a, 3, (*relay_from, c), (*relay_to, c)))
                later += [copy(a, 4, (*x_nbr, c), sibling), copy(a, 5, (*y_nbr, c), sibling)]
                for cp in later[-3:]:
                    cp.start()
            for a in range(n):
                copy(a, 3, (*diag, c), me).wait_recv()
                later.append(copy(a, 6, (*diag, c), sibling))
                later[-1].start()
            for a in range(n):
                copy(a, 0, sibling, me).wait_recv()
                for k, chip in ((4, x_nbr), (5, y_nbr), (6, diag)):
                    copy(a, k, (*chip, 1 - c), me).wait_recv()
            for cp in first + later:
                cp.wait_send()
            for cp in mine:
                cp.wait()

        return start, finish

    return _Exchange(list(blocks), [jax.ShapeDtypeStruct((NDEV,) + b.shape, b.dtype) for b in blocks],
                     [pltpu.SemaphoreType.DMA((n, 7)), pltpu.SemaphoreType.DMA((n, 7)), pltpu.SemaphoreType.DMA((n,))], make)


def _combine(*exchanges):
    def make(ins, outs, sems):
        pairs = []
        for e in exchanges:
            ni, no, ns = len(e.arrays), len(e.out_shape), len(e.sems)
            pairs.append(e.make(ins[:ni], outs[:no], sems[:ns]))
            ins, outs, sems = ins[ni:], outs[no:], sems[ns:]

        def start():
            for s, _ in pairs:
                s()

        def finish():
            for _, f in pairs:
                f()

        return start, finish

    return _Exchange(sum((list(e.arrays) for e in exchanges), []), sum((list(e.out_shape) for e in exchanges), []),
                     sum((list(e.sems) for e in exchanges), []), make)


def _start_wait_all(copies, local=()):
    def start():
        for cp in list(local) + list(copies):
            cp.start()

    def finish():
        for cp in copies:
            cp.wait()
        for cp in local:
            cp.wait()

    return start, finish


def _no_exchange():
    return _Exchange([], [], [], lambda ins, outs, sems: (lambda: None, lambda: None))


def _direct_exchange(grads):
    n = len(grads)

    def make(ins, outs, sems):
        send_sems, recv_sems = sems
        x, y, c = _mesh_position()
        copies = []
        for a in range(n):
            for r in range(1, NDEV):
                peer = (x ^ ((r >> 2) & 1), y ^ ((r >> 1) & 1), c ^ (r & 1))
                copies.append(pltpu.make_async_remote_copy(
                    src_ref=ins[a].at[_dev_index(*peer)], dst_ref=outs[a].at[r - 1], send_sem=send_sems.at[a, r - 1],
                    recv_sem=recv_sems.at[a, r - 1], device_id=peer, device_id_type=MESH))
        return _start_wait_all(copies)

    return _Exchange(list(grads), [jax.ShapeDtypeStruct((NDEV - 1,) + g.shape[1:], g.dtype) for g in grads],
                     [pltpu.SemaphoreType.DMA((n, NDEV - 1)), pltpu.SemaphoreType.DMA((n, NDEV - 1))], make)


def _direct_gather_exchange(smalls):
    n = len(smalls)

    def make(ins, outs, sems):
        send_sems, recv_sems, local_sems = sems
        x, y, c = _mesh_position()
        copies, local = [], []
        for a in range(n):
            slot = outs[a].at[_dev_index(x, y, c)]
            local.append(pltpu.make_async_copy(ins[a], slot, local_sems.at[a]))
            for r in range(1, NDEV):
                peer = (x ^ ((r >> 2) & 1), y ^ ((r >> 1) & 1), c ^ (r & 1))
                copies.append(pltpu.make_async_remote_copy(src_ref=ins[a], dst_ref=slot, send_sem=send_sems.at[a, r - 1],
                                                           recv_sem=recv_sems.at[a, r - 1], device_id=peer, device_id_type=MESH))
        return _start_wait_all(copies, local=local)

    return _Exchange(list(smalls), [jax.ShapeDtypeStruct((NDEV,) + s.shape, s.dtype) for s in smalls],
                     [pltpu.SemaphoreType.DMA((n, 7)), pltpu.SemaphoreType.DMA((n, 7)), pltpu.SemaphoreType.DMA((n,))], make)


_W_IN_ROWS = {"u": (0, 1024), "v": (1024, 2048), "za": (2048, 3072), "zb": (3072, 5120), "xbc": (5120, 8192),
              "dt": (8192, 8224), "ga": (8224, 9248), "gb": (9248, 10272)}
_PROJ_ORDER = ("xbc", "zb", "u", "v", "za", "ga", "gb", "dt")


def _w_in_t_rows(wt):
    z = jnp.zeros((NP - NIN, wt.shape[1]), wt.dtype)
    return jnp.concatenate([wt[slice(*_W_IN_ROWS[n])] for n in _PROJ_ORDER] + [z], axis=0)


_WEIGHTS = ["norm_g", "w_in", "ln_a_g", "ln_a_b", "w_s", "b_s", "conv_w", "conv_b", "dt_bias", "a_log", "d_skip", "ssm_norm_g",
            "w_oa", "w_ob", "w_out", "ple_norm_g", "w_pg", "w_ple", "final_g"]


def _rows_pack(d, dtype):
    return jnp.concatenate([d["w_oa"].reshape(128, D), d["w_ob"].reshape(256, D), d["w_out"].reshape(128, D),
                            d["w_pg"].reshape(128, D), d["w_ple"].reshape(32, D)], axis=0).astype(dtype)


def kernel(x, p, norm_g, w_in, ln_a_g, ln_a_b, w_s, b_s, conv_w, conv_b, dt_bias, a_log, d_skip, ssm_norm_g, w_oa, w_ob, w_out, ple_norm_g, w_pg, w_ple, final_g, loss_target, m_norm_g, m_w_in, m_ln_a_g, m_ln_a_b, m_w_s, m_b_s, m_conv_w, m_conv_b, m_dt_bias, m_a_log, m_d_skip, m_ssm_norm_g, m_w_oa, m_w_ob, m_w_out, m_ple_norm_g, m_w_pg, m_w_ple, m_final_g, v_norm_g, v_w_in, v_ln_a_g, v_ln_a_b, v_w_s, v_b_s, v_conv_w, v_conv_b, v_dt_bias, v_a_log, v_d_skip, v_ssm_norm_g, v_w_oa, v_w_ob, v_w_out, v_ple_norm_g, v_w_pg, v_w_ple, v_final_g):
    args = dict(locals())
    w = {n: args[n] for n in _WEIGHTS}
    m = {n: args["m_" + n] for n in _WEIGHTS}
    v = {n: args["v_" + n] for n in _WEIGHTS}
    T = x.shape[1]
    xi, yi, ci = lax.axis_index("x"), lax.axis_index("y"), lax.axis_index("c")
    me = 4 * xi + 2 * yi + ci
    x2, p2, tgt = x.reshape(T, D), p.reshape(T, PLE), loss_target.reshape(T, D)

    norm_g2 = w["norm_g"].reshape(1, D)
    ws = jnp.where(jnp.tril(jnp.ones((CH, CH), bool))[None], w["w_s"].reshape(NG, CH, CH), 0.0).astype(BF16)
    wst = jnp.transpose(ws, (0, 2, 1))
    bst = jnp.broadcast_to(w["b_s"].reshape(NG, CH, 1), (NG, CH, 256))
    ln_g, ln_b = w["ln_a_g"].reshape(1, D), w["ln_a_b"].reshape(1, D)
    cb = w["conv_b"].reshape(1, CD)
    pad32 = lambda a: jnp.pad(a.reshape(1, NH), ((0, 0), (0, DTW - NH)))
    dtb, alog = pad32(w["dt_bias"]), pad32(w["a_log"])
    dskx = jnp.repeat(w["d_skip"].reshape(NH), HD).reshape(1, DI)
    sg = w["ssm_norm_g"].reshape(1, DI)
    ple_g, fin_g = w["ple_norm_g"].reshape(1, D), w["final_g"].reshape(1, D)
    e128 = (jnp.arange(DTW)[:, None] == (jnp.arange(DI)[None, :] // HD)).astype(BF16)
    et128 = e128.T
    gsel = ((jnp.arange(D)[:, None] // 256) == jnp.arange(128)[None, :]).astype(BF16)

    w_in_t = lambda a: jnp.transpose(a.reshape(D, WSH))
    (a_all,) = _run_exchange(_relay_gather_exchange([w_in_t(w["w_in"]).astype(BF16)]), "all_gather_w_in")
    w_in_full_t = a_all.reshape(NIN, D)
    wp = _w_in_t_rows(w_in_full_t)
    (proj, dtr, h), (r_all, cw_all) = _proj_fwd(x2, norm_g2, wp, _gather_exchange([_rows_pack(w, BF16), w["conv_w"].reshape(4, CD // NDEV)]))
    f_oa = r_all[:, R_OA:R_OB].reshape(D, D)
    f_ob = r_all[:, R_OB:R_OUT].reshape(DI, D)
    f_out = r_all[:, R_OUT:R_PG].reshape(D, D)
    f_pg = r_all[:, R_PG:R_PLE].reshape(D, D)
    f_ple = jnp.transpose(r_all[:, R_PLE:R_ROWS].reshape(NDEV, PLE, 128), (1, 0, 2)).reshape(PLE, D)
    cw = jnp.transpose(cw_all, (1, 0, 2)).reshape(4, CD)

    y, yb, hprev, pre_all = _ssd_fwd(proj, dtr, cw, cb, dtb, alog, dskx, sg, e128)
    dx1, dx1b, ya, mg, hp, dpre, dpe, doa, dob, dya, dyb, dgab, acc = _merge(
        x2, yb, proj, p2, tgt, (ln_g, ln_b, ws, bst), f_oa, f_ob, f_out, f_pg, f_ple, ple_g, fin_g)

    gple = jnp.transpose(_wgrad(p2, dpe, "wgrad_ple").reshape(PLE, NDEV, 128), (1, 0, 2)).reshape(NDEV, 32, D)
    gr = jnp.concatenate([_wgrad(ya, doa, "wgrad_oa").reshape(NDEV, 128, D), _wgrad(yb, dob, "wgrad_ob").reshape(NDEV, 256, D),
                          _wgrad(mg, dx1b, "wgrad_out").reshape(NDEV, 128, D), _wgrad(hp, dpre, "wgrad_pg").reshape(NDEV, 128, D),
                          gple], axis=1)
    (duvz, dws, dbs, dln), _ = _gmlp_bwd(proj, dya, ln_g, ln_b, ws, wst, bst, gsel, _no_exchange())
    (dxz, ddt, dcw, dsm, dsg), (rr,) = _ssd_bwd(proj, pre_all, dtr, y, dyb, hprev, cw, dtb, alog, dskx, sg, e128, et128, _direct_exchange([gr]))

    g_xz = _wgrad(dxz, h, "wgrad_xbc_zb")
    g_w_in_t = jnp.concatenate([_wgrad(duvz, h, "wgrad_u_v_za"), g_xz[CD:], g_xz[:CD], _wgrad(ddt, h, "wgrad_dt")[:NH],
                                _wgrad(dgab, h, "wgrad_ga_gb")], axis=0)
    ga = g_w_in_t.reshape(NDEV, WSH, D)
    wide = jnp.concatenate([dln, acc, dsg.reshape(16, D), dcw.reshape(24, D)], axis=0)
    narrow = jnp.concatenate([dws.reshape(NG * CH, CH), jnp.pad(dbs[:, :NG].T, ((0, 8 - NG), (0, 0))), dsm], axis=0)
    (gx, dng), (ra, wide_all, narrow_all) = _proj_bwd(x2, dx1, norm_g2, wp, [dxz, duvz, dgab], ddt,
                                                      _combine(_direct_exchange([ga]), _direct_gather_exchange([wide, narrow])))
    (norm_all,) = _run_exchange(_direct_gather_exchange([dng]), "all_gather_d_norm_g")
    ga_own = lax.dynamic_index_in_dim(ga, me, 0, keepdims=False)
    gr_own = lax.dynamic_index_in_dim(gr, me, 0, keepdims=False)

    out_g, out_d, out_m, out_v = {}, {}, {}, {}
    outs = (out_g, out_d, out_m, out_v)
    tiles = lambda a: jnp.transpose(a.reshape(8, 128, WSH), (2, 0, 1))
    g2 = _sum_parts([ga_own] + [(ra, k) for k in range(NDEV - 1)], "sum_w_in")
    res = _adamw_tiles(tiles(w["w_in"]), tiles(m["w_in"]), tiles(v["w_in"]), g2.reshape(WSH, 8, 128), "adamw_w_in")
    for dst, val in zip(outs, res):
        dst["w_in"] = jnp.transpose(val, (1, 2, 0)).reshape(1, D, WSH)
    parts_r = [gr_own] + [(rr, k) for k in range(NDEV - 1)]
    for name, row, rows in (("w_oa", R_OA, 128), ("w_ob", R_OB, 256), ("w_out", R_OUT, 128), ("w_pg", R_PG, 128)):
        res = _adamw(w[name].reshape(rows, D), m[name].reshape(rows, D), v[name].reshape(rows, D), parts_r, "adamw_" + name,
                     part_row=row, block_rows=128)
        for dst, val in zip(outs, res):
            dst[name] = val.reshape(1, rows, D)
    res = _adamw(w["w_ple"].reshape(32, D), m["w_ple"].reshape(32, D), v["w_ple"].reshape(32, D), parts_r, "adamw_w_ple",
                 part_row=R_PLE, block_rows=32)
    for dst, val in zip(outs, res):
        dst["w_ple"] = val.reshape(1, PLE, 128)
    two_d = lambda d: {n: d[n].reshape(_SMALL_SHAPES[n]) for n in _SMALL_SHAPES}
    *res, g_cw_wide, loss = _adamw_small(two_d(w), two_d(m), two_d(v), wide_all, narrow_all, norm_all)
    for dst, val in zip(outs, res):
        dst.update({n: val[n].reshape(w[n].shape) for n in _SMALL_SHAPES})
    g_cw = lax.dynamic_slice_in_dim(g_cw_wide.reshape(4, CD), me * (CD // NDEV), CD // NDEV, axis=1).reshape(12, 128)
    res = _adamw(w["conv_w"].reshape(12, 128), m["conv_w"].reshape(12, 128), v["conv_w"].reshape(12, 128), [g_cw], "adamw_conv_w")
    for dst, val in zip((out_g, out_d, out_m, out_v), res):
        dst["conv_w"] = val.reshape(1, 4, CD // NDEV)

    return (loss, gx.reshape(1, T, D), *[out_g[n] for n in _WEIGHTS], *[out_d[n] for n in _WEIGHTS],
            *[out_m[n] for n in _WEIGHTS], *[out_v[n] for n in _WEIGHTS])
```

```python
import functools
import math
from typing import Callable, NamedTuple

import jax
import jax.numpy as jnp
from jax import lax
from jax.experimental import pallas as pl
from jax.experimental.pallas import tpu as pltpu

F32 = jnp.float32
BF16 = jnp.bfloat16
MESH = pl.DeviceIdType.MESH

D = 1024
DI = 2048
CD = 3072
NH = 32
HD = 64
NST = 128
NG = 4
CH = 128
PLE = 256
NIN = 10272
NDEV = 8
WSH = NIN // NDEV
EPS = 1e-6
OFF_XBC, OFF_ZB, OFF_U, OFF_V, OFF_ZA, OFF_GA, OFF_GB, OFF_DT = 0, 3072, 5120, 6144, 7168, 8192, 9216, 10240
NP = 10368
DTW = 128
R_OA, R_OB, R_OUT, R_PG, R_PLE, R_ROWS = 0, 128, 384, 512, 640, 672

ADAM_LR, ADAM_B1, ADAM_B2, ADAM_EPS, ADAM_WD, ADAM_STEP = 0.001, 0.9, 0.999, 1e-08, 0.01, 10

V7X_VMEM_LIMIT = 56 * 1024 * 1024


def _cp(sem=None):
    return pltpu.CompilerParams(dimension_semantics=sem, vmem_limit_bytes=V7X_VMEM_LIMIT)


def _dot(a, b, prec=None):
    return jnp.dot(a, b, preferred_element_type=F32, precision=prec)


def _dot_nt(a, b, prec=None):
    return lax.dot_general(a, b, (((1,), (1,)), ((), ())), preferred_element_type=F32, precision=prec)


def _dot_tn(a, b, prec=None):
    return lax.dot_general(a, b, (((0,), (0,)), ((), ())), preferred_element_type=F32, precision=prec)


def _sigmoid(x):
    return 1.0 / (1.0 + jnp.exp(-x))


def _gelu_and_grad(x):
    c = math.sqrt(2.0 / math.pi)
    x2 = x * x
    t = jnp.tanh(c * (x + 0.044715 * x * x2))
    g = 0.5 * x * (1.0 + t)
    dg = 0.5 * (1.0 + t) + 0.5 * x * (1.0 - t * t) * c * (1.0 + 3.0 * 0.044715 * x2)
    return g, dg


def _gelu(x):
    c = math.sqrt(2.0 / math.pi)
    return 0.5 * x * (1.0 + jnp.tanh(c * (x + 0.044715 * x * x * x)))


def _softplus(x):
    return jnp.maximum(x, 0.0) + jnp.log(1.0 + jnp.exp(-jnp.abs(x)))


def _full(shape):
    n = len(shape)
    return pl.BlockSpec(shape, lambda *_: (0,) * n)


_ANY = pl.BlockSpec(memory_space=pl.ANY)


class _Exchange(NamedTuple):
    arrays: list
    out_shape: list
    sems: list
    make: Callable


def _call(body, ex, *, name, grid, in_specs, out_specs, out_shape, scratch_shapes, args):
    ki, ko, ks = len(in_specs), len(out_specs), len(scratch_shapes)
    ei, eo = len(ex.arrays), len(ex.out_shape)
    last = [g - 1 for g in grid]

    def full_body(*refs):
        r = list(refs)
        ins, eins, r = r[:ki], r[ki:ki + ei], r[ki + ei:]
        outs, eouts, r = r[:ko], r[ko:ko + eo], r[ko + eo:]
        scr, esems = r[:ks], r[ks:]
        start, finish = ex.make(eins, eouts, esems)
        ids = [pl.program_id(a) for a in range(len(grid))]
        is_first = functools.reduce(lambda p, q: p & q, [i == 0 for i in ids])
        is_last = functools.reduce(lambda p, q: p & q, [i == l for i, l in zip(ids, last)])
        pl.when(is_first)(start)
        body(*ins, *outs, *scr)
        pl.when(is_last)(finish)

    res = pl.pallas_call(
        full_body, name=name, grid=grid, in_specs=list(in_specs) + [_ANY] * ei, out_specs=list(out_specs) + [_ANY] * eo,
        out_shape=list(out_shape) + list(ex.out_shape), scratch_shapes=list(scratch_shapes) + list(ex.sems),
        compiler_params=_cp(("arbitrary",) * len(grid)),
    )(*args, *ex.arrays)
    return res[:ko], res[ko:]


def _run_exchange(ex, name):
    ni, no = len(ex.arrays), len(ex.out_shape)

    def body(*refs):
        start, finish = ex.make(refs[:ni], refs[ni:ni + no], refs[ni + no:])
        start()
        finish()

    return pl.pallas_call(body, name=name, in_specs=[_ANY] * ni, out_specs=[_ANY] * no, out_shape=list(ex.out_shape),
                          scratch_shapes=list(ex.sems))(*ex.arrays)


def _proj_fwd(x, norm_g, wp, ex):
    T = x.shape[0]
    tm, tn = min(T, 1024), 2048
    nj = OFF_DT // tn
    assert OFF_DT % tn == 0 and OFF_DT + DTW == NP

    def body(x_ref, g_ref, w_ref, wdt_ref, proj_ref, dt_ref, h_ref, hs_ref):
        j = pl.program_id(1)

        @pl.when(j == 0)
        def _():
            xf = x_ref[...]
            r = lax.rsqrt(jnp.mean(xf * xf, axis=-1, keepdims=True) + EPS)
            h = (xf * r * g_ref[...]).astype(BF16)
            hs_ref[...] = h
            h_ref[...] = h

        proj_ref[...] = _dot_nt(hs_ref[...], w_ref[...]).astype(BF16)

        @pl.when(j == nj - 1)
        def _():
            dt_ref[...] = _dot_nt(hs_ref[...], wdt_ref[...])

    return _call(
        body, ex, name="proj_fwd", grid=(T // tm, nj),
        in_specs=[pl.BlockSpec((tm, D), lambda i, j: (i, 0)), _full((1, D)), pl.BlockSpec((tn, D), lambda i, j: (j, 0)),
                  pl.BlockSpec((DTW, D), lambda i, j: (OFF_DT // DTW, 0))],
        out_specs=[pl.BlockSpec((tm, tn), lambda i, j: (i, j)), pl.BlockSpec((tm, DTW), lambda i, j: (i, 0)),
                   pl.BlockSpec((tm, D), lambda i, j: (i, 0))],
        out_shape=[jax.ShapeDtypeStruct((T, OFF_DT), BF16), jax.ShapeDtypeStruct((T, DTW), F32), jax.ShapeDtypeStruct((T, D), BF16)],
        scratch_shapes=[pltpu.VMEM((tm, D), BF16)], args=(x, norm_g, wp, wp))


def _gmlp_tile():
    return 512


def _gmlp_fwd_tile(u_ref, v_ref, z_ref, lg_ref, lb_ref, ws_ref, bs_ref, ya_ref, vn_s):
    tm = u_ref.shape[0]
    vg = _gelu(v_ref[...].astype(F32))
    mu = jnp.mean(vg, axis=-1, keepdims=True)
    xc = vg - mu
    rstd = lax.rsqrt(jnp.mean(xc * xc, axis=-1, keepdims=True) + EPS)
    vn_s[...] = (xc * rstd * lg_ref[...] + lb_ref[...]).astype(BF16)
    for c in range(tm // CH):
        rs = slice(c * CH, (c + 1) * CH)
        for g in range(NG):
            cs_ = slice(g * 256, (g + 1) * 256)
            sv = _dot(ws_ref[g], vn_s[rs, cs_]) + bs_ref[g]
            z = z_ref[rs, cs_].astype(F32)
            ya_ref[rs, cs_] = (_gelu(u_ref[rs, cs_].astype(F32)) * sv * (z * _sigmoid(z))).astype(BF16)


def _gmlp_bwd(proj, dya, ln_g, ln_b, ws, wst, bst, gsel, ex):
    T = proj.shape[0]
    tm = min(T, _gmlp_tile())

    def body(u_ref, v_ref, z_ref, dy_ref, lg_ref, lb_ref, ws_ref, wst_ref, bs_ref, gsel_ref,
             d_ref, dws_ref, dbs_ref, dln_ref, vn_s, dsv_s, dvn_s):
        du_ref, dv_ref, dz_ref = d_ref.at[:, 0:D], d_ref.at[:, D:2 * D], d_ref.at[:, 2 * D:3 * D]
        @pl.when(pl.program_id(0) == 0)
        def _():
            dws_ref[...] = jnp.zeros_like(dws_ref)
            dbs_ref[...] = jnp.zeros_like(dbs_ref)
            dln_ref[...] = jnp.zeros_like(dln_ref)

        vg, dvg_dv = _gelu_and_grad(v_ref[...].astype(F32))
        mu = jnp.mean(vg, axis=-1, keepdims=True)
        xc = vg - mu
        rstd = lax.rsqrt(jnp.mean(xc * xc, axis=-1, keepdims=True) + EPS)
        vhat = xc * rstd
        vn_s[...] = (vhat * lg_ref[...] + lb_ref[...]).astype(BF16)
        ri = lax.broadcasted_iota(jnp.int32, (CH, CH), 0)
        ci = lax.broadcasted_iota(jnp.int32, (CH, CH), 1)
        tril = (ri >= ci).astype(F32)
        for c in range(tm // CH):
            rs = slice(c * CH, (c + 1) * CH)
            for g in range(NG):
                cs_ = slice(g * 256, (g + 1) * 256)
                vn = vn_s[rs, cs_]
                sv = _dot(ws_ref[g], vn) + bs_ref[g]
                z = z_ref[rs, cs_].astype(F32)
                sz = _sigmoid(z)
                ug, dug_du = _gelu_and_grad(u_ref[rs, cs_].astype(F32))
                dy = dy_ref[rs, cs_].astype(F32)
                zs = z * sz
                t = dy * zs
                dsv_f = dy * sv
                du_ref[rs, cs_] = (zs * dsv_f * dug_du).astype(BF16)
                dz_ref[rs, cs_] = (dsv_f * ug * (sz + zs * (1.0 - sz))).astype(BF16)
                dsv = (t * ug).astype(BF16)
                dsv_s[rs, cs_] = dsv
                dvn_s[rs, cs_] = _dot(wst_ref[g], dsv)
                dws_ref[g] += _dot_nt(dsv, vn) * tril
            dbs_ref[...] += _dot(dsv_s[rs, :], gsel_ref[...])
        dvn = dvn_s[...]
        dln_ref[0:1, :] += jnp.sum(dvn * vhat, axis=0, keepdims=True)
        dln_ref[1:2, :] += jnp.sum(dvn, axis=0, keepdims=True)
        dvh = dvn * lg_ref[...]
        dvg = rstd * (dvh - jnp.mean(dvh, axis=-1, keepdims=True) - vhat * jnp.mean(dvh * vhat, axis=-1, keepdims=True))
        dv_ref[...] = (dvg * dvg_dv).astype(BF16)

    blk = lambda off: pl.BlockSpec((tm, D), lambda i: (i, off // D))
    row = pl.BlockSpec((tm, D), lambda i: (i, 0))
    return _call(
        body, ex, name="gmlp_bwd", grid=(T // tm,),
        in_specs=[blk(OFF_U), blk(OFF_V), blk(OFF_ZA), row, _full((1, D)), _full((1, D)), _full((NG, CH, CH)),
                  _full((NG, CH, CH)), _full((NG, CH, 256)), _full((D, 128))],
        out_specs=[pl.BlockSpec((tm, 3 * D), lambda i: (i, 0)), _full((NG, CH, CH)), _full((CH, 128)), _full((8, D))],
        out_shape=[jax.ShapeDtypeStruct((T, 3 * D), BF16),
                   jax.ShapeDtypeStruct((NG, CH, CH), F32), jax.ShapeDtypeStruct((CH, 128), F32), jax.ShapeDtypeStruct((8, D), F32)],
        scratch_shapes=[pltpu.VMEM((tm, D), BF16), pltpu.VMEM((tm, D), BF16), pltpu.VMEM((tm, D), F32)],
        args=(proj, proj, proj, dya, ln_g, ln_b, ws, wst, bst, gsel))


def _zb_cols(zb0_ref, zb1_ref, cols):
    ref = zb0_ref if cols.start < D else zb1_ref
    return ref[:, cols.start % D:cols.start % D + (cols.stop - cols.start)]


def _shift_matrix(down):
    t = jnp.arange(CH)[None, :, None]
    j = jnp.arange(1, 4)[:, None, None]
    col = jnp.arange(2 * CH)[None, None, :]
    src = CH + t - j if down else t + j
    return (col == src).astype(BF16).reshape(3 * CH, 2 * CH)


def _conv_pre(x, moved, cw_ref, cb_ref):
    pre = cb_ref[...] + cw_ref[3:4, :] * x
    for j in (1, 2, 3):
        pre = pre + cw_ref[3 - j:4 - j, :] * moved[(j - 1) * CH:j * CH]
    return pre


def _split_dot(x, w, parts, w_left=False):
    acc, r = None, x
    for k in range(parts):
        hi = r.astype(BF16)
        d = _dot(w, hi) if w_left else _dot(hi, w)
        acc = d if acc is None else acc + d
        if k + 1 < parts:
            r = r - hi.astype(F32)
    return acc


def _chunk_decays(dt, alog_ref, e_ref, cs_s, cst_s, csx_s):
    a = -jnp.exp(alog_ref[...])
    ri = lax.broadcasted_iota(jnp.int32, (CH, CH), 0)
    ci = lax.broadcasted_iota(jnp.int32, (CH, CH), 1)
    tril = ri >= ci
    cs = _split_dot(dt * a, tril.astype(BF16), 3, w_left=True)
    cs_s[...] = cs
    cst_s[...] = cs.T
    csx_s[...] = _split_dot(cs, e_ref[...], 3)
    return a, tril, ri, ci


def _lmat(cst_s, h, tril):
    rowb = jnp.broadcast_to(cst_s[h:h + 1, :], (CH, CH))
    return jnp.exp(jnp.where(tril, rowb.T - rowb, -jnp.inf))


def _head_pair_rows(v, lane):
    return jnp.concatenate([jnp.where(lane < HD, v, 0.0), jnp.where(lane < HD, 0.0, v)], axis=0).astype(BF16)


def _ssd_fwd(proj, dtr, cw, cb, dtb, alog, dskx, sg, e128):
    T = proj.shape[0]
    nc = T // CH

    def body(xbc_ref, zb0_ref, zb1_ref, dt_ref, cw_ref, cb_ref, dtb_ref, alog_ref, dx_ref, sg_ref, e_ref, shift_ref,
             y_ref, yb_ref, hp_ref, pre_ref, xx_s, h_s, cs_s, cst_s, csx_s, yz_s):
        @pl.when(pl.program_id(0) == 0)
        def _():
            xx_s[...] = jnp.zeros_like(xx_s)
            h_s[...] = jnp.zeros_like(h_s)

        xx_s[CH:, :] = xbc_ref[...]
        moved = _dot(shift_ref[...], xx_s[...])
        xx_s[CH - 16:CH, :] = xbc_ref[CH - 16:, :]
        x = xbc_ref[...].astype(F32)
        pre = _conv_pre(x, moved, cw_ref, cb_ref)
        pre_ref[...] = pre
        xc = pre * _sigmoid(pre)
        dt = _softplus(dt_ref[...] + dtb_ref[...])
        a, tril, _, lane = _chunk_decays(dt, alog_ref, e_ref, cs_s, cst_s, csx_s)
        dt_x = _split_dot(dt, e_ref[...], 2)
        cs_last_x = csx_s[CH - 1:CH, :]
        hp_ref[0] = h_s[...]
        for g in range(NG):
            gs = slice(g * 512, (g + 1) * 512)
            bg = xc[:, DI + g * NST:DI + (g + 1) * NST].astype(BF16)
            cg = xc[:, DI + 512 + g * NST:DI + 512 + (g + 1) * NST].astype(BF16)
            cbm = _dot_nt(cg, bg)
            xg = xc[:, gs]
            xdt = xg * dt_x[:, gs]
            hprev = h_s[:, gs]
            csx = csx_s[:, gs]
            yoff = _dot(cg, hprev.astype(BF16)) * jnp.exp(csx)
            st = _dot_tn(bg, (xdt * jnp.exp(cs_last_x[:, gs] - csx)).astype(BF16))
            h_s[:, gs] = jnp.exp(cs_last_x[:, gs]) * hprev + st
            ssq = jnp.zeros((CH, 1), F32)
            for q in range(4):
                h0 = g * 8 + 2 * q
                ps = slice(q * 128, (q + 1) * 128)
                cols = slice(g * 512 + q * 128, g * 512 + (q + 1) * 128)
                m01 = jnp.concatenate([cbm * _lmat(cst_s, h0, tril), cbm * _lmat(cst_s, h0 + 1, tril)], axis=1).astype(BF16)
                yq = _dot(m01, _head_pair_rows(xdt[:, ps], lane)) + yoff[:, ps] + xg[:, ps] * dx_ref[:, cols]
                y_ref[:, cols] = yq
                z = _zb_cols(zb0_ref, zb1_ref, cols).astype(F32)
                yz = yq * z * _sigmoid(z)
                yz_s[:, cols] = yz
                ssq = ssq + jnp.sum(yz * yz, axis=1, keepdims=True)
            rg = lax.rsqrt(ssq * (1.0 / 512.0) + EPS)
            yb_ref[:, gs] = (yz_s[:, gs] * rg * sg_ref[:, gs]).astype(BF16)

    return pl.pallas_call(
        body, name="ssd_fwd", grid=(nc,),
        in_specs=[pl.BlockSpec((CH, CD), lambda c: (c, OFF_XBC // CD)), pl.BlockSpec((CH, D), lambda c: (c, OFF_ZB // D)),
                  pl.BlockSpec((CH, D), lambda c: (c, OFF_ZB // D + 1)),
                  pl.BlockSpec((CH, DTW), lambda c: (c, 0)), _full((4, CD)), _full((1, CD)), _full((1, DTW)),
                  _full((1, DTW)), _full((1, DI)), _full((1, DI)), _full((DTW, DI)), _full((3 * CH, 2 * CH))],
        out_specs=[pl.BlockSpec((CH, DI), lambda c: (c, 0)), pl.BlockSpec((CH, DI), lambda c: (c, 0)),
                   pl.BlockSpec((1, NST, DI), lambda c: (c, 0, 0)), pl.BlockSpec((CH, CD), lambda c: (c, 0))],
        out_shape=[jax.ShapeDtypeStruct((T, DI), F32), jax.ShapeDtypeStruct((T, DI), BF16),
                   jax.ShapeDtypeStruct((nc, NST, DI), F32), jax.ShapeDtypeStruct((T, CD), F32)],
        scratch_shapes=[pltpu.VMEM((2 * CH, CD), BF16), pltpu.VMEM((NST, DI), F32), pltpu.VMEM((CH, CH), F32),
                        pltpu.VMEM((CH, CH), F32), pltpu.VMEM((CH, DI), F32), pltpu.VMEM((CH, DI), F32)],
        compiler_params=_cp(("arbitrary",)),
    )(proj, proj, proj, dtr, cw, cb, dtb, alog, dskx, sg, e128, _shift_matrix(down=True))


def _ssd_bwd(proj, pre_all, dtr, y, dyb, hprev_all, cw, dtb, alog, dskx, sg, e128, et128, ex):
    T = proj.shape[0]
    nc = T // CH

    def body(xbc_ref, pre_ref, zb0_ref, zb1_ref, dt_ref, y_ref, dyb_ref, hp_ref, cw_ref, dtb_ref, alog_ref, dx_ref, sg_ref,
             e_ref, et_ref, shift_ref, d_ref, ddt_ref, dcw_ref, dsm_ref, dsg_ref,
             g_s, dd_s, cs_s, cst_s, csx_s, dy_s, dxdt_s, dxs_s, dsd_s, dxc_s, gh_s):
        dxbc_ref, dzb_ref = d_ref.at[:, 0:CD], d_ref.at[:, CD:CD + DI]
        i = pl.program_id(0)

        @pl.when(i == 0)
        def _():
            g_s[...] = jnp.zeros_like(g_s)
            dd_s[...] = jnp.zeros_like(dd_s)
            dcw_ref[...] = jnp.zeros_like(dcw_ref)
            dsm_ref[...] = jnp.zeros_like(dsm_ref)
            dsg_ref[...] = jnp.zeros_like(dsg_ref)

        pre = pre_ref[...]
        sp = _sigmoid(pre)
        xc = pre * sp
        dtr = dt_ref[...] + dtb_ref[...]
        dt = _softplus(dtr)
        a, tril, ri, lane = _chunk_decays(dt, alog_ref, e_ref, cs_s, cst_s, csx_s)
        et = et_ref[...]
        dt_x = _split_dot(dt, e_ref[...], 2)
        cs_last_x = csx_s[CH - 1:CH, :]

        for g in range(NG):
            gs = slice(g * 512, (g + 1) * 512)
            z = _zb_cols(zb0_ref, zb1_ref, gs).astype(F32)
            sz = _sigmoid(z)
            yv = y_ref[:, gs]
            zs = z * sz
            yz = yv * zs
            rg = lax.rsqrt(jnp.mean(yz * yz, axis=-1, keepdims=True) + EPS)
            yn = yz * rg
            dyb = dyb_ref[:, gs].astype(F32)
            dsg_ref[0:1, gs] += jnp.sum(dyb * yn, axis=0, keepdims=True)
            dyn = dyb * sg_ref[:, gs]
            dyz = rg * (dyn - yn * jnp.mean(dyn * yn, axis=-1, keepdims=True))
            dy_s[:, gs] = dyz * zs
            dzb_ref[:, gs] = (dyz * yv * (sz + zs * (1.0 - sz))).astype(BF16)

        rsum = jnp.zeros((CH, DTW), F32)
        csum_t = jnp.zeros((DTW, CH), F32)
        for g in range(NG):
            gs = slice(g * 512, (g + 1) * 512)
            bg = xc[:, DI + g * NST:DI + (g + 1) * NST].astype(BF16)
            cg = xc[:, DI + 512 + g * NST:DI + 512 + (g + 1) * NST].astype(BF16)
            cbm = _dot_nt(cg, bg)
            xdt = xc[:, gs] * dt_x[:, gs]
            hprev = hp_ref[0, :, gs]
            hpb = hprev.astype(BF16)
            gn = g_s[:, gs]
            gnb = gn.astype(BF16)
            dy = dy_s[:, gs]
            csx = csx_s[:, gs]
            ecs = jnp.exp(csx)
            dec = jnp.exp(cs_last_x[:, gs] - csx)
            dye = (dy * ecs).astype(BF16)
            dc = _dot_nt(dye, hpb)
            dprev = _dot_tn(cg, dye)
            dxdt_state = dec * _dot(bg, gnb)
            db = _dot_nt((xdt * dec).astype(BF16), gnb)
            dcb = jnp.zeros((CH, CH), F32)
            for q in range(4):
                h0 = g * 8 + 2 * q
                ps = slice(q * 128, (q + 1) * 128)
                dyp = dy[:, ps]
                l0 = _lmat(cst_s, h0, tril)
                l1 = _lmat(cst_s, h0 + 1, tril)
                m0 = cbm * l0
                m1 = cbm * l1
                dm = _dot_nt(dyp.astype(BF16), _head_pair_rows(xdt[:, ps], lane))
                dm0 = dm[:, :CH]
                dm1 = dm[:, CH:]
                dcb = dcb + dm0 * l0 + dm1 * l1
                for hh, qm in ((h0, dm0 * m0), (h0 + 1, dm1 * m1)):
                    rsum = jnp.where(lane == hh, jnp.sum(qm, axis=1, keepdims=True), rsum)
                    csum_t = jnp.where(ri == hh, jnp.sum(qm, axis=0, keepdims=True), csum_t)
                mst = jnp.concatenate([m0, m1], axis=0).astype(BF16)
                d = _dot_tn(mst, _head_pair_rows(dyp, lane))
                dxdt_s[:, g * 512 + q * 128:g * 512 + (q + 1) * 128] = d + dxdt_state[:, ps]
            yoff = _dot(cg, hpb) * ecs
            dsd_s[:, gs] = xdt * dxdt_state
            dxs_s[:, gs] = dy * yoff
            dcbb = dcb.astype(BF16)
            dxc_s[:, DI + 512 + g * NST:DI + 512 + (g + 1) * NST] = dc + _dot(dcbb, bg)
            dxc_s[:, DI + g * NST:DI + (g + 1) * NST] = db + _dot_tn(dcbb, cg)
            gh_s[:, gs] = jnp.broadcast_to(jnp.sum(gn * hprev, axis=0, keepdims=True), (8, 512))
            g_s[:, gs] = dprev + jnp.exp(cs_last_x[:, gs]) * gn

        xs = xc[:, :DI]
        dy = dy_s[...]
        dxdt = dxdt_s[...]
        cs_last = cs_s[CH - 1:CH, :]
        state_e = _split_dot(dsd_s[...], et, 2)
        dcd = 0.125 * jnp.sum(_split_dot(gh_s[...], et, 2), axis=0, keepdims=True) * jnp.exp(cs_last)
        row = lax.broadcasted_iota(jnp.int32, (CH, 1), 0)
        dcs = rsum - csum_t.T + _split_dot(dxs_s[...], et, 2) - state_e
        dcs = dcs + jnp.where(row == CH - 1, jnp.sum(state_e, axis=0, keepdims=True) + dcd, 0.0)
        dda = _split_dot(dcs, (lane >= ri).astype(BF16), 3, w_left=True)
        ddt = dda * a + _dot((dxdt * xs).astype(BF16), et)
        ddtr = jnp.where(lane < NH, ddt * _sigmoid(dtr), 0.0)
        ddt_ref[...] = ddtr.astype(BF16)
        dsm_ref[0:1, :] += jnp.sum(ddtr, axis=0, keepdims=True)
        dsm_ref[1:2, :] += jnp.sum(dda * dt, axis=0, keepdims=True) * a
        dsm_ref[2:3, :] += jnp.sum(_dot((dy * xs).astype(BF16), et), axis=0, keepdims=True)
        dxc_s[:, :DI] = dxdt * dt_x + dy * dx_ref[...]

        dpre = dxc_s[...] * (sp + xc * (1.0 - sp))
        dpre_b = dpre.astype(BF16)
        dd_s[:CH, :] = dpre_b
        moved = _dot(shift_ref[...], dd_s[...])
        dd_s[CH:CH + 16, :] = dpre_b[:16]
        x = xbc_ref[...].astype(F32)
        dcw_ref[4:5, :] += jnp.sum(dpre, axis=0, keepdims=True)
        dxbc = cw_ref[3:4, :] * dpre
        dcw_ref[3:4, :] += jnp.sum(dpre * x, axis=0, keepdims=True)
        for j in (1, 2, 3):
            ahead = moved[(j - 1) * CH:j * CH]
            dcw_ref[3 - j:4 - j, :] += jnp.sum(ahead * x, axis=0, keepdims=True)
            dxbc = dxbc + cw_ref[3 - j:4 - j, :] * ahead
        dxbc_ref[...] = dxbc.astype(BF16)

    rev = lambda c: nc - 1 - c
    return _call(
        body, ex, name="ssd_bwd", grid=(nc,),
        in_specs=[pl.BlockSpec((CH, CD), lambda c: (rev(c), OFF_XBC // CD)),
                  pl.BlockSpec((CH, CD), lambda c: (rev(c), 0)),
                  pl.BlockSpec((CH, D), lambda c: (rev(c), OFF_ZB // D)), pl.BlockSpec((CH, D), lambda c: (rev(c), OFF_ZB // D + 1)),
                  pl.BlockSpec((CH, DTW), lambda c: (rev(c), 0)),
                  pl.BlockSpec((CH, DI), lambda c: (rev(c), 0)), pl.BlockSpec((CH, DI), lambda c: (rev(c), 0)),
                  pl.BlockSpec((1, NST, DI), lambda c: (rev(c), 0, 0)),
                  _full((4, CD)), _full((1, DTW)), _full((1, DTW)), _full((1, DI)), _full((1, DI)),
                  _full((DTW, DI)), _full((DI, DTW)), _full((3 * CH, 2 * CH))],
        out_specs=[pl.BlockSpec((CH, CD + DI), lambda c: (rev(c), 0)),
                   pl.BlockSpec((CH, DTW), lambda c: (rev(c), 0)), _full((8, CD)), _full((8, DTW)), _full((8, DI))],
        out_shape=[jax.ShapeDtypeStruct((T, CD + DI), BF16), jax.ShapeDtypeStruct((T, DTW), BF16),
                   jax.ShapeDtypeStruct((8, CD), F32), jax.ShapeDtypeStruct((8, DTW), F32), jax.ShapeDtypeStruct((8, DI), F32)],
        scratch_shapes=[pltpu.VMEM((NST, DI), F32), pltpu.VMEM((2 * CH, CD), BF16), pltpu.VMEM((CH, CH), F32), pltpu.VMEM((CH, CH), F32),
                        pltpu.VMEM((CH, DI), F32), pltpu.VMEM((CH, DI), F32), pltpu.VMEM((CH, DI), F32), pltpu.VMEM((CH, DI), F32),
                        pltpu.VMEM((CH, DI), F32), pltpu.VMEM((CH, CD), F32), pltpu.VMEM((8, DI), F32)],
        args=(proj, pre_all, proj, proj, dtr, y, dyb, hprev_all, cw, dtb, alog, dskx, sg, e128, et128, _shift_matrix(down=False)))


def _merge_tile():
    return 256


def _merge(x, yb, proj, p, tgt, gmlp, w_oa, w_ob, w_out, w_pg, w_ple, ple_g, fin_g):
    T = x.shape[0]
    tm = min(T, _merge_tile())

    def body(x_ref, u_ref, v_ref, za_ref, yb_ref, ga_ref, gb_ref, p_ref, t_ref, lg_ref, lb_ref, ws_ref, bs_ref,
             woa, wob, wout, wpg, wple, pg_ref, fg_ref,
             dx1_ref, dx1b_ref, ya_ref, mg_ref, hp_ref, dpre_ref, dpe_ref, doa_ref, dob_ref, dya_ref, dyb_ref, dg_ref, acc_ref, vn_s):
        @pl.when(pl.program_id(0) == 0)
        def _():
            acc_ref[...] = jnp.zeros_like(acc_ref)

        _gmlp_fwd_tile(u_ref, v_ref, za_ref, lg_ref, lb_ref, ws_ref, bs_ref, ya_ref, vn_s)
        oa = _dot(ya_ref[...], woa[...])
        ob = _dot(yb_ref[...], wob[...])
        sa = _sigmoid(ga_ref[...].astype(F32))
        sb = _sigmoid(gb_ref[...].astype(F32))
        mg = sa * oa + sb * ob
        mgb = mg.astype(BF16)
        mg_ref[...] = mgb
        x1 = x_ref[...] + _dot(mgb, wout[...])
        r2 = lax.rsqrt(jnp.mean(x1 * x1, axis=-1, keepdims=True) + EPS)
        xh1 = x1 * r2
        hpb = (xh1 * pg_ref[...]).astype(BF16)
        hp_ref[...] = hpb
        gate = _sigmoid(_dot(hpb, wpg[...]))
        pe = _dot(p_ref[...].astype(BF16), wple[...])
        x2 = x1 + gate * pe
        r3 = lax.rsqrt(jnp.mean(x2 * x2, axis=-1, keepdims=True) + EPS)
        xh2 = x2 * r3
        err = xh2 * fg_ref[...] - t_ref[...]
        acc_ref[2:3, :] += 0.5 * jnp.sum(jnp.mean(err * err, axis=-1, keepdims=True))
        dyo = err * (1.0 / D)
        acc_ref[0:1, :] += jnp.sum(dyo * xh2, axis=0, keepdims=True)
        dn = dyo * fg_ref[...]
        dx2 = r3 * (dn - xh2 * jnp.mean(dn * xh2, axis=-1, keepdims=True))
        dpe_ref[...] = (dx2 * gate).astype(BF16)
        dpre = (dx2 * pe * gate * (1.0 - gate)).astype(BF16)
        dpre_ref[...] = dpre
        dhp = _dot_nt(dpre, wpg[...])
        acc_ref[1:2, :] += jnp.sum(dhp * xh1, axis=0, keepdims=True)
        dhn = dhp * pg_ref[...]
        dx1 = dx2 + r2 * (dhn - xh1 * jnp.mean(dhn * xh1, axis=-1, keepdims=True))
        dx1_ref[...] = dx1
        dx1b = dx1.astype(BF16)
        dx1b_ref[...] = dx1b
        dmg = _dot_nt(dx1b, wout[...])
        doa = (dmg * sa).astype(BF16)
        dob = (dmg * sb).astype(BF16)
        doa_ref[...] = doa
        dob_ref[...] = dob
        dg_ref[:, :D] = (dmg * oa * sa * (1.0 - sa)).astype(BF16)
        dg_ref[:, D:] = (dmg * ob * sb * (1.0 - sb)).astype(BF16)
        dya_ref[...] = _dot_nt(doa, woa[...]).astype(BF16)
        dyb_ref[...] = _dot_nt(dob, wob[...]).astype(BF16)

    row = lambda w: pl.BlockSpec((tm, w), lambda i: (i, 0))
    blk = lambda off: pl.BlockSpec((tm, D), lambda i: (i, off // D))
    wsp = lambda s: pl.BlockSpec(s, lambda i: (0, 0), pipeline_mode=pl.Buffered(1))
    return pl.pallas_call(
        body, name="merge", grid=(T // tm,),
        in_specs=[row(D), blk(OFF_U), blk(OFF_V), blk(OFF_ZA), row(DI), blk(OFF_GA), blk(OFF_GB), row(PLE), row(D),
                  _full((1, D)), _full((1, D)), _full((NG, CH, CH)), _full((NG, CH, 256)),
                  wsp((D, D)), wsp((DI, D)), wsp((D, D)), wsp((D, D)), wsp((PLE, D)), _full((1, D)), _full((1, D))],
        out_specs=[row(D)] * 10 + [row(DI), row(2 * D), _full((8, D))],
        out_shape=[jax.ShapeDtypeStruct((T, D), F32)] + [jax.ShapeDtypeStruct((T, D), BF16)] * 9
        + [jax.ShapeDtypeStruct((T, DI), BF16), jax.ShapeDtypeStruct((T, 2 * D), BF16), jax.ShapeDtypeStruct((8, D), F32)],
        scratch_shapes=[pltpu.VMEM((tm, D), BF16)],
        compiler_params=_cp(("arbitrary",)),
    )(x, proj, proj, proj, yb, proj, proj, p, tgt, *gmlp, w_oa, w_ob, w_out, w_pg, w_ple, ple_g, fin_g)


def _wgrad(a, b, name):
    T, K = a.shape
    N = b.shape[1]
    tt, tk, tn = min(T, 2048), min(K, 1024), min(N, 1024)
    nt = T // tt

    def body(a_ref, b_ref, o_ref, acc_s):
        t = pl.program_id(2)

        @pl.when(t == 0)
        def _():
            acc_s[...] = jnp.zeros_like(acc_s)

        acc_s[...] += _dot_tn(a_ref[...].astype(BF16), b_ref[...])

        @pl.when(t == nt - 1)
        def _():
            o_ref[...] = acc_s[...].astype(BF16)

    return pl.pallas_call(
        body, name=name, grid=(K // tk, N // tn, nt),
        in_specs=[pl.BlockSpec((tt, tk), lambda k, n, t: (t, k)), pl.BlockSpec((tt, tn), lambda k, n, t: (t, n))],
        out_specs=pl.BlockSpec((tk, tn), lambda k, n, t: (k, n)),
        out_shape=jax.ShapeDtypeStruct((K, N), BF16),
        scratch_shapes=[pltpu.VMEM((tk, tn), F32)],
        compiler_params=_cp(("parallel", "parallel", "arbitrary")),
    )(a, b)


def _proj_bwd(x, dx1, norm_g, wt, pieces, ddt, ex):
    T = x.shape[0]
    tm = min(T, 1024)
    nk = OFF_DT // D + 1
    starts = [sum(a.shape[1] for a in pieces[:n]) // D for n in range(len(pieces))]
    ranges = [(s, s + a.shape[1] // D) for s, a in zip(starts, pieces)]
    assert ranges[-1][1] == nk - 1
    npc = len(pieces)

    def body(x_ref, dx1_ref, g_ref, w_ref, wdt_ref, *rest):
        piece_refs, ddt_ref, gx_ref, dng_ref, acc_s = rest[:npc], rest[npc], rest[npc + 1], rest[npc + 2], rest[npc + 3]
        i, k = pl.program_id(0), pl.program_id(1)

        @pl.when((i == 0) & (k == 0))
        def _():
            dng_ref[...] = jnp.zeros_like(dng_ref)

        @pl.when(k == 0)
        def _():
            acc_s[...] = jnp.zeros_like(acc_s)

        for ref, (lo, hi) in zip(piece_refs, ranges):
            @pl.when((k >= lo) & (k < hi))
            def _(ref=ref):
                acc_s[...] += _dot(ref[...], w_ref[...])

        @pl.when(k == nk - 1)
        def _():
            dh = acc_s[...] + _dot(ddt_ref[...], wdt_ref[...])
            xf = x_ref[...]
            r = lax.rsqrt(jnp.mean(xf * xf, axis=-1, keepdims=True) + EPS)
            xh = xf * r
            dng_ref[0:1, :] += jnp.sum(dh * xh, axis=0, keepdims=True)
            dxn = dh * g_ref[...]
            gx_ref[...] = dx1_ref[...] + r * (dxn - xh * jnp.mean(dxn * xh, axis=-1, keepdims=True))

    def piece_spec(lo, hi):
        return pl.BlockSpec((tm, D), lambda i, k: (i, jnp.clip(k - lo, 0, hi - lo - 1)))

    row = pl.BlockSpec((tm, D), lambda i, k: (i, 0))
    return _call(
        body, ex, name="proj_bwd", grid=(T // tm, nk),
        in_specs=[row, row, _full((1, D)), pl.BlockSpec((D, D), lambda i, k: (jnp.minimum(k, nk - 2), 0)),
                  pl.BlockSpec((DTW, D), lambda i, k: (OFF_DT // DTW, 0))]
        + [piece_spec(lo, hi) for lo, hi in ranges] + [pl.BlockSpec((tm, DTW), lambda i, k: (i, 0))],
        out_specs=[row, _full((8, D))],
        out_shape=[jax.ShapeDtypeStruct((T, D), F32), jax.ShapeDtypeStruct((8, D), F32)],
        scratch_shapes=[pltpu.VMEM((tm, D), F32)],
        args=(x, dx1, norm_g, wt, wt, *pieces, ddt))


def _elementwise_tile(R, C, limit=1 << 20):
    if R * C * 4 <= limit:
        return R, C
    rows = [t for t in range(16, R, 16) if R % t == 0 and t * C * 4 <= limit]
    if rows:
        return rows[-1], C
    cols = [t for t in range(128, C, 128) if C % t == 0 and R * t * 4 <= limit]
    return R, cols[-1]


def _adam_update(w, m, v, g):
    c1 = 1.0 - ADAM_B1 ** ADAM_STEP
    c2 = 1.0 - ADAM_B2 ** ADAM_STEP
    mm = ADAM_B1 * m + (1.0 - ADAM_B1) * g
    vv = ADAM_B2 * v + (1.0 - ADAM_B2) * (g * g)
    return -ADAM_LR * ((mm / c1) / (jnp.sqrt(vv / c2) + ADAM_EPS) + ADAM_WD * w), mm, vv


def _sum_parts(parts, name, ex):
    first = parts[0][0] if isinstance(parts[0], tuple) else parts[0]
    R, C = first.shape[-2:]
    tr, tc = _elementwise_tile(R, C)
    flat = pl.BlockSpec((tr, tc), lambda i, j: (i, j))
    slot = lambda k: pl.BlockSpec((None, tr, tc), lambda i, j: (k, i, j))
    specs = [slot(p[1]) if isinstance(p, tuple) else flat for p in parts]
    arrays = [p[0] if isinstance(p, tuple) else p for p in parts]

    def body(*refs):
        g = refs[0][...].astype(F32)
        for r in refs[1:-1]:
            g = g + r[...].astype(F32)
        refs[-1][...] = g

    (out,), got = _call(body, ex, name=name, grid=(R // tr, C // tc), in_specs=specs, out_specs=[flat],
                        out_shape=[jax.ShapeDtypeStruct((R, C), F32)], scratch_shapes=[], args=arrays)
    return out, got


def _adamw_tiles(w, m, v, g, name):
    R = w.shape[0]
    tr = 107
    assert R % tr == 0
    spec = pl.BlockSpec((tr, 8, 128), lambda i: (i, 0, 0))

    def body(w_ref, m_ref, v_ref, g_in, g_ref, d_ref, nm_ref, nv_ref):
        g = g_in[...]
        g_ref[...] = g
        d_ref[...], nm_ref[...], nv_ref[...] = _adam_update(w_ref[...], m_ref[...], v_ref[...], g)

    return pl.pallas_call(body, name=name, grid=(R // tr,), in_specs=[spec] * 4, out_specs=[spec] * 4,
                          out_shape=[jax.ShapeDtypeStruct(w.shape, F32)] * 4, compiler_params=_cp(("parallel",)))(w, m, v, g)


def _adamw(w, m, v, parts, name, part_row=0, block_rows=None):
    R, C = w.shape
    tr, tc = _elementwise_tile(R, C) if block_rows is None else (block_rows, C)
    assert R % tr == 0 and part_row % tr == 0
    first = part_row // tr
    n = len(parts)
    wspec = pl.BlockSpec((tr, tc), lambda i, j: (i, j))
    flat = pl.BlockSpec((tr, tc), lambda i, j: (first + i, j))
    slot = lambda k: pl.BlockSpec((None, tr, tc), lambda i, j: (k, first + i, j))
    part_specs = [slot(p[1]) if isinstance(p, tuple) else flat for p in parts]
    part_arrays = [p[0] if isinstance(p, tuple) else p for p in parts]

    def body(*refs):
        w_ref, m_ref, v_ref = refs[:3]
        g_ref, d_ref, nm_ref, nv_ref = refs[3 + n:]
        g = refs[3][...].astype(F32)
        for r in refs[4:3 + n]:
            g = g + r[...].astype(F32)
        g_ref[...] = g
        d_ref[...], nm_ref[...], nv_ref[...] = _adam_update(w_ref[...], m_ref[...], v_ref[...], g)

    return pl.pallas_call(
        body, name=name, grid=(R // tr, C // tc), in_specs=[wspec] * 3 + part_specs, out_specs=[wspec] * 4,
        out_shape=[jax.ShapeDtypeStruct(w.shape, F32)] * 4, compiler_params=_cp(("parallel", "parallel")),
    )(w, m, v, *part_arrays)


_SMALL_WIDE = {"ln_a_g": 0, "ln_a_b": 1, "final_g": 8, "ple_norm_g": 9, "ssm_norm_g": 16, "conv_b": 44}
_WIDE_CONV_W, _WIDE_ROWS = 32, 56
_WIDE_LOSS = 10
_SMALL_NARROW = {"w_s": (0, NG * CH, 128), "b_s": (512, NG, 128), "dt_bias": (520, 1, NH), "a_log": (521, 1, NH),
                 "d_skip": (522, 1, NH)}
_NARROW_ROWS = 528
_SMALL_SHAPES = {"norm_g": (1, D), "ln_a_g": (1, D), "ln_a_b": (1, D), "ple_norm_g": (1, D), "final_g": (1, D),
                 "ssm_norm_g": (1, DI), "conv_b": (1, CD), "w_s": (NG * CH, CH), "b_s": (NG, CH), "dt_bias": (1, NH),
                 "a_log": (1, NH), "d_skip": (1, NH)}


def _adamw_small(w, m, v, wide_all, narrow_all, norm_all):
    names = list(_SMALL_SHAPES)
    n = len(names)

    def body(*refs):
        wr, mr, vr = refs[:n], refs[n:2 * n], refs[2 * n:3 * n]
        wide_ref, narrow_ref, norm_ref = refs[3 * n:3 * n + 3]
        outs = refs[3 * n + 3:]
        gr, dr, nmr, nvr, cw_ref, loss_ref = outs[:n], outs[n:2 * n], outs[2 * n:3 * n], outs[3 * n:4 * n], outs[4 * n], outs[4 * n + 1]

        def total(ref, rows, lanes):
            acc = ref[0, rows, lanes]
            for d in range(1, NDEV):
                acc = acc + ref[d, rows, lanes]
            return acc

        for k, name in enumerate(names):
            if name in _SMALL_WIDE or name == "norm_g":
                for part in range(_SMALL_SHAPES[name][1] // D):
                    pack, r = (norm_ref, 0) if name == "norm_g" else (wide_ref, _SMALL_WIDE[name] + part)
                    cols = slice(part * D, (part + 1) * D)
                    g = total(pack, slice(r, r + 1), slice(None))
                    gr[k][:, cols] = g
                    dr[k][:, cols], nmr[k][:, cols], nvr[k][:, cols] = _adam_update(wr[k][:, cols], mr[k][:, cols], vr[k][:, cols], g)
            else:
                r, rows, lanes = _SMALL_NARROW[name]
                g = total(narrow_ref, slice(r, r + rows), slice(0, lanes))
                gr[k][...] = g
                dr[k][...], nmr[k][...], nvr[k][...] = _adam_update(wr[k][...], mr[k][...], vr[k][...], g)
        cw_ref[...] = total(wide_ref, slice(_WIDE_CONV_W, _WIDE_CONV_W + 12), slice(None))
        loss_ref[...] = total(wide_ref, slice(_WIDE_LOSS, _WIDE_LOSS + 1), slice(None))

    shapes = [jax.ShapeDtypeStruct(_SMALL_SHAPES[k], F32) for k in names]
    specs = [_full(_SMALL_SHAPES[k]) for k in names]
    res = pl.pallas_call(
        body, name="adamw_small", grid=(1,),
        in_specs=specs * 3 + [_full(wide_all.shape), _full(narrow_all.shape), _full(norm_all.shape)],
        out_specs=specs * 4 + [_full((12, D)), _full((1, D))],
        out_shape=shapes * 4 + [jax.ShapeDtypeStruct((12, D), F32), jax.ShapeDtypeStruct((1, D), F32)],
        compiler_params=_cp(("arbitrary",)),
    )(*[w[k] for k in names], *[m[k] for k in names], *[v[k] for k in names], wide_all, narrow_all, norm_all)
    groups = [dict(zip(names, res[q * n:(q + 1) * n])) for q in range(4)]
    return groups[0], groups[1], groups[2], groups[3], res[4 * n], res[4 * n + 1][0, 0]


def _dev_index(px, py, pc):
    return 4 * px + 2 * py + pc


def _mesh_position():
    return lax.axis_index("x"), lax.axis_index("y"), lax.axis_index("c")


def _gather_exchange(blocks):
    n = len(blocks)

    def make(ins, outs, sems):
        send_sems, recv_sems, local_sems = sems
        x, y, c = _mesh_position()
        me, sibling = (x, y, c), (x, y, 1 - c)
        chips = [(1 - x, y), (x, 1 - y), (1 - x, 1 - y)]

        def copy(a, k, block, to, src=None):
            dst = outs[a].at[_dev_index(*block)]
            return pltpu.make_async_remote_copy(src_ref=dst if src is None else src, dst_ref=dst, send_sem=send_sems.at[a, k],
                                                recv_sem=recv_sems.at[a, k], device_id=to, device_id_type=MESH)

        mine = [pltpu.make_async_copy(ins[a], outs[a].at[_dev_index(*me)], local_sems.at[a]) for a in range(n)]
        first = []
        for a in range(n):
            first.append(copy(a, 0, me, sibling, src=ins[a]))
            first += [copy(a, 1 + j, me, (*chip, c), src=ins[a]) for j, chip in enumerate(chips)]

        def start():
            for cp in mine + first:
                cp.start()

        def finish():
            passed = []
            for j, chip in enumerate(chips):
                for a in range(n):
                    copy(a, 1 + j, (*chip, c), me).wait_recv()
                    fwd = copy(a, 4 + j, (*chip, c), sibling)
                    fwd.start()
                    passed.append(fwd)
            for a in range(n):
                copy(a, 0, sibling, me).wait_recv()
                for j, chip in enumerate(chips):
                    copy(a, 4 + j, (*chip, 1 - c), me).wait_recv()
            for cp in first + passed:
                cp.wait_send()
            for cp in mine:
                cp.wait()

        return start, finish

    return _Exchange(list(blocks), [jax.ShapeDtypeStruct((NDEV,) + b.shape, b.dtype) for b in blocks],
                     [pltpu.SemaphoreType.DMA((n, 7)), pltpu.SemaphoreType.DMA((n, 7)), pltpu.SemaphoreType.DMA((n,))], make)


def _relay_gather_exchange(blocks):
    n = len(blocks)

    def make(ins, outs, sems):
        send_sems, recv_sems, local_sems = sems
        x, y, c = _mesh_position()
        me, sibling = (x, y, c), (x, y, 1 - c)
        x_nbr, y_nbr, diag = (1 - x, y), (x, 1 - y), (1 - x, 1 - y)
        relay_from = (jnp.where(c == 0, x, 1 - x), jnp.where(c == 0, 1 - y, y))
        relay_to = (jnp.where(c == 0, 1 - x, x), jnp.where(c == 0, y, 1 - y))

        def copy(a, k, block, to, src=None):
            dst = outs[a].at[_dev_index(*block)]
            return pltpu.make_async_remote_copy(src_ref=dst if src is None else src, dst_ref=dst, send_sem=send_sems.at[a, k],
                                                recv_sem=recv_sems.at[a, k], device_id=to, device_id_type=MESH)

        mine = [pltpu.make_async_copy(ins[a], outs[a].at[_dev_index(*me)], local_sems.at[a]) for a in range(n)]
        first = []
        for a in range(n):
            first += [copy(a, 0, me, sibling, src=ins[a]), copy(a, 1, me, (*x_nbr, c), src=ins[a]), copy(a, 2, me, (*y_nbr, c), src=ins[a])]

        def start():
            for cp in mine + first:
                cp.start()

        def finish():
            later = []
            for a in range(n):
                copy(a, 1, (*x_nbr, c), me).wait_recv()
                copy(a, 2, (*y_nbr, c), me).wait_recv()
                later.append(copy(a, 3, (*relay_from, c), (*relay_to, c)))
                later += [copy(a, 4, (*x_nbr, c), sibling), copy(a, 5, (*y_nbr, c), sibling)]
                for cp in later[-3:]:
                    cp.start()
            for a in range(n):
                copy(a, 3, (*diag, c), me).wait_recv()
                later.append(copy(a, 6, (*diag, c), sibling))
                later[-1].start()
            for a in range(n):
                copy(a, 0, sibling, me).wait_recv()
                for k, chip in ((4, x_nbr), (5, y_nbr), (6, diag)):
                    copy(a, k, (*chip, 1 - c), me).wait_recv()
            for cp in first + later:
                cp.wait_send()
            for cp in mine:
                cp.wait()

        return start, finish

    return _Exchange(list(blocks), [jax.ShapeDtypeStruct((NDEV,) + b.shape, b.dtype) for b in blocks],
                     [pltpu.SemaphoreType.DMA((n, 7)), pltpu.SemaphoreType.DMA((n, 7)), pltpu.SemaphoreType.DMA((n,))], make)


def _combine(*exchanges):
    def make(ins, outs, sems):
        pairs = []
        for e in exchanges:
            ni, no, ns = len(e.arrays), len(e.out_shape), len(e.sems)
            pairs.append(e.make(ins[:ni], outs[:no], sems[:ns]))
            ins, outs, sems = ins[ni:], outs[no:], sems[ns:]

        def start():
            for s, _ in pairs:
                s()

        def finish():
            for _, f in pairs:
                f()

        return start, finish

    return _Exchange(sum((list(e.arrays) for e in exchanges), []), sum((list(e.out_shape) for e in exchanges), []),
                     sum((list(e.sems) for e in exchanges), []), make)


def _start_wait_all(copies, local=()):
    def start():
        for cp in list(local) + list(copies):
            cp.start()

    def finish():
        for cp in copies:
            cp.wait()
        for cp in local:
            cp.wait()

    return start, finish


def _no_exchange():
    return _Exchange([], [], [], lambda ins, outs, sems: (lambda: None, lambda: None))


def _direct_exchange(grads):
    n = len(grads)

    def make(ins, outs, sems):
        send_sems, recv_sems = sems
        x, y, c = _mesh_position()
        copies = []
        for a in range(n):
            for r in range(1, NDEV):
                peer = (x ^ ((r >> 2) & 1), y ^ ((r >> 1) & 1), c ^ (r & 1))
                copies.append(pltpu.make_async_remote_copy(
                    src_ref=ins[a].at[_dev_index(*peer)], dst_ref=outs[a].at[r - 1], send_sem=send_sems.at[a, r - 1],
                    recv_sem=recv_sems.at[a, r - 1], device_id=peer, device_id_type=MESH))
        return _start_wait_all(copies)

    return _Exchange(list(grads), [jax.ShapeDtypeStruct((NDEV - 1,) + g.shape[1:], g.dtype) for g in grads],
                     [pltpu.SemaphoreType.DMA((n, NDEV - 1)), pltpu.SemaphoreType.DMA((n, NDEV - 1))], make)


def _direct_gather_exchange(smalls):
    n = len(smalls)

    def make(ins, outs, sems):
        send_sems, recv_sems, local_sems = sems
        x, y, c = _mesh_position()
        copies, local = [], []
        for a in range(n):
            slot = outs[a].at[_dev_index(x, y, c)]
            local.append(pltpu.make_async_copy(ins[a], slot, local_sems.at[a]))
            for r in range(1, NDEV):
                peer = (x ^ ((r >> 2) & 1), y ^ ((r >> 1) & 1), c ^ (r & 1))
                copies.append(pltpu.make_async_remote_copy(src_ref=ins[a], dst_ref=slot, send_sem=send_sems.at[a, r - 1],
                                                           recv_sem=recv_sems.at[a, r - 1], device_id=peer, device_id_type=MESH))
        return _start_wait_all(copies, local=local)

    return _Exchange(list(smalls), [jax.ShapeDtypeStruct((NDEV,) + s.shape, s.dtype) for s in smalls],
                     [pltpu.SemaphoreType.DMA((n, 7)), pltpu.SemaphoreType.DMA((n, 7)), pltpu.SemaphoreType.DMA((n,))], make)


_W_IN_ROWS = {"u": (0, 1024), "v": (1024, 2048), "za": (2048, 3072), "zb": (3072, 5120), "xbc": (5120, 8192),
              "dt": (8192, 8224), "ga": (8224, 9248), "gb": (9248, 10272)}
_PROJ_ORDER = ("xbc", "zb", "u", "v", "za", "ga", "gb", "dt")


def _w_in_t_rows(wt):
    z = jnp.zeros((NP - NIN, wt.shape[1]), wt.dtype)
    return jnp.concatenate([wt[slice(*_W_IN_ROWS[n])] for n in _PROJ_ORDER] + [z], axis=0)


_WEIGHTS = ["norm_g", "w_in", "ln_a_g", "ln_a_b", "w_s", "b_s", "conv_w", "conv_b", "dt_bias", "a_log", "d_skip", "ssm_norm_g",
            "w_oa", "w_ob", "w_out", "ple_norm_g", "w_pg", "w_ple", "final_g"]


def _rows_pack(d, dtype):
    return jnp.concatenate([d["w_oa"].reshape(128, D), d["w_ob"].reshape(256, D), d["w_out"].reshape(128, D),
                            d["w_pg"].reshape(128, D), d["w_ple"].reshape(32, D)], axis=0).astype(dtype)


def kernel(x, p, norm_g, w_in, ln_a_g, ln_a_b, w_s, b_s, conv_w, conv_b, dt_bias, a_log, d_skip, ssm_norm_g, w_oa, w_ob, w_out, ple_norm_g, w_pg, w_ple, final_g, loss_target, m_norm_g, m_w_in, m_ln_a_g, m_ln_a_b, m_w_s, m_b_s, m_conv_w, m_conv_b, m_dt_bias, m_a_log, m_d_skip, m_ssm_norm_g, m_w_oa, m_w_ob, m_w_out, m_ple_norm_g, m_w_pg, m_w_ple, m_final_g, v_norm_g, v_w_in, v_ln_a_g, v_ln_a_b, v_w_s, v_b_s, v_conv_w, v_conv_b, v_dt_bias, v_a_log, v_d_skip, v_ssm_norm_g, v_w_oa, v_w_ob, v_w_out, v_ple_norm_g, v_w_pg, v_w_ple, v_final_g):
    args = dict(locals())
    w = {n: args[n] for n in _WEIGHTS}
    m = {n: args["m_" + n] for n in _WEIGHTS}
    v = {n: args["v_" + n] for n in _WEIGHTS}
    T = x.shape[1]
    xi, yi, ci = lax.axis_index("x"), lax.axis_index("y"), lax.axis_index("c")
    me = 4 * xi + 2 * yi + ci
    x2, p2, tgt = x.reshape(T, D), p.reshape(T, PLE), loss_target.reshape(T, D)

    norm_g2 = w["norm_g"].reshape(1, D)
    ws = jnp.where(jnp.tril(jnp.ones((CH, CH), bool))[None], w["w_s"].reshape(NG, CH, CH), 0.0).astype(BF16)
    wst = jnp.transpose(ws, (0, 2, 1))
    bst = jnp.broadcast_to(w["b_s"].reshape(NG, CH, 1), (NG, CH, 256))
    ln_g, ln_b = w["ln_a_g"].reshape(1, D), w["ln_a_b"].reshape(1, D)
    cb = w["conv_b"].reshape(1, CD)
    pad32 = lambda a: jnp.pad(a.reshape(1, NH), ((0, 0), (0, DTW - NH)))
    dtb, alog = pad32(w["dt_bias"]), pad32(w["a_log"])
    dskx = jnp.repeat(w["d_skip"].reshape(NH), HD).reshape(1, DI)
    sg = w["ssm_norm_g"].reshape(1, DI)
    ple_g, fin_g = w["ple_norm_g"].reshape(1, D), w["final_g"].reshape(1, D)
    e128 = (jnp.arange(DTW)[:, None] == (jnp.arange(DI)[None, :] // HD)).astype(BF16)
    et128 = e128.T
    gsel = ((jnp.arange(D)[:, None] // 256) == jnp.arange(128)[None, :]).astype(BF16)

    w_in_t = lambda a: jnp.transpose(a.reshape(D, WSH))
    (a_all,) = _run_exchange(_relay_gather_exchange([w_in_t(w["w_in"]).astype(BF16)]), "all_gather_w_in")
    w_in_full_t = a_all.reshape(NIN, D)
    wp = _w_in_t_rows(w_in_full_t)
    (proj, dtr, h), (r_all, cw_all) = _proj_fwd(x2, norm_g2, wp, _gather_exchange([_rows_pack(w, BF16), w["conv_w"].reshape(4, CD // NDEV)]))
    f_oa = r_all[:, R_OA:R_OB].reshape(D, D)
    f_ob = r_all[:, R_OB:R_OUT].reshape(DI, D)
    f_out = r_all[:, R_OUT:R_PG].reshape(D, D)
    f_pg = r_all[:, R_PG:R_PLE].reshape(D, D)
    f_ple = jnp.transpose(r_all[:, R_PLE:R_ROWS].reshape(NDEV, PLE, 128), (1, 0, 2)).reshape(PLE, D)
    cw = jnp.transpose(cw_all, (1, 0, 2)).reshape(4, CD)

    y, yb, hprev, pre_all = _ssd_fwd(proj, dtr, cw, cb, dtb, alog, dskx, sg, e128)
    dx1, dx1b, ya, mg, hp, dpre, dpe, doa, dob, dya, dyb, dgab, acc = _merge(
        x2, yb, proj, p2, tgt, (ln_g, ln_b, ws, bst), f_oa, f_ob, f_out, f_pg, f_ple, ple_g, fin_g)

    gple = jnp.transpose(_wgrad(p2, dpe, "wgrad_ple").reshape(PLE, NDEV, 128), (1, 0, 2)).reshape(NDEV, 32, D)
    gr = jnp.concatenate([_wgrad(ya, doa, "wgrad_oa").reshape(NDEV, 128, D), _wgrad(yb, dob, "wgrad_ob").reshape(NDEV, 256, D),
                          _wgrad(mg, dx1b, "wgrad_out").reshape(NDEV, 128, D), _wgrad(hp, dpre, "wgrad_pg").reshape(NDEV, 128, D),
                          gple], axis=1)
    (duvz, dws, dbs, dln), _ = _gmlp_bwd(proj, dya, ln_g, ln_b, ws, wst, bst, gsel, _no_exchange())
    (dxz, ddt, dcw, dsm, dsg), (rr,) = _ssd_bwd(proj, pre_all, dtr, y, dyb, hprev, cw, dtb, alog, dskx, sg, e128, et128, _direct_exchange([gr]))

    g_xz = _wgrad(dxz, h, "wgrad_xbc_zb")
    g_w_in_t = jnp.concatenate([_wgrad(duvz, h, "wgrad_u_v_za"), g_xz[CD:], g_xz[:CD], _wgrad(ddt, h, "wgrad_dt")[:NH],
                                _wgrad(dgab, h, "wgrad_ga_gb")], axis=0)
    ga = g_w_in_t.reshape(NDEV, WSH, D)
    wide = jnp.concatenate([dln, acc, dsg.reshape(16, D), dcw.reshape(24, D)], axis=0)
    narrow = jnp.concatenate([dws.reshape(NG * CH, CH), jnp.pad(dbs[:, :NG].T, ((0, 8 - NG), (0, 0))), dsm], axis=0)
    (gx, dng), (ra, wide_all, narrow_all) = _proj_bwd(x2, dx1, norm_g2, wp, [dxz, duvz, dgab], ddt,
                                                      _combine(_direct_exchange([ga]), _direct_gather_exchange([wide, narrow])))
    ga_own = lax.dynamic_index_in_dim(ga, me, 0, keepdims=False)
    gr_own = lax.dynamic_index_in_dim(gr, me, 0, keepdims=False)

    out_g, out_d, out_m, out_v = {}, {}, {}, {}
    outs = (out_g, out_d, out_m, out_v)
    tiles = lambda a: jnp.transpose(a.reshape(8, 128, WSH), (2, 0, 1))
    g2, (norm_all,) = _sum_parts([ga_own] + [(ra, k) for k in range(NDEV - 1)], "sum_w_in", _direct_gather_exchange([dng]))
    res = _adamw_tiles(tiles(w["w_in"]), tiles(m["w_in"]), tiles(v["w_in"]), g2.reshape(WSH, 8, 128), "adamw_w_in")
    for dst, val in zip(outs, res):
        dst["w_in"] = jnp.transpose(val, (1, 2, 0)).reshape(1, D, WSH)
    parts_r = [gr_own] + [(rr, k) for k in range(NDEV - 1)]
    for name, row, rows in (("w_oa", R_OA, 128), ("w_ob", R_OB, 256), ("w_out", R_OUT, 128), ("w_pg", R_PG, 128)):
        res = _adamw(w[name].reshape(rows, D), m[name].reshape(rows, D), v[name].reshape(rows, D), parts_r, "adamw_" + name,
                     part_row=row, block_rows=128)
        for dst, val in zip(outs, res):
            dst[name] = val.reshape(1, rows, D)
    res = _adamw(w["w_ple"].reshape(32, D), m["w_ple"].reshape(32, D), v["w_ple"].reshape(32, D), parts_r, "adamw_w_ple",
                 part_row=R_PLE, block_rows=32)
    for dst, val in zip(outs, res):
        dst["w_ple"] = val.reshape(1, PLE, 128)
    two_d = lambda d: {n: d[n].reshape(_SMALL_SHAPES[n]) for n in _SMALL_SHAPES}
    *res, g_cw_wide, loss = _adamw_small(two_d(w), two_d(m), two_d(v), wide_all, narrow_all, norm_all)
    for dst, val in zip(outs, res):
        dst.update({n: val[n].reshape(w[n].shape) for n in _SMALL_SHAPES})
    g_cw = lax.dynamic_slice_in_dim(g_cw_wide.reshape(4, CD), me * (CD // NDEV), CD // NDEV, axis=1).reshape(12, 128)
    res = _adamw(w["conv_w"].reshape(12, 128), m["conv_w"].reshape(12, 128), v["conv_w"].reshape(12, 128), [g_cw], "adamw_conv_w")
    for dst, val in zip((out_g, out_d, out_m, out_v), res):
        dst["conv_w"] = val.reshape(1, 4, CD // NDEV)

    return (loss, gx.reshape(1, T, D), *[out_g[n] for n in _WEIGHTS], *[out_d[n] for n in _WEIGHTS],
            *[out_m[n] for n in _WEIGHTS], *[out_v[n] for n in _WEIGHTS])
```

```python
import functools
import math
from typing import Callable, NamedTuple

import jax
import jax.numpy as jnp
from jax import lax
from jax.experimental import pallas as pl
from jax.experimental.pallas import tpu as pltpu

F32 = jnp.float32
BF16 = jnp.bfloat16
MESH = pl.DeviceIdType.MESH

D = 1024
DI = 2048
CD = 3072
NH = 32
HD = 64
NST = 128
NG = 4
CH = 128
PLE = 256
NIN = 10272
NDEV = 8
WSH = NIN // NDEV
EPS = 1e-6
OFF_XBC, OFF_ZB, OFF_U, OFF_V, OFF_ZA, OFF_GA, OFF_GB, OFF_DT = 0, 3072, 5120, 6144, 7168, 8192, 9216, 10240
NP = 10368
DTW = 128
R_OA, R_OB, R_OUT, R_PG, R_PLE, R_ROWS = 0, 128, 384, 512, 640, 672

ADAM_LR, ADAM_B1, ADAM_B2, ADAM_EPS, ADAM_WD, ADAM_STEP = 0.001, 0.9, 0.999, 1e-08, 0.01, 10

V7X_VMEM_LIMIT = 56 * 1024 * 1024


def _cp(sem=None):
    return pltpu.CompilerParams(dimension_semantics=sem, vmem_limit_bytes=V7X_VMEM_LIMIT)


def _dot(a, b, prec=None):
    return jnp.dot(a, b, preferred_element_type=F32, precision=prec)


def _dot_nt(a, b, prec=None):
    return lax.dot_general(a, b, (((1,), (1,)), ((), ())), preferred_element_type=F32, precision=prec)


def _dot_tn(a, b, prec=None):
    return lax.dot_general(a, b, (((0,), (0,)), ((), ())), preferred_element_type=F32, precision=prec)


def _sigmoid(x):
    return 1.0 / (1.0 + jnp.exp(-x))


def _gelu_and_grad(x):
    c = math.sqrt(2.0 / math.pi)
    x2 = x * x
    t = jnp.tanh(c * (x + 0.044715 * x * x2))
    g = 0.5 * x * (1.0 + t)
    dg = 0.5 * (1.0 + t) + 0.5 * x * (1.0 - t * t) * c * (1.0 + 3.0 * 0.044715 * x2)
    return g, dg


def _gelu(x):
    c = math.sqrt(2.0 / math.pi)
    return 0.5 * x * (1.0 + jnp.tanh(c * (x + 0.044715 * x * x * x)))


def _softplus(x):
    return jnp.maximum(x, 0.0) + jnp.log(1.0 + jnp.exp(-jnp.abs(x)))


def _full(shape):
    n = len(shape)
    return pl.BlockSpec(shape, lambda *_: (0,) * n)


_ANY = pl.BlockSpec(memory_space=pl.ANY)


class _Exchange(NamedTuple):
    arrays: list
    out_shape: list
    sems: list
    make: Callable


def _call(body, ex, *, name, grid, in_specs, out_specs, out_shape, scratch_shapes, args):
    ki, ko, ks = len(in_specs), len(out_specs), len(scratch_shapes)
    ei, eo = len(ex.arrays), len(ex.out_shape)
    last = [g - 1 for g in grid]

    def full_body(*refs):
        r = list(refs)
        ins, eins, r = r[:ki], r[ki:ki + ei], r[ki + ei:]
        outs, eouts, r = r[:ko], r[ko:ko + eo], r[ko + eo:]
        scr, esems = r[:ks], r[ks:]
        start, finish = ex.make(eins, eouts, esems)
        ids = [pl.program_id(a) for a in range(len(grid))]
        is_first = functools.reduce(lambda p, q: p & q, [i == 0 for i in ids])
        is_last = functools.reduce(lambda p, q: p & q, [i == l for i, l in zip(ids, last)])
        pl.when(is_first)(start)
        body(*ins, *outs, *scr)
        pl.when(is_last)(finish)

    res = pl.pallas_call(
        full_body, name=name, grid=grid, in_specs=list(in_specs) + [_ANY] * ei, out_specs=list(out_specs) + [_ANY] * eo,
        out_shape=list(out_shape) + list(ex.out_shape), scratch_shapes=list(scratch_shapes) + list(ex.sems),
        compiler_params=_cp(("arbitrary",) * len(grid)),
    )(*args, *ex.arrays)
    return res[:ko], res[ko:]


def _run_exchange(ex, name):
    ni, no = len(ex.arrays), len(ex.out_shape)

    def body(*refs):
        start, finish = ex.make(refs[:ni], refs[ni:ni + no], refs[ni + no:])
        start()
        finish()

    return pl.pallas_call(body, name=name, in_specs=[_ANY] * ni, out_specs=[_ANY] * no, out_shape=list(ex.out_shape),
                          scratch_shapes=list(ex.sems))(*ex.arrays)


def _proj_fwd(x, norm_g, wp, ex):
    T = x.shape[0]
    tm, tn = min(T, 1024), 2048
    nj = OFF_DT // tn
    assert OFF_DT % tn == 0 and OFF_DT + DTW == NP

    def body(x_ref, g_ref, w_ref, wdt_ref, proj_ref, dt_ref, h_ref, hs_ref):
        j = pl.program_id(1)

        @pl.when(j == 0)
        def _():
            xf = x_ref[...]
            r = lax.rsqrt(jnp.mean(xf * xf, axis=-1, keepdims=True) + EPS)
            h = (xf * r * g_ref[...]).astype(BF16)
            hs_ref[...] = h
            h_ref[...] = h

        proj_ref[...] = _dot_nt(hs_ref[...], w_ref[...]).astype(BF16)

        @pl.when(j == nj - 1)
        def _():
            dt_ref[...] = _dot_nt(hs_ref[...], wdt_ref[...])

    return _call(
        body, ex, name="proj_fwd", grid=(T // tm, nj),
        in_specs=[pl.BlockSpec((tm, D), lambda i, j: (i, 0)), _full((1, D)), pl.BlockSpec((tn, D), lambda i, j: (j, 0)),
                  pl.BlockSpec((DTW, D), lambda i, j: (OFF_DT // DTW, 0))],
        out_specs=[pl.BlockSpec((tm, tn), lambda i, j: (i, j)), pl.BlockSpec((tm, DTW), lambda i, j: (i, 0)),
                   pl.BlockSpec((tm, D), lambda i, j: (i, 0))],
        out_shape=[jax.ShapeDtypeStruct((T, OFF_DT), BF16), jax.ShapeDtypeStruct((T, DTW), F32), jax.ShapeDtypeStruct((T, D), BF16)],
        scratch_shapes=[pltpu.VMEM((tm, D), BF16)], args=(x, norm_g, wp, wp))


def _gmlp_tile():
    return 512


def _gmlp_fwd_tile(u_ref, v_ref, z_ref, lg_ref, lb_ref, ws_ref, bs_ref, ya_ref, vn_s):
    tm = u_ref.shape[0]
    vg = _gelu(v_ref[...].astype(F32))
    mu = jnp.mean(vg, axis=-1, keepdims=True)
    xc = vg - mu
    rstd = lax.rsqrt(jnp.mean(xc * xc, axis=-1, keepdims=True) + EPS)
    vn_s[...] = (xc * rstd * lg_ref[...] + lb_ref[...]).astype(BF16)
    for c in range(tm // CH):
        rs = slice(c * CH, (c + 1) * CH)
        for g in range(NG):
            cs_ = slice(g * 256, (g + 1) * 256)
            sv = _dot(ws_ref[g], vn_s[rs, cs_]) + bs_ref[g]
            z = z_ref[rs, cs_].astype(F32)
            ya_ref[rs, cs_] = (_gelu(u_ref[rs, cs_].astype(F32)) * sv * (z * _sigmoid(z))).astype(BF16)


def _gmlp_bwd(proj, dya, ln_g, ln_b, ws, wst, bst, gsel, ex):
    T = proj.shape[0]
    tm = min(T, _gmlp_tile())

    def body(u_ref, v_ref, z_ref, dy_ref, lg_ref, lb_ref, ws_ref, wst_ref, bs_ref, gsel_ref,
             d_ref, dws_ref, dbs_ref, dln_ref, vn_s, dsv_s, dvn_s):
        du_ref, dv_ref, dz_ref = d_ref.at[:, 0:D], d_ref.at[:, D:2 * D], d_ref.at[:, 2 * D:3 * D]
        @pl.when(pl.program_id(0) == 0)
        def _():
            dws_ref[...] = jnp.zeros_like(dws_ref)
            dbs_ref[...] = jnp.zeros_like(dbs_ref)
            dln_ref[...] = jnp.zeros_like(dln_ref)

        vg, dvg_dv = _gelu_and_grad(v_ref[...].astype(F32))
        mu = jnp.mean(vg, axis=-1, keepdims=True)
        xc = vg - mu
        rstd = lax.rsqrt(jnp.mean(xc * xc, axis=-1, keepdims=True) + EPS)
        vhat = xc * rstd
        vn_s[...] = (vhat * lg_ref[...] + lb_ref[...]).astype(BF16)
        ri = lax.broadcasted_iota(jnp.int32, (CH, CH), 0)
        ci = lax.broadcasted_iota(jnp.int32, (CH, CH), 1)
        tril = (ri >= ci).astype(F32)
        for c in range(tm // CH):
            rs = slice(c * CH, (c + 1) * CH)
            for g in range(NG):
                cs_ = slice(g * 256, (g + 1) * 256)
                vn = vn_s[rs, cs_]
                sv = _dot(ws_ref[g], vn) + bs_ref[g]
                z = z_ref[rs, cs_].astype(F32)
                sz = _sigmoid(z)
                ug, dug_du = _gelu_and_grad(u_ref[rs, cs_].astype(F32))
                dy = dy_ref[rs, cs_].astype(F32)
                zs = z * sz
                t = dy * zs
                dsv_f = dy * sv
                du_ref[rs, cs_] = (zs * dsv_f * dug_du).astype(BF16)
                dz_ref[rs, cs_] = (dsv_f * ug * (sz + zs * (1.0 - sz))).astype(BF16)
                dsv = (t * ug).astype(BF16)
                dsv_s[rs, cs_] = dsv
                dvn_s[rs, cs_] = _dot(wst_ref[g], dsv)
                dws_ref[g] += _dot_nt(dsv, vn) * tril
            dbs_ref[...] += _dot(dsv_s[rs, :], gsel_ref[...])
        dvn = dvn_s[...]
        dln_ref[0:1, :] += jnp.sum(dvn * vhat, axis=0, keepdims=True)
        dln_ref[1:2, :] += jnp.sum(dvn, axis=0, keepdims=True)
        dvh = dvn * lg_ref[...]
        dvg = rstd * (dvh - jnp.mean(dvh, axis=-1, keepdims=True) - vhat * jnp.mean(dvh * vhat, axis=-1, keepdims=True))
        dv_ref[...] = (dvg * dvg_dv).astype(BF16)

    blk = lambda off: pl.BlockSpec((tm, D), lambda i: (i, off // D))
    row = pl.BlockSpec((tm, D), lambda i: (i, 0))
    return _call(
        body, ex, name="gmlp_bwd", grid=(T // tm,),
        in_specs=[blk(OFF_U), blk(OFF_V), blk(OFF_ZA), row, _full((1, D)), _full((1, D)), _full((NG, CH, CH)),
                  _full((NG, CH, CH)), _full((NG, CH, 256)), _full((D, 128))],
        out_specs=[pl.BlockSpec((tm, 3 * D), lambda i: (i, 0)), _full((NG, CH, CH)), _full((CH, 128)), _full((8, D))],
        out_shape=[jax.ShapeDtypeStruct((T, 3 * D), BF16),
                   jax.ShapeDtypeStruct((NG, CH, CH), F32), jax.ShapeDtypeStruct((CH, 128), F32), jax.ShapeDtypeStruct((8, D), F32)],
        scratch_shapes=[pltpu.VMEM((tm, D), BF16), pltpu.VMEM((tm, D), BF16), pltpu.VMEM((tm, D), F32)],
        args=(proj, proj, proj, dya, ln_g, ln_b, ws, wst, bst, gsel))


def _zb_cols(zb0_ref, zb1_ref, cols):
    ref = zb0_ref if cols.start < D else zb1_ref
    return ref[:, cols.start % D:cols.start % D + (cols.stop - cols.start)]


def _shift_matrix(down):
    t = jnp.arange(CH)[None, :, None]
    j = jnp.arange(1, 4)[:, None, None]
    col = jnp.arange(2 * CH)[None, None, :]
    src = CH + t - j if down else t + j
    return (col == src).astype(BF16).reshape(3 * CH, 2 * CH)


def _conv_pre(x, moved, cw_ref, cb_ref):
    pre = cb_ref[...] + cw_ref[3:4, :] * x
    for j in (1, 2, 3):
        pre = pre + cw_ref[3 - j:4 - j, :] * moved[(j - 1) * CH:j * CH]
    return pre


def _split_dot(x, w, parts, w_left=False):
    acc, r = None, x
    for k in range(parts):
        hi = r.astype(BF16)
        d = _dot(w, hi) if w_left else _dot(hi, w)
        acc = d if acc is None else acc + d
        if k + 1 < parts:
            r = r - hi.astype(F32)
    return acc


def _chunk_decays(dt, alog_ref, e_ref, cs_s, cst_s, csx_s):
    a = -jnp.exp(alog_ref[...])
    ri = lax.broadcasted_iota(jnp.int32, (CH, CH), 0)
    ci = lax.broadcasted_iota(jnp.int32, (CH, CH), 1)
    tril = ri >= ci
    cs = _split_dot(dt * a, tril.astype(BF16), 3, w_left=True)
    cs_s[...] = cs
    cst_s[...] = cs.T
    csx_s[...] = _split_dot(cs, e_ref[...], 3)
    return a, tril, ri, ci


def _lmat(cst_s, h, tril):
    rowb = jnp.broadcast_to(cst_s[h:h + 1, :], (CH, CH))
    return jnp.exp(jnp.where(tril, rowb.T - rowb, -jnp.inf))


def _head_pair_rows(v, lane):
    return jnp.concatenate([jnp.where(lane < HD, v, 0.0), jnp.where(lane < HD, 0.0, v)], axis=0).astype(BF16)


def _ssd_fwd(proj, dtr, cw, cb, dtb, alog, dskx, sg, e128):
    T = proj.shape[0]
    nc = T // CH

    def body(xbc_ref, zb0_ref, zb1_ref, dt_ref, cw_ref, cb_ref, dtb_ref, alog_ref, dx_ref, sg_ref, e_ref, shift_ref,
             y_ref, yb_ref, hp_ref, pre_ref, xx_s, h_s, cs_s, cst_s, csx_s, yz_s):
        @pl.when(pl.program_id(0) == 0)
        def _():
            xx_s[...] = jnp.zeros_like(xx_s)
            h_s[...] = jnp.zeros_like(h_s)

        xx_s[CH:, :] = xbc_ref[...]
        moved = _dot(shift_ref[...], xx_s[...])
        xx_s[CH - 16:CH, :] = xbc_ref[CH - 16:, :]
        x = xbc_ref[...].astype(F32)
        pre = _conv_pre(x, moved, cw_ref, cb_ref)
        pre_ref[...] = pre
        xc = pre * _sigmoid(pre)
        dt = _softplus(dt_ref[...] + dtb_ref[...])
        a, tril, _, lane = _chunk_decays(dt, alog_ref, e_ref, cs_s, cst_s, csx_s)
        dt_x = _split_dot(dt, e_ref[...], 2)
        cs_last_x = csx_s[CH - 1:CH, :]
        hp_ref[0] = h_s[...]
        for g in range(NG):
            gs = slice(g * 512, (g + 1) * 512)
            bg = xc[:, DI + g * NST:DI + (g + 1) * NST].astype(BF16)
            cg = xc[:, DI + 512 + g * NST:DI + 512 + (g + 1) * NST].astype(BF16)
            cbm = _dot_nt(cg, bg)
            xg = xc[:, gs]
            xdt = xg * dt_x[:, gs]
            hprev = h_s[:, gs]
            csx = csx_s[:, gs]
            yoff = _dot(cg, hprev.astype(BF16)) * jnp.exp(csx)
            st = _dot_tn(bg, (xdt * jnp.exp(cs_last_x[:, gs] - csx)).astype(BF16))
            h_s[:, gs] = jnp.exp(cs_last_x[:, gs]) * hprev + st
            ssq = jnp.zeros((CH, 1), F32)
            for q in range(4):
                h0 = g * 8 + 2 * q
                ps = slice(q * 128, (q + 1) * 128)
                cols = slice(g * 512 + q * 128, g * 512 + (q + 1) * 128)
                m01 = jnp.concatenate([cbm * _lmat(cst_s, h0, tril), cbm * _lmat(cst_s, h0 + 1, tril)], axis=1).astype(BF16)
                yq = _dot(m01, _head_pair_rows(xdt[:, ps], lane)) + yoff[:, ps] + xg[:, ps] * dx_ref[:, cols]
                y_ref[:, cols] = yq
                z = _zb_cols(zb0_ref, zb1_ref, cols).astype(F32)
                yz = yq * z * _sigmoid(z)
                yz_s[:, cols] = yz
                ssq = ssq + jnp.sum(yz * yz, axis=1, keepdims=True)
            rg = lax.rsqrt(ssq * (1.0 / 512.0) + EPS)
            yb_ref[:, gs] = (yz_s[:, gs] * rg * sg_ref[:, gs]).astype(BF16)

    return pl.pallas_call(
        body, name="ssd_fwd", grid=(nc,),
        in_specs=[pl.BlockSpec((CH, CD), lambda c: (c, OFF_XBC // CD)), pl.BlockSpec((CH, D), lambda c: (c, OFF_ZB // D)),
                  pl.BlockSpec((CH, D), lambda c: (c, OFF_ZB // D + 1)),
                  pl.BlockSpec((CH, DTW), lambda c: (c, 0)), _full((4, CD)), _full((1, CD)), _full((1, DTW)),
                  _full((1, DTW)), _full((1, DI)), _full((1, DI)), _full((DTW, DI)), _full((3 * CH, 2 * CH))],
        out_specs=[pl.BlockSpec((CH, DI), lambda c: (c, 0)), pl.BlockSpec((CH, DI), lambda c: (c, 0)),
                   pl.BlockSpec((1, NST, DI), lambda c: (c, 0, 0)), pl.BlockSpec((CH, CD), lambda c: (c, 0))],
        out_shape=[jax.ShapeDtypeStruct((T, DI), F32), jax.ShapeDtypeStruct((T, DI), BF16),
                   jax.ShapeDtypeStruct((nc, NST, DI), F32), jax.ShapeDtypeStruct((T, CD), F32)],
        scratch_shapes=[pltpu.VMEM((2 * CH, CD), BF16), pltpu.VMEM((NST, DI), F32), pltpu.VMEM((CH, CH), F32),
                        pltpu.VMEM((CH, CH), F32), pltpu.VMEM((CH, DI), F32), pltpu.VMEM((CH, DI), F32)],
        compiler_params=_cp(("arbitrary",)),
    )(proj, proj, proj, dtr, cw, cb, dtb, alog, dskx, sg, e128, _shift_matrix(down=True))


def _ssd_bwd(proj, pre_all, dtr, y, dyb, hprev_all, cw, dtb, alog, dskx, sg, e128, et128, ex):
    T = proj.shape[0]
    nc = T // CH

    def body(xbc_ref, pre_ref, zb0_ref, zb1_ref, dt_ref, y_ref, dyb_ref, hp_ref, cw_ref, dtb_ref, alog_ref, dx_ref, sg_ref,
             e_ref, et_ref, shift_ref, d_ref, ddt_ref, dcw_ref, dsm_ref, dsg_ref,
             g_s, dd_s, cs_s, cst_s, csx_s, dy_s, dxdt_s, dxs_s, dsd_s, dxc_s, gh_s):
        dxbc_ref, dzb_ref = d_ref.at[:, 0:CD], d_ref.at[:, CD:CD + DI]
        i = pl.program_id(0)

        @pl.when(i == 0)
        def _():
            g_s[...] = jnp.zeros_like(g_s)
            dd_s[...] = jnp.zeros_like(dd_s)
            dcw_ref[...] = jnp.zeros_like(dcw_ref)
            dsm_ref[...] = jnp.zeros_like(dsm_ref)
            dsg_ref[...] = jnp.zeros_like(dsg_ref)

        pre = pre_ref[...]
        sp = _sigmoid(pre)
        xc = pre * sp
        dtr = dt_ref[...] + dtb_ref[...]
        dt = _softplus(dtr)
        a, tril, ri, lane = _chunk_decays(dt, alog_ref, e_ref, cs_s, cst_s, csx_s)
        et = et_ref[...]
        dt_x = _split_dot(dt, e_ref[...], 2)
        cs_last_x = csx_s[CH - 1:CH, :]

        for g in range(NG):
            gs = slice(g * 512, (g + 1) * 512)
            z = _zb_cols(zb0_ref, zb1_ref, gs).astype(F32)
            sz = _sigmoid(z)
            yv = y_ref[:, gs]
            zs = z * sz
            yz = yv * zs
            rg = lax.rsqrt(jnp.mean(yz * yz, axis=-1, keepdims=True) + EPS)
            yn = yz * rg
            dyb = dyb_ref[:, gs].astype(F32)
            dsg_ref[0:1, gs] += jnp.sum(dyb * yn, axis=0, keepdims=True)
            dyn = dyb * sg_ref[:, gs]
            dyz = rg * (dyn - yn * jnp.mean(dyn * yn, axis=-1, keepdims=True))
            dy_s[:, gs] = dyz * zs
            dzb_ref[:, gs] = (dyz * yv * (sz + zs * (1.0 - sz))).astype(BF16)

        rsum = jnp.zeros((CH, DTW), F32)
        csum_t = jnp.zeros((DTW, CH), F32)
        for g in range(NG):
            gs = slice(g * 512, (g + 1) * 512)
            bg = xc[:, DI + g * NST:DI + (g + 1) * NST].astype(BF16)
            cg = xc[:, DI + 512 + g * NST:DI + 512 + (g + 1) * NST].astype(BF16)
            cbm = _dot_nt(cg, bg)
            xdt = xc[:, gs] * dt_x[:, gs]
            hprev = hp_ref[0, :, gs]
            hpb = hprev.astype(BF16)
            gn = g_s[:, gs]
            gnb = gn.astype(BF16)
            dy = dy_s[:, gs]
            csx = csx_s[:, gs]
            ecs = jnp.exp(csx)
            dec = jnp.exp(cs_last_x[:, gs] - csx)
            dye = (dy * ecs).astype(BF16)
            dc = _dot_nt(dye, hpb)
            dprev = _dot_tn(cg, dye)
            dxdt_state = dec * _dot(bg, gnb)
            db = _dot_nt((xdt * dec).astype(BF16), gnb)
            dcb = jnp.zeros((CH, CH), F32)
            for q in range(4):
                h0 = g * 8 + 2 * q
                ps = slice(q * 128, (q + 1) * 128)
                dyp = dy[:, ps]
                l0 = _lmat(cst_s, h0, tril)
                l1 = _lmat(cst_s, h0 + 1, tril)
                m0 = cbm * l0
                m1 = cbm * l1
                dm = _dot_nt(dyp.astype(BF16), _head_pair_rows(xdt[:, ps], lane))
                dm0 = dm[:, :CH]
                dm1 = dm[:, CH:]
                dcb = dcb + dm0 * l0 + dm1 * l1
                for hh, qm in ((h0, dm0 * m0), (h0 + 1, dm1 * m1)):
                    rsum = jnp.where(lane == hh, jnp.sum(qm, axis=1, keepdims=True), rsum)
                    csum_t = jnp.where(ri == hh, jnp.sum(qm, axis=0, keepdims=True), csum_t)
                mst = jnp.concatenate([m0, m1], axis=0).astype(BF16)
                d = _dot_tn(mst, _head_pair_rows(dyp, lane))
                dxdt_s[:, g * 512 + q * 128:g * 512 + (q + 1) * 128] = d + dxdt_state[:, ps]
            yoff = _dot(cg, hpb) * ecs
            dsd_s[:, gs] = xdt * dxdt_state
            dxs_s[:, gs] = dy * yoff
            dcbb = dcb.astype(BF16)
            dxc_s[:, DI + 512 + g * NST:DI + 512 + (g + 1) * NST] = dc + _dot(dcbb, bg)
            dxc_s[:, DI + g * NST:DI + (g + 1) * NST] = db + _dot_tn(dcbb, cg)
            gh_s[:, gs] = jnp.broadcast_to(jnp.sum(gn * hprev, axis=0, keepdims=True), (8, 512))
            g_s[:, gs] = dprev + jnp.exp(cs_last_x[:, gs]) * gn

        xs = xc[:, :DI]
        dy = dy_s[...]
        dxdt = dxdt_s[...]
        cs_last = cs_s[CH - 1:CH, :]
        state_e = _split_dot(dsd_s[...], et, 2)
        dcd = 0.125 * jnp.sum(_split_dot(gh_s[...], et, 2), axis=0, keepdims=True) * jnp.exp(cs_last)
        row = lax.broadcasted_iota(jnp.int32, (CH, 1), 0)
        dcs = rsum - csum_t.T + _split_dot(dxs_s[...], et, 2) - state_e
        dcs = dcs + jnp.where(row == CH - 1, jnp.sum(state_e, axis=0, keepdims=True) + dcd, 0.0)
        dda = _split_dot(dcs, (lane >= ri).astype(BF16), 3, w_left=True)
        ddt = dda * a + _dot((dxdt * xs).astype(BF16), et)
        ddtr = jnp.where(lane < NH, ddt * _sigmoid(dtr), 0.0)
        ddt_ref[...] = ddtr.astype(BF16)
        dsm_ref[0:1, :] += jnp.sum(ddtr, axis=0, keepdims=True)
        dsm_ref[1:2, :] += jnp.sum(dda * dt, axis=0, keepdims=True) * a
        dsm_ref[2:3, :] += jnp.sum(_dot((dy * xs).astype(BF16), et), axis=0, keepdims=True)
        dxc_s[:, :DI] = dxdt * dt_x + dy * dx_ref[...]

        dpre = dxc_s[...] * (sp + xc * (1.0 - sp))
        dpre_b = dpre.astype(BF16)
        dd_s[:CH, :] = dpre_b
        moved = _dot(shift_ref[...], dd_s[...])
        dd_s[CH:CH + 16, :] = dpre_b[:16]
        x = xbc_ref[...].astype(F32)
        dcw_ref[4:5, :] += jnp.sum(dpre, axis=0, keepdims=True)
        dxbc = cw_ref[3:4, :] * dpre
        dcw_ref[3:4, :] += jnp.sum(dpre * x, axis=0, keepdims=True)
        for j in (1, 2, 3):
            ahead = moved[(j - 1) * CH:j * CH]
            dcw_ref[3 - j:4 - j, :] += jnp.sum(ahead * x, axis=0, keepdims=True)
            dxbc = dxbc + cw_ref[3 - j:4 - j, :] * ahead
        dxbc_ref[...] = dxbc.astype(BF16)

    rev = lambda c: nc - 1 - c
    return _call(
        body, ex, name="ssd_bwd", grid=(nc,),
        in_specs=[pl.BlockSpec((CH, CD), lambda c: (rev(c), OFF_XBC // CD)),
                  pl.BlockSpec((CH, CD), lambda c: (rev(c), 0)),
                  pl.BlockSpec((CH, D), lambda c: (rev(c), OFF_ZB // D)), pl.BlockSpec((CH, D), lambda c: (rev(c), OFF_ZB // D + 1)),
                  pl.BlockSpec((CH, DTW), lambda c: (rev(c), 0)),
                  pl.BlockSpec((CH, DI), lambda c: (rev(c), 0)), pl.BlockSpec((CH, DI), lambda c: (rev(c), 0)),
                  pl.BlockSpec((1, NST, DI), lambda c: (rev(c), 0, 0)),
                  _full((4, CD)), _full((1, DTW)), _full((1, DTW)), _full((1, DI)), _full((1, DI)),
                  _full((DTW, DI)), _full((DI, DTW)), _full((3 * CH, 2 * CH))],
        out_specs=[pl.BlockSpec((CH, CD + DI), lambda c: (rev(c), 0)),
                   pl.BlockSpec((CH, DTW), lambda c: (rev(c), 0)), _full((8, CD)), _full((8, DTW)), _full((8, DI))],
        out_shape=[jax.ShapeDtypeStruct((T, CD + DI), BF16), jax.ShapeDtypeStruct((T, DTW), BF16),
                   jax.ShapeDtypeStruct((8, CD), F32), jax.ShapeDtypeStruct((8, DTW), F32), jax.ShapeDtypeStruct((8, DI), F32)],
        scratch_shapes=[pltpu.VMEM((NST, DI), F32), pltpu.VMEM((2 * CH, CD), BF16), pltpu.VMEM((CH, CH), F32), pltpu.VMEM((CH, CH), F32),
                        pltpu.VMEM((CH, DI), F32), pltpu.VMEM((CH, DI), F32), pltpu.VMEM((CH, DI), F32), pltpu.VMEM((CH, DI), F32),
                        pltpu.VMEM((CH, DI), F32), pltpu.VMEM((CH, CD), F32), pltpu.VMEM((8, DI), F32)],
        args=(proj, pre_all, proj, proj, dtr, y, dyb, hprev_all, cw, dtb, alog, dskx, sg, e128, et128, _shift_matrix(down=False)))


def _merge_tile():
    return 256


def _merge(x, yb, proj, p, tgt, gmlp, w_oa, w_ob, w_out, w_pg, w_ple, ple_g, fin_g):
    T = x.shape[0]
    tm = min(T, _merge_tile())

    def body(x_ref, u_ref, v_ref, za_ref, yb_ref, ga_ref, gb_ref, p_ref, t_ref, lg_ref, lb_ref, ws_ref, bs_ref,
             woa, wob, wout, wpg, wple, pg_ref, fg_ref,
             dx1_ref, dx1b_ref, ya_ref, mg_ref, hp_ref, dpre_ref, dpe_ref, doa_ref, dob_ref, dya_ref, dyb_ref, dg_ref, acc_ref, vn_s):
        @pl.when(pl.program_id(0) == 0)
        def _():
            acc_ref[...] = jnp.zeros_like(acc_ref)

        _gmlp_fwd_tile(u_ref, v_ref, za_ref, lg_ref, lb_ref, ws_ref, bs_ref, ya_ref, vn_s)
        oa = _dot(ya_ref[...], woa[...])
        ob = _dot(yb_ref[...], wob[...])
        sa = _sigmoid(ga_ref[...].astype(F32))
        sb = _sigmoid(gb_ref[...].astype(F32))
        mg = sa * oa + sb * ob
        mgb = mg.astype(BF16)
        mg_ref[...] = mgb
        x1 = x_ref[...] + _dot(mgb, wout[...])
        r2 = lax.rsqrt(jnp.mean(x1 * x1, axis=-1, keepdims=True) + EPS)
        xh1 = x1 * r2
        hpb = (xh1 * pg_ref[...]).astype(BF16)
        hp_ref[...] = hpb
        gate = _sigmoid(_dot(hpb, wpg[...]))
        pe = _dot(p_ref[...].astype(BF16), wple[...])
        x2 = x1 + gate * pe
        r3 = lax.rsqrt(jnp.mean(x2 * x2, axis=-1, keepdims=True) + EPS)
        xh2 = x2 * r3
        err = xh2 * fg_ref[...] - t_ref[...]
        acc_ref[2:3, :] += 0.5 * jnp.sum(jnp.mean(err * err, axis=-1, keepdims=True))
        dyo = err * (1.0 / D)
        acc_ref[0:1, :] += jnp.sum(dyo * xh2, axis=0, keepdims=True)
        dn = dyo * fg_ref[...]
        dx2 = r3 * (dn - xh2 * jnp.mean(dn * xh2, axis=-1, keepdims=True))
        dpe_ref[...] = (dx2 * gate).astype(BF16)
        dpre = (dx2 * pe * gate * (1.0 - gate)).astype(BF16)
        dpre_ref[...] = dpre
        dhp = _dot_nt(dpre, wpg[...])
        acc_ref[1:2, :] += jnp.sum(dhp * xh1, axis=0, keepdims=True)
        dhn = dhp * pg_ref[...]
        dx1 = dx2 + r2 * (dhn - xh1 * jnp.mean(dhn * xh1, axis=-1, keepdims=True))
        dx1_ref[...] = dx1
        dx1b = dx1.astype(BF16)
        dx1b_ref[...] = dx1b
        dmg = _dot_nt(dx1b, wout[...])
        doa = (dmg * sa).astype(BF16)
        dob = (dmg * sb).astype(BF16)
        doa_ref[...] = doa
        dob_ref[...] = dob
        dg_ref[:, :D] = (dmg * oa * sa * (1.0 - sa)).astype(BF16)
        dg_ref[:, D:] = (dmg * ob * sb * (1.0 - sb)).astype(BF16)
        dya_ref[...] = _dot_nt(doa, woa[...]).astype(BF16)
        dyb_ref[...] = _dot_nt(dob, wob[...]).astype(BF16)

    row = lambda w: pl.BlockSpec((tm, w), lambda i: (i, 0))
    blk = lambda off: pl.BlockSpec((tm, D), lambda i: (i, off // D))
    wsp = lambda s: pl.BlockSpec(s, lambda i: (0, 0), pipeline_mode=pl.Buffered(1))
    return pl.pallas_call(
        body, name="merge", grid=(T // tm,),
        in_specs=[row(D), blk(OFF_U), blk(OFF_V), blk(OFF_ZA), row(DI), blk(OFF_GA), blk(OFF_GB), row(PLE), row(D),
                  _full((1, D)), _full((1, D)), _full((NG, CH, CH)), _full((NG, CH, 256)),
                  wsp((D, D)), wsp((DI, D)), wsp((D, D)), wsp((D, D)), wsp((PLE, D)), _full((1, D)), _full((1, D))],
        out_specs=[row(D)] * 10 + [row(DI), row(2 * D), _full((8, D))],
        out_shape=[jax.ShapeDtypeStruct((T, D), F32)] + [jax.ShapeDtypeStruct((T, D), BF16)] * 9
        + [jax.ShapeDtypeStruct((T, DI), BF16), jax.ShapeDtypeStruct((T, 2 * D), BF16), jax.ShapeDtypeStruct((8, D), F32)],
        scratch_shapes=[pltpu.VMEM((tm, D), BF16)],
        compiler_params=_cp(("arbitrary",)),
    )(x, proj, proj, proj, yb, proj, proj, p, tgt, *gmlp, w_oa, w_ob, w_out, w_pg, w_ple, ple_g, fin_g)


def _wgrad(a, b, name):
    T, K = a.shape
    N = b.shape[1]
    tt, tk, tn = min(T, 2048), min(K, 1024), min(N, 1024)
    nt = T // tt

    def body(a_ref, b_ref, o_ref, acc_s):
        t = pl.program_id(2)

        @pl.when(t == 0)
        def _():
            acc_s[...] = jnp.zeros_like(acc_s)

        acc_s[...] += _dot_tn(a_ref[...].astype(BF16), b_ref[...])

        @pl.when(t == nt - 1)
        def _():
            o_ref[...] = acc_s[...].astype(BF16)

    return pl.pallas_call(
        body, name=name, grid=(K // tk, N // tn, nt),
        in_specs=[pl.BlockSpec((tt, tk), lambda k, n, t: (t, k)), pl.BlockSpec((tt, tn), lambda k, n, t: (t, n))],
        out_specs=pl.BlockSpec((tk, tn), lambda k, n, t: (k, n)),
        out_shape=jax.ShapeDtypeStruct((K, N), BF16),
        scratch_shapes=[pltpu.VMEM((tk, tn), F32)],
        compiler_params=_cp(("parallel", "parallel", "arbitrary")),
    )(a, b)


def _proj_bwd(x, dx1, norm_g, wt, pieces, ddt, ex):
    T = x.shape[0]
    tm = min(T, 1024)
    nk = OFF_DT // D + 1
    starts = [sum(a.shape[1] for a in pieces[:n]) // D for n in range(len(pieces))]
    ranges = [(s, s + a.shape[1] // D) for s, a in zip(starts, pieces)]
    assert ranges[-1][1] == nk - 1
    npc = len(pieces)

    def body(x_hbm, dx1_hbm, g_ref, w_ref, wdt_ref, *rest):
        piece_refs, ddt_ref, gx_ref, dng_ref = rest[:npc], rest[npc], rest[npc + 1], rest[npc + 2]
        acc_s, x_ref, dx1_ref, row_sems = rest[npc + 3:]
        i, k = pl.program_id(0), pl.program_id(1)
        rows = pl.ds(pl.multiple_of(i * tm, tm), tm)
        fetches = [pltpu.make_async_copy(x_hbm.at[rows], x_ref, row_sems.at[0]),
                   pltpu.make_async_copy(dx1_hbm.at[rows], dx1_ref, row_sems.at[1])]

        @pl.when((i == 0) & (k == 0))
        def _():
            dng_ref[...] = jnp.zeros_like(dng_ref)

        @pl.when(k == 0)
        def _():
            acc_s[...] = jnp.zeros_like(acc_s)

        @pl.when(k == 1)
        def _():
            for cp in fetches:
                cp.start()

        for ref, (lo, hi) in zip(piece_refs, ranges):
            @pl.when((k >= lo) & (k < hi))
            def _(ref=ref):
                acc_s[...] += _dot(ref[...], w_ref[...])

        @pl.when(k == nk - 1)
        def _():
            for cp in fetches:
                cp.wait()
            dh = acc_s[...] + _dot(ddt_ref[...], wdt_ref[...])
            xf = x_ref[...]
            r = lax.rsqrt(jnp.mean(xf * xf, axis=-1, keepdims=True) + EPS)
            xh = xf * r
            dng_ref[0:1, :] += jnp.sum(dh * xh, axis=0, keepdims=True)
            dxn = dh * g_ref[...]
            gx_ref[...] = dx1_ref[...] + r * (dxn - xh * jnp.mean(dxn * xh, axis=-1, keepdims=True))

    def piece_spec(lo, hi):
        return pl.BlockSpec((tm, D), lambda i, k: (i, jnp.clip(k - lo, 0, hi - lo - 1)))

    row = pl.BlockSpec((tm, D), lambda i, k: (i, 0))
    return _call(
        body, ex, name="proj_bwd", grid=(T // tm, nk),
        in_specs=[_ANY, _ANY, _full((1, D)), pl.BlockSpec((D, D), lambda i, k: (jnp.minimum(k, nk - 2), 0)),
                  pl.BlockSpec((DTW, D), lambda i, k: (OFF_DT // DTW, 0))]
        + [piece_spec(lo, hi) for lo, hi in ranges] + [pl.BlockSpec((tm, DTW), lambda i, k: (i, 0))],
        out_specs=[row, _full((8, D))],
        out_shape=[jax.ShapeDtypeStruct((T, D), F32), jax.ShapeDtypeStruct((8, D), F32)],
        scratch_shapes=[pltpu.VMEM((tm, D), F32), pltpu.VMEM((tm, D), F32), pltpu.VMEM((tm, D), F32), pltpu.SemaphoreType.DMA((2,))],
        args=(x, dx1, norm_g, wt, wt, *pieces, ddt))


def _elementwise_tile(R, C, limit=1 << 20):
    if R * C * 4 <= limit:
        return R, C
    rows = [t for t in range(16, R, 16) if R % t == 0 and t * C * 4 <= limit]
    if rows:
        return rows[-1], C
    cols = [t for t in range(128, C, 128) if C % t == 0 and R * t * 4 <= limit]
    return R, cols[-1]


def _adam_update(w, m, v, g):
    c1 = 1.0 - ADAM_B1 ** ADAM_STEP
    c2 = 1.0 - ADAM_B2 ** ADAM_STEP
    mm = ADAM_B1 * m + (1.0 - ADAM_B1) * g
    vv = ADAM_B2 * v + (1.0 - ADAM_B2) * (g * g)
    return -ADAM_LR * ((mm / c1) / (jnp.sqrt(vv / c2) + ADAM_EPS) + ADAM_WD * w), mm, vv


def _sum_parts(parts, name, ex):
    first = parts[0][0] if isinstance(parts[0], tuple) else parts[0]
    R, C = first.shape[-2:]
    tr, tc = _elementwise_tile(R, C)
    flat = pl.BlockSpec((tr, tc), lambda i, j: (i, j))
    slot = lambda k: pl.BlockSpec((None, tr, tc), lambda i, j: (k, i, j))
    specs = [slot(p[1]) if isinstance(p, tuple) else flat for p in parts]
    arrays = [p[0] if isinstance(p, tuple) else p for p in parts]

    def body(*refs):
        g = refs[0][...].astype(F32)
        for r in refs[1:-1]:
            g = g + r[...].astype(F32)
        refs[-1][...] = g

    (out,), got = _call(body, ex, name=name, grid=(R // tr, C // tc), in_specs=specs, out_specs=[flat],
                        out_shape=[jax.ShapeDtypeStruct((R, C), F32)], scratch_shapes=[], args=arrays)
    return out, got


def _adamw_tiles(w, m, v, g, name):
    R = w.shape[0]
    tr = 107
    assert R % tr == 0
    spec = pl.BlockSpec((tr, 8, 128), lambda i: (i, 0, 0))

    def body(w_ref, m_ref, v_ref, g_in, g_ref, d_ref, nm_ref, nv_ref):
        g = g_in[...]
        g_ref[...] = g
        d_ref[...], nm_ref[...], nv_ref[...] = _adam_update(w_ref[...], m_ref[...], v_ref[...], g)

    return pl.pallas_call(body, name=name, grid=(R // tr,), in_specs=[spec] * 4, out_specs=[spec] * 4,
                          out_shape=[jax.ShapeDtypeStruct(w.shape, F32)] * 4, compiler_params=_cp(("parallel",)))(w, m, v, g)


def _adamw(w, m, v, parts, name, part_row=0, block_rows=None):
    R, C = w.shape
    tr, tc = _elementwise_tile(R, C) if block_rows is None else (block_rows, C)
    assert R % tr == 0 and part_row % tr == 0
    first = part_row // tr
    n = len(parts)
    wspec = pl.BlockSpec((tr, tc), lambda i, j: (i, j))
    flat = pl.BlockSpec((tr, tc), lambda i, j: (first + i, j))
    slot = lambda k: pl.BlockSpec((None, tr, tc), lambda i, j: (k, first + i, j))
    part_specs = [slot(p[1]) if isinstance(p, tuple) else flat for p in parts]
    part_arrays = [p[0] if isinstance(p, tuple) else p for p in parts]

    def body(*refs):
        w_ref, m_ref, v_ref = refs[:3]
        g_ref, d_ref, nm_ref, nv_ref = refs[3 + n:]
        g = refs[3][...].astype(F32)
        for r in refs[4:3 + n]:
            g = g + r[...].astype(F32)
        g_ref[...] = g
        d_ref[...], nm_ref[...], nv_ref[...] = _adam_update(w_ref[...], m_ref[...], v_ref[...], g)

    return pl.pallas_call(
        body, name=name, grid=(R // tr, C // tc), in_specs=[wspec] * 3 + part_specs, out_specs=[wspec] * 4,
        out_shape=[jax.ShapeDtypeStruct(w.shape, F32)] * 4, compiler_params=_cp(("parallel", "parallel")),
    )(w, m, v, *part_arrays)


_SMALL_WIDE = {"ln_a_g": 0, "ln_a_b": 1, "final_g": 8, "ple_norm_g": 9, "ssm_norm_g": 16, "conv_b": 44}
_WIDE_CONV_W, _WIDE_ROWS = 32, 56
_WIDE_LOSS = 10
_SMALL_NARROW = {"w_s": (0, NG * CH, 128), "b_s": (512, NG, 128), "dt_bias": (520, 1, NH), "a_log": (521, 1, NH),
                 "d_skip": (522, 1, NH)}
_NARROW_ROWS = 528
_SMALL_SHAPES = {"norm_g": (1, D), "ln_a_g": (1, D), "ln_a_b": (1, D), "ple_norm_g": (1, D), "final_g": (1, D),
                 "ssm_norm_g": (1, DI), "conv_b": (1, CD), "w_s": (NG * CH, CH), "b_s": (NG, CH), "dt_bias": (1, NH),
                 "a_log": (1, NH), "d_skip": (1, NH)}


def _adamw_small(w, m, v, wide_all, narrow_all, norm_all):
    names = list(_SMALL_SHAPES)
    n = len(names)

    def body(*refs):
        wr, mr, vr = refs[:n], refs[n:2 * n], refs[2 * n:3 * n]
        wide_ref, narrow_ref, norm_ref = refs[3 * n:3 * n + 3]
        outs = refs[3 * n + 3:]
        gr, dr, nmr, nvr, cw_ref, loss_ref = outs[:n], outs[n:2 * n], outs[2 * n:3 * n], outs[3 * n:4 * n], outs[4 * n], outs[4 * n + 1]

        def total(ref, rows, lanes):
            acc = ref[0, rows, lanes]
            for d in range(1, NDEV):
                acc = acc + ref[d, rows, lanes]
            return acc

        for k, name in enumerate(names):
            if name in _SMALL_WIDE or name == "norm_g":
                for part in range(_SMALL_SHAPES[name][1] // D):
                    pack, r = (norm_ref, 0) if name == "norm_g" else (wide_ref, _SMALL_WIDE[name] + part)
                    cols = slice(part * D, (part + 1) * D)
                    g = total(pack, slice(r, r + 1), slice(None))
                    gr[k][:, cols] = g
                    dr[k][:, cols], nmr[k][:, cols], nvr[k][:, cols] = _adam_update(wr[k][:, cols], mr[k][:, cols], vr[k][:, cols], g)
            else:
                r, rows, lanes = _SMALL_NARROW[name]
                g = total(narrow_ref, slice(r, r + rows), slice(0, lanes))
                gr[k][...] = g
                dr[k][...], nmr[k][...], nvr[k][...] = _adam_update(wr[k][...], mr[k][...], vr[k][...], g)
        cw_ref[...] = total(wide_ref, slice(_WIDE_CONV_W, _WIDE_CONV_W + 12), slice(None))
        loss_ref[...] = total(wide_ref, slice(_WIDE_LOSS, _WIDE_LOSS + 1), slice(None))

    shapes = [jax.ShapeDtypeStruct(_SMALL_SHAPES[k], F32) for k in names]
    specs = [_full(_SMALL_SHAPES[k]) for k in names]
    res = pl.pallas_call(
        body, name="adamw_small", grid=(1,),
        in_specs=specs * 3 + [_full(wide_all.shape), _full(narrow_all.shape), _full(norm_all.shape)],
        out_specs=specs * 4 + [_full((12, D)), _full((1, D))],
        out_shape=shapes * 4 + [jax.ShapeDtypeStruct((12, D), F32), jax.ShapeDtypeStruct((1, D), F32)],
        compiler_params=_cp(("arbitrary",)),
    )(*[w[k] for k in names], *[m[k] for k in names], *[v[k] for k in names], wide_all, narrow_all, norm_all)
    groups = [dict(zip(names, res[q * n:(q + 1) * n])) for q in range(4)]
    return groups[0], groups[1], groups[2], groups[3], res[4 * n], res[4 * n + 1][0, 0]


def _dev_index(px, py, pc):
    return 4 * px + 2 * py + pc


def _mesh_position():
    return lax.axis_index("x"), lax.axis_index("y"), lax.axis_index("c")


def _gather_exchange(blocks):
    n = len(blocks)

    def make(ins, outs, sems):
        send_sems, recv_sems, local_sems = sems
        x, y, c = _mesh_position()
        me, sibling = (x, y, c), (x, y, 1 - c)
        chips = [(1 - x, y), (x, 1 - y), (1 - x, 1 - y)]

        def copy(a, k, block, to, src=None):
            dst = outs[a].at[_dev_index(*block)]
            return pltpu.make_async_remote_copy(src_ref=dst if src is None else src, dst_ref=dst, send_sem=send_sems.at[a, k],
                                                recv_sem=recv_sems.at[a, k], device_id=to, device_id_type=MESH)

        mine = [pltpu.make_async_copy(ins[a], outs[a].at[_dev_index(*me)], local_sems.at[a]) for a in range(n)]
        first = []
        for a in range(n):
            first.append(copy(a, 0, me, sibling, src=ins[a]))
            first += [copy(a, 1 + j, me, (*chip, c), src=ins[a]) for j, chip in enumerate(chips)]

        def start():
            for cp in mine + first:
                cp.start()

        def finish():
            passed = []
            for j, chip in enumerate(chips):
                for a in range(n):
                    copy(a, 1 + j, (*chip, c), me).wait_recv()
                    fwd = copy(a, 4 + j, (*chip, c), sibling)
                    fwd.start()
                    passed.append(fwd)
            for a in range(n):
                copy(a, 0, sibling, me).wait_recv()
                for j, chip in enumerate(chips):
                    copy(a, 4 + j, (*chip, 1 - c), me).wait_recv()
            for cp in first + passed:
                cp.wait_send()
            for cp in mine:
                cp.wait()

        return start, finish

    return _Exchange(list(blocks), [jax.ShapeDtypeStruct((NDEV,) + b.shape, b.dtype) for b in blocks],
                     [pltpu.SemaphoreType.DMA((n, 7)), pltpu.SemaphoreType.DMA((n, 7)), pltpu.SemaphoreType.DMA((n,))], make)


def _relay_gather_exchange(blocks):
    n = len(blocks)

    def make(ins, outs, sems):
        send_sems, recv_sems, local_sems = sems
        x, y, c = _mesh_position()
        me, sibling = (x, y, c), (x, y, 1 - c)
        x_nbr, y_nbr, diag = (1 - x, y), (x, 1 - y), (1 - x, 1 - y)
        relay_from = (jnp.where(c == 0, x, 1 - x), jnp.where(c == 0, 1 - y, y))
        relay_to = (jnp.where(c == 0, 1 - x, x), jnp.where(c == 0, y, 1 - y))

        def copy(a, k, block, to, src=None):
            dst = outs[a].at[_dev_index(*block)]
            return pltpu.make_async_remote_copy(src_ref=dst if src is None else src, dst_ref=dst, send_sem=send_sems.at[a, k],
                                                recv_sem=recv_sems.at[a, k], device_id=to, device_id_type=MESH)

        mine = [pltpu.make_async_copy(ins[a], outs[a].at[_dev_index(*me)], local_sems.at[a]) for a in range(n)]
        first = []
        for a in range(n):
            first += [copy(a, 0, me, sibling, src=ins[a]), copy(a, 1, me, (*x_nbr, c), src=ins[a]), copy(a, 2, me, (*y_nbr, c), src=ins[a])]

        def start():
            for cp in mine + first:
                cp.start()

        def finish():
            later = []
            for a in range(n):
                copy(a, 1, (*x_nbr, c), me).wait_recv()
                copy(a, 2, (*y_nbr, c), me).wait_recv()
                later.append(copy(a, 3, (*relay_from, c), (*relay_to, c)))
                later += [copy(a, 4, (*x_nbr, c), sibling), copy(a, 5, (*y_nbr, c), sibling)]
                for cp in later[-3:]:
                    cp.start()
            for a in range(n):
                copy(a, 3, (*diag, c), me).wait_recv()
                later.append(copy(a, 6, (*diag, c), sibling))
                later[-1].start()
            for a in range(n):
                copy(a, 0, sibling, me).wait_recv()
                for k, chip in ((4, x_nbr), (5, y_nbr), (6, diag)):
                    copy(a, k, (*chip, 1 - c), me).wait_recv()
            for cp in first + later:
                cp.wait_send()
            for cp in mine:
                cp.wait()

        return start, finish

    return _Exchange(list(blocks), [jax.ShapeDtypeStruct((NDEV,) + b.shape, b.dtype) for b in blocks],
                     [pltpu.SemaphoreType.DMA((n, 7)), pltpu.SemaphoreType.DMA((n, 7)), pltpu.SemaphoreType.DMA((n,))], make)


def _combine(*exchanges):
    def make(ins, outs, sems):
        pairs = []
        for e in exchanges:
            ni, no, ns = len(e.arrays), len(e.out_shape), len(e.sems)
            pairs.append(e.make(ins[:ni], outs[:no], sems[:ns]))
            ins, outs, sems = ins[ni:], outs[no:], sems[ns:]

        def start():
            for s, _ in pairs:
                s()

        def finish():
            for _, f in pairs:
                f()

        return start, finish

    return _Exchange(sum((list(e.arrays) for e in exchanges), []), sum((list(e.out_shape) for e in exchanges), []),
                     sum((list(e.sems) for e in exchanges), []), make)


def _start_wait_all(copies, local=()):
    def start():
        for cp in list(local) + list(copies):
            cp.start()

    def finish():
        for cp in copies:
            cp.wait()
        for cp in local:
            cp.wait()

    return start, finish


def _no_exchange():
    return _Exchange([], [], [], lambda ins, outs, sems: (lambda: None, lambda: None))


def _direct_exchange(grads):
    n = len(grads)

    def make(ins, outs, sems):
        send_sems, recv_sems = sems
        x, y, c = _mesh_position()
        copies = []
        for a in range(n):
            for r in range(1, NDEV):
                peer = (x ^ ((r >> 2) & 1), y ^ ((r >> 1) & 1), c ^ (r & 1))
                copies.append(pltpu.make_async_remote_copy(
                    src_ref=ins[a].at[_dev_index(*peer)], dst_ref=outs[a].at[r - 1], send_sem=send_sems.at[a, r - 1],
                    recv_sem=recv_sems.at[a, r - 1], device_id=peer, device_id_type=MESH))
        return _start_wait_all(copies)

    return _Exchange(list(grads), [jax.ShapeDtypeStruct((NDEV - 1,) + g.shape[1:], g.dtype) for g in grads],
                     [pltpu.SemaphoreType.DMA((n, NDEV - 1)), pltpu.SemaphoreType.DMA((n, NDEV - 1))], make)


def _direct_gather_exchange(smalls):
    n = len(smalls)

    def make(ins, outs, sems):
        send_sems, recv_sems, local_sems = sems
        x, y, c = _mesh_position()
        copies, local = [], []
        for a in range(n):
            slot = outs[a].at[_dev_index(x, y, c)]
            local.append(pltpu.make_async_copy(ins[a], slot, local_sems.at[a]))
            for r in range(1, NDEV):
                peer = (x ^ ((r >> 2) & 1), y ^ ((r >> 1) & 1), c ^ (r & 1))
                copies.append(pltpu.make_async_remote_copy(src_ref=ins[a], dst_ref=slot, send_sem=send_sems.at[a, r - 1],
                                                           recv_sem=recv_sems.at[a, r - 1], device_id=peer, device_id_type=MESH))
        return _start_wait_all(copies, local=local)

    return _Exchange(list(smalls), [jax.ShapeDtypeStruct((NDEV,) + s.shape, s.dtype) for s in smalls],
                     [pltpu.SemaphoreType.DMA((n, 7)), pltpu.SemaphoreType.DMA((n, 7)), pltpu.SemaphoreType.DMA((n,))], make)


_W_IN_ROWS = {"u": (0, 1024), "v": (1024, 2048), "za": (2048, 3072), "zb": (3072, 5120), "xbc": (5120, 8192),
              "dt": (8192, 8224), "ga": (8224, 9248), "gb": (9248, 10272)}
_PROJ_ORDER = ("xbc", "zb", "u", "v", "za", "ga", "gb", "dt")


def _w_in_t_rows(wt):
    z = jnp.zeros((NP - NIN, wt.shape[1]), wt.dtype)
    return jnp.concatenate([wt[slice(*_W_IN_ROWS[n])] for n in _PROJ_ORDER] + [z], axis=0)


_WEIGHTS = ["norm_g", "w_in", "ln_a_g", "ln_a_b", "w_s", "b_s", "conv_w", "conv_b", "dt_bias", "a_log", "d_skip", "ssm_norm_g",
            "w_oa", "w_ob", "w_out", "ple_norm_g", "w_pg", "w_ple", "final_g"]


def _rows_pack(d, dtype):
    return jnp.concatenate([d["w_oa"].reshape(128, D), d["w_ob"].reshape(256, D), d["w_out"].reshape(128, D),
                            d["w_pg"].reshape(128, D), d["w_ple"].reshape(32, D)], axis=0).astype(dtype)


def kernel(x, p, norm_g, w_in, ln_a_g, ln_a_b, w_s, b_s, conv_w, conv_b, dt_bias, a_log, d_skip, ssm_norm_g, w_oa, w_ob, w_out, ple_norm_g, w_pg, w_ple, final_g, loss_target, m_norm_g, m_w_in, m_ln_a_g, m_ln_a_b, m_w_s, m_b_s, m_conv_w, m_conv_b, m_dt_bias, m_a_log, m_d_skip, m_ssm_norm_g, m_w_oa, m_w_ob, m_w_out, m_ple_norm_g, m_w_pg, m_w_ple, m_final_g, v_norm_g, v_w_in, v_ln_a_g, v_ln_a_b, v_w_s, v_b_s, v_conv_w, v_conv_b, v_dt_bias, v_a_log, v_d_skip, v_ssm_norm_g, v_w_oa, v_w_ob, v_w_out, v_ple_norm_g, v_w_pg, v_w_ple, v_final_g):
    args = dict(locals())
    w = {n: args[n] for n in _WEIGHTS}
    m = {n: args["m_" + n] for n in _WEIGHTS}
    v = {n: args["v_" + n] for n in _WEIGHTS}
    T = x.shape[1]
    xi, yi, ci = lax.axis_index("x"), lax.axis_index("y"), lax.axis_index("c")
    me = 4 * xi + 2 * yi + ci
    x2, p2, tgt = x.reshape(T, D), p.reshape(T, PLE), loss_target.reshape(T, D)

    norm_g2 = w["norm_g"].reshape(1, D)
    ws = jnp.where(jnp.tril(jnp.ones((CH, CH), bool))[None], w["w_s"].reshape(NG, CH, CH), 0.0).astype(BF16)
    wst = jnp.transpose(ws, (0, 2, 1))
    bst = jnp.broadcast_to(w["b_s"].reshape(NG, CH, 1), (NG, CH, 256))
    ln_g, ln_b = w["ln_a_g"].reshape(1, D), w["ln_a_b"].reshape(1, D)
    cb = w["conv_b"].reshape(1, CD)
    pad32 = lambda a: jnp.pad(a.reshape(1, NH), ((0, 0), (0, DTW - NH)))
    dtb, alog = pad32(w["dt_bias"]), pad32(w["a_log"])
    dskx = jnp.repeat(w["d_skip"].reshape(NH), HD).reshape(1, DI)
    sg = w["ssm_norm_g"].reshape(1, DI)
    ple_g, fin_g = w["ple_norm_g"].reshape(1, D), w["final_g"].reshape(1, D)
    e128 = (jnp.arange(DTW)[:, None] == (jnp.arange(DI)[None, :] // HD)).astype(BF16)
    et128 = e128.T
    gsel = ((jnp.arange(D)[:, None] // 256) == jnp.arange(128)[None, :]).astype(BF16)

    w_in_t = lambda a: jnp.transpose(a.reshape(D, WSH))
    (a_all,) = _run_exchange(_relay_gather_exchange([w_in_t(w["w_in"]).astype(BF16)]), "all_gather_w_in")
    w_in_full_t = a_all.reshape(NIN, D)
    wp = _w_in_t_rows(w_in_full_t)
    (proj, dtr, h), (r_all, cw_all) = _proj_fwd(x2, norm_g2, wp, _gather_exchange([_rows_pack(w, BF16), w["conv_w"].reshape(4, CD // NDEV)]))
    f_oa = r_all[:, R_OA:R_OB].reshape(D, D)
    f_ob = r_all[:, R_OB:R_OUT].reshape(DI, D)
    f_out = r_all[:, R_OUT:R_PG].reshape(D, D)
    f_pg = r_all[:, R_PG:R_PLE].reshape(D, D)
    f_ple = jnp.transpose(r_all[:, R_PLE:R_ROWS].reshape(NDEV, PLE, 128), (1, 0, 2)).reshape(PLE, D)
    cw = jnp.transpose(cw_all, (1, 0, 2)).reshape(4, CD)

    y, yb, hprev, pre_all = _ssd_fwd(proj, dtr, cw, cb, dtb, alog, dskx, sg, e128)
    dx1, dx1b, ya, mg, hp, dpre, dpe, doa, dob, dya, dyb, dgab, acc = _merge(
        x2, yb, proj, p2, tgt, (ln_g, ln_b, ws, bst), f_oa, f_ob, f_out, f_pg, f_ple, ple_g, fin_g)

    gple = jnp.transpose(_wgrad(p2, dpe, "wgrad_ple").reshape(PLE, NDEV, 128), (1, 0, 2)).reshape(NDEV, 32, D)
    gr = jnp.concatenate([_wgrad(ya, doa, "wgrad_oa").reshape(NDEV, 128, D), _wgrad(yb, dob, "wgrad_ob").reshape(NDEV, 256, D),
                          _wgrad(mg, dx1b, "wgrad_out").reshape(NDEV, 128, D), _wgrad(hp, dpre, "wgrad_pg").reshape(NDEV, 128, D),
                          gple], axis=1)
    (duvz, dws, dbs, dln), _ = _gmlp_bwd(proj, dya, ln_g, ln_b, ws, wst, bst, gsel, _no_exchange())
    (dxz, ddt, dcw, dsm, dsg), (rr,) = _ssd_bwd(proj, pre_all, dtr, y, dyb, hprev, cw, dtb, alog, dskx, sg, e128, et128, _direct_exchange([gr]))

    g_xz = _wgrad(dxz, h, "wgrad_xbc_zb")
    g_w_in_t = jnp.concatenate([_wgrad(duvz, h, "wgrad_u_v_za"), g_xz[CD:], g_xz[:CD], _wgrad(ddt, h, "wgrad_dt")[:NH],
                                _wgrad(dgab, h, "wgrad_ga_gb")], axis=0)
    ga = g_w_in_t.reshape(NDEV, WSH, D)
    wide = jnp.concatenate([dln, acc, dsg.reshape(16, D), dcw.reshape(24, D)], axis=0)
    narrow = jnp.concatenate([dws.reshape(NG * CH, CH), jnp.pad(dbs[:, :NG].T, ((0, 8 - NG), (0, 0))), dsm], axis=0)
    (gx, dng), (ra, wide_all, narrow_all) = _proj_bwd(x2, dx1, norm_g2, wp, [dxz, duvz, dgab], ddt,
                                                      _combine(_direct_exchange([ga]), _direct_gather_exchange([wide, narrow])))
    ga_own = lax.dynamic_index_in_dim(ga, me, 0, keepdims=False)
    gr_own = lax.dynamic_index_in_dim(gr, me, 0, keepdims=False)

    out_g, out_d, out_m, out_v = {}, {}, {}, {}
    outs = (out_g, out_d, out_m, out_v)
    tiles = lambda a: jnp.transpose(a.reshape(8, 128, WSH), (2, 0, 1))
    g2, (norm_all,) = _sum_parts([ga_own] + [(ra, k) for k in range(NDEV - 1)], "sum_w_in", _direct_gather_exchange([dng]))
    res = _adamw_tiles(tiles(w["w_in"]), tiles(m["w_in"]), tiles(v["w_in"]), g2.reshape(WSH, 8, 128), "adamw_w_in")
    for dst, val in zip(outs, res):
        dst["w_in"] = jnp.transpose(val, (1, 2, 0)).reshape(1, D, WSH)
    parts_r = [gr_own] + [(rr, k) for k in range(NDEV - 1)]
    for name, row, rows in (("w_oa", R_OA, 128), ("w_ob", R_OB, 256), ("w_out", R_OUT, 128), ("w_pg", R_PG, 128)):
        res = _adamw(w[name].reshape(rows, D), m[name].reshape(rows, D), v[name].reshape(rows, D), parts_r, "adamw_" + name,
                     part_row=row, block_rows=128)
        for dst, val in zip(outs, res):
            dst[name] = val.reshape(1, rows, D)
    res = _adamw(w["w_ple"].reshape(32, D), m["w_ple"].reshape(32, D), v["w_ple"].reshape(32, D), parts_r, "adamw_w_ple",
                 part_row=R_PLE, block_rows=32)
    for dst, val in zip(outs, res):
        dst["w_ple"] = val.reshape(1, PLE, 128)
    two_d = lambda d: {n: d[n].reshape(_SMALL_SHAPES[n]) for n in _SMALL_SHAPES}
    *res, g_cw_wide, loss = _adamw_small(two_d(w), two_d(m), two_d(v), wide_all, narrow_all, norm_all)
    for dst, val in zip(outs, res):
        dst.update({n: val[n].reshape(w[n].shape) for n in _SMALL_SHAPES})
    g_cw = lax.dynamic_slice_in_dim(g_cw_wide.reshape(4, CD), me * (CD // NDEV), CD // NDEV, axis=1).reshape(12, 128)
    res = _adamw(w["conv_w"].reshape(12, 128), m["conv_w"].reshape(12, 128), v["conv_w"].reshape(12, 128), [g_cw], "adamw_conv_w")
    for dst, val in zip((out_g, out_d, out_m, out_v), res):
        dst["conv_w"] = val.reshape(1, 4, CD // NDEV)

    return (loss, gx.reshape(1, T, D), *[out_g[n] for n in _WEIGHTS], *[out_d[n] for n in _WEIGHTS],
            *[out_m[n] for n in _WEIGHTS], *[out_v[n] for n in _WEIGHTS])
```

```python
import functools
import math
from typing import Callable, NamedTuple

import jax
import jax.numpy as jnp
from jax import lax
from jax.experimental import pallas as pl
from jax.experimental.pallas import tpu as pltpu

F32 = jnp.float32
BF16 = jnp.bfloat16
MESH = pl.DeviceIdType.MESH

D = 1024
DI = 2048
CD = 3072
NH = 32
HD = 64
NST = 128
NG = 4
CH = 128
PLE = 256
NIN = 10272
NDEV = 8
WSH = NIN // NDEV
EPS = 1e-6
OFF_XBC, OFF_ZB, OFF_U, OFF_V, OFF_ZA, OFF_GA, OFF_GB, OFF_DT = 0, 3072, 5120, 6144, 7168, 8192, 9216, 10240
NP = 10368
DTW = 128
R_OA, R_OB, R_OUT, R_PG, R_PLE, R_ROWS = 0, 128, 384, 512, 640, 672

ADAM_LR, ADAM_B1, ADAM_B2, ADAM_EPS, ADAM_WD, ADAM_STEP = 0.001, 0.9, 0.999, 1e-08, 0.01, 10

V7X_VMEM_LIMIT = 56 * 1024 * 1024


def _cp(sem=None):
    return pltpu.CompilerParams(dimension_semantics=sem, vmem_limit_bytes=V7X_VMEM_LIMIT)


def _dot(a, b, prec=None):
    return jnp.dot(a, b, preferred_element_type=F32, precision=prec)


def _dot_nt(a, b, prec=None):
    return lax.dot_general(a, b, (((1,), (1,)), ((), ())), preferred_element_type=F32, precision=prec)


def _dot_tn(a, b, prec=None):
    return lax.dot_general(a, b, (((0,), (0,)), ((), ())), preferred_element_type=F32, precision=prec)


def _sigmoid(x):
    return 1.0 / (1.0 + jnp.exp(-x))


def _gelu_and_grad(x):
    c = math.sqrt(2.0 / math.pi)
    x2 = x * x
    t = jnp.tanh(c * (x + 0.044715 * x * x2))
    g = 0.5 * x * (1.0 + t)
    dg = 0.5 * (1.0 + t) + 0.5 * x * (1.0 - t * t) * c * (1.0 + 3.0 * 0.044715 * x2)
    return g, dg


def _gelu(x):
    c = math.sqrt(2.0 / math.pi)
    return 0.5 * x * (1.0 + jnp.tanh(c * (x + 0.044715 * x * x * x)))


def _softplus(x):
    return jnp.maximum(x, 0.0) + jnp.log(1.0 + jnp.exp(-jnp.abs(x)))


def _full(shape):
    n = len(shape)
    return pl.BlockSpec(shape, lambda *_: (0,) * n)


_ANY = pl.BlockSpec(memory_space=pl.ANY)


class _Exchange(NamedTuple):
    arrays: list
    out_shape: list
    sems: list
    make: Callable


def _call(body, ex, *, name, grid, in_specs, out_specs, out_shape, scratch_shapes, args):
    ki, ko, ks = len(in_specs), len(out_specs), len(scratch_shapes)
    ei, eo = len(ex.arrays), len(ex.out_shape)
    last = [g - 1 for g in grid]

    def full_body(*refs):
        r = list(refs)
        ins, eins, r = r[:ki], r[ki:ki + ei], r[ki + ei:]
        outs, eouts, r = r[:ko], r[ko:ko + eo], r[ko + eo:]
        scr, esems = r[:ks], r[ks:]
        start, finish = ex.make(eins, eouts, esems)
        ids = [pl.program_id(a) for a in range(len(grid))]
        is_first = functools.reduce(lambda p, q: p & q, [i == 0 for i in ids])
        is_last = functools.reduce(lambda p, q: p & q, [i == l for i, l in zip(ids, last)])
        pl.when(is_first)(start)
        body(*ins, *outs, *scr)
        pl.when(is_last)(finish)

    res = pl.pallas_call(
        full_body, name=name, grid=grid, in_specs=list(in_specs) + [_ANY] * ei, out_specs=list(out_specs) + [_ANY] * eo,
        out_shape=list(out_shape) + list(ex.out_shape), scratch_shapes=list(scratch_shapes) + list(ex.sems),
        compiler_params=_cp(("arbitrary",) * len(grid)),
    )(*args, *ex.arrays)
    return res[:ko], res[ko:]


def _run_exchange(ex, name):
    ni, no = len(ex.arrays), len(ex.out_shape)

    def body(*refs):
        start, finish = ex.make(refs[:ni], refs[ni:ni + no], refs[ni + no:])
        start()
        finish()

    return pl.pallas_call(body, name=name, in_specs=[_ANY] * ni, out_specs=[_ANY] * no, out_shape=list(ex.out_shape),
                          scratch_shapes=list(ex.sems))(*ex.arrays)


def _proj_fwd(x, norm_g, wp, ex):
    T = x.shape[0]
    tm, tn = min(T, 1024), 2048
    nj = OFF_DT // tn
    assert OFF_DT % tn == 0 and OFF_DT + DTW == NP

    def body(x_ref, g_ref, w_ref, wdt_ref, proj_ref, dt_ref, h_ref, hs_ref):
        j = pl.program_id(1)

        @pl.when(j == 0)
        def _():
            xf = x_ref[...]
            r = lax.rsqrt(jnp.mean(xf * xf, axis=-1, keepdims=True) + EPS)
            h = (xf * r * g_ref[...]).astype(BF16)
            hs_ref[...] = h
            h_ref[...] = h

        proj_ref[...] = _dot_nt(hs_ref[...], w_ref[...]).astype(BF16)

        @pl.when(j == nj - 1)
        def _():
            dt_ref[...] = _dot_nt(hs_ref[...], wdt_ref[...])

    return _call(
        body, ex, name="proj_fwd", grid=(T // tm, nj),
        in_specs=[pl.BlockSpec((tm, D), lambda i, j: (i, 0)), _full((1, D)), pl.BlockSpec((tn, D), lambda i, j: (j, 0)),
                  pl.BlockSpec((DTW, D), lambda i, j: (OFF_DT // DTW, 0))],
        out_specs=[pl.BlockSpec((tm, tn), lambda i, j: (i, j)), pl.BlockSpec((tm, DTW), lambda i, j: (i, 0)),
                   pl.BlockSpec((tm, D), lambda i, j: (i, 0))],
        out_shape=[jax.ShapeDtypeStruct((T, OFF_DT), BF16), jax.ShapeDtypeStruct((T, DTW), F32), jax.ShapeDtypeStruct((T, D), BF16)],
        scratch_shapes=[pltpu.VMEM((tm, D), BF16)], args=(x, norm_g, wp, wp))


def _gmlp_tile():
    return 512


def _gmlp_fwd_tile(u_ref, v_ref, z_ref, lg_ref, lb_ref, ws_ref, bs_ref, ya_ref, vn_s):
    tm = u_ref.shape[0]
    vg = _gelu(v_ref[...].astype(F32))
    mu = jnp.mean(vg, axis=-1, keepdims=True)
    xc = vg - mu
    rstd = lax.rsqrt(jnp.mean(xc * xc, axis=-1, keepdims=True) + EPS)
    vn_s[...] = (xc * rstd * lg_ref[...] + lb_ref[...]).astype(BF16)
    for c in range(tm // CH):
        rs = slice(c * CH, (c + 1) * CH)
        for g in range(NG):
            cs_ = slice(g * 256, (g + 1) * 256)
            sv = _dot(ws_ref[g], vn_s[rs, cs_]) + bs_ref[g]
            z = z_ref[rs, cs_].astype(F32)
            ya_ref[rs, cs_] = (_gelu(u_ref[rs, cs_].astype(F32)) * sv * (z * _sigmoid(z))).astype(BF16)


def _gmlp_bwd(proj, dya, ln_g, ln_b, ws, wst, bst, gsel, ex):
    T = proj.shape[0]
    tm = min(T, _gmlp_tile())

    def body(u_ref, v_ref, z_ref, dy_ref, lg_ref, lb_ref, ws_ref, wst_ref, bs_ref, gsel_ref,
             d_ref, dws_ref, dbs_ref, dln_ref, vn_s, dsv_s, dvn_s):
        du_ref, dv_ref, dz_ref = d_ref.at[:, 0:D], d_ref.at[:, D:2 * D], d_ref.at[:, 2 * D:3 * D]
        @pl.when(pl.program_id(0) == 0)
        def _():
            dws_ref[...] = jnp.zeros_like(dws_ref)
            dbs_ref[...] = jnp.zeros_like(dbs_ref)
            dln_ref[...] = jnp.zeros_like(dln_ref)

        vg, dvg_dv = _gelu_and_grad(v_ref[...].astype(F32))
        mu = jnp.mean(vg, axis=-1, keepdims=True)
        xc = vg - mu
        rstd = lax.rsqrt(jnp.mean(xc * xc, axis=-1, keepdims=True) + EPS)
        vhat = xc * rstd
        vn_s[...] = (vhat * lg_ref[...] + lb_ref[...]).astype(BF16)
        ri = lax.broadcasted_iota(jnp.int32, (CH, CH), 0)
        ci = lax.broadcasted_iota(jnp.int32, (CH, CH), 1)
        tril = (ri >= ci).astype(F32)
        for c in range(tm // CH):
            rs = slice(c * CH, (c + 1) * CH)
            for g in range(NG):
                cs_ = slice(g * 256, (g + 1) * 256)
                vn = vn_s[rs, cs_]
                sv = _dot(ws_ref[g], vn) + bs_ref[g]
                z = z_ref[rs, cs_].astype(F32)
                sz = _sigmoid(z)
                ug, dug_du = _gelu_and_grad(u_ref[rs, cs_].astype(F32))
                dy = dy_ref[rs, cs_].astype(F32)
                zs = z * sz
                t = dy * zs
                dsv_f = dy * sv
                du_ref[rs, cs_] = (zs * dsv_f * dug_du).astype(BF16)
                dz_ref[rs, cs_] = (dsv_f * ug * (sz + zs * (1.0 - sz))).astype(BF16)
                dsv = (t * ug).astype(BF16)
                dsv_s[rs, cs_] = dsv
                dvn_s[rs, cs_] = _dot(wst_ref[g], dsv)
                dws_ref[g] += _dot_nt(dsv, vn) * tril
            dbs_ref[...] += _dot(dsv_s[rs, :], gsel_ref[...])
        dvn = dvn_s[...]
        dln_ref[0:1, :] += jnp.sum(dvn * vhat, axis=0, keepdims=True)
        dln_ref[1:2, :] += jnp.sum(dvn, axis=0, keepdims=True)
        dvh = dvn * lg_ref[...]
        dvg = rstd * (dvh - jnp.mean(dvh, axis=-1, keepdims=True) - vhat * jnp.mean(dvh * vhat, axis=-1, keepdims=True))
        dv_ref[...] = (dvg * dvg_dv).astype(BF16)

    blk = lambda off: pl.BlockSpec((tm, D), lambda i: (i, off // D))
    row = pl.BlockSpec((tm, D), lambda i: (i, 0))
    return _call(
        body, ex, name="gmlp_bwd", grid=(T // tm,),
        in_specs=[blk(OFF_U), blk(OFF_V), blk(OFF_ZA), row, _full((1, D)), _full((1, D)), _full((NG, CH, CH)),
                  _full((NG, CH, CH)), _full((NG, CH, 256)), _full((D, 128))],
        out_specs=[pl.BlockSpec((tm, 3 * D), lambda i: (i, 0)), _full((NG, CH, CH)), _full((CH, 128)), _full((8, D))],
        out_shape=[jax.ShapeDtypeStruct((T, 3 * D), BF16),
                   jax.ShapeDtypeStruct((NG, CH, CH), F32), jax.ShapeDtypeStruct((CH, 128), F32), jax.ShapeDtypeStruct((8, D), F32)],
        scratch_shapes=[pltpu.VMEM((tm, D), BF16), pltpu.VMEM((tm, D), BF16), pltpu.VMEM((tm, D), F32)],
        args=(proj, proj, proj, dya, ln_g, ln_b, ws, wst, bst, gsel))


def _zb_cols(zb0_ref, zb1_ref, cols):
    ref = zb0_ref if cols.start < D else zb1_ref
    return ref[:, cols.start % D:cols.start % D + (cols.stop - cols.start)]


def _shift_matrix(down):
    t = jnp.arange(CH)[None, :, None]
    j = jnp.arange(1, 4)[:, None, None]
    col = jnp.arange(2 * CH)[None, None, :]
    src = CH + t - j if down else t + j
    return (col == src).astype(BF16).reshape(3 * CH, 2 * CH)


def _conv_pre(x, moved, cw_ref, cb_ref):
    pre = cb_ref[...] + cw_ref[3:4, :] * x
    for j in (1, 2, 3):
        pre = pre + cw_ref[3 - j:4 - j, :] * moved[(j - 1) * CH:j * CH]
    return pre


def _split_dot(x, w, parts, w_left=False):
    acc, r = None, x
    for k in range(parts):
        hi = r.astype(BF16)
        d = _dot(w, hi) if w_left else _dot(hi, w)
        acc = d if acc is None else acc + d
        if k + 1 < parts:
            r = r - hi.astype(F32)
    return acc


def _chunk_decays(dt, alog_ref, e_ref, cs_s, cst_s, csx_s):
    a = -jnp.exp(alog_ref[...])
    ri = lax.broadcasted_iota(jnp.int32, (CH, CH), 0)
    ci = lax.broadcasted_iota(jnp.int32, (CH, CH), 1)
    tril = ri >= ci
    cs = _split_dot(dt * a, tril.astype(BF16), 3, w_left=True)
    cs_s[...] = cs
    cst_s[...] = cs.T
    csx_s[...] = _split_dot(cs, e_ref[...], 3)
    return a, tril, ri, ci


def _lmat(cst_s, h, tril):
    rowb = jnp.broadcast_to(cst_s[h:h + 1, :], (CH, CH))
    return jnp.exp(jnp.where(tril, rowb.T - rowb, -jnp.inf))


def _head_pair_rows(v, lane):
    return jnp.concatenate([jnp.where(lane < HD, v, 0.0), jnp.where(lane < HD, 0.0, v)], axis=0).astype(BF16)


def _ssd_fwd(proj, dtr, cw, cb, dtb, alog, dskx, sg, e128):
    T = proj.shape[0]
    nc = T // CH

    def body(xbc_ref, zb0_ref, zb1_ref, dt_ref, cw_ref, cb_ref, dtb_ref, alog_ref, dx_ref, sg_ref, e_ref, shift_ref,
             y_ref, yb_ref, hp_ref, pre_ref, xx_s, h_s, cs_s, cst_s, csx_s, yz_s):
        @pl.when(pl.program_id(0) == 0)
        def _():
            xx_s[...] = jnp.zeros_like(xx_s)
            h_s[...] = jnp.zeros_like(h_s)

        xx_s[CH:, :] = xbc_ref[...]
        moved = _dot(shift_ref[...], xx_s[...])
        xx_s[CH - 16:CH, :] = xbc_ref[CH - 16:, :]
        x = xbc_ref[...].astype(F32)
        pre = _conv_pre(x, moved, cw_ref, cb_ref)
        pre_ref[...] = pre
        xc = pre * _sigmoid(pre)
        dt = _softplus(dt_ref[...] + dtb_ref[...])
        a, tril, _, lane = _chunk_decays(dt, alog_ref, e_ref, cs_s, cst_s, csx_s)
        dt_x = _split_dot(dt, e_ref[...], 2)
        cs_last_x = csx_s[CH - 1:CH, :]
        hp_ref[0] = h_s[...]
        for g in range(NG):
            gs = slice(g * 512, (g + 1) * 512)
            bg = xc[:, DI + g * NST:DI + (g + 1) * NST].astype(BF16)
            cg = xc[:, DI + 512 + g * NST:DI + 512 + (g + 1) * NST].astype(BF16)
            cbm = _dot_nt(cg, bg)
            xg = xc[:, gs]
            xdt = xg * dt_x[:, gs]
            hprev = h_s[:, gs]
            csx = csx_s[:, gs]
            yoff = _dot(cg, hprev.astype(BF16)) * jnp.exp(csx)
            st = _dot_tn(bg, (xdt * jnp.exp(cs_last_x[:, gs] - csx)).astype(BF16))
            h_s[:, gs] = jnp.exp(cs_last_x[:, gs]) * hprev + st
            ssq = jnp.zeros((CH, 1), F32)
            for q in range(4):
                h0 = g * 8 + 2 * q
                ps = slice(q * 128, (q + 1) * 128)
                cols = slice(g * 512 + q * 128, g * 512 + (q + 1) * 128)
                m01 = jnp.concatenate([cbm * _lmat(cst_s, h0, tril), cbm * _lmat(cst_s, h0 + 1, tril)], axis=1).astype(BF16)
                yq = _dot(m01, _head_pair_rows(xdt[:, ps], lane)) + yoff[:, ps] + xg[:, ps] * dx_ref[:, cols]
                y_ref[:, cols] = yq
                z = _zb_cols(zb0_ref, zb1_ref, cols).astype(F32)
                yz = yq * z * _sigmoid(z)
                yz_s[:, cols] = yz
                ssq = ssq + jnp.sum(yz * yz, axis=1, keepdims=True)
            rg = lax.rsqrt(ssq * (1.0 / 512.0) + EPS)
            yb_ref[:, gs] = (yz_s[:, gs] * rg * sg_ref[:, gs]).astype(BF16)

    return pl.pallas_call(
        body, name="ssd_fwd", grid=(nc,),
        in_specs=[pl.BlockSpec((CH, CD), lambda c: (c, OFF_XBC // CD)), pl.BlockSpec((CH, D), lambda c: (c, OFF_ZB // D)),
                  pl.BlockSpec((CH, D), lambda c: (c, OFF_ZB // D + 1)),
                  pl.BlockSpec((CH, DTW), lambda c: (c, 0)), _full((4, CD)), _full((1, CD)), _full((1, DTW)),
                  _full((1, DTW)), _full((1, DI)), _full((1, DI)), _full((DTW, DI)), _full((3 * CH, 2 * CH))],
        out_specs=[pl.BlockSpec((CH, DI), lambda c: (c, 0)), pl.BlockSpec((CH, DI), lambda c: (c, 0)),
                   pl.BlockSpec((1, NST, DI), lambda c: (c, 0, 0)), pl.BlockSpec((CH, CD), lambda c: (c, 0))],
        out_shape=[jax.ShapeDtypeStruct((T, DI), F32), jax.ShapeDtypeStruct((T, DI), BF16),
                   jax.ShapeDtypeStruct((nc, NST, DI), F32), jax.ShapeDtypeStruct((T, CD), F32)],
        scratch_shapes=[pltpu.VMEM((2 * CH, CD), BF16), pltpu.VMEM((NST, DI), F32), pltpu.VMEM((CH, CH), F32),
                        pltpu.VMEM((CH, CH), F32), pltpu.VMEM((CH, DI), F32), pltpu.VMEM((CH, DI), F32)],
        compiler_params=_cp(("arbitrary",)),
    )(proj, proj, proj, dtr, cw, cb, dtb, alog, dskx, sg, e128, _shift_matrix(down=True))


def _ssd_bwd(proj, pre_all, dtr, y, dyb, hprev_all, cw, dtb, alog, dskx, sg, e128, et128, ex):
    T = proj.shape[0]
    nc = T // CH

    def body(xbc_ref, pre_ref, zb0_ref, zb1_ref, dt_ref, y_ref, dyb_ref, hp_ref, cw_ref, dtb_ref, alog_ref, dx_ref, sg_ref,
             e_ref, et_ref, shift_ref, d_ref, ddt_ref, dcw_ref, dsm_ref, dsg_ref,
             g_s, dd_s, cs_s, cst_s, csx_s, dy_s, dxdt_s, dxs_s, dsd_s, dxc_s, gh_s):
        dxbc_ref, dzb_ref = d_ref.at[:, 0:CD], d_ref.at[:, CD:CD + DI]
        i = pl.program_id(0)

        @pl.when(i == 0)
        def _():
            g_s[...] = jnp.zeros_like(g_s)
            dd_s[...] = jnp.zeros_like(dd_s)
            dcw_ref[...] = jnp.zeros_like(dcw_ref)
            dsm_ref[...] = jnp.zeros_like(dsm_ref)
            dsg_ref[...] = jnp.zeros_like(dsg_ref)

        pre = pre_ref[...]
        sp = _sigmoid(pre)
        xc = pre * sp
        dtr = dt_ref[...] + dtb_ref[...]
        dt = _softplus(dtr)
        a, tril, ri, lane = _chunk_decays(dt, alog_ref, e_ref, cs_s, cst_s, csx_s)
        et = et_ref[...]
        dt_x = _split_dot(dt, e_ref[...], 2)
        cs_last_x = csx_s[CH - 1:CH, :]

        for g in range(NG):
            gs = slice(g * 512, (g + 1) * 512)
            z = _zb_cols(zb0_ref, zb1_ref, gs).astype(F32)
            sz = _sigmoid(z)
            yv = y_ref[:, gs]
            zs = z * sz
            yz = yv * zs
            rg = lax.rsqrt(jnp.mean(yz * yz, axis=-1, keepdims=True) + EPS)
            yn = yz * rg
            dyb = dyb_ref[:, gs].astype(F32)
            dsg_ref[0:1, gs] += jnp.sum(dyb * yn, axis=0, keepdims=True)
            dyn = dyb * sg_ref[:, gs]
            dyz = rg * (dyn - yn * jnp.mean(dyn * yn, axis=-1, keepdims=True))
            dy_s[:, gs] = dyz * zs
            dzb_ref[:, gs] = (dyz * yv * (sz + zs * (1.0 - sz))).astype(BF16)

        rsum = jnp.zeros((CH, DTW), F32)
        csum_t = jnp.zeros((DTW, CH), F32)
        for g in range(NG):
            gs = slice(g * 512, (g + 1) * 512)
            bg = xc[:, DI + g * NST:DI + (g + 1) * NST].astype(BF16)
            cg = xc[:, DI + 512 + g * NST:DI + 512 + (g + 1) * NST].astype(BF16)
            cbm = _dot_nt(cg, bg)
            xdt = xc[:, gs] * dt_x[:, gs]
            hprev = hp_ref[0, :, gs]
            hpb = hprev.astype(BF16)
            gn = g_s[:, gs]
            gnb = gn.astype(BF16)
            dy = dy_s[:, gs]
            csx = csx_s[:, gs]
            ecs = jnp.exp(csx)
            dec = jnp.exp(cs_last_x[:, gs] - csx)
            dye = (dy * ecs).astype(BF16)
            dc = _dot_nt(dye, hpb)
            dprev = _dot_tn(cg, dye)
            dxdt_state = dec * _dot(bg, gnb)
            db = _dot_nt((xdt * dec).astype(BF16), gnb)
            dcb = jnp.zeros((CH, CH), F32)
            for q in range(4):
                h0 = g * 8 + 2 * q
                ps = slice(q * 128, (q + 1) * 128)
                dyp = dy[:, ps]
                l0 = _lmat(cst_s, h0, tril)
                l1 = _lmat(cst_s, h0 + 1, tril)
                m0 = cbm * l0
                m1 = cbm * l1
                dm = _dot_nt(dyp.astype(BF16), _head_pair_rows(xdt[:, ps], lane))
                dm0 = dm[:, :CH]
                dm1 = dm[:, CH:]
                dcb = dcb + dm0 * l0 + dm1 * l1
                for hh, qm in ((h0, dm0 * m0), (h0 + 1, dm1 * m1)):
                    rsum = jnp.where(lane == hh, jnp.sum(qm, axis=1, keepdims=True), rsum)
                    csum_t = jnp.where(ri == hh, jnp.sum(qm, axis=0, keepdims=True), csum_t)
                mst = jnp.concatenate([m0, m1], axis=0).astype(BF16)
                d = _dot_tn(mst, _head_pair_rows(dyp, lane))
                dxdt_s[:, g * 512 + q * 128:g * 512 + (q + 1) * 128] = d + dxdt_state[:, ps]
            yoff = _dot(cg, hpb) * ecs
            dsd_s[:, gs] = xdt * dxdt_state
            dxs_s[:, gs] = dy * yoff
            dcbb = dcb.astype(BF16)
            dxc_s[:, DI + 512 + g * NST:DI + 512 + (g + 1) * NST] = dc + _dot(dcbb, bg)
            dxc_s[:, DI + g * NST:DI + (g + 1) * NST] = db + _dot_tn(dcbb, cg)
            gh_s[:, gs] = jnp.broadcast_to(jnp.sum(gn * hprev, axis=0, keepdims=True), (8, 512))
            g_s[:, gs] = dprev + jnp.exp(cs_last_x[:, gs]) * gn

        xs = xc[:, :DI]
        dy = dy_s[...]
        dxdt = dxdt_s[...]
        cs_last = cs_s[CH - 1:CH, :]
        state_e = _split_dot(dsd_s[...], et, 2)
        dcd = 0.125 * jnp.sum(_split_dot(gh_s[...], et, 2), axis=0, keepdims=True) * jnp.exp(cs_last)
        row = lax.broadcasted_iota(jnp.int32, (CH, 1), 0)
        dcs = rsum - csum_t.T + _split_dot(dxs_s[...], et, 2) - state_e
        dcs = dcs + jnp.where(row == CH - 1, jnp.sum(state_e, axis=0, keepdims=True) + dcd, 0.0)
        dda = _split_dot(dcs, (lane >= ri).astype(BF16), 3, w_left=True)
        ddt = dda * a + _dot((dxdt * xs).astype(BF16), et)
        ddtr = jnp.where(lane < NH, ddt * _sigmoid(dtr), 0.0)
        ddt_ref[...] = ddtr.astype(BF16)
        dsm_ref[0:1, :] += jnp.sum(ddtr, axis=0, keepdims=True)
        dsm_ref[1:2, :] += jnp.sum(dda * dt, axis=0, keepdims=True) * a
        dsm_ref[2:3, :] += jnp.sum(_dot((dy * xs).astype(BF16), et), axis=0, keepdims=True)
        dxc_s[:, :DI] = dxdt * dt_x + dy * dx_ref[...]

        dpre = dxc_s[...] * (sp + xc * (1.0 - sp))
        dpre_b = dpre.astype(BF16)
        dd_s[:CH, :] = dpre_b
        moved = _dot(shift_ref[...], dd_s[...])
        dd_s[CH:CH + 16, :] = dpre_b[:16]
        x = xbc_ref[...].astype(F32)
        dcw_ref[4:5, :] += jnp.sum(dpre, axis=0, keepdims=True)
        dxbc = cw_ref[3:4, :] * dpre
        dcw_ref[3:4, :] += jnp.sum(dpre * x, axis=0, keepdims=True)
        for j in (1, 2, 3):
            ahead = moved[(j - 1) * CH:j * CH]
            dcw_ref[3 - j:4 - j, :] += jnp.sum(ahead * x, axis=0, keepdims=True)
            dxbc = dxbc + cw_ref[3 - j:4 - j, :] * ahead
        dxbc_ref[...] = dxbc.astype(BF16)

    rev = lambda c: nc - 1 - c
    return _call(
        body, ex, name="ssd_bwd", grid=(nc,),
        in_specs=[pl.BlockSpec((CH, CD), lambda c: (rev(c), OFF_XBC // CD)),
                  pl.BlockSpec((CH, CD), lambda c: (rev(c), 0)),
                  pl.BlockSpec((CH, D), lambda c: (rev(c), OFF_ZB // D)), pl.BlockSpec((CH, D), lambda c: (rev(c), OFF_ZB // D + 1)),
                  pl.BlockSpec((CH, DTW), lambda c: (rev(c), 0)),
                  pl.BlockSpec((CH, DI), lambda c: (rev(c), 0)), pl.BlockSpec((CH, DI), lambda c: (rev(c), 0)),
                  pl.BlockSpec((1, NST, DI), lambda c: (rev(c), 0, 0)),
                  _full((4, CD)), _full((1, DTW)), _full((1, DTW)), _full((1, DI)), _full((1, DI)),
                  _full((DTW, DI)), _full((DI, DTW)), _full((3 * CH, 2 * CH))],
        out_specs=[pl.BlockSpec((CH, CD + DI), lambda c: (rev(c), 0)),
                   pl.BlockSpec((CH, DTW), lambda c: (rev(c), 0)), _full((8, CD)), _full((8, DTW)), _full((8, DI))],
        out_shape=[jax.ShapeDtypeStruct((T, CD + DI), BF16), jax.ShapeDtypeStruct((T, DTW), BF16),
                   jax.ShapeDtypeStruct((8, CD), F32), jax.ShapeDtypeStruct((8, DTW), F32), jax.ShapeDtypeStruct((8, DI), F32)],
        scratch_shapes=[pltpu.VMEM((NST, DI), F32), pltpu.VMEM((2 * CH, CD), BF16), pltpu.VMEM((CH, CH), F32), pltpu.VMEM((CH, CH), F32),
                        pltpu.VMEM((CH, DI), F32), pltpu.VMEM((CH, DI), F32), pltpu.VMEM((CH, DI), F32), pltpu.VMEM((CH, DI), F32),
                        pltpu.VMEM((CH, DI), F32), pltpu.VMEM((CH, CD), F32), pltpu.VMEM((8, DI), F32)],
        args=(proj, pre_all, proj, proj, dtr, y, dyb, hprev_all, cw, dtb, alog, dskx, sg, e128, et128, _shift_matrix(down=False)))


def _merge_tile():
    return 256


def _merge(x, yb, proj, p, tgt, gmlp, w_oa, w_ob, w_out, w_pg, w_ple, ple_g, fin_g):
    T = x.shape[0]
    tm = min(T, _merge_tile())

    def body(x_ref, u_ref, v_ref, za_ref, yb_ref, ga_ref, gb_ref, p_ref, t_ref, lg_ref, lb_ref, ws_ref, bs_ref,
             woa, wob, wout, wpg, wple, pg_ref, fg_ref,
             dx1_ref, dx1b_ref, ya_ref, mg_ref, hp_ref, dpre_ref, dpe_ref, doa_ref, dob_ref, dya_ref, dyb_ref, dg_ref, acc_ref, vn_s):
        @pl.when(pl.program_id(0) == 0)
        def _():
            acc_ref[...] = jnp.zeros_like(acc_ref)

        _gmlp_fwd_tile(u_ref, v_ref, za_ref, lg_ref, lb_ref, ws_ref, bs_ref, ya_ref, vn_s)
        oa = _dot(ya_ref[...], woa[...])
        ob = _dot(yb_ref[...], wob[...])
        sa = _sigmoid(ga_ref[...].astype(F32))
        sb = _sigmoid(gb_ref[...].astype(F32))
        mg = sa * oa + sb * ob
        mgb = mg.astype(BF16)
        mg_ref[...] = mgb
        x1 = x_ref[...] + _dot(mgb, wout[...])
        r2 = lax.rsqrt(jnp.mean(x1 * x1, axis=-1, keepdims=True) + EPS)
        xh1 = x1 * r2
        hpb = (xh1 * pg_ref[...]).astype(BF16)
        hp_ref[...] = hpb
        gate = _sigmoid(_dot(hpb, wpg[...]))
        pe = _dot(p_ref[...].astype(BF16), wple[...])
        x2 = x1 + gate * pe
        r3 = lax.rsqrt(jnp.mean(x2 * x2, axis=-1, keepdims=True) + EPS)
        xh2 = x2 * r3
        err = xh2 * fg_ref[...] - t_ref[...]
        acc_ref[2:3, :] += 0.5 * jnp.sum(jnp.mean(err * err, axis=-1, keepdims=True))
        dyo = err * (1.0 / D)
        acc_ref[0:1, :] += jnp.sum(dyo * xh2, axis=0, keepdims=True)
        dn = dyo * fg_ref[...]
        dx2 = r3 * (dn - xh2 * jnp.mean(dn * xh2, axis=-1, keepdims=True))
        dpe_ref[...] = (dx2 * gate).astype(BF16)
        dpre = (dx2 * pe * gate * (1.0 - gate)).astype(BF16)
        dpre_ref[...] = dpre
        dhp = _dot_nt(dpre, wpg[...])
        acc_ref[1:2, :] += jnp.sum(dhp * xh1, axis=0, keepdims=True)
        dhn = dhp * pg_ref[...]
        dx1 = dx2 + r2 * (dhn - xh1 * jnp.mean(dhn * xh1, axis=-1, keepdims=True))
        dx1_ref[...] = dx1
        dx1b = dx1.astype(BF16)
        dx1b_ref[...] = dx1b
        dmg = _dot_nt(dx1b, wout[...])
        doa = (dmg * sa).astype(BF16)
        dob = (dmg * sb).astype(BF16)
        doa_ref[...] = doa
        dob_ref[...] = dob
        dg_ref[:, :D] = (dmg * oa * sa * (1.0 - sa)).astype(BF16)
        dg_ref[:, D:] = (dmg * ob * sb * (1.0 - sb)).astype(BF16)
        dya_ref[...] = _dot_nt(doa, woa[...]).astype(BF16)
        dyb_ref[...] = _dot_nt(dob, wob[...]).astype(BF16)

    row = lambda w: pl.BlockSpec((tm, w), lambda i: (i, 0))
    blk = lambda off: pl.BlockSpec((tm, D), lambda i: (i, off // D))
    wsp = lambda s: pl.BlockSpec(s, lambda i: (0, 0), pipeline_mode=pl.Buffered(1))
    return pl.pallas_call(
        body, name="merge", grid=(T // tm,),
        in_specs=[row(D), blk(OFF_U), blk(OFF_V), blk(OFF_ZA), row(DI), blk(OFF_GA), blk(OFF_GB), row(PLE), row(D),
                  _full((1, D)), _full((1, D)), _full((NG, CH, CH)), _full((NG, CH, 256)),
                  wsp((D, D)), wsp((DI, D)), wsp((D, D)), wsp((D, D)), wsp((PLE, D)), _full((1, D)), _full((1, D))],
        out_specs=[row(D)] * 10 + [row(DI), row(2 * D), _full((8, D))],
        out_shape=[jax.ShapeDtypeStruct((T, D), F32)] + [jax.ShapeDtypeStruct((T, D), BF16)] * 9
        + [jax.ShapeDtypeStruct((T, DI), BF16), jax.ShapeDtypeStruct((T, 2 * D), BF16), jax.ShapeDtypeStruct((8, D), F32)],
        scratch_shapes=[pltpu.VMEM((tm, D), BF16)],
        compiler_params=_cp(("arbitrary",)),
    )(x, proj, proj, proj, yb, proj, proj, p, tgt, *gmlp, w_oa, w_ob, w_out, w_pg, w_ple, ple_g, fin_g)


def _wgrad(a, b, name):
    T, K = a.shape
    N = b.shape[1]
    tt, tk, tn = min(T, 4096), min(K, 1024), min(N, 1024)
    nt = T // tt

    def body(a_ref, b_ref, o_ref, acc_s):
        t = pl.program_id(2)

        @pl.when(t == 0)
        def _():
            acc_s[...] = jnp.zeros_like(acc_s)

        acc_s[...] += _dot_tn(a_ref[...].astype(BF16), b_ref[...])

        @pl.when(t == nt - 1)
        def _():
            o_ref[...] = acc_s[...].astype(BF16)

    return pl.pallas_call(
        body, name=name, grid=(K // tk, N // tn, nt),
        in_specs=[pl.BlockSpec((tt, tk), lambda k, n, t: (t, k)), pl.BlockSpec((tt, tn), lambda k, n, t: (t, n))],
        out_specs=pl.BlockSpec((tk, tn), lambda k, n, t: (k, n)),
        out_shape=jax.ShapeDtypeStruct((K, N), BF16),
        scratch_shapes=[pltpu.VMEM((tk, tn), F32)],
        compiler_params=_cp(("parallel", "parallel", "arbitrary")),
    )(a, b)


def _proj_bwd(x, dx1, norm_g, wt, pieces, ddt, ex):
    T = x.shape[0]
    tm = min(T, 1024)
    nk = OFF_DT // D + 1
    starts = [sum(a.shape[1] for a in pieces[:n]) // D for n in range(len(pieces))]
    ranges = [(s, s + a.shape[1] // D) for s, a in zip(starts, pieces)]
    assert ranges[-1][1] == nk - 1
    npc = len(pieces)

    def body(x_hbm, dx1_hbm, g_ref, w_ref, wdt_ref, *rest):
        piece_refs, ddt_ref, gx_ref, dng_ref = rest[:npc], rest[npc], rest[npc + 1], rest[npc + 2]
        acc_s, x_ref, dx1_ref, row_sems = rest[npc + 3:]
        i, k = pl.program_id(0), pl.program_id(1)
        rows = pl.ds(pl.multiple_of(i * tm, tm), tm)
        fetches = [pltpu.make_async_copy(x_hbm.at[rows], x_ref, row_sems.at[0]),
                   pltpu.make_async_copy(dx1_hbm.at[rows], dx1_ref, row_sems.at[1])]

        @pl.when((i == 0) & (k == 0))
        def _():
            dng_ref[...] = jnp.zeros_like(dng_ref)

        @pl.when(k == 0)
        def _():
            acc_s[...] = jnp.zeros_like(acc_s)

        @pl.when(k == 1)
        def _():
            for cp in fetches:
                cp.start()

        for ref, (lo, hi) in zip(piece_refs, ranges):
            @pl.when((k >= lo) & (k < hi))
            def _(ref=ref):
                acc_s[...] += _dot(ref[...], w_ref[...])

        @pl.when(k == nk - 1)
        def _():
            for cp in fetches:
                cp.wait()
            dh = acc_s[...] + _dot(ddt_ref[...], wdt_ref[...])
            xf = x_ref[...]
            r = lax.rsqrt(jnp.mean(xf * xf, axis=-1, keepdims=True) + EPS)
            xh = xf * r
            dng_ref[0:1, :] += jnp.sum(dh * xh, axis=0, keepdims=True)
            dxn = dh * g_ref[...]
            gx_ref[...] = dx1_ref[...] + r * (dxn - xh * jnp.mean(dxn * xh, axis=-1, keepdims=True))

    def piece_spec(lo, hi):
        return pl.BlockSpec((tm, D), lambda i, k: (i, jnp.clip(k - lo, 0, hi - lo - 1)))

    row = pl.BlockSpec((tm, D), lambda i, k: (i, 0))
    return _call(
        body, ex, name="proj_bwd", grid=(T // tm, nk),
        in_specs=[_ANY, _ANY, _full((1, D)), pl.BlockSpec((D, D), lambda i, k: (jnp.minimum(k, nk - 2), 0)),
                  pl.BlockSpec((DTW, D), lambda i, k: (OFF_DT // DTW, 0))]
        + [piece_spec(lo, hi) for lo, hi in ranges] + [pl.BlockSpec((tm, DTW), lambda i, k: (i, 0))],
        out_specs=[row, _full((8, D))],
        out_shape=[jax.ShapeDtypeStruct((T, D), F32), jax.ShapeDtypeStruct((8, D), F32)],
        scratch_shapes=[pltpu.VMEM((tm, D), F32), pltpu.VMEM((tm, D), F32), pltpu.VMEM((tm, D), F32), pltpu.SemaphoreType.DMA((2,))],
        args=(x, dx1, norm_g, wt, wt, *pieces, ddt))


def _elementwise_tile(R, C, limit=1 << 20):
    if R * C * 4 <= limit:
        return R, C
    rows = [t for t in range(16, R, 16) if R % t == 0 and t * C * 4 <= limit]
    if rows:
        return rows[-1], C
    cols = [t for t in range(128, C, 128) if C % t == 0 and R * t * 4 <= limit]
    return R, cols[-1]


def _adam_update(w, m, v, g):
    c1 = 1.0 - ADAM_B1 ** ADAM_STEP
    c2 = 1.0 - ADAM_B2 ** ADAM_STEP
    mm = ADAM_B1 * m + (1.0 - ADAM_B1) * g
    vv = ADAM_B2 * v + (1.0 - ADAM_B2) * (g * g)
    return -ADAM_LR * ((mm / c1) / (jnp.sqrt(vv / c2) + ADAM_EPS) + ADAM_WD * w), mm, vv


def _sum_parts(parts, name, ex):
    first = parts[0][0] if isinstance(parts[0], tuple) else parts[0]
    R, C = first.shape[-2:]
    tr, tc = _elementwise_tile(R, C)
    flat = pl.BlockSpec((tr, tc), lambda i, j: (i, j))
    slot = lambda k: pl.BlockSpec((None, tr, tc), lambda i, j: (k, i, j))
    specs = [slot(p[1]) if isinstance(p, tuple) else flat for p in parts]
    arrays = [p[0] if isinstance(p, tuple) else p for p in parts]

    def body(*refs):
        g = refs[0][...].astype(F32)
        for r in refs[1:-1]:
            g = g + r[...].astype(F32)
        refs[-1][...] = g

    (out,), got = _call(body, ex, name=name, grid=(R // tr, C // tc), in_specs=specs, out_specs=[flat],
                        out_shape=[jax.ShapeDtypeStruct((R, C), F32)], scratch_shapes=[], args=arrays)
    return out, got


def _adamw_tiles(w, m, v, g, name):
    R = w.shape[0]
    tr = 107
    assert R % tr == 0
    spec = pl.BlockSpec((tr, 8, 128), lambda i: (i, 0, 0))

    def body(w_ref, m_ref, v_ref, g_in, g_ref, d_ref, nm_ref, nv_ref):
        g = g_in[...]
        g_ref[...] = g
        d_ref[...], nm_ref[...], nv_ref[...] = _adam_update(w_ref[...], m_ref[...], v_ref[...], g)

    return pl.pallas_call(body, name=name, grid=(R // tr,), in_specs=[spec] * 4, out_specs=[spec] * 4,
                          out_shape=[jax.ShapeDtypeStruct(w.shape, F32)] * 4, compiler_params=_cp(("parallel",)))(w, m, v, g)


def _adamw(w, m, v, parts, name, part_row=0, block_rows=None):
    R, C = w.shape
    tr, tc = _elementwise_tile(R, C) if block_rows is None else (block_rows, C)
    assert R % tr == 0 and part_row % tr == 0
    first = part_row // tr
    n = len(parts)
    wspec = pl.BlockSpec((tr, tc), lambda i, j: (i, j))
    flat = pl.BlockSpec((tr, tc), lambda i, j: (first + i, j))
    slot = lambda k: pl.BlockSpec((None, tr, tc), lambda i, j: (k, first + i, j))
    part_specs = [slot(p[1]) if isinstance(p, tuple) else flat for p in parts]
    part_arrays = [p[0] if isinstance(p, tuple) else p for p in parts]

    def body(*refs):
        w_ref, m_ref, v_ref = refs[:3]
        g_ref, d_ref, nm_ref, nv_ref = refs[3 + n:]
        g = refs[3][...].astype(F32)
        for r in refs[4:3 + n]:
            g = g + r[...].astype(F32)
        g_ref[...] = g
        d_ref[...], nm_ref[...], nv_ref[...] = _adam_update(w_ref[...], m_ref[...], v_ref[...], g)

    return pl.pallas_call(
        body, name=name, grid=(R // tr, C // tc), in_specs=[wspec] * 3 + part_specs, out_specs=[wspec] * 4,
        out_shape=[jax.ShapeDtypeStruct(w.shape, F32)] * 4, compiler_params=_cp(("parallel", "parallel")),
    )(w, m, v, *part_arrays)


_SMALL_WIDE = {"ln_a_g": 0, "ln_a_b": 1, "final_g": 8, "ple_norm_g": 9, "ssm_norm_g": 16, "conv_b": 44}
_WIDE_CONV_W, _WIDE_ROWS = 32, 56
_WIDE_LOSS = 10
_SMALL_NARROW = {"w_s": (0, NG * CH, 128), "b_s": (512, NG, 128), "dt_bias": (520, 1, NH), "a_log": (521, 1, NH),
                 "d_skip": (522, 1, NH)}
_NARROW_ROWS = 528
_SMALL_SHAPES = {"norm_g": (1, D), "ln_a_g": (1, D), "ln_a_b": (1, D), "ple_norm_g": (1, D), "final_g": (1, D),
                 "ssm_norm_g": (1, DI), "conv_b": (1, CD), "w_s": (NG * CH, CH), "b_s": (NG, CH), "dt_bias": (1, NH),
                 "a_log": (1, NH), "d_skip": (1, NH)}


def _adamw_small(w, m, v, wide_all, narrow_all, norm_all):
    names = list(_SMALL_SHAPES)
    n = len(names)

    def body(*refs):
        wr, mr, vr = refs[:n], refs[n:2 * n], refs[2 * n:3 * n]
        wide_ref, narrow_ref, norm_ref = refs[3 * n:3 * n + 3]
        outs = refs[3 * n + 3:]
        gr, dr, nmr, nvr, cw_ref, loss_ref = outs[:n], outs[n:2 * n], outs[2 * n:3 * n], outs[3 * n:4 * n], outs[4 * n], outs[4 * n + 1]

        def total(ref, rows, lanes):
            acc = ref[0, rows, lanes]
            for d in range(1, NDEV):
                acc = acc + ref[d, rows, lanes]
            return acc

        for k, name in enumerate(names):
            if name in _SMALL_WIDE or name == "norm_g":
                for part in range(_SMALL_SHAPES[name][1] // D):
                    pack, r = (norm_ref, 0) if name == "norm_g" else (wide_ref, _SMALL_WIDE[name] + part)
                    cols = slice(part * D, (part + 1) * D)
                    g = total(pack, slice(r, r + 1), slice(None))
                    gr[k][:, cols] = g
                    dr[k][:, cols], nmr[k][:, cols], nvr[k][:, cols] = _adam_update(wr[k][:, cols], mr[k][:, cols], vr[k][:, cols], g)
            else:
                r, rows, lanes = _SMALL_NARROW[name]
                g = total(narrow_ref, slice(r, r + rows), slice(0, lanes))
                gr[k][...] = g
                dr[k][...], nmr[k][...], nvr[k][...] = _adam_update(wr[k][...], mr[k][...], vr[k][...], g)
        cw_ref[...] = total(wide_ref, slice(_WIDE_CONV_W, _WIDE_CONV_W + 12), slice(None))
        loss_ref[...] = total(wide_ref, slice(_WIDE_LOSS, _WIDE_LOSS + 1), slice(None))

    shapes = [jax.ShapeDtypeStruct(_SMALL_SHAPES[k], F32) for k in names]
    specs = [_full(_SMALL_SHAPES[k]) for k in names]
    res = pl.pallas_call(
        body, name="adamw_small", grid=(1,),
        in_specs=specs * 3 + [_full(wide_all.shape), _full(narrow_all.shape), _full(norm_all.shape)],
        out_specs=specs * 4 + [_full((12, D)), _full((1, D))],
        out_shape=shapes * 4 + [jax.ShapeDtypeStruct((12, D), F32), jax.ShapeDtypeStruct((1, D), F32)],
        compiler_params=_cp(("arbitrary",)),
    )(*[w[k] for k in names], *[m[k] for k in names], *[v[k] for k in names], wide_all, narrow_all, norm_all)
    groups = [dict(zip(names, res[q * n:(q + 1) * n])) for q in range(4)]
    return groups[0], groups[1], groups[2], groups[3], res[4 * n], res[4 * n + 1][0, 0]


def _dev_index(px, py, pc):
    return 4 * px + 2 * py + pc


def _mesh_position():
    return lax.axis_index("x"), lax.axis_index("y"), lax.axis_index("c")


def _gather_exchange(blocks):
    n = len(blocks)

    def make(ins, outs, sems):
        send_sems, recv_sems, local_sems = sems
        x, y, c = _mesh_position()
        me, sibling = (x, y, c), (x, y, 1 - c)
        chips = [(1 - x, y), (x, 1 - y), (1 - x, 1 - y)]

        def copy(a, k, block, to, src=None):
            dst = outs[a].at[_dev_index(*block)]
            return pltpu.make_async_remote_copy(src_ref=dst if src is None else src, dst_ref=dst, send_sem=send_sems.at[a, k],
                                                recv_sem=recv_sems.at[a, k], device_id=to, device_id_type=MESH)

        mine = [pltpu.make_async_copy(ins[a], outs[a].at[_dev_index(*me)], local_sems.at[a]) for a in range(n)]
        first = []
        for a in range(n):
            first.append(copy(a, 0, me, sibling, src=ins[a]))
            first += [copy(a, 1 + j, me, (*chip, c), src=ins[a]) for j, chip in enumerate(chips)]

        def start():
            for cp in mine + first:
                cp.start()

        def finish():
            passed = []
            for j, chip in enumerate(chips):
                for a in range(n):
                    copy(a, 1 + j, (*chip, c), me).wait_recv()
                    fwd = copy(a, 4 + j, (*chip, c), sibling)
                    fwd.start()
                    passed.append(fwd)
            for a in range(n):
                copy(a, 0, sibling, me).wait_recv()
                for j, chip in enumerate(chips):
                    copy(a, 4 + j, (*chip, 1 - c), me).wait_recv()
            for cp in first + passed:
                cp.wait_send()
            for cp in mine:
                cp.wait()

        return start, finish

    return _Exchange(list(blocks), [jax.ShapeDtypeStruct((NDEV,) + b.shape, b.dtype) for b in blocks],
                     [pltpu.SemaphoreType.DMA((n, 7)), pltpu.SemaphoreType.DMA((n, 7)), pltpu.SemaphoreType.DMA((n,))], make)


def _relay_gather_exchange(blocks):
    n = len(blocks)

    def make(ins, outs, sems):
        send_sems, recv_sems, local_sems = sems
        x, y, c = _mesh_position()
        me, sibling = (x, y, c), (x, y, 1 - c)
        x_nbr, y_nbr, diag = (1 - x, y), (x, 1 - y), (1 - x, 1 - y)
        relay_from = (jnp.where(c == 0, x, 1 - x), jnp.where(c == 0, 1 - y, y))
        relay_to = (jnp.where(c == 0, 1 - x, x), jnp.where(c == 0, y, 1 - y))

        def copy(a, k, block, to, src=None):
            dst = outs[a].at[_dev_index(*block)]
            return pltpu.make_async_remote_copy(src_ref=dst if src is None else src, dst_ref=dst, send_sem=send_sems.at[a, k],
                                                recv_sem=recv_sems.at[a, k], device_id=to, device_id_type=MESH)

        mine = [pltpu.make_async_copy(ins[a], outs[a].at[_dev_index(*me)], local_sems.at[a]) for a in range(n)]
        first = []
        for a in range(n):
            first += [copy(a, 0, me, sibling, src=ins[a]), copy(a, 1, me, (*x_nbr, c), src=ins[a]), copy(a, 2, me, (*y_nbr, c), src=ins[a])]

        def start():
            for cp in mine + first:
                cp.start()

        def finish():
            later = []
            for a in range(n):
                copy(a, 1, (*x_nbr, c), me).wait_recv()
                copy(a, 2, (*y_nbr, c), me).wait_recv()
                later.append(copy(a, 3, (*relay_from, c), (*relay_to, c)))
                later += [copy(a, 4, (*x_nbr, c), sibling), copy(a, 5, (*y_nbr, c), sibling)]
                for cp in later[-3:]:
                    cp.start()
            for a in range(n):
                copy(a, 3, (*diag, c), me).wait_recv()
                later.append(copy(a, 6, (*diag, c), sibling))
                later[-1].start()
            for a in range(n):
                copy(a, 0, sibling, me).wait_recv()
                for k, chip in ((4, x_nbr), (5, y_nbr), (6, diag)):
                    copy(a, k, (*chip, 1 - c), me).wait_recv()
            for cp in first + later:
                cp.wait_send()
            for cp in mine:
                cp.wait()

        return start, finish

    return _Exchange(list(blocks), [jax.ShapeDtypeStruct((NDEV,) + b.shape, b.dtype) for b in blocks],
                     [pltpu.SemaphoreType.DMA((n, 7)), pltpu.SemaphoreType.DMA((n, 7)), pltpu.SemaphoreType.DMA((n,))], make)


def _combine(*exchanges):
    def make(ins, outs, sems):
        pairs = []
        for e in exchanges:
            ni, no, ns = len(e.arrays), len(e.out_shape), len(e.sems)
            pairs.append(e.make(ins[:ni], outs[:no], sems[:ns]))
            ins, outs, sems = ins[ni:], outs[no:], sems[ns:]

        def start():
            for s, _ in pairs:
                s()

        def finish():
            for _, f in pairs:
                f()

        return start, finish

    return _Exchange(sum((list(e.arrays) for e in exchanges), []), sum((list(e.out_shape) for e in exchanges), []),
                     sum((list(e.sems) for e in exchanges), []), make)


def _start_wait_all(copies, local=()):
    def start():
        for cp in list(local) + list(copies):
            cp.start()

    def finish():
        for cp in copies:
            cp.wait()
        for cp in local:
            cp.wait()

    return start, finish


def _no_exchange():
    return _Exchange([], [], [], lambda ins, outs, sems: (lambda: None, lambda: None))


def _direct_exchange(grads):
    n = len(grads)

    def make(ins, outs, sems):
        send_sems, recv_sems = sems
        x, y, c = _mesh_position()
        copies = []
        for a in range(n):
            for r in range(1, NDEV):
                peer = (x ^ ((r >> 2) & 1), y ^ ((r >> 1) & 1), c ^ (r & 1))
                copies.append(pltpu.make_async_remote_copy(
                    src_ref=ins[a].at[_dev_index(*peer)], dst_ref=outs[a].at[r - 1], send_sem=send_sems.at[a, r - 1],
                    recv_sem=recv_sems.at[a, r - 1], device_id=peer, device_id_type=MESH))
        return _start_wait_all(copies)

    return _Exchange(list(grads), [jax.ShapeDtypeStruct((NDEV - 1,) + g.shape[1:], g.dtype) for g in grads],
                     [pltpu.SemaphoreType.DMA((n, NDEV - 1)), pltpu.SemaphoreType.DMA((n, NDEV - 1))], make)


def _direct_gather_exchange(smalls):
    n = len(smalls)

    def make(ins, outs, sems):
        send_sems, recv_sems, local_sems = sems
        x, y, c = _mesh_position()
        copies, local = [], []
        for a in range(n):
            slot = outs[a].at[_dev_index(x, y, c)]
            local.append(pltpu.make_async_copy(ins[a], slot, local_sems.at[a]))
            for r in range(1, NDEV):
                peer = (x ^ ((r >> 2) & 1), y ^ ((r >> 1) & 1), c ^ (r & 1))
                copies.append(pltpu.make_async_remote_copy(src_ref=ins[a], dst_ref=slot, send_sem=send_sems.at[a, r - 1],
                                                           recv_sem=recv_sems.at[a, r - 1], device_id=peer, device_id_type=MESH))
        return _start_wait_all(copies, local=local)

    return _Exchange(list(smalls), [jax.ShapeDtypeStruct((NDEV,) + s.shape, s.dtype) for s in smalls],
                     [pltpu.SemaphoreType.DMA((n, 7)), pltpu.SemaphoreType.DMA((n, 7)), pltpu.SemaphoreType.DMA((n,))], make)


_W_IN_ROWS = {"u": (0, 1024), "v": (1024, 2048), "za": (2048, 3072), "zb": (3072, 5120), "xbc": (5120, 8192),
              "dt": (8192, 8224), "ga": (8224, 9248), "gb": (9248, 10272)}
_PROJ_ORDER = ("xbc", "zb", "u", "v", "za", "ga", "gb", "dt")


def _w_in_t_rows(wt):
    z = jnp.zeros((NP - NIN, wt.shape[1]), wt.dtype)
    return jnp.concatenate([wt[slice(*_W_IN_ROWS[n])] for n in _PROJ_ORDER] + [z], axis=0)


_WEIGHTS = ["norm_g", "w_in", "ln_a_g", "ln_a_b", "w_s", "b_s", "conv_w", "conv_b", "dt_bias", "a_log", "d_skip", "ssm_norm_g",
            "w_oa", "w_ob", "w_out", "ple_norm_g", "w_pg", "w_ple", "final_g"]


def _rows_pack(d, dtype):
    return jnp.concatenate([d["w_oa"].reshape(128, D), d["w_ob"].reshape(256, D), d["w_out"].reshape(128, D),
                            d["w_pg"].reshape(128, D), d["w_ple"].reshape(32, D)], axis=0).astype(dtype)


def kernel(x, p, norm_g, w_in, ln_a_g, ln_a_b, w_s, b_s, conv_w, conv_b, dt_bias, a_log, d_skip, ssm_norm_g, w_oa, w_ob, w_out, ple_norm_g, w_pg, w_ple, final_g, loss_target, m_norm_g, m_w_in, m_ln_a_g, m_ln_a_b, m_w_s, m_b_s, m_conv_w, m_conv_b, m_dt_bias, m_a_log, m_d_skip, m_ssm_norm_g, m_w_oa, m_w_ob, m_w_out, m_ple_norm_g, m_w_pg, m_w_ple, m_final_g, v_norm_g, v_w_in, v_ln_a_g, v_ln_a_b, v_w_s, v_b_s, v_conv_w, v_conv_b, v_dt_bias, v_a_log, v_d_skip, v_ssm_norm_g, v_w_oa, v_w_ob, v_w_out, v_ple_norm_g, v_w_pg, v_w_ple, v_final_g):
    args = dict(locals())
    w = {n: args[n] for n in _WEIGHTS}
    m = {n: args["m_" + n] for n in _WEIGHTS}
    v = {n: args["v_" + n] for n in _WEIGHTS}
    T = x.shape[1]
    xi, yi, ci = lax.axis_index("x"), lax.axis_index("y"), lax.axis_index("c")
    me = 4 * xi + 2 * yi + ci
    x2, p2, tgt = x.reshape(T, D), p.reshape(T, PLE), loss_target.reshape(T, D)

    norm_g2 = w["norm_g"].reshape(1, D)
    ws = jnp.where(jnp.tril(jnp.ones((CH, CH), bool))[None], w["w_s"].reshape(NG, CH, CH), 0.0).astype(BF16)
    wst = jnp.transpose(ws, (0, 2, 1))
    bst = jnp.broadcast_to(w["b_s"].reshape(NG, CH, 1), (NG, CH, 256))
    ln_g, ln_b = w["ln_a_g"].reshape(1, D), w["ln_a_b"].reshape(1, D)
    cb = w["conv_b"].reshape(1, CD)
    pad32 = lambda a: jnp.pad(a.reshape(1, NH), ((0, 0), (0, DTW - NH)))
    dtb, alog = pad32(w["dt_bias"]), pad32(w["a_log"])
    dskx = jnp.repeat(w["d_skip"].reshape(NH), HD).reshape(1, DI)
    sg = w["ssm_norm_g"].reshape(1, DI)
    ple_g, fin_g = w["ple_norm_g"].reshape(1, D), w["final_g"].reshape(1, D)
    e128 = (jnp.arange(DTW)[:, None] == (jnp.arange(DI)[None, :] // HD)).astype(BF16)
    et128 = e128.T
    gsel = ((jnp.arange(D)[:, None] // 256) == jnp.arange(128)[None, :]).astype(BF16)

    w_in_t = lambda a: jnp.transpose(a.reshape(D, WSH))
    (a_all,) = _run_exchange(_relay_gather_exchange([w_in_t(w["w_in"]).astype(BF16)]), "all_gather_w_in")
    w_in_full_t = a_all.reshape(NIN, D)
    wp = _w_in_t_rows(w_in_full_t)
    (proj, dtr, h), (r_all, cw_all) = _proj_fwd(x2, norm_g2, wp, _gather_exchange([_rows_pack(w, BF16), w["conv_w"].reshape(4, CD // NDEV)]))
    f_oa = r_all[:, R_OA:R_OB].reshape(D, D)
    f_ob = r_all[:, R_OB:R_OUT].reshape(DI, D)
    f_out = r_all[:, R_OUT:R_PG].reshape(D, D)
    f_pg = r_all[:, R_PG:R_PLE].reshape(D, D)
    f_ple = jnp.transpose(r_all[:, R_PLE:R_ROWS].reshape(NDEV, PLE, 128), (1, 0, 2)).reshape(PLE, D)
    cw = jnp.transpose(cw_all, (1, 0, 2)).reshape(4, CD)

    y, yb, hprev, pre_all = _ssd_fwd(proj, dtr, cw, cb, dtb, alog, dskx, sg, e128)
    dx1, dx1b, ya, mg, hp, dpre, dpe, doa, dob, dya, dyb, dgab, acc = _merge(
        x2, yb, proj, p2, tgt, (ln_g, ln_b, ws, bst), f_oa, f_ob, f_out, f_pg, f_ple, ple_g, fin_g)

    gple = jnp.transpose(_wgrad(p2, dpe, "wgrad_ple").reshape(PLE, NDEV, 128), (1, 0, 2)).reshape(NDEV, 32, D)
    gr = jnp.concatenate([_wgrad(ya, doa, "wgrad_oa").reshape(NDEV, 128, D), _wgrad(yb, dob, "wgrad_ob").reshape(NDEV, 256, D),
                          _wgrad(mg, dx1b, "wgrad_out").reshape(NDEV, 128, D), _wgrad(hp, dpre, "wgrad_pg").reshape(NDEV, 128, D),
                          gple], axis=1)
    (duvz, dws, dbs, dln), _ = _gmlp_bwd(proj, dya, ln_g, ln_b, ws, wst, bst, gsel, _no_exchange())
    (dxz, ddt, dcw, dsm, dsg), (rr,) = _ssd_bwd(proj, pre_all, dtr, y, dyb, hprev, cw, dtb, alog, dskx, sg, e128, et128, _direct_exchange([gr]))

    g_xz = _wgrad(dxz, h, "wgrad_xbc_zb")
    g_w_in_t = jnp.concatenate([_wgrad(duvz, h, "wgrad_u_v_za"), g_xz[CD:], g_xz[:CD], _wgrad(ddt, h, "wgrad_dt")[:NH],
                                _wgrad(dgab, h, "wgrad_ga_gb")], axis=0)
    ga = g_w_in_t.reshape(NDEV, WSH, D)
    wide = jnp.concatenate([dln, acc, dsg.reshape(16, D), dcw.reshape(24, D)], axis=0)
    narrow = jnp.concatenate([dws.reshape(NG * CH, CH), jnp.pad(dbs[:, :NG].T, ((0, 8 - NG), (0, 0))), dsm], axis=0)
    (gx, dng), (ra, wide_all, narrow_all) = _proj_bwd(x2, dx1, norm_g2, wp, [dxz, duvz, dgab], ddt,
                                                      _combine(_direct_exchange([ga]), _direct_gather_exchange([wide, narrow])))
    ga_own = lax.dynamic_index_in_dim(ga, me, 0, keepdims=False)
    gr_own = lax.dynamic_index_in_dim(gr, me, 0, keepdims=False)

    out_g, out_d, out_m, out_v = {}, {}, {}, {}
    outs = (out_g, out_d, out_m, out_v)
    tiles = lambda a: jnp.transpose(a.reshape(8, 128, WSH), (2, 0, 1))
    g2, (norm_all,) = _sum_parts([ga_own] + [(ra, k) for k in range(NDEV - 1)], "sum_w_in", _direct_gather_exchange([dng]))
    res = _adamw_tiles(tiles(w["w_in"]), tiles(m["w_in"]), tiles(v["w_in"]), g2.reshape(WSH, 8, 128), "adamw_w_in")
    for dst, val in zip(outs, res):
        dst["w_in"] = jnp.transpose(val, (1, 2, 0)).reshape(1, D, WSH)
    parts_r = [gr_own] + [(rr, k) for k in range(NDEV - 1)]
    for name, row, rows in (("w_oa", R_OA, 128), ("w_ob", R_OB, 256), ("w_out", R_OUT, 128), ("w_pg", R_PG, 128)):
        res = _adamw(w[name].reshape(rows, D), m[name].reshape(rows, D), v[name].reshape(rows, D), parts_r, "adamw_" + name,
                     part_row=row, block_rows=128)
        for dst, val in zip(outs, res):
            dst[name] = val.reshape(1, rows, D)
    res = _adamw(w["w_ple"].reshape(32, D), m["w_ple"].reshape(32, D), v["w_ple"].reshape(32, D), parts_r, "adamw_w_ple",
                 part_row=R_PLE, block_rows=32)
    for dst, val in zip(outs, res):
        dst["w_ple"] = val.reshape(1, PLE, 128)
    two_d = lambda d: {n: d[n].reshape(_SMALL_SHAPES[n]) for n in _SMALL_SHAPES}
    *res, g_cw_wide, loss = _adamw_small(two_d(w), two_d(m), two_d(v), wide_all, narrow_all, norm_all)
    for dst, val in zip(outs, res):
        dst.update({n: val[n].reshape(w[n].shape) for n in _SMALL_SHAPES})
    g_cw = lax.dynamic_slice_in_dim(g_cw_wide.reshape(4, CD), me * (CD // NDEV), CD // NDEV, axis=1).reshape(12, 128)
    res = _adamw(w["conv_w"].reshape(12, 128), m["conv_w"].reshape(12, 128), v["conv_w"].reshape(12, 128), [g_cw], "adamw_conv_w")
    for dst, val in zip((out_g, out_d, out_m, out_v), res):
        dst["conv_w"] = val.reshape(1, 4, CD // NDEV)

    return (loss, gx.reshape(1, T, D), *[out_g[n] for n in _WEIGHTS], *[out_d[n] for n in _WEIGHTS],
            *[out_m[n] for n in _WEIGHTS], *[out_v[n] for n in _WEIGHTS])
```

```python
import functools
import math
from typing import Callable, NamedTuple

import jax
import jax.numpy as jnp
from jax import lax
from jax.experimental import pallas as pl
from jax.experimental.pallas import tpu as pltpu

F32 = jnp.float32
BF16 = jnp.bfloat16
MESH = pl.DeviceIdType.MESH

D = 1024
DI = 2048
CD = 3072
NH = 32
HD = 64
NST = 128
NG = 4
CH = 128
PLE = 256
NIN = 10272
NDEV = 8
WSH = NIN // NDEV
EPS = 1e-6
OFF_XBC, OFF_ZB, OFF_U, OFF_V, OFF_ZA, OFF_GA, OFF_GB, OFF_DT = 0, 3072, 5120, 6144, 7168, 8192, 9216, 10240
NP = 10368
DTW = 128
R_OA, R_OB, R_OUT, R_PG, R_PLE, R_ROWS = 0, 128, 384, 512, 640, 672

ADAM_LR, ADAM_B1, ADAM_B2, ADAM_EPS, ADAM_WD, ADAM_STEP = 0.001, 0.9, 0.999, 1e-08, 0.01, 10

V7X_VMEM_LIMIT = 56 * 1024 * 1024


def _cp(sem=None):
    return pltpu.CompilerParams(dimension_semantics=sem, vmem_limit_bytes=V7X_VMEM_LIMIT)


def _dot(a, b, prec=None):
    return jnp.dot(a, b, preferred_element_type=F32, precision=prec)


def _dot_nt(a, b, prec=None):
    return lax.dot_general(a, b, (((1,), (1,)), ((), ())), preferred_element_type=F32, precision=prec)


def _dot_tn(a, b, prec=None):
    return lax.dot_general(a, b, (((0,), (0,)), ((), ())), preferred_element_type=F32, precision=prec)


def _sigmoid(x):
    return 0.5 * jnp.tanh(0.5 * x) + 0.5


def _gelu_and_grad(x):
    c = math.sqrt(2.0 / math.pi)
    x2 = x * x
    t = jnp.tanh(c * (x + 0.044715 * x * x2))
    g = 0.5 * x * (1.0 + t)
    dg = 0.5 * (1.0 + t) + 0.5 * x * (1.0 - t * t) * c * (1.0 + 3.0 * 0.044715 * x2)
    return g, dg


def _gelu(x):
    c = math.sqrt(2.0 / math.pi)
    return 0.5 * x * (1.0 + jnp.tanh(c * (x + 0.044715 * x * x * x)))


def _softplus(x):
    return jnp.maximum(x, 0.0) + jnp.log(1.0 + jnp.exp(-jnp.abs(x)))


def _full(shape):
    n = len(shape)
    return pl.BlockSpec(shape, lambda *_: (0,) * n)


_ANY = pl.BlockSpec(memory_space=pl.ANY)


class _Exchange(NamedTuple):
    arrays: list
    out_shape: list
    sems: list
    make: Callable


def _call(body, ex, *, name, grid, in_specs, out_specs, out_shape, scratch_shapes, args):
    ki, ko, ks = len(in_specs), len(out_specs), len(scratch_shapes)
    ei, eo = len(ex.arrays), len(ex.out_shape)
    last = [g - 1 for g in grid]

    def full_body(*refs):
        r = list(refs)
        ins, eins, r = r[:ki], r[ki:ki + ei], r[ki + ei:]
        outs, eouts, r = r[:ko], r[ko:ko + eo], r[ko + eo:]
        scr, esems = r[:ks], r[ks:]
        start, finish = ex.make(eins, eouts, esems)
        ids = [pl.program_id(a) for a in range(len(grid))]
        is_first = functools.reduce(lambda p, q: p & q, [i == 0 for i in ids])
        is_last = functools.reduce(lambda p, q: p & q, [i == l for i, l in zip(ids, last)])
        pl.when(is_first)(start)
        body(*ins, *outs, *scr)
        pl.when(is_last)(finish)

    res = pl.pallas_call(
        full_body, name=name, grid=grid, in_specs=list(in_specs) + [_ANY] * ei, out_specs=list(out_specs) + [_ANY] * eo,
        out_shape=list(out_shape) + list(ex.out_shape), scratch_shapes=list(scratch_shapes) + list(ex.sems),
        compiler_params=_cp(("arbitrary",) * len(grid)),
    )(*args, *ex.arrays)
    return res[:ko], res[ko:]


def _run_exchange(ex, name):
    ni, no = len(ex.arrays), len(ex.out_shape)

    def body(*refs):
        start, finish = ex.make(refs[:ni], refs[ni:ni + no], refs[ni + no:])
        start()
        finish()

    return pl.pallas_call(body, name=name, in_specs=[_ANY] * ni, out_specs=[_ANY] * no, out_shape=list(ex.out_shape),
                          scratch_shapes=list(ex.sems))(*ex.arrays)


def _proj_fwd(x, norm_g, wp, ex):
    T = x.shape[0]
    tm, tn = min(T, 1024), 2048
    nj = OFF_DT // tn
    assert OFF_DT % tn == 0 and OFF_DT + DTW == NP

    def body(x_ref, g_ref, w_ref, wdt_ref, proj_ref, dt_ref, h_ref, hs_ref):
        j = pl.program_id(1)

        @pl.when(j == 0)
        def _():
            xf = x_ref[...]
            r = lax.rsqrt(jnp.mean(xf * xf, axis=-1, keepdims=True) + EPS)
            h = (xf * r * g_ref[...]).astype(BF16)
            hs_ref[...] = h
            h_ref[...] = h

        proj_ref[...] = _dot_nt(hs_ref[...], w_ref[...]).astype(BF16)

        @pl.when(j == nj - 1)
        def _():
            dt_ref[...] = _dot_nt(hs_ref[...], wdt_ref[...])

    return _call(
        body, ex, name="proj_fwd", grid=(T // tm, nj),
        in_specs=[pl.BlockSpec((tm, D), lambda i, j: (i, 0)), _full((1, D)), pl.BlockSpec((tn, D), lambda i, j: (j, 0)),
                  pl.BlockSpec((DTW, D), lambda i, j: (OFF_DT // DTW, 0))],
        out_specs=[pl.BlockSpec((tm, tn), lambda i, j: (i, j)), pl.BlockSpec((tm, DTW), lambda i, j: (i, 0)),
                   pl.BlockSpec((tm, D), lambda i, j: (i, 0))],
        out_shape=[jax.ShapeDtypeStruct((T, OFF_DT), BF16), jax.ShapeDtypeStruct((T, DTW), F32), jax.ShapeDtypeStruct((T, D), BF16)],
        scratch_shapes=[pltpu.VMEM((tm, D), BF16)], args=(x, norm_g, wp, wp))


def _gmlp_tile():
    return 512


def _gmlp_fwd_tile(u_ref, v_ref, z_ref, lg_ref, lb_ref, ws_ref, bs_ref, ya_ref, vn_s):
    tm = u_ref.shape[0]
    vg = _gelu(v_ref[...].astype(F32))
    mu = jnp.mean(vg, axis=-1, keepdims=True)
    xc = vg - mu
    rstd = lax.rsqrt(jnp.mean(xc * xc, axis=-1, keepdims=True) + EPS)
    vn_s[...] = (xc * rstd * lg_ref[...] + lb_ref[...]).astype(BF16)
    for c in range(tm // CH):
        rs = slice(c * CH, (c + 1) * CH)
        for g in range(NG):
            cs_ = slice(g * 256, (g + 1) * 256)
            sv = _dot(ws_ref[g], vn_s[rs, cs_]) + bs_ref[g]
            z = z_ref[rs, cs_].astype(F32)
            ya_ref[rs, cs_] = (_gelu(u_ref[rs, cs_].astype(F32)) * sv * (z * _sigmoid(z))).astype(BF16)


def _gmlp_bwd(proj, dya, ln_g, ln_b, ws, wst, bst, gsel, ex):
    T = proj.shape[0]
    tm = min(T, _gmlp_tile())

    def body(u_ref, v_ref, z_ref, dy_ref, lg_ref, lb_ref, ws_ref, wst_ref, bs_ref, gsel_ref,
             d_ref, dws_ref, dbs_ref, dln_ref, vn_s, dsv_s, dvn_s):
        du_ref, dv_ref, dz_ref = d_ref.at[:, 0:D], d_ref.at[:, D:2 * D], d_ref.at[:, 2 * D:3 * D]
        @pl.when(pl.program_id(0) == 0)
        def _():
            dws_ref[...] = jnp.zeros_like(dws_ref)
            dbs_ref[...] = jnp.zeros_like(dbs_ref)
            dln_ref[...] = jnp.zeros_like(dln_ref)

        vg, dvg_dv = _gelu_and_grad(v_ref[...].astype(F32))
        mu = jnp.mean(vg, axis=-1, keepdims=True)
        xc = vg - mu
        rstd = lax.rsqrt(jnp.mean(xc * xc, axis=-1, keepdims=True) + EPS)
        vhat = xc * rstd
        vn_s[...] = (vhat * lg_ref[...] + lb_ref[...]).astype(BF16)
        ri = lax.broadcasted_iota(jnp.int32, (CH, CH), 0)
        ci = lax.broadcasted_iota(jnp.int32, (CH, CH), 1)
        tril = (ri >= ci).astype(F32)
        for c in range(tm // CH):
            rs = slice(c * CH, (c + 1) * CH)
            for g in range(NG):
                cs_ = slice(g * 256, (g + 1) * 256)
                vn = vn_s[rs, cs_]
                sv = _dot(ws_ref[g], vn) + bs_ref[g]
                z = z_ref[rs, cs_].astype(F32)
                sz = _sigmoid(z)
                ug, dug_du = _gelu_and_grad(u_ref[rs, cs_].astype(F32))
                dy = dy_ref[rs, cs_].astype(F32)
                zs = z * sz
                t = dy * zs
                dsv_f = dy * sv
                du_ref[rs, cs_] = (zs * dsv_f * dug_du).astype(BF16)
                dz_ref[rs, cs_] = (dsv_f * ug * (sz + zs * (1.0 - sz))).astype(BF16)
                dsv = (t * ug).astype(BF16)
                dsv_s[rs, cs_] = dsv
                dvn_s[rs, cs_] = _dot(wst_ref[g], dsv)
                dws_ref[g] += _dot_nt(dsv, vn) * tril
            dbs_ref[...] += _dot(dsv_s[rs, :], gsel_ref[...])
        dvn = dvn_s[...]
        dln_ref[0:1, :] += jnp.sum(dvn * vhat, axis=0, keepdims=True)
        dln_ref[1:2, :] += jnp.sum(dvn, axis=0, keepdims=True)
        dvh = dvn * lg_ref[...]
        dvg = rstd * (dvh - jnp.mean(dvh, axis=-1, keepdims=True) - vhat * jnp.mean(dvh * vhat, axis=-1, keepdims=True))
        dv_ref[...] = (dvg * dvg_dv).astype(BF16)

    blk = lambda off: pl.BlockSpec((tm, D), lambda i: (i, off // D))
    row = pl.BlockSpec((tm, D), lambda i: (i, 0))
    return _call(
        body, ex, name="gmlp_bwd", grid=(T // tm,),
        in_specs=[blk(OFF_U), blk(OFF_V), blk(OFF_ZA), row, _full((1, D)), _full((1, D)), _full((NG, CH, CH)),
                  _full((NG, CH, CH)), _full((NG, CH, 256)), _full((D, 128))],
        out_specs=[pl.BlockSpec((tm, 3 * D), lambda i: (i, 0)), _full((NG, CH, CH)), _full((CH, 128)), _full((8, D))],
        out_shape=[jax.ShapeDtypeStruct((T, 3 * D), BF16),
                   jax.ShapeDtypeStruct((NG, CH, CH), F32), jax.ShapeDtypeStruct((CH, 128), F32), jax.ShapeDtypeStruct((8, D), F32)],
        scratch_shapes=[pltpu.VMEM((tm, D), BF16), pltpu.VMEM((tm, D), BF16), pltpu.VMEM((tm, D), F32)],
        args=(proj, proj, proj, dya, ln_g, ln_b, ws, wst, bst, gsel))


def _zb_cols(zb0_ref, zb1_ref, cols):
    ref = zb0_ref if cols.start < D else zb1_ref
    return ref[:, cols.start % D:cols.start % D + (cols.stop - cols.start)]


def _shift_matrix(down):
    t = jnp.arange(CH)[None, :, None]
    j = jnp.arange(1, 4)[:, None, None]
    col = jnp.arange(2 * CH)[None, None, :]
    src = CH + t - j if down else t + j
    return (col == src).astype(BF16).reshape(3 * CH, 2 * CH)


def _conv_pre(x, moved, cw_ref, cb_ref):
    pre = cb_ref[...] + cw_ref[3:4, :] * x
    for j in (1, 2, 3):
        pre = pre + cw_ref[3 - j:4 - j, :] * moved[(j - 1) * CH:j * CH]
    return pre


def _split_dot(x, w, parts, w_left=False):
    acc, r = None, x
    for k in range(parts):
        hi = r.astype(BF16)
        d = _dot(w, hi) if w_left else _dot(hi, w)
        acc = d if acc is None else acc + d
        if k + 1 < parts:
            r = r - hi.astype(F32)
    return acc


def _chunk_decays(dt, alog_ref, e_ref, cs_s, cst_s, csx_s):
    a = -jnp.exp(alog_ref[...])
    ri = lax.broadcasted_iota(jnp.int32, (CH, CH), 0)
    ci = lax.broadcasted_iota(jnp.int32, (CH, CH), 1)
    tril = ri >= ci
    cs = _split_dot(dt * a, tril.astype(BF16), 3, w_left=True)
    cs_s[...] = cs
    cst_s[...] = cs.T
    csx_s[...] = _split_dot(cs, e_ref[...], 3)
    return a, tril, ri, ci


def _lmat(cst_s, h, tril):
    rowb = jnp.broadcast_to(cst_s[h:h + 1, :], (CH, CH))
    return jnp.exp(jnp.where(tril, rowb.T - rowb, -jnp.inf))


def _head_pair_rows(v, lane):
    return jnp.concatenate([jnp.where(lane < HD, v, 0.0), jnp.where(lane < HD, 0.0, v)], axis=0).astype(BF16)


def _ssd_fwd(proj, dtr, cw, cb, dtb, alog, dskx, sg, e128):
    T = proj.shape[0]
    nc = T // CH

    def body(xbc_ref, zb0_ref, zb1_ref, dt_ref, cw_ref, cb_ref, dtb_ref, alog_ref, dx_ref, sg_ref, e_ref, shift_ref,
             y_ref, yb_ref, hp_ref, pre_ref, xx_s, h_s, cs_s, cst_s, csx_s, yz_s):
        @pl.when(pl.program_id(0) == 0)
        def _():
            xx_s[...] = jnp.zeros_like(xx_s)
            h_s[...] = jnp.zeros_like(h_s)

        xx_s[CH:, :] = xbc_ref[...]
        moved = _dot(shift_ref[...], xx_s[...])
        xx_s[CH - 16:CH, :] = xbc_ref[CH - 16:, :]
        x = xbc_ref[...].astype(F32)
        pre = _conv_pre(x, moved, cw_ref, cb_ref)
        pre_ref[...] = pre
        xc = pre * _sigmoid(pre)
        dt = _softplus(dt_ref[...] + dtb_ref[...])
        a, tril, _, lane = _chunk_decays(dt, alog_ref, e_ref, cs_s, cst_s, csx_s)
        dt_x = _split_dot(dt, e_ref[...], 2)
        cs_last_x = csx_s[CH - 1:CH, :]
        hp_ref[0] = h_s[...]
        for g in range(NG):
            gs = slice(g * 512, (g + 1) * 512)
            bg = xc[:, DI + g * NST:DI + (g + 1) * NST].astype(BF16)
            cg = xc[:, DI + 512 + g * NST:DI + 512 + (g + 1) * NST].astype(BF16)
            cbm = _dot_nt(cg, bg)
            xg = xc[:, gs]
            xdt = xg * dt_x[:, gs]
            hprev = h_s[:, gs]
            csx = csx_s[:, gs]
            yoff = _dot(cg, hprev.astype(BF16)) * jnp.exp(csx)
            st = _dot_tn(bg, (xdt * jnp.exp(cs_last_x[:, gs] - csx)).astype(BF16))
            h_s[:, gs] = jnp.exp(cs_last_x[:, gs]) * hprev + st
            ssq = jnp.zeros((CH, 1), F32)
            for q in range(4):
                h0 = g * 8 + 2 * q
                ps = slice(q * 128, (q + 1) * 128)
                cols = slice(g * 512 + q * 128, g * 512 + (q + 1) * 128)
                m01 = jnp.concatenate([cbm * _lmat(cst_s, h0, tril), cbm * _lmat(cst_s, h0 + 1, tril)], axis=1).astype(BF16)
                yq = _dot(m01, _head_pair_rows(xdt[:, ps], lane)) + yoff[:, ps] + xg[:, ps] * dx_ref[:, cols]
                y_ref[:, cols] = yq
                z = _zb_cols(zb0_ref, zb1_ref, cols).astype(F32)
                yz = yq * z * _sigmoid(z)
                yz_s[:, cols] = yz
                ssq = ssq + jnp.sum(yz * yz, axis=1, keepdims=True)
            rg = lax.rsqrt(ssq * (1.0 / 512.0) + EPS)
            yb_ref[:, gs] = (yz_s[:, gs] * rg * sg_ref[:, gs]).astype(BF16)

    return pl.pallas_call(
        body, name="ssd_fwd", grid=(nc,),
        in_specs=[pl.BlockSpec((CH, CD), lambda c: (c, OFF_XBC // CD)), pl.BlockSpec((CH, D), lambda c: (c, OFF_ZB // D)),
                  pl.BlockSpec((CH, D), lambda c: (c, OFF_ZB // D + 1)),
                  pl.BlockSpec((CH, DTW), lambda c: (c, 0)), _full((4, CD)), _full((1, CD)), _full((1, DTW)),
                  _full((1, DTW)), _full((1, DI)), _full((1, DI)), _full((DTW, DI)), _full((3 * CH, 2 * CH))],
        out_specs=[pl.BlockSpec((CH, DI), lambda c: (c, 0)), pl.BlockSpec((CH, DI), lambda c: (c, 0)),
                   pl.BlockSpec((1, NST, DI), lambda c: (c, 0, 0)), pl.BlockSpec((CH, CD), lambda c: (c, 0))],
        out_shape=[jax.ShapeDtypeStruct((T, DI), F32), jax.ShapeDtypeStruct((T, DI), BF16),
                   jax.ShapeDtypeStruct((nc, NST, DI), F32), jax.ShapeDtypeStruct((T, CD), F32)],
        scratch_shapes=[pltpu.VMEM((2 * CH, CD), BF16), pltpu.VMEM((NST, DI), F32), pltpu.VMEM((CH, CH), F32),
                        pltpu.VMEM((CH, CH), F32), pltpu.VMEM((CH, DI), F32), pltpu.VMEM((CH, DI), F32)],
        compiler_params=_cp(("arbitrary",)),
    )(proj, proj, proj, dtr, cw, cb, dtb, alog, dskx, sg, e128, _shift_matrix(down=True))


def _ssd_bwd(proj, pre_all, dtr, y, dyb, hprev_all, cw, dtb, alog, dskx, sg, e128, et128, ex):
    T = proj.shape[0]
    nc = T // CH

    def body(xbc_ref, pre_ref, zb0_ref, zb1_ref, dt_ref, y_ref, dyb_ref, hp_ref, cw_ref, dtb_ref, alog_ref, dx_ref, sg_ref,
             e_ref, et_ref, shift_ref, d_ref, ddt_ref, dcw_ref, dsm_ref, dsg_ref,
             g_s, dd_s, cs_s, cst_s, csx_s, dy_s, dxdt_s, dxs_s, dsd_s, dxc_s, gh_s):
        dxbc_ref, dzb_ref = d_ref.at[:, 0:CD], d_ref.at[:, CD:CD + DI]
        i = pl.program_id(0)

        @pl.when(i == 0)
        def _():
            g_s[...] = jnp.zeros_like(g_s)
            dd_s[...] = jnp.zeros_like(dd_s)
            dcw_ref[...] = jnp.zeros_like(dcw_ref)
            dsm_ref[...] = jnp.zeros_like(dsm_ref)
            dsg_ref[...] = jnp.zeros_like(dsg_ref)

        pre = pre_ref[...]
        sp = _sigmoid(pre)
        xc = pre * sp
        dtr = dt_ref[...] + dtb_ref[...]
        dt = _softplus(dtr)
        a, tril, ri, lane = _chunk_decays(dt, alog_ref, e_ref, cs_s, cst_s, csx_s)
        et = et_ref[...]
        dt_x = _split_dot(dt, e_ref[...], 2)
        cs_last_x = csx_s[CH - 1:CH, :]

        for g in range(NG):
            gs = slice(g * 512, (g + 1) * 512)
            z = _zb_cols(zb0_ref, zb1_ref, gs).astype(F32)
            sz = _sigmoid(z)
            yv = y_ref[:, gs]
            zs = z * sz
            yz = yv * zs
            rg = lax.rsqrt(jnp.mean(yz * yz, axis=-1, keepdims=True) + EPS)
            yn = yz * rg
            dyb = dyb_ref[:, gs].astype(F32)
            dsg_ref[0:1, gs] += jnp.sum(dyb * yn, axis=0, keepdims=True)
            dyn = dyb * sg_ref[:, gs]
            dyz = rg * (dyn - yn * jnp.mean(dyn * yn, axis=-1, keepdims=True))
            dy_s[:, gs] = dyz * zs
            dzb_ref[:, gs] = (dyz * yv * (sz + zs * (1.0 - sz))).astype(BF16)

        rsum = jnp.zeros((CH, DTW), F32)
        csum_t = jnp.zeros((DTW, CH), F32)
        for g in range(NG):
            gs = slice(g * 512, (g + 1) * 512)
            bg = xc[:, DI + g * NST:DI + (g + 1) * NST].astype(BF16)
            cg = xc[:, DI + 512 + g * NST:DI + 512 + (g + 1) * NST].astype(BF16)
            cbm = _dot_nt(cg, bg)
            xdt = xc[:, gs] * dt_x[:, gs]
            hprev = hp_ref[0, :, gs]
            hpb = hprev.astype(BF16)
            gn = g_s[:, gs]
            gnb = gn.astype(BF16)
            dy = dy_s[:, gs]
            csx = csx_s[:, gs]
            ecs = jnp.exp(csx)
            dec = jnp.exp(cs_last_x[:, gs] - csx)
            dye = (dy * ecs).astype(BF16)
            dc = _dot_nt(dye, hpb)
            dprev = _dot_tn(cg, dye)
            dxdt_state = dec * _dot(bg, gnb)
            db = _dot_nt((xdt * dec).astype(BF16), gnb)
            dcb = jnp.zeros((CH, CH), F32)
            for q in range(4):
                h0 = g * 8 + 2 * q
                ps = slice(q * 128, (q + 1) * 128)
                dyp = dy[:, ps]
                l0 = _lmat(cst_s, h0, tril)
                l1 = _lmat(cst_s, h0 + 1, tril)
                m0 = cbm * l0
                m1 = cbm * l1
                dm = _dot_nt(dyp.astype(BF16), _head_pair_rows(xdt[:, ps], lane))
                dm0 = dm[:, :CH]
                dm1 = dm[:, CH:]
                dcb = dcb + dm0 * l0 + dm1 * l1
                for hh, qm in ((h0, dm0 * m0), (h0 + 1, dm1 * m1)):
                    rsum = jnp.where(lane == hh, jnp.sum(qm, axis=1, keepdims=True), rsum)
                    csum_t = jnp.where(ri == hh, jnp.sum(qm, axis=0, keepdims=True), csum_t)
                mst = jnp.concatenate([m0, m1], axis=0).astype(BF16)
                d = _dot_tn(mst, _head_pair_rows(dyp, lane))
                dxdt_s[:, g * 512 + q * 128:g * 512 + (q + 1) * 128] = d + dxdt_state[:, ps]
            yoff = _dot(cg, hpb) * ecs
            dsd_s[:, gs] = xdt * dxdt_state
            dxs_s[:, gs] = dy * yoff
            dcbb = dcb.astype(BF16)
            dxc_s[:, DI + 512 + g * NST:DI + 512 + (g + 1) * NST] = dc + _dot(dcbb, bg)
            dxc_s[:, DI + g * NST:DI + (g + 1) * NST] = db + _dot_tn(dcbb, cg)
            gh_s[:, gs] = jnp.broadcast_to(jnp.sum(gn * hprev, axis=0, keepdims=True), (8, 512))
            g_s[:, gs] = dprev + jnp.exp(cs_last_x[:, gs]) * gn

        xs = xc[:, :DI]
        dy = dy_s[...]
        dxdt = dxdt_s[...]
        cs_last = cs_s[CH - 1:CH, :]
        state_e = _split_dot(dsd_s[...], et, 2)
        dcd = 0.125 * jnp.sum(_split_dot(gh_s[...], et, 2), axis=0, keepdims=True) * jnp.exp(cs_last)
        row = lax.broadcasted_iota(jnp.int32, (CH, 1), 0)
        dcs = rsum - csum_t.T + _split_dot(dxs_s[...], et, 2) - state_e
        dcs = dcs + jnp.where(row == CH - 1, jnp.sum(state_e, axis=0, keepdims=True) + dcd, 0.0)
        dda = _split_dot(dcs, (lane >= ri).astype(BF16), 3, w_left=True)
        ddt = dda * a + _dot((dxdt * xs).astype(BF16), et)
        ddtr = jnp.where(lane < NH, ddt * _sigmoid(dtr), 0.0)
        ddt_ref[...] = ddtr.astype(BF16)
        dsm_ref[0:1, :] += jnp.sum(ddtr, axis=0, keepdims=True)
        dsm_ref[1:2, :] += jnp.sum(dda * dt, axis=0, keepdims=True) * a
        dsm_ref[2:3, :] += jnp.sum(_dot((dy * xs).astype(BF16), et), axis=0, keepdims=True)
        dxc_s[:, :DI] = dxdt * dt_x + dy * dx_ref[...]

        dpre = dxc_s[...] * (sp + xc * (1.0 - sp))
        dpre_b = dpre.astype(BF16)
        dd_s[:CH, :] = dpre_b
        moved = _dot(shift_ref[...], dd_s[...])
        dd_s[CH:CH + 16, :] = dpre_b[:16]
        x = xbc_ref[...].astype(F32)
        dcw_ref[4:5, :] += jnp.sum(dpre, axis=0, keepdims=True)
        dxbc = cw_ref[3:4, :] * dpre
        dcw_ref[3:4, :] += jnp.sum(dpre * x, axis=0, keepdims=True)
        for j in (1, 2, 3):
            ahead = moved[(j - 1) * CH:j * CH]
            dcw_ref[3 - j:4 - j, :] += jnp.sum(ahead * x, axis=0, keepdims=True)
            dxbc = dxbc + cw_ref[3 - j:4 - j, :] * ahead
        dxbc_ref[...] = dxbc.astype(BF16)

    rev = lambda c: nc - 1 - c
    return _call(
        body, ex, name="ssd_bwd", grid=(nc,),
        in_specs=[pl.BlockSpec((CH, CD), lambda c: (rev(c), OFF_XBC // CD)),
                  pl.BlockSpec((CH, CD), lambda c: (rev(c), 0)),
                  pl.BlockSpec((CH, D), lambda c: (rev(c), OFF_ZB // D)), pl.BlockSpec((CH, D), lambda c: (rev(c), OFF_ZB // D + 1)),
                  pl.BlockSpec((CH, DTW), lambda c: (rev(c), 0)),
                  pl.BlockSpec((CH, DI), lambda c: (rev(c), 0)), pl.BlockSpec((CH, DI), lambda c: (rev(c), 0)),
                  pl.BlockSpec((1, NST, DI), lambda c: (rev(c), 0, 0)),
                  _full((4, CD)), _full((1, DTW)), _full((1, DTW)), _full((1, DI)), _full((1, DI)),
                  _full((DTW, DI)), _full((DI, DTW)), _full((3 * CH, 2 * CH))],
        out_specs=[pl.BlockSpec((CH, CD + DI), lambda c: (rev(c), 0)),
                   pl.BlockSpec((CH, DTW), lambda c: (rev(c), 0)), _full((8, CD)), _full((8, DTW)), _full((8, DI))],
        out_shape=[jax.ShapeDtypeStruct((T, CD + DI), BF16), jax.ShapeDtypeStruct((T, DTW), BF16),
                   jax.ShapeDtypeStruct((8, CD), F32), jax.ShapeDtypeStruct((8, DTW), F32), jax.ShapeDtypeStruct((8, DI), F32)],
        scratch_shapes=[pltpu.VMEM((NST, DI), F32), pltpu.VMEM((2 * CH, CD), BF16), pltpu.VMEM((CH, CH), F32), pltpu.VMEM((CH, CH), F32),
                        pltpu.VMEM((CH, DI), F32), pltpu.VMEM((CH, DI), F32), pltpu.VMEM((CH, DI), F32), pltpu.VMEM((CH, DI), F32),
                        pltpu.VMEM((CH, DI), F32), pltpu.VMEM((CH, CD), F32), pltpu.VMEM((8, DI), F32)],
        args=(proj, pre_all, proj, proj, dtr, y, dyb, hprev_all, cw, dtb, alog, dskx, sg, e128, et128, _shift_matrix(down=False)))


def _merge_tile():
    return 256


def _merge(x, yb, proj, p, tgt, gmlp, w_oa, w_ob, w_out, w_pg, w_ple, ple_g, fin_g):
    T = x.shape[0]
    tm = min(T, _merge_tile())

    def body(x_ref, u_ref, v_ref, za_ref, yb_ref, ga_ref, gb_ref, p_ref, t_ref, lg_ref, lb_ref, ws_ref, bs_ref,
             woa, wob, wout, wpg, wple, pg_ref, fg_ref,
             dx1_ref, dx1b_ref, ya_ref, mg_ref, hp_ref, dpre_ref, dpe_ref, doa_ref, dob_ref, dya_ref, dyb_ref, dg_ref, acc_ref, vn_s):
        @pl.when(pl.program_id(0) == 0)
        def _():
            acc_ref[...] = jnp.zeros_like(acc_ref)

        _gmlp_fwd_tile(u_ref, v_ref, za_ref, lg_ref, lb_ref, ws_ref, bs_ref, ya_ref, vn_s)
        oa = _dot(ya_ref[...], woa[...])
        ob = _dot(yb_ref[...], wob[...])
        sa = _sigmoid(ga_ref[...].astype(F32))
        sb = _sigmoid(gb_ref[...].astype(F32))
        mg = sa * oa + sb * ob
        mgb = mg.astype(BF16)
        mg_ref[...] = mgb
        x1 = x_ref[...] + _dot(mgb, wout[...])
        r2 = lax.rsqrt(jnp.mean(x1 * x1, axis=-1, keepdims=True) + EPS)
        xh1 = x1 * r2
        hpb = (xh1 * pg_ref[...]).astype(BF16)
        hp_ref[...] = hpb
        gate = _sigmoid(_dot(hpb, wpg[...]))
        pe = _dot(p_ref[...].astype(BF16), wple[...])
        x2 = x1 + gate * pe
        r3 = lax.rsqrt(jnp.mean(x2 * x2, axis=-1, keepdims=True) + EPS)
        xh2 = x2 * r3
        err = xh2 * fg_ref[...] - t_ref[...]
        acc_ref[2:3, :] += 0.5 * jnp.sum(jnp.mean(err * err, axis=-1, keepdims=True))
        dyo = err * (1.0 / D)
        acc_ref[0:1, :] += jnp.sum(dyo * xh2, axis=0, keepdims=True)
        dn = dyo * fg_ref[...]
        dx2 = r3 * (dn - xh2 * jnp.mean(dn * xh2, axis=-1, keepdims=True))
        dpe_ref[...] = (dx2 * gate).astype(BF16)
        dpre = (dx2 * pe * gate * (1.0 - gate)).astype(BF16)
        dpre_ref[...] = dpre
        dhp = _dot_nt(dpre, wpg[...])
        acc_ref[1:2, :] += jnp.sum(dhp * xh1, axis=0, keepdims=True)
        dhn = dhp * pg_ref[...]
        dx1 = dx2 + r2 * (dhn - xh1 * jnp.mean(dhn * xh1, axis=-1, keepdims=True))
        dx1_ref[...] = dx1
        dx1b = dx1.astype(BF16)
        dx1b_ref[...] = dx1b
        dmg = _dot_nt(dx1b, wout[...])
        doa = (dmg * sa).astype(BF16)
        dob = (dmg * sb).astype(BF16)
        doa_ref[...] = doa
        dob_ref[...] = dob
        dg_ref[:, :D] = (dmg * oa * sa * (1.0 - sa)).astype(BF16)
        dg_ref[:, D:] = (dmg * ob * sb * (1.0 - sb)).astype(BF16)
        dya_ref[...] = _dot_nt(doa, woa[...]).astype(BF16)
        dyb_ref[...] = _dot_nt(dob, wob[...]).astype(BF16)

    row = lambda w: pl.BlockSpec((tm, w), lambda i: (i, 0))
    blk = lambda off: pl.BlockSpec((tm, D), lambda i: (i, off // D))
    wsp = lambda s: pl.BlockSpec(s, lambda i: (0, 0), pipeline_mode=pl.Buffered(1))
    return pl.pallas_call(
        body, name="merge", grid=(T // tm,),
        in_specs=[row(D), blk(OFF_U), blk(OFF_V), blk(OFF_ZA), row(DI), blk(OFF_GA), blk(OFF_GB), row(PLE), row(D),
                  _full((1, D)), _full((1, D)), _full((NG, CH, CH)), _full((NG, CH, 256)),
                  wsp((D, D)), wsp((DI, D)), wsp((D, D)), wsp((D, D)), wsp((PLE, D)), _full((1, D)), _full((1, D))],
        out_specs=[row(D)] * 10 + [row(DI), row(2 * D), _full((8, D))],
        out_shape=[jax.ShapeDtypeStruct((T, D), F32)] + [jax.ShapeDtypeStruct((T, D), BF16)] * 9
        + [jax.ShapeDtypeStruct((T, DI), BF16), jax.ShapeDtypeStruct((T, 2 * D), BF16), jax.ShapeDtypeStruct((8, D), F32)],
        scratch_shapes=[pltpu.VMEM((tm, D), BF16)],
        compiler_params=_cp(("arbitrary",)),
    )(x, proj, proj, proj, yb, proj, proj, p, tgt, *gmlp, w_oa, w_ob, w_out, w_pg, w_ple, ple_g, fin_g)


def _wgrad(a, b, name):
    T, K = a.shape
    N = b.shape[1]
    tt, tk, tn = min(T, 2048), min(K, 1024), min(N, 1024)
    nt = T // tt

    def body(a_ref, b_ref, o_ref, acc_s):
        t = pl.program_id(2)

        @pl.when(t == 0)
        def _():
            acc_s[...] = jnp.zeros_like(acc_s)

        acc_s[...] += _dot_tn(a_ref[...].astype(BF16), b_ref[...])

        @pl.when(t == nt - 1)
        def _():
            o_ref[...] = acc_s[...].astype(BF16)

    return pl.pallas_call(
        body, name=name, grid=(K // tk, N // tn, nt),
        in_specs=[pl.BlockSpec((tt, tk), lambda k, n, t: (t, k)), pl.BlockSpec((tt, tn), lambda k, n, t: (t, n))],
        out_specs=pl.BlockSpec((tk, tn), lambda k, n, t: (k, n)),
        out_shape=jax.ShapeDtypeStruct((K, N), BF16),
        scratch_shapes=[pltpu.VMEM((tk, tn), F32)],
        compiler_params=_cp(("parallel", "parallel", "arbitrary")),
    )(a, b)


def _proj_bwd(x, dx1, norm_g, wt, pieces, ddt, ex):
    T = x.shape[0]
    tm = min(T, 1024)
    nk = OFF_DT // D + 1
    starts = [sum(a.shape[1] for a in pieces[:n]) // D for n in range(len(pieces))]
    ranges = [(s, s + a.shape[1] // D) for s, a in zip(starts, pieces)]
    assert ranges[-1][1] == nk - 1
    npc = len(pieces)

    def body(x_hbm, dx1_hbm, g_ref, w_ref, wdt_ref, *rest):
        piece_refs, ddt_ref, gx_ref, dng_ref = rest[:npc], rest[npc], rest[npc + 1], rest[npc + 2]
        acc_s, x_ref, dx1_ref, row_sems = rest[npc + 3:]
        i, k = pl.program_id(0), pl.program_id(1)
        rows = pl.ds(pl.multiple_of(i * tm, tm), tm)
        fetches = [pltpu.make_async_copy(x_hbm.at[rows], x_ref, row_sems.at[0]),
                   pltpu.make_async_copy(dx1_hbm.at[rows], dx1_ref, row_sems.at[1])]

        @pl.when((i == 0) & (k == 0))
        def _():
            dng_ref[...] = jnp.zeros_like(dng_ref)

        @pl.when(k == 0)
        def _():
            acc_s[...] = jnp.zeros_like(acc_s)

        @pl.when(k == 1)
        def _():
            for cp in fetches:
                cp.start()

        for ref, (lo, hi) in zip(piece_refs, ranges):
            @pl.when((k >= lo) & (k < hi))
            def _(ref=ref):
                acc_s[...] += _dot(ref[...], w_ref[...])

        @pl.when(k == nk - 1)
        def _():
            for cp in fetches:
                cp.wait()
            dh = acc_s[...] + _dot(ddt_ref[...], wdt_ref[...])
            xf = x_ref[...]
            r = lax.rsqrt(jnp.mean(xf * xf, axis=-1, keepdims=True) + EPS)
            xh = xf * r
            dng_ref[0:1, :] += jnp.sum(dh * xh, axis=0, keepdims=True)
            dxn = dh * g_ref[...]
            gx_ref[...] = dx1_ref[...] + r * (dxn - xh * jnp.mean(dxn * xh, axis=-1, keepdims=True))

    def piece_spec(lo, hi):
        return pl.BlockSpec((tm, D), lambda i, k: (i, jnp.clip(k - lo, 0, hi - lo - 1)))

    row = pl.BlockSpec((tm, D), lambda i, k: (i, 0))
    return _call(
        body, ex, name="proj_bwd", grid=(T // tm, nk),
        in_specs=[_ANY, _ANY, _full((1, D)), pl.BlockSpec((D, D), lambda i, k: (jnp.minimum(k, nk - 2), 0)),
                  pl.BlockSpec((DTW, D), lambda i, k: (OFF_DT // DTW, 0))]
        + [piece_spec(lo, hi) for lo, hi in ranges] + [pl.BlockSpec((tm, DTW), lambda i, k: (i, 0))],
        out_specs=[row, _full((8, D))],
        out_shape=[jax.ShapeDtypeStruct((T, D), F32), jax.ShapeDtypeStruct((8, D), F32)],
        scratch_shapes=[pltpu.VMEM((tm, D), F32), pltpu.VMEM((tm, D), F32), pltpu.VMEM((tm, D), F32), pltpu.SemaphoreType.DMA((2,))],
        args=(x, dx1, norm_g, wt, wt, *pieces, ddt))


def _elementwise_tile(R, C, limit=1 << 20):
    if R * C * 4 <= limit:
        return R, C
    rows = [t for t in range(16, R, 16) if R % t == 0 and t * C * 4 <= limit]
    if rows:
        return rows[-1], C
    cols = [t for t in range(128, C, 128) if C % t == 0 and R * t * 4 <= limit]
    return R, cols[-1]


def _adam_update(w, m, v, g):
    c1 = 1.0 - ADAM_B1 ** ADAM_STEP
    c2 = 1.0 - ADAM_B2 ** ADAM_STEP
    mm = ADAM_B1 * m + (1.0 - ADAM_B1) * g
    vv = ADAM_B2 * v + (1.0 - ADAM_B2) * (g * g)
    return -ADAM_LR * ((mm / c1) / (jnp.sqrt(vv / c2) + ADAM_EPS) + ADAM_WD * w), mm, vv


def _sum_parts(parts, name, ex):
    first = parts[0][0] if isinstance(parts[0], tuple) else parts[0]
    R, C = first.shape[-2:]
    tr, tc = _elementwise_tile(R, C)
    flat = pl.BlockSpec((tr, tc), lambda i, j: (i, j))
    slot = lambda k: pl.BlockSpec((None, tr, tc), lambda i, j: (k, i, j))
    specs = [slot(p[1]) if isinstance(p, tuple) else flat for p in parts]
    arrays = [p[0] if isinstance(p, tuple) else p for p in parts]

    def body(*refs):
        g = refs[0][...].astype(F32)
        for r in refs[1:-1]:
            g = g + r[...].astype(F32)
        refs[-1][...] = g

    (out,), got = _call(body, ex, name=name, grid=(R // tr, C // tc), in_specs=specs, out_specs=[flat],
                        out_shape=[jax.ShapeDtypeStruct((R, C), F32)], scratch_shapes=[], args=arrays)
    return out, got


def _adamw_tiles(w, m, v, g, name):
    R = w.shape[0]
    tr = 107
    assert R % tr == 0
    spec = pl.BlockSpec((tr, 8, 128), lambda i: (i, 0, 0))

    def body(w_ref, m_ref, v_ref, g_in, g_ref, d_ref, nm_ref, nv_ref):
        g = g_in[...]
        g_ref[...] = g
        d_ref[...], nm_ref[...], nv_ref[...] = _adam_update(w_ref[...], m_ref[...], v_ref[...], g)

    return pl.pallas_call(body, name=name, grid=(R // tr,), in_specs=[spec] * 4, out_specs=[spec] * 4,
                          out_shape=[jax.ShapeDtypeStruct(w.shape, F32)] * 4, compiler_params=_cp(("parallel",)))(w, m, v, g)


def _adamw(w, m, v, parts, name, part_row=0, block_rows=None):
    R, C = w.shape
    tr, tc = _elementwise_tile(R, C) if block_rows is None else (block_rows, C)
    assert R % tr == 0 and part_row % tr == 0
    first = part_row // tr
    n = len(parts)
    wspec = pl.BlockSpec((tr, tc), lambda i, j: (i, j))
    flat = pl.BlockSpec((tr, tc), lambda i, j: (first + i, j))
    slot = lambda k: pl.BlockSpec((None, tr, tc), lambda i, j: (k, first + i, j))
    part_specs = [slot(p[1]) if isinstance(p, tuple) else flat for p in parts]
    part_arrays = [p[0] if isinstance(p, tuple) else p for p in parts]

    def body(*refs):
        w_ref, m_ref, v_ref = refs[:3]
        g_ref, d_ref, nm_ref, nv_ref = refs[3 + n:]
        g = refs[3][...].astype(F32)
        for r in refs[4:3 + n]:
            g = g + r[...].astype(F32)
        g_ref[...] = g
        d_ref[...], nm_ref[...], nv_ref[...] = _adam_update(w_ref[...], m_ref[...], v_ref[...], g)

    return pl.pallas_call(
        body, name=name, grid=(R // tr, C // tc), in_specs=[wspec] * 3 + part_specs, out_specs=[wspec] * 4,
        out_shape=[jax.ShapeDtypeStruct(w.shape, F32)] * 4, compiler_params=_cp(("parallel", "parallel")),
    )(w, m, v, *part_arrays)


_SMALL_WIDE = {"ln_a_g": 0, "ln_a_b": 1, "final_g": 8, "ple_norm_g": 9, "ssm_norm_g": 16, "conv_b": 44}
_WIDE_CONV_W, _WIDE_ROWS = 32, 56
_WIDE_LOSS = 10
_SMALL_NARROW = {"w_s": (0, NG * CH, 128), "b_s": (512, NG, 128), "dt_bias": (520, 1, NH), "a_log": (521, 1, NH),
                 "d_skip": (522, 1, NH)}
_NARROW_ROWS = 528
_SMALL_SHAPES = {"norm_g": (1, D), "ln_a_g": (1, D), "ln_a_b": (1, D), "ple_norm_g": (1, D), "final_g": (1, D),
                 "ssm_norm_g": (1, DI), "conv_b": (1, CD), "w_s": (NG * CH, CH), "b_s": (NG, CH), "dt_bias": (1, NH),
                 "a_log": (1, NH), "d_skip": (1, NH)}


def _adamw_small(w, m, v, wide_all, narrow_all, norm_all):
    names = list(_SMALL_SHAPES)
    n = len(names)

    def body(*refs):
        wr, mr, vr = refs[:n], refs[n:2 * n], refs[2 * n:3 * n]
        wide_ref, narrow_ref, norm_ref = refs[3 * n:3 * n + 3]
        outs = refs[3 * n + 3:]
        gr, dr, nmr, nvr, cw_ref, loss_ref = outs[:n], outs[n:2 * n], outs[2 * n:3 * n], outs[3 * n:4 * n], outs[4 * n], outs[4 * n + 1]

        def total(ref, rows, lanes):
            acc = ref[0, rows, lanes]
            for d in range(1, NDEV):
                acc = acc + ref[d, rows, lanes]
            return acc

        for k, name in enumerate(names):
            if name in _SMALL_WIDE or name == "norm_g":
                for part in range(_SMALL_SHAPES[name][1] // D):
                    pack, r = (norm_ref, 0) if name == "norm_g" else (wide_ref, _SMALL_WIDE[name] + part)
                    cols = slice(part * D, (part + 1) * D)
                    g = total(pack, slice(r, r + 1), slice(None))
                    gr[k][:, cols] = g
                    dr[k][:, cols], nmr[k][:, cols], nvr[k][:, cols] = _adam_update(wr[k][:, cols], mr[k][:, cols], vr[k][:, cols], g)
            else:
                r, rows, lanes = _SMALL_NARROW[name]
                g = total(narrow_ref, slice(r, r + rows), slice(0, lanes))
                gr[k][...] = g
                dr[k][...], nmr[k][...], nvr[k][...] = _adam_update(wr[k][...], mr[k][...], vr[k][...], g)
        cw_ref[...] = total(wide_ref, slice(_WIDE_CONV_W, _WIDE_CONV_W + 12), slice(None))
        loss_ref[...] = total(wide_ref, slice(_WIDE_LOSS, _WIDE_LOSS + 1), slice(None))

    shapes = [jax.ShapeDtypeStruct(_SMALL_SHAPES[k], F32) for k in names]
    specs = [_full(_SMALL_SHAPES[k]) for k in names]
    res = pl.pallas_call(
        body, name="adamw_small", grid=(1,),
        in_specs=specs * 3 + [_full(wide_all.shape), _full(narrow_all.shape), _full(norm_all.shape)],
        out_specs=specs * 4 + [_full((12, D)), _full((1, D))],
        out_shape=shapes * 4 + [jax.ShapeDtypeStruct((12, D), F32), jax.ShapeDtypeStruct((1, D), F32)],
        compiler_params=_cp(("arbitrary",)),
    )(*[w[k] for k in names], *[m[k] for k in names], *[v[k] for k in names], wide_all, narrow_all, norm_all)
    groups = [dict(zip(names, res[q * n:(q + 1) * n])) for q in range(4)]
    return groups[0], groups[1], groups[2], groups[3], res[4 * n], res[4 * n + 1][0, 0]


def _dev_index(px, py, pc):
    return 4 * px + 2 * py + pc


def _mesh_position():
    return lax.axis_index("x"), lax.axis_index("y"), lax.axis_index("c")


def _gather_exchange(blocks):
    n = len(blocks)

    def make(ins, outs, sems):
        send_sems, recv_sems, local_sems = sems
        x, y, c = _mesh_position()
        me, sibling = (x, y, c), (x, y, 1 - c)
        chips = [(1 - x, y), (x, 1 - y), (1 - x, 1 - y)]

        def copy(a, k, block, to, src=None):
            dst = outs[a].at[_dev_index(*block)]
            return pltpu.make_async_remote_copy(src_ref=dst if src is None else src, dst_ref=dst, send_sem=send_sems.at[a, k],
                                                recv_sem=recv_sems.at[a, k], device_id=to, device_id_type=MESH)

        mine = [pltpu.make_async_copy(ins[a], outs[a].at[_dev_index(*me)], local_sems.at[a]) for a in range(n)]
        first = []
        for a in range(n):
            first.append(copy(a, 0, me, sibling, src=ins[a]))
            first += [copy(a, 1 + j, me, (*chip, c), src=ins[a]) for j, chip in enumerate(chips)]

        def start():
            for cp in mine + first:
                cp.start()

        def finish():
            passed = []
            for j, chip in enumerate(chips):
                for a in range(n):
                    copy(a, 1 + j, (*chip, c), me).wait_recv()
                    fwd = copy(a, 4 + j, (*chip, c), sibling)
                    fwd.start()
                    passed.append(fwd)
            for a in range(n):
                copy(a, 0, sibling, me).wait_recv()
                for j, chip in enumerate(chips):
                    copy(a, 4 + j, (*chip, 1 - c), me).wait_recv()
            for cp in first + passed:
                cp.wait_send()
            for cp in mine:
                cp.wait()

        return start, finish

    return _Exchange(list(blocks), [jax.ShapeDtypeStruct((NDEV,) + b.shape, b.dtype) for b in blocks],
                     [pltpu.SemaphoreType.DMA((n, 7)), pltpu.SemaphoreType.DMA((n, 7)), pltpu.SemaphoreType.DMA((n,))], make)


def _relay_gather_exchange(blocks):
    n = len(blocks)

    def make(ins, outs, sems):
        send_sems, recv_sems, local_sems = sems
        x, y, c = _mesh_position()
        me, sibling = (x, y, c), (x, y, 1 - c)
        x_nbr, y_nbr, diag = (1 - x, y), (x, 1 - y), (1 - x, 1 - y)
        relay_from = (jnp.where(c == 0, x, 1 - x), jnp.where(c == 0, 1 - y, y))
        relay_to = (jnp.where(c == 0, 1 - x, x), jnp.where(c == 0, y, 1 - y))

        def copy(a, k, block, to, src=None):
            dst = outs[a].at[_dev_index(*block)]
            return pltpu.make_async_remote_copy(src_ref=dst if src is None else src, dst_ref=dst, send_sem=send_sems.at[a, k],
                                                recv_sem=recv_sems.at[a, k], device_id=to, device_id_type=MESH)

        mine = [pltpu.make_async_copy(ins[a], outs[a].at[_dev_index(*me)], local_sems.at[a]) for a in range(n)]
        first = []
        for a in range(n):
            first += [copy(a, 0, me, sibling, src=ins[a]), copy(a, 1, me, (*x_nbr, c), src=ins[a]), copy(a, 2, me, (*y_nbr, c), src=ins[a])]

        def start():
            for cp in mine + first:
                cp.start()

        def finish():
            later = []
            for a in range(n):
                copy(a, 1, (*x_nbr, c), me).wait_recv()
                copy(a, 2, (*y_nbr, c), me).wait_recv()
                later.append(copy(a, 3, (*relay_from, c), (*relay_to, c)))
                later += [copy(a, 4, (*x_nbr, c), sibling), copy(a, 5, (*y_nbr, c), sibling)]
                for cp in later[-3:]:
                    cp.start()
            for a in range(n):
                copy(a, 3, (*diag, c), me).wait_recv()
                later.append(copy(a, 6, (*diag, c), sibling))
                later[-1].start()
            for a in range(n):
                copy(a, 0, sibling, me).wait_recv()
                for k, chip in ((4, x_nbr), (5, y_nbr), (6, diag)):
                    copy(a, k, (*chip, 1 - c), me).wait_recv()
            for cp in first + later:
                cp.wait_send()
            for cp in mine:
                cp.wait()

        return start, finish

    return _Exchange(list(blocks), [jax.ShapeDtypeStruct((NDEV,) + b.shape, b.dtype) for b in blocks],
                     [pltpu.SemaphoreType.DMA((n, 7)), pltpu.SemaphoreType.DMA((n, 7)), pltpu.SemaphoreType.DMA((n,))], make)


def _combine(*exchanges):
    def make(ins, outs, sems):
        pairs = []
        for e in exchanges:
            ni, no, ns = len(e.arrays), len(e.out_shape), len(e.sems)
            pairs.append(e.make(ins[:ni], outs[:no], sems[:ns]))
            ins, outs, sems = ins[ni:], outs[no:], sems[ns:]

        def start():
            for s, _ in pairs:
                s()

        def finish():
            for _, f in pairs:
                f()

        return start, finish

    return _Exchange(sum((list(e.arrays) for e in exchanges), []), sum((list(e.out_shape) for e in exchanges), []),
                     sum((list(e.sems) for e in exchanges), []), make)


def _start_wait_all(copies, local=()):
    def start():
        for cp in list(local) + list(copies):
            cp.start()

    def finish():
        for cp in copies:
            cp.wait()
        for cp in local:
            cp.wait()

    return start, finish


def _no_exchange():
    return _Exchange([], [], [], lambda ins, outs, sems: (lambda: None, lambda: None))


def _direct_exchange(grads):
    n = len(grads)

    def make(ins, outs, sems):
        send_sems, recv_sems = sems
        x, y, c = _mesh_position()
        copies = []
        for a in range(n):
            for r in range(1, NDEV):
                peer = (x ^ ((r >> 2) & 1), y ^ ((r >> 1) & 1), c ^ (r & 1))
                copies.append(pltpu.make_async_remote_copy(
                    src_ref=ins[a].at[_dev_index(*peer)], dst_ref=outs[a].at[r - 1], send_sem=send_sems.at[a, r - 1],
                    recv_sem=recv_sems.at[a, r - 1], device_id=peer, device_id_type=MESH))
        return _start_wait_all(copies)

    return _Exchange(list(grads), [jax.ShapeDtypeStruct((NDEV - 1,) + g.shape[1:], g.dtype) for g in grads],
                     [pltpu.SemaphoreType.DMA((n, NDEV - 1)), pltpu.SemaphoreType.DMA((n, NDEV - 1))], make)


def _direct_gather_exchange(smalls):
    n = len(smalls)

    def make(ins, outs, sems):
        send_sems, recv_sems, local_sems = sems
        x, y, c = _mesh_position()
        copies, local = [], []
        for a in range(n):
            slot = outs[a].at[_dev_index(x, y, c)]
            local.append(pltpu.make_async_copy(ins[a], slot, local_sems.at[a]))
            for r in range(1, NDEV):
                peer = (x ^ ((r >> 2) & 1), y ^ ((r >> 1) & 1), c ^ (r & 1))
                copies.append(pltpu.make_async_remote_copy(src_ref=ins[a], dst_ref=slot, send_sem=send_sems.at[a, r - 1],
                                                           recv_sem=recv_sems.at[a, r - 1], device_id=peer, device_id_type=MESH))
        return _start_wait_all(copies, local=local)

    return _Exchange(list(smalls), [jax.ShapeDtypeStruct((NDEV,) + s.shape, s.dtype) for s in smalls],
                     [pltpu.SemaphoreType.DMA((n, 7)), pltpu.SemaphoreType.DMA((n, 7)), pltpu.SemaphoreType.DMA((n,))], make)


_W_IN_ROWS = {"u": (0, 1024), "v": (1024, 2048), "za": (2048, 3072), "zb": (3072, 5120), "xbc": (5120, 8192),
              "dt": (8192, 8224), "ga": (8224, 9248), "gb": (9248, 10272)}
_PROJ_ORDER = ("xbc", "zb", "u", "v", "za", "ga", "gb", "dt")


def _w_in_t_rows(wt):
    z = jnp.zeros((NP - NIN, wt.shape[1]), wt.dtype)
    return jnp.concatenate([wt[slice(*_W_IN_ROWS[n])] for n in _PROJ_ORDER] + [z], axis=0)


_WEIGHTS = ["norm_g", "w_in", "ln_a_g", "ln_a_b", "w_s", "b_s", "conv_w", "conv_b", "dt_bias", "a_log", "d_skip", "ssm_norm_g",
            "w_oa", "w_ob", "w_out", "ple_norm_g", "w_pg", "w_ple", "final_g"]


def _rows_pack(d, dtype):
    return jnp.concatenate([d["w_oa"].reshape(128, D), d["w_ob"].reshape(256, D), d["w_out"].reshape(128, D),
                            d["w_pg"].reshape(128, D), d["w_ple"].reshape(32, D)], axis=0).astype(dtype)


def kernel(x, p, norm_g, w_in, ln_a_g, ln_a_b, w_s, b_s, conv_w, conv_b, dt_bias, a_log, d_skip, ssm_norm_g, w_oa, w_ob, w_out, ple_norm_g, w_pg, w_ple, final_g, loss_target, m_norm_g, m_w_in, m_ln_a_g, m_ln_a_b, m_w_s, m_b_s, m_conv_w, m_conv_b, m_dt_bias, m_a_log, m_d_skip, m_ssm_norm_g, m_w_oa, m_w_ob, m_w_out, m_ple_norm_g, m_w_pg, m_w_ple, m_final_g, v_norm_g, v_w_in, v_ln_a_g, v_ln_a_b, v_w_s, v_b_s, v_conv_w, v_conv_b, v_dt_bias, v_a_log, v_d_skip, v_ssm_norm_g, v_w_oa, v_w_ob, v_w_out, v_ple_norm_g, v_w_pg, v_w_ple, v_final_g):
    args = dict(locals())
    w = {n: args[n] for n in _WEIGHTS}
    m = {n: args["m_" + n] for n in _WEIGHTS}
    v = {n: args["v_" + n] for n in _WEIGHTS}
    T = x.shape[1]
    xi, yi, ci = lax.axis_index("x"), lax.axis_index("y"), lax.axis_index("c")
    me = 4 * xi + 2 * yi + ci
    x2, p2, tgt = x.reshape(T, D), p.reshape(T, PLE), loss_target.reshape(T, D)

    norm_g2 = w["norm_g"].reshape(1, D)
    ws = jnp.where(jnp.tril(jnp.ones((CH, CH), bool))[None], w["w_s"].reshape(NG, CH, CH), 0.0).astype(BF16)
    wst = jnp.transpose(ws, (0, 2, 1))
    bst = jnp.broadcast_to(w["b_s"].reshape(NG, CH, 1), (NG, CH, 256))
    ln_g, ln_b = w["ln_a_g"].reshape(1, D), w["ln_a_b"].reshape(1, D)
    cb = w["conv_b"].reshape(1, CD)
    pad32 = lambda a: jnp.pad(a.reshape(1, NH), ((0, 0), (0, DTW - NH)))
    dtb, alog = pad32(w["dt_bias"]), pad32(w["a_log"])
    dskx = jnp.repeat(w["d_skip"].reshape(NH), HD).reshape(1, DI)
    sg = w["ssm_norm_g"].reshape(1, DI)
    ple_g, fin_g = w["ple_norm_g"].reshape(1, D), w["final_g"].reshape(1, D)
    e128 = (jnp.arange(DTW)[:, None] == (jnp.arange(DI)[None, :] // HD)).astype(BF16)
    et128 = e128.T
    gsel = ((jnp.arange(D)[:, None] // 256) == jnp.arange(128)[None, :]).astype(BF16)

    w_in_t = lambda a: jnp.transpose(a.reshape(D, WSH))
    (a_all,) = _run_exchange(_relay_gather_exchange([w_in_t(w["w_in"]).astype(BF16)]), "all_gather_w_in")
    w_in_full_t = a_all.reshape(NIN, D)
    wp = _w_in_t_rows(w_in_full_t)
    (proj, dtr, h), (r_all, cw_all) = _proj_fwd(x2, norm_g2, wp, _gather_exchange([_rows_pack(w, BF16), w["conv_w"].reshape(4, CD // NDEV)]))
    f_oa = r_all[:, R_OA:R_OB].reshape(D, D)
    f_ob = r_all[:, R_OB:R_OUT].reshape(DI, D)
    f_out = r_all[:, R_OUT:R_PG].reshape(D, D)
    f_pg = r_all[:, R_PG:R_PLE].reshape(D, D)
    f_ple = jnp.transpose(r_all[:, R_PLE:R_ROWS].reshape(NDEV, PLE, 128), (1, 0, 2)).reshape(PLE, D)
    cw = jnp.transpose(cw_all, (1, 0, 2)).reshape(4, CD)

    y, yb, hprev, pre_all = _ssd_fwd(proj, dtr, cw, cb, dtb, alog, dskx, sg, e128)
    dx1, dx1b, ya, mg, hp, dpre, dpe, doa, dob, dya, dyb, dgab, acc = _merge(
        x2, yb, proj, p2, tgt, (ln_g, ln_b, ws, bst), f_oa, f_ob, f_out, f_pg, f_ple, ple_g, fin_g)

    gple = jnp.transpose(_wgrad(p2, dpe, "wgrad_ple").reshape(PLE, NDEV, 128), (1, 0, 2)).reshape(NDEV, 32, D)
    gr = jnp.concatenate([_wgrad(ya, doa, "wgrad_oa").reshape(NDEV, 128, D), _wgrad(yb, dob, "wgrad_ob").reshape(NDEV, 256, D),
                          _wgrad(mg, dx1b, "wgrad_out").reshape(NDEV, 128, D), _wgrad(hp, dpre, "wgrad_pg").reshape(NDEV, 128, D),
                          gple], axis=1)
    (duvz, dws, dbs, dln), _ = _gmlp_bwd(proj, dya, ln_g, ln_b, ws, wst, bst, gsel, _no_exchange())
    (dxz, ddt, dcw, dsm, dsg), (rr,) = _ssd_bwd(proj, pre_all, dtr, y, dyb, hprev, cw, dtb, alog, dskx, sg, e128, et128, _direct_exchange([gr]))

    g_xz = _wgrad(dxz, h, "wgrad_xbc_zb")
    g_w_in_t = jnp.concatenate([_wgrad(duvz, h, "wgrad_u_v_za"), g_xz[CD:], g_xz[:CD], _wgrad(ddt, h, "wgrad_dt")[:NH],
                                _wgrad(dgab, h, "wgrad_ga_gb")], axis=0)
    ga = g_w_in_t.reshape(NDEV, WSH, D)
    wide = jnp.concatenate([dln, acc, dsg.reshape(16, D), dcw.reshape(24, D)], axis=0)
    narrow = jnp.concatenate([dws.reshape(NG * CH, CH), jnp.pad(dbs[:, :NG].T, ((0, 8 - NG), (0, 0))), dsm], axis=0)
    (gx, dng), (ra, wide_all, narrow_all) = _proj_bwd(x2, dx1, norm_g2, wp, [dxz, duvz, dgab], ddt,
                                                      _combine(_direct_exchange([ga]), _direct_gather_exchange([wide, narrow])))
    ga_own = lax.dynamic_index_in_dim(ga, me, 0, keepdims=False)
    gr_own = lax.dynamic_index_in_dim(gr, me, 0, keepdims=False)

    out_g, out_d, out_m, out_v = {}, {}, {}, {}
    outs = (out_g, out_d, out_m, out_v)
    tiles = lambda a: jnp.transpose(a.reshape(8, 128, WSH), (2, 0, 1))
    g2, (norm_all,) = _sum_parts([ga_own] + [(ra, k) for k in range(NDEV - 1)], "sum_w_in", _direct_gather_exchange([dng]))
    res = _adamw_tiles(tiles(w["w_in"]), tiles(m["w_in"]), tiles(v["w_in"]), g2.reshape(WSH, 8, 128), "adamw_w_in")
    for dst, val in zip(outs, res):
        dst["w_in"] = jnp.transpose(val, (1, 2, 0)).reshape(1, D, WSH)
    parts_r = [gr_own] + [(rr, k) for k in range(NDEV - 1)]
    for name, row, rows in (("w_oa", R_OA, 128), ("w_ob", R_OB, 256), ("w_out", R_OUT, 128), ("w_pg", R_PG, 128)):
        res = _adamw(w[name].reshape(rows, D), m[name].reshape(rows, D), v[name].reshape(rows, D), parts_r, "adamw_" + name,
                     part_row=row, block_rows=128)
        for dst, val in zip(outs, res):
            dst[name] = val.reshape(1, rows, D)
    res = _adamw(w["w_ple"].reshape(32, D), m["w_ple"].reshape(32, D), v["w_ple"].reshape(32, D), parts_r, "adamw_w_ple",
                 part_row=R_PLE, block_rows=32)
    for dst, val in zip(outs, res):
        dst["w_ple"] = val.reshape(1, PLE, 128)
    two_d = lambda d: {n: d[n].reshape(_SMALL_SHAPES[n]) for n in _SMALL_SHAPES}
    *res, g_cw_wide, loss = _adamw_small(two_d(w), two_d(m), two_d(v), wide_all, narrow_all, norm_all)
    for dst, val in zip(outs, res):
        dst.update({n: val[n].reshape(w[n].shape) for n in _SMALL_SHAPES})
    g_cw = lax.dynamic_slice_in_dim(g_cw_wide.reshape(4, CD), me * (CD // NDEV), CD // NDEV, axis=1).reshape(12, 128)
    res = _adamw(w["conv_w"].reshape(12, 128), m["conv_w"].reshape(12, 128), v["conv_w"].reshape(12, 128), [g_cw], "adamw_conv_w")
    for dst, val in zip((out_g, out_d, out_m, out_v), res):
        dst["conv_w"] = val.reshape(1, 4, CD // NDEV)

    return (loss, gx.reshape(1, T, D), *[out_g[n] for n in _WEIGHTS], *[out_d[n] for n in _WEIGHTS],
            *[out_m[n] for n in _WEIGHTS], *[out_v[n] for n in _WEIGHTS])
```

```python
import functools
import math
from typing import Callable, NamedTuple

import jax
import jax.numpy as jnp
from jax import lax
from jax.experimental import pallas as pl
from jax.experimental.pallas import tpu as pltpu

F32 = jnp.float32
BF16 = jnp.bfloat16
MESH = pl.DeviceIdType.MESH

D = 1024
DI = 2048
CD = 3072
NH = 32
HD = 64
NST = 128
NG = 4
CH = 128
PLE = 256
NIN = 10272
NDEV = 8
WSH = NIN // NDEV
EPS = 1e-6
OFF_XBC, OFF_ZB, OFF_U, OFF_V, OFF_ZA, OFF_GA, OFF_GB, OFF_DT = 0, 3072, 5120, 6144, 7168, 8192, 9216, 10240
NP = 10368
DTW = 128
R_OA, R_OB, R_OUT, R_PG, R_PLE, R_ROWS = 0, 128, 384, 512, 640, 672

ADAM_LR, ADAM_B1, ADAM_B2, ADAM_EPS, ADAM_WD, ADAM_STEP = 0.001, 0.9, 0.999, 1e-08, 0.01, 10

V7X_VMEM_LIMIT = 56 * 1024 * 1024


def _cp(sem=None):
    return pltpu.CompilerParams(dimension_semantics=sem, vmem_limit_bytes=V7X_VMEM_LIMIT)


def _dot(a, b, prec=None):
    return jnp.dot(a, b, preferred_element_type=F32, precision=prec)


def _dot_nt(a, b, prec=None):
    return lax.dot_general(a, b, (((1,), (1,)), ((), ())), preferred_element_type=F32, precision=prec)


def _dot_tn(a, b, prec=None):
    return lax.dot_general(a, b, (((0,), (0,)), ((), ())), preferred_element_type=F32, precision=prec)


def _sigmoid(x):
    return 0.5 * jnp.tanh(0.5 * x) + 0.5


_GELU_C = math.sqrt(2.0 / math.pi)
_GELU_K = 0.044715


def _gelu_and_grad(x):
    x2 = x * x
    t = jnp.tanh(x * (_GELU_C + (_GELU_C * _GELU_K) * x2))
    hx = 0.5 * x
    g = hx + hx * t
    dg = (0.5 + 0.5 * t) + hx * (1.0 - t * t) * (_GELU_C + (3.0 * _GELU_C * _GELU_K) * x2)
    return g, dg


def _gelu(x):
    hx = 0.5 * x
    return hx + hx * jnp.tanh(x * (_GELU_C + (_GELU_C * _GELU_K) * (x * x)))


def _softplus(x):
    return jnp.maximum(x, 0.0) + jnp.log(1.0 + jnp.exp(-jnp.abs(x)))


def _full(shape):
    n = len(shape)
    return pl.BlockSpec(shape, lambda *_: (0,) * n)


_ANY = pl.BlockSpec(memory_space=pl.ANY)


class _Exchange(NamedTuple):
    arrays: list
    out_shape: list
    sems: list
    make: Callable


def _call(body, ex, *, name, grid, in_specs, out_specs, out_shape, scratch_shapes, args):
    ki, ko, ks = len(in_specs), len(out_specs), len(scratch_shapes)
    ei, eo = len(ex.arrays), len(ex.out_shape)
    last = [g - 1 for g in grid]

    def full_body(*refs):
        r = list(refs)
        ins, eins, r = r[:ki], r[ki:ki + ei], r[ki + ei:]
        outs, eouts, r = r[:ko], r[ko:ko + eo], r[ko + eo:]
        scr, esems = r[:ks], r[ks:]
        start, finish = ex.make(eins, eouts, esems)
        ids = [pl.program_id(a) for a in range(len(grid))]
        is_first = functools.reduce(lambda p, q: p & q, [i == 0 for i in ids])
        is_last = functools.reduce(lambda p, q: p & q, [i == l for i, l in zip(ids, last)])
        pl.when(is_first)(start)
        body(*ins, *outs, *scr)
        pl.when(is_last)(finish)

    res = pl.pallas_call(
        full_body, name=name, grid=grid, in_specs=list(in_specs) + [_ANY] * ei, out_specs=list(out_specs) + [_ANY] * eo,
        out_shape=list(out_shape) + list(ex.out_shape), scratch_shapes=list(scratch_shapes) + list(ex.sems),
        compiler_params=_cp(("arbitrary",) * len(grid)),
    )(*args, *ex.arrays)
    return res[:ko], res[ko:]


def _run_exchange(ex, name):
    ni, no = len(ex.arrays), len(ex.out_shape)

    def body(*refs):
        start, finish = ex.make(refs[:ni], refs[ni:ni + no], refs[ni + no:])
        start()
        finish()

    return pl.pallas_call(body, name=name, in_specs=[_ANY] * ni, out_specs=[_ANY] * no, out_shape=list(ex.out_shape),
                          scratch_shapes=list(ex.sems))(*ex.arrays)


def _proj_fwd(x, norm_g, wp, ex):
    T = x.shape[0]
    tm, tn = min(T, 1024), 2048
    nj = OFF_DT // tn
    assert OFF_DT % tn == 0 and OFF_DT + DTW == NP

    def body(x_ref, g_ref, w_ref, wdt_ref, proj_ref, dt_ref, h_ref, hs_ref):
        j = pl.program_id(1)

        @pl.when(j == 0)
        def _():
            xf = x_ref[...]
            r = lax.rsqrt(jnp.mean(xf * xf, axis=-1, keepdims=True) + EPS)
            h = (xf * r * g_ref[...]).astype(BF16)
            hs_ref[...] = h
            h_ref[...] = h

        proj_ref[...] = _dot_nt(hs_ref[...], w_ref[...]).astype(BF16)

        @pl.when(j == nj - 1)
        def _():
            dt_ref[...] = _dot_nt(hs_ref[...], wdt_ref[...])

    return _call(
        body, ex, name="proj_fwd", grid=(T // tm, nj),
        in_specs=[pl.BlockSpec((tm, D), lambda i, j: (i, 0)), _full((1, D)), pl.BlockSpec((tn, D), lambda i, j: (j, 0)),
                  pl.BlockSpec((DTW, D), lambda i, j: (OFF_DT // DTW, 0))],
        out_specs=[pl.BlockSpec((tm, tn), lambda i, j: (i, j)), pl.BlockSpec((tm, DTW), lambda i, j: (i, 0)),
                   pl.BlockSpec((tm, D), lambda i, j: (i, 0))],
        out_shape=[jax.ShapeDtypeStruct((T, OFF_DT), BF16), jax.ShapeDtypeStruct((T, DTW), F32), jax.ShapeDtypeStruct((T, D), BF16)],
        scratch_shapes=[pltpu.VMEM((tm, D), BF16)], args=(x, norm_g, wp, wp))


def _gmlp_tile():
    return 512


def _gmlp_fwd_tile(u_ref, v_ref, z_ref, lg_ref, lb_ref, ws_ref, bs_ref, ya_ref, vn_s):
    tm = u_ref.shape[0]
    vg = _gelu(v_ref[...].astype(F32))
    mu = jnp.mean(vg, axis=-1, keepdims=True)
    xc = vg - mu
    rstd = lax.rsqrt(jnp.mean(xc * xc, axis=-1, keepdims=True) + EPS)
    vn_s[...] = (xc * rstd * lg_ref[...] + lb_ref[...]).astype(BF16)
    for c in range(tm // CH):
        rs = slice(c * CH, (c + 1) * CH)
        for g in range(NG):
            cs_ = slice(g * 256, (g + 1) * 256)
            sv = _dot(ws_ref[g], vn_s[rs, cs_]) + bs_ref[g]
            z = z_ref[rs, cs_].astype(F32)
            ya_ref[rs, cs_] = (_gelu(u_ref[rs, cs_].astype(F32)) * sv * (z * _sigmoid(z))).astype(BF16)


def _gmlp_bwd(proj, dya, ln_g, ln_b, ws, wst, bst, gsel, ex):
    T = proj.shape[0]
    tm = min(T, _gmlp_tile())

    def body(u_ref, v_ref, z_ref, dy_ref, lg_ref, lb_ref, ws_ref, wst_ref, bs_ref, gsel_ref,
             d_ref, dws_ref, dbs_ref, dln_ref, vn_s, dsv_s, dvn_s):
        du_ref, dv_ref, dz_ref = d_ref.at[:, 0:D], d_ref.at[:, D:2 * D], d_ref.at[:, 2 * D:3 * D]
        @pl.when(pl.program_id(0) == 0)
        def _():
            dws_ref[...] = jnp.zeros_like(dws_ref)
            dbs_ref[...] = jnp.zeros_like(dbs_ref)
            dln_ref[...] = jnp.zeros_like(dln_ref)

        vg, dvg_dv = _gelu_and_grad(v_ref[...].astype(F32))
        mu = jnp.mean(vg, axis=-1, keepdims=True)
        xc = vg - mu
        rstd = lax.rsqrt(jnp.mean(xc * xc, axis=-1, keepdims=True) + EPS)
        vhat = xc * rstd
        vn_s[...] = (vhat * lg_ref[...] + lb_ref[...]).astype(BF16)
        ri = lax.broadcasted_iota(jnp.int32, (CH, CH), 0)
        ci = lax.broadcasted_iota(jnp.int32, (CH, CH), 1)
        tril = (ri >= ci).astype(F32)
        for c in range(tm // CH):
            rs = slice(c * CH, (c + 1) * CH)
            for g in range(NG):
                cs_ = slice(g * 256, (g + 1) * 256)
                vn = vn_s[rs, cs_]
                sv = _dot(ws_ref[g], vn) + bs_ref[g]
                z = z_ref[rs, cs_].astype(F32)
                sz = _sigmoid(z)
                ug, dug_du = _gelu_and_grad(u_ref[rs, cs_].astype(F32))
                dy = dy_ref[rs, cs_].astype(F32)
                zs = z * sz
                t = dy * zs
                dsv_f = dy * sv
                du_ref[rs, cs_] = (zs * dsv_f * dug_du).astype(BF16)
                dz_ref[rs, cs_] = (dsv_f * ug * (sz + zs * (1.0 - sz))).astype(BF16)
                dsv = (t * ug).astype(BF16)
                dsv_s[rs, cs_] = dsv
                dvn_s[rs, cs_] = _dot(wst_ref[g], dsv)
                dws_ref[g] += _dot_nt(dsv, vn) * tril
            dbs_ref[...] += _dot(dsv_s[rs, :], gsel_ref[...])
        dvn = dvn_s[...]
        dln_ref[0:1, :] += jnp.sum(dvn * vhat, axis=0, keepdims=True)
        dln_ref[1:2, :] += jnp.sum(dvn, axis=0, keepdims=True)
        dvh = dvn * lg_ref[...]
        dvg = rstd * (dvh - jnp.mean(dvh, axis=-1, keepdims=True) - vhat * jnp.mean(dvh * vhat, axis=-1, keepdims=True))
        dv_ref[...] = (dvg * dvg_dv).astype(BF16)

    blk = lambda off: pl.BlockSpec((tm, D), lambda i: (i, off // D))
    row = pl.BlockSpec((tm, D), lambda i: (i, 0))
    return _call(
        body, ex, name="gmlp_bwd", grid=(T // tm,),
        in_specs=[blk(OFF_U), blk(OFF_V), blk(OFF_ZA), row, _full((1, D)), _full((1, D)), _full((NG, CH, CH)),
                  _full((NG, CH, CH)), _full((NG, CH, 256)), _full((D, 128))],
        out_specs=[pl.BlockSpec((tm, 3 * D), lambda i: (i, 0)), _full((NG, CH, CH)), _full((CH, 128)), _full((8, D))],
        out_shape=[jax.ShapeDtypeStruct((T, 3 * D), BF16),
                   jax.ShapeDtypeStruct((NG, CH, CH), F32), jax.ShapeDtypeStruct((CH, 128), F32), jax.ShapeDtypeStruct((8, D), F32)],
        scratch_shapes=[pltpu.VMEM((tm, D), BF16), pltpu.VMEM((tm, D), BF16), pltpu.VMEM((tm, D), F32)],
        args=(proj, proj, proj, dya, ln_g, ln_b, ws, wst, bst, gsel))


def _zb_cols(zb0_ref, zb1_ref, cols):
    ref = zb0_ref if cols.start < D else zb1_ref
    return ref[:, cols.start % D:cols.start % D + (cols.stop - cols.start)]


def _shift_matrix(down):
    t = jnp.arange(CH)[None, :, None]
    j = jnp.arange(1, 4)[:, None, None]
    col = jnp.arange(2 * CH)[None, None, :]
    src = CH + t - j if down else t + j
    return (col == src).astype(BF16).reshape(3 * CH, 2 * CH)


def _conv_pre(x, moved, cw_ref, cb_ref):
    pre = cb_ref[...] + cw_ref[3:4, :] * x
    for j in (1, 2, 3):
        pre = pre + cw_ref[3 - j:4 - j, :] * moved[(j - 1) * CH:j * CH]
    return pre


def _split_dot(x, w, parts, w_left=False):
    acc, r = None, x
    for k in range(parts):
        hi = r.astype(BF16)
        d = _dot(w, hi) if w_left else _dot(hi, w)
        acc = d if acc is None else acc + d
        if k + 1 < parts:
            r = r - hi.astype(F32)
    return acc


def _chunk_decays(dt, alog_ref, e_ref, cs_s, cst_s, csx_s):
    a = -jnp.exp(alog_ref[...])
    ri = lax.broadcasted_iota(jnp.int32, (CH, CH), 0)
    ci = lax.broadcasted_iota(jnp.int32, (CH, CH), 1)
    tril = ri >= ci
    cs = _split_dot(dt * a, tril.astype(BF16), 3, w_left=True)
    cs_s[...] = cs
    cst_s[...] = cs.T
    csx_s[...] = _split_dot(cs, e_ref[...], 3)
    return a, tril, ri, ci


def _lmat(cst_s, h, tril):
    rowb = jnp.broadcast_to(cst_s[h:h + 1, :], (CH, CH))
    return jnp.exp(jnp.where(tril, rowb.T - rowb, -jnp.inf))


def _head_pair_rows(v, lane):
    return jnp.concatenate([jnp.where(lane < HD, v, 0.0), jnp.where(lane < HD, 0.0, v)], axis=0).astype(BF16)


def _ssd_fwd(proj, dtr, cw, cb, dtb, alog, dskx, sg, e128):
    T = proj.shape[0]
    nc = T // CH

    def body(xbc_ref, zb0_ref, zb1_ref, dt_ref, cw_ref, cb_ref, dtb_ref, alog_ref, dx_ref, sg_ref, e_ref, shift_ref,
             y_ref, yb_ref, hp_ref, pre_ref, xx_s, h_s, cs_s, cst_s, csx_s, yz_s):
        @pl.when(pl.program_id(0) == 0)
        def _():
            xx_s[...] = jnp.zeros_like(xx_s)
            h_s[...] = jnp.zeros_like(h_s)

        xx_s[CH:, :] = xbc_ref[...]
        moved = _dot(shift_ref[...], xx_s[...])
        xx_s[CH - 16:CH, :] = xbc_ref[CH - 16:, :]
        x = xbc_ref[...].astype(F32)
        pre = _conv_pre(x, moved, cw_ref, cb_ref)
        pre_ref[...] = pre
        xc = pre * _sigmoid(pre)
        dt = _softplus(dt_ref[...] + dtb_ref[...])
        a, tril, _, lane = _chunk_decays(dt, alog_ref, e_ref, cs_s, cst_s, csx_s)
        dt_x = _split_dot(dt, e_ref[...], 2)
        cs_last_x = csx_s[CH - 1:CH, :]
        hp_ref[0] = h_s[...]
        for g in range(NG):
            gs = slice(g * 512, (g + 1) * 512)
            bg = xc[:, DI + g * NST:DI + (g + 1) * NST].astype(BF16)
            cg = xc[:, DI + 512 + g * NST:DI + 512 + (g + 1) * NST].astype(BF16)
            cbm = _dot_nt(cg, bg)
            xg = xc[:, gs]
            xdt = xg * dt_x[:, gs]
            hprev = h_s[:, gs]
            csx = csx_s[:, gs]
            yoff = _dot(cg, hprev.astype(BF16)) * jnp.exp(csx)
            st = _dot_tn(bg, (xdt * jnp.exp(cs_last_x[:, gs] - csx)).astype(BF16))
            h_s[:, gs] = jnp.exp(cs_last_x[:, gs]) * hprev + st
            ssq = jnp.zeros((CH, 1), F32)
            for q in range(4):
                h0 = g * 8 + 2 * q
                ps = slice(q * 128, (q + 1) * 128)
                cols = slice(g * 512 + q * 128, g * 512 + (q + 1) * 128)
                m01 = jnp.concatenate([cbm * _lmat(cst_s, h0, tril), cbm * _lmat(cst_s, h0 + 1, tril)], axis=1).astype(BF16)
                yq = _dot(m01, _head_pair_rows(xdt[:, ps], lane)) + yoff[:, ps] + xg[:, ps] * dx_ref[:, cols]
                y_ref[:, cols] = yq
                z = _zb_cols(zb0_ref, zb1_ref, cols).astype(F32)
                yz = yq * z * _sigmoid(z)
                yz_s[:, cols] = yz
                ssq = ssq + jnp.sum(yz * yz, axis=1, keepdims=True)
            rg = lax.rsqrt(ssq * (1.0 / 512.0) + EPS)
            yb_ref[:, gs] = (yz_s[:, gs] * rg * sg_ref[:, gs]).astype(BF16)

    return pl.pallas_call(
        body, name="ssd_fwd", grid=(nc,),
        in_specs=[pl.BlockSpec((CH, CD), lambda c: (c, OFF_XBC // CD)), pl.BlockSpec((CH, D), lambda c: (c, OFF_ZB // D)),
                  pl.BlockSpec((CH, D), lambda c: (c, OFF_ZB // D + 1)),
                  pl.BlockSpec((CH, DTW), lambda c: (c, 0)), _full((4, CD)), _full((1, CD)), _full((1, DTW)),
                  _full((1, DTW)), _full((1, DI)), _full((1, DI)), _full((DTW, DI)), _full((3 * CH, 2 * CH))],
        out_specs=[pl.BlockSpec((CH, DI), lambda c: (c, 0)), pl.BlockSpec((CH, DI), lambda c: (c, 0)),
                   pl.BlockSpec((1, NST, DI), lambda c: (c, 0, 0)), pl.BlockSpec((CH, CD), lambda c: (c, 0))],
        out_shape=[jax.ShapeDtypeStruct((T, DI), F32), jax.ShapeDtypeStruct((T, DI), BF16),
                   jax.ShapeDtypeStruct((nc, NST, DI), F32), jax.ShapeDtypeStruct((T, CD), F32)],
        scratch_shapes=[pltpu.VMEM((2 * CH, CD), BF16), pltpu.VMEM((NST, DI), F32), pltpu.VMEM((CH, CH), F32),
                        pltpu.VMEM((CH, CH), F32), pltpu.VMEM((CH, DI), F32), pltpu.VMEM((CH, DI), F32)],
        compiler_params=_cp(("arbitrary",)),
    )(proj, proj, proj, dtr, cw, cb, dtb, alog, dskx, sg, e128, _shift_matrix(down=True))


def _ssd_bwd(proj, pre_all, dtr, y, dyb, hprev_all, cw, dtb, alog, dskx, sg, e128, et128, ex):
    T = proj.shape[0]
    nc = T // CH

    def body(xbc_ref, pre_ref, zb0_ref, zb1_ref, dt_ref, y_ref, dyb_ref, hp_ref, cw_ref, dtb_ref, alog_ref, dx_ref, sg_ref,
             e_ref, et_ref, shift_ref, d_ref, ddt_ref, dcw_ref, dsm_ref, dsg_ref,
             g_s, dd_s, cs_s, cst_s, csx_s, dy_s, dxdt_s, dxs_s, dsd_s, dxc_s, gh_s):
        dxbc_ref, dzb_ref = d_ref.at[:, 0:CD], d_ref.at[:, CD:CD + DI]
        i = pl.program_id(0)

        @pl.when(i == 0)
        def _():
            g_s[...] = jnp.zeros_like(g_s)
            dd_s[...] = jnp.zeros_like(dd_s)
            dcw_ref[...] = jnp.zeros_like(dcw_ref)
            dsm_ref[...] = jnp.zeros_like(dsm_ref)
            dsg_ref[...] = jnp.zeros_like(dsg_ref)

        pre = pre_ref[...]
        sp = _sigmoid(pre)
        xc = pre * sp
        dtr = dt_ref[...] + dtb_ref[...]
        dt = _softplus(dtr)
        a, tril, ri, lane = _chunk_decays(dt, alog_ref, e_ref, cs_s, cst_s, csx_s)
        et = et_ref[...]
        dt_x = _split_dot(dt, e_ref[...], 2)
        cs_last_x = csx_s[CH - 1:CH, :]

        for g in range(NG):
            gs = slice(g * 512, (g + 1) * 512)
            z = _zb_cols(zb0_ref, zb1_ref, gs).astype(F32)
            sz = _sigmoid(z)
            yv = y_ref[:, gs]
            zs = z * sz
            yz = yv * zs
            rg = lax.rsqrt(jnp.mean(yz * yz, axis=-1, keepdims=True) + EPS)
            yn = yz * rg
            dyb = dyb_ref[:, gs].astype(F32)
            dsg_ref[0:1, gs] += jnp.sum(dyb * yn, axis=0, keepdims=True)
            dyn = dyb * sg_ref[:, gs]
            dyz = rg * (dyn - yn * jnp.mean(dyn * yn, axis=-1, keepdims=True))
            dy_s[:, gs] = dyz * zs
            dzb_ref[:, gs] = (dyz * yv * (sz + zs * (1.0 - sz))).astype(BF16)

        rsum = jnp.zeros((CH, DTW), F32)
        csum_t = jnp.zeros((DTW, CH), F32)
        for g in range(NG):
            gs = slice(g * 512, (g + 1) * 512)
            bg = xc[:, DI + g * NST:DI + (g + 1) * NST].astype(BF16)
            cg = xc[:, DI + 512 + g * NST:DI + 512 + (g + 1) * NST].astype(BF16)
            cbm = _dot_nt(cg, bg)
            xdt = xc[:, gs] * dt_x[:, gs]
            hprev = hp_ref[0, :, gs]
            hpb = hprev.astype(BF16)
            gn = g_s[:, gs]
            gnb = gn.astype(BF16)
            dy = dy_s[:, gs]
            csx = csx_s[:, gs]
            ecs = jnp.exp(csx)
            dec = jnp.exp(cs_last_x[:, gs] - csx)
            dye = (dy * ecs).astype(BF16)
            dc = _dot_nt(dye, hpb)
            dprev = _dot_tn(cg, dye)
            dxdt_state = dec * _dot(bg, gnb)
            db = _dot_nt((xdt * dec).astype(BF16), gnb)
            dcb = jnp.zeros((CH, CH), F32)
            for q in range(4):
                h0 = g * 8 + 2 * q
                ps = slice(q * 128, (q + 1) * 128)
                dyp = dy[:, ps]
                l0 = _lmat(cst_s, h0, tril)
                l1 = _lmat(cst_s, h0 + 1, tril)
                m0 = cbm * l0
                m1 = cbm * l1
                dm = _dot_nt(dyp.astype(BF16), _head_pair_rows(xdt[:, ps], lane))
                dm0 = dm[:, :CH]
                dm1 = dm[:, CH:]
                dcb = dcb + dm0 * l0 + dm1 * l1
                for hh, qm in ((h0, dm0 * m0), (h0 + 1, dm1 * m1)):
                    rsum = jnp.where(lane == hh, jnp.sum(qm, axis=1, keepdims=True), rsum)
                    csum_t = jnp.where(ri == hh, jnp.sum(qm, axis=0, keepdims=True), csum_t)
                mst = jnp.concatenate([m0, m1], axis=0).astype(BF16)
                d = _dot_tn(mst, _head_pair_rows(dyp, lane))
                dxdt_s[:, g * 512 + q * 128:g * 512 + (q + 1) * 128] = d + dxdt_state[:, ps]
            yoff = _dot(cg, hpb) * ecs
            dsd_s[:, gs] = xdt * dxdt_state
            dxs_s[:, gs] = dy * yoff
            dcbb = dcb.astype(BF16)
            dxc_s[:, DI + 512 + g * NST:DI + 512 + (g + 1) * NST] = dc + _dot(dcbb, bg)
            dxc_s[:, DI + g * NST:DI + (g + 1) * NST] = db + _dot_tn(dcbb, cg)
            gh_s[:, gs] = jnp.broadcast_to(jnp.sum(gn * hprev, axis=0, keepdims=True), (8, 512))
            g_s[:, gs] = dprev + jnp.exp(cs_last_x[:, gs]) * gn

        xs = xc[:, :DI]
        dy = dy_s[...]
        dxdt = dxdt_s[...]
        cs_last = cs_s[CH - 1:CH, :]
        state_e = _split_dot(dsd_s[...], et, 2)
        dcd = 0.125 * jnp.sum(_split_dot(gh_s[...], et, 2), axis=0, keepdims=True) * jnp.exp(cs_last)
        row = lax.broadcasted_iota(jnp.int32, (CH, 1), 0)
        dcs = rsum - csum_t.T + _split_dot(dxs_s[...], et, 2) - state_e
        dcs = dcs + jnp.where(row == CH - 1, jnp.sum(state_e, axis=0, keepdims=True) + dcd, 0.0)
        dda = _split_dot(dcs, (lane >= ri).astype(BF16), 3, w_left=True)
        ddt = dda * a + _dot((dxdt * xs).astype(BF16), et)
        ddtr = jnp.where(lane < NH, ddt * _sigmoid(dtr), 0.0)
        ddt_ref[...] = ddtr.astype(BF16)
        dsm_ref[0:1, :] += jnp.sum(ddtr, axis=0, keepdims=True)
        dsm_ref[1:2, :] += jnp.sum(dda * dt, axis=0, keepdims=True) * a
        dsm_ref[2:3, :] += jnp.sum(_dot((dy * xs).astype(BF16), et), axis=0, keepdims=True)
        dxc_s[:, :DI] = dxdt * dt_x + dy * dx_ref[...]

        dpre = dxc_s[...] * (sp + xc * (1.0 - sp))
        dpre_b = dpre.astype(BF16)
        dd_s[:CH, :] = dpre_b
        moved = _dot(shift_ref[...], dd_s[...])
        dd_s[CH:CH + 16, :] = dpre_b[:16]
        x = xbc_ref[...].astype(F32)
        dcw_ref[4:5, :] += jnp.sum(dpre, axis=0, keepdims=True)
        dxbc = cw_ref[3:4, :] * dpre
        dcw_ref[3:4, :] += jnp.sum(dpre * x, axis=0, keepdims=True)
        for j in (1, 2, 3):
            ahead = moved[(j - 1) * CH:j * CH]
            dcw_ref[3 - j:4 - j, :] += jnp.sum(ahead * x, axis=0, keepdims=True)
            dxbc = dxbc + cw_ref[3 - j:4 - j, :] * ahead
        dxbc_ref[...] = dxbc.astype(BF16)

    rev = lambda c: nc - 1 - c
    return _call(
        body, ex, name="ssd_bwd", grid=(nc,),
        in_specs=[pl.BlockSpec((CH, CD), lambda c: (rev(c), OFF_XBC // CD)),
                  pl.BlockSpec((CH, CD), lambda c: (rev(c), 0)),
                  pl.BlockSpec((CH, D), lambda c: (rev(c), OFF_ZB // D)), pl.BlockSpec((CH, D), lambda c: (rev(c), OFF_ZB // D + 1)),
                  pl.BlockSpec((CH, DTW), lambda c: (rev(c), 0)),
                  pl.BlockSpec((CH, DI), lambda c: (rev(c), 0)), pl.BlockSpec((CH, DI), lambda c: (rev(c), 0)),
                  pl.BlockSpec((1, NST, DI), lambda c: (rev(c), 0, 0)),
                  _full((4, CD)), _full((1, DTW)), _full((1, DTW)), _full((1, DI)), _full((1, DI)),
                  _full((DTW, DI)), _full((DI, DTW)), _full((3 * CH, 2 * CH))],
        out_specs=[pl.BlockSpec((CH, CD + DI), lambda c: (rev(c), 0)),
                   pl.BlockSpec((CH, DTW), lambda c: (rev(c), 0)), _full((8, CD)), _full((8, DTW)), _full((8, DI))],
        out_shape=[jax.ShapeDtypeStruct((T, CD + DI), BF16), jax.ShapeDtypeStruct((T, DTW), BF16),
                   jax.ShapeDtypeStruct((8, CD), F32), jax.ShapeDtypeStruct((8, DTW), F32), jax.ShapeDtypeStruct((8, DI), F32)],
        scratch_shapes=[pltpu.VMEM((NST, DI), F32), pltpu.VMEM((2 * CH, CD), BF16), pltpu.VMEM((CH, CH), F32), pltpu.VMEM((CH, CH), F32),
                        pltpu.VMEM((CH, DI), F32), pltpu.VMEM((CH, DI), F32), pltpu.VMEM((CH, DI), F32), pltpu.VMEM((CH, DI), F32),
                        pltpu.VMEM((CH, DI), F32), pltpu.VMEM((CH, CD), F32), pltpu.VMEM((8, DI), F32)],
        args=(proj, pre_all, proj, proj, dtr, y, dyb, hprev_all, cw, dtb, alog, dskx, sg, e128, et128, _shift_matrix(down=False)))


def _merge_tile():
    return 256


def _merge(x, yb, proj, p, tgt, gmlp, w_oa, w_ob, w_out, w_pg, w_ple, ple_g, fin_g):
    T = x.shape[0]
    tm = min(T, _merge_tile())

    def body(x_ref, u_ref, v_ref, za_ref, yb_ref, ga_ref, gb_ref, p_ref, t_ref, lg_ref, lb_ref, ws_ref, bs_ref,
             woa, wob, wout, wpg, wple, pg_ref, fg_ref,
             dx1_ref, dx1b_ref, ya_ref, mg_ref, hp_ref, dpre_ref, dpe_ref, doa_ref, dob_ref, dya_ref, dyb_ref, dg_ref, acc_ref, vn_s):
        @pl.when(pl.program_id(0) == 0)
        def _():
            acc_ref[...] = jnp.zeros_like(acc_ref)

        _gmlp_fwd_tile(u_ref, v_ref, za_ref, lg_ref, lb_ref, ws_ref, bs_ref, ya_ref, vn_s)
        oa = _dot(ya_ref[...], woa[...])
        ob = _dot(yb_ref[...], wob[...])
        sa = _sigmoid(ga_ref[...].astype(F32))
        sb = _sigmoid(gb_ref[...].astype(F32))
        mg = sa * oa + sb * ob
        mgb = mg.astype(BF16)
        mg_ref[...] = mgb
        x1 = x_ref[...] + _dot(mgb, wout[...])
        r2 = lax.rsqrt(jnp.mean(x1 * x1, axis=-1, keepdims=True) + EPS)
        xh1 = x1 * r2
        hpb = (xh1 * pg_ref[...]).astype(BF16)
        hp_ref[...] = hpb
        gate = _sigmoid(_dot(hpb, wpg[...]))
        pe = _dot(p_ref[...].astype(BF16), wple[...])
        x2 = x1 + gate * pe
        r3 = lax.rsqrt(jnp.mean(x2 * x2, axis=-1, keepdims=True) + EPS)
        xh2 = x2 * r3
        err = xh2 * fg_ref[...] - t_ref[...]
        acc_ref[2:3, :] += 0.5 * jnp.sum(jnp.mean(err * err, axis=-1, keepdims=True))
        dyo = err * (1.0 / D)
        acc_ref[0:1, :] += jnp.sum(dyo * xh2, axis=0, keepdims=True)
        dn = dyo * fg_ref[...]
        dx2 = r3 * (dn - xh2 * jnp.mean(dn * xh2, axis=-1, keepdims=True))
        dpe_ref[...] = (dx2 * gate).astype(BF16)
        dpre = (dx2 * pe * gate * (1.0 - gate)).astype(BF16)
        dpre_ref[...] = dpre
        dhp = _dot_nt(dpre, wpg[...])
        acc_ref[1:2, :] += jnp.sum(dhp * xh1, axis=0, keepdims=True)
        dhn = dhp * pg_ref[...]
        dx1 = dx2 + r2 * (dhn - xh1 * jnp.mean(dhn * xh1, axis=-1, keepdims=True))
        dx1_ref[...] = dx1
        dx1b = dx1.astype(BF16)
        dx1b_ref[...] = dx1b
        dmg = _dot_nt(dx1b, wout[...])
        doa = (dmg * sa).astype(BF16)
        dob = (dmg * sb).astype(BF16)
        doa_ref[...] = doa
        dob_ref[...] = dob
        dg_ref[:, :D] = (dmg * oa * sa * (1.0 - sa)).astype(BF16)
        dg_ref[:, D:] = (dmg * ob * sb * (1.0 - sb)).astype(BF16)
        dya_ref[...] = _dot_nt(doa, woa[...]).astype(BF16)
        dyb_ref[...] = _dot_nt(dob, wob[...]).astype(BF16)

    row = lambda w: pl.BlockSpec((tm, w), lambda i: (i, 0))
    blk = lambda off: pl.BlockSpec((tm, D), lambda i: (i, off // D))
    wsp = lambda s: pl.BlockSpec(s, lambda i: (0, 0), pipeline_mode=pl.Buffered(1))
    return pl.pallas_call(
        body, name="merge", grid=(T // tm,),
        in_specs=[row(D), blk(OFF_U), blk(OFF_V), blk(OFF_ZA), row(DI), blk(OFF_GA), blk(OFF_GB), row(PLE), row(D),
                  _full((1, D)), _full((1, D)), _full((NG, CH, CH)), _full((NG, CH, 256)),
                  wsp((D, D)), wsp((DI, D)), wsp((D, D)), wsp((D, D)), wsp((PLE, D)), _full((1, D)), _full((1, D))],
        out_specs=[row(D)] * 10 + [row(DI), row(2 * D), _full((8, D))],
        out_shape=[jax.ShapeDtypeStruct((T, D), F32)] + [jax.ShapeDtypeStruct((T, D), BF16)] * 9
        + [jax.ShapeDtypeStruct((T, DI), BF16), jax.ShapeDtypeStruct((T, 2 * D), BF16), jax.ShapeDtypeStruct((8, D), F32)],
        scratch_shapes=[pltpu.VMEM((tm, D), BF16)],
        compiler_params=_cp(("arbitrary",)),
    )(x, proj, proj, proj, yb, proj, proj, p, tgt, *gmlp, w_oa, w_ob, w_out, w_pg, w_ple, ple_g, fin_g)


def _wgrad(a, b, name):
    T, K = a.shape
    N = b.shape[1]
    tt, tk, tn = min(T, 2048), min(K, 1024), min(N, 1024)
    nt = T // tt

    def body(a_ref, b_ref, o_ref, acc_s):
        t = pl.program_id(2)

        @pl.when(t == 0)
        def _():
            acc_s[...] = jnp.zeros_like(acc_s)

        acc_s[...] += _dot_tn(a_ref[...].astype(BF16), b_ref[...])

        @pl.when(t == nt - 1)
        def _():
            o_ref[...] = acc_s[...].astype(BF16)

    return pl.pallas_call(
        body, name=name, grid=(K // tk, N // tn, nt),
        in_specs=[pl.BlockSpec((tt, tk), lambda k, n, t: (t, k)), pl.BlockSpec((tt, tn), lambda k, n, t: (t, n))],
        out_specs=pl.BlockSpec((tk, tn), lambda k, n, t: (k, n)),
        out_shape=jax.ShapeDtypeStruct((K, N), BF16),
        scratch_shapes=[pltpu.VMEM((tk, tn), F32)],
        compiler_params=_cp(("parallel", "parallel", "arbitrary")),
    )(a, b)


def _proj_bwd(x, dx1, norm_g, wt, pieces, ddt, ex):
    T = x.shape[0]
    tm = min(T, 1024)
    nk = OFF_DT // D + 1
    starts = [sum(a.shape[1] for a in pieces[:n]) // D for n in range(len(pieces))]
    ranges = [(s, s + a.shape[1] // D) for s, a in zip(starts, pieces)]
    assert ranges[-1][1] == nk - 1
    npc = len(pieces)

    def body(x_hbm, dx1_hbm, g_ref, w_ref, wdt_ref, *rest):
        piece_refs, ddt_ref, gx_ref, dng_ref = rest[:npc], rest[npc], rest[npc + 1], rest[npc + 2]
        acc_s, x_ref, dx1_ref, row_sems = rest[npc + 3:]
        i, k = pl.program_id(0), pl.program_id(1)
        rows = pl.ds(pl.multiple_of(i * tm, tm), tm)
        fetches = [pltpu.make_async_copy(x_hbm.at[rows], x_ref, row_sems.at[0]),
                   pltpu.make_async_copy(dx1_hbm.at[rows], dx1_ref, row_sems.at[1])]

        @pl.when((i == 0) & (k == 0))
        def _():
            dng_ref[...] = jnp.zeros_like(dng_ref)

        @pl.when(k == 0)
        def _():
            acc_s[...] = jnp.zeros_like(acc_s)

        @pl.when(k == 1)
        def _():
            for cp in fetches:
                cp.start()

        for ref, (lo, hi) in zip(piece_refs, ranges):
            @pl.when((k >= lo) & (k < hi))
            def _(ref=ref):
                acc_s[...] += _dot(ref[...], w_ref[...])

        @pl.when(k == nk - 1)
        def _():
            for cp in fetches:
                cp.wait()
            dh = acc_s[...] + _dot(ddt_ref[...], wdt_ref[...])
            xf = x_ref[...]
            r = lax.rsqrt(jnp.mean(xf * xf, axis=-1, keepdims=True) + EPS)
            xh = xf * r
            dng_ref[0:1, :] += jnp.sum(dh * xh, axis=0, keepdims=True)
            dxn = dh * g_ref[...]
            gx_ref[...] = dx1_ref[...] + r * (dxn - xh * jnp.mean(dxn * xh, axis=-1, keepdims=True))

    def piece_spec(lo, hi):
        return pl.BlockSpec((tm, D), lambda i, k: (i, jnp.clip(k - lo, 0, hi - lo - 1)))

    row = pl.BlockSpec((tm, D), lambda i, k: (i, 0))
    return _call(
        body, ex, name="proj_bwd", grid=(T // tm, nk),
        in_specs=[_ANY, _ANY, _full((1, D)), pl.BlockSpec((D, D), lambda i, k: (jnp.minimum(k, nk - 2), 0)),
                  pl.BlockSpec((DTW, D), lambda i, k: (OFF_DT // DTW, 0))]
        + [piece_spec(lo, hi) for lo, hi in ranges] + [pl.BlockSpec((tm, DTW), lambda i, k: (i, 0))],
        out_specs=[row, _full((8, D))],
        out_shape=[jax.ShapeDtypeStruct((T, D), F32), jax.ShapeDtypeStruct((8, D), F32)],
        scratch_shapes=[pltpu.VMEM((tm, D), F32), pltpu.VMEM((tm, D), F32), pltpu.VMEM((tm, D), F32), pltpu.SemaphoreType.DMA((2,))],
        args=(x, dx1, norm_g, wt, wt, *pieces, ddt))


def _elementwise_tile(R, C, limit=1 << 20):
    if R * C * 4 <= limit:
        return R, C
    rows = [t for t in range(16, R, 16) if R % t == 0 and t * C * 4 <= limit]
    if rows:
        return rows[-1], C
    cols = [t for t in range(128, C, 128) if C % t == 0 and R * t * 4 <= limit]
    return R, cols[-1]


def _adam_update(w, m, v, g):
    c1 = 1.0 - ADAM_B1 ** ADAM_STEP
    c2 = 1.0 - ADAM_B2 ** ADAM_STEP
    mm = ADAM_B1 * m + (1.0 - ADAM_B1) * g
    vv = ADAM_B2 * v + (1.0 - ADAM_B2) * (g * g)
    return -ADAM_LR * ((mm / c1) / (jnp.sqrt(vv / c2) + ADAM_EPS) + ADAM_WD * w), mm, vv


def _sum_parts(parts, name, ex):
    first = parts[0][0] if isinstance(parts[0], tuple) else parts[0]
    R, C = first.shape[-2:]
    tr, tc = _elementwise_tile(R, C)
    flat = pl.BlockSpec((tr, tc), lambda i, j: (i, j))
    slot = lambda k: pl.BlockSpec((None, tr, tc), lambda i, j: (k, i, j))
    specs = [slot(p[1]) if isinstance(p, tuple) else flat for p in parts]
    arrays = [p[0] if isinstance(p, tuple) else p for p in parts]

    def body(*refs):
        g = refs[0][...].astype(F32)
        for r in refs[1:-1]:
            g = g + r[...].astype(F32)
        refs[-1][...] = g

    (out,), got = _call(body, ex, name=name, grid=(R // tr, C // tc), in_specs=specs, out_specs=[flat],
                        out_shape=[jax.ShapeDtypeStruct((R, C), F32)], scratch_shapes=[], args=arrays)
    return out, got


def _adamw_tiles(w, m, v, g, name):
    R = w.shape[0]
    tr = 107
    assert R % tr == 0
    spec = pl.BlockSpec((tr, 8, 128), lambda i: (i, 0, 0))

    def body(w_ref, m_ref, v_ref, g_in, g_ref, d_ref, nm_ref, nv_ref):
        g = g_in[...]
        g_ref[...] = g
        d_ref[...], nm_ref[...], nv_ref[...] = _adam_update(w_ref[...], m_ref[...], v_ref[...], g)

    return pl.pallas_call(body, name=name, grid=(R // tr,), in_specs=[spec] * 4, out_specs=[spec] * 4,
                          out_shape=[jax.ShapeDtypeStruct(w.shape, F32)] * 4, compiler_params=_cp(("parallel",)))(w, m, v, g)


def _adamw(w, m, v, parts, name, part_row=0, block_rows=None):
    R, C = w.shape
    tr, tc = _elementwise_tile(R, C) if block_rows is None else (block_rows, C)
    assert R % tr == 0 and part_row % tr == 0
    first = part_row // tr
    n = len(parts)
    wspec = pl.BlockSpec((tr, tc), lambda i, j: (i, j))
    flat = pl.BlockSpec((tr, tc), lambda i, j: (first + i, j))
    slot = lambda k: pl.BlockSpec((None, tr, tc), lambda i, j: (k, first + i, j))
    part_specs = [slot(p[1]) if isinstance(p, tuple) else flat for p in parts]
    part_arrays = [p[0] if isinstance(p, tuple) else p for p in parts]

    def body(*refs):
        w_ref, m_ref, v_ref = refs[:3]
        g_ref, d_ref, nm_ref, nv_ref = refs[3 + n:]
        g = refs[3][...].astype(F32)
        for r in refs[4:3 + n]:
            g = g + r[...].astype(F32)
        g_ref[...] = g
        d_ref[...], nm_ref[...], nv_ref[...] = _adam_update(w_ref[...], m_ref[...], v_ref[...], g)

    return pl.pallas_call(
        body, name=name, grid=(R // tr, C // tc), in_specs=[wspec] * 3 + part_specs, out_specs=[wspec] * 4,
        out_shape=[jax.ShapeDtypeStruct(w.shape, F32)] * 4, compiler_params=_cp(("parallel", "parallel")),
    )(w, m, v, *part_arrays)


_SMALL_WIDE = {"ln_a_g": 0, "ln_a_b": 1, "final_g": 8, "ple_norm_g": 9, "ssm_norm_g": 16, "conv_b": 44}
_WIDE_CONV_W, _WIDE_ROWS = 32, 56
_WIDE_LOSS = 10
_SMALL_NARROW = {"w_s": (0, NG * CH, 128), "b_s": (512, NG, 128), "dt_bias": (520, 1, NH), "a_log": (521, 1, NH),
                 "d_skip": (522, 1, NH)}
_NARROW_ROWS = 528
_SMALL_SHAPES = {"norm_g": (1, D), "ln_a_g": (1, D), "ln_a_b": (1, D), "ple_norm_g": (1, D), "final_g": (1, D),
                 "ssm_norm_g": (1, DI), "conv_b": (1, CD), "w_s": (NG * CH, CH), "b_s": (NG, CH), "dt_bias": (1, NH),
                 "a_log": (1, NH), "d_skip": (1, NH)}


def _adamw_small(w, m, v, wide_all, narrow_all, norm_all):
    names = list(_SMALL_SHAPES)
    n = len(names)

    def body(*refs):
        wr, mr, vr = refs[:n], refs[n:2 * n], refs[2 * n:3 * n]
        wide_ref, narrow_ref, norm_ref = refs[3 * n:3 * n + 3]
        outs = refs[3 * n + 3:]
        gr, dr, nmr, nvr, cw_ref, loss_ref = outs[:n], outs[n:2 * n], outs[2 * n:3 * n], outs[3 * n:4 * n], outs[4 * n], outs[4 * n + 1]

        def total(ref, rows, lanes):
            acc = ref[0, rows, lanes]
            for d in range(1, NDEV):
                acc = acc + ref[d, rows, lanes]
            return acc

        for k, name in enumerate(names):
            if name in _SMALL_WIDE or name == "norm_g":
                for part in range(_SMALL_SHAPES[name][1] // D):
                    pack, r = (norm_ref, 0) if name == "norm_g" else (wide_ref, _SMALL_WIDE[name] + part)
                    cols = slice(part * D, (part + 1) * D)
                    g = total(pack, slice(r, r + 1), slice(None))
                    gr[k][:, cols] = g
                    dr[k][:, cols], nmr[k][:, cols], nvr[k][:, cols] = _adam_update(wr[k][:, cols], mr[k][:, cols], vr[k][:, cols], g)
            else:
                r, rows, lanes = _SMALL_NARROW[name]
                g = total(narrow_ref, slice(r, r + rows), slice(0, lanes))
                gr[k][...] = g
                dr[k][...], nmr[k][...], nvr[k][...] = _adam_update(wr[k][...], mr[k][...], vr[k][...], g)
        cw_ref[...] = total(wide_ref, slice(_WIDE_CONV_W, _WIDE_CONV_W + 12), slice(None))
        loss_ref[...] = total(wide_ref, slice(_WIDE_LOSS, _WIDE_LOSS + 1), slice(None))

    shapes = [jax.ShapeDtypeStruct(_SMALL_SHAPES[k], F32) for k in names]
    specs = [_full(_SMALL_SHAPES[k]) for k in names]
    res = pl.pallas_call(
        body, name="adamw_small", grid=(1,),
        in_specs=specs * 3 + [_full(wide_all.shape), _full(narrow_all.shape), _full(norm_all.shape)],
        out_specs=specs * 4 + [_full((12, D)), _full((1, D))],
        out_shape=shapes * 4 + [jax.ShapeDtypeStruct((12, D), F32), jax.ShapeDtypeStruct((1, D), F32)],
        compiler_params=_cp(("arbitrary",)),
    )(*[w[k] for k in names], *[m[k] for k in names], *[v[k] for k in names], wide_all, narrow_all, norm_all)
    groups = [dict(zip(names, res[q * n:(q + 1) * n])) for q in range(4)]
    return groups[0], groups[1], groups[2], groups[3], res[4 * n], res[4 * n + 1][0, 0]


def _dev_index(px, py, pc):
    return 4 * px + 2 * py + pc


def _mesh_position():
    return lax.axis_index("x"), lax.axis_index("y"), lax.axis_index("c")


def _gather_exchange(blocks):
    n = len(blocks)

    def make(ins, outs, sems):
        send_sems, recv_sems, local_sems = sems
        x, y, c = _mesh_position()
        me, sibling = (x, y, c), (x, y, 1 - c)
        chips = [(1 - x, y), (x, 1 - y), (1 - x, 1 - y)]

        def copy(a, k, block, to, src=None):
            dst = outs[a].at[_dev_index(*block)]
            return pltpu.make_async_remote_copy(src_ref=dst if src is None else src, dst_ref=dst, send_sem=send_sems.at[a, k],
                                                recv_sem=recv_sems.at[a, k], device_id=to, device_id_type=MESH)

        mine = [pltpu.make_async_copy(ins[a], outs[a].at[_dev_index(*me)], local_sems.at[a]) for a in range(n)]
        first = []
        for a in range(n):
            first.append(copy(a, 0, me, sibling, src=ins[a]))
            first += [copy(a, 1 + j, me, (*chip, c), src=ins[a]) for j, chip in enumerate(chips)]

        def start():
            for cp in mine + first:
                cp.start()

        def finish():
            passed = []
            for j, chip in enumerate(chips):
                for a in range(n):
                    copy(a, 1 + j, (*chip, c), me).wait_recv()
                    fwd = copy(a, 4 + j, (*chip, c), sibling)
                    fwd.start()
                    passed.append(fwd)
            for a in range(n):
                copy(a, 0, sibling, me).wait_recv()
                for j, chip in enumerate(chips):
                    copy(a, 4 + j, (*chip, 1 - c), me).wait_recv()
            for cp in first + passed:
                cp.wait_send()
            for cp in mine:
                cp.wait()

        return start, finish

    return _Exchange(list(blocks), [jax.ShapeDtypeStruct((NDEV,) + b.shape, b.dtype) for b in blocks],
                     [pltpu.SemaphoreType.DMA((n, 7)), pltpu.SemaphoreType.DMA((n, 7)), pltpu.SemaphoreType.DMA((n,))], make)


def _relay_gather_exchange(blocks):
    n = len(blocks)

    def make(ins, outs, sems):
        send_sems, recv_sems, local_sems = sems
        x, y, c = _mesh_position()
        me, sibling = (x, y, c), (x, y, 1 - c)
        x_nbr, y_nbr, diag = (1 - x, y), (x, 1 - y), (1 - x, 1 - y)
        relay_from = (jnp.where(c == 0, x, 1 - x), jnp.where(c == 0, 1 - y, y))
        relay_to = (jnp.where(c == 0, 1 - x, x), jnp.where(c == 0, y, 1 - y))

        def copy(a, k, block, to, src=None):
            dst = outs[a].at[_dev_index(*block)]
            return pltpu.make_async_remote_copy(src_ref=dst if src is None else src, dst_ref=dst, send_sem=send_sems.at[a, k],
                                                recv_sem=recv_sems.at[a, k], device_id=to, device_id_type=MESH)

        mine = [pltpu.make_async_copy(ins[a], outs[a].at[_dev_index(*me)], local_sems.at[a]) for a in range(n)]
        first = []
        for a in range(n):
            first += [copy(a, 0, me, sibling, src=ins[a]), copy(a, 1, me, (*x_nbr, c), src=ins[a]), copy(a, 2, me, (*y_nbr, c), src=ins[a])]

        def start():
            for cp in mine + first:
                cp.start()

        def finish():
            later = []
            for a in range(n):
                copy(a, 1, (*x_nbr, c), me).wait_recv()
                copy(a, 2, (*y_nbr, c), me).wait_recv()
                later.append(copy(a, 3, (*relay_from, c), (*relay_to, c)))
                later += [copy(a, 4, (*x_nbr, c), sibling), copy(a, 5, (*y_nbr, c), sibling)]
                for cp in later[-3:]:
                    cp.start()
            for a in range(n):
                copy(a, 3, (*diag, c), me).wait_recv()
                later.append(copy(a, 6, (*diag, c), sibling))
                later[-1].start()
            for a in range(n):
                copy(a, 0, sibling, me).wait_recv()
                for k, chip in ((4, x_nbr), (5, y_nbr), (6, diag)):
                    copy(a, k, (*chip, 1 - c), me).wait_recv()
            for cp in first + later:
                cp.wait_send()
            for cp in mine:
                cp.wait()

        return start, finish

    return _Exchange(list(blocks), [jax.ShapeDtypeStruct((NDEV,) + b.shape, b.dtype) for b in blocks],
                     [pltpu.SemaphoreType.DMA((n, 7)), pltpu.SemaphoreType.DMA((n, 7)), pltpu.SemaphoreType.DMA((n,))], make)


def _combine(*exchanges):
    def make(ins, outs, sems):
        pairs = []
        for e in exchanges:
            ni, no, ns = len(e.arrays), len(e.out_shape), len(e.sems)
            pairs.append(e.make(ins[:ni], outs[:no], sems[:ns]))
            ins, outs, sems = ins[ni:], outs[no:], sems[ns:]

        def start():
            for s, _ in pairs:
                s()

        def finish():
            for _, f in pairs:
                f()

        return start, finish

    return _Exchange(sum((list(e.arrays) for e in exchanges), []), sum((list(e.out_shape) for e in exchanges), []),
                     sum((list(e.sems) for e in exchanges), []), make)


def _start_wait_all(copies, local=()):
    def start():
        for cp in list(local) + list(copies):
            cp.start()

    def finish():
        for cp in copies:
            cp.wait()
        for cp in local:
            cp.wait()

    return start, finish


def _no_exchange():
    return _Exchange([], [], [], lambda ins, outs, sems: (lambda: None, lambda: None))


def _direct_exchange(grads):
    n = len(grads)

    def make(ins, outs, sems):
        send_sems, recv_sems = sems
        x, y, c = _mesh_position()
        copies = []
        for a in range(n):
            for r in range(1, NDEV):
                peer = (x ^ ((r >> 2) & 1), y ^ ((r >> 1) & 1), c ^ (r & 1))
                copies.append(pltpu.make_async_remote_copy(
                    src_ref=ins[a].at[_dev_index(*peer)], dst_ref=outs[a].at[r - 1], send_sem=send_sems.at[a, r - 1],
                    recv_sem=recv_sems.at[a, r - 1], device_id=peer, device_id_type=MESH))
        return _start_wait_all(copies)

    return _Exchange(list(grads), [jax.ShapeDtypeStruct((NDEV - 1,) + g.shape[1:], g.dtype) for g in grads],
                     [pltpu.SemaphoreType.DMA((n, NDEV - 1)), pltpu.SemaphoreType.DMA((n, NDEV - 1))], make)


def _direct_gather_exchange(smalls):
    n = len(smalls)

    def make(ins, outs, sems):
        send_sems, recv_sems, local_sems = sems
        x, y, c = _mesh_position()
        copies, local = [], []
        for a in range(n):
            slot = outs[a].at[_dev_index(x, y, c)]
            local.append(pltpu.make_async_copy(ins[a], slot, local_sems.at[a]))
            for r in range(1, NDEV):
                peer = (x ^ ((r >> 2) & 1), y ^ ((r >> 1) & 1), c ^ (r & 1))
                copies.append(pltpu.make_async_remote_copy(src_ref=ins[a], dst_ref=slot, send_sem=send_sems.at[a, r - 1],
                                                           recv_sem=recv_sems.at[a, r - 1], device_id=peer, device_id_type=MESH))
        return _start_wait_all(copies, local=local)

    return _Exchange(list(smalls), [jax.ShapeDtypeStruct((NDEV,) + s.shape, s.dtype) for s in smalls],
                     [pltpu.SemaphoreType.DMA((n, 7)), pltpu.SemaphoreType.DMA((n, 7)), pltpu.SemaphoreType.DMA((n,))], make)


_W_IN_ROWS = {"u": (0, 1024), "v": (1024, 2048), "za": (2048, 3072), "zb": (3072, 5120), "xbc": (5120, 8192),
              "dt": (8192, 8224), "ga": (8224, 9248), "gb": (9248, 10272)}
_PROJ_ORDER = ("xbc", "zb", "u", "v", "za", "ga", "gb", "dt")


def _w_in_t_rows(wt):
    z = jnp.zeros((NP - NIN, wt.shape[1]), wt.dtype)
    return jnp.concatenate([wt[slice(*_W_IN_ROWS[n])] for n in _PROJ_ORDER] + [z], axis=0)


_WEIGHTS = ["norm_g", "w_in", "ln_a_g", "ln_a_b", "w_s", "b_s", "conv_w", "conv_b", "dt_bias", "a_log", "d_skip", "ssm_norm_g",
            "w_oa", "w_ob", "w_out", "ple_norm_g", "w_pg", "w_ple", "final_g"]


def _rows_pack(d, dtype):
    return jnp.concatenate([d["w_oa"].reshape(128, D), d["w_ob"].reshape(256, D), d["w_out"].reshape(128, D),
                            d["w_pg"].reshape(128, D), d["w_ple"].reshape(32, D)], axis=0).astype(dtype)


def kernel(x, p, norm_g, w_in, ln_a_g, ln_a_b, w_s, b_s, conv_w, conv_b, dt_bias, a_log, d_skip, ssm_norm_g, w_oa, w_ob, w_out, ple_norm_g, w_pg, w_ple, final_g, loss_target, m_norm_g, m_w_in, m_ln_a_g, m_ln_a_b, m_w_s, m_b_s, m_conv_w, m_conv_b, m_dt_bias, m_a_log, m_d_skip, m_ssm_norm_g, m_w_oa, m_w_ob, m_w_out, m_ple_norm_g, m_w_pg, m_w_ple, m_final_g, v_norm_g, v_w_in, v_ln_a_g, v_ln_a_b, v_w_s, v_b_s, v_conv_w, v_conv_b, v_dt_bias, v_a_log, v_d_skip, v_ssm_norm_g, v_w_oa, v_w_ob, v_w_out, v_ple_norm_g, v_w_pg, v_w_ple, v_final_g):
    args = dict(locals())
    w = {n: args[n] for n in _WEIGHTS}
    m = {n: args["m_" + n] for n in _WEIGHTS}
    v = {n: args["v_" + n] for n in _WEIGHTS}
    T = x.shape[1]
    xi, yi, ci = lax.axis_index("x"), lax.axis_index("y"), lax.axis_index("c")
    me = 4 * xi + 2 * yi + ci
    x2, p2, tgt = x.reshape(T, D), p.reshape(T, PLE), loss_target.reshape(T, D)

    norm_g2 = w["norm_g"].reshape(1, D)
    ws = jnp.where(jnp.tril(jnp.ones((CH, CH), bool))[None], w["w_s"].reshape(NG, CH, CH), 0.0).astype(BF16)
    wst = jnp.transpose(ws, (0, 2, 1))
    bst = jnp.broadcast_to(w["b_s"].reshape(NG, CH, 1), (NG, CH, 256))
    ln_g, ln_b = w["ln_a_g"].reshape(1, D), w["ln_a_b"].reshape(1, D)
    cb = w["conv_b"].reshape(1, CD)
    pad32 = lambda a: jnp.pad(a.reshape(1, NH), ((0, 0), (0, DTW - NH)))
    dtb, alog = pad32(w["dt_bias"]), pad32(w["a_log"])
    dskx = jnp.repeat(w["d_skip"].reshape(NH), HD).reshape(1, DI)
    sg = w["ssm_norm_g"].reshape(1, DI)
    ple_g, fin_g = w["ple_norm_g"].reshape(1, D), w["final_g"].reshape(1, D)
    e128 = (jnp.arange(DTW)[:, None] == (jnp.arange(DI)[None, :] // HD)).astype(BF16)
    et128 = e128.T
    gsel = ((jnp.arange(D)[:, None] // 256) == jnp.arange(128)[None, :]).astype(BF16)

    w_in_t = lambda a: jnp.transpose(a.reshape(D, WSH))
    (a_all,) = _run_exchange(_relay_gather_exchange([w_in_t(w["w_in"]).astype(BF16)]), "all_gather_w_in")
    w_in_full_t = a_all.reshape(NIN, D)
    wp = _w_in_t_rows(w_in_full_t)
    (proj, dtr, h), (r_all, cw_all) = _proj_fwd(x2, norm_g2, wp, _gather_exchange([_rows_pack(w, BF16), w["conv_w"].reshape(4, CD // NDEV)]))
    f_oa = r_all[:, R_OA:R_OB].reshape(D, D)
    f_ob = r_all[:, R_OB:R_OUT].reshape(DI, D)
    f_out = r_all[:, R_OUT:R_PG].reshape(D, D)
    f_pg = r_all[:, R_PG:R_PLE].reshape(D, D)
    f_ple = jnp.transpose(r_all[:, R_PLE:R_ROWS].reshape(NDEV, PLE, 128), (1, 0, 2)).reshape(PLE, D)
    cw = jnp.transpose(cw_all, (1, 0, 2)).reshape(4, CD)

    y, yb, hprev, pre_all = _ssd_fwd(proj, dtr, cw, cb, dtb, alog, dskx, sg, e128)
    dx1, dx1b, ya, mg, hp, dpre, dpe, doa, dob, dya, dyb, dgab, acc = _merge(
        x2, yb, proj, p2, tgt, (ln_g, ln_b, ws, bst), f_oa, f_ob, f_out, f_pg, f_ple, ple_g, fin_g)

    gple = jnp.transpose(_wgrad(p2, dpe, "wgrad_ple").reshape(PLE, NDEV, 128), (1, 0, 2)).reshape(NDEV, 32, D)
    gr = jnp.concatenate([_wgrad(ya, doa, "wgrad_oa").reshape(NDEV, 128, D), _wgrad(yb, dob, "wgrad_ob").reshape(NDEV, 256, D),
                          _wgrad(mg, dx1b, "wgrad_out").reshape(NDEV, 128, D), _wgrad(hp, dpre, "wgrad_pg").reshape(NDEV, 128, D),
                          gple], axis=1)
    (duvz, dws, dbs, dln), _ = _gmlp_bwd(proj, dya, ln_g, ln_b, ws, wst, bst, gsel, _no_exchange())
    (dxz, ddt, dcw, dsm, dsg), (rr,) = _ssd_bwd(proj, pre_all, dtr, y, dyb, hprev, cw, dtb, alog, dskx, sg, e128, et128, _direct_exchange([gr]))

    g_xz = _wgrad(dxz, h, "wgrad_xbc_zb")
    g_w_in_t = jnp.concatenate([_wgrad(duvz, h, "wgrad_u_v_za"), g_xz[CD:], g_xz[:CD], _wgrad(ddt, h, "wgrad_dt")[:NH],
                                _wgrad(dgab, h, "wgrad_ga_gb")], axis=0)
    ga = g_w_in_t.reshape(NDEV, WSH, D)
    wide = jnp.concatenate([dln, acc, dsg.reshape(16, D), dcw.reshape(24, D)], axis=0)
    narrow = jnp.concatenate([dws.reshape(NG * CH, CH), jnp.pad(dbs[:, :NG].T, ((0, 8 - NG), (0, 0))), dsm], axis=0)
    (gx, dng), (ra, wide_all, narrow_all) = _proj_bwd(x2, dx1, norm_g2, wp, [dxz, duvz, dgab], ddt,
                                                      _combine(_direct_exchange([ga]), _direct_gather_exchange([wide, narrow])))
    ga_own = lax.dynamic_index_in_dim(ga, me, 0, keepdims=False)
    gr_own = lax.dynamic_index_in_dim(gr, me, 0, keepdims=False)

    out_g, out_d, out_m, out_v = {}, {}, {}, {}
    outs = (out_g, out_d, out_m, out_v)
    tiles = lambda a: jnp.transpose(a.reshape(8, 128, WSH), (2, 0, 1))
    g2, (norm_all,) = _sum_parts([ga_own] + [(ra, k) for k in range(NDEV - 1)], "sum_w_in", _direct_gather_exchange([dng]))
    res = _adamw_tiles(tiles(w["w_in"]), tiles(m["w_in"]), tiles(v["w_in"]), g2.reshape(WSH, 8, 128), "adamw_w_in")
    for dst, val in zip(outs, res):
        dst["w_in"] = jnp.transpose(val, (1, 2, 0)).reshape(1, D, WSH)
    parts_r = [gr_own] + [(rr, k) for k in range(NDEV - 1)]
    for name, row, rows in (("w_oa", R_OA, 128), ("w_ob", R_OB, 256), ("w_out", R_OUT, 128), ("w_pg", R_PG, 128)):
        res = _adamw(w[name].reshape(rows, D), m[name].reshape(rows, D), v[name].reshape(rows, D), parts_r, "adamw_" + name,
                     part_row=row, block_rows=128)
        for dst, val in zip(outs, res):
            dst[name] = val.reshape(1, rows, D)
    res = _adamw(w["w_ple"].reshape(32, D), m["w_ple"].reshape(32, D), v["w_ple"].reshape(32, D), parts_r, "adamw_w_ple",
                 part_row=R_PLE, block_rows=32)
    for dst, val in zip(outs, res):
        dst["w_ple"] = val.reshape(1, PLE, 128)
    two_d = lambda d: {n: d[n].reshape(_SMALL_SHAPES[n]) for n in _SMALL_SHAPES}
    *res, g_cw_wide, loss = _adamw_small(two_d(w), two_d(m), two_d(v), wide_all, narrow_all, norm_all)
    for dst, val in zip(outs, res):
        dst.update({n: val[n].reshape(w[n].shape) for n in _SMALL_SHAPES})
    g_cw = lax.dynamic_slice_in_dim(g_cw_wide.reshape(4, CD), me * (CD // NDEV), CD // NDEV, axis=1).reshape(12, 128)
    res = _adamw(w["conv_w"].reshape(12, 128), m["conv_w"].reshape(12, 128), v["conv_w"].reshape(12, 128), [g_cw], "adamw_conv_w")
    for dst, val in zip((out_g, out_d, out_m, out_v), res):
        dst["conv_w"] = val.reshape(1, 4, CD // NDEV)

    return (loss, gx.reshape(1, T, D), *[out_g[n] for n in _WEIGHTS], *[out_d[n] for n in _WEIGHTS],
            *[out_m[n] for n in _WEIGHTS], *[out_v[n] for n in _WEIGHTS])
```

```python
import functools
import math
from typing import Callable, NamedTuple

import jax
import jax.numpy as jnp
from jax import lax
from jax.experimental import pallas as pl
from jax.experimental.pallas import tpu as pltpu

F32 = jnp.float32
BF16 = jnp.bfloat16
MESH = pl.DeviceIdType.MESH

D = 1024
DI = 2048
CD = 3072
NH = 32
HD = 64
NST = 128
NG = 4
CH = 128
PLE = 256
NIN = 10272
NDEV = 8
WSH = NIN // NDEV
EPS = 1e-6
OFF_XBC, OFF_ZB, OFF_U, OFF_V, OFF_ZA, OFF_GA, OFF_GB, OFF_DT = 0, 3072, 5120, 6144, 7168, 8192, 9216, 10240
NP = 10368
DTW = 128
R_OA, R_OB, R_OUT, R_PG, R_PLE, R_ROWS = 0, 128, 384, 512, 640, 672

ADAM_LR, ADAM_B1, ADAM_B2, ADAM_EPS, ADAM_WD, ADAM_STEP = 0.001, 0.9, 0.999, 1e-08, 0.01, 10

V7X_VMEM_LIMIT = 56 * 1024 * 1024


def _cp(sem=None):
    return pltpu.CompilerParams(dimension_semantics=sem, vmem_limit_bytes=V7X_VMEM_LIMIT)


def _dot(a, b, prec=None):
    return jnp.dot(a, b, preferred_element_type=F32, precision=prec)


def _dot_nt(a, b, prec=None):
    return lax.dot_general(a, b, (((1,), (1,)), ((), ())), preferred_element_type=F32, precision=prec)


def _dot_tn(a, b, prec=None):
    return lax.dot_general(a, b, (((0,), (0,)), ((), ())), preferred_element_type=F32, precision=prec)


def _sigmoid(x):
    return 0.5 * jnp.tanh(0.5 * x) + 0.5


_GELU_C = math.sqrt(2.0 / math.pi)
_GELU_K = 0.044715


def _gelu_and_grad(x):
    x2 = x * x
    t = jnp.tanh(x * (_GELU_C + (_GELU_C * _GELU_K) * x2))
    hx = 0.5 * x
    g = hx + hx * t
    dg = (0.5 + 0.5 * t) + hx * (1.0 - t * t) * (_GELU_C + (3.0 * _GELU_C * _GELU_K) * x2)
    return g, dg


def _gelu(x):
    hx = 0.5 * x
    return hx + hx * jnp.tanh(x * (_GELU_C + (_GELU_C * _GELU_K) * (x * x)))


def _softplus(x):
    return jnp.maximum(x, 0.0) + jnp.log(1.0 + jnp.exp(-jnp.abs(x)))


def _full(shape):
    n = len(shape)
    return pl.BlockSpec(shape, lambda *_: (0,) * n)


_ANY = pl.BlockSpec(memory_space=pl.ANY)


class _Exchange(NamedTuple):
    arrays: list
    out_shape: list
    sems: list
    make: Callable


def _call(body, ex, *, name, grid, in_specs, out_specs, out_shape, scratch_shapes, args):
    ki, ko, ks = len(in_specs), len(out_specs), len(scratch_shapes)
    ei, eo = len(ex.arrays), len(ex.out_shape)
    last = [g - 1 for g in grid]

    def full_body(*refs):
        r = list(refs)
        ins, eins, r = r[:ki], r[ki:ki + ei], r[ki + ei:]
        outs, eouts, r = r[:ko], r[ko:ko + eo], r[ko + eo:]
        scr, esems = r[:ks], r[ks:]
        start, finish = ex.make(eins, eouts, esems)
        ids = [pl.program_id(a) for a in range(len(grid))]
        is_first = functools.reduce(lambda p, q: p & q, [i == 0 for i in ids])
        is_last = functools.reduce(lambda p, q: p & q, [i == l for i, l in zip(ids, last)])
        pl.when(is_first)(start)
        body(*ins, *outs, *scr)
        pl.when(is_last)(finish)

    res = pl.pallas_call(
        full_body, name=name, grid=grid, in_specs=list(in_specs) + [_ANY] * ei, out_specs=list(out_specs) + [_ANY] * eo,
        out_shape=list(out_shape) + list(ex.out_shape), scratch_shapes=list(scratch_shapes) + list(ex.sems),
        compiler_params=_cp(("arbitrary",) * len(grid)),
    )(*args, *ex.arrays)
    return res[:ko], res[ko:]


def _run_exchange(ex, name):
    ni, no = len(ex.arrays), len(ex.out_shape)

    def body(*refs):
        start, finish = ex.make(refs[:ni], refs[ni:ni + no], refs[ni + no:])
        start()
        finish()

    return pl.pallas_call(body, name=name, in_specs=[_ANY] * ni, out_specs=[_ANY] * no, out_shape=list(ex.out_shape),
                          scratch_shapes=list(ex.sems))(*ex.arrays)


def _proj_fwd(x, norm_g, wp, ex):
    T = x.shape[0]
    tm, tn = min(T, 1024), 2048
    nj = OFF_DT // tn
    assert OFF_DT % tn == 0 and OFF_DT + DTW == NP

    def body(x_ref, g_ref, w_ref, wdt_ref, proj_ref, dt_ref, h_ref, hs_ref):
        j = pl.program_id(1)

        @pl.when(j == 0)
        def _():
            xf = x_ref[...]
            r = lax.rsqrt(jnp.mean(xf * xf, axis=-1, keepdims=True) + EPS)
            h = (xf * r * g_ref[...]).astype(BF16)
            hs_ref[...] = h
            h_ref[...] = h

        proj_ref[...] = _dot_nt(hs_ref[...], w_ref[...]).astype(BF16)

        @pl.when(j == nj - 1)
        def _():
            dt_ref[...] = _dot_nt(hs_ref[...], wdt_ref[...])

    return _call(
        body, ex, name="proj_fwd", grid=(T // tm, nj),
        in_specs=[pl.BlockSpec((tm, D), lambda i, j: (i, 0)), _full((1, D)), pl.BlockSpec((tn, D), lambda i, j: (j, 0)),
                  pl.BlockSpec((DTW, D), lambda i, j: (OFF_DT // DTW, 0))],
        out_specs=[pl.BlockSpec((tm, tn), lambda i, j: (i, j)), pl.BlockSpec((tm, DTW), lambda i, j: (i, 0)),
                   pl.BlockSpec((tm, D), lambda i, j: (i, 0))],
        out_shape=[jax.ShapeDtypeStruct((T, OFF_DT), BF16), jax.ShapeDtypeStruct((T, DTW), F32), jax.ShapeDtypeStruct((T, D), BF16)],
        scratch_shapes=[pltpu.VMEM((tm, D), BF16)], args=(x, norm_g, wp, wp))


def _gmlp_tile():
    return 512


def _gmlp_fwd_tile(u_ref, v_ref, z_ref, lg_ref, lb_ref, ws_ref, bs_ref, ya_ref, vn_s):
    tm = u_ref.shape[0]
    vg = _gelu(v_ref[...].astype(F32))
    mu = jnp.mean(vg, axis=-1, keepdims=True)
    xc = vg - mu
    rstd = lax.rsqrt(jnp.mean(xc * xc, axis=-1, keepdims=True) + EPS)
    vn_s[...] = (xc * rstd * lg_ref[...] + lb_ref[...]).astype(BF16)
    for c in range(tm // CH):
        rs = slice(c * CH, (c + 1) * CH)
        for g in range(NG):
            cs_ = slice(g * 256, (g + 1) * 256)
            sv = _dot(ws_ref[g], vn_s[rs, cs_]) + bs_ref[g]
            z = z_ref[rs, cs_].astype(F32)
            ya_ref[rs, cs_] = (_gelu(u_ref[rs, cs_].astype(F32)) * sv * (z * _sigmoid(z))).astype(BF16)


def _gmlp_bwd(proj, dya, ln_g, ln_b, ws, wst, bst, gsel, ex):
    T = proj.shape[0]
    tm = min(T, _gmlp_tile())

    def body(u_ref, v_ref, z_ref, dy_ref, lg_ref, lb_ref, ws_ref, wst_ref, bs_ref, gsel_ref,
             d_ref, dws_ref, dbs_ref, dln_ref, vn_s, dsv_s, dvn_s):
        du_ref, dv_ref, dz_ref = d_ref.at[:, 0:D], d_ref.at[:, D:2 * D], d_ref.at[:, 2 * D:3 * D]
        @pl.when(pl.program_id(0) == 0)
        def _():
            dws_ref[...] = jnp.zeros_like(dws_ref)
            dbs_ref[...] = jnp.zeros_like(dbs_ref)
            dln_ref[...] = jnp.zeros_like(dln_ref)

        vg, dvg_dv = _gelu_and_grad(v_ref[...].astype(F32))
        mu = jnp.mean(vg, axis=-1, keepdims=True)
        xc = vg - mu
        rstd = lax.rsqrt(jnp.mean(xc * xc, axis=-1, keepdims=True) + EPS)
        vhat = xc * rstd
        vn_s[...] = (vhat * lg_ref[...] + lb_ref[...]).astype(BF16)
        ri = lax.broadcasted_iota(jnp.int32, (CH, CH), 0)
        ci = lax.broadcasted_iota(jnp.int32, (CH, CH), 1)
        tril = (ri >= ci).astype(F32)
        for c in range(tm // CH):
            rs = slice(c * CH, (c + 1) * CH)
            for g in range(NG):
                cs_ = slice(g * 256, (g + 1) * 256)
                vn = vn_s[rs, cs_]
                sv = _dot(ws_ref[g], vn) + bs_ref[g]
                z = z_ref[rs, cs_].astype(F32)
                sz = _sigmoid(z)
                ug, dug_du = _gelu_and_grad(u_ref[rs, cs_].astype(F32))
                dy = dy_ref[rs, cs_].astype(F32)
                zs = z * sz
                t = dy * zs
                dsv_f = dy * sv
                du_ref[rs, cs_] = (zs * dsv_f * dug_du).astype(BF16)
                dz_ref[rs, cs_] = (dsv_f * ug * (sz + zs * (1.0 - sz))).astype(BF16)
                dsv = (t * ug).astype(BF16)
                dsv_s[rs, cs_] = dsv
                dvn_s[rs, cs_] = _dot(wst_ref[g], dsv)
                dws_ref[g] += _dot_nt(dsv, vn) * tril
            dbs_ref[...] += _dot(dsv_s[rs, :], gsel_ref[...])
        dvn = dvn_s[...]
        dln_ref[0:1, :] += jnp.sum(dvn * vhat, axis=0, keepdims=True)
        dln_ref[1:2, :] += jnp.sum(dvn, axis=0, keepdims=True)
        dvh = dvn * lg_ref[...]
        dvg = rstd * (dvh - jnp.mean(dvh, axis=-1, keepdims=True) - vhat * jnp.mean(dvh * vhat, axis=-1, keepdims=True))
        dv_ref[...] = (dvg * dvg_dv).astype(BF16)

    blk = lambda off: pl.BlockSpec((tm, D), lambda i: (i, off // D))
    row = pl.BlockSpec((tm, D), lambda i: (i, 0))
    return _call(
        body, ex, name="gmlp_bwd", grid=(T // tm,),
        in_specs=[blk(OFF_U), blk(OFF_V), blk(OFF_ZA), row, _full((1, D)), _full((1, D)), _full((NG, CH, CH)),
                  _full((NG, CH, CH)), _full((NG, CH, 256)), _full((D, 128))],
        out_specs=[pl.BlockSpec((tm, 3 * D), lambda i: (i, 0)), _full((NG, CH, CH)), _full((CH, 128)), _full((8, D))],
        out_shape=[jax.ShapeDtypeStruct((T, 3 * D), BF16),
                   jax.ShapeDtypeStruct((NG, CH, CH), F32), jax.ShapeDtypeStruct((CH, 128), F32), jax.ShapeDtypeStruct((8, D), F32)],
        scratch_shapes=[pltpu.VMEM((tm, D), BF16), pltpu.VMEM((tm, D), BF16), pltpu.VMEM((tm, D), F32)],
        args=(proj, proj, proj, dya, ln_g, ln_b, ws, wst, bst, gsel))


def _zb_cols(zb0_ref, zb1_ref, cols):
    ref = zb0_ref if cols.start < D else zb1_ref
    return ref[:, cols.start % D:cols.start % D + (cols.stop - cols.start)]


def _shift_matrix(down):
    t = jnp.arange(CH)[None, :, None]
    j = jnp.arange(1, 4)[:, None, None]
    col = jnp.arange(2 * CH)[None, None, :]
    src = CH + t - j if down else t + j
    return (col == src).astype(BF16).reshape(3 * CH, 2 * CH)


def _conv_pre(x, moved, cw_ref, cb_ref):
    pre = cb_ref[...] + cw_ref[3:4, :] * x
    for j in (1, 2, 3):
        pre = pre + cw_ref[3 - j:4 - j, :] * moved[(j - 1) * CH:j * CH]
    return pre


def _split_dot(x, w, parts, w_left=False):
    acc, r = None, x
    for k in range(parts):
        hi = r.astype(BF16)
        d = _dot(w, hi) if w_left else _dot(hi, w)
        acc = d if acc is None else acc + d
        if k + 1 < parts:
            r = r - hi.astype(F32)
    return acc


def _chunk_decays(dt, alog_ref, e_ref, cs_s, cst_s, csx_s):
    a = -jnp.exp(alog_ref[...])
    ri = lax.broadcasted_iota(jnp.int32, (CH, CH), 0)
    ci = lax.broadcasted_iota(jnp.int32, (CH, CH), 1)
    tril = ri >= ci
    cs = _split_dot(dt * a, tril.astype(BF16), 3, w_left=True)
    cs_s[...] = cs
    cst_s[...] = cs.T
    csx_s[...] = _split_dot(cs, e_ref[...], 3)
    return a, tril, ri, ci


def _lmat(cst_s, h, tril):
    rowb = jnp.broadcast_to(cst_s[h:h + 1, :], (CH, CH))
    return jnp.exp(jnp.where(tril, rowb.T - rowb, -jnp.inf))


def _head_pair_rows(v, lane):
    return jnp.concatenate([jnp.where(lane < HD, v, 0.0), jnp.where(lane < HD, 0.0, v)], axis=0).astype(BF16)


def _ssd_fwd(proj, dtr, cw, cb, dtb, alog, dskx, sg, e128):
    T = proj.shape[0]
    nc = T // CH

    def body(xbc_ref, zb0_ref, zb1_ref, dt_ref, cw_ref, cb_ref, dtb_ref, alog_ref, dx_ref, sg_ref, e_ref, shift_ref,
             y_ref, yb_ref, hp_ref, pre_ref, xx_s, h_s, cs_s, cst_s, csx_s, yz_s):
        @pl.when(pl.program_id(0) == 0)
        def _():
            xx_s[...] = jnp.zeros_like(xx_s)
            h_s[...] = jnp.zeros_like(h_s)

        xx_s[CH:, :] = xbc_ref[...]
        moved = _dot(shift_ref[...], xx_s[...])
        xx_s[CH - 16:CH, :] = xbc_ref[CH - 16:, :]
        x = xbc_ref[...].astype(F32)
        pre = _conv_pre(x, moved, cw_ref, cb_ref)
        pre_ref[...] = pre
        xc = pre * _sigmoid(pre)
        dt = _softplus(dt_ref[...] + dtb_ref[...])
        a, tril, _, lane = _chunk_decays(dt, alog_ref, e_ref, cs_s, cst_s, csx_s)
        dt_x = _split_dot(dt, e_ref[...], 2)
        cs_last_x = csx_s[CH - 1:CH, :]
        hp_ref[0] = h_s[...]
        for g in range(NG):
            gs = slice(g * 512, (g + 1) * 512)
            bg = xc[:, DI + g * NST:DI + (g + 1) * NST].astype(BF16)
            cg = xc[:, DI + 512 + g * NST:DI + 512 + (g + 1) * NST].astype(BF16)
            cbm = _dot_nt(cg, bg)
            xg = xc[:, gs]
            xdt = xg * dt_x[:, gs]
            hprev = h_s[:, gs]
            csx = csx_s[:, gs]
            yoff = _dot(cg, hprev.astype(BF16)) * jnp.exp(csx)
            st = _dot_tn(bg, (xdt * jnp.exp(cs_last_x[:, gs] - csx)).astype(BF16))
            h_s[:, gs] = jnp.exp(cs_last_x[:, gs]) * hprev + st
            ssq = jnp.zeros((CH, 1), F32)
            for q in range(4):
                h0 = g * 8 + 2 * q
                ps = slice(q * 128, (q + 1) * 128)
                cols = slice(g * 512 + q * 128, g * 512 + (q + 1) * 128)
                m01 = jnp.concatenate([cbm * _lmat(cst_s, h0, tril), cbm * _lmat(cst_s, h0 + 1, tril)], axis=1).astype(BF16)
                yq = _dot(m01, _head_pair_rows(xdt[:, ps], lane)) + yoff[:, ps] + xg[:, ps] * dx_ref[:, cols]
                y_ref[:, cols] = yq
                z = _zb_cols(zb0_ref, zb1_ref, cols).astype(F32)
                yz = yq * z * _sigmoid(z)
                yz_s[:, cols] = yz
                ssq = ssq + jnp.sum(yz * yz, axis=1, keepdims=True)
            rg = lax.rsqrt(ssq * (1.0 / 512.0) + EPS)
            yb_ref[:, gs] = (yz_s[:, gs] * rg * sg_ref[:, gs]).astype(BF16)

    return pl.pallas_call(
        body, name="ssd_fwd", grid=(nc,),
        in_specs=[pl.BlockSpec((CH, CD), lambda c: (c, OFF_XBC // CD)), pl.BlockSpec((CH, D), lambda c: (c, OFF_ZB // D)),
                  pl.BlockSpec((CH, D), lambda c: (c, OFF_ZB // D + 1)),
                  pl.BlockSpec((CH, DTW), lambda c: (c, 0)), _full((4, CD)), _full((1, CD)), _full((1, DTW)),
                  _full((1, DTW)), _full((1, DI)), _full((1, DI)), _full((DTW, DI)), _full((3 * CH, 2 * CH))],
        out_specs=[pl.BlockSpec((CH, DI), lambda c: (c, 0)), pl.BlockSpec((CH, DI), lambda c: (c, 0)),
                   pl.BlockSpec((1, NST, DI), lambda c: (c, 0, 0)), pl.BlockSpec((CH, CD), lambda c: (c, 0))],
        out_shape=[jax.ShapeDtypeStruct((T, DI), F32), jax.ShapeDtypeStruct((T, DI), BF16),
                   jax.ShapeDtypeStruct((nc, NST, DI), F32), jax.ShapeDtypeStruct((T, CD), F32)],
        scratch_shapes=[pltpu.VMEM((2 * CH, CD), BF16), pltpu.VMEM((NST, DI), F32), pltpu.VMEM((CH, CH), F32),
                        pltpu.VMEM((CH, CH), F32), pltpu.VMEM((CH, DI), F32), pltpu.VMEM((CH, DI), F32)],
        compiler_params=_cp(("arbitrary",)),
    )(proj, proj, proj, dtr, cw, cb, dtb, alog, dskx, sg, e128, _shift_matrix(down=True))


def _ssd_bwd(proj, pre_all, dtr, y, dyb, hprev_all, cw, dtb, alog, dskx, sg, e128, et128, ex):
    T = proj.shape[0]
    nc = T // CH

    def body(xbc_ref, pre_ref, zb0_ref, zb1_ref, dt_ref, y_ref, dyb_ref, hp_ref, cw_ref, dtb_ref, alog_ref, dx_ref, sg_ref,
             e_ref, et_ref, shift_ref, d_ref, ddt_ref, dcw_ref, dsm_ref, dsg_ref,
             g_s, dd_s, cs_s, cst_s, csx_s, dy_s, dxdt_s, dxs_s, dsd_s, dxc_s, gh_s):
        dxbc_ref, dzb_ref = d_ref.at[:, 0:CD], d_ref.at[:, CD:CD + DI]
        i = pl.program_id(0)

        @pl.when(i == 0)
        def _():
            g_s[...] = jnp.zeros_like(g_s)
            dd_s[...] = jnp.zeros_like(dd_s)
            dcw_ref[...] = jnp.zeros_like(dcw_ref)
            dsm_ref[...] = jnp.zeros_like(dsm_ref)
            dsg_ref[...] = jnp.zeros_like(dsg_ref)

        pre = pre_ref[...]
        sp = _sigmoid(pre)
        xc = pre * sp
        dtr = dt_ref[...] + dtb_ref[...]
        dt = _softplus(dtr)
        a, tril, ri, lane = _chunk_decays(dt, alog_ref, e_ref, cs_s, cst_s, csx_s)
        et = et_ref[...]
        dt_x = _split_dot(dt, e_ref[...], 2)
        cs_last_x = csx_s[CH - 1:CH, :]

        for g in range(NG):
            gs = slice(g * 512, (g + 1) * 512)
            z = _zb_cols(zb0_ref, zb1_ref, gs).astype(F32)
            sz = _sigmoid(z)
            yv = y_ref[:, gs]
            zs = z * sz
            yz = yv * zs
            rg = lax.rsqrt(jnp.mean(yz * yz, axis=-1, keepdims=True) + EPS)
            yn = yz * rg
            dyb = dyb_ref[:, gs].astype(F32)
            dsg_ref[0:1, gs] += jnp.sum(dyb * yn, axis=0, keepdims=True)
            dyn = dyb * sg_ref[:, gs]
            dyz = rg * (dyn - yn * jnp.mean(dyn * yn, axis=-1, keepdims=True))
            dy_s[:, gs] = dyz * zs
            dzb_ref[:, gs] = (dyz * yv * (sz + zs * (1.0 - sz))).astype(BF16)

        rsum = jnp.zeros((CH, DTW), F32)
        csum_t = jnp.zeros((DTW, CH), F32)
        for g in range(NG):
            gs = slice(g * 512, (g + 1) * 512)
            bg = xc[:, DI + g * NST:DI + (g + 1) * NST].astype(BF16)
            cg = xc[:, DI + 512 + g * NST:DI + 512 + (g + 1) * NST].astype(BF16)
            cbm = _dot_nt(cg, bg)
            xdt = xc[:, gs] * dt_x[:, gs]
            hprev = hp_ref[0, :, gs]
            hpb = hprev.astype(BF16)
            gn = g_s[:, gs]
            gnb = gn.astype(BF16)
            dy = dy_s[:, gs]
            csx = csx_s[:, gs]
            ecs = jnp.exp(csx)
            dec = jnp.exp(cs_last_x[:, gs] - csx)
            dye_f = dy * ecs
            dye = dye_f.astype(BF16)
            dc = _dot_nt(dye, hpb)
            dprev = _dot_tn(cg, dye)
            dxdt_state = dec * _dot(bg, gnb)
            db = _dot_nt((xdt * dec).astype(BF16), gnb)
            dcb = jnp.zeros((CH, CH), F32)
            for q in range(4):
                h0 = g * 8 + 2 * q
                ps = slice(q * 128, (q + 1) * 128)
                dyp = dy[:, ps]
                l0 = _lmat(cst_s, h0, tril)
                l1 = _lmat(cst_s, h0 + 1, tril)
                m0 = cbm * l0
                m1 = cbm * l1
                dm = _dot_nt(dyp.astype(BF16), _head_pair_rows(xdt[:, ps], lane))
                dm0 = dm[:, :CH]
                dm1 = dm[:, CH:]
                dcb = dcb + dm0 * l0 + dm1 * l1
                for hh, qm in ((h0, dm0 * m0), (h0 + 1, dm1 * m1)):
                    rsum = jnp.where(lane == hh, jnp.sum(qm, axis=1, keepdims=True), rsum)
                    csum_t = jnp.where(ri == hh, jnp.sum(qm, axis=0, keepdims=True), csum_t)
                mst = jnp.concatenate([m0, m1], axis=0).astype(BF16)
                d = _dot_tn(mst, _head_pair_rows(dyp, lane))
                dxdt_s[:, g * 512 + q * 128:g * 512 + (q + 1) * 128] = d + dxdt_state[:, ps]
            dsd_s[:, gs] = xdt * dxdt_state
            dxs_s[:, gs] = dye_f * _dot(cg, hpb)
            dcbb = dcb.astype(BF16)
            dxc_s[:, DI + 512 + g * NST:DI + 512 + (g + 1) * NST] = dc + _dot(dcbb, bg)
            dxc_s[:, DI + g * NST:DI + (g + 1) * NST] = db + _dot_tn(dcbb, cg)
            gh_s[:, gs] = jnp.broadcast_to(jnp.sum(gn * hprev, axis=0, keepdims=True), (8, 512))
            g_s[:, gs] = dprev + jnp.exp(cs_last_x[:, gs]) * gn

        xs = xc[:, :DI]
        dy = dy_s[...]
        dxdt = dxdt_s[...]
        cs_last = cs_s[CH - 1:CH, :]
        state_e = _split_dot(dsd_s[...], et, 2)
        dcd = 0.125 * jnp.sum(_split_dot(gh_s[...], et, 2), axis=0, keepdims=True) * jnp.exp(cs_last)
        row = lax.broadcasted_iota(jnp.int32, (CH, 1), 0)
        dcs = rsum - csum_t.T + _split_dot(dxs_s[...], et, 2) - state_e
        dcs = dcs + jnp.where(row == CH - 1, jnp.sum(state_e, axis=0, keepdims=True) + dcd, 0.0)
        dda = _split_dot(dcs, (lane >= ri).astype(BF16), 3, w_left=True)
        ddt = dda * a + _dot((dxdt * xs).astype(BF16), et)
        ddtr = jnp.where(lane < NH, ddt * _sigmoid(dtr), 0.0)
        ddt_ref[...] = ddtr.astype(BF16)
        dsm_ref[0:1, :] += jnp.sum(ddtr, axis=0, keepdims=True)
        dsm_ref[1:2, :] += jnp.sum(dda * dt, axis=0, keepdims=True) * a
        dsm_ref[2:3, :] += jnp.sum(_dot((dy * xs).astype(BF16), et), axis=0, keepdims=True)
        dxc_s[:, :DI] = dxdt * dt_x + dy * dx_ref[...]

        dpre = dxc_s[...] * (sp + xc * (1.0 - sp))
        dpre_b = dpre.astype(BF16)
        dd_s[:CH, :] = dpre_b
        moved = _dot(shift_ref[...], dd_s[...])
        dd_s[CH:CH + 16, :] = dpre_b[:16]
        x = xbc_ref[...].astype(F32)
        dcw_ref[4:5, :] += jnp.sum(dpre, axis=0, keepdims=True)
        dxbc = cw_ref[3:4, :] * dpre
        dcw_ref[3:4, :] += jnp.sum(dpre * x, axis=0, keepdims=True)
        for j in (1, 2, 3):
            ahead = moved[(j - 1) * CH:j * CH]
            dcw_ref[3 - j:4 - j, :] += jnp.sum(ahead * x, axis=0, keepdims=True)
            dxbc = dxbc + cw_ref[3 - j:4 - j, :] * ahead
        dxbc_ref[...] = dxbc.astype(BF16)

    rev = lambda c: nc - 1 - c
    return _call(
        body, ex, name="ssd_bwd", grid=(nc,),
        in_specs=[pl.BlockSpec((CH, CD), lambda c: (rev(c), OFF_XBC // CD)),
                  pl.BlockSpec((CH, CD), lambda c: (rev(c), 0)),
                  pl.BlockSpec((CH, D), lambda c: (rev(c), OFF_ZB // D)), pl.BlockSpec((CH, D), lambda c: (rev(c), OFF_ZB // D + 1)),
                  pl.BlockSpec((CH, DTW), lambda c: (rev(c), 0)),
                  pl.BlockSpec((CH, DI), lambda c: (rev(c), 0)), pl.BlockSpec((CH, DI), lambda c: (rev(c), 0)),
                  pl.BlockSpec((1, NST, DI), lambda c: (rev(c), 0, 0)),
                  _full((4, CD)), _full((1, DTW)), _full((1, DTW)), _full((1, DI)), _full((1, DI)),
                  _full((DTW, DI)), _full((DI, DTW)), _full((3 * CH, 2 * CH))],
        out_specs=[pl.BlockSpec((CH, CD + DI), lambda c: (rev(c), 0)),
                   pl.BlockSpec((CH, DTW), lambda c: (rev(c), 0)), _full((8, CD)), _full((8, DTW)), _full((8, DI))],
        out_shape=[jax.ShapeDtypeStruct((T, CD + DI), BF16), jax.ShapeDtypeStruct((T, DTW), BF16),
                   jax.ShapeDtypeStruct((8, CD), F32), jax.ShapeDtypeStruct((8, DTW), F32), jax.ShapeDtypeStruct((8, DI), F32)],
        scratch_shapes=[pltpu.VMEM((NST, DI), F32), pltpu.VMEM((2 * CH, CD), BF16), pltpu.VMEM((CH, CH), F32), pltpu.VMEM((CH, CH), F32),
                        pltpu.VMEM((CH, DI), F32), pltpu.VMEM((CH, DI), F32), pltpu.VMEM((CH, DI), F32), pltpu.VMEM((CH, DI), F32),
                        pltpu.VMEM((CH, DI), F32), pltpu.VMEM((CH, CD), F32), pltpu.VMEM((8, DI), F32)],
        args=(proj, pre_all, proj, proj, dtr, y, dyb, hprev_all, cw, dtb, alog, dskx, sg, e128, et128, _shift_matrix(down=False)))


def _merge_tile():
    return 256


def _merge(x, yb, proj, p, tgt, gmlp, w_oa, w_ob, w_out, w_pg, w_ple, ple_g, fin_g):
    T = x.shape[0]
    tm = min(T, _merge_tile())

    def body(x_ref, u_ref, v_ref, za_ref, yb_ref, ga_ref, gb_ref, p_ref, t_ref, lg_ref, lb_ref, ws_ref, bs_ref,
             woa, wob, wout, wpg, wple, pg_ref, fg_ref,
             dx1_ref, dx1b_ref, ya_ref, mg_ref, hp_ref, dpre_ref, dpe_ref, doa_ref, dob_ref, dya_ref, dyb_ref, dg_ref, acc_ref, vn_s):
        @pl.when(pl.program_id(0) == 0)
        def _():
            acc_ref[...] = jnp.zeros_like(acc_ref)

        _gmlp_fwd_tile(u_ref, v_ref, za_ref, lg_ref, lb_ref, ws_ref, bs_ref, ya_ref, vn_s)
        oa = _dot(ya_ref[...], woa[...])
        ob = _dot(yb_ref[...], wob[...])
        sa = _sigmoid(ga_ref[...].astype(F32))
        sb = _sigmoid(gb_ref[...].astype(F32))
        mg = sa * oa + sb * ob
        mgb = mg.astype(BF16)
        mg_ref[...] = mgb
        x1 = x_ref[...] + _dot(mgb, wout[...])
        r2 = lax.rsqrt(jnp.mean(x1 * x1, axis=-1, keepdims=True) + EPS)
        xh1 = x1 * r2
        hpb = (xh1 * pg_ref[...]).astype(BF16)
        hp_ref[...] = hpb
        gate = _sigmoid(_dot(hpb, wpg[...]))
        pe = _dot(p_ref[...].astype(BF16), wple[...])
        x2 = x1 + gate * pe
        r3 = lax.rsqrt(jnp.mean(x2 * x2, axis=-1, keepdims=True) + EPS)
        xh2 = x2 * r3
        err = xh2 * fg_ref[...] - t_ref[...]
        acc_ref[2:3, :] += 0.5 * jnp.sum(jnp.mean(err * err, axis=-1, keepdims=True))
        dyo = err * (1.0 / D)
        acc_ref[0:1, :] += jnp.sum(dyo * xh2, axis=0, keepdims=True)
        dn = dyo * fg_ref[...]
        dx2 = r3 * (dn - xh2 * jnp.mean(dn * xh2, axis=-1, keepdims=True))
        dpe_ref[...] = (dx2 * gate).astype(BF16)
        dpre = (dx2 * pe * gate * (1.0 - gate)).astype(BF16)
        dpre_ref[...] = dpre
        dhp = _dot_nt(dpre, wpg[...])
        acc_ref[1:2, :] += jnp.sum(dhp * xh1, axis=0, keepdims=True)
        dhn = dhp * pg_ref[...]
        dx1 = dx2 + r2 * (dhn - xh1 * jnp.mean(dhn * xh1, axis=-1, keepdims=True))
        dx1_ref[...] = dx1
        dx1b = dx1.astype(BF16)
        dx1b_ref[...] = dx1b
        dmg = _dot_nt(dx1b, wout[...])
        doa = (dmg * sa).astype(BF16)
        dob = (dmg * sb).astype(BF16)
        doa_ref[...] = doa
        dob_ref[...] = dob
        dg_ref[:, :D] = (dmg * oa * sa * (1.0 - sa)).astype(BF16)
        dg_ref[:, D:] = (dmg * ob * sb * (1.0 - sb)).astype(BF16)
        dya_ref[...] = _dot_nt(doa, woa[...]).astype(BF16)
        dyb_ref[...] = _dot_nt(dob, wob[...]).astype(BF16)

    row = lambda w: pl.BlockSpec((tm, w), lambda i: (i, 0))
    blk = lambda off: pl.BlockSpec((tm, D), lambda i: (i, off // D))
    wsp = lambda s: pl.BlockSpec(s, lambda i: (0, 0), pipeline_mode=pl.Buffered(1))
    return pl.pallas_call(
        body, name="merge", grid=(T // tm,),
        in_specs=[row(D), blk(OFF_U), blk(OFF_V), blk(OFF_ZA), row(DI), blk(OFF_GA), blk(OFF_GB), row(PLE), row(D),
                  _full((1, D)), _full((1, D)), _full((NG, CH, CH)), _full((NG, CH, 256)),
                  wsp((D, D)), wsp((DI, D)), wsp((D, D)), wsp((D, D)), wsp((PLE, D)), _full((1, D)), _full((1, D))],
        out_specs=[row(D)] * 10 + [row(DI), row(2 * D), _full((8, D))],
        out_shape=[jax.ShapeDtypeStruct((T, D), F32)] + [jax.ShapeDtypeStruct((T, D), BF16)] * 9
        + [jax.ShapeDtypeStruct((T, DI), BF16), jax.ShapeDtypeStruct((T, 2 * D), BF16), jax.ShapeDtypeStruct((8, D), F32)],
        scratch_shapes=[pltpu.VMEM((tm, D), BF16)],
        compiler_params=_cp(("arbitrary",)),
    )(x, proj, proj, proj, yb, proj, proj, p, tgt, *gmlp, w_oa, w_ob, w_out, w_pg, w_ple, ple_g, fin_g)


def _wgrad(a, b, name):
    T, K = a.shape
    N = b.shape[1]
    tt, tk, tn = min(T, 2048), min(K, 1024), min(N, 1024)
    nt = T // tt

    def body(a_ref, b_ref, o_ref, acc_s):
        t = pl.program_id(2)

        @pl.when(t == 0)
        def _():
            acc_s[...] = jnp.zeros_like(acc_s)

        acc_s[...] += _dot_tn(a_ref[...].astype(BF16), b_ref[...])

        @pl.when(t == nt - 1)
        def _():
            o_ref[...] = acc_s[...].astype(BF16)

    return pl.pallas_call(
        body, name=name, grid=(K // tk, N // tn, nt),
        in_specs=[pl.BlockSpec((tt, tk), lambda k, n, t: (t, k)), pl.BlockSpec((tt, tn), lambda k, n, t: (t, n))],
        out_specs=pl.BlockSpec((tk, tn), lambda k, n, t: (k, n)),
        out_shape=jax.ShapeDtypeStruct((K, N), BF16),
        scratch_shapes=[pltpu.VMEM((tk, tn), F32)],
        compiler_params=_cp(("parallel", "parallel", "arbitrary")),
    )(a, b)


def _proj_bwd(x, dx1, norm_g, wt, pieces, ddt, ex):
    T = x.shape[0]
    tm = min(T, 1024)
    nk = OFF_DT // D + 1
    starts = [sum(a.shape[1] for a in pieces[:n]) // D for n in range(len(pieces))]
    ranges = [(s, s + a.shape[1] // D) for s, a in zip(starts, pieces)]
    assert ranges[-1][1] == nk - 1
    npc = len(pieces)

    def body(x_hbm, dx1_hbm, g_ref, w_ref, wdt_ref, *rest):
        piece_refs, ddt_ref, gx_ref, dng_ref = rest[:npc], rest[npc], rest[npc + 1], rest[npc + 2]
        acc_s, x_ref, dx1_ref, row_sems = rest[npc + 3:]
        i, k = pl.program_id(0), pl.program_id(1)
        rows = pl.ds(pl.multiple_of(i * tm, tm), tm)
        fetches = [pltpu.make_async_copy(x_hbm.at[rows], x_ref, row_sems.at[0]),
                   pltpu.make_async_copy(dx1_hbm.at[rows], dx1_ref, row_sems.at[1])]

        @pl.when((i == 0) & (k == 0))
        def _():
            dng_ref[...] = jnp.zeros_like(dng_ref)

        @pl.when(k == 0)
        def _():
            acc_s[...] = jnp.zeros_like(acc_s)

        @pl.when(k == 1)
        def _():
            for cp in fetches:
                cp.start()

        for ref, (lo, hi) in zip(piece_refs, ranges):
            @pl.when((k >= lo) & (k < hi))
            def _(ref=ref):
                acc_s[...] += _dot(ref[...], w_ref[...])

        @pl.when(k == nk - 1)
        def _():
            for cp in fetches:
                cp.wait()
            dh = acc_s[...] + _dot(ddt_ref[...], wdt_ref[...])
            xf = x_ref[...]
            r = lax.rsqrt(jnp.mean(xf * xf, axis=-1, keepdims=True) + EPS)
            xh = xf * r
            dng_ref[0:1, :] += jnp.sum(dh * xh, axis=0, keepdims=True)
            dxn = dh * g_ref[...]
            gx_ref[...] = dx1_ref[...] + r * (dxn - xh * jnp.mean(dxn * xh, axis=-1, keepdims=True))

    def piece_spec(lo, hi):
        return pl.BlockSpec((tm, D), lambda i, k: (i, jnp.clip(k - lo, 0, hi - lo - 1)))

    row = pl.BlockSpec((tm, D), lambda i, k: (i, 0))
    return _call(
        body, ex, name="proj_bwd", grid=(T // tm, nk),
        in_specs=[_ANY, _ANY, _full((1, D)), pl.BlockSpec((D, D), lambda i, k: (jnp.minimum(k, nk - 2), 0)),
                  pl.BlockSpec((DTW, D), lambda i, k: (OFF_DT // DTW, 0))]
        + [piece_spec(lo, hi) for lo, hi in ranges] + [pl.BlockSpec((tm, DTW), lambda i, k: (i, 0))],
        out_specs=[row, _full((8, D))],
        out_shape=[jax.ShapeDtypeStruct((T, D), F32), jax.ShapeDtypeStruct((8, D), F32)],
        scratch_shapes=[pltpu.VMEM((tm, D), F32), pltpu.VMEM((tm, D), F32), pltpu.VMEM((tm, D), F32), pltpu.SemaphoreType.DMA((2,))],
        args=(x, dx1, norm_g, wt, wt, *pieces, ddt))


def _elementwise_tile(R, C, limit=1 << 20):
    if R * C * 4 <= limit:
        return R, C
    rows = [t for t in range(16, R, 16) if R % t == 0 and t * C * 4 <= limit]
    if rows:
        return rows[-1], C
    cols = [t for t in range(128, C, 128) if C % t == 0 and R * t * 4 <= limit]
    return R, cols[-1]


def _adam_update(w, m, v, g):
    c1 = 1.0 - ADAM_B1 ** ADAM_STEP
    c2 = 1.0 - ADAM_B2 ** ADAM_STEP
    mm = ADAM_B1 * m + (1.0 - ADAM_B1) * g
    vv = ADAM_B2 * v + (1.0 - ADAM_B2) * (g * g)
    return -ADAM_LR * ((mm / c1) / (jnp.sqrt(vv / c2) + ADAM_EPS) + ADAM_WD * w), mm, vv


def _sum_parts(parts, name, ex):
    first = parts[0][0] if isinstance(parts[0], tuple) else parts[0]
    R, C = first.shape[-2:]
    tr, tc = _elementwise_tile(R, C)
    flat = pl.BlockSpec((tr, tc), lambda i, j: (i, j))
    slot = lambda k: pl.BlockSpec((None, tr, tc), lambda i, j: (k, i, j))
    specs = [slot(p[1]) if isinstance(p, tuple) else flat for p in parts]
    arrays = [p[0] if isinstance(p, tuple) else p for p in parts]

    def body(*refs):
        g = refs[0][...].astype(F32)
        for r in refs[1:-1]:
            g = g + r[...].astype(F32)
        refs[-1][...] = g

    (out,), got = _call(body, ex, name=name, grid=(R // tr, C // tc), in_specs=specs, out_specs=[flat],
                        out_shape=[jax.ShapeDtypeStruct((R, C), F32)], scratch_shapes=[], args=arrays)
    return out, got


def _adamw_tiles(w, m, v, g, name):
    R = w.shape[0]
    tr = 107
    assert R % tr == 0
    spec = pl.BlockSpec((tr, 8, 128), lambda i: (i, 0, 0))

    def body(w_ref, m_ref, v_ref, g_in, g_ref, d_ref, nm_ref, nv_ref):
        g = g_in[...]
        g_ref[...] = g
        d_ref[...], nm_ref[...], nv_ref[...] = _adam_update(w_ref[...], m_ref[...], v_ref[...], g)

    return pl.pallas_call(body, name=name, grid=(R // tr,), in_specs=[spec] * 4, out_specs=[spec] * 4,
                          out_shape=[jax.ShapeDtypeStruct(w.shape, F32)] * 4, compiler_params=_cp(("parallel",)))(w, m, v, g)


def _adamw(w, m, v, parts, name, part_row=0, block_rows=None):
    R, C = w.shape
    tr, tc = _elementwise_tile(R, C) if block_rows is None else (block_rows, C)
    assert R % tr == 0 and part_row % tr == 0
    first = part_row // tr
    n = len(parts)
    wspec = pl.BlockSpec((tr, tc), lambda i, j: (i, j))
    flat = pl.BlockSpec((tr, tc), lambda i, j: (first + i, j))
    slot = lambda k: pl.BlockSpec((None, tr, tc), lambda i, j: (k, first + i, j))
    part_specs = [slot(p[1]) if isinstance(p, tuple) else flat for p in parts]
    part_arrays = [p[0] if isinstance(p, tuple) else p for p in parts]

    def body(*refs):
        w_ref, m_ref, v_ref = refs[:3]
        g_ref, d_ref, nm_ref, nv_ref = refs[3 + n:]
        g = refs[3][...].astype(F32)
        for r in refs[4:3 + n]:
            g = g + r[...].astype(F32)
        g_ref[...] = g
        d_ref[...], nm_ref[...], nv_ref[...] = _adam_update(w_ref[...], m_ref[...], v_ref[...], g)

    return pl.pallas_call(
        body, name=name, grid=(R // tr, C // tc), in_specs=[wspec] * 3 + part_specs, out_specs=[wspec] * 4,
        out_shape=[jax.ShapeDtypeStruct(w.shape, F32)] * 4, compiler_params=_cp(("parallel", "parallel")),
    )(w, m, v, *part_arrays)


_SMALL_WIDE = {"ln_a_g": 0, "ln_a_b": 1, "final_g": 8, "ple_norm_g": 9, "ssm_norm_g": 16, "conv_b": 44}
_WIDE_CONV_W, _WIDE_ROWS = 32, 56
_WIDE_LOSS = 10
_SMALL_NARROW = {"w_s": (0, NG * CH, 128), "b_s": (512, NG, 128), "dt_bias": (520, 1, NH), "a_log": (521, 1, NH),
                 "d_skip": (522, 1, NH)}
_NARROW_ROWS = 528
_SMALL_SHAPES = {"norm_g": (1, D), "ln_a_g": (1, D), "ln_a_b": (1, D), "ple_norm_g": (1, D), "final_g": (1, D),
                 "ssm_norm_g": (1, DI), "conv_b": (1, CD), "w_s": (NG * CH, CH), "b_s": (NG, CH), "dt_bias": (1, NH),
                 "a_log": (1, NH), "d_skip": (1, NH)}


def _adamw_small(w, m, v, wide_all, narrow_all, norm_all):
    names = list(_SMALL_SHAPES)
    n = len(names)

    def body(*refs):
        wr, mr, vr = refs[:n], refs[n:2 * n], refs[2 * n:3 * n]
        wide_ref, narrow_ref, norm_ref = refs[3 * n:3 * n + 3]
        outs = refs[3 * n + 3:]
        gr, dr, nmr, nvr, cw_ref, loss_ref = outs[:n], outs[n:2 * n], outs[2 * n:3 * n], outs[3 * n:4 * n], outs[4 * n], outs[4 * n + 1]

        def total(ref, rows, lanes):
            acc = ref[0, rows, lanes]
            for d in range(1, NDEV):
                acc = acc + ref[d, rows, lanes]
            return acc

        for k, name in enumerate(names):
            if name in _SMALL_WIDE or name == "norm_g":
                for part in range(_SMALL_SHAPES[name][1] // D):
                    pack, r = (norm_ref, 0) if name == "norm_g" else (wide_ref, _SMALL_WIDE[name] + part)
                    cols = slice(part * D, (part + 1) * D)
                    g = total(pack, slice(r, r + 1), slice(None))
                    gr[k][:, cols] = g
                    dr[k][:, cols], nmr[k][:, cols], nvr[k][:, cols] = _adam_update(wr[k][:, cols], mr[k][:, cols], vr[k][:, cols], g)
            else:
                r, rows, lanes = _SMALL_NARROW[name]
                g = total(narrow_ref, slice(r, r + rows), slice(0, lanes))
                gr[k][...] = g
                dr[k][...], nmr[k][...], nvr[k][...] = _adam_update(wr[k][...], mr[k][...], vr[k][...], g)
        cw_ref[...] = total(wide_ref, slice(_WIDE_CONV_W, _WIDE_CONV_W + 12), slice(None))
        loss_ref[...] = total(wide_ref, slice(_WIDE_LOSS, _WIDE_LOSS + 1), slice(None))

    shapes = [jax.ShapeDtypeStruct(_SMALL_SHAPES[k], F32) for k in names]
    specs = [_full(_SMALL_SHAPES[k]) for k in names]
    res = pl.pallas_call(
        body, name="adamw_small", grid=(1,),
        in_specs=specs * 3 + [_full(wide_all.shape), _full(narrow_all.shape), _full(norm_all.shape)],
        out_specs=specs * 4 + [_full((12, D)), _full((1, D))],
        out_shape=shapes * 4 + [jax.ShapeDtypeStruct((12, D), F32), jax.ShapeDtypeStruct((1, D), F32)],
        compiler_params=_cp(("arbitrary",)),
    )(*[w[k] for k in names], *[m[k] for k in names], *[v[k] for k in names], wide_all, narrow_all, norm_all)
    groups = [dict(zip(names, res[q * n:(q + 1) * n])) for q in range(4)]
    return groups[0], groups[1], groups[2], groups[3], res[4 * n], res[4 * n + 1][0, 0]


def _dev_index(px, py, pc):
    return 4 * px + 2 * py + pc


def _mesh_position():
    return lax.axis_index("x"), lax.axis_index("y"), lax.axis_index("c")


def _gather_exchange(blocks):
    n = len(blocks)

    def make(ins, outs, sems):
        send_sems, recv_sems, local_sems = sems
        x, y, c = _mesh_position()
        me, sibling = (x, y, c), (x, y, 1 - c)
        chips = [(1 - x, y), (x, 1 - y), (1 - x, 1 - y)]

        def copy(a, k, block, to, src=None):
            dst = outs[a].at[_dev_index(*block)]
            return pltpu.make_async_remote_copy(src_ref=dst if src is None else src, dst_ref=dst, send_sem=send_sems.at[a, k],
                                                recv_sem=recv_sems.at[a, k], device_id=to, device_id_type=MESH)

        mine = [pltpu.make_async_copy(ins[a], outs[a].at[_dev_index(*me)], local_sems.at[a]) for a in range(n)]
        first = []
        for a in range(n):
            first.append(copy(a, 0, me, sibling, src=ins[a]))
            first += [copy(a, 1 + j, me, (*chip, c), src=ins[a]) for j, chip in enumerate(chips)]

        def start():
            for cp in mine + first:
                cp.start()

        def finish():
            passed = []
            for j, chip in enumerate(chips):
                for a in range(n):
                    copy(a, 1 + j, (*chip, c), me).wait_recv()
                    fwd = copy(a, 4 + j, (*chip, c), sibling)
                    fwd.start()
                    passed.append(fwd)
            for a in range(n):
                copy(a, 0, sibling, me).wait_recv()
                for j, chip in enumerate(chips):
                    copy(a, 4 + j, (*chip, 1 - c), me).wait_recv()
            for cp in first + passed:
                cp.wait_send()
            for cp in mine:
                cp.wait()

        return start, finish

    return _Exchange(list(blocks), [jax.ShapeDtypeStruct((NDEV,) + b.shape, b.dtype) for b in blocks],
                     [pltpu.SemaphoreType.DMA((n, 7)), pltpu.SemaphoreType.DMA((n, 7)), pltpu.SemaphoreType.DMA((n,))], make)


def _relay_gather_exchange(blocks):
    n = len(blocks)

    def make(ins, outs, sems):
        send_sems, recv_sems, local_sems = sems
        x, y, c = _mesh_position()
        me, sibling = (x, y, c), (x, y, 1 - c)
        x_nbr, y_nbr, diag = (1 - x, y), (x, 1 - y), (1 - x, 1 - y)
        relay_from = (jnp.where(c == 0, x, 1 - x), jnp.where(c == 0, 1 - y, y))
        relay_to = (jnp.where(c == 0, 1 - x, x), jnp.where(c == 0, y, 1 - y))

        def copy(a, k, block, to, src=None):
            dst = outs[a].at[_dev_index(*block)]
            return pltpu.make_async_remote_copy(src_ref=dst if src is None else src, dst_ref=dst, send_sem=send_sems.at[a, k],
                                                recv_sem=recv_sems.at[a, k], device_id=to, device_id_type=MESH)

        mine = [pltpu.make_async_copy(ins[a], outs[a].at[_dev_index(*me)], local_sems.at[a]) for a in range(n)]
        first = []
        for a in range(n):
            first += [copy(a, 0, me, sibling, src=ins[a]), copy(a, 1, me, (*x_nbr, c), src=ins[a]), copy(a, 2, me, (*y_nbr, c), src=ins[a])]

        def start():
            for cp in mine + first:
                cp.start()

        def finish():
            later = []
            for a in range(n):
                copy(a, 1, (*x_nbr, c), me).wait_recv()
                copy(a, 2, (*y_nbr, c), me).wait_recv()
                later.append(copy(a, 3, (*relay_from, c), (*relay_to, c)))
                later += [copy(a, 4, (*x_nbr, c), sibling), copy(a, 5, (*y_nbr, c), sibling)]
                for cp in later[-3:]:
                    cp.start()
            for a in range(n):
                copy(a, 3, (*diag, c), me).wait_recv()
                later.append(copy(a, 6, (*diag, c), sibling))
                later[-1].start()
            for a in range(n):
                copy(a, 0, sibling, me).wait_recv()
                for k, chip in ((4, x_nbr), (5, y_nbr), (6, diag)):
                    copy(a, k, (*chip, 1 - c), me).wait_recv()
            for cp in first + later:
                cp.wait_send()
            for cp in mine:
                cp.wait()

        return start, finish

    return _Exchange(list(blocks), [jax.ShapeDtypeStruct((NDEV,) + b.shape, b.dtype) for b in blocks],
                     [pltpu.SemaphoreType.DMA((n, 7)), pltpu.SemaphoreType.DMA((n, 7)), pltpu.SemaphoreType.DMA((n,))], make)


def _combine(*exchanges):
    def make(ins, outs, sems):
        pairs = []
        for e in exchanges:
            ni, no, ns = len(e.arrays), len(e.out_shape), len(e.sems)
            pairs.append(e.make(ins[:ni], outs[:no], sems[:ns]))
            ins, outs, sems = ins[ni:], outs[no:], sems[ns:]

        def start():
            for s, _ in pairs:
                s()

        def finish():
            for _, f in pairs:
                f()

        return start, finish

    return _Exchange(sum((list(e.arrays) for e in exchanges), []), sum((list(e.out_shape) for e in exchanges), []),
                     sum((list(e.sems) for e in exchanges), []), make)


def _start_wait_all(copies, local=()):
    def start():
        for cp in list(local) + list(copies):
            cp.start()

    def finish():
        for cp in copies:
            cp.wait()
        for cp in local:
            cp.wait()

    return start, finish


def _no_exchange():
    return _Exchange([], [], [], lambda ins, outs, sems: (lambda: None, lambda: None))


def _direct_exchange(grads):
    n = len(grads)

    def make(ins, outs, sems):
        send_sems, recv_sems = sems
        x, y, c = _mesh_position()
        copies = []
        for a in range(n):
            for r in range(1, NDEV):
                peer = (x ^ ((r >> 2) & 1), y ^ ((r >> 1) & 1), c ^ (r & 1))
                copies.append(pltpu.make_async_remote_copy(
                    src_ref=ins[a].at[_dev_index(*peer)], dst_ref=outs[a].at[r - 1], send_sem=send_sems.at[a, r - 1],
                    recv_sem=recv_sems.at[a, r - 1], device_id=peer, device_id_type=MESH))
        return _start_wait_all(copies)

    return _Exchange(list(grads), [jax.ShapeDtypeStruct((NDEV - 1,) + g.shape[1:], g.dtype) for g in grads],
                     [pltpu.SemaphoreType.DMA((n, NDEV - 1)), pltpu.SemaphoreType.DMA((n, NDEV - 1))], make)


def _direct_gather_exchange(smalls):
    n = len(smalls)

    def make(ins, outs, sems):
        send_sems, recv_sems, local_sems = sems
        x, y, c = _mesh_position()
        copies, local = [], []
        for a in range(n):
            slot = outs[a].at[_dev_index(x, y, c)]
            local.append(pltpu.make_async_copy(ins[a], slot, local_sems.at[a]))
            for r in range(1, NDEV):
                peer = (x ^ ((r >> 2) & 1), y ^ ((r >> 1) & 1), c ^ (r & 1))
                copies.append(pltpu.make_async_remote_copy(src_ref=ins[a], dst_ref=slot, send_sem=send_sems.at[a, r - 1],
                                                           recv_sem=recv_sems.at[a, r - 1], device_id=peer, device_id_type=MESH))
        return _start_wait_all(copies, local=local)

    return _Exchange(list(smalls), [jax.ShapeDtypeStruct((NDEV,) + s.shape, s.dtype) for s in smalls],
                     [pltpu.SemaphoreType.DMA((n, 7)), pltpu.SemaphoreType.DMA((n, 7)), pltpu.SemaphoreType.DMA((n,))], make)


_W_IN_ROWS = {"u": (0, 1024), "v": (1024, 2048), "za": (2048, 3072), "zb": (3072, 5120), "xbc": (5120, 8192),
              "dt": (8192, 8224), "ga": (8224, 9248), "gb": (9248, 10272)}
_PROJ_ORDER = ("xbc", "zb", "u", "v", "za", "ga", "gb", "dt")


def _w_in_t_rows(wt):
    z = jnp.zeros((NP - NIN, wt.shape[1]), wt.dtype)
    return jnp.concatenate([wt[slice(*_W_IN_ROWS[n])] for n in _PROJ_ORDER] + [z], axis=0)


_WEIGHTS = ["norm_g", "w_in", "ln_a_g", "ln_a_b", "w_s", "b_s", "conv_w", "conv_b", "dt_bias", "a_log", "d_skip", "ssm_norm_g",
            "w_oa", "w_ob", "w_out", "ple_norm_g", "w_pg", "w_ple", "final_g"]


def _rows_pack(d, dtype):
    return jnp.concatenate([d["w_oa"].reshape(128, D), d["w_ob"].reshape(256, D), d["w_out"].reshape(128, D),
                            d["w_pg"].reshape(128, D), d["w_ple"].reshape(32, D)], axis=0).astype(dtype)


def kernel(x, p, norm_g, w_in, ln_a_g, ln_a_b, w_s, b_s, conv_w, conv_b, dt_bias, a_log, d_skip, ssm_norm_g, w_oa, w_ob, w_out, ple_norm_g, w_pg, w_ple, final_g, loss_target, m_norm_g, m_w_in, m_ln_a_g, m_ln_a_b, m_w_s, m_b_s, m_conv_w, m_conv_b, m_dt_bias, m_a_log, m_d_skip, m_ssm_norm_g, m_w_oa, m_w_ob, m_w_out, m_ple_norm_g, m_w_pg, m_w_ple, m_final_g, v_norm_g, v_w_in, v_ln_a_g, v_ln_a_b, v_w_s, v_b_s, v_conv_w, v_conv_b, v_dt_bias, v_a_log, v_d_skip, v_ssm_norm_g, v_w_oa, v_w_ob, v_w_out, v_ple_norm_g, v_w_pg, v_w_ple, v_final_g):
    args = dict(locals())
    w = {n: args[n] for n in _WEIGHTS}
    m = {n: args["m_" + n] for n in _WEIGHTS}
    v = {n: args["v_" + n] for n in _WEIGHTS}
    T = x.shape[1]
    xi, yi, ci = lax.axis_index("x"), lax.axis_index("y"), lax.axis_index("c")
    me = 4 * xi + 2 * yi + ci
    x2, p2, tgt = x.reshape(T, D), p.reshape(T, PLE), loss_target.reshape(T, D)

    norm_g2 = w["norm_g"].reshape(1, D)
    ws = jnp.where(jnp.tril(jnp.ones((CH, CH), bool))[None], w["w_s"].reshape(NG, CH, CH), 0.0).astype(BF16)
    wst = jnp.transpose(ws, (0, 2, 1))
    bst = jnp.broadcast_to(w["b_s"].reshape(NG, CH, 1), (NG, CH, 256))
    ln_g, ln_b = w["ln_a_g"].reshape(1, D), w["ln_a_b"].reshape(1, D)
    cb = w["conv_b"].reshape(1, CD)
    pad32 = lambda a: jnp.pad(a.reshape(1, NH), ((0, 0), (0, DTW - NH)))
    dtb, alog = pad32(w["dt_bias"]), pad32(w["a_log"])
    dskx = jnp.repeat(w["d_skip"].reshape(NH), HD).reshape(1, DI)
    sg = w["ssm_norm_g"].reshape(1, DI)
    ple_g, fin_g = w["ple_norm_g"].reshape(1, D), w["final_g"].reshape(1, D)
    e128 = (jnp.arange(DTW)[:, None] == (jnp.arange(DI)[None, :] // HD)).astype(BF16)
    et128 = e128.T
    gsel = ((jnp.arange(D)[:, None] // 256) == jnp.arange(128)[None, :]).astype(BF16)

    w_in_t = lambda a: jnp.transpose(a.reshape(D, WSH))
    shard = w_in_t(w["w_in"]).astype(BF16)
    half = WSH // 32 * 16
    a_lo, a_hi = _run_exchange(_relay_gather_exchange([shard[:half], shard[half:]]), "all_gather_w_in")
    w_in_full_t = jnp.concatenate([a_lo, a_hi], axis=1).reshape(NIN, D)
    wp = _w_in_t_rows(w_in_full_t)
    (proj, dtr, h), (r_all, cw_all) = _proj_fwd(x2, norm_g2, wp, _gather_exchange([_rows_pack(w, BF16), w["conv_w"].reshape(4, CD // NDEV)]))
    f_oa = r_all[:, R_OA:R_OB].reshape(D, D)
    f_ob = r_all[:, R_OB:R_OUT].reshape(DI, D)
    f_out = r_all[:, R_OUT:R_PG].reshape(D, D)
    f_pg = r_all[:, R_PG:R_PLE].reshape(D, D)
    f_ple = jnp.transpose(r_all[:, R_PLE:R_ROWS].reshape(NDEV, PLE, 128), (1, 0, 2)).reshape(PLE, D)
    cw = jnp.transpose(cw_all, (1, 0, 2)).reshape(4, CD)

    y, yb, hprev, pre_all = _ssd_fwd(proj, dtr, cw, cb, dtb, alog, dskx, sg, e128)
    dx1, dx1b, ya, mg, hp, dpre, dpe, doa, dob, dya, dyb, dgab, acc = _merge(
        x2, yb, proj, p2, tgt, (ln_g, ln_b, ws, bst), f_oa, f_ob, f_out, f_pg, f_ple, ple_g, fin_g)

    gple = jnp.transpose(_wgrad(p2, dpe, "wgrad_ple").reshape(PLE, NDEV, 128), (1, 0, 2)).reshape(NDEV, 32, D)
    gr = jnp.concatenate([_wgrad(ya, doa, "wgrad_oa").reshape(NDEV, 128, D), _wgrad(yb, dob, "wgrad_ob").reshape(NDEV, 256, D),
                          _wgrad(mg, dx1b, "wgrad_out").reshape(NDEV, 128, D), _wgrad(hp, dpre, "wgrad_pg").reshape(NDEV, 128, D),
                          gple], axis=1)
    (duvz, dws, dbs, dln), _ = _gmlp_bwd(proj, dya, ln_g, ln_b, ws, wst, bst, gsel, _no_exchange())
    (dxz, ddt, dcw, dsm, dsg), (rr,) = _ssd_bwd(proj, pre_all, dtr, y, dyb, hprev, cw, dtb, alog, dskx, sg, e128, et128, _direct_exchange([gr]))

    g_xz = _wgrad(dxz, h, "wgrad_xbc_zb")
    g_w_in_t = jnp.concatenate([_wgrad(duvz, h, "wgrad_u_v_za"), g_xz[CD:], g_xz[:CD], _wgrad(ddt, h, "wgrad_dt")[:NH],
                                _wgrad(dgab, h, "wgrad_ga_gb")], axis=0)
    ga = g_w_in_t.reshape(NDEV, WSH, D)
    wide = jnp.concatenate([dln, acc, dsg.reshape(16, D), dcw.reshape(24, D)], axis=0)
    narrow = jnp.concatenate([dws.reshape(NG * CH, CH), jnp.pad(dbs[:, :NG].T, ((0, 8 - NG), (0, 0))), dsm], axis=0)
    (gx, dng), (ra, wide_all, narrow_all) = _proj_bwd(x2, dx1, norm_g2, wp, [dxz, duvz, dgab], ddt,
                                                      _combine(_direct_exchange([ga]), _direct_gather_exchange([wide, narrow])))
    ga_own = lax.dynamic_index_in_dim(ga, me, 0, keepdims=False)
    gr_own = lax.dynamic_index_in_dim(gr, me, 0, keepdims=False)

    out_g, out_d, out_m, out_v = {}, {}, {}, {}
    outs = (out_g, out_d, out_m, out_v)
    tiles = lambda a: jnp.transpose(a.reshape(8, 128, WSH), (2, 0, 1))
    g2, (norm_all,) = _sum_parts([ga_own] + [(ra, k) for k in range(NDEV - 1)], "sum_w_in", _direct_gather_exchange([dng]))
    res = _adamw_tiles(tiles(w["w_in"]), tiles(m["w_in"]), tiles(v["w_in"]), g2.reshape(WSH, 8, 128), "adamw_w_in")
    for dst, val in zip(outs, res):
        dst["w_in"] = jnp.transpose(val, (1, 2, 0)).reshape(1, D, WSH)
    parts_r = [gr_own] + [(rr, k) for k in range(NDEV - 1)]
    for name, row, rows in (("w_oa", R_OA, 128), ("w_ob", R_OB, 256), ("w_out", R_OUT, 128), ("w_pg", R_PG, 128)):
        res = _adamw(w[name].reshape(rows, D), m[name].reshape(rows, D), v[name].reshape(rows, D), parts_r, "adamw_" + name,
                     part_row=row, block_rows=128)
        for dst, val in zip(outs, res):
            dst[name] = val.reshape(1, rows, D)
    res = _adamw(w["w_ple"].reshape(32, D), m["w_ple"].reshape(32, D), v["w_ple"].reshape(32, D), parts_r, "adamw_w_ple",
                 part_row=R_PLE, block_rows=32)
    for dst, val in zip(outs, res):
        dst["w_ple"] = val.reshape(1, PLE, 128)
    two_d = lambda d: {n: d[n].reshape(_SMALL_SHAPES[n]) for n in _SMALL_SHAPES}
    *res, g_cw_wide, loss = _adamw_small(two_d(w), two_d(m), two_d(v), wide_all, narrow_all, norm_all)
    for dst, val in zip(outs, res):
        dst.update({n: val[n].reshape(w[n].shape) for n in _SMALL_SHAPES})
    g_cw = lax.dynamic_slice_in_dim(g_cw_wide.reshape(4, CD), me * (CD // NDEV), CD // NDEV, axis=1).reshape(12, 128)
    res = _adamw(w["conv_w"].reshape(12, 128), m["conv_w"].reshape(12, 128), v["conv_w"].reshape(12, 128), [g_cw], "adamw_conv_w")
    for dst, val in zip((out_g, out_d, out_m, out_v), res):
        dst["conv_w"] = val.reshape(1, 4, CD // NDEV)

    return (loss, gx.reshape(1, T, D), *[out_g[n] for n in _WEIGHTS], *[out_d[n] for n in _WEIGHTS],
            *[out_m[n] for n in _WEIGHTS], *[out_v[n] for n in _WEIGHTS])
```

```python
import functools
import math
from typing import Callable, NamedTuple

import jax
import jax.numpy as jnp
from jax import lax
from jax.experimental import pallas as pl
from jax.experimental.pallas import tpu as pltpu

F32 = jnp.float32
BF16 = jnp.bfloat16
MESH = pl.DeviceIdType.MESH

D = 1024
DI = 2048
CD = 3072
NH = 32
HD = 64
NST = 128
NG = 4
CH = 128
PLE = 256
NIN = 10272
NDEV = 8
WSH = NIN // NDEV
EPS = 1e-6
OFF_XBC, OFF_ZB, OFF_U, OFF_V, OFF_ZA, OFF_GA, OFF_GB, OFF_DT = 0, 3072, 5120, 6144, 7168, 8192, 9216, 10240
NP = 10368
DTW = 128
R_OA, R_OB, R_OUT, R_PG, R_PLE, R_ROWS = 0, 128, 384, 512, 640, 672

ADAM_LR, ADAM_B1, ADAM_B2, ADAM_EPS, ADAM_WD, ADAM_STEP = 0.001, 0.9, 0.999, 1e-08, 0.01, 10

V7X_VMEM_LIMIT = 56 * 1024 * 1024


def _cp(sem=None):
    return pltpu.CompilerParams(dimension_semantics=sem, vmem_limit_bytes=V7X_VMEM_LIMIT)


def _dot(a, b, prec=None):
    return jnp.dot(a, b, preferred_element_type=F32, precision=prec)


def _dot_nt(a, b, prec=None):
    return lax.dot_general(a, b, (((1,), (1,)), ((), ())), preferred_element_type=F32, precision=prec)


def _dot_tn(a, b, prec=None):
    return lax.dot_general(a, b, (((0,), (0,)), ((), ())), preferred_element_type=F32, precision=prec)


def _sigmoid(x):
    return 0.5 * jnp.tanh(0.5 * x) + 0.5


_GELU_C = math.sqrt(2.0 / math.pi)
_GELU_K = 0.044715


def _gelu_and_grad(x):
    x2 = x * x
    t = jnp.tanh(x * (_GELU_C + (_GELU_C * _GELU_K) * x2))
    hx = 0.5 * x
    g = hx + hx * t
    dg = (0.5 + 0.5 * t) + hx * (1.0 - t * t) * (_GELU_C + (3.0 * _GELU_C * _GELU_K) * x2)
    return g, dg


def _gelu(x):
    hx = 0.5 * x
    return hx + hx * jnp.tanh(x * (_GELU_C + (_GELU_C * _GELU_K) * (x * x)))


def _softplus(x):
    return jnp.maximum(x, 0.0) + jnp.log(1.0 + jnp.exp(-jnp.abs(x)))


def _full(shape):
    n = len(shape)
    return pl.BlockSpec(shape, lambda *_: (0,) * n)


_ANY = pl.BlockSpec(memory_space=pl.ANY)


class _Exchange(NamedTuple):
    arrays: list
    out_shape: list
    sems: list
    make: Callable


def _call(body, ex, *, name, grid, in_specs, out_specs, out_shape, scratch_shapes, args):
    ki, ko, ks = len(in_specs), len(out_specs), len(scratch_shapes)
    ei, eo = len(ex.arrays), len(ex.out_shape)
    last = [g - 1 for g in grid]

    def full_body(*refs):
        r = list(refs)
        ins, eins, r = r[:ki], r[ki:ki + ei], r[ki + ei:]
        outs, eouts, r = r[:ko], r[ko:ko + eo], r[ko + eo:]
        scr, esems = r[:ks], r[ks:]
        start, finish = ex.make(eins, eouts, esems)
        ids = [pl.program_id(a) for a in range(len(grid))]
        is_first = functools.reduce(lambda p, q: p & q, [i == 0 for i in ids])
        is_last = functools.reduce(lambda p, q: p & q, [i == l for i, l in zip(ids, last)])
        pl.when(is_first)(start)
        body(*ins, *outs, *scr)
        pl.when(is_last)(finish)

    res = pl.pallas_call(
        full_body, name=name, grid=grid, in_specs=list(in_specs) + [_ANY] * ei, out_specs=list(out_specs) + [_ANY] * eo,
        out_shape=list(out_shape) + list(ex.out_shape), scratch_shapes=list(scratch_shapes) + list(ex.sems),
        compiler_params=_cp(("arbitrary",) * len(grid)),
    )(*args, *ex.arrays)
    return res[:ko], res[ko:]


def _run_exchange(ex, name):
    ni, no = len(ex.arrays), len(ex.out_shape)

    def body(*refs):
        start, finish = ex.make(refs[:ni], refs[ni:ni + no], refs[ni + no:])
        start()
        finish()

    return pl.pallas_call(body, name=name, in_specs=[_ANY] * ni, out_specs=[_ANY] * no, out_shape=list(ex.out_shape),
                          scratch_shapes=list(ex.sems))(*ex.arrays)


def _proj_fwd(x, norm_g, wp, ex):
    T = x.shape[0]
    tm, tn = min(T, 1024), 2048
    nj = OFF_DT // tn
    assert OFF_DT % tn == 0 and OFF_DT + DTW == NP

    def body(x_ref, g_ref, w_ref, wdt_ref, proj_ref, dt_ref, h_ref, hs_ref):
        j = pl.program_id(1)

        @pl.when(j == 0)
        def _():
            xf = x_ref[...]
            r = lax.rsqrt(jnp.mean(xf * xf, axis=-1, keepdims=True) + EPS)
            h = (xf * r * g_ref[...]).astype(BF16)
            hs_ref[...] = h
            h_ref[...] = h

        proj_ref[...] = _dot_nt(hs_ref[...], w_ref[...]).astype(BF16)

        @pl.when(j == nj - 1)
        def _():
            dt_ref[...] = _dot_nt(hs_ref[...], wdt_ref[...])

    return _call(
        body, ex, name="proj_fwd", grid=(T // tm, nj),
        in_specs=[pl.BlockSpec((tm, D), lambda i, j: (i, 0)), _full((1, D)), pl.BlockSpec((tn, D), lambda i, j: (j, 0)),
                  pl.BlockSpec((DTW, D), lambda i, j: (OFF_DT // DTW, 0))],
        out_specs=[pl.BlockSpec((tm, tn), lambda i, j: (i, j)), pl.BlockSpec((tm, DTW), lambda i, j: (i, 0)),
                   pl.BlockSpec((tm, D), lambda i, j: (i, 0))],
        out_shape=[jax.ShapeDtypeStruct((T, OFF_DT), BF16), jax.ShapeDtypeStruct((T, DTW), F32), jax.ShapeDtypeStruct((T, D), BF16)],
        scratch_shapes=[pltpu.VMEM((tm, D), BF16)], args=(x, norm_g, wp, wp))


def _gmlp_tile():
    return 512


def _gmlp_fwd_tile(u_ref, v_ref, z_ref, lg_ref, lb_ref, ws_ref, bs_ref, ya_ref, vn_s):
    tm = u_ref.shape[0]
    vg = _gelu(v_ref[...].astype(F32))
    mu = jnp.mean(vg, axis=-1, keepdims=True)
    xc = vg - mu
    rstd = lax.rsqrt(jnp.mean(xc * xc, axis=-1, keepdims=True) + EPS)
    vn_s[...] = (xc * rstd * lg_ref[...] + lb_ref[...]).astype(BF16)
    for c in range(tm // CH):
        rs = slice(c * CH, (c + 1) * CH)
        for g in range(NG):
            cs_ = slice(g * 256, (g + 1) * 256)
            sv = _dot(ws_ref[g], vn_s[rs, cs_]) + bs_ref[g]
            z = z_ref[rs, cs_].astype(F32)
            ya_ref[rs, cs_] = (_gelu(u_ref[rs, cs_].astype(F32)) * sv * (z * _sigmoid(z))).astype(BF16)


def _gmlp_bwd(proj, dya, ln_g, ln_b, ws, wst, bst, gsel, ex):
    T = proj.shape[0]
    tm = min(T, _gmlp_tile())

    def body(u_ref, v_ref, z_ref, dy_ref, lg_ref, lb_ref, ws_ref, wst_ref, bs_ref, gsel_ref,
             d_ref, dws_ref, dbs_ref, dln_ref, vn_s, dsv_s, dvn_s):
        du_ref, dv_ref, dz_ref = d_ref.at[:, 0:D], d_ref.at[:, D:2 * D], d_ref.at[:, 2 * D:3 * D]
        @pl.when(pl.program_id(0) == 0)
        def _():
            dws_ref[...] = jnp.zeros_like(dws_ref)
            dbs_ref[...] = jnp.zeros_like(dbs_ref)
            dln_ref[...] = jnp.zeros_like(dln_ref)

        vg, dvg_dv = _gelu_and_grad(v_ref[...].astype(F32))
        mu = jnp.mean(vg, axis=-1, keepdims=True)
        xc = vg - mu
        rstd = lax.rsqrt(jnp.mean(xc * xc, axis=-1, keepdims=True) + EPS)
        vhat = xc * rstd
        vn_s[...] = (vhat * lg_ref[...] + lb_ref[...]).astype(BF16)
        ri = lax.broadcasted_iota(jnp.int32, (CH, CH), 0)
        ci = lax.broadcasted_iota(jnp.int32, (CH, CH), 1)
        tril = (ri >= ci).astype(F32)
        for c in range(tm // CH):
            rs = slice(c * CH, (c + 1) * CH)
            for g in range(NG):
                cs_ = slice(g * 256, (g + 1) * 256)
                vn = vn_s[rs, cs_]
                sv = _dot(ws_ref[g], vn) + bs_ref[g]
                z = z_ref[rs, cs_].astype(F32)
                sz = _sigmoid(z)
                ug, dug_du = _gelu_and_grad(u_ref[rs, cs_].astype(F32))
                dy = dy_ref[rs, cs_].astype(F32)
                zs = z * sz
                t = dy * zs
                dsv_f = dy * sv
                du_ref[rs, cs_] = (zs * dsv_f * dug_du).astype(BF16)
                dz_ref[rs, cs_] = (dsv_f * ug * (sz + zs * (1.0 - sz))).astype(BF16)
                dsv = (t * ug).astype(BF16)
                dsv_s[rs, cs_] = dsv
                dvn_s[rs, cs_] = _dot(wst_ref[g], dsv)
                dws_ref[g] += _dot_nt(dsv, vn) * tril
            dbs_ref[...] += _dot(dsv_s[rs, :], gsel_ref[...])
        dvn = dvn_s[...]
        dln_ref[0:1, :] += jnp.sum(dvn * vhat, axis=0, keepdims=True)
        dln_ref[1:2, :] += jnp.sum(dvn, axis=0, keepdims=True)
        dvh = dvn * lg_ref[...]
        dvg = rstd * (dvh - jnp.mean(dvh, axis=-1, keepdims=True) - vhat * jnp.mean(dvh * vhat, axis=-1, keepdims=True))
        dv_ref[...] = (dvg * dvg_dv).astype(BF16)

    blk = lambda off: pl.BlockSpec((tm, D), lambda i: (i, off // D))
    row = pl.BlockSpec((tm, D), lambda i: (i, 0))
    return _call(
        body, ex, name="gmlp_bwd", grid=(T // tm,),
        in_specs=[blk(OFF_U), blk(OFF_V), blk(OFF_ZA), row, _full((1, D)), _full((1, D)), _full((NG, CH, CH)),
                  _full((NG, CH, CH)), _full((NG, CH, 256)), _full((D, 128))],
        out_specs=[pl.BlockSpec((tm, 3 * D), lambda i: (i, 0)), _full((NG, CH, CH)), _full((CH, 128)), _full((8, D))],
        out_shape=[jax.ShapeDtypeStruct((T, 3 * D), BF16),
                   jax.ShapeDtypeStruct((NG, CH, CH), F32), jax.ShapeDtypeStruct((CH, 128), F32), jax.ShapeDtypeStruct((8, D), F32)],
        scratch_shapes=[pltpu.VMEM((tm, D), BF16), pltpu.VMEM((tm, D), BF16), pltpu.VMEM((tm, D), F32)],
        args=(proj, proj, proj, dya, ln_g, ln_b, ws, wst, bst, gsel))


def _zb_cols(zb0_ref, zb1_ref, cols):
    ref = zb0_ref if cols.start < D else zb1_ref
    return ref[:, cols.start % D:cols.start % D + (cols.stop - cols.start)]


def _shift_matrix(down):
    t = jnp.arange(CH)[None, :, None]
    j = jnp.arange(1, 4)[:, None, None]
    col = jnp.arange(2 * CH)[None, None, :]
    src = CH + t - j if down else t + j
    return (col == src).astype(BF16).reshape(3 * CH, 2 * CH)


def _conv_pre(x, moved, cw_ref, cb_ref):
    pre = cb_ref[...] + cw_ref[3:4, :] * x
    for j in (1, 2, 3):
        pre = pre + cw_ref[3 - j:4 - j, :] * moved[(j - 1) * CH:j * CH]
    return pre


def _split_dot(x, w, parts, w_left=False):
    acc, r = None, x
    for k in range(parts):
        hi = r.astype(BF16)
        d = _dot(w, hi) if w_left else _dot(hi, w)
        acc = d if acc is None else acc + d
        if k + 1 < parts:
            r = r - hi.astype(F32)
    return acc


def _chunk_decays(dt, alog_ref, e_ref, cs_s, cst_s, csx_s):
    a = -jnp.exp(alog_ref[...])
    ri = lax.broadcasted_iota(jnp.int32, (CH, CH), 0)
    ci = lax.broadcasted_iota(jnp.int32, (CH, CH), 1)
    tril = ri >= ci
    cs = _split_dot(dt * a, tril.astype(BF16), 3, w_left=True)
    cs_s[...] = cs
    cst_s[...] = cs.T
    csx_s[...] = _split_dot(cs, e_ref[...], 3)
    return a, tril, ri, ci


def _lmat(cst_s, h, tril):
    rowb = jnp.broadcast_to(cst_s[h:h + 1, :], (CH, CH))
    return jnp.exp(jnp.where(tril, rowb.T - rowb, -jnp.inf))


def _head_pair_rows(v, lane):
    return jnp.concatenate([jnp.where(lane < HD, v, 0.0), jnp.where(lane < HD, 0.0, v)], axis=0).astype(BF16)


def _ssd_fwd(proj, dtr, cw, cb, dtb, alog, dskx, sg, e128):
    T = proj.shape[0]
    nc = T // CH

    def body(xbc_ref, zb0_ref, zb1_ref, dt_ref, cw_ref, cb_ref, dtb_ref, alog_ref, dx_ref, sg_ref, e_ref, shift_ref,
             y_ref, yb_ref, hp_ref, pre_ref, xx_s, h_s, cs_s, cst_s, csx_s, yz_s):
        @pl.when(pl.program_id(0) == 0)
        def _():
            xx_s[...] = jnp.zeros_like(xx_s)
            h_s[...] = jnp.zeros_like(h_s)

        xx_s[CH:, :] = xbc_ref[...]
        moved = _dot(shift_ref[...], xx_s[...])
        xx_s[CH - 16:CH, :] = xbc_ref[CH - 16:, :]
        x = xbc_ref[...].astype(F32)
        pre = _conv_pre(x, moved, cw_ref, cb_ref)
        pre_ref[...] = pre
        xc = pre * _sigmoid(pre)
        dt = _softplus(dt_ref[...] + dtb_ref[...])
        a, tril, _, lane = _chunk_decays(dt, alog_ref, e_ref, cs_s, cst_s, csx_s)
        dt_x = _split_dot(dt, e_ref[...], 2)
        cs_last_x = csx_s[CH - 1:CH, :]
        hp_ref[0] = h_s[...]
        for g in range(NG):
            gs = slice(g * 512, (g + 1) * 512)
            bg = xc[:, DI + g * NST:DI + (g + 1) * NST].astype(BF16)
            cg = xc[:, DI + 512 + g * NST:DI + 512 + (g + 1) * NST].astype(BF16)
            cbm = _dot_nt(cg, bg)
            xg = xc[:, gs]
            xdt = xg * dt_x[:, gs]
            hprev = h_s[:, gs]
            csx = csx_s[:, gs]
            yoff = _dot(cg, hprev.astype(BF16)) * jnp.exp(csx)
            st = _dot_tn(bg, (xdt * jnp.exp(cs_last_x[:, gs] - csx)).astype(BF16))
            h_s[:, gs] = jnp.exp(cs_last_x[:, gs]) * hprev + st
            ssq = jnp.zeros((CH, 1), F32)
            for q in range(4):
                h0 = g * 8 + 2 * q
                ps = slice(q * 128, (q + 1) * 128)
                cols = slice(g * 512 + q * 128, g * 512 + (q + 1) * 128)
                m01 = jnp.concatenate([cbm * _lmat(cst_s, h0, tril), cbm * _lmat(cst_s, h0 + 1, tril)], axis=1).astype(BF16)
                yq = _dot(m01, _head_pair_rows(xdt[:, ps], lane)) + yoff[:, ps] + xg[:, ps] * dx_ref[:, cols]
                y_ref[:, cols] = yq
                z = _zb_cols(zb0_ref, zb1_ref, cols).astype(F32)
                yz = yq * z * _sigmoid(z)
                yz_s[:, cols] = yz
                ssq = ssq + jnp.sum(yz * yz, axis=1, keepdims=True)
            rg = lax.rsqrt(ssq * (1.0 / 512.0) + EPS)
            yb_ref[:, gs] = (yz_s[:, gs] * rg * sg_ref[:, gs]).astype(BF16)

    return pl.pallas_call(
        body, name="ssd_fwd", grid=(nc,),
        in_specs=[pl.BlockSpec((CH, CD), lambda c: (c, OFF_XBC // CD)), pl.BlockSpec((CH, D), lambda c: (c, OFF_ZB // D)),
                  pl.BlockSpec((CH, D), lambda c: (c, OFF_ZB // D + 1)),
                  pl.BlockSpec((CH, DTW), lambda c: (c, 0)), _full((4, CD)), _full((1, CD)), _full((1, DTW)),
                  _full((1, DTW)), _full((1, DI)), _full((1, DI)), _full((DTW, DI)), _full((3 * CH, 2 * CH))],
        out_specs=[pl.BlockSpec((CH, DI), lambda c: (c, 0)), pl.BlockSpec((CH, DI), lambda c: (c, 0)),
                   pl.BlockSpec((1, NST, DI), lambda c: (c, 0, 0)), pl.BlockSpec((CH, CD), lambda c: (c, 0))],
        out_shape=[jax.ShapeDtypeStruct((T, DI), F32), jax.ShapeDtypeStruct((T, DI), BF16),
                   jax.ShapeDtypeStruct((nc, NST, DI), F32), jax.ShapeDtypeStruct((T, CD), F32)],
        scratch_shapes=[pltpu.VMEM((2 * CH, CD), BF16), pltpu.VMEM((NST, DI), F32), pltpu.VMEM((CH, CH), F32),
                        pltpu.VMEM((CH, CH), F32), pltpu.VMEM((CH, DI), F32), pltpu.VMEM((CH, DI), F32)],
        compiler_params=_cp(("arbitrary",)),
    )(proj, proj, proj, dtr, cw, cb, dtb, alog, dskx, sg, e128, _shift_matrix(down=True))


def _ssd_bwd(proj, pre_all, dtr, y, dyb, hprev_all, cw, dtb, alog, dskx, sg, e128, et128, ex):
    T = proj.shape[0]
    nc = T // CH

    def body(xbc_ref, pre_ref, zb0_ref, zb1_ref, dt_ref, y_ref, dyb_ref, hp_ref, cw_ref, dtb_ref, alog_ref, dx_ref, sg_ref,
             e_ref, et_ref, shift_ref, d_ref, ddt_ref, dcw_ref, dsm_ref, dsg_ref,
             g_s, dd_s, cs_s, cst_s, csx_s, dy_s, dxdt_s, dxs_s, dsd_s, dxc_s, gh_s):
        dxbc_ref, dzb_ref = d_ref.at[:, 0:CD], d_ref.at[:, CD:CD + DI]
        i = pl.program_id(0)

        @pl.when(i == 0)
        def _():
            g_s[...] = jnp.zeros_like(g_s)
            dd_s[...] = jnp.zeros_like(dd_s)
            dcw_ref[...] = jnp.zeros_like(dcw_ref)
            dsm_ref[...] = jnp.zeros_like(dsm_ref)
            dsg_ref[...] = jnp.zeros_like(dsg_ref)

        pre = pre_ref[...]
        sp = _sigmoid(pre)
        xc = pre * sp
        dtr = dt_ref[...] + dtb_ref[...]
        dt = _softplus(dtr)
        a, tril, ri, lane = _chunk_decays(dt, alog_ref, e_ref, cs_s, cst_s, csx_s)
        et = et_ref[...]
        dt_x = _split_dot(dt, e_ref[...], 2)
        cs_last_x = csx_s[CH - 1:CH, :]

        for g in range(NG):
            gs = slice(g * 512, (g + 1) * 512)
            z = _zb_cols(zb0_ref, zb1_ref, gs).astype(F32)
            sz = _sigmoid(z)
            yv = y_ref[:, gs]
            zs = z * sz
            yz = yv * zs
            rg = lax.rsqrt(jnp.mean(yz * yz, axis=-1, keepdims=True) + EPS)
            yn = yz * rg
            dyb = dyb_ref[:, gs].astype(F32)
            dsg_ref[0:1, gs] += jnp.sum(dyb * yn, axis=0, keepdims=True)
            dyn = dyb * sg_ref[:, gs]
            dyz = rg * (dyn - yn * jnp.mean(dyn * yn, axis=-1, keepdims=True))
            dy_s[:, gs] = dyz * zs
            dzb_ref[:, gs] = (dyz * yv * (sz + zs * (1.0 - sz))).astype(BF16)

        rsum = jnp.zeros((CH, DTW), F32)
        csum_t = jnp.zeros((DTW, CH), F32)
        for g in range(NG):
            gs = slice(g * 512, (g + 1) * 512)
            bg = xc[:, DI + g * NST:DI + (g + 1) * NST].astype(BF16)
            cg = xc[:, DI + 512 + g * NST:DI + 512 + (g + 1) * NST].astype(BF16)
            cbm = _dot_nt(cg, bg)
            xdt = xc[:, gs] * dt_x[:, gs]
            hprev = hp_ref[0, :, gs]
            hpb = hprev.astype(BF16)
            gn = g_s[:, gs]
            gnb = gn.astype(BF16)
            dy = dy_s[:, gs]
            csx = csx_s[:, gs]
            ecs = jnp.exp(csx)
            dec = jnp.exp(cs_last_x[:, gs] - csx)
            dye_f = dy * ecs
            dye = dye_f.astype(BF16)
            dc = _dot_nt(dye, hpb)
            dprev = _dot_tn(cg, dye)
            dxdt_state = dec * _dot(bg, gnb)
            db = _dot_nt((xdt * dec).astype(BF16), gnb)
            dcb = jnp.zeros((CH, CH), F32)
            for q in range(4):
                h0 = g * 8 + 2 * q
                ps = slice(q * 128, (q + 1) * 128)
                dyp = dy[:, ps]
                l0 = _lmat(cst_s, h0, tril)
                l1 = _lmat(cst_s, h0 + 1, tril)
                m0 = cbm * l0
                m1 = cbm * l1
                dm = _dot_nt(dyp.astype(BF16), _head_pair_rows(xdt[:, ps], lane))
                dm0 = dm[:, :CH]
                dm1 = dm[:, CH:]
                dcb = dcb + dm0 * l0 + dm1 * l1
                for hh, qm in ((h0, dm0 * m0), (h0 + 1, dm1 * m1)):
                    rsum = jnp.where(lane == hh, jnp.sum(qm, axis=1, keepdims=True), rsum)
                    csum_t = jnp.where(ri == hh, jnp.sum(qm, axis=0, keepdims=True), csum_t)
                mst = jnp.concatenate([m0, m1], axis=0).astype(BF16)
                d = _dot_tn(mst, _head_pair_rows(dyp, lane))
                dxdt_s[:, g * 512 + q * 128:g * 512 + (q + 1) * 128] = d + dxdt_state[:, ps]
            dsd_s[:, gs] = xdt * dxdt_state
            dxs_s[:, gs] = dye_f * _dot(cg, hpb)
            dcbb = dcb.astype(BF16)
            dxc_s[:, DI + 512 + g * NST:DI + 512 + (g + 1) * NST] = dc + _dot(dcbb, bg)
            dxc_s[:, DI + g * NST:DI + (g + 1) * NST] = db + _dot_tn(dcbb, cg)
            gh_s[:, gs] = jnp.broadcast_to(jnp.sum(gn * hprev, axis=0, keepdims=True), (8, 512))
            g_s[:, gs] = dprev + jnp.exp(cs_last_x[:, gs]) * gn

        xs = xc[:, :DI]
        dy = dy_s[...]
        dxdt = dxdt_s[...]
        cs_last = cs_s[CH - 1:CH, :]
        state_e = _split_dot(dsd_s[...], et, 1)
        dcd = 0.125 * jnp.sum(_split_dot(gh_s[...], et, 2), axis=0, keepdims=True) * jnp.exp(cs_last)
        row = lax.broadcasted_iota(jnp.int32, (CH, 1), 0)
        dcs = rsum - csum_t.T + _split_dot(dxs_s[...], et, 1) - state_e
        dcs = dcs + jnp.where(row == CH - 1, jnp.sum(state_e, axis=0, keepdims=True) + dcd, 0.0)
        dda = _split_dot(dcs, (lane >= ri).astype(BF16), 3, w_left=True)
        ddt = dda * a + _dot((dxdt * xs).astype(BF16), et)
        ddtr = jnp.where(lane < NH, ddt * _sigmoid(dtr), 0.0)
        ddt_ref[...] = ddtr.astype(BF16)
        dsm_ref[0:1, :] += jnp.sum(ddtr, axis=0, keepdims=True)
        dsm_ref[1:2, :] += jnp.sum(dda * dt, axis=0, keepdims=True) * a
        dsm_ref[2:3, :] += jnp.sum(_dot((dy * xs).astype(BF16), et), axis=0, keepdims=True)
        dxc_s[:, :DI] = dxdt * dt_x + dy * dx_ref[...]

        dpre = dxc_s[...] * (sp + xc * (1.0 - sp))
        dpre_b = dpre.astype(BF16)
        dd_s[:CH, :] = dpre_b
        moved = _dot(shift_ref[...], dd_s[...])
        dd_s[CH:CH + 16, :] = dpre_b[:16]
        x = xbc_ref[...].astype(F32)
        dcw_ref[4:5, :] += jnp.sum(dpre, axis=0, keepdims=True)
        dxbc = cw_ref[3:4, :] * dpre
        dcw_ref[3:4, :] += jnp.sum(dpre * x, axis=0, keepdims=True)
        for j in (1, 2, 3):
            ahead = moved[(j - 1) * CH:j * CH]
            dcw_ref[3 - j:4 - j, :] += jnp.sum(ahead * x, axis=0, keepdims=True)
            dxbc = dxbc + cw_ref[3 - j:4 - j, :] * ahead
        dxbc_ref[...] = dxbc.astype(BF16)

    rev = lambda c: nc - 1 - c
    return _call(
        body, ex, name="ssd_bwd", grid=(nc,),
        in_specs=[pl.BlockSpec((CH, CD), lambda c: (rev(c), OFF_XBC // CD)),
                  pl.BlockSpec((CH, CD), lambda c: (rev(c), 0)),
                  pl.BlockSpec((CH, D), lambda c: (rev(c), OFF_ZB // D)), pl.BlockSpec((CH, D), lambda c: (rev(c), OFF_ZB // D + 1)),
                  pl.BlockSpec((CH, DTW), lambda c: (rev(c), 0)),
                  pl.BlockSpec((CH, DI), lambda c: (rev(c), 0)), pl.BlockSpec((CH, DI), lambda c: (rev(c), 0)),
                  pl.BlockSpec((1, NST, DI), lambda c: (rev(c), 0, 0)),
                  _full((4, CD)), _full((1, DTW)), _full((1, DTW)), _full((1, DI)), _full((1, DI)),
                  _full((DTW, DI)), _full((DI, DTW)), _full((3 * CH, 2 * CH))],
        out_specs=[pl.BlockSpec((CH, CD + DI), lambda c: (rev(c), 0)),
                   pl.BlockSpec((CH, DTW), lambda c: (rev(c), 0)), _full((8, CD)), _full((8, DTW)), _full((8, DI))],
        out_shape=[jax.ShapeDtypeStruct((T, CD + DI), BF16), jax.ShapeDtypeStruct((T, DTW), BF16),
                   jax.ShapeDtypeStruct((8, CD), F32), jax.ShapeDtypeStruct((8, DTW), F32), jax.ShapeDtypeStruct((8, DI), F32)],
        scratch_shapes=[pltpu.VMEM((NST, DI), F32), pltpu.VMEM((2 * CH, CD), BF16), pltpu.VMEM((CH, CH), F32), pltpu.VMEM((CH, CH), F32),
                        pltpu.VMEM((CH, DI), F32), pltpu.VMEM((CH, DI), F32), pltpu.VMEM((CH, DI), F32), pltpu.VMEM((CH, DI), F32),
                        pltpu.VMEM((CH, DI), F32), pltpu.VMEM((CH, CD), F32), pltpu.VMEM((8, DI), F32)],
        args=(proj, pre_all, proj, proj, dtr, y, dyb, hprev_all, cw, dtb, alog, dskx, sg, e128, et128, _shift_matrix(down=False)))


def _merge_tile():
    return 256


def _merge(x, yb, proj, p, tgt, gmlp, w_oa, w_ob, w_out, w_pg, w_ple, ple_g, fin_g):
    T = x.shape[0]
    tm = min(T, _merge_tile())

    def body(x_ref, u_ref, v_ref, za_ref, yb_ref, ga_ref, gb_ref, p_ref, t_ref, lg_ref, lb_ref, ws_ref, bs_ref,
             woa, wob, wout, wpg, wple, pg_ref, fg_ref,
             dx1_ref, dx1b_ref, ya_ref, mg_ref, hp_ref, dpre_ref, dpe_ref, doa_ref, dob_ref, dya_ref, dyb_ref, dg_ref, acc_ref, vn_s):
        @pl.when(pl.program_id(0) == 0)
        def _():
            acc_ref[...] = jnp.zeros_like(acc_ref)

        _gmlp_fwd_tile(u_ref, v_ref, za_ref, lg_ref, lb_ref, ws_ref, bs_ref, ya_ref, vn_s)
        oa = _dot(ya_ref[...], woa[...])
        ob = _dot(yb_ref[...], wob[...])
        sa = _sigmoid(ga_ref[...].astype(F32))
        sb = _sigmoid(gb_ref[...].astype(F32))
        mg = sa * oa + sb * ob
        mgb = mg.astype(BF16)
        mg_ref[...] = mgb
        x1 = x_ref[...] + _dot(mgb, wout[...])
        r2 = lax.rsqrt(jnp.mean(x1 * x1, axis=-1, keepdims=True) + EPS)
        xh1 = x1 * r2
        hpb = (xh1 * pg_ref[...]).astype(BF16)
        hp_ref[...] = hpb
        gate = _sigmoid(_dot(hpb, wpg[...]))
        pe = _dot(p_ref[...].astype(BF16), wple[...])
        x2 = x1 + gate * pe
        r3 = lax.rsqrt(jnp.mean(x2 * x2, axis=-1, keepdims=True) + EPS)
        xh2 = x2 * r3
        err = xh2 * fg_ref[...] - t_ref[...]
        acc_ref[2:3, :] += 0.5 * jnp.sum(jnp.mean(err * err, axis=-1, keepdims=True))
        dyo = err * (1.0 / D)
        acc_ref[0:1, :] += jnp.sum(dyo * xh2, axis=0, keepdims=True)
        dn = dyo * fg_ref[...]
        dx2 = r3 * (dn - xh2 * jnp.mean(dn * xh2, axis=-1, keepdims=True))
        dpe_ref[...] = (dx2 * gate).astype(BF16)
        dpre = (dx2 * pe * gate * (1.0 - gate)).astype(BF16)
        dpre_ref[...] = dpre
        dhp = _dot_nt(dpre, wpg[...])
        acc_ref[1:2, :] += jnp.sum(dhp * xh1, axis=0, keepdims=True)
        dhn = dhp * pg_ref[...]
        dx1 = dx2 + r2 * (dhn - xh1 * jnp.mean(dhn * xh1, axis=-1, keepdims=True))
        dx1_ref[...] = dx1
        dx1b = dx1.astype(BF16)
        dx1b_ref[...] = dx1b
        dmg = _dot_nt(dx1b, wout[...])
        doa = (dmg * sa).astype(BF16)
        dob = (dmg * sb).astype(BF16)
        doa_ref[...] = doa
        dob_ref[...] = dob
        dg_ref[:, :D] = (dmg * oa * sa * (1.0 - sa)).astype(BF16)
        dg_ref[:, D:] = (dmg * ob * sb * (1.0 - sb)).astype(BF16)
        dya_ref[...] = _dot_nt(doa, woa[...]).astype(BF16)
        dyb_ref[...] = _dot_nt(dob, wob[...]).astype(BF16)

    row = lambda w: pl.BlockSpec((tm, w), lambda i: (i, 0))
    blk = lambda off: pl.BlockSpec((tm, D), lambda i: (i, off // D))
    wsp = lambda s: pl.BlockSpec(s, lambda i: (0, 0), pipeline_mode=pl.Buffered(1))
    return pl.pallas_call(
        body, name="merge", grid=(T // tm,),
        in_specs=[row(D), blk(OFF_U), blk(OFF_V), blk(OFF_ZA), row(DI), blk(OFF_GA), blk(OFF_GB), row(PLE), row(D),
                  _full((1, D)), _full((1, D)), _full((NG, CH, CH)), _full((NG, CH, 256)),
                  wsp((D, D)), wsp((DI, D)), wsp((D, D)), wsp((D, D)), wsp((PLE, D)), _full((1, D)), _full((1, D))],
        out_specs=[row(D)] * 10 + [row(DI), row(2 * D), _full((8, D))],
        out_shape=[jax.ShapeDtypeStruct((T, D), F32)] + [jax.ShapeDtypeStruct((T, D), BF16)] * 9
        + [jax.ShapeDtypeStruct((T, DI), BF16), jax.ShapeDtypeStruct((T, 2 * D), BF16), jax.ShapeDtypeStruct((8, D), F32)],
        scratch_shapes=[pltpu.VMEM((tm, D), BF16)],
        compiler_params=_cp(("arbitrary",)),
    )(x, proj, proj, proj, yb, proj, proj, p, tgt, *gmlp, w_oa, w_ob, w_out, w_pg, w_ple, ple_g, fin_g)


def _wgrad(a, b, name):
    T, K = a.shape
    N = b.shape[1]
    tt, tk, tn = min(T, 2048), min(K, 1024), min(N, 1024)
    nt = T // tt

    def body(a_ref, b_ref, o_ref, acc_s):
        t = pl.program_id(2)

        @pl.when(t == 0)
        def _():
            acc_s[...] = jnp.zeros_like(acc_s)

        acc_s[...] += _dot_tn(a_ref[...].astype(BF16), b_ref[...])

        @pl.when(t == nt - 1)
        def _():
            o_ref[...] = acc_s[...].astype(BF16)

    return pl.pallas_call(
        body, name=name, grid=(K // tk, N // tn, nt),
        in_specs=[pl.BlockSpec((tt, tk), lambda k, n, t: (t, k)), pl.BlockSpec((tt, tn), lambda k, n, t: (t, n))],
        out_specs=pl.BlockSpec((tk, tn), lambda k, n, t: (k, n)),
        out_shape=jax.ShapeDtypeStruct((K, N), BF16),
        scratch_shapes=[pltpu.VMEM((tk, tn), F32)],
        compiler_params=_cp(("parallel", "parallel", "arbitrary")),
    )(a, b)


def _proj_bwd(x, dx1, norm_g, wt, pieces, ddt, ex):
    T = x.shape[0]
    tm = min(T, 1024)
    nk = OFF_DT // D + 1
    starts = [sum(a.shape[1] for a in pieces[:n]) // D for n in range(len(pieces))]
    ranges = [(s, s + a.shape[1] // D) for s, a in zip(starts, pieces)]
    assert ranges[-1][1] == nk - 1
    npc = len(pieces)

    def body(x_hbm, dx1_hbm, g_ref, w_ref, wdt_ref, *rest):
        piece_refs, ddt_ref, gx_ref, dng_ref = rest[:npc], rest[npc], rest[npc + 1], rest[npc + 2]
        acc_s, x_ref, dx1_ref, row_sems = rest[npc + 3:]
        i, k = pl.program_id(0), pl.program_id(1)
        rows = pl.ds(pl.multiple_of(i * tm, tm), tm)
        fetches = [pltpu.make_async_copy(x_hbm.at[rows], x_ref, row_sems.at[0]),
                   pltpu.make_async_copy(dx1_hbm.at[rows], dx1_ref, row_sems.at[1])]

        @pl.when((i == 0) & (k == 0))
        def _():
            dng_ref[...] = jnp.zeros_like(dng_ref)

        @pl.when(k == 0)
        def _():
            acc_s[...] = jnp.zeros_like(acc_s)

        @pl.when(k == 1)
        def _():
            for cp in fetches:
                cp.start()

        for ref, (lo, hi) in zip(piece_refs, ranges):
            @pl.when((k >= lo) & (k < hi))
            def _(ref=ref):
                acc_s[...] += _dot(ref[...], w_ref[...])

        @pl.when(k == nk - 1)
        def _():
            for cp in fetches:
                cp.wait()
            dh = acc_s[...] + _dot(ddt_ref[...], wdt_ref[...])
            xf = x_ref[...]
            r = lax.rsqrt(jnp.mean(xf * xf, axis=-1, keepdims=True) + EPS)
            xh = xf * r
            dng_ref[0:1, :] += jnp.sum(dh * xh, axis=0, keepdims=True)
            dxn = dh * g_ref[...]
            gx_ref[...] = dx1_ref[...] + r * (dxn - xh * jnp.mean(dxn * xh, axis=-1, keepdims=True))

    def piece_spec(lo, hi):
        return pl.BlockSpec((tm, D), lambda i, k: (i, jnp.clip(k - lo, 0, hi - lo - 1)))

    row = pl.BlockSpec((tm, D), lambda i, k: (i, 0))
    return _call(
        body, ex, name="proj_bwd", grid=(T // tm, nk),
        in_specs=[_ANY, _ANY, _full((1, D)), pl.BlockSpec((D, D), lambda i, k: (jnp.minimum(k, nk - 2), 0)),
                  pl.BlockSpec((DTW, D), lambda i, k: (OFF_DT // DTW, 0))]
        + [piece_spec(lo, hi) for lo, hi in ranges] + [pl.BlockSpec((tm, DTW), lambda i, k: (i, 0))],
        out_specs=[row, _full((8, D))],
        out_shape=[jax.ShapeDtypeStruct((T, D), F32), jax.ShapeDtypeStruct((8, D), F32)],
        scratch_shapes=[pltpu.VMEM((tm, D), F32), pltpu.VMEM((tm, D), F32), pltpu.VMEM((tm, D), F32), pltpu.SemaphoreType.DMA((2,))],
        args=(x, dx1, norm_g, wt, wt, *pieces, ddt))


def _elementwise_tile(R, C, limit=1 << 20):
    if R * C * 4 <= limit:
        return R, C
    rows = [t for t in range(16, R, 16) if R % t == 0 and t * C * 4 <= limit]
    if rows:
        return rows[-1], C
    cols = [t for t in range(128, C, 128) if C % t == 0 and R * t * 4 <= limit]
    return R, cols[-1]


def _adam_update(w, m, v, g):
    c1 = 1.0 - ADAM_B1 ** ADAM_STEP
    c2 = 1.0 - ADAM_B2 ** ADAM_STEP
    mm = ADAM_B1 * m + (1.0 - ADAM_B1) * g
    vv = ADAM_B2 * v + (1.0 - ADAM_B2) * (g * g)
    return -ADAM_LR * ((mm / c1) / (jnp.sqrt(vv / c2) + ADAM_EPS) + ADAM_WD * w), mm, vv


def _sum_parts(parts, name, ex):
    first = parts[0][0] if isinstance(parts[0], tuple) else parts[0]
    R, C = first.shape[-2:]
    tr, tc = _elementwise_tile(R, C)
    flat = pl.BlockSpec((tr, tc), lambda i, j: (i, j))
    slot = lambda k: pl.BlockSpec((None, tr, tc), lambda i, j: (k, i, j))
    specs = [slot(p[1]) if isinstance(p, tuple) else flat for p in parts]
    arrays = [p[0] if isinstance(p, tuple) else p for p in parts]

    def body(*refs):
        g = refs[0][...].astype(F32)
        for r in refs[1:-1]:
            g = g + r[...].astype(F32)
        refs[-1][...] = g

    (out,), got = _call(body, ex, name=name, grid=(R // tr, C // tc), in_specs=specs, out_specs=[flat],
                        out_shape=[jax.ShapeDtypeStruct((R, C), F32)], scratch_shapes=[], args=arrays)
    return out, got


def _adamw_tiles(w, m, v, g, name):
    R = w.shape[0]
    tr = 107
    assert R % tr == 0
    spec = pl.BlockSpec((tr, 8, 128), lambda i: (i, 0, 0))

    def body(w_ref, m_ref, v_ref, g_in, g_ref, d_ref, nm_ref, nv_ref):
        g = g_in[...]
        g_ref[...] = g
        d_ref[...], nm_ref[...], nv_ref[...] = _adam_update(w_ref[...], m_ref[...], v_ref[...], g)

    return pl.pallas_call(body, name=name, grid=(R // tr,), in_specs=[spec] * 4, out_specs=[spec] * 4,
                          out_shape=[jax.ShapeDtypeStruct(w.shape, F32)] * 4, compiler_params=_cp(("parallel",)))(w, m, v, g)


def _adamw(w, m, v, parts, name, part_row=0, block_rows=None):
    R, C = w.shape
    tr, tc = _elementwise_tile(R, C) if block_rows is None else (block_rows, C)
    assert R % tr == 0 and part_row % tr == 0
    first = part_row // tr
    n = len(parts)
    wspec = pl.BlockSpec((tr, tc), lambda i, j: (i, j))
    flat = pl.BlockSpec((tr, tc), lambda i, j: (first + i, j))
    slot = lambda k: pl.BlockSpec((None, tr, tc), lambda i, j: (k, first + i, j))
    part_specs = [slot(p[1]) if isinstance(p, tuple) else flat for p in parts]
    part_arrays = [p[0] if isinstance(p, tuple) else p for p in parts]

    def body(*refs):
        w_ref, m_ref, v_ref = refs[:3]
        g_ref, d_ref, nm_ref, nv_ref = refs[3 + n:]
        g = refs[3][...].astype(F32)
        for r in refs[4:3 + n]:
            g = g + r[...].astype(F32)
        g_ref[...] = g
        d_ref[...], nm_ref[...], nv_ref[...] = _adam_update(w_ref[...], m_ref[...], v_ref[...], g)

    return pl.pallas_call(
        body, name=name, grid=(R // tr, C // tc), in_specs=[wspec] * 3 + part_specs, out_specs=[wspec] * 4,
        out_shape=[jax.ShapeDtypeStruct(w.shape, F32)] * 4, compiler_params=_cp(("parallel", "parallel")),
    )(w, m, v, *part_arrays)


_SMALL_WIDE = {"ln_a_g": 0, "ln_a_b": 1, "final_g": 8, "ple_norm_g": 9, "ssm_norm_g": 16, "conv_b": 44}
_WIDE_CONV_W, _WIDE_ROWS = 32, 56
_WIDE_LOSS = 10
_SMALL_NARROW = {"w_s": (0, NG * CH, 128), "b_s": (512, NG, 128), "dt_bias": (520, 1, NH), "a_log": (521, 1, NH),
                 "d_skip": (522, 1, NH)}
_NARROW_ROWS = 528
_SMALL_SHAPES = {"norm_g": (1, D), "ln_a_g": (1, D), "ln_a_b": (1, D), "ple_norm_g": (1, D), "final_g": (1, D),
                 "ssm_norm_g": (1, DI), "conv_b": (1, CD), "w_s": (NG * CH, CH), "b_s": (NG, CH), "dt_bias": (1, NH),
                 "a_log": (1, NH), "d_skip": (1, NH)}


def _adamw_small(w, m, v, wide_all, narrow_all, norm_all):
    names = list(_SMALL_SHAPES)
    n = len(names)

    def body(*refs):
        wr, mr, vr = refs[:n], refs[n:2 * n], refs[2 * n:3 * n]
        wide_ref, narrow_ref, norm_ref = refs[3 * n:3 * n + 3]
        outs = refs[3 * n + 3:]
        gr, dr, nmr, nvr, cw_ref, loss_ref = outs[:n], outs[n:2 * n], outs[2 * n:3 * n], outs[3 * n:4 * n], outs[4 * n], outs[4 * n + 1]

        def total(ref, rows, lanes):
            acc = ref[0, rows, lanes]
            for d in range(1, NDEV):
                acc = acc + ref[d, rows, lanes]
            return acc

        for k, name in enumerate(names):
            if name in _SMALL_WIDE or name == "norm_g":
                for part in range(_SMALL_SHAPES[name][1] // D):
                    pack, r = (norm_ref, 0) if name == "norm_g" else (wide_ref, _SMALL_WIDE[name] + part)
                    cols = slice(part * D, (part + 1) * D)
                    g = total(pack, slice(r, r + 1), slice(None))
                    gr[k][:, cols] = g
                    dr[k][:, cols], nmr[k][:, cols], nvr[k][:, cols] = _adam_update(wr[k][:, cols], mr[k][:, cols], vr[k][:, cols], g)
            else:
                r, rows, lanes = _SMALL_NARROW[name]
                g = total(narrow_ref, slice(r, r + rows), slice(0, lanes))
                gr[k][...] = g
                dr[k][...], nmr[k][...], nvr[k][...] = _adam_update(wr[k][...], mr[k][...], vr[k][...], g)
        cw_ref[...] = total(wide_ref, slice(_WIDE_CONV_W, _WIDE_CONV_W + 12), slice(None))
        loss_ref[...] = total(wide_ref, slice(_WIDE_LOSS, _WIDE_LOSS + 1), slice(None))

    shapes = [jax.ShapeDtypeStruct(_SMALL_SHAPES[k], F32) for k in names]
    specs = [_full(_SMALL_SHAPES[k]) for k in names]
    res = pl.pallas_call(
        body, name="adamw_small", grid=(1,),
        in_specs=specs * 3 + [_full(wide_all.shape), _full(narrow_all.shape), _full(norm_all.shape)],
        out_specs=specs * 4 + [_full((12, D)), _full((1, D))],
        out_shape=shapes * 4 + [jax.ShapeDtypeStruct((12, D), F32), jax.ShapeDtypeStruct((1, D), F32)],
        compiler_params=_cp(("arbitrary",)),
    )(*[w[k] for k in names], *[m[k] for k in names], *[v[k] for k in names], wide_all, narrow_all, norm_all)
    groups = [dict(zip(names, res[q * n:(q + 1) * n])) for q in range(4)]
    return groups[0], groups[1], groups[2], groups[3], res[4 * n], res[4 * n + 1][0, 0]


def _dev_index(px, py, pc):
    return 4 * px + 2 * py + pc


def _mesh_position():
    return lax.axis_index("x"), lax.axis_index("y"), lax.axis_index("c")


def _gather_exchange(blocks):
    n = len(blocks)

    def make(ins, outs, sems):
        send_sems, recv_sems, local_sems = sems
        x, y, c = _mesh_position()
        me, sibling = (x, y, c), (x, y, 1 - c)
        chips = [(1 - x, y), (x, 1 - y), (1 - x, 1 - y)]

        def copy(a, k, block, to, src=None):
            dst = outs[a].at[_dev_index(*block)]
            return pltpu.make_async_remote_copy(src_ref=dst if src is None else src, dst_ref=dst, send_sem=send_sems.at[a, k],
                                                recv_sem=recv_sems.at[a, k], device_id=to, device_id_type=MESH)

        mine = [pltpu.make_async_copy(ins[a], outs[a].at[_dev_index(*me)], local_sems.at[a]) for a in range(n)]
        first = []
        for a in range(n):
            first.append(copy(a, 0, me, sibling, src=ins[a]))
            first += [copy(a, 1 + j, me, (*chip, c), src=ins[a]) for j, chip in enumerate(chips)]

        def start():
            for cp in mine + first:
                cp.start()

        def finish():
            passed = []
            for j, chip in enumerate(chips):
                for a in range(n):
                    copy(a, 1 + j, (*chip, c), me).wait_recv()
                    fwd = copy(a, 4 + j, (*chip, c), sibling)
                    fwd.start()
                    passed.append(fwd)
            for a in range(n):
                copy(a, 0, sibling, me).wait_recv()
                for j, chip in enumerate(chips):
                    copy(a, 4 + j, (*chip, 1 - c), me).wait_recv()
            for cp in first + passed:
                cp.wait_send()
            for cp in mine:
                cp.wait()

        return start, finish

    return _Exchange(list(blocks), [jax.ShapeDtypeStruct((NDEV,) + b.shape, b.dtype) for b in blocks],
                     [pltpu.SemaphoreType.DMA((n, 7)), pltpu.SemaphoreType.DMA((n, 7)), pltpu.SemaphoreType.DMA((n,))], make)


def _relay_gather_exchange(blocks):
    n = len(blocks)

    def make(ins, outs, sems):
        send_sems, recv_sems, local_sems = sems
        x, y, c = _mesh_position()
        me, sibling = (x, y, c), (x, y, 1 - c)
        x_nbr, y_nbr, diag = (1 - x, y), (x, 1 - y), (1 - x, 1 - y)
        relay_from = (jnp.where(c == 0, x, 1 - x), jnp.where(c == 0, 1 - y, y))
        relay_to = (jnp.where(c == 0, 1 - x, x), jnp.where(c == 0, y, 1 - y))

        def copy(a, k, block, to, src=None):
            dst = outs[a].at[_dev_index(*block)]
            return pltpu.make_async_remote_copy(src_ref=dst if src is None else src, dst_ref=dst, send_sem=send_sems.at[a, k],
                                                recv_sem=recv_sems.at[a, k], device_id=to, device_id_type=MESH)

        mine = [pltpu.make_async_copy(ins[a], outs[a].at[_dev_index(*me)], local_sems.at[a]) for a in range(n)]
        first = []
        for a in range(n):
            first += [copy(a, 0, me, sibling, src=ins[a]), copy(a, 1, me, (*x_nbr, c), src=ins[a]), copy(a, 2, me, (*y_nbr, c), src=ins[a])]

        def start():
            for cp in mine + first:
                cp.start()

        def finish():
            later = []
            for a in range(n):
                copy(a, 1, (*x_nbr, c), me).wait_recv()
                copy(a, 2, (*y_nbr, c), me).wait_recv()
                later.append(copy(a, 3, (*relay_from, c), (*relay_to, c)))
                later += [copy(a, 4, (*x_nbr, c), sibling), copy(a, 5, (*y_nbr, c), sibling)]
                for cp in later[-3:]:
                    cp.start()
            for a in range(n):
                copy(a, 3, (*diag, c), me).wait_recv()
                later.append(copy(a, 6, (*diag, c), sibling))
                later[-1].start()
            for a in range(n):
                copy(a, 0, sibling, me).wait_recv()
                for k, chip in ((4, x_nbr), (5, y_nbr), (6, diag)):
                    copy(a, k, (*chip, 1 - c), me).wait_recv()
            for cp in first + later:
                cp.wait_send()
            for cp in mine:
                cp.wait()

        return start, finish

    return _Exchange(list(blocks), [jax.ShapeDtypeStruct((NDEV,) + b.shape, b.dtype) for b in blocks],
                     [pltpu.SemaphoreType.DMA((n, 7)), pltpu.SemaphoreType.DMA((n, 7)), pltpu.SemaphoreType.DMA((n,))], make)


def _combine(*exchanges):
    def make(ins, outs, sems):
        pairs = []
        for e in exchanges:
            ni, no, ns = len(e.arrays), len(e.out_shape), len(e.sems)
            pairs.append(e.make(ins[:ni], outs[:no], sems[:ns]))
            ins, outs, sems = ins[ni:], outs[no:], sems[ns:]

        def start():
            for s, _ in pairs:
                s()

        def finish():
            for _, f in pairs:
                f()

        return start, finish

    return _Exchange(sum((list(e.arrays) for e in exchanges), []), sum((list(e.out_shape) for e in exchanges), []),
                     sum((list(e.sems) for e in exchanges), []), make)


def _start_wait_all(copies, local=()):
    def start():
        for cp in list(local) + list(copies):
            cp.start()

    def finish():
        for cp in copies:
            cp.wait()
        for cp in local:
            cp.wait()

    return start, finish


def _no_exchange():
    return _Exchange([], [], [], lambda ins, outs, sems: (lambda: None, lambda: None))


def _direct_exchange(grads):
    n = len(grads)

    def make(ins, outs, sems):
        send_sems, recv_sems = sems
        x, y, c = _mesh_position()
        copies = []
        for a in range(n):
            for r in range(1, NDEV):
                peer = (x ^ ((r >> 2) & 1), y ^ ((r >> 1) & 1), c ^ (r & 1))
                copies.append(pltpu.make_async_remote_copy(
                    src_ref=ins[a].at[_dev_index(*peer)], dst_ref=outs[a].at[r - 1], send_sem=send_sems.at[a, r - 1],
                    recv_sem=recv_sems.at[a, r - 1], device_id=peer, device_id_type=MESH))
        return _start_wait_all(copies)

    return _Exchange(list(grads), [jax.ShapeDtypeStruct((NDEV - 1,) + g.shape[1:], g.dtype) for g in grads],
                     [pltpu.SemaphoreType.DMA((n, NDEV - 1)), pltpu.SemaphoreType.DMA((n, NDEV - 1))], make)


def _direct_gather_exchange(smalls):
    n = len(smalls)

    def make(ins, outs, sems):
        send_sems, recv_sems, local_sems = sems
        x, y, c = _mesh_position()
        copies, local = [], []
        for a in range(n):
            slot = outs[a].at[_dev_index(x, y, c)]
            local.append(pltpu.make_async_copy(ins[a], slot, local_sems.at[a]))
            for r in range(1, NDEV):
                peer = (x ^ ((r >> 2) & 1), y ^ ((r >> 1) & 1), c ^ (r & 1))
                copies.append(pltpu.make_async_remote_copy(src_ref=ins[a], dst_ref=slot, send_sem=send_sems.at[a, r - 1],
                                                           recv_sem=recv_sems.at[a, r - 1], device_id=peer, device_id_type=MESH))
        return _start_wait_all(copies, local=local)

    return _Exchange(list(smalls), [jax.ShapeDtypeStruct((NDEV,) + s.shape, s.dtype) for s in smalls],
                     [pltpu.SemaphoreType.DMA((n, 7)), pltpu.SemaphoreType.DMA((n, 7)), pltpu.SemaphoreType.DMA((n,))], make)


_W_IN_ROWS = {"u": (0, 1024), "v": (1024, 2048), "za": (2048, 3072), "zb": (3072, 5120), "xbc": (5120, 8192),
              "dt": (8192, 8224), "ga": (8224, 9248), "gb": (9248, 10272)}
_PROJ_ORDER = ("xbc", "zb", "u", "v", "za", "ga", "gb", "dt")


def _w_in_t_rows(wt):
    z = jnp.zeros((NP - NIN, wt.shape[1]), wt.dtype)
    return jnp.concatenate([wt[slice(*_W_IN_ROWS[n])] for n in _PROJ_ORDER] + [z], axis=0)


_WEIGHTS = ["norm_g", "w_in", "ln_a_g", "ln_a_b", "w_s", "b_s", "conv_w", "conv_b", "dt_bias", "a_log", "d_skip", "ssm_norm_g",
            "w_oa", "w_ob", "w_out", "ple_norm_g", "w_pg", "w_ple", "final_g"]


def _rows_pack(d, dtype):
    return jnp.concatenate([d["w_oa"].reshape(128, D), d["w_ob"].reshape(256, D), d["w_out"].reshape(128, D),
                            d["w_pg"].reshape(128, D), d["w_ple"].reshape(32, D)], axis=0).astype(dtype)


def kernel(x, p, norm_g, w_in, ln_a_g, ln_a_b, w_s, b_s, conv_w, conv_b, dt_bias, a_log, d_skip, ssm_norm_g, w_oa, w_ob, w_out, ple_norm_g, w_pg, w_ple, final_g, loss_target, m_norm_g, m_w_in, m_ln_a_g, m_ln_a_b, m_w_s, m_b_s, m_conv_w, m_conv_b, m_dt_bias, m_a_log, m_d_skip, m_ssm_norm_g, m_w_oa, m_w_ob, m_w_out, m_ple_norm_g, m_w_pg, m_w_ple, m_final_g, v_norm_g, v_w_in, v_ln_a_g, v_ln_a_b, v_w_s, v_b_s, v_conv_w, v_conv_b, v_dt_bias, v_a_log, v_d_skip, v_ssm_norm_g, v_w_oa, v_w_ob, v_w_out, v_ple_norm_g, v_w_pg, v_w_ple, v_final_g):
    args = dict(locals())
    w = {n: args[n] for n in _WEIGHTS}
    m = {n: args["m_" + n] for n in _WEIGHTS}
    v = {n: args["v_" + n] for n in _WEIGHTS}
    T = x.shape[1]
    xi, yi, ci = lax.axis_index("x"), lax.axis_index("y"), lax.axis_index("c")
    me = 4 * xi + 2 * yi + ci
    x2, p2, tgt = x.reshape(T, D), p.reshape(T, PLE), loss_target.reshape(T, D)

    norm_g2 = w["norm_g"].reshape(1, D)
    ws = jnp.where(jnp.tril(jnp.ones((CH, CH), bool))[None], w["w_s"].reshape(NG, CH, CH), 0.0).astype(BF16)
    wst = jnp.transpose(ws, (0, 2, 1))
    bst = jnp.broadcast_to(w["b_s"].reshape(NG, CH, 1), (NG, CH, 256))
    ln_g, ln_b = w["ln_a_g"].reshape(1, D), w["ln_a_b"].reshape(1, D)
    cb = w["conv_b"].reshape(1, CD)
    pad32 = lambda a: jnp.pad(a.reshape(1, NH), ((0, 0), (0, DTW - NH)))
    dtb, alog = pad32(w["dt_bias"]), pad32(w["a_log"])
    dskx = jnp.repeat(w["d_skip"].reshape(NH), HD).reshape(1, DI)
    sg = w["ssm_norm_g"].reshape(1, DI)
    ple_g, fin_g = w["ple_norm_g"].reshape(1, D), w["final_g"].reshape(1, D)
    e128 = (jnp.arange(DTW)[:, None] == (jnp.arange(DI)[None, :] // HD)).astype(BF16)
    et128 = e128.T
    gsel = ((jnp.arange(D)[:, None] // 256) == jnp.arange(128)[None, :]).astype(BF16)

    w_in_t = lambda a: jnp.transpose(a.reshape(D, WSH))
    (a_all,) = _run_exchange(_relay_gather_exchange([w_in_t(w["w_in"]).astype(BF16)]), "all_gather_w_in")
    w_in_full_t = a_all.reshape(NIN, D)
    wp = _w_in_t_rows(w_in_full_t)
    (proj, dtr, h), (r_all, cw_all) = _proj_fwd(x2, norm_g2, wp, _gather_exchange([_rows_pack(w, BF16), w["conv_w"].reshape(4, CD // NDEV)]))
    f_oa = r_all[:, R_OA:R_OB].reshape(D, D)
    f_ob = r_all[:, R_OB:R_OUT].reshape(DI, D)
    f_out = r_all[:, R_OUT:R_PG].reshape(D, D)
    f_pg = r_all[:, R_PG:R_PLE].reshape(D, D)
    f_ple = jnp.transpose(r_all[:, R_PLE:R_ROWS].reshape(NDEV, PLE, 128), (1, 0, 2)).reshape(PLE, D)
    cw = jnp.transpose(cw_all, (1, 0, 2)).reshape(4, CD)

    y, yb, hprev, pre_all = _ssd_fwd(proj, dtr, cw, cb, dtb, alog, dskx, sg, e128)
    dx1, dx1b, ya, mg, hp, dpre, dpe, doa, dob, dya, dyb, dgab, acc = _merge(
        x2, yb, proj, p2, tgt, (ln_g, ln_b, ws, bst), f_oa, f_ob, f_out, f_pg, f_ple, ple_g, fin_g)

    gple = jnp.transpose(_wgrad(p2, dpe, "wgrad_ple").reshape(PLE, NDEV, 128), (1, 0, 2)).reshape(NDEV, 32, D)
    gr = jnp.concatenate([_wgrad(ya, doa, "wgrad_oa").reshape(NDEV, 128, D), _wgrad(yb, dob, "wgrad_ob").reshape(NDEV, 256, D),
                          _wgrad(mg, dx1b, "wgrad_out").reshape(NDEV, 128, D), _wgrad(hp, dpre, "wgrad_pg").reshape(NDEV, 128, D),
                          gple], axis=1)
    (duvz, dws, dbs, dln), _ = _gmlp_bwd(proj, dya, ln_g, ln_b, ws, wst, bst, gsel, _no_exchange())
    (dxz, ddt, dcw, dsm, dsg), (rr,) = _ssd_bwd(proj, pre_all, dtr, y, dyb, hprev, cw, dtb, alog, dskx, sg, e128, et128, _direct_exchange([gr]))

    g_xz = _wgrad(dxz, h, "wgrad_xbc_zb")
    g_w_in_t = jnp.concatenate([_wgrad(duvz, h, "wgrad_u_v_za"), g_xz[CD:], g_xz[:CD], _wgrad(ddt, h, "wgrad_dt")[:NH],
                                _wgrad(dgab, h, "wgrad_ga_gb")], axis=0)
    ga = g_w_in_t.reshape(NDEV, WSH, D)
    wide = jnp.concatenate([dln, acc, dsg.reshape(16, D), dcw.reshape(24, D)], axis=0)
    narrow = jnp.concatenate([dws.reshape(NG * CH, CH), jnp.pad(dbs[:, :NG].T, ((0, 8 - NG), (0, 0))), dsm], axis=0)
    (gx, dng), (ra, wide_all, narrow_all) = _proj_bwd(x2, dx1, norm_g2, wp, [dxz, duvz, dgab], ddt,
                                                      _combine(_direct_exchange([ga]), _direct_gather_exchange([wide, narrow])))
    ga_own = lax.dynamic_index_in_dim(ga, me, 0, keepdims=False)
    gr_own = lax.dynamic_index_in_dim(gr, me, 0, keepdims=False)

    out_g, out_d, out_m, out_v = {}, {}, {}, {}
    outs = (out_g, out_d, out_m, out_v)
    tiles = lambda a: jnp.transpose(a.reshape(8, 128, WSH), (2, 0, 1))
    g2, (norm_all,) = _sum_parts([ga_own] + [(ra, k) for k in range(NDEV - 1)], "sum_w_in", _direct_gather_exchange([dng]))
    res = _adamw_tiles(tiles(w["w_in"]), tiles(m["w_in"]), tiles(v["w_in"]), g2.reshape(WSH, 8, 128), "adamw_w_in")
    for dst, val in zip(outs, res):
        dst["w_in"] = jnp.transpose(val, (1, 2, 0)).reshape(1, D, WSH)
    parts_r = [gr_own] + [(rr, k) for k in range(NDEV - 1)]
    for name, row, rows in (("w_oa", R_OA, 128), ("w_ob", R_OB, 256), ("w_out", R_OUT, 128), ("w_pg", R_PG, 128)):
        res = _adamw(w[name].reshape(rows, D), m[name].reshape(rows, D), v[name].reshape(rows, D), parts_r, "adamw_" + name,
                     part_row=row, block_rows=128)
        for dst, val in zip(outs, res):
            dst[name] = val.reshape(1, rows, D)
    res = _adamw(w["w_ple"].reshape(32, D), m["w_ple"].reshape(32, D), v["w_ple"].reshape(32, D), parts_r, "adamw_w_ple",
                 part_row=R_PLE, block_rows=32)
    for dst, val in zip(outs, res):
        dst["w_ple"] = val.reshape(1, PLE, 128)
    two_d = lambda d: {n: d[n].reshape(_SMALL_SHAPES[n]) for n in _SMALL_SHAPES}
    *res, g_cw_wide, loss = _adamw_small(two_d(w), two_d(m), two_d(v), wide_all, narrow_all, norm_all)
    for dst, val in zip(outs, res):
        dst.update({n: val[n].reshape(w[n].shape) for n in _SMALL_SHAPES})
    g_cw = lax.dynamic_slice_in_dim(g_cw_wide.reshape(4, CD), me * (CD // NDEV), CD // NDEV, axis=1).reshape(12, 128)
    res = _adamw(w["conv_w"].reshape(12, 128), m["conv_w"].reshape(12, 128), v["conv_w"].reshape(12, 128), [g_cw], "adamw_conv_w")
    for dst, val in zip((out_g, out_d, out_m, out_v), res):
        dst["conv_w"] = val.reshape(1, 4, CD // NDEV)

    return (loss, gx.reshape(1, T, D), *[out_g[n] for n in _WEIGHTS], *[out_d[n] for n in _WEIGHTS],
            *[out_m[n] for n in _WEIGHTS], *[out_v[n] for n in _WEIGHTS])
```

```python
import functools
import math
from typing import Callable, NamedTuple

import jax
import jax.numpy as jnp
from jax import lax
from jax.experimental import pallas as pl
from jax.experimental.pallas import tpu as pltpu

F32 = jnp.float32
BF16 = jnp.bfloat16
MESH = pl.DeviceIdType.MESH

D = 1024
DI = 2048
CD = 3072
NH = 32
HD = 64
NST = 128
NG = 4
CH = 128
PLE = 256
NIN = 10272
NDEV = 8
WSH = NIN // NDEV
EPS = 1e-6
OFF_XBC, OFF_ZB, OFF_U, OFF_V, OFF_ZA, OFF_GA, OFF_GB, OFF_DT = 0, 3072, 5120, 6144, 7168, 8192, 9216, 10240
NP = 10368
DTW = 128
R_OA, R_OB, R_OUT, R_PG, R_PLE, R_ROWS = 0, 128, 384, 512, 640, 672

ADAM_LR, ADAM_B1, ADAM_B2, ADAM_EPS, ADAM_WD, ADAM_STEP = 0.001, 0.9, 0.999, 1e-08, 0.01, 10

V7X_VMEM_LIMIT = 56 * 1024 * 1024


def _cp(sem=None):
    return pltpu.CompilerParams(dimension_semantics=sem, vmem_limit_bytes=V7X_VMEM_LIMIT)


def _dot(a, b, prec=None):
    return jnp.dot(a, b, preferred_element_type=F32, precision=prec)


def _dot_nt(a, b, prec=None):
    return lax.dot_general(a, b, (((1,), (1,)), ((), ())), preferred_element_type=F32, precision=prec)


def _dot_tn(a, b, prec=None):
    return lax.dot_general(a, b, (((0,), (0,)), ((), ())), preferred_element_type=F32, precision=prec)


def _sigmoid(x):
    return 0.5 * jnp.tanh(0.5 * x) + 0.5


_GELU_C = math.sqrt(2.0 / math.pi)
_GELU_K = 0.044715


def _gelu_and_grad(x):
    x2 = x * x
    t = jnp.tanh(x * (_GELU_C + (_GELU_C * _GELU_K) * x2))
    hx = 0.5 * x
    g = hx + hx * t
    dg = (0.5 + 0.5 * t) + hx * (1.0 - t * t) * (_GELU_C + (3.0 * _GELU_C * _GELU_K) * x2)
    return g, dg


def _gelu(x):
    hx = 0.5 * x
    return hx + hx * jnp.tanh(x * (_GELU_C + (_GELU_C * _GELU_K) * (x * x)))


def _softplus(x):
    return jnp.maximum(x, 0.0) + jnp.log(1.0 + jnp.exp(-jnp.abs(x)))


def _full(shape):
    n = len(shape)
    return pl.BlockSpec(shape, lambda *_: (0,) * n)


_ANY = pl.BlockSpec(memory_space=pl.ANY)


class _Exchange(NamedTuple):
    arrays: list
    out_shape: list
    sems: list
    make: Callable


def _call(body, ex, *, name, grid, in_specs, out_specs, out_shape, scratch_shapes, args):
    ki, ko, ks = len(in_specs), len(out_specs), len(scratch_shapes)
    ei, eo = len(ex.arrays), len(ex.out_shape)
    last = [g - 1 for g in grid]

    def full_body(*refs):
        r = list(refs)
        ins, eins, r = r[:ki], r[ki:ki + ei], r[ki + ei:]
        outs, eouts, r = r[:ko], r[ko:ko + eo], r[ko + eo:]
        scr, esems = r[:ks], r[ks:]
        start, finish = ex.make(eins, eouts, esems)
        ids = [pl.program_id(a) for a in range(len(grid))]
        is_first = functools.reduce(lambda p, q: p & q, [i == 0 for i in ids])
        is_last = functools.reduce(lambda p, q: p & q, [i == l for i, l in zip(ids, last)])
        pl.when(is_first)(start)
        body(*ins, *outs, *scr)
        pl.when(is_last)(finish)

    res = pl.pallas_call(
        full_body, name=name, grid=grid, in_specs=list(in_specs) + [_ANY] * ei, out_specs=list(out_specs) + [_ANY] * eo,
        out_shape=list(out_shape) + list(ex.out_shape), scratch_shapes=list(scratch_shapes) + list(ex.sems),
        compiler_params=_cp(("arbitrary",) * len(grid)),
    )(*args, *ex.arrays)
    return res[:ko], res[ko:]


def _run_exchange(ex, name):
    ni, no = len(ex.arrays), len(ex.out_shape)

    def body(*refs):
        start, finish = ex.make(refs[:ni], refs[ni:ni + no], refs[ni + no:])
        start()
        finish()

    return pl.pallas_call(body, name=name, in_specs=[_ANY] * ni, out_specs=[_ANY] * no, out_shape=list(ex.out_shape),
                          scratch_shapes=list(ex.sems))(*ex.arrays)


def _proj_fwd(x, norm_g, wp, ex):
    T = x.shape[0]
    tm, tn = min(T, 1024), 2048
    nj = OFF_DT // tn
    assert OFF_DT % tn == 0 and OFF_DT + DTW == NP

    def body(x_ref, g_ref, w_ref, wdt_ref, proj_ref, dt_ref, h_ref, hs_ref):
        j = pl.program_id(1)

        @pl.when(j == 0)
        def _():
            xf = x_ref[...]
            r = lax.rsqrt(jnp.mean(xf * xf, axis=-1, keepdims=True) + EPS)
            h = (xf * r * g_ref[...]).astype(BF16)
            hs_ref[...] = h
            h_ref[...] = h

        proj_ref[...] = _dot_nt(hs_ref[...], w_ref[...]).astype(BF16)

        @pl.when(j == nj - 1)
        def _():
            dt_ref[...] = _dot_nt(hs_ref[...], wdt_ref[...])

    return _call(
        body, ex, name="proj_fwd", grid=(T // tm, nj),
        in_specs=[pl.BlockSpec((tm, D), lambda i, j: (i, 0)), _full((1, D)), pl.BlockSpec((tn, D), lambda i, j: (j, 0)),
                  pl.BlockSpec((DTW, D), lambda i, j: (OFF_DT // DTW, 0))],
        out_specs=[pl.BlockSpec((tm, tn), lambda i, j: (i, j)), pl.BlockSpec((tm, DTW), lambda i, j: (i, 0)),
                   pl.BlockSpec((tm, D), lambda i, j: (i, 0))],
        out_shape=[jax.ShapeDtypeStruct((T, OFF_DT), BF16), jax.ShapeDtypeStruct((T, DTW), F32), jax.ShapeDtypeStruct((T, D), BF16)],
        scratch_shapes=[pltpu.VMEM((tm, D), BF16)], args=(x, norm_g, wp, wp))


def _gmlp_tile():
    return 512


def _gmlp_fwd_tile(u_ref, v_ref, z_ref, lg_ref, lb_ref, ws_ref, bs_ref, ya_ref, vn_s):
    tm = u_ref.shape[0]
    vg = _gelu(v_ref[...].astype(F32))
    mu = jnp.mean(vg, axis=-1, keepdims=True)
    xc = vg - mu
    rstd = lax.rsqrt(jnp.mean(xc * xc, axis=-1, keepdims=True) + EPS)
    vn_s[...] = (xc * rstd * lg_ref[...] + lb_ref[...]).astype(BF16)
    for c in range(tm // CH):
        rs = slice(c * CH, (c + 1) * CH)
        for g in range(NG):
            cs_ = slice(g * 256, (g + 1) * 256)
            sv = _dot(ws_ref[g], vn_s[rs, cs_]) + bs_ref[g]
            z = z_ref[rs, cs_].astype(F32)
            ya_ref[rs, cs_] = (_gelu(u_ref[rs, cs_].astype(F32)) * sv * (z * _sigmoid(z))).astype(BF16)


def _gmlp_bwd(proj, dya, ln_g, ln_b, ws, wst, bst, gsel, ex):
    T = proj.shape[0]
    tm = min(T, _gmlp_tile())

    def body(u_ref, v_ref, z_ref, dy_ref, lg_ref, lb_ref, ws_ref, wst_ref, bs_ref, gsel_ref,
             d_ref, dws_ref, dbs_ref, dln_ref, vn_s, dsv_s, dvn_s):
        du_ref, dv_ref, dz_ref = d_ref.at[:, 0:D], d_ref.at[:, D:2 * D], d_ref.at[:, 2 * D:3 * D]
        @pl.when(pl.program_id(0) == 0)
        def _():
            dws_ref[...] = jnp.zeros_like(dws_ref)
            dbs_ref[...] = jnp.zeros_like(dbs_ref)
            dln_ref[...] = jnp.zeros_like(dln_ref)

        vg, dvg_dv = _gelu_and_grad(v_ref[...].astype(F32))
        mu = jnp.mean(vg, axis=-1, keepdims=True)
        xc = vg - mu
        rstd = lax.rsqrt(jnp.mean(xc * xc, axis=-1, keepdims=True) + EPS)
        vhat = xc * rstd
        vn_s[...] = (vhat * lg_ref[...] + lb_ref[...]).astype(BF16)
        ri = lax.broadcasted_iota(jnp.int32, (CH, CH), 0)
        ci = lax.broadcasted_iota(jnp.int32, (CH, CH), 1)
        tril = (ri >= ci).astype(F32)
        for c in range(tm // CH):
            rs = slice(c * CH, (c + 1) * CH)
            for g in range(NG):
                cs_ = slice(g * 256, (g + 1) * 256)
                vn = vn_s[rs, cs_]
                sv = _dot(ws_ref[g], vn) + bs_ref[g]
                z = z_ref[rs, cs_].astype(F32)
                sz = _sigmoid(z)
                ug, dug_du = _gelu_and_grad(u_ref[rs, cs_].astype(F32))
                dy = dy_ref[rs, cs_].astype(F32)
                zs = z * sz
                t = dy * zs
                dsv_f = dy * sv
                du_ref[rs, cs_] = (zs * dsv_f * dug_du).astype(BF16)
                dz_ref[rs, cs_] = (dsv_f * ug * (sz + zs * (1.0 - sz))).astype(BF16)
                dsv = (t * ug).astype(BF16)
                dsv_s[rs, cs_] = dsv
                dvn_s[rs, cs_] = _dot(wst_ref[g], dsv)
                dws_ref[g] += _dot_nt(dsv, vn) * tril
            dbs_ref[...] += _dot(dsv_s[rs, :], gsel_ref[...])
        dvn = dvn_s[...]
        dln_ref[0:1, :] += jnp.sum(dvn * vhat, axis=0, keepdims=True)
        dln_ref[1:2, :] += jnp.sum(dvn, axis=0, keepdims=True)
        dvh = dvn * lg_ref[...]
        dvg = rstd * (dvh - jnp.mean(dvh, axis=-1, keepdims=True) - vhat * jnp.mean(dvh * vhat, axis=-1, keepdims=True))
        dv_ref[...] = (dvg * dvg_dv).astype(BF16)

    blk = lambda off: pl.BlockSpec((tm, D), lambda i: (i, off // D))
    row = pl.BlockSpec((tm, D), lambda i: (i, 0))
    return _call(
        body, ex, name="gmlp_bwd", grid=(T // tm,),
        in_specs=[blk(OFF_U), blk(OFF_V), blk(OFF_ZA), row, _full((1, D)), _full((1, D)), _full((NG, CH, CH)),
                  _full((NG, CH, CH)), _full((NG, CH, 256)), _full((D, 128))],
        out_specs=[pl.BlockSpec((tm, 3 * D), lambda i: (i, 0)), _full((NG, CH, CH)), _full((CH, 128)), _full((8, D))],
        out_shape=[jax.ShapeDtypeStruct((T, 3 * D), BF16),
                   jax.ShapeDtypeStruct((NG, CH, CH), F32), jax.ShapeDtypeStruct((CH, 128), F32), jax.ShapeDtypeStruct((8, D), F32)],
        scratch_shapes=[pltpu.VMEM((tm, D), BF16), pltpu.VMEM((tm, D), BF16), pltpu.VMEM((tm, D), F32)],
        args=(proj, proj, proj, dya, ln_g, ln_b, ws, wst, bst, gsel))


def _zb_cols(zb0_ref, zb1_ref, cols):
    ref = zb0_ref if cols.start < D else zb1_ref
    return ref[:, cols.start % D:cols.start % D + (cols.stop - cols.start)]


def _shift_matrix(down):
    t = jnp.arange(CH)[None, :, None]
    j = jnp.arange(1, 4)[:, None, None]
    col = jnp.arange(2 * CH)[None, None, :]
    src = CH + t - j if down else t + j
    return (col == src).astype(BF16).reshape(3 * CH, 2 * CH)


def _conv_pre(x, moved, cw_ref, cb_ref):
    pre = cb_ref[...] + cw_ref[3:4, :] * x
    for j in (1, 2, 3):
        pre = pre + cw_ref[3 - j:4 - j, :] * moved[(j - 1) * CH:j * CH]
    return pre


def _split_dot(x, w, parts, w_left=False):
    acc, r = None, x
    for k in range(parts):
        hi = r.astype(BF16)
        d = _dot(w, hi) if w_left else _dot(hi, w)
        acc = d if acc is None else acc + d
        if k + 1 < parts:
            r = r - hi.astype(F32)
    return acc


def _chunk_decays(dt, alog_ref, e_ref, cs_s, cst_s, csx_s):
    a = -jnp.exp(alog_ref[...])
    ri = lax.broadcasted_iota(jnp.int32, (CH, CH), 0)
    ci = lax.broadcasted_iota(jnp.int32, (CH, CH), 1)
    tril = ri >= ci
    cs = _split_dot(dt * a, tril.astype(BF16), 3, w_left=True)
    cs_s[...] = cs
    cst_s[...] = cs.T
    csx_s[...] = _split_dot(cs, e_ref[...], 3)
    return a, tril, ri, ci


def _lmat(cst_s, h, tril):
    rowb = jnp.broadcast_to(cst_s[h:h + 1, :], (CH, CH))
    return jnp.exp(jnp.where(tril, rowb.T - rowb, -jnp.inf))


def _head_pair_rows(v, lane):
    return jnp.concatenate([jnp.where(lane < HD, v, 0.0), jnp.where(lane < HD, 0.0, v)], axis=0).astype(BF16)


def _ssd_fwd(proj, dtr, cw, cb, dtb, alog, dskx, sg, e128):
    T = proj.shape[0]
    nc = T // CH

    def body(xbc_ref, zb0_ref, zb1_ref, dt_ref, cw_ref, cb_ref, dtb_ref, alog_ref, dx_ref, sg_ref, e_ref, shift_ref,
             y_ref, yb_ref, hp_ref, pre_ref, xx_s, h_s, cs_s, cst_s, csx_s, yz_s):
        @pl.when(pl.program_id(0) == 0)
        def _():
            xx_s[...] = jnp.zeros_like(xx_s)
            h_s[...] = jnp.zeros_like(h_s)

        xx_s[CH:, :] = xbc_ref[...]
        moved = _dot(shift_ref[...], xx_s[...])
        xx_s[CH - 16:CH, :] = xbc_ref[CH - 16:, :]
        x = xbc_ref[...].astype(F32)
        pre = _conv_pre(x, moved, cw_ref, cb_ref)
        pre_ref[...] = pre
        xc = pre * _sigmoid(pre)
        dt = _softplus(dt_ref[...] + dtb_ref[...])
        a, tril, _, lane = _chunk_decays(dt, alog_ref, e_ref, cs_s, cst_s, csx_s)
        dt_x = _split_dot(dt, e_ref[...], 2)
        cs_last_x = csx_s[CH - 1:CH, :]
        hp_ref[0] = h_s[...]
        for g in range(NG):
            gs = slice(g * 512, (g + 1) * 512)
            bg = xc[:, DI + g * NST:DI + (g + 1) * NST].astype(BF16)
            cg = xc[:, DI + 512 + g * NST:DI + 512 + (g + 1) * NST].astype(BF16)
            cbm = _dot_nt(cg, bg)
            xg = xc[:, gs]
            xdt = xg * dt_x[:, gs]
            hprev = h_s[:, gs]
            csx = csx_s[:, gs]
            yoff = _dot(cg, hprev.astype(BF16)) * jnp.exp(csx)
            st = _dot_tn(bg, (xdt * jnp.exp(cs_last_x[:, gs] - csx)).astype(BF16))
            h_s[:, gs] = jnp.exp(cs_last_x[:, gs]) * hprev + st
            ssq = jnp.zeros((CH, 1), F32)
            for q in range(4):
                h0 = g * 8 + 2 * q
                ps = slice(q * 128, (q + 1) * 128)
                cols = slice(g * 512 + q * 128, g * 512 + (q + 1) * 128)
                m01 = jnp.concatenate([cbm * _lmat(cst_s, h0, tril), cbm * _lmat(cst_s, h0 + 1, tril)], axis=1).astype(BF16)
                yq = _dot(m01, _head_pair_rows(xdt[:, ps], lane)) + yoff[:, ps] + xg[:, ps] * dx_ref[:, cols]
                y_ref[:, cols] = yq
                z = _zb_cols(zb0_ref, zb1_ref, cols).astype(F32)
                yz = yq * z * _sigmoid(z)
                yz_s[:, cols] = yz
                ssq = ssq + jnp.sum(yz * yz, axis=1, keepdims=True)
            rg = lax.rsqrt(ssq * (1.0 / 512.0) + EPS)
            yb_ref[:, gs] = (yz_s[:, gs] * rg * sg_ref[:, gs]).astype(BF16)

    return pl.pallas_call(
        body, name="ssd_fwd", grid=(nc,),
        in_specs=[pl.BlockSpec((CH, CD), lambda c: (c, OFF_XBC // CD)), pl.BlockSpec((CH, D), lambda c: (c, OFF_ZB // D)),
                  pl.BlockSpec((CH, D), lambda c: (c, OFF_ZB // D + 1)),
                  pl.BlockSpec((CH, DTW), lambda c: (c, 0)), _full((4, CD)), _full((1, CD)), _full((1, DTW)),
                  _full((1, DTW)), _full((1, DI)), _full((1, DI)), _full((DTW, DI)), _full((3 * CH, 2 * CH))],
        out_specs=[pl.BlockSpec((CH, DI), lambda c: (c, 0)), pl.BlockSpec((CH, DI), lambda c: (c, 0)),
                   pl.BlockSpec((1, NST, DI), lambda c: (c, 0, 0)), pl.BlockSpec((CH, CD), lambda c: (c, 0))],
        out_shape=[jax.ShapeDtypeStruct((T, DI), F32), jax.ShapeDtypeStruct((T, DI), BF16),
                   jax.ShapeDtypeStruct((nc, NST, DI), F32), jax.ShapeDtypeStruct((T, CD), F32)],
        scratch_shapes=[pltpu.VMEM((2 * CH, CD), BF16), pltpu.VMEM((NST, DI), F32), pltpu.VMEM((CH, CH), F32),
                        pltpu.VMEM((CH, CH), F32), pltpu.VMEM((CH, DI), F32), pltpu.VMEM((CH, DI), F32)],
        compiler_params=_cp(("arbitrary",)),
    )(proj, proj, proj, dtr, cw, cb, dtb, alog, dskx, sg, e128, _shift_matrix(down=True))


def _ssd_bwd(proj, pre_all, dtr, y, dyb, hprev_all, cw, dtb, alog, dskx, sg, e128, et128, ex):
    T = proj.shape[0]
    nc = T // CH

    def body(xbc_ref, pre_ref, zb0_ref, zb1_ref, dt_ref, y_ref, dyb_ref, hp_ref, cw_ref, dtb_ref, alog_ref, dx_ref, sg_ref,
             e_ref, et_ref, shift_ref, d_ref, ddt_ref, dcw_ref, dsm_ref, dsg_ref,
             g_s, dd_s, cs_s, cst_s, csx_s, dy_s, dxdt_s, dxs_s, dsd_s, dxc_s, gh_s):
        dxbc_ref, dzb_ref = d_ref.at[:, 0:CD], d_ref.at[:, CD:CD + DI]
        i = pl.program_id(0)

        @pl.when(i == 0)
        def _():
            g_s[...] = jnp.zeros_like(g_s)
            dd_s[...] = jnp.zeros_like(dd_s)
            dcw_ref[...] = jnp.zeros_like(dcw_ref)
            dsm_ref[...] = jnp.zeros_like(dsm_ref)
            dsg_ref[...] = jnp.zeros_like(dsg_ref)

        pre = pre_ref[...]
        sp = _sigmoid(pre)
        xc = pre * sp
        dtr = dt_ref[...] + dtb_ref[...]
        dt = _softplus(dtr)
        a, tril, ri, lane = _chunk_decays(dt, alog_ref, e_ref, cs_s, cst_s, csx_s)
        et = et_ref[...]
        dt_x = _split_dot(dt, e_ref[...], 2)
        cs_last_x = csx_s[CH - 1:CH, :]

        for g in range(NG):
            gs = slice(g * 512, (g + 1) * 512)
            z = _zb_cols(zb0_ref, zb1_ref, gs).astype(F32)
            sz = _sigmoid(z)
            yv = y_ref[:, gs]
            zs = z * sz
            yz = yv * zs
            rg = lax.rsqrt(jnp.mean(yz * yz, axis=-1, keepdims=True) + EPS)
            yn = yz * rg
            dyb = dyb_ref[:, gs].astype(F32)
            dsg_ref[0:1, gs] += jnp.sum(dyb * yn, axis=0, keepdims=True)
            dyn = dyb * sg_ref[:, gs]
            dyz = rg * (dyn - yn * jnp.mean(dyn * yn, axis=-1, keepdims=True))
            dy_s[:, gs] = dyz * zs
            dzb_ref[:, gs] = (dyz * yv * (sz + zs * (1.0 - sz))).astype(BF16)

        rsum = jnp.zeros((CH, DTW), F32)
        csum_t = jnp.zeros((DTW, CH), F32)
        for g in range(NG):
            gs = slice(g * 512, (g + 1) * 512)
            bg = xc[:, DI + g * NST:DI + (g + 1) * NST].astype(BF16)
            cg = xc[:, DI + 512 + g * NST:DI + 512 + (g + 1) * NST].astype(BF16)
            cbm = _dot_nt(cg, bg)
            xdt = xc[:, gs] * dt_x[:, gs]
            hprev = hp_ref[0, :, gs]
            hpb = hprev.astype(BF16)
            gn = g_s[:, gs]
            gnb = gn.astype(BF16)
            dy = dy_s[:, gs]
            csx = csx_s[:, gs]
            ecs = jnp.exp(csx)
            dec = jnp.exp(cs_last_x[:, gs] - csx)
            dye_f = dy * ecs
            dye = dye_f.astype(BF16)
            dc = _dot_nt(dye, hpb)
            dprev = _dot_tn(cg, dye)
            dxdt_state = dec * _dot(bg, gnb)
            db = _dot_nt((xdt * dec).astype(BF16), gnb)
            dcb = jnp.zeros((CH, CH), F32)
            for q in range(4):
                h0 = g * 8 + 2 * q
                ps = slice(q * 128, (q + 1) * 128)
                dyp = dy[:, ps]
                l0 = _lmat(cst_s, h0, tril)
                l1 = _lmat(cst_s, h0 + 1, tril)
                m0 = cbm * l0
                m1 = cbm * l1
                dm = _dot_nt(dyp.astype(BF16), _head_pair_rows(xdt[:, ps], lane))
                dm0 = dm[:, :CH]
                dm1 = dm[:, CH:]
                dcb = dcb + dm0 * l0 + dm1 * l1
                for hh, qm in ((h0, dm0 * m0), (h0 + 1, dm1 * m1)):
                    rsum = jnp.where(lane == hh, jnp.sum(qm, axis=1, keepdims=True), rsum)
                    csum_t = jnp.where(ri == hh, jnp.sum(qm, axis=0, keepdims=True), csum_t)
                mst = jnp.concatenate([m0, m1], axis=0).astype(BF16)
                d = _dot_tn(mst, _head_pair_rows(dyp, lane))
                dxdt_s[:, g * 512 + q * 128:g * 512 + (q + 1) * 128] = d + dxdt_state[:, ps]
            dsd_s[:, gs] = xdt * dxdt_state
            dxs_s[:, gs] = dye_f * _dot(cg, hpb)
            dcbb = dcb.astype(BF16)
            dxc_s[:, DI + 512 + g * NST:DI + 512 + (g + 1) * NST] = dc + _dot(dcbb, bg)
            dxc_s[:, DI + g * NST:DI + (g + 1) * NST] = db + _dot_tn(dcbb, cg)
            gh_s[:, gs] = jnp.broadcast_to(jnp.sum(gn * hprev, axis=0, keepdims=True), (8, 512))
            g_s[:, gs] = dprev + jnp.exp(cs_last_x[:, gs]) * gn

        xs = xc[:, :DI]
        dy = dy_s[...]
        dxdt = dxdt_s[...]
        cs_last = cs_s[CH - 1:CH, :]
        state_e = _split_dot(dsd_s[...], et, 1)
        dcd = 0.125 * jnp.sum(_split_dot(gh_s[...], et, 2), axis=0, keepdims=True) * jnp.exp(cs_last)
        row = lax.broadcasted_iota(jnp.int32, (CH, 1), 0)
        dcs = rsum - csum_t.T + _split_dot(dxs_s[...], et, 1) - state_e
        dcs = dcs + jnp.where(row == CH - 1, jnp.sum(state_e, axis=0, keepdims=True) + dcd, 0.0)
        dda = _split_dot(dcs, (lane >= ri).astype(BF16), 3, w_left=True)
        ddt = dda * a + _dot((dxdt * xs).astype(BF16), et)
        ddtr = jnp.where(lane < NH, ddt * _sigmoid(dtr), 0.0)
        ddt_ref[...] = ddtr.astype(BF16)
        dsm_ref[0:1, :] += jnp.sum(ddtr, axis=0, keepdims=True)
        dsm_ref[1:2, :] += jnp.sum(dda * dt, axis=0, keepdims=True) * a
        dsm_ref[2:3, :] += jnp.sum(_dot((dy * xs).astype(BF16), et), axis=0, keepdims=True)
        dxc_s[:, :DI] = dxdt * dt_x + dy * dx_ref[...]

        dpre = dxc_s[...] * (sp + xc * (1.0 - sp))
        dpre_b = dpre.astype(BF16)
        dd_s[:CH, :] = dpre_b
        moved = _dot(shift_ref[...], dd_s[...])
        dd_s[CH:CH + 16, :] = dpre_b[:16]
        x = xbc_ref[...].astype(F32)
        dcw_ref[4:5, :] += jnp.sum(dpre, axis=0, keepdims=True)
        dxbc = cw_ref[3:4, :] * dpre
        dcw_ref[3:4, :] += jnp.sum(dpre * x, axis=0, keepdims=True)
        for j in (1, 2, 3):
            ahead = moved[(j - 1) * CH:j * CH]
            dcw_ref[3 - j:4 - j, :] += jnp.sum(ahead * x, axis=0, keepdims=True)
            dxbc = dxbc + cw_ref[3 - j:4 - j, :] * ahead
        dxbc_ref[...] = dxbc.astype(BF16)

    rev = lambda c: nc - 1 - c
    return _call(
        body, ex, name="ssd_bwd", grid=(nc,),
        in_specs=[pl.BlockSpec((CH, CD), lambda c: (rev(c), OFF_XBC // CD)),
                  pl.BlockSpec((CH, CD), lambda c: (rev(c), 0)),
                  pl.BlockSpec((CH, D), lambda c: (rev(c), OFF_ZB // D)), pl.BlockSpec((CH, D), lambda c: (rev(c), OFF_ZB // D + 1)),
                  pl.BlockSpec((CH, DTW), lambda c: (rev(c), 0)),
                  pl.BlockSpec((CH, DI), lambda c: (rev(c), 0)), pl.BlockSpec((CH, DI), lambda c: (rev(c), 0)),
                  pl.BlockSpec((1, NST, DI), lambda c: (rev(c), 0, 0)),
                  _full((4, CD)), _full((1, DTW)), _full((1, DTW)), _full((1, DI)), _full((1, DI)),
                  _full((DTW, DI)), _full((DI, DTW)), _full((3 * CH, 2 * CH))],
        out_specs=[pl.BlockSpec((CH, CD + DI), lambda c: (rev(c), 0)),
                   pl.BlockSpec((CH, DTW), lambda c: (rev(c), 0)), _full((8, CD)), _full((8, DTW)), _full((8, DI))],
        out_shape=[jax.ShapeDtypeStruct((T, CD + DI), BF16), jax.ShapeDtypeStruct((T, DTW), BF16),
                   jax.ShapeDtypeStruct((8, CD), F32), jax.ShapeDtypeStruct((8, DTW), F32), jax.ShapeDtypeStruct((8, DI), F32)],
        scratch_shapes=[pltpu.VMEM((NST, DI), F32), pltpu.VMEM((2 * CH, CD), BF16), pltpu.VMEM((CH, CH), F32), pltpu.VMEM((CH, CH), F32),
                        pltpu.VMEM((CH, DI), F32), pltpu.VMEM((CH, DI), F32), pltpu.VMEM((CH, DI), F32), pltpu.VMEM((CH, DI), F32),
                        pltpu.VMEM((CH, DI), F32), pltpu.VMEM((CH, CD), F32), pltpu.VMEM((8, DI), F32)],
        args=(proj, pre_all, proj, proj, dtr, y, dyb, hprev_all, cw, dtb, alog, dskx, sg, e128, et128, _shift_matrix(down=False)))


def _merge_tile():
    return 256


def _merge(x, yb, proj, p, tgt, gmlp, w_oa, w_ob, w_out, w_pg, w_ple, ple_g, fin_g):
    T = x.shape[0]
    tm = min(T, _merge_tile())

    def body(x_ref, u_ref, v_ref, za_ref, yb_ref, ga_ref, gb_ref, p_ref, t_ref, lg_ref, lb_ref, ws_ref, bs_ref,
             woa, wob, wout, wpg, wple, pg_ref, fg_ref,
             dx1_ref, dx1b_ref, ya_ref, mg_ref, hp_ref, dpre_ref, dpe_ref, doa_ref, dob_ref, dya_ref, dyb_ref, dg_ref, acc_ref, vn_s):
        @pl.when(pl.program_id(0) == 0)
        def _():
            acc_ref[...] = jnp.zeros_like(acc_ref)

        _gmlp_fwd_tile(u_ref, v_ref, za_ref, lg_ref, lb_ref, ws_ref, bs_ref, ya_ref, vn_s)
        oa = _dot(ya_ref[...], woa[...])
        ob = _dot(yb_ref[...], wob[...])
        sa = _sigmoid(ga_ref[...].astype(F32))
        sb = _sigmoid(gb_ref[...].astype(F32))
        mg = sa * oa + sb * ob
        mgb = mg.astype(BF16)
        mg_ref[...] = mgb
        x1 = x_ref[...] + _dot(mgb, wout[...])
        r2 = lax.rsqrt(jnp.mean(x1 * x1, axis=-1, keepdims=True) + EPS)
        xh1 = x1 * r2
        hpb = (xh1 * pg_ref[...]).astype(BF16)
        hp_ref[...] = hpb
        gate = _sigmoid(_dot(hpb, wpg[...]))
        pe = _dot(p_ref[...].astype(BF16), wple[...])
        x2 = x1 + gate * pe
        r3 = lax.rsqrt(jnp.mean(x2 * x2, axis=-1, keepdims=True) + EPS)
        xh2 = x2 * r3
        err = xh2 * fg_ref[...] - t_ref[...]
        acc_ref[2:3, :] += 0.5 * jnp.sum(jnp.mean(err * err, axis=-1, keepdims=True))
        dyo = err * (1.0 / D)
        acc_ref[0:1, :] += jnp.sum(dyo * xh2, axis=0, keepdims=True)
        dn = dyo * fg_ref[...]
        dx2 = r3 * (dn - xh2 * jnp.mean(dn * xh2, axis=-1, keepdims=True))
        dpe_ref[...] = (dx2 * gate).astype(BF16)
        dpre = (dx2 * pe * gate * (1.0 - gate)).astype(BF16)
        dpre_ref[...] = dpre
        dhp = _dot_nt(dpre, wpg[...])
        acc_ref[1:2, :] += jnp.sum(dhp * xh1, axis=0, keepdims=True)
        dhn = dhp * pg_ref[...]
        dx1 = dx2 + r2 * (dhn - xh1 * jnp.mean(dhn * xh1, axis=-1, keepdims=True))
        dx1_ref[...] = dx1
        dx1b = dx1.astype(BF16)
        dx1b_ref[...] = dx1b
        dmg = _dot_nt(dx1b, wout[...])
        doa = (dmg * sa).astype(BF16)
        dob = (dmg * sb).astype(BF16)
        doa_ref[...] = doa
        dob_ref[...] = dob
        dg_ref[:, :D] = (dmg * oa * sa * (1.0 - sa)).astype(BF16)
        dg_ref[:, D:] = (dmg * ob * sb * (1.0 - sb)).astype(BF16)
        dya_ref[...] = _dot_nt(doa, woa[...]).astype(BF16)
        dyb_ref[...] = _dot_nt(dob, wob[...]).astype(BF16)

    row = lambda w: pl.BlockSpec((tm, w), lambda i: (i, 0))
    blk = lambda off: pl.BlockSpec((tm, D), lambda i: (i, off // D))
    wsp = lambda s: pl.BlockSpec(s, lambda i: (0, 0), pipeline_mode=pl.Buffered(1))
    return pl.pallas_call(
        body, name="merge", grid=(T // tm,),
        in_specs=[row(D), blk(OFF_U), blk(OFF_V), blk(OFF_ZA), row(DI), blk(OFF_GA), blk(OFF_GB), row(PLE), row(D),
                  _full((1, D)), _full((1, D)), _full((NG, CH, CH)), _full((NG, CH, 256)),
                  wsp((D, D)), wsp((DI, D)), wsp((D, D)), wsp((D, D)), wsp((PLE, D)), _full((1, D)), _full((1, D))],
        out_specs=[row(D)] * 10 + [row(DI), row(2 * D), _full((8, D))],
        out_shape=[jax.ShapeDtypeStruct((T, D), F32)] + [jax.ShapeDtypeStruct((T, D), BF16)] * 9
        + [jax.ShapeDtypeStruct((T, DI), BF16), jax.ShapeDtypeStruct((T, 2 * D), BF16), jax.ShapeDtypeStruct((8, D), F32)],
        scratch_shapes=[pltpu.VMEM((tm, D), BF16)],
        compiler_params=_cp(("arbitrary",)),
    )(x, proj, proj, proj, yb, proj, proj, p, tgt, *gmlp, w_oa, w_ob, w_out, w_pg, w_ple, ple_g, fin_g)


def _wgrad(a, b, name):
    T, K = a.shape
    N = b.shape[1]
    tt, tk, tn = min(T, 2048), min(K, 1024), min(N, 1024)
    nt = T // tt

    def body(a_ref, b_ref, o_ref, acc_s):
        t = pl.program_id(2)

        @pl.when(t == 0)
        def _():
            acc_s[...] = jnp.zeros_like(acc_s)

        acc_s[...] += _dot_tn(a_ref[...].astype(BF16), b_ref[...])

        @pl.when(t == nt - 1)
        def _():
            o_ref[...] = acc_s[...].astype(BF16)

    return pl.pallas_call(
        body, name=name, grid=(K // tk, N // tn, nt),
        in_specs=[pl.BlockSpec((tt, tk), lambda k, n, t: (t, k)), pl.BlockSpec((tt, tn), lambda k, n, t: (t, n))],
        out_specs=pl.BlockSpec((tk, tn), lambda k, n, t: (k, n)),
        out_shape=jax.ShapeDtypeStruct((K, N), BF16),
        scratch_shapes=[pltpu.VMEM((tk, tn), F32)],
        compiler_params=_cp(("parallel", "parallel", "arbitrary")),
    )(a, b)


def _proj_bwd(x, dx1, norm_g, wt, pieces, ddt, ex):
    T = x.shape[0]
    tm = min(T, 256)
    starts = [sum(a.shape[1] for a in pieces[:n]) for n in range(len(pieces))]
    assert starts[-1] + pieces[-1].shape[1] == OFF_DT
    npc = len(pieces)

    def body(x_ref, dx1_ref, g_ref, w_ref, *rest):
        piece_refs, ddt_ref, gx_ref, dng_ref = rest[:npc], rest[npc], rest[npc + 1], rest[npc + 2]

        @pl.when(pl.program_id(0) == 0)
        def _():
            dng_ref[...] = jnp.zeros_like(dng_ref)

        dh = _dot(ddt_ref[...], w_ref[OFF_DT:NP, :])
        for ref, lo in zip(piece_refs, starts):
            dh = dh + _dot(ref[...], w_ref[lo:lo + ref.shape[1], :])
        xf = x_ref[...]
        r = lax.rsqrt(jnp.mean(xf * xf, axis=-1, keepdims=True) + EPS)
        xh = xf * r
        dng_ref[0:1, :] += jnp.sum(dh * xh, axis=0, keepdims=True)
        dxn = dh * g_ref[...]
        gx_ref[...] = dx1_ref[...] + r * (dxn - xh * jnp.mean(dxn * xh, axis=-1, keepdims=True))

    row = lambda w: pl.BlockSpec((tm, w), lambda i: (i, 0))
    return _call(
        body, ex, name="proj_bwd", grid=(T // tm,),
        in_specs=[row(D), row(D), _full((1, D)), pl.BlockSpec((NP, D), lambda i: (0, 0), pipeline_mode=pl.Buffered(1))]
        + [row(a.shape[1]) for a in pieces] + [row(DTW)],
        out_specs=[row(D), _full((8, D))],
        out_shape=[jax.ShapeDtypeStruct((T, D), F32), jax.ShapeDtypeStruct((8, D), F32)],
        scratch_shapes=[], args=(x, dx1, norm_g, wt, *pieces, ddt))


def _elementwise_tile(R, C, limit=1 << 20):
    if R * C * 4 <= limit:
        return R, C
    rows = [t for t in range(16, R, 16) if R % t == 0 and t * C * 4 <= limit]
    if rows:
        return rows[-1], C
    cols = [t for t in range(128, C, 128) if C % t == 0 and R * t * 4 <= limit]
    return R, cols[-1]


def _adam_update(w, m, v, g):
    c1 = 1.0 - ADAM_B1 ** ADAM_STEP
    c2 = 1.0 - ADAM_B2 ** ADAM_STEP
    mm = ADAM_B1 * m + (1.0 - ADAM_B1) * g
    vv = ADAM_B2 * v + (1.0 - ADAM_B2) * (g * g)
    return -ADAM_LR * ((mm / c1) / (jnp.sqrt(vv / c2) + ADAM_EPS) + ADAM_WD * w), mm, vv


def _sum_parts(parts, name, ex):
    first = parts[0][0] if isinstance(parts[0], tuple) else parts[0]
    R, C = first.shape[-2:]
    tr, tc = _elementwise_tile(R, C)
    flat = pl.BlockSpec((tr, tc), lambda i, j: (i, j))
    slot = lambda k: pl.BlockSpec((None, tr, tc), lambda i, j: (k, i, j))
    specs = [slot(p[1]) if isinstance(p, tuple) else flat for p in parts]
    arrays = [p[0] if isinstance(p, tuple) else p for p in parts]

    def body(*refs):
        g = refs[0][...].astype(F32)
        for r in refs[1:-1]:
            g = g + r[...].astype(F32)
        refs[-1][...] = g

    (out,), got = _call(body, ex, name=name, grid=(R // tr, C // tc), in_specs=specs, out_specs=[flat],
                        out_shape=[jax.ShapeDtypeStruct((R, C), F32)], scratch_shapes=[], args=arrays)
    return out, got


def _adamw_tiles(w, m, v, g, name):
    R = w.shape[0]
    tr = 107
    assert R % tr == 0
    spec = pl.BlockSpec((tr, 8, 128), lambda i: (i, 0, 0))

    def body(w_ref, m_ref, v_ref, g_in, g_ref, d_ref, nm_ref, nv_ref):
        g = g_in[...]
        g_ref[...] = g
        d_ref[...], nm_ref[...], nv_ref[...] = _adam_update(w_ref[...], m_ref[...], v_ref[...], g)

    return pl.pallas_call(body, name=name, grid=(R // tr,), in_specs=[spec] * 4, out_specs=[spec] * 4,
                          out_shape=[jax.ShapeDtypeStruct(w.shape, F32)] * 4, compiler_params=_cp(("parallel",)))(w, m, v, g)


def _adamw(w, m, v, parts, name, part_row=0, block_rows=None):
    R, C = w.shape
    tr, tc = _elementwise_tile(R, C) if block_rows is None else (block_rows, C)
    assert R % tr == 0 and part_row % tr == 0
    first = part_row // tr
    n = len(parts)
    wspec = pl.BlockSpec((tr, tc), lambda i, j: (i, j))
    flat = pl.BlockSpec((tr, tc), lambda i, j: (first + i, j))
    slot = lambda k: pl.BlockSpec((None, tr, tc), lambda i, j: (k, first + i, j))
    part_specs = [slot(p[1]) if isinstance(p, tuple) else flat for p in parts]
    part_arrays = [p[0] if isinstance(p, tuple) else p for p in parts]

    def body(*refs):
        w_ref, m_ref, v_ref = refs[:3]
        g_ref, d_ref, nm_ref, nv_ref = refs[3 + n:]
        g = refs[3][...].astype(F32)
        for r in refs[4:3 + n]:
            g = g + r[...].astype(F32)
        g_ref[...] = g
        d_ref[...], nm_ref[...], nv_ref[...] = _adam_update(w_ref[...], m_ref[...], v_ref[...], g)

    return pl.pallas_call(
        body, name=name, grid=(R // tr, C // tc), in_specs=[wspec] * 3 + part_specs, out_specs=[wspec] * 4,
        out_shape=[jax.ShapeDtypeStruct(w.shape, F32)] * 4, compiler_params=_cp(("parallel", "parallel")),
    )(w, m, v, *part_arrays)


_SMALL_WIDE = {"ln_a_g": 0, "ln_a_b": 1, "final_g": 8, "ple_norm_g": 9, "ssm_norm_g": 16, "conv_b": 44}
_WIDE_CONV_W, _WIDE_ROWS = 32, 56
_WIDE_LOSS = 10
_SMALL_NARROW = {"w_s": (0, NG * CH, 128), "b_s": (512, NG, 128), "dt_bias": (520, 1, NH), "a_log": (521, 1, NH),
                 "d_skip": (522, 1, NH)}
_NARROW_ROWS = 528
_SMALL_SHAPES = {"norm_g": (1, D), "ln_a_g": (1, D), "ln_a_b": (1, D), "ple_norm_g": (1, D), "final_g": (1, D),
                 "ssm_norm_g": (1, DI), "conv_b": (1, CD), "w_s": (NG * CH, CH), "b_s": (NG, CH), "dt_bias": (1, NH),
                 "a_log": (1, NH), "d_skip": (1, NH)}


def _adamw_small(w, m, v, wide_all, narrow_all, norm_all):
    names = list(_SMALL_SHAPES)
    n = len(names)

    def body(*refs):
        wr, mr, vr = refs[:n], refs[n:2 * n], refs[2 * n:3 * n]
        wide_ref, narrow_ref, norm_ref = refs[3 * n:3 * n + 3]
        outs = refs[3 * n + 3:]
        gr, dr, nmr, nvr, cw_ref, loss_ref = outs[:n], outs[n:2 * n], outs[2 * n:3 * n], outs[3 * n:4 * n], outs[4 * n], outs[4 * n + 1]

        def total(ref, rows, lanes):
            acc = ref[0, rows, lanes]
            for d in range(1, NDEV):
                acc = acc + ref[d, rows, lanes]
            return acc

        for k, name in enumerate(names):
            if name in _SMALL_WIDE or name == "norm_g":
                for part in range(_SMALL_SHAPES[name][1] // D):
                    pack, r = (norm_ref, 0) if name == "norm_g" else (wide_ref, _SMALL_WIDE[name] + part)
                    cols = slice(part * D, (part + 1) * D)
                    g = total(pack, slice(r, r + 1), slice(None))
                    gr[k][:, cols] = g
                    dr[k][:, cols], nmr[k][:, cols], nvr[k][:, cols] = _adam_update(wr[k][:, cols], mr[k][:, cols], vr[k][:, cols], g)
            else:
                r, rows, lanes = _SMALL_NARROW[name]
                g = total(narrow_ref, slice(r, r + rows), slice(0, lanes))
                gr[k][...] = g
                dr[k][...], nmr[k][...], nvr[k][...] = _adam_update(wr[k][...], mr[k][...], vr[k][...], g)
        cw_ref[...] = total(wide_ref, slice(_WIDE_CONV_W, _WIDE_CONV_W + 12), slice(None))
        loss_ref[...] = total(wide_ref, slice(_WIDE_LOSS, _WIDE_LOSS + 1), slice(None))

    shapes = [jax.ShapeDtypeStruct(_SMALL_SHAPES[k], F32) for k in names]
    specs = [_full(_SMALL_SHAPES[k]) for k in names]
    res = pl.pallas_call(
        body, name="adamw_small", grid=(1,),
        in_specs=specs * 3 + [_full(wide_all.shape), _full(narrow_all.shape), _full(norm_all.shape)],
        out_specs=specs * 4 + [_full((12, D)), _full((1, D))],
        out_shape=shapes * 4 + [jax.ShapeDtypeStruct((12, D), F32), jax.ShapeDtypeStruct((1, D), F32)],
        compiler_params=_cp(("arbitrary",)),
    )(*[w[k] for k in names], *[m[k] for k in names], *[v[k] for k in names], wide_all, narrow_all, norm_all)
    groups = [dict(zip(names, res[q * n:(q + 1) * n])) for q in range(4)]
    return groups[0], groups[1], groups[2], groups[3], res[4 * n], res[4 * n + 1][0, 0]


def _dev_index(px, py, pc):
    return 4 * px + 2 * py + pc


def _mesh_position():
    return lax.axis_index("x"), lax.axis_index("y"), lax.axis_index("c")


def _gather_exchange(blocks):
    n = len(blocks)

    def make(ins, outs, sems):
        send_sems, recv_sems, local_sems = sems
        x, y, c = _mesh_position()
        me, sibling = (x, y, c), (x, y, 1 - c)
        chips = [(1 - x, y), (x, 1 - y), (1 - x, 1 - y)]

        def copy(a, k, block, to, src=None):
            dst = outs[a].at[_dev_index(*block)]
            return pltpu.make_async_remote_copy(src_ref=dst if src is None else src, dst_ref=dst, send_sem=send_sems.at[a, k],
                                                recv_sem=recv_sems.at[a, k], device_id=to, device_id_type=MESH)

        mine = [pltpu.make_async_copy(ins[a], outs[a].at[_dev_index(*me)], local_sems.at[a]) for a in range(n)]
        first = []
        for a in range(n):
            first.append(copy(a, 0, me, sibling, src=ins[a]))
            first += [copy(a, 1 + j, me, (*chip, c), src=ins[a]) for j, chip in enumerate(chips)]

        def start():
            for cp in mine + first:
                cp.start()

        def finish():
            passed = []
            for j, chip in enumerate(chips):
                for a in range(n):
                    copy(a, 1 + j, (*chip, c), me).wait_recv()
                    fwd = copy(a, 4 + j, (*chip, c), sibling)
                    fwd.start()
                    passed.append(fwd)
            for a in range(n):
                copy(a, 0, sibling, me).wait_recv()
                for j, chip in enumerate(chips):
                    copy(a, 4 + j, (*chip, 1 - c), me).wait_recv()
            for cp in first + passed:
                cp.wait_send()
            for cp in mine:
                cp.wait()

        return start, finish

    return _Exchange(list(blocks), [jax.ShapeDtypeStruct((NDEV,) + b.shape, b.dtype) for b in blocks],
                     [pltpu.SemaphoreType.DMA((n, 7)), pltpu.SemaphoreType.DMA((n, 7)), pltpu.SemaphoreType.DMA((n,))], make)


def _relay_gather_exchange(blocks):
    n = len(blocks)

    def make(ins, outs, sems):
        send_sems, recv_sems, local_sems = sems
        x, y, c = _mesh_position()
        me, sibling = (x, y, c), (x, y, 1 - c)
        x_nbr, y_nbr, diag = (1 - x, y), (x, 1 - y), (1 - x, 1 - y)
        relay_from = (jnp.where(c == 0, x, 1 - x), jnp.where(c == 0, 1 - y, y))
        relay_to = (jnp.where(c == 0, 1 - x, x), jnp.where(c == 0, y, 1 - y))

        def copy(a, k, block, to, src=None):
            dst = outs[a].at[_dev_index(*block)]
            return pltpu.make_async_remote_copy(src_ref=dst if src is None else src, dst_ref=dst, send_sem=send_sems.at[a, k],
                                                recv_sem=recv_sems.at[a, k], device_id=to, device_id_type=MESH)

        mine = [pltpu.make_async_copy(ins[a], outs[a].at[_dev_index(*me)], local_sems.at[a]) for a in range(n)]
        first = []
        for a in range(n):
            first += [copy(a, 0, me, sibling, src=ins[a]), copy(a, 1, me, (*x_nbr, c), src=ins[a]), copy(a, 2, me, (*y_nbr, c), src=ins[a])]

        def start():
            for cp in mine + first:
                cp.start()

        def finish():
            later = []
            for a in range(n):
                copy(a, 1, (*x_nbr, c), me).wait_recv()
                copy(a, 2, (*y_nbr, c), me).wait_recv()
                later.append(copy(a, 3, (*relay_from, c), (*relay_to, c)))
                later += [copy(a, 4, (*x_nbr, c), sibling), copy(a, 5, (*y_nbr, c), sibling)]
                for cp in later[-3:]:
                    cp.start()
            for a in range(n):
                copy(a, 3, (*diag, c), me).wait_recv()
                later.append(copy(a, 6, (*diag, c), sibling))
                later[-1].start()
            for a in range(n):
                copy(a, 0, sibling, me).wait_recv()
                for k, chip in ((4, x_nbr), (5, y_nbr), (6, diag)):
                    copy(a, k, (*chip, 1 - c), me).wait_recv()
            for cp in first + later:
                cp.wait_send()
            for cp in mine:
                cp.wait()

        return start, finish

    return _Exchange(list(blocks), [jax.ShapeDtypeStruct((NDEV,) + b.shape, b.dtype) for b in blocks],
                     [pltpu.SemaphoreType.DMA((n, 7)), pltpu.SemaphoreType.DMA((n, 7)), pltpu.SemaphoreType.DMA((n,))], make)


def _combine(*exchanges):
    def make(ins, outs, sems):
        pairs = []
        for e in exchanges:
            ni, no, ns = len(e.arrays), len(e.out_shape), len(e.sems)
            pairs.append(e.make(ins[:ni], outs[:no], sems[:ns]))
            ins, outs, sems = ins[ni:], outs[no:], sems[ns:]

        def start():
            for s, _ in pairs:
                s()

        def finish():
            for _, f in pairs:
                f()

        return start, finish

    return _Exchange(sum((list(e.arrays) for e in exchanges), []), sum((list(e.out_shape) for e in exchanges), []),
                     sum((list(e.sems) for e in exchanges), []), make)


def _start_wait_all(copies, local=()):
    def start():
        for cp in list(local) + list(copies):
            cp.start()

    def finish():
        for cp in copies:
            cp.wait()
        for cp in local:
            cp.wait()

    return start, finish


def _no_exchange():
    return _Exchange([], [], [], lambda ins, outs, sems: (lambda: None, lambda: None))


def _direct_exchange(grads):
    n = len(grads)

    def make(ins, outs, sems):
        send_sems, recv_sems = sems
        x, y, c = _mesh_position()
        copies = []
        for a in range(n):
            for r in range(1, NDEV):
                peer = (x ^ ((r >> 2) & 1), y ^ ((r >> 1) & 1), c ^ (r & 1))
                copies.append(pltpu.make_async_remote_copy(
                    src_ref=ins[a].at[_dev_index(*peer)], dst_ref=outs[a].at[r - 1], send_sem=send_sems.at[a, r - 1],
                    recv_sem=recv_sems.at[a, r - 1], device_id=peer, device_id_type=MESH))
        return _start_wait_all(copies)

    return _Exchange(list(grads), [jax.ShapeDtypeStruct((NDEV - 1,) + g.shape[1:], g.dtype) for g in grads],
                     [pltpu.SemaphoreType.DMA((n, NDEV - 1)), pltpu.SemaphoreType.DMA((n, NDEV - 1))], make)


def _direct_gather_exchange(smalls):
    n = len(smalls)

    def make(ins, outs, sems):
        send_sems, recv_sems, local_sems = sems
        x, y, c = _mesh_position()
        copies, local = [], []
        for a in range(n):
            slot = outs[a].at[_dev_index(x, y, c)]
            local.append(pltpu.make_async_copy(ins[a], slot, local_sems.at[a]))
            for r in range(1, NDEV):
                peer = (x ^ ((r >> 2) & 1), y ^ ((r >> 1) & 1), c ^ (r & 1))
                copies.append(pltpu.make_async_remote_copy(src_ref=ins[a], dst_ref=slot, send_sem=send_sems.at[a, r - 1],
                                                           recv_sem=recv_sems.at[a, r - 1], device_id=peer, device_id_type=MESH))
        return _start_wait_all(copies, local=local)

    return _Exchange(list(smalls), [jax.ShapeDtypeStruct((NDEV,) + s.shape, s.dtype) for s in smalls],
                     [pltpu.SemaphoreType.DMA((n, 7)), pltpu.SemaphoreType.DMA((n, 7)), pltpu.SemaphoreType.DMA((n,))], make)


_W_IN_ROWS = {"u": (0, 1024), "v": (1024, 2048), "za": (2048, 3072), "zb": (3072, 5120), "xbc": (5120, 8192),
              "dt": (8192, 8224), "ga": (8224, 9248), "gb": (9248, 10272)}
_PROJ_ORDER = ("xbc", "zb", "u", "v", "za", "ga", "gb", "dt")


def _w_in_t_rows(wt):
    z = jnp.zeros((NP - NIN, wt.shape[1]), wt.dtype)
    return jnp.concatenate([wt[slice(*_W_IN_ROWS[n])] for n in _PROJ_ORDER] + [z], axis=0)


_WEIGHTS = ["norm_g", "w_in", "ln_a_g", "ln_a_b", "w_s", "b_s", "conv_w", "conv_b", "dt_bias", "a_log", "d_skip", "ssm_norm_g",
            "w_oa", "w_ob", "w_out", "ple_norm_g", "w_pg", "w_ple", "final_g"]


def _rows_pack(d, dtype):
    return jnp.concatenate([d["w_oa"].reshape(128, D), d["w_ob"].reshape(256, D), d["w_out"].reshape(128, D),
                            d["w_pg"].reshape(128, D), d["w_ple"].reshape(32, D)], axis=0).astype(dtype)


def kernel(x, p, norm_g, w_in, ln_a_g, ln_a_b, w_s, b_s, conv_w, conv_b, dt_bias, a_log, d_skip, ssm_norm_g, w_oa, w_ob, w_out, ple_norm_g, w_pg, w_ple, final_g, loss_target, m_norm_g, m_w_in, m_ln_a_g, m_ln_a_b, m_w_s, m_b_s, m_conv_w, m_conv_b, m_dt_bias, m_a_log, m_d_skip, m_ssm_norm_g, m_w_oa, m_w_ob, m_w_out, m_ple_norm_g, m_w_pg, m_w_ple, m_final_g, v_norm_g, v_w_in, v_ln_a_g, v_ln_a_b, v_w_s, v_b_s, v_conv_w, v_conv_b, v_dt_bias, v_a_log, v_d_skip, v_ssm_norm_g, v_w_oa, v_w_ob, v_w_out, v_ple_norm_g, v_w_pg, v_w_ple, v_final_g):
    args = dict(locals())
    w = {n: args[n] for n in _WEIGHTS}
    m = {n: args["m_" + n] for n in _WEIGHTS}
    v = {n: args["v_" + n] for n in _WEIGHTS}
    T = x.shape[1]
    xi, yi, ci = lax.axis_index("x"), lax.axis_index("y"), lax.axis_index("c")
    me = 4 * xi + 2 * yi + ci
    x2, p2, tgt = x.reshape(T, D), p.reshape(T, PLE), loss_target.reshape(T, D)

    norm_g2 = w["norm_g"].reshape(1, D)
    ws = jnp.where(jnp.tril(jnp.ones((CH, CH), bool))[None], w["w_s"].reshape(NG, CH, CH), 0.0).astype(BF16)
    wst = jnp.transpose(ws, (0, 2, 1))
    bst = jnp.broadcast_to(w["b_s"].reshape(NG, CH, 1), (NG, CH, 256))
    ln_g, ln_b = w["ln_a_g"].reshape(1, D), w["ln_a_b"].reshape(1, D)
    cb = w["conv_b"].reshape(1, CD)
    pad32 = lambda a: jnp.pad(a.reshape(1, NH), ((0, 0), (0, DTW - NH)))
    dtb, alog = pad32(w["dt_bias"]), pad32(w["a_log"])
    dskx = jnp.repeat(w["d_skip"].reshape(NH), HD).reshape(1, DI)
    sg = w["ssm_norm_g"].reshape(1, DI)
    ple_g, fin_g = w["ple_norm_g"].reshape(1, D), w["final_g"].reshape(1, D)
    e128 = (jnp.arange(DTW)[:, None] == (jnp.arange(DI)[None, :] // HD)).astype(BF16)
    et128 = e128.T
    gsel = ((jnp.arange(D)[:, None] // 256) == jnp.arange(128)[None, :]).astype(BF16)

    w_in_t = lambda a: jnp.transpose(a.reshape(D, WSH))
    (a_all,) = _run_exchange(_relay_gather_exchange([w_in_t(w["w_in"]).astype(BF16)]), "all_gather_w_in")
    w_in_full_t = a_all.reshape(NIN, D)
    wp = _w_in_t_rows(w_in_full_t)
    (proj, dtr, h), (r_all, cw_all) = _proj_fwd(x2, norm_g2, wp, _gather_exchange([_rows_pack(w, BF16), w["conv_w"].reshape(4, CD // NDEV)]))
    f_oa = r_all[:, R_OA:R_OB].reshape(D, D)
    f_ob = r_all[:, R_OB:R_OUT].reshape(DI, D)
    f_out = r_all[:, R_OUT:R_PG].reshape(D, D)
    f_pg = r_all[:, R_PG:R_PLE].reshape(D, D)
    f_ple = jnp.transpose(r_all[:, R_PLE:R_ROWS].reshape(NDEV, PLE, 128), (1, 0, 2)).reshape(PLE, D)
    cw = jnp.transpose(cw_all, (1, 0, 2)).reshape(4, CD)

    y, yb, hprev, pre_all = _ssd_fwd(proj, dtr, cw, cb, dtb, alog, dskx, sg, e128)
    dx1, dx1b, ya, mg, hp, dpre, dpe, doa, dob, dya, dyb, dgab, acc = _merge(
        x2, yb, proj, p2, tgt, (ln_g, ln_b, ws, bst), f_oa, f_ob, f_out, f_pg, f_ple, ple_g, fin_g)

    gple = jnp.transpose(_wgrad(p2, dpe, "wgrad_ple").reshape(PLE, NDEV, 128), (1, 0, 2)).reshape(NDEV, 32, D)
    gr = jnp.concatenate([_wgrad(ya, doa, "wgrad_oa").reshape(NDEV, 128, D), _wgrad(yb, dob, "wgrad_ob").reshape(NDEV, 256, D),
                          _wgrad(mg, dx1b, "wgrad_out").reshape(NDEV, 128, D), _wgrad(hp, dpre, "wgrad_pg").reshape(NDEV, 128, D),
                          gple], axis=1)
    (duvz, dws, dbs, dln), _ = _gmlp_bwd(proj, dya, ln_g, ln_b, ws, wst, bst, gsel, _no_exchange())
    (dxz, ddt, dcw, dsm, dsg), (rr,) = _ssd_bwd(proj, pre_all, dtr, y, dyb, hprev, cw, dtb, alog, dskx, sg, e128, et128, _direct_exchange([gr]))

    g_xz = _wgrad(dxz, h, "wgrad_xbc_zb")
    g_w_in_t = jnp.concatenate([_wgrad(duvz, h, "wgrad_u_v_za"), g_xz[CD:], g_xz[:CD], _wgrad(ddt, h, "wgrad_dt")[:NH],
                                _wgrad(dgab, h, "wgrad_ga_gb")], axis=0)
    ga = g_w_in_t.reshape(NDEV, WSH, D)
    wide = jnp.concatenate([dln, acc, dsg.reshape(16, D), dcw.reshape(24, D)], axis=0)
    narrow = jnp.concatenate([dws.reshape(NG * CH, CH), jnp.pad(dbs[:, :NG].T, ((0, 8 - NG), (0, 0))), dsm], axis=0)
    (gx, dng), (ra, wide_all, narrow_all) = _proj_bwd(x2, dx1, norm_g2, wp, [dxz, duvz, dgab], ddt,
                                                      _combine(_direct_exchange([ga]), _direct_gather_exchange([wide, narrow])))
    ga_own = lax.dynamic_index_in_dim(ga, me, 0, keepdims=False)
    gr_own = lax.dynamic_index_in_dim(gr, me, 0, keepdims=False)

    out_g, out_d, out_m, out_v = {}, {}, {}, {}
    outs = (out_g, out_d, out_m, out_v)
    tiles = lambda a: jnp.transpose(a.reshape(8, 128, WSH), (2, 0, 1))
    g2, (norm_all,) = _sum_parts([ga_own] + [(ra, k) for k in range(NDEV - 1)], "sum_w_in", _direct_gather_exchange([dng]))
    res = _adamw_tiles(tiles(w["w_in"]), tiles(m["w_in"]), tiles(v["w_in"]), g2.reshape(WSH, 8, 128), "adamw_w_in")
    for dst, val in zip(outs, res):
        dst["w_in"] = jnp.transpose(val, (1, 2, 0)).reshape(1, D, WSH)
    parts_r = [gr_own] + [(rr, k) for k in range(NDEV - 1)]
    for name, row, rows in (("w_oa", R_OA, 128), ("w_ob", R_OB, 256), ("w_out", R_OUT, 128), ("w_pg", R_PG, 128)):
        res = _adamw(w[name].reshape(rows, D), m[name].reshape(rows, D), v[name].reshape(rows, D), parts_r, "adamw_" + name,
                     part_row=row, block_rows=128)
        for dst, val in zip(outs, res):
            dst[name] = val.reshape(1, rows, D)
    res = _adamw(w["w_ple"].reshape(32, D), m["w_ple"].reshape(32, D), v["w_ple"].reshape(32, D), parts_r, "adamw_w_ple",
                 part_row=R_PLE, block_rows=32)
    for dst, val in zip(outs, res):
        dst["w_ple"] = val.reshape(1, PLE, 128)
    two_d = lambda d: {n: d[n].reshape(_SMALL_SHAPES[n]) for n in _SMALL_SHAPES}
    *res, g_cw_wide, loss = _adamw_small(two_d(w), two_d(m), two_d(v), wide_all, narrow_all, norm_all)
    for dst, val in zip(outs, res):
        dst.update({n: val[n].reshape(w[n].shape) for n in _SMALL_SHAPES})
    g_cw = lax.dynamic_slice_in_dim(g_cw_wide.reshape(4, CD), me * (CD // NDEV), CD // NDEV, axis=1).reshape(12, 128)
    res = _adamw(w["conv_w"].reshape(12, 128), m["conv_w"].reshape(12, 128), v["conv_w"].reshape(12, 128), [g_cw], "adamw_conv_w")
    for dst, val in zip((out_g, out_d, out_m, out_v), res):
        dst["conv_w"] = val.reshape(1, 4, CD // NDEV)

    return (loss, gx.reshape(1, T, D), *[out_g[n] for n in _WEIGHTS], *[out_d[n] for n in _WEIGHTS],
            *[out_m[n] for n in _WEIGHTS], *[out_v[n] for n in _WEIGHTS])
```

```python
import functools
import math
from typing import Callable, NamedTuple

import jax
import jax.numpy as jnp
from jax import lax
from jax.experimental import pallas as pl
from jax.experimental.pallas import tpu as pltpu

F32 = jnp.float32
BF16 = jnp.bfloat16
MESH = pl.DeviceIdType.MESH

D = 1024
DI = 2048
CD = 3072
NH = 32
HD = 64
NST = 128
NG = 4
CH = 128
PLE = 256
NIN = 10272
NDEV = 8
WSH = NIN // NDEV
EPS = 1e-6
OFF_XBC, OFF_ZB, OFF_U, OFF_V, OFF_ZA, OFF_GA, OFF_GB, OFF_DT = 0, 3072, 5120, 6144, 7168, 8192, 9216, 10240
NP = 10368
DTW = 128
R_OA, R_OB, R_OUT, R_PG, R_PLE, R_ROWS = 0, 128, 384, 512, 640, 672

ADAM_LR, ADAM_B1, ADAM_B2, ADAM_EPS, ADAM_WD, ADAM_STEP = 0.001, 0.9, 0.999, 1e-08, 0.01, 10

V7X_VMEM_LIMIT = 56 * 1024 * 1024


def _cp(sem=None):
    return pltpu.CompilerParams(dimension_semantics=sem, vmem_limit_bytes=V7X_VMEM_LIMIT)


def _dot(a, b, prec=None):
    return jnp.dot(a, b, preferred_element_type=F32, precision=prec)


def _dot_nt(a, b, prec=None):
    return lax.dot_general(a, b, (((1,), (1,)), ((), ())), preferred_element_type=F32, precision=prec)


def _dot_tn(a, b, prec=None):
    return lax.dot_general(a, b, (((0,), (0,)), ((), ())), preferred_element_type=F32, precision=prec)


def _sigmoid(x):
    return 0.5 * jnp.tanh(0.5 * x) + 0.5


_GELU_C = math.sqrt(2.0 / math.pi)
_GELU_K = 0.044715


def _gelu_and_grad(x):
    x2 = x * x
    t = jnp.tanh(x * (_GELU_C + (_GELU_C * _GELU_K) * x2))
    hx = 0.5 * x
    g = hx + hx * t
    dg = (0.5 + 0.5 * t) + hx * (1.0 - t * t) * (_GELU_C + (3.0 * _GELU_C * _GELU_K) * x2)
    return g, dg


def _gelu(x):
    hx = 0.5 * x
    return hx + hx * jnp.tanh(x * (_GELU_C + (_GELU_C * _GELU_K) * (x * x)))


def _softplus(x):
    return jnp.maximum(x, 0.0) + jnp.log(1.0 + jnp.exp(-jnp.abs(x)))


def _full(shape):
    n = len(shape)
    return pl.BlockSpec(shape, lambda *_: (0,) * n)


_ANY = pl.BlockSpec(memory_space=pl.ANY)


class _Exchange(NamedTuple):
    arrays: list
    out_shape: list
    sems: list
    make: Callable


def _call(body, ex, *, name, grid, in_specs, out_specs, out_shape, scratch_shapes, args):
    ki, ko, ks = len(in_specs), len(out_specs), len(scratch_shapes)
    ei, eo = len(ex.arrays), len(ex.out_shape)
    last = [g - 1 for g in grid]

    def full_body(*refs):
        r = list(refs)
        ins, eins, r = r[:ki], r[ki:ki + ei], r[ki + ei:]
        outs, eouts, r = r[:ko], r[ko:ko + eo], r[ko + eo:]
        scr, esems = r[:ks], r[ks:]
        start, finish = ex.make(eins, eouts, esems)
        ids = [pl.program_id(a) for a in range(len(grid))]
        is_first = functools.reduce(lambda p, q: p & q, [i == 0 for i in ids])
        is_last = functools.reduce(lambda p, q: p & q, [i == l for i, l in zip(ids, last)])
        pl.when(is_first)(start)
        body(*ins, *outs, *scr)
        pl.when(is_last)(finish)

    res = pl.pallas_call(
        full_body, name=name, grid=grid, in_specs=list(in_specs) + [_ANY] * ei, out_specs=list(out_specs) + [_ANY] * eo,
        out_shape=list(out_shape) + list(ex.out_shape), scratch_shapes=list(scratch_shapes) + list(ex.sems),
        compiler_params=_cp(("arbitrary",) * len(grid)),
    )(*args, *ex.arrays)
    return res[:ko], res[ko:]


def _run_exchange(ex, name):
    ni, no = len(ex.arrays), len(ex.out_shape)

    def body(*refs):
        start, finish = ex.make(refs[:ni], refs[ni:ni + no], refs[ni + no:])
        start()
        finish()

    return pl.pallas_call(body, name=name, in_specs=[_ANY] * ni, out_specs=[_ANY] * no, out_shape=list(ex.out_shape),
                          scratch_shapes=list(ex.sems))(*ex.arrays)


def _proj_fwd(x, norm_g, wp, ex):
    T = x.shape[0]
    tm, tn = min(T, 1024), 2048
    nj = OFF_DT // tn
    assert OFF_DT % tn == 0 and OFF_DT + DTW == NP

    def body(x_ref, g_ref, w_ref, wdt_ref, proj_ref, dt_ref, h_ref, hs_ref):
        j = pl.program_id(1)

        @pl.when(j == 0)
        def _():
            xf = x_ref[...]
            r = lax.rsqrt(jnp.mean(xf * xf, axis=-1, keepdims=True) + EPS)
            h = (xf * r * g_ref[...]).astype(BF16)
            hs_ref[...] = h
            h_ref[...] = h

        proj_ref[...] = _dot_nt(hs_ref[...], w_ref[...]).astype(BF16)

        @pl.when(j == nj - 1)
        def _():
            dt_ref[...] = _dot_nt(hs_ref[...], wdt_ref[...])

    return _call(
        body, ex, name="proj_fwd", grid=(T // tm, nj),
        in_specs=[pl.BlockSpec((tm, D), lambda i, j: (i, 0)), _full((1, D)), pl.BlockSpec((tn, D), lambda i, j: (j, 0)),
                  pl.BlockSpec((DTW, D), lambda i, j: (OFF_DT // DTW, 0))],
        out_specs=[pl.BlockSpec((tm, tn), lambda i, j: (i, j)), pl.BlockSpec((tm, DTW), lambda i, j: (i, 0)),
                   pl.BlockSpec((tm, D), lambda i, j: (i, 0))],
        out_shape=[jax.ShapeDtypeStruct((T, OFF_DT), BF16), jax.ShapeDtypeStruct((T, DTW), F32), jax.ShapeDtypeStruct((T, D), BF16)],
        scratch_shapes=[pltpu.VMEM((tm, D), BF16)], args=(x, norm_g, wp, wp))


def _gmlp_tile():
    return 512


def _gmlp_fwd_tile(u_ref, v_ref, z_ref, lg_ref, lb_ref, ws_ref, bs_ref, ya_ref, vn_s):
    tm = u_ref.shape[0]
    vg = _gelu(v_ref[...].astype(F32))
    mu = jnp.mean(vg, axis=-1, keepdims=True)
    xc = vg - mu
    rstd = lax.rsqrt(jnp.mean(xc * xc, axis=-1, keepdims=True) + EPS)
    vn_s[...] = (xc * rstd * lg_ref[...] + lb_ref[...]).astype(BF16)
    for c in range(tm // CH):
        rs = slice(c * CH, (c + 1) * CH)
        for g in range(NG):
            cs_ = slice(g * 256, (g + 1) * 256)
            sv = _dot(ws_ref[g], vn_s[rs, cs_]) + bs_ref[g]
            z = z_ref[rs, cs_].astype(F32)
            ya_ref[rs, cs_] = (_gelu(u_ref[rs, cs_].astype(F32)) * sv * (z * _sigmoid(z))).astype(BF16)


def _gmlp_bwd(proj, dya, ln_g, ln_b, ws, wst, bst, gsel, ex):
    T = proj.shape[0]
    tm = min(T, _gmlp_tile())

    def body(u_ref, v_ref, z_ref, dy_ref, lg_ref, lb_ref, ws_ref, wst_ref, bs_ref, gsel_ref,
             d_ref, dws_ref, dbs_ref, dln_ref, vn_s, dsv_s, dvn_s):
        du_ref, dv_ref, dz_ref = d_ref.at[:, 0:D], d_ref.at[:, D:2 * D], d_ref.at[:, 2 * D:3 * D]
        @pl.when(pl.program_id(0) == 0)
        def _():
            dws_ref[...] = jnp.zeros_like(dws_ref)
            dbs_ref[...] = jnp.zeros_like(dbs_ref)
            dln_ref[...] = jnp.zeros_like(dln_ref)

        vg, dvg_dv = _gelu_and_grad(v_ref[...].astype(F32))
        mu = jnp.mean(vg, axis=-1, keepdims=True)
        xc = vg - mu
        rstd = lax.rsqrt(jnp.mean(xc * xc, axis=-1, keepdims=True) + EPS)
        vhat = xc * rstd
        vn_s[...] = (vhat * lg_ref[...] + lb_ref[...]).astype(BF16)
        ri = lax.broadcasted_iota(jnp.int32, (CH, CH), 0)
        ci = lax.broadcasted_iota(jnp.int32, (CH, CH), 1)
        tril = (ri >= ci).astype(F32)
        for c in range(tm // CH):
            rs = slice(c * CH, (c + 1) * CH)
            for g in range(NG):
                cs_ = slice(g * 256, (g + 1) * 256)
                vn = vn_s[rs, cs_]
                sv = _dot(ws_ref[g], vn) + bs_ref[g]
                z = z_ref[rs, cs_].astype(F32)
                sz = _sigmoid(z)
                ug, dug_du = _gelu_and_grad(u_ref[rs, cs_].astype(F32))
                dy = dy_ref[rs, cs_].astype(F32)
                zs = z * sz
                t = dy * zs
                dsv_f = dy * sv
                du_ref[rs, cs_] = (zs * dsv_f * dug_du).astype(BF16)
                dz_ref[rs, cs_] = (dsv_f * ug * (sz + zs * (1.0 - sz))).astype(BF16)
                dsv = (t * ug).astype(BF16)
                dsv_s[rs, cs_] = dsv
                dvn_s[rs, cs_] = _dot(wst_ref[g], dsv)
                dws_ref[g] += _dot_nt(dsv, vn) * tril
            dbs_ref[...] += _dot(dsv_s[rs, :], gsel_ref[...])
        dvn = dvn_s[...]
        dln_ref[0:1, :] += jnp.sum(dvn * vhat, axis=0, keepdims=True)
        dln_ref[1:2, :] += jnp.sum(dvn, axis=0, keepdims=True)
        dvh = dvn * lg_ref[...]
        dvg = rstd * (dvh - jnp.mean(dvh, axis=-1, keepdims=True) - vhat * jnp.mean(dvh * vhat, axis=-1, keepdims=True))
        dv_ref[...] = (dvg * dvg_dv).astype(BF16)

    blk = lambda off: pl.BlockSpec((tm, D), lambda i: (i, off // D))
    row = pl.BlockSpec((tm, D), lambda i: (i, 0))
    return _call(
        body, ex, name="gmlp_bwd", grid=(T // tm,),
        in_specs=[blk(OFF_U), blk(OFF_V), blk(OFF_ZA), row, _full((1, D)), _full((1, D)), _full((NG, CH, CH)),
                  _full((NG, CH, CH)), _full((NG, CH, 256)), _full((D, 128))],
        out_specs=[pl.BlockSpec((tm, 3 * D), lambda i: (i, 0)), _full((NG, CH, CH)), _full((CH, 128)), _full((8, D))],
        out_shape=[jax.ShapeDtypeStruct((T, 3 * D), BF16),
                   jax.ShapeDtypeStruct((NG, CH, CH), F32), jax.ShapeDtypeStruct((CH, 128), F32), jax.ShapeDtypeStruct((8, D), F32)],
        scratch_shapes=[pltpu.VMEM((tm, D), BF16), pltpu.VMEM((tm, D), BF16), pltpu.VMEM((tm, D), F32)],
        args=(proj, proj, proj, dya, ln_g, ln_b, ws, wst, bst, gsel))


def _zb_cols(zb0_ref, zb1_ref, cols):
    ref = zb0_ref if cols.start < D else zb1_ref
    return ref[:, cols.start % D:cols.start % D + (cols.stop - cols.start)]


def _shift_matrix(down):
    t = jnp.arange(CH)[None, :, None]
    j = jnp.arange(1, 4)[:, None, None]
    col = jnp.arange(2 * CH)[None, None, :]
    src = CH + t - j if down else t + j
    return (col == src).astype(BF16).reshape(3 * CH, 2 * CH)


def _conv_pre(x, moved, cw_ref, cb_ref):
    pre = cb_ref[...] + cw_ref[3:4, :] * x
    for j in (1, 2, 3):
        pre = pre + cw_ref[3 - j:4 - j, :] * moved[(j - 1) * CH:j * CH]
    return pre


def _split_dot(x, w, parts, w_left=False):
    acc, r = None, x
    for k in range(parts):
        hi = r.astype(BF16)
        d = _dot(w, hi) if w_left else _dot(hi, w)
        acc = d if acc is None else acc + d
        if k + 1 < parts:
            r = r - hi.astype(F32)
    return acc


def _chunk_decays(dt, alog_ref, e_ref, cs_s, cst_s, csx_s):
    a = -jnp.exp(alog_ref[...])
    ri = lax.broadcasted_iota(jnp.int32, (CH, CH), 0)
    ci = lax.broadcasted_iota(jnp.int32, (CH, CH), 1)
    tril = ri >= ci
    cs = _split_dot(dt * a, tril.astype(BF16), 3, w_left=True)
    cs_s[...] = cs
    cst_s[...] = cs.T
    csx_s[...] = _split_dot(cs, e_ref[...], 3)
    return a, tril, ri, ci


def _lmat(cst_s, h, tril):
    rowb = jnp.broadcast_to(cst_s[h:h + 1, :], (CH, CH))
    return jnp.exp(jnp.where(tril, rowb.T - rowb, -jnp.inf))


def _head_pair_rows(v, lane):
    return jnp.concatenate([jnp.where(lane < HD, v, 0.0), jnp.where(lane < HD, 0.0, v)], axis=0).astype(BF16)


def _ssd_fwd(proj, dtr, cw, cb, dtb, alog, dskx, sg, e128):
    T = proj.shape[0]
    nc = T // CH

    def body(xbc_ref, zb0_ref, zb1_ref, dt_ref, cw_ref, cb_ref, dtb_ref, alog_ref, dx_ref, sg_ref, e_ref, shift_ref,
             y_ref, yb_ref, hp_ref, pre_ref, xx_s, h_s, cs_s, cst_s, csx_s, yz_s):
        @pl.when(pl.program_id(0) == 0)
        def _():
            xx_s[...] = jnp.zeros_like(xx_s)
            h_s[...] = jnp.zeros_like(h_s)

        xx_s[CH:, :] = xbc_ref[...]
        moved = _dot(shift_ref[...], xx_s[...])
        xx_s[CH - 16:CH, :] = xbc_ref[CH - 16:, :]
        x = xbc_ref[...].astype(F32)
        pre = _conv_pre(x, moved, cw_ref, cb_ref)
        pre_ref[...] = pre
        xc = pre * _sigmoid(pre)
        dt = _softplus(dt_ref[...] + dtb_ref[...])
        a, tril, _, lane = _chunk_decays(dt, alog_ref, e_ref, cs_s, cst_s, csx_s)
        dt_x = _split_dot(dt, e_ref[...], 2)
        cs_last_x = csx_s[CH - 1:CH, :]
        hp_ref[0] = h_s[...]
        for g in range(NG):
            gs = slice(g * 512, (g + 1) * 512)
            bg = xc[:, DI + g * NST:DI + (g + 1) * NST].astype(BF16)
            cg = xc[:, DI + 512 + g * NST:DI + 512 + (g + 1) * NST].astype(BF16)
            cbm = _dot_nt(cg, bg)
            xg = xc[:, gs]
            xdt = xg * dt_x[:, gs]
            hprev = h_s[:, gs]
            csx = csx_s[:, gs]
            yoff = _dot(cg, hprev.astype(BF16)) * jnp.exp(csx)
            st = _dot_tn(bg, (xdt * jnp.exp(cs_last_x[:, gs] - csx)).astype(BF16))
            h_s[:, gs] = jnp.exp(cs_last_x[:, gs]) * hprev + st
            ssq = jnp.zeros((CH, 1), F32)
            for q in range(4):
                h0 = g * 8 + 2 * q
                ps = slice(q * 128, (q + 1) * 128)
                cols = slice(g * 512 + q * 128, g * 512 + (q + 1) * 128)
                m01 = jnp.concatenate([cbm * _lmat(cst_s, h0, tril), cbm * _lmat(cst_s, h0 + 1, tril)], axis=1).astype(BF16)
                yq = _dot(m01, _head_pair_rows(xdt[:, ps], lane)) + yoff[:, ps] + xg[:, ps] * dx_ref[:, cols]
                y_ref[:, cols] = yq
                z = _zb_cols(zb0_ref, zb1_ref, cols).astype(F32)
                yz = yq * z * _sigmoid(z)
                yz_s[:, cols] = yz
                ssq = ssq + jnp.sum(yz * yz, axis=1, keepdims=True)
            rg = lax.rsqrt(ssq * (1.0 / 512.0) + EPS)
            yb_ref[:, gs] = (yz_s[:, gs] * rg * sg_ref[:, gs]).astype(BF16)

    return pl.pallas_call(
        body, name="ssd_fwd", grid=(nc,),
        in_specs=[pl.BlockSpec((CH, CD), lambda c: (c, OFF_XBC // CD)), pl.BlockSpec((CH, D), lambda c: (c, OFF_ZB // D)),
                  pl.BlockSpec((CH, D), lambda c: (c, OFF_ZB // D + 1)),
                  pl.BlockSpec((CH, DTW), lambda c: (c, 0)), _full((4, CD)), _full((1, CD)), _full((1, DTW)),
                  _full((1, DTW)), _full((1, DI)), _full((1, DI)), _full((DTW, DI)), _full((3 * CH, 2 * CH))],
        out_specs=[pl.BlockSpec((CH, DI), lambda c: (c, 0)), pl.BlockSpec((CH, DI), lambda c: (c, 0)),
                   pl.BlockSpec((1, NST, DI), lambda c: (c, 0, 0)), pl.BlockSpec((CH, CD), lambda c: (c, 0))],
        out_shape=[jax.ShapeDtypeStruct((T, DI), F32), jax.ShapeDtypeStruct((T, DI), BF16),
                   jax.ShapeDtypeStruct((nc, NST, DI), F32), jax.ShapeDtypeStruct((T, CD), F32)],
        scratch_shapes=[pltpu.VMEM((2 * CH, CD), BF16), pltpu.VMEM((NST, DI), F32), pltpu.VMEM((CH, CH), F32),
                        pltpu.VMEM((CH, CH), F32), pltpu.VMEM((CH, DI), F32), pltpu.VMEM((CH, DI), F32)],
        compiler_params=_cp(("arbitrary",)),
    )(proj, proj, proj, dtr, cw, cb, dtb, alog, dskx, sg, e128, _shift_matrix(down=True))


def _ssd_bwd(proj, pre_all, dtr, y, dyb, hprev_all, cw, dtb, alog, dskx, sg, e128, et128, ex):
    T = proj.shape[0]
    nc = T // CH

    def body(xbc_ref, pre_ref, zb0_ref, zb1_ref, dt_ref, y_ref, dyb_ref, hp_ref, cw_ref, dtb_ref, alog_ref, dx_ref, sg_ref,
             e_ref, et_ref, shift_ref, d_ref, ddt_ref, dcw_ref, dsm_ref, dsg_ref,
             g_s, dd_s, cs_s, cst_s, csx_s, dy_s, dxdt_s, dxs_s, dsd_s, dxc_s, gh_s):
        dxbc_ref, dzb_ref = d_ref.at[:, 0:CD], d_ref.at[:, CD:CD + DI]
        i = pl.program_id(0)

        @pl.when(i == 0)
        def _():
            g_s[...] = jnp.zeros_like(g_s)
            dd_s[...] = jnp.zeros_like(dd_s)
            dcw_ref[...] = jnp.zeros_like(dcw_ref)
            dsm_ref[...] = jnp.zeros_like(dsm_ref)
            dsg_ref[...] = jnp.zeros_like(dsg_ref)

        pre = pre_ref[...]
        sp = _sigmoid(pre)
        xc = pre * sp
        dtr = dt_ref[...] + dtb_ref[...]
        dt = _softplus(dtr)
        a, tril, ri, lane = _chunk_decays(dt, alog_ref, e_ref, cs_s, cst_s, csx_s)
        et = et_ref[...]
        dt_x = _split_dot(dt, e_ref[...], 2)
        cs_last_x = csx_s[CH - 1:CH, :]

        for g in range(NG):
            gs = slice(g * 512, (g + 1) * 512)
            z = _zb_cols(zb0_ref, zb1_ref, gs).astype(F32)
            sz = _sigmoid(z)
            yv = y_ref[:, gs]
            zs = z * sz
            yz = yv * zs
            rg = lax.rsqrt(jnp.mean(yz * yz, axis=-1, keepdims=True) + EPS)
            yn = yz * rg
            dyb = dyb_ref[:, gs].astype(F32)
            dsg_ref[0:1, gs] += jnp.sum(dyb * yn, axis=0, keepdims=True)
            dyn = dyb * sg_ref[:, gs]
            dyz = rg * (dyn - yn * jnp.mean(dyn * yn, axis=-1, keepdims=True))
            dy_s[:, gs] = dyz * zs
            dzb_ref[:, gs] = (dyz * yv * (sz + zs * (1.0 - sz))).astype(BF16)

        rsum = jnp.zeros((CH, DTW), F32)
        csum_t = jnp.zeros((DTW, CH), F32)
        for g in range(NG):
            gs = slice(g * 512, (g + 1) * 512)
            bg = xc[:, DI + g * NST:DI + (g + 1) * NST].astype(BF16)
            cg = xc[:, DI + 512 + g * NST:DI + 512 + (g + 1) * NST].astype(BF16)
            cbm = _dot_nt(cg, bg)
            xdt = xc[:, gs] * dt_x[:, gs]
            hprev = hp_ref[0, :, gs]
            hpb = hprev.astype(BF16)
            gn = g_s[:, gs]
            gnb = gn.astype(BF16)
            dy = dy_s[:, gs]
            csx = csx_s[:, gs]
            ecs = jnp.exp(csx)
            dec = jnp.exp(cs_last_x[:, gs] - csx)
            dye_f = dy * ecs
            dye = dye_f.astype(BF16)
            dc = _dot_nt(dye, hpb)
            dprev = _dot_tn(cg, dye)
            dxdt_state = dec * _dot(bg, gnb)
            db = _dot_nt((xdt * dec).astype(BF16), gnb)
            dcb = jnp.zeros((CH, CH), F32)
            for q in range(4):
                h0 = g * 8 + 2 * q
                ps = slice(q * 128, (q + 1) * 128)
                dyp = dy[:, ps]
                l0 = _lmat(cst_s, h0, tril)
                l1 = _lmat(cst_s, h0 + 1, tril)
                m0 = cbm * l0
                m1 = cbm * l1
                dm = _dot_nt(dyp.astype(BF16), _head_pair_rows(xdt[:, ps], lane))
                dm0 = dm[:, :CH]
                dm1 = dm[:, CH:]
                dcb = dcb + dm0 * l0 + dm1 * l1
                for hh, qm in ((h0, dm0 * m0), (h0 + 1, dm1 * m1)):
                    rsum = jnp.where(lane == hh, jnp.sum(qm, axis=1, keepdims=True), rsum)
                    csum_t = jnp.where(ri == hh, jnp.sum(qm, axis=0, keepdims=True), csum_t)
                mst = jnp.concatenate([m0, m1], axis=0).astype(BF16)
                d = _dot_tn(mst, _head_pair_rows(dyp, lane))
                dxdt_s[:, g * 512 + q * 128:g * 512 + (q + 1) * 128] = d + dxdt_state[:, ps]
            dsd_s[:, gs] = xdt * dxdt_state
            dxs_s[:, gs] = dye_f * _dot(cg, hpb)
            dcbb = dcb.astype(BF16)
            dxc_s[:, DI + 512 + g * NST:DI + 512 + (g + 1) * NST] = dc + _dot(dcbb, bg)
            dxc_s[:, DI + g * NST:DI + (g + 1) * NST] = db + _dot_tn(dcbb, cg)
            gh_s[:, gs] = jnp.broadcast_to(jnp.sum(gn * hprev, axis=0, keepdims=True), (8, 512))
            g_s[:, gs] = dprev + jnp.exp(cs_last_x[:, gs]) * gn

        xs = xc[:, :DI]
        dy = dy_s[...]
        dxdt = dxdt_s[...]
        cs_last = cs_s[CH - 1:CH, :]
        state_e = _split_dot(dsd_s[...], et, 1)
        dcd = 0.125 * jnp.sum(_split_dot(gh_s[...], et, 2), axis=0, keepdims=True) * jnp.exp(cs_last)
        row = lax.broadcasted_iota(jnp.int32, (CH, 1), 0)
        dcs = rsum - csum_t.T + _split_dot(dxs_s[...], et, 1) - state_e
        dcs = dcs + jnp.where(row == CH - 1, jnp.sum(state_e, axis=0, keepdims=True) + dcd, 0.0)
        dda = _split_dot(dcs, (lane >= ri).astype(BF16), 3, w_left=True)
        ddt = dda * a + _dot((dxdt * xs).astype(BF16), et)
        ddtr = jnp.where(lane < NH, ddt * _sigmoid(dtr), 0.0)
        ddt_ref[...] = ddtr.astype(BF16)
        dsm_ref[0:1, :] += jnp.sum(ddtr, axis=0, keepdims=True)
        dsm_ref[1:2, :] += jnp.sum(dda * dt, axis=0, keepdims=True) * a
        dsm_ref[2:3, :] += jnp.sum(_dot((dy * xs).astype(BF16), et), axis=0, keepdims=True)
        dxc_s[:, :DI] = dxdt * dt_x + dy * dx_ref[...]

        dpre = dxc_s[...] * (sp + xc * (1.0 - sp))
        dpre_b = dpre.astype(BF16)
        dd_s[:CH, :] = dpre_b
        moved = _dot(shift_ref[...], dd_s[...])
        dd_s[CH:CH + 16, :] = dpre_b[:16]
        x = xbc_ref[...].astype(F32)
        dcw_ref[4:5, :] += jnp.sum(dpre, axis=0, keepdims=True)
        dxbc = cw_ref[3:4, :] * dpre
        dcw_ref[3:4, :] += jnp.sum(dpre * x, axis=0, keepdims=True)
        for j in (1, 2, 3):
            ahead = moved[(j - 1) * CH:j * CH]
            dcw_ref[3 - j:4 - j, :] += jnp.sum(ahead * x, axis=0, keepdims=True)
            dxbc = dxbc + cw_ref[3 - j:4 - j, :] * ahead
        dxbc_ref[...] = dxbc.astype(BF16)

    rev = lambda c: nc - 1 - c
    return _call(
        body, ex, name="ssd_bwd", grid=(nc,),
        in_specs=[pl.BlockSpec((CH, CD), lambda c: (rev(c), OFF_XBC // CD)),
                  pl.BlockSpec((CH, CD), lambda c: (rev(c), 0)),
                  pl.BlockSpec((CH, D), lambda c: (rev(c), OFF_ZB // D)), pl.BlockSpec((CH, D), lambda c: (rev(c), OFF_ZB // D + 1)),
                  pl.BlockSpec((CH, DTW), lambda c: (rev(c), 0)),
                  pl.BlockSpec((CH, DI), lambda c: (rev(c), 0)), pl.BlockSpec((CH, DI), lambda c: (rev(c), 0)),
                  pl.BlockSpec((1, NST, DI), lambda c: (rev(c), 0, 0)),
                  _full((4, CD)), _full((1, DTW)), _full((1, DTW)), _full((1, DI)), _full((1, DI)),
                  _full((DTW, DI)), _full((DI, DTW)), _full((3 * CH, 2 * CH))],
        out_specs=[pl.BlockSpec((CH, CD + DI), lambda c: (rev(c), 0)),
                   pl.BlockSpec((CH, DTW), lambda c: (rev(c), 0)), _full((8, CD)), _full((8, DTW)), _full((8, DI))],
        out_shape=[jax.ShapeDtypeStruct((T, CD + DI), BF16), jax.ShapeDtypeStruct((T, DTW), BF16),
                   jax.ShapeDtypeStruct((8, CD), F32), jax.ShapeDtypeStruct((8, DTW), F32), jax.ShapeDtypeStruct((8, DI), F32)],
        scratch_shapes=[pltpu.VMEM((NST, DI), F32), pltpu.VMEM((2 * CH, CD), BF16), pltpu.VMEM((CH, CH), F32), pltpu.VMEM((CH, CH), F32),
                        pltpu.VMEM((CH, DI), F32), pltpu.VMEM((CH, DI), F32), pltpu.VMEM((CH, DI), F32), pltpu.VMEM((CH, DI), F32),
                        pltpu.VMEM((CH, DI), F32), pltpu.VMEM((CH, CD), F32), pltpu.VMEM((8, DI), F32)],
        args=(proj, pre_all, proj, proj, dtr, y, dyb, hprev_all, cw, dtb, alog, dskx, sg, e128, et128, _shift_matrix(down=False)))


def _merge_tile():
    return 256


def _merge(x, yb, proj, p, tgt, gmlp, w_oa, w_ob, w_out, w_pg, w_ple, ple_g, fin_g):
    T = x.shape[0]
    tm = min(T, _merge_tile())

    def body(x_ref, u_ref, v_ref, za_ref, yb_ref, ga_ref, gb_ref, p_ref, t_ref, lg_ref, lb_ref, ws_ref, bs_ref,
             woa, wob, wout, wpg, wple, pg_ref, fg_ref,
             dx1_ref, dx1b_ref, ya_ref, mg_ref, hp_ref, dpre_ref, dpe_ref, doa_ref, dob_ref, dya_ref, dyb_ref, dg_ref, acc_ref, vn_s):
        @pl.when(pl.program_id(0) == 0)
        def _():
            acc_ref[...] = jnp.zeros_like(acc_ref)

        _gmlp_fwd_tile(u_ref, v_ref, za_ref, lg_ref, lb_ref, ws_ref, bs_ref, ya_ref, vn_s)
        oa = _dot(ya_ref[...], woa[...])
        ob = _dot(yb_ref[...], wob[...])
        sa = _sigmoid(ga_ref[...].astype(F32))
        sb = _sigmoid(gb_ref[...].astype(F32))
        mg = sa * oa + sb * ob
        mgb = mg.astype(BF16)
        mg_ref[...] = mgb
        x1 = x_ref[...] + _dot(mgb, wout[...])
        r2 = lax.rsqrt(jnp.mean(x1 * x1, axis=-1, keepdims=True) + EPS)
        xh1 = x1 * r2
        hpb = (xh1 * pg_ref[...]).astype(BF16)
        hp_ref[...] = hpb
        gate = _sigmoid(_dot(hpb, wpg[...]))
        pe = _dot(p_ref[...].astype(BF16), wple[...])
        x2 = x1 + gate * pe
        r3 = lax.rsqrt(jnp.mean(x2 * x2, axis=-1, keepdims=True) + EPS)
        xh2 = x2 * r3
        err = xh2 * fg_ref[...] - t_ref[...]
        acc_ref[2:3, :] += 0.5 * jnp.sum(jnp.mean(err * err, axis=-1, keepdims=True))
        dyo = err * (1.0 / D)
        acc_ref[0:1, :] += jnp.sum(dyo * xh2, axis=0, keepdims=True)
        dn = dyo * fg_ref[...]
        dx2 = r3 * (dn - xh2 * jnp.mean(dn * xh2, axis=-1, keepdims=True))
        dpe_ref[...] = (dx2 * gate).astype(BF16)
        dpre = (dx2 * pe * gate * (1.0 - gate)).astype(BF16)
        dpre_ref[...] = dpre
        dhp = _dot_nt(dpre, wpg[...])
        acc_ref[1:2, :] += jnp.sum(dhp * xh1, axis=0, keepdims=True)
        dhn = dhp * pg_ref[...]
        dx1 = dx2 + r2 * (dhn - xh1 * jnp.mean(dhn * xh1, axis=-1, keepdims=True))
        dx1_ref[...] = dx1
        dx1b = dx1.astype(BF16)
        dx1b_ref[...] = dx1b
        dmg = _dot_nt(dx1b, wout[...])
        doa = (dmg * sa).astype(BF16)
        dob = (dmg * sb).astype(BF16)
        doa_ref[...] = doa
        dob_ref[...] = dob
        dg_ref[:, :D] = (dmg * oa * sa * (1.0 - sa)).astype(BF16)
        dg_ref[:, D:] = (dmg * ob * sb * (1.0 - sb)).astype(BF16)
        dya_ref[...] = _dot_nt(doa, woa[...]).astype(BF16)
        dyb_ref[...] = _dot_nt(dob, wob[...]).astype(BF16)

    row = lambda w: pl.BlockSpec((tm, w), lambda i: (i, 0))
    blk = lambda off: pl.BlockSpec((tm, D), lambda i: (i, off // D))
    wsp = lambda s: pl.BlockSpec(s, lambda i: (0, 0), pipeline_mode=pl.Buffered(1))
    return pl.pallas_call(
        body, name="merge", grid=(T // tm,),
        in_specs=[row(D), blk(OFF_U), blk(OFF_V), blk(OFF_ZA), row(DI), blk(OFF_GA), blk(OFF_GB), row(PLE), row(D),
                  _full((1, D)), _full((1, D)), _full((NG, CH, CH)), _full((NG, CH, 256)),
                  wsp((D, D)), wsp((DI, D)), wsp((D, D)), wsp((D, D)), wsp((PLE, D)), _full((1, D)), _full((1, D))],
        out_specs=[row(D)] * 10 + [row(DI), row(2 * D), _full((8, D))],
        out_shape=[jax.ShapeDtypeStruct((T, D), F32)] + [jax.ShapeDtypeStruct((T, D), BF16)] * 9
        + [jax.ShapeDtypeStruct((T, DI), BF16), jax.ShapeDtypeStruct((T, 2 * D), BF16), jax.ShapeDtypeStruct((8, D), F32)],
        scratch_shapes=[pltpu.VMEM((tm, D), BF16)],
        compiler_params=_cp(("arbitrary",)),
    )(x, proj, proj, proj, yb, proj, proj, p, tgt, *gmlp, w_oa, w_ob, w_out, w_pg, w_ple, ple_g, fin_g)


def _wgrad(a, b, name):
    T, K = a.shape
    N = b.shape[1]
    tt, tk, tn = min(T, 2048), min(K, 1024), min(N, 1024)
    nt = T // tt

    def body(a_ref, b_ref, o_ref, acc_s):
        t = pl.program_id(2)

        @pl.when(t == 0)
        def _():
            acc_s[...] = jnp.zeros_like(acc_s)

        acc_s[...] += _dot_tn(a_ref[...].astype(BF16), b_ref[...])

        @pl.when(t == nt - 1)
        def _():
            o_ref[...] = acc_s[...].astype(BF16)

    return pl.pallas_call(
        body, name=name, grid=(K // tk, N // tn, nt),
        in_specs=[pl.BlockSpec((tt, tk), lambda k, n, t: (t, k)), pl.BlockSpec((tt, tn), lambda k, n, t: (t, n))],
        out_specs=pl.BlockSpec((tk, tn), lambda k, n, t: (k, n)),
        out_shape=jax.ShapeDtypeStruct((K, N), BF16),
        scratch_shapes=[pltpu.VMEM((tk, tn), F32)],
        compiler_params=_cp(("parallel", "parallel", "arbitrary")),
    )(a, b)


def _wgrad_into(a, b, name, buf, total_rows, row_of, keep=None):
    T, K = a.shape
    N = b.shape[1]
    assert N == D
    tt, tk = min(T, 2048), min(K, 1024)
    nt, nb = T // tt, K // tk
    keep = tk if keep is None else keep

    def body(*refs):
        a_ref, b_ref = refs[:2]
        o_hbm, acc_s, stage, sems = refs[-4:]
        k, t = pl.program_id(0), pl.program_id(1)
        slot = k % 2

        def store(s):
            return pltpu.make_async_copy(stage.at[s, 0:keep], o_hbm.at[pl.ds(pl.multiple_of(row_of(k), 16), keep)], sems.at[s])

        @pl.when(t == 0)
        def _():
            acc_s[...] = jnp.zeros_like(acc_s)

        acc_s[...] += _dot_tn(a_ref[...].astype(BF16), b_ref[...])

        @pl.when(t == nt - 1)
        def _():
            @pl.when(k >= 2)
            def _():
                store(slot).wait()
            stage[slot] = acc_s[...].astype(BF16)
            store(slot).start()

        @pl.when((t == nt - 1) & (k == nb - 1))
        def _():
            store(slot).wait()
            if nb >= 2:
                store(1 - slot).wait()

    return pl.pallas_call(
        body, name=name, grid=(nb, nt),
        in_specs=[pl.BlockSpec((tt, tk), lambda k, t: (t, k)), pl.BlockSpec((tt, D), lambda k, t: (t, 0))] + ([] if buf is None else [_ANY]),
        out_specs=_ANY,
        out_shape=jax.ShapeDtypeStruct((total_rows, D), BF16),
        scratch_shapes=[pltpu.VMEM((tk, D), F32), pltpu.VMEM((2, tk, D), BF16), pltpu.SemaphoreType.DMA((2,))],
        input_output_aliases={} if buf is None else {2: 0},
        compiler_params=_cp(("arbitrary", "arbitrary")),
    )(*((a, b) if buf is None else (a, b, buf)))


def _proj_bwd(x, dx1, norm_g, wt, pieces, ddt, ex):
    T = x.shape[0]
    tm = min(T, 1024)
    nk = OFF_DT // D + 1
    starts = [sum(a.shape[1] for a in pieces[:n]) // D for n in range(len(pieces))]
    ranges = [(s, s + a.shape[1] // D) for s, a in zip(starts, pieces)]
    assert ranges[-1][1] == nk - 1
    npc = len(pieces)

    def body(x_hbm, dx1_hbm, g_ref, w_ref, wdt_ref, *rest):
        piece_refs, ddt_ref, gx_ref, dng_ref = rest[:npc], rest[npc], rest[npc + 1], rest[npc + 2]
        acc_s, x_ref, dx1_ref, row_sems = rest[npc + 3:]
        i, k = pl.program_id(0), pl.program_id(1)
        rows = pl.ds(pl.multiple_of(i * tm, tm), tm)
        fetches = [pltpu.make_async_copy(x_hbm.at[rows], x_ref, row_sems.at[0]),
                   pltpu.make_async_copy(dx1_hbm.at[rows], dx1_ref, row_sems.at[1])]

        @pl.when((i == 0) & (k == 0))
        def _():
            dng_ref[...] = jnp.zeros_like(dng_ref)

        @pl.when(k == 0)
        def _():
            acc_s[...] = jnp.zeros_like(acc_s)

        @pl.when(k == 1)
        def _():
            for cp in fetches:
                cp.start()

        for ref, (lo, hi) in zip(piece_refs, ranges):
            @pl.when((k >= lo) & (k < hi))
            def _(ref=ref):
                acc_s[...] += _dot(ref[...], w_ref[...])

        @pl.when(k == nk - 1)
        def _():
            for cp in fetches:
                cp.wait()
            dh = acc_s[...] + _dot(ddt_ref[...], wdt_ref[...])
            xf = x_ref[...]
            r = lax.rsqrt(jnp.mean(xf * xf, axis=-1, keepdims=True) + EPS)
            xh = xf * r
            dng_ref[0:1, :] += jnp.sum(dh * xh, axis=0, keepdims=True)
            dxn = dh * g_ref[...]
            gx_ref[...] = dx1_ref[...] + r * (dxn - xh * jnp.mean(dxn * xh, axis=-1, keepdims=True))

    def piece_spec(lo, hi):
        return pl.BlockSpec((tm, D), lambda i, k: (i, jnp.clip(k - lo, 0, hi - lo - 1)))

    row = pl.BlockSpec((tm, D), lambda i, k: (i, 0))
    return _call(
        body, ex, name="proj_bwd", grid=(T // tm, nk),
        in_specs=[_ANY, _ANY, _full((1, D)), pl.BlockSpec((D, D), lambda i, k: (jnp.minimum(k, nk - 2), 0)),
                  pl.BlockSpec((DTW, D), lambda i, k: (OFF_DT // DTW, 0))]
        + [piece_spec(lo, hi) for lo, hi in ranges] + [pl.BlockSpec((tm, DTW), lambda i, k: (i, 0))],
        out_specs=[row, _full((8, D))],
        out_shape=[jax.ShapeDtypeStruct((T, D), F32), jax.ShapeDtypeStruct((8, D), F32)],
        scratch_shapes=[pltpu.VMEM((tm, D), F32), pltpu.VMEM((tm, D), F32), pltpu.VMEM((tm, D), F32), pltpu.SemaphoreType.DMA((2,))],
        args=(x, dx1, norm_g, wt, wt, *pieces, ddt))


def _elementwise_tile(R, C, limit=1 << 20):
    if R * C * 4 <= limit:
        return R, C
    rows = [t for t in range(16, R, 16) if R % t == 0 and t * C * 4 <= limit]
    if rows:
        return rows[-1], C
    cols = [t for t in range(128, C, 128) if C % t == 0 and R * t * 4 <= limit]
    return R, cols[-1]


def _adam_update(w, m, v, g):
    c1 = 1.0 - ADAM_B1 ** ADAM_STEP
    c2 = 1.0 - ADAM_B2 ** ADAM_STEP
    mm = ADAM_B1 * m + (1.0 - ADAM_B1) * g
    vv = ADAM_B2 * v + (1.0 - ADAM_B2) * (g * g)
    return -ADAM_LR * ((mm / c1) / (jnp.sqrt(vv / c2) + ADAM_EPS) + ADAM_WD * w), mm, vv


def _sum_parts(parts, name, ex):
    first = parts[0][0] if isinstance(parts[0], tuple) else parts[0]
    R, C = first.shape[-2:]
    tr, tc = _elementwise_tile(R, C)
    flat = pl.BlockSpec((tr, tc), lambda i, j: (i, j))
    slot = lambda k: pl.BlockSpec((None, tr, tc), lambda i, j: (k, i, j))
    specs = [slot(p[1]) if isinstance(p, tuple) else flat for p in parts]
    arrays = [p[0] if isinstance(p, tuple) else p for p in parts]

    def body(*refs):
        g = refs[0][...].astype(F32)
        for r in refs[1:-1]:
            g = g + r[...].astype(F32)
        refs[-1][...] = g

    (out,), got = _call(body, ex, name=name, grid=(R // tr, C // tc), in_specs=specs, out_specs=[flat],
                        out_shape=[jax.ShapeDtypeStruct((R, C), F32)], scratch_shapes=[], args=arrays)
    return out, got


def _adamw_tiles(w, m, v, g, name):
    R = w.shape[0]
    tr = 107
    assert R % tr == 0
    spec = pl.BlockSpec((tr, 8, 128), lambda i: (i, 0, 0))

    def body(w_ref, m_ref, v_ref, g_in, g_ref, d_ref, nm_ref, nv_ref):
        g = g_in[...]
        g_ref[...] = g
        d_ref[...], nm_ref[...], nv_ref[...] = _adam_update(w_ref[...], m_ref[...], v_ref[...], g)

    return pl.pallas_call(body, name=name, grid=(R // tr,), in_specs=[spec] * 4, out_specs=[spec] * 4,
                          out_shape=[jax.ShapeDtypeStruct(w.shape, F32)] * 4, compiler_params=_cp(("parallel",)))(w, m, v, g)


def _adamw(w, m, v, parts, name, part_row=0, block_rows=None):
    R, C = w.shape
    tr, tc = _elementwise_tile(R, C) if block_rows is None else (block_rows, C)
    assert R % tr == 0 and part_row % tr == 0
    first = part_row // tr
    n = len(parts)
    wspec = pl.BlockSpec((tr, tc), lambda i, j: (i, j))
    flat = pl.BlockSpec((tr, tc), lambda i, j: (first + i, j))
    slot = lambda k: pl.BlockSpec((None, tr, tc), lambda i, j: (k, first + i, j))
    part_specs = [slot(p[1]) if isinstance(p, tuple) else flat for p in parts]
    part_arrays = [p[0] if isinstance(p, tuple) else p for p in parts]

    def body(*refs):
        w_ref, m_ref, v_ref = refs[:3]
        g_ref, d_ref, nm_ref, nv_ref = refs[3 + n:]
        g = refs[3][...].astype(F32)
        for r in refs[4:3 + n]:
            g = g + r[...].astype(F32)
        g_ref[...] = g
        d_ref[...], nm_ref[...], nv_ref[...] = _adam_update(w_ref[...], m_ref[...], v_ref[...], g)

    return pl.pallas_call(
        body, name=name, grid=(R // tr, C // tc), in_specs=[wspec] * 3 + part_specs, out_specs=[wspec] * 4,
        out_shape=[jax.ShapeDtypeStruct(w.shape, F32)] * 4, compiler_params=_cp(("parallel", "parallel")),
    )(w, m, v, *part_arrays)


_SMALL_WIDE = {"ln_a_g": 0, "ln_a_b": 1, "final_g": 8, "ple_norm_g": 9, "ssm_norm_g": 16, "conv_b": 44}
_WIDE_CONV_W, _WIDE_ROWS = 32, 56
_WIDE_LOSS = 10
_SMALL_NARROW = {"w_s": (0, NG * CH, 128), "b_s": (512, NG, 128), "dt_bias": (520, 1, NH), "a_log": (521, 1, NH),
                 "d_skip": (522, 1, NH)}
_NARROW_ROWS = 528
_SMALL_SHAPES = {"norm_g": (1, D), "ln_a_g": (1, D), "ln_a_b": (1, D), "ple_norm_g": (1, D), "final_g": (1, D),
                 "ssm_norm_g": (1, DI), "conv_b": (1, CD), "w_s": (NG * CH, CH), "b_s": (NG, CH), "dt_bias": (1, NH),
                 "a_log": (1, NH), "d_skip": (1, NH)}


def _adamw_small(w, m, v, wide_all, narrow_all, norm_all):
    names = list(_SMALL_SHAPES)
    n = len(names)

    def body(*refs):
        wr, mr, vr = refs[:n], refs[n:2 * n], refs[2 * n:3 * n]
        wide_ref, narrow_ref, norm_ref = refs[3 * n:3 * n + 3]
        outs = refs[3 * n + 3:]
        gr, dr, nmr, nvr, cw_ref, loss_ref = outs[:n], outs[n:2 * n], outs[2 * n:3 * n], outs[3 * n:4 * n], outs[4 * n], outs[4 * n + 1]

        def total(ref, rows, lanes):
            acc = ref[0, rows, lanes]
            for d in range(1, NDEV):
                acc = acc + ref[d, rows, lanes]
            return acc

        for k, name in enumerate(names):
            if name in _SMALL_WIDE or name == "norm_g":
                for part in range(_SMALL_SHAPES[name][1] // D):
                    pack, r = (norm_ref, 0) if name == "norm_g" else (wide_ref, _SMALL_WIDE[name] + part)
                    cols = slice(part * D, (part + 1) * D)
                    g = total(pack, slice(r, r + 1), slice(None))
                    gr[k][:, cols] = g
                    dr[k][:, cols], nmr[k][:, cols], nvr[k][:, cols] = _adam_update(wr[k][:, cols], mr[k][:, cols], vr[k][:, cols], g)
            else:
                r, rows, lanes = _SMALL_NARROW[name]
                g = total(narrow_ref, slice(r, r + rows), slice(0, lanes))
                gr[k][...] = g
                dr[k][...], nmr[k][...], nvr[k][...] = _adam_update(wr[k][...], mr[k][...], vr[k][...], g)
        cw_ref[...] = total(wide_ref, slice(_WIDE_CONV_W, _WIDE_CONV_W + 12), slice(None))
        loss_ref[...] = total(wide_ref, slice(_WIDE_LOSS, _WIDE_LOSS + 1), slice(None))

    shapes = [jax.ShapeDtypeStruct(_SMALL_SHAPES[k], F32) for k in names]
    specs = [_full(_SMALL_SHAPES[k]) for k in names]
    res = pl.pallas_call(
        body, name="adamw_small", grid=(1,),
        in_specs=specs * 3 + [_full(wide_all.shape), _full(narrow_all.shape), _full(norm_all.shape)],
        out_specs=specs * 4 + [_full((12, D)), _full((1, D))],
        out_shape=shapes * 4 + [jax.ShapeDtypeStruct((12, D), F32), jax.ShapeDtypeStruct((1, D), F32)],
        compiler_params=_cp(("arbitrary",)),
    )(*[w[k] for k in names], *[m[k] for k in names], *[v[k] for k in names], wide_all, narrow_all, norm_all)
    groups = [dict(zip(names, res[q * n:(q + 1) * n])) for q in range(4)]
    return groups[0], groups[1], groups[2], groups[3], res[4 * n], res[4 * n + 1][0, 0]


def _dev_index(px, py, pc):
    return 4 * px + 2 * py + pc


def _mesh_position():
    return lax.axis_index("x"), lax.axis_index("y"), lax.axis_index("c")


def _gather_exchange(blocks):
    n = len(blocks)

    def make(ins, outs, sems):
        send_sems, recv_sems, local_sems = sems
        x, y, c = _mesh_position()
        me, sibling = (x, y, c), (x, y, 1 - c)
        chips = [(1 - x, y), (x, 1 - y), (1 - x, 1 - y)]

        def copy(a, k, block, to, src=None):
            dst = outs[a].at[_dev_index(*block)]
            return pltpu.make_async_remote_copy(src_ref=dst if src is None else src, dst_ref=dst, send_sem=send_sems.at[a, k],
                                                recv_sem=recv_sems.at[a, k], device_id=to, device_id_type=MESH)

        mine = [pltpu.make_async_copy(ins[a], outs[a].at[_dev_index(*me)], local_sems.at[a]) for a in range(n)]
        first = []
        for a in range(n):
            first.append(copy(a, 0, me, sibling, src=ins[a]))
            first += [copy(a, 1 + j, me, (*chip, c), src=ins[a]) for j, chip in enumerate(chips)]

        def start():
            for cp in mine + first:
                cp.start()

        def finish():
            passed = []
            for j, chip in enumerate(chips):
                for a in range(n):
                    copy(a, 1 + j, (*chip, c), me).wait_recv()
                    fwd = copy(a, 4 + j, (*chip, c), sibling)
                    fwd.start()
                    passed.append(fwd)
            for a in range(n):
                copy(a, 0, sibling, me).wait_recv()
                for j, chip in enumerate(chips):
                    copy(a, 4 + j, (*chip, 1 - c), me).wait_recv()
            for cp in first + passed:
                cp.wait_send()
            for cp in mine:
                cp.wait()

        return start, finish

    return _Exchange(list(blocks), [jax.ShapeDtypeStruct((NDEV,) + b.shape, b.dtype) for b in blocks],
                     [pltpu.SemaphoreType.DMA((n, 7)), pltpu.SemaphoreType.DMA((n, 7)), pltpu.SemaphoreType.DMA((n,))], make)


def _relay_gather_exchange(blocks):
    n = len(blocks)

    def make(ins, outs, sems):
        send_sems, recv_sems, local_sems = sems
        x, y, c = _mesh_position()
        me, sibling = (x, y, c), (x, y, 1 - c)
        x_nbr, y_nbr, diag = (1 - x, y), (x, 1 - y), (1 - x, 1 - y)
        relay_from = (jnp.where(c == 0, x, 1 - x), jnp.where(c == 0, 1 - y, y))
        relay_to = (jnp.where(c == 0, 1 - x, x), jnp.where(c == 0, y, 1 - y))

        def copy(a, k, block, to, src=None):
            dst = outs[a].at[_dev_index(*block)]
            return pltpu.make_async_remote_copy(src_ref=dst if src is None else src, dst_ref=dst, send_sem=send_sems.at[a, k],
                                                recv_sem=recv_sems.at[a, k], device_id=to, device_id_type=MESH)

        mine = [pltpu.make_async_copy(ins[a], outs[a].at[_dev_index(*me)], local_sems.at[a]) for a in range(n)]
        first = []
        for a in range(n):
            first += [copy(a, 0, me, sibling, src=ins[a]), copy(a, 1, me, (*x_nbr, c), src=ins[a]), copy(a, 2, me, (*y_nbr, c), src=ins[a])]

        def start():
            for cp in mine + first:
                cp.start()

        def finish():
            later = []
            for a in range(n):
                copy(a, 1, (*x_nbr, c), me).wait_recv()
                copy(a, 2, (*y_nbr, c), me).wait_recv()
                later.append(copy(a, 3, (*relay_from, c), (*relay_to, c)))
                later += [copy(a, 4, (*x_nbr, c), sibling), copy(a, 5, (*y_nbr, c), sibling)]
                for cp in later[-3:]:
                    cp.start()
            for a in range(n):
                copy(a, 3, (*diag, c), me).wait_recv()
                later.append(copy(a, 6, (*diag, c), sibling))
                later[-1].start()
            for a in range(n):
                copy(a, 0, sibling, me).wait_recv()
                for k, chip in ((4, x_nbr), (5, y_nbr), (6, diag)):
                    copy(a, k, (*chip, 1 - c), me).wait_recv()
            for cp in first + later:
                cp.wait_send()
            for cp in mine:
                cp.wait()

        return start, finish

    return _Exchange(list(blocks), [jax.ShapeDtypeStruct((NDEV,) + b.shape, b.dtype) for b in blocks],
                     [pltpu.SemaphoreType.DMA((n, 7)), pltpu.SemaphoreType.DMA((n, 7)), pltpu.SemaphoreType.DMA((n,))], make)


def _combine(*exchanges):
    def make(ins, outs, sems):
        pairs = []
        for e in exchanges:
            ni, no, ns = len(e.arrays), len(e.out_shape), len(e.sems)
            pairs.append(e.make(ins[:ni], outs[:no], sems[:ns]))
            ins, outs, sems = ins[ni:], outs[no:], sems[ns:]

        def start():
            for s, _ in pairs:
                s()

        def finish():
            for _, f in pairs:
                f()

        return start, finish

    return _Exchange(sum((list(e.arrays) for e in exchanges), []), sum((list(e.out_shape) for e in exchanges), []),
                     sum((list(e.sems) for e in exchanges), []), make)


def _start_wait_all(copies, local=()):
    def start():
        for cp in list(local) + list(copies):
            cp.start()

    def finish():
        for cp in copies:
            cp.wait()
        for cp in local:
            cp.wait()

    return start, finish


def _no_exchange():
    return _Exchange([], [], [], lambda ins, outs, sems: (lambda: None, lambda: None))


def _direct_exchange(grads):
    n = len(grads)

    def make(ins, outs, sems):
        send_sems, recv_sems = sems
        x, y, c = _mesh_position()
        copies = []
        for a in range(n):
            for r in range(1, NDEV):
                peer = (x ^ ((r >> 2) & 1), y ^ ((r >> 1) & 1), c ^ (r & 1))
                copies.append(pltpu.make_async_remote_copy(
                    src_ref=ins[a].at[_dev_index(*peer)], dst_ref=outs[a].at[r - 1], send_sem=send_sems.at[a, r - 1],
                    recv_sem=recv_sems.at[a, r - 1], device_id=peer, device_id_type=MESH))
        return _start_wait_all(copies)

    return _Exchange(list(grads), [jax.ShapeDtypeStruct((NDEV - 1,) + g.shape[1:], g.dtype) for g in grads],
                     [pltpu.SemaphoreType.DMA((n, NDEV - 1)), pltpu.SemaphoreType.DMA((n, NDEV - 1))], make)


def _direct_gather_exchange(smalls):
    n = len(smalls)

    def make(ins, outs, sems):
        send_sems, recv_sems, local_sems = sems
        x, y, c = _mesh_position()
        copies, local = [], []
        for a in range(n):
            slot = outs[a].at[_dev_index(x, y, c)]
            local.append(pltpu.make_async_copy(ins[a], slot, local_sems.at[a]))
            for r in range(1, NDEV):
                peer = (x ^ ((r >> 2) & 1), y ^ ((r >> 1) & 1), c ^ (r & 1))
                copies.append(pltpu.make_async_remote_copy(src_ref=ins[a], dst_ref=slot, send_sem=send_sems.at[a, r - 1],
                                                           recv_sem=recv_sems.at[a, r - 1], device_id=peer, device_id_type=MESH))
        return _start_wait_all(copies, local=local)

    return _Exchange(list(smalls), [jax.ShapeDtypeStruct((NDEV,) + s.shape, s.dtype) for s in smalls],
                     [pltpu.SemaphoreType.DMA((n, 7)), pltpu.SemaphoreType.DMA((n, 7)), pltpu.SemaphoreType.DMA((n,))], make)


_W_IN_ROWS = {"u": (0, 1024), "v": (1024, 2048), "za": (2048, 3072), "zb": (3072, 5120), "xbc": (5120, 8192),
              "dt": (8192, 8224), "ga": (8224, 9248), "gb": (9248, 10272)}
_PROJ_ORDER = ("xbc", "zb", "u", "v", "za", "ga", "gb", "dt")


def _w_in_t_rows(wt):
    z = jnp.zeros((NP - NIN, wt.shape[1]), wt.dtype)
    return jnp.concatenate([wt[slice(*_W_IN_ROWS[n])] for n in _PROJ_ORDER] + [z], axis=0)


_WEIGHTS = ["norm_g", "w_in", "ln_a_g", "ln_a_b", "w_s", "b_s", "conv_w", "conv_b", "dt_bias", "a_log", "d_skip", "ssm_norm_g",
            "w_oa", "w_ob", "w_out", "ple_norm_g", "w_pg", "w_ple", "final_g"]


def _rows_pack(d, dtype):
    return jnp.concatenate([d["w_oa"].reshape(128, D), d["w_ob"].reshape(256, D), d["w_out"].reshape(128, D),
                            d["w_pg"].reshape(128, D), d["w_ple"].reshape(32, D)], axis=0).astype(dtype)


def kernel(x, p, norm_g, w_in, ln_a_g, ln_a_b, w_s, b_s, conv_w, conv_b, dt_bias, a_log, d_skip, ssm_norm_g, w_oa, w_ob, w_out, ple_norm_g, w_pg, w_ple, final_g, loss_target, m_norm_g, m_w_in, m_ln_a_g, m_ln_a_b, m_w_s, m_b_s, m_conv_w, m_conv_b, m_dt_bias, m_a_log, m_d_skip, m_ssm_norm_g, m_w_oa, m_w_ob, m_w_out, m_ple_norm_g, m_w_pg, m_w_ple, m_final_g, v_norm_g, v_w_in, v_ln_a_g, v_ln_a_b, v_w_s, v_b_s, v_conv_w, v_conv_b, v_dt_bias, v_a_log, v_d_skip, v_ssm_norm_g, v_w_oa, v_w_ob, v_w_out, v_ple_norm_g, v_w_pg, v_w_ple, v_final_g):
    args = dict(locals())
    w = {n: args[n] for n in _WEIGHTS}
    m = {n: args["m_" + n] for n in _WEIGHTS}
    v = {n: args["v_" + n] for n in _WEIGHTS}
    T = x.shape[1]
    xi, yi, ci = lax.axis_index("x"), lax.axis_index("y"), lax.axis_index("c")
    me = 4 * xi + 2 * yi + ci
    x2, p2, tgt = x.reshape(T, D), p.reshape(T, PLE), loss_target.reshape(T, D)

    norm_g2 = w["norm_g"].reshape(1, D)
    ws = jnp.where(jnp.tril(jnp.ones((CH, CH), bool))[None], w["w_s"].reshape(NG, CH, CH), 0.0).astype(BF16)
    wst = jnp.transpose(ws, (0, 2, 1))
    bst = jnp.broadcast_to(w["b_s"].reshape(NG, CH, 1), (NG, CH, 256))
    ln_g, ln_b = w["ln_a_g"].reshape(1, D), w["ln_a_b"].reshape(1, D)
    cb = w["conv_b"].reshape(1, CD)
    pad32 = lambda a: jnp.pad(a.reshape(1, NH), ((0, 0), (0, DTW - NH)))
    dtb, alog = pad32(w["dt_bias"]), pad32(w["a_log"])
    dskx = jnp.repeat(w["d_skip"].reshape(NH), HD).reshape(1, DI)
    sg = w["ssm_norm_g"].reshape(1, DI)
    ple_g, fin_g = w["ple_norm_g"].reshape(1, D), w["final_g"].reshape(1, D)
    e128 = (jnp.arange(DTW)[:, None] == (jnp.arange(DI)[None, :] // HD)).astype(BF16)
    et128 = e128.T
    gsel = ((jnp.arange(D)[:, None] // 256) == jnp.arange(128)[None, :]).astype(BF16)

    w_in_t = lambda a: jnp.transpose(a.reshape(D, WSH))
    (a_all,) = _run_exchange(_relay_gather_exchange([w_in_t(w["w_in"]).astype(BF16)]), "all_gather_w_in")
    w_in_full_t = a_all.reshape(NIN, D)
    wp = _w_in_t_rows(w_in_full_t)
    (proj, dtr, h), (r_all, cw_all) = _proj_fwd(x2, norm_g2, wp, _gather_exchange([_rows_pack(w, BF16), w["conv_w"].reshape(4, CD // NDEV)]))
    f_oa = r_all[:, R_OA:R_OB].reshape(D, D)
    f_ob = r_all[:, R_OB:R_OUT].reshape(DI, D)
    f_out = r_all[:, R_OUT:R_PG].reshape(D, D)
    f_pg = r_all[:, R_PG:R_PLE].reshape(D, D)
    f_ple = jnp.transpose(r_all[:, R_PLE:R_ROWS].reshape(NDEV, PLE, 128), (1, 0, 2)).reshape(PLE, D)
    cw = jnp.transpose(cw_all, (1, 0, 2)).reshape(4, CD)

    y, yb, hprev, pre_all = _ssd_fwd(proj, dtr, cw, cb, dtb, alog, dskx, sg, e128)
    dx1, dx1b, ya, mg, hp, dpre, dpe, doa, dob, dya, dyb, dgab, acc = _merge(
        x2, yb, proj, p2, tgt, (ln_g, ln_b, ws, bst), f_oa, f_ob, f_out, f_pg, f_ple, ple_g, fin_g)

    gple = jnp.transpose(_wgrad(p2, dpe, "wgrad_ple").reshape(PLE, NDEV, 128), (1, 0, 2)).reshape(NDEV, 32, D)
    gr = jnp.concatenate([_wgrad(ya, doa, "wgrad_oa").reshape(NDEV, 128, D), _wgrad(yb, dob, "wgrad_ob").reshape(NDEV, 256, D),
                          _wgrad(mg, dx1b, "wgrad_out").reshape(NDEV, 128, D), _wgrad(hp, dpre, "wgrad_pg").reshape(NDEV, 128, D),
                          gple], axis=1)
    (duvz, dws, dbs, dln), _ = _gmlp_bwd(proj, dya, ln_g, ln_b, ws, wst, bst, gsel, _no_exchange())
    (dxz, ddt, dcw, dsm, dsg), (rr,) = _ssd_bwd(proj, pre_all, dtr, y, dyb, hprev, cw, dtb, alog, dskx, sg, e128, et128, _direct_exchange([gr]))

    nx = CD // 1024
    g_w_in_t = _wgrad_into(duvz, h, "wgrad_u_v_za", None, NIN, lambda k: k * 1024)
    g_w_in_t = _wgrad_into(dxz, h, "wgrad_xbc_zb", g_w_in_t, NIN,
                           lambda k: jnp.where(k < nx, 3072 + 2048 + k * 1024, 3072 + (k - nx) * 1024))
    g_w_in_t = _wgrad_into(ddt, h, "wgrad_dt", g_w_in_t, NIN, lambda k: 8192, keep=NH)
    g_w_in_t = _wgrad_into(dgab, h, "wgrad_ga_gb", g_w_in_t, NIN, lambda k: 8192 + NH + k * 1024)
    ga = g_w_in_t.reshape(NDEV, WSH, D)
    wide = jnp.concatenate([dln, acc, dsg.reshape(16, D), dcw.reshape(24, D)], axis=0)
    narrow = jnp.concatenate([dws.reshape(NG * CH, CH), jnp.pad(dbs[:, :NG].T, ((0, 8 - NG), (0, 0))), dsm], axis=0)
    (gx, dng), (ra, wide_all, narrow_all) = _proj_bwd(x2, dx1, norm_g2, wp, [dxz, duvz, dgab], ddt,
                                                      _combine(_direct_exchange([ga]), _direct_gather_exchange([wide, narrow])))
    ga_own = lax.dynamic_index_in_dim(ga, me, 0, keepdims=False)
    gr_own = lax.dynamic_index_in_dim(gr, me, 0, keepdims=False)

    out_g, out_d, out_m, out_v = {}, {}, {}, {}
    outs = (out_g, out_d, out_m, out_v)
    tiles = lambda a: jnp.transpose(a.reshape(8, 128, WSH), (2, 0, 1))
    g2, (norm_all,) = _sum_parts([ga_own] + [(ra, k) for k in range(NDEV - 1)], "sum_w_in", _direct_gather_exchange([dng]))
    res = _adamw_tiles(tiles(w["w_in"]), tiles(m["w_in"]), tiles(v["w_in"]), g2.reshape(WSH, 8, 128), "adamw_w_in")
    for dst, val in zip(outs, res):
        dst["w_in"] = jnp.transpose(val, (1, 2, 0)).reshape(1, D, WSH)
    parts_r = [gr_own] + [(rr, k) for k in range(NDEV - 1)]
    for name, row, rows in (("w_oa", R_OA, 128), ("w_ob", R_OB, 256), ("w_out", R_OUT, 128), ("w_pg", R_PG, 128)):
        res = _adamw(w[name].reshape(rows, D), m[name].reshape(rows, D), v[name].reshape(rows, D), parts_r, "adamw_" + name,
                     part_row=row, block_rows=128)
        for dst, val in zip(outs, res):
            dst[name] = val.reshape(1, rows, D)
    res = _adamw(w["w_ple"].reshape(32, D), m["w_ple"].reshape(32, D), v["w_ple"].reshape(32, D), parts_r, "adamw_w_ple",
                 part_row=R_PLE, block_rows=32)
    for dst, val in zip(outs, res):
        dst["w_ple"] = val.reshape(1, PLE, 128)
    two_d = lambda d: {n: d[n].reshape(_SMALL_SHAPES[n]) for n in _SMALL_SHAPES}
    *res, g_cw_wide, loss = _adamw_small(two_d(w), two_d(m), two_d(v), wide_all, narrow_all, norm_all)
    for dst, val in zip(outs, res):
        dst.update({n: val[n].reshape(w[n].shape) for n in _SMALL_SHAPES})
    g_cw = lax.dynamic_slice_in_dim(g_cw_wide.reshape(4, CD), me * (CD // NDEV), CD // NDEV, axis=1).reshape(12, 128)
    res = _adamw(w["conv_w"].reshape(12, 128), m["conv_w"].reshape(12, 128), v["conv_w"].reshape(12, 128), [g_cw], "adamw_conv_w")
    for dst, val in zip((out_g, out_d, out_m, out_v), res):
        dst["conv_w"] = val.reshape(1, 4, CD // NDEV)

    return (loss, gx.reshape(1, T, D), *[out_g[n] for n in _WEIGHTS], *[out_d[n] for n in _WEIGHTS],
            *[out_m[n] for n in _WEIGHTS], *[out_v[n] for n in _WEIGHTS])
```

```python
import functools
import math
from typing import Callable, NamedTuple

import jax
import jax.numpy as jnp
from jax import lax
from jax.experimental import pallas as pl
from jax.experimental.pallas import tpu as pltpu

F32 = jnp.float32
BF16 = jnp.bfloat16
MESH = pl.DeviceIdType.MESH

D = 1024
DI = 2048
CD = 3072
NH = 32
HD = 64
NST = 128
NG = 4
CH = 128
PLE = 256
NIN = 10272
NDEV = 8
WSH = NIN // NDEV
EPS = 1e-6
OFF_XBC, OFF_ZB, OFF_U, OFF_V, OFF_ZA, OFF_GA, OFF_GB, OFF_DT = 0, 3072, 5120, 6144, 7168, 8192, 9216, 10240
NP = 10368
DTW = 128
R_OA, R_OB, R_OUT, R_PG, R_PLE, R_ROWS = 0, 128, 384, 512, 640, 672

ADAM_LR, ADAM_B1, ADAM_B2, ADAM_EPS, ADAM_WD, ADAM_STEP = 0.001, 0.9, 0.999, 1e-08, 0.01, 10

V7X_VMEM_LIMIT = 56 * 1024 * 1024


def _cp(sem=None):
    return pltpu.CompilerParams(dimension_semantics=sem, vmem_limit_bytes=V7X_VMEM_LIMIT)


def _dot(a, b, prec=None):
    return jnp.dot(a, b, preferred_element_type=F32, precision=prec)


def _dot_nt(a, b, prec=None):
    return lax.dot_general(a, b, (((1,), (1,)), ((), ())), preferred_element_type=F32, precision=prec)


def _dot_tn(a, b, prec=None):
    return lax.dot_general(a, b, (((0,), (0,)), ((), ())), preferred_element_type=F32, precision=prec)


def _sigmoid(x):
    return 0.5 * jnp.tanh(0.5 * x) + 0.5


_GELU_C = math.sqrt(2.0 / math.pi)
_GELU_K = 0.044715


def _gelu_and_grad(x):
    x2 = x * x
    t = jnp.tanh(x * (_GELU_C + (_GELU_C * _GELU_K) * x2))
    hx = 0.5 * x
    g = hx + hx * t
    dg = (0.5 + 0.5 * t) + hx * (1.0 - t * t) * (_GELU_C + (3.0 * _GELU_C * _GELU_K) * x2)
    return g, dg


def _gelu(x):
    hx = 0.5 * x
    return hx + hx * jnp.tanh(x * (_GELU_C + (_GELU_C * _GELU_K) * (x * x)))


def _softplus(x):
    return jnp.maximum(x, 0.0) + jnp.log(1.0 + jnp.exp(-jnp.abs(x)))


def _full(shape):
    n = len(shape)
    return pl.BlockSpec(shape, lambda *_: (0,) * n)


_ANY = pl.BlockSpec(memory_space=pl.ANY)


class _Exchange(NamedTuple):
    arrays: list
    out_shape: list
    sems: list
    make: Callable


def _call(body, ex, *, name, grid, in_specs, out_specs, out_shape, scratch_shapes, args):
    ki, ko, ks = len(in_specs), len(out_specs), len(scratch_shapes)
    ei, eo = len(ex.arrays), len(ex.out_shape)
    last = [g - 1 for g in grid]

    def full_body(*refs):
        r = list(refs)
        ins, eins, r = r[:ki], r[ki:ki + ei], r[ki + ei:]
        outs, eouts, r = r[:ko], r[ko:ko + eo], r[ko + eo:]
        scr, esems = r[:ks], r[ks:]
        start, finish = ex.make(eins, eouts, esems)
        ids = [pl.program_id(a) for a in range(len(grid))]
        is_first = functools.reduce(lambda p, q: p & q, [i == 0 for i in ids])
        is_last = functools.reduce(lambda p, q: p & q, [i == l for i, l in zip(ids, last)])
        pl.when(is_first)(start)
        body(*ins, *outs, *scr)
        pl.when(is_last)(finish)

    res = pl.pallas_call(
        full_body, name=name, grid=grid, in_specs=list(in_specs) + [_ANY] * ei, out_specs=list(out_specs) + [_ANY] * eo,
        out_shape=list(out_shape) + list(ex.out_shape), scratch_shapes=list(scratch_shapes) + list(ex.sems),
        compiler_params=_cp(("arbitrary",) * len(grid)),
    )(*args, *ex.arrays)
    return res[:ko], res[ko:]


def _run_exchange(ex, name):
    ni, no = len(ex.arrays), len(ex.out_shape)

    def body(*refs):
        start, finish = ex.make(refs[:ni], refs[ni:ni + no], refs[ni + no:])
        start()
        finish()

    return pl.pallas_call(body, name=name, in_specs=[_ANY] * ni, out_specs=[_ANY] * no, out_shape=list(ex.out_shape),
                          scratch_shapes=list(ex.sems))(*ex.arrays)


def _proj_fwd(x, norm_g, wp, ex):
    T = x.shape[0]
    tm, tn = min(T, 1024), 2048
    nj = OFF_DT // tn
    assert OFF_DT % tn == 0 and OFF_DT + DTW == NP

    def body(x_ref, g_ref, w_ref, wdt_ref, proj_ref, dt_ref, h_ref, hs_ref):
        j = pl.program_id(1)

        @pl.when(j == 0)
        def _():
            xf = x_ref[...]
            r = lax.rsqrt(jnp.mean(xf * xf, axis=-1, keepdims=True) + EPS)
            h = (xf * r * g_ref[...]).astype(BF16)
            hs_ref[...] = h
            h_ref[...] = h

        proj_ref[...] = _dot_nt(hs_ref[...], w_ref[...]).astype(BF16)

        @pl.when(j == nj - 1)
        def _():
            dt_ref[...] = _dot_nt(hs_ref[...], wdt_ref[...])

    return _call(
        body, ex, name="proj_fwd", grid=(T // tm, nj),
        in_specs=[pl.BlockSpec((tm, D), lambda i, j: (i, 0)), _full((1, D)), pl.BlockSpec((tn, D), lambda i, j: (j, 0)),
                  pl.BlockSpec((DTW, D), lambda i, j: (OFF_DT // DTW, 0))],
        out_specs=[pl.BlockSpec((tm, tn), lambda i, j: (i, j)), pl.BlockSpec((tm, DTW), lambda i, j: (i, 0)),
                   pl.BlockSpec((tm, D), lambda i, j: (i, 0))],
        out_shape=[jax.ShapeDtypeStruct((T, OFF_DT), BF16), jax.ShapeDtypeStruct((T, DTW), F32), jax.ShapeDtypeStruct((T, D), BF16)],
        scratch_shapes=[pltpu.VMEM((tm, D), BF16)], args=(x, norm_g, wp, wp))


def _gmlp_tile():
    return 512


def _gmlp_fwd_tile(u_ref, v_ref, z_ref, lg_ref, lb_ref, ws_ref, bs_ref, ya_ref, vn_s):
    tm = u_ref.shape[0]
    vg = _gelu(v_ref[...].astype(F32))
    mu = jnp.mean(vg, axis=-1, keepdims=True)
    xc = vg - mu
    rstd = lax.rsqrt(jnp.mean(xc * xc, axis=-1, keepdims=True) + EPS)
    vn_s[...] = (xc * rstd * lg_ref[...] + lb_ref[...]).astype(BF16)
    for c in range(tm // CH):
        rs = slice(c * CH, (c + 1) * CH)
        for g in range(NG):
            cs_ = slice(g * 256, (g + 1) * 256)
            sv = _dot(ws_ref[g], vn_s[rs, cs_]) + bs_ref[g]
            z = z_ref[rs, cs_].astype(F32)
            ya_ref[rs, cs_] = (_gelu(u_ref[rs, cs_].astype(F32)) * sv * (z * _sigmoid(z))).astype(BF16)


def _gmlp_bwd(proj, dya, ln_g, ln_b, ws, wst, bst, gsel, ex):
    T = proj.shape[0]
    tm = min(T, _gmlp_tile())

    def body(u_ref, v_ref, z_ref, dy_ref, lg_ref, lb_ref, ws_ref, wst_ref, bs_ref, gsel_ref,
             d_ref, dws_ref, dbs_ref, dln_ref, vn_s, dsv_s, dvn_s):
        du_ref, dv_ref, dz_ref = d_ref.at[:, 0:D], d_ref.at[:, D:2 * D], d_ref.at[:, 2 * D:3 * D]
        @pl.when(pl.program_id(0) == 0)
        def _():
            dws_ref[...] = jnp.zeros_like(dws_ref)
            dbs_ref[...] = jnp.zeros_like(dbs_ref)
            dln_ref[...] = jnp.zeros_like(dln_ref)

        vg, dvg_dv = _gelu_and_grad(v_ref[...].astype(F32))
        mu = jnp.mean(vg, axis=-1, keepdims=True)
        xc = vg - mu
        rstd = lax.rsqrt(jnp.mean(xc * xc, axis=-1, keepdims=True) + EPS)
        vhat = xc * rstd
        vn_s[...] = (vhat * lg_ref[...] + lb_ref[...]).astype(BF16)
        ri = lax.broadcasted_iota(jnp.int32, (CH, CH), 0)
        ci = lax.broadcasted_iota(jnp.int32, (CH, CH), 1)
        tril = (ri >= ci).astype(F32)
        for c in range(tm // CH):
            rs = slice(c * CH, (c + 1) * CH)
            for g in range(NG):
                cs_ = slice(g * 256, (g + 1) * 256)
                vn = vn_s[rs, cs_]
                sv = _dot(ws_ref[g], vn) + bs_ref[g]
                z = z_ref[rs, cs_].astype(F32)
                sz = _sigmoid(z)
                ug, dug_du = _gelu_and_grad(u_ref[rs, cs_].astype(F32))
                dy = dy_ref[rs, cs_].astype(F32)
                zs = z * sz
                t = dy * zs
                dsv_f = dy * sv
                du_ref[rs, cs_] = (zs * dsv_f * dug_du).astype(BF16)
                dz_ref[rs, cs_] = (dsv_f * ug * (sz + zs * (1.0 - sz))).astype(BF16)
                dsv = (t * ug).astype(BF16)
                dsv_s[rs, cs_] = dsv
                dvn_s[rs, cs_] = _dot(wst_ref[g], dsv)
                dws_ref[g] += _dot_nt(dsv, vn) * tril
            dbs_ref[...] += _dot(dsv_s[rs, :], gsel_ref[...])
        dvn = dvn_s[...]
        dln_ref[0:1, :] += jnp.sum(dvn * vhat, axis=0, keepdims=True)
        dln_ref[1:2, :] += jnp.sum(dvn, axis=0, keepdims=True)
        dvh = dvn * lg_ref[...]
        dvg = rstd * (dvh - jnp.mean(dvh, axis=-1, keepdims=True) - vhat * jnp.mean(dvh * vhat, axis=-1, keepdims=True))
        dv_ref[...] = (dvg * dvg_dv).astype(BF16)

    blk = lambda off: pl.BlockSpec((tm, D), lambda i: (i, off // D))
    row = pl.BlockSpec((tm, D), lambda i: (i, 0))
    return _call(
        body, ex, name="gmlp_bwd", grid=(T // tm,),
        in_specs=[blk(OFF_U), blk(OFF_V), blk(OFF_ZA), row, _full((1, D)), _full((1, D)), _full((NG, CH, CH)),
                  _full((NG, CH, CH)), _full((NG, CH, 256)), _full((D, 128))],
        out_specs=[pl.BlockSpec((tm, 3 * D), lambda i: (i, 0)), _full((NG, CH, CH)), _full((CH, 128)), _full((8, D))],
        out_shape=[jax.ShapeDtypeStruct((T, 3 * D), BF16),
                   jax.ShapeDtypeStruct((NG, CH, CH), F32), jax.ShapeDtypeStruct((CH, 128), F32), jax.ShapeDtypeStruct((8, D), F32)],
        scratch_shapes=[pltpu.VMEM((tm, D), BF16), pltpu.VMEM((tm, D), BF16), pltpu.VMEM((tm, D), F32)],
        args=(proj, proj, proj, dya, ln_g, ln_b, ws, wst, bst, gsel))


def _zb_cols(zb0_ref, zb1_ref, cols):
    ref = zb0_ref if cols.start < D else zb1_ref
    return ref[:, cols.start % D:cols.start % D + (cols.stop - cols.start)]


def _shift_matrix(down):
    t = jnp.arange(CH)[None, :, None]
    j = jnp.arange(1, 4)[:, None, None]
    col = jnp.arange(2 * CH)[None, None, :]
    src = CH + t - j if down else t + j
    return (col == src).astype(BF16).reshape(3 * CH, 2 * CH)


def _conv_pre(x, moved, cw_ref, cb_ref):
    pre = cb_ref[...] + cw_ref[3:4, :] * x
    for j in (1, 2, 3):
        pre = pre + cw_ref[3 - j:4 - j, :] * moved[(j - 1) * CH:j * CH]
    return pre


def _split_dot(x, w, parts, w_left=False):
    acc, r = None, x
    for k in range(parts):
        hi = r.astype(BF16)
        d = _dot(w, hi) if w_left else _dot(hi, w)
        acc = d if acc is None else acc + d
        if k + 1 < parts:
            r = r - hi.astype(F32)
    return acc


def _chunk_decays(dt, alog_ref, e_ref, cs_s, cst_s, csx_s):
    a = -jnp.exp(alog_ref[...])
    ri = lax.broadcasted_iota(jnp.int32, (CH, CH), 0)
    ci = lax.broadcasted_iota(jnp.int32, (CH, CH), 1)
    tril = ri >= ci
    cs = _split_dot(dt * a, tril.astype(BF16), 3, w_left=True)
    cs_s[...] = cs
    cst_s[...] = cs.T
    csx_s[...] = _split_dot(cs, e_ref[...], 3)
    return a, tril, ri, ci


def _lmat(cst_s, h, tril):
    rowb = jnp.broadcast_to(cst_s[h:h + 1, :], (CH, CH))
    return jnp.exp(jnp.where(tril, rowb.T - rowb, -jnp.inf))


def _head_pair_rows(v, lane):
    return jnp.concatenate([jnp.where(lane < HD, v, 0.0), jnp.where(lane < HD, 0.0, v)], axis=0).astype(BF16)


def _ssd_fwd(proj, dtr, cw, cb, dtb, alog, dskx, sg, e128):
    T = proj.shape[0]
    nc = T // CH

    def body(xbc_ref, zb0_ref, zb1_ref, dt_ref, cw_ref, cb_ref, dtb_ref, alog_ref, dx_ref, sg_ref, e_ref, shift_ref,
             y_ref, yb_ref, hp_ref, pre_ref, xx_s, h_s, cs_s, cst_s, csx_s, yz_s):
        @pl.when(pl.program_id(0) == 0)
        def _():
            xx_s[...] = jnp.zeros_like(xx_s)
            h_s[...] = jnp.zeros_like(h_s)

        xx_s[CH:, :] = xbc_ref[...]
        moved = _dot(shift_ref[...], xx_s[...])
        xx_s[CH - 16:CH, :] = xbc_ref[CH - 16:, :]
        x = xbc_ref[...].astype(F32)
        pre = _conv_pre(x, moved, cw_ref, cb_ref)
        pre_ref[...] = pre
        xc = pre * _sigmoid(pre)
        dt = _softplus(dt_ref[...] + dtb_ref[...])
        a, tril, _, lane = _chunk_decays(dt, alog_ref, e_ref, cs_s, cst_s, csx_s)
        dt_x = _split_dot(dt, e_ref[...], 2)
        cs_last_x = csx_s[CH - 1:CH, :]
        hp_ref[0] = h_s[...]
        for g in range(NG):
            gs = slice(g * 512, (g + 1) * 512)
            bg = xc[:, DI + g * NST:DI + (g + 1) * NST].astype(BF16)
            cg = xc[:, DI + 512 + g * NST:DI + 512 + (g + 1) * NST].astype(BF16)
            cbm = _dot_nt(cg, bg)
            xg = xc[:, gs]
            xdt = xg * dt_x[:, gs]
            hprev = h_s[:, gs]
            csx = csx_s[:, gs]
            yoff = _dot(cg, hprev.astype(BF16)) * jnp.exp(csx)
            st = _dot_tn(bg, (xdt * jnp.exp(cs_last_x[:, gs] - csx)).astype(BF16))
            h_s[:, gs] = jnp.exp(cs_last_x[:, gs]) * hprev + st
            ssq = jnp.zeros((CH, 1), F32)
            for q in range(4):
                h0 = g * 8 + 2 * q
                ps = slice(q * 128, (q + 1) * 128)
                cols = slice(g * 512 + q * 128, g * 512 + (q + 1) * 128)
                m01 = jnp.concatenate([cbm * _lmat(cst_s, h0, tril), cbm * _lmat(cst_s, h0 + 1, tril)], axis=1).astype(BF16)
                yq = _dot(m01, _head_pair_rows(xdt[:, ps], lane)) + yoff[:, ps] + xg[:, ps] * dx_ref[:, cols]
                y_ref[:, cols] = yq
                z = _zb_cols(zb0_ref, zb1_ref, cols).astype(F32)
                yz = yq * z * _sigmoid(z)
                yz_s[:, cols] = yz
                ssq = ssq + jnp.sum(yz * yz, axis=1, keepdims=True)
            rg = lax.rsqrt(ssq * (1.0 / 512.0) + EPS)
            yb_ref[:, gs] = (yz_s[:, gs] * rg * sg_ref[:, gs]).astype(BF16)

    return pl.pallas_call(
        body, name="ssd_fwd", grid=(nc,),
        in_specs=[pl.BlockSpec((CH, CD), lambda c: (c, OFF_XBC // CD)), pl.BlockSpec((CH, D), lambda c: (c, OFF_ZB // D)),
                  pl.BlockSpec((CH, D), lambda c: (c, OFF_ZB // D + 1)),
                  pl.BlockSpec((CH, DTW), lambda c: (c, 0)), _full((4, CD)), _full((1, CD)), _full((1, DTW)),
                  _full((1, DTW)), _full((1, DI)), _full((1, DI)), _full((DTW, DI)), _full((3 * CH, 2 * CH))],
        out_specs=[pl.BlockSpec((CH, DI), lambda c: (c, 0)), pl.BlockSpec((CH, DI), lambda c: (c, 0)),
                   pl.BlockSpec((1, NST, DI), lambda c: (c, 0, 0)), pl.BlockSpec((CH, CD), lambda c: (c, 0))],
        out_shape=[jax.ShapeDtypeStruct((T, DI), F32), jax.ShapeDtypeStruct((T, DI), BF16),
                   jax.ShapeDtypeStruct((nc, NST, DI), F32), jax.ShapeDtypeStruct((T, CD), F32)],
        scratch_shapes=[pltpu.VMEM((2 * CH, CD), BF16), pltpu.VMEM((NST, DI), F32), pltpu.VMEM((CH, CH), F32),
                        pltpu.VMEM((CH, CH), F32), pltpu.VMEM((CH, DI), F32), pltpu.VMEM((CH, DI), F32)],
        compiler_params=_cp(("arbitrary",)),
    )(proj, proj, proj, dtr, cw, cb, dtb, alog, dskx, sg, e128, _shift_matrix(down=True))


def _ssd_bwd(proj, pre_all, dtr, y, dyb, hprev_all, cw, dtb, alog, dskx, sg, e128, et128, ex):
    T = proj.shape[0]
    nc = T // CH

    def body(xbc_ref, pre_ref, zb0_ref, zb1_ref, dt_ref, y_ref, dyb_ref, hp_ref, cw_ref, dtb_ref, alog_ref, dx_ref, sg_ref,
             e_ref, et_ref, shift_ref, d_ref, ddt_ref, dcw_ref, dsm_ref, dsg_ref,
             g_s, dd_s, cs_s, cst_s, csx_s, dy_s, dxdt_s, dxs_s, dsd_s, dxc_s, gh_s):
        dxbc_ref, dzb_ref = d_ref.at[:, 0:CD], d_ref.at[:, CD:CD + DI]
        i = pl.program_id(0)

        @pl.when(i == 0)
        def _():
            g_s[...] = jnp.zeros_like(g_s)
            dd_s[...] = jnp.zeros_like(dd_s)
            dcw_ref[...] = jnp.zeros_like(dcw_ref)
            dsm_ref[...] = jnp.zeros_like(dsm_ref)
            dsg_ref[...] = jnp.zeros_like(dsg_ref)

        pre = pre_ref[...]
        sp = _sigmoid(pre)
        xc = pre * sp
        dtr = dt_ref[...] + dtb_ref[...]
        dt = _softplus(dtr)
        a, tril, ri, lane = _chunk_decays(dt, alog_ref, e_ref, cs_s, cst_s, csx_s)
        et = et_ref[...]
        dt_x = _split_dot(dt, e_ref[...], 2)
        cs_last_x = csx_s[CH - 1:CH, :]

        for g in range(NG):
            gs = slice(g * 512, (g + 1) * 512)
            z = _zb_cols(zb0_ref, zb1_ref, gs).astype(F32)
            sz = _sigmoid(z)
            yv = y_ref[:, gs]
            zs = z * sz
            yz = yv * zs
            rg = lax.rsqrt(jnp.mean(yz * yz, axis=-1, keepdims=True) + EPS)
            yn = yz * rg
            dyb = dyb_ref[:, gs].astype(F32)
            dsg_ref[0:1, gs] += jnp.sum(dyb * yn, axis=0, keepdims=True)
            dyn = dyb * sg_ref[:, gs]
            dyz = rg * (dyn - yn * jnp.mean(dyn * yn, axis=-1, keepdims=True))
            dy_s[:, gs] = dyz * zs
            dzb_ref[:, gs] = (dyz * yv * (sz + zs * (1.0 - sz))).astype(BF16)

        rsum = jnp.zeros((CH, DTW), F32)
        csum_t = jnp.zeros((DTW, CH), F32)
        for g in range(NG):
            gs = slice(g * 512, (g + 1) * 512)
            bg = xc[:, DI + g * NST:DI + (g + 1) * NST].astype(BF16)
            cg = xc[:, DI + 512 + g * NST:DI + 512 + (g + 1) * NST].astype(BF16)
            cbm = _dot_nt(cg, bg)
            xdt = xc[:, gs] * dt_x[:, gs]
            hprev = hp_ref[0, :, gs]
            hpb = hprev.astype(BF16)
            gn = g_s[:, gs]
            gnb = gn.astype(BF16)
            dy = dy_s[:, gs]
            csx = csx_s[:, gs]
            ecs = jnp.exp(csx)
            dec = jnp.exp(cs_last_x[:, gs] - csx)
            dye_f = dy * ecs
            dye = dye_f.astype(BF16)
            dc = _dot_nt(dye, hpb)
            dprev = _dot_tn(cg, dye)
            dxdt_state = dec * _dot(bg, gnb)
            db = _dot_nt((xdt * dec).astype(BF16), gnb)
            dcb = jnp.zeros((CH, CH), F32)
            for q in range(4):
                h0 = g * 8 + 2 * q
                ps = slice(q * 128, (q + 1) * 128)
                dyp = dy[:, ps]
                l0 = _lmat(cst_s, h0, tril)
                l1 = _lmat(cst_s, h0 + 1, tril)
                m0 = cbm * l0
                m1 = cbm * l1
                dm = _dot_nt(dyp.astype(BF16), _head_pair_rows(xdt[:, ps], lane))
                dm0 = dm[:, :CH]
                dm1 = dm[:, CH:]
                dcb = dcb + dm0 * l0 + dm1 * l1
                for hh, qm in ((h0, dm0 * m0), (h0 + 1, dm1 * m1)):
                    rsum = jnp.where(lane == hh, jnp.sum(qm, axis=1, keepdims=True), rsum)
                    csum_t = jnp.where(ri == hh, jnp.sum(qm, axis=0, keepdims=True), csum_t)
                mst = jnp.concatenate([m0, m1], axis=0).astype(BF16)
                d = _dot_tn(mst, _head_pair_rows(dyp, lane))
                dxdt_s[:, g * 512 + q * 128:g * 512 + (q + 1) * 128] = d + dxdt_state[:, ps]
            dsd_s[:, gs] = xdt * dxdt_state
            dxs_s[:, gs] = dye_f * _dot(cg, hpb)
            dcbb = dcb.astype(BF16)
            dxc_s[:, DI + 512 + g * NST:DI + 512 + (g + 1) * NST] = dc + _dot(dcbb, bg)
            dxc_s[:, DI + g * NST:DI + (g + 1) * NST] = db + _dot_tn(dcbb, cg)
            gh_s[:, gs] = jnp.broadcast_to(jnp.sum(gn * hprev, axis=0, keepdims=True), (8, 512))
            g_s[:, gs] = dprev + jnp.exp(cs_last_x[:, gs]) * gn

        xs = xc[:, :DI]
        dy = dy_s[...]
        dxdt = dxdt_s[...]
        cs_last = cs_s[CH - 1:CH, :]
        state_e = _split_dot(dsd_s[...], et, 1)
        dcd = 0.125 * jnp.sum(_split_dot(gh_s[...], et, 2), axis=0, keepdims=True) * jnp.exp(cs_last)
        row = lax.broadcasted_iota(jnp.int32, (CH, 1), 0)
        dcs = rsum - csum_t.T + _split_dot(dxs_s[...], et, 1) - state_e
        dcs = dcs + jnp.where(row == CH - 1, jnp.sum(state_e, axis=0, keepdims=True) + dcd, 0.0)
        dda = _split_dot(dcs, (lane >= ri).astype(BF16), 3, w_left=True)
        ddt = dda * a + _dot((dxdt * xs).astype(BF16), et)
        ddtr = jnp.where(lane < NH, ddt * _sigmoid(dtr), 0.0)
        ddt_ref[...] = ddtr.astype(BF16)
        dsm_ref[0:1, :] += jnp.sum(ddtr, axis=0, keepdims=True)
        dsm_ref[1:2, :] += jnp.sum(dda * dt, axis=0, keepdims=True) * a
        dsm_ref[2:3, :] += jnp.sum(_dot((dy * xs).astype(BF16), et), axis=0, keepdims=True)
        dxc_s[:, :DI] = dxdt * dt_x + dy * dx_ref[...]

        dpre = dxc_s[...] * (sp + xc * (1.0 - sp))
        dpre_b = dpre.astype(BF16)
        dd_s[:CH, :] = dpre_b
        moved = _dot(shift_ref[...], dd_s[...])
        dd_s[CH:CH + 16, :] = dpre_b[:16]
        x = xbc_ref[...].astype(F32)
        dcw_ref[4:5, :] += jnp.sum(dpre, axis=0, keepdims=True)
        dxbc = cw_ref[3:4, :] * dpre
        dcw_ref[3:4, :] += jnp.sum(dpre * x, axis=0, keepdims=True)
        for j in (1, 2, 3):
            ahead = moved[(j - 1) * CH:j * CH]
            dcw_ref[3 - j:4 - j, :] += jnp.sum(ahead * x, axis=0, keepdims=True)
            dxbc = dxbc + cw_ref[3 - j:4 - j, :] * ahead
        dxbc_ref[...] = dxbc.astype(BF16)

    rev = lambda c: nc - 1 - c
    return _call(
        body, ex, name="ssd_bwd", grid=(nc,),
        in_specs=[pl.BlockSpec((CH, CD), lambda c: (rev(c), OFF_XBC // CD)),
                  pl.BlockSpec((CH, CD), lambda c: (rev(c), 0)),
                  pl.BlockSpec((CH, D), lambda c: (rev(c), OFF_ZB // D)), pl.BlockSpec((CH, D), lambda c: (rev(c), OFF_ZB // D + 1)),
                  pl.BlockSpec((CH, DTW), lambda c: (rev(c), 0)),
                  pl.BlockSpec((CH, DI), lambda c: (rev(c), 0)), pl.BlockSpec((CH, DI), lambda c: (rev(c), 0)),
                  pl.BlockSpec((1, NST, DI), lambda c: (rev(c), 0, 0)),
                  _full((4, CD)), _full((1, DTW)), _full((1, DTW)), _full((1, DI)), _full((1, DI)),
                  _full((DTW, DI)), _full((DI, DTW)), _full((3 * CH, 2 * CH))],
        out_specs=[pl.BlockSpec((CH, CD + DI), lambda c: (rev(c), 0)),
                   pl.BlockSpec((CH, DTW), lambda c: (rev(c), 0)), _full((8, CD)), _full((8, DTW)), _full((8, DI))],
        out_shape=[jax.ShapeDtypeStruct((T, CD + DI), BF16), jax.ShapeDtypeStruct((T, DTW), BF16),
                   jax.ShapeDtypeStruct((8, CD), F32), jax.ShapeDtypeStruct((8, DTW), F32), jax.ShapeDtypeStruct((8, DI), F32)],
        scratch_shapes=[pltpu.VMEM((NST, DI), F32), pltpu.VMEM((2 * CH, CD), BF16), pltpu.VMEM((CH, CH), F32), pltpu.VMEM((CH, CH), F32),
                        pltpu.VMEM((CH, DI), F32), pltpu.VMEM((CH, DI), F32), pltpu.VMEM((CH, DI), F32), pltpu.VMEM((CH, DI), F32),
                        pltpu.VMEM((CH, DI), F32), pltpu.VMEM((CH, CD), F32), pltpu.VMEM((8, DI), F32)],
        args=(proj, pre_all, proj, proj, dtr, y, dyb, hprev_all, cw, dtb, alog, dskx, sg, e128, et128, _shift_matrix(down=False)))


def _merge_tile():
    return 256


def _merge(x, yb, proj, p, tgt, gmlp, w_oa, w_ob, w_out, w_pg, w_ple, ple_g, fin_g):
    T = x.shape[0]
    tm = min(T, _merge_tile())

    def body(x_ref, u_ref, v_ref, za_ref, yb_ref, ga_ref, gb_ref, p_ref, t_ref, lg_ref, lb_ref, ws_ref, bs_ref,
             woa, wob, wout, wpg, wple, pg_ref, fg_ref,
             dx1_ref, dx1b_ref, ya_ref, mg_ref, hp_ref, dpre_ref, dpe_ref, doa_ref, dob_ref, dya_ref, dyb_ref, dg_ref, acc_ref, vn_s):
        @pl.when(pl.program_id(0) == 0)
        def _():
            acc_ref[...] = jnp.zeros_like(acc_ref)

        _gmlp_fwd_tile(u_ref, v_ref, za_ref, lg_ref, lb_ref, ws_ref, bs_ref, ya_ref, vn_s)
        oa = _dot(ya_ref[...], woa[...])
        ob = _dot(yb_ref[...], wob[...])
        sa = _sigmoid(ga_ref[...].astype(F32))
        sb = _sigmoid(gb_ref[...].astype(F32))
        mg = sa * oa + sb * ob
        mgb = mg.astype(BF16)
        mg_ref[...] = mgb
        x1 = x_ref[...] + _dot(mgb, wout[...])
        r2 = lax.rsqrt(jnp.mean(x1 * x1, axis=-1, keepdims=True) + EPS)
        xh1 = x1 * r2
        hpb = (xh1 * pg_ref[...]).astype(BF16)
        hp_ref[...] = hpb
        gate = _sigmoid(_dot(hpb, wpg[...]))
        pe = _dot(p_ref[...].astype(BF16), wple[...])
        x2 = x1 + gate * pe
        r3 = lax.rsqrt(jnp.mean(x2 * x2, axis=-1, keepdims=True) + EPS)
        xh2 = x2 * r3
        err = xh2 * fg_ref[...] - t_ref[...]
        acc_ref[2:3, :] += 0.5 * jnp.sum(jnp.mean(err * err, axis=-1, keepdims=True))
        dyo = err * (1.0 / D)
        acc_ref[0:1, :] += jnp.sum(dyo * xh2, axis=0, keepdims=True)
        dn = dyo * fg_ref[...]
        dx2 = r3 * (dn - xh2 * jnp.mean(dn * xh2, axis=-1, keepdims=True))
        dpe_ref[...] = (dx2 * gate).astype(BF16)
        dpre = (dx2 * pe * gate * (1.0 - gate)).astype(BF16)
        dpre_ref[...] = dpre
        dhp = _dot_nt(dpre, wpg[...])
        acc_ref[1:2, :] += jnp.sum(dhp * xh1, axis=0, keepdims=True)
        dhn = dhp * pg_ref[...]
        dx1 = dx2 + r2 * (dhn - xh1 * jnp.mean(dhn * xh1, axis=-1, keepdims=True))
        dx1_ref[...] = dx1
        dx1b = dx1.astype(BF16)
        dx1b_ref[...] = dx1b
        dmg = _dot_nt(dx1b, wout[...])
        doa = (dmg * sa).astype(BF16)
        dob = (dmg * sb).astype(BF16)
        doa_ref[...] = doa
        dob_ref[...] = dob
        dg_ref[:, :D] = (dmg * oa * sa * (1.0 - sa)).astype(BF16)
        dg_ref[:, D:] = (dmg * ob * sb * (1.0 - sb)).astype(BF16)
        dya_ref[...] = _dot_nt(doa, woa[...]).astype(BF16)
        dyb_ref[...] = _dot_nt(dob, wob[...]).astype(BF16)

    row = lambda w: pl.BlockSpec((tm, w), lambda i: (i, 0))
    blk = lambda off: pl.BlockSpec((tm, D), lambda i: (i, off // D))
    wsp = lambda s: pl.BlockSpec(s, lambda i: (0, 0), pipeline_mode=pl.Buffered(1))
    return pl.pallas_call(
        body, name="merge", grid=(T // tm,),
        in_specs=[row(D), blk(OFF_U), blk(OFF_V), blk(OFF_ZA), row(DI), blk(OFF_GA), blk(OFF_GB), row(PLE), row(D),
                  _full((1, D)), _full((1, D)), _full((NG, CH, CH)), _full((NG, CH, 256)),
                  wsp((D, D)), wsp((DI, D)), wsp((D, D)), wsp((D, D)), wsp((PLE, D)), _full((1, D)), _full((1, D))],
        out_specs=[row(D)] * 10 + [row(DI), row(2 * D), _full((8, D))],
        out_shape=[jax.ShapeDtypeStruct((T, D), F32)] + [jax.ShapeDtypeStruct((T, D), BF16)] * 9
        + [jax.ShapeDtypeStruct((T, DI), BF16), jax.ShapeDtypeStruct((T, 2 * D), BF16), jax.ShapeDtypeStruct((8, D), F32)],
        scratch_shapes=[pltpu.VMEM((tm, D), BF16)],
        compiler_params=_cp(("arbitrary",)),
    )(x, proj, proj, proj, yb, proj, proj, p, tgt, *gmlp, w_oa, w_ob, w_out, w_pg, w_ple, ple_g, fin_g)


def _wgrad(a, b, name):
    T, K = a.shape
    N = b.shape[1]
    tt, tk, tn = min(T, 2048), min(K, 1024), min(N, 1024)
    nt = T // tt

    def body(a_ref, b_ref, o_ref, acc_s):
        t = pl.program_id(2)

        @pl.when(t == 0)
        def _():
            acc_s[...] = jnp.zeros_like(acc_s)

        acc_s[...] += _dot_tn(a_ref[...].astype(BF16), b_ref[...])

        @pl.when(t == nt - 1)
        def _():
            o_ref[...] = acc_s[...].astype(BF16)

    return pl.pallas_call(
        body, name=name, grid=(K // tk, N // tn, nt),
        in_specs=[pl.BlockSpec((tt, tk), lambda k, n, t: (t, k)), pl.BlockSpec((tt, tn), lambda k, n, t: (t, n))],
        out_specs=pl.BlockSpec((tk, tn), lambda k, n, t: (k, n)),
        out_shape=jax.ShapeDtypeStruct((K, N), BF16),
        scratch_shapes=[pltpu.VMEM((tk, tn), F32)],
        compiler_params=_cp(("parallel", "parallel", "arbitrary")),
    )(a, b)


def _wgrad_into(a, b, name, buf, total_rows, row_of, keep=None):
    T, K = a.shape
    N = b.shape[1]
    assert N == D
    tt, tk = min(T, 2048), min(K, 1024)
    nt, nb = T // tt, K // tk
    keep = tk if keep is None else keep

    def body(*refs):
        a_ref, b_ref = refs[:2]
        o_hbm, acc_s, stage, sems = refs[-4:]
        k, t = pl.program_id(0), pl.program_id(1)
        slot = k % 2

        def store(s):
            return pltpu.make_async_copy(stage.at[s, 0:keep], o_hbm.at[pl.ds(pl.multiple_of(row_of(k), 16), keep)], sems.at[s])

        @pl.when(t == 0)
        def _():
            acc_s[...] = jnp.zeros_like(acc_s)

        acc_s[...] += _dot_tn(a_ref[...].astype(BF16), b_ref[...])

        @pl.when(t == nt - 1)
        def _():
            @pl.when(k >= 2)
            def _():
                store(slot).wait()
            stage[slot] = acc_s[...].astype(BF16)
            store(slot).start()

        @pl.when((t == nt - 1) & (k == nb - 1))
        def _():
            store(slot).wait()
            if nb >= 2:
                store(1 - slot).wait()

    return pl.pallas_call(
        body, name=name, grid=(nb, nt),
        in_specs=[pl.BlockSpec((tt, tk), lambda k, t: (t, k)), pl.BlockSpec((tt, D), lambda k, t: (t, 0))] + ([] if buf is None else [_ANY]),
        out_specs=_ANY,
        out_shape=jax.ShapeDtypeStruct((total_rows, D), BF16),
        scratch_shapes=[pltpu.VMEM((tk, D), F32), pltpu.VMEM((2, tk, D), BF16), pltpu.SemaphoreType.DMA((2,))],
        input_output_aliases={} if buf is None else {2: 0},
        compiler_params=_cp(("arbitrary", "arbitrary")),
    )(*((a, b) if buf is None else (a, b, buf)))


def _wgrad_rows_into(a, b, name, buf, row):
    T, K = a.shape
    assert b.shape[1] == D and K % 1024 == 0
    tt, tk, per = min(T, 2048), 1024, K // NDEV
    nt, nb, nd = T // tt, K // 1024, 1024 // per

    def body(*refs):
        a_ref, b_ref = refs[:2]
        o_hbm, acc_s, stage, sems = refs[-4:]
        k, t = pl.program_id(0), pl.program_id(1)
        slot = k % 2

        def store(s):
            return pltpu.make_async_copy(stage.at[s], o_hbm.at[pl.ds(k * nd, nd), pl.ds(row, per)], sems.at[s])

        @pl.when(t == 0)
        def _():
            acc_s[...] = jnp.zeros_like(acc_s)

        acc_s[...] += _dot_tn(a_ref[...].astype(BF16), b_ref[...])

        @pl.when(t == nt - 1)
        def _():
            @pl.when(k >= 2)
            def _():
                store(slot).wait()
            stage[slot] = acc_s[...].astype(BF16).reshape(nd, per, D)
            store(slot).start()

        @pl.when((t == nt - 1) & (k == nb - 1))
        def _():
            store(slot).wait()
            if nb >= 2:
                store(1 - slot).wait()

    return pl.pallas_call(
        body, name=name, grid=(nb, nt),
        in_specs=[pl.BlockSpec((tt, tk), lambda k, t: (t, k)), pl.BlockSpec((tt, D), lambda k, t: (t, 0))] + ([] if buf is None else [_ANY]),
        out_specs=_ANY,
        out_shape=jax.ShapeDtypeStruct((NDEV, R_ROWS, D), BF16),
        scratch_shapes=[pltpu.VMEM((tk, D), F32), pltpu.VMEM((2, nd, per, D), BF16), pltpu.SemaphoreType.DMA((2,))],
        input_output_aliases={} if buf is None else {2: 0},
        compiler_params=_cp(("arbitrary", "arbitrary")),
    )(*((a, b) if buf is None else (a, b, buf)))


def _proj_bwd(x, dx1, norm_g, wt, pieces, ddt, ex):
    T = x.shape[0]
    tm = min(T, 1024)
    nk = OFF_DT // D + 1
    starts = [sum(a.shape[1] for a in pieces[:n]) // D for n in range(len(pieces))]
    ranges = [(s, s + a.shape[1] // D) for s, a in zip(starts, pieces)]
    assert ranges[-1][1] == nk - 1
    npc = len(pieces)

    def body(x_hbm, dx1_hbm, g_ref, w_ref, wdt_ref, *rest):
        piece_refs, ddt_ref, gx_ref, dng_ref = rest[:npc], rest[npc], rest[npc + 1], rest[npc + 2]
        acc_s, x_ref, dx1_ref, row_sems = rest[npc + 3:]
        i, k = pl.program_id(0), pl.program_id(1)
        rows = pl.ds(pl.multiple_of(i * tm, tm), tm)
        fetches = [pltpu.make_async_copy(x_hbm.at[rows], x_ref, row_sems.at[0]),
                   pltpu.make_async_copy(dx1_hbm.at[rows], dx1_ref, row_sems.at[1])]

        @pl.when((i == 0) & (k == 0))
        def _():
            dng_ref[...] = jnp.zeros_like(dng_ref)

        @pl.when(k == 0)
        def _():
            acc_s[...] = jnp.zeros_like(acc_s)

        @pl.when(k == 1)
        def _():
            for cp in fetches:
                cp.start()

        for ref, (lo, hi) in zip(piece_refs, ranges):
            @pl.when((k >= lo) & (k < hi))
            def _(ref=ref):
                acc_s[...] += _dot(ref[...], w_ref[...])

        @pl.when(k == nk - 1)
        def _():
            for cp in fetches:
                cp.wait()
            dh = acc_s[...] + _dot(ddt_ref[...], wdt_ref[...])
            xf = x_ref[...]
            r = lax.rsqrt(jnp.mean(xf * xf, axis=-1, keepdims=True) + EPS)
            xh = xf * r
            dng_ref[0:1, :] += jnp.sum(dh * xh, axis=0, keepdims=True)
            dxn = dh * g_ref[...]
            gx_ref[...] = dx1_ref[...] + r * (dxn - xh * jnp.mean(dxn * xh, axis=-1, keepdims=True))

    def piece_spec(lo, hi):
        return pl.BlockSpec((tm, D), lambda i, k: (i, jnp.clip(k - lo, 0, hi - lo - 1)))

    row = pl.BlockSpec((tm, D), lambda i, k: (i, 0))
    return _call(
        body, ex, name="proj_bwd", grid=(T // tm, nk),
        in_specs=[_ANY, _ANY, _full((1, D)), pl.BlockSpec((D, D), lambda i, k: (jnp.minimum(k, nk - 2), 0)),
                  pl.BlockSpec((DTW, D), lambda i, k: (OFF_DT // DTW, 0))]
        + [piece_spec(lo, hi) for lo, hi in ranges] + [pl.BlockSpec((tm, DTW), lambda i, k: (i, 0))],
        out_specs=[row, _full((8, D))],
        out_shape=[jax.ShapeDtypeStruct((T, D), F32), jax.ShapeDtypeStruct((8, D), F32)],
        scratch_shapes=[pltpu.VMEM((tm, D), F32), pltpu.VMEM((tm, D), F32), pltpu.VMEM((tm, D), F32), pltpu.SemaphoreType.DMA((2,))],
        args=(x, dx1, norm_g, wt, wt, *pieces, ddt))


def _elementwise_tile(R, C, limit=1 << 20):
    if R * C * 4 <= limit:
        return R, C
    rows = [t for t in range(16, R, 16) if R % t == 0 and t * C * 4 <= limit]
    if rows:
        return rows[-1], C
    cols = [t for t in range(128, C, 128) if C % t == 0 and R * t * 4 <= limit]
    return R, cols[-1]


def _adam_update(w, m, v, g):
    c1 = 1.0 - ADAM_B1 ** ADAM_STEP
    c2 = 1.0 - ADAM_B2 ** ADAM_STEP
    mm = ADAM_B1 * m + (1.0 - ADAM_B1) * g
    vv = ADAM_B2 * v + (1.0 - ADAM_B2) * (g * g)
    return -ADAM_LR * ((mm / c1) / (jnp.sqrt(vv / c2) + ADAM_EPS) + ADAM_WD * w), mm, vv


def _sum_parts(parts, name, ex):
    first = parts[0][0] if isinstance(parts[0], tuple) else parts[0]
    R, C = first.shape[-2:]
    tr, tc = _elementwise_tile(R, C)
    flat = pl.BlockSpec((tr, tc), lambda i, j: (i, j))
    slot = lambda k: pl.BlockSpec((None, tr, tc), lambda i, j: (k, i, j))
    specs = [slot(p[1]) if isinstance(p, tuple) else flat for p in parts]
    arrays = [p[0] if isinstance(p, tuple) else p for p in parts]

    def body(*refs):
        g = refs[0][...].astype(F32)
        for r in refs[1:-1]:
            g = g + r[...].astype(F32)
        refs[-1][...] = g

    (out,), got = _call(body, ex, name=name, grid=(R // tr, C // tc), in_specs=specs, out_specs=[flat],
                        out_shape=[jax.ShapeDtypeStruct((R, C), F32)], scratch_shapes=[], args=arrays)
    return out, got


def _adamw_tiles(w, m, v, g, name):
    R = w.shape[0]
    tr = 107
    assert R % tr == 0
    spec = pl.BlockSpec((tr, 8, 128), lambda i: (i, 0, 0))

    def body(w_ref, m_ref, v_ref, g_in, g_ref, d_ref, nm_ref, nv_ref):
        g = g_in[...]
        g_ref[...] = g
        d_ref[...], nm_ref[...], nv_ref[...] = _adam_update(w_ref[...], m_ref[...], v_ref[...], g)

    return pl.pallas_call(body, name=name, grid=(R // tr,), in_specs=[spec] * 4, out_specs=[spec] * 4,
                          out_shape=[jax.ShapeDtypeStruct(w.shape, F32)] * 4, compiler_params=_cp(("parallel",)))(w, m, v, g)


def _adamw(w, m, v, parts, name, part_row=0, block_rows=None):
    R, C = w.shape
    tr, tc = _elementwise_tile(R, C) if block_rows is None else (block_rows, C)
    assert R % tr == 0 and part_row % tr == 0
    first = part_row // tr
    n = len(parts)
    wspec = pl.BlockSpec((tr, tc), lambda i, j: (i, j))
    flat = pl.BlockSpec((tr, tc), lambda i, j: (first + i, j))
    slot = lambda k: pl.BlockSpec((None, tr, tc), lambda i, j: (k, first + i, j))
    part_specs = [slot(p[1]) if isinstance(p, tuple) else flat for p in parts]
    part_arrays = [p[0] if isinstance(p, tuple) else p for p in parts]

    def body(*refs):
        w_ref, m_ref, v_ref = refs[:3]
        g_ref, d_ref, nm_ref, nv_ref = refs[3 + n:]
        g = refs[3][...].astype(F32)
        for r in refs[4:3 + n]:
            g = g + r[...].astype(F32)
        g_ref[...] = g
        d_ref[...], nm_ref[...], nv_ref[...] = _adam_update(w_ref[...], m_ref[...], v_ref[...], g)

    return pl.pallas_call(
        body, name=name, grid=(R // tr, C // tc), in_specs=[wspec] * 3 + part_specs, out_specs=[wspec] * 4,
        out_shape=[jax.ShapeDtypeStruct(w.shape, F32)] * 4, compiler_params=_cp(("parallel", "parallel")),
    )(w, m, v, *part_arrays)


_SMALL_WIDE = {"ln_a_g": 0, "ln_a_b": 1, "final_g": 8, "ple_norm_g": 9, "ssm_norm_g": 16, "conv_b": 44}
_WIDE_CONV_W, _WIDE_ROWS = 32, 56
_WIDE_LOSS = 10
_SMALL_NARROW = {"w_s": (0, NG * CH, 128), "b_s": (512, NG, 128), "dt_bias": (520, 1, NH), "a_log": (521, 1, NH),
                 "d_skip": (522, 1, NH)}
_NARROW_ROWS = 528
_SMALL_SHAPES = {"norm_g": (1, D), "ln_a_g": (1, D), "ln_a_b": (1, D), "ple_norm_g": (1, D), "final_g": (1, D),
                 "ssm_norm_g": (1, DI), "conv_b": (1, CD), "w_s": (NG * CH, CH), "b_s": (NG, CH), "dt_bias": (1, NH),
                 "a_log": (1, NH), "d_skip": (1, NH)}


def _adamw_small(w, m, v, wide_all, narrow_all, norm_all):
    names = list(_SMALL_SHAPES)
    n = len(names)

    def body(*refs):
        wr, mr, vr = refs[:n], refs[n:2 * n], refs[2 * n:3 * n]
        wide_ref, narrow_ref, norm_ref = refs[3 * n:3 * n + 3]
        outs = refs[3 * n + 3:]
        gr, dr, nmr, nvr, cw_ref, loss_ref = outs[:n], outs[n:2 * n], outs[2 * n:3 * n], outs[3 * n:4 * n], outs[4 * n], outs[4 * n + 1]

        def total(ref, rows, lanes):
            acc = ref[0, rows, lanes]
            for d in range(1, NDEV):
                acc = acc + ref[d, rows, lanes]
            return acc

        for k, name in enumerate(names):
            if name in _SMALL_WIDE or name == "norm_g":
                for part in range(_SMALL_SHAPES[name][1] // D):
                    pack, r = (norm_ref, 0) if name == "norm_g" else (wide_ref, _SMALL_WIDE[name] + part)
                    cols = slice(part * D, (part + 1) * D)
                    g = total(pack, slice(r, r + 1), slice(None))
                    gr[k][:, cols] = g
                    dr[k][:, cols], nmr[k][:, cols], nvr[k][:, cols] = _adam_update(wr[k][:, cols], mr[k][:, cols], vr[k][:, cols], g)
            else:
                r, rows, lanes = _SMALL_NARROW[name]
                g = total(narrow_ref, slice(r, r + rows), slice(0, lanes))
                gr[k][...] = g
                dr[k][...], nmr[k][...], nvr[k][...] = _adam_update(wr[k][...], mr[k][...], vr[k][...], g)
        cw_ref[...] = total(wide_ref, slice(_WIDE_CONV_W, _WIDE_CONV_W + 12), slice(None))
        loss_ref[...] = total(wide_ref, slice(_WIDE_LOSS, _WIDE_LOSS + 1), slice(None))

    shapes = [jax.ShapeDtypeStruct(_SMALL_SHAPES[k], F32) for k in names]
    specs = [_full(_SMALL_SHAPES[k]) for k in names]
    res = pl.pallas_call(
        body, name="adamw_small", grid=(1,),
        in_specs=specs * 3 + [_full(wide_all.shape), _full(narrow_all.shape), _full(norm_all.shape)],
        out_specs=specs * 4 + [_full((12, D)), _full((1, D))],
        out_shape=shapes * 4 + [jax.ShapeDtypeStruct((12, D), F32), jax.ShapeDtypeStruct((1, D), F32)],
        compiler_params=_cp(("arbitrary",)),
    )(*[w[k] for k in names], *[m[k] for k in names], *[v[k] for k in names], wide_all, narrow_all, norm_all)
    groups = [dict(zip(names, res[q * n:(q + 1) * n])) for q in range(4)]
    return groups[0], groups[1], groups[2], groups[3], res[4 * n], res[4 * n + 1][0, 0]


def _dev_index(px, py, pc):
    return 4 * px + 2 * py + pc


def _mesh_position():
    return lax.axis_index("x"), lax.axis_index("y"), lax.axis_index("c")


def _gather_exchange(blocks):
    n = len(blocks)

    def make(ins, outs, sems):
        send_sems, recv_sems, local_sems = sems
        x, y, c = _mesh_position()
        me, sibling = (x, y, c), (x, y, 1 - c)
        chips = [(1 - x, y), (x, 1 - y), (1 - x, 1 - y)]

        def copy(a, k, block, to, src=None):
            dst = outs[a].at[_dev_index(*block)]
            return pltpu.make_async_remote_copy(src_ref=dst if src is None else src, dst_ref=dst, send_sem=send_sems.at[a, k],
                                                recv_sem=recv_sems.at[a, k], device_id=to, device_id_type=MESH)

        mine = [pltpu.make_async_copy(ins[a], outs[a].at[_dev_index(*me)], local_sems.at[a]) for a in range(n)]
        first = []
        for a in range(n):
            first.append(copy(a, 0, me, sibling, src=ins[a]))
            first += [copy(a, 1 + j, me, (*chip, c), src=ins[a]) for j, chip in enumerate(chips)]

        def start():
            for cp in mine + first:
                cp.start()

        def finish():
            passed = []
            for j, chip in enumerate(chips):
                for a in range(n):
                    copy(a, 1 + j, (*chip, c), me).wait_recv()
                    fwd = copy(a, 4 + j, (*chip, c), sibling)
                    fwd.start()
                    passed.append(fwd)
            for a in range(n):
                copy(a, 0, sibling, me).wait_recv()
                for j, chip in enumerate(chips):
                    copy(a, 4 + j, (*chip, 1 - c), me).wait_recv()
            for cp in first + passed:
                cp.wait_send()
            for cp in mine:
                cp.wait()

        return start, finish

    return _Exchange(list(blocks), [jax.ShapeDtypeStruct((NDEV,) + b.shape, b.dtype) for b in blocks],
                     [pltpu.SemaphoreType.DMA((n, 7)), pltpu.SemaphoreType.DMA((n, 7)), pltpu.SemaphoreType.DMA((n,))], make)


def _relay_gather_exchange(blocks):
    n = len(blocks)

    def make(ins, outs, sems):
        send_sems, recv_sems, local_sems = sems
        x, y, c = _mesh_position()
        me, sibling = (x, y, c), (x, y, 1 - c)
        x_nbr, y_nbr, diag = (1 - x, y), (x, 1 - y), (1 - x, 1 - y)
        relay_from = (jnp.where(c == 0, x, 1 - x), jnp.where(c == 0, 1 - y, y))
        relay_to = (jnp.where(c == 0, 1 - x, x), jnp.where(c == 0, y, 1 - y))

        def copy(a, k, block, to, src=None):
            dst = outs[a].at[_dev_index(*block)]
            return pltpu.make_async_remote_copy(src_ref=dst if src is None else src, dst_ref=dst, send_sem=send_sems.at[a, k],
                                                recv_sem=recv_sems.at[a, k], device_id=to, device_id_type=MESH)

        mine = [pltpu.make_async_copy(ins[a], outs[a].at[_dev_index(*me)], local_sems.at[a]) for a in range(n)]
        first = []
        for a in range(n):
            first += [copy(a, 0, me, sibling, src=ins[a]), copy(a, 1, me, (*x_nbr, c), src=ins[a]), copy(a, 2, me, (*y_nbr, c), src=ins[a])]

        def start():
            for cp in mine + first:
                cp.start()

        def finish():
            later = []
            for a in range(n):
                copy(a, 1, (*x_nbr, c), me).wait_recv()
                copy(a, 2, (*y_nbr, c), me).wait_recv()
                later.append(copy(a, 3, (*relay_from, c), (*relay_to, c)))
                later += [copy(a, 4, (*x_nbr, c), sibling), copy(a, 5, (*y_nbr, c), sibling)]
                for cp in later[-3:]:
                    cp.start()
            for a in range(n):
                copy(a, 3, (*diag, c), me).wait_recv()
                later.append(copy(a, 6, (*diag, c), sibling))
                later[-1].start()
            for a in range(n):
                copy(a, 0, sibling, me).wait_recv()
                for k, chip in ((4, x_nbr), (5, y_nbr), (6, diag)):
                    copy(a, k, (*chip, 1 - c), me).wait_recv()
            for cp in first + later:
                cp.wait_send()
            for cp in mine:
                cp.wait()

        return start, finish

    return _Exchange(list(blocks), [jax.ShapeDtypeStruct((NDEV,) + b.shape, b.dtype) for b in blocks],
                     [pltpu.SemaphoreType.DMA((n, 7)), pltpu.SemaphoreType.DMA((n, 7)), pltpu.SemaphoreType.DMA((n,))], make)


def _combine(*exchanges):
    def make(ins, outs, sems):
        pairs = []
        for e in exchanges:
            ni, no, ns = len(e.arrays), len(e.out_shape), len(e.sems)
            pairs.append(e.make(ins[:ni], outs[:no], sems[:ns]))
            ins, outs, sems = ins[ni:], outs[no:], sems[ns:]

        def start():
            for s, _ in pairs:
                s()

        def finish():
            for _, f in pairs:
                f()

        return start, finish

    return _Exchange(sum((list(e.arrays) for e in exchanges), []), sum((list(e.out_shape) for e in exchanges), []),
                     sum((list(e.sems) for e in exchanges), []), make)


def _start_wait_all(copies, local=()):
    def start():
        for cp in list(local) + list(copies):
            cp.start()

    def finish():
        for cp in copies:
            cp.wait()
        for cp in local:
            cp.wait()

    return start, finish


def _no_exchange():
    return _Exchange([], [], [], lambda ins, outs, sems: (lambda: None, lambda: None))


def _direct_exchange(grads):
    n = len(grads)

    def make(ins, outs, sems):
        send_sems, recv_sems = sems
        x, y, c = _mesh_position()
        copies = []
        for a in range(n):
            for r in range(1, NDEV):
                peer = (x ^ ((r >> 2) & 1), y ^ ((r >> 1) & 1), c ^ (r & 1))
                copies.append(pltpu.make_async_remote_copy(
                    src_ref=ins[a].at[_dev_index(*peer)], dst_ref=outs[a].at[r - 1], send_sem=send_sems.at[a, r - 1],
                    recv_sem=recv_sems.at[a, r - 1], device_id=peer, device_id_type=MESH))
        return _start_wait_all(copies)

    return _Exchange(list(grads), [jax.ShapeDtypeStruct((NDEV - 1,) + g.shape[1:], g.dtype) for g in grads],
                     [pltpu.SemaphoreType.DMA((n, NDEV - 1)), pltpu.SemaphoreType.DMA((n, NDEV - 1))], make)


def _direct_gather_exchange(smalls):
    n = len(smalls)

    def make(ins, outs, sems):
        send_sems, recv_sems, local_sems = sems
        x, y, c = _mesh_position()
        copies, local = [], []
        for a in range(n):
            slot = outs[a].at[_dev_index(x, y, c)]
            local.append(pltpu.make_async_copy(ins[a], slot, local_sems.at[a]))
            for r in range(1, NDEV):
                peer = (x ^ ((r >> 2) & 1), y ^ ((r >> 1) & 1), c ^ (r & 1))
                copies.append(pltpu.make_async_remote_copy(src_ref=ins[a], dst_ref=slot, send_sem=send_sems.at[a, r - 1],
                                                           recv_sem=recv_sems.at[a, r - 1], device_id=peer, device_id_type=MESH))
        return _start_wait_all(copies, local=local)

    return _Exchange(list(smalls), [jax.ShapeDtypeStruct((NDEV,) + s.shape, s.dtype) for s in smalls],
                     [pltpu.SemaphoreType.DMA((n, 7)), pltpu.SemaphoreType.DMA((n, 7)), pltpu.SemaphoreType.DMA((n,))], make)


_W_IN_ROWS = {"u": (0, 1024), "v": (1024, 2048), "za": (2048, 3072), "zb": (3072, 5120), "xbc": (5120, 8192),
              "dt": (8192, 8224), "ga": (8224, 9248), "gb": (9248, 10272)}
_PROJ_ORDER = ("xbc", "zb", "u", "v", "za", "ga", "gb", "dt")


def _w_in_t_rows(wt):
    z = jnp.zeros((NP - NIN, wt.shape[1]), wt.dtype)
    return jnp.concatenate([wt[slice(*_W_IN_ROWS[n])] for n in _PROJ_ORDER] + [z], axis=0)


_WEIGHTS = ["norm_g", "w_in", "ln_a_g", "ln_a_b", "w_s", "b_s", "conv_w", "conv_b", "dt_bias", "a_log", "d_skip", "ssm_norm_g",
            "w_oa", "w_ob", "w_out", "ple_norm_g", "w_pg", "w_ple", "final_g"]


def _rows_pack(d, dtype):
    return jnp.concatenate([d["w_oa"].reshape(128, D), d["w_ob"].reshape(256, D), d["w_out"].reshape(128, D),
                            d["w_pg"].reshape(128, D), d["w_ple"].reshape(32, D)], axis=0).astype(dtype)


def kernel(x, p, norm_g, w_in, ln_a_g, ln_a_b, w_s, b_s, conv_w, conv_b, dt_bias, a_log, d_skip, ssm_norm_g, w_oa, w_ob, w_out, ple_norm_g, w_pg, w_ple, final_g, loss_target, m_norm_g, m_w_in, m_ln_a_g, m_ln_a_b, m_w_s, m_b_s, m_conv_w, m_conv_b, m_dt_bias, m_a_log, m_d_skip, m_ssm_norm_g, m_w_oa, m_w_ob, m_w_out, m_ple_norm_g, m_w_pg, m_w_ple, m_final_g, v_norm_g, v_w_in, v_ln_a_g, v_ln_a_b, v_w_s, v_b_s, v_conv_w, v_conv_b, v_dt_bias, v_a_log, v_d_skip, v_ssm_norm_g, v_w_oa, v_w_ob, v_w_out, v_ple_norm_g, v_w_pg, v_w_ple, v_final_g):
    args = dict(locals())
    w = {n: args[n] for n in _WEIGHTS}
    m = {n: args["m_" + n] for n in _WEIGHTS}
    v = {n: args["v_" + n] for n in _WEIGHTS}
    T = x.shape[1]
    xi, yi, ci = lax.axis_index("x"), lax.axis_index("y"), lax.axis_index("c")
    me = 4 * xi + 2 * yi + ci
    x2, p2, tgt = x.reshape(T, D), p.reshape(T, PLE), loss_target.reshape(T, D)

    norm_g2 = w["norm_g"].reshape(1, D)
    ws = jnp.where(jnp.tril(jnp.ones((CH, CH), bool))[None], w["w_s"].reshape(NG, CH, CH), 0.0).astype(BF16)
    wst = jnp.transpose(ws, (0, 2, 1))
    bst = jnp.broadcast_to(w["b_s"].reshape(NG, CH, 1), (NG, CH, 256))
    ln_g, ln_b = w["ln_a_g"].reshape(1, D), w["ln_a_b"].reshape(1, D)
    cb = w["conv_b"].reshape(1, CD)
    pad32 = lambda a: jnp.pad(a.reshape(1, NH), ((0, 0), (0, DTW - NH)))
    dtb, alog = pad32(w["dt_bias"]), pad32(w["a_log"])
    dskx = jnp.repeat(w["d_skip"].reshape(NH), HD).reshape(1, DI)
    sg = w["ssm_norm_g"].reshape(1, DI)
    ple_g, fin_g = w["ple_norm_g"].reshape(1, D), w["final_g"].reshape(1, D)
    e128 = (jnp.arange(DTW)[:, None] == (jnp.arange(DI)[None, :] // HD)).astype(BF16)
    et128 = e128.T
    gsel = ((jnp.arange(D)[:, None] // 256) == jnp.arange(128)[None, :]).astype(BF16)

    w_in_t = lambda a: jnp.transpose(a.reshape(D, WSH))
    (a_all,) = _run_exchange(_relay_gather_exchange([w_in_t(w["w_in"]).astype(BF16)]), "all_gather_w_in")
    w_in_full_t = a_all.reshape(NIN, D)
    wp = _w_in_t_rows(w_in_full_t)
    (proj, dtr, h), (r_all, cw_all) = _proj_fwd(x2, norm_g2, wp, _gather_exchange([_rows_pack(w, BF16), w["conv_w"].reshape(4, CD // NDEV)]))
    f_oa = r_all[:, R_OA:R_OB].reshape(D, D)
    f_ob = r_all[:, R_OB:R_OUT].reshape(DI, D)
    f_out = r_all[:, R_OUT:R_PG].reshape(D, D)
    f_pg = r_all[:, R_PG:R_PLE].reshape(D, D)
    f_ple = jnp.transpose(r_all[:, R_PLE:R_ROWS].reshape(NDEV, PLE, 128), (1, 0, 2)).reshape(PLE, D)
    cw = jnp.transpose(cw_all, (1, 0, 2)).reshape(4, CD)

    y, yb, hprev, pre_all = _ssd_fwd(proj, dtr, cw, cb, dtb, alog, dskx, sg, e128)
    dx1, dx1b, ya, mg, hp, dpre, dpe, doa, dob, dya, dyb, dgab, acc = _merge(
        x2, yb, proj, p2, tgt, (ln_g, ln_b, ws, bst), f_oa, f_ob, f_out, f_pg, f_ple, ple_g, fin_g)

    gple = jnp.transpose(_wgrad(p2, dpe, "wgrad_ple").reshape(PLE, NDEV, 128), (1, 0, 2)).reshape(NDEV, 32, D)
    gr = _wgrad_rows_into(ya, doa, "wgrad_oa", None, R_OA)
    gr = _wgrad_rows_into(yb, dob, "wgrad_ob", gr, R_OB)
    gr = _wgrad_rows_into(mg, dx1b, "wgrad_out", gr, R_OUT)
    gr = _wgrad_rows_into(hp, dpre, "wgrad_pg", gr, R_PG)
    gr = lax.dynamic_update_slice(gr, gple, (0, R_PLE, 0))
    (duvz, dws, dbs, dln), _ = _gmlp_bwd(proj, dya, ln_g, ln_b, ws, wst, bst, gsel, _no_exchange())
    (dxz, ddt, dcw, dsm, dsg), (rr,) = _ssd_bwd(proj, pre_all, dtr, y, dyb, hprev, cw, dtb, alog, dskx, sg, e128, et128, _direct_exchange([gr]))

    nx = CD // 1024
    g_w_in_t = _wgrad_into(duvz, h, "wgrad_u_v_za", None, NIN, lambda k: k * 1024)
    g_w_in_t = _wgrad_into(dxz, h, "wgrad_xbc_zb", g_w_in_t, NIN,
                           lambda k: jnp.where(k < nx, 3072 + 2048 + k * 1024, 3072 + (k - nx) * 1024))
    g_w_in_t = _wgrad_into(ddt, h, "wgrad_dt", g_w_in_t, NIN, lambda k: 8192, keep=NH)
    g_w_in_t = _wgrad_into(dgab, h, "wgrad_ga_gb", g_w_in_t, NIN, lambda k: 8192 + NH + k * 1024)
    ga = g_w_in_t.reshape(NDEV, WSH, D)
    wide = jnp.concatenate([dln, acc, dsg.reshape(16, D), dcw.reshape(24, D)], axis=0)
    narrow = jnp.concatenate([dws.reshape(NG * CH, CH), jnp.pad(dbs[:, :NG].T, ((0, 8 - NG), (0, 0))), dsm], axis=0)
    (gx, dng), (ra, wide_all, narrow_all) = _proj_bwd(x2, dx1, norm_g2, wp, [dxz, duvz, dgab], ddt,
                                                      _combine(_direct_exchange([ga]), _direct_gather_exchange([wide, narrow])))
    ga_own = lax.dynamic_index_in_dim(ga, me, 0, keepdims=False)
    gr_own = lax.dynamic_index_in_dim(gr, me, 0, keepdims=False)

    out_g, out_d, out_m, out_v = {}, {}, {}, {}
    outs = (out_g, out_d, out_m, out_v)
    tiles = lambda a: jnp.transpose(a.reshape(8, 128, WSH), (2, 0, 1))
    g2, (norm_all,) = _sum_parts([ga_own] + [(ra, k) for k in range(NDEV - 1)], "sum_w_in", _direct_gather_exchange([dng]))
    res = _adamw_tiles(tiles(w["w_in"]), tiles(m["w_in"]), tiles(v["w_in"]), g2.reshape(WSH, 8, 128), "adamw_w_in")
    for dst, val in zip(outs, res):
        dst["w_in"] = jnp.transpose(val, (1, 2, 0)).reshape(1, D, WSH)
    parts_r = [gr_own] + [(rr, k) for k in range(NDEV - 1)]
    for name, row, rows in (("w_oa", R_OA, 128), ("w_ob", R_OB, 256), ("w_out", R_OUT, 128), ("w_pg", R_PG, 128)):
        res = _adamw(w[name].reshape(rows, D), m[name].reshape(rows, D), v[name].reshape(rows, D), parts_r, "adamw_" + name,
                     part_row=row, block_rows=128)
        for dst, val in zip(outs, res):
            dst[name] = val.reshape(1, rows, D)
    res = _adamw(w["w_ple"].reshape(32, D), m["w_ple"].reshape(32, D), v["w_ple"].reshape(32, D), parts_r, "adamw_w_ple",
                 part_row=R_PLE, block_rows=32)
    for dst, val in zip(outs, res):
        dst["w_ple"] = val.reshape(1, PLE, 128)
    two_d = lambda d: {n: d[n].reshape(_SMALL_SHAPES[n]) for n in _SMALL_SHAPES}
    *res, g_cw_wide, loss = _adamw_small(two_d(w), two_d(m), two_d(v), wide_all, narrow_all, norm_all)
    for dst, val in zip(outs, res):
        dst.update({n: val[n].reshape(w[n].shape) for n in _SMALL_SHAPES})
    g_cw = lax.dynamic_slice_in_dim(g_cw_wide.reshape(4, CD), me * (CD // NDEV), CD // NDEV, axis=1).reshape(12, 128)
    res = _adamw(w["conv_w"].reshape(12, 128), m["conv_w"].reshape(12, 128), v["conv_w"].reshape(12, 128), [g_cw], "adamw_conv_w")
    for dst, val in zip((out_g, out_d, out_m, out_v), res):
        dst["conv_w"] = val.reshape(1, 4, CD // NDEV)

    return (loss, gx.reshape(1, T, D), *[out_g[n] for n in _WEIGHTS], *[out_d[n] for n in _WEIGHTS],
            *[out_m[n] for n in _WEIGHTS], *[out_v[n] for n in _WEIGHTS])
```
